```python
import math
import jax, jax.numpy as jnp
from jax import lax
import numpy as np

D_MODEL = 1024
BATCH = 8
SEQ = 4096
DEPTH = 1

D_RNN = 1024
RNN_BLOCKS = 16
RNN_BLOCK_W = D_RNN // RNN_BLOCKS
CONV_W = 4
LRU_C = 8.0
N_Q_HEADS = 16
N_KV_HEADS = 4
GROUP = N_Q_HEADS // N_KV_HEADS
HEAD_DIM = 64
D_ATTN = N_Q_HEADS * HEAD_DIM
D_KV = N_KV_HEADS * HEAD_DIM
WINDOW = 128
BLOCK = 128
ALIBI_MAX_BIAS = 8.0
N_BRANCH = 2
EPS = 1e-6

SPLIT_SIZES = (D_RNN, D_RNN, D_ATTN, D_KV, D_KV, D_ATTN, N_BRANCH * D_MODEL)
D_IN = sum(SPLIT_SIZES)
SPLIT_POINTS = tuple(int(v) for v in np.cumsum(SPLIT_SIZES)[:-1])

kernel_name = "hybrid_rglru_swa_sink_alibi_block"


def rms_norm(x, g):
    xf = x.astype(jnp.float32)
    y = xf * lax.rsqrt(jnp.mean(xf * xf, axis=-1, keepdims=True) + EPS)
    return (y * g.astype(jnp.float32)).astype(x.dtype)


def causal_depthwise_conv(x, w, b):
    y = lax.conv_general_dilated(
        x, w[:, None, :].astype(x.dtype), window_strides=(1,),
        padding=[(CONV_W - 1, 0)], dimension_numbers=("NWC", "WIO", "NWC"),
        feature_group_count=x.shape[-1])
    return y + b


def rg_lru(x, w_a, b_a, w_x, b_x, lam):
    B, T, _ = x.shape
    xb = x.reshape(B, T, RNN_BLOCKS, RNN_BLOCK_W)
    r = jax.nn.sigmoid(jnp.einsum("bthi,hij->bthj", xb, w_a) + b_a).reshape(B, T, D_RNN)
    i = jax.nn.sigmoid(jnp.einsum("bthi,hij->bthj", xb, w_x) + b_x).reshape(B, T, D_RNN)
    log_a = -LRU_C * r.astype(jnp.float32) * jax.nn.softplus(-lam.astype(jnp.float32))
    a = jnp.exp(log_a)
    mult = jnp.sqrt(-jnp.expm1(2.0 * log_a))
    u = mult * (i * x).astype(jnp.float32)

    def combine(left, right):
        a_l, u_l = left
        a_r, u_r = right
        return a_l * a_r, a_r * u_l + u_r

    _, h = lax.associative_scan(combine, (a, u), axis=1)
    return h.astype(x.dtype)


def sliding_window_sink_attention(q, k, v, sinks):
    B, T, _, _ = q.shape
    nb = T // BLOCK
    scale = HEAD_DIM ** -0.5
    qb = q.reshape(B, nb, BLOCK, N_KV_HEADS, GROUP, HEAD_DIM)

    def band(t):
        tp = jnp.pad(t, ((0, 0), (BLOCK, 0), (0, 0), (0, 0)))
        tp = tp.reshape(B, nb + 1, BLOCK, N_KV_HEADS, HEAD_DIM)
        return jnp.concatenate([tp[:, :-1], tp[:, 1:]], axis=2)

    kw, vw = band(k), band(v)
    scores = jnp.einsum("bnqhgd,bnkhd->bnhgqk", qb, kw).astype(jnp.float32) * scale

    q_loc = jnp.arange(BLOCK)[:, None] + BLOCK
    k_loc = jnp.arange(2 * BLOCK)[None, :]
    dist = q_loc - k_loc
    in_window = (dist >= 0) & (dist < WINDOW)
    k_abs = jnp.arange(nb)[:, None] * BLOCK - BLOCK + jnp.arange(2 * BLOCK)[None, :]
    mask = in_window[None, :, :] & (k_abs >= 0)[:, None, :]

    slopes = 2.0 ** (-ALIBI_MAX_BIAS * jnp.arange(1, N_Q_HEADS + 1, dtype=jnp.float32) / N_Q_HEADS)
    slopes = slopes.reshape(N_KV_HEADS, GROUP)
    alibi = -slopes[:, :, None, None] * dist.astype(jnp.float32)

    scores = jnp.where(mask[None, :, None, None], scores + alibi[None, None], jnp.float32(-1e30))
    sink = sinks.astype(jnp.float32).reshape(N_KV_HEADS, GROUP)[None, None, :, :, None, None]
    m = jnp.maximum(jnp.max(scores, axis=-1, keepdims=True), sink)
    p = jnp.exp(scores - m)
    denom = jnp.sum(p, axis=-1, keepdims=True) + jnp.exp(sink - m)
    probs = (p / denom).astype(v.dtype)
    out = jnp.einsum("bnhgqk,bnkhd->bnqhgd", probs, vw)
    return out.reshape(B, T, D_ATTN)


def _fwd_setup_inputs(seed: int = 0) -> dict:
    key = jax.random.key(seed)
    ks = jax.random.split(key, 16)
    f32 = jnp.float32
    x = jax.random.normal(ks[0], (BATCH, SEQ, D_MODEL), f32)
    pre_norm_g = 1.0 + 0.05 * jax.random.normal(ks[1], (DEPTH, D_MODEL), f32)
    w_in = jax.random.normal(ks[2], (DEPTH, D_MODEL, D_IN), f32) * D_MODEL ** -0.5
    b_gate = 0.01 * jax.random.normal(ks[3], (DEPTH, N_BRANCH * D_MODEL), f32)
    conv_w = jax.random.normal(ks[4], (DEPTH, CONV_W, D_RNN), f32) * CONV_W ** -0.5
    conv_b = 0.01 * jax.random.normal(ks[5], (DEPTH, D_RNN), f32)
    w_rg_a = jax.random.normal(ks[6], (DEPTH, RNN_BLOCKS, RNN_BLOCK_W, RNN_BLOCK_W), f32) * RNN_BLOCK_W ** -0.5
    b_rg_a = 0.01 * jax.random.normal(ks[7], (DEPTH, RNN_BLOCKS, RNN_BLOCK_W), f32)
    w_rg_x = jax.random.normal(ks[8], (DEPTH, RNN_BLOCKS, RNN_BLOCK_W, RNN_BLOCK_W), f32) * RNN_BLOCK_W ** -0.5
    b_rg_x = 0.01 * jax.random.normal(ks[9], (DEPTH, RNN_BLOCKS, RNN_BLOCK_W), f32)
    a0 = jax.random.uniform(ks[10], (DEPTH, D_RNN), f32, minval=0.9, maxval=0.999)
    a_base = a0 ** (1.0 / LRU_C)
    lru_lambda = jnp.log(a_base) - jnp.log1p(-a_base)
    attn_sinks = 0.5 * jax.random.normal(ks[11], (DEPTH, N_Q_HEADS), f32)
    w_rnn_out = jax.random.normal(ks[12], (DEPTH, D_RNN, D_MODEL), f32) * D_RNN ** -0.5
    w_attn_out = jax.random.normal(ks[13], (DEPTH, D_ATTN, D_MODEL), f32) * D_ATTN ** -0.5
    w_out = jax.random.normal(ks[14], (DEPTH, D_MODEL, D_MODEL), f32) * D_MODEL ** -0.5
    post_norm_g = 1.0 + 0.05 * jax.random.normal(ks[15], (DEPTH, D_MODEL), f32)
    return {"x": x, "pre_norm_g": pre_norm_g, "w_in": w_in, "b_gate": b_gate,
            "conv_w": conv_w, "conv_b": conv_b, "w_rg_a": w_rg_a, "b_rg_a": b_rg_a,
            "w_rg_x": w_rg_x, "b_rg_x": b_rg_x, "lru_lambda": lru_lambda,
            "attn_sinks": attn_sinks, "w_rnn_out": w_rnn_out, "w_attn_out": w_attn_out,
            "w_out": w_out, "post_norm_g": post_norm_g}


def _fwd_reference(x, pre_norm_g, w_in, b_gate, conv_w, conv_b, w_rg_a, b_rg_a, w_rg_x, b_rg_x,
              lru_lambda, attn_sinks, w_rnn_out, w_attn_out, w_out, post_norm_g):
    B, T, _ = x.shape
    for l in range(DEPTH):
        h = rms_norm(x, pre_norm_g[l])
        proj = h @ w_in[l]
        rnn_x, rnn_gate, q, k, v, attn_gate, merge_logits = jnp.split(proj, SPLIT_POINTS, axis=-1)

        c = causal_depthwise_conv(rnn_x, conv_w[l], conv_b[l])
        y_rnn = rg_lru(c, w_rg_a[l], b_rg_a[l], w_rg_x[l], b_rg_x[l], lru_lambda[l])
        br_rnn = (y_rnn * jax.nn.silu(rnn_gate)) @ w_rnn_out[l]

        qh = q.reshape(B, T, N_Q_HEADS, HEAD_DIM)
        kh = k.reshape(B, T, N_KV_HEADS, HEAD_DIM)
        vh = v.reshape(B, T, N_KV_HEADS, HEAD_DIM)
        y_attn = sliding_window_sink_attention(qh, kh, vh, attn_sinks[l])
        br_attn = (y_attn * jax.nn.silu(attn_gate)) @ w_attn_out[l]

        g_rnn, g_attn = jnp.split(jax.nn.sigmoid(merge_logits + b_gate[l]), N_BRANCH, axis=-1)
        merged = g_rnn * br_rnn + g_attn * br_attn
        out = merged @ w_out[l]
        x = x + rms_norm(out, post_norm_g[l])
    return x


import jax as _jax
import jax.numpy as _jnp

TWIN_FORMAT = 'train_step'
FWD_PARAMS = ['x', 'pre_norm_g', 'w_in', 'b_gate', 'conv_w', 'conv_b', 'w_rg_a', 'b_rg_a', 'w_rg_x', 'b_rg_x', 'lru_lambda', 'attn_sinks', 'w_rnn_out', 'w_attn_out', 'w_out', 'post_norm_g']
TWIN_WEIGHTS = ['pre_norm_g', 'w_in', 'b_gate', 'conv_w', 'conv_b', 'w_rg_a', 'b_rg_a', 'w_rg_x', 'b_rg_x', 'lru_lambda', 'attn_sinks', 'w_rnn_out', 'w_attn_out', 'w_out', 'post_norm_g']
TWIN_DIFF_INPUT = 'x'
TWIN_INPUTS = ['x', 'pre_norm_g', 'w_in', 'b_gate', 'conv_w', 'conv_b', 'w_rg_a', 'b_rg_a', 'w_rg_x', 'b_rg_x', 'lru_lambda', 'attn_sinks', 'w_rnn_out', 'w_attn_out', 'w_out', 'post_norm_g', 'loss_target', 'm_pre_norm_g', 'm_w_in', 'm_b_gate', 'm_conv_w', 'm_conv_b', 'm_w_rg_a', 'm_b_rg_a', 'm_w_rg_x', 'm_b_rg_x', 'm_lru_lambda', 'm_attn_sinks', 'm_w_rnn_out', 'm_w_attn_out', 'm_w_out', 'm_post_norm_g', 'v_pre_norm_g', 'v_w_in', 'v_b_gate', 'v_conv_w', 'v_conv_b', 'v_w_rg_a', 'v_b_rg_a', 'v_w_rg_x', 'v_b_rg_x', 'v_lru_lambda', 'v_attn_sinks', 'v_w_rnn_out', 'v_w_attn_out', 'v_w_out', 'v_post_norm_g']
TWIN_OUTPUTS = ['loss', 'grad_x', 'grad_pre_norm_g', 'grad_w_in', 'grad_b_gate', 'grad_conv_w', 'grad_conv_b', 'grad_w_rg_a', 'grad_b_rg_a', 'grad_w_rg_x', 'grad_b_rg_x', 'grad_lru_lambda', 'grad_attn_sinks', 'grad_w_rnn_out', 'grad_w_attn_out', 'grad_w_out', 'grad_post_norm_g', 'delta_pre_norm_g', 'delta_w_in', 'delta_b_gate', 'delta_conv_w', 'delta_conv_b', 'delta_w_rg_a', 'delta_b_rg_a', 'delta_w_rg_x', 'delta_b_rg_x', 'delta_lru_lambda', 'delta_attn_sinks', 'delta_w_rnn_out', 'delta_w_attn_out', 'delta_w_out', 'delta_post_norm_g', 'new_m_pre_norm_g', 'new_m_w_in', 'new_m_b_gate', 'new_m_conv_w', 'new_m_conv_b', 'new_m_w_rg_a', 'new_m_b_rg_a', 'new_m_w_rg_x', 'new_m_b_rg_x', 'new_m_lru_lambda', 'new_m_attn_sinks', 'new_m_w_rnn_out', 'new_m_w_attn_out', 'new_m_w_out', 'new_m_post_norm_g', 'new_v_pre_norm_g', 'new_v_w_in', 'new_v_b_gate', 'new_v_conv_w', 'new_v_conv_b', 'new_v_w_rg_a', 'new_v_b_rg_a', 'new_v_w_rg_x', 'new_v_b_rg_x', 'new_v_lru_lambda', 'new_v_attn_sinks', 'new_v_w_rnn_out', 'new_v_w_attn_out', 'new_v_w_out', 'new_v_post_norm_g']
TWIN_LEAF_KINDS = {'loss': 'loss', 'grad_x': 'grad_x', 'grad_pre_norm_g': 'grad_w', 'grad_w_in': 'grad_w', 'grad_b_gate': 'grad_w', 'grad_conv_w': 'grad_w', 'grad_conv_b': 'grad_w', 'grad_w_rg_a': 'grad_w', 'grad_b_rg_a': 'grad_w', 'grad_w_rg_x': 'grad_w', 'grad_b_rg_x': 'grad_w', 'grad_lru_lambda': 'grad_w', 'grad_attn_sinks': 'grad_w', 'grad_w_rnn_out': 'grad_w', 'grad_w_attn_out': 'grad_w', 'grad_w_out': 'grad_w', 'grad_post_norm_g': 'grad_w', 'delta_pre_norm_g': 'delta_w', 'delta_w_in': 'delta_w', 'delta_b_gate': 'delta_w', 'delta_conv_w': 'delta_w', 'delta_conv_b': 'delta_w', 'delta_w_rg_a': 'delta_w', 'delta_b_rg_a': 'delta_w', 'delta_w_rg_x': 'delta_w', 'delta_b_rg_x': 'delta_w', 'delta_lru_lambda': 'delta_w', 'delta_attn_sinks': 'delta_w', 'delta_w_rnn_out': 'delta_w', 'delta_w_attn_out': 'delta_w', 'delta_w_out': 'delta_w', 'delta_post_norm_g': 'delta_w', 'new_m_pre_norm_g': 'new_m', 'new_m_w_in': 'new_m', 'new_m_b_gate': 'new_m', 'new_m_conv_w': 'new_m', 'new_m_conv_b': 'new_m', 'new_m_w_rg_a': 'new_m', 'new_m_b_rg_a': 'new_m', 'new_m_w_rg_x': 'new_m', 'new_m_b_rg_x': 'new_m', 'new_m_lru_lambda': 'new_m', 'new_m_attn_sinks': 'new_m', 'new_m_w_rnn_out': 'new_m', 'new_m_w_attn_out': 'new_m', 'new_m_w_out': 'new_m', 'new_m_post_norm_g': 'new_m', 'new_v_pre_norm_g': 'new_v', 'new_v_w_in': 'new_v', 'new_v_b_gate': 'new_v', 'new_v_conv_w': 'new_v', 'new_v_conv_b': 'new_v', 'new_v_w_rg_a': 'new_v', 'new_v_b_rg_a': 'new_v', 'new_v_w_rg_x': 'new_v', 'new_v_b_rg_x': 'new_v', 'new_v_lru_lambda': 'new_v', 'new_v_attn_sinks': 'new_v', 'new_v_w_rnn_out': 'new_v', 'new_v_w_attn_out': 'new_v', 'new_v_w_out': 'new_v', 'new_v_post_norm_g': 'new_v'}


def _forward(args):
    return _fwd_reference(*[args[k] for k in FWD_PARAMS])


def _output_shape():
    out = _jax.eval_shape(lambda: _forward(_fwd_setup_inputs(0)))
    return out.shape, out.dtype

N_MICROBATCH = 1
ADAM_LR = 0.001
ADAM_B1 = 0.9
ADAM_B2 = 0.999
ADAM_EPS = 1e-08
ADAM_WD = 0.01
ADAM_STEP = 10
PER_EXAMPLE_BATCH_AXIS = {'x': 0, 'loss_target': 0}
SHARED_INPUTS = []
_WEIGHT_DTYPES = {'pre_norm_g': _jnp.float32, 'w_in': _jnp.float32, 'b_gate': _jnp.float32, 'conv_w': _jnp.float32, 'conv_b': _jnp.float32, 'w_rg_a': _jnp.float32, 'b_rg_a': _jnp.float32, 'w_rg_x': _jnp.float32, 'b_rg_x': _jnp.float32, 'lru_lambda': _jnp.float32, 'attn_sinks': _jnp.float32, 'w_rnn_out': _jnp.float32, 'w_attn_out': _jnp.float32, 'w_out': _jnp.float32, 'post_norm_g': _jnp.float32}
MOMENT_SCALE = {'pre_norm_g': 3.987135e-01, 'w_in': 1.687490e-01, 'b_gate': 1.224375e-01, 'conv_w': 3.335921e-01, 'conv_b': 5.646521e+00, 'w_rg_a': 1.939383e-01, 'b_rg_a': 1.214772e-01, 'w_rg_x': 3.592738e-01, 'b_rg_x': 9.139338e-02, 'lru_lambda': 1.604733e-01, 'attn_sinks': 1.758621e-01, 'w_rnn_out': 3.733835e-01, 'w_attn_out': 1.245364e-01, 'w_out': 3.704077e-01, 'post_norm_g': 3.209720e+01}


def _to_microbatches(a, axis):
    t = _jnp.moveaxis(a, axis, 0)
    t = t.reshape((N_MICROBATCH, t.shape[0] // N_MICROBATCH) + t.shape[1:])
    return _jnp.moveaxis(t, 1, axis + 1)


def setup_inputs(seed: int = 0) -> dict:
    inp = _fwd_setup_inputs(seed)
    key = _jax.random.fold_in(_jax.random.key(seed), 7919)
    shape, _ = _output_shape()
    out = dict(inp)
    out["loss_target"] = _jax.random.normal(_jax.random.fold_in(key, 0), shape, _jnp.float32)
    for i, name in enumerate(TWIN_WEIGHTS):
        w = inp[name].astype(_jnp.float32)
        if MOMENT_SCALE is None:
            s = _jnp.sqrt(_jnp.mean(_jnp.square(w)) + 1e-30)
        else:
            s = MOMENT_SCALE[name]
        km, kv = _jax.random.split(_jax.random.fold_in(key, i + 1))
        out[name] = w
        out["m_" + name] = s * _jax.random.normal(km, w.shape, _jnp.float32)
        out["v_" + name] = (s * s) * _jax.random.uniform(kv, w.shape, _jnp.float32, 0.5, 1.5)
    if N_MICROBATCH > 1:
        for name, axis in PER_EXAMPLE_BATCH_AXIS.items():
            out[name] = _to_microbatches(out[name], axis)
    return {'x': out['x'], 'pre_norm_g': out['pre_norm_g'], 'w_in': out['w_in'], 'b_gate': out['b_gate'], 'conv_w': out['conv_w'], 'conv_b': out['conv_b'], 'w_rg_a': out['w_rg_a'], 'b_rg_a': out['b_rg_a'], 'w_rg_x': out['w_rg_x'], 'b_rg_x': out['b_rg_x'], 'lru_lambda': out['lru_lambda'], 'attn_sinks': out['attn_sinks'], 'w_rnn_out': out['w_rnn_out'], 'w_attn_out': out['w_attn_out'], 'w_out': out['w_out'], 'post_norm_g': out['post_norm_g'], 'loss_target': out['loss_target'], 'm_pre_norm_g': out['m_pre_norm_g'], 'm_w_in': out['m_w_in'], 'm_b_gate': out['m_b_gate'], 'm_conv_w': out['m_conv_w'], 'm_conv_b': out['m_conv_b'], 'm_w_rg_a': out['m_w_rg_a'], 'm_b_rg_a': out['m_b_rg_a'], 'm_w_rg_x': out['m_w_rg_x'], 'm_b_rg_x': out['m_b_rg_x'], 'm_lru_lambda': out['m_lru_lambda'], 'm_attn_sinks': out['m_attn_sinks'], 'm_w_rnn_out': out['m_w_rnn_out'], 'm_w_attn_out': out['m_w_attn_out'], 'm_w_out': out['m_w_out'], 'm_post_norm_g': out['m_post_norm_g'], 'v_pre_norm_g': out['v_pre_norm_g'], 'v_w_in': out['v_w_in'], 'v_b_gate': out['v_b_gate'], 'v_conv_w': out['v_conv_w'], 'v_conv_b': out['v_conv_b'], 'v_w_rg_a': out['v_w_rg_a'], 'v_b_rg_a': out['v_b_rg_a'], 'v_w_rg_x': out['v_w_rg_x'], 'v_b_rg_x': out['v_b_rg_x'], 'v_lru_lambda': out['v_lru_lambda'], 'v_attn_sinks': out['v_attn_sinks'], 'v_w_rnn_out': out['v_w_rnn_out'], 'v_w_attn_out': out['v_w_attn_out'], 'v_w_out': out['v_w_out'], 'v_post_norm_g': out['v_post_norm_g']}


def _loss(weights, diff, rest, loss_target):
    with _jax.named_scope("forward"):
        args = {**rest, TWIN_DIFF_INPUT: diff, **{k: w.astype(_WEIGHT_DTYPES[k]) for k, w in weights.items()}}
        y = _forward(args)
    with _jax.named_scope("loss_head"):
        err = _jnp.square(y.astype(_jnp.float32) - loss_target)
        return 0.5 * _jnp.sum(_jnp.mean(err, axis=-1)) if err.ndim else 0.5 * err


def _adamw(w, g, m, v):
    m = ADAM_B1 * m + (1.0 - ADAM_B1) * g
    v = ADAM_B2 * v + (1.0 - ADAM_B2) * _jnp.square(g)
    m_hat = m / (1.0 - ADAM_B1 ** ADAM_STEP)
    v_hat = v / (1.0 - ADAM_B2 ** ADAM_STEP)
    delta = -ADAM_LR * (m_hat / (_jnp.sqrt(v_hat) + ADAM_EPS) + ADAM_WD * w)
    return delta, m, v


def reference(x, pre_norm_g, w_in, b_gate, conv_w, conv_b, w_rg_a, b_rg_a, w_rg_x, b_rg_x, lru_lambda, attn_sinks, w_rnn_out, w_attn_out, w_out, post_norm_g, loss_target, m_pre_norm_g, m_w_in, m_b_gate, m_conv_w, m_conv_b, m_w_rg_a, m_b_rg_a, m_w_rg_x, m_b_rg_x, m_lru_lambda, m_attn_sinks, m_w_rnn_out, m_w_attn_out, m_w_out, m_post_norm_g, v_pre_norm_g, v_w_in, v_b_gate, v_conv_w, v_conv_b, v_w_rg_a, v_b_rg_a, v_w_rg_x, v_b_rg_x, v_lru_lambda, v_attn_sinks, v_w_rnn_out, v_w_attn_out, v_w_out, v_post_norm_g):
    given = dict(x=x, pre_norm_g=pre_norm_g, w_in=w_in, b_gate=b_gate, conv_w=conv_w, conv_b=conv_b, w_rg_a=w_rg_a, b_rg_a=b_rg_a, w_rg_x=w_rg_x, b_rg_x=b_rg_x, lru_lambda=lru_lambda, attn_sinks=attn_sinks, w_rnn_out=w_rnn_out, w_attn_out=w_attn_out, w_out=w_out, post_norm_g=post_norm_g, loss_target=loss_target, m_pre_norm_g=m_pre_norm_g, m_w_in=m_w_in, m_b_gate=m_b_gate, m_conv_w=m_conv_w, m_conv_b=m_conv_b, m_w_rg_a=m_w_rg_a, m_b_rg_a=m_b_rg_a, m_w_rg_x=m_w_rg_x, m_b_rg_x=m_b_rg_x, m_lru_lambda=m_lru_lambda, m_attn_sinks=m_attn_sinks, m_w_rnn_out=m_w_rnn_out, m_w_attn_out=m_w_attn_out, m_w_out=m_w_out, m_post_norm_g=m_post_norm_g, v_pre_norm_g=v_pre_norm_g, v_w_in=v_w_in, v_b_gate=v_b_gate, v_conv_w=v_conv_w, v_conv_b=v_conv_b, v_w_rg_a=v_w_rg_a, v_b_rg_a=v_b_rg_a, v_w_rg_x=v_w_rg_x, v_b_rg_x=v_b_rg_x, v_lru_lambda=v_lru_lambda, v_attn_sinks=v_attn_sinks, v_w_rnn_out=v_w_rnn_out, v_w_attn_out=v_w_attn_out, v_w_out=v_w_out, v_post_norm_g=v_post_norm_g)
    weights = {n: given[n] for n in TWIN_WEIGHTS}
    shared = {n: given[n] for n in SHARED_INPUTS}
    per_example = {n: given[n] for n in ['x']}
    grad_fn = _jax.value_and_grad(_loss, argnums=(0, 1))

    def one_microbatch(ex, loss_target):
        ex = dict(ex)
        diff = ex.pop(TWIN_DIFF_INPUT)
        return grad_fn(weights, diff, {**shared, **ex}, loss_target)

    if N_MICROBATCH == 1:
        loss, (grad_w, grad_x) = one_microbatch(per_example, given["loss_target"])
    else:
        def body(carry, xs):
            loss_sum, grad_sum = carry
            l_k, (gw_k, gx_k) = one_microbatch(xs[0], xs[1])
            with _jax.named_scope("update"):
                return (loss_sum + l_k, _jax.tree.map(_jnp.add, grad_sum, gw_k)), gx_k

        init = (_jnp.zeros((), _jnp.float32), _jax.tree.map(_jnp.zeros_like, weights))
        (loss, grad_w), grad_x = _jax.lax.scan(body, init, (per_example, given["loss_target"]))
    with _jax.named_scope("update"):
        delta_w, new_m, new_v = {}, {}, {}
        for n in TWIN_WEIGHTS:
            delta_w[n], new_m[n], new_v[n] = _adamw(weights[n], grad_w[n], given["m_" + n], given["v_" + n])
    return (loss, grad_x, *[grad_w[n] for n in TWIN_WEIGHTS], *[delta_w[n] for n in TWIN_WEIGHTS],
            *[new_m[n] for n in TWIN_WEIGHTS], *[new_v[n] for n in TWIN_WEIGHTS])
```

```python
import functools

import jax
import jax.numpy as jnp
from jax import lax
from jax.experimental import pallas as pl
from jax.experimental.pallas import tpu as pltpu

F32 = jnp.float32
BF16 = jnp.bfloat16

D_MODEL = 1024
D_RNN = 1024
RNN_BLOCKS = 16
RNN_BLOCK_W = 64
LRU_C = 8.0
N_Q_HEADS = 16
HEAD_DIM = 64
D_KV = 256
BLOCK = 128
ALIBI_MAX_BIAS = 8.0
EPS = 1e-6
D_IN = 6656
N_DEV = 8
N_CHIP = 4
SHARD_IN = D_IN // N_DEV
SHARD_OUT = D_MODEL // N_DEV
ATTN_SCALE = HEAD_DIM ** -0.5
MASKED = -1e30

ADAM_LR = 0.001
ADAM_B1 = 0.9
ADAM_B2 = 0.999
ADAM_EPS = 1e-08
ADAM_WD = 0.01
ADAM_STEP = 10

VMEM_LIMIT_BYTES = 52 * 1024 * 1024
LANE = 128
GROUP_W = 256
N_GROUPS = D_RNN // GROUP_W
SEG_CHUNK = 512

NT_DIMS = (((1,), (1,)), ((), ()))
TN_DIMS = (((0,), (0,)), ((), ()))
MESH = pl.DeviceIdType.MESH
ANY = pl.BlockSpec(memory_space=pl.ANY)


def _params(*semantics):
    return pltpu.CompilerParams(dimension_semantics=semantics, vmem_limit_bytes=VMEM_LIMIT_BYTES)


def _sigmoid(x):
    return 1.0 / (1.0 + jnp.exp(-x))


def _log1p(e):
    u = 1.0 + e
    den = jnp.where(u == 1.0, 1.0, u - 1.0)
    return jnp.where(u == 1.0, e, jnp.log(u) * (e / den))


def _softplus(z):
    return jnp.maximum(z, 0.0) + _log1p(jnp.exp(-jnp.abs(z)))


def _neg_expm1(x):
    series = x * (1.0 + x * (0.5 + x * (1.0 / 6.0 + x * (1.0 / 24.0))))
    return jnp.where(x > -0.02, -series, 1.0 - jnp.exp(x))


def _rows8(rows, width):
    idx = lax.broadcasted_iota(jnp.int32, (8, width), 0)
    out = jnp.zeros((8, width), F32)
    for r, v in enumerate(rows):
        out = jnp.where(idx == r, v, out)
    return out


def _mm_nt(a, b, b_row_off, n, out_dtype, name, tm=1024, tn=512):
    m, k = a.shape
    tm = min(tm, m)
    off = b_row_off // tn

    def body(a_ref, b_ref, o_ref):
        o_ref[...] = lax.dot_general(a_ref[...], b_ref[...], NT_DIMS, preferred_element_type=F32).astype(o_ref.dtype)

    return pl.pallas_call(
        body, name=name, grid=(m // tm, n // tn),
        in_specs=[pl.BlockSpec((tm, k), lambda i, j: (i, 0)), pl.BlockSpec((tn, k), lambda i, j: (j + off, 0))],
        out_specs=pl.BlockSpec((tm, tn), lambda i, j: (i, j)),
        out_shape=jax.ShapeDtypeStruct((m, n), out_dtype),
        compiler_params=_params("parallel", "parallel"))(a, b)


def _mm_nn(a, b, out_dtype, name, tm=1024, tn=512):
    m, k = a.shape
    n = b.shape[1]
    tm = min(tm, m)

    def body(a_ref, b_ref, o_ref):
        o_ref[...] = jnp.dot(a_ref[...], b_ref[...], preferred_element_type=F32).astype(o_ref.dtype)

    return pl.pallas_call(
        body, name=name, grid=(m // tm, n // tn),
        in_specs=[pl.BlockSpec((tm, k), lambda i, j: (i, 0)), pl.BlockSpec((k, tn), lambda i, j: (0, j))],
        out_specs=pl.BlockSpec((tm, tn), lambda i, j: (i, j)),
        out_shape=jax.ShapeDtypeStruct((m, n), out_dtype),
        compiler_params=_params("parallel", "parallel"))(a, b)


def _mm_tn(a, b, name, tm=512, tk=1024):
    ktok, m = a.shape
    n = b.shape[1]
    tk = min(tk, ktok)

    def body(a_ref, b_ref, o_ref):
        @pl.when(pl.program_id(1) == 0)
        def _():
            o_ref[...] = jnp.zeros_like(o_ref)

        o_ref[...] += lax.dot_general(a_ref[...], b_ref[...], TN_DIMS, preferred_element_type=F32)

    return pl.pallas_call(
        body, name=name, grid=(m // tm, ktok // tk),
        in_specs=[pl.BlockSpec((tk, tm), lambda i, kk: (kk, i)), pl.BlockSpec((tk, n), lambda i, kk: (kk, 0))],
        out_specs=pl.BlockSpec((tm, n), lambda i, kk: (i, 0)),
        out_shape=jax.ShapeDtypeStruct((m, n), F32),
        compiler_params=_params("parallel", "arbitrary"))(a, b)


def _segment_chunks(segs):
    bounds = [0]
    for s in segs:
        bounds.append(bounds[-1] + s.shape[1] // SEG_CHUNK)
    return bounds


def _seg_index_map(lo, hi, tok_axis_first):
    if tok_axis_first:
        return lambda i, c: (i, jnp.clip(c - lo, 0, hi - lo - 1))
    return lambda c, kk: (jnp.where((c >= lo) & (c < hi), kk, 0), jnp.clip(c - lo, 0, hi - lo - 1))


def _mm_nn_seg(segs, wt, name, tm=1024):
    m = segs[0].shape[0]
    n = wt.shape[1]
    tm = min(tm, m)
    bounds = _segment_chunks(segs)
    n_seg = len(segs)

    def body(*refs):
        a_refs, w_ref, o_ref = refs[:n_seg], refs[n_seg], refs[n_seg + 1]
        c = pl.program_id(1)

        @pl.when(c == 0)
        def _():
            o_ref[...] = jnp.zeros_like(o_ref)

        for s in range(n_seg):
            @pl.when((c >= bounds[s]) & (c < bounds[s + 1]))
            def _(s=s):
                o_ref[...] += jnp.dot(a_refs[s][...], w_ref[...], preferred_element_type=F32)

    in_specs = [pl.BlockSpec((tm, SEG_CHUNK), _seg_index_map(bounds[s], bounds[s + 1], True)) for s in range(n_seg)]
    in_specs.append(pl.BlockSpec((SEG_CHUNK, n), lambda i, c: (c, 0)))
    return pl.pallas_call(
        body, name=name, grid=(m // tm, bounds[-1]),
        in_specs=in_specs, out_specs=pl.BlockSpec((tm, n), lambda i, c: (i, 0)),
        out_shape=jax.ShapeDtypeStruct((m, n), F32),
        compiler_params=_params("parallel", "arbitrary"))(*segs, wt)


def _mm_tn_seg(segs, b, name, tk=1024):
    ktok = segs[0].shape[0]
    n = b.shape[1]
    tk = min(tk, ktok)
    bounds = _segment_chunks(segs)
    n_seg = len(segs)

    def body(*refs):
        a_refs, b_ref, o_ref = refs[:n_seg], refs[n_seg], refs[n_seg + 1]
        c = pl.program_id(0)
        kk = pl.program_id(1)

        @pl.when(kk == 0)
        def _():
            o_ref[...] = jnp.zeros_like(o_ref)

        rows = b_ref[pl.ds(pl.multiple_of(kk * tk, tk), tk), :]
        for s in range(n_seg):
            @pl.when((c >= bounds[s]) & (c < bounds[s + 1]))
            def _(s=s):
                o_ref[...] += lax.dot_general(a_refs[s][...], rows, TN_DIMS, preferred_element_type=F32)

    in_specs = [pl.BlockSpec((tk, SEG_CHUNK), _seg_index_map(bounds[s], bounds[s + 1], False)) for s in range(n_seg)]
    in_specs.append(pl.BlockSpec((ktok, n), lambda c, kk: (0, 0)))
    return pl.pallas_call(
        body, name=name, grid=(bounds[-1], ktok // tk),
        in_specs=in_specs, out_specs=pl.BlockSpec((SEG_CHUNK, n), lambda c, kk: (c, 0)),
        out_shape=jax.ShapeDtypeStruct((bounds[-1] * SEG_CHUNK, n), F32),
        compiler_params=_params("parallel", "arbitrary"))(*segs, b)


def _prenorm(x, g, name, tm=512):
    t, d = x.shape
    tm = min(tm, t)

    def body(x_ref, g_ref, h_ref):
        x = x_ref[...]
        r = lax.rsqrt(jnp.mean(x * x, axis=-1, keepdims=True) + EPS)
        h_ref[...] = (x * r * g_ref[...]).astype(BF16)

    return pl.pallas_call(
        body, name=name, grid=(t // tm,),
        in_specs=[pl.BlockSpec((tm, d), lambda i: (i, 0)), pl.BlockSpec((1, d), lambda i: (0, 0))],
        out_specs=pl.BlockSpec((tm, d), lambda i: (i, 0)),
        out_shape=jax.ShapeDtypeStruct((t, d), BF16),
        compiler_params=_params("parallel"))(x, g)


def _prenorm_bwd(x, g, dh, dy, name, tm=512):
    t, d = x.shape
    tm = min(tm, t)

    def body(x_ref, g_ref, dh_ref, dy_ref, gx_ref, st_ref):
        @pl.when(pl.program_id(0) == 0)
        def _():
            st_ref[...] = jnp.zeros_like(st_ref)

        x = x_ref[...]
        r = lax.rsqrt(jnp.mean(x * x, axis=-1, keepdims=True) + EPS)
        xn = x * r
        dh = dh_ref[...]
        dxn = dh * g_ref[...]
        dx = r * (dxn - xn * jnp.mean(dxn * xn, axis=-1, keepdims=True))
        gx_ref[...] = dy_ref[...] + dx
        st_ref[...] += _rows8([jnp.sum(dh * xn, axis=0, keepdims=True)], d)

    tile = pl.BlockSpec((tm, d), lambda i: (i, 0))
    return pl.pallas_call(
        body, name=name, grid=(t // tm,),
        in_specs=[tile, pl.BlockSpec((1, d), lambda i: (0, 0)), tile, tile],
        out_specs=[tile, pl.BlockSpec((8, d), lambda i: (0, 0))],
        out_shape=[jax.ShapeDtypeStruct((t, d), F32), jax.ShapeDtypeStruct((8, d), F32)],
        compiler_params=_params("arbitrary"))(x, g, dh, dy)


def _loss_head(x, out, target, g, name, tm=512):
    t, d = x.shape
    tm = min(tm, t)

    def body(x_ref, o_ref, t_ref, g_ref, dy_ref, do_ref, st_ref):
        @pl.when(pl.program_id(0) == 0)
        def _():
            st_ref[...] = jnp.zeros_like(st_ref)

        o = o_ref[...]
        g = g_ref[...]
        r = lax.rsqrt(jnp.mean(o * o, axis=-1, keepdims=True) + EPS)
        nrm = o * r
        err = x_ref[...] + nrm * g - t_ref[...]
        dy = err * (1.0 / d)
        dy_ref[...] = dy
        dn = dy * g
        do_ref[...] = (r * (dn - nrm * jnp.mean(dn * nrm, axis=-1, keepdims=True))).astype(BF16)
        st_ref[...] += _rows8([jnp.sum(dy * nrm, axis=0, keepdims=True), jnp.sum(err * err, axis=0, keepdims=True)], d)

    tile = pl.BlockSpec((tm, d), lambda i: (i, 0))
    return pl.pallas_call(
        body, name=name, grid=(t // tm,),
        in_specs=[tile, tile, tile, pl.BlockSpec((1, d), lambda i: (0, 0))],
        out_specs=[tile, tile, pl.BlockSpec((8, d), lambda i: (0, 0))],
        out_shape=[jax.ShapeDtypeStruct((t, d), F32), jax.ShapeDtypeStruct((t, d), BF16), jax.ShapeDtypeStruct((8, d), F32)],
        compiler_params=_params("arbitrary"))(x, out, target, g)


def _merge(br_rnn, br_attn, ag_ml, b_gate, name, tm=512):
    t, d = br_rnn.shape
    tm = min(tm, t)

    def body(r_ref, a_ref, lr_ref, la_ref, br_ref, ba_ref, o_ref):
        g_rnn = _sigmoid(lr_ref[...] + br_ref[...])
        g_attn = _sigmoid(la_ref[...] + ba_ref[...])
        o_ref[...] = (g_rnn * r_ref[...] + g_attn * a_ref[...]).astype(BF16)

    tile = pl.BlockSpec((tm, d), lambda i: (i, 0))
    return pl.pallas_call(
        body, name=name, grid=(t // tm,),
        in_specs=[tile, tile, pl.BlockSpec((tm, d), lambda i: (i, 1)), pl.BlockSpec((tm, d), lambda i: (i, 2)),
                  pl.BlockSpec((1, d), lambda i: (0, 0)), pl.BlockSpec((1, d), lambda i: (0, 1))],
        out_specs=tile, out_shape=jax.ShapeDtypeStruct((t, d), BF16),
        compiler_params=_params("parallel"))(br_rnn, br_attn, ag_ml, ag_ml, b_gate, b_gate)


def _merge_bwd(dmerged, br_rnn, br_attn, ag_ml, b_gate, name, tm=512):
    t, d = br_rnn.shape
    tm = min(tm, t)

    def body(dm_ref, r_ref, a_ref, lr_ref, la_ref, br_ref, ba_ref, dr_ref, da_ref, dl_ref, st_ref):
        @pl.when(pl.program_id(0) == 0)
        def _():
            st_ref[...] = jnp.zeros_like(st_ref)

        dm = dm_ref[...]
        g_rnn = _sigmoid(lr_ref[...] + br_ref[...])
        g_attn = _sigmoid(la_ref[...] + ba_ref[...])
        dr_ref[...] = (dm * g_rnn).astype(BF16)
        da_ref[...] = (dm * g_attn).astype(BF16)
        dl_rnn = dm * r_ref[...] * g_rnn * (1.0 - g_rnn)
        dl_attn = dm * a_ref[...] * g_attn * (1.0 - g_attn)
        dl_ref[:, 0:d] = dl_rnn.astype(BF16)
        dl_ref[:, d:2 * d] = dl_attn.astype(BF16)
        st_ref[...] += _rows8([jnp.sum(dl_rnn, axis=0, keepdims=True), jnp.sum(dl_attn, axis=0, keepdims=True)], d)

    tile = pl.BlockSpec((tm, d), lambda i: (i, 0))
    return pl.pallas_call(
        body, name=name, grid=(t // tm,),
        in_specs=[tile, tile, tile, pl.BlockSpec((tm, d), lambda i: (i, 1)), pl.BlockSpec((tm, d), lambda i: (i, 2)),
                  pl.BlockSpec((1, d), lambda i: (0, 0)), pl.BlockSpec((1, d), lambda i: (0, 1))],
        out_specs=[tile, tile, pl.BlockSpec((tm, 2 * d), lambda i: (i, 0)), pl.BlockSpec((8, d), lambda i: (0, 0))],
        out_shape=[jax.ShapeDtypeStruct((t, d), BF16), jax.ShapeDtypeStruct((t, d), BF16),
                   jax.ShapeDtypeStruct((t, 2 * d), BF16), jax.ShapeDtypeStruct((8, d), F32)],
        compiler_params=_params("arbitrary"))(dmerged, br_rnn, br_attn, ag_ml, ag_ml, b_gate, b_gate)


def _lru_gates(c, wa, ba, wx, bx, sp):
    cb = c.astype(BF16)
    r = _sigmoid(jnp.dot(cb, wa, preferred_element_type=F32) + ba)
    ig = _sigmoid(jnp.dot(cb, wx, preferred_element_type=F32) + bx)
    log_a = (-LRU_C) * r * sp
    a = jnp.exp(log_a)
    mult = jnp.sqrt(_neg_expm1(2.0 * log_a))
    return cb, r, ig, a, mult


def _scan_fwd(a, u, tt):
    row = lax.broadcasted_iota(jnp.int32, a.shape, 0)
    d = 1
    while d < tt:
        keep = row >= d
        u = u + a * jnp.where(keep, pltpu.roll(u, d, 0), 0.0)
        if 2 * d < tt:
            a = a * jnp.where(keep, pltpu.roll(a, d, 0), 1.0)
        d *= 2
    return u


def _scan_rev(b, g, tt):
    row = lax.broadcasted_iota(jnp.int32, b.shape, 0)
    d = 1
    while d < tt:
        keep = row < tt - d
        g = g + b * jnp.where(keep, pltpu.roll(g, tt - d, 0), 0.0)
        if 2 * d < tt:
            b = b * jnp.where(keep, pltpu.roll(b, tt - d, 0), 1.0)
        d *= 2
    return g


def _conv_taps(cw, bias, x, ext_ref, tt):
    x2 = ext_ref[7:7 + tt, :]
    x1 = ext_ref[6:6 + tt, :]
    x0 = ext_ref[5:5 + tt, :]
    c = bias + cw[3:4] * x + cw[2:3] * x2 + cw[1:2] * x1 + cw[0:1] * x0
    return c, x2, x1, x0


def _rnn_fwd(rx_rg, cw, cb, wa, ba, wx, bx, lam, name, tt=512):
    t = rx_rg.shape[0]
    tt = min(tt, t)
    w = GROUP_W

    def body(rx_ref, rg_ref, cw_ref, cb_ref, wa_ref, ba_ref, wx_ref, bx_ref, lam_ref, y_ref, z_ref, ext_ref, hc_ref):
        @pl.when(pl.program_id(1) == 0)
        def _():
            ext_ref[0:8, :] = jnp.zeros((8, w), F32)
            hc_ref[...] = jnp.zeros((8, w), F32)

        x = rx_ref[...]
        ext_ref[8:8 + tt, :] = x
        c, _, _, _ = _conv_taps(cw_ref[...], cb_ref[...], x, ext_ref, tt)
        ext_ref[0:8, :] = x[tt - 8:tt, :]
        sp = _softplus(-lam_ref[...])
        _, _, ig, a, mult = _lru_gates(c, wa_ref[...], ba_ref[...], wx_ref[...], bx_ref[...], sp)
        u = mult * (ig * c)
        row = lax.broadcasted_iota(jnp.int32, (tt, w), 0)
        u = u + jnp.where(row == 0, a * hc_ref[7:8, :], 0.0)
        h = _scan_fwd(a, u, tt)
        hc_ref[...] = h[tt - 8:tt, :]
        y_ref[...] = h
        rg = rg_ref[...]
        z_ref[...] = (h * rg * _sigmoid(rg)).astype(BF16)

    vec = pl.BlockSpec((1, w), lambda g, i: (0, g))
    mat = pl.BlockSpec((None, w, w), lambda g, i: (g, 0, 0))
    tile = pl.BlockSpec((tt, w), lambda g, i: (i, g))
    return pl.pallas_call(
        body, name=name, grid=(N_GROUPS, t // tt),
        in_specs=[tile, pl.BlockSpec((tt, w), lambda g, i: (i, N_GROUPS + g)),
                  pl.BlockSpec((4, w), lambda g, i: (0, g)), vec, mat, vec, mat, vec, vec],
        out_specs=[tile, tile],
        out_shape=[jax.ShapeDtypeStruct((t, D_RNN), F32), jax.ShapeDtypeStruct((t, D_RNN), BF16)],
        scratch_shapes=[pltpu.VMEM((tt + 8, w), F32), pltpu.VMEM((8, w), F32)],
        compiler_params=_params("parallel", "arbitrary"))(rx_rg, rx_rg, cw, cb, wa, ba, wx, bx, lam)


def _rnn_bwd(rx_rg, y, dz, cw, cb, wa, ba, wx, bx, lam, name, tt=512):
    t = rx_rg.shape[0]
    tt = min(tt, t)
    nt = t // tt
    w = GROUP_W

    def body(rx_ref, rg_ref, rxt_ref, y_ref, yt_ref, dz_ref, cw_ref, cb_ref, wa_ref, ba_ref, wx_ref, bx_ref, lam_ref,
             drx_ref, drg_ref, st_ref, gda_ref, gdx_ref, ext_ref, dcx_ref, wcar_ref, dwa_ref, dwx_ref):
        ii = pl.program_id(1)

        @pl.when(ii == 0)
        def _():
            wcar_ref[...] = jnp.zeros((8, w), F32)
            dcx_ref[tt:tt + 8, :] = jnp.zeros((8, w), F32)
            st_ref[...] = jnp.zeros_like(st_ref)
            dwa_ref[...] = jnp.zeros_like(dwa_ref)
            dwx_ref[...] = jnp.zeros_like(dwx_ref)

        has_prev = jnp.where(ii == nt - 1, 0.0, 1.0)
        x = rx_ref[...]
        ext_ref[0:8, :] = rxt_ref[...] * has_prev
        ext_ref[8:8 + tt, :] = x
        cwv = cw_ref[...]
        c, x2, x1, x0 = _conv_taps(cwv, cb_ref[...], x, ext_ref, tt)
        lam = lam_ref[...]
        sp = _softplus(-lam)
        wa = wa_ref[...]
        wx = wx_ref[...]
        cb16, r, ig, a, mult = _lru_gates(c, wa, ba_ref[...], wx, bx_ref[...], sp)

        rg = rg_ref[...]
        sg = _sigmoid(rg)
        dz = dz_ref[...]
        yv = y_ref[...]
        drg_ref[...] = (dz * yv * (sg * (1.0 + rg * (1.0 - sg)))).astype(BF16)

        row = lax.broadcasted_iota(jnp.int32, (tt, w), 0)
        g = dz * (rg * sg) + jnp.where(row == tt - 1, wcar_ref[0:1, :], 0.0)
        b = jnp.where(row < tt - 1, pltpu.roll(a, tt - 1, 0), 0.0)
        dh = _scan_rev(b, g, tt)
        wcar_ref[...] = (a * dh)[0:8, :]

        hprev = jnp.where(row >= 1, pltpu.roll(yv, 1, 0), yt_ref[7:8, :] * has_prev)
        dmult = dh * (ig * c)
        dig = dh * mult * c
        dlog_a = dh * hprev * a - dmult * (a * a / mult)
        dpa = dlog_a * ((-LRU_C) * sp) * r * (1.0 - r)
        dpx = dig * ig * (1.0 - ig)
        dsp = jnp.sum(dlog_a * r, axis=0, keepdims=True) * (-LRU_C)
        dlam = dsp * (-_sigmoid(-lam))
        dpa16 = dpa.astype(BF16)
        dpx16 = dpx.astype(BF16)
        dwa_ref[...] += lax.dot_general(cb16, dpa16, TN_DIMS, preferred_element_type=F32)
        dwx_ref[...] += lax.dot_general(cb16, dpx16, TN_DIMS, preferred_element_type=F32)
        dc = (dh * mult * ig
              + lax.dot_general(dpa16, wa, NT_DIMS, preferred_element_type=F32)
              + lax.dot_general(dpx16, wx, NT_DIMS, preferred_element_type=F32))

        dcx_ref[0:tt, :] = dc
        drx = (cwv[3:4] * dc + cwv[2:3] * dcx_ref[1:1 + tt, :] + cwv[1:2] * dcx_ref[2:2 + tt, :]
               + cwv[0:1] * dcx_ref[3:3 + tt, :])
        drx_ref[...] = drx.astype(BF16)
        dcx_ref[tt:tt + 8, :] = dc[0:8, :]

        def colsum(v):
            return jnp.sum(v, axis=0, keepdims=True)

        st_ref[...] += _rows8([colsum(dc), colsum(dpa), colsum(dpx), dlam,
                               colsum(dc * x0), colsum(dc * x1), colsum(dc * x2), colsum(dc * x)], w)

        @pl.when(ii == nt - 1)
        def _():
            for blk in range(GROUP_W // RNN_BLOCK_W):
                rows = slice(blk * RNN_BLOCK_W, (blk + 1) * RNN_BLOCK_W)
                gda_ref[blk] = dwa_ref[rows, rows]
                gdx_ref[blk] = dwx_ref[rows, rows]

    def rev(ii):
        return nt - 1 - ii

    def tail(g, ii):
        return (jnp.maximum(rev(ii) * (tt // 8) - 1, 0), g)

    vec = pl.BlockSpec((1, w), lambda g, ii: (0, g))
    mat = pl.BlockSpec((None, w, w), lambda g, ii: (g, 0, 0))
    tile = pl.BlockSpec((tt, w), lambda g, ii: (rev(ii), g))
    diag_shape = (N_GROUPS, GROUP_W // RNN_BLOCK_W, RNN_BLOCK_W, RNN_BLOCK_W)
    diag = pl.BlockSpec((None,) + diag_shape[1:], lambda g, ii: (g, 0, 0, 0))
    return pl.pallas_call(
        body, name=name, grid=(N_GROUPS, nt),
        in_specs=[tile, pl.BlockSpec((tt, w), lambda g, ii: (rev(ii), N_GROUPS + g)), pl.BlockSpec((8, w), tail),
                  tile, pl.BlockSpec((8, w), tail), tile,
                  pl.BlockSpec((4, w), lambda g, ii: (0, g)), vec, mat, vec, mat, vec, vec],
        out_specs=[tile, tile, pl.BlockSpec((8, w), lambda g, ii: (0, g)), diag, diag],
        out_shape=[jax.ShapeDtypeStruct((t, D_RNN), BF16), jax.ShapeDtypeStruct((t, D_RNN), BF16),
                   jax.ShapeDtypeStruct((8, D_RNN), F32),
                   jax.ShapeDtypeStruct(diag_shape, F32), jax.ShapeDtypeStruct(diag_shape, F32)],
        scratch_shapes=[pltpu.VMEM((tt + 8, w), F32), pltpu.VMEM((tt + 8, w), F32), pltpu.VMEM((8, w), F32),
                        pltpu.VMEM((w, w), F32), pltpu.VMEM((w, w), F32)],
        compiler_params=_params("parallel", "arbitrary"))(rx_rg, rx_rg, rx_rg, y, y, dz, cw, cb, wa, ba, wx, bx, lam)


def _half_mask(shape, half):
    lane = lax.broadcasted_iota(jnp.int32, shape, 1)
    return (lane >= HEAD_DIM) if half else (lane < HEAD_DIM)


def _dup_half(t, half):
    sel = jnp.where(_half_mask(t.shape, half), t, 0.0)
    return sel + pltpu.roll(sel, HEAD_DIM, 1)


def _band_geometry(n):
    qi = lax.broadcasted_iota(jnp.int32, (BLOCK, 2 * BLOCK), 0)
    kj = lax.broadcasted_iota(jnp.int32, (BLOCK, 2 * BLOCK), 1)
    dist = BLOCK + qi - kj
    first_key = jnp.where(n > 0, 0, BLOCK)
    valid = (dist >= 0) & (dist < BLOCK) & (kj >= first_key)
    return dist.astype(F32), valid


def _kv_dup(prev_ref, cur_ref, hk):
    tile = hk // 2
    kt = jnp.concatenate([prev_ref[:, tile * LANE:(tile + 1) * LANE], cur_ref[:, tile * LANE:(tile + 1) * LANE]], axis=0)
    return _dup_half(kt.astype(F32), hk % 2).astype(BF16)


def _head_probs(qm, kdup, sink, slope, distf, valid):
    s = lax.dot_general(qm, kdup, NT_DIMS, preferred_element_type=F32) * ATTN_SCALE - slope * distf
    s = jnp.where(valid, s, MASKED)
    m = jnp.maximum(jnp.max(s, axis=1, keepdims=True), sink)
    e = jnp.exp(s - m)
    es = jnp.exp(sink - m)
    inv = 1.0 / (jnp.sum(e, axis=1, keepdims=True) + es)
    return e * inv, es * inv


def _attn_specs(nb, clamp_last):
    def blk(n):
        return jnp.minimum(n, nb - 1) if clamp_last else n

    q_spec = pl.BlockSpec((BLOCK, D_MODEL), lambda n: (blk(n), 0))
    k_prev = pl.BlockSpec((BLOCK, D_KV), lambda n: (jnp.maximum(blk(n) - 1, 0), D_MODEL // D_KV))
    k_cur = pl.BlockSpec((BLOCK, D_KV), lambda n: (blk(n), D_MODEL // D_KV))
    v_prev = pl.BlockSpec((BLOCK, D_KV), lambda n: (jnp.maximum(blk(n) - 1, 0), D_MODEL // D_KV + 1))
    v_cur = pl.BlockSpec((BLOCK, D_KV), lambda n: (blk(n), D_MODEL // D_KV + 1))
    return q_spec, k_prev, k_cur, v_prev, v_cur


def _attn_fwd(sm, qkv, ag_ml, name):
    t = qkv.shape[0]
    nb = t // BLOCK

    def body(sm_ref, q_ref, kp_ref, kc_ref, vp_ref, vc_ref, ag_ref, y_ref, z_ref):
        n = pl.program_id(0)
        distf, valid = _band_geometry(n)
        for hk in range(4):
            kdup = _kv_dup(kp_ref, kc_ref, hk)
            vdup = _kv_dup(vp_ref, vc_ref, hk)
            for tile in (2 * hk, 2 * hk + 1):
                cols = slice(tile * LANE, (tile + 1) * LANE)
                q2 = q_ref[:, cols]
                outs = []
                for half in (0, 1):
                    head = 2 * tile + half
                    qm = jnp.where(_half_mask(q2.shape, half), q2, jnp.zeros_like(q2))
                    probs, _ = _head_probs(qm, kdup, sm_ref[0, head], sm_ref[1, head], distf, valid)
                    outs.append(jnp.dot(probs.astype(BF16), vdup, preferred_element_type=F32))
                yt = jnp.where(_half_mask(outs[0].shape, 0), outs[0], outs[1])
                y_ref[:, cols] = yt
                ag = ag_ref[:, cols]
                z_ref[:, cols] = (yt * ag * _sigmoid(ag)).astype(BF16)

    q_spec, k_prev, k_cur, v_prev, v_cur = _attn_specs(nb, False)
    wide = pl.BlockSpec((BLOCK, D_MODEL), lambda n: (n, 0))
    return pl.pallas_call(
        body, name=name, grid=(nb,),
        in_specs=[pl.BlockSpec(memory_space=pltpu.SMEM), q_spec, k_prev, k_cur, v_prev, v_cur, wide],
        out_specs=[wide, wide],
        out_shape=[jax.ShapeDtypeStruct((t, D_MODEL), F32), jax.ShapeDtypeStruct((t, D_MODEL), BF16)],
        compiler_params=_params("parallel"))(sm, qkv, qkv, qkv, qkv, qkv, ag_ml)


def _attn_bwd(sm, qkv, ag_ml, y, dz, name):
    t = qkv.shape[0]
    nb = t // BLOCK

    def body(sm_ref, q_ref, kp_ref, kc_ref, vp_ref, vc_ref, ag_ref, y_ref, dz_ref,
             dq_ref, dkv_ref, dag_ref, ds_ref, ck_ref, cv_ref):
        n = pl.program_id(0)

        @pl.when(n == 0)
        def _():
            ck_ref[...] = jnp.zeros_like(ck_ref)
            cv_ref[...] = jnp.zeros_like(cv_ref)
            ds_ref[...] = jnp.zeros_like(ds_ref)

        @pl.when(n < nb)
        def _():
            distf, valid = _band_geometry(n)
            lane8 = lax.broadcasted_iota(jnp.int32, (8, LANE), 1)
            row8 = lax.broadcasted_iota(jnp.int32, (8, LANE), 0)
            dsink = jnp.zeros((8, LANE), F32)
            dk_heads, dv_heads = [], []
            for hk in range(4):
                kdup = _kv_dup(kp_ref, kc_ref, hk)
                vdup = _kv_dup(vp_ref, vc_ref, hk)
                dk_acc = jnp.zeros((2 * BLOCK, LANE), F32)
                dv_acc = jnp.zeros((2 * BLOCK, LANE), F32)
                for tile in (2 * hk, 2 * hk + 1):
                    cols = slice(tile * LANE, (tile + 1) * LANE)
                    q2 = q_ref[:, cols]
                    ag = ag_ref[:, cols]
                    sg = _sigmoid(ag)
                    dzt = dz_ref[:, cols]
                    yt = y_ref[:, cols]
                    dag_ref[:, cols] = (dzt * yt * (sg * (1.0 + ag * (1.0 - sg)))).astype(BF16)
                    dyt = dzt * (ag * sg)
                    dq_halves = []
                    for half in (0, 1):
                        head = 2 * tile + half
                        hm = _half_mask(q2.shape, half)
                        qm = jnp.where(hm, q2, jnp.zeros_like(q2))
                        probs, psink = _head_probs(qm, kdup, sm_ref[0, head], sm_ref[1, head], distf, valid)
                        dyh = jnp.where(hm, dyt, 0.0)
                        delta = jnp.sum(dyh * yt, axis=1, keepdims=True)
                        dyh16 = dyh.astype(BF16)
                        dp = lax.dot_general(dyh16, vdup, NT_DIMS, preferred_element_type=F32)
                        ds16 = (probs * (dp - delta) * ATTN_SCALE).astype(BF16)
                        dsink = dsink + jnp.where((row8 == 0) & (lane8 == head), -jnp.sum(psink * delta, axis=0, keepdims=True), 0.0)
                        dq_halves.append(jnp.dot(ds16, kdup, preferred_element_type=F32))
                        dk_acc = dk_acc + lax.dot_general(ds16, qm, TN_DIMS, preferred_element_type=F32)
                        dv_acc = dv_acc + lax.dot_general(probs.astype(BF16), dyh16, TN_DIMS, preferred_element_type=F32)
                    dq_ref[:, cols] = jnp.where(_half_mask(dq_halves[0].shape, 0), dq_halves[0], dq_halves[1]).astype(BF16)
                dk_heads.append(dk_acc + pltpu.roll(dk_acc, HEAD_DIM, 1))
                dv_heads.append(dv_acc + pltpu.roll(dv_acc, HEAD_DIM, 1))
            ds_ref[...] += dsink
            low = _half_mask((2 * BLOCK, LANE), 0)
            for tile in range(2):
                cols = slice(tile * LANE, (tile + 1) * LANE)
                dkt = jnp.where(low, dk_heads[2 * tile], dk_heads[2 * tile + 1])
                dvt = jnp.where(low, dv_heads[2 * tile], dv_heads[2 * tile + 1])
                dkv_ref[:, cols] = (ck_ref[:, cols] + dkt[0:BLOCK, :]).astype(BF16)
                dkv_ref[:, D_KV + tile * LANE:D_KV + (tile + 1) * LANE] = (cv_ref[:, cols] + dvt[0:BLOCK, :]).astype(BF16)
                ck_ref[:, cols] = dkt[BLOCK:2 * BLOCK, :]
                cv_ref[:, cols] = dvt[BLOCK:2 * BLOCK, :]

        @pl.when(n == nb)
        def _():
            dkv_ref[:, 0:D_KV] = ck_ref[...].astype(BF16)
            dkv_ref[:, D_KV:2 * D_KV] = cv_ref[...].astype(BF16)

    q_spec, k_prev, k_cur, v_prev, v_cur = _attn_specs(nb, True)
    wide = pl.BlockSpec((BLOCK, D_MODEL), lambda n: (jnp.minimum(n, nb - 1), 0))
    return pl.pallas_call(
        body, name=name, grid=(nb + 1,),
        in_specs=[pl.BlockSpec(memory_space=pltpu.SMEM), q_spec, k_prev, k_cur, v_prev, v_cur, wide, wide, wide],
        out_specs=[wide, pl.BlockSpec((BLOCK, 2 * D_KV), lambda n: (jnp.maximum(n - 1, 0), 0)), wide,
                   pl.BlockSpec((8, LANE), lambda n: (0, 0))],
        out_shape=[jax.ShapeDtypeStruct((t, D_MODEL), BF16), jax.ShapeDtypeStruct((t, 2 * D_KV), BF16),
                   jax.ShapeDtypeStruct((t, D_MODEL), BF16), jax.ShapeDtypeStruct((8, LANE), F32)],
        scratch_shapes=[pltpu.VMEM((BLOCK, D_KV), F32), pltpu.VMEM((BLOCK, D_KV), F32)],
        compiler_params=_params("arbitrary"))(sm, qkv, qkv, qkv, qkv, qkv, ag_ml, y, dz)


def _local_grads(x, target, p):
    wt = p["wt"]
    h = _prenorm(x, p["pre_g"], "prenorm")
    rx_rg = _mm_nt(h, wt, 0, 2 * D_RNN, F32, "proj_rnn")
    qkv = _mm_nt(h, wt, 2 * D_RNN, D_MODEL + 2 * D_KV, BF16, "proj_qkv")
    ag_ml = _mm_nt(h, wt, 2 * D_RNN + D_MODEL + 2 * D_KV, 3 * D_MODEL, F32, "proj_gates")
    lru = (p["cw"], p["cb"], p["wbd_a"], p["b_a"], p["wbd_x"], p["b_x"], p["lam"])
    y_rnn, z_rnn = _rnn_fwd(rx_rg, *lru, "rnn_fwd")
    y_attn, z_attn = _attn_fwd(p["sm"], qkv, ag_ml, "attn_fwd")
    br_rnn = _mm_nn(z_rnn, p["w_rnn"], F32, "out_rnn")
    br_attn = _mm_nn(z_attn, p["w_attn"], F32, "out_attn")
    merged = _merge(br_rnn, br_attn, ag_ml, p["b_gate"], "merge")
    out = _mm_nn(merged, p["w_out"], F32, "out_proj")
    dy, dout, st_post = _loss_head(x, out, target, p["post_g"], "loss_head")

    dmerged = _mm_nt(dout, p["w_out"], 0, D_MODEL, F32, "d_merged")
    gw_out = _mm_tn(merged, dout, "gw_out")
    dbr_rnn, dbr_attn, d_ml, st_merge = _merge_bwd(dmerged, br_rnn, br_attn, ag_ml, p["b_gate"], "merge_bwd")
    dz_rnn = _mm_nt(dbr_rnn, p["w_rnn"], 0, D_RNN, F32, "d_z_rnn")
    gw_rnn = _mm_tn(z_rnn, dbr_rnn, "gw_rnn")
    dz_attn = _mm_nt(dbr_attn, p["w_attn"], 0, D_MODEL, F32, "d_z_attn")
    gw_attn = _mm_tn(z_attn, dbr_attn, "gw_attn")
    d_rx, d_rg, st_rnn, g_rg_a, g_rg_x = _rnn_bwd(rx_rg, y_rnn, dz_rnn, *lru, "rnn_bwd")
    dq, dkv, d_ag, st_sink = _attn_bwd(p["sm"], qkv, ag_ml, y_attn, dz_attn, "attn_bwd")

    segs = [d_rx, d_rg, dq, dkv, d_ag, d_ml]
    dh = _mm_nn_seg(segs, wt, "d_h")
    gwt = _mm_tn_seg(segs, h, "gw_in")
    grad_x, st_pre = _prenorm_bwd(x, p["pre_g"], dh, dy, "prenorm_bwd")
    return dict(grad_x=grad_x, gwt=gwt, gw_rnn=gw_rnn, gw_attn=gw_attn, gw_out=gw_out,
                st_post=st_post, st_merge=st_merge, st_rnn=st_rnn, st_sink=st_sink, st_pre=st_pre,
                g_rg_a=g_rg_a, g_rg_x=g_rg_x)


def _place():
    x, y, c = lax.axis_index("x"), lax.axis_index("y"), lax.axis_index("c")
    return x, y, c


def _all_gather(shards, name):
    n = len(shards)

    def body(*refs):
        ins, outs = refs[:n], refs[n:2 * n]
        send_sems, recv_sems, local_sems = refs[2 * n:]
        x, y, c = _place()
        me, sibling = (x, y, c), (x, y, 1 - c)
        chips = [(1 - x, y), (x, 1 - y), (1 - x, 1 - y)]

        def slot(a, dev):
            return outs[a].at[4 * dev[0] + 2 * dev[1] + dev[2]]

        def copy(a, k, block, to, src=None):
            return pltpu.make_async_remote_copy(
                src_ref=slot(a, block) if src is None else src, dst_ref=slot(a, block),
                send_sem=send_sems.at[a, k], recv_sem=recv_sems.at[a, k], device_id=to, device_id_type=MESH)

        mine = [pltpu.make_async_copy(ins[a], slot(a, me), local_sems.at[a]) for a in range(n)]
        for cp in mine:
            cp.start()
        sent = []
        for a in range(n):
            sent.append(copy(a, 0, me, sibling, src=ins[a]))
            sent += [copy(a, 1 + j, me, (*chip, c), src=ins[a]) for j, chip in enumerate(chips)]
        for cp in sent:
            cp.start()
        for j, chip in enumerate(chips):
            for a in range(n):
                copy(a, 1 + j, (*chip, c), me).wait_recv()
                cp = copy(a, 4 + j, (*chip, c), sibling)
                cp.start()
                sent.append(cp)
        for a in range(n):
            copy(a, 0, sibling, me).wait_recv()
            for j, chip in enumerate(chips):
                copy(a, 4 + j, (*chip, 1 - c), me).wait_recv()
        for cp in sent:
            cp.wait_send()
        for cp in mine:
            cp.wait()

    return pl.pallas_call(
        body, name=name,
        in_specs=[ANY] * n, out_specs=[ANY] * n,
        out_shape=[jax.ShapeDtypeStruct((N_DEV, *s.shape), s.dtype) for s in shards],
        scratch_shapes=[pltpu.SemaphoreType.DMA((n, 7)), pltpu.SemaphoreType.DMA((n, 7)), pltpu.SemaphoreType.DMA((n,))],
    )(*shards)


def _exchange_sibling(scatter, whole, name):
    ns, nw = len(scatter), len(whole)

    def body(*refs):
        s_in, w_in = refs[:ns], refs[ns:ns + nw]
        s_out, w_out = refs[ns + nw:2 * ns + nw], refs[2 * ns + nw:2 * (ns + nw)]
        send_sems, recv_sems = refs[2 * (ns + nw):]
        x, y, c = _place()
        sibling = (x, y, 1 - c)
        copies = []
        for a in range(ns):
            for chip in range(N_CHIP):
                copies.append(pltpu.make_async_remote_copy(
                    src_ref=s_in[a].at[2 * chip + (1 - c)], dst_ref=s_out[a].at[chip],
                    send_sem=send_sems.at[a * N_CHIP + chip], recv_sem=recv_sems.at[a * N_CHIP + chip],
                    device_id=sibling, device_id_type=MESH))
        for a in range(nw):
            copies.append(pltpu.make_async_remote_copy(
                src_ref=w_in[a], dst_ref=w_out[a],
                send_sem=send_sems.at[ns * N_CHIP + a], recv_sem=recv_sems.at[ns * N_CHIP + a],
                device_id=sibling, device_id_type=MESH))
        for cp in copies:
            cp.start()
        for cp in copies:
            cp.wait()

    n_sem = ns * N_CHIP + nw
    return pl.pallas_call(
        body, name=name,
        in_specs=[ANY] * (ns + nw), out_specs=[ANY] * (ns + nw),
        out_shape=([jax.ShapeDtypeStruct((N_CHIP, *s.shape[1:]), s.dtype) for s in scatter]
                   + [jax.ShapeDtypeStruct(s.shape, s.dtype) for s in whole]),
        scratch_shapes=[pltpu.SemaphoreType.DMA((n_sem,)), pltpu.SemaphoreType.DMA((n_sem,))],
    )(*scatter, *whole)


def _exchange_chips(scatter, whole, name):
    ns, nw = len(scatter), len(whole)
    n = ns + nw

    def body(*refs):
        ins, outs = refs[:n], refs[n:2 * n]
        send_sems, recv_sems, local_sems = refs[2 * n:]
        x, y, c = _place()
        own = 2 * x + y
        chips = [(1 - x, y), (x, 1 - y), (1 - x, 1 - y)]

        def src(a, chip_idx):
            return ins[a].at[chip_idx] if a < ns else ins[a]

        local = [pltpu.make_async_copy(src(a, own), outs[a].at[own], local_sems.at[a]) for a in range(n)]
        for cp in local:
            cp.start()
        sent = []
        for a in range(n):
            for j, chip in enumerate(chips):
                sent.append(pltpu.make_async_remote_copy(
                    src_ref=src(a, 2 * chip[0] + chip[1]), dst_ref=outs[a].at[own],
                    send_sem=send_sems.at[a, j], recv_sem=recv_sems.at[a, own],
                    device_id=(*chip, c), device_id_type=MESH))
        for cp in sent:
            cp.start()
        for a in range(n):
            for chip in chips:
                k = 2 * chip[0] + chip[1]
                pltpu.make_async_remote_copy(
                    src_ref=outs[a].at[k], dst_ref=outs[a].at[k], send_sem=send_sems.at[a, 0],
                    recv_sem=recv_sems.at[a, k], device_id=(*chip, c), device_id_type=MESH).wait_recv()
        for cp in sent:
            cp.wait_send()
        for cp in local:
            cp.wait()

    return pl.pallas_call(
        body, name=name,
        in_specs=[ANY] * n, out_specs=[ANY] * n,
        out_shape=([jax.ShapeDtypeStruct(s.shape, s.dtype) for s in scatter]
                   + [jax.ShapeDtypeStruct((N_CHIP, *s.shape), s.dtype) for s in whole]),
        scratch_shapes=[pltpu.SemaphoreType.DMA((n, 3)), pltpu.SemaphoreType.DMA((n, N_CHIP)),
                        pltpu.SemaphoreType.DMA((n,))],
    )(*scatter, *whole)


def _pair_sum_scatter(part, recv, core, name):
    _, r, cdim = part.shape
    tr = min(r, 416 if r % 416 == 0 else 128)

    def body(core_ref, p_ref, r_ref, o_ref):
        del core_ref
        o_ref[...] = p_ref[...] + r_ref[...]

    blk = (None, tr, cdim)
    return pl.pallas_call(
        body, name=name,
        grid_spec=pltpu.PrefetchScalarGridSpec(
            num_scalar_prefetch=1, grid=(N_CHIP, r // tr),
            in_specs=[pl.BlockSpec(blk, lambda k, i, core_ref: (2 * k + core_ref[0], i, 0)),
                      pl.BlockSpec(blk, lambda k, i, core_ref: (k, i, 0))],
            out_specs=pl.BlockSpec(blk, lambda k, i, core_ref: (k, i, 0))),
        out_shape=jax.ShapeDtypeStruct((N_CHIP, r, cdim), F32),
        compiler_params=_params("parallel", "parallel"))(core, part, recv)


def _pair_sum_whole(a, b, name):
    def body(a_ref, b_ref, o_ref):
        o_ref[...] = a_ref[...] + b_ref[...]

    return pl.pallas_call(body, name=name, out_shape=jax.ShapeDtypeStruct(a.shape, F32))(a, b)


def _adamw(g, w, m, v):
    m = ADAM_B1 * m + (1.0 - ADAM_B1) * g
    v = ADAM_B2 * v + (1.0 - ADAM_B2) * (g * g)
    m_hat = m / (1.0 - ADAM_B1 ** ADAM_STEP)
    v_hat = v / (1.0 - ADAM_B2 ** ADAM_STEP)
    delta = -ADAM_LR * (m_hat / (jnp.sqrt(v_hat) + ADAM_EPS) + ADAM_WD * w)
    return delta, m, v


def _adam_parts(parts, w, m, v, name, tr=None):
    npart, r, c = parts.shape
    tr = r if tr is None else min(tr, r)

    def body(p_ref, w_ref, m_ref, v_ref, g_ref, d_ref, nm_ref, nv_ref):
        g = p_ref[0]
        for k in range(1, npart):
            g = g + p_ref[k]
        g_ref[...] = g
        d_ref[...], nm_ref[...], nv_ref[...] = _adamw(g, w_ref[...], m_ref[...], v_ref[...])

    tile = pl.BlockSpec((tr, c), lambda i: (i, 0))
    return pl.pallas_call(
        body, name=name, grid=(r // tr,),
        in_specs=[pl.BlockSpec((npart, tr, c), lambda i: (0, i, 0)), tile, tile, tile],
        out_specs=[tile] * 4, out_shape=[jax.ShapeDtypeStruct((r, c), F32)] * 4,
        compiler_params=_params("parallel"))(parts, w, m, v)


def _adam_transposed(parts, w, m, v, name, tc=256):
    npart, c, r = parts.shape

    def body(p_ref, w_ref, m_ref, v_ref, g_ref, d_ref, nm_ref, nv_ref):
        gt = p_ref[0]
        for k in range(1, npart):
            gt = gt + p_ref[k]
        g = gt.T
        g_ref[...] = g
        d_ref[...], nm_ref[...], nv_ref[...] = _adamw(g, w_ref[...], m_ref[...], v_ref[...])

    tile = pl.BlockSpec((tc, c), lambda i: (i, 0))
    return pl.pallas_call(
        body, name=name, grid=(r // tc,),
        in_specs=[pl.BlockSpec((npart, c, tc), lambda i: (0, 0, i)), tile, tile, tile],
        out_specs=[tile] * 4, out_shape=[jax.ShapeDtypeStruct((r, c), F32)] * 4,
        compiler_params=_params("parallel"))(parts, w, m, v)


def _transpose_bf16(w, name, tr=256):
    r, c = w.shape

    def body(w_ref, o_ref):
        o_ref[...] = w_ref[...].T.astype(BF16)

    return pl.pallas_call(
        body, name=name, grid=(r // tr,),
        in_specs=[pl.BlockSpec((tr, c), lambda i: (i, 0))],
        out_specs=pl.BlockSpec((c, tr), lambda i: (0, i)),
        out_shape=jax.ShapeDtypeStruct((c, r), BF16),
        compiler_params=_params("parallel"))(w)


def _block_diag(w):
    w4 = w.reshape(N_GROUPS, 4, RNN_BLOCK_W, RNN_BLOCK_W)
    eye = jnp.eye(4, dtype=w.dtype)
    return jnp.einsum("gbij,bc->gbicj", w4, eye).reshape(N_GROUPS, GROUP_W, GROUP_W).astype(BF16)


SMALL_ROWS = 16
ROW_PRE_G, ROW_BGATE, ROW_CONV_B, ROW_B_A, ROW_B_X, ROW_LAM, ROW_POST_G, ROW_LOSS, ROW_SINKS, ROW_CONV_W = 0, 1, 3, 4, 5, 6, 7, 8, 9, 10


def _pack_stats(st_pre, st_merge, st_rnn, st_post, st_sink, name):
    d = D_MODEL

    def body(pre_ref, mg_ref, rnn_ref, post_ref, sink_ref, o_ref):
        rnn = rnn_ref[...]
        sinks = jnp.concatenate([sink_ref[0:1, :], jnp.zeros((1, d - LANE), F32)], axis=1)
        o_ref[0:8, :] = _rows8([pre_ref[0:1, :], mg_ref[0:1, :], mg_ref[1:2, :], rnn[0:1], rnn[1:2], rnn[2:3], rnn[3:4],
                                post_ref[0:1, :]], d)
        o_ref[8:16, :] = _rows8([post_ref[1:2, :], sinks, rnn[4:5], rnn[5:6], rnn[6:7], rnn[7:8]], d)

    return pl.pallas_call(body, name=name, out_shape=jax.ShapeDtypeStruct((SMALL_ROWS, d), F32))(
        st_pre, st_merge, st_rnn, st_post, st_sink)


def _adam_small(parts, w, m, v, name):
    d = D_MODEL
    n_in = len(w)

    def pack(refs):
        pre, bg, cbias, ba, bx, lam, post, sinks = [r[...] for r in refs]
        top = _rows8([pre, bg[:, 0:d], bg[:, d:2 * d], cbias, ba, bx, lam, post], d)
        return jnp.concatenate([top, _rows8([jnp.zeros((1, d), F32), sinks], d)], axis=0)

    def body(*refs):
        p_ref = refs[0]
        w_refs, m_refs, v_refs = (refs[1 + k * n_in:1 + (k + 1) * n_in] for k in range(3))
        outs = refs[1 + 3 * n_in:]
        g = p_ref[0]
        for k in range(1, N_CHIP):
            g = g + p_ref[k]
        res = (g,) + _adamw(g, pack(w_refs), pack(m_refs), pack(v_refs))
        for k in range(4):
            outs[k][...] = res[k]
            outs[4 + k][...] = jnp.concatenate([res[k][ROW_BGATE:ROW_BGATE + 1], res[k][ROW_BGATE + 1:ROW_BGATE + 2]], axis=1)

    return pl.pallas_call(
        body, name=name,
        out_shape=[jax.ShapeDtypeStruct((SMALL_ROWS, d), F32)] * 4 + [jax.ShapeDtypeStruct((1, 2 * d), F32)] * 4,
    )(parts, *w, *m, *v)


def kernel(x, pre_norm_g, w_in, b_gate, conv_w, conv_b, w_rg_a, b_rg_a, w_rg_x, b_rg_x, lru_lambda, attn_sinks, w_rnn_out, w_attn_out, w_out, post_norm_g, loss_target, m_pre_norm_g, m_w_in, m_b_gate, m_conv_w, m_conv_b, m_w_rg_a, m_b_rg_a, m_w_rg_x, m_b_rg_x, m_lru_lambda, m_attn_sinks, m_w_rnn_out, m_w_attn_out, m_w_out, m_post_norm_g, v_pre_norm_g, v_w_in, v_b_gate, v_conv_w, v_conv_b, v_w_rg_a, v_b_rg_a, v_w_rg_x, v_b_rg_x, v_lru_lambda, v_attn_sinks, v_w_rnn_out, v_w_attn_out, v_w_out, v_post_norm_g):
    cx, cy, cc = _place()
    dev = 4 * cx + 2 * cy + cc
    core = jnp.reshape(cc, (1,)).astype(jnp.int32)

    wt_shard = _transpose_bf16(w_in[0], "wt_shard")
    wt_all, w_rnn_all, w_attn_all, w_out_all, cw_all = _all_gather(
        [wt_shard, w_rnn_out[0].astype(BF16), w_attn_out[0].astype(BF16), w_out[0].astype(BF16), conv_w[0]],
        "gather_weights")
    heads = jnp.arange(1, N_Q_HEADS + 1, dtype=F32)
    slopes = jnp.exp2(-ALIBI_MAX_BIAS * heads / N_Q_HEADS)
    b_a = b_rg_a.reshape(1, D_RNN)
    b_x = b_rg_x.reshape(1, D_RNN)
    p = dict(
        wt=wt_all.reshape(D_IN, D_MODEL),
        w_rnn=w_rnn_all.reshape(D_RNN, D_MODEL), w_attn=w_attn_all.reshape(D_MODEL, D_MODEL),
        w_out=w_out_all.reshape(D_MODEL, D_MODEL),
        cw=jnp.transpose(cw_all, (1, 0, 2)).reshape(4, D_RNN),
        pre_g=pre_norm_g, post_g=post_norm_g, b_gate=b_gate, cb=conv_b,
        wbd_a=_block_diag(w_rg_a[0]), b_a=b_a, wbd_x=_block_diag(w_rg_x[0]), b_x=b_x, lam=lru_lambda,
        sm=jnp.pad(attn_sinks, ((0, 1), (0, 0))) + jnp.pad(slopes[None, :], ((1, 0), (0, 0))))

    g = _local_grads(x[0], loss_target[0], p)

    flat = (RNN_BLOCKS * RNN_BLOCK_W, RNN_BLOCK_W)
    small = _pack_stats(g["st_pre"], g["st_merge"], g["st_rnn"], g["st_post"], g["st_sink"], "pack_stats")
    scatter = [g["gwt"].reshape(N_DEV, SHARD_IN, D_MODEL), g["gw_rnn"].reshape(N_DEV, SHARD_OUT, D_MODEL),
               g["gw_attn"].reshape(N_DEV, SHARD_OUT, D_MODEL), g["gw_out"].reshape(N_DEV, SHARD_OUT, D_MODEL)]
    whole = [small, g["g_rg_a"].reshape(flat), g["g_rg_x"].reshape(flat)]
    recv = _exchange_sibling(scatter, whole, "reduce_sibling")
    names = ("w_in", "w_rnn_out", "w_attn_out", "w_out", "small", "w_rg_a", "w_rg_x")
    pair = [_pair_sum_scatter(scatter[a], recv[a], core, "pair_" + names[a]) for a in range(4)]
    pair += [_pair_sum_whole(whole[a], recv[4 + a], "pair_" + names[4 + a]) for a in range(3)]
    parts = _exchange_chips(pair[:4], pair[4:], "reduce_chips")

    out = {}
    out["w_in"] = _adam_transposed(parts[0], w_in[0], m_w_in[0], v_w_in[0], "adam_w_in")
    out["w_rnn_out"] = _adam_parts(parts[1], w_rnn_out[0], m_w_rnn_out[0], v_w_rnn_out[0], "adam_w_rnn_out")
    out["w_attn_out"] = _adam_parts(parts[2], w_attn_out[0], m_w_attn_out[0], v_w_attn_out[0], "adam_w_attn_out")
    out["w_out"] = _adam_parts(parts[3], w_out[0], m_w_out[0], v_w_out[0], "adam_w_out")
    out["w_rg_a"] = _adam_parts(parts[5], w_rg_a.reshape(flat), m_w_rg_a.reshape(flat), v_w_rg_a.reshape(flat), "adam_w_rg_a", tr=256)
    out["w_rg_x"] = _adam_parts(parts[6], w_rg_x.reshape(flat), m_w_rg_x.reshape(flat), v_w_rg_x.reshape(flat), "adam_w_rg_x", tr=256)

    def rows(pre, bg, cbias, ba, bx, lam, post, sinks):
        return (pre, bg, cbias, ba.reshape(1, D_RNN), bx.reshape(1, D_RNN), lam, post,
                jnp.pad(sinks, ((0, 0), (0, D_MODEL - N_Q_HEADS))))

    small_out = _adam_small(
        parts[4],
        rows(pre_norm_g, b_gate, conv_b, b_rg_a, b_rg_x, lru_lambda, post_norm_g, attn_sinks),
        rows(m_pre_norm_g, m_b_gate, m_conv_b, m_b_rg_a, m_b_rg_x, m_lru_lambda, m_post_norm_g, m_attn_sinks),
        rows(v_pre_norm_g, v_b_gate, v_conv_b, v_b_rg_a, v_b_rg_x, v_lru_lambda, v_post_norm_g, v_attn_sinks),
        "adam_small")
    packed, bgate_out = small_out[:4], small_out[4:]
    g_cw = lax.dynamic_slice(packed[0][ROW_CONV_W:ROW_CONV_W + 4], (0, dev * SHARD_OUT), (4, SHARD_OUT))
    out["conv_w"] = _adam_parts(g_cw[None], conv_w[0], m_conv_w[0], v_conv_w[0], "adam_conv_w")

    def unpack(kind, name):
        if name == "b_gate":
            return bgate_out[kind]
        row = dict(pre_norm_g=ROW_PRE_G, conv_b=ROW_CONV_B, b_rg_a=ROW_B_A, b_rg_x=ROW_B_X, lru_lambda=ROW_LAM,
                   post_norm_g=ROW_POST_G, attn_sinks=ROW_SINKS)[name]
        r = packed[kind][row:row + 1]
        if name == "attn_sinks":
            return r[:, 0:N_Q_HEADS]
        if name in ("b_rg_a", "b_rg_x"):
            return r.reshape(1, RNN_BLOCKS, RNN_BLOCK_W)
        return r

    shapes = dict(w_in=(1, D_MODEL, SHARD_IN), w_rnn_out=(1, SHARD_OUT, D_MODEL), w_attn_out=(1, SHARD_OUT, D_MODEL),
                  w_out=(1, SHARD_OUT, D_MODEL), w_rg_a=(1, RNN_BLOCKS, RNN_BLOCK_W, RNN_BLOCK_W),
                  w_rg_x=(1, RNN_BLOCKS, RNN_BLOCK_W, RNN_BLOCK_W), conv_w=(1, 4, SHARD_OUT))
    weights = ["pre_norm_g", "w_in", "b_gate", "conv_w", "conv_b", "w_rg_a", "b_rg_a", "w_rg_x", "b_rg_x",
               "lru_lambda", "attn_sinks", "w_rnn_out", "w_attn_out", "w_out", "post_norm_g"]
    results = []
    for kind in range(4):
        for name in weights:
            if name in out:
                results.append(out[name][kind].reshape(shapes[name]))
            else:
                results.append(unpack(kind, name))
    loss = 0.5 / D_MODEL * jnp.sum(packed[0][ROW_LOSS])
    return (loss, g["grad_x"][None], *results)
```

```python
import functools

import jax
import jax.numpy as jnp
from jax import lax
from jax.experimental import pallas as pl
from jax.experimental.pallas import tpu as pltpu

F32 = jnp.float32
BF16 = jnp.bfloat16

D_MODEL = 1024
D_RNN = 1024
RNN_BLOCKS = 16
RNN_BLOCK_W = 64
LRU_C = 8.0
N_Q_HEADS = 16
HEAD_DIM = 64
D_KV = 256
BLOCK = 128
ALIBI_MAX_BIAS = 8.0
EPS = 1e-6
D_IN = 6656
N_DEV = 8
N_CHIP = 4
SHARD_IN = D_IN // N_DEV
SHARD_OUT = D_MODEL // N_DEV
ATTN_SCALE = HEAD_DIM ** -0.5
MASKED = -1e30

ADAM_LR = 0.001
ADAM_B1 = 0.9
ADAM_B2 = 0.999
ADAM_EPS = 1e-08
ADAM_WD = 0.01
ADAM_STEP = 10

VMEM_LIMIT_BYTES = 52 * 1024 * 1024
LANE = 128
GROUP_W = 256
N_GROUPS = D_RNN // GROUP_W
SEG_CHUNK = 512

NT_DIMS = (((1,), (1,)), ((), ()))
TN_DIMS = (((0,), (0,)), ((), ()))
MESH = pl.DeviceIdType.MESH
ANY = pl.BlockSpec(memory_space=pl.ANY)


def _params(*semantics):
    return pltpu.CompilerParams(dimension_semantics=semantics, vmem_limit_bytes=VMEM_LIMIT_BYTES)


def _sigmoid(x):
    return 1.0 / (1.0 + jnp.exp(-x))


def _log1p(e):
    u = 1.0 + e
    den = jnp.where(u == 1.0, 1.0, u - 1.0)
    return jnp.where(u == 1.0, e, jnp.log(u) * (e / den))


def _softplus(z):
    return jnp.maximum(z, 0.0) + _log1p(jnp.exp(-jnp.abs(z)))


def _neg_expm1(x):
    series = x * (1.0 + x * (0.5 + x * (1.0 / 6.0 + x * (1.0 / 24.0))))
    return jnp.where(x > -0.02, -series, 1.0 - jnp.exp(x))


def _rows8(rows, width):
    idx = lax.broadcasted_iota(jnp.int32, (8, width), 0)
    out = jnp.zeros((8, width), F32)
    for r, v in enumerate(rows):
        out = jnp.where(idx == r, v, out)
    return out


class _Ride:
    def __init__(self, arrays, out_shapes, scratch_shapes, start, finish):
        self.arrays, self.out_shapes, self.scratch_shapes = list(arrays), list(out_shapes), list(scratch_shapes)
        self.start, self.finish = start, finish


class _Hosted:
    def __init__(self, ride, n_in, n_out, n_scratch=0):
        self.ride = ride
        self.sizes = (n_in, len(ride.arrays) if ride else 0, n_out, len(ride.out_shapes) if ride else 0, n_scratch)
        self.arrays = ride.arrays if ride else []
        self.in_specs = [ANY] * len(self.arrays)
        self.out_shapes = ride.out_shapes if ride else []
        self.out_specs = [ANY] * len(self.out_shapes)
        self.scratch_shapes = ride.scratch_shapes if ride else []

    def split(self, refs):
        n_in, r_in, n_out, r_out, n_scr = self.sizes
        cuts = [0, n_in, n_in + r_in, n_in + r_in + n_out, n_in + r_in + n_out + r_out, n_in + r_in + n_out + r_out + n_scr]
        host_in, ride_in, host_out, ride_out, host_scr = (refs[cuts[k]:cuts[k + 1]] for k in range(5))
        ride_scr = refs[cuts[5]:]

        def start(when):
            if self.ride is not None:
                pl.when(when)(lambda: self.ride.start(ride_in, ride_out, ride_scr))

        def finish(when):
            if self.ride is not None:
                pl.when(when)(lambda: self.ride.finish(ride_in, ride_out, ride_scr))

        return tuple(host_in) + tuple(host_out) + tuple(host_scr), start, finish

    def results(self, outs, n_out):
        outs = list(outs) if isinstance(outs, (list, tuple)) else [outs]
        return outs[:n_out], outs[n_out:]


def _run_ride(ride, name):
    n_in, n_out = len(ride.arrays), len(ride.out_shapes)

    def body(*refs):
        ins, outs, sems = refs[:n_in], refs[n_in:n_in + n_out], refs[n_in + n_out:]
        ride.start(ins, outs, sems)
        ride.finish(ins, outs, sems)

    return pl.pallas_call(
        body, name=name, in_specs=[ANY] * n_in, out_specs=[ANY] * n_out,
        out_shape=ride.out_shapes, scratch_shapes=ride.scratch_shapes)(*ride.arrays)


def _mm_nt(a, b, b_row_off, n, out_dtype, name, tm=1024, tn=512, ride=None):
    m, k = a.shape
    tm = min(tm, m)
    off = b_row_off // tn
    ni, nj = m // tm, n // tn
    host = _Hosted(ride, 2, 1)

    def body(*refs):
        (a_ref, b_ref, o_ref), start, finish = host.split(refs)
        i, j = pl.program_id(0), pl.program_id(1)
        start((i == 0) & (j == 0))
        o_ref[...] = lax.dot_general(a_ref[...], b_ref[...], NT_DIMS, preferred_element_type=F32).astype(o_ref.dtype)
        finish((i == ni - 1) & (j == nj - 1))

    outs = pl.pallas_call(
        body, name=name, grid=(ni, nj),
        in_specs=[pl.BlockSpec((tm, k), lambda i, j: (i, 0)), pl.BlockSpec((tn, k), lambda i, j: (j + off, 0))] + host.in_specs,
        out_specs=[pl.BlockSpec((tm, tn), lambda i, j: (i, j))] + host.out_specs,
        out_shape=[jax.ShapeDtypeStruct((m, n), out_dtype)] + host.out_shapes,
        scratch_shapes=host.scratch_shapes,
        compiler_params=_params("arbitrary", "arbitrary") if ride else _params("parallel", "parallel"))(a, b, *host.arrays)
    (res,), landed = host.results(outs, 1)
    return (res, landed) if ride else res


def _mm_nn(a, b, out_dtype, name, tm=1024, tn=512):
    m, k = a.shape
    n = b.shape[1]
    tm = min(tm, m)

    def body(a_ref, b_ref, o_ref):
        o_ref[...] = jnp.dot(a_ref[...], b_ref[...], preferred_element_type=F32).astype(o_ref.dtype)

    return pl.pallas_call(
        body, name=name, grid=(m // tm, n // tn),
        in_specs=[pl.BlockSpec((tm, k), lambda i, j: (i, 0)), pl.BlockSpec((k, tn), lambda i, j: (0, j))],
        out_specs=pl.BlockSpec((tm, tn), lambda i, j: (i, j)),
        out_shape=jax.ShapeDtypeStruct((m, n), out_dtype),
        compiler_params=_params("parallel", "parallel"))(a, b)


def _mm_tn(a, b, name, tm=512, tk=1024):
    ktok, m = a.shape
    n = b.shape[1]
    tk = min(tk, ktok)

    def body(a_ref, b_ref, o_ref):
        @pl.when(pl.program_id(1) == 0)
        def _():
            o_ref[...] = jnp.zeros_like(o_ref)

        o_ref[...] += lax.dot_general(a_ref[...], b_ref[...], TN_DIMS, preferred_element_type=F32)

    return pl.pallas_call(
        body, name=name, grid=(m // tm, ktok // tk),
        in_specs=[pl.BlockSpec((tk, tm), lambda i, kk: (kk, i)), pl.BlockSpec((tk, n), lambda i, kk: (kk, 0))],
        out_specs=pl.BlockSpec((tm, n), lambda i, kk: (i, 0)),
        out_shape=jax.ShapeDtypeStruct((m, n), F32),
        compiler_params=_params("parallel", "arbitrary"))(a, b)


def _segment_chunks(segs):
    bounds = [0]
    for s in segs:
        bounds.append(bounds[-1] + s.shape[1] // SEG_CHUNK)
    return bounds


def _seg_index_map(lo, hi, tok_axis_first):
    if tok_axis_first:
        return lambda i, c: (i, jnp.clip(c - lo, 0, hi - lo - 1))
    return lambda c, kk: (jnp.where((c >= lo) & (c < hi), kk, 0), jnp.clip(c - lo, 0, hi - lo - 1))


def _mm_nn_seg(segs, wt, name, tm=1024, ride=None):
    m = segs[0].shape[0]
    n = wt.shape[1]
    tm = min(tm, m)
    bounds = _segment_chunks(segs)
    n_seg = len(segs)
    ni, nc = m // tm, bounds[-1]
    host = _Hosted(ride, n_seg + 1, 1)

    def body(*refs):
        host_refs, start, finish = host.split(refs)
        a_refs, w_ref, o_ref = host_refs[:n_seg], host_refs[n_seg], host_refs[n_seg + 1]
        i, c = pl.program_id(0), pl.program_id(1)
        start((i == 0) & (c == 0))

        @pl.when(c == 0)
        def _():
            o_ref[...] = jnp.zeros_like(o_ref)

        for s in range(n_seg):
            @pl.when((c >= bounds[s]) & (c < bounds[s + 1]))
            def _(s=s):
                o_ref[...] += jnp.dot(a_refs[s][...], w_ref[...], preferred_element_type=F32)

        finish((i == ni - 1) & (c == nc - 1))

    in_specs = [pl.BlockSpec((tm, SEG_CHUNK), _seg_index_map(bounds[s], bounds[s + 1], True)) for s in range(n_seg)]
    in_specs.append(pl.BlockSpec((SEG_CHUNK, n), lambda i, c: (c, 0)))
    outs = pl.pallas_call(
        body, name=name, grid=(ni, nc),
        in_specs=in_specs + host.in_specs, out_specs=[pl.BlockSpec((tm, n), lambda i, c: (i, 0))] + host.out_specs,
        out_shape=[jax.ShapeDtypeStruct((m, n), F32)] + host.out_shapes,
        scratch_shapes=host.scratch_shapes,
        compiler_params=_params("arbitrary" if ride else "parallel", "arbitrary"))(*segs, wt, *host.arrays)
    (res,), landed = host.results(outs, 1)
    return (res, landed) if ride else res


def _mm_tn_seg(segs, b, name, tk=1024):
    ktok = segs[0].shape[0]
    n = b.shape[1]
    tk = min(tk, ktok)
    bounds = _segment_chunks(segs)
    n_seg = len(segs)

    def body(*refs):
        a_refs, b_ref, o_ref = refs[:n_seg], refs[n_seg], refs[n_seg + 1]
        c = pl.program_id(0)
        kk = pl.program_id(1)

        @pl.when(kk == 0)
        def _():
            o_ref[...] = jnp.zeros_like(o_ref)

        rows = b_ref[pl.ds(pl.multiple_of(kk * tk, tk), tk), :]
        for s in range(n_seg):
            @pl.when((c >= bounds[s]) & (c < bounds[s + 1]))
            def _(s=s):
                o_ref[...] += lax.dot_general(a_refs[s][...], rows, TN_DIMS, preferred_element_type=F32)

    in_specs = [pl.BlockSpec((tk, SEG_CHUNK), _seg_index_map(bounds[s], bounds[s + 1], False)) for s in range(n_seg)]
    in_specs.append(pl.BlockSpec((ktok, n), lambda c, kk: (0, 0)))
    return pl.pallas_call(
        body, name=name, grid=(bounds[-1], ktok // tk),
        in_specs=in_specs, out_specs=pl.BlockSpec((SEG_CHUNK, n), lambda c, kk: (c, 0)),
        out_shape=jax.ShapeDtypeStruct((bounds[-1] * SEG_CHUNK, n), F32),
        compiler_params=_params("parallel", "arbitrary"))(*segs, b)


def _prenorm(x, g, name, tm=512):
    t, d = x.shape
    tm = min(tm, t)

    def body(x_ref, g_ref, h_ref):
        x = x_ref[...]
        r = lax.rsqrt(jnp.mean(x * x, axis=-1, keepdims=True) + EPS)
        h_ref[...] = (x * r * g_ref[...]).astype(BF16)

    return pl.pallas_call(
        body, name=name, grid=(t // tm,),
        in_specs=[pl.BlockSpec((tm, d), lambda i: (i, 0)), pl.BlockSpec((1, d), lambda i: (0, 0))],
        out_specs=pl.BlockSpec((tm, d), lambda i: (i, 0)),
        out_shape=jax.ShapeDtypeStruct((t, d), BF16),
        compiler_params=_params("parallel"))(x, g)


def _prenorm_bwd(x, g, dh, dy, name, tm=512):
    t, d = x.shape
    tm = min(tm, t)

    def body(x_ref, g_ref, dh_ref, dy_ref, gx_ref, st_ref):
        @pl.when(pl.program_id(0) == 0)
        def _():
            st_ref[...] = jnp.zeros_like(st_ref)

        x = x_ref[...]
        r = lax.rsqrt(jnp.mean(x * x, axis=-1, keepdims=True) + EPS)
        xn = x * r
        dh = dh_ref[...]
        dxn = dh * g_ref[...]
        dx = r * (dxn - xn * jnp.mean(dxn * xn, axis=-1, keepdims=True))
        gx_ref[...] = dy_ref[...] + dx
        st_ref[...] += _rows8([jnp.sum(dh * xn, axis=0, keepdims=True)], d)

    tile = pl.BlockSpec((tm, d), lambda i: (i, 0))
    return pl.pallas_call(
        body, name=name, grid=(t // tm,),
        in_specs=[tile, pl.BlockSpec((1, d), lambda i: (0, 0)), tile, tile],
        out_specs=[tile, pl.BlockSpec((8, d), lambda i: (0, 0))],
        out_shape=[jax.ShapeDtypeStruct((t, d), F32), jax.ShapeDtypeStruct((8, d), F32)],
        compiler_params=_params("arbitrary"))(x, g, dh, dy)


def _loss_head(x, out, target, g, name, tm=512):
    t, d = x.shape
    tm = min(tm, t)

    def body(x_ref, o_ref, t_ref, g_ref, dy_ref, do_ref, st_ref):
        @pl.when(pl.program_id(0) == 0)
        def _():
            st_ref[...] = jnp.zeros_like(st_ref)

        o = o_ref[...]
        g = g_ref[...]
        r = lax.rsqrt(jnp.mean(o * o, axis=-1, keepdims=True) + EPS)
        nrm = o * r
        err = x_ref[...] + nrm * g - t_ref[...]
        dy = err * (1.0 / d)
        dy_ref[...] = dy
        dn = dy * g
        do_ref[...] = (r * (dn - nrm * jnp.mean(dn * nrm, axis=-1, keepdims=True))).astype(BF16)
        st_ref[...] += _rows8([jnp.sum(dy * nrm, axis=0, keepdims=True), jnp.sum(err * err, axis=0, keepdims=True)], d)

    tile = pl.BlockSpec((tm, d), lambda i: (i, 0))
    return pl.pallas_call(
        body, name=name, grid=(t // tm,),
        in_specs=[tile, tile, tile, pl.BlockSpec((1, d), lambda i: (0, 0))],
        out_specs=[tile, tile, pl.BlockSpec((8, d), lambda i: (0, 0))],
        out_shape=[jax.ShapeDtypeStruct((t, d), F32), jax.ShapeDtypeStruct((t, d), BF16), jax.ShapeDtypeStruct((8, d), F32)],
        compiler_params=_params("arbitrary"))(x, out, target, g)


def _merge(br_rnn, br_attn, ag_ml, b_gate, name, tm=512):
    t, d = br_rnn.shape
    tm = min(tm, t)

    def body(r_ref, a_ref, lr_ref, la_ref, br_ref, ba_ref, o_ref):
        g_rnn = _sigmoid(lr_ref[...] + br_ref[...])
        g_attn = _sigmoid(la_ref[...] + ba_ref[...])
        o_ref[...] = (g_rnn * r_ref[...] + g_attn * a_ref[...]).astype(BF16)

    tile = pl.BlockSpec((tm, d), lambda i: (i, 0))
    return pl.pallas_call(
        body, name=name, grid=(t // tm,),
        in_specs=[tile, tile, pl.BlockSpec((tm, d), lambda i: (i, 1)), pl.BlockSpec((tm, d), lambda i: (i, 2)),
                  pl.BlockSpec((1, d), lambda i: (0, 0)), pl.BlockSpec((1, d), lambda i: (0, 1))],
        out_specs=tile, out_shape=jax.ShapeDtypeStruct((t, d), BF16),
        compiler_params=_params("parallel"))(br_rnn, br_attn, ag_ml, ag_ml, b_gate, b_gate)


def _merge_bwd(dmerged, br_rnn, br_attn, ag_ml, b_gate, name, tm=512):
    t, d = br_rnn.shape
    tm = min(tm, t)

    def body(dm_ref, r_ref, a_ref, lr_ref, la_ref, br_ref, ba_ref, dr_ref, da_ref, dl_ref, st_ref):
        @pl.when(pl.program_id(0) == 0)
        def _():
            st_ref[...] = jnp.zeros_like(st_ref)

        dm = dm_ref[...]
        g_rnn = _sigmoid(lr_ref[...] + br_ref[...])
        g_attn = _sigmoid(la_ref[...] + ba_ref[...])
        dr_ref[...] = (dm * g_rnn).astype(BF16)
        da_ref[...] = (dm * g_attn).astype(BF16)
        dl_rnn = dm * r_ref[...] * g_rnn * (1.0 - g_rnn)
        dl_attn = dm * a_ref[...] * g_attn * (1.0 - g_attn)
        dl_ref[:, 0:d] = dl_rnn.astype(BF16)
        dl_ref[:, d:2 * d] = dl_attn.astype(BF16)
        st_ref[...] += _rows8([jnp.sum(dl_rnn, axis=0, keepdims=True), jnp.sum(dl_attn, axis=0, keepdims=True)], d)

    tile = pl.BlockSpec((tm, d), lambda i: (i, 0))
    return pl.pallas_call(
        body, name=name, grid=(t // tm,),
        in_specs=[tile, tile, tile, pl.BlockSpec((tm, d), lambda i: (i, 1)), pl.BlockSpec((tm, d), lambda i: (i, 2)),
                  pl.BlockSpec((1, d), lambda i: (0, 0)), pl.BlockSpec((1, d), lambda i: (0, 1))],
        out_specs=[tile, tile, pl.BlockSpec((tm, 2 * d), lambda i: (i, 0)), pl.BlockSpec((8, d), lambda i: (0, 0))],
        out_shape=[jax.ShapeDtypeStruct((t, d), BF16), jax.ShapeDtypeStruct((t, d), BF16),
                   jax.ShapeDtypeStruct((t, 2 * d), BF16), jax.ShapeDtypeStruct((8, d), F32)],
        compiler_params=_params("arbitrary"))(dmerged, br_rnn, br_attn, ag_ml, ag_ml, b_gate, b_gate)


def _lru_gates(c, wa, ba, wx, bx, sp):
    cb = c.astype(BF16)
    r = _sigmoid(jnp.dot(cb, wa, preferred_element_type=F32) + ba)
    ig = _sigmoid(jnp.dot(cb, wx, preferred_element_type=F32) + bx)
    log_a = (-LRU_C) * r * sp
    a = jnp.exp(log_a)
    mult = jnp.sqrt(_neg_expm1(2.0 * log_a))
    return cb, r, ig, a, mult


def _scan_fwd(a, u, tt):
    row = lax.broadcasted_iota(jnp.int32, a.shape, 0)
    d = 1
    while d < tt:
        keep = row >= d
        u = u + a * jnp.where(keep, pltpu.roll(u, d, 0), 0.0)
        if 2 * d < tt:
            a = a * jnp.where(keep, pltpu.roll(a, d, 0), 1.0)
        d *= 2
    return u


def _scan_rev(b, g, tt):
    row = lax.broadcasted_iota(jnp.int32, b.shape, 0)
    d = 1
    while d < tt:
        keep = row < tt - d
        g = g + b * jnp.where(keep, pltpu.roll(g, tt - d, 0), 0.0)
        if 2 * d < tt:
            b = b * jnp.where(keep, pltpu.roll(b, tt - d, 0), 1.0)
        d *= 2
    return g


def _conv_taps(cw, bias, x, ext_ref, tt):
    x2 = ext_ref[7:7 + tt, :]
    x1 = ext_ref[6:6 + tt, :]
    x0 = ext_ref[5:5 + tt, :]
    c = bias + cw[3:4] * x + cw[2:3] * x2 + cw[1:2] * x1 + cw[0:1] * x0
    return c, x2, x1, x0


def _rnn_fwd(rx_rg, cw, cb, wa, ba, wx, bx, lam, name, tt=512):
    t = rx_rg.shape[0]
    tt = min(tt, t)
    w = GROUP_W

    def body(rx_ref, rg_ref, cw_ref, cb_ref, wa_ref, ba_ref, wx_ref, bx_ref, lam_ref, y_ref, z_ref, ext_ref, hc_ref):
        @pl.when(pl.program_id(1) == 0)
        def _():
            ext_ref[0:8, :] = jnp.zeros((8, w), F32)
            hc_ref[...] = jnp.zeros((8, w), F32)

        x = rx_ref[...]
        ext_ref[8:8 + tt, :] = x
        c, _, _, _ = _conv_taps(cw_ref[...], cb_ref[...], x, ext_ref, tt)
        ext_ref[0:8, :] = x[tt - 8:tt, :]
        sp = _softplus(-lam_ref[...])
        _, _, ig, a, mult = _lru_gates(c, wa_ref[...], ba_ref[...], wx_ref[...], bx_ref[...], sp)
        u = mult * (ig * c)
        row = lax.broadcasted_iota(jnp.int32, (tt, w), 0)
        u = u + jnp.where(row == 0, a * hc_ref[7:8, :], 0.0)
        h = _scan_fwd(a, u, tt)
        hc_ref[...] = h[tt - 8:tt, :]
        y_ref[...] = h
        rg = rg_ref[...]
        z_ref[...] = (h * rg * _sigmoid(rg)).astype(BF16)

    vec = pl.BlockSpec((1, w), lambda g, i: (0, g))
    mat = pl.BlockSpec((None, w, w), lambda g, i: (g, 0, 0))
    tile = pl.BlockSpec((tt, w), lambda g, i: (i, g))
    return pl.pallas_call(
        body, name=name, grid=(N_GROUPS, t // tt),
        in_specs=[tile, pl.BlockSpec((tt, w), lambda g, i: (i, N_GROUPS + g)),
                  pl.BlockSpec((4, w), lambda g, i: (0, g)), vec, mat, vec, mat, vec, vec],
        out_specs=[tile, tile],
        out_shape=[jax.ShapeDtypeStruct((t, D_RNN), F32), jax.ShapeDtypeStruct((t, D_RNN), BF16)],
        scratch_shapes=[pltpu.VMEM((tt + 8, w), F32), pltpu.VMEM((8, w), F32)],
        compiler_params=_params("parallel", "arbitrary"))(rx_rg, rx_rg, cw, cb, wa, ba, wx, bx, lam)


def _rnn_bwd(rx_rg, y, dz, cw, cb, wa, ba, wx, bx, lam, name, tt=512):
    t = rx_rg.shape[0]
    tt = min(tt, t)
    nt = t // tt
    w = GROUP_W

    def body(rx_ref, rg_ref, rxt_ref, y_ref, yt_ref, dz_ref, cw_ref, cb_ref, wa_ref, ba_ref, wx_ref, bx_ref, lam_ref,
             drx_ref, drg_ref, st_ref, gda_ref, gdx_ref, ext_ref, dcx_ref, wcar_ref, dwa_ref, dwx_ref):
        ii = pl.program_id(1)

        @pl.when(ii == 0)
        def _():
            wcar_ref[...] = jnp.zeros((8, w), F32)
            dcx_ref[tt:tt + 8, :] = jnp.zeros((8, w), F32)
            st_ref[...] = jnp.zeros_like(st_ref)
            dwa_ref[...] = jnp.zeros_like(dwa_ref)
            dwx_ref[...] = jnp.zeros_like(dwx_ref)

        has_prev = jnp.where(ii == nt - 1, 0.0, 1.0)
        x = rx_ref[...]
        ext_ref[0:8, :] = rxt_ref[...] * has_prev
        ext_ref[8:8 + tt, :] = x
        cwv = cw_ref[...]
        c, x2, x1, x0 = _conv_taps(cwv, cb_ref[...], x, ext_ref, tt)
        lam = lam_ref[...]
        sp = _softplus(-lam)
        wa = wa_ref[...]
        wx = wx_ref[...]
        cb16, r, ig, a, mult = _lru_gates(c, wa, ba_ref[...], wx, bx_ref[...], sp)

        rg = rg_ref[...]
        sg = _sigmoid(rg)
        dz = dz_ref[...]
        yv = y_ref[...]
        drg_ref[...] = (dz * yv * (sg * (1.0 + rg * (1.0 - sg)))).astype(BF16)

        row = lax.broadcasted_iota(jnp.int32, (tt, w), 0)
        g = dz * (rg * sg) + jnp.where(row == tt - 1, wcar_ref[0:1, :], 0.0)
        b = jnp.where(row < tt - 1, pltpu.roll(a, tt - 1, 0), 0.0)
        dh = _scan_rev(b, g, tt)
        wcar_ref[...] = (a * dh)[0:8, :]

        hprev = jnp.where(row >= 1, pltpu.roll(yv, 1, 0), yt_ref[7:8, :] * has_prev)
        dmult = dh * (ig * c)
        dig = dh * mult * c
        dlog_a = dh * hprev * a - dmult * (a * a / mult)
        dpa = dlog_a * ((-LRU_C) * sp) * r * (1.0 - r)
        dpx = dig * ig * (1.0 - ig)
        dsp = jnp.sum(dlog_a * r, axis=0, keepdims=True) * (-LRU_C)
        dlam = dsp * (-_sigmoid(-lam))
        dpa16 = dpa.astype(BF16)
        dpx16 = dpx.astype(BF16)
        dwa_ref[...] += lax.dot_general(cb16, dpa16, TN_DIMS, preferred_element_type=F32)
        dwx_ref[...] += lax.dot_general(cb16, dpx16, TN_DIMS, preferred_element_type=F32)
        dc = (dh * mult * ig
              + lax.dot_general(dpa16, wa, NT_DIMS, preferred_element_type=F32)
              + lax.dot_general(dpx16, wx, NT_DIMS, preferred_element_type=F32))

        dcx_ref[0:tt, :] = dc
        drx = (cwv[3:4] * dc + cwv[2:3] * dcx_ref[1:1 + tt, :] + cwv[1:2] * dcx_ref[2:2 + tt, :]
               + cwv[0:1] * dcx_ref[3:3 + tt, :])
        drx_ref[...] = drx.astype(BF16)
        dcx_ref[tt:tt + 8, :] = dc[0:8, :]

        def colsum(v):
            return jnp.sum(v, axis=0, keepdims=True)

        st_ref[...] += _rows8([colsum(dc), colsum(dpa), colsum(dpx), dlam,
                               colsum(dc * x0), colsum(dc * x1), colsum(dc * x2), colsum(dc * x)], w)

        @pl.when(ii == nt - 1)
        def _():
            for blk in range(GROUP_W // RNN_BLOCK_W):
                rows = slice(blk * RNN_BLOCK_W, (blk + 1) * RNN_BLOCK_W)
                gda_ref[blk] = dwa_ref[rows, rows]
                gdx_ref[blk] = dwx_ref[rows, rows]

    def rev(ii):
        return nt - 1 - ii

    def tail(g, ii):
        return (jnp.maximum(rev(ii) * (tt // 8) - 1, 0), g)

    vec = pl.BlockSpec((1, w), lambda g, ii: (0, g))
    mat = pl.BlockSpec((None, w, w), lambda g, ii: (g, 0, 0))
    tile = pl.BlockSpec((tt, w), lambda g, ii: (rev(ii), g))
    diag_shape = (N_GROUPS, GROUP_W // RNN_BLOCK_W, RNN_BLOCK_W, RNN_BLOCK_W)
    diag = pl.BlockSpec((None,) + diag_shape[1:], lambda g, ii: (g, 0, 0, 0))
    return pl.pallas_call(
        body, name=name, grid=(N_GROUPS, nt),
        in_specs=[tile, pl.BlockSpec((tt, w), lambda g, ii: (rev(ii), N_GROUPS + g)), pl.BlockSpec((8, w), tail),
                  tile, pl.BlockSpec((8, w), tail), tile,
                  pl.BlockSpec((4, w), lambda g, ii: (0, g)), vec, mat, vec, mat, vec, vec],
        out_specs=[tile, tile, pl.BlockSpec((8, w), lambda g, ii: (0, g)), diag, diag],
        out_shape=[jax.ShapeDtypeStruct((t, D_RNN), BF16), jax.ShapeDtypeStruct((t, D_RNN), BF16),
                   jax.ShapeDtypeStruct((8, D_RNN), F32),
                   jax.ShapeDtypeStruct(diag_shape, F32), jax.ShapeDtypeStruct(diag_shape, F32)],
        scratch_shapes=[pltpu.VMEM((tt + 8, w), F32), pltpu.VMEM((tt + 8, w), F32), pltpu.VMEM((8, w), F32),
                        pltpu.VMEM((w, w), F32), pltpu.VMEM((w, w), F32)],
        compiler_params=_params("parallel", "arbitrary"))(rx_rg, rx_rg, rx_rg, y, y, dz, cw, cb, wa, ba, wx, bx, lam)


def _half_mask(shape, half):
    lane = lax.broadcasted_iota(jnp.int32, shape, 1)
    return (lane >= HEAD_DIM) if half else (lane < HEAD_DIM)


def _dup_half(t, half):
    sel = jnp.where(_half_mask(t.shape, half), t, 0.0)
    return sel + pltpu.roll(sel, HEAD_DIM, 1)


def _band_geometry(n):
    qi = lax.broadcasted_iota(jnp.int32, (BLOCK, 2 * BLOCK), 0)
    kj = lax.broadcasted_iota(jnp.int32, (BLOCK, 2 * BLOCK), 1)
    dist = BLOCK + qi - kj
    first_key = jnp.where(n > 0, 0, BLOCK)
    valid = (dist >= 0) & (dist < BLOCK) & (kj >= first_key)
    return dist.astype(F32), valid


def _kv_dup(prev_ref, cur_ref, hk):
    tile = hk // 2
    kt = jnp.concatenate([prev_ref[:, tile * LANE:(tile + 1) * LANE], cur_ref[:, tile * LANE:(tile + 1) * LANE]], axis=0)
    return _dup_half(kt.astype(F32), hk % 2).astype(BF16)


def _head_probs(qm, kdup, sink, slope, distf, valid):
    s = lax.dot_general(qm, kdup, NT_DIMS, preferred_element_type=F32) * ATTN_SCALE - slope * distf
    s = jnp.where(valid, s, MASKED)
    m = jnp.maximum(jnp.max(s, axis=1, keepdims=True), sink)
    e = jnp.exp(s - m)
    es = jnp.exp(sink - m)
    inv = 1.0 / (jnp.sum(e, axis=1, keepdims=True) + es)
    return e * inv, es * inv


def _attn_specs(nb, clamp_last):
    def blk(n):
        return jnp.minimum(n, nb - 1) if clamp_last else n

    q_spec = pl.BlockSpec((BLOCK, D_MODEL), lambda n: (blk(n), 0))
    k_prev = pl.BlockSpec((BLOCK, D_KV), lambda n: (jnp.maximum(blk(n) - 1, 0), D_MODEL // D_KV))
    k_cur = pl.BlockSpec((BLOCK, D_KV), lambda n: (blk(n), D_MODEL // D_KV))
    v_prev = pl.BlockSpec((BLOCK, D_KV), lambda n: (jnp.maximum(blk(n) - 1, 0), D_MODEL // D_KV + 1))
    v_cur = pl.BlockSpec((BLOCK, D_KV), lambda n: (blk(n), D_MODEL // D_KV + 1))
    return q_spec, k_prev, k_cur, v_prev, v_cur


def _attn_fwd(sm, qkv, ag_ml, name):
    t = qkv.shape[0]
    nb = t // BLOCK

    def body(sm_ref, q_ref, kp_ref, kc_ref, vp_ref, vc_ref, ag_ref, y_ref, z_ref):
        n = pl.program_id(0)
        distf, valid = _band_geometry(n)
        for hk in range(4):
            kdup = _kv_dup(kp_ref, kc_ref, hk)
            vdup = _kv_dup(vp_ref, vc_ref, hk)
            for tile in (2 * hk, 2 * hk + 1):
                cols = slice(tile * LANE, (tile + 1) * LANE)
                q2 = q_ref[:, cols]
                outs = []
                for half in (0, 1):
                    head = 2 * tile + half
                    qm = jnp.where(_half_mask(q2.shape, half), q2, jnp.zeros_like(q2))
                    probs, _ = _head_probs(qm, kdup, sm_ref[0, head], sm_ref[1, head], distf, valid)
                    outs.append(jnp.dot(probs.astype(BF16), vdup, preferred_element_type=F32))
                yt = jnp.where(_half_mask(outs[0].shape, 0), outs[0], outs[1])
                y_ref[:, cols] = yt
                ag = ag_ref[:, cols]
                z_ref[:, cols] = (yt * ag * _sigmoid(ag)).astype(BF16)

    q_spec, k_prev, k_cur, v_prev, v_cur = _attn_specs(nb, False)
    wide = pl.BlockSpec((BLOCK, D_MODEL), lambda n: (n, 0))
    return pl.pallas_call(
        body, name=name, grid=(nb,),
        in_specs=[pl.BlockSpec(memory_space=pltpu.SMEM), q_spec, k_prev, k_cur, v_prev, v_cur, wide],
        out_specs=[wide, wide],
        out_shape=[jax.ShapeDtypeStruct((t, D_MODEL), F32), jax.ShapeDtypeStruct((t, D_MODEL), BF16)],
        compiler_params=_params("parallel"))(sm, qkv, qkv, qkv, qkv, qkv, ag_ml)


def _attn_bwd(sm, qkv, ag_ml, y, dz, name, ride=None):
    t = qkv.shape[0]
    nb = t // BLOCK
    host = _Hosted(ride, 9, 4, 2)

    def body(*refs):
        host_refs, start, finish = host.split(refs)
        (sm_ref, q_ref, kp_ref, kc_ref, vp_ref, vc_ref, ag_ref, y_ref, dz_ref,
         dq_ref, dkv_ref, dag_ref, ds_ref, ck_ref, cv_ref) = host_refs
        n = pl.program_id(0)
        start(n == 0)

        @pl.when(n == 0)
        def _():
            ck_ref[...] = jnp.zeros_like(ck_ref)
            cv_ref[...] = jnp.zeros_like(cv_ref)
            ds_ref[...] = jnp.zeros_like(ds_ref)

        @pl.when(n < nb)
        def _():
            distf, valid = _band_geometry(n)
            lane8 = lax.broadcasted_iota(jnp.int32, (8, LANE), 1)
            row8 = lax.broadcasted_iota(jnp.int32, (8, LANE), 0)
            dsink = jnp.zeros((8, LANE), F32)
            dk_heads, dv_heads = [], []
            for hk in range(4):
                kdup = _kv_dup(kp_ref, kc_ref, hk)
                vdup = _kv_dup(vp_ref, vc_ref, hk)
                dk_acc = jnp.zeros((2 * BLOCK, LANE), F32)
                dv_acc = jnp.zeros((2 * BLOCK, LANE), F32)
                for tile in (2 * hk, 2 * hk + 1):
                    cols = slice(tile * LANE, (tile + 1) * LANE)
                    q2 = q_ref[:, cols]
                    ag = ag_ref[:, cols]
                    sg = _sigmoid(ag)
                    dzt = dz_ref[:, cols]
                    yt = y_ref[:, cols]
                    dag_ref[:, cols] = (dzt * yt * (sg * (1.0 + ag * (1.0 - sg)))).astype(BF16)
                    dyt = dzt * (ag * sg)
                    dq_halves = []
                    for half in (0, 1):
                        head = 2 * tile + half
                        hm = _half_mask(q2.shape, half)
                        qm = jnp.where(hm, q2, jnp.zeros_like(q2))
                        probs, psink = _head_probs(qm, kdup, sm_ref[0, head], sm_ref[1, head], distf, valid)
                        dyh = jnp.where(hm, dyt, 0.0)
                        delta = jnp.sum(dyh * yt, axis=1, keepdims=True)
                        dyh16 = dyh.astype(BF16)
                        dp = lax.dot_general(dyh16, vdup, NT_DIMS, preferred_element_type=F32)
                        ds16 = (probs * (dp - delta) * ATTN_SCALE).astype(BF16)
                        dsink = dsink + jnp.where((row8 == 0) & (lane8 == head), -jnp.sum(psink * delta, axis=0, keepdims=True), 0.0)
                        dq_halves.append(jnp.dot(ds16, kdup, preferred_element_type=F32))
                        dk_acc = dk_acc + lax.dot_general(ds16, qm, TN_DIMS, preferred_element_type=F32)
                        dv_acc = dv_acc + lax.dot_general(probs.astype(BF16), dyh16, TN_DIMS, preferred_element_type=F32)
                    dq_ref[:, cols] = jnp.where(_half_mask(dq_halves[0].shape, 0), dq_halves[0], dq_halves[1]).astype(BF16)
                dk_heads.append(dk_acc + pltpu.roll(dk_acc, HEAD_DIM, 1))
                dv_heads.append(dv_acc + pltpu.roll(dv_acc, HEAD_DIM, 1))
            ds_ref[...] += dsink
            low = _half_mask((2 * BLOCK, LANE), 0)
            for tile in range(2):
                cols = slice(tile * LANE, (tile + 1) * LANE)
                dkt = jnp.where(low, dk_heads[2 * tile], dk_heads[2 * tile + 1])
                dvt = jnp.where(low, dv_heads[2 * tile], dv_heads[2 * tile + 1])
                dkv_ref[:, cols] = (ck_ref[:, cols] + dkt[0:BLOCK, :]).astype(BF16)
                dkv_ref[:, D_KV + tile * LANE:D_KV + (tile + 1) * LANE] = (cv_ref[:, cols] + dvt[0:BLOCK, :]).astype(BF16)
                ck_ref[:, cols] = dkt[BLOCK:2 * BLOCK, :]
                cv_ref[:, cols] = dvt[BLOCK:2 * BLOCK, :]

        @pl.when(n == nb)
        def _():
            dkv_ref[:, 0:D_KV] = ck_ref[...].astype(BF16)
            dkv_ref[:, D_KV:2 * D_KV] = cv_ref[...].astype(BF16)

        finish(n == nb)

    q_spec, k_prev, k_cur, v_prev, v_cur = _attn_specs(nb, True)
    wide = pl.BlockSpec((BLOCK, D_MODEL), lambda n: (jnp.minimum(n, nb - 1), 0))
    outs = pl.pallas_call(
        body, name=name, grid=(nb + 1,),
        in_specs=[pl.BlockSpec(memory_space=pltpu.SMEM), q_spec, k_prev, k_cur, v_prev, v_cur, wide, wide, wide] + host.in_specs,
        out_specs=[wide, pl.BlockSpec((BLOCK, 2 * D_KV), lambda n: (jnp.maximum(n - 1, 0), 0)), wide,
                   pl.BlockSpec((8, LANE), lambda n: (0, 0))] + host.out_specs,
        out_shape=[jax.ShapeDtypeStruct((t, D_MODEL), BF16), jax.ShapeDtypeStruct((t, 2 * D_KV), BF16),
                   jax.ShapeDtypeStruct((t, D_MODEL), BF16), jax.ShapeDtypeStruct((8, LANE), F32)] + host.out_shapes,
        scratch_shapes=[pltpu.VMEM((BLOCK, D_KV), F32), pltpu.VMEM((BLOCK, D_KV), F32)] + host.scratch_shapes,
        compiler_params=_params("arbitrary"))(sm, qkv, qkv, qkv, qkv, qkv, ag_ml, y, dz, *host.arrays)
    res, landed = host.results(outs, 4)
    return (*res, landed) if ride else tuple(res)


def _local_grads(x, target, p, late_weights=None, reduce_out=None, reduce_in=None):
    wt = p["wt"]
    h = _prenorm(x, p["pre_g"], "prenorm")
    rx_rg = _mm_nt(h, wt, 0, 2 * D_RNN, F32, "proj_rnn")
    qkv = _mm_nt(h, wt, 2 * D_RNN, D_MODEL + 2 * D_KV, BF16, "proj_qkv")
    gates_off = 2 * D_RNN + D_MODEL + 2 * D_KV
    if late_weights is None:
        ag_ml = _mm_nt(h, wt, gates_off, 3 * D_MODEL, F32, "proj_gates")
    else:
        ag_ml, landed = _mm_nt(h, wt, gates_off, 3 * D_MODEL, F32, "proj_gates", ride=late_weights[0])
        p = {**p, **late_weights[1](landed)}
    lru =(p["cw"], p["cb"], p["wbd_a"], p["b_a"], p["wbd_x"], p["b_x"], p["lam"])
    y_rnn, z_rnn = _rnn_fwd(rx_rg, *lru, "rnn_fwd")
    y_attn, z_attn = _attn_fwd(p["sm"], qkv, ag_ml, "attn_fwd")
    br_rnn = _mm_nn(z_rnn, p["w_rnn"], F32, "out_rnn")
    br_attn = _mm_nn(z_attn, p["w_attn"], F32, "out_attn")
    merged = _merge(br_rnn, br_attn, ag_ml, p["b_gate"], "merge")
    out = _mm_nn(merged, p["w_out"], F32, "out_proj")
    dy, dout, st_post = _loss_head(x, out, target, p["post_g"], "loss_head")

    dmerged = _mm_nt(dout, p["w_out"], 0, D_MODEL, F32, "d_merged")
    gw_out = _mm_tn(merged, dout, "gw_out")
    dbr_rnn, dbr_attn, d_ml, st_merge = _merge_bwd(dmerged, br_rnn, br_attn, ag_ml, p["b_gate"], "merge_bwd")
    dz_rnn = _mm_nt(dbr_rnn, p["w_rnn"], 0, D_RNN, F32, "d_z_rnn")
    gw_rnn = _mm_tn(z_rnn, dbr_rnn, "gw_rnn")
    dz_attn = _mm_nt(dbr_attn, p["w_attn"], 0, D_MODEL, F32, "d_z_attn")
    gw_attn = _mm_tn(z_attn, dbr_attn, "gw_attn")
    d_rx, d_rg, st_rnn, g_rg_a, g_rg_x = _rnn_bwd(rx_rg, y_rnn, dz_rnn, *lru, "rnn_bwd")
    red_out = red_in = None
    if reduce_out is None:
        dq, dkv, d_ag, st_sink = _attn_bwd(p["sm"], qkv, ag_ml, y_attn, dz_attn, "attn_bwd")
    else:
        dq, dkv, d_ag, st_sink, red_out = _attn_bwd(p["sm"], qkv, ag_ml, y_attn, dz_attn, "attn_bwd",
                                                    ride=reduce_out(gw_rnn, gw_attn, gw_out, g_rg_a, g_rg_x))

    segs = [d_rx, d_rg, dq, dkv, d_ag, d_ml]
    gwt = _mm_tn_seg(segs, h, "gw_in")
    if reduce_in is None:
        dh = _mm_nn_seg(segs, wt, "d_h")
    else:
        dh, red_in = _mm_nn_seg(segs, wt, "d_h", ride=reduce_in(gwt))
    grad_x, st_pre = _prenorm_bwd(x, p["pre_g"], dh, dy, "prenorm_bwd")
    return dict(grad_x=grad_x, gwt=gwt, gw_rnn=gw_rnn, gw_attn=gw_attn, gw_out=gw_out,
                st_post=st_post, st_merge=st_merge, st_rnn=st_rnn, st_sink=st_sink, st_pre=st_pre,
                g_rg_a=g_rg_a, g_rg_x=g_rg_x, red_out=red_out, red_in=red_in)


def _place():
    x, y, c = lax.axis_index("x"), lax.axis_index("y"), lax.axis_index("c")
    return x, y, c


def _gather_ride(shards):
    n = len(shards)

    def copies(ins, outs, sems):
        send_sems, recv_sems, local_sems = sems
        x, y, c = _place()
        me, sibling = (x, y, c), (x, y, 1 - c)
        chips = [(1 - x, y), (x, 1 - y), (1 - x, 1 - y)]

        def slot(a, dev):
            return outs[a].at[4 * dev[0] + 2 * dev[1] + dev[2]]

        def copy(a, k, block, to, src=None):
            return pltpu.make_async_remote_copy(
                src_ref=slot(a, block) if src is None else src, dst_ref=slot(a, block),
                send_sem=send_sems.at[a, k], recv_sem=recv_sems.at[a, k], device_id=to, device_id_type=MESH)

        mine = [pltpu.make_async_copy(ins[a], slot(a, me), local_sems.at[a]) for a in range(n)]
        first = []
        for a in range(n):
            first.append(copy(a, 0, me, sibling, src=ins[a]))
            first += [copy(a, 1 + j, me, (*chip, c), src=ins[a]) for j, chip in enumerate(chips)]
        return me, sibling, chips, c, copy, mine, first

    def start(ins, outs, sems):
        *_, mine, first = copies(ins, outs, sems)
        for cp in mine + first:
            cp.start()

    def finish(ins, outs, sems):
        me, sibling, chips, c, copy, mine, first = copies(ins, outs, sems)
        passed = []
        for j, chip in enumerate(chips):
            for a in range(n):
                copy(a, 1 + j, (*chip, c), me).wait_recv()
                cp = copy(a, 4 + j, (*chip, c), sibling)
                cp.start()
                passed.append(cp)
        for a in range(n):
            copy(a, 0, sibling, me).wait_recv()
            for j, chip in enumerate(chips):
                copy(a, 4 + j, (*chip, 1 - c), me).wait_recv()
        for cp in first + passed:
            cp.wait_send()
        for cp in mine:
            cp.wait()

    return _Ride(
        shards, [jax.ShapeDtypeStruct((N_DEV, *s.shape), s.dtype) for s in shards],
        [pltpu.SemaphoreType.DMA((n, 7)), pltpu.SemaphoreType.DMA((n, 7)), pltpu.SemaphoreType.DMA((n,))],
        start, finish)


def _exchange_sibling(scatter, whole, name):
    ns, nw = len(scatter), len(whole)

    def body(*refs):
        s_in, w_in = refs[:ns], refs[ns:ns + nw]
        s_out, w_out = refs[ns + nw:2 * ns + nw], refs[2 * ns + nw:2 * (ns + nw)]
        send_sems, recv_sems = refs[2 * (ns + nw):]
        x, y, c = _place()
        sibling = (x, y, 1 - c)
        copies = []
        for a in range(ns):
            for chip in range(N_CHIP):
                copies.append(pltpu.make_async_remote_copy(
                    src_ref=s_in[a].at[2 * chip + (1 - c)], dst_ref=s_out[a].at[chip],
                    send_sem=send_sems.at[a * N_CHIP + chip], recv_sem=recv_sems.at[a * N_CHIP + chip],
                    device_id=sibling, device_id_type=MESH))
        for a in range(nw):
            copies.append(pltpu.make_async_remote_copy(
                src_ref=w_in[a], dst_ref=w_out[a],
                send_sem=send_sems.at[ns * N_CHIP + a], recv_sem=recv_sems.at[ns * N_CHIP + a],
                device_id=sibling, device_id_type=MESH))
        for cp in copies:
            cp.start()
        for cp in copies:
            cp.wait()

    n_sem = ns * N_CHIP + nw
    return pl.pallas_call(
        body, name=name,
        in_specs=[ANY] * (ns + nw), out_specs=[ANY] * (ns + nw),
        out_shape=([jax.ShapeDtypeStruct((N_CHIP, *s.shape[1:]), s.dtype) for s in scatter]
                   + [jax.ShapeDtypeStruct(s.shape, s.dtype) for s in whole]),
        scratch_shapes=[pltpu.SemaphoreType.DMA((n_sem,)), pltpu.SemaphoreType.DMA((n_sem,))],
    )(*scatter, *whole)


def _chips_ride(scatter, whole):
    ns, nw = len(scatter), len(whole)
    n = ns + nw

    def copies(ins, outs, sems):
        send_sems, recv_sems, local_sems = sems
        x, y, c = _place()
        own = 2 * x + y
        chips = [(1 - x, y), (x, 1 - y), (1 - x, 1 - y)]

        def src(a, chip_idx):
            return ins[a].at[chip_idx] if a < ns else ins[a]

        local = [pltpu.make_async_copy(src(a, own), outs[a].at[own], local_sems.at[a]) for a in range(n)]
        sent = [pltpu.make_async_remote_copy(
            src_ref=src(a, 2 * chip[0] + chip[1]), dst_ref=outs[a].at[own],
            send_sem=send_sems.at[a, j], recv_sem=recv_sems.at[a, own], device_id=(*chip, c), device_id_type=MESH)
            for a in range(n) for j, chip in enumerate(chips)]
        return chips, c, local, sent

    def start(ins, outs, sems):
        _, _, local, sent = copies(ins, outs, sems)
        for cp in local + sent:
            cp.start()

    def finish(ins, outs, sems):
        send_sems, recv_sems, _ = sems
        chips, c, local, sent = copies(ins, outs, sems)
        for a in range(n):
            for chip in chips:
                k = 2 * chip[0] + chip[1]
                pltpu.make_async_remote_copy(
                    src_ref=outs[a].at[k], dst_ref=outs[a].at[k], send_sem=send_sems.at[a, 0],
                    recv_sem=recv_sems.at[a, k], device_id=(*chip, c), device_id_type=MESH).wait_recv()
        for cp in sent:
            cp.wait_send()
        for cp in local:
            cp.wait()

    return _Ride(
        list(scatter) + list(whole),
        [jax.ShapeDtypeStruct(s.shape, s.dtype) for s in scatter]
        + [jax.ShapeDtypeStruct((N_CHIP, *s.shape), s.dtype) for s in whole],
        [pltpu.SemaphoreType.DMA((n, 3)), pltpu.SemaphoreType.DMA((n, N_CHIP)), pltpu.SemaphoreType.DMA((n,))],
        start, finish)


def _pair_sum_scatter(parts, recvs, core, name):
    na = len(parts)
    _, r, cdim = parts[0].shape
    tr = min(r, 416 if r % 416 == 0 else 128)

    def body(core_ref, *refs):
        del core_ref
        for a in range(na):
            refs[2 * na + a][...] = (refs[a][...] + refs[na + a][...]).astype(BF16)

    blk = (None, tr, cdim)
    mine = pl.BlockSpec(blk, lambda k, i, core_ref: (2 * k + core_ref[0], i, 0))
    slot = pl.BlockSpec(blk, lambda k, i, core_ref: (k, i, 0))
    return pl.pallas_call(
        body, name=name,
        grid_spec=pltpu.PrefetchScalarGridSpec(
            num_scalar_prefetch=1, grid=(N_CHIP, r // tr),
            in_specs=[mine] * na + [slot] * na, out_specs=[slot] * na),
        out_shape=[jax.ShapeDtypeStruct((N_CHIP, r, cdim), BF16)] * na,
        compiler_params=_params("parallel", "parallel"))(core, *parts, *recvs)


def _pair_sum_whole(mine, recvs, name):
    na = len(mine)

    def body(*refs):
        for a in range(na):
            refs[2 * na + a][...] = refs[a][...] + refs[na + a][...]

    return pl.pallas_call(body, name=name, out_shape=[jax.ShapeDtypeStruct(m.shape, F32) for m in mine])(*mine, *recvs)


def _allreduce_small(pack, name):
    shape = pack.shape

    def body(x_ref, o_ref, sib_ref, chip_ref, send_sems, recv_sems):
        x, y, c = _place()
        own = 2 * x + y
        chips = [(1 - x, y), (x, 1 - y), (1 - x, 1 - y)]
        to_sibling = pltpu.make_async_remote_copy(
            src_ref=x_ref, dst_ref=sib_ref, send_sem=send_sems.at[0], recv_sem=recv_sems.at[0],
            device_id=(x, y, 1 - c), device_id_type=MESH)
        to_sibling.start()
        to_sibling.wait()
        chip_ref[own] = x_ref[...] + sib_ref[...]
        sent = [pltpu.make_async_remote_copy(
            src_ref=chip_ref.at[own], dst_ref=chip_ref.at[own], send_sem=send_sems.at[1 + j],
            recv_sem=recv_sems.at[1 + own], device_id=(*chip, c), device_id_type=MESH) for j, chip in enumerate(chips)]
        for cp in sent:
            cp.start()
        for chip in chips:
            k = 2 * chip[0] + chip[1]
            pltpu.make_async_remote_copy(
                src_ref=chip_ref.at[k], dst_ref=chip_ref.at[k], send_sem=send_sems.at[1],
                recv_sem=recv_sems.at[1 + k], device_id=(*chip, c), device_id_type=MESH).wait_recv()
        for cp in sent:
            cp.wait_send()
        o_ref[...] = (chip_ref[0] + chip_ref[1]) + (chip_ref[2] + chip_ref[3])

    return pl.pallas_call(
        body, name=name, out_shape=jax.ShapeDtypeStruct(shape, F32),
        in_specs=[pl.BlockSpec(memory_space=pltpu.VMEM)], out_specs=pl.BlockSpec(memory_space=pltpu.VMEM),
        scratch_shapes=[pltpu.VMEM(shape, F32), pltpu.VMEM((N_CHIP, *shape), F32),
                        pltpu.SemaphoreType.DMA((4,)), pltpu.SemaphoreType.DMA((1 + N_CHIP,))],
    )(pack)


def _adamw(g, w, m, v):
    m = ADAM_B1 * m + (1.0 - ADAM_B1) * g
    v = ADAM_B2 * v + (1.0 - ADAM_B2) * (g * g)
    m_hat = m / (1.0 - ADAM_B1 ** ADAM_STEP)
    v_hat = v / (1.0 - ADAM_B2 ** ADAM_STEP)
    delta = -ADAM_LR * (m_hat / (jnp.sqrt(v_hat) + ADAM_EPS) + ADAM_WD * w)
    return delta, m, v


def _adam_parts(parts, w, m, v, name, tr=None):
    npart, r, c = parts.shape
    tr = r if tr is None else min(tr, r)

    def body(p_ref, w_ref, m_ref, v_ref, g_ref, d_ref, nm_ref, nv_ref):
        g = p_ref[0].astype(F32)
        for k in range(1, npart):
            g = g + p_ref[k].astype(F32)
        g_ref[...] = g
        d_ref[...], nm_ref[...], nv_ref[...] = _adamw(g, w_ref[...], m_ref[...], v_ref[...])

    tile = pl.BlockSpec((tr, c), lambda i: (i, 0))
    return pl.pallas_call(
        body, name=name, grid=(r // tr,),
        in_specs=[pl.BlockSpec((npart, tr, c), lambda i: (0, i, 0)), tile, tile, tile],
        out_specs=[tile] * 4, out_shape=[jax.ShapeDtypeStruct((r, c), F32)] * 4,
        compiler_params=_params("parallel"))(parts, w, m, v)


def _adam_transposed(parts, w, m, v, name, tc=256):
    npart, c, r = parts.shape

    def body(p_ref, w_ref, m_ref, v_ref, g_ref, d_ref, nm_ref, nv_ref):
        gt = p_ref[0].astype(F32)
        for k in range(1, npart):
            gt = gt + p_ref[k].astype(F32)
        g = gt.T
        g_ref[...] = g
        d_ref[...], nm_ref[...], nv_ref[...] = _adamw(g, w_ref[...], m_ref[...], v_ref[...])

    tile = pl.BlockSpec((tc, c), lambda i: (i, 0))
    return pl.pallas_call(
        body, name=name, grid=(r // tc,),
        in_specs=[pl.BlockSpec((npart, c, tc), lambda i: (0, 0, i)), tile, tile, tile],
        out_specs=[tile] * 4, out_shape=[jax.ShapeDtypeStruct((r, c), F32)] * 4,
        compiler_params=_params("parallel"))(parts, w, m, v)


def _transpose_bf16(w, name, tr=256):
    r, c = w.shape

    def body(w_ref, o_ref):
        o_ref[...] = w_ref[...].T.astype(BF16)

    return pl.pallas_call(
        body, name=name, grid=(r // tr,),
        in_specs=[pl.BlockSpec((tr, c), lambda i: (i, 0))],
        out_specs=pl.BlockSpec((c, tr), lambda i: (0, i)),
        out_shape=jax.ShapeDtypeStruct((c, r), BF16),
        compiler_params=_params("parallel"))(w)


def _block_diag(w):
    w4 = w.reshape(N_GROUPS, 4, RNN_BLOCK_W, RNN_BLOCK_W)
    eye = jnp.eye(4, dtype=w.dtype)
    return jnp.einsum("gbij,bc->gbicj", w4, eye).reshape(N_GROUPS, GROUP_W, GROUP_W).astype(BF16)


SMALL_ROWS = 16
ROW_PRE_G, ROW_BGATE, ROW_CONV_B, ROW_B_A, ROW_B_X, ROW_LAM, ROW_POST_G, ROW_LOSS, ROW_SINKS, ROW_CONV_W = 0, 1, 3, 4, 5, 6, 7, 8, 9, 10


def _pack_stats(st_pre, st_merge, st_rnn, st_post, st_sink, name):
    d = D_MODEL

    def body(pre_ref, mg_ref, rnn_ref, post_ref, sink_ref, o_ref):
        rnn = rnn_ref[...]
        sinks = jnp.concatenate([sink_ref[0:1, :], jnp.zeros((1, d - LANE), F32)], axis=1)
        o_ref[0:8, :] = _rows8([pre_ref[0:1, :], mg_ref[0:1, :], mg_ref[1:2, :], rnn[0:1], rnn[1:2], rnn[2:3], rnn[3:4],
                                post_ref[0:1, :]], d)
        o_ref[8:16, :] = _rows8([post_ref[1:2, :], sinks, rnn[4:5], rnn[5:6], rnn[6:7], rnn[7:8]], d)

    return pl.pallas_call(body, name=name, out_shape=jax.ShapeDtypeStruct((SMALL_ROWS, d), F32))(
        st_pre, st_merge, st_rnn, st_post, st_sink)


def _adam_small(total, w, m, v, name):
    d = D_MODEL
    n_in = len(w)

    def pack(refs):
        pre, bg, cbias, ba, bx, lam, post, sinks = [r[...] for r in refs]
        top = _rows8([pre, bg[:, 0:d], bg[:, d:2 * d], cbias, ba, bx, lam, post], d)
        return jnp.concatenate([top, _rows8([jnp.zeros((1, d), F32), sinks], d)], axis=0)

    def body(*refs):
        p_ref = refs[0]
        w_refs, m_refs, v_refs = (refs[1 + k * n_in:1 + (k + 1) * n_in] for k in range(3))
        outs = refs[1 + 3 * n_in:]
        g = p_ref[...]
        res = (g,) + _adamw(g, pack(w_refs), pack(m_refs), pack(v_refs))
        for k in range(4):
            outs[k][...] = res[k]
            outs[4 + k][...] = jnp.concatenate([res[k][ROW_BGATE:ROW_BGATE + 1], res[k][ROW_BGATE + 1:ROW_BGATE + 2]], axis=1)

    return pl.pallas_call(
        body, name=name,
        out_shape=[jax.ShapeDtypeStruct((SMALL_ROWS, d), F32)] * 4 + [jax.ShapeDtypeStruct((1, 2 * d), F32)] * 4,
    )(total, *w, *m, *v)


def kernel(x, pre_norm_g, w_in, b_gate, conv_w, conv_b, w_rg_a, b_rg_a, w_rg_x, b_rg_x, lru_lambda, attn_sinks, w_rnn_out, w_attn_out, w_out, post_norm_g, loss_target, m_pre_norm_g, m_w_in, m_b_gate, m_conv_w, m_conv_b, m_w_rg_a, m_b_rg_a, m_w_rg_x, m_b_rg_x, m_lru_lambda, m_attn_sinks, m_w_rnn_out, m_w_attn_out, m_w_out, m_post_norm_g, v_pre_norm_g, v_w_in, v_b_gate, v_conv_w, v_conv_b, v_w_rg_a, v_b_rg_a, v_w_rg_x, v_b_rg_x, v_lru_lambda, v_attn_sinks, v_w_rnn_out, v_w_attn_out, v_w_out, v_post_norm_g):
    cx, cy, cc = _place()
    dev = 4 * cx + 2 * cy + cc
    core = jnp.reshape(cc, (1,)).astype(jnp.int32)

    wt_shard = _transpose_bf16(w_in[0], "wt_shard")
    (wt_all,) = _run_ride(_gather_ride([wt_shard]), "gather_w_in")
    heads = jnp.arange(1, N_Q_HEADS + 1, dtype=F32)
    slopes = jnp.exp2(-ALIBI_MAX_BIAS * heads / N_Q_HEADS)
    b_a = b_rg_a.reshape(1, D_RNN)
    b_x = b_rg_x.reshape(1, D_RNN)
    p = dict(
        wt=wt_all.reshape(D_IN, D_MODEL),
        pre_g=pre_norm_g, post_g=post_norm_g, b_gate=b_gate, cb=conv_b,
        wbd_a=_block_diag(w_rg_a[0]), b_a=b_a, wbd_x=_block_diag(w_rg_x[0]), b_x=b_x, lam=lru_lambda,
        sm=jnp.pad(attn_sinks, ((0, 1), (0, 0))) + jnp.pad(slopes[None, :], ((1, 0), (0, 0))))

    def late_unpack(landed):
        w_rnn_all, w_attn_all, w_out_all, cw_all = landed
        return dict(w_rnn=w_rnn_all.reshape(D_RNN, D_MODEL), w_attn=w_attn_all.reshape(D_MODEL, D_MODEL),
                    w_out=w_out_all.reshape(D_MODEL, D_MODEL), cw=jnp.transpose(cw_all, (1, 0, 2)).reshape(4, D_RNN))

    late_weights = (_gather_ride([w_rnn_out[0].astype(BF16), w_attn_out[0].astype(BF16), w_out[0].astype(BF16), conv_w[0]]),
                    late_unpack)
    flat = (RNN_BLOCKS * RNN_BLOCK_W, RNN_BLOCK_W)

    def reduce_out(gw_rnn, gw_attn, gw_out, g_rg_a, g_rg_x):
        scatter = [gw.reshape(N_DEV, SHARD_OUT, D_MODEL) for gw in (gw_rnn, gw_attn, gw_out)]
        whole = [g_rg_a.reshape(flat), g_rg_x.reshape(flat)]
        recv = _exchange_sibling(scatter, whole, "sibling_out")
        return _chips_ride(_pair_sum_scatter(scatter, recv[:3], core, "pair_out"),
                           _pair_sum_whole(whole, recv[3:], "pair_rg"))

    def reduce_in(gwt):
        scatter = [gwt.reshape(N_DEV, SHARD_IN, D_MODEL)]
        recv = _exchange_sibling(scatter, [], "sibling_in")
        return _chips_ride(_pair_sum_scatter(scatter, recv, core, "pair_in"), [])

    g = _local_grads(x[0], loss_target[0], p, late_weights, reduce_out, reduce_in)
    small = _allreduce_small(
        _pack_stats(g["st_pre"], g["st_merge"], g["st_rnn"], g["st_post"], g["st_sink"], "pack_stats"), "allreduce_small")

    out = {}
    red = g["red_out"]
    out["w_in"] = _adam_transposed(g["red_in"][0], w_in[0], m_w_in[0], v_w_in[0], "adam_w_in")
    out["w_rnn_out"] = _adam_parts(red[0], w_rnn_out[0], m_w_rnn_out[0], v_w_rnn_out[0], "adam_w_rnn_out")
    out["w_attn_out"] = _adam_parts(red[1], w_attn_out[0], m_w_attn_out[0], v_w_attn_out[0], "adam_w_attn_out")
    out["w_out"] = _adam_parts(red[2], w_out[0], m_w_out[0], v_w_out[0], "adam_w_out")
    out["w_rg_a"] = _adam_parts(red[3], w_rg_a.reshape(flat), m_w_rg_a.reshape(flat), v_w_rg_a.reshape(flat), "adam_w_rg_a", tr=256)
    out["w_rg_x"] = _adam_parts(red[4], w_rg_x.reshape(flat), m_w_rg_x.reshape(flat), v_w_rg_x.reshape(flat), "adam_w_rg_x", tr=256)

    def rows(pre, bg, cbias, ba, bx, lam, post, sinks):
        return (pre, bg, cbias, ba.reshape(1, D_RNN), bx.reshape(1, D_RNN), lam, post,
                jnp.pad(sinks, ((0, 0), (0, D_MODEL - N_Q_HEADS))))

    small_out = _adam_small(
        small,
        rows(pre_norm_g, b_gate, conv_b, b_rg_a, b_rg_x, lru_lambda, post_norm_g, attn_sinks),
        rows(m_pre_norm_g, m_b_gate, m_conv_b, m_b_rg_a, m_b_rg_x, m_lru_lambda, m_post_norm_g, m_attn_sinks),
        rows(v_pre_norm_g, v_b_gate, v_conv_b, v_b_rg_a, v_b_rg_x, v_lru_lambda, v_post_norm_g, v_attn_sinks),
        "adam_small")
    packed, bgate_out = small_out[:4], small_out[4:]
    g_cw = lax.dynamic_slice(packed[0][ROW_CONV_W:ROW_CONV_W + 4], (0, dev * SHARD_OUT), (4, SHARD_OUT))
    out["conv_w"] = _adam_parts(g_cw[None], conv_w[0], m_conv_w[0], v_conv_w[0], "adam_conv_w")

    def unpack(kind, name):
        if name == "b_gate":
            return bgate_out[kind]
        row = dict(pre_norm_g=ROW_PRE_G, conv_b=ROW_CONV_B, b_rg_a=ROW_B_A, b_rg_x=ROW_B_X, lru_lambda=ROW_LAM,
                   post_norm_g=ROW_POST_G, attn_sinks=ROW_SINKS)[name]
        r = packed[kind][row:row + 1]
        if name == "attn_sinks":
            return r[:, 0:N_Q_HEADS]
        if name in ("b_rg_a", "b_rg_x"):
            return r.reshape(1, RNN_BLOCKS, RNN_BLOCK_W)
        return r

    shapes = dict(w_in=(1, D_MODEL, SHARD_IN), w_rnn_out=(1, SHARD_OUT, D_MODEL), w_attn_out=(1, SHARD_OUT, D_MODEL),
                  w_out=(1, SHARD_OUT, D_MODEL), w_rg_a=(1, RNN_BLOCKS, RNN_BLOCK_W, RNN_BLOCK_W),
                  w_rg_x=(1, RNN_BLOCKS, RNN_BLOCK_W, RNN_BLOCK_W), conv_w=(1, 4, SHARD_OUT))
    weights = ["pre_norm_g", "w_in", "b_gate", "conv_w", "conv_b", "w_rg_a", "b_rg_a", "w_rg_x", "b_rg_x",
               "lru_lambda", "attn_sinks", "w_rnn_out", "w_attn_out", "w_out", "post_norm_g"]
    results = []
    for kind in range(4):
        for name in weights:
            if name in out:
                results.append(out[name][kind].reshape(shapes[name]))
            else:
                results.append(unpack(kind, name))
    loss = 0.5 / D_MODEL * jnp.sum(packed[0][ROW_LOSS])
    return (loss, g["grad_x"][None], *results)
```

```python
import functools

import jax
import jax.numpy as jnp
from jax import lax
from jax.experimental import pallas as pl
from jax.experimental.pallas import tpu as pltpu

F32 = jnp.float32
BF16 = jnp.bfloat16

D_MODEL = 1024
D_RNN = 1024
RNN_BLOCKS = 16
RNN_BLOCK_W = 64
LRU_C = 8.0
N_Q_HEADS = 16
HEAD_DIM = 64
D_KV = 256
BLOCK = 128
ALIBI_MAX_BIAS = 8.0
EPS = 1e-6
D_IN = 6656
N_DEV = 8
N_CHIP = 4
SHARD_IN = D_IN // N_DEV
SHARD_OUT = D_MODEL // N_DEV
ATTN_SCALE = HEAD_DIM ** -0.5
MASKED = -1e30

ADAM_LR = 0.001
ADAM_B1 = 0.9
ADAM_B2 = 0.999
ADAM_EPS = 1e-08
ADAM_WD = 0.01
ADAM_STEP = 10

VMEM_LIMIT_BYTES = 52 * 1024 * 1024
LANE = 128
GROUP_W = 256
N_GROUPS = D_RNN // GROUP_W
SEG_CHUNK = 512

NT_DIMS = (((1,), (1,)), ((), ()))
TN_DIMS = (((0,), (0,)), ((), ()))
MESH = pl.DeviceIdType.MESH
ANY = pl.BlockSpec(memory_space=pl.ANY)


def _params(*semantics):
    return pltpu.CompilerParams(dimension_semantics=semantics, vmem_limit_bytes=VMEM_LIMIT_BYTES)


def _sigmoid(x):
    return 0.5 * jnp.tanh(0.5 * x) + 0.5


def _log1p(e):
    u = 1.0 + e
    den = jnp.where(u == 1.0, 1.0, u - 1.0)
    return jnp.where(u == 1.0, e, jnp.log(u) * (e / den))


def _softplus(z):
    return jnp.maximum(z, 0.0) + _log1p(jnp.exp(-jnp.abs(z)))


def _rows8(rows, width):
    idx = lax.broadcasted_iota(jnp.int32, (8, width), 0)
    out = jnp.zeros((8, width), F32)
    for r, v in enumerate(rows):
        out = jnp.where(idx == r, v, out)
    return out


class _Ride:
    def __init__(self, arrays, out_shapes, scratch_shapes, start, finish):
        self.arrays, self.out_shapes, self.scratch_shapes = list(arrays), list(out_shapes), list(scratch_shapes)
        self.start, self.finish = start, finish


class _Hosted:
    def __init__(self, ride, n_in, n_out, n_scratch=0):
        self.ride = ride
        self.sizes = (n_in, len(ride.arrays) if ride else 0, n_out, len(ride.out_shapes) if ride else 0, n_scratch)
        self.arrays = ride.arrays if ride else []
        self.in_specs = [ANY] * len(self.arrays)
        self.out_shapes = ride.out_shapes if ride else []
        self.out_specs = [ANY] * len(self.out_shapes)
        self.scratch_shapes = ride.scratch_shapes if ride else []

    def split(self, refs):
        n_in, r_in, n_out, r_out, n_scr = self.sizes
        cuts = [0, n_in, n_in + r_in, n_in + r_in + n_out, n_in + r_in + n_out + r_out, n_in + r_in + n_out + r_out + n_scr]
        host_in, ride_in, host_out, ride_out, host_scr = (refs[cuts[k]:cuts[k + 1]] for k in range(5))
        ride_scr = refs[cuts[5]:]

        def start(when):
            if self.ride is not None:
                pl.when(when)(lambda: self.ride.start(ride_in, ride_out, ride_scr))

        def finish(when):
            if self.ride is not None:
                pl.when(when)(lambda: self.ride.finish(ride_in, ride_out, ride_scr))

        return tuple(host_in) + tuple(host_out) + tuple(host_scr), start, finish

    def results(self, outs, n_out):
        outs = list(outs) if isinstance(outs, (list, tuple)) else [outs]
        return outs[:n_out], outs[n_out:]


def _run_ride(ride, name):
    n_in, n_out = len(ride.arrays), len(ride.out_shapes)

    def body(*refs):
        ins, outs, sems = refs[:n_in], refs[n_in:n_in + n_out], refs[n_in + n_out:]
        ride.start(ins, outs, sems)
        ride.finish(ins, outs, sems)

    return pl.pallas_call(
        body, name=name, in_specs=[ANY] * n_in, out_specs=[ANY] * n_out,
        out_shape=ride.out_shapes, scratch_shapes=ride.scratch_shapes)(*ride.arrays)


def _mm_nt(a, b, b_row_off, n, out_dtype, name, tm=1024, tn=512, ride=None):
    m, k = a.shape
    tm = min(tm, m)
    off = b_row_off // tn
    ni, nj = m // tm, n // tn
    host = _Hosted(ride, 2, 1)

    def body(*refs):
        (a_ref, b_ref, o_ref), start, finish = host.split(refs)
        i, j = pl.program_id(0), pl.program_id(1)
        start((i == 0) & (j == 0))
        o_ref[...] = lax.dot_general(a_ref[...], b_ref[...], NT_DIMS, preferred_element_type=F32).astype(o_ref.dtype)
        finish((i == ni - 1) & (j == nj - 1))

    outs = pl.pallas_call(
        body, name=name, grid=(ni, nj),
        in_specs=[pl.BlockSpec((tm, k), lambda i, j: (i, 0)), pl.BlockSpec((tn, k), lambda i, j: (j + off, 0))] + host.in_specs,
        out_specs=[pl.BlockSpec((tm, tn), lambda i, j: (i, j))] + host.out_specs,
        out_shape=[jax.ShapeDtypeStruct((m, n), out_dtype)] + host.out_shapes,
        scratch_shapes=host.scratch_shapes,
        compiler_params=_params("arbitrary", "arbitrary") if ride else _params("parallel", "parallel"))(a, b, *host.arrays)
    (res,), landed = host.results(outs, 1)
    return (res, landed) if ride else res


def _mm_nn(a, b, out_dtype, name, tm=1024, tn=512):
    m, k = a.shape
    n = b.shape[1]
    tm = min(tm, m)

    def body(a_ref, b_ref, o_ref):
        o_ref[...] = jnp.dot(a_ref[...], b_ref[...], preferred_element_type=F32).astype(o_ref.dtype)

    return pl.pallas_call(
        body, name=name, grid=(m // tm, n // tn),
        in_specs=[pl.BlockSpec((tm, k), lambda i, j: (i, 0)), pl.BlockSpec((k, tn), lambda i, j: (0, j))],
        out_specs=pl.BlockSpec((tm, tn), lambda i, j: (i, j)),
        out_shape=jax.ShapeDtypeStruct((m, n), out_dtype),
        compiler_params=_params("parallel", "parallel"))(a, b)


def _mm_tn(a, b, name, tm=512, tk=1024):
    ktok, m = a.shape
    n = b.shape[1]
    tk = min(tk, ktok)

    def body(a_ref, b_ref, o_ref):
        @pl.when(pl.program_id(1) == 0)
        def _():
            o_ref[...] = jnp.zeros_like(o_ref)

        o_ref[...] += lax.dot_general(a_ref[...], b_ref[...], TN_DIMS, preferred_element_type=F32)

    return pl.pallas_call(
        body, name=name, grid=(m // tm, ktok // tk),
        in_specs=[pl.BlockSpec((tk, tm), lambda i, kk: (kk, i)), pl.BlockSpec((tk, n), lambda i, kk: (kk, 0))],
        out_specs=pl.BlockSpec((tm, n), lambda i, kk: (i, 0)),
        out_shape=jax.ShapeDtypeStruct((m, n), F32),
        compiler_params=_params("parallel", "arbitrary"))(a, b)


def _segment_chunks(segs):
    bounds = [0]
    for s in segs:
        bounds.append(bounds[-1] + s.shape[1] // SEG_CHUNK)
    return bounds


def _seg_index_map(lo, hi, tok_axis_first):
    if tok_axis_first:
        return lambda i, c: (i, jnp.clip(c - lo, 0, hi - lo - 1))
    return lambda c, kk: (jnp.where((c >= lo) & (c < hi), kk, 0), jnp.clip(c - lo, 0, hi - lo - 1))


def _mm_nn_seg(segs, wt, name, tm=1024, ride=None):
    m = segs[0].shape[0]
    n = wt.shape[1]
    tm = min(tm, m)
    bounds = _segment_chunks(segs)
    n_seg = len(segs)
    ni, nc = m // tm, bounds[-1]
    host = _Hosted(ride, n_seg + 1, 1)

    def body(*refs):
        host_refs, start, finish = host.split(refs)
        a_refs, w_ref, o_ref = host_refs[:n_seg], host_refs[n_seg], host_refs[n_seg + 1]
        i, c = pl.program_id(0), pl.program_id(1)
        start((i == 0) & (c == 0))

        @pl.when(c == 0)
        def _():
            o_ref[...] = jnp.zeros_like(o_ref)

        for s in range(n_seg):
            @pl.when((c >= bounds[s]) & (c < bounds[s + 1]))
            def _(s=s):
                o_ref[...] += jnp.dot(a_refs[s][...], w_ref[...], preferred_element_type=F32)

        finish((i == ni - 1) & (c == nc - 1))

    in_specs = [pl.BlockSpec((tm, SEG_CHUNK), _seg_index_map(bounds[s], bounds[s + 1], True)) for s in range(n_seg)]
    in_specs.append(pl.BlockSpec((SEG_CHUNK, n), lambda i, c: (c, 0)))
    outs = pl.pallas_call(
        body, name=name, grid=(ni, nc),
        in_specs=in_specs + host.in_specs, out_specs=[pl.BlockSpec((tm, n), lambda i, c: (i, 0))] + host.out_specs,
        out_shape=[jax.ShapeDtypeStruct((m, n), F32)] + host.out_shapes,
        scratch_shapes=host.scratch_shapes,
        compiler_params=_params("arbitrary" if ride else "parallel", "arbitrary"))(*segs, wt, *host.arrays)
    (res,), landed = host.results(outs, 1)
    return (res, landed) if ride else res


def _mm_tn_seg(segs, b, name, tk=1024):
    ktok = segs[0].shape[0]
    n = b.shape[1]
    tk = min(tk, ktok)
    bounds = _segment_chunks(segs)
    n_seg = len(segs)

    def body(*refs):
        a_refs, b_ref, o_ref = refs[:n_seg], refs[n_seg], refs[n_seg + 1]
        c = pl.program_id(0)
        kk = pl.program_id(1)

        @pl.when(kk == 0)
        def _():
            o_ref[...] = jnp.zeros_like(o_ref)

        rows = b_ref[pl.ds(pl.multiple_of(kk * tk, tk), tk), :]
        for s in range(n_seg):
            @pl.when((c >= bounds[s]) & (c < bounds[s + 1]))
            def _(s=s):
                o_ref[...] += lax.dot_general(a_refs[s][...], rows, TN_DIMS, preferred_element_type=F32)

    in_specs = [pl.BlockSpec((tk, SEG_CHUNK), _seg_index_map(bounds[s], bounds[s + 1], False)) for s in range(n_seg)]
    in_specs.append(pl.BlockSpec((ktok, n), lambda c, kk: (0, 0)))
    return pl.pallas_call(
        body, name=name, grid=(bounds[-1], ktok // tk),
        in_specs=in_specs, out_specs=pl.BlockSpec((SEG_CHUNK, n), lambda c, kk: (c, 0)),
        out_shape=jax.ShapeDtypeStruct((bounds[-1] * SEG_CHUNK, n), F32),
        compiler_params=_params("parallel", "arbitrary"))(*segs, b)


def _prenorm(x, g, name, tm=512):
    t, d = x.shape
    tm = min(tm, t)

    def body(x_ref, g_ref, h_ref):
        x = x_ref[...]
        r = lax.rsqrt(jnp.mean(x * x, axis=-1, keepdims=True) + EPS)
        h_ref[...] = (x * r * g_ref[...]).astype(BF16)

    return pl.pallas_call(
        body, name=name, grid=(t // tm,),
        in_specs=[pl.BlockSpec((tm, d), lambda i: (i, 0)), pl.BlockSpec((1, d), lambda i: (0, 0))],
        out_specs=pl.BlockSpec((tm, d), lambda i: (i, 0)),
        out_shape=jax.ShapeDtypeStruct((t, d), BF16),
        compiler_params=_params("parallel"))(x, g)


def _prenorm_bwd(x, g, dh, dy, name, tm=512):
    t, d = x.shape
    tm = min(tm, t)

    def body(x_ref, g_ref, dh_ref, dy_ref, gx_ref, st_ref):
        @pl.when(pl.program_id(0) == 0)
        def _():
            st_ref[...] = jnp.zeros_like(st_ref)

        x = x_ref[...]
        r = lax.rsqrt(jnp.mean(x * x, axis=-1, keepdims=True) + EPS)
        xn = x * r
        dh = dh_ref[...]
        dxn = dh * g_ref[...]
        dx = r * (dxn - xn * jnp.mean(dxn * xn, axis=-1, keepdims=True))
        gx_ref[...] = dy_ref[...] + dx
        st_ref[...] += _rows8([jnp.sum(dh * xn, axis=0, keepdims=True)], d)

    tile = pl.BlockSpec((tm, d), lambda i: (i, 0))
    return pl.pallas_call(
        body, name=name, grid=(t // tm,),
        in_specs=[tile, pl.BlockSpec((1, d), lambda i: (0, 0)), tile, tile],
        out_specs=[tile, pl.BlockSpec((8, d), lambda i: (0, 0))],
        out_shape=[jax.ShapeDtypeStruct((t, d), F32), jax.ShapeDtypeStruct((8, d), F32)],
        compiler_params=_params("arbitrary"))(x, g, dh, dy)


def _loss_head(x, out, target, g, name, tm=512):
    t, d = x.shape
    tm = min(tm, t)

    def body(x_ref, o_ref, t_ref, g_ref, dy_ref, do_ref, st_ref):
        @pl.when(pl.program_id(0) == 0)
        def _():
            st_ref[...] = jnp.zeros_like(st_ref)

        o = o_ref[...]
        g = g_ref[...]
        r = lax.rsqrt(jnp.mean(o * o, axis=-1, keepdims=True) + EPS)
        nrm = o * r
        err = x_ref[...] + nrm * g - t_ref[...]
        dy = err * (1.0 / d)
        dy_ref[...] = dy
        dn = dy * g
        do_ref[...] = (r * (dn - nrm * jnp.mean(dn * nrm, axis=-1, keepdims=True))).astype(BF16)
        st_ref[...] += _rows8([jnp.sum(dy * nrm, axis=0, keepdims=True), jnp.sum(err * err, axis=0, keepdims=True)], d)

    tile = pl.BlockSpec((tm, d), lambda i: (i, 0))
    return pl.pallas_call(
        body, name=name, grid=(t // tm,),
        in_specs=[tile, tile, tile, pl.BlockSpec((1, d), lambda i: (0, 0))],
        out_specs=[tile, tile, pl.BlockSpec((8, d), lambda i: (0, 0))],
        out_shape=[jax.ShapeDtypeStruct((t, d), F32), jax.ShapeDtypeStruct((t, d), BF16), jax.ShapeDtypeStruct((8, d), F32)],
        compiler_params=_params("arbitrary"))(x, out, target, g)


def _merge(br_rnn, br_attn, ag_ml, b_gate, name, tm=512):
    t, d = br_rnn.shape
    tm = min(tm, t)

    def body(r_ref, a_ref, lr_ref, la_ref, br_ref, ba_ref, o_ref):
        g_rnn = _sigmoid(lr_ref[...] + br_ref[...])
        g_attn = _sigmoid(la_ref[...] + ba_ref[...])
        o_ref[...] = (g_rnn * r_ref[...] + g_attn * a_ref[...]).astype(BF16)

    tile = pl.BlockSpec((tm, d), lambda i: (i, 0))
    return pl.pallas_call(
        body, name=name, grid=(t // tm,),
        in_specs=[tile, tile, pl.BlockSpec((tm, d), lambda i: (i, 1)), pl.BlockSpec((tm, d), lambda i: (i, 2)),
                  pl.BlockSpec((1, d), lambda i: (0, 0)), pl.BlockSpec((1, d), lambda i: (0, 1))],
        out_specs=tile, out_shape=jax.ShapeDtypeStruct((t, d), BF16),
        compiler_params=_params("parallel"))(br_rnn, br_attn, ag_ml, ag_ml, b_gate, b_gate)


def _merge_bwd(dmerged, br_rnn, br_attn, ag_ml, b_gate, name, tm=512):
    t, d = br_rnn.shape
    tm = min(tm, t)

    def body(dm_ref, r_ref, a_ref, lr_ref, la_ref, br_ref, ba_ref, dr_ref, da_ref, dl_ref, st_ref):
        @pl.when(pl.program_id(0) == 0)
        def _():
            st_ref[...] = jnp.zeros_like(st_ref)

        dm = dm_ref[...]
        g_rnn = _sigmoid(lr_ref[...] + br_ref[...])
        g_attn = _sigmoid(la_ref[...] + ba_ref[...])
        dr_ref[...] = (dm * g_rnn).astype(BF16)
        da_ref[...] = (dm * g_attn).astype(BF16)
        dl_rnn = dm * r_ref[...] * g_rnn * (1.0 - g_rnn)
        dl_attn = dm * a_ref[...] * g_attn * (1.0 - g_attn)
        dl_ref[:, 0:d] = dl_rnn.astype(BF16)
        dl_ref[:, d:2 * d] = dl_attn.astype(BF16)
        st_ref[...] += _rows8([jnp.sum(dl_rnn, axis=0, keepdims=True), jnp.sum(dl_attn, axis=0, keepdims=True)], d)

    tile = pl.BlockSpec((tm, d), lambda i: (i, 0))
    return pl.pallas_call(
        body, name=name, grid=(t // tm,),
        in_specs=[tile, tile, tile, pl.BlockSpec((tm, d), lambda i: (i, 1)), pl.BlockSpec((tm, d), lambda i: (i, 2)),
                  pl.BlockSpec((1, d), lambda i: (0, 0)), pl.BlockSpec((1, d), lambda i: (0, 1))],
        out_specs=[tile, tile, pl.BlockSpec((tm, 2 * d), lambda i: (i, 0)), pl.BlockSpec((8, d), lambda i: (0, 0))],
        out_shape=[jax.ShapeDtypeStruct((t, d), BF16), jax.ShapeDtypeStruct((t, d), BF16),
                   jax.ShapeDtypeStruct((t, 2 * d), BF16), jax.ShapeDtypeStruct((8, d), F32)],
        compiler_params=_params("arbitrary"))(dmerged, br_rnn, br_attn, ag_ml, ag_ml, b_gate, b_gate)


def _lru_gates(c, wa, ba, wx, bx, sp):
    cb = c.astype(BF16)
    r = _sigmoid(jnp.dot(cb, wa, preferred_element_type=F32) + ba)
    ig = _sigmoid(jnp.dot(cb, wx, preferred_element_type=F32) + bx)
    log_a = (-LRU_C) * r * sp
    a = jnp.exp(log_a)
    mult = jnp.sqrt(-jnp.tanh(log_a) * (a * a + 1.0))
    return cb, r, ig, a, mult


SUBLANES = 8


def _scan_fwd(a, u, carry, tt):
    w = a.shape[1]
    ng = tt // SUBLANES
    a3 = a.reshape(ng, SUBLANES, w)
    u3 = u.reshape(ng, SUBLANES, w)
    sub = lax.broadcasted_iota(jnp.int32, (ng, SUBLANES, w), 1)
    d = 1
    while d < SUBLANES:
        keep = sub >= d
        u3 = u3 + a3 * jnp.where(keep, pltpu.roll(u3, d, 1), 0.0)
        a3 = a3 * jnp.where(keep, pltpu.roll(a3, d, 1), 1.0)
        d *= 2
    out = []
    for g in range(ng):
        hg = u3[g] + a3[g] * carry
        out.append(hg)
        carry = hg[SUBLANES - 1:SUBLANES, :]
    return jnp.concatenate(out, axis=0)


def _scan_rev(b, g, carry, tt):
    w = b.shape[1]
    ng = tt // SUBLANES
    b3 = b.reshape(ng, SUBLANES, w)
    g3 = g.reshape(ng, SUBLANES, w)
    sub = lax.broadcasted_iota(jnp.int32, (ng, SUBLANES, w), 1)
    d = 1
    while d < SUBLANES:
        keep = sub < SUBLANES - d
        g3 = g3 + b3 * jnp.where(keep, pltpu.roll(g3, SUBLANES - d, 1), 0.0)
        b3 = b3 * jnp.where(keep, pltpu.roll(b3, SUBLANES - d, 1), 1.0)
        d *= 2
    out = [None] * ng
    for k in range(ng - 1, -1, -1):
        hk = g3[k] + b3[k] * carry
        out[k] = hk
        carry = hk[0:1, :]
    return jnp.concatenate(out, axis=0)


def _conv_taps(cw, bias, x, ext_ref, tt):
    x2 = ext_ref[7:7 + tt, :]
    x1 = ext_ref[6:6 + tt, :]
    x0 = ext_ref[5:5 + tt, :]
    c = bias + cw[3:4] * x + cw[2:3] * x2 + cw[1:2] * x1 + cw[0:1] * x0
    return c, x2, x1, x0


def _rnn_fwd(rx_rg, cw, cb, wa, ba, wx, bx, lam, name, tt=512):
    t = rx_rg.shape[0]
    tt = min(tt, t)
    w = GROUP_W

    def body(rx_ref, rg_ref, cw_ref, cb_ref, wa_ref, ba_ref, wx_ref, bx_ref, lam_ref, y_ref, z_ref, ext_ref, hc_ref):
        @pl.when(pl.program_id(1) == 0)
        def _():
            ext_ref[0:8, :] = jnp.zeros((8, w), F32)
            hc_ref[...] = jnp.zeros((8, w), F32)

        x = rx_ref[...]
        ext_ref[8:8 + tt, :] = x
        c, _, _, _ = _conv_taps(cw_ref[...], cb_ref[...], x, ext_ref, tt)
        ext_ref[0:8, :] = x[tt - 8:tt, :]
        sp = _softplus(-lam_ref[...])
        _, _, ig, a, mult = _lru_gates(c, wa_ref[...], ba_ref[...], wx_ref[...], bx_ref[...], sp)
        h = _scan_fwd(a, mult * (ig * c), hc_ref[7:8, :], tt)
        hc_ref[...] = h[tt - 8:tt, :]
        y_ref[...] = h
        rg = rg_ref[...]
        z_ref[...] = (h * rg * _sigmoid(rg)).astype(BF16)

    vec = pl.BlockSpec((1, w), lambda g, i: (0, g))
    mat = pl.BlockSpec((None, w, w), lambda g, i: (g, 0, 0))
    tile = pl.BlockSpec((tt, w), lambda g, i: (i, g))
    return pl.pallas_call(
        body, name=name, grid=(N_GROUPS, t // tt),
        in_specs=[tile, pl.BlockSpec((tt, w), lambda g, i: (i, N_GROUPS + g)),
                  pl.BlockSpec((4, w), lambda g, i: (0, g)), vec, mat, vec, mat, vec, vec],
        out_specs=[tile, tile],
        out_shape=[jax.ShapeDtypeStruct((t, D_RNN), F32), jax.ShapeDtypeStruct((t, D_RNN), BF16)],
        scratch_shapes=[pltpu.VMEM((tt + 8, w), F32), pltpu.VMEM((8, w), F32)],
        compiler_params=_params("parallel", "arbitrary"))(rx_rg, rx_rg, cw, cb, wa, ba, wx, bx, lam)


def _rnn_bwd(rx_rg, y, dz, cw, cb, wa, ba, wx, bx, lam, name, tt=512):
    t = rx_rg.shape[0]
    tt = min(tt, t)
    nt = t // tt
    w = GROUP_W

    def body(rx_ref, rg_ref, rxt_ref, y_ref, yt_ref, dz_ref, cw_ref, cb_ref, wa_ref, ba_ref, wx_ref, bx_ref, lam_ref,
             drx_ref, drg_ref, st_ref, gda_ref, gdx_ref, ext_ref, dcx_ref, wcar_ref, acar_ref, dwa_ref, dwx_ref):
        ii = pl.program_id(1)

        @pl.when(ii == 0)
        def _():
            wcar_ref[...] = jnp.zeros((8, w), F32)
            acar_ref[...] = jnp.zeros((8, w), F32)
            dcx_ref[tt:tt + 8, :] = jnp.zeros((8, w), F32)
            st_ref[...] = jnp.zeros_like(st_ref)
            dwa_ref[...] = jnp.zeros_like(dwa_ref)
            dwx_ref[...] = jnp.zeros_like(dwx_ref)

        has_prev = jnp.where(ii == nt - 1, 0.0, 1.0)
        x = rx_ref[...]
        ext_ref[0:8, :] = rxt_ref[...] * has_prev
        ext_ref[8:8 + tt, :] = x
        cwv = cw_ref[...]
        c, x2, x1, x0 = _conv_taps(cwv, cb_ref[...], x, ext_ref, tt)
        lam = lam_ref[...]
        sp = _softplus(-lam)
        wa = wa_ref[...]
        wx = wx_ref[...]
        cb16, r, ig, a, mult = _lru_gates(c, wa, ba_ref[...], wx, bx_ref[...], sp)

        rg = rg_ref[...]
        sg = _sigmoid(rg)
        dz = dz_ref[...]
        yv = y_ref[...]
        drg_ref[...] = (dz * yv * (sg * (1.0 + rg * (1.0 - sg)))).astype(BF16)

        row = lax.broadcasted_iota(jnp.int32, (tt, w), 0)
        b = jnp.where(row < tt - 1, pltpu.roll(a, tt - 1, 0), acar_ref[0:1, :])
        dh = _scan_rev(b, dz * (rg * sg), wcar_ref[0:1, :], tt)
        wcar_ref[...] = dh[0:8, :]
        acar_ref[...] = a[0:8, :]

        hprev = jnp.where(row >= 1, pltpu.roll(yv, 1, 0), yt_ref[7:8, :] * has_prev)
        dmult = dh * (ig * c)
        dig = dh * mult * c
        dlog_a = dh * hprev * a - dmult * (a * a / mult)
        dpa = dlog_a * ((-LRU_C) * sp) * r * (1.0 - r)
        dpx = dig * ig * (1.0 - ig)
        dsp = jnp.sum(dlog_a * r, axis=0, keepdims=True) * (-LRU_C)
        dlam = dsp * (-_sigmoid(-lam))
        dpa16 = dpa.astype(BF16)
        dpx16 = dpx.astype(BF16)
        dwa_ref[...] += lax.dot_general(cb16, dpa16, TN_DIMS, preferred_element_type=F32)
        dwx_ref[...] += lax.dot_general(cb16, dpx16, TN_DIMS, preferred_element_type=F32)
        dc = (dh * mult * ig
              + lax.dot_general(dpa16, wa, NT_DIMS, preferred_element_type=F32)
              + lax.dot_general(dpx16, wx, NT_DIMS, preferred_element_type=F32))

        dcx_ref[0:tt, :] = dc
        drx = (cwv[3:4] * dc + cwv[2:3] * dcx_ref[1:1 + tt, :] + cwv[1:2] * dcx_ref[2:2 + tt, :]
               + cwv[0:1] * dcx_ref[3:3 + tt, :])
        drx_ref[...] = drx.astype(BF16)
        dcx_ref[tt:tt + 8, :] = dc[0:8, :]

        def colsum(v):
            return jnp.sum(v, axis=0, keepdims=True)

        st_ref[...] += _rows8([colsum(dc), colsum(dpa), colsum(dpx), dlam,
                               colsum(dc * x0), colsum(dc * x1), colsum(dc * x2), colsum(dc * x)], w)

        @pl.when(ii == nt - 1)
        def _():
            for blk in range(GROUP_W // RNN_BLOCK_W):
                rows = slice(blk * RNN_BLOCK_W, (blk + 1) * RNN_BLOCK_W)
                gda_ref[blk] = dwa_ref[rows, rows]
                gdx_ref[blk] = dwx_ref[rows, rows]

    def rev(ii):
        return nt - 1 - ii

    def tail(g, ii):
        return (jnp.maximum(rev(ii) * (tt // 8) - 1, 0), g)

    vec = pl.BlockSpec((1, w), lambda g, ii: (0, g))
    mat = pl.BlockSpec((None, w, w), lambda g, ii: (g, 0, 0))
    tile = pl.BlockSpec((tt, w), lambda g, ii: (rev(ii), g))
    diag_shape = (N_GROUPS, GROUP_W // RNN_BLOCK_W, RNN_BLOCK_W, RNN_BLOCK_W)
    diag = pl.BlockSpec((None,) + diag_shape[1:], lambda g, ii: (g, 0, 0, 0))
    return pl.pallas_call(
        body, name=name, grid=(N_GROUPS, nt),
        in_specs=[tile, pl.BlockSpec((tt, w), lambda g, ii: (rev(ii), N_GROUPS + g)), pl.BlockSpec((8, w), tail),
                  tile, pl.BlockSpec((8, w), tail), tile,
                  pl.BlockSpec((4, w), lambda g, ii: (0, g)), vec, mat, vec, mat, vec, vec],
        out_specs=[tile, tile, pl.BlockSpec((8, w), lambda g, ii: (0, g)), diag, diag],
        out_shape=[jax.ShapeDtypeStruct((t, D_RNN), BF16), jax.ShapeDtypeStruct((t, D_RNN), BF16),
                   jax.ShapeDtypeStruct((8, D_RNN), F32),
                   jax.ShapeDtypeStruct(diag_shape, F32), jax.ShapeDtypeStruct(diag_shape, F32)],
        scratch_shapes=[pltpu.VMEM((tt + 8, w), F32), pltpu.VMEM((tt + 8, w), F32), pltpu.VMEM((8, w), F32),
                        pltpu.VMEM((8, w), F32), pltpu.VMEM((w, w), F32), pltpu.VMEM((w, w), F32)],
        compiler_params=_params("parallel", "arbitrary"))(rx_rg, rx_rg, rx_rg, y, y, dz, cw, cb, wa, ba, wx, bx, lam)


def _half_mask(shape, half):
    lane = lax.broadcasted_iota(jnp.int32, shape, 1)
    return (lane >= HEAD_DIM) if half else (lane < HEAD_DIM)


def _dup_half(t, half):
    sel = jnp.where(_half_mask(t.shape, half), t, 0.0)
    return sel + pltpu.roll(sel, HEAD_DIM, 1)


def _band_geometry(n):
    qi = lax.broadcasted_iota(jnp.int32, (BLOCK, 2 * BLOCK), 0)
    kj = lax.broadcasted_iota(jnp.int32, (BLOCK, 2 * BLOCK), 1)
    dist = BLOCK + qi - kj
    first_key = jnp.where(n > 0, 0, BLOCK)
    valid = (dist >= 0) & (dist < BLOCK) & (kj >= first_key)
    return dist.astype(F32), valid


GROUP = 4
STACK = 1


def _kv_dup(prev_ref, cur_ref, hk, scale=1.0):
    tile = hk // 2
    kt = jnp.concatenate([prev_ref[:, tile * LANE:(tile + 1) * LANE], cur_ref[:, tile * LANE:(tile + 1) * LANE]], axis=0)
    return (_dup_half(kt.astype(F32), hk % 2) * scale).astype(BF16)


def _chains():
    return [list(range(c * STACK, (c + 1) * STACK)) for c in range(GROUP // STACK)]


def _stack(tiles, rs, zero):
    rows = []
    for r in rs:
        t = tiles[r // 2]
        low = _half_mask(t.shape, 0)
        rows.append(jnp.where(low, t, zero) if r % 2 == 0 else jnp.where(low, zero, t))
    return rows[0] if len(rows) == 1 else jnp.concatenate(rows, axis=0)


def _tiles_from_heads(per_head):
    low = _half_mask((BLOCK, LANE), 0)
    return jnp.where(low, per_head[0], per_head[1]), jnp.where(low, per_head[2], per_head[3])


def _fill_bias(bias_ref, sm_ref, n):
    distf, valid = _band_geometry(n)
    for head in range(N_Q_HEADS):
        bias_ref[head] = jnp.where(valid, -sm_ref[1, head] * distf, MASKED)


def _chain_consts(bias_ref, sm_ref, hk, rs):
    heads = [GROUP * hk + r for r in rs]
    if len(heads) == 1:
        return bias_ref[heads[0]], sm_ref[0, heads[0]]
    bias = jnp.concatenate([bias_ref[h] for h in heads], axis=0)
    sink = jnp.concatenate([jnp.full((BLOCK, 1), sm_ref[0, h], F32) for h in heads], axis=0)
    return bias, sink


def _chain_probs(qs, kdup, bias, sink):
    s = lax.dot_general(qs, kdup, NT_DIMS, preferred_element_type=F32) + bias
    m = jnp.maximum(jnp.max(s, axis=1, keepdims=True), sink)
    e = jnp.exp(s - m)
    es = jnp.exp(sink - m)
    inv = 1.0 / (jnp.sum(e, axis=1, keepdims=True) + es)
    return e * inv, es * inv


def _attn_specs(nb, clamp_last):
    def blk(n):
        return jnp.minimum(n, nb - 1) if clamp_last else n

    q_spec = pl.BlockSpec((BLOCK, D_MODEL), lambda n: (blk(n), 0))
    k_prev = pl.BlockSpec((BLOCK, D_KV), lambda n: (jnp.maximum(blk(n) - 1, 0), D_MODEL // D_KV))
    k_cur = pl.BlockSpec((BLOCK, D_KV), lambda n: (blk(n), D_MODEL // D_KV))
    v_prev = pl.BlockSpec((BLOCK, D_KV), lambda n: (jnp.maximum(blk(n) - 1, 0), D_MODEL // D_KV + 1))
    v_cur = pl.BlockSpec((BLOCK, D_KV), lambda n: (blk(n), D_MODEL // D_KV + 1))
    return q_spec, k_prev, k_cur, v_prev, v_cur


def _attn_fwd(sm, qkv, ag_ml, name):
    t = qkv.shape[0]
    nb = t // BLOCK

    def body(sm_ref, q_ref, kp_ref, kc_ref, vp_ref, vc_ref, ag_ref, y_ref, z_ref, bias_ref):
        n = pl.program_id(0)

        @pl.when(n <= 1)
        def _():
            _fill_bias(bias_ref, sm_ref, n)

        for hk in range(N_Q_HEADS // GROUP):
            kdup = _kv_dup(kp_ref, kc_ref, hk)
            vdup = _kv_dup(vp_ref, vc_ref, hk)
            cols = [slice((2 * hk + k) * LANE, (2 * hk + k + 1) * LANE) for k in (0, 1)]
            q_tiles = [q_ref[:, c] * ATTN_SCALE for c in cols]
            per_head = {}
            for rs in _chains():
                qs = _stack(q_tiles, rs, jnp.zeros_like(q_tiles[0]))
                probs, _ = _chain_probs(qs, kdup, *_chain_consts(bias_ref, sm_ref, hk, rs))
                o = jnp.dot(probs.astype(BF16), vdup, preferred_element_type=F32)
                for k, r in enumerate(rs):
                    per_head[r] = o[k * BLOCK:(k + 1) * BLOCK]
            for c, yt in zip(cols, _tiles_from_heads(per_head)):
                y_ref[:, c] = yt
                ag = ag_ref[:, c]
                z_ref[:, c] = (yt * ag * _sigmoid(ag)).astype(BF16)

    q_spec, k_prev, k_cur, v_prev, v_cur = _attn_specs(nb, False)
    wide = pl.BlockSpec((BLOCK, D_MODEL), lambda n: (n, 0))
    return pl.pallas_call(
        body, name=name, grid=(nb,),
        in_specs=[pl.BlockSpec(memory_space=pltpu.SMEM), q_spec, k_prev, k_cur, v_prev, v_cur, wide],
        out_specs=[wide, wide],
        out_shape=[jax.ShapeDtypeStruct((t, D_MODEL), F32), jax.ShapeDtypeStruct((t, D_MODEL), BF16)],
        scratch_shapes=[pltpu.VMEM((N_Q_HEADS, BLOCK, 2 * BLOCK), F32)],
        compiler_params=_params("arbitrary"))(sm, qkv, qkv, qkv, qkv, qkv, ag_ml)


def _attn_bwd(sm, qkv, ag_ml, y, dz, name, ride=None):
    t = qkv.shape[0]
    nb = t // BLOCK
    host = _Hosted(ride, 9, 4, 3)

    def body(*refs):
        host_refs, start, finish = host.split(refs)
        (sm_ref, q_ref, kp_ref, kc_ref, vp_ref, vc_ref, ag_ref, y_ref, dz_ref,
         dq_ref, dkv_ref, dag_ref, ds_ref, ck_ref, cv_ref, bias_ref) = host_refs
        n = pl.program_id(0)
        start(n == 0)

        @pl.when(n == 0)
        def _():
            ck_ref[...] = jnp.zeros_like(ck_ref)
            cv_ref[...] = jnp.zeros_like(cv_ref)
            ds_ref[...] = jnp.zeros_like(ds_ref)

        @pl.when(n <= 1)
        def _():
            _fill_bias(bias_ref, sm_ref, n)

        @pl.when(n < nb)
        def _():
            lane8 = lax.broadcasted_iota(jnp.int32, (8, LANE), 1)
            row8 = lax.broadcasted_iota(jnp.int32, (8, LANE), 0)
            dsink = jnp.zeros((8, LANE), F32)
            dk_heads, dv_heads = [], []
            for hk in range(N_Q_HEADS // GROUP):
                kdup = _kv_dup(kp_ref, kc_ref, hk)
                ks = _kv_dup(kp_ref, kc_ref, hk, ATTN_SCALE)
                vdup = _kv_dup(vp_ref, vc_ref, hk)
                cols = [slice((2 * hk + k) * LANE, (2 * hk + k + 1) * LANE) for k in (0, 1)]
                dy_tiles, y_tiles = [], []
                for c in cols:
                    ag = ag_ref[:, c]
                    sg = _sigmoid(ag)
                    dzt = dz_ref[:, c]
                    yt = y_ref[:, c]
                    dag_ref[:, c] = (dzt * yt * (sg * (1.0 + ag * (1.0 - sg)))).astype(BF16)
                    dy_tiles.append(dzt * (ag * sg))
                    y_tiles.append(yt)
                q_tiles = [q_ref[:, c] * ATTN_SCALE for c in cols]
                dq_head = {}
                dk_acc = jnp.zeros((2 * BLOCK, LANE), F32)
                dv_acc = jnp.zeros((2 * BLOCK, LANE), F32)
                for rs in _chains():
                    qs = _stack(q_tiles, rs, jnp.zeros_like(q_tiles[0]))
                    probs, psink = _chain_probs(qs, kdup, *_chain_consts(bias_ref, sm_ref, hk, rs))
                    do = _stack(dy_tiles, rs, 0.0)
                    ycat = y_tiles[rs[0] // 2] if len(rs) == 1 else jnp.concatenate([y_tiles[r // 2] for r in rs], axis=0)
                    delta = jnp.sum(do * ycat, axis=1, keepdims=True)
                    do16 = do.astype(BF16)
                    dp = lax.dot_general(do16, vdup, NT_DIMS, preferred_element_type=F32)
                    ds16 = (probs * (dp - delta)).astype(BF16)
                    psd = psink * delta
                    dq = jnp.dot(ds16, ks, preferred_element_type=F32)
                    for k, r in enumerate(rs):
                        dq_head[r] = dq[k * BLOCK:(k + 1) * BLOCK]
                        dsink = dsink + jnp.where((row8 == 0) & (lane8 == GROUP * hk + r),
                                                  -jnp.sum(psd[k * BLOCK:(k + 1) * BLOCK], axis=0, keepdims=True), 0.0)
                    dk_acc = dk_acc + lax.dot_general(ds16, qs, TN_DIMS, preferred_element_type=F32)
                    dv_acc = dv_acc + lax.dot_general(probs.astype(BF16), do16, TN_DIMS, preferred_element_type=F32)
                for c, dqt in zip(cols, _tiles_from_heads(dq_head)):
                    dq_ref[:, c] = dqt.astype(BF16)
                dk_heads.append(dk_acc + pltpu.roll(dk_acc, HEAD_DIM, 1))
                dv_heads.append(dv_acc + pltpu.roll(dv_acc, HEAD_DIM, 1))
            ds_ref[...] += dsink
            low = _half_mask((2 * BLOCK, LANE), 0)
            for tile in range(2):
                cols = slice(tile * LANE, (tile + 1) * LANE)
                dkt = jnp.where(low, dk_heads[2 * tile], dk_heads[2 * tile + 1])
                dvt = jnp.where(low, dv_heads[2 * tile], dv_heads[2 * tile + 1])
                dkv_ref[:, cols] = (ck_ref[:, cols] + dkt[0:BLOCK, :]).astype(BF16)
                dkv_ref[:, D_KV + tile * LANE:D_KV + (tile + 1) * LANE] = (cv_ref[:, cols] + dvt[0:BLOCK, :]).astype(BF16)
                ck_ref[:, cols] = dkt[BLOCK:2 * BLOCK, :]
                cv_ref[:, cols] = dvt[BLOCK:2 * BLOCK, :]

        @pl.when(n == nb)
        def _():
            dkv_ref[:, 0:D_KV] = ck_ref[...].astype(BF16)
            dkv_ref[:, D_KV:2 * D_KV] = cv_ref[...].astype(BF16)

        finish(n == nb)

    q_spec, k_prev, k_cur, v_prev, v_cur = _attn_specs(nb, True)
    wide = pl.BlockSpec((BLOCK, D_MODEL), lambda n: (jnp.minimum(n, nb - 1), 0))
    outs = pl.pallas_call(
        body, name=name, grid=(nb + 1,),
        in_specs=[pl.BlockSpec(memory_space=pltpu.SMEM), q_spec, k_prev, k_cur, v_prev, v_cur, wide, wide, wide] + host.in_specs,
        out_specs=[wide, pl.BlockSpec((BLOCK, 2 * D_KV), lambda n: (jnp.maximum(n - 1, 0), 0)), wide,
                   pl.BlockSpec((8, LANE), lambda n: (0, 0))] + host.out_specs,
        out_shape=[jax.ShapeDtypeStruct((t, D_MODEL), BF16), jax.ShapeDtypeStruct((t, 2 * D_KV), BF16),
                   jax.ShapeDtypeStruct((t, D_MODEL), BF16), jax.ShapeDtypeStruct((8, LANE), F32)] + host.out_shapes,
        scratch_shapes=[pltpu.VMEM((BLOCK, D_KV), F32), pltpu.VMEM((BLOCK, D_KV), F32),
                        pltpu.VMEM((N_Q_HEADS, BLOCK, 2 * BLOCK), F32)] + host.scratch_shapes,
        compiler_params=_params("arbitrary"))(sm, qkv, qkv, qkv, qkv, qkv, ag_ml, y, dz, *host.arrays)
    res, landed = host.results(outs, 4)
    return (*res, landed) if ride else tuple(res)


def _local_grads(x, target, p, late_weights=None, reduce_out=None, reduce_in=None):
    wt = p["wt"]
    h = _prenorm(x, p["pre_g"], "prenorm")
    rx_rg = _mm_nt(h, wt, 0, 2 * D_RNN, F32, "proj_rnn")
    qkv = _mm_nt(h, wt, 2 * D_RNN, D_MODEL + 2 * D_KV, BF16, "proj_qkv")
    gates_off = 2 * D_RNN + D_MODEL + 2 * D_KV
    if late_weights is None:
        ag_ml = _mm_nt(h, wt, gates_off, 3 * D_MODEL, F32, "proj_gates")
    else:
        ag_ml, landed = _mm_nt(h, wt, gates_off, 3 * D_MODEL, F32, "proj_gates", ride=late_weights[0])
        p = {**p, **late_weights[1](landed)}
    lru =(p["cw"], p["cb"], p["wbd_a"], p["b_a"], p["wbd_x"], p["b_x"], p["lam"])
    y_rnn, z_rnn = _rnn_fwd(rx_rg, *lru, "rnn_fwd")
    y_attn, z_attn = _attn_fwd(p["sm"], qkv, ag_ml, "attn_fwd")
    br_rnn = _mm_nn(z_rnn, p["w_rnn"], F32, "out_rnn")
    br_attn = _mm_nn(z_attn, p["w_attn"], F32, "out_attn")
    merged = _merge(br_rnn, br_attn, ag_ml, p["b_gate"], "merge")
    out = _mm_nn(merged, p["w_out"], F32, "out_proj")
    dy, dout, st_post = _loss_head(x, out, target, p["post_g"], "loss_head")

    dmerged = _mm_nt(dout, p["w_out"], 0, D_MODEL, F32, "d_merged")
    gw_out = _mm_tn(merged, dout, "gw_out")
    dbr_rnn, dbr_attn, d_ml, st_merge = _merge_bwd(dmerged, br_rnn, br_attn, ag_ml, p["b_gate"], "merge_bwd")
    dz_rnn = _mm_nt(dbr_rnn, p["w_rnn"], 0, D_RNN, F32, "d_z_rnn")
    gw_rnn = _mm_tn(z_rnn, dbr_rnn, "gw_rnn")
    dz_attn = _mm_nt(dbr_attn, p["w_attn"], 0, D_MODEL, F32, "d_z_attn")
    gw_attn = _mm_tn(z_attn, dbr_attn, "gw_attn")
    d_rx, d_rg, st_rnn, g_rg_a, g_rg_x = _rnn_bwd(rx_rg, y_rnn, dz_rnn, *lru, "rnn_bwd")
    red_out = red_in = None
    if reduce_out is None:
        dq, dkv, d_ag, st_sink = _attn_bwd(p["sm"], qkv, ag_ml, y_attn, dz_attn, "attn_bwd")
    else:
        dq, dkv, d_ag, st_sink, red_out = _attn_bwd(p["sm"], qkv, ag_ml, y_attn, dz_attn, "attn_bwd",
                                                    ride=reduce_out(gw_rnn, gw_attn, gw_out, g_rg_a, g_rg_x))

    segs = [d_rx, d_rg, dq, dkv, d_ag, d_ml]
    gwt = _mm_tn_seg(segs, h, "gw_in")
    if reduce_in is None:
        dh = _mm_nn_seg(segs, wt, "d_h")
    else:
        dh, red_in = _mm_nn_seg(segs, wt, "d_h", ride=reduce_in(gwt))
    grad_x, st_pre = _prenorm_bwd(x, p["pre_g"], dh, dy, "prenorm_bwd")
    return dict(grad_x=grad_x, gwt=gwt, gw_rnn=gw_rnn, gw_attn=gw_attn, gw_out=gw_out,
                st_post=st_post, st_merge=st_merge, st_rnn=st_rnn, st_sink=st_sink, st_pre=st_pre,
                g_rg_a=g_rg_a, g_rg_x=g_rg_x, red_out=red_out, red_in=red_in)


def _place():
    x, y, c = lax.axis_index("x"), lax.axis_index("y"), lax.axis_index("c")
    return x, y, c


def _gather_ride(shards):
    n = len(shards)

    def copies(ins, outs, sems):
        send_sems, recv_sems, local_sems = sems
        x, y, c = _place()
        me, sibling = (x, y, c), (x, y, 1 - c)
        chips = [(1 - x, y), (x, 1 - y), (1 - x, 1 - y)]

        def slot(a, dev):
            return outs[a].at[4 * dev[0] + 2 * dev[1] + dev[2]]

        def copy(a, k, block, to, src=None):
            return pltpu.make_async_remote_copy(
                src_ref=slot(a, block) if src is None else src, dst_ref=slot(a, block),
                send_sem=send_sems.at[a, k], recv_sem=recv_sems.at[a, k], device_id=to, device_id_type=MESH)

        mine = [pltpu.make_async_copy(ins[a], slot(a, me), local_sems.at[a]) for a in range(n)]
        first = []
        for a in range(n):
            first.append(copy(a, 0, me, sibling, src=ins[a]))
            first += [copy(a, 1 + j, me, (*chip, c), src=ins[a]) for j, chip in enumerate(chips)]
        return me, sibling, chips, c, copy, mine, first

    def start(ins, outs, sems):
        *_, mine, first = copies(ins, outs, sems)
        for cp in mine + first:
            cp.start()

    def finish(ins, outs, sems):
        me, sibling, chips, c, copy, mine, first = copies(ins, outs, sems)
        passed = []
        for j, chip in enumerate(chips):
            for a in range(n):
                copy(a, 1 + j, (*chip, c), me).wait_recv()
                cp = copy(a, 4 + j, (*chip, c), sibling)
                cp.start()
                passed.append(cp)
        for a in range(n):
            copy(a, 0, sibling, me).wait_recv()
            for j, chip in enumerate(chips):
                copy(a, 4 + j, (*chip, 1 - c), me).wait_recv()
        for cp in first + passed:
            cp.wait_send()
        for cp in mine:
            cp.wait()

    return _Ride(
        shards, [jax.ShapeDtypeStruct((N_DEV, *s.shape), s.dtype) for s in shards],
        [pltpu.SemaphoreType.DMA((n, 7)), pltpu.SemaphoreType.DMA((n, 7)), pltpu.SemaphoreType.DMA((n,))],
        start, finish)


def _exchange_sibling(scatter, whole, name):
    ns, nw = len(scatter), len(whole)

    def body(*refs):
        s_in, w_in = refs[:ns], refs[ns:ns + nw]
        s_out, w_out = refs[ns + nw:2 * ns + nw], refs[2 * ns + nw:2 * (ns + nw)]
        send_sems, recv_sems = refs[2 * (ns + nw):]
        x, y, c = _place()
        sibling = (x, y, 1 - c)
        copies = []
        for a in range(ns):
            for chip in range(N_CHIP):
                copies.append(pltpu.make_async_remote_copy(
                    src_ref=s_in[a].at[2 * chip + (1 - c)], dst_ref=s_out[a].at[chip],
                    send_sem=send_sems.at[a * N_CHIP + chip], recv_sem=recv_sems.at[a * N_CHIP + chip],
                    device_id=sibling, device_id_type=MESH))
        for a in range(nw):
            copies.append(pltpu.make_async_remote_copy(
                src_ref=w_in[a], dst_ref=w_out[a],
                send_sem=send_sems.at[ns * N_CHIP + a], recv_sem=recv_sems.at[ns * N_CHIP + a],
                device_id=sibling, device_id_type=MESH))
        for cp in copies:
            cp.start()
        for cp in copies:
            cp.wait()

    n_sem = ns * N_CHIP + nw
    return pl.pallas_call(
        body, name=name,
        in_specs=[ANY] * (ns + nw), out_specs=[ANY] * (ns + nw),
        out_shape=([jax.ShapeDtypeStruct((N_CHIP, *s.shape[1:]), s.dtype) for s in scatter]
                   + [jax.ShapeDtypeStruct(s.shape, s.dtype) for s in whole]),
        scratch_shapes=[pltpu.SemaphoreType.DMA((n_sem,)), pltpu.SemaphoreType.DMA((n_sem,))],
    )(*scatter, *whole)


def _chips_ride(scatter, whole):
    ns, nw = len(scatter), len(whole)
    n = ns + nw

    def copies(ins, outs, sems):
        send_sems, recv_sems, local_sems = sems
        x, y, c = _place()
        own = 2 * x + y
        chips = [(1 - x, y), (x, 1 - y), (1 - x, 1 - y)]

        def src(a, chip_idx):
            return ins[a].at[chip_idx] if a < ns else ins[a]

        local = [pltpu.make_async_copy(src(a, own), outs[a].at[own], local_sems.at[a]) for a in range(n)]
        sent = [pltpu.make_async_remote_copy(
            src_ref=src(a, 2 * chip[0] + chip[1]), dst_ref=outs[a].at[own],
            send_sem=send_sems.at[a, j], recv_sem=recv_sems.at[a, own], device_id=(*chip, c), device_id_type=MESH)
            for a in range(n) for j, chip in enumerate(chips)]
        return chips, c, local, sent

    def start(ins, outs, sems):
        _, _, local, sent = copies(ins, outs, sems)
        for cp in local + sent:
            cp.start()

    def finish(ins, outs, sems):
        send_sems, recv_sems, _ = sems
        chips, c, local, sent = copies(ins, outs, sems)
        for a in range(n):
            for chip in chips:
                k = 2 * chip[0] + chip[1]
                pltpu.make_async_remote_copy(
                    src_ref=outs[a].at[k], dst_ref=outs[a].at[k], send_sem=send_sems.at[a, 0],
                    recv_sem=recv_sems.at[a, k], device_id=(*chip, c), device_id_type=MESH).wait_recv()
        for cp in sent:
            cp.wait_send()
        for cp in local:
            cp.wait()

    return _Ride(
        list(scatter) + list(whole),
        [jax.ShapeDtypeStruct(s.shape, s.dtype) for s in scatter]
        + [jax.ShapeDtypeStruct((N_CHIP, *s.shape), s.dtype) for s in whole],
        [pltpu.SemaphoreType.DMA((n, 3)), pltpu.SemaphoreType.DMA((n, N_CHIP)), pltpu.SemaphoreType.DMA((n,))],
        start, finish)


def _pair_sum_scatter(parts, recvs, core, name):
    na = len(parts)
    _, r, cdim = parts[0].shape
    tr = min(r, 416 if r % 416 == 0 else 128)

    def body(core_ref, *refs):
        del core_ref
        for a in range(na):
            refs[2 * na + a][...] = (refs[a][...] + refs[na + a][...]).astype(BF16)

    blk = (None, tr, cdim)
    mine = pl.BlockSpec(blk, lambda k, i, core_ref: (2 * k + core_ref[0], i, 0))
    slot = pl.BlockSpec(blk, lambda k, i, core_ref: (k, i, 0))
    return pl.pallas_call(
        body, name=name,
        grid_spec=pltpu.PrefetchScalarGridSpec(
            num_scalar_prefetch=1, grid=(N_CHIP, r // tr),
            in_specs=[mine] * na + [slot] * na, out_specs=[slot] * na),
        out_shape=[jax.ShapeDtypeStruct((N_CHIP, r, cdim), BF16)] * na,
        compiler_params=_params("parallel", "parallel"))(core, *parts, *recvs)


def _pair_sum_whole(mine, recvs, name):
    na = len(mine)

    def body(*refs):
        for a in range(na):
            refs[2 * na + a][...] = refs[a][...] + refs[na + a][...]

    return pl.pallas_call(body, name=name, out_shape=[jax.ShapeDtypeStruct(m.shape, F32) for m in mine])(*mine, *recvs)


def _allreduce_small(pack, name):
    shape = pack.shape

    def body(x_ref, o_ref, sib_ref, chip_ref, send_sems, recv_sems):
        x, y, c = _place()
        own = 2 * x + y
        chips = [(1 - x, y), (x, 1 - y), (1 - x, 1 - y)]
        to_sibling = pltpu.make_async_remote_copy(
            src_ref=x_ref, dst_ref=sib_ref, send_sem=send_sems.at[0], recv_sem=recv_sems.at[0],
            device_id=(x, y, 1 - c), device_id_type=MESH)
        to_sibling.start()
        to_sibling.wait()
        chip_ref[own] = x_ref[...] + sib_ref[...]
        sent = [pltpu.make_async_remote_copy(
            src_ref=chip_ref.at[own], dst_ref=chip_ref.at[own], send_sem=send_sems.at[1 + j],
            recv_sem=recv_sems.at[1 + own], device_id=(*chip, c), device_id_type=MESH) for j, chip in enumerate(chips)]
        for cp in sent:
            cp.start()
        for chip in chips:
            k = 2 * chip[0] + chip[1]
            pltpu.make_async_remote_copy(
                src_ref=chip_ref.at[k], dst_ref=chip_ref.at[k], send_sem=send_sems.at[1],
                recv_sem=recv_sems.at[1 + k], device_id=(*chip, c), device_id_type=MESH).wait_recv()
        for cp in sent:
            cp.wait_send()
        o_ref[...] = (chip_ref[0] + chip_ref[1]) + (chip_ref[2] + chip_ref[3])

    return pl.pallas_call(
        body, name=name, out_shape=jax.ShapeDtypeStruct(shape, F32),
        in_specs=[pl.BlockSpec(memory_space=pltpu.VMEM)], out_specs=pl.BlockSpec(memory_space=pltpu.VMEM),
        scratch_shapes=[pltpu.VMEM(shape, F32), pltpu.VMEM((N_CHIP, *shape), F32),
                        pltpu.SemaphoreType.DMA((4,)), pltpu.SemaphoreType.DMA((1 + N_CHIP,))],
    )(pack)


def _adamw(g, w, m, v):
    m = ADAM_B1 * m + (1.0 - ADAM_B1) * g
    v = ADAM_B2 * v + (1.0 - ADAM_B2) * (g * g)
    m_hat = m / (1.0 - ADAM_B1 ** ADAM_STEP)
    v_hat = v / (1.0 - ADAM_B2 ** ADAM_STEP)
    delta = -ADAM_LR * (m_hat / (jnp.sqrt(v_hat) + ADAM_EPS) + ADAM_WD * w)
    return delta, m, v


def _adam_parts(parts, w, m, v, name, tr=None):
    npart, r, c = parts.shape
    tr = r if tr is None else min(tr, r)

    def body(p_ref, w_ref, m_ref, v_ref, g_ref, d_ref, nm_ref, nv_ref):
        g = p_ref[0].astype(F32)
        for k in range(1, npart):
            g = g + p_ref[k].astype(F32)
        g_ref[...] = g
        d_ref[...], nm_ref[...], nv_ref[...] = _adamw(g, w_ref[...], m_ref[...], v_ref[...])

    tile = pl.BlockSpec((tr, c), lambda i: (i, 0))
    return pl.pallas_call(
        body, name=name, grid=(r // tr,),
        in_specs=[pl.BlockSpec((npart, tr, c), lambda i: (0, i, 0)), tile, tile, tile],
        out_specs=[tile] * 4, out_shape=[jax.ShapeDtypeStruct((r, c), F32)] * 4,
        compiler_params=_params("parallel"))(parts, w, m, v)


def _adam_transposed(parts, w, m, v, name, tc=256):
    npart, c, r = parts.shape

    def body(p_ref, w_ref, m_ref, v_ref, g_ref, d_ref, nm_ref, nv_ref):
        gt = p_ref[0].astype(F32)
        for k in range(1, npart):
            gt = gt + p_ref[k].astype(F32)
        g = gt.T
        g_ref[...] = g
        d_ref[...], nm_ref[...], nv_ref[...] = _adamw(g, w_ref[...], m_ref[...], v_ref[...])

    tile = pl.BlockSpec((tc, c), lambda i: (i, 0))
    return pl.pallas_call(
        body, name=name, grid=(r // tc,),
        in_specs=[pl.BlockSpec((npart, c, tc), lambda i: (0, 0, i)), tile, tile, tile],
        out_specs=[tile] * 4, out_shape=[jax.ShapeDtypeStruct((r, c), F32)] * 4,
        compiler_params=_params("parallel"))(parts, w, m, v)


def _transpose_bf16(w, name, tr=256):
    r, c = w.shape

    def body(w_ref, o_ref):
        o_ref[...] = w_ref[...].T.astype(BF16)

    return pl.pallas_call(
        body, name=name, grid=(r // tr,),
        in_specs=[pl.BlockSpec((tr, c), lambda i: (i, 0))],
        out_specs=pl.BlockSpec((c, tr), lambda i: (0, i)),
        out_shape=jax.ShapeDtypeStruct((c, r), BF16),
        compiler_params=_params("parallel"))(w)


def _block_diag(w):
    w4 = w.reshape(N_GROUPS, 4, RNN_BLOCK_W, RNN_BLOCK_W)
    eye = jnp.eye(4, dtype=w.dtype)
    return jnp.einsum("gbij,bc->gbicj", w4, eye).reshape(N_GROUPS, GROUP_W, GROUP_W).astype(BF16)


SMALL_ROWS = 16
ROW_PRE_G, ROW_BGATE, ROW_CONV_B, ROW_B_A, ROW_B_X, ROW_LAM, ROW_POST_G, ROW_LOSS, ROW_SINKS, ROW_CONV_W = 0, 1, 3, 4, 5, 6, 7, 8, 9, 10


def _pack_stats(st_pre, st_merge, st_rnn, st_post, st_sink, name):
    d = D_MODEL

    def body(pre_ref, mg_ref, rnn_ref, post_ref, sink_ref, o_ref):
        rnn = rnn_ref[...]
        sinks = jnp.concatenate([sink_ref[0:1, :], jnp.zeros((1, d - LANE), F32)], axis=1)
        o_ref[0:8, :] = _rows8([pre_ref[0:1, :], mg_ref[0:1, :], mg_ref[1:2, :], rnn[0:1], rnn[1:2], rnn[2:3], rnn[3:4],
                                post_ref[0:1, :]], d)
        o_ref[8:16, :] = _rows8([post_ref[1:2, :], sinks, rnn[4:5], rnn[5:6], rnn[6:7], rnn[7:8]], d)

    return pl.pallas_call(body, name=name, out_shape=jax.ShapeDtypeStruct((SMALL_ROWS, d), F32))(
        st_pre, st_merge, st_rnn, st_post, st_sink)


def _adam_small(total, w, m, v, name):
    d = D_MODEL
    n_in = len(w)

    def pack(refs):
        pre, bg, cbias, ba, bx, lam, post, sinks = [r[...] for r in refs]
        top = _rows8([pre, bg[:, 0:d], bg[:, d:2 * d], cbias, ba, bx, lam, post], d)
        return jnp.concatenate([top, _rows8([jnp.zeros((1, d), F32), sinks], d)], axis=0)

    def body(*refs):
        p_ref = refs[0]
        w_refs, m_refs, v_refs = (refs[1 + k * n_in:1 + (k + 1) * n_in] for k in range(3))
        outs = refs[1 + 3 * n_in:]
        g = p_ref[...]
        res = (g,) + _adamw(g, pack(w_refs), pack(m_refs), pack(v_refs))
        for k in range(4):
            outs[k][...] = res[k]
            outs[4 + k][...] = jnp.concatenate([res[k][ROW_BGATE:ROW_BGATE + 1], res[k][ROW_BGATE + 1:ROW_BGATE + 2]], axis=1)

    return pl.pallas_call(
        body, name=name,
        out_shape=[jax.ShapeDtypeStruct((SMALL_ROWS, d), F32)] * 4 + [jax.ShapeDtypeStruct((1, 2 * d), F32)] * 4,
    )(total, *w, *m, *v)


def kernel(x, pre_norm_g, w_in, b_gate, conv_w, conv_b, w_rg_a, b_rg_a, w_rg_x, b_rg_x, lru_lambda, attn_sinks, w_rnn_out, w_attn_out, w_out, post_norm_g, loss_target, m_pre_norm_g, m_w_in, m_b_gate, m_conv_w, m_conv_b, m_w_rg_a, m_b_rg_a, m_w_rg_x, m_b_rg_x, m_lru_lambda, m_attn_sinks, m_w_rnn_out, m_w_attn_out, m_w_out, m_post_norm_g, v_pre_norm_g, v_w_in, v_b_gate, v_conv_w, v_conv_b, v_w_rg_a, v_b_rg_a, v_w_rg_x, v_b_rg_x, v_lru_lambda, v_attn_sinks, v_w_rnn_out, v_w_attn_out, v_w_out, v_post_norm_g):
    cx, cy, cc = _place()
    dev = 4 * cx + 2 * cy + cc
    core = jnp.reshape(cc, (1,)).astype(jnp.int32)

    wt_shard = _transpose_bf16(w_in[0], "wt_shard")
    (wt_all,) = _run_ride(_gather_ride([wt_shard]), "gather_w_in")
    heads = jnp.arange(1, N_Q_HEADS + 1, dtype=F32)
    slopes = jnp.exp2(-ALIBI_MAX_BIAS * heads / N_Q_HEADS)
    b_a = b_rg_a.reshape(1, D_RNN)
    b_x = b_rg_x.reshape(1, D_RNN)
    p = dict(
        wt=wt_all.reshape(D_IN, D_MODEL),
        pre_g=pre_norm_g, post_g=post_norm_g, b_gate=b_gate, cb=conv_b,
        wbd_a=_block_diag(w_rg_a[0]), b_a=b_a, wbd_x=_block_diag(w_rg_x[0]), b_x=b_x, lam=lru_lambda,
        sm=jnp.pad(attn_sinks, ((0, 1), (0, 0))) + jnp.pad(slopes[None, :], ((1, 0), (0, 0))))

    def late_unpack(landed):
        w_rnn_all, w_attn_all, w_out_all, cw_all = landed
        return dict(w_rnn=w_rnn_all.reshape(D_RNN, D_MODEL), w_attn=w_attn_all.reshape(D_MODEL, D_MODEL),
                    w_out=w_out_all.reshape(D_MODEL, D_MODEL), cw=jnp.transpose(cw_all, (1, 0, 2)).reshape(4, D_RNN))

    late_weights = (_gather_ride([w_rnn_out[0].astype(BF16), w_attn_out[0].astype(BF16), w_out[0].astype(BF16), conv_w[0]]),
                    late_unpack)
    flat = (RNN_BLOCKS * RNN_BLOCK_W, RNN_BLOCK_W)

    def reduce_out(gw_rnn, gw_attn, gw_out, g_rg_a, g_rg_x):
        scatter = [gw.reshape(N_DEV, SHARD_OUT, D_MODEL) for gw in (gw_rnn, gw_attn, gw_out)]
        whole = [g_rg_a.reshape(flat), g_rg_x.reshape(flat)]
        recv = _exchange_sibling(scatter, whole, "sibling_out")
        return _chips_ride(_pair_sum_scatter(scatter, recv[:3], core, "pair_out"),
                           _pair_sum_whole(whole, recv[3:], "pair_rg"))

    def reduce_in(gwt):
        scatter = [gwt.reshape(N_DEV, SHARD_IN, D_MODEL)]
        recv = _exchange_sibling(scatter, [], "sibling_in")
        return _chips_ride(_pair_sum_scatter(scatter, recv, core, "pair_in"), [])

    g = _local_grads(x[0], loss_target[0], p, late_weights, reduce_out, reduce_in)
    small = _allreduce_small(
        _pack_stats(g["st_pre"], g["st_merge"], g["st_rnn"], g["st_post"], g["st_sink"], "pack_stats"), "allreduce_small")

    out = {}
    red = g["red_out"]
    out["w_in"] = _adam_transposed(g["red_in"][0], w_in[0], m_w_in[0], v_w_in[0], "adam_w_in")
    out["w_rnn_out"] = _adam_parts(red[0], w_rnn_out[0], m_w_rnn_out[0], v_w_rnn_out[0], "adam_w_rnn_out")
    out["w_attn_out"] = _adam_parts(red[1], w_attn_out[0], m_w_attn_out[0], v_w_attn_out[0], "adam_w_attn_out")
    out["w_out"] = _adam_parts(red[2], w_out[0], m_w_out[0], v_w_out[0], "adam_w_out")
    out["w_rg_a"] = _adam_parts(red[3], w_rg_a.reshape(flat), m_w_rg_a.reshape(flat), v_w_rg_a.reshape(flat), "adam_w_rg_a", tr=256)
    out["w_rg_x"] = _adam_parts(red[4], w_rg_x.reshape(flat), m_w_rg_x.reshape(flat), v_w_rg_x.reshape(flat), "adam_w_rg_x", tr=256)

    def rows(pre, bg, cbias, ba, bx, lam, post, sinks):
        return (pre, bg, cbias, ba.reshape(1, D_RNN), bx.reshape(1, D_RNN), lam, post,
                jnp.pad(sinks, ((0, 0), (0, D_MODEL - N_Q_HEADS))))

    small_out = _adam_small(
        small,
        rows(pre_norm_g, b_gate, conv_b, b_rg_a, b_rg_x, lru_lambda, post_norm_g, attn_sinks),
        rows(m_pre_norm_g, m_b_gate, m_conv_b, m_b_rg_a, m_b_rg_x, m_lru_lambda, m_post_norm_g, m_attn_sinks),
        rows(v_pre_norm_g, v_b_gate, v_conv_b, v_b_rg_a, v_b_rg_x, v_lru_lambda, v_post_norm_g, v_attn_sinks),
        "adam_small")
    packed, bgate_out = small_out[:4], small_out[4:]
    g_cw = lax.dynamic_slice(packed[0][ROW_CONV_W:ROW_CONV_W + 4], (0, dev * SHARD_OUT), (4, SHARD_OUT))
    out["conv_w"] = _adam_parts(g_cw[None], conv_w[0], m_conv_w[0], v_conv_w[0], "adam_conv_w")

    def unpack(kind, name):
        if name == "b_gate":
            return bgate_out[kind]
        row = dict(pre_norm_g=ROW_PRE_G, conv_b=ROW_CONV_B, b_rg_a=ROW_B_A, b_rg_x=ROW_B_X, lru_lambda=ROW_LAM,
                   post_norm_g=ROW_POST_G, attn_sinks=ROW_SINKS)[name]
        r = packed[kind][row:row + 1]
        if name == "attn_sinks":
            return r[:, 0:N_Q_HEADS]
        if name in ("b_rg_a", "b_rg_x"):
            return r.reshape(1, RNN_BLOCKS, RNN_BLOCK_W)
        return r

    shapes = dict(w_in=(1, D_MODEL, SHARD_IN), w_rnn_out=(1, SHARD_OUT, D_MODEL), w_attn_out=(1, SHARD_OUT, D_MODEL),
                  w_out=(1, SHARD_OUT, D_MODEL), w_rg_a=(1, RNN_BLOCKS, RNN_BLOCK_W, RNN_BLOCK_W),
                  w_rg_x=(1, RNN_BLOCKS, RNN_BLOCK_W, RNN_BLOCK_W), conv_w=(1, 4, SHARD_OUT))
    weights = ["pre_norm_g", "w_in", "b_gate", "conv_w", "conv_b", "w_rg_a", "b_rg_a", "w_rg_x", "b_rg_x",
               "lru_lambda", "attn_sinks", "w_rnn_out", "w_attn_out", "w_out", "post_norm_g"]
    results = []
    for kind in range(4):
        for name in weights:
            if name in out:
                results.append(out[name][kind].reshape(shapes[name]))
            else:
                results.append(unpack(kind, name))
    loss = 0.5 / D_MODEL * jnp.sum(packed[0][ROW_LOSS])
    return (loss, g["grad_x"][None], *results)
```

```python
import functools

import jax
import jax.numpy as jnp
from jax import lax
from jax.experimental import pallas as pl
from jax.experimental.pallas import tpu as pltpu

F32 = jnp.float32
BF16 = jnp.bfloat16

D_MODEL = 1024
D_RNN = 1024
RNN_BLOCKS = 16
RNN_BLOCK_W = 64
LRU_C = 8.0
N_Q_HEADS = 16
HEAD_DIM = 64
D_KV = 256
BLOCK = 128
ALIBI_MAX_BIAS = 8.0
EPS = 1e-6
D_IN = 6656
N_DEV = 8
N_CHIP = 4
SHARD_IN = D_IN // N_DEV
SHARD_OUT = D_MODEL // N_DEV
ATTN_SCALE = HEAD_DIM ** -0.5
MASKED = -1e30

ADAM_LR = 0.001
ADAM_B1 = 0.9
ADAM_B2 = 0.999
ADAM_EPS = 1e-08
ADAM_WD = 0.01
ADAM_STEP = 10

VMEM_LIMIT_BYTES = 52 * 1024 * 1024
LANE = 128
GROUP_W = 256
N_GROUPS = D_RNN // GROUP_W
SEG_CHUNK = 512

NT_DIMS = (((1,), (1,)), ((), ()))
TN_DIMS = (((0,), (0,)), ((), ()))
MESH = pl.DeviceIdType.MESH
ANY = pl.BlockSpec(memory_space=pl.ANY)


def _params(*semantics):
    return pltpu.CompilerParams(dimension_semantics=semantics, vmem_limit_bytes=VMEM_LIMIT_BYTES)


def _sigmoid(x):
    return 0.5 * jnp.tanh(0.5 * x) + 0.5


def _log1p(e):
    u = 1.0 + e
    den = jnp.where(u == 1.0, 1.0, u - 1.0)
    return jnp.where(u == 1.0, e, jnp.log(u) * (e / den))


def _softplus(z):
    return jnp.maximum(z, 0.0) + _log1p(jnp.exp(-jnp.abs(z)))


def _rows8(rows, width):
    idx = lax.broadcasted_iota(jnp.int32, (8, width), 0)
    out = jnp.zeros((8, width), F32)
    for r, v in enumerate(rows):
        out = jnp.where(idx == r, v, out)
    return out


class _Ride:
    def __init__(self, arrays, out_shapes, scratch_shapes, start, finish):
        self.arrays, self.out_shapes, self.scratch_shapes = list(arrays), list(out_shapes), list(scratch_shapes)
        self.start, self.finish = start, finish


class _Hosted:
    def __init__(self, ride, n_in, n_out, n_scratch=0):
        self.ride = ride
        self.sizes = (n_in, len(ride.arrays) if ride else 0, n_out, len(ride.out_shapes) if ride else 0, n_scratch)
        self.arrays = ride.arrays if ride else []
        self.in_specs = [ANY] * len(self.arrays)
        self.out_shapes = ride.out_shapes if ride else []
        self.out_specs = [ANY] * len(self.out_shapes)
        self.scratch_shapes = ride.scratch_shapes if ride else []

    def split(self, refs):
        n_in, r_in, n_out, r_out, n_scr = self.sizes
        cuts = [0, n_in, n_in + r_in, n_in + r_in + n_out, n_in + r_in + n_out + r_out, n_in + r_in + n_out + r_out + n_scr]
        host_in, ride_in, host_out, ride_out, host_scr = (refs[cuts[k]:cuts[k + 1]] for k in range(5))
        ride_scr = refs[cuts[5]:]

        def start(when):
            if self.ride is not None:
                pl.when(when)(lambda: self.ride.start(ride_in, ride_out, ride_scr))

        def finish(when):
            if self.ride is not None:
                pl.when(when)(lambda: self.ride.finish(ride_in, ride_out, ride_scr))

        return tuple(host_in) + tuple(host_out) + tuple(host_scr), start, finish

    def results(self, outs, n_out):
        outs = list(outs) if isinstance(outs, (list, tuple)) else [outs]
        return outs[:n_out], outs[n_out:]


def _run_ride(ride, name):
    n_in, n_out = len(ride.arrays), len(ride.out_shapes)

    def body(*refs):
        ins, outs, sems = refs[:n_in], refs[n_in:n_in + n_out], refs[n_in + n_out:]
        ride.start(ins, outs, sems)
        ride.finish(ins, outs, sems)

    return pl.pallas_call(
        body, name=name, in_specs=[ANY] * n_in, out_specs=[ANY] * n_out,
        out_shape=ride.out_shapes, scratch_shapes=ride.scratch_shapes)(*ride.arrays)


def _mm_nt(a, b, b_row_off, n, out_dtype, name, tm=1024, tn=512, ride=None):
    m, k = a.shape
    tm = min(tm, m)
    off = b_row_off // tn
    ni, nj = m // tm, n // tn
    host = _Hosted(ride, 2, 1)

    def body(*refs):
        (a_ref, b_ref, o_ref), start, finish = host.split(refs)
        i, j = pl.program_id(0), pl.program_id(1)
        start((i == 0) & (j == 0))
        o_ref[...] = lax.dot_general(a_ref[...], b_ref[...], NT_DIMS, preferred_element_type=F32).astype(o_ref.dtype)
        finish((i == ni - 1) & (j == nj - 1))

    outs = pl.pallas_call(
        body, name=name, grid=(ni, nj),
        in_specs=[pl.BlockSpec((tm, k), lambda i, j: (i, 0)), pl.BlockSpec((tn, k), lambda i, j: (j + off, 0))] + host.in_specs,
        out_specs=[pl.BlockSpec((tm, tn), lambda i, j: (i, j))] + host.out_specs,
        out_shape=[jax.ShapeDtypeStruct((m, n), out_dtype)] + host.out_shapes,
        scratch_shapes=host.scratch_shapes,
        compiler_params=_params("arbitrary", "arbitrary") if ride else _params("parallel", "parallel"))(a, b, *host.arrays)
    (res,), landed = host.results(outs, 1)
    return (res, landed) if ride else res


def _mm_tn(a, b, name, tm=512, tk=1024):
    ktok, m = a.shape
    n = b.shape[1]
    tk = min(tk, ktok)

    def body(a_ref, b_ref, o_ref):
        @pl.when(pl.program_id(1) == 0)
        def _():
            o_ref[...] = jnp.zeros_like(o_ref)

        o_ref[...] += lax.dot_general(a_ref[...], b_ref[...], TN_DIMS, preferred_element_type=F32)

    return pl.pallas_call(
        body, name=name, grid=(m // tm, ktok // tk),
        in_specs=[pl.BlockSpec((tk, tm), lambda i, kk: (kk, i)), pl.BlockSpec((tk, n), lambda i, kk: (kk, 0))],
        out_specs=pl.BlockSpec((tm, n), lambda i, kk: (i, 0)),
        out_shape=jax.ShapeDtypeStruct((m, n), F32),
        compiler_params=_params("parallel", "arbitrary"))(a, b)


def _segment_chunks(segs):
    bounds = [0]
    for s in segs:
        bounds.append(bounds[-1] + s.shape[1] // SEG_CHUNK)
    return bounds


def _seg_index_map(lo, hi, tok_axis_first):
    if tok_axis_first:
        return lambda i, c: (i, jnp.clip(c - lo, 0, hi - lo - 1))
    return lambda c, kk: (jnp.where((c >= lo) & (c < hi), kk, 0), jnp.clip(c - lo, 0, hi - lo - 1))


def _mm_nn_seg(segs, wt, name, tm=1024, ride=None):
    m = segs[0].shape[0]
    n = wt.shape[1]
    tm = min(tm, m)
    bounds = _segment_chunks(segs)
    n_seg = len(segs)
    ni, nc = m // tm, bounds[-1]
    host = _Hosted(ride, n_seg + 1, 1)

    def body(*refs):
        host_refs, start, finish = host.split(refs)
        a_refs, w_ref, o_ref = host_refs[:n_seg], host_refs[n_seg], host_refs[n_seg + 1]
        i, c = pl.program_id(0), pl.program_id(1)
        start((i == 0) & (c == 0))

        @pl.when(c == 0)
        def _():
            o_ref[...] = jnp.zeros_like(o_ref)

        for s in range(n_seg):
            @pl.when((c >= bounds[s]) & (c < bounds[s + 1]))
            def _(s=s):
                o_ref[...] += jnp.dot(a_refs[s][...], w_ref[...], preferred_element_type=F32)

        finish((i == ni - 1) & (c == nc - 1))

    in_specs = [pl.BlockSpec((tm, SEG_CHUNK), _seg_index_map(bounds[s], bounds[s + 1], True)) for s in range(n_seg)]
    in_specs.append(pl.BlockSpec((SEG_CHUNK, n), lambda i, c: (c, 0)))
    outs = pl.pallas_call(
        body, name=name, grid=(ni, nc),
        in_specs=in_specs + host.in_specs, out_specs=[pl.BlockSpec((tm, n), lambda i, c: (i, 0))] + host.out_specs,
        out_shape=[jax.ShapeDtypeStruct((m, n), F32)] + host.out_shapes,
        scratch_shapes=host.scratch_shapes,
        compiler_params=_params("arbitrary" if ride else "parallel", "arbitrary"))(*segs, wt, *host.arrays)
    (res,), landed = host.results(outs, 1)
    return (res, landed) if ride else res


def _mm_tn_seg(segs, b, name, tk=1024):
    ktok = segs[0].shape[0]
    n = b.shape[1]
    tk = min(tk, ktok)
    bounds = _segment_chunks(segs)
    n_seg = len(segs)

    def body(*refs):
        a_refs, b_ref, o_ref = refs[:n_seg], refs[n_seg], refs[n_seg + 1]
        c = pl.program_id(0)
        kk = pl.program_id(1)

        @pl.when(kk == 0)
        def _():
            o_ref[...] = jnp.zeros_like(o_ref)

        rows = b_ref[pl.ds(pl.multiple_of(kk * tk, tk), tk), :]
        for s in range(n_seg):
            @pl.when((c >= bounds[s]) & (c < bounds[s + 1]))
            def _(s=s):
                o_ref[...] += lax.dot_general(a_refs[s][...], rows, TN_DIMS, preferred_element_type=F32)

    in_specs = [pl.BlockSpec((tk, SEG_CHUNK), _seg_index_map(bounds[s], bounds[s + 1], False)) for s in range(n_seg)]
    in_specs.append(pl.BlockSpec((ktok, n), lambda c, kk: (0, 0)))
    return pl.pallas_call(
        body, name=name, grid=(bounds[-1], ktok // tk),
        in_specs=in_specs, out_specs=pl.BlockSpec((SEG_CHUNK, n), lambda c, kk: (c, 0)),
        out_shape=jax.ShapeDtypeStruct((bounds[-1] * SEG_CHUNK, n), F32),
        compiler_params=_params("parallel", "arbitrary"))(*segs, b)


def _prenorm(x, g, name, tm=512):
    t, d = x.shape
    tm = min(tm, t)

    def body(x_ref, g_ref, h_ref):
        x = x_ref[...]
        r = lax.rsqrt(jnp.mean(x * x, axis=-1, keepdims=True) + EPS)
        h_ref[...] = (x * r * g_ref[...]).astype(BF16)

    return pl.pallas_call(
        body, name=name, grid=(t // tm,),
        in_specs=[pl.BlockSpec((tm, d), lambda i: (i, 0)), pl.BlockSpec((1, d), lambda i: (0, 0))],
        out_specs=pl.BlockSpec((tm, d), lambda i: (i, 0)),
        out_shape=jax.ShapeDtypeStruct((t, d), BF16),
        compiler_params=_params("parallel"))(x, g)


def _prenorm_bwd(x, g, dh, dy, name, tm=512):
    t, d = x.shape
    tm = min(tm, t)

    def body(x_ref, g_ref, dh_ref, dy_ref, gx_ref, st_ref):
        @pl.when(pl.program_id(0) == 0)
        def _():
            st_ref[...] = jnp.zeros_like(st_ref)

        x = x_ref[...]
        r = lax.rsqrt(jnp.mean(x * x, axis=-1, keepdims=True) + EPS)
        xn = x * r
        dh = dh_ref[...]
        dxn = dh * g_ref[...]
        dx = r * (dxn - xn * jnp.mean(dxn * xn, axis=-1, keepdims=True))
        gx_ref[...] = dy_ref[...] + dx
        st_ref[...] += _rows8([jnp.sum(dh * xn, axis=0, keepdims=True)], d)

    tile = pl.BlockSpec((tm, d), lambda i: (i, 0))
    return pl.pallas_call(
        body, name=name, grid=(t // tm,),
        in_specs=[tile, pl.BlockSpec((1, d), lambda i: (0, 0)), tile, tile],
        out_specs=[tile, pl.BlockSpec((8, d), lambda i: (0, 0))],
        out_shape=[jax.ShapeDtypeStruct((t, d), F32), jax.ShapeDtypeStruct((8, d), F32)],
        compiler_params=_params("arbitrary"))(x, g, dh, dy)


def _branches_fwd(z_rnn, z_attn, ag_ml, b_gate, w_rnn, w_attn, w_out, x, target, g_post, name, tm=512):
    t, d = x.shape
    tm = min(tm, t)

    def body(zr_ref, za_ref, lr_ref, la_ref, br_ref, ba_ref, wr_ref, wa_ref, wo_ref, x_ref, t_ref, g_ref,
             brr_ref, bra_ref, mg_ref, do_ref, dy_ref, st_ref):
        @pl.when(pl.program_id(0) == 0)
        def _():
            st_ref[...] = jnp.zeros_like(st_ref)

        br_rnn = jnp.dot(zr_ref[...], wr_ref[...], preferred_element_type=F32)
        br_attn = jnp.dot(za_ref[...], wa_ref[...], preferred_element_type=F32)
        brr_ref[...] = br_rnn.astype(BF16)
        bra_ref[...] = br_attn.astype(BF16)
        g_rnn = _sigmoid(lr_ref[...] + br_ref[...])
        g_attn = _sigmoid(la_ref[...] + ba_ref[...])
        merged = (g_rnn * br_rnn + g_attn * br_attn).astype(BF16)
        mg_ref[...] = merged
        o = jnp.dot(merged, wo_ref[...], preferred_element_type=F32)
        g = g_ref[...]
        r = lax.rsqrt(jnp.mean(o * o, axis=-1, keepdims=True) + EPS)
        nrm = o * r
        err = x_ref[...] + nrm * g - t_ref[...]
        dy = err * (1.0 / d)
        dy_ref[...] = dy
        dn = dy * g
        do_ref[...] = (r * (dn - nrm * jnp.mean(dn * nrm, axis=-1, keepdims=True))).astype(BF16)
        st_ref[...] += _rows8([jnp.sum(dy * nrm, axis=0, keepdims=True), jnp.sum(err * err, axis=0, keepdims=True)], d)

    tile = pl.BlockSpec((tm, d), lambda i: (i, 0))
    weight = pl.BlockSpec((d, d), lambda i: (0, 0))
    bf = jax.ShapeDtypeStruct((t, d), BF16)
    return pl.pallas_call(
        body, name=name, grid=(t // tm,),
        in_specs=[tile, tile, pl.BlockSpec((tm, d), lambda i: (i, 1)), pl.BlockSpec((tm, d), lambda i: (i, 2)),
                  pl.BlockSpec((1, d), lambda i: (0, 0)), pl.BlockSpec((1, d), lambda i: (0, 1)),
                  weight, weight, weight, tile, tile, pl.BlockSpec((1, d), lambda i: (0, 0))],
        out_specs=[tile, tile, tile, tile, tile, pl.BlockSpec((8, d), lambda i: (0, 0))],
        out_shape=[bf, bf, bf, bf, jax.ShapeDtypeStruct((t, d), F32), jax.ShapeDtypeStruct((8, d), F32)],
        compiler_params=_params("arbitrary"))(z_rnn, z_attn, ag_ml, ag_ml, b_gate, b_gate, w_rnn, w_attn, w_out, x, target, g_post)


def _branches_bwd(dout, br_rnn, br_attn, ag_ml, b_gate, w_rnn, w_attn, w_out, name, tm=512):
    t, d = br_rnn.shape
    tm = min(tm, t)

    def body(do_ref, r_ref, a_ref, lr_ref, la_ref, br_ref, ba_ref, wr_ref, wa_ref, wo_ref,
             dr_ref, da_ref, dl_ref, dzr_ref, dza_ref, st_ref):
        @pl.when(pl.program_id(0) == 0)
        def _():
            st_ref[...] = jnp.zeros_like(st_ref)

        dm = lax.dot_general(do_ref[...], wo_ref[...], NT_DIMS, preferred_element_type=F32)
        g_rnn = _sigmoid(lr_ref[...] + br_ref[...])
        g_attn = _sigmoid(la_ref[...] + ba_ref[...])
        dbr_rnn = (dm * g_rnn).astype(BF16)
        dbr_attn = (dm * g_attn).astype(BF16)
        dr_ref[...] = dbr_rnn
        da_ref[...] = dbr_attn
        dl_rnn = dm * r_ref[...].astype(F32) * g_rnn * (1.0 - g_rnn)
        dl_attn = dm * a_ref[...].astype(F32) * g_attn * (1.0 - g_attn)
        dl_ref[:, 0:d] = dl_rnn.astype(BF16)
        dl_ref[:, d:2 * d] = dl_attn.astype(BF16)
        st_ref[...] += _rows8([jnp.sum(dl_rnn, axis=0, keepdims=True), jnp.sum(dl_attn, axis=0, keepdims=True)], d)
        dzr_ref[...] = lax.dot_general(dbr_rnn, wr_ref[...], NT_DIMS, preferred_element_type=F32)
        dza_ref[...] = lax.dot_general(dbr_attn, wa_ref[...], NT_DIMS, preferred_element_type=F32)

    tile = pl.BlockSpec((tm, d), lambda i: (i, 0))
    weight = pl.BlockSpec((d, d), lambda i: (0, 0))
    bf = jax.ShapeDtypeStruct((t, d), BF16)
    return pl.pallas_call(
        body, name=name, grid=(t // tm,),
        in_specs=[tile, tile, tile, pl.BlockSpec((tm, d), lambda i: (i, 1)), pl.BlockSpec((tm, d), lambda i: (i, 2)),
                  pl.BlockSpec((1, d), lambda i: (0, 0)), pl.BlockSpec((1, d), lambda i: (0, 1)), weight, weight, weight],
        out_specs=[tile, tile, pl.BlockSpec((tm, 2 * d), lambda i: (i, 0)), tile, tile, pl.BlockSpec((8, d), lambda i: (0, 0))],
        out_shape=[bf, bf, jax.ShapeDtypeStruct((t, 2 * d), BF16), jax.ShapeDtypeStruct((t, d), F32),
                   jax.ShapeDtypeStruct((t, d), F32), jax.ShapeDtypeStruct((8, d), F32)],
        compiler_params=_params("arbitrary"))(dout, br_rnn, br_attn, ag_ml, ag_ml, b_gate, b_gate, w_rnn, w_attn, w_out)


def _lru_gates(c, wa, ba, wx, bx, sp):
    cb = c.astype(BF16)
    r = _sigmoid(jnp.dot(cb, wa, preferred_element_type=F32) + ba)
    ig = _sigmoid(jnp.dot(cb, wx, preferred_element_type=F32) + bx)
    log_a = (-LRU_C) * r * sp
    a = jnp.exp(log_a)
    mult = jnp.sqrt(-jnp.tanh(log_a) * (a * a + 1.0))
    return cb, r, ig, a, mult


SUBLANES = 8


def _scan_fwd(a, u, carry, tt):
    w = a.shape[1]
    ng = tt // SUBLANES
    a3 = a.reshape(ng, SUBLANES, w)
    u3 = u.reshape(ng, SUBLANES, w)
    sub = lax.broadcasted_iota(jnp.int32, (ng, SUBLANES, w), 1)
    d = 1
    while d < SUBLANES:
        keep = sub >= d
        u3 = u3 + a3 * jnp.where(keep, pltpu.roll(u3, d, 1), 0.0)
        a3 = a3 * jnp.where(keep, pltpu.roll(a3, d, 1), 1.0)
        d *= 2
    out = []
    for g in range(ng):
        hg = u3[g] + a3[g] * carry
        out.append(hg)
        carry = hg[SUBLANES - 1:SUBLANES, :]
    return jnp.concatenate(out, axis=0)


def _scan_rev(b, g, carry, tt):
    w = b.shape[1]
    ng = tt // SUBLANES
    b3 = b.reshape(ng, SUBLANES, w)
    g3 = g.reshape(ng, SUBLANES, w)
    sub = lax.broadcasted_iota(jnp.int32, (ng, SUBLANES, w), 1)
    d = 1
    while d < SUBLANES:
        keep = sub < SUBLANES - d
        g3 = g3 + b3 * jnp.where(keep, pltpu.roll(g3, SUBLANES - d, 1), 0.0)
        b3 = b3 * jnp.where(keep, pltpu.roll(b3, SUBLANES - d, 1), 1.0)
        d *= 2
    out = [None] * ng
    for k in range(ng - 1, -1, -1):
        hk = g3[k] + b3[k] * carry
        out[k] = hk
        carry = hk[0:1, :]
    return jnp.concatenate(out, axis=0)


def _conv_taps(cw, bias, x, ext_ref, tt):
    x2 = ext_ref[7:7 + tt, :]
    x1 = ext_ref[6:6 + tt, :]
    x0 = ext_ref[5:5 + tt, :]
    c = bias + cw[3:4] * x + cw[2:3] * x2 + cw[1:2] * x1 + cw[0:1] * x0
    return c, x2, x1, x0


def _rnn_fwd(rx_rg, cw, cb, wa, ba, wx, bx, lam, name, tt=512):
    t = rx_rg.shape[0]
    tt = min(tt, t)
    w = GROUP_W

    def body(rx_ref, rg_ref, cw_ref, cb_ref, wa_ref, ba_ref, wx_ref, bx_ref, lam_ref, y_ref, z_ref, ext_ref, hc_ref):
        @pl.when(pl.program_id(1) == 0)
        def _():
            ext_ref[0:8, :] = jnp.zeros((8, w), F32)
            hc_ref[...] = jnp.zeros((8, w), F32)

        x = rx_ref[...]
        ext_ref[8:8 + tt, :] = x
        c, _, _, _ = _conv_taps(cw_ref[...], cb_ref[...], x, ext_ref, tt)
        ext_ref[0:8, :] = x[tt - 8:tt, :]
        sp = _softplus(-lam_ref[...])
        _, _, ig, a, mult = _lru_gates(c, wa_ref[...], ba_ref[...], wx_ref[...], bx_ref[...], sp)
        h = _scan_fwd(a, mult * (ig * c), hc_ref[7:8, :], tt)
        hc_ref[...] = h[tt - 8:tt, :]
        y_ref[...] = h
        rg = rg_ref[...]
        z_ref[...] = (h * rg * _sigmoid(rg)).astype(BF16)

    vec = pl.BlockSpec((1, w), lambda g, i: (0, g))
    mat = pl.BlockSpec((None, w, w), lambda g, i: (g, 0, 0))
    tile = pl.BlockSpec((tt, w), lambda g, i: (i, g))
    return pl.pallas_call(
        body, name=name, grid=(N_GROUPS, t // tt),
        in_specs=[tile, pl.BlockSpec((tt, w), lambda g, i: (i, N_GROUPS + g)),
                  pl.BlockSpec((4, w), lambda g, i: (0, g)), vec, mat, vec, mat, vec, vec],
        out_specs=[tile, tile],
        out_shape=[jax.ShapeDtypeStruct((t, D_RNN), F32), jax.ShapeDtypeStruct((t, D_RNN), BF16)],
        scratch_shapes=[pltpu.VMEM((tt + 8, w), F32), pltpu.VMEM((8, w), F32)],
        compiler_params=_params("parallel", "arbitrary"))(rx_rg, rx_rg, cw, cb, wa, ba, wx, bx, lam)


def _rnn_bwd(rx_rg, y, dz, cw, cb, wa, ba, wx, bx, lam, name, tt=512):
    t = rx_rg.shape[0]
    tt = min(tt, t)
    nt = t // tt
    w = GROUP_W

    def body(rx_ref, rg_ref, rxt_ref, y_ref, yt_ref, dz_ref, cw_ref, cb_ref, wa_ref, ba_ref, wx_ref, bx_ref, lam_ref,
             drx_ref, drg_ref, st_ref, gda_ref, gdx_ref, ext_ref, dcx_ref, wcar_ref, acar_ref, dwa_ref, dwx_ref):
        ii = pl.program_id(1)

        @pl.when(ii == 0)
        def _():
            wcar_ref[...] = jnp.zeros((8, w), F32)
            acar_ref[...] = jnp.zeros((8, w), F32)
            dcx_ref[tt:tt + 8, :] = jnp.zeros((8, w), F32)
            st_ref[...] = jnp.zeros_like(st_ref)
            dwa_ref[...] = jnp.zeros_like(dwa_ref)
            dwx_ref[...] = jnp.zeros_like(dwx_ref)

        has_prev = jnp.where(ii == nt - 1, 0.0, 1.0)
        x = rx_ref[...]
        ext_ref[0:8, :] = rxt_ref[...] * has_prev
        ext_ref[8:8 + tt, :] = x
        cwv = cw_ref[...]
        c, x2, x1, x0 = _conv_taps(cwv, cb_ref[...], x, ext_ref, tt)
        lam = lam_ref[...]
        sp = _softplus(-lam)
        wa = wa_ref[...]
        wx = wx_ref[...]
        cb16, r, ig, a, mult = _lru_gates(c, wa, ba_ref[...], wx, bx_ref[...], sp)

        rg = rg_ref[...]
        sg = _sigmoid(rg)
        dz = dz_ref[...]
        yv = y_ref[...]
        drg_ref[...] = (dz * yv * (sg * (1.0 + rg * (1.0 - sg)))).astype(BF16)

        row = lax.broadcasted_iota(jnp.int32, (tt, w), 0)
        b = jnp.where(row < tt - 1, pltpu.roll(a, tt - 1, 0), acar_ref[0:1, :])
        dh = _scan_rev(b, dz * (rg * sg), wcar_ref[0:1, :], tt)
        wcar_ref[...] = dh[0:8, :]
        acar_ref[...] = a[0:8, :]

        hprev = jnp.where(row >= 1, pltpu.roll(yv, 1, 0), yt_ref[7:8, :] * has_prev)
        dmult = dh * (ig * c)
        dig = dh * mult * c
        dlog_a = dh * hprev * a - dmult * (a * a / mult)
        dpa = dlog_a * ((-LRU_C) * sp) * r * (1.0 - r)
        dpx = dig * ig * (1.0 - ig)
        dsp = jnp.sum(dlog_a * r, axis=0, keepdims=True) * (-LRU_C)
        dlam = dsp * (-_sigmoid(-lam))
        dpa16 = dpa.astype(BF16)
        dpx16 = dpx.astype(BF16)
        dwa_ref[...] += lax.dot_general(cb16, dpa16, TN_DIMS, preferred_element_type=F32)
        dwx_ref[...] += lax.dot_general(cb16, dpx16, TN_DIMS, preferred_element_type=F32)
        dc = (dh * mult * ig
              + lax.dot_general(dpa16, wa, NT_DIMS, preferred_element_type=F32)
              + lax.dot_general(dpx16, wx, NT_DIMS, preferred_element_type=F32))

        dcx_ref[0:tt, :] = dc
        drx = (cwv[3:4] * dc + cwv[2:3] * dcx_ref[1:1 + tt, :] + cwv[1:2] * dcx_ref[2:2 + tt, :]
               + cwv[0:1] * dcx_ref[3:3 + tt, :])
        drx_ref[...] = drx.astype(BF16)
        dcx_ref[tt:tt + 8, :] = dc[0:8, :]

        def colsum(v):
            return jnp.sum(v, axis=0, keepdims=True)

        st_ref[...] += _rows8([colsum(dc), colsum(dpa), colsum(dpx), dlam,
                               colsum(dc * x0), colsum(dc * x1), colsum(dc * x2), colsum(dc * x)], w)

        @pl.when(ii == nt - 1)
        def _():
            for blk in range(GROUP_W // RNN_BLOCK_W):
                rows = slice(blk * RNN_BLOCK_W, (blk + 1) * RNN_BLOCK_W)
                gda_ref[blk] = dwa_ref[rows, rows]
                gdx_ref[blk] = dwx_ref[rows, rows]

    def rev(ii):
        return nt - 1 - ii

    def tail(g, ii):
        return (jnp.maximum(rev(ii) * (tt // 8) - 1, 0), g)

    vec = pl.BlockSpec((1, w), lambda g, ii: (0, g))
    mat = pl.BlockSpec((None, w, w), lambda g, ii: (g, 0, 0))
    tile = pl.BlockSpec((tt, w), lambda g, ii: (rev(ii), g))
    diag_shape = (N_GROUPS, GROUP_W // RNN_BLOCK_W, RNN_BLOCK_W, RNN_BLOCK_W)
    diag = pl.BlockSpec((None,) + diag_shape[1:], lambda g, ii: (g, 0, 0, 0))
    return pl.pallas_call(
        body, name=name, grid=(N_GROUPS, nt),
        in_specs=[tile, pl.BlockSpec((tt, w), lambda g, ii: (rev(ii), N_GROUPS + g)), pl.BlockSpec((8, w), tail),
                  tile, pl.BlockSpec((8, w), tail), tile,
                  pl.BlockSpec((4, w), lambda g, ii: (0, g)), vec, mat, vec, mat, vec, vec],
        out_specs=[tile, tile, pl.BlockSpec((8, w), lambda g, ii: (0, g)), diag, diag],
        out_shape=[jax.ShapeDtypeStruct((t, D_RNN), BF16), jax.ShapeDtypeStruct((t, D_RNN), BF16),
                   jax.ShapeDtypeStruct((8, D_RNN), F32),
                   jax.ShapeDtypeStruct(diag_shape, F32), jax.ShapeDtypeStruct(diag_shape, F32)],
        scratch_shapes=[pltpu.VMEM((tt + 8, w), F32), pltpu.VMEM((tt + 8, w), F32), pltpu.VMEM((8, w), F32),
                        pltpu.VMEM((8, w), F32), pltpu.VMEM((w, w), F32), pltpu.VMEM((w, w), F32)],
        compiler_params=_params("parallel", "arbitrary"))(rx_rg, rx_rg, rx_rg, y, y, dz, cw, cb, wa, ba, wx, bx, lam)


def _half_mask(shape, half):
    lane = lax.broadcasted_iota(jnp.int32, shape, 1)
    return (lane >= HEAD_DIM) if half else (lane < HEAD_DIM)


def _dup_half(t, half):
    sel = jnp.where(_half_mask(t.shape, half), t, 0.0)
    return sel + pltpu.roll(sel, HEAD_DIM, 1)


def _band_geometry(n):
    qi = lax.broadcasted_iota(jnp.int32, (BLOCK, 2 * BLOCK), 0)
    kj = lax.broadcasted_iota(jnp.int32, (BLOCK, 2 * BLOCK), 1)
    dist = BLOCK + qi - kj
    first_key = jnp.where(n > 0, 0, BLOCK)
    valid = (dist >= 0) & (dist < BLOCK) & (kj >= first_key)
    return dist.astype(F32), valid


GROUP = 4
STACK = 1


def _kv_dup(prev_ref, cur_ref, hk, scale=1.0):
    tile = hk // 2
    kt = jnp.concatenate([prev_ref[:, tile * LANE:(tile + 1) * LANE], cur_ref[:, tile * LANE:(tile + 1) * LANE]], axis=0)
    return (_dup_half(kt.astype(F32), hk % 2) * scale).astype(BF16)


def _chains():
    return [list(range(c * STACK, (c + 1) * STACK)) for c in range(GROUP // STACK)]


def _stack(tiles, rs, zero):
    rows = []
    for r in rs:
        t = tiles[r // 2]
        low = _half_mask(t.shape, 0)
        rows.append(jnp.where(low, t, zero) if r % 2 == 0 else jnp.where(low, zero, t))
    return rows[0] if len(rows) == 1 else jnp.concatenate(rows, axis=0)


def _tiles_from_heads(per_head):
    low = _half_mask((BLOCK, LANE), 0)
    return jnp.where(low, per_head[0], per_head[1]), jnp.where(low, per_head[2], per_head[3])


def _fill_bias(bias_ref, sm_ref, n):
    distf, valid = _band_geometry(n)
    for head in range(N_Q_HEADS):
        bias_ref[head] = jnp.where(valid, -sm_ref[1, head] * distf, MASKED)


def _chain_consts(bias_ref, sm_ref, hk, rs):
    heads = [GROUP * hk + r for r in rs]
    if len(heads) == 1:
        return bias_ref[heads[0]], sm_ref[0, heads[0]]
    bias = jnp.concatenate([bias_ref[h] for h in heads], axis=0)
    sink = jnp.concatenate([jnp.full((BLOCK, 1), sm_ref[0, h], F32) for h in heads], axis=0)
    return bias, sink


def _chain_probs(qs, kdup, bias, sink):
    s = lax.dot_general(qs, kdup, NT_DIMS, preferred_element_type=F32) + bias
    m = jnp.maximum(jnp.max(s, axis=1, keepdims=True), sink)
    e = jnp.exp(s - m)
    es = jnp.exp(sink - m)
    inv = 1.0 / (jnp.sum(e, axis=1, keepdims=True) + es)
    return e * inv, es * inv


def _attn_specs(nb, clamp_last):
    def blk(n):
        return jnp.minimum(n, nb - 1) if clamp_last else n

    q_spec = pl.BlockSpec((BLOCK, D_MODEL), lambda n: (blk(n), 0))
    k_prev = pl.BlockSpec((BLOCK, D_KV), lambda n: (jnp.maximum(blk(n) - 1, 0), D_MODEL // D_KV))
    k_cur = pl.BlockSpec((BLOCK, D_KV), lambda n: (blk(n), D_MODEL // D_KV))
    v_prev = pl.BlockSpec((BLOCK, D_KV), lambda n: (jnp.maximum(blk(n) - 1, 0), D_MODEL // D_KV + 1))
    v_cur = pl.BlockSpec((BLOCK, D_KV), lambda n: (blk(n), D_MODEL // D_KV + 1))
    return q_spec, k_prev, k_cur, v_prev, v_cur


def _attn_fwd(sm, qkv, ag_ml, name):
    t = qkv.shape[0]
    nb = t // BLOCK

    def body(sm_ref, q_ref, kp_ref, kc_ref, vp_ref, vc_ref, ag_ref, y_ref, z_ref, bias_ref):
        n = pl.program_id(0)

        @pl.when(n <= 1)
        def _():
            _fill_bias(bias_ref, sm_ref, n)

        for hk in range(N_Q_HEADS // GROUP):
            kdup = _kv_dup(kp_ref, kc_ref, hk)
            vdup = _kv_dup(vp_ref, vc_ref, hk)
            cols = [slice((2 * hk + k) * LANE, (2 * hk + k + 1) * LANE) for k in (0, 1)]
            q_tiles = [q_ref[:, c] * ATTN_SCALE for c in cols]
            per_head = {}
            for rs in _chains():
                qs = _stack(q_tiles, rs, jnp.zeros_like(q_tiles[0]))
                probs, _ = _chain_probs(qs, kdup, *_chain_consts(bias_ref, sm_ref, hk, rs))
                o = jnp.dot(probs.astype(BF16), vdup, preferred_element_type=F32)
                for k, r in enumerate(rs):
                    per_head[r] = o[k * BLOCK:(k + 1) * BLOCK]
            for c, yt in zip(cols, _tiles_from_heads(per_head)):
                y_ref[:, c] = yt
                ag = ag_ref[:, c]
                z_ref[:, c] = (yt * ag * _sigmoid(ag)).astype(BF16)

    q_spec, k_prev, k_cur, v_prev, v_cur = _attn_specs(nb, False)
    wide = pl.BlockSpec((BLOCK, D_MODEL), lambda n: (n, 0))
    return pl.pallas_call(
        body, name=name, grid=(nb,),
        in_specs=[pl.BlockSpec(memory_space=pltpu.SMEM), q_spec, k_prev, k_cur, v_prev, v_cur, wide],
        out_specs=[wide, wide],
        out_shape=[jax.ShapeDtypeStruct((t, D_MODEL), F32), jax.ShapeDtypeStruct((t, D_MODEL), BF16)],
        scratch_shapes=[pltpu.VMEM((N_Q_HEADS, BLOCK, 2 * BLOCK), F32)],
        compiler_params=_params("arbitrary"))(sm, qkv, qkv, qkv, qkv, qkv, ag_ml)


def _attn_bwd(sm, qkv, ag_ml, y, dz, name, ride=None):
    t = qkv.shape[0]
    nb = t // BLOCK
    host = _Hosted(ride, 9, 4, 3)

    def body(*refs):
        host_refs, start, finish = host.split(refs)
        (sm_ref, q_ref, kp_ref, kc_ref, vp_ref, vc_ref, ag_ref, y_ref, dz_ref,
         dq_ref, dkv_ref, dag_ref, ds_ref, ck_ref, cv_ref, bias_ref) = host_refs
        n = pl.program_id(0)
        start(n == 0)

        @pl.when(n == 0)
        def _():
            ck_ref[...] = jnp.zeros_like(ck_ref)
            cv_ref[...] = jnp.zeros_like(cv_ref)
            ds_ref[...] = jnp.zeros_like(ds_ref)

        @pl.when(n <= 1)
        def _():
            _fill_bias(bias_ref, sm_ref, n)

        @pl.when(n < nb)
        def _():
            lane8 = lax.broadcasted_iota(jnp.int32, (8, LANE), 1)
            row8 = lax.broadcasted_iota(jnp.int32, (8, LANE), 0)
            dsink = jnp.zeros((8, LANE), F32)
            dk_heads, dv_heads = [], []
            for hk in range(N_Q_HEADS // GROUP):
                kdup = _kv_dup(kp_ref, kc_ref, hk)
                ks = _kv_dup(kp_ref, kc_ref, hk, ATTN_SCALE)
                vdup = _kv_dup(vp_ref, vc_ref, hk)
                cols = [slice((2 * hk + k) * LANE, (2 * hk + k + 1) * LANE) for k in (0, 1)]
                dy_tiles, y_tiles = [], []
                for c in cols:
                    ag = ag_ref[:, c]
                    sg = _sigmoid(ag)
                    dzt = dz_ref[:, c]
                    yt = y_ref[:, c]
                    dag_ref[:, c] = (dzt * yt * (sg * (1.0 + ag * (1.0 - sg)))).astype(BF16)
                    dy_tiles.append(dzt * (ag * sg))
                    y_tiles.append(yt)
                q_tiles = [q_ref[:, c] * ATTN_SCALE for c in cols]
                dq_head = {}
                dk_acc = jnp.zeros((2 * BLOCK, LANE), F32)
                dv_acc = jnp.zeros((2 * BLOCK, LANE), F32)
                for rs in _chains():
                    qs = _stack(q_tiles, rs, jnp.zeros_like(q_tiles[0]))
                    probs, psink = _chain_probs(qs, kdup, *_chain_consts(bias_ref, sm_ref, hk, rs))
                    do = _stack(dy_tiles, rs, 0.0)
                    ycat = y_tiles[rs[0] // 2] if len(rs) == 1 else jnp.concatenate([y_tiles[r // 2] for r in rs], axis=0)
                    delta = jnp.sum(do * ycat, axis=1, keepdims=True)
                    do16 = do.astype(BF16)
                    dp = lax.dot_general(do16, vdup, NT_DIMS, preferred_element_type=F32)
                    ds16 = (probs * (dp - delta)).astype(BF16)
                    psd = psink * delta
                    dq = jnp.dot(ds16, ks, preferred_element_type=F32)
                    for k, r in enumerate(rs):
                        dq_head[r] = dq[k * BLOCK:(k + 1) * BLOCK]
                        dsink = dsink + jnp.where((row8 == 0) & (lane8 == GROUP * hk + r),
                                                  -jnp.sum(psd[k * BLOCK:(k + 1) * BLOCK], axis=0, keepdims=True), 0.0)
                    dk_acc = dk_acc + lax.dot_general(ds16, qs, TN_DIMS, preferred_element_type=F32)
                    dv_acc = dv_acc + lax.dot_general(probs.astype(BF16), do16, TN_DIMS, preferred_element_type=F32)
                for c, dqt in zip(cols, _tiles_from_heads(dq_head)):
                    dq_ref[:, c] = dqt.astype(BF16)
                dk_heads.append(dk_acc + pltpu.roll(dk_acc, HEAD_DIM, 1))
                dv_heads.append(dv_acc + pltpu.roll(dv_acc, HEAD_DIM, 1))
            ds_ref[...] += dsink
            low = _half_mask((2 * BLOCK, LANE), 0)
            for tile in range(2):
                cols = slice(tile * LANE, (tile + 1) * LANE)
                dkt = jnp.where(low, dk_heads[2 * tile], dk_heads[2 * tile + 1])
                dvt = jnp.where(low, dv_heads[2 * tile], dv_heads[2 * tile + 1])
                dkv_ref[:, cols] = (ck_ref[:, cols] + dkt[0:BLOCK, :]).astype(BF16)
                dkv_ref[:, D_KV + tile * LANE:D_KV + (tile + 1) * LANE] = (cv_ref[:, cols] + dvt[0:BLOCK, :]).astype(BF16)
                ck_ref[:, cols] = dkt[BLOCK:2 * BLOCK, :]
                cv_ref[:, cols] = dvt[BLOCK:2 * BLOCK, :]

        @pl.when(n == nb)
        def _():
            dkv_ref[:, 0:D_KV] = ck_ref[...].astype(BF16)
            dkv_ref[:, D_KV:2 * D_KV] = cv_ref[...].astype(BF16)

        finish(n == nb)

    q_spec, k_prev, k_cur, v_prev, v_cur = _attn_specs(nb, True)
    wide = pl.BlockSpec((BLOCK, D_MODEL), lambda n: (jnp.minimum(n, nb - 1), 0))
    outs = pl.pallas_call(
        body, name=name, grid=(nb + 1,),
        in_specs=[pl.BlockSpec(memory_space=pltpu.SMEM), q_spec, k_prev, k_cur, v_prev, v_cur, wide, wide, wide] + host.in_specs,
        out_specs=[wide, pl.BlockSpec((BLOCK, 2 * D_KV), lambda n: (jnp.maximum(n - 1, 0), 0)), wide,
                   pl.BlockSpec((8, LANE), lambda n: (0, 0))] + host.out_specs,
        out_shape=[jax.ShapeDtypeStruct((t, D_MODEL), BF16), jax.ShapeDtypeStruct((t, 2 * D_KV), BF16),
                   jax.ShapeDtypeStruct((t, D_MODEL), BF16), jax.ShapeDtypeStruct((8, LANE), F32)] + host.out_shapes,
        scratch_shapes=[pltpu.VMEM((BLOCK, D_KV), F32), pltpu.VMEM((BLOCK, D_KV), F32),
                        pltpu.VMEM((N_Q_HEADS, BLOCK, 2 * BLOCK), F32)] + host.scratch_shapes,
        compiler_params=_params("arbitrary"))(sm, qkv, qkv, qkv, qkv, qkv, ag_ml, y, dz, *host.arrays)
    res, landed = host.results(outs, 4)
    return (*res, landed) if ride else tuple(res)


def _local_grads(x, target, p, late_weights=None, reduce_out=None, reduce_in=None):
    wt = p["wt"]
    h = _prenorm(x, p["pre_g"], "prenorm")
    rx_rg = _mm_nt(h, wt, 0, 2 * D_RNN, F32, "proj_rnn")
    qkv = _mm_nt(h, wt, 2 * D_RNN, D_MODEL + 2 * D_KV, BF16, "proj_qkv")
    gates_off = 2 * D_RNN + D_MODEL + 2 * D_KV
    if late_weights is None:
        ag_ml = _mm_nt(h, wt, gates_off, 3 * D_MODEL, F32, "proj_gates")
    else:
        ag_ml, landed = _mm_nt(h, wt, gates_off, 3 * D_MODEL, F32, "proj_gates", ride=late_weights[0])
        p = {**p, **late_weights[1](landed)}
    lru =(p["cw"], p["cb"], p["wbd_a"], p["b_a"], p["wbd_x"], p["b_x"], p["lam"])
    y_rnn, z_rnn = _rnn_fwd(rx_rg, *lru, "rnn_fwd")
    y_attn, z_attn = _attn_fwd(p["sm"], qkv, ag_ml, "attn_fwd")
    br_rnn, br_attn, merged, dout, dy, st_post = _branches_fwd(
        z_rnn, z_attn, ag_ml, p["b_gate"], p["w_rnn"], p["w_attn"], p["w_out"], x, target, p["post_g"], "branches_fwd")

    dbr_rnn, dbr_attn, d_ml, dz_rnn, dz_attn, st_merge = _branches_bwd(
        dout, br_rnn, br_attn, ag_ml, p["b_gate"], p["w_rnn"], p["w_attn"], p["w_out"], "branches_bwd")
    gw_out = _mm_tn(merged, dout, "gw_out")
    gw_rnn = _mm_tn(z_rnn, dbr_rnn, "gw_rnn")
    gw_attn = _mm_tn(z_attn, dbr_attn, "gw_attn")
    d_rx, d_rg, st_rnn, g_rg_a, g_rg_x = _rnn_bwd(rx_rg, y_rnn, dz_rnn, *lru, "rnn_bwd")
    red_out = red_in = None
    if reduce_out is None:
        dq, dkv, d_ag, st_sink = _attn_bwd(p["sm"], qkv, ag_ml, y_attn, dz_attn, "attn_bwd")
    else:
        dq, dkv, d_ag, st_sink, red_out = _attn_bwd(p["sm"], qkv, ag_ml, y_attn, dz_attn, "attn_bwd",
                                                    ride=reduce_out(gw_rnn, gw_attn, gw_out, g_rg_a, g_rg_x))

    segs = [d_rx, d_rg, dq, dkv, d_ag, d_ml]
    gwt = _mm_tn_seg(segs, h, "gw_in")
    if reduce_in is None:
        dh = _mm_nn_seg(segs, wt, "d_h")
    else:
        dh, red_in = _mm_nn_seg(segs, wt, "d_h", ride=reduce_in(gwt))
    grad_x, st_pre = _prenorm_bwd(x, p["pre_g"], dh, dy, "prenorm_bwd")
    return dict(grad_x=grad_x, gwt=gwt, gw_rnn=gw_rnn, gw_attn=gw_attn, gw_out=gw_out,
                st_post=st_post, st_merge=st_merge, st_rnn=st_rnn, st_sink=st_sink, st_pre=st_pre,
                g_rg_a=g_rg_a, g_rg_x=g_rg_x, red_out=red_out, red_in=red_in)


def _place():
    x, y, c = lax.axis_index("x"), lax.axis_index("y"), lax.axis_index("c")
    return x, y, c


def _gather_ride(shards):
    n = len(shards)

    def copies(ins, outs, sems):
        send_sems, recv_sems, local_sems = sems
        x, y, c = _place()
        me, sibling = (x, y, c), (x, y, 1 - c)
        chips = [(1 - x, y), (x, 1 - y), (1 - x, 1 - y)]

        def slot(a, dev):
            return outs[a].at[4 * dev[0] + 2 * dev[1] + dev[2]]

        def copy(a, k, block, to, src=None):
            return pltpu.make_async_remote_copy(
                src_ref=slot(a, block) if src is None else src, dst_ref=slot(a, block),
                send_sem=send_sems.at[a, k], recv_sem=recv_sems.at[a, k], device_id=to, device_id_type=MESH)

        mine = [pltpu.make_async_copy(ins[a], slot(a, me), local_sems.at[a]) for a in range(n)]
        first = []
        for a in range(n):
            first.append(copy(a, 0, me, sibling, src=ins[a]))
            first += [copy(a, 1 + j, me, (*chip, c), src=ins[a]) for j, chip in enumerate(chips)]
        return me, sibling, chips, c, copy, mine, first

    def start(ins, outs, sems):
        *_, mine, first = copies(ins, outs, sems)
        for cp in mine + first:
            cp.start()

    def finish(ins, outs, sems):
        me, sibling, chips, c, copy, mine, first = copies(ins, outs, sems)
        passed = []
        for j, chip in enumerate(chips):
            for a in range(n):
                copy(a, 1 + j, (*chip, c), me).wait_recv()
                cp = copy(a, 4 + j, (*chip, c), sibling)
                cp.start()
                passed.append(cp)
        for a in range(n):
            copy(a, 0, sibling, me).wait_recv()
            for j, chip in enumerate(chips):
                copy(a, 4 + j, (*chip, 1 - c), me).wait_recv()
        for cp in first + passed:
            cp.wait_send()
        for cp in mine:
            cp.wait()

    return _Ride(
        shards, [jax.ShapeDtypeStruct((N_DEV, *s.shape), s.dtype) for s in shards],
        [pltpu.SemaphoreType.DMA((n, 7)), pltpu.SemaphoreType.DMA((n, 7)), pltpu.SemaphoreType.DMA((n,))],
        start, finish)


def _exchange_sibling(scatter, whole, name):
    ns, nw = len(scatter), len(whole)

    def body(*refs):
        s_in, w_in = refs[:ns], refs[ns:ns + nw]
        s_out, w_out = refs[ns + nw:2 * ns + nw], refs[2 * ns + nw:2 * (ns + nw)]
        send_sems, recv_sems = refs[2 * (ns + nw):]
        x, y, c = _place()
        sibling = (x, y, 1 - c)
        copies = []
        for a in range(ns):
            for chip in range(N_CHIP):
                copies.append(pltpu.make_async_remote_copy(
                    src_ref=s_in[a].at[2 * chip + (1 - c)], dst_ref=s_out[a].at[chip],
                    send_sem=send_sems.at[a * N_CHIP + chip], recv_sem=recv_sems.at[a * N_CHIP + chip],
                    device_id=sibling, device_id_type=MESH))
        for a in range(nw):
            copies.append(pltpu.make_async_remote_copy(
                src_ref=w_in[a], dst_ref=w_out[a],
                send_sem=send_sems.at[ns * N_CHIP + a], recv_sem=recv_sems.at[ns * N_CHIP + a],
                device_id=sibling, device_id_type=MESH))
        for cp in copies:
            cp.start()
        for cp in copies:
            cp.wait()

    n_sem = ns * N_CHIP + nw
    return pl.pallas_call(
        body, name=name,
        in_specs=[ANY] * (ns + nw), out_specs=[ANY] * (ns + nw),
        out_shape=([jax.ShapeDtypeStruct((N_CHIP, *s.shape[1:]), s.dtype) for s in scatter]
                   + [jax.ShapeDtypeStruct(s.shape, s.dtype) for s in whole]),
        scratch_shapes=[pltpu.SemaphoreType.DMA((n_sem,)), pltpu.SemaphoreType.DMA((n_sem,))],
    )(*scatter, *whole)


def _chips_ride(scatter, whole):
    ns, nw = len(scatter), len(whole)
    n = ns + nw

    def copies(ins, outs, sems):
        send_sems, recv_sems, local_sems = sems
        x, y, c = _place()
        own = 2 * x + y
        chips = [(1 - x, y), (x, 1 - y), (1 - x, 1 - y)]

        def src(a, chip_idx):
            return ins[a].at[chip_idx] if a < ns else ins[a]

        local = [pltpu.make_async_copy(src(a, own), outs[a].at[own], local_sems.at[a]) for a in range(n)]
        sent = [pltpu.make_async_remote_copy(
            src_ref=src(a, 2 * chip[0] + chip[1]), dst_ref=outs[a].at[own],
            send_sem=send_sems.at[a, j], recv_sem=recv_sems.at[a, own], device_id=(*chip, c), device_id_type=MESH)
            for a in range(n) for j, chip in enumerate(chips)]
        return chips, c, local, sent

    def start(ins, outs, sems):
        _, _, local, sent = copies(ins, outs, sems)
        for cp in local + sent:
            cp.start()

    def finish(ins, outs, sems):
        send_sems, recv_sems, _ = sems
        chips, c, local, sent = copies(ins, outs, sems)
        for a in range(n):
            for chip in chips:
                k = 2 * chip[0] + chip[1]
                pltpu.make_async_remote_copy(
                    src_ref=outs[a].at[k], dst_ref=outs[a].at[k], send_sem=send_sems.at[a, 0],
                    recv_sem=recv_sems.at[a, k], device_id=(*chip, c), device_id_type=MESH).wait_recv()
        for cp in sent:
            cp.wait_send()
        for cp in local:
            cp.wait()

    return _Ride(
        list(scatter) + list(whole),
        [jax.ShapeDtypeStruct(s.shape, s.dtype) for s in scatter]
        + [jax.ShapeDtypeStruct((N_CHIP, *s.shape), s.dtype) for s in whole],
        [pltpu.SemaphoreType.DMA((n, 3)), pltpu.SemaphoreType.DMA((n, N_CHIP)), pltpu.SemaphoreType.DMA((n,))],
        start, finish)


def _pair_sum_scatter(parts, recvs, core, name):
    na = len(parts)
    _, r, cdim = parts[0].shape
    tr = min(r, 416 if r % 416 == 0 else 128)

    def body(core_ref, *refs):
        del core_ref
        for a in range(na):
            refs[2 * na + a][...] = (refs[a][...] + refs[na + a][...]).astype(BF16)

    blk = (None, tr, cdim)
    mine = pl.BlockSpec(blk, lambda k, i, core_ref: (2 * k + core_ref[0], i, 0))
    slot = pl.BlockSpec(blk, lambda k, i, core_ref: (k, i, 0))
    return pl.pallas_call(
        body, name=name,
        grid_spec=pltpu.PrefetchScalarGridSpec(
            num_scalar_prefetch=1, grid=(N_CHIP, r // tr),
            in_specs=[mine] * na + [slot] * na, out_specs=[slot] * na),
        out_shape=[jax.ShapeDtypeStruct((N_CHIP, r, cdim), BF16)] * na,
        compiler_params=_params("parallel", "parallel"))(core, *parts, *recvs)


def _pair_sum_whole(mine, recvs, name):
    na = len(mine)

    def body(*refs):
        for a in range(na):
            refs[2 * na + a][...] = refs[a][...] + refs[na + a][...]

    return pl.pallas_call(body, name=name, out_shape=[jax.ShapeDtypeStruct(m.shape, F32) for m in mine])(*mine, *recvs)


def _allreduce_small(pack, name):
    shape = pack.shape

    def body(x_ref, o_ref, sib_ref, chip_ref, send_sems, recv_sems):
        x, y, c = _place()
        own = 2 * x + y
        chips = [(1 - x, y), (x, 1 - y), (1 - x, 1 - y)]
        to_sibling = pltpu.make_async_remote_copy(
            src_ref=x_ref, dst_ref=sib_ref, send_sem=send_sems.at[0], recv_sem=recv_sems.at[0],
            device_id=(x, y, 1 - c), device_id_type=MESH)
        to_sibling.start()
        to_sibling.wait()
        chip_ref[own] = x_ref[...] + sib_ref[...]
        sent = [pltpu.make_async_remote_copy(
            src_ref=chip_ref.at[own], dst_ref=chip_ref.at[own], send_sem=send_sems.at[1 + j],
            recv_sem=recv_sems.at[1 + own], device_id=(*chip, c), device_id_type=MESH) for j, chip in enumerate(chips)]
        for cp in sent:
            cp.start()
        for chip in chips:
            k = 2 * chip[0] + chip[1]
            pltpu.make_async_remote_copy(
                src_ref=chip_ref.at[k], dst_ref=chip_ref.at[k], send_sem=send_sems.at[1],
                recv_sem=recv_sems.at[1 + k], device_id=(*chip, c), device_id_type=MESH).wait_recv()
        for cp in sent:
            cp.wait_send()
        o_ref[...] = (chip_ref[0] + chip_ref[1]) + (chip_ref[2] + chip_ref[3])

    return pl.pallas_call(
        body, name=name, out_shape=jax.ShapeDtypeStruct(shape, F32),
        in_specs=[pl.BlockSpec(memory_space=pltpu.VMEM)], out_specs=pl.BlockSpec(memory_space=pltpu.VMEM),
        scratch_shapes=[pltpu.VMEM(shape, F32), pltpu.VMEM((N_CHIP, *shape), F32),
                        pltpu.SemaphoreType.DMA((4,)), pltpu.SemaphoreType.DMA((1 + N_CHIP,))],
    )(pack)


def _adamw(g, w, m, v):
    m = ADAM_B1 * m + (1.0 - ADAM_B1) * g
    v = ADAM_B2 * v + (1.0 - ADAM_B2) * (g * g)
    m_hat = m / (1.0 - ADAM_B1 ** ADAM_STEP)
    v_hat = v / (1.0 - ADAM_B2 ** ADAM_STEP)
    delta = -ADAM_LR * (m_hat / (jnp.sqrt(v_hat) + ADAM_EPS) + ADAM_WD * w)
    return delta, m, v


def _adam_parts(parts, w, m, v, name, tr=None):
    npart, r, c = parts.shape
    tr = r if tr is None else min(tr, r)

    def body(p_ref, w_ref, m_ref, v_ref, g_ref, d_ref, nm_ref, nv_ref):
        g = p_ref[0].astype(F32)
        for k in range(1, npart):
            g = g + p_ref[k].astype(F32)
        g_ref[...] = g
        d_ref[...], nm_ref[...], nv_ref[...] = _adamw(g, w_ref[...], m_ref[...], v_ref[...])

    tile = pl.BlockSpec((tr, c), lambda i: (i, 0))
    return pl.pallas_call(
        body, name=name, grid=(r // tr,),
        in_specs=[pl.BlockSpec((npart, tr, c), lambda i: (0, i, 0)), tile, tile, tile],
        out_specs=[tile] * 4, out_shape=[jax.ShapeDtypeStruct((r, c), F32)] * 4,
        compiler_params=_params("parallel"))(parts, w, m, v)


def _adam_transposed(parts, w, m, v, name, tc=256):
    npart, c, r = parts.shape

    def body(p_ref, w_ref, m_ref, v_ref, g_ref, d_ref, nm_ref, nv_ref):
        gt = p_ref[0].astype(F32)
        for k in range(1, npart):
            gt = gt + p_ref[k].astype(F32)
        g = gt.T
        g_ref[...] = g
        d_ref[...], nm_ref[...], nv_ref[...] = _adamw(g, w_ref[...], m_ref[...], v_ref[...])

    tile = pl.BlockSpec((tc, c), lambda i: (i, 0))
    return pl.pallas_call(
        body, name=name, grid=(r // tc,),
        in_specs=[pl.BlockSpec((npart, c, tc), lambda i: (0, 0, i)), tile, tile, tile],
        out_specs=[tile] * 4, out_shape=[jax.ShapeDtypeStruct((r, c), F32)] * 4,
        compiler_params=_params("parallel"))(parts, w, m, v)


def _transpose_bf16(w, name, tr=256):
    r, c = w.shape

    def body(w_ref, o_ref):
        o_ref[...] = w_ref[...].T.astype(BF16)

    return pl.pallas_call(
        body, name=name, grid=(r // tr,),
        in_specs=[pl.BlockSpec((tr, c), lambda i: (i, 0))],
        out_specs=pl.BlockSpec((c, tr), lambda i: (0, i)),
        out_shape=jax.ShapeDtypeStruct((c, r), BF16),
        compiler_params=_params("parallel"))(w)


def _block_diag(w):
    w4 = w.reshape(N_GROUPS, 4, RNN_BLOCK_W, RNN_BLOCK_W)
    eye = jnp.eye(4, dtype=w.dtype)
    return jnp.einsum("gbij,bc->gbicj", w4, eye).reshape(N_GROUPS, GROUP_W, GROUP_W).astype(BF16)


SMALL_ROWS = 16
ROW_PRE_G, ROW_BGATE, ROW_CONV_B, ROW_B_A, ROW_B_X, ROW_LAM, ROW_POST_G, ROW_LOSS, ROW_SINKS, ROW_CONV_W = 0, 1, 3, 4, 5, 6, 7, 8, 9, 10


def _pack_stats(st_pre, st_merge, st_rnn, st_post, st_sink, name):
    d = D_MODEL

    def body(pre_ref, mg_ref, rnn_ref, post_ref, sink_ref, o_ref):
        rnn = rnn_ref[...]
        sinks = jnp.concatenate([sink_ref[0:1, :], jnp.zeros((1, d - LANE), F32)], axis=1)
        o_ref[0:8, :] = _rows8([pre_ref[0:1, :], mg_ref[0:1, :], mg_ref[1:2, :], rnn[0:1], rnn[1:2], rnn[2:3], rnn[3:4],
                                post_ref[0:1, :]], d)
        o_ref[8:16, :] = _rows8([post_ref[1:2, :], sinks, rnn[4:5], rnn[5:6], rnn[6:7], rnn[7:8]], d)

    return pl.pallas_call(body, name=name, out_shape=jax.ShapeDtypeStruct((SMALL_ROWS, d), F32))(
        st_pre, st_merge, st_rnn, st_post, st_sink)


def _adam_small(total, w, m, v, name):
    d = D_MODEL
    n_in = len(w)

    def pack(refs):
        pre, bg, cbias, ba, bx, lam, post, sinks = [r[...] for r in refs]
        top = _rows8([pre, bg[:, 0:d], bg[:, d:2 * d], cbias, ba, bx, lam, post], d)
        return jnp.concatenate([top, _rows8([jnp.zeros((1, d), F32), sinks], d)], axis=0)

    def body(*refs):
        p_ref = refs[0]
        w_refs, m_refs, v_refs = (refs[1 + k * n_in:1 + (k + 1) * n_in] for k in range(3))
        outs = refs[1 + 3 * n_in:]
        g = p_ref[...]
        res = (g,) + _adamw(g, pack(w_refs), pack(m_refs), pack(v_refs))
        for k in range(4):
            outs[k][...] = res[k]
            outs[4 + k][...] = jnp.concatenate([res[k][ROW_BGATE:ROW_BGATE + 1], res[k][ROW_BGATE + 1:ROW_BGATE + 2]], axis=1)

    return pl.pallas_call(
        body, name=name,
        out_shape=[jax.ShapeDtypeStruct((SMALL_ROWS, d), F32)] * 4 + [jax.ShapeDtypeStruct((1, 2 * d), F32)] * 4,
    )(total, *w, *m, *v)


def kernel(x, pre_norm_g, w_in, b_gate, conv_w, conv_b, w_rg_a, b_rg_a, w_rg_x, b_rg_x, lru_lambda, attn_sinks, w_rnn_out, w_attn_out, w_out, post_norm_g, loss_target, m_pre_norm_g, m_w_in, m_b_gate, m_conv_w, m_conv_b, m_w_rg_a, m_b_rg_a, m_w_rg_x, m_b_rg_x, m_lru_lambda, m_attn_sinks, m_w_rnn_out, m_w_attn_out, m_w_out, m_post_norm_g, v_pre_norm_g, v_w_in, v_b_gate, v_conv_w, v_conv_b, v_w_rg_a, v_b_rg_a, v_w_rg_x, v_b_rg_x, v_lru_lambda, v_attn_sinks, v_w_rnn_out, v_w_attn_out, v_w_out, v_post_norm_g):
    cx, cy, cc = _place()
    dev = 4 * cx + 2 * cy + cc
    core = jnp.reshape(cc, (1,)).astype(jnp.int32)

    wt_shard = _transpose_bf16(w_in[0], "wt_shard")
    (wt_all,) = _run_ride(_gather_ride([wt_shard]), "gather_w_in")
    heads = jnp.arange(1, N_Q_HEADS + 1, dtype=F32)
    slopes = jnp.exp2(-ALIBI_MAX_BIAS * heads / N_Q_HEADS)
    b_a = b_rg_a.reshape(1, D_RNN)
    b_x = b_rg_x.reshape(1, D_RNN)
    p = dict(
        wt=wt_all.reshape(D_IN, D_MODEL),
        pre_g=pre_norm_g, post_g=post_norm_g, b_gate=b_gate, cb=conv_b,
        wbd_a=_block_diag(w_rg_a[0]), b_a=b_a, wbd_x=_block_diag(w_rg_x[0]), b_x=b_x, lam=lru_lambda,
        sm=jnp.pad(attn_sinks, ((0, 1), (0, 0))) + jnp.pad(slopes[None, :], ((1, 0), (0, 0))))

    def late_unpack(landed):
        w_rnn_all, w_attn_all, w_out_all, cw_all = landed
        return dict(w_rnn=w_rnn_all.reshape(D_RNN, D_MODEL), w_attn=w_attn_all.reshape(D_MODEL, D_MODEL),
                    w_out=w_out_all.reshape(D_MODEL, D_MODEL), cw=jnp.transpose(cw_all, (1, 0, 2)).reshape(4, D_RNN))

    late_weights = (_gather_ride([w_rnn_out[0].astype(BF16), w_attn_out[0].astype(BF16), w_out[0].astype(BF16), conv_w[0]]),
                    late_unpack)
    flat = (RNN_BLOCKS * RNN_BLOCK_W, RNN_BLOCK_W)

    def reduce_out(gw_rnn, gw_attn, gw_out, g_rg_a, g_rg_x):
        scatter = [gw.reshape(N_DEV, SHARD_OUT, D_MODEL) for gw in (gw_rnn, gw_attn, gw_out)]
        whole = [g_rg_a.reshape(flat), g_rg_x.reshape(flat)]
        recv = _exchange_sibling(scatter, whole, "sibling_out")
        return _chips_ride(_pair_sum_scatter(scatter, recv[:3], core, "pair_out"),
                           _pair_sum_whole(whole, recv[3:], "pair_rg"))

    def reduce_in(gwt):
        scatter = [gwt.reshape(N_DEV, SHARD_IN, D_MODEL)]
        recv = _exchange_sibling(scatter, [], "sibling_in")
        return _chips_ride(_pair_sum_scatter(scatter, recv, core, "pair_in"), [])

    g = _local_grads(x[0], loss_target[0], p, late_weights, reduce_out, reduce_in)
    small = _allreduce_small(
        _pack_stats(g["st_pre"], g["st_merge"], g["st_rnn"], g["st_post"], g["st_sink"], "pack_stats"), "allreduce_small")

    out = {}
    red = g["red_out"]
    out["w_in"] = _adam_transposed(g["red_in"][0], w_in[0], m_w_in[0], v_w_in[0], "adam_w_in")
    out["w_rnn_out"] = _adam_parts(red[0], w_rnn_out[0], m_w_rnn_out[0], v_w_rnn_out[0], "adam_w_rnn_out")
    out["w_attn_out"] = _adam_parts(red[1], w_attn_out[0], m_w_attn_out[0], v_w_attn_out[0], "adam_w_attn_out")
    out["w_out"] = _adam_parts(red[2], w_out[0], m_w_out[0], v_w_out[0], "adam_w_out")
    out["w_rg_a"] = _adam_parts(red[3], w_rg_a.reshape(flat), m_w_rg_a.reshape(flat), v_w_rg_a.reshape(flat), "adam_w_rg_a", tr=256)
    out["w_rg_x"] = _adam_parts(red[4], w_rg_x.reshape(flat), m_w_rg_x.reshape(flat), v_w_rg_x.reshape(flat), "adam_w_rg_x", tr=256)

    def rows(pre, bg, cbias, ba, bx, lam, post, sinks):
        return (pre, bg, cbias, ba.reshape(1, D_RNN), bx.reshape(1, D_RNN), lam, post,
                jnp.pad(sinks, ((0, 0), (0, D_MODEL - N_Q_HEADS))))

    small_out = _adam_small(
        small,
        rows(pre_norm_g, b_gate, conv_b, b_rg_a, b_rg_x, lru_lambda, post_norm_g, attn_sinks),
        rows(m_pre_norm_g, m_b_gate, m_conv_b, m_b_rg_a, m_b_rg_x, m_lru_lambda, m_post_norm_g, m_attn_sinks),
        rows(v_pre_norm_g, v_b_gate, v_conv_b, v_b_rg_a, v_b_rg_x, v_lru_lambda, v_post_norm_g, v_attn_sinks),
        "adam_small")
    packed, bgate_out = small_out[:4], small_out[4:]
    g_cw = lax.dynamic_slice(packed[0][ROW_CONV_W:ROW_CONV_W + 4], (0, dev * SHARD_OUT), (4, SHARD_OUT))
    out["conv_w"] = _adam_parts(g_cw[None], conv_w[0], m_conv_w[0], v_conv_w[0], "adam_conv_w")

    def unpack(kind, name):
        if name == "b_gate":
            return bgate_out[kind]
        row = dict(pre_norm_g=ROW_PRE_G, conv_b=ROW_CONV_B, b_rg_a=ROW_B_A, b_rg_x=ROW_B_X, lru_lambda=ROW_LAM,
                   post_norm_g=ROW_POST_G, attn_sinks=ROW_SINKS)[name]
        r = packed[kind][row:row + 1]
        if name == "attn_sinks":
            return r[:, 0:N_Q_HEADS]
        if name in ("b_rg_a", "b_rg_x"):
            return r.reshape(1, RNN_BLOCKS, RNN_BLOCK_W)
        return r

    shapes = dict(w_in=(1, D_MODEL, SHARD_IN), w_rnn_out=(1, SHARD_OUT, D_MODEL), w_attn_out=(1, SHARD_OUT, D_MODEL),
                  w_out=(1, SHARD_OUT, D_MODEL), w_rg_a=(1, RNN_BLOCKS, RNN_BLOCK_W, RNN_BLOCK_W),
                  w_rg_x=(1, RNN_BLOCKS, RNN_BLOCK_W, RNN_BLOCK_W), conv_w=(1, 4, SHARD_OUT))
    weights = ["pre_norm_g", "w_in", "b_gate", "conv_w", "conv_b", "w_rg_a", "b_rg_a", "w_rg_x", "b_rg_x",
               "lru_lambda", "attn_sinks", "w_rnn_out", "w_attn_out", "w_out", "post_norm_g"]
    results = []
    for kind in range(4):
        for name in weights:
            if name in out:
                results.append(out[name][kind].reshape(shapes[name]))
            else:
                results.append(unpack(kind, name))
    loss = 0.5 / D_MODEL * jnp.sum(packed[0][ROW_LOSS])
    return (loss, g["grad_x"][None], *results)
```

```python
import functools

import jax
import jax.numpy as jnp
from jax import lax
from jax.experimental import pallas as pl
from jax.experimental.pallas import tpu as pltpu

F32 = jnp.float32
BF16 = jnp.bfloat16

D_MODEL = 1024
D_RNN = 1024
RNN_BLOCKS = 16
RNN_BLOCK_W = 64
LRU_C = 8.0
N_Q_HEADS = 16
HEAD_DIM = 64
D_KV = 256
BLOCK = 128
ALIBI_MAX_BIAS = 8.0
EPS = 1e-6
D_IN = 6656
N_DEV = 8
N_CHIP = 4
SHARD_IN = D_IN // N_DEV
SHARD_OUT = D_MODEL // N_DEV
ATTN_SCALE = HEAD_DIM ** -0.5
MASKED = -1e30

ADAM_LR = 0.001
ADAM_B1 = 0.9
ADAM_B2 = 0.999
ADAM_EPS = 1e-08
ADAM_WD = 0.01
ADAM_STEP = 10

VMEM_LIMIT_BYTES = 52 * 1024 * 1024
LANE = 128
GROUP_W = 256
N_GROUPS = D_RNN // GROUP_W
SEG_CHUNK = 512

NT_DIMS = (((1,), (1,)), ((), ()))
TN_DIMS = (((0,), (0,)), ((), ()))
MESH = pl.DeviceIdType.MESH
ANY = pl.BlockSpec(memory_space=pl.ANY)


def _params(*semantics):
    return pltpu.CompilerParams(dimension_semantics=semantics, vmem_limit_bytes=VMEM_LIMIT_BYTES)


def _sigmoid(x):
    return 0.5 * jnp.tanh(0.5 * x) + 0.5


def _log1p(e):
    u = 1.0 + e
    den = jnp.where(u == 1.0, 1.0, u - 1.0)
    return jnp.where(u == 1.0, e, jnp.log(u) * (e / den))


def _softplus(z):
    return jnp.maximum(z, 0.0) + _log1p(jnp.exp(-jnp.abs(z)))


def _rows8(rows, width):
    idx = lax.broadcasted_iota(jnp.int32, (8, width), 0)
    out = jnp.zeros((8, width), F32)
    for r, v in enumerate(rows):
        out = jnp.where(idx == r, v, out)
    return out


class _Ride:
    def __init__(self, arrays, out_shapes, scratch_shapes, start, finish):
        self.arrays, self.out_shapes, self.scratch_shapes = list(arrays), list(out_shapes), list(scratch_shapes)
        self.start, self.finish = start, finish


class _Hosted:
    def __init__(self, ride, n_in, n_out, n_scratch=0):
        self.ride = ride
        self.sizes = (n_in, len(ride.arrays) if ride else 0, n_out, len(ride.out_shapes) if ride else 0, n_scratch)
        self.arrays = ride.arrays if ride else []
        self.in_specs = [ANY] * len(self.arrays)
        self.out_shapes = ride.out_shapes if ride else []
        self.out_specs = [ANY] * len(self.out_shapes)
        self.scratch_shapes = ride.scratch_shapes if ride else []

    def split(self, refs):
        n_in, r_in, n_out, r_out, n_scr = self.sizes
        cuts = [0, n_in, n_in + r_in, n_in + r_in + n_out, n_in + r_in + n_out + r_out, n_in + r_in + n_out + r_out + n_scr]
        host_in, ride_in, host_out, ride_out, host_scr = (refs[cuts[k]:cuts[k + 1]] for k in range(5))
        ride_scr = refs[cuts[5]:]

        def start(when):
            if self.ride is not None:
                pl.when(when)(lambda: self.ride.start(ride_in, ride_out, ride_scr))

        def finish(when):
            if self.ride is not None:
                pl.when(when)(lambda: self.ride.finish(ride_in, ride_out, ride_scr))

        return tuple(host_in) + tuple(host_out) + tuple(host_scr), start, finish

    def results(self, outs, n_out):
        outs = list(outs) if isinstance(outs, (list, tuple)) else [outs]
        return outs[:n_out], outs[n_out:]


def _run_ride(ride, name):
    n_in, n_out = len(ride.arrays), len(ride.out_shapes)

    def body(*refs):
        ins, outs, sems = refs[:n_in], refs[n_in:n_in + n_out], refs[n_in + n_out:]
        ride.start(ins, outs, sems)
        ride.finish(ins, outs, sems)

    return pl.pallas_call(
        body, name=name, in_specs=[ANY] * n_in, out_specs=[ANY] * n_out,
        out_shape=ride.out_shapes, scratch_shapes=ride.scratch_shapes)(*ride.arrays)


def _mm_nt(a, b, b_row_off, n, out_dtype, name, tm=1024, tn=512, ride=None):
    m, k = a.shape
    tm = min(tm, m)
    off = b_row_off // tn
    ni, nj = m // tm, n // tn
    host = _Hosted(ride, 2, 1)

    def body(*refs):
        (a_ref, b_ref, o_ref), start, finish = host.split(refs)
        i, j = pl.program_id(0), pl.program_id(1)
        start((i == 0) & (j == 0))
        o_ref[...] = lax.dot_general(a_ref[...], b_ref[...], NT_DIMS, preferred_element_type=F32).astype(o_ref.dtype)
        finish((i == ni - 1) & (j == nj - 1))

    outs = pl.pallas_call(
        body, name=name, grid=(ni, nj),
        in_specs=[pl.BlockSpec((tm, k), lambda i, j: (i, 0)), pl.BlockSpec((tn, k), lambda i, j: (j + off, 0))] + host.in_specs,
        out_specs=[pl.BlockSpec((tm, tn), lambda i, j: (i, j))] + host.out_specs,
        out_shape=[jax.ShapeDtypeStruct((m, n), out_dtype)] + host.out_shapes,
        scratch_shapes=host.scratch_shapes,
        compiler_params=_params("arbitrary", "arbitrary") if ride else _params("parallel", "parallel"))(a, b, *host.arrays)
    (res,), landed = host.results(outs, 1)
    return (res, landed) if ride else res


def _mm_tn(a, b, name, tm=512, tk=4096):
    ktok, m = a.shape
    n = b.shape[1]
    tk = min(tk, ktok)

    def body(a_ref, b_ref, o_ref):
        @pl.when(pl.program_id(1) == 0)
        def _():
            o_ref[...] = jnp.zeros_like(o_ref)

        o_ref[...] += lax.dot_general(a_ref[...], b_ref[...], TN_DIMS, preferred_element_type=F32)

    return pl.pallas_call(
        body, name=name, grid=(m // tm, ktok // tk),
        in_specs=[pl.BlockSpec((tk, tm), lambda i, kk: (kk, i)), pl.BlockSpec((tk, n), lambda i, kk: (kk, 0))],
        out_specs=pl.BlockSpec((tm, n), lambda i, kk: (i, 0)),
        out_shape=jax.ShapeDtypeStruct((m, n), F32),
        compiler_params=_params("parallel", "arbitrary"))(a, b)


def _segment_chunks(segs):
    bounds = [0]
    for s in segs:
        bounds.append(bounds[-1] + s.shape[1] // SEG_CHUNK)
    return bounds


def _seg_index_map(lo, hi, tok_axis_first):
    if tok_axis_first:
        return lambda i, c: (i, jnp.clip(c - lo, 0, hi - lo - 1))
    return lambda c, kk: (jnp.where((c >= lo) & (c < hi), kk, 0), jnp.clip(c - lo, 0, hi - lo - 1))


def _mm_nn_seg(segs, wt, name, tm=1024, ride=None):
    m = segs[0].shape[0]
    n = wt.shape[1]
    tm = min(tm, m)
    bounds = _segment_chunks(segs)
    n_seg = len(segs)
    ni, nc = m // tm, bounds[-1]
    host = _Hosted(ride, n_seg + 1, 1)

    def body(*refs):
        host_refs, start, finish = host.split(refs)
        a_refs, w_ref, o_ref = host_refs[:n_seg], host_refs[n_seg], host_refs[n_seg + 1]
        i, c = pl.program_id(0), pl.program_id(1)
        start((i == 0) & (c == 0))

        @pl.when(c == 0)
        def _():
            o_ref[...] = jnp.zeros_like(o_ref)

        for s in range(n_seg):
            @pl.when((c >= bounds[s]) & (c < bounds[s + 1]))
            def _(s=s):
                o_ref[...] += jnp.dot(a_refs[s][...], w_ref[...], preferred_element_type=F32)

        finish((i == ni - 1) & (c == nc - 1))

    in_specs = [pl.BlockSpec((tm, SEG_CHUNK), _seg_index_map(bounds[s], bounds[s + 1], True)) for s in range(n_seg)]
    in_specs.append(pl.BlockSpec((SEG_CHUNK, n), lambda i, c: (c, 0)))
    outs = pl.pallas_call(
        body, name=name, grid=(ni, nc),
        in_specs=in_specs + host.in_specs, out_specs=[pl.BlockSpec((tm, n), lambda i, c: (i, 0))] + host.out_specs,
        out_shape=[jax.ShapeDtypeStruct((m, n), F32)] + host.out_shapes,
        scratch_shapes=host.scratch_shapes,
        compiler_params=_params("arbitrary" if ride else "parallel", "arbitrary"))(*segs, wt, *host.arrays)
    (res,), landed = host.results(outs, 1)
    return (res, landed) if ride else res


def _mm_tn_seg(segs, b, name, tk=2048):
    ktok = segs[0].shape[0]
    n = b.shape[1]
    tk = min(tk, ktok)
    bounds = _segment_chunks(segs)
    n_seg = len(segs)

    def body(*refs):
        a_refs, b_ref, o_ref = refs[:n_seg], refs[n_seg], refs[n_seg + 1]
        c = pl.program_id(0)
        kk = pl.program_id(1)

        @pl.when(kk == 0)
        def _():
            o_ref[...] = jnp.zeros_like(o_ref)

        rows = b_ref[pl.ds(pl.multiple_of(kk * tk, tk), tk), :]
        for s in range(n_seg):
            @pl.when((c >= bounds[s]) & (c < bounds[s + 1]))
            def _(s=s):
                o_ref[...] += lax.dot_general(a_refs[s][...], rows, TN_DIMS, preferred_element_type=F32)

    in_specs = [pl.BlockSpec((tk, SEG_CHUNK), _seg_index_map(bounds[s], bounds[s + 1], False)) for s in range(n_seg)]
    in_specs.append(pl.BlockSpec((ktok, n), lambda c, kk: (0, 0)))
    return pl.pallas_call(
        body, name=name, grid=(bounds[-1], ktok // tk),
        in_specs=in_specs, out_specs=pl.BlockSpec((SEG_CHUNK, n), lambda c, kk: (c, 0)),
        out_shape=jax.ShapeDtypeStruct((bounds[-1] * SEG_CHUNK, n), F32),
        compiler_params=_params("parallel", "arbitrary"))(*segs, b)


def _prenorm(x, g, name, tm=512):
    t, d = x.shape
    tm = min(tm, t)

    def body(x_ref, g_ref, h_ref):
        x = x_ref[...]
        r = lax.rsqrt(jnp.mean(x * x, axis=-1, keepdims=True) + EPS)
        h_ref[...] = (x * r * g_ref[...]).astype(BF16)

    return pl.pallas_call(
        body, name=name, grid=(t // tm,),
        in_specs=[pl.BlockSpec((tm, d), lambda i: (i, 0)), pl.BlockSpec((1, d), lambda i: (0, 0))],
        out_specs=pl.BlockSpec((tm, d), lambda i: (i, 0)),
        out_shape=jax.ShapeDtypeStruct((t, d), BF16),
        compiler_params=_params("parallel"))(x, g)


def _prenorm_bwd(x, g, dh, dy, name, tm=512):
    t, d = x.shape
    tm = min(tm, t)

    def body(x_ref, g_ref, dh_ref, dy_ref, gx_ref, st_ref):
        @pl.when(pl.program_id(0) == 0)
        def _():
            st_ref[...] = jnp.zeros_like(st_ref)

        x = x_ref[...]
        r = lax.rsqrt(jnp.mean(x * x, axis=-1, keepdims=True) + EPS)
        xn = x * r
        dh = dh_ref[...]
        dxn = dh * g_ref[...]
        dx = r * (dxn - xn * jnp.mean(dxn * xn, axis=-1, keepdims=True))
        gx_ref[...] = dy_ref[...] + dx
        st_ref[...] += _rows8([jnp.sum(dh * xn, axis=0, keepdims=True)], d)

    tile = pl.BlockSpec((tm, d), lambda i: (i, 0))
    return pl.pallas_call(
        body, name=name, grid=(t // tm,),
        in_specs=[tile, pl.BlockSpec((1, d), lambda i: (0, 0)), tile, tile],
        out_specs=[tile, pl.BlockSpec((8, d), lambda i: (0, 0))],
        out_shape=[jax.ShapeDtypeStruct((t, d), F32), jax.ShapeDtypeStruct((8, d), F32)],
        compiler_params=_params("arbitrary"))(x, g, dh, dy)


def _branches_fwd(z_rnn, z_attn, ag_ml, b_gate, w_rnn, w_attn, w_out, x, target, g_post, name, tm=512):
    t, d = x.shape
    tm = min(tm, t)

    def body(zr_ref, za_ref, lr_ref, la_ref, br_ref, ba_ref, wr_ref, wa_ref, wo_ref, x_ref, t_ref, g_ref,
             brr_ref, bra_ref, mg_ref, do_ref, dy_ref, st_ref):
        @pl.when(pl.program_id(0) == 0)
        def _():
            st_ref[...] = jnp.zeros_like(st_ref)

        br_rnn = jnp.dot(zr_ref[...], wr_ref[...], preferred_element_type=F32)
        br_attn = jnp.dot(za_ref[...], wa_ref[...], preferred_element_type=F32)
        brr_ref[...] = br_rnn.astype(BF16)
        bra_ref[...] = br_attn.astype(BF16)
        g_rnn = _sigmoid(lr_ref[...] + br_ref[...])
        g_attn = _sigmoid(la_ref[...] + ba_ref[...])
        merged = (g_rnn * br_rnn + g_attn * br_attn).astype(BF16)
        mg_ref[...] = merged
        o = jnp.dot(merged, wo_ref[...], preferred_element_type=F32)
        g = g_ref[...]
        r = lax.rsqrt(jnp.mean(o * o, axis=-1, keepdims=True) + EPS)
        nrm = o * r
        err = x_ref[...] + nrm * g - t_ref[...]
        dy = err * (1.0 / d)
        dy_ref[...] = dy
        dn = dy * g
        do_ref[...] = (r * (dn - nrm * jnp.mean(dn * nrm, axis=-1, keepdims=True))).astype(BF16)
        st_ref[...] += _rows8([jnp.sum(dy * nrm, axis=0, keepdims=True), jnp.sum(err * err, axis=0, keepdims=True)], d)

    tile = pl.BlockSpec((tm, d), lambda i: (i, 0))
    weight = pl.BlockSpec((d, d), lambda i: (0, 0))
    bf = jax.ShapeDtypeStruct((t, d), BF16)
    return pl.pallas_call(
        body, name=name, grid=(t // tm,),
        in_specs=[tile, tile, pl.BlockSpec((tm, d), lambda i: (i, 1)), pl.BlockSpec((tm, d), lambda i: (i, 2)),
                  pl.BlockSpec((1, d), lambda i: (0, 0)), pl.BlockSpec((1, d), lambda i: (0, 1)),
                  weight, weight, weight, tile, tile, pl.BlockSpec((1, d), lambda i: (0, 0))],
        out_specs=[tile, tile, tile, tile, tile, pl.BlockSpec((8, d), lambda i: (0, 0))],
        out_shape=[bf, bf, bf, bf, jax.ShapeDtypeStruct((t, d), F32), jax.ShapeDtypeStruct((8, d), F32)],
        compiler_params=_params("arbitrary"))(z_rnn, z_attn, ag_ml, ag_ml, b_gate, b_gate, w_rnn, w_attn, w_out, x, target, g_post)


def _branches_bwd(dout, br_rnn, br_attn, ag_ml, b_gate, w_rnn, w_attn, w_out, name, tm=512):
    t, d = br_rnn.shape
    tm = min(tm, t)

    def body(do_ref, r_ref, a_ref, lr_ref, la_ref, br_ref, ba_ref, wr_ref, wa_ref, wo_ref,
             dr_ref, da_ref, dl_ref, dzr_ref, dza_ref, st_ref):
        @pl.when(pl.program_id(0) == 0)
        def _():
            st_ref[...] = jnp.zeros_like(st_ref)

        dm = lax.dot_general(do_ref[...], wo_ref[...], NT_DIMS, preferred_element_type=F32)
        g_rnn = _sigmoid(lr_ref[...] + br_ref[...])
        g_attn = _sigmoid(la_ref[...] + ba_ref[...])
        dbr_rnn = (dm * g_rnn).astype(BF16)
        dbr_attn = (dm * g_attn).astype(BF16)
        dr_ref[...] = dbr_rnn
        da_ref[...] = dbr_attn
        dl_rnn = dm * r_ref[...].astype(F32) * g_rnn * (1.0 - g_rnn)
        dl_attn = dm * a_ref[...].astype(F32) * g_attn * (1.0 - g_attn)
        dl_ref[:, 0:d] = dl_rnn.astype(BF16)
        dl_ref[:, d:2 * d] = dl_attn.astype(BF16)
        st_ref[...] += _rows8([jnp.sum(dl_rnn, axis=0, keepdims=True), jnp.sum(dl_attn, axis=0, keepdims=True)], d)
        dzr_ref[...] = lax.dot_general(dbr_rnn, wr_ref[...], NT_DIMS, preferred_element_type=F32)
        dza_ref[...] = lax.dot_general(dbr_attn, wa_ref[...], NT_DIMS, preferred_element_type=F32)

    tile = pl.BlockSpec((tm, d), lambda i: (i, 0))
    weight = pl.BlockSpec((d, d), lambda i: (0, 0))
    bf = jax.ShapeDtypeStruct((t, d), BF16)
    return pl.pallas_call(
        body, name=name, grid=(t // tm,),
        in_specs=[tile, tile, tile, pl.BlockSpec((tm, d), lambda i: (i, 1)), pl.BlockSpec((tm, d), lambda i: (i, 2)),
                  pl.BlockSpec((1, d), lambda i: (0, 0)), pl.BlockSpec((1, d), lambda i: (0, 1)), weight, weight, weight],
        out_specs=[tile, tile, pl.BlockSpec((tm, 2 * d), lambda i: (i, 0)), tile, tile, pl.BlockSpec((8, d), lambda i: (0, 0))],
        out_shape=[bf, bf, jax.ShapeDtypeStruct((t, 2 * d), BF16), jax.ShapeDtypeStruct((t, d), F32),
                   jax.ShapeDtypeStruct((t, d), F32), jax.ShapeDtypeStruct((8, d), F32)],
        compiler_params=_params("arbitrary"))(dout, br_rnn, br_attn, ag_ml, ag_ml, b_gate, b_gate, w_rnn, w_attn, w_out)


def _lru_gates(c, wa, ba, wx, bx, sp):
    cb = c.astype(BF16)
    r = _sigmoid(jnp.dot(cb, wa, preferred_element_type=F32) + ba)
    ig = _sigmoid(jnp.dot(cb, wx, preferred_element_type=F32) + bx)
    log_a = (-LRU_C) * r * sp
    a = jnp.exp(log_a)
    mult = jnp.sqrt(-jnp.tanh(log_a) * (a * a + 1.0))
    return cb, r, ig, a, mult


SUBLANES = 8


def _scan_fwd(a, u, carry, tt):
    w = a.shape[1]
    ng = tt // SUBLANES
    a3 = a.reshape(ng, SUBLANES, w)
    u3 = u.reshape(ng, SUBLANES, w)
    sub = lax.broadcasted_iota(jnp.int32, (ng, SUBLANES, w), 1)
    d = 1
    while d < SUBLANES:
        keep = sub >= d
        u3 = u3 + a3 * jnp.where(keep, pltpu.roll(u3, d, 1), 0.0)
        a3 = a3 * jnp.where(keep, pltpu.roll(a3, d, 1), 1.0)
        d *= 2
    out = []
    for g in range(ng):
        hg = u3[g] + a3[g] * carry
        out.append(hg)
        carry = hg[SUBLANES - 1:SUBLANES, :]
    return jnp.concatenate(out, axis=0)


def _scan_rev(b, g, carry, tt):
    w = b.shape[1]
    ng = tt // SUBLANES
    b3 = b.reshape(ng, SUBLANES, w)
    g3 = g.reshape(ng, SUBLANES, w)
    sub = lax.broadcasted_iota(jnp.int32, (ng, SUBLANES, w), 1)
    d = 1
    while d < SUBLANES:
        keep = sub < SUBLANES - d
        g3 = g3 + b3 * jnp.where(keep, pltpu.roll(g3, SUBLANES - d, 1), 0.0)
        b3 = b3 * jnp.where(keep, pltpu.roll(b3, SUBLANES - d, 1), 1.0)
        d *= 2
    out = [None] * ng
    for k in range(ng - 1, -1, -1):
        hk = g3[k] + b3[k] * carry
        out[k] = hk
        carry = hk[0:1, :]
    return jnp.concatenate(out, axis=0)


def _conv_taps(cw, bias, x, ext_ref, tt):
    x2 = ext_ref[7:7 + tt, :]
    x1 = ext_ref[6:6 + tt, :]
    x0 = ext_ref[5:5 + tt, :]
    c = bias + cw[3:4] * x + cw[2:3] * x2 + cw[1:2] * x1 + cw[0:1] * x0
    return c, x2, x1, x0


def _rnn_fwd(rx_rg, cw, cb, wa, ba, wx, bx, lam, name, tt=512):
    t = rx_rg.shape[0]
    tt = min(tt, t)
    w = GROUP_W

    def body(rx_ref, rg_ref, cw_ref, cb_ref, wa_ref, ba_ref, wx_ref, bx_ref, lam_ref, y_ref, z_ref, ext_ref, hc_ref):
        @pl.when(pl.program_id(1) == 0)
        def _():
            ext_ref[0:8, :] = jnp.zeros((8, w), F32)
            hc_ref[...] = jnp.zeros((8, w), F32)

        x = rx_ref[...]
        ext_ref[8:8 + tt, :] = x
        c, _, _, _ = _conv_taps(cw_ref[...], cb_ref[...], x, ext_ref, tt)
        ext_ref[0:8, :] = x[tt - 8:tt, :]
        sp = _softplus(-lam_ref[...])
        _, _, ig, a, mult = _lru_gates(c, wa_ref[...], ba_ref[...], wx_ref[...], bx_ref[...], sp)
        h = _scan_fwd(a, mult * (ig * c), hc_ref[7:8, :], tt)
        hc_ref[...] = h[tt - 8:tt, :]
        y_ref[...] = h
        rg = rg_ref[...]
        z_ref[...] = (h * rg * _sigmoid(rg)).astype(BF16)

    vec = pl.BlockSpec((1, w), lambda g, i: (0, g))
    mat = pl.BlockSpec((None, w, w), lambda g, i: (g, 0, 0))
    tile = pl.BlockSpec((tt, w), lambda g, i: (i, g))
    return pl.pallas_call(
        body, name=name, grid=(N_GROUPS, t // tt),
        in_specs=[tile, pl.BlockSpec((tt, w), lambda g, i: (i, N_GROUPS + g)),
                  pl.BlockSpec((4, w), lambda g, i: (0, g)), vec, mat, vec, mat, vec, vec],
        out_specs=[tile, tile],
        out_shape=[jax.ShapeDtypeStruct((t, D_RNN), F32), jax.ShapeDtypeStruct((t, D_RNN), BF16)],
        scratch_shapes=[pltpu.VMEM((tt + 8, w), F32), pltpu.VMEM((8, w), F32)],
        compiler_params=_params("parallel", "arbitrary"))(rx_rg, rx_rg, cw, cb, wa, ba, wx, bx, lam)


def _rnn_bwd(rx_rg, y, dz, cw, cb, wa, ba, wx, bx, lam, name, tt=512):
    t = rx_rg.shape[0]
    tt = min(tt, t)
    nt = t // tt
    w = GROUP_W

    def body(rx_ref, rg_ref, rxt_ref, y_ref, yt_ref, dz_ref, cw_ref, cb_ref, wa_ref, ba_ref, wx_ref, bx_ref, lam_ref,
             drx_ref, drg_ref, st_ref, gda_ref, gdx_ref, ext_ref, dcx_ref, wcar_ref, acar_ref, dwa_ref, dwx_ref):
        ii = pl.program_id(1)

        @pl.when(ii == 0)
        def _():
            wcar_ref[...] = jnp.zeros((8, w), F32)
            acar_ref[...] = jnp.zeros((8, w), F32)
            dcx_ref[tt:tt + 8, :] = jnp.zeros((8, w), F32)
            st_ref[...] = jnp.zeros_like(st_ref)
            dwa_ref[...] = jnp.zeros_like(dwa_ref)
            dwx_ref[...] = jnp.zeros_like(dwx_ref)

        has_prev = jnp.where(ii == nt - 1, 0.0, 1.0)
        x = rx_ref[...]
        ext_ref[0:8, :] = rxt_ref[...] * has_prev
        ext_ref[8:8 + tt, :] = x
        cwv = cw_ref[...]
        c, x2, x1, x0 = _conv_taps(cwv, cb_ref[...], x, ext_ref, tt)
        lam = lam_ref[...]
        sp = _softplus(-lam)
        wa = wa_ref[...]
        wx = wx_ref[...]
        cb16, r, ig, a, mult = _lru_gates(c, wa, ba_ref[...], wx, bx_ref[...], sp)

        rg = rg_ref[...]
        sg = _sigmoid(rg)
        dz = dz_ref[...]
        yv = y_ref[...]
        drg_ref[...] = (dz * yv * (sg * (1.0 + rg * (1.0 - sg)))).astype(BF16)

        row = lax.broadcasted_iota(jnp.int32, (tt, w), 0)
        b = jnp.where(row < tt - 1, pltpu.roll(a, tt - 1, 0), acar_ref[0:1, :])
        dh = _scan_rev(b, dz * (rg * sg), wcar_ref[0:1, :], tt)
        wcar_ref[...] = dh[0:8, :]
        acar_ref[...] = a[0:8, :]

        hprev = jnp.where(row >= 1, pltpu.roll(yv, 1, 0), yt_ref[7:8, :] * has_prev)
        dmult = dh * (ig * c)
        dig = dh * mult * c
        dlog_a = dh * hprev * a - dmult * (a * a / mult)
        dpa = dlog_a * ((-LRU_C) * sp) * r * (1.0 - r)
        dpx = dig * ig * (1.0 - ig)
        dsp = jnp.sum(dlog_a * r, axis=0, keepdims=True) * (-LRU_C)
        dlam = dsp * (-_sigmoid(-lam))
        dpa16 = dpa.astype(BF16)
        dpx16 = dpx.astype(BF16)
        dwa_ref[...] += lax.dot_general(cb16, dpa16, TN_DIMS, preferred_element_type=F32)
        dwx_ref[...] += lax.dot_general(cb16, dpx16, TN_DIMS, preferred_element_type=F32)
        dc = (dh * mult * ig
              + lax.dot_general(dpa16, wa, NT_DIMS, preferred_element_type=F32)
              + lax.dot_general(dpx16, wx, NT_DIMS, preferred_element_type=F32))

        dcx_ref[0:tt, :] = dc
        drx = (cwv[3:4] * dc + cwv[2:3] * dcx_ref[1:1 + tt, :] + cwv[1:2] * dcx_ref[2:2 + tt, :]
               + cwv[0:1] * dcx_ref[3:3 + tt, :])
        drx_ref[...] = drx.astype(BF16)
        dcx_ref[tt:tt + 8, :] = dc[0:8, :]

        def colsum(v):
            return jnp.sum(v, axis=0, keepdims=True)

        st_ref[...] += _rows8([colsum(dc), colsum(dpa), colsum(dpx), dlam,
                               colsum(dc * x0), colsum(dc * x1), colsum(dc * x2), colsum(dc * x)], w)

        @pl.when(ii == nt - 1)
        def _():
            for blk in range(GROUP_W // RNN_BLOCK_W):
                rows = slice(blk * RNN_BLOCK_W, (blk + 1) * RNN_BLOCK_W)
                gda_ref[blk] = dwa_ref[rows, rows]
                gdx_ref[blk] = dwx_ref[rows, rows]

    def rev(ii):
        return nt - 1 - ii

    def tail(g, ii):
        return (jnp.maximum(rev(ii) * (tt // 8) - 1, 0), g)

    vec = pl.BlockSpec((1, w), lambda g, ii: (0, g))
    mat = pl.BlockSpec((None, w, w), lambda g, ii: (g, 0, 0))
    tile = pl.BlockSpec((tt, w), lambda g, ii: (rev(ii), g))
    diag_shape = (N_GROUPS, GROUP_W // RNN_BLOCK_W, RNN_BLOCK_W, RNN_BLOCK_W)
    diag = pl.BlockSpec((None,) + diag_shape[1:], lambda g, ii: (g, 0, 0, 0))
    return pl.pallas_call(
        body, name=name, grid=(N_GROUPS, nt),
        in_specs=[tile, pl.BlockSpec((tt, w), lambda g, ii: (rev(ii), N_GROUPS + g)), pl.BlockSpec((8, w), tail),
                  tile, pl.BlockSpec((8, w), tail), tile,
                  pl.BlockSpec((4, w), lambda g, ii: (0, g)), vec, mat, vec, mat, vec, vec],
        out_specs=[tile, tile, pl.BlockSpec((8, w), lambda g, ii: (0, g)), diag, diag],
        out_shape=[jax.ShapeDtypeStruct((t, D_RNN), BF16), jax.ShapeDtypeStruct((t, D_RNN), BF16),
                   jax.ShapeDtypeStruct((8, D_RNN), F32),
                   jax.ShapeDtypeStruct(diag_shape, F32), jax.ShapeDtypeStruct(diag_shape, F32)],
        scratch_shapes=[pltpu.VMEM((tt + 8, w), F32), pltpu.VMEM((tt + 8, w), F32), pltpu.VMEM((8, w), F32),
                        pltpu.VMEM((8, w), F32), pltpu.VMEM((w, w), F32), pltpu.VMEM((w, w), F32)],
        compiler_params=_params("parallel", "arbitrary"))(rx_rg, rx_rg, rx_rg, y, y, dz, cw, cb, wa, ba, wx, bx, lam)


def _half_mask(shape, half):
    lane = lax.broadcasted_iota(jnp.int32, shape, 1)
    return (lane >= HEAD_DIM) if half else (lane < HEAD_DIM)


def _dup_half(t, half):
    sel = jnp.where(_half_mask(t.shape, half), t, 0.0)
    return sel + pltpu.roll(sel, HEAD_DIM, 1)


def _band_geometry(n):
    qi = lax.broadcasted_iota(jnp.int32, (BLOCK, 2 * BLOCK), 0)
    kj = lax.broadcasted_iota(jnp.int32, (BLOCK, 2 * BLOCK), 1)
    dist = BLOCK + qi - kj
    first_key = jnp.where(n > 0, 0, BLOCK)
    valid = (dist >= 0) & (dist < BLOCK) & (kj >= first_key)
    return dist.astype(F32), valid


GROUP = 4


def _kv_dup(prev_ref, cur_ref, hk, scale=1.0):
    tile = hk // 2
    kt = jnp.concatenate([prev_ref[:, tile * LANE:(tile + 1) * LANE], cur_ref[:, tile * LANE:(tile + 1) * LANE]], axis=0)
    return (_dup_half(kt.astype(F32), hk % 2) * scale).astype(BF16)


def _fill_bias(bias_ref, sm_ref, n):
    distf, valid = _band_geometry(n)
    for head in range(N_Q_HEADS):
        bias_ref[head] = jnp.where(valid, -sm_ref[1, head] * distf, MASKED)


def _head_scores(q2s, half, kdup, bias):
    qm = jnp.where(_half_mask(q2s.shape, half), q2s, jnp.zeros_like(q2s))
    return qm, lax.dot_general(qm, kdup, NT_DIMS, preferred_element_type=F32) + bias


def _attn_specs(nb, clamp_last):
    def blk(n):
        return jnp.minimum(n, nb - 1) if clamp_last else n

    q_spec = pl.BlockSpec((BLOCK, D_MODEL), lambda n: (blk(n), 0))
    k_prev = pl.BlockSpec((BLOCK, D_KV), lambda n: (jnp.maximum(blk(n) - 1, 0), D_MODEL // D_KV))
    k_cur = pl.BlockSpec((BLOCK, D_KV), lambda n: (blk(n), D_MODEL // D_KV))
    v_prev = pl.BlockSpec((BLOCK, D_KV), lambda n: (jnp.maximum(blk(n) - 1, 0), D_MODEL // D_KV + 1))
    v_cur = pl.BlockSpec((BLOCK, D_KV), lambda n: (blk(n), D_MODEL // D_KV + 1))
    return q_spec, k_prev, k_cur, v_prev, v_cur


def _attn_fwd(sm, qkv, ag_ml, name):
    t = qkv.shape[0]
    nb = t // BLOCK

    def body(sm_ref, q_ref, kp_ref, kc_ref, vp_ref, vc_ref, ag_ref, y_ref, z_ref, lse_ref, bias_ref):
        n = pl.program_id(0)

        @pl.when(n <= 1)
        def _():
            _fill_bias(bias_ref, sm_ref, n)

        lane = lax.broadcasted_iota(jnp.int32, (BLOCK, LANE), 1)
        low = lane < HEAD_DIM
        lse = jnp.zeros((BLOCK, LANE), F32)
        for hk in range(N_Q_HEADS // GROUP):
            kdup = _kv_dup(kp_ref, kc_ref, hk)
            vdup = _kv_dup(vp_ref, vc_ref, hk)
            for k in (0, 1):
                c = slice((2 * hk + k) * LANE, (2 * hk + k + 1) * LANE)
                q2s = q_ref[:, c] * ATTN_SCALE
                outs = []
                for half in (0, 1):
                    head = GROUP * hk + 2 * k + half
                    _, s = _head_scores(q2s, half, kdup, bias_ref[head])
                    sink = sm_ref[0, head]
                    m = jnp.maximum(jnp.max(s, axis=1, keepdims=True), sink)
                    e = jnp.exp(s - m)
                    l = jnp.sum(e, axis=1, keepdims=True) + jnp.exp(sink - m)
                    outs.append(jnp.dot((e * (1.0 / l)).astype(BF16), vdup, preferred_element_type=F32))
                    lse = jnp.where(lane == head, m + jnp.log(l), lse)
                yt = jnp.where(low, outs[0], outs[1])
                y_ref[:, c] = yt
                ag = ag_ref[:, c]
                z_ref[:, c] = (yt * ag * _sigmoid(ag)).astype(BF16)
        lse_ref[...] = lse

    q_spec, k_prev, k_cur, v_prev, v_cur = _attn_specs(nb, False)
    wide = pl.BlockSpec((BLOCK, D_MODEL), lambda n: (n, 0))
    return pl.pallas_call(
        body, name=name, grid=(nb,),
        in_specs=[pl.BlockSpec(memory_space=pltpu.SMEM), q_spec, k_prev, k_cur, v_prev, v_cur, wide],
        out_specs=[wide, wide, pl.BlockSpec((BLOCK, LANE), lambda n: (n, 0))],
        out_shape=[jax.ShapeDtypeStruct((t, D_MODEL), F32), jax.ShapeDtypeStruct((t, D_MODEL), BF16),
                   jax.ShapeDtypeStruct((t, LANE), F32)],
        scratch_shapes=[pltpu.VMEM((N_Q_HEADS, BLOCK, 2 * BLOCK), F32)],
        compiler_params=_params("arbitrary"))(sm, qkv, qkv, qkv, qkv, qkv, ag_ml)


def _attn_bwd(sm, qkv, ag_ml, y, lse, dz, name, ride=None):
    t = qkv.shape[0]
    nb = t // BLOCK
    host = _Hosted(ride, 10, 4, 3)

    def body(*refs):
        host_refs, start, finish = host.split(refs)
        (sm_ref, q_ref, kp_ref, kc_ref, vp_ref, vc_ref, ag_ref, y_ref, lse_ref, dz_ref,
         dq_ref, dkv_ref, dag_ref, ds_ref, ck_ref, cv_ref, bias_ref) = host_refs
        n = pl.program_id(0)
        start(n == 0)

        @pl.when(n == 0)
        def _():
            ck_ref[...] = jnp.zeros_like(ck_ref)
            cv_ref[...] = jnp.zeros_like(cv_ref)
            ds_ref[...] = jnp.zeros_like(ds_ref)

        @pl.when(n <= 1)
        def _():
            _fill_bias(bias_ref, sm_ref, n)

        @pl.when(n < nb)
        def _():
            lane8 = lax.broadcasted_iota(jnp.int32, (8, LANE), 1)
            row8 = lax.broadcasted_iota(jnp.int32, (8, LANE), 0)
            dsink = jnp.zeros((8, LANE), F32)
            dk_heads, dv_heads = [], []
            lse_tile = lse_ref[...]
            for hk in range(N_Q_HEADS // GROUP):
                kdup = _kv_dup(kp_ref, kc_ref, hk)
                ks = _kv_dup(kp_ref, kc_ref, hk, ATTN_SCALE)
                vdup = _kv_dup(vp_ref, vc_ref, hk)
                dk_acc = jnp.zeros((2 * BLOCK, LANE), F32)
                dv_acc = jnp.zeros((2 * BLOCK, LANE), F32)
                for k in (0, 1):
                    c = slice((2 * hk + k) * LANE, (2 * hk + k + 1) * LANE)
                    ag = ag_ref[:, c]
                    sg = _sigmoid(ag)
                    dzt = dz_ref[:, c]
                    yt = y_ref[:, c]
                    dag_ref[:, c] = (dzt * yt * (sg * (1.0 + ag * (1.0 - sg)))).astype(BF16)
                    dyt = dzt * (ag * sg)
                    q2s = q_ref[:, c] * ATTN_SCALE
                    dq_halves = []
                    for half in (0, 1):
                        head = GROUP * hk + 2 * k + half
                        qm, s = _head_scores(q2s, half, kdup, bias_ref[head])
                        lh = lse_tile[:, head:head + 1]
                        probs = jnp.exp(s - lh)
                        psink = jnp.exp(sm_ref[0, head] - lh)
                        dyh = jnp.where(_half_mask(dyt.shape, half), dyt, 0.0)
                        delta = jnp.sum(dyh * yt, axis=1, keepdims=True)
                        dyh16 = dyh.astype(BF16)
                        dp = lax.dot_general(dyh16, vdup, NT_DIMS, preferred_element_type=F32)
                        ds16 = (probs * (dp - delta)).astype(BF16)
                        dsink = dsink + jnp.where((row8 == 0) & (lane8 == head),
                                                  -jnp.sum(psink * delta, axis=0, keepdims=True), 0.0)
                        dq_halves.append(jnp.dot(ds16, ks, preferred_element_type=F32))
                        dk_acc = dk_acc + lax.dot_general(ds16, qm, TN_DIMS, preferred_element_type=F32)
                        dv_acc = dv_acc + lax.dot_general(probs.astype(BF16), dyh16, TN_DIMS, preferred_element_type=F32)
                    dq_ref[:, c] = jnp.where(_half_mask((BLOCK, LANE), 0), dq_halves[0], dq_halves[1]).astype(BF16)
                dk_heads.append(dk_acc + pltpu.roll(dk_acc, HEAD_DIM, 1))
                dv_heads.append(dv_acc + pltpu.roll(dv_acc, HEAD_DIM, 1))
            ds_ref[...] += dsink
            low = _half_mask((2 * BLOCK, LANE), 0)
            for tile in range(2):
                cols = slice(tile * LANE, (tile + 1) * LANE)
                dkt = jnp.where(low, dk_heads[2 * tile], dk_heads[2 * tile + 1])
                dvt = jnp.where(low, dv_heads[2 * tile], dv_heads[2 * tile + 1])
                dkv_ref[:, cols] = (ck_ref[:, cols] + dkt[0:BLOCK, :]).astype(BF16)
                dkv_ref[:, D_KV + tile * LANE:D_KV + (tile + 1) * LANE] = (cv_ref[:, cols] + dvt[0:BLOCK, :]).astype(BF16)
                ck_ref[:, cols] = dkt[BLOCK:2 * BLOCK, :]
                cv_ref[:, cols] = dvt[BLOCK:2 * BLOCK, :]

        @pl.when(n == nb)
        def _():
            dkv_ref[:, 0:D_KV] = ck_ref[...].astype(BF16)
            dkv_ref[:, D_KV:2 * D_KV] = cv_ref[...].astype(BF16)

        finish(n == nb)

    q_spec, k_prev, k_cur, v_prev, v_cur = _attn_specs(nb, True)
    wide = pl.BlockSpec((BLOCK, D_MODEL), lambda n: (jnp.minimum(n, nb - 1), 0))
    outs = pl.pallas_call(
        body, name=name, grid=(nb + 1,),
        in_specs=[pl.BlockSpec(memory_space=pltpu.SMEM), q_spec, k_prev, k_cur, v_prev, v_cur, wide, wide,
                  pl.BlockSpec((BLOCK, LANE), lambda n: (jnp.minimum(n, nb - 1), 0)), wide] + host.in_specs,
        out_specs=[wide, pl.BlockSpec((BLOCK, 2 * D_KV), lambda n: (jnp.maximum(n - 1, 0), 0)), wide,
                   pl.BlockSpec((8, LANE), lambda n: (0, 0))] + host.out_specs,
        out_shape=[jax.ShapeDtypeStruct((t, D_MODEL), BF16), jax.ShapeDtypeStruct((t, 2 * D_KV), BF16),
                   jax.ShapeDtypeStruct((t, D_MODEL), BF16), jax.ShapeDtypeStruct((8, LANE), F32)] + host.out_shapes,
        scratch_shapes=[pltpu.VMEM((BLOCK, D_KV), F32), pltpu.VMEM((BLOCK, D_KV), F32),
                        pltpu.VMEM((N_Q_HEADS, BLOCK, 2 * BLOCK), F32)] + host.scratch_shapes,
        compiler_params=_params("arbitrary"))(sm, qkv, qkv, qkv, qkv, qkv, ag_ml, y, lse, dz, *host.arrays)
    res, landed = host.results(outs, 4)
    return (*res, landed) if ride else tuple(res)


def _local_grads(x, target, p, late_weights=None, reduce_out=None, reduce_in=None):
    wt = p["wt"]
    h = _prenorm(x, p["pre_g"], "prenorm")
    rx_rg = _mm_nt(h, wt, 0, 2 * D_RNN, F32, "proj_rnn")
    qkv = _mm_nt(h, wt, 2 * D_RNN, D_MODEL + 2 * D_KV, BF16, "proj_qkv")
    gates_off = 2 * D_RNN + D_MODEL + 2 * D_KV
    if late_weights is None:
        ag_ml = _mm_nt(h, wt, gates_off, 3 * D_MODEL, F32, "proj_gates")
    else:
        ag_ml, landed = _mm_nt(h, wt, gates_off, 3 * D_MODEL, F32, "proj_gates", ride=late_weights[0])
        p = {**p, **late_weights[1](landed)}
    lru =(p["cw"], p["cb"], p["wbd_a"], p["b_a"], p["wbd_x"], p["b_x"], p["lam"])
    y_rnn, z_rnn = _rnn_fwd(rx_rg, *lru, "rnn_fwd")
    y_attn, z_attn, lse = _attn_fwd(p["sm"], qkv, ag_ml, "attn_fwd")
    br_rnn, br_attn, merged, dout, dy, st_post = _branches_fwd(
        z_rnn, z_attn, ag_ml, p["b_gate"], p["w_rnn"], p["w_attn"], p["w_out"], x, target, p["post_g"], "branches_fwd")

    dbr_rnn, dbr_attn, d_ml, dz_rnn, dz_attn, st_merge = _branches_bwd(
        dout, br_rnn, br_attn, ag_ml, p["b_gate"], p["w_rnn"], p["w_attn"], p["w_out"], "branches_bwd")
    gw_out = _mm_tn(merged, dout, "gw_out")
    gw_rnn = _mm_tn(z_rnn, dbr_rnn, "gw_rnn")
    gw_attn = _mm_tn(z_attn, dbr_attn, "gw_attn")
    d_rx, d_rg, st_rnn, g_rg_a, g_rg_x = _rnn_bwd(rx_rg, y_rnn, dz_rnn, *lru, "rnn_bwd")
    red_out = red_in = None
    if reduce_out is None:
        dq, dkv, d_ag, st_sink = _attn_bwd(p["sm"], qkv, ag_ml, y_attn, lse, dz_attn, "attn_bwd")
    else:
        dq, dkv, d_ag, st_sink, red_out = _attn_bwd(p["sm"], qkv, ag_ml, y_attn, lse, dz_attn, "attn_bwd",
                                                    ride=reduce_out(gw_rnn, gw_attn, gw_out, g_rg_a, g_rg_x))

    segs = [d_rx, d_rg, dq, dkv, d_ag, d_ml]
    gwt = _mm_tn_seg(segs, h, "gw_in")
    if reduce_in is None:
        dh = _mm_nn_seg(segs, wt, "d_h")
    else:
        dh, red_in = _mm_nn_seg(segs, wt, "d_h", ride=reduce_in(gwt))
    grad_x, st_pre = _prenorm_bwd(x, p["pre_g"], dh, dy, "prenorm_bwd")
    return dict(grad_x=grad_x, gwt=gwt, gw_rnn=gw_rnn, gw_attn=gw_attn, gw_out=gw_out,
                st_post=st_post, st_merge=st_merge, st_rnn=st_rnn, st_sink=st_sink, st_pre=st_pre,
                g_rg_a=g_rg_a, g_rg_x=g_rg_x, red_out=red_out, red_in=red_in)


def _place():
    x, y, c = lax.axis_index("x"), lax.axis_index("y"), lax.axis_index("c")
    return x, y, c


def _gather_ride(shards):
    n = len(shards)

    def copies(ins, outs, sems):
        send_sems, recv_sems, local_sems = sems
        x, y, c = _place()
        me, sibling = (x, y, c), (x, y, 1 - c)
        chips = [(1 - x, y), (x, 1 - y), (1 - x, 1 - y)]

        def slot(a, dev):
            return outs[a].at[4 * dev[0] + 2 * dev[1] + dev[2]]

        def copy(a, k, block, to, src=None):
            return pltpu.make_async_remote_copy(
                src_ref=slot(a, block) if src is None else src, dst_ref=slot(a, block),
                send_sem=send_sems.at[a, k], recv_sem=recv_sems.at[a, k], device_id=to, device_id_type=MESH)

        mine = [pltpu.make_async_copy(ins[a], slot(a, me), local_sems.at[a]) for a in range(n)]
        first = []
        for a in range(n):
            first.append(copy(a, 0, me, sibling, src=ins[a]))
            first += [copy(a, 1 + j, me, (*chip, c), src=ins[a]) for j, chip in enumerate(chips)]
        return me, sibling, chips, c, copy, mine, first

    def start(ins, outs, sems):
        *_, mine, first = copies(ins, outs, sems)
        for cp in mine + first:
            cp.start()

    def finish(ins, outs, sems):
        me, sibling, chips, c, copy, mine, first = copies(ins, outs, sems)
        passed = []
        for j, chip in enumerate(chips):
            for a in range(n):
                copy(a, 1 + j, (*chip, c), me).wait_recv()
                cp = copy(a, 4 + j, (*chip, c), sibling)
                cp.start()
                passed.append(cp)
        for a in range(n):
            copy(a, 0, sibling, me).wait_recv()
            for j, chip in enumerate(chips):
                copy(a, 4 + j, (*chip, 1 - c), me).wait_recv()
        for cp in first + passed:
            cp.wait_send()
        for cp in mine:
            cp.wait()

    return _Ride(
        shards, [jax.ShapeDtypeStruct((N_DEV, *s.shape), s.dtype) for s in shards],
        [pltpu.SemaphoreType.DMA((n, 7)), pltpu.SemaphoreType.DMA((n, 7)), pltpu.SemaphoreType.DMA((n,))],
        start, finish)


def _exchange_sibling(scatter, whole, name):
    ns, nw = len(scatter), len(whole)

    def body(*refs):
        s_in, w_in = refs[:ns], refs[ns:ns + nw]
        s_out, w_out = refs[ns + nw:2 * ns + nw], refs[2 * ns + nw:2 * (ns + nw)]
        send_sems, recv_sems = refs[2 * (ns + nw):]
        x, y, c = _place()
        sibling = (x, y, 1 - c)
        copies = []
        for a in range(ns):
            for chip in range(N_CHIP):
                copies.append(pltpu.make_async_remote_copy(
                    src_ref=s_in[a].at[2 * chip + (1 - c)], dst_ref=s_out[a].at[chip],
                    send_sem=send_sems.at[a * N_CHIP + chip], recv_sem=recv_sems.at[a * N_CHIP + chip],
                    device_id=sibling, device_id_type=MESH))
        for a in range(nw):
            copies.append(pltpu.make_async_remote_copy(
                src_ref=w_in[a], dst_ref=w_out[a],
                send_sem=send_sems.at[ns * N_CHIP + a], recv_sem=recv_sems.at[ns * N_CHIP + a],
                device_id=sibling, device_id_type=MESH))
        for cp in copies:
            cp.start()
        for cp in copies:
            cp.wait()

    n_sem = ns * N_CHIP + nw
    return pl.pallas_call(
        body, name=name,
        in_specs=[ANY] * (ns + nw), out_specs=[ANY] * (ns + nw),
        out_shape=([jax.ShapeDtypeStruct((N_CHIP, *s.shape[1:]), s.dtype) for s in scatter]
                   + [jax.ShapeDtypeStruct(s.shape, s.dtype) for s in whole]),
        scratch_shapes=[pltpu.SemaphoreType.DMA((n_sem,)), pltpu.SemaphoreType.DMA((n_sem,))],
    )(*scatter, *whole)


def _chips_ride(scatter, whole):
    ns, nw = len(scatter), len(whole)
    n = ns + nw

    def copies(ins, outs, sems):
        send_sems, recv_sems, local_sems = sems
        x, y, c = _place()
        own = 2 * x + y
        chips = [(1 - x, y), (x, 1 - y), (1 - x, 1 - y)]

        def src(a, chip_idx):
            return ins[a].at[chip_idx] if a < ns else ins[a]

        local = [pltpu.make_async_copy(src(a, own), outs[a].at[own], local_sems.at[a]) for a in range(n)]
        sent = [pltpu.make_async_remote_copy(
            src_ref=src(a, 2 * chip[0] + chip[1]), dst_ref=outs[a].at[own],
            send_sem=send_sems.at[a, j], recv_sem=recv_sems.at[a, own], device_id=(*chip, c), device_id_type=MESH)
            for a in range(n) for j, chip in enumerate(chips)]
        return chips, c, local, sent

    def start(ins, outs, sems):
        _, _, local, sent = copies(ins, outs, sems)
        for cp in local + sent:
            cp.start()

    def finish(ins, outs, sems):
        send_sems, recv_sems, _ = sems
        chips, c, local, sent = copies(ins, outs, sems)
        for a in range(n):
            for chip in chips:
                k = 2 * chip[0] + chip[1]
                pltpu.make_async_remote_copy(
                    src_ref=outs[a].at[k], dst_ref=outs[a].at[k], send_sem=send_sems.at[a, 0],
                    recv_sem=recv_sems.at[a, k], device_id=(*chip, c), device_id_type=MESH).wait_recv()
        for cp in sent:
            cp.wait_send()
        for cp in local:
            cp.wait()

    return _Ride(
        list(scatter) + list(whole),
        [jax.ShapeDtypeStruct(s.shape, s.dtype) for s in scatter]
        + [jax.ShapeDtypeStruct((N_CHIP, *s.shape), s.dtype) for s in whole],
        [pltpu.SemaphoreType.DMA((n, 3)), pltpu.SemaphoreType.DMA((n, N_CHIP)), pltpu.SemaphoreType.DMA((n,))],
        start, finish)


def _pair_sum_scatter(parts, recvs, core, name):
    na = len(parts)
    _, r, cdim = parts[0].shape
    tr = min(r, 416 if r % 416 == 0 else 128)

    def body(core_ref, *refs):
        del core_ref
        for a in range(na):
            refs[2 * na + a][...] = (refs[a][...] + refs[na + a][...]).astype(BF16)

    blk = (None, tr, cdim)
    mine = pl.BlockSpec(blk, lambda k, i, core_ref: (2 * k + core_ref[0], i, 0))
    slot = pl.BlockSpec(blk, lambda k, i, core_ref: (k, i, 0))
    return pl.pallas_call(
        body, name=name,
        grid_spec=pltpu.PrefetchScalarGridSpec(
            num_scalar_prefetch=1, grid=(N_CHIP, r // tr),
            in_specs=[mine] * na + [slot] * na, out_specs=[slot] * na),
        out_shape=[jax.ShapeDtypeStruct((N_CHIP, r, cdim), BF16)] * na,
        compiler_params=_params("parallel", "parallel"))(core, *parts, *recvs)


def _pair_sum_whole(mine, recvs, name):
    na = len(mine)

    def body(*refs):
        for a in range(na):
            refs[2 * na + a][...] = refs[a][...] + refs[na + a][...]

    return pl.pallas_call(body, name=name, out_shape=[jax.ShapeDtypeStruct(m.shape, F32) for m in mine])(*mine, *recvs)


def _allreduce_small(pack, name):
    shape = pack.shape

    def body(x_ref, o_ref, sib_ref, chip_ref, send_sems, recv_sems):
        x, y, c = _place()
        own = 2 * x + y
        chips = [(1 - x, y), (x, 1 - y), (1 - x, 1 - y)]
        to_sibling = pltpu.make_async_remote_copy(
            src_ref=x_ref, dst_ref=sib_ref, send_sem=send_sems.at[0], recv_sem=recv_sems.at[0],
            device_id=(x, y, 1 - c), device_id_type=MESH)
        to_sibling.start()
        to_sibling.wait()
        chip_ref[own] = x_ref[...] + sib_ref[...]
        sent = [pltpu.make_async_remote_copy(
            src_ref=chip_ref.at[own], dst_ref=chip_ref.at[own], send_sem=send_sems.at[1 + j],
            recv_sem=recv_sems.at[1 + own], device_id=(*chip, c), device_id_type=MESH) for j, chip in enumerate(chips)]
        for cp in sent:
            cp.start()
        for chip in chips:
            k = 2 * chip[0] + chip[1]
            pltpu.make_async_remote_copy(
                src_ref=chip_ref.at[k], dst_ref=chip_ref.at[k], send_sem=send_sems.at[1],
                recv_sem=recv_sems.at[1 + k], device_id=(*chip, c), device_id_type=MESH).wait_recv()
        for cp in sent:
            cp.wait_send()
        o_ref[...] = (chip_ref[0] + chip_ref[1]) + (chip_ref[2] + chip_ref[3])

    return pl.pallas_call(
        body, name=name, out_shape=jax.ShapeDtypeStruct(shape, F32),
        in_specs=[pl.BlockSpec(memory_space=pltpu.VMEM)], out_specs=pl.BlockSpec(memory_space=pltpu.VMEM),
        scratch_shapes=[pltpu.VMEM(shape, F32), pltpu.VMEM((N_CHIP, *shape), F32),
                        pltpu.SemaphoreType.DMA((4,)), pltpu.SemaphoreType.DMA((1 + N_CHIP,))],
    )(pack)


def _adamw(g, w, m, v):
    m = ADAM_B1 * m + (1.0 - ADAM_B1) * g
    v = ADAM_B2 * v + (1.0 - ADAM_B2) * (g * g)
    m_hat = m / (1.0 - ADAM_B1 ** ADAM_STEP)
    v_hat = v / (1.0 - ADAM_B2 ** ADAM_STEP)
    delta = -ADAM_LR * (m_hat / (jnp.sqrt(v_hat) + ADAM_EPS) + ADAM_WD * w)
    return delta, m, v


def _adam_parts(parts, w, m, v, name, tr=None):
    npart, r, c = parts.shape
    tr = r if tr is None else min(tr, r)

    def body(p_ref, w_ref, m_ref, v_ref, g_ref, d_ref, nm_ref, nv_ref):
        g = p_ref[0].astype(F32)
        for k in range(1, npart):
            g = g + p_ref[k].astype(F32)
        g_ref[...] = g
        d_ref[...], nm_ref[...], nv_ref[...] = _adamw(g, w_ref[...], m_ref[...], v_ref[...])

    tile = pl.BlockSpec((tr, c), lambda i: (i, 0))
    return pl.pallas_call(
        body, name=name, grid=(r // tr,),
        in_specs=[pl.BlockSpec((npart, tr, c), lambda i: (0, i, 0)), tile, tile, tile],
        out_specs=[tile] * 4, out_shape=[jax.ShapeDtypeStruct((r, c), F32)] * 4,
        compiler_params=_params("parallel"))(parts, w, m, v)


def _adam_transposed(parts, w, m, v, name, tc=256):
    npart, c, r = parts.shape

    def body(p_ref, w_ref, m_ref, v_ref, g_ref, d_ref, nm_ref, nv_ref):
        gt = p_ref[0].astype(F32)
        for k in range(1, npart):
            gt = gt + p_ref[k].astype(F32)
        g = gt.T
        g_ref[...] = g
        d_ref[...], nm_ref[...], nv_ref[...] = _adamw(g, w_ref[...], m_ref[...], v_ref[...])

    tile = pl.BlockSpec((tc, c), lambda i: (i, 0))
    return pl.pallas_call(
        body, name=name, grid=(r // tc,),
        in_specs=[pl.BlockSpec((npart, c, tc), lambda i: (0, 0, i)), tile, tile, tile],
        out_specs=[tile] * 4, out_shape=[jax.ShapeDtypeStruct((r, c), F32)] * 4,
        compiler_params=_params("parallel"))(parts, w, m, v)


def _transpose_bf16(w, name, tr=256):
    r, c = w.shape

    def body(w_ref, o_ref):
        o_ref[...] = w_ref[...].T.astype(BF16)

    return pl.pallas_call(
        body, name=name, grid=(r // tr,),
        in_specs=[pl.BlockSpec((tr, c), lambda i: (i, 0))],
        out_specs=pl.BlockSpec((c, tr), lambda i: (0, i)),
        out_shape=jax.ShapeDtypeStruct((c, r), BF16),
        compiler_params=_params("parallel"))(w)


def _block_diag(w):
    w4 = w.reshape(N_GROUPS, 4, RNN_BLOCK_W, RNN_BLOCK_W)
    eye = jnp.eye(4, dtype=w.dtype)
    return jnp.einsum("gbij,bc->gbicj", w4, eye).reshape(N_GROUPS, GROUP_W, GROUP_W).astype(BF16)


SMALL_ROWS = 16
ROW_PRE_G, ROW_BGATE, ROW_CONV_B, ROW_B_A, ROW_B_X, ROW_LAM, ROW_POST_G, ROW_LOSS, ROW_SINKS, ROW_CONV_W = 0, 1, 3, 4, 5, 6, 7, 8, 9, 10


def _pack_stats(st_pre, st_merge, st_rnn, st_post, st_sink, name):
    d = D_MODEL

    def body(pre_ref, mg_ref, rnn_ref, post_ref, sink_ref, o_ref):
        rnn = rnn_ref[...]
        sinks = jnp.concatenate([sink_ref[0:1, :], jnp.zeros((1, d - LANE), F32)], axis=1)
        o_ref[0:8, :] = _rows8([pre_ref[0:1, :], mg_ref[0:1, :], mg_ref[1:2, :], rnn[0:1], rnn[1:2], rnn[2:3], rnn[3:4],
                                post_ref[0:1, :]], d)
        o_ref[8:16, :] = _rows8([post_ref[1:2, :], sinks, rnn[4:5], rnn[5:6], rnn[6:7], rnn[7:8]], d)

    return pl.pallas_call(body, name=name, out_shape=jax.ShapeDtypeStruct((SMALL_ROWS, d), F32))(
        st_pre, st_merge, st_rnn, st_post, st_sink)


def _adam_small(total, w, m, v, name):
    d = D_MODEL
    n_in = len(w)

    def pack(refs):
        pre, bg, cbias, ba, bx, lam, post, sinks = [r[...] for r in refs]
        top = _rows8([pre, bg[:, 0:d], bg[:, d:2 * d], cbias, ba, bx, lam, post], d)
        return jnp.concatenate([top, _rows8([jnp.zeros((1, d), F32), sinks], d)], axis=0)

    def body(*refs):
        p_ref = refs[0]
        w_refs, m_refs, v_refs = (refs[1 + k * n_in:1 + (k + 1) * n_in] for k in range(3))
        outs = refs[1 + 3 * n_in:]
        g = p_ref[...]
        res = (g,) + _adamw(g, pack(w_refs), pack(m_refs), pack(v_refs))
        for k in range(4):
            outs[k][...] = res[k]
            outs[4 + k][...] = jnp.concatenate([res[k][ROW_BGATE:ROW_BGATE + 1], res[k][ROW_BGATE + 1:ROW_BGATE + 2]], axis=1)

    return pl.pallas_call(
        body, name=name,
        out_shape=[jax.ShapeDtypeStruct((SMALL_ROWS, d), F32)] * 4 + [jax.ShapeDtypeStruct((1, 2 * d), F32)] * 4,
    )(total, *w, *m, *v)


def kernel(x, pre_norm_g, w_in, b_gate, conv_w, conv_b, w_rg_a, b_rg_a, w_rg_x, b_rg_x, lru_lambda, attn_sinks, w_rnn_out, w_attn_out, w_out, post_norm_g, loss_target, m_pre_norm_g, m_w_in, m_b_gate, m_conv_w, m_conv_b, m_w_rg_a, m_b_rg_a, m_w_rg_x, m_b_rg_x, m_lru_lambda, m_attn_sinks, m_w_rnn_out, m_w_attn_out, m_w_out, m_post_norm_g, v_pre_norm_g, v_w_in, v_b_gate, v_conv_w, v_conv_b, v_w_rg_a, v_b_rg_a, v_w_rg_x, v_b_rg_x, v_lru_lambda, v_attn_sinks, v_w_rnn_out, v_w_attn_out, v_w_out, v_post_norm_g):
    cx, cy, cc = _place()
    dev = 4 * cx + 2 * cy + cc
    core = jnp.reshape(cc, (1,)).astype(jnp.int32)

    wt_shard = _transpose_bf16(w_in[0], "wt_shard")
    (wt_all,) = _run_ride(_gather_ride([wt_shard]), "gather_w_in")
    heads = jnp.arange(1, N_Q_HEADS + 1, dtype=F32)
    slopes = jnp.exp2(-ALIBI_MAX_BIAS * heads / N_Q_HEADS)
    b_a = b_rg_a.reshape(1, D_RNN)
    b_x = b_rg_x.reshape(1, D_RNN)
    p = dict(
        wt=wt_all.reshape(D_IN, D_MODEL),
        pre_g=pre_norm_g, post_g=post_norm_g, b_gate=b_gate, cb=conv_b,
        wbd_a=_block_diag(w_rg_a[0]), b_a=b_a, wbd_x=_block_diag(w_rg_x[0]), b_x=b_x, lam=lru_lambda,
        sm=jnp.pad(attn_sinks, ((0, 1), (0, 0))) + jnp.pad(slopes[None, :], ((1, 0), (0, 0))))

    def late_unpack(landed):
        w_rnn_all, w_attn_all, w_out_all, cw_all = landed
        return dict(w_rnn=w_rnn_all.reshape(D_RNN, D_MODEL), w_attn=w_attn_all.reshape(D_MODEL, D_MODEL),
                    w_out=w_out_all.reshape(D_MODEL, D_MODEL), cw=jnp.transpose(cw_all, (1, 0, 2)).reshape(4, D_RNN))

    late_weights = (_gather_ride([w_rnn_out[0].astype(BF16), w_attn_out[0].astype(BF16), w_out[0].astype(BF16), conv_w[0]]),
                    late_unpack)
    flat = (RNN_BLOCKS * RNN_BLOCK_W, RNN_BLOCK_W)

    def reduce_out(gw_rnn, gw_attn, gw_out, g_rg_a, g_rg_x):
        scatter = [gw.reshape(N_DEV, SHARD_OUT, D_MODEL) for gw in (gw_rnn, gw_attn, gw_out)]
        whole = [g_rg_a.reshape(flat), g_rg_x.reshape(flat)]
        recv = _exchange_sibling(scatter, whole, "sibling_out")
        return _chips_ride(_pair_sum_scatter(scatter, recv[:3], core, "pair_out"),
                           _pair_sum_whole(whole, recv[3:], "pair_rg"))

    def reduce_in(gwt):
        scatter = [gwt.reshape(N_DEV, SHARD_IN, D_MODEL)]
        recv = _exchange_sibling(scatter, [], "sibling_in")
        return _chips_ride(_pair_sum_scatter(scatter, recv, core, "pair_in"), [])

    g = _local_grads(x[0], loss_target[0], p, late_weights, reduce_out, reduce_in)
    small = _allreduce_small(
        _pack_stats(g["st_pre"], g["st_merge"], g["st_rnn"], g["st_post"], g["st_sink"], "pack_stats"), "allreduce_small")

    out = {}
    red = g["red_out"]
    out["w_in"] = _adam_transposed(g["red_in"][0], w_in[0], m_w_in[0], v_w_in[0], "adam_w_in")
    out["w_rnn_out"] = _adam_parts(red[0], w_rnn_out[0], m_w_rnn_out[0], v_w_rnn_out[0], "adam_w_rnn_out")
    out["w_attn_out"] = _adam_parts(red[1], w_attn_out[0], m_w_attn_out[0], v_w_attn_out[0], "adam_w_attn_out")
    out["w_out"] = _adam_parts(red[2], w_out[0], m_w_out[0], v_w_out[0], "adam_w_out")
    out["w_rg_a"] = _adam_parts(red[3], w_rg_a.reshape(flat), m_w_rg_a.reshape(flat), v_w_rg_a.reshape(flat), "adam_w_rg_a", tr=256)
    out["w_rg_x"] = _adam_parts(red[4], w_rg_x.reshape(flat), m_w_rg_x.reshape(flat), v_w_rg_x.reshape(flat), "adam_w_rg_x", tr=256)

    def rows(pre, bg, cbias, ba, bx, lam, post, sinks):
        return (pre, bg, cbias, ba.reshape(1, D_RNN), bx.reshape(1, D_RNN), lam, post,
                jnp.pad(sinks, ((0, 0), (0, D_MODEL - N_Q_HEADS))))

    small_out = _adam_small(
        small,
        rows(pre_norm_g, b_gate, conv_b, b_rg_a, b_rg_x, lru_lambda, post_norm_g, attn_sinks),
        rows(m_pre_norm_g, m_b_gate, m_conv_b, m_b_rg_a, m_b_rg_x, m_lru_lambda, m_post_norm_g, m_attn_sinks),
        rows(v_pre_norm_g, v_b_gate, v_conv_b, v_b_rg_a, v_b_rg_x, v_lru_lambda, v_post_norm_g, v_attn_sinks),
        "adam_small")
    packed, bgate_out = small_out[:4], small_out[4:]
    g_cw = lax.dynamic_slice(packed[0][ROW_CONV_W:ROW_CONV_W + 4], (0, dev * SHARD_OUT), (4, SHARD_OUT))
    out["conv_w"] = _adam_parts(g_cw[None], conv_w[0], m_conv_w[0], v_conv_w[0], "adam_conv_w")

    def unpack(kind, name):
        if name == "b_gate":
            return bgate_out[kind]
        row = dict(pre_norm_g=ROW_PRE_G, conv_b=ROW_CONV_B, b_rg_a=ROW_B_A, b_rg_x=ROW_B_X, lru_lambda=ROW_LAM,
                   post_norm_g=ROW_POST_G, attn_sinks=ROW_SINKS)[name]
        r = packed[kind][row:row + 1]
        if name == "attn_sinks":
            return r[:, 0:N_Q_HEADS]
        if name in ("b_rg_a", "b_rg_x"):
            return r.reshape(1, RNN_BLOCKS, RNN_BLOCK_W)
        return r

    shapes = dict(w_in=(1, D_MODEL, SHARD_IN), w_rnn_out=(1, SHARD_OUT, D_MODEL), w_attn_out=(1, SHARD_OUT, D_MODEL),
                  w_out=(1, SHARD_OUT, D_MODEL), w_rg_a=(1, RNN_BLOCKS, RNN_BLOCK_W, RNN_BLOCK_W),
                  w_rg_x=(1, RNN_BLOCKS, RNN_BLOCK_W, RNN_BLOCK_W), conv_w=(1, 4, SHARD_OUT))
    weights = ["pre_norm_g", "w_in", "b_gate", "conv_w", "conv_b", "w_rg_a", "b_rg_a", "w_rg_x", "b_rg_x",
               "lru_lambda", "attn_sinks", "w_rnn_out", "w_attn_out", "w_out", "post_norm_g"]
    results = []
    for kind in range(4):
        for name in weights:
            if name in out:
                results.append(out[name][kind].reshape(shapes[name]))
            else:
                results.append(unpack(kind, name))
    loss = 0.5 / D_MODEL * jnp.sum(packed[0][ROW_LOSS])
    return (loss, g["grad_x"][None], *results)
```

```python
import functools

import jax
import jax.numpy as jnp
from jax import lax
from jax.experimental import pallas as pl
from jax.experimental.pallas import tpu as pltpu

F32 = jnp.float32
BF16 = jnp.bfloat16

D_MODEL = 1024
D_RNN = 1024
RNN_BLOCKS = 16
RNN_BLOCK_W = 64
LRU_C = 8.0
N_Q_HEADS = 16
HEAD_DIM = 64
D_KV = 256
BLOCK = 128
ALIBI_MAX_BIAS = 8.0
EPS = 1e-6
D_IN = 6656
N_DEV = 8
N_CHIP = 4
SHARD_IN = D_IN // N_DEV
SHARD_OUT = D_MODEL // N_DEV
ATTN_SCALE = HEAD_DIM ** -0.5
MASKED = -1e30

ADAM_LR = 0.001
ADAM_B1 = 0.9
ADAM_B2 = 0.999
ADAM_EPS = 1e-08
ADAM_WD = 0.01
ADAM_STEP = 10

VMEM_LIMIT_BYTES = 52 * 1024 * 1024
LANE = 128
GROUP_W = 256
N_GROUPS = D_RNN // GROUP_W
SEG_CHUNK = 512

NT_DIMS = (((1,), (1,)), ((), ()))
TN_DIMS = (((0,), (0,)), ((), ()))
MESH = pl.DeviceIdType.MESH
ANY = pl.BlockSpec(memory_space=pl.ANY)


def _params(*semantics):
    return pltpu.CompilerParams(dimension_semantics=semantics, vmem_limit_bytes=VMEM_LIMIT_BYTES)


def _sigmoid(x):
    return 0.5 * jnp.tanh(0.5 * x) + 0.5


def _log1p(e):
    u = 1.0 + e
    den = jnp.where(u == 1.0, 1.0, u - 1.0)
    return jnp.where(u == 1.0, e, jnp.log(u) * (e / den))


def _softplus(z):
    return jnp.maximum(z, 0.0) + _log1p(jnp.exp(-jnp.abs(z)))


def _rows8(rows, width):
    idx = lax.broadcasted_iota(jnp.int32, (8, width), 0)
    out = jnp.zeros((8, width), F32)
    for r, v in enumerate(rows):
        out = jnp.where(idx == r, v, out)
    return out


class _Ride:
    def __init__(self, arrays, out_shapes, scratch_shapes, start, finish):
        self.arrays, self.out_shapes, self.scratch_shapes = list(arrays), list(out_shapes), list(scratch_shapes)
        self.start, self.finish = start, finish


class _Hosted:
    def __init__(self, ride, n_in, n_out, n_scratch=0):
        self.ride = ride
        self.sizes = (n_in, len(ride.arrays) if ride else 0, n_out, len(ride.out_shapes) if ride else 0, n_scratch)
        self.arrays = ride.arrays if ride else []
        self.in_specs = [ANY] * len(self.arrays)
        self.out_shapes = ride.out_shapes if ride else []
        self.out_specs = [ANY] * len(self.out_shapes)
        self.scratch_shapes = ride.scratch_shapes if ride else []

    def split(self, refs):
        n_in, r_in, n_out, r_out, n_scr = self.sizes
        cuts = [0, n_in, n_in + r_in, n_in + r_in + n_out, n_in + r_in + n_out + r_out, n_in + r_in + n_out + r_out + n_scr]
        host_in, ride_in, host_out, ride_out, host_scr = (refs[cuts[k]:cuts[k + 1]] for k in range(5))
        ride_scr = refs[cuts[5]:]

        def start(when):
            if self.ride is not None:
                pl.when(when)(lambda: self.ride.start(ride_in, ride_out, ride_scr))

        def finish(when):
            if self.ride is not None:
                pl.when(when)(lambda: self.ride.finish(ride_in, ride_out, ride_scr))

        return tuple(host_in) + tuple(host_out) + tuple(host_scr), start, finish

    def results(self, outs, n_out):
        outs = list(outs) if isinstance(outs, (list, tuple)) else [outs]
        return outs[:n_out], outs[n_out:]


def _run_ride(ride, name):
    n_in, n_out = len(ride.arrays), len(ride.out_shapes)

    def body(*refs):
        ins, outs, sems = refs[:n_in], refs[n_in:n_in + n_out], refs[n_in + n_out:]
        ride.start(ins, outs, sems)
        ride.finish(ins, outs, sems)

    return pl.pallas_call(
        body, name=name, in_specs=[ANY] * n_in, out_specs=[ANY] * n_out,
        out_shape=ride.out_shapes, scratch_shapes=ride.scratch_shapes)(*ride.arrays)


def _load_resident(w_hbm, w_vmem, sems, first):
    def piece(c):
        rows = pl.ds(c * SEG_CHUNK, SEG_CHUNK)
        return pltpu.make_async_copy(w_hbm.at[rows], w_vmem.at[rows], sems.at[c])

    @pl.when(first)
    def _():
        for c in range(w_vmem.shape[0] // SEG_CHUNK):
            piece(c).start()

    def ready(c):
        @pl.when(first)
        def _():
            piece(c).wait()

    return ready


def _proj(h, wt, widths_dtypes, name, tm=512, ride=None):
    t, k = h.shape
    n = wt.shape[0]
    tm = min(tm, t)
    ni = t // tm
    n_out = len(widths_dtypes)
    host = _Hosted(ride, 2, n_out, 2)

    def body(*refs):
        host_refs, start, finish = host.split(refs)
        h_ref, wt_hbm = host_refs[:2]
        o_refs = host_refs[2:2 + n_out]
        wt_vmem, sems = host_refs[2 + n_out:]
        i = pl.program_id(0)
        start(i == 0)
        ready = _load_resident(wt_hbm, wt_vmem, sems, i == 0)
        hv = h_ref[...]
        c = 0
        for o_ref, (width, _) in zip(o_refs, widths_dtypes):
            for kk in range(width // SEG_CHUNK):
                ready(c)
                o_ref[:, kk * SEG_CHUNK:(kk + 1) * SEG_CHUNK] = lax.dot_general(
                    hv, wt_vmem[c * SEG_CHUNK:(c + 1) * SEG_CHUNK, :], NT_DIMS, preferred_element_type=F32).astype(o_ref.dtype)
                c += 1
        finish(i == ni - 1)

    outs = pl.pallas_call(
        body, name=name, grid=(ni,),
        in_specs=[pl.BlockSpec((tm, k), lambda i: (i, 0)), ANY] + host.in_specs,
        out_specs=[pl.BlockSpec((tm, w), lambda i: (i, 0)) for w, _ in widths_dtypes] + host.out_specs,
        out_shape=[jax.ShapeDtypeStruct((t, w), dt) for w, dt in widths_dtypes] + host.out_shapes,
        scratch_shapes=[pltpu.VMEM((n, k), wt.dtype), pltpu.SemaphoreType.DMA((n // SEG_CHUNK,))] + host.scratch_shapes,
        compiler_params=_params("arbitrary"))(h, wt, *host.arrays)
    res, landed = host.results(outs, n_out)
    return (res, landed) if ride else res


def _mm_tn(a, b, name, tm=512, tk=4096):
    ktok, m = a.shape
    n = b.shape[1]
    tk = min(tk, ktok)

    def body(a_ref, b_ref, o_ref):
        @pl.when(pl.program_id(1) == 0)
        def _():
            o_ref[...] = jnp.zeros_like(o_ref)

        o_ref[...] += lax.dot_general(a_ref[...], b_ref[...], TN_DIMS, preferred_element_type=F32)

    return pl.pallas_call(
        body, name=name, grid=(m // tm, ktok // tk),
        in_specs=[pl.BlockSpec((tk, tm), lambda i, kk: (kk, i)), pl.BlockSpec((tk, n), lambda i, kk: (kk, 0))],
        out_specs=pl.BlockSpec((tm, n), lambda i, kk: (i, 0)),
        out_shape=jax.ShapeDtypeStruct((m, n), F32),
        compiler_params=_params("parallel", "arbitrary"))(a, b)


def _segment_chunks(segs):
    bounds = [0]
    for s in segs:
        bounds.append(bounds[-1] + s.shape[1] // SEG_CHUNK)
    return bounds


def _seg_index_map(lo, hi):
    return lambda c, kk: (jnp.where((c >= lo) & (c < hi), kk, 0), jnp.clip(c - lo, 0, hi - lo - 1))


def _mm_nn_seg(segs, wt, name, tm=512, ride=None):
    m = segs[0].shape[0]
    rows, n = wt.shape
    tm = min(tm, m)
    bounds = _segment_chunks(segs)
    n_seg = len(segs)
    ni = m // tm
    host = _Hosted(ride, n_seg + 1, 1, 2)

    def body(*refs):
        host_refs, start, finish = host.split(refs)
        a_refs, wt_hbm, o_ref, wt_vmem, sems = host_refs[:n_seg], host_refs[n_seg], host_refs[n_seg + 1], host_refs[n_seg + 2], host_refs[n_seg + 3]
        i = pl.program_id(0)
        start(i == 0)
        ready = _load_resident(wt_hbm, wt_vmem, sems, i == 0)
        acc = None
        for s in range(n_seg):
            for c in range(bounds[s], bounds[s + 1]):
                ready(c)
            part = jnp.dot(a_refs[s][...], wt_vmem[bounds[s] * SEG_CHUNK:bounds[s + 1] * SEG_CHUNK, :], preferred_element_type=F32)
            acc = part if acc is None else acc + part
        o_ref[...] = acc
        finish(i == ni - 1)

    outs = pl.pallas_call(
        body, name=name, grid=(ni,),
        in_specs=[pl.BlockSpec((tm, sg.shape[1]), lambda i: (i, 0)) for sg in segs] + [ANY] + host.in_specs,
        out_specs=[pl.BlockSpec((tm, n), lambda i: (i, 0))] + host.out_specs,
        out_shape=[jax.ShapeDtypeStruct((m, n), F32)] + host.out_shapes,
        scratch_shapes=[pltpu.VMEM((rows, n), wt.dtype), pltpu.SemaphoreType.DMA((rows // SEG_CHUNK,))] + host.scratch_shapes,
        compiler_params=_params("arbitrary"))(*segs, wt, *host.arrays)
    (res,), landed = host.results(outs, 1)
    return (res, landed) if ride else res


def _mm_tn_seg(segs, b, name, tk=2048):
    ktok = segs[0].shape[0]
    n = b.shape[1]
    tk = min(tk, ktok)
    bounds = _segment_chunks(segs)
    n_seg = len(segs)

    def body(*refs):
        a_refs, b_ref, o_ref = refs[:n_seg], refs[n_seg], refs[n_seg + 1]
        c = pl.program_id(0)
        kk = pl.program_id(1)

        @pl.when(kk == 0)
        def _():
            o_ref[...] = jnp.zeros_like(o_ref)

        rows = b_ref[pl.ds(pl.multiple_of(kk * tk, tk), tk), :]
        for s in range(n_seg):
            @pl.when((c >= bounds[s]) & (c < bounds[s + 1]))
            def _(s=s):
                o_ref[...] += lax.dot_general(a_refs[s][...], rows, TN_DIMS, preferred_element_type=F32)

    in_specs = [pl.BlockSpec((tk, SEG_CHUNK), _seg_index_map(bounds[s], bounds[s + 1])) for s in range(n_seg)]
    in_specs.append(pl.BlockSpec((ktok, n), lambda c, kk: (0, 0)))
    return pl.pallas_call(
        body, name=name, grid=(bounds[-1], ktok // tk),
        in_specs=in_specs, out_specs=pl.BlockSpec((SEG_CHUNK, n), lambda c, kk: (c, 0)),
        out_shape=jax.ShapeDtypeStruct((bounds[-1] * SEG_CHUNK, n), F32),
        compiler_params=_params("parallel", "arbitrary"))(*segs, b)


def _prenorm(x, g, name, tm=512):
    t, d = x.shape
    tm = min(tm, t)

    def body(x_ref, g_ref, h_ref):
        x = x_ref[...]
        r = lax.rsqrt(jnp.mean(x * x, axis=-1, keepdims=True) + EPS)
        h_ref[...] = (x * r * g_ref[...]).astype(BF16)

    return pl.pallas_call(
        body, name=name, grid=(t // tm,),
        in_specs=[pl.BlockSpec((tm, d), lambda i: (i, 0)), pl.BlockSpec((1, d), lambda i: (0, 0))],
        out_specs=pl.BlockSpec((tm, d), lambda i: (i, 0)),
        out_shape=jax.ShapeDtypeStruct((t, d), BF16),
        compiler_params=_params("parallel"))(x, g)


def _prenorm_bwd(x, g, dh, dy, name, tm=512):
    t, d = x.shape
    tm = min(tm, t)

    def body(x_ref, g_ref, dh_ref, dy_ref, gx_ref, st_ref):
        @pl.when(pl.program_id(0) == 0)
        def _():
            st_ref[...] = jnp.zeros_like(st_ref)

        x = x_ref[...]
        r = lax.rsqrt(jnp.mean(x * x, axis=-1, keepdims=True) + EPS)
        xn = x * r
        dh = dh_ref[...]
        dxn = dh * g_ref[...]
        dx = r * (dxn - xn * jnp.mean(dxn * xn, axis=-1, keepdims=True))
        gx_ref[...] = dy_ref[...] + dx
        st_ref[...] += _rows8([jnp.sum(dh * xn, axis=0, keepdims=True)], d)

    tile = pl.BlockSpec((tm, d), lambda i: (i, 0))
    return pl.pallas_call(
        body, name=name, grid=(t // tm,),
        in_specs=[tile, pl.BlockSpec((1, d), lambda i: (0, 0)), tile, tile],
        out_specs=[tile, pl.BlockSpec((8, d), lambda i: (0, 0))],
        out_shape=[jax.ShapeDtypeStruct((t, d), F32), jax.ShapeDtypeStruct((8, d), F32)],
        compiler_params=_params("arbitrary"))(x, g, dh, dy)


def _branches_fwd(z_rnn, z_attn, ag_ml, b_gate, w_rnn, w_attn, w_out, x, target, g_post, name, tm=512):
    t, d = x.shape
    tm = min(tm, t)

    def body(zr_ref, za_ref, lr_ref, la_ref, br_ref, ba_ref, wr_ref, wa_ref, wo_ref, x_ref, t_ref, g_ref,
             brr_ref, bra_ref, mg_ref, do_ref, dy_ref, st_ref):
        @pl.when(pl.program_id(0) == 0)
        def _():
            st_ref[...] = jnp.zeros_like(st_ref)

        br_rnn = jnp.dot(zr_ref[...], wr_ref[...], preferred_element_type=F32)
        br_attn = jnp.dot(za_ref[...], wa_ref[...], preferred_element_type=F32)
        brr_ref[...] = br_rnn.astype(BF16)
        bra_ref[...] = br_attn.astype(BF16)
        g_rnn = _sigmoid(lr_ref[...] + br_ref[...])
        g_attn = _sigmoid(la_ref[...] + ba_ref[...])
        merged = (g_rnn * br_rnn + g_attn * br_attn).astype(BF16)
        mg_ref[...] = merged
        o = jnp.dot(merged, wo_ref[...], preferred_element_type=F32)
        g = g_ref[...]
        r = lax.rsqrt(jnp.mean(o * o, axis=-1, keepdims=True) + EPS)
        nrm = o * r
        err = x_ref[...] + nrm * g - t_ref[...]
        dy = err * (1.0 / d)
        dy_ref[...] = dy
        dn = dy * g
        do_ref[...] = (r * (dn - nrm * jnp.mean(dn * nrm, axis=-1, keepdims=True))).astype(BF16)
        st_ref[...] += _rows8([jnp.sum(dy * nrm, axis=0, keepdims=True), jnp.sum(err * err, axis=0, keepdims=True)], d)

    tile = pl.BlockSpec((tm, d), lambda i: (i, 0))
    weight = pl.BlockSpec((d, d), lambda i: (0, 0))
    bf = jax.ShapeDtypeStruct((t, d), BF16)
    return pl.pallas_call(
        body, name=name, grid=(t // tm,),
        in_specs=[tile, tile, pl.BlockSpec((tm, d), lambda i: (i, 1)), pl.BlockSpec((tm, d), lambda i: (i, 2)),
                  pl.BlockSpec((1, d), lambda i: (0, 0)), pl.BlockSpec((1, d), lambda i: (0, 1)),
                  weight, weight, weight, tile, tile, pl.BlockSpec((1, d), lambda i: (0, 0))],
        out_specs=[tile, tile, tile, tile, tile, pl.BlockSpec((8, d), lambda i: (0, 0))],
        out_shape=[bf, bf, bf, bf, jax.ShapeDtypeStruct((t, d), F32), jax.ShapeDtypeStruct((8, d), F32)],
        compiler_params=_params("arbitrary"))(z_rnn, z_attn, ag_ml, ag_ml, b_gate, b_gate, w_rnn, w_attn, w_out, x, target, g_post)


def _branches_bwd(dout, br_rnn, br_attn, ag_ml, b_gate, w_rnn, w_attn, w_out, name, tm=512):
    t, d = br_rnn.shape
    tm = min(tm, t)

    def body(do_ref, r_ref, a_ref, lr_ref, la_ref, br_ref, ba_ref, wr_ref, wa_ref, wo_ref,
             dr_ref, da_ref, dl_ref, dzr_ref, dza_ref, st_ref):
        @pl.when(pl.program_id(0) == 0)
        def _():
            st_ref[...] = jnp.zeros_like(st_ref)

        dm = lax.dot_general(do_ref[...], wo_ref[...], NT_DIMS, preferred_element_type=F32)
        g_rnn = _sigmoid(lr_ref[...] + br_ref[...])
        g_attn = _sigmoid(la_ref[...] + ba_ref[...])
        dbr_rnn = (dm * g_rnn).astype(BF16)
        dbr_attn = (dm * g_attn).astype(BF16)
        dr_ref[...] = dbr_rnn
        da_ref[...] = dbr_attn
        dl_rnn = dm * r_ref[...].astype(F32) * g_rnn * (1.0 - g_rnn)
        dl_attn = dm * a_ref[...].astype(F32) * g_attn * (1.0 - g_attn)
        dl_ref[:, 0:d] = dl_rnn.astype(BF16)
        dl_ref[:, d:2 * d] = dl_attn.astype(BF16)
        st_ref[...] += _rows8([jnp.sum(dl_rnn, axis=0, keepdims=True), jnp.sum(dl_attn, axis=0, keepdims=True)], d)
        dzr_ref[...] = lax.dot_general(dbr_rnn, wr_ref[...], NT_DIMS, preferred_element_type=F32)
        dza_ref[...] = lax.dot_general(dbr_attn, wa_ref[...], NT_DIMS, preferred_element_type=F32)

    tile = pl.BlockSpec((tm, d), lambda i: (i, 0))
    weight = pl.BlockSpec((d, d), lambda i: (0, 0))
    bf = jax.ShapeDtypeStruct((t, d), BF16)
    return pl.pallas_call(
        body, name=name, grid=(t // tm,),
        in_specs=[tile, tile, tile, pl.BlockSpec((tm, d), lambda i: (i, 1)), pl.BlockSpec((tm, d), lambda i: (i, 2)),
                  pl.BlockSpec((1, d), lambda i: (0, 0)), pl.BlockSpec((1, d), lambda i: (0, 1)), weight, weight, weight],
        out_specs=[tile, tile, pl.BlockSpec((tm, 2 * d), lambda i: (i, 0)), tile, tile, pl.BlockSpec((8, d), lambda i: (0, 0))],
        out_shape=[bf, bf, jax.ShapeDtypeStruct((t, 2 * d), BF16), jax.ShapeDtypeStruct((t, d), F32),
                   jax.ShapeDtypeStruct((t, d), F32), jax.ShapeDtypeStruct((8, d), F32)],
        compiler_params=_params("arbitrary"))(dout, br_rnn, br_attn, ag_ml, ag_ml, b_gate, b_gate, w_rnn, w_attn, w_out)


def _lru_gates(c, wa, ba, wx, bx, sp):
    cb = c.astype(BF16)
    r = _sigmoid(jnp.dot(cb, wa, preferred_element_type=F32) + ba)
    ig = _sigmoid(jnp.dot(cb, wx, preferred_element_type=F32) + bx)
    log_a = (-LRU_C) * r * sp
    a = jnp.exp(log_a)
    mult = jnp.sqrt(-jnp.tanh(log_a) * (a * a + 1.0))
    return cb, r, ig, a, mult


SUBLANES = 8


def _scan_fwd(a, u, carry, tt):
    w = a.shape[1]
    ng = tt // SUBLANES
    a3 = a.reshape(ng, SUBLANES, w)
    u3 = u.reshape(ng, SUBLANES, w)
    sub = lax.broadcasted_iota(jnp.int32, (ng, SUBLANES, w), 1)
    d = 1
    while d < SUBLANES:
        keep = sub >= d
        u3 = u3 + a3 * jnp.where(keep, pltpu.roll(u3, d, 1), 0.0)
        a3 = a3 * jnp.where(keep, pltpu.roll(a3, d, 1), 1.0)
        d *= 2
    out = []
    for g in range(ng):
        hg = u3[g] + a3[g] * carry
        out.append(hg)
        carry = hg[SUBLANES - 1:SUBLANES, :]
    return jnp.concatenate(out, axis=0)


def _scan_rev(b, g, carry, tt):
    w = b.shape[1]
    ng = tt // SUBLANES
    b3 = b.reshape(ng, SUBLANES, w)
    g3 = g.reshape(ng, SUBLANES, w)
    sub = lax.broadcasted_iota(jnp.int32, (ng, SUBLANES, w), 1)
    d = 1
    while d < SUBLANES:
        keep = sub < SUBLANES - d
        g3 = g3 + b3 * jnp.where(keep, pltpu.roll(g3, SUBLANES - d, 1), 0.0)
        b3 = b3 * jnp.where(keep, pltpu.roll(b3, SUBLANES - d, 1), 1.0)
        d *= 2
    out = [None] * ng
    for k in range(ng - 1, -1, -1):
        hk = g3[k] + b3[k] * carry
        out[k] = hk
        carry = hk[0:1, :]
    return jnp.concatenate(out, axis=0)


def _conv_taps(cw, bias, x, ext_ref, tt):
    x2 = ext_ref[7:7 + tt, :]
    x1 = ext_ref[6:6 + tt, :]
    x0 = ext_ref[5:5 + tt, :]
    c = bias + cw[3:4] * x + cw[2:3] * x2 + cw[1:2] * x1 + cw[0:1] * x0
    return c, x2, x1, x0


def _rnn_fwd(rx_rg, cw, cb, wa, ba, wx, bx, lam, name, tt=512):
    t = rx_rg.shape[0]
    tt = min(tt, t)
    w = GROUP_W

    def body(rx_ref, rg_ref, cw_ref, cb_ref, wa_ref, ba_ref, wx_ref, bx_ref, lam_ref, y_ref, z_ref, ext_ref, hc_ref):
        @pl.when(pl.program_id(1) == 0)
        def _():
            ext_ref[0:8, :] = jnp.zeros((8, w), F32)
            hc_ref[...] = jnp.zeros((8, w), F32)

        x = rx_ref[...]
        ext_ref[8:8 + tt, :] = x
        c, _, _, _ = _conv_taps(cw_ref[...], cb_ref[...], x, ext_ref, tt)
        ext_ref[0:8, :] = x[tt - 8:tt, :]
        sp = _softplus(-lam_ref[...])
        _, _, ig, a, mult = _lru_gates(c, wa_ref[...], ba_ref[...], wx_ref[...], bx_ref[...], sp)
        h = _scan_fwd(a, mult * (ig * c), hc_ref[7:8, :], tt)
        hc_ref[...] = h[tt - 8:tt, :]
        y_ref[...] = h
        rg = rg_ref[...]
        z_ref[...] = (h * rg * _sigmoid(rg)).astype(BF16)

    vec = pl.BlockSpec((1, w), lambda g, i: (0, g))
    mat = pl.BlockSpec((None, w, w), lambda g, i: (g, 0, 0))
    tile = pl.BlockSpec((tt, w), lambda g, i: (i, g))
    return pl.pallas_call(
        body, name=name, grid=(N_GROUPS, t // tt),
        in_specs=[tile, pl.BlockSpec((tt, w), lambda g, i: (i, N_GROUPS + g)),
                  pl.BlockSpec((4, w), lambda g, i: (0, g)), vec, mat, vec, mat, vec, vec],
        out_specs=[tile, tile],
        out_shape=[jax.ShapeDtypeStruct((t, D_RNN), F32), jax.ShapeDtypeStruct((t, D_RNN), BF16)],
        scratch_shapes=[pltpu.VMEM((tt + 8, w), F32), pltpu.VMEM((8, w), F32)],
        compiler_params=_params("parallel", "arbitrary"))(rx_rg, rx_rg, cw, cb, wa, ba, wx, bx, lam)


def _rnn_bwd(rx_rg, y, dz, cw, cb, wa, ba, wx, bx, lam, name, tt=512):
    t = rx_rg.shape[0]
    tt = min(tt, t)
    nt = t // tt
    w = GROUP_W

    def body(rx_ref, rg_ref, rxt_ref, y_ref, yt_ref, dz_ref, cw_ref, cb_ref, wa_ref, ba_ref, wx_ref, bx_ref, lam_ref,
             drx_ref, drg_ref, st_ref, gda_ref, gdx_ref, ext_ref, dcx_ref, wcar_ref, acar_ref, dwa_ref, dwx_ref):
        ii = pl.program_id(1)

        @pl.when(ii == 0)
        def _():
            wcar_ref[...] = jnp.zeros((8, w), F32)
            acar_ref[...] = jnp.zeros((8, w), F32)
            dcx_ref[tt:tt + 8, :] = jnp.zeros((8, w), F32)
            st_ref[...] = jnp.zeros_like(st_ref)
            dwa_ref[...] = jnp.zeros_like(dwa_ref)
            dwx_ref[...] = jnp.zeros_like(dwx_ref)

        has_prev = jnp.where(ii == nt - 1, 0.0, 1.0)
        x = rx_ref[...]
        ext_ref[0:8, :] = rxt_ref[...] * has_prev
        ext_ref[8:8 + tt, :] = x
        cwv = cw_ref[...]
        c, x2, x1, x0 = _conv_taps(cwv, cb_ref[...], x, ext_ref, tt)
        lam = lam_ref[...]
        sp = _softplus(-lam)
        wa = wa_ref[...]
        wx = wx_ref[...]
        cb16, r, ig, a, mult = _lru_gates(c, wa, ba_ref[...], wx, bx_ref[...], sp)

        rg = rg_ref[...]
        sg = _sigmoid(rg)
        dz = dz_ref[...]
        yv = y_ref[...]
        drg_ref[...] = (dz * yv * (sg * (1.0 + rg * (1.0 - sg)))).astype(BF16)

        row = lax.broadcasted_iota(jnp.int32, (tt, w), 0)
        b = jnp.where(row < tt - 1, pltpu.roll(a, tt - 1, 0), acar_ref[0:1, :])
        dh = _scan_rev(b, dz * (rg * sg), wcar_ref[0:1, :], tt)
        wcar_ref[...] = dh[0:8, :]
        acar_ref[...] = a[0:8, :]

        hprev = jnp.where(row >= 1, pltpu.roll(yv, 1, 0), yt_ref[7:8, :] * has_prev)
        dmult = dh * (ig * c)
        dig = dh * mult * c
        dlog_a = dh * hprev * a - dmult * (a * a / mult)
        dpa = dlog_a * ((-LRU_C) * sp) * r * (1.0 - r)
        dpx = dig * ig * (1.0 - ig)
        dsp = jnp.sum(dlog_a * r, axis=0, keepdims=True) * (-LRU_C)
        dlam = dsp * (-_sigmoid(-lam))
        dpa16 = dpa.astype(BF16)
        dpx16 = dpx.astype(BF16)
        dwa_ref[...] += lax.dot_general(cb16, dpa16, TN_DIMS, preferred_element_type=F32)
        dwx_ref[...] += lax.dot_general(cb16, dpx16, TN_DIMS, preferred_element_type=F32)
        dc = (dh * mult * ig
              + lax.dot_general(dpa16, wa, NT_DIMS, preferred_element_type=F32)
              + lax.dot_general(dpx16, wx, NT_DIMS, preferred_element_type=F32))

        dcx_ref[0:tt, :] = dc
        drx = (cwv[3:4] * dc + cwv[2:3] * dcx_ref[1:1 + tt, :] + cwv[1:2] * dcx_ref[2:2 + tt, :]
               + cwv[0:1] * dcx_ref[3:3 + tt, :])
        drx_ref[...] = drx.astype(BF16)
        dcx_ref[tt:tt + 8, :] = dc[0:8, :]

        def colsum(v):
            return jnp.sum(v, axis=0, keepdims=True)

        st_ref[...] += _rows8([colsum(dc), colsum(dpa), colsum(dpx), dlam,
                               colsum(dc * x0), colsum(dc * x1), colsum(dc * x2), colsum(dc * x)], w)

        @pl.when(ii == nt - 1)
        def _():
            for blk in range(GROUP_W // RNN_BLOCK_W):
                rows = slice(blk * RNN_BLOCK_W, (blk + 1) * RNN_BLOCK_W)
                gda_ref[blk] = dwa_ref[rows, rows]
                gdx_ref[blk] = dwx_ref[rows, rows]

    def rev(ii):
        return nt - 1 - ii

    def tail(g, ii):
        return (jnp.maximum(rev(ii) * (tt // 8) - 1, 0), g)

    vec = pl.BlockSpec((1, w), lambda g, ii: (0, g))
    mat = pl.BlockSpec((None, w, w), lambda g, ii: (g, 0, 0))
    tile = pl.BlockSpec((tt, w), lambda g, ii: (rev(ii), g))
    diag_shape = (N_GROUPS, GROUP_W // RNN_BLOCK_W, RNN_BLOCK_W, RNN_BLOCK_W)
    diag = pl.BlockSpec((None,) + diag_shape[1:], lambda g, ii: (g, 0, 0, 0))
    return pl.pallas_call(
        body, name=name, grid=(N_GROUPS, nt),
        in_specs=[tile, pl.BlockSpec((tt, w), lambda g, ii: (rev(ii), N_GROUPS + g)), pl.BlockSpec((8, w), tail),
                  tile, pl.BlockSpec((8, w), tail), tile,
                  pl.BlockSpec((4, w), lambda g, ii: (0, g)), vec, mat, vec, mat, vec, vec],
        out_specs=[tile, tile, pl.BlockSpec((8, w), lambda g, ii: (0, g)), diag, diag],
        out_shape=[jax.ShapeDtypeStruct((t, D_RNN), BF16), jax.ShapeDtypeStruct((t, D_RNN), BF16),
                   jax.ShapeDtypeStruct((8, D_RNN), F32),
                   jax.ShapeDtypeStruct(diag_shape, F32), jax.ShapeDtypeStruct(diag_shape, F32)],
        scratch_shapes=[pltpu.VMEM((tt + 8, w), F32), pltpu.VMEM((tt + 8, w), F32), pltpu.VMEM((8, w), F32),
                        pltpu.VMEM((8, w), F32), pltpu.VMEM((w, w), F32), pltpu.VMEM((w, w), F32)],
        compiler_params=_params("parallel", "arbitrary"))(rx_rg, rx_rg, rx_rg, y, y, dz, cw, cb, wa, ba, wx, bx, lam)


def _half_mask(shape, half):
    lane = lax.broadcasted_iota(jnp.int32, shape, 1)
    return (lane >= HEAD_DIM) if half else (lane < HEAD_DIM)


def _dup_half(t, half):
    sel = jnp.where(_half_mask(t.shape, half), t, 0.0)
    return sel + pltpu.roll(sel, HEAD_DIM, 1)


def _band_geometry(n):
    qi = lax.broadcasted_iota(jnp.int32, (BLOCK, 2 * BLOCK), 0)
    kj = lax.broadcasted_iota(jnp.int32, (BLOCK, 2 * BLOCK), 1)
    dist = BLOCK + qi - kj
    first_key = jnp.where(n > 0, 0, BLOCK)
    valid = (dist >= 0) & (dist < BLOCK) & (kj >= first_key)
    return dist.astype(F32), valid


GROUP = 4


def _kv_dup(prev_ref, cur_ref, hk, scale=1.0):
    tile = hk // 2
    kt = jnp.concatenate([prev_ref[:, tile * LANE:(tile + 1) * LANE], cur_ref[:, tile * LANE:(tile + 1) * LANE]], axis=0)
    return (_dup_half(kt.astype(F32), hk % 2) * scale).astype(BF16)


def _fill_bias(bias_ref, sm_ref, n):
    distf, valid = _band_geometry(n)
    for head in range(N_Q_HEADS):
        bias_ref[head] = jnp.where(valid, -sm_ref[1, head] * distf, MASKED)


def _head_scores(q2s, half, kdup, bias):
    qm = jnp.where(_half_mask(q2s.shape, half), q2s, jnp.zeros_like(q2s))
    return qm, lax.dot_general(qm, kdup, NT_DIMS, preferred_element_type=F32) + bias


def _attn_specs(nb, clamp_last):
    def blk(n):
        return jnp.minimum(n, nb - 1) if clamp_last else n

    q_spec = pl.BlockSpec((BLOCK, D_MODEL), lambda n: (blk(n), 0))
    k_prev = pl.BlockSpec((BLOCK, D_KV), lambda n: (jnp.maximum(blk(n) - 1, 0), D_MODEL // D_KV))
    k_cur = pl.BlockSpec((BLOCK, D_KV), lambda n: (blk(n), D_MODEL // D_KV))
    v_prev = pl.BlockSpec((BLOCK, D_KV), lambda n: (jnp.maximum(blk(n) - 1, 0), D_MODEL // D_KV + 1))
    v_cur = pl.BlockSpec((BLOCK, D_KV), lambda n: (blk(n), D_MODEL // D_KV + 1))
    return q_spec, k_prev, k_cur, v_prev, v_cur


def _attn_fwd(sm, qkv, ag_ml, name):
    t = qkv.shape[0]
    nb = t // BLOCK

    def body(sm_ref, q_ref, kp_ref, kc_ref, vp_ref, vc_ref, ag_ref, y_ref, z_ref, lse_ref, bias_ref):
        n = pl.program_id(0)

        @pl.when(n <= 1)
        def _():
            _fill_bias(bias_ref, sm_ref, n)

        lane = lax.broadcasted_iota(jnp.int32, (BLOCK, LANE), 1)
        low = lane < HEAD_DIM
        lse = jnp.zeros((BLOCK, LANE), F32)
        for hk in range(N_Q_HEADS // GROUP):
            kdup = _kv_dup(kp_ref, kc_ref, hk)
            vdup = _kv_dup(vp_ref, vc_ref, hk)
            for k in (0, 1):
                c = slice((2 * hk + k) * LANE, (2 * hk + k + 1) * LANE)
                q2s = q_ref[:, c] * ATTN_SCALE
                outs = []
                for half in (0, 1):
                    head = GROUP * hk + 2 * k + half
                    _, s = _head_scores(q2s, half, kdup, bias_ref[head])
                    sink = sm_ref[0, head]
                    m = jnp.maximum(jnp.max(s, axis=1, keepdims=True), sink)
                    e = jnp.exp(s - m)
                    l = jnp.sum(e, axis=1, keepdims=True) + jnp.exp(sink - m)
                    outs.append(jnp.dot((e * (1.0 / l)).astype(BF16), vdup, preferred_element_type=F32))
                    lse = jnp.where(lane == head, m + jnp.log(l), lse)
                yt = jnp.where(low, outs[0], outs[1])
                y_ref[:, c] = yt
                ag = ag_ref[:, c]
                z_ref[:, c] = (yt * ag * _sigmoid(ag)).astype(BF16)
        lse_ref[...] = lse

    q_spec, k_prev, k_cur, v_prev, v_cur = _attn_specs(nb, False)
    wide = pl.BlockSpec((BLOCK, D_MODEL), lambda n: (n, 0))
    return pl.pallas_call(
        body, name=name, grid=(nb,),
        in_specs=[pl.BlockSpec(memory_space=pltpu.SMEM), q_spec, k_prev, k_cur, v_prev, v_cur, wide],
        out_specs=[wide, wide, pl.BlockSpec((BLOCK, LANE), lambda n: (n, 0))],
        out_shape=[jax.ShapeDtypeStruct((t, D_MODEL), F32), jax.ShapeDtypeStruct((t, D_MODEL), BF16),
                   jax.ShapeDtypeStruct((t, LANE), F32)],
        scratch_shapes=[pltpu.VMEM((N_Q_HEADS, BLOCK, 2 * BLOCK), F32)],
        compiler_params=_params("arbitrary"))(sm, qkv, qkv, qkv, qkv, qkv, ag_ml)


def _attn_bwd(sm, qkv, ag_ml, y, lse, dz, name, ride=None):
    t = qkv.shape[0]
    nb = t // BLOCK
    host = _Hosted(ride, 10, 4, 3)

    def body(*refs):
        host_refs, start, finish = host.split(refs)
        (sm_ref, q_ref, kp_ref, kc_ref, vp_ref, vc_ref, ag_ref, y_ref, lse_ref, dz_ref,
         dq_ref, dkv_ref, dag_ref, ds_ref, ck_ref, cv_ref, bias_ref) = host_refs
        n = pl.program_id(0)
        start(n == 0)

        @pl.when(n == 0)
        def _():
            ck_ref[...] = jnp.zeros_like(ck_ref)
            cv_ref[...] = jnp.zeros_like(cv_ref)
            ds_ref[...] = jnp.zeros_like(ds_ref)

        @pl.when(n <= 1)
        def _():
            _fill_bias(bias_ref, sm_ref, n)

        @pl.when(n < nb)
        def _():
            lane8 = lax.broadcasted_iota(jnp.int32, (8, LANE), 1)
            row8 = lax.broadcasted_iota(jnp.int32, (8, LANE), 0)
            dsink = jnp.zeros((8, LANE), F32)
            dk_heads, dv_heads = [], []
            lse_tile = lse_ref[...]
            for hk in range(N_Q_HEADS // GROUP):
                kdup = _kv_dup(kp_ref, kc_ref, hk)
                ks = _kv_dup(kp_ref, kc_ref, hk, ATTN_SCALE)
                vdup = _kv_dup(vp_ref, vc_ref, hk)
                dk_acc = jnp.zeros((2 * BLOCK, LANE), F32)
                dv_acc = jnp.zeros((2 * BLOCK, LANE), F32)
                for k in (0, 1):
                    c = slice((2 * hk + k) * LANE, (2 * hk + k + 1) * LANE)
                    ag = ag_ref[:, c]
                    sg = _sigmoid(ag)
                    dzt = dz_ref[:, c]
                    yt = y_ref[:, c]
                    dag_ref[:, c] = (dzt * yt * (sg * (1.0 + ag * (1.0 - sg)))).astype(BF16)
                    dyt = dzt * (ag * sg)
                    q2s = q_ref[:, c] * ATTN_SCALE
                    dq_halves = []
                    for half in (0, 1):
                        head = GROUP * hk + 2 * k + half
                        qm, s = _head_scores(q2s, half, kdup, bias_ref[head])
                        lh = lse_tile[:, head:head + 1]
                        probs = jnp.exp(s - lh)
                        psink = jnp.exp(sm_ref[0, head] - lh)
                        dyh = jnp.where(_half_mask(dyt.shape, half), dyt, 0.0)
                        delta = jnp.sum(dyh * yt, axis=1, keepdims=True)
                        dyh16 = dyh.astype(BF16)
                        dp = lax.dot_general(dyh16, vdup, NT_DIMS, preferred_element_type=F32)
                        ds16 = (probs * (dp - delta)).astype(BF16)
                        dsink = dsink + jnp.where((row8 == 0) & (lane8 == head),
                                                  -jnp.sum(psink * delta, axis=0, keepdims=True), 0.0)
                        dq_halves.append(jnp.dot(ds16, ks, preferred_element_type=F32))
                        dk_acc = dk_acc + lax.dot_general(ds16, qm, TN_DIMS, preferred_element_type=F32)
                        dv_acc = dv_acc + lax.dot_general(probs.astype(BF16), dyh16, TN_DIMS, preferred_element_type=F32)
                    dq_ref[:, c] = jnp.where(_half_mask((BLOCK, LANE), 0), dq_halves[0], dq_halves[1]).astype(BF16)
                dk_heads.append(dk_acc + pltpu.roll(dk_acc, HEAD_DIM, 1))
                dv_heads.append(dv_acc + pltpu.roll(dv_acc, HEAD_DIM, 1))
            ds_ref[...] += dsink
            low = _half_mask((2 * BLOCK, LANE), 0)
            for tile in range(2):
                cols = slice(tile * LANE, (tile + 1) * LANE)
                dkt = jnp.where(low, dk_heads[2 * tile], dk_heads[2 * tile + 1])
                dvt = jnp.where(low, dv_heads[2 * tile], dv_heads[2 * tile + 1])
                dkv_ref[:, cols] = (ck_ref[:, cols] + dkt[0:BLOCK, :]).astype(BF16)
                dkv_ref[:, D_KV + tile * LANE:D_KV + (tile + 1) * LANE] = (cv_ref[:, cols] + dvt[0:BLOCK, :]).astype(BF16)
                ck_ref[:, cols] = dkt[BLOCK:2 * BLOCK, :]
                cv_ref[:, cols] = dvt[BLOCK:2 * BLOCK, :]

        @pl.when(n == nb)
        def _():
            dkv_ref[:, 0:D_KV] = ck_ref[...].astype(BF16)
            dkv_ref[:, D_KV:2 * D_KV] = cv_ref[...].astype(BF16)

        finish(n == nb)

    q_spec, k_prev, k_cur, v_prev, v_cur = _attn_specs(nb, True)
    wide = pl.BlockSpec((BLOCK, D_MODEL), lambda n: (jnp.minimum(n, nb - 1), 0))
    outs = pl.pallas_call(
        body, name=name, grid=(nb + 1,),
        in_specs=[pl.BlockSpec(memory_space=pltpu.SMEM), q_spec, k_prev, k_cur, v_prev, v_cur, wide, wide,
                  pl.BlockSpec((BLOCK, LANE), lambda n: (jnp.minimum(n, nb - 1), 0)), wide] + host.in_specs,
        out_specs=[wide, pl.BlockSpec((BLOCK, 2 * D_KV), lambda n: (jnp.maximum(n - 1, 0), 0)), wide,
                   pl.BlockSpec((8, LANE), lambda n: (0, 0))] + host.out_specs,
        out_shape=[jax.ShapeDtypeStruct((t, D_MODEL), BF16), jax.ShapeDtypeStruct((t, 2 * D_KV), BF16),
                   jax.ShapeDtypeStruct((t, D_MODEL), BF16), jax.ShapeDtypeStruct((8, LANE), F32)] + host.out_shapes,
        scratch_shapes=[pltpu.VMEM((BLOCK, D_KV), F32), pltpu.VMEM((BLOCK, D_KV), F32),
                        pltpu.VMEM((N_Q_HEADS, BLOCK, 2 * BLOCK), F32)] + host.scratch_shapes,
        compiler_params=_params("arbitrary"))(sm, qkv, qkv, qkv, qkv, qkv, ag_ml, y, lse, dz, *host.arrays)
    res, landed = host.results(outs, 4)
    return (*res, landed) if ride else tuple(res)


def _local_grads(x, target, p, late_weights=None, reduce_out=None, reduce_in=None):
    wt = p["wt"]
    h = _prenorm(x, p["pre_g"], "prenorm")
    splits = ((2 * D_RNN, F32), (D_MODEL + 2 * D_KV, BF16), (3 * D_MODEL, F32))
    if late_weights is None:
        rx_rg, qkv, ag_ml = _proj(h, wt, splits, "proj")
    else:
        (rx_rg, qkv, ag_ml), landed = _proj(h, wt, splits, "proj", ride=late_weights[0])
        p = {**p, **late_weights[1](landed)}
    lru =(p["cw"], p["cb"], p["wbd_a"], p["b_a"], p["wbd_x"], p["b_x"], p["lam"])
    y_rnn, z_rnn = _rnn_fwd(rx_rg, *lru, "rnn_fwd")
    y_attn, z_attn, lse = _attn_fwd(p["sm"], qkv, ag_ml, "attn_fwd")
    br_rnn, br_attn, merged, dout, dy, st_post = _branches_fwd(
        z_rnn, z_attn, ag_ml, p["b_gate"], p["w_rnn"], p["w_attn"], p["w_out"], x, target, p["post_g"], "branches_fwd")

    dbr_rnn, dbr_attn, d_ml, dz_rnn, dz_attn, st_merge = _branches_bwd(
        dout, br_rnn, br_attn, ag_ml, p["b_gate"], p["w_rnn"], p["w_attn"], p["w_out"], "branches_bwd")
    gw_out = _mm_tn(merged, dout, "gw_out")
    gw_rnn = _mm_tn(z_rnn, dbr_rnn, "gw_rnn")
    gw_attn = _mm_tn(z_attn, dbr_attn, "gw_attn")
    d_rx, d_rg, st_rnn, g_rg_a, g_rg_x = _rnn_bwd(rx_rg, y_rnn, dz_rnn, *lru, "rnn_bwd")
    red_out = red_in = None
    if reduce_out is None:
        dq, dkv, d_ag, st_sink = _attn_bwd(p["sm"], qkv, ag_ml, y_attn, lse, dz_attn, "attn_bwd")
    else:
        dq, dkv, d_ag, st_sink, red_out = _attn_bwd(p["sm"], qkv, ag_ml, y_attn, lse, dz_attn, "attn_bwd",
                                                    ride=reduce_out(gw_rnn, gw_attn, gw_out, g_rg_a, g_rg_x))

    segs = [d_rx, d_rg, dq, dkv, d_ag, d_ml]
    gwt = _mm_tn_seg(segs, h, "gw_in")
    if reduce_in is None:
        dh = _mm_nn_seg(segs, wt, "d_h")
    else:
        dh, red_in = _mm_nn_seg(segs, wt, "d_h", ride=reduce_in(gwt))
    grad_x, st_pre = _prenorm_bwd(x, p["pre_g"], dh, dy, "prenorm_bwd")
    return dict(grad_x=grad_x, gwt=gwt, gw_rnn=gw_rnn, gw_attn=gw_attn, gw_out=gw_out,
                st_post=st_post, st_merge=st_merge, st_rnn=st_rnn, st_sink=st_sink, st_pre=st_pre,
                g_rg_a=g_rg_a, g_rg_x=g_rg_x, red_out=red_out, red_in=red_in)


def _place():
    x, y, c = lax.axis_index("x"), lax.axis_index("y"), lax.axis_index("c")
    return x, y, c


def _gather_ride(shards):
    n = len(shards)

    def copies(ins, outs, sems):
        send_sems, recv_sems, local_sems = sems
        x, y, c = _place()
        me, sibling = (x, y, c), (x, y, 1 - c)
        chips = [(1 - x, y), (x, 1 - y), (1 - x, 1 - y)]

        def slot(a, dev):
            return outs[a].at[4 * dev[0] + 2 * dev[1] + dev[2]]

        def copy(a, k, block, to, src=None):
            return pltpu.make_async_remote_copy(
                src_ref=slot(a, block) if src is None else src, dst_ref=slot(a, block),
                send_sem=send_sems.at[a, k], recv_sem=recv_sems.at[a, k], device_id=to, device_id_type=MESH)

        mine = [pltpu.make_async_copy(ins[a], slot(a, me), local_sems.at[a]) for a in range(n)]
        first = []
        for a in range(n):
            first.append(copy(a, 0, me, sibling, src=ins[a]))
            first += [copy(a, 1 + j, me, (*chip, c), src=ins[a]) for j, chip in enumerate(chips)]
        return me, sibling, chips, c, copy, mine, first

    def start(ins, outs, sems):
        *_, mine, first = copies(ins, outs, sems)
        for cp in mine + first:
            cp.start()

    def finish(ins, outs, sems):
        me, sibling, chips, c, copy, mine, first = copies(ins, outs, sems)
        passed = []
        for j, chip in enumerate(chips):
            for a in range(n):
                copy(a, 1 + j, (*chip, c), me).wait_recv()
                cp = copy(a, 4 + j, (*chip, c), sibling)
                cp.start()
                passed.append(cp)
        for a in range(n):
            copy(a, 0, sibling, me).wait_recv()
            for j, chip in enumerate(chips):
                copy(a, 4 + j, (*chip, 1 - c), me).wait_recv()
        for cp in first + passed:
            cp.wait_send()
        for cp in mine:
            cp.wait()

    return _Ride(
        shards, [jax.ShapeDtypeStruct((N_DEV, *s.shape), s.dtype) for s in shards],
        [pltpu.SemaphoreType.DMA((n, 7)), pltpu.SemaphoreType.DMA((n, 7)), pltpu.SemaphoreType.DMA((n,))],
        start, finish)


def _exchange_sibling(scatter, whole, name):
    ns, nw = len(scatter), len(whole)

    def body(*refs):
        s_in, w_in = refs[:ns], refs[ns:ns + nw]
        s_out, w_out = refs[ns + nw:2 * ns + nw], refs[2 * ns + nw:2 * (ns + nw)]
        send_sems, recv_sems = refs[2 * (ns + nw):]
        x, y, c = _place()
        sibling = (x, y, 1 - c)
        copies = []
        for a in range(ns):
            for chip in range(N_CHIP):
                copies.append(pltpu.make_async_remote_copy(
                    src_ref=s_in[a].at[2 * chip + (1 - c)], dst_ref=s_out[a].at[chip],
                    send_sem=send_sems.at[a * N_CHIP + chip], recv_sem=recv_sems.at[a * N_CHIP + chip],
                    device_id=sibling, device_id_type=MESH))
        for a in range(nw):
            copies.append(pltpu.make_async_remote_copy(
                src_ref=w_in[a], dst_ref=w_out[a],
                send_sem=send_sems.at[ns * N_CHIP + a], recv_sem=recv_sems.at[ns * N_CHIP + a],
                device_id=sibling, device_id_type=MESH))
        for cp in copies:
            cp.start()
        for cp in copies:
            cp.wait()

    n_sem = ns * N_CHIP + nw
    return pl.pallas_call(
        body, name=name,
        in_specs=[ANY] * (ns + nw), out_specs=[ANY] * (ns + nw),
        out_shape=([jax.ShapeDtypeStruct((N_CHIP, *s.shape[1:]), s.dtype) for s in scatter]
                   + [jax.ShapeDtypeStruct(s.shape, s.dtype) for s in whole]),
        scratch_shapes=[pltpu.SemaphoreType.DMA((n_sem,)), pltpu.SemaphoreType.DMA((n_sem,))],
    )(*scatter, *whole)


def _chips_ride(scatter, whole):
    ns, nw = len(scatter), len(whole)
    n = ns + nw

    def copies(ins, outs, sems):
        send_sems, recv_sems, local_sems = sems
        x, y, c = _place()
        own = 2 * x + y
        chips = [(1 - x, y), (x, 1 - y), (1 - x, 1 - y)]

        def src(a, chip_idx):
            return ins[a].at[chip_idx] if a < ns else ins[a]

        local = [pltpu.make_async_copy(src(a, own), outs[a].at[own], local_sems.at[a]) for a in range(n)]
        sent = [pltpu.make_async_remote_copy(
            src_ref=src(a, 2 * chip[0] + chip[1]), dst_ref=outs[a].at[own],
            send_sem=send_sems.at[a, j], recv_sem=recv_sems.at[a, own], device_id=(*chip, c), device_id_type=MESH)
            for a in range(n) for j, chip in enumerate(chips)]
        return chips, c, local, sent

    def start(ins, outs, sems):
        _, _, local, sent = copies(ins, outs, sems)
        for cp in local + sent:
            cp.start()

    def finish(ins, outs, sems):
        send_sems, recv_sems, _ = sems
        chips, c, local, sent = copies(ins, outs, sems)
        for a in range(n):
            for chip in chips:
                k = 2 * chip[0] + chip[1]
                pltpu.make_async_remote_copy(
                    src_ref=outs[a].at[k], dst_ref=outs[a].at[k], send_sem=send_sems.at[a, 0],
                    recv_sem=recv_sems.at[a, k], device_id=(*chip, c), device_id_type=MESH).wait_recv()
        for cp in sent:
            cp.wait_send()
        for cp in local:
            cp.wait()

    return _Ride(
        list(scatter) + list(whole),
        [jax.ShapeDtypeStruct(s.shape, s.dtype) for s in scatter]
        + [jax.ShapeDtypeStruct((N_CHIP, *s.shape), s.dtype) for s in whole],
        [pltpu.SemaphoreType.DMA((n, 3)), pltpu.SemaphoreType.DMA((n, N_CHIP)), pltpu.SemaphoreType.DMA((n,))],
        start, finish)


def _pair_sum_scatter(parts, recvs, core, name):
    na = len(parts)
    _, r, cdim = parts[0].shape
    tr = min(r, 416 if r % 416 == 0 else 128)

    def body(core_ref, *refs):
        del core_ref
        for a in range(na):
            refs[2 * na + a][...] = (refs[a][...] + refs[na + a][...]).astype(BF16)

    blk = (None, tr, cdim)
    mine = pl.BlockSpec(blk, lambda k, i, core_ref: (2 * k + core_ref[0], i, 0))
    slot = pl.BlockSpec(blk, lambda k, i, core_ref: (k, i, 0))
    return pl.pallas_call(
        body, name=name,
        grid_spec=pltpu.PrefetchScalarGridSpec(
            num_scalar_prefetch=1, grid=(N_CHIP, r // tr),
            in_specs=[mine] * na + [slot] * na, out_specs=[slot] * na),
        out_shape=[jax.ShapeDtypeStruct((N_CHIP, r, cdim), BF16)] * na,
        compiler_params=_params("parallel", "parallel"))(core, *parts, *recvs)


def _pair_sum_whole(mine, recvs, name):
    na = len(mine)

    def body(*refs):
        for a in range(na):
            refs[2 * na + a][...] = refs[a][...] + refs[na + a][...]

    return pl.pallas_call(body, name=name, out_shape=[jax.ShapeDtypeStruct(m.shape, F32) for m in mine])(*mine, *recvs)


def _allreduce_small(pack, name):
    shape = pack.shape

    def body(x_ref, o_ref, sib_ref, chip_ref, send_sems, recv_sems):
        x, y, c = _place()
        own = 2 * x + y
        chips = [(1 - x, y), (x, 1 - y), (1 - x, 1 - y)]
        to_sibling = pltpu.make_async_remote_copy(
            src_ref=x_ref, dst_ref=sib_ref, send_sem=send_sems.at[0], recv_sem=recv_sems.at[0],
            device_id=(x, y, 1 - c), device_id_type=MESH)
        to_sibling.start()
        to_sibling.wait()
        chip_ref[own] = x_ref[...] + sib_ref[...]
        sent = [pltpu.make_async_remote_copy(
            src_ref=chip_ref.at[own], dst_ref=chip_ref.at[own], send_sem=send_sems.at[1 + j],
            recv_sem=recv_sems.at[1 + own], device_id=(*chip, c), device_id_type=MESH) for j, chip in enumerate(chips)]
        for cp in sent:
            cp.start()
        for chip in chips:
            k = 2 * chip[0] + chip[1]
            pltpu.make_async_remote_copy(
                src_ref=chip_ref.at[k], dst_ref=chip_ref.at[k], send_sem=send_sems.at[1],
                recv_sem=recv_sems.at[1 + k], device_id=(*chip, c), device_id_type=MESH).wait_recv()
        for cp in sent:
            cp.wait_send()
        o_ref[...] = (chip_ref[0] + chip_ref[1]) + (chip_ref[2] + chip_ref[3])

    return pl.pallas_call(
        body, name=name, out_shape=jax.ShapeDtypeStruct(shape, F32),
        in_specs=[pl.BlockSpec(memory_space=pltpu.VMEM)], out_specs=pl.BlockSpec(memory_space=pltpu.VMEM),
        scratch_shapes=[pltpu.VMEM(shape, F32), pltpu.VMEM((N_CHIP, *shape), F32),
                        pltpu.SemaphoreType.DMA((4,)), pltpu.SemaphoreType.DMA((1 + N_CHIP,))],
    )(pack)


def _adamw(g, w, m, v):
    m = ADAM_B1 * m + (1.0 - ADAM_B1) * g
    v = ADAM_B2 * v + (1.0 - ADAM_B2) * (g * g)
    m_hat = m / (1.0 - ADAM_B1 ** ADAM_STEP)
    v_hat = v / (1.0 - ADAM_B2 ** ADAM_STEP)
    delta = -ADAM_LR * (m_hat / (jnp.sqrt(v_hat) + ADAM_EPS) + ADAM_WD * w)
    return delta, m, v


def _adam_parts(parts, w, m, v, name, tr=None):
    npart, r, c = parts.shape
    tr = r if tr is None else min(tr, r)

    def body(p_ref, w_ref, m_ref, v_ref, g_ref, d_ref, nm_ref, nv_ref):
        g = p_ref[0].astype(F32)
        for k in range(1, npart):
            g = g + p_ref[k].astype(F32)
        g_ref[...] = g
        d_ref[...], nm_ref[...], nv_ref[...] = _adamw(g, w_ref[...], m_ref[...], v_ref[...])

    tile = pl.BlockSpec((tr, c), lambda i: (i, 0))
    return pl.pallas_call(
        body, name=name, grid=(r // tr,),
        in_specs=[pl.BlockSpec((npart, tr, c), lambda i: (0, i, 0)), tile, tile, tile],
        out_specs=[tile] * 4, out_shape=[jax.ShapeDtypeStruct((r, c), F32)] * 4,
        compiler_params=_params("parallel"))(parts, w, m, v)


def _block_diag(w):
    w4 = w.reshape(N_GROUPS, 4, RNN_BLOCK_W, RNN_BLOCK_W)
    eye = jnp.eye(4, dtype=w.dtype)
    return jnp.einsum("gbij,bc->gbicj", w4, eye).reshape(N_GROUPS, GROUP_W, GROUP_W).astype(BF16)


SMALL_ROWS = 16
ROW_PRE_G, ROW_BGATE, ROW_CONV_B, ROW_B_A, ROW_B_X, ROW_LAM, ROW_POST_G, ROW_LOSS, ROW_SINKS, ROW_CONV_W = 0, 1, 3, 4, 5, 6, 7, 8, 9, 10


def _pack_stats(st_pre, st_merge, st_rnn, st_post, st_sink, name):
    d = D_MODEL

    def body(pre_ref, mg_ref, rnn_ref, post_ref, sink_ref, o_ref):
        rnn = rnn_ref[...]
        sinks = jnp.concatenate([sink_ref[0:1, :], jnp.zeros((1, d - LANE), F32)], axis=1)
        o_ref[0:8, :] = _rows8([pre_ref[0:1, :], mg_ref[0:1, :], mg_ref[1:2, :], rnn[0:1], rnn[1:2], rnn[2:3], rnn[3:4],
                                post_ref[0:1, :]], d)
        o_ref[8:16, :] = _rows8([post_ref[1:2, :], sinks, rnn[4:5], rnn[5:6], rnn[6:7], rnn[7:8]], d)

    return pl.pallas_call(body, name=name, out_shape=jax.ShapeDtypeStruct((SMALL_ROWS, d), F32))(
        st_pre, st_merge, st_rnn, st_post, st_sink)


def _adam_small(total, w, m, v, name):
    d = D_MODEL
    n_in = len(w)

    def pack(refs):
        pre, bg, cbias, ba, bx, lam, post, sinks = [r[...] for r in refs]
        top = _rows8([pre, bg[:, 0:d], bg[:, d:2 * d], cbias, ba, bx, lam, post], d)
        return jnp.concatenate([top, _rows8([jnp.zeros((1, d), F32), sinks], d)], axis=0)

    def body(*refs):
        p_ref = refs[0]
        w_refs, m_refs, v_refs = (refs[1 + k * n_in:1 + (k + 1) * n_in] for k in range(3))
        outs = refs[1 + 3 * n_in:]
        g = p_ref[...]
        res = (g,) + _adamw(g, pack(w_refs), pack(m_refs), pack(v_refs))
        for k in range(4):
            outs[k][...] = res[k]
            outs[4 + k][...] = jnp.concatenate([res[k][ROW_BGATE:ROW_BGATE + 1], res[k][ROW_BGATE + 1:ROW_BGATE + 2]], axis=1)

    return pl.pallas_call(
        body, name=name,
        out_shape=[jax.ShapeDtypeStruct((SMALL_ROWS, d), F32)] * 4 + [jax.ShapeDtypeStruct((1, 2 * d), F32)] * 4,
    )(total, *w, *m, *v)


def kernel(x, pre_norm_g, w_in, b_gate, conv_w, conv_b, w_rg_a, b_rg_a, w_rg_x, b_rg_x, lru_lambda, attn_sinks, w_rnn_out, w_attn_out, w_out, post_norm_g, loss_target, m_pre_norm_g, m_w_in, m_b_gate, m_conv_w, m_conv_b, m_w_rg_a, m_b_rg_a, m_w_rg_x, m_b_rg_x, m_lru_lambda, m_attn_sinks, m_w_rnn_out, m_w_attn_out, m_w_out, m_post_norm_g, v_pre_norm_g, v_w_in, v_b_gate, v_conv_w, v_conv_b, v_w_rg_a, v_b_rg_a, v_w_rg_x, v_b_rg_x, v_lru_lambda, v_attn_sinks, v_w_rnn_out, v_w_attn_out, v_w_out, v_post_norm_g):
    cx, cy, cc = _place()
    dev = 4 * cx + 2 * cy + cc
    core = jnp.reshape(cc, (1,)).astype(jnp.int32)

    w_in_t, m_in_t, v_in_t = (jnp.transpose(a[0]) for a in (w_in, m_w_in, v_w_in))
    wt_shard = w_in_t.astype(BF16)
    (wt_all,) = _run_ride(_gather_ride([wt_shard]), "gather_w_in")
    heads = jnp.arange(1, N_Q_HEADS + 1, dtype=F32)
    slopes = jnp.exp2(-ALIBI_MAX_BIAS * heads / N_Q_HEADS)
    b_a = b_rg_a.reshape(1, D_RNN)
    b_x = b_rg_x.reshape(1, D_RNN)
    p = dict(
        wt=wt_all.reshape(D_IN, D_MODEL),
        pre_g=pre_norm_g, post_g=post_norm_g, b_gate=b_gate, cb=conv_b,
        wbd_a=_block_diag(w_rg_a[0]), b_a=b_a, wbd_x=_block_diag(w_rg_x[0]), b_x=b_x, lam=lru_lambda,
        sm=jnp.pad(attn_sinks, ((0, 1), (0, 0))) + jnp.pad(slopes[None, :], ((1, 0), (0, 0))))

    def late_unpack(landed):
        w_rnn_all, w_attn_all, w_out_all, cw_all = landed
        return dict(w_rnn=w_rnn_all.reshape(D_RNN, D_MODEL), w_attn=w_attn_all.reshape(D_MODEL, D_MODEL),
                    w_out=w_out_all.reshape(D_MODEL, D_MODEL), cw=jnp.transpose(cw_all, (1, 0, 2)).reshape(4, D_RNN))

    late_weights = (_gather_ride([w_rnn_out[0].astype(BF16), w_attn_out[0].astype(BF16), w_out[0].astype(BF16), conv_w[0]]),
                    late_unpack)
    flat = (RNN_BLOCKS * RNN_BLOCK_W, RNN_BLOCK_W)

    def reduce_out(gw_rnn, gw_attn, gw_out, g_rg_a, g_rg_x):
        scatter = [gw.reshape(N_DEV, SHARD_OUT, D_MODEL) for gw in (gw_rnn, gw_attn, gw_out)]
        whole = [g_rg_a.reshape(flat), g_rg_x.reshape(flat)]
        recv = _exchange_sibling(scatter, whole, "sibling_out")
        return _chips_ride(_pair_sum_scatter(scatter, recv[:3], core, "pair_out"),
                           _pair_sum_whole(whole, recv[3:], "pair_rg"))

    def reduce_in(gwt):
        scatter = [gwt.reshape(N_DEV, SHARD_IN, D_MODEL)]
        recv = _exchange_sibling(scatter, [], "sibling_in")
        return _chips_ride(_pair_sum_scatter(scatter, recv, core, "pair_in"), [])

    g = _local_grads(x[0], loss_target[0], p, late_weights, reduce_out, reduce_in)
    small = _allreduce_small(
        _pack_stats(g["st_pre"], g["st_merge"], g["st_rnn"], g["st_post"], g["st_sink"], "pack_stats"), "allreduce_small")

    out = {}
    red = g["red_out"]
    out["w_in"] = [jnp.transpose(o) for o in _adam_parts(g["red_in"][0], w_in_t, m_in_t, v_in_t, "adam_w_in", tr=SHARD_IN // 2)]
    out["w_rnn_out"] = _adam_parts(red[0], w_rnn_out[0], m_w_rnn_out[0], v_w_rnn_out[0], "adam_w_rnn_out")
    out["w_attn_out"] = _adam_parts(red[1], w_attn_out[0], m_w_attn_out[0], v_w_attn_out[0], "adam_w_attn_out")
    out["w_out"] = _adam_parts(red[2], w_out[0], m_w_out[0], v_w_out[0], "adam_w_out")
    out["w_rg_a"] = _adam_parts(red[3], w_rg_a.reshape(flat), m_w_rg_a.reshape(flat), v_w_rg_a.reshape(flat), "adam_w_rg_a", tr=256)
    out["w_rg_x"] = _adam_parts(red[4], w_rg_x.reshape(flat), m_w_rg_x.reshape(flat), v_w_rg_x.reshape(flat), "adam_w_rg_x", tr=256)

    def rows(pre, bg, cbias, ba, bx, lam, post, sinks):
        return (pre, bg, cbias, ba.reshape(1, D_RNN), bx.reshape(1, D_RNN), lam, post,
                jnp.pad(sinks, ((0, 0), (0, D_MODEL - N_Q_HEADS))))

    small_out = _adam_small(
        small,
        rows(pre_norm_g, b_gate, conv_b, b_rg_a, b_rg_x, lru_lambda, post_norm_g, attn_sinks),
        rows(m_pre_norm_g, m_b_gate, m_conv_b, m_b_rg_a, m_b_rg_x, m_lru_lambda, m_post_norm_g, m_attn_sinks),
        rows(v_pre_norm_g, v_b_gate, v_conv_b, v_b_rg_a, v_b_rg_x, v_lru_lambda, v_post_norm_g, v_attn_sinks),
        "adam_small")
    packed, bgate_out = small_out[:4], small_out[4:]
    g_cw = lax.dynamic_slice(packed[0][ROW_CONV_W:ROW_CONV_W + 4], (0, dev * SHARD_OUT), (4, SHARD_OUT))
    out["conv_w"] = _adam_parts(g_cw[None], conv_w[0], m_conv_w[0], v_conv_w[0], "adam_conv_w")

    def unpack(kind, name):
        if name == "b_gate":
            return bgate_out[kind]
        row = dict(pre_norm_g=ROW_PRE_G, conv_b=ROW_CONV_B, b_rg_a=ROW_B_A, b_rg_x=ROW_B_X, lru_lambda=ROW_LAM,
                   post_norm_g=ROW_POST_G, attn_sinks=ROW_SINKS)[name]
        r = packed[kind][row:row + 1]
        if name == "attn_sinks":
            return r[:, 0:N_Q_HEADS]
        if name in ("b_rg_a", "b_rg_x"):
            return r.reshape(1, RNN_BLOCKS, RNN_BLOCK_W)
        return r

    shapes = dict(w_in=(1, D_MODEL, SHARD_IN), w_rnn_out=(1, SHARD_OUT, D_MODEL), w_attn_out=(1, SHARD_OUT, D_MODEL),
                  w_out=(1, SHARD_OUT, D_MODEL), w_rg_a=(1, RNN_BLOCKS, RNN_BLOCK_W, RNN_BLOCK_W),
                  w_rg_x=(1, RNN_BLOCKS, RNN_BLOCK_W, RNN_BLOCK_W), conv_w=(1, 4, SHARD_OUT))
    weights = ["pre_norm_g", "w_in", "b_gate", "conv_w", "conv_b", "w_rg_a", "b_rg_a", "w_rg_x", "b_rg_x",
               "lru_lambda", "attn_sinks", "w_rnn_out", "w_attn_out", "w_out", "post_norm_g"]
    results = []
    for kind in range(4):
        for name in weights:
            if name in out:
                results.append(out[name][kind].reshape(shapes[name]))
            else:
                results.append(unpack(kind, name))
    loss = 0.5 / D_MODEL * jnp.sum(packed[0][ROW_LOSS])
    return (loss, g["grad_x"][None], *results)
```

```python
import functools

import jax
import jax.numpy as jnp
from jax import lax
from jax.experimental import pallas as pl
from jax.experimental.pallas import tpu as pltpu

F32 = jnp.float32
BF16 = jnp.bfloat16

D_MODEL = 1024
D_RNN = 1024
RNN_BLOCKS = 16
RNN_BLOCK_W = 64
LRU_C = 8.0
N_Q_HEADS = 16
HEAD_DIM = 64
D_KV = 256
BLOCK = 128
ALIBI_MAX_BIAS = 8.0
EPS = 1e-6
D_IN = 6656
N_DEV = 8
N_CHIP = 4
SHARD_IN = D_IN // N_DEV
SHARD_OUT = D_MODEL // N_DEV
ATTN_SCALE = HEAD_DIM ** -0.5
MASKED = -1e30

ADAM_LR = 0.001
ADAM_B1 = 0.9
ADAM_B2 = 0.999
ADAM_EPS = 1e-08
ADAM_WD = 0.01
ADAM_STEP = 10

VMEM_LIMIT_BYTES = 52 * 1024 * 1024
LANE = 128
GROUP_W = 256
N_GROUPS = D_RNN // GROUP_W
SEG_CHUNK = 512

NT_DIMS = (((1,), (1,)), ((), ()))
TN_DIMS = (((0,), (0,)), ((), ()))
MESH = pl.DeviceIdType.MESH
ANY = pl.BlockSpec(memory_space=pl.ANY)


def _params(*semantics):
    return pltpu.CompilerParams(dimension_semantics=semantics, vmem_limit_bytes=VMEM_LIMIT_BYTES)


def _sigmoid(x):
    return 0.5 * jnp.tanh(0.5 * x) + 0.5


def _log1p(e):
    u = 1.0 + e
    den = jnp.where(u == 1.0, 1.0, u - 1.0)
    return jnp.where(u == 1.0, e, jnp.log(u) * (e / den))


def _softplus(z):
    return jnp.maximum(z, 0.0) + _log1p(jnp.exp(-jnp.abs(z)))


def _rows8(rows, width):
    idx = lax.broadcasted_iota(jnp.int32, (8, width), 0)
    out = jnp.zeros((8, width), F32)
    for r, v in enumerate(rows):
        out = jnp.where(idx == r, v, out)
    return out


class _Ride:
    def __init__(self, arrays, out_shapes, scratch_shapes, start, finish):
        self.arrays, self.out_shapes, self.scratch_shapes = list(arrays), list(out_shapes), list(scratch_shapes)
        self.start, self.finish = start, finish


class _Hosted:
    def __init__(self, ride, n_in, n_out, n_scratch=0):
        self.ride = ride
        self.sizes = (n_in, len(ride.arrays) if ride else 0, n_out, len(ride.out_shapes) if ride else 0, n_scratch)
        self.arrays = ride.arrays if ride else []
        self.in_specs = [ANY] * len(self.arrays)
        self.out_shapes = ride.out_shapes if ride else []
        self.out_specs = [ANY] * len(self.out_shapes)
        self.scratch_shapes = ride.scratch_shapes if ride else []

    def split(self, refs):
        n_in, r_in, n_out, r_out, n_scr = self.sizes
        cuts = [0, n_in, n_in + r_in, n_in + r_in + n_out, n_in + r_in + n_out + r_out, n_in + r_in + n_out + r_out + n_scr]
        host_in, ride_in, host_out, ride_out, host_scr = (refs[cuts[k]:cuts[k + 1]] for k in range(5))
        ride_scr = refs[cuts[5]:]

        def start(when):
            if self.ride is not None:
                pl.when(when)(lambda: self.ride.start(ride_in, ride_out, ride_scr))

        def finish(when):
            if self.ride is not None:
                pl.when(when)(lambda: self.ride.finish(ride_in, ride_out, ride_scr))

        return tuple(host_in) + tuple(host_out) + tuple(host_scr), start, finish

    def results(self, outs, n_out):
        outs = list(outs) if isinstance(outs, (list, tuple)) else [outs]
        return outs[:n_out], outs[n_out:]


def _run_ride(ride, name):
    n_in, n_out = len(ride.arrays), len(ride.out_shapes)

    def body(*refs):
        ins, outs, sems = refs[:n_in], refs[n_in:n_in + n_out], refs[n_in + n_out:]
        ride.start(ins, outs, sems)
        ride.finish(ins, outs, sems)

    return pl.pallas_call(
        body, name=name, in_specs=[ANY] * n_in, out_specs=[ANY] * n_out,
        out_shape=ride.out_shapes, scratch_shapes=ride.scratch_shapes)(*ride.arrays)


def _load_resident(w_hbm, w_vmem, sems, first):
    def piece(c):
        rows = pl.ds(c * SEG_CHUNK, SEG_CHUNK)
        return pltpu.make_async_copy(w_hbm.at[rows], w_vmem.at[rows], sems.at[c])

    @pl.when(first)
    def _():
        for c in range(w_vmem.shape[0] // SEG_CHUNK):
            piece(c).start()

    def ready(c):
        @pl.when(first)
        def _():
            piece(c).wait()

    return ready


def _proj(h, wt, widths_dtypes, name, tm=512, ride=None):
    t, k = h.shape
    n = wt.shape[0]
    tm = min(tm, t)
    ni = t // tm
    n_out = len(widths_dtypes)
    n_pieces = n // SEG_CHUNK
    host = _Hosted(ride, 2, n_out, 3)

    def body(*refs):
        host_refs, start, finish = host.split(refs)
        h_ref, wt_hbm = host_refs[:2]
        o_refs = host_refs[2:2 + n_out]
        w_vmem, stage_ref, sems = host_refs[2 + n_out:]
        i = pl.program_id(0)
        start(i == 0)

        def piece(c):
            return pltpu.make_async_copy(wt_hbm.at[pl.ds(c * SEG_CHUNK, SEG_CHUNK)], stage_ref.at[c % 2], sems.at[c % 2])

        @pl.when(i == 0)
        def _():
            piece(0).start()
            piece(1).start()
            for c in range(n_pieces):
                piece(c).wait()
                w_vmem[:, c * SEG_CHUNK:(c + 1) * SEG_CHUNK] = stage_ref[c % 2].T
                if c + 2 < n_pieces:
                    piece(c + 2).start()

        hv = h_ref[...]
        c = 0
        for o_ref, (width, _) in zip(o_refs, widths_dtypes):
            for kk in range(width // SEG_CHUNK):
                o_ref[:, kk * SEG_CHUNK:(kk + 1) * SEG_CHUNK] = jnp.dot(
                    hv, w_vmem[:, c * SEG_CHUNK:(c + 1) * SEG_CHUNK], preferred_element_type=F32).astype(o_ref.dtype)
                c += 1
        finish(i == ni - 1)

    outs = pl.pallas_call(
        body, name=name, grid=(ni,),
        in_specs=[pl.BlockSpec((tm, k), lambda i: (i, 0)), ANY] + host.in_specs,
        out_specs=[pl.BlockSpec((tm, w), lambda i: (i, 0)) for w, _ in widths_dtypes] + host.out_specs,
        out_shape=[jax.ShapeDtypeStruct((t, w), dt) for w, dt in widths_dtypes] + host.out_shapes,
        scratch_shapes=[pltpu.VMEM((k, n), wt.dtype), pltpu.VMEM((2, SEG_CHUNK, k), wt.dtype),
                        pltpu.SemaphoreType.DMA((2,))] + host.scratch_shapes,
        compiler_params=_params("arbitrary"))(h, wt, *host.arrays)
    res, landed = host.results(outs, n_out)
    return (res, landed) if ride else res


def _mm_tn(a, b, name, tm=512, tk=4096):
    ktok, m = a.shape
    n = b.shape[1]
    tk = min(tk, ktok)

    def body(a_ref, b_ref, o_ref):
        @pl.when(pl.program_id(1) == 0)
        def _():
            o_ref[...] = jnp.zeros_like(o_ref)

        o_ref[...] += lax.dot_general(a_ref[...], b_ref[...], TN_DIMS, preferred_element_type=F32)

    return pl.pallas_call(
        body, name=name, grid=(m // tm, ktok // tk),
        in_specs=[pl.BlockSpec((tk, tm), lambda i, kk: (kk, i)), pl.BlockSpec((tk, n), lambda i, kk: (kk, 0))],
        out_specs=pl.BlockSpec((tm, n), lambda i, kk: (i, 0)),
        out_shape=jax.ShapeDtypeStruct((m, n), F32),
        compiler_params=_params("parallel", "arbitrary"))(a, b)


def _segment_chunks(segs):
    bounds = [0]
    for s in segs:
        bounds.append(bounds[-1] + s.shape[1] // SEG_CHUNK)
    return bounds


def _seg_index_map(lo, hi):
    return lambda c, kk: (jnp.where((c >= lo) & (c < hi), kk, 0), jnp.clip(c - lo, 0, hi - lo - 1))


def _mm_nn_seg(segs, wt, name, tm=512, ride=None):
    m = segs[0].shape[0]
    rows, n = wt.shape
    tm = min(tm, m)
    bounds = _segment_chunks(segs)
    n_seg = len(segs)
    ni = m // tm
    host = _Hosted(ride, n_seg + 1, 1, 2)

    def body(*refs):
        host_refs, start, finish = host.split(refs)
        a_refs, wt_hbm, o_ref, wt_vmem, sems = host_refs[:n_seg], host_refs[n_seg], host_refs[n_seg + 1], host_refs[n_seg + 2], host_refs[n_seg + 3]
        i = pl.program_id(0)
        start(i == 0)
        ready = _load_resident(wt_hbm, wt_vmem, sems, i == 0)
        acc = None
        for s in range(n_seg):
            for c in range(bounds[s], bounds[s + 1]):
                ready(c)
            part = jnp.dot(a_refs[s][...], wt_vmem[bounds[s] * SEG_CHUNK:bounds[s + 1] * SEG_CHUNK, :], preferred_element_type=F32)
            acc = part if acc is None else acc + part
        o_ref[...] = acc
        finish(i == ni - 1)

    outs = pl.pallas_call(
        body, name=name, grid=(ni,),
        in_specs=[pl.BlockSpec((tm, sg.shape[1]), lambda i: (i, 0)) for sg in segs] + [ANY] + host.in_specs,
        out_specs=[pl.BlockSpec((tm, n), lambda i: (i, 0))] + host.out_specs,
        out_shape=[jax.ShapeDtypeStruct((m, n), F32)] + host.out_shapes,
        scratch_shapes=[pltpu.VMEM((rows, n), wt.dtype), pltpu.SemaphoreType.DMA((rows // SEG_CHUNK,))] + host.scratch_shapes,
        compiler_params=_params("arbitrary"))(*segs, wt, *host.arrays)
    (res,), landed = host.results(outs, 1)
    return (res, landed) if ride else res


def _mm_tn_seg(segs, b, name, tk=2048):
    ktok = segs[0].shape[0]
    n = b.shape[1]
    tk = min(tk, ktok)
    bounds = _segment_chunks(segs)
    n_seg = len(segs)

    def body(*refs):
        a_refs, b_ref, o_ref = refs[:n_seg], refs[n_seg], refs[n_seg + 1]
        c = pl.program_id(0)
        kk = pl.program_id(1)

        @pl.when(kk == 0)
        def _():
            o_ref[...] = jnp.zeros_like(o_ref)

        rows = b_ref[pl.ds(pl.multiple_of(kk * tk, tk), tk), :]
        for s in range(n_seg):
            @pl.when((c >= bounds[s]) & (c < bounds[s + 1]))
            def _(s=s):
                o_ref[...] += lax.dot_general(a_refs[s][...], rows, TN_DIMS, preferred_element_type=F32)

    in_specs = [pl.BlockSpec((tk, SEG_CHUNK), _seg_index_map(bounds[s], bounds[s + 1])) for s in range(n_seg)]
    in_specs.append(pl.BlockSpec((ktok, n), lambda c, kk: (0, 0)))
    return pl.pallas_call(
        body, name=name, grid=(bounds[-1], ktok // tk),
        in_specs=in_specs, out_specs=pl.BlockSpec((SEG_CHUNK, n), lambda c, kk: (c, 0)),
        out_shape=jax.ShapeDtypeStruct((bounds[-1] * SEG_CHUNK, n), F32),
        compiler_params=_params("parallel", "arbitrary"))(*segs, b)


def _prenorm(x, g, name, tm=512):
    t, d = x.shape
    tm = min(tm, t)

    def body(x_ref, g_ref, h_ref):
        x = x_ref[...]
        r = lax.rsqrt(jnp.mean(x * x, axis=-1, keepdims=True) + EPS)
        h_ref[...] = (x * r * g_ref[...]).astype(BF16)

    return pl.pallas_call(
        body, name=name, grid=(t // tm,),
        in_specs=[pl.BlockSpec((tm, d), lambda i: (i, 0)), pl.BlockSpec((1, d), lambda i: (0, 0))],
        out_specs=pl.BlockSpec((tm, d), lambda i: (i, 0)),
        out_shape=jax.ShapeDtypeStruct((t, d), BF16),
        compiler_params=_params("parallel"))(x, g)


def _prenorm_bwd(x, g, dh, dy, name, tm=512):
    t, d = x.shape
    tm = min(tm, t)

    def body(x_ref, g_ref, dh_ref, dy_ref, gx_ref, st_ref):
        @pl.when(pl.program_id(0) == 0)
        def _():
            st_ref[...] = jnp.zeros_like(st_ref)

        x = x_ref[...]
        r = lax.rsqrt(jnp.mean(x * x, axis=-1, keepdims=True) + EPS)
        xn = x * r
        dh = dh_ref[...]
        dxn = dh * g_ref[...]
        dx = r * (dxn - xn * jnp.mean(dxn * xn, axis=-1, keepdims=True))
        gx_ref[...] = dy_ref[...] + dx
        st_ref[...] += _rows8([jnp.sum(dh * xn, axis=0, keepdims=True)], d)

    tile = pl.BlockSpec((tm, d), lambda i: (i, 0))
    return pl.pallas_call(
        body, name=name, grid=(t // tm,),
        in_specs=[tile, pl.BlockSpec((1, d), lambda i: (0, 0)), tile, tile],
        out_specs=[tile, pl.BlockSpec((8, d), lambda i: (0, 0))],
        out_shape=[jax.ShapeDtypeStruct((t, d), F32), jax.ShapeDtypeStruct((8, d), F32)],
        compiler_params=_params("arbitrary"))(x, g, dh, dy)


def _branches_fwd(z_rnn, z_attn, ag_ml, b_gate, w_rnn, w_attn, w_out, x, target, g_post, name, tm=512):
    t, d = x.shape
    tm = min(tm, t)

    def body(zr_ref, za_ref, lr_ref, la_ref, br_ref, ba_ref, wr_ref, wa_ref, wo_ref, x_ref, t_ref, g_ref,
             brr_ref, bra_ref, mg_ref, do_ref, dy_ref, st_ref):
        @pl.when(pl.program_id(0) == 0)
        def _():
            st_ref[...] = jnp.zeros_like(st_ref)

        br_rnn = jnp.dot(zr_ref[...], wr_ref[...], preferred_element_type=F32)
        br_attn = jnp.dot(za_ref[...], wa_ref[...], preferred_element_type=F32)
        brr_ref[...] = br_rnn.astype(BF16)
        bra_ref[...] = br_attn.astype(BF16)
        g_rnn = _sigmoid(lr_ref[...] + br_ref[...])
        g_attn = _sigmoid(la_ref[...] + ba_ref[...])
        merged = (g_rnn * br_rnn + g_attn * br_attn).astype(BF16)
        mg_ref[...] = merged
        o = jnp.dot(merged, wo_ref[...], preferred_element_type=F32)
        g = g_ref[...]
        r = lax.rsqrt(jnp.mean(o * o, axis=-1, keepdims=True) + EPS)
        nrm = o * r
        err = x_ref[...] + nrm * g - t_ref[...]
        dy = err * (1.0 / d)
        dy_ref[...] = dy
        dn = dy * g
        do_ref[...] = (r * (dn - nrm * jnp.mean(dn * nrm, axis=-1, keepdims=True))).astype(BF16)
        st_ref[...] += _rows8([jnp.sum(dy * nrm, axis=0, keepdims=True), jnp.sum(err * err, axis=0, keepdims=True)], d)

    tile = pl.BlockSpec((tm, d), lambda i: (i, 0))
    weight = pl.BlockSpec((d, d), lambda i: (0, 0))
    bf = jax.ShapeDtypeStruct((t, d), BF16)
    return pl.pallas_call(
        body, name=name, grid=(t // tm,),
        in_specs=[tile, tile, pl.BlockSpec((tm, d), lambda i: (i, 1)), pl.BlockSpec((tm, d), lambda i: (i, 2)),
                  pl.BlockSpec((1, d), lambda i: (0, 0)), pl.BlockSpec((1, d), lambda i: (0, 1)),
                  weight, weight, weight, tile, tile, pl.BlockSpec((1, d), lambda i: (0, 0))],
        out_specs=[tile, tile, tile, tile, tile, pl.BlockSpec((8, d), lambda i: (0, 0))],
        out_shape=[bf, bf, bf, bf, jax.ShapeDtypeStruct((t, d), F32), jax.ShapeDtypeStruct((8, d), F32)],
        compiler_params=_params("arbitrary"))(z_rnn, z_attn, ag_ml, ag_ml, b_gate, b_gate, w_rnn, w_attn, w_out, x, target, g_post)


def _branches_bwd(dout, br_rnn, br_attn, ag_ml, b_gate, w_rnn, w_attn, w_out, name, tm=512):
    t, d = br_rnn.shape
    tm = min(tm, t)

    def body(do_ref, r_ref, a_ref, lr_ref, la_ref, br_ref, ba_ref, wr_ref, wa_ref, wo_ref,
             dr_ref, da_ref, dl_ref, dzr_ref, dza_ref, st_ref, wt_ref):
        @pl.when(pl.program_id(0) == 0)
        def _():
            st_ref[...] = jnp.zeros_like(st_ref)
            wt_ref[0] = wo_ref[...].T
            wt_ref[1] = wr_ref[...].T
            wt_ref[2] = wa_ref[...].T

        dm = jnp.dot(do_ref[...], wt_ref[0], preferred_element_type=F32)
        g_rnn = _sigmoid(lr_ref[...] + br_ref[...])
        g_attn = _sigmoid(la_ref[...] + ba_ref[...])
        dbr_rnn = (dm * g_rnn).astype(BF16)
        dbr_attn = (dm * g_attn).astype(BF16)
        dr_ref[...] = dbr_rnn
        da_ref[...] = dbr_attn
        dl_rnn = dm * r_ref[...].astype(F32) * g_rnn * (1.0 - g_rnn)
        dl_attn = dm * a_ref[...].astype(F32) * g_attn * (1.0 - g_attn)
        dl_ref[:, 0:d] = dl_rnn.astype(BF16)
        dl_ref[:, d:2 * d] = dl_attn.astype(BF16)
        st_ref[...] += _rows8([jnp.sum(dl_rnn, axis=0, keepdims=True), jnp.sum(dl_attn, axis=0, keepdims=True)], d)
        dzr_ref[...] = jnp.dot(dbr_rnn, wt_ref[1], preferred_element_type=F32)
        dza_ref[...] = jnp.dot(dbr_attn, wt_ref[2], preferred_element_type=F32)

    tile = pl.BlockSpec((tm, d), lambda i: (i, 0))
    weight = pl.BlockSpec((d, d), lambda i: (0, 0))
    bf = jax.ShapeDtypeStruct((t, d), BF16)
    return pl.pallas_call(
        body, name=name, grid=(t // tm,),
        in_specs=[tile, tile, tile, pl.BlockSpec((tm, d), lambda i: (i, 1)), pl.BlockSpec((tm, d), lambda i: (i, 2)),
                  pl.BlockSpec((1, d), lambda i: (0, 0)), pl.BlockSpec((1, d), lambda i: (0, 1)), weight, weight, weight],
        out_specs=[tile, tile, pl.BlockSpec((tm, 2 * d), lambda i: (i, 0)), tile, tile, pl.BlockSpec((8, d), lambda i: (0, 0))],
        out_shape=[bf, bf, jax.ShapeDtypeStruct((t, 2 * d), BF16), jax.ShapeDtypeStruct((t, d), F32),
                   jax.ShapeDtypeStruct((t, d), F32), jax.ShapeDtypeStruct((8, d), F32)],
        scratch_shapes=[pltpu.VMEM((3, d, d), BF16)],
        compiler_params=_params("arbitrary"))(dout, br_rnn, br_attn, ag_ml, ag_ml, b_gate, b_gate, w_rnn, w_attn, w_out)


def _lru_gates(c, wa, ba, wx, bx, sp):
    cb = c.astype(BF16)
    r = _sigmoid(jnp.dot(cb, wa, preferred_element_type=F32) + ba)
    ig = _sigmoid(jnp.dot(cb, wx, preferred_element_type=F32) + bx)
    log_a = (-LRU_C) * r * sp
    a = jnp.exp(log_a)
    mult = jnp.sqrt(-jnp.tanh(log_a) * (a * a + 1.0))
    return cb, r, ig, a, mult


SUBLANES = 8


def _scan_fwd(a, u, carry, tt):
    w = a.shape[1]
    ng = tt // SUBLANES
    a3 = a.reshape(ng, SUBLANES, w)
    u3 = u.reshape(ng, SUBLANES, w)
    sub = lax.broadcasted_iota(jnp.int32, (ng, SUBLANES, w), 1)
    d = 1
    while d < SUBLANES:
        keep = sub >= d
        u3 = u3 + a3 * jnp.where(keep, pltpu.roll(u3, d, 1), 0.0)
        a3 = a3 * jnp.where(keep, pltpu.roll(a3, d, 1), 1.0)
        d *= 2
    out = []
    for g in range(ng):
        hg = u3[g] + a3[g] * carry
        out.append(hg)
        carry = hg[SUBLANES - 1:SUBLANES, :]
    return jnp.concatenate(out, axis=0)


def _scan_rev(b, g, carry, tt):
    w = b.shape[1]
    ng = tt // SUBLANES
    b3 = b.reshape(ng, SUBLANES, w)
    g3 = g.reshape(ng, SUBLANES, w)
    sub = lax.broadcasted_iota(jnp.int32, (ng, SUBLANES, w), 1)
    d = 1
    while d < SUBLANES:
        keep = sub < SUBLANES - d
        g3 = g3 + b3 * jnp.where(keep, pltpu.roll(g3, SUBLANES - d, 1), 0.0)
        b3 = b3 * jnp.where(keep, pltpu.roll(b3, SUBLANES - d, 1), 1.0)
        d *= 2
    out = [None] * ng
    for k in range(ng - 1, -1, -1):
        hk = g3[k] + b3[k] * carry
        out[k] = hk
        carry = hk[0:1, :]
    return jnp.concatenate(out, axis=0)


def _conv_taps(cw, bias, x, ext_ref, tt):
    x2 = ext_ref[7:7 + tt, :]
    x1 = ext_ref[6:6 + tt, :]
    x0 = ext_ref[5:5 + tt, :]
    c = bias + cw[3:4] * x + cw[2:3] * x2 + cw[1:2] * x1 + cw[0:1] * x0
    return c, x2, x1, x0


def _rnn_fwd(rx_rg, cw, cb, wa, ba, wx, bx, lam, name, tt=512):
    t = rx_rg.shape[0]
    tt = min(tt, t)
    w = GROUP_W

    def body(rx_ref, rg_ref, cw_ref, cb_ref, wa_ref, ba_ref, wx_ref, bx_ref, lam_ref, y_ref, z_ref, ext_ref, hc_ref):
        @pl.when(pl.program_id(1) == 0)
        def _():
            ext_ref[0:8, :] = jnp.zeros((8, w), F32)
            hc_ref[...] = jnp.zeros((8, w), F32)

        x = rx_ref[...]
        ext_ref[8:8 + tt, :] = x
        c, _, _, _ = _conv_taps(cw_ref[...], cb_ref[...], x, ext_ref, tt)
        ext_ref[0:8, :] = x[tt - 8:tt, :]
        sp = _softplus(-lam_ref[...])
        _, _, ig, a, mult = _lru_gates(c, wa_ref[...], ba_ref[...], wx_ref[...], bx_ref[...], sp)
        h = _scan_fwd(a, mult * (ig * c), hc_ref[7:8, :], tt)
        hc_ref[...] = h[tt - 8:tt, :]
        y_ref[...] = h
        rg = rg_ref[...]
        z_ref[...] = (h * rg * _sigmoid(rg)).astype(BF16)

    vec = pl.BlockSpec((1, w), lambda g, i: (0, g))
    mat = pl.BlockSpec((None, w, w), lambda g, i: (g, 0, 0))
    tile = pl.BlockSpec((tt, w), lambda g, i: (i, g))
    return pl.pallas_call(
        body, name=name, grid=(N_GROUPS, t // tt),
        in_specs=[tile, pl.BlockSpec((tt, w), lambda g, i: (i, N_GROUPS + g)),
                  pl.BlockSpec((4, w), lambda g, i: (0, g)), vec, mat, vec, mat, vec, vec],
        out_specs=[tile, tile],
        out_shape=[jax.ShapeDtypeStruct((t, D_RNN), F32), jax.ShapeDtypeStruct((t, D_RNN), BF16)],
        scratch_shapes=[pltpu.VMEM((tt + 8, w), F32), pltpu.VMEM((8, w), F32)],
        compiler_params=_params("parallel", "arbitrary"))(rx_rg, rx_rg, cw, cb, wa, ba, wx, bx, lam)


def _rnn_bwd(rx_rg, y, dz, cw, cb, wa, ba, wx, bx, lam, name, tt=512, ride=None):
    t = rx_rg.shape[0]
    tt = min(tt, t)
    nt = t // tt
    w = GROUP_W

    host = _Hosted(ride, 13, 5, 6)

    def body(*refs):
        host_refs, start, finish = host.split(refs)
        (rx_ref, rg_ref, rxt_ref, y_ref, yt_ref, dz_ref, cw_ref, cb_ref, wa_ref, ba_ref, wx_ref, bx_ref, lam_ref,
         drx_ref, drg_ref, st_ref, gda_ref, gdx_ref, ext_ref, dcx_ref, wcar_ref, acar_ref, dwa_ref, dwx_ref) = host_refs
        ii = pl.program_id(1)
        start((pl.program_id(0) == 0) & (ii == 0))

        @pl.when(ii == 0)
        def _():
            wcar_ref[...] = jnp.zeros((8, w), F32)
            acar_ref[...] = jnp.zeros((8, w), F32)
            dcx_ref[tt:tt + 8, :] = jnp.zeros((8, w), F32)
            st_ref[...] = jnp.zeros_like(st_ref)
            dwa_ref[...] = jnp.zeros_like(dwa_ref)
            dwx_ref[...] = jnp.zeros_like(dwx_ref)

        has_prev = jnp.where(ii == nt - 1, 0.0, 1.0)
        x = rx_ref[...]
        ext_ref[0:8, :] = rxt_ref[...] * has_prev
        ext_ref[8:8 + tt, :] = x
        cwv = cw_ref[...]
        c, x2, x1, x0 = _conv_taps(cwv, cb_ref[...], x, ext_ref, tt)
        lam = lam_ref[...]
        sp = _softplus(-lam)
        wa = wa_ref[...]
        wx = wx_ref[...]
        cb16, r, ig, a, mult = _lru_gates(c, wa, ba_ref[...], wx, bx_ref[...], sp)

        rg = rg_ref[...]
        sg = _sigmoid(rg)
        dz = dz_ref[...]
        yv = y_ref[...]
        drg_ref[...] = (dz * yv * (sg * (1.0 + rg * (1.0 - sg)))).astype(BF16)

        row = lax.broadcasted_iota(jnp.int32, (tt, w), 0)
        b = jnp.where(row < tt - 1, pltpu.roll(a, tt - 1, 0), acar_ref[0:1, :])
        dh = _scan_rev(b, dz * (rg * sg), wcar_ref[0:1, :], tt)
        wcar_ref[...] = dh[0:8, :]
        acar_ref[...] = a[0:8, :]

        hprev = jnp.where(row >= 1, pltpu.roll(yv, 1, 0), yt_ref[7:8, :] * has_prev)
        dmult = dh * (ig * c)
        dig = dh * mult * c
        dlog_a = dh * hprev * a - dmult * (a * a / mult)
        dpa = dlog_a * ((-LRU_C) * sp) * r * (1.0 - r)
        dpx = dig * ig * (1.0 - ig)
        dsp = jnp.sum(dlog_a * r, axis=0, keepdims=True) * (-LRU_C)
        dlam = dsp * (-_sigmoid(-lam))
        dpa16 = dpa.astype(BF16)
        dpx16 = dpx.astype(BF16)
        dwa_ref[...] += lax.dot_general(cb16, dpa16, TN_DIMS, preferred_element_type=F32)
        dwx_ref[...] += lax.dot_general(cb16, dpx16, TN_DIMS, preferred_element_type=F32)
        dc = (dh * mult * ig
              + lax.dot_general(dpa16, wa, NT_DIMS, preferred_element_type=F32)
              + lax.dot_general(dpx16, wx, NT_DIMS, preferred_element_type=F32))

        dcx_ref[0:tt, :] = dc
        drx = (cwv[3:4] * dc + cwv[2:3] * dcx_ref[1:1 + tt, :] + cwv[1:2] * dcx_ref[2:2 + tt, :]
               + cwv[0:1] * dcx_ref[3:3 + tt, :])
        drx_ref[...] = drx.astype(BF16)
        dcx_ref[tt:tt + 8, :] = dc[0:8, :]

        def colsum(v):
            return jnp.sum(v, axis=0, keepdims=True)

        st_ref[...] += _rows8([colsum(dc), colsum(dpa), colsum(dpx), dlam,
                               colsum(dc * x0), colsum(dc * x1), colsum(dc * x2), colsum(dc * x)], w)

        @pl.when(ii == nt - 1)
        def _():
            for blk in range(GROUP_W // RNN_BLOCK_W):
                rows = slice(blk * RNN_BLOCK_W, (blk + 1) * RNN_BLOCK_W)
                gda_ref[blk] = dwa_ref[rows, rows]
                gdx_ref[blk] = dwx_ref[rows, rows]

        finish((pl.program_id(0) == N_GROUPS - 1) & (ii == nt - 1))

    def rev(ii):
        return nt - 1 - ii

    def tail(g, ii):
        return (jnp.maximum(rev(ii) * (tt // 8) - 1, 0), g)

    vec = pl.BlockSpec((1, w), lambda g, ii: (0, g))
    mat = pl.BlockSpec((None, w, w), lambda g, ii: (g, 0, 0))
    tile = pl.BlockSpec((tt, w), lambda g, ii: (rev(ii), g))
    diag_shape = (N_GROUPS, GROUP_W // RNN_BLOCK_W, RNN_BLOCK_W, RNN_BLOCK_W)
    diag = pl.BlockSpec((None,) + diag_shape[1:], lambda g, ii: (g, 0, 0, 0))
    outs = pl.pallas_call(
        body, name=name, grid=(N_GROUPS, nt),
        in_specs=[tile, pl.BlockSpec((tt, w), lambda g, ii: (rev(ii), N_GROUPS + g)), pl.BlockSpec((8, w), tail),
                  tile, pl.BlockSpec((8, w), tail), tile,
                  pl.BlockSpec((4, w), lambda g, ii: (0, g)), vec, mat, vec, mat, vec, vec] + host.in_specs,
        out_specs=[tile, tile, pl.BlockSpec((8, w), lambda g, ii: (0, g)), diag, diag] + host.out_specs,
        out_shape=[jax.ShapeDtypeStruct((t, D_RNN), BF16), jax.ShapeDtypeStruct((t, D_RNN), BF16),
                   jax.ShapeDtypeStruct((8, D_RNN), F32),
                   jax.ShapeDtypeStruct(diag_shape, F32), jax.ShapeDtypeStruct(diag_shape, F32)] + host.out_shapes,
        scratch_shapes=[pltpu.VMEM((tt + 8, w), F32), pltpu.VMEM((tt + 8, w), F32), pltpu.VMEM((8, w), F32),
                        pltpu.VMEM((8, w), F32), pltpu.VMEM((w, w), F32), pltpu.VMEM((w, w), F32)] + host.scratch_shapes,
        compiler_params=_params("arbitrary" if ride else "parallel", "arbitrary"))(
            rx_rg, rx_rg, rx_rg, y, y, dz, cw, cb, wa, ba, wx, bx, lam, *host.arrays)
    res, landed = host.results(outs, 5)
    return (*res, landed) if ride else tuple(res)


def _half_mask(shape, half):
    lane = lax.broadcasted_iota(jnp.int32, shape, 1)
    return (lane >= HEAD_DIM) if half else (lane < HEAD_DIM)


def _dup_half(t, half):
    sel = jnp.where(_half_mask(t.shape, half), t, 0.0)
    return sel + pltpu.roll(sel, HEAD_DIM, 1)


def _band_geometry(n):
    qi = lax.broadcasted_iota(jnp.int32, (BLOCK, 2 * BLOCK), 0)
    kj = lax.broadcasted_iota(jnp.int32, (BLOCK, 2 * BLOCK), 1)
    dist = BLOCK + qi - kj
    first_key = jnp.where(n > 0, 0, BLOCK)
    valid = (dist >= 0) & (dist < BLOCK) & (kj >= first_key)
    return dist.astype(F32), valid


GROUP = 4


def _kv_dup(prev_ref, cur_ref, hk, scale=1.0):
    tile = hk // 2
    kt = jnp.concatenate([prev_ref[:, tile * LANE:(tile + 1) * LANE], cur_ref[:, tile * LANE:(tile + 1) * LANE]], axis=0)
    return (_dup_half(kt.astype(F32), hk % 2) * scale).astype(BF16)


def _fill_bias(bias_ref, sm_ref, n):
    distf, valid = _band_geometry(n)
    for head in range(N_Q_HEADS):
        bias_ref[head] = jnp.where(valid, -sm_ref[1, head] * distf, MASKED)


def _head_scores(q2s, half, kdup, bias):
    qm = jnp.where(_half_mask(q2s.shape, half), q2s, jnp.zeros_like(q2s))
    return qm, lax.dot_general(qm, kdup, NT_DIMS, preferred_element_type=F32) + bias


def _attn_specs(nb, clamp_last):
    def blk(n):
        return jnp.minimum(n, nb - 1) if clamp_last else n

    q_spec = pl.BlockSpec((BLOCK, D_MODEL), lambda n: (blk(n), 0))
    k_prev = pl.BlockSpec((BLOCK, D_KV), lambda n: (jnp.maximum(blk(n) - 1, 0), D_MODEL // D_KV))
    k_cur = pl.BlockSpec((BLOCK, D_KV), lambda n: (blk(n), D_MODEL // D_KV))
    v_prev = pl.BlockSpec((BLOCK, D_KV), lambda n: (jnp.maximum(blk(n) - 1, 0), D_MODEL // D_KV + 1))
    v_cur = pl.BlockSpec((BLOCK, D_KV), lambda n: (blk(n), D_MODEL // D_KV + 1))
    return q_spec, k_prev, k_cur, v_prev, v_cur


def _attn_fwd(sm, qkv, ag_ml, name):
    t = qkv.shape[0]
    nb = t // BLOCK

    def body(sm_ref, q_ref, kp_ref, kc_ref, vp_ref, vc_ref, ag_ref, y_ref, z_ref, lse_ref, bias_ref):
        n = pl.program_id(0)

        @pl.when(n <= 1)
        def _():
            _fill_bias(bias_ref, sm_ref, n)

        lane = lax.broadcasted_iota(jnp.int32, (BLOCK, LANE), 1)
        low = lane < HEAD_DIM
        lse = jnp.zeros((BLOCK, LANE), F32)
        for hk in range(N_Q_HEADS // GROUP):
            kdup = _kv_dup(kp_ref, kc_ref, hk)
            vdup = _kv_dup(vp_ref, vc_ref, hk)
            for k in (0, 1):
                c = slice((2 * hk + k) * LANE, (2 * hk + k + 1) * LANE)
                q2s = q_ref[:, c] * ATTN_SCALE
                outs = []
                for half in (0, 1):
                    head = GROUP * hk + 2 * k + half
                    _, s = _head_scores(q2s, half, kdup, bias_ref[head])
                    sink = sm_ref[0, head]
                    m = jnp.maximum(jnp.max(s, axis=1, keepdims=True), sink)
                    e = jnp.exp(s - m)
                    l = jnp.sum(e, axis=1, keepdims=True) + jnp.exp(sink - m)
                    outs.append(jnp.dot((e * (1.0 / l)).astype(BF16), vdup, preferred_element_type=F32))
                    lse = jnp.where(lane == head, m + jnp.log(l), lse)
                yt = jnp.where(low, outs[0], outs[1])
                y_ref[:, c] = yt
                ag = ag_ref[:, c]
                z_ref[:, c] = (yt * ag * _sigmoid(ag)).astype(BF16)
        lse_ref[...] = lse

    q_spec, k_prev, k_cur, v_prev, v_cur = _attn_specs(nb, False)
    wide = pl.BlockSpec((BLOCK, D_MODEL), lambda n: (n, 0))
    return pl.pallas_call(
        body, name=name, grid=(nb,),
        in_specs=[pl.BlockSpec(memory_space=pltpu.SMEM), q_spec, k_prev, k_cur, v_prev, v_cur, wide],
        out_specs=[wide, wide, pl.BlockSpec((BLOCK, LANE), lambda n: (n, 0))],
        out_shape=[jax.ShapeDtypeStruct((t, D_MODEL), F32), jax.ShapeDtypeStruct((t, D_MODEL), BF16),
                   jax.ShapeDtypeStruct((t, LANE), F32)],
        scratch_shapes=[pltpu.VMEM((N_Q_HEADS, BLOCK, 2 * BLOCK), F32)],
        compiler_params=_params("arbitrary"))(sm, qkv, qkv, qkv, qkv, qkv, ag_ml)


def _attn_bwd(sm, qkv, ag_ml, y, lse, dz, name, ride=None):
    t = qkv.shape[0]
    nb = t // BLOCK
    host = _Hosted(ride, 10, 4, 3)

    def body(*refs):
        host_refs, start, finish = host.split(refs)
        (sm_ref, q_ref, kp_ref, kc_ref, vp_ref, vc_ref, ag_ref, y_ref, lse_ref, dz_ref,
         dq_ref, dkv_ref, dag_ref, ds_ref, ck_ref, cv_ref, bias_ref) = host_refs
        n = pl.program_id(0)
        start(n == 0)

        @pl.when(n == 0)
        def _():
            ck_ref[...] = jnp.zeros_like(ck_ref)
            cv_ref[...] = jnp.zeros_like(cv_ref)
            ds_ref[...] = jnp.zeros_like(ds_ref)

        @pl.when(n <= 1)
        def _():
            _fill_bias(bias_ref, sm_ref, n)

        @pl.when(n < nb)
        def _():
            lane8 = lax.broadcasted_iota(jnp.int32, (8, LANE), 1)
            row8 = lax.broadcasted_iota(jnp.int32, (8, LANE), 0)
            dsink = jnp.zeros((8, LANE), F32)
            dk_heads, dv_heads = [], []
            lse_tile = lse_ref[...]
            for hk in range(N_Q_HEADS // GROUP):
                kdup = _kv_dup(kp_ref, kc_ref, hk)
                ks = _kv_dup(kp_ref, kc_ref, hk, ATTN_SCALE)
                vdup = _kv_dup(vp_ref, vc_ref, hk)
                dk_acc = jnp.zeros((2 * BLOCK, LANE), F32)
                dv_acc = jnp.zeros((2 * BLOCK, LANE), F32)
                for k in (0, 1):
                    c = slice((2 * hk + k) * LANE, (2 * hk + k + 1) * LANE)
                    ag = ag_ref[:, c]
                    sg = _sigmoid(ag)
                    dzt = dz_ref[:, c]
                    yt = y_ref[:, c]
                    dag_ref[:, c] = (dzt * yt * (sg * (1.0 + ag * (1.0 - sg)))).astype(BF16)
                    dyt = dzt * (ag * sg)
                    q2s = q_ref[:, c] * ATTN_SCALE
                    dq_halves = []
                    for half in (0, 1):
                        head = GROUP * hk + 2 * k + half
                        qm, s = _head_scores(q2s, half, kdup, bias_ref[head])
                        lh = lse_tile[:, head:head + 1]
                        probs = jnp.exp(s - lh)
                        psink = jnp.exp(sm_ref[0, head] - lh)
                        dyh = jnp.where(_half_mask(dyt.shape, half), dyt, 0.0)
                        delta = jnp.sum(dyh * yt, axis=1, keepdims=True)
                        dyh16 = dyh.astype(BF16)
                        dp = lax.dot_general(dyh16, vdup, NT_DIMS, preferred_element_type=F32)
                        ds16 = (probs * (dp - delta)).astype(BF16)
                        dsink = dsink + jnp.where((row8 == 0) & (lane8 == head),
                                                  -jnp.sum(psink * delta, axis=0, keepdims=True), 0.0)
                        dq_halves.append(jnp.dot(ds16, ks, preferred_element_type=F32))
                        dk_acc = dk_acc + lax.dot_general(ds16, qm, TN_DIMS, preferred_element_type=F32)
                        dv_acc = dv_acc + lax.dot_general(probs.astype(BF16), dyh16, TN_DIMS, preferred_element_type=F32)
                    dq_ref[:, c] = jnp.where(_half_mask((BLOCK, LANE), 0), dq_halves[0], dq_halves[1]).astype(BF16)
                dk_heads.append(dk_acc + pltpu.roll(dk_acc, HEAD_DIM, 1))
                dv_heads.append(dv_acc + pltpu.roll(dv_acc, HEAD_DIM, 1))
            ds_ref[...] += dsink
            low = _half_mask((2 * BLOCK, LANE), 0)
            for tile in range(2):
                cols = slice(tile * LANE, (tile + 1) * LANE)
                dkt = jnp.where(low, dk_heads[2 * tile], dk_heads[2 * tile + 1])
                dvt = jnp.where(low, dv_heads[2 * tile], dv_heads[2 * tile + 1])
                dkv_ref[:, cols] = (ck_ref[:, cols] + dkt[0:BLOCK, :]).astype(BF16)
                dkv_ref[:, D_KV + tile * LANE:D_KV + (tile + 1) * LANE] = (cv_ref[:, cols] + dvt[0:BLOCK, :]).astype(BF16)
                ck_ref[:, cols] = dkt[BLOCK:2 * BLOCK, :]
                cv_ref[:, cols] = dvt[BLOCK:2 * BLOCK, :]

        @pl.when(n == nb)
        def _():
            dkv_ref[:, 0:D_KV] = ck_ref[...].astype(BF16)
            dkv_ref[:, D_KV:2 * D_KV] = cv_ref[...].astype(BF16)

        finish(n == nb)

    q_spec, k_prev, k_cur, v_prev, v_cur = _attn_specs(nb, True)
    wide = pl.BlockSpec((BLOCK, D_MODEL), lambda n: (jnp.minimum(n, nb - 1), 0))
    outs = pl.pallas_call(
        body, name=name, grid=(nb + 1,),
        in_specs=[pl.BlockSpec(memory_space=pltpu.SMEM), q_spec, k_prev, k_cur, v_prev, v_cur, wide, wide,
                  pl.BlockSpec((BLOCK, LANE), lambda n: (jnp.minimum(n, nb - 1), 0)), wide] + host.in_specs,
        out_specs=[wide, pl.BlockSpec((BLOCK, 2 * D_KV), lambda n: (jnp.maximum(n - 1, 0), 0)), wide,
                   pl.BlockSpec((8, LANE), lambda n: (0, 0))] + host.out_specs,
        out_shape=[jax.ShapeDtypeStruct((t, D_MODEL), BF16), jax.ShapeDtypeStruct((t, 2 * D_KV), BF16),
                   jax.ShapeDtypeStruct((t, D_MODEL), BF16), jax.ShapeDtypeStruct((8, LANE), F32)] + host.out_shapes,
        scratch_shapes=[pltpu.VMEM((BLOCK, D_KV), F32), pltpu.VMEM((BLOCK, D_KV), F32),
                        pltpu.VMEM((N_Q_HEADS, BLOCK, 2 * BLOCK), F32)] + host.scratch_shapes,
        compiler_params=_params("arbitrary"))(sm, qkv, qkv, qkv, qkv, qkv, ag_ml, y, lse, dz, *host.arrays)
    res, landed = host.results(outs, 4)
    return (*res, landed) if ride else tuple(res)


def _local_grads(x, target, p, late_weights=None, reduce_out=None, reduce_in=None):
    wt = p["wt"]
    h = _prenorm(x, p["pre_g"], "prenorm")
    splits = ((2 * D_RNN, F32), (D_MODEL + 2 * D_KV, BF16), (3 * D_MODEL, F32))
    if late_weights is None:
        rx_rg, qkv, ag_ml = _proj(h, wt, splits, "proj")
    else:
        (rx_rg, qkv, ag_ml), landed = _proj(h, wt, splits, "proj", ride=late_weights[0])
        p = {**p, **late_weights[1](landed)}
    lru =(p["cw"], p["cb"], p["wbd_a"], p["b_a"], p["wbd_x"], p["b_x"], p["lam"])
    y_rnn, z_rnn = _rnn_fwd(rx_rg, *lru, "rnn_fwd")
    y_attn, z_attn, lse = _attn_fwd(p["sm"], qkv, ag_ml, "attn_fwd")
    br_rnn, br_attn, merged, dout, dy, st_post = _branches_fwd(
        z_rnn, z_attn, ag_ml, p["b_gate"], p["w_rnn"], p["w_attn"], p["w_out"], x, target, p["post_g"], "branches_fwd")

    dbr_rnn, dbr_attn, d_ml, dz_rnn, dz_attn, st_merge = _branches_bwd(
        dout, br_rnn, br_attn, ag_ml, p["b_gate"], p["w_rnn"], p["w_attn"], p["w_out"], "branches_bwd")
    gw_out = _mm_tn(merged, dout, "gw_out")
    gw_rnn = _mm_tn(z_rnn, dbr_rnn, "gw_rnn")
    gw_attn = _mm_tn(z_attn, dbr_attn, "gw_attn")
    red_out = red_in = None
    if reduce_out is None:
        d_rx, d_rg, st_rnn, g_rg_a, g_rg_x = _rnn_bwd(rx_rg, y_rnn, dz_rnn, *lru, "rnn_bwd")
        dq, dkv, d_ag, st_sink = _attn_bwd(p["sm"], qkv, ag_ml, y_attn, lse, dz_attn, "attn_bwd")
    else:
        d_rx, d_rg, st_rnn, g_rg_a, g_rg_x, from_sibling = _rnn_bwd(
            rx_rg, y_rnn, dz_rnn, *lru, "rnn_bwd", ride=reduce_out[0](gw_rnn, gw_attn, gw_out))
        dq, dkv, d_ag, st_sink, red_out = _attn_bwd(p["sm"], qkv, ag_ml, y_attn, lse, dz_attn, "attn_bwd",
                                                    ride=reduce_out[1](from_sibling, g_rg_a, g_rg_x))

    segs = [d_rx, d_rg, dq, dkv, d_ag, d_ml]
    gwt = _mm_tn_seg(segs, h, "gw_in")
    if reduce_in is None:
        dh = _mm_nn_seg(segs, wt, "d_h")
    else:
        dh, red_in = _mm_nn_seg(segs, wt, "d_h", ride=reduce_in(gwt))
    grad_x, st_pre = _prenorm_bwd(x, p["pre_g"], dh, dy, "prenorm_bwd")
    return dict(grad_x=grad_x, gwt=gwt, gw_rnn=gw_rnn, gw_attn=gw_attn, gw_out=gw_out,
                st_post=st_post, st_merge=st_merge, st_rnn=st_rnn, st_sink=st_sink, st_pre=st_pre,
                g_rg_a=g_rg_a, g_rg_x=g_rg_x, red_out=red_out, red_in=red_in)


def _place():
    x, y, c = lax.axis_index("x"), lax.axis_index("y"), lax.axis_index("c")
    return x, y, c


def _gather_ride(shards):
    n = len(shards)

    def copies(ins, outs, sems):
        send_sems, recv_sems, local_sems = sems
        x, y, c = _place()
        me, sibling = (x, y, c), (x, y, 1 - c)
        chips = [(1 - x, y), (x, 1 - y), (1 - x, 1 - y)]

        def slot(a, dev):
            return outs[a].at[4 * dev[0] + 2 * dev[1] + dev[2]]

        def copy(a, k, block, to, src=None):
            return pltpu.make_async_remote_copy(
                src_ref=slot(a, block) if src is None else src, dst_ref=slot(a, block),
                send_sem=send_sems.at[a, k], recv_sem=recv_sems.at[a, k], device_id=to, device_id_type=MESH)

        mine = [pltpu.make_async_copy(ins[a], slot(a, me), local_sems.at[a]) for a in range(n)]
        first = []
        for a in range(n):
            first.append(copy(a, 0, me, sibling, src=ins[a]))
            first += [copy(a, 1 + j, me, (*chip, c), src=ins[a]) for j, chip in enumerate(chips)]
        return me, sibling, chips, c, copy, mine, first

    def start(ins, outs, sems):
        *_, mine, first = copies(ins, outs, sems)
        for cp in mine + first:
            cp.start()

    def finish(ins, outs, sems):
        me, sibling, chips, c, copy, mine, first = copies(ins, outs, sems)
        passed = []
        for j, chip in enumerate(chips):
            for a in range(n):
                copy(a, 1 + j, (*chip, c), me).wait_recv()
                cp = copy(a, 4 + j, (*chip, c), sibling)
                cp.start()
                passed.append(cp)
        for a in range(n):
            copy(a, 0, sibling, me).wait_recv()
            for j, chip in enumerate(chips):
                copy(a, 4 + j, (*chip, 1 - c), me).wait_recv()
        for cp in first + passed:
            cp.wait_send()
        for cp in mine:
            cp.wait()

    return _Ride(
        shards, [jax.ShapeDtypeStruct((N_DEV, *s.shape), s.dtype) for s in shards],
        [pltpu.SemaphoreType.DMA((n, 7)), pltpu.SemaphoreType.DMA((n, 7)), pltpu.SemaphoreType.DMA((n,))],
        start, finish)


def _sibling_ride(scatter, whole):
    ns, nw = len(scatter), len(whole)

    def copies(ins, outs, sems):
        send_sems, recv_sems = sems
        x, y, c = _place()
        cps = [pltpu.make_async_remote_copy(
            src_ref=ins[a].at[2 * chip + (1 - c)], dst_ref=outs[a].at[chip],
            send_sem=send_sems.at[a * N_CHIP + chip], recv_sem=recv_sems.at[a * N_CHIP + chip],
            device_id=(x, y, 1 - c), device_id_type=MESH) for a in range(ns) for chip in range(N_CHIP)]
        cps += [pltpu.make_async_remote_copy(
            src_ref=ins[ns + a], dst_ref=outs[ns + a],
            send_sem=send_sems.at[ns * N_CHIP + a], recv_sem=recv_sems.at[ns * N_CHIP + a],
            device_id=(x, y, 1 - c), device_id_type=MESH) for a in range(nw)]
        return cps

    def start(ins, outs, sems):
        for cp in copies(ins, outs, sems):
            cp.start()

    def finish(ins, outs, sems):
        for cp in copies(ins, outs, sems):
            cp.wait()

    n_sem = ns * N_CHIP + nw
    return _Ride(
        list(scatter) + list(whole),
        [jax.ShapeDtypeStruct((N_CHIP, *s.shape[1:]), s.dtype) for s in scatter]
        + [jax.ShapeDtypeStruct(s.shape, s.dtype) for s in whole],
        [pltpu.SemaphoreType.DMA((n_sem,)), pltpu.SemaphoreType.DMA((n_sem,))], start, finish)


def _chips_ride(scatter, whole):
    ns, nw = len(scatter), len(whole)
    n = ns + nw

    def copies(ins, outs, sems):
        send_sems, recv_sems, local_sems = sems
        x, y, c = _place()
        own = 2 * x + y
        chips = [(1 - x, y), (x, 1 - y), (1 - x, 1 - y)]

        def src(a, chip_idx):
            return ins[a].at[chip_idx] if a < ns else ins[a]

        local = [pltpu.make_async_copy(src(a, own), outs[a].at[own], local_sems.at[a]) for a in range(n)]
        sent = [pltpu.make_async_remote_copy(
            src_ref=src(a, 2 * chip[0] + chip[1]), dst_ref=outs[a].at[own],
            send_sem=send_sems.at[a, j], recv_sem=recv_sems.at[a, own], device_id=(*chip, c), device_id_type=MESH)
            for a in range(n) for j, chip in enumerate(chips)]
        return chips, c, local, sent

    def start(ins, outs, sems):
        _, _, local, sent = copies(ins, outs, sems)
        for cp in local + sent:
            cp.start()

    def finish(ins, outs, sems):
        send_sems, recv_sems, _ = sems
        chips, c, local, sent = copies(ins, outs, sems)
        for a in range(n):
            for chip in chips:
                k = 2 * chip[0] + chip[1]
                pltpu.make_async_remote_copy(
                    src_ref=outs[a].at[k], dst_ref=outs[a].at[k], send_sem=send_sems.at[a, 0],
                    recv_sem=recv_sems.at[a, k], device_id=(*chip, c), device_id_type=MESH).wait_recv()
        for cp in sent:
            cp.wait_send()
        for cp in local:
            cp.wait()

    return _Ride(
        list(scatter) + list(whole),
        [jax.ShapeDtypeStruct(s.shape, s.dtype) for s in scatter]
        + [jax.ShapeDtypeStruct((N_CHIP, *s.shape), s.dtype) for s in whole],
        [pltpu.SemaphoreType.DMA((n, 3)), pltpu.SemaphoreType.DMA((n, N_CHIP)), pltpu.SemaphoreType.DMA((n,))],
        start, finish)


def _pair_sum_scatter(parts, recvs, core, name):
    na = len(parts)
    _, r, cdim = parts[0].shape
    tr = min(r, 416 if r % 416 == 0 else 128)

    def body(core_ref, *refs):
        del core_ref
        for a in range(na):
            refs[2 * na + a][...] = (refs[a][...] + refs[na + a][...]).astype(BF16)

    blk = (None, tr, cdim)
    mine = pl.BlockSpec(blk, lambda k, i, core_ref: (2 * k + core_ref[0], i, 0))
    slot = pl.BlockSpec(blk, lambda k, i, core_ref: (k, i, 0))
    return pl.pallas_call(
        body, name=name,
        grid_spec=pltpu.PrefetchScalarGridSpec(
            num_scalar_prefetch=1, grid=(N_CHIP, r // tr),
            in_specs=[mine] * na + [slot] * na, out_specs=[slot] * na),
        out_shape=[jax.ShapeDtypeStruct((N_CHIP, r, cdim), BF16)] * na,
        compiler_params=_params("parallel", "parallel"))(core, *parts, *recvs)


def _pair_sum_whole(mine, recvs, name):
    na = len(mine)

    def body(*refs):
        for a in range(na):
            refs[2 * na + a][...] = refs[a][...] + refs[na + a][...]

    return pl.pallas_call(body, name=name, out_shape=[jax.ShapeDtypeStruct(m.shape, F32) for m in mine])(*mine, *recvs)


def _allreduce_small(pack, name):
    shape = pack.shape

    def body(x_ref, o_ref, sib_ref, chip_ref, send_sems, recv_sems):
        x, y, c = _place()
        own = 2 * x + y
        chips = [(1 - x, y), (x, 1 - y), (1 - x, 1 - y)]
        to_sibling = pltpu.make_async_remote_copy(
            src_ref=x_ref, dst_ref=sib_ref, send_sem=send_sems.at[0], recv_sem=recv_sems.at[0],
            device_id=(x, y, 1 - c), device_id_type=MESH)
        to_sibling.start()
        to_sibling.wait()
        chip_ref[own] = x_ref[...] + sib_ref[...]
        sent = [pltpu.make_async_remote_copy(
            src_ref=chip_ref.at[own], dst_ref=chip_ref.at[own], send_sem=send_sems.at[1 + j],
            recv_sem=recv_sems.at[1 + own], device_id=(*chip, c), device_id_type=MESH) for j, chip in enumerate(chips)]
        for cp in sent:
            cp.start()
        for chip in chips:
            k = 2 * chip[0] + chip[1]
            pltpu.make_async_remote_copy(
                src_ref=chip_ref.at[k], dst_ref=chip_ref.at[k], send_sem=send_sems.at[1],
                recv_sem=recv_sems.at[1 + k], device_id=(*chip, c), device_id_type=MESH).wait_recv()
        for cp in sent:
            cp.wait_send()
        o_ref[...] = (chip_ref[0] + chip_ref[1]) + (chip_ref[2] + chip_ref[3])

    return pl.pallas_call(
        body, name=name, out_shape=jax.ShapeDtypeStruct(shape, F32),
        in_specs=[pl.BlockSpec(memory_space=pltpu.VMEM)], out_specs=pl.BlockSpec(memory_space=pltpu.VMEM),
        scratch_shapes=[pltpu.VMEM(shape, F32), pltpu.VMEM((N_CHIP, *shape), F32),
                        pltpu.SemaphoreType.DMA((4,)), pltpu.SemaphoreType.DMA((1 + N_CHIP,))],
    )(pack)


def _adamw(g, w, m, v):
    m = ADAM_B1 * m + (1.0 - ADAM_B1) * g
    v = ADAM_B2 * v + (1.0 - ADAM_B2) * (g * g)
    m_hat = m / (1.0 - ADAM_B1 ** ADAM_STEP)
    v_hat = v / (1.0 - ADAM_B2 ** ADAM_STEP)
    delta = -ADAM_LR * (m_hat / (jnp.sqrt(v_hat) + ADAM_EPS) + ADAM_WD * w)
    return delta, m, v


def _adam_parts(parts, w, m, v, name, tr=None):
    npart, r, c = parts.shape
    tr = r if tr is None else min(tr, r)

    def body(p_ref, w_ref, m_ref, v_ref, g_ref, d_ref, nm_ref, nv_ref):
        g = p_ref[0].astype(F32)
        for k in range(1, npart):
            g = g + p_ref[k].astype(F32)
        g_ref[...] = g
        d_ref[...], nm_ref[...], nv_ref[...] = _adamw(g, w_ref[...], m_ref[...], v_ref[...])

    tile = pl.BlockSpec((tr, c), lambda i: (i, 0))
    return pl.pallas_call(
        body, name=name, grid=(r // tr,),
        in_specs=[pl.BlockSpec((npart, tr, c), lambda i: (0, i, 0)), tile, tile, tile],
        out_specs=[tile] * 4, out_shape=[jax.ShapeDtypeStruct((r, c), F32)] * 4,
        compiler_params=_params("parallel"))(parts, w, m, v)


def _block_diag(w):
    w4 = w.reshape(N_GROUPS, 4, RNN_BLOCK_W, RNN_BLOCK_W)
    eye = jnp.eye(4, dtype=w.dtype)
    return jnp.einsum("gbij,bc->gbicj", w4, eye).reshape(N_GROUPS, GROUP_W, GROUP_W).astype(BF16)


SMALL_ROWS = 16
ROW_PRE_G, ROW_BGATE, ROW_CONV_B, ROW_B_A, ROW_B_X, ROW_LAM, ROW_POST_G, ROW_LOSS, ROW_SINKS, ROW_CONV_W = 0, 1, 3, 4, 5, 6, 7, 8, 9, 10


def _pack_stats(st_pre, st_merge, st_rnn, st_post, st_sink, name):
    d = D_MODEL

    def body(pre_ref, mg_ref, rnn_ref, post_ref, sink_ref, o_ref):
        rnn = rnn_ref[...]
        sinks = jnp.concatenate([sink_ref[0:1, :], jnp.zeros((1, d - LANE), F32)], axis=1)
        o_ref[0:8, :] = _rows8([pre_ref[0:1, :], mg_ref[0:1, :], mg_ref[1:2, :], rnn[0:1], rnn[1:2], rnn[2:3], rnn[3:4],
                                post_ref[0:1, :]], d)
        o_ref[8:16, :] = _rows8([post_ref[1:2, :], sinks, rnn[4:5], rnn[5:6], rnn[6:7], rnn[7:8]], d)

    return pl.pallas_call(body, name=name, out_shape=jax.ShapeDtypeStruct((SMALL_ROWS, d), F32))(
        st_pre, st_merge, st_rnn, st_post, st_sink)


def _adam_small(total, w, m, v, name):
    d = D_MODEL
    n_in = len(w)

    def pack(refs):
        pre, bg, cbias, ba, bx, lam, post, sinks = [r[...] for r in refs]
        top = _rows8([pre, bg[:, 0:d], bg[:, d:2 * d], cbias, ba, bx, lam, post], d)
        return jnp.concatenate([top, _rows8([jnp.zeros((1, d), F32), sinks], d)], axis=0)

    def body(*refs):
        p_ref = refs[0]
        w_refs, m_refs, v_refs = (refs[1 + k * n_in:1 + (k + 1) * n_in] for k in range(3))
        outs = refs[1 + 3 * n_in:]
        g = p_ref[...]
        res = (g,) + _adamw(g, pack(w_refs), pack(m_refs), pack(v_refs))
        for k in range(4):
            outs[k][...] = res[k]
            outs[4 + k][...] = jnp.concatenate([res[k][ROW_BGATE:ROW_BGATE + 1], res[k][ROW_BGATE + 1:ROW_BGATE + 2]], axis=1)

    return pl.pallas_call(
        body, name=name,
        out_shape=[jax.ShapeDtypeStruct((SMALL_ROWS, d), F32)] * 4 + [jax.ShapeDtypeStruct((1, 2 * d), F32)] * 4,
    )(total, *w, *m, *v)


def kernel(x, pre_norm_g, w_in, b_gate, conv_w, conv_b, w_rg_a, b_rg_a, w_rg_x, b_rg_x, lru_lambda, attn_sinks, w_rnn_out, w_attn_out, w_out, post_norm_g, loss_target, m_pre_norm_g, m_w_in, m_b_gate, m_conv_w, m_conv_b, m_w_rg_a, m_b_rg_a, m_w_rg_x, m_b_rg_x, m_lru_lambda, m_attn_sinks, m_w_rnn_out, m_w_attn_out, m_w_out, m_post_norm_g, v_pre_norm_g, v_w_in, v_b_gate, v_conv_w, v_conv_b, v_w_rg_a, v_b_rg_a, v_w_rg_x, v_b_rg_x, v_lru_lambda, v_attn_sinks, v_w_rnn_out, v_w_attn_out, v_w_out, v_post_norm_g):
    cx, cy, cc = _place()
    dev = 4 * cx + 2 * cy + cc
    core = jnp.reshape(cc, (1,)).astype(jnp.int32)

    w_in_t, m_in_t, v_in_t = (jnp.transpose(a[0]) for a in (w_in, m_w_in, v_w_in))
    wt_shard = w_in_t.astype(BF16)
    (wt_all,) = _run_ride(_gather_ride([wt_shard]), "gather_w_in")
    heads = jnp.arange(1, N_Q_HEADS + 1, dtype=F32)
    slopes = jnp.exp2(-ALIBI_MAX_BIAS * heads / N_Q_HEADS)
    b_a = b_rg_a.reshape(1, D_RNN)
    b_x = b_rg_x.reshape(1, D_RNN)
    p = dict(
        wt=wt_all.reshape(D_IN, D_MODEL),
        pre_g=pre_norm_g, post_g=post_norm_g, b_gate=b_gate, cb=conv_b,
        wbd_a=_block_diag(w_rg_a[0]), b_a=b_a, wbd_x=_block_diag(w_rg_x[0]), b_x=b_x, lam=lru_lambda,
        sm=jnp.pad(attn_sinks, ((0, 1), (0, 0))) + jnp.pad(slopes[None, :], ((1, 0), (0, 0))))

    def late_unpack(landed):
        w_rnn_all, w_attn_all, w_out_all, cw_all = landed
        return dict(w_rnn=w_rnn_all.reshape(D_RNN, D_MODEL), w_attn=w_attn_all.reshape(D_MODEL, D_MODEL),
                    w_out=w_out_all.reshape(D_MODEL, D_MODEL), cw=jnp.transpose(cw_all, (1, 0, 2)).reshape(4, D_RNN))

    late_weights = (_gather_ride([w_rnn_out[0].astype(BF16), w_attn_out[0].astype(BF16), w_out[0].astype(BF16), conv_w[0]]),
                    late_unpack)
    flat = (RNN_BLOCKS * RNN_BLOCK_W, RNN_BLOCK_W)

    out_scatter = []

    def out_sibling(gw_rnn, gw_attn, gw_out):
        out_scatter.extend(gw.reshape(N_DEV, SHARD_OUT, D_MODEL) for gw in (gw_rnn, gw_attn, gw_out))
        return _sibling_ride(out_scatter, [])

    def out_chips(from_sibling, g_rg_a, g_rg_x):
        whole = [g_rg_a.reshape(flat), g_rg_x.reshape(flat)]
        rg_sibling = _run_ride(_sibling_ride([], whole), "sibling_rg")
        return _chips_ride(_pair_sum_scatter(out_scatter, from_sibling, core, "pair_out"),
                           _pair_sum_whole(whole, rg_sibling, "pair_rg"))

    def reduce_in(gwt):
        scatter = [gwt.reshape(N_DEV, SHARD_IN, D_MODEL)]
        from_sibling = _run_ride(_sibling_ride(scatter, []), "sibling_in")
        return _chips_ride(_pair_sum_scatter(scatter, from_sibling, core, "pair_in"), [])

    g = _local_grads(x[0], loss_target[0], p, late_weights, (out_sibling, out_chips), reduce_in)
    small = _allreduce_small(
        _pack_stats(g["st_pre"], g["st_merge"], g["st_rnn"], g["st_post"], g["st_sink"], "pack_stats"), "allreduce_small")

    out = {}
    red = g["red_out"]
    out["w_in"] = [jnp.transpose(o) for o in _adam_parts(g["red_in"][0], w_in_t, m_in_t, v_in_t, "adam_w_in", tr=SHARD_IN // 2)]
    out["w_rnn_out"] = _adam_parts(red[0], w_rnn_out[0], m_w_rnn_out[0], v_w_rnn_out[0], "adam_w_rnn_out")
    out["w_attn_out"] = _adam_parts(red[1], w_attn_out[0], m_w_attn_out[0], v_w_attn_out[0], "adam_w_attn_out")
    out["w_out"] = _adam_parts(red[2], w_out[0], m_w_out[0], v_w_out[0], "adam_w_out")
    out["w_rg_a"] = _adam_parts(red[3], w_rg_a.reshape(flat), m_w_rg_a.reshape(flat), v_w_rg_a.reshape(flat), "adam_w_rg_a", tr=256)
    out["w_rg_x"] = _adam_parts(red[4], w_rg_x.reshape(flat), m_w_rg_x.reshape(flat), v_w_rg_x.reshape(flat), "adam_w_rg_x", tr=256)

    def rows(pre, bg, cbias, ba, bx, lam, post, sinks):
        return (pre, bg, cbias, ba.reshape(1, D_RNN), bx.reshape(1, D_RNN), lam, post,
                jnp.pad(sinks, ((0, 0), (0, D_MODEL - N_Q_HEADS))))

    small_out = _adam_small(
        small,
        rows(pre_norm_g, b_gate, conv_b, b_rg_a, b_rg_x, lru_lambda, post_norm_g, attn_sinks),
        rows(m_pre_norm_g, m_b_gate, m_conv_b, m_b_rg_a, m_b_rg_x, m_lru_lambda, m_post_norm_g, m_attn_sinks),
        rows(v_pre_norm_g, v_b_gate, v_conv_b, v_b_rg_a, v_b_rg_x, v_lru_lambda, v_post_norm_g, v_attn_sinks),
        "adam_small")
    packed, bgate_out = small_out[:4], small_out[4:]
    g_cw = lax.dynamic_slice(packed[0][ROW_CONV_W:ROW_CONV_W + 4], (0, dev * SHARD_OUT), (4, SHARD_OUT))
    out["conv_w"] = _adam_parts(g_cw[None], conv_w[0], m_conv_w[0], v_conv_w[0], "adam_conv_w")

    def unpack(kind, name):
        if name == "b_gate":
            return bgate_out[kind]
        row = dict(pre_norm_g=ROW_PRE_G, conv_b=ROW_CONV_B, b_rg_a=ROW_B_A, b_rg_x=ROW_B_X, lru_lambda=ROW_LAM,
                   post_norm_g=ROW_POST_G, attn_sinks=ROW_SINKS)[name]
        r = packed[kind][row:row + 1]
        if name == "attn_sinks":
            return r[:, 0:N_Q_HEADS]
        if name in ("b_rg_a", "b_rg_x"):
            return r.reshape(1, RNN_BLOCKS, RNN_BLOCK_W)
        return r

    shapes = dict(w_in=(1, D_MODEL, SHARD_IN), w_rnn_out=(1, SHARD_OUT, D_MODEL), w_attn_out=(1, SHARD_OUT, D_MODEL),
                  w_out=(1, SHARD_OUT, D_MODEL), w_rg_a=(1, RNN_BLOCKS, RNN_BLOCK_W, RNN_BLOCK_W),
                  w_rg_x=(1, RNN_BLOCKS, RNN_BLOCK_W, RNN_BLOCK_W), conv_w=(1, 4, SHARD_OUT))
    weights = ["pre_norm_g", "w_in", "b_gate", "conv_w", "conv_b", "w_rg_a", "b_rg_a", "w_rg_x", "b_rg_x",
               "lru_lambda", "attn_sinks", "w_rnn_out", "w_attn_out", "w_out", "post_norm_g"]
    results = []
    for kind in range(4):
        for name in weights:
            if name in out:
                results.append(out[name][kind].reshape(shapes[name]))
            else:
                results.append(unpack(kind, name))
    loss = 0.5 / D_MODEL * jnp.sum(packed[0][ROW_LOSS])
    return (loss, g["grad_x"][None], *results)
```

```python
import functools

import jax
import jax.numpy as jnp
from jax import lax
from jax.experimental import pallas as pl
from jax.experimental.pallas import tpu as pltpu

F32 = jnp.float32
BF16 = jnp.bfloat16

D_MODEL = 1024
D_RNN = 1024
RNN_BLOCKS = 16
RNN_BLOCK_W = 64
LRU_C = 8.0
N_Q_HEADS = 16
HEAD_DIM = 64
D_KV = 256
BLOCK = 128
ALIBI_MAX_BIAS = 8.0
EPS = 1e-6
D_IN = 6656
N_DEV = 8
N_CHIP = 4
SHARD_IN = D_IN // N_DEV
SHARD_OUT = D_MODEL // N_DEV
ATTN_SCALE = HEAD_DIM ** -0.5
MASKED = -1e30

ADAM_LR = 0.001
ADAM_B1 = 0.9
ADAM_B2 = 0.999
ADAM_EPS = 1e-08
ADAM_WD = 0.01
ADAM_STEP = 10

VMEM_LIMIT_BYTES = 52 * 1024 * 1024
LANE = 128
GROUP_W = 256
N_GROUPS = D_RNN // GROUP_W
SEG_CHUNK = 512

NT_DIMS = (((1,), (1,)), ((), ()))
TN_DIMS = (((0,), (0,)), ((), ()))
MESH = pl.DeviceIdType.MESH
ANY = pl.BlockSpec(memory_space=pl.ANY)


def _params(*semantics):
    return pltpu.CompilerParams(dimension_semantics=semantics, vmem_limit_bytes=VMEM_LIMIT_BYTES)


def _sigmoid(x):
    return 0.5 * jnp.tanh(0.5 * x) + 0.5


def _log1p(e):
    u = 1.0 + e
    den = jnp.where(u == 1.0, 1.0, u - 1.0)
    return jnp.where(u == 1.0, e, jnp.log(u) * (e / den))


def _softplus(z):
    return jnp.maximum(z, 0.0) + _log1p(jnp.exp(-jnp.abs(z)))


def _rows8(rows, width):
    idx = lax.broadcasted_iota(jnp.int32, (8, width), 0)
    out = jnp.zeros((8, width), F32)
    for r, v in enumerate(rows):
        out = jnp.where(idx == r, v, out)
    return out


class _Ride:
    def __init__(self, arrays, out_shapes, scratch_shapes, start, finish):
        self.arrays, self.out_shapes, self.scratch_shapes = list(arrays), list(out_shapes), list(scratch_shapes)
        self.start, self.finish = start, finish


class _Hosted:
    def __init__(self, ride, n_in, n_out, n_scratch=0):
        self.ride = ride
        self.sizes = (n_in, len(ride.arrays) if ride else 0, n_out, len(ride.out_shapes) if ride else 0, n_scratch)
        self.arrays = ride.arrays if ride else []
        self.in_specs = [ANY] * len(self.arrays)
        self.out_shapes = ride.out_shapes if ride else []
        self.out_specs = [ANY] * len(self.out_shapes)
        self.scratch_shapes = ride.scratch_shapes if ride else []

    def split(self, refs):
        n_in, r_in, n_out, r_out, n_scr = self.sizes
        cuts = [0, n_in, n_in + r_in, n_in + r_in + n_out, n_in + r_in + n_out + r_out, n_in + r_in + n_out + r_out + n_scr]
        host_in, ride_in, host_out, ride_out, host_scr = (refs[cuts[k]:cuts[k + 1]] for k in range(5))
        ride_scr = refs[cuts[5]:]

        def start(when):
            if self.ride is not None:
                pl.when(when)(lambda: self.ride.start(ride_in, ride_out, ride_scr))

        def finish(when):
            if self.ride is not None:
                pl.when(when)(lambda: self.ride.finish(ride_in, ride_out, ride_scr))

        return tuple(host_in) + tuple(host_out) + tuple(host_scr), start, finish

    def results(self, outs, n_out):
        outs = list(outs) if isinstance(outs, (list, tuple)) else [outs]
        return outs[:n_out], outs[n_out:]


def _run_ride(ride, name):
    n_in, n_out = len(ride.arrays), len(ride.out_shapes)

    def body(*refs):
        ins, outs, sems = refs[:n_in], refs[n_in:n_in + n_out], refs[n_in + n_out:]
        ride.start(ins, outs, sems)
        ride.finish(ins, outs, sems)

    return pl.pallas_call(
        body, name=name, in_specs=[ANY] * n_in, out_specs=[ANY] * n_out,
        out_shape=ride.out_shapes, scratch_shapes=ride.scratch_shapes)(*ride.arrays)


def _load_resident(w_hbm, w_vmem, sems, first):
    def piece(c):
        rows = pl.ds(c * SEG_CHUNK, SEG_CHUNK)
        return pltpu.make_async_copy(w_hbm.at[rows], w_vmem.at[rows], sems.at[c])

    @pl.when(first)
    def _():
        for c in range(w_vmem.shape[0] // SEG_CHUNK):
            piece(c).start()

    def ready(c):
        @pl.when(first)
        def _():
            piece(c).wait()

    return ready


def _norm_proj(x, g, wt, widths_dtypes, name, tm=512, ride=None):
    t, k = x.shape
    n = wt.shape[0]
    tm = min(tm, t)
    ni = t // tm
    n_out = len(widths_dtypes)
    n_pieces = n // SEG_CHUNK
    host = _Hosted(ride, 3, 1 + n_out, 3)

    def body(*refs):
        host_refs, start, finish = host.split(refs)
        x_ref, g_ref, wt_hbm, h_ref = host_refs[:4]
        o_refs = host_refs[4:4 + n_out]
        w_vmem, stage_ref, sems = host_refs[4 + n_out:]
        i = pl.program_id(0)
        start(i == 0)

        def piece(c):
            return pltpu.make_async_copy(wt_hbm.at[pl.ds(c * SEG_CHUNK, SEG_CHUNK)], stage_ref.at[c % 2], sems.at[c % 2])

        @pl.when(i == 0)
        def _():
            piece(0).start()
            piece(1).start()
            for c in range(n_pieces):
                piece(c).wait()
                w_vmem[:, c * SEG_CHUNK:(c + 1) * SEG_CHUNK] = stage_ref[c % 2].T
                if c + 2 < n_pieces:
                    piece(c + 2).start()

        xv = x_ref[...]
        hv = (xv * lax.rsqrt(jnp.mean(xv * xv, axis=-1, keepdims=True) + EPS) * g_ref[...]).astype(BF16)
        h_ref[...] = hv
        c = 0
        for o_ref, (width, _) in zip(o_refs, widths_dtypes):
            for kk in range(width // SEG_CHUNK):
                o_ref[:, kk * SEG_CHUNK:(kk + 1) * SEG_CHUNK] = jnp.dot(
                    hv, w_vmem[:, c * SEG_CHUNK:(c + 1) * SEG_CHUNK], preferred_element_type=F32).astype(o_ref.dtype)
                c += 1
        finish(i == ni - 1)

    outs = pl.pallas_call(
        body, name=name, grid=(ni,),
        in_specs=[pl.BlockSpec((tm, k), lambda i: (i, 0)), pl.BlockSpec((1, k), lambda i: (0, 0)), ANY] + host.in_specs,
        out_specs=[pl.BlockSpec((tm, w), lambda i: (i, 0)) for w, _ in ((k, BF16),) + tuple(widths_dtypes)] + host.out_specs,
        out_shape=[jax.ShapeDtypeStruct((t, w), dt) for w, dt in ((k, BF16),) + tuple(widths_dtypes)] + host.out_shapes,
        scratch_shapes=[pltpu.VMEM((k, n), wt.dtype), pltpu.VMEM((2, SEG_CHUNK, k), wt.dtype),
                        pltpu.SemaphoreType.DMA((2,))] + host.scratch_shapes,
        compiler_params=_params("arbitrary"))(x, g, wt, *host.arrays)
    res, landed = host.results(outs, 1 + n_out)
    return (res, landed) if ride else res


def _mm_tn(a, b, name, tm=512, tk=4096):
    ktok, m = a.shape
    n = b.shape[1]
    tk = min(tk, ktok)

    def body(a_ref, b_ref, o_ref):
        @pl.when(pl.program_id(1) == 0)
        def _():
            o_ref[...] = jnp.zeros_like(o_ref)

        o_ref[...] += lax.dot_general(a_ref[...], b_ref[...], TN_DIMS, preferred_element_type=F32)

    return pl.pallas_call(
        body, name=name, grid=(m // tm, ktok // tk),
        in_specs=[pl.BlockSpec((tk, tm), lambda i, kk: (kk, i)), pl.BlockSpec((tk, n), lambda i, kk: (kk, 0))],
        out_specs=pl.BlockSpec((tm, n), lambda i, kk: (i, 0)),
        out_shape=jax.ShapeDtypeStruct((m, n), F32),
        compiler_params=_params("parallel", "arbitrary"))(a, b)


def _segment_chunks(segs):
    bounds = [0]
    for s in segs:
        bounds.append(bounds[-1] + s.shape[1] // SEG_CHUNK)
    return bounds


def _input_grad(segs, wt, x, g, dy, name, tm=512, ride=None):
    m = segs[0].shape[0]
    rows, n = wt.shape
    tm = min(tm, m)
    bounds = _segment_chunks(segs)
    n_seg = len(segs)
    ni = m // tm
    host = _Hosted(ride, n_seg + 4, 2, 2)

    def body(*refs):
        host_refs, start, finish = host.split(refs)
        a_refs = host_refs[:n_seg]
        wt_hbm, x_ref, g_ref, dy_ref, gx_ref, st_ref, wt_vmem, sems = host_refs[n_seg:]
        i = pl.program_id(0)
        start(i == 0)

        @pl.when(i == 0)
        def _():
            st_ref[...] = jnp.zeros_like(st_ref)

        ready = _load_resident(wt_hbm, wt_vmem, sems, i == 0)
        dh = None
        for s in range(n_seg):
            for c in range(bounds[s], bounds[s + 1]):
                ready(c)
            part = jnp.dot(a_refs[s][...], wt_vmem[bounds[s] * SEG_CHUNK:bounds[s + 1] * SEG_CHUNK, :], preferred_element_type=F32)
            dh = part if dh is None else dh + part
        xv = x_ref[...]
        r = lax.rsqrt(jnp.mean(xv * xv, axis=-1, keepdims=True) + EPS)
        xn = xv * r
        dxn = dh * g_ref[...]
        gx_ref[...] = dy_ref[...] + r * (dxn - xn * jnp.mean(dxn * xn, axis=-1, keepdims=True))
        st_ref[...] += _rows8([jnp.sum(dh * xn, axis=0, keepdims=True)], n)
        finish(i == ni - 1)

    tile = pl.BlockSpec((tm, n), lambda i: (i, 0))
    outs = pl.pallas_call(
        body, name=name, grid=(ni,),
        in_specs=[pl.BlockSpec((tm, sg.shape[1]), lambda i: (i, 0)) for sg in segs]
        + [ANY, tile, pl.BlockSpec((1, n), lambda i: (0, 0)), tile] + host.in_specs,
        out_specs=[tile, pl.BlockSpec((8, n), lambda i: (0, 0))] + host.out_specs,
        out_shape=[jax.ShapeDtypeStruct((m, n), F32), jax.ShapeDtypeStruct((8, n), F32)] + host.out_shapes,
        scratch_shapes=[pltpu.VMEM((rows, n), wt.dtype), pltpu.SemaphoreType.DMA((rows // SEG_CHUNK,))] + host.scratch_shapes,
        compiler_params=_params("arbitrary"))(*segs, wt, x, g, dy, *host.arrays)
    res, landed = host.results(outs, 2)
    return (*res, landed) if ride else tuple(res)


def _mm_tn_seg(segs, b, name):
    ktok = segs[0].shape[0]
    n = b.shape[1]
    bounds = _segment_chunks(segs)
    n_seg = len(segs)
    nc = bounds[-1]
    seg_of = [s for s in range(n_seg) for _ in range(bounds[s], bounds[s + 1])]

    def body(*refs):
        a_hbm, b_hbm, o_ref = refs[:n_seg], refs[n_seg], refs[n_seg + 1]
        a_buf, b_vmem, a_sems, b_sem = refs[n_seg + 2:]
        c = pl.program_id(0)

        def fetch(cc):
            s = seg_of[cc]
            cols = pl.ds((cc - bounds[s]) * SEG_CHUNK, SEG_CHUNK)
            return pltpu.make_async_copy(a_hbm[s].at[:, cols], a_buf.at[cc % 2], a_sems.at[cc % 2])

        @pl.when(c == 0)
        def _():
            whole = pltpu.make_async_copy(b_hbm, b_vmem, b_sem)
            whole.start()
            fetch(0).start()
            whole.wait()

        for cc in range(nc):
            @pl.when(c == cc)
            def _(cc=cc):
                if cc + 1 < nc:
                    fetch(cc + 1).start()
                fetch(cc).wait()

        o_ref[...] = lax.dot_general(a_buf[c % 2], b_vmem[...], TN_DIMS, preferred_element_type=F32)

    return pl.pallas_call(
        body, name=name, grid=(nc,),
        in_specs=[ANY] * (n_seg + 1), out_specs=pl.BlockSpec((SEG_CHUNK, n), lambda c: (c, 0)),
        out_shape=jax.ShapeDtypeStruct((nc * SEG_CHUNK, n), F32),
        scratch_shapes=[pltpu.VMEM((2, ktok, SEG_CHUNK), segs[0].dtype), pltpu.VMEM((ktok, n), b.dtype),
                        pltpu.SemaphoreType.DMA((2,)), pltpu.SemaphoreType.DMA],
        compiler_params=_params("arbitrary"))(*segs, b)


def _branches_fwd(z_rnn, z_attn, ag_ml, b_gate, w_rnn, w_attn, w_out, x, target, g_post, name, tm=512):
    t, d = x.shape
    tm = min(tm, t)

    def body(zr_ref, za_ref, lr_ref, la_ref, br_ref, ba_ref, wr_ref, wa_ref, wo_ref, x_ref, t_ref, g_ref,
             brr_ref, bra_ref, mg_ref, do_ref, dy_ref, st_ref):
        @pl.when(pl.program_id(0) == 0)
        def _():
            st_ref[...] = jnp.zeros_like(st_ref)

        br_rnn = jnp.dot(zr_ref[...], wr_ref[...], preferred_element_type=F32)
        br_attn = jnp.dot(za_ref[...], wa_ref[...], preferred_element_type=F32)
        brr_ref[...] = br_rnn.astype(BF16)
        bra_ref[...] = br_attn.astype(BF16)
        g_rnn = _sigmoid(lr_ref[...] + br_ref[...])
        g_attn = _sigmoid(la_ref[...] + ba_ref[...])
        merged = (g_rnn * br_rnn + g_attn * br_attn).astype(BF16)
        mg_ref[...] = merged
        o = jnp.dot(merged, wo_ref[...], preferred_element_type=F32)
        g = g_ref[...]
        r = lax.rsqrt(jnp.mean(o * o, axis=-1, keepdims=True) + EPS)
        nrm = o * r
        err = x_ref[...] + nrm * g - t_ref[...]
        dy = err * (1.0 / d)
        dy_ref[...] = dy
        dn = dy * g
        do_ref[...] = (r * (dn - nrm * jnp.mean(dn * nrm, axis=-1, keepdims=True))).astype(BF16)
        st_ref[...] += _rows8([jnp.sum(dy * nrm, axis=0, keepdims=True), jnp.sum(err * err, axis=0, keepdims=True)], d)

    tile = pl.BlockSpec((tm, d), lambda i: (i, 0))
    weight = pl.BlockSpec((d, d), lambda i: (0, 0))
    bf = jax.ShapeDtypeStruct((t, d), BF16)
    return pl.pallas_call(
        body, name=name, grid=(t // tm,),
        in_specs=[tile, tile, pl.BlockSpec((tm, d), lambda i: (i, 1)), pl.BlockSpec((tm, d), lambda i: (i, 2)),
                  pl.BlockSpec((1, d), lambda i: (0, 0)), pl.BlockSpec((1, d), lambda i: (0, 1)),
                  weight, weight, weight, tile, tile, pl.BlockSpec((1, d), lambda i: (0, 0))],
        out_specs=[tile, tile, tile, tile, tile, pl.BlockSpec((8, d), lambda i: (0, 0))],
        out_shape=[bf, bf, bf, bf, jax.ShapeDtypeStruct((t, d), F32), jax.ShapeDtypeStruct((8, d), F32)],
        compiler_params=_params("arbitrary"))(z_rnn, z_attn, ag_ml, ag_ml, b_gate, b_gate, w_rnn, w_attn, w_out, x, target, g_post)


def _branches_bwd(dout, br_rnn, br_attn, ag_ml, b_gate, w_rnn, w_attn, w_out, name, tm=512):
    t, d = br_rnn.shape
    tm = min(tm, t)

    def body(do_ref, r_ref, a_ref, lr_ref, la_ref, br_ref, ba_ref, wr_ref, wa_ref, wo_ref,
             dr_ref, da_ref, dl_ref, dzr_ref, dza_ref, st_ref, wt_ref):
        @pl.when(pl.program_id(0) == 0)
        def _():
            st_ref[...] = jnp.zeros_like(st_ref)
            wt_ref[0] = wo_ref[...].T
            wt_ref[1] = wr_ref[...].T
            wt_ref[2] = wa_ref[...].T

        dm = jnp.dot(do_ref[...], wt_ref[0], preferred_element_type=F32)
        g_rnn = _sigmoid(lr_ref[...] + br_ref[...])
        g_attn = _sigmoid(la_ref[...] + ba_ref[...])
        dbr_rnn = (dm * g_rnn).astype(BF16)
        dbr_attn = (dm * g_attn).astype(BF16)
        dr_ref[...] = dbr_rnn
        da_ref[...] = dbr_attn
        dl_rnn = dm * r_ref[...].astype(F32) * g_rnn * (1.0 - g_rnn)
        dl_attn = dm * a_ref[...].astype(F32) * g_attn * (1.0 - g_attn)
        dl_ref[:, 0:d] = dl_rnn.astype(BF16)
        dl_ref[:, d:2 * d] = dl_attn.astype(BF16)
        st_ref[...] += _rows8([jnp.sum(dl_rnn, axis=0, keepdims=True), jnp.sum(dl_attn, axis=0, keepdims=True)], d)
        dzr_ref[...] = jnp.dot(dbr_rnn, wt_ref[1], preferred_element_type=F32)
        dza_ref[...] = jnp.dot(dbr_attn, wt_ref[2], preferred_element_type=F32)

    tile = pl.BlockSpec((tm, d), lambda i: (i, 0))
    weight = pl.BlockSpec((d, d), lambda i: (0, 0))
    bf = jax.ShapeDtypeStruct((t, d), BF16)
    return pl.pallas_call(
        body, name=name, grid=(t // tm,),
        in_specs=[tile, tile, tile, pl.BlockSpec((tm, d), lambda i: (i, 1)), pl.BlockSpec((tm, d), lambda i: (i, 2)),
                  pl.BlockSpec((1, d), lambda i: (0, 0)), pl.BlockSpec((1, d), lambda i: (0, 1)), weight, weight, weight],
        out_specs=[tile, tile, pl.BlockSpec((tm, 2 * d), lambda i: (i, 0)), tile, tile, pl.BlockSpec((8, d), lambda i: (0, 0))],
        out_shape=[bf, bf, jax.ShapeDtypeStruct((t, 2 * d), BF16), jax.ShapeDtypeStruct((t, d), F32),
                   jax.ShapeDtypeStruct((t, d), F32), jax.ShapeDtypeStruct((8, d), F32)],
        scratch_shapes=[pltpu.VMEM((3, d, d), BF16)],
        compiler_params=_params("arbitrary"))(dout, br_rnn, br_attn, ag_ml, ag_ml, b_gate, b_gate, w_rnn, w_attn, w_out)


def _lru_gates(c, wa, ba, wx, bx, sp):
    cb = c.astype(BF16)
    r = _sigmoid(jnp.dot(cb, wa, preferred_element_type=F32) + ba)
    ig = _sigmoid(jnp.dot(cb, wx, preferred_element_type=F32) + bx)
    log_a = (-LRU_C) * r * sp
    a = jnp.exp(log_a)
    mult = jnp.sqrt(-jnp.tanh(log_a) * (a * a + 1.0))
    return cb, r, ig, a, mult


SUBLANES = 8


def _scan_fwd(a, u, carry, tt):
    w = a.shape[1]
    ng = tt // SUBLANES
    a3 = a.reshape(ng, SUBLANES, w)
    u3 = u.reshape(ng, SUBLANES, w)
    sub = lax.broadcasted_iota(jnp.int32, (ng, SUBLANES, w), 1)
    d = 1
    while d < SUBLANES:
        keep = sub >= d
        u3 = u3 + a3 * jnp.where(keep, pltpu.roll(u3, d, 1), 0.0)
        a3 = a3 * jnp.where(keep, pltpu.roll(a3, d, 1), 1.0)
        d *= 2
    out = []
    for g in range(ng):
        hg = u3[g] + a3[g] * carry
        out.append(hg)
        carry = hg[SUBLANES - 1:SUBLANES, :]
    return jnp.concatenate(out, axis=0)


def _scan_rev(b, g, carry, tt):
    w = b.shape[1]
    ng = tt // SUBLANES
    b3 = b.reshape(ng, SUBLANES, w)
    g3 = g.reshape(ng, SUBLANES, w)
    sub = lax.broadcasted_iota(jnp.int32, (ng, SUBLANES, w), 1)
    d = 1
    while d < SUBLANES:
        keep = sub < SUBLANES - d
        g3 = g3 + b3 * jnp.where(keep, pltpu.roll(g3, SUBLANES - d, 1), 0.0)
        b3 = b3 * jnp.where(keep, pltpu.roll(b3, SUBLANES - d, 1), 1.0)
        d *= 2
    out = [None] * ng
    for k in range(ng - 1, -1, -1):
        hk = g3[k] + b3[k] * carry
        out[k] = hk
        carry = hk[0:1, :]
    return jnp.concatenate(out, axis=0)


def _conv_taps(cw, bias, x, ext_ref, tt):
    x2 = ext_ref[7:7 + tt, :]
    x1 = ext_ref[6:6 + tt, :]
    x0 = ext_ref[5:5 + tt, :]
    c = bias + cw[3:4] * x + cw[2:3] * x2 + cw[1:2] * x1 + cw[0:1] * x0
    return c, x2, x1, x0


def _rnn_fwd(rx_rg, cw, cb, wa, ba, wx, bx, lam, name, tt=512):
    t = rx_rg.shape[0]
    tt = min(tt, t)
    w = GROUP_W

    def body(rx_ref, rg_ref, cw_ref, cb_ref, wa_ref, ba_ref, wx_ref, bx_ref, lam_ref, y_ref, z_ref, ext_ref, hc_ref):
        @pl.when(pl.program_id(1) == 0)
        def _():
            ext_ref[0:8, :] = jnp.zeros((8, w), F32)
            hc_ref[...] = jnp.zeros((8, w), F32)

        x = rx_ref[...]
        ext_ref[8:8 + tt, :] = x
        c, _, _, _ = _conv_taps(cw_ref[...], cb_ref[...], x, ext_ref, tt)
        ext_ref[0:8, :] = x[tt - 8:tt, :]
        sp = _softplus(-lam_ref[...])
        _, _, ig, a, mult = _lru_gates(c, wa_ref[...], ba_ref[...], wx_ref[...], bx_ref[...], sp)
        h = _scan_fwd(a, mult * (ig * c), hc_ref[7:8, :], tt)
        hc_ref[...] = h[tt - 8:tt, :]
        y_ref[...] = h
        rg = rg_ref[...]
        z_ref[...] = (h * rg * _sigmoid(rg)).astype(BF16)

    vec = pl.BlockSpec((1, w), lambda g, i: (0, g))
    mat = pl.BlockSpec((None, w, w), lambda g, i: (g, 0, 0))
    tile = pl.BlockSpec((tt, w), lambda g, i: (i, g))
    return pl.pallas_call(
        body, name=name, grid=(N_GROUPS, t // tt),
        in_specs=[tile, pl.BlockSpec((tt, w), lambda g, i: (i, N_GROUPS + g)),
                  pl.BlockSpec((4, w), lambda g, i: (0, g)), vec, mat, vec, mat, vec, vec],
        out_specs=[tile, tile],
        out_shape=[jax.ShapeDtypeStruct((t, D_RNN), F32), jax.ShapeDtypeStruct((t, D_RNN), BF16)],
        scratch_shapes=[pltpu.VMEM((tt + 8, w), F32), pltpu.VMEM((8, w), F32)],
        compiler_params=_params("parallel", "arbitrary"))(rx_rg, rx_rg, cw, cb, wa, ba, wx, bx, lam)


def _rnn_bwd(rx_rg, y, dz, cw, cb, wa, ba, wx, bx, lam, name, tt=512, ride=None):
    t = rx_rg.shape[0]
    tt = min(tt, t)
    nt = t // tt
    w = GROUP_W

    host = _Hosted(ride, 13, 5, 6)

    def body(*refs):
        host_refs, start, finish = host.split(refs)
        (rx_ref, rg_ref, rxt_ref, y_ref, yt_ref, dz_ref, cw_ref, cb_ref, wa_ref, ba_ref, wx_ref, bx_ref, lam_ref,
         drx_ref, drg_ref, st_ref, gda_ref, gdx_ref, ext_ref, dcx_ref, wcar_ref, acar_ref, dwa_ref, dwx_ref) = host_refs
        ii = pl.program_id(1)
        start((pl.program_id(0) == 0) & (ii == 0))

        @pl.when(ii == 0)
        def _():
            wcar_ref[...] = jnp.zeros((8, w), F32)
            acar_ref[...] = jnp.zeros((8, w), F32)
            dcx_ref[tt:tt + 8, :] = jnp.zeros((8, w), F32)
            st_ref[...] = jnp.zeros_like(st_ref)
            dwa_ref[...] = jnp.zeros_like(dwa_ref)
            dwx_ref[...] = jnp.zeros_like(dwx_ref)

        has_prev = jnp.where(ii == nt - 1, 0.0, 1.0)
        x = rx_ref[...]
        ext_ref[0:8, :] = rxt_ref[...] * has_prev
        ext_ref[8:8 + tt, :] = x
        cwv = cw_ref[...]
        c, x2, x1, x0 = _conv_taps(cwv, cb_ref[...], x, ext_ref, tt)
        lam = lam_ref[...]
        sp = _softplus(-lam)
        wa = wa_ref[...]
        wx = wx_ref[...]
        cb16, r, ig, a, mult = _lru_gates(c, wa, ba_ref[...], wx, bx_ref[...], sp)

        rg = rg_ref[...]
        sg = _sigmoid(rg)
        dz = dz_ref[...]
        yv = y_ref[...]
        drg_ref[...] = (dz * yv * (sg * (1.0 + rg * (1.0 - sg)))).astype(BF16)

        row = lax.broadcasted_iota(jnp.int32, (tt, w), 0)
        b = jnp.where(row < tt - 1, pltpu.roll(a, tt - 1, 0), acar_ref[0:1, :])
        dh = _scan_rev(b, dz * (rg * sg), wcar_ref[0:1, :], tt)
        wcar_ref[...] = dh[0:8, :]
        acar_ref[...] = a[0:8, :]

        hprev = jnp.where(row >= 1, pltpu.roll(yv, 1, 0), yt_ref[7:8, :] * has_prev)
        dmult = dh * (ig * c)
        dig = dh * mult * c
        dlog_a = dh * hprev * a - dmult * (a * a / mult)
        dpa = dlog_a * ((-LRU_C) * sp) * r * (1.0 - r)
        dpx = dig * ig * (1.0 - ig)
        dsp = jnp.sum(dlog_a * r, axis=0, keepdims=True) * (-LRU_C)
        dlam = dsp * (-_sigmoid(-lam))
        dpa16 = dpa.astype(BF16)
        dpx16 = dpx.astype(BF16)
        dwa_ref[...] += lax.dot_general(cb16, dpa16, TN_DIMS, preferred_element_type=F32)
        dwx_ref[...] += lax.dot_general(cb16, dpx16, TN_DIMS, preferred_element_type=F32)
        dc = (dh * mult * ig
              + lax.dot_general(dpa16, wa, NT_DIMS, preferred_element_type=F32)
              + lax.dot_general(dpx16, wx, NT_DIMS, preferred_element_type=F32))

        dcx_ref[0:tt, :] = dc
        drx = (cwv[3:4] * dc + cwv[2:3] * dcx_ref[1:1 + tt, :] + cwv[1:2] * dcx_ref[2:2 + tt, :]
               + cwv[0:1] * dcx_ref[3:3 + tt, :])
        drx_ref[...] = drx.astype(BF16)
        dcx_ref[tt:tt + 8, :] = dc[0:8, :]

        def colsum(v):
            return jnp.sum(v, axis=0, keepdims=True)

        st_ref[...] += _rows8([colsum(dc), colsum(dpa), colsum(dpx), dlam,
                               colsum(dc * x0), colsum(dc * x1), colsum(dc * x2), colsum(dc * x)], w)

        @pl.when(ii == nt - 1)
        def _():
            for blk in range(GROUP_W // RNN_BLOCK_W):
                rows = slice(blk * RNN_BLOCK_W, (blk + 1) * RNN_BLOCK_W)
                gda_ref[blk] = dwa_ref[rows, rows]
                gdx_ref[blk] = dwx_ref[rows, rows]

        finish((pl.program_id(0) == N_GROUPS - 1) & (ii == nt - 1))

    def rev(ii):
        return nt - 1 - ii

    def tail(g, ii):
        return (jnp.maximum(rev(ii) * (tt // 8) - 1, 0), g)

    vec = pl.BlockSpec((1, w), lambda g, ii: (0, g))
    mat = pl.BlockSpec((None, w, w), lambda g, ii: (g, 0, 0))
    tile = pl.BlockSpec((tt, w), lambda g, ii: (rev(ii), g))
    diag_shape = (N_GROUPS, GROUP_W // RNN_BLOCK_W, RNN_BLOCK_W, RNN_BLOCK_W)
    diag = pl.BlockSpec((None,) + diag_shape[1:], lambda g, ii: (g, 0, 0, 0))
    outs = pl.pallas_call(
        body, name=name, grid=(N_GROUPS, nt),
        in_specs=[tile, pl.BlockSpec((tt, w), lambda g, ii: (rev(ii), N_GROUPS + g)), pl.BlockSpec((8, w), tail),
                  tile, pl.BlockSpec((8, w), tail), tile,
                  pl.BlockSpec((4, w), lambda g, ii: (0, g)), vec, mat, vec, mat, vec, vec] + host.in_specs,
        out_specs=[tile, tile, pl.BlockSpec((8, w), lambda g, ii: (0, g)), diag, diag] + host.out_specs,
        out_shape=[jax.ShapeDtypeStruct((t, D_RNN), BF16), jax.ShapeDtypeStruct((t, D_RNN), BF16),
                   jax.ShapeDtypeStruct((8, D_RNN), F32),
                   jax.ShapeDtypeStruct(diag_shape, F32), jax.ShapeDtypeStruct(diag_shape, F32)] + host.out_shapes,
        scratch_shapes=[pltpu.VMEM((tt + 8, w), F32), pltpu.VMEM((tt + 8, w), F32), pltpu.VMEM((8, w), F32),
                        pltpu.VMEM((8, w), F32), pltpu.VMEM((w, w), F32), pltpu.VMEM((w, w), F32)] + host.scratch_shapes,
        compiler_params=_params("arbitrary" if ride else "parallel", "arbitrary"))(
            rx_rg, rx_rg, rx_rg, y, y, dz, cw, cb, wa, ba, wx, bx, lam, *host.arrays)
    res, landed = host.results(outs, 5)
    return (*res, landed) if ride else tuple(res)


def _half_mask(shape, half):
    lane = lax.broadcasted_iota(jnp.int32, shape, 1)
    return (lane >= HEAD_DIM) if half else (lane < HEAD_DIM)


def _dup_half(t, half):
    sel = jnp.where(_half_mask(t.shape, half), t, 0.0)
    return sel + pltpu.roll(sel, HEAD_DIM, 1)


def _band_geometry(n):
    qi = lax.broadcasted_iota(jnp.int32, (BLOCK, 2 * BLOCK), 0)
    kj = lax.broadcasted_iota(jnp.int32, (BLOCK, 2 * BLOCK), 1)
    dist = BLOCK + qi - kj
    first_key = jnp.where(n > 0, 0, BLOCK)
    valid = (dist >= 0) & (dist < BLOCK) & (kj >= first_key)
    return dist.astype(F32), valid


GROUP = 4


def _kv_dup(prev_ref, cur_ref, hk, scale=1.0):
    tile = hk // 2
    kt = jnp.concatenate([prev_ref[:, tile * LANE:(tile + 1) * LANE], cur_ref[:, tile * LANE:(tile + 1) * LANE]], axis=0)
    return (_dup_half(kt.astype(F32), hk % 2) * scale).astype(BF16)


def _fill_bias(bias_ref, sm_ref, n):
    distf, valid = _band_geometry(n)
    for head in range(N_Q_HEADS):
        bias_ref[head] = jnp.where(valid, -sm_ref[1, head] * distf, MASKED)


def _head_scores(q2s, half, kdup, bias):
    qm = jnp.where(_half_mask(q2s.shape, half), q2s, jnp.zeros_like(q2s))
    return qm, lax.dot_general(qm, kdup, NT_DIMS, preferred_element_type=F32) + bias


def _attn_specs(nb, clamp_last):
    def blk(n):
        return jnp.minimum(n, nb - 1) if clamp_last else n

    q_spec = pl.BlockSpec((BLOCK, D_MODEL), lambda n: (blk(n), 0))
    k_prev = pl.BlockSpec((BLOCK, D_KV), lambda n: (jnp.maximum(blk(n) - 1, 0), D_MODEL // D_KV))
    k_cur = pl.BlockSpec((BLOCK, D_KV), lambda n: (blk(n), D_MODEL // D_KV))
    v_prev = pl.BlockSpec((BLOCK, D_KV), lambda n: (jnp.maximum(blk(n) - 1, 0), D_MODEL // D_KV + 1))
    v_cur = pl.BlockSpec((BLOCK, D_KV), lambda n: (blk(n), D_MODEL // D_KV + 1))
    return q_spec, k_prev, k_cur, v_prev, v_cur


def _attn_fwd(sm, qkv, ag_ml, name):
    t = qkv.shape[0]
    nb = t // BLOCK

    def body(sm_ref, q_ref, kp_ref, kc_ref, vp_ref, vc_ref, ag_ref, y_ref, z_ref, lse_ref, bias_ref):
        n = pl.program_id(0)

        @pl.when(n <= 1)
        def _():
            _fill_bias(bias_ref, sm_ref, n)

        lane = lax.broadcasted_iota(jnp.int32, (BLOCK, LANE), 1)
        low = lane < HEAD_DIM
        lse = jnp.zeros((BLOCK, LANE), F32)
        for hk in range(N_Q_HEADS // GROUP):
            kdup = _kv_dup(kp_ref, kc_ref, hk)
            vdup = _kv_dup(vp_ref, vc_ref, hk)
            for k in (0, 1):
                c = slice((2 * hk + k) * LANE, (2 * hk + k + 1) * LANE)
                q2s = q_ref[:, c] * ATTN_SCALE
                outs = []
                for half in (0, 1):
                    head = GROUP * hk + 2 * k + half
                    _, s = _head_scores(q2s, half, kdup, bias_ref[head])
                    sink = sm_ref[0, head]
                    m = jnp.maximum(jnp.max(s, axis=1, keepdims=True), sink)
                    e = jnp.exp(s - m)
                    l = jnp.sum(e, axis=1, keepdims=True) + jnp.exp(sink - m)
                    outs.append(jnp.dot((e * (1.0 / l)).astype(BF16), vdup, preferred_element_type=F32))
                    lse = jnp.where(lane == head, m + jnp.log(l), lse)
                yt = jnp.where(low, outs[0], outs[1])
                y_ref[:, c] = yt
                ag = ag_ref[:, c]
                z_ref[:, c] = (yt * ag * _sigmoid(ag)).astype(BF16)
        lse_ref[...] = lse

    q_spec, k_prev, k_cur, v_prev, v_cur = _attn_specs(nb, False)
    wide = pl.BlockSpec((BLOCK, D_MODEL), lambda n: (n, 0))
    return pl.pallas_call(
        body, name=name, grid=(nb,),
        in_specs=[pl.BlockSpec(memory_space=pltpu.SMEM), q_spec, k_prev, k_cur, v_prev, v_cur, wide],
        out_specs=[wide, wide, pl.BlockSpec((BLOCK, LANE), lambda n: (n, 0))],
        out_shape=[jax.ShapeDtypeStruct((t, D_MODEL), F32), jax.ShapeDtypeStruct((t, D_MODEL), BF16),
                   jax.ShapeDtypeStruct((t, LANE), F32)],
        scratch_shapes=[pltpu.VMEM((N_Q_HEADS, BLOCK, 2 * BLOCK), F32)],
        compiler_params=_params("arbitrary"))(sm, qkv, qkv, qkv, qkv, qkv, ag_ml)


def _attn_bwd(sm, qkv, ag_ml, y, lse, dz, name, ride=None):
    t = qkv.shape[0]
    nb = t // BLOCK
    host = _Hosted(ride, 10, 4, 3)

    def body(*refs):
        host_refs, start, finish = host.split(refs)
        (sm_ref, q_ref, kp_ref, kc_ref, vp_ref, vc_ref, ag_ref, y_ref, lse_ref, dz_ref,
         dq_ref, dkv_ref, dag_ref, ds_ref, ck_ref, cv_ref, bias_ref) = host_refs
        n = pl.program_id(0)
        start(n == 0)

        @pl.when(n == 0)
        def _():
            ck_ref[...] = jnp.zeros_like(ck_ref)
            cv_ref[...] = jnp.zeros_like(cv_ref)
            ds_ref[...] = jnp.zeros_like(ds_ref)

        @pl.when(n <= 1)
        def _():
            _fill_bias(bias_ref, sm_ref, n)

        @pl.when(n < nb)
        def _():
            lane8 = lax.broadcasted_iota(jnp.int32, (8, LANE), 1)
            row8 = lax.broadcasted_iota(jnp.int32, (8, LANE), 0)
            dsink = jnp.zeros((8, LANE), F32)
            dk_heads, dv_heads = [], []
            lse_tile = lse_ref[...]
            for hk in range(N_Q_HEADS // GROUP):
                kdup = _kv_dup(kp_ref, kc_ref, hk)
                ks = _kv_dup(kp_ref, kc_ref, hk, ATTN_SCALE)
                vdup = _kv_dup(vp_ref, vc_ref, hk)
                dk_acc = jnp.zeros((2 * BLOCK, LANE), F32)
                dv_acc = jnp.zeros((2 * BLOCK, LANE), F32)
                for k in (0, 1):
                    c = slice((2 * hk + k) * LANE, (2 * hk + k + 1) * LANE)
                    ag = ag_ref[:, c]
                    sg = _sigmoid(ag)
                    dzt = dz_ref[:, c]
                    yt = y_ref[:, c]
                    dag_ref[:, c] = (dzt * yt * (sg * (1.0 + ag * (1.0 - sg)))).astype(BF16)
                    dyt = dzt * (ag * sg)
                    q2s = q_ref[:, c] * ATTN_SCALE
                    dq_halves = []
                    for half in (0, 1):
                        head = GROUP * hk + 2 * k + half
                        qm, s = _head_scores(q2s, half, kdup, bias_ref[head])
                        lh = lse_tile[:, head:head + 1]
                        probs = jnp.exp(s - lh)
                        psink = jnp.exp(sm_ref[0, head] - lh)
                        dyh = jnp.where(_half_mask(dyt.shape, half), dyt, 0.0)
                        delta = jnp.sum(dyh * yt, axis=1, keepdims=True)
                        dyh16 = dyh.astype(BF16)
                        dp = lax.dot_general(dyh16, vdup, NT_DIMS, preferred_element_type=F32)
                        ds16 = (probs * (dp - delta)).astype(BF16)
                        dsink = dsink + jnp.where((row8 == 0) & (lane8 == head),
                                                  -jnp.sum(psink * delta, axis=0, keepdims=True), 0.0)
                        dq_halves.append(jnp.dot(ds16, ks, preferred_element_type=F32))
                        dk_acc = dk_acc + lax.dot_general(ds16, qm, TN_DIMS, preferred_element_type=F32)
                        dv_acc = dv_acc + lax.dot_general(probs.astype(BF16), dyh16, TN_DIMS, preferred_element_type=F32)
                    dq_ref[:, c] = jnp.where(_half_mask((BLOCK, LANE), 0), dq_halves[0], dq_halves[1]).astype(BF16)
                dk_heads.append(dk_acc + pltpu.roll(dk_acc, HEAD_DIM, 1))
                dv_heads.append(dv_acc + pltpu.roll(dv_acc, HEAD_DIM, 1))
            ds_ref[...] += dsink
            low = _half_mask((2 * BLOCK, LANE), 0)
            for tile in range(2):
                cols = slice(tile * LANE, (tile + 1) * LANE)
                dkt = jnp.where(low, dk_heads[2 * tile], dk_heads[2 * tile + 1])
                dvt = jnp.where(low, dv_heads[2 * tile], dv_heads[2 * tile + 1])
                dkv_ref[:, cols] = (ck_ref[:, cols] + dkt[0:BLOCK, :]).astype(BF16)
                dkv_ref[:, D_KV + tile * LANE:D_KV + (tile + 1) * LANE] = (cv_ref[:, cols] + dvt[0:BLOCK, :]).astype(BF16)
                ck_ref[:, cols] = dkt[BLOCK:2 * BLOCK, :]
                cv_ref[:, cols] = dvt[BLOCK:2 * BLOCK, :]

        @pl.when(n == nb)
        def _():
            dkv_ref[:, 0:D_KV] = ck_ref[...].astype(BF16)
            dkv_ref[:, D_KV:2 * D_KV] = cv_ref[...].astype(BF16)

        finish(n == nb)

    q_spec, k_prev, k_cur, v_prev, v_cur = _attn_specs(nb, True)
    wide = pl.BlockSpec((BLOCK, D_MODEL), lambda n: (jnp.minimum(n, nb - 1), 0))
    outs = pl.pallas_call(
        body, name=name, grid=(nb + 1,),
        in_specs=[pl.BlockSpec(memory_space=pltpu.SMEM), q_spec, k_prev, k_cur, v_prev, v_cur, wide, wide,
                  pl.BlockSpec((BLOCK, LANE), lambda n: (jnp.minimum(n, nb - 1), 0)), wide] + host.in_specs,
        out_specs=[wide, pl.BlockSpec((BLOCK, 2 * D_KV), lambda n: (jnp.maximum(n - 1, 0), 0)), wide,
                   pl.BlockSpec((8, LANE), lambda n: (0, 0))] + host.out_specs,
        out_shape=[jax.ShapeDtypeStruct((t, D_MODEL), BF16), jax.ShapeDtypeStruct((t, 2 * D_KV), BF16),
                   jax.ShapeDtypeStruct((t, D_MODEL), BF16), jax.ShapeDtypeStruct((8, LANE), F32)] + host.out_shapes,
        scratch_shapes=[pltpu.VMEM((BLOCK, D_KV), F32), pltpu.VMEM((BLOCK, D_KV), F32),
                        pltpu.VMEM((N_Q_HEADS, BLOCK, 2 * BLOCK), F32)] + host.scratch_shapes,
        compiler_params=_params("arbitrary"))(sm, qkv, qkv, qkv, qkv, qkv, ag_ml, y, lse, dz, *host.arrays)
    res, landed = host.results(outs, 4)
    return (*res, landed) if ride else tuple(res)


def _local_grads(x, target, p, late_weights=None, reduce_out=None, reduce_in=None):
    wt = p["wt"]
    splits = ((2 * D_RNN, F32), (D_MODEL + 2 * D_KV, BF16), (3 * D_MODEL, F32))
    if late_weights is None:
        h, rx_rg, qkv, ag_ml = _norm_proj(x, p["pre_g"], wt, splits, "norm_proj")
    else:
        (h, rx_rg, qkv, ag_ml), landed = _norm_proj(x, p["pre_g"], wt, splits, "norm_proj", ride=late_weights[0])
        p = {**p, **late_weights[1](landed)}
    lru =(p["cw"], p["cb"], p["wbd_a"], p["b_a"], p["wbd_x"], p["b_x"], p["lam"])
    y_rnn, z_rnn = _rnn_fwd(rx_rg, *lru, "rnn_fwd")
    y_attn, z_attn, lse = _attn_fwd(p["sm"], qkv, ag_ml, "attn_fwd")
    br_rnn, br_attn, merged, dout, dy, st_post = _branches_fwd(
        z_rnn, z_attn, ag_ml, p["b_gate"], p["w_rnn"], p["w_attn"], p["w_out"], x, target, p["post_g"], "branches_fwd")

    dbr_rnn, dbr_attn, d_ml, dz_rnn, dz_attn, st_merge = _branches_bwd(
        dout, br_rnn, br_attn, ag_ml, p["b_gate"], p["w_rnn"], p["w_attn"], p["w_out"], "branches_bwd")
    gw_out = _mm_tn(merged, dout, "gw_out")
    gw_rnn = _mm_tn(z_rnn, dbr_rnn, "gw_rnn")
    gw_attn = _mm_tn(z_attn, dbr_attn, "gw_attn")
    red_out = red_in = None
    if reduce_out is None:
        d_rx, d_rg, st_rnn, g_rg_a, g_rg_x = _rnn_bwd(rx_rg, y_rnn, dz_rnn, *lru, "rnn_bwd")
        dq, dkv, d_ag, st_sink = _attn_bwd(p["sm"], qkv, ag_ml, y_attn, lse, dz_attn, "attn_bwd")
    else:
        d_rx, d_rg, st_rnn, g_rg_a, g_rg_x, from_sibling = _rnn_bwd(
            rx_rg, y_rnn, dz_rnn, *lru, "rnn_bwd", ride=reduce_out[0](gw_rnn, gw_attn, gw_out))
        dq, dkv, d_ag, st_sink, red_out = _attn_bwd(p["sm"], qkv, ag_ml, y_attn, lse, dz_attn, "attn_bwd",
                                                    ride=reduce_out[1](from_sibling, g_rg_a, g_rg_x))

    segs = [d_rx, d_rg, dq, dkv, d_ag, d_ml]
    gwt = _mm_tn_seg(segs, h, "gw_in")
    if reduce_in is None:
        grad_x, st_pre = _input_grad(segs, wt, x, p["pre_g"], dy, "input_grad")
    else:
        grad_x, st_pre, red_in = _input_grad(segs, wt, x, p["pre_g"], dy, "input_grad", ride=reduce_in(gwt))
    return dict(grad_x=grad_x, gwt=gwt, gw_rnn=gw_rnn, gw_attn=gw_attn, gw_out=gw_out,
                st_post=st_post, st_merge=st_merge, st_rnn=st_rnn, st_sink=st_sink, st_pre=st_pre,
                g_rg_a=g_rg_a, g_rg_x=g_rg_x, red_out=red_out, red_in=red_in)


def _place():
    x, y, c = lax.axis_index("x"), lax.axis_index("y"), lax.axis_index("c")
    return x, y, c


def _gather_ride(shards):
    n = len(shards)

    def copies(ins, outs, sems):
        send_sems, recv_sems, local_sems = sems
        x, y, c = _place()
        me, sibling = (x, y, c), (x, y, 1 - c)
        chips = [(1 - x, y), (x, 1 - y), (1 - x, 1 - y)]

        def slot(a, dev):
            return outs[a].at[4 * dev[0] + 2 * dev[1] + dev[2]]

        def copy(a, k, block, to, src=None):
            return pltpu.make_async_remote_copy(
                src_ref=slot(a, block) if src is None else src, dst_ref=slot(a, block),
                send_sem=send_sems.at[a, k], recv_sem=recv_sems.at[a, k], device_id=to, device_id_type=MESH)

        mine = [pltpu.make_async_copy(ins[a], slot(a, me), local_sems.at[a]) for a in range(n)]
        first = []
        for a in range(n):
            first.append(copy(a, 0, me, sibling, src=ins[a]))
            first += [copy(a, 1 + j, me, (*chip, c), src=ins[a]) for j, chip in enumerate(chips)]
        return me, sibling, chips, c, copy, mine, first

    def start(ins, outs, sems):
        *_, mine, first = copies(ins, outs, sems)
        for cp in mine + first:
            cp.start()

    def finish(ins, outs, sems):
        me, sibling, chips, c, copy, mine, first = copies(ins, outs, sems)
        passed = []
        for j, chip in enumerate(chips):
            for a in range(n):
                copy(a, 1 + j, (*chip, c), me).wait_recv()
                cp = copy(a, 4 + j, (*chip, c), sibling)
                cp.start()
                passed.append(cp)
        for a in range(n):
            copy(a, 0, sibling, me).wait_recv()
            for j, chip in enumerate(chips):
                copy(a, 4 + j, (*chip, 1 - c), me).wait_recv()
        for cp in first + passed:
            cp.wait_send()
        for cp in mine:
            cp.wait()

    return _Ride(
        shards, [jax.ShapeDtypeStruct((N_DEV, *s.shape), s.dtype) for s in shards],
        [pltpu.SemaphoreType.DMA((n, 7)), pltpu.SemaphoreType.DMA((n, 7)), pltpu.SemaphoreType.DMA((n,))],
        start, finish)


def _sibling_ride(scatter, whole):
    ns, nw = len(scatter), len(whole)

    def copies(ins, outs, sems):
        send_sems, recv_sems = sems
        x, y, c = _place()
        cps = [pltpu.make_async_remote_copy(
            src_ref=ins[a].at[2 * chip + (1 - c)], dst_ref=outs[a].at[chip],
            send_sem=send_sems.at[a * N_CHIP + chip], recv_sem=recv_sems.at[a * N_CHIP + chip],
            device_id=(x, y, 1 - c), device_id_type=MESH) for a in range(ns) for chip in range(N_CHIP)]
        cps += [pltpu.make_async_remote_copy(
            src_ref=ins[ns + a], dst_ref=outs[ns + a],
            send_sem=send_sems.at[ns * N_CHIP + a], recv_sem=recv_sems.at[ns * N_CHIP + a],
            device_id=(x, y, 1 - c), device_id_type=MESH) for a in range(nw)]
        return cps

    def start(ins, outs, sems):
        for cp in copies(ins, outs, sems):
            cp.start()

    def finish(ins, outs, sems):
        for cp in copies(ins, outs, sems):
            cp.wait()

    n_sem = ns * N_CHIP + nw
    return _Ride(
        list(scatter) + list(whole),
        [jax.ShapeDtypeStruct((N_CHIP, *s.shape[1:]), s.dtype) for s in scatter]
        + [jax.ShapeDtypeStruct(s.shape, s.dtype) for s in whole],
        [pltpu.SemaphoreType.DMA((n_sem,)), pltpu.SemaphoreType.DMA((n_sem,))], start, finish)


def _chips_ride(scatter, whole):
    ns, nw = len(scatter), len(whole)
    n = ns + nw

    def copies(ins, outs, sems):
        send_sems, recv_sems, local_sems = sems
        x, y, c = _place()
        own = 2 * x + y
        chips = [(1 - x, y), (x, 1 - y), (1 - x, 1 - y)]

        def src(a, chip_idx):
            return ins[a].at[chip_idx] if a < ns else ins[a]

        local = [pltpu.make_async_copy(src(a, own), outs[a].at[own], local_sems.at[a]) for a in range(n)]
        sent = [pltpu.make_async_remote_copy(
            src_ref=src(a, 2 * chip[0] + chip[1]), dst_ref=outs[a].at[own],
            send_sem=send_sems.at[a, j], recv_sem=recv_sems.at[a, own], device_id=(*chip, c), device_id_type=MESH)
            for a in range(n) for j, chip in enumerate(chips)]
        return chips, c, local, sent

    def start(ins, outs, sems):
        _, _, local, sent = copies(ins, outs, sems)
        for cp in local + sent:
            cp.start()

    def finish(ins, outs, sems):
        send_sems, recv_sems, _ = sems
        chips, c, local, sent = copies(ins, outs, sems)
        for a in range(n):
            for chip in chips:
                k = 2 * chip[0] + chip[1]
                pltpu.make_async_remote_copy(
                    src_ref=outs[a].at[k], dst_ref=outs[a].at[k], send_sem=send_sems.at[a, 0],
                    recv_sem=recv_sems.at[a, k], device_id=(*chip, c), device_id_type=MESH).wait_recv()
        for cp in sent:
            cp.wait_send()
        for cp in local:
            cp.wait()

    return _Ride(
        list(scatter) + list(whole),
        [jax.ShapeDtypeStruct(s.shape, s.dtype) for s in scatter]
        + [jax.ShapeDtypeStruct((N_CHIP, *s.shape), s.dtype) for s in whole],
        [pltpu.SemaphoreType.DMA((n, 3)), pltpu.SemaphoreType.DMA((n, N_CHIP)), pltpu.SemaphoreType.DMA((n,))],
        start, finish)


def _pair_sum_scatter(parts, recvs, core, name):
    na = len(parts)
    _, r, cdim = parts[0].shape
    tr = min(r, 416 if r % 416 == 0 else 128)

    def body(core_ref, *refs):
        del core_ref
        for a in range(na):
            refs[2 * na + a][...] = (refs[a][...] + refs[na + a][...]).astype(BF16)

    blk = (None, tr, cdim)
    mine = pl.BlockSpec(blk, lambda k, i, core_ref: (2 * k + core_ref[0], i, 0))
    slot = pl.BlockSpec(blk, lambda k, i, core_ref: (k, i, 0))
    return pl.pallas_call(
        body, name=name,
        grid_spec=pltpu.PrefetchScalarGridSpec(
            num_scalar_prefetch=1, grid=(N_CHIP, r // tr),
            in_specs=[mine] * na + [slot] * na, out_specs=[slot] * na),
        out_shape=[jax.ShapeDtypeStruct((N_CHIP, r, cdim), BF16)] * na,
        compiler_params=_params("parallel", "parallel"))(core, *parts, *recvs)


def _pair_sum_whole(mine, recvs, name):
    na = len(mine)

    def body(*refs):
        for a in range(na):
            refs[2 * na + a][...] = refs[a][...] + refs[na + a][...]

    return pl.pallas_call(body, name=name, out_shape=[jax.ShapeDtypeStruct(m.shape, F32) for m in mine])(*mine, *recvs)


def _allreduce_small(pack, name):
    shape = pack.shape

    def body(x_ref, o_ref, sib_ref, chip_ref, send_sems, recv_sems):
        x, y, c = _place()
        own = 2 * x + y
        chips = [(1 - x, y), (x, 1 - y), (1 - x, 1 - y)]
        to_sibling = pltpu.make_async_remote_copy(
            src_ref=x_ref, dst_ref=sib_ref, send_sem=send_sems.at[0], recv_sem=recv_sems.at[0],
            device_id=(x, y, 1 - c), device_id_type=MESH)
        to_sibling.start()
        to_sibling.wait()
        chip_ref[own] = x_ref[...] + sib_ref[...]
        sent = [pltpu.make_async_remote_copy(
            src_ref=chip_ref.at[own], dst_ref=chip_ref.at[own], send_sem=send_sems.at[1 + j],
            recv_sem=recv_sems.at[1 + own], device_id=(*chip, c), device_id_type=MESH) for j, chip in enumerate(chips)]
        for cp in sent:
            cp.start()
        for chip in chips:
            k = 2 * chip[0] + chip[1]
            pltpu.make_async_remote_copy(
                src_ref=chip_ref.at[k], dst_ref=chip_ref.at[k], send_sem=send_sems.at[1],
                recv_sem=recv_sems.at[1 + k], device_id=(*chip, c), device_id_type=MESH).wait_recv()
        for cp in sent:
            cp.wait_send()
        o_ref[...] = (chip_ref[0] + chip_ref[1]) + (chip_ref[2] + chip_ref[3])

    return pl.pallas_call(
        body, name=name, out_shape=jax.ShapeDtypeStruct(shape, F32),
        in_specs=[pl.BlockSpec(memory_space=pltpu.VMEM)], out_specs=pl.BlockSpec(memory_space=pltpu.VMEM),
        scratch_shapes=[pltpu.VMEM(shape, F32), pltpu.VMEM((N_CHIP, *shape), F32),
                        pltpu.SemaphoreType.DMA((4,)), pltpu.SemaphoreType.DMA((1 + N_CHIP,))],
    )(pack)


def _adamw(g, w, m, v):
    m = ADAM_B1 * m + (1.0 - ADAM_B1) * g
    v = ADAM_B2 * v + (1.0 - ADAM_B2) * (g * g)
    m_hat = m / (1.0 - ADAM_B1 ** ADAM_STEP)
    v_hat = v / (1.0 - ADAM_B2 ** ADAM_STEP)
    delta = -ADAM_LR * (m_hat / (jnp.sqrt(v_hat) + ADAM_EPS) + ADAM_WD * w)
    return delta, m, v


def _adam_parts(parts, w, m, v, name, tr=None):
    npart, r, c = parts.shape
    tr = r if tr is None else min(tr, r)

    def body(p_ref, w_ref, m_ref, v_ref, g_ref, d_ref, nm_ref, nv_ref):
        g = p_ref[0].astype(F32)
        for k in range(1, npart):
            g = g + p_ref[k].astype(F32)
        g_ref[...] = g
        d_ref[...], nm_ref[...], nv_ref[...] = _adamw(g, w_ref[...], m_ref[...], v_ref[...])

    tile = pl.BlockSpec((tr, c), lambda i: (i, 0))
    return pl.pallas_call(
        body, name=name, grid=(r // tr,),
        in_specs=[pl.BlockSpec((npart, tr, c), lambda i: (0, i, 0)), tile, tile, tile],
        out_specs=[tile] * 4, out_shape=[jax.ShapeDtypeStruct((r, c), F32)] * 4,
        compiler_params=_params("parallel"))(parts, w, m, v)


def _block_diag(w):
    w4 = w.reshape(N_GROUPS, 4, RNN_BLOCK_W, RNN_BLOCK_W)
    eye = jnp.eye(4, dtype=w.dtype)
    return jnp.einsum("gbij,bc->gbicj", w4, eye).reshape(N_GROUPS, GROUP_W, GROUP_W).astype(BF16)


SMALL_ROWS = 16
ROW_PRE_G, ROW_BGATE, ROW_CONV_B, ROW_B_A, ROW_B_X, ROW_LAM, ROW_POST_G, ROW_LOSS, ROW_SINKS, ROW_CONV_W = 0, 1, 3, 4, 5, 6, 7, 8, 9, 10


def _pack_stats(st_pre, st_merge, st_rnn, st_post, st_sink, name):
    d = D_MODEL

    def body(pre_ref, mg_ref, rnn_ref, post_ref, sink_ref, o_ref):
        rnn = rnn_ref[...]
        sinks = jnp.concatenate([sink_ref[0:1, :], jnp.zeros((1, d - LANE), F32)], axis=1)
        o_ref[0:8, :] = _rows8([pre_ref[0:1, :], mg_ref[0:1, :], mg_ref[1:2, :], rnn[0:1], rnn[1:2], rnn[2:3], rnn[3:4],
                                post_ref[0:1, :]], d)
        o_ref[8:16, :] = _rows8([post_ref[1:2, :], sinks, rnn[4:5], rnn[5:6], rnn[6:7], rnn[7:8]], d)

    return pl.pallas_call(body, name=name, out_shape=jax.ShapeDtypeStruct((SMALL_ROWS, d), F32))(
        st_pre, st_merge, st_rnn, st_post, st_sink)


def _adam_small(total, w, m, v, name):
    d = D_MODEL
    n_in = len(w)

    def pack(refs):
        pre, bg, cbias, ba, bx, lam, post, sinks = [r[...] for r in refs]
        top = _rows8([pre, bg[:, 0:d], bg[:, d:2 * d], cbias, ba, bx, lam, post], d)
        return jnp.concatenate([top, _rows8([jnp.zeros((1, d), F32), sinks], d)], axis=0)

    def body(*refs):
        p_ref = refs[0]
        w_refs, m_refs, v_refs = (refs[1 + k * n_in:1 + (k + 1) * n_in] for k in range(3))
        outs = refs[1 + 3 * n_in:]
        g = p_ref[...]
        res = (g,) + _adamw(g, pack(w_refs), pack(m_refs), pack(v_refs))
        for k in range(4):
            outs[k][...] = res[k]
            outs[4 + k][...] = jnp.concatenate([res[k][ROW_BGATE:ROW_BGATE + 1], res[k][ROW_BGATE + 1:ROW_BGATE + 2]], axis=1)

    return pl.pallas_call(
        body, name=name,
        out_shape=[jax.ShapeDtypeStruct((SMALL_ROWS, d), F32)] * 4 + [jax.ShapeDtypeStruct((1, 2 * d), F32)] * 4,
    )(total, *w, *m, *v)


def kernel(x, pre_norm_g, w_in, b_gate, conv_w, conv_b, w_rg_a, b_rg_a, w_rg_x, b_rg_x, lru_lambda, attn_sinks, w_rnn_out, w_attn_out, w_out, post_norm_g, loss_target, m_pre_norm_g, m_w_in, m_b_gate, m_conv_w, m_conv_b, m_w_rg_a, m_b_rg_a, m_w_rg_x, m_b_rg_x, m_lru_lambda, m_attn_sinks, m_w_rnn_out, m_w_attn_out, m_w_out, m_post_norm_g, v_pre_norm_g, v_w_in, v_b_gate, v_conv_w, v_conv_b, v_w_rg_a, v_b_rg_a, v_w_rg_x, v_b_rg_x, v_lru_lambda, v_attn_sinks, v_w_rnn_out, v_w_attn_out, v_w_out, v_post_norm_g):
    cx, cy, cc = _place()
    dev = 4 * cx + 2 * cy + cc
    core = jnp.reshape(cc, (1,)).astype(jnp.int32)

    w_in_t, m_in_t, v_in_t = (jnp.transpose(a[0]) for a in (w_in, m_w_in, v_w_in))
    wt_shard = w_in_t.astype(BF16)
    (wt_all,) = _run_ride(_gather_ride([wt_shard]), "gather_w_in")
    heads = jnp.arange(1, N_Q_HEADS + 1, dtype=F32)
    slopes = jnp.exp2(-ALIBI_MAX_BIAS * heads / N_Q_HEADS)
    b_a = b_rg_a.reshape(1, D_RNN)
    b_x = b_rg_x.reshape(1, D_RNN)
    p = dict(
        wt=wt_all.reshape(D_IN, D_MODEL),
        pre_g=pre_norm_g, post_g=post_norm_g, b_gate=b_gate, cb=conv_b,
        wbd_a=_block_diag(w_rg_a[0]), b_a=b_a, wbd_x=_block_diag(w_rg_x[0]), b_x=b_x, lam=lru_lambda,
        sm=jnp.pad(attn_sinks, ((0, 1), (0, 0))) + jnp.pad(slopes[None, :], ((1, 0), (0, 0))))

    def late_unpack(landed):
        w_rnn_all, w_attn_all, w_out_all, cw_all = landed
        return dict(w_rnn=w_rnn_all.reshape(D_RNN, D_MODEL), w_attn=w_attn_all.reshape(D_MODEL, D_MODEL),
                    w_out=w_out_all.reshape(D_MODEL, D_MODEL), cw=jnp.transpose(cw_all, (1, 0, 2)).reshape(4, D_RNN))

    late_weights = (_gather_ride([w_rnn_out[0].astype(BF16), w_attn_out[0].astype(BF16), w_out[0].astype(BF16), conv_w[0]]),
                    late_unpack)
    flat = (RNN_BLOCKS * RNN_BLOCK_W, RNN_BLOCK_W)

    out_scatter = []

    def out_sibling(gw_rnn, gw_attn, gw_out):
        out_scatter.extend(gw.reshape(N_DEV, SHARD_OUT, D_MODEL) for gw in (gw_rnn, gw_attn, gw_out))
        return _sibling_ride(out_scatter, [])

    def out_chips(from_sibling, g_rg_a, g_rg_x):
        whole = [g_rg_a.reshape(flat), g_rg_x.reshape(flat)]
        rg_sibling = _run_ride(_sibling_ride([], whole), "sibling_rg")
        return _chips_ride(_pair_sum_scatter(out_scatter, from_sibling, core, "pair_out"),
                           _pair_sum_whole(whole, rg_sibling, "pair_rg"))

    def reduce_in(gwt):
        scatter = [gwt.reshape(N_DEV, SHARD_IN, D_MODEL)]
        from_sibling = _run_ride(_sibling_ride(scatter, []), "sibling_in")
        return _chips_ride(_pair_sum_scatter(scatter, from_sibling, core, "pair_in"), [])

    g = _local_grads(x[0], loss_target[0], p, late_weights, (out_sibling, out_chips), reduce_in)
    small = _allreduce_small(
        _pack_stats(g["st_pre"], g["st_merge"], g["st_rnn"], g["st_post"], g["st_sink"], "pack_stats"), "allreduce_small")

    out = {}
    red = g["red_out"]
    out["w_in"] = [jnp.transpose(o) for o in _adam_parts(g["red_in"][0], w_in_t, m_in_t, v_in_t, "adam_w_in", tr=SHARD_IN // 2)]
    out["w_rnn_out"] = _adam_parts(red[0], w_rnn_out[0], m_w_rnn_out[0], v_w_rnn_out[0], "adam_w_rnn_out")
    out["w_attn_out"] = _adam_parts(red[1], w_attn_out[0], m_w_attn_out[0], v_w_attn_out[0], "adam_w_attn_out")
    out["w_out"] = _adam_parts(red[2], w_out[0], m_w_out[0], v_w_out[0], "adam_w_out")
    out["w_rg_a"] = _adam_parts(red[3], w_rg_a.reshape(flat), m_w_rg_a.reshape(flat), v_w_rg_a.reshape(flat), "adam_w_rg_a", tr=256)
    out["w_rg_x"] = _adam_parts(red[4], w_rg_x.reshape(flat), m_w_rg_x.reshape(flat), v_w_rg_x.reshape(flat), "adam_w_rg_x", tr=256)

    def rows(pre, bg, cbias, ba, bx, lam, post, sinks):
        return (pre, bg, cbias, ba.reshape(1, D_RNN), bx.reshape(1, D_RNN), lam, post,
                jnp.pad(sinks, ((0, 0), (0, D_MODEL - N_Q_HEADS))))

    small_out = _adam_small(
        small,
        rows(pre_norm_g, b_gate, conv_b, b_rg_a, b_rg_x, lru_lambda, post_norm_g, attn_sinks),
        rows(m_pre_norm_g, m_b_gate, m_conv_b, m_b_rg_a, m_b_rg_x, m_lru_lambda, m_post_norm_g, m_attn_sinks),
        rows(v_pre_norm_g, v_b_gate, v_conv_b, v_b_rg_a, v_b_rg_x, v_lru_lambda, v_post_norm_g, v_attn_sinks),
        "adam_small")
    packed, bgate_out = small_out[:4], small_out[4:]
    g_cw = lax.dynamic_slice(packed[0][ROW_CONV_W:ROW_CONV_W + 4], (0, dev * SHARD_OUT), (4, SHARD_OUT))
    out["conv_w"] = _adam_parts(g_cw[None], conv_w[0], m_conv_w[0], v_conv_w[0], "adam_conv_w")

    def unpack(kind, name):
        if name == "b_gate":
            return bgate_out[kind]
        row = dict(pre_norm_g=ROW_PRE_G, conv_b=ROW_CONV_B, b_rg_a=ROW_B_A, b_rg_x=ROW_B_X, lru_lambda=ROW_LAM,
                   post_norm_g=ROW_POST_G, attn_sinks=ROW_SINKS)[name]
        r = packed[kind][row:row + 1]
        if name == "attn_sinks":
            return r[:, 0:N_Q_HEADS]
        if name in ("b_rg_a", "b_rg_x"):
            return r.reshape(1, RNN_BLOCKS, RNN_BLOCK_W)
        return r

    shapes = dict(w_in=(1, D_MODEL, SHARD_IN), w_rnn_out=(1, SHARD_OUT, D_MODEL), w_attn_out=(1, SHARD_OUT, D_MODEL),
                  w_out=(1, SHARD_OUT, D_MODEL), w_rg_a=(1, RNN_BLOCKS, RNN_BLOCK_W, RNN_BLOCK_W),
                  w_rg_x=(1, RNN_BLOCKS, RNN_BLOCK_W, RNN_BLOCK_W), conv_w=(1, 4, SHARD_OUT))
    weights = ["pre_norm_g", "w_in", "b_gate", "conv_w", "conv_b", "w_rg_a", "b_rg_a", "w_rg_x", "b_rg_x",
               "lru_lambda", "attn_sinks", "w_rnn_out", "w_attn_out", "w_out", "post_norm_g"]
    results = []
    for kind in range(4):
        for name in weights:
            if name in out:
                results.append(out[name][kind].reshape(shapes[name]))
            else:
                results.append(unpack(kind, name))
    loss = 0.5 / D_MODEL * jnp.sum(packed[0][ROW_LOSS])
    return (loss, g["grad_x"][None], *results)
```

```python
import functools

import jax
import jax.numpy as jnp
from jax import lax
from jax.experimental import pallas as pl
from jax.experimental.pallas import tpu as pltpu

F32 = jnp.float32
BF16 = jnp.bfloat16

D_MODEL = 1024
D_RNN = 1024
RNN_BLOCKS = 16
RNN_BLOCK_W = 64
LRU_C = 8.0
N_Q_HEADS = 16
HEAD_DIM = 64
D_KV = 256
BLOCK = 128
ALIBI_MAX_BIAS = 8.0
EPS = 1e-6
D_IN = 6656
N_DEV = 8
N_CHIP = 4
SHARD_IN = D_IN // N_DEV
SHARD_OUT = D_MODEL // N_DEV
ATTN_SCALE = HEAD_DIM ** -0.5
MASKED = -1e30

ADAM_LR = 0.001
ADAM_B1 = 0.9
ADAM_B2 = 0.999
ADAM_EPS = 1e-08
ADAM_WD = 0.01
ADAM_STEP = 10

VMEM_LIMIT_BYTES = 52 * 1024 * 1024
LANE = 128
GROUP_W = 256
N_GROUPS = D_RNN // GROUP_W
SEG_CHUNK = 512

NT_DIMS = (((1,), (1,)), ((), ()))
TN_DIMS = (((0,), (0,)), ((), ()))
MESH = pl.DeviceIdType.MESH
ANY = pl.BlockSpec(memory_space=pl.ANY)


def _params(*semantics):
    return pltpu.CompilerParams(dimension_semantics=semantics, vmem_limit_bytes=VMEM_LIMIT_BYTES)


def _sigmoid(x):
    return 0.5 * jnp.tanh(0.5 * x) + 0.5


def _log1p(e):
    u = 1.0 + e
    den = jnp.where(u == 1.0, 1.0, u - 1.0)
    return jnp.where(u == 1.0, e, jnp.log(u) * (e / den))


def _softplus(z):
    return jnp.maximum(z, 0.0) + _log1p(jnp.exp(-jnp.abs(z)))


def _rows8(rows, width):
    idx = lax.broadcasted_iota(jnp.int32, (8, width), 0)
    out = jnp.zeros((8, width), F32)
    for r, v in enumerate(rows):
        out = jnp.where(idx == r, v, out)
    return out


class _Ride:
    def __init__(self, arrays, out_shapes, scratch_shapes, start, finish):
        self.arrays, self.out_shapes, self.scratch_shapes = list(arrays), list(out_shapes), list(scratch_shapes)
        self.start, self.finish = start, finish


class _Hosted:
    def __init__(self, ride, n_in, n_out, n_scratch=0):
        self.ride = ride
        self.sizes = (n_in, len(ride.arrays) if ride else 0, n_out, len(ride.out_shapes) if ride else 0, n_scratch)
        self.arrays = ride.arrays if ride else []
        self.in_specs = [ANY] * len(self.arrays)
        self.out_shapes = ride.out_shapes if ride else []
        self.out_specs = [ANY] * len(self.out_shapes)
        self.scratch_shapes = ride.scratch_shapes if ride else []

    def split(self, refs):
        n_in, r_in, n_out, r_out, n_scr = self.sizes
        cuts = [0, n_in, n_in + r_in, n_in + r_in + n_out, n_in + r_in + n_out + r_out, n_in + r_in + n_out + r_out + n_scr]
        host_in, ride_in, host_out, ride_out, host_scr = (refs[cuts[k]:cuts[k + 1]] for k in range(5))
        ride_scr = refs[cuts[5]:]

        def start(when):
            if self.ride is not None:
                pl.when(when)(lambda: self.ride.start(ride_in, ride_out, ride_scr))

        def finish(when):
            if self.ride is not None:
                pl.when(when)(lambda: self.ride.finish(ride_in, ride_out, ride_scr))

        return tuple(host_in) + tuple(host_out) + tuple(host_scr), start, finish

    def results(self, outs, n_out):
        outs = list(outs) if isinstance(outs, (list, tuple)) else [outs]
        return outs[:n_out], outs[n_out:]


def _run_ride(ride, name):
    n_in, n_out = len(ride.arrays), len(ride.out_shapes)

    def body(*refs):
        ins, outs, sems = refs[:n_in], refs[n_in:n_in + n_out], refs[n_in + n_out:]
        ride.start(ins, outs, sems)
        ride.finish(ins, outs, sems)

    return pl.pallas_call(
        body, name=name, in_specs=[ANY] * n_in, out_specs=[ANY] * n_out,
        out_shape=ride.out_shapes, scratch_shapes=ride.scratch_shapes)(*ride.arrays)


def _load_resident(w_hbm, w_vmem, sems, first):
    def piece(c):
        rows = pl.ds(c * SEG_CHUNK, SEG_CHUNK)
        return pltpu.make_async_copy(w_hbm.at[rows], w_vmem.at[rows], sems.at[c])

    @pl.when(first)
    def _():
        for c in range(w_vmem.shape[0] // SEG_CHUNK):
            piece(c).start()

    def ready(c):
        @pl.when(first)
        def _():
            piece(c).wait()

    return ready


CHIP_ROWS = 2 * SHARD_IN
PROJ_WIDTHS = (2 * D_RNN, D_MODEL + 2 * D_KV, 3 * D_MODEL)
PROJ_DTYPES = (F32, BF16, F32)


def _chip_pieces():
    starts = [0, PROJ_WIDTHS[0], PROJ_WIDTHS[0] + PROJ_WIDTHS[1], D_IN]
    pieces = []
    for k in range(N_CHIP):
        lo, hi = k * CHIP_ROWS, (k + 1) * CHIP_ROWS
        cur = []
        for a in range(len(PROJ_WIDTHS)):
            s0, s1 = max(lo, starts[a]), min(hi, starts[a + 1])
            if s0 < s1:
                cur.append((a, s0 - starts[a], s1 - s0, s0 - lo))
        pieces.append(cur)
    return pieces


def _gather_project(x, g, wt_shard, name, tm=512):
    t, k = x.shape
    tm = min(tm, t)
    nt = t // tm
    pieces = _chip_pieces()
    w16 = max(w for cur in pieces for a, _, w, _ in cur if PROJ_DTYPES[a] == BF16)

    def body(x_ref, g_ref, shard_ref, h_out, rx_ref, qkv_ref, ag_ref, wt_all,
             w_c, stage, o32, o16, h16, send_sems, recv_sems, local_sem, stage_sems, out_sems, h_sems):
        s, ti = pl.program_id(0), pl.program_id(1)
        px, py, pc = _place()
        me, sibling = (px, py, pc), (px, py, 1 - pc)
        chips = [(px, py), (1 - px, py), (px, 1 - py), (1 - px, 1 - py)]
        outs = (rx_ref, qkv_ref, ag_ref)

        def slot(dev):
            return wt_all.at[4 * dev[0] + 2 * dev[1] + dev[2]]

        def copy(kk, block, to, src=None):
            return pltpu.make_async_remote_copy(
                src_ref=slot(block) if src is None else src, dst_ref=slot(block),
                send_sem=send_sems.at[kk], recv_sem=recv_sems.at[kk], device_id=to, device_id_type=MESH)

        mine = pltpu.make_async_copy(shard_ref, slot(me), local_sem)
        first = [copy(0, me, sibling, src=shard_ref)] + [copy(1 + j, me, (*chips[1 + j], pc), src=shard_ref) for j in range(3)]
        passed = [copy(4 + j, (*chips[1 + j], pc), sibling) for j in range(3)]

        @pl.when((s == 0) & (ti == 0))
        def _():
            mine.start()
            for cp in first:
                cp.start()

        for step in range(N_CHIP):
            @pl.when((s == step) & (ti == 0))
            def _(step=step):
                chip = chips[step]
                if step == 0:
                    mine.wait()
                    copy(0, sibling, me).wait_recv()
                else:
                    copy(step, (*chip, pc), me).wait_recv()
                    passed[step - 1].start()
                    copy(3 + step, (*chip, 1 - pc), me).wait_recv()
                loads = [pltpu.make_async_copy(slot((*chip, core)), stage.at[pl.ds(core * SHARD_IN, SHARD_IN)], stage_sems.at[core])
                         for core in (0, 1)]
                for cp in loads:
                    cp.start()
                for cp in loads:
                    cp.wait()
                w_c[...] = stage[...].T

        xv = x_ref[...]
        hv = (xv * lax.rsqrt(jnp.mean(xv * xv, axis=-1, keepdims=True) + EPS) * g_ref[...]).astype(BF16)
        res = jnp.dot(hv, w_c[...], preferred_element_type=F32)

        n = s * nt + ti
        buf = lax.rem(n, 2)
        chip_idx = [2 * cx + cy for cx, cy in chips]

        def chip_at(step):
            return jnp.where(step == 0, chip_idx[0], jnp.where(step == 1, chip_idx[1], jnp.where(step == 2, chip_idx[2], chip_idx[3])))

        def h_write(b, tile):
            return pltpu.make_async_copy(h16.at[b], h_out.at[pl.ds(tile * tm, tm)], h_sems.at[b])

        def writes(kchip, b, tile):
            cps = []
            for idx, (a, col, w, src) in enumerate(pieces[kchip]):
                staged = o16.at[b, :, pl.ds(0, w)] if PROJ_DTYPES[a] == BF16 else o32.at[b, :, pl.ds(src, w)]
                cps.append(pltpu.make_async_copy(staged, outs[a].at[pl.ds(tile * tm, tm), pl.ds(col, w)], out_sems.at[b, idx]))
            return cps

        o32[buf] = res
        kcur = chip_at(s)

        @pl.when(s == 0)
        def _():
            h16[buf] = hv
        for kchip in range(N_CHIP):
            for a, _, w, src in pieces[kchip]:
                if PROJ_DTYPES[a] == BF16:
                    @pl.when(kcur == kchip)
                    def _(w=w, src=src):
                        o16[buf, :, 0:w] = res[:, src:src + w].astype(BF16)

        @pl.when(n > 0)
        def _():
            kprev = chip_at(lax.div(n - 1, nt))
            for kchip in range(N_CHIP):
                @pl.when(kprev == kchip)
                def _(kchip=kchip):
                    for cp in writes(kchip, 1 - buf, lax.rem(n - 1, nt)):
                        cp.wait()

            @pl.when(n <= nt)
            def _():
                h_write(1 - buf, n - 1).wait()

        @pl.when(s == 0)
        def _():
            h_write(buf, ti).start()

        for kchip in range(N_CHIP):
            @pl.when(kcur == kchip)
            def _(kchip=kchip):
                for cp in writes(kchip, buf, ti):
                    cp.start()

        @pl.when(n == N_CHIP * nt - 1)
        def _():
            for kchip in range(N_CHIP):
                @pl.when(kcur == kchip)
                def _(kchip=kchip):
                    for cp in writes(kchip, buf, ti):
                        cp.wait()
            for cp in first + passed:
                cp.wait_send()

    tile = pl.BlockSpec((tm, k), lambda s, ti: (ti, 0))
    return pl.pallas_call(
        body, name=name, grid=(N_CHIP, nt),
        in_specs=[tile, pl.BlockSpec((1, k), lambda s, ti: (0, 0)), ANY],
        out_specs=[ANY, ANY, ANY, ANY, ANY],
        out_shape=[jax.ShapeDtypeStruct((t, k), BF16)]
        + [jax.ShapeDtypeStruct((t, w), dt) for w, dt in zip(PROJ_WIDTHS, PROJ_DTYPES)]
        + [jax.ShapeDtypeStruct((N_DEV, SHARD_IN, k), BF16)],
        scratch_shapes=[pltpu.VMEM((k, CHIP_ROWS), BF16), pltpu.VMEM((CHIP_ROWS, k), BF16),
                        pltpu.VMEM((2, tm, CHIP_ROWS), F32), pltpu.VMEM((2, tm, w16), BF16), pltpu.VMEM((2, tm, k), BF16),
                        pltpu.SemaphoreType.DMA((7,)), pltpu.SemaphoreType.DMA((7,)), pltpu.SemaphoreType.DMA,
                        pltpu.SemaphoreType.DMA((2,)), pltpu.SemaphoreType.DMA((2, 2)), pltpu.SemaphoreType.DMA((2,))],
        compiler_params=_params("arbitrary", "arbitrary"))(x, g, wt_shard)


def _mm_tn(a, b, name, tm=512, tk=4096):
    ktok, m = a.shape
    n = b.shape[1]
    tk = min(tk, ktok)

    def body(a_ref, b_ref, o_ref):
        @pl.when(pl.program_id(1) == 0)
        def _():
            o_ref[...] = jnp.zeros_like(o_ref)

        o_ref[...] += lax.dot_general(a_ref[...], b_ref[...], TN_DIMS, preferred_element_type=F32)

    return pl.pallas_call(
        body, name=name, grid=(m // tm, ktok // tk),
        in_specs=[pl.BlockSpec((tk, tm), lambda i, kk: (kk, i)), pl.BlockSpec((tk, n), lambda i, kk: (kk, 0))],
        out_specs=pl.BlockSpec((tm, n), lambda i, kk: (i, 0)),
        out_shape=jax.ShapeDtypeStruct((m, n), F32),
        compiler_params=_params("parallel", "arbitrary"))(a, b)


def _segment_chunks(segs):
    bounds = [0]
    for s in segs:
        bounds.append(bounds[-1] + s.shape[1] // SEG_CHUNK)
    return bounds


def _input_grad(segs, wt, x, g, dy, name, tm=512, ride=None):
    m = segs[0].shape[0]
    rows, n = wt.shape
    tm = min(tm, m)
    bounds = _segment_chunks(segs)
    n_seg = len(segs)
    ni = m // tm
    host = _Hosted(ride, n_seg + 4, 2, 2)

    def body(*refs):
        host_refs, start, finish = host.split(refs)
        a_refs = host_refs[:n_seg]
        wt_hbm, x_ref, g_ref, dy_ref, gx_ref, st_ref, wt_vmem, sems = host_refs[n_seg:]
        i = pl.program_id(0)
        start(i == 0)

        @pl.when(i == 0)
        def _():
            st_ref[...] = jnp.zeros_like(st_ref)

        ready = _load_resident(wt_hbm, wt_vmem, sems, i == 0)
        dh = None
        for s in range(n_seg):
            for c in range(bounds[s], bounds[s + 1]):
                ready(c)
            part = jnp.dot(a_refs[s][...], wt_vmem[bounds[s] * SEG_CHUNK:bounds[s + 1] * SEG_CHUNK, :], preferred_element_type=F32)
            dh = part if dh is None else dh + part
        xv = x_ref[...]
        r = lax.rsqrt(jnp.mean(xv * xv, axis=-1, keepdims=True) + EPS)
        xn = xv * r
        dxn = dh * g_ref[...]
        gx_ref[...] = dy_ref[...] + r * (dxn - xn * jnp.mean(dxn * xn, axis=-1, keepdims=True))
        st_ref[...] += _rows8([jnp.sum(dh * xn, axis=0, keepdims=True)], n)
        finish(i == ni - 1)

    tile = pl.BlockSpec((tm, n), lambda i: (i, 0))
    outs = pl.pallas_call(
        body, name=name, grid=(ni,),
        in_specs=[pl.BlockSpec((tm, sg.shape[1]), lambda i: (i, 0)) for sg in segs]
        + [ANY, tile, pl.BlockSpec((1, n), lambda i: (0, 0)), tile] + host.in_specs,
        out_specs=[tile, pl.BlockSpec((8, n), lambda i: (0, 0))] + host.out_specs,
        out_shape=[jax.ShapeDtypeStruct((m, n), F32), jax.ShapeDtypeStruct((8, n), F32)] + host.out_shapes,
        scratch_shapes=[pltpu.VMEM((rows, n), wt.dtype), pltpu.SemaphoreType.DMA((rows // SEG_CHUNK,))] + host.scratch_shapes,
        compiler_params=_params("arbitrary"))(*segs, wt, x, g, dy, *host.arrays)
    res, landed = host.results(outs, 2)
    return (*res, landed) if ride else tuple(res)


def _mm_tn_seg(segs, b, name):
    ktok = segs[0].shape[0]
    n = b.shape[1]
    bounds = _segment_chunks(segs)
    n_seg = len(segs)
    nc = bounds[-1]
    seg_of = [s for s in range(n_seg) for _ in range(bounds[s], bounds[s + 1])]

    def body(*refs):
        a_hbm, b_hbm, o_ref = refs[:n_seg], refs[n_seg], refs[n_seg + 1]
        a_buf, b_vmem, a_sems, b_sem = refs[n_seg + 2:]
        c = pl.program_id(0)

        def fetch(cc):
            s = seg_of[cc]
            cols = pl.ds((cc - bounds[s]) * SEG_CHUNK, SEG_CHUNK)
            return pltpu.make_async_copy(a_hbm[s].at[:, cols], a_buf.at[cc % 2], a_sems.at[cc % 2])

        @pl.when(c == 0)
        def _():
            whole = pltpu.make_async_copy(b_hbm, b_vmem, b_sem)
            whole.start()
            fetch(0).start()
            whole.wait()

        for cc in range(nc):
            @pl.when(c == cc)
            def _(cc=cc):
                if cc + 1 < nc:
                    fetch(cc + 1).start()
                fetch(cc).wait()

        o_ref[...] = lax.dot_general(a_buf[c % 2], b_vmem[...], TN_DIMS, preferred_element_type=F32)

    return pl.pallas_call(
        body, name=name, grid=(nc,),
        in_specs=[ANY] * (n_seg + 1), out_specs=pl.BlockSpec((SEG_CHUNK, n), lambda c: (c, 0)),
        out_shape=jax.ShapeDtypeStruct((nc * SEG_CHUNK, n), F32),
        scratch_shapes=[pltpu.VMEM((2, ktok, SEG_CHUNK), segs[0].dtype), pltpu.VMEM((ktok, n), b.dtype),
                        pltpu.SemaphoreType.DMA((2,)), pltpu.SemaphoreType.DMA],
        compiler_params=_params("arbitrary"))(*segs, b)


def _branches_fwd(z_rnn, z_attn, ag_ml, b_gate, w_rnn, w_attn, w_out, x, target, g_post, name, tm=512):
    t, d = x.shape
    tm = min(tm, t)

    def body(zr_ref, za_ref, lr_ref, la_ref, br_ref, ba_ref, wr_ref, wa_ref, wo_ref, x_ref, t_ref, g_ref,
             brr_ref, bra_ref, mg_ref, do_ref, dy_ref, st_ref):
        @pl.when(pl.program_id(0) == 0)
        def _():
            st_ref[...] = jnp.zeros_like(st_ref)

        br_rnn = jnp.dot(zr_ref[...], wr_ref[...], preferred_element_type=F32)
        br_attn = jnp.dot(za_ref[...], wa_ref[...], preferred_element_type=F32)
        brr_ref[...] = br_rnn.astype(BF16)
        bra_ref[...] = br_attn.astype(BF16)
        g_rnn = _sigmoid(lr_ref[...] + br_ref[...])
        g_attn = _sigmoid(la_ref[...] + ba_ref[...])
        merged = (g_rnn * br_rnn + g_attn * br_attn).astype(BF16)
        mg_ref[...] = merged
        o = jnp.dot(merged, wo_ref[...], preferred_element_type=F32)
        g = g_ref[...]
        r = lax.rsqrt(jnp.mean(o * o, axis=-1, keepdims=True) + EPS)
        nrm = o * r
        err = x_ref[...] + nrm * g - t_ref[...]
        dy = err * (1.0 / d)
        dy_ref[...] = dy
        dn = dy * g
        do_ref[...] = (r * (dn - nrm * jnp.mean(dn * nrm, axis=-1, keepdims=True))).astype(BF16)
        st_ref[...] += _rows8([jnp.sum(dy * nrm, axis=0, keepdims=True), jnp.sum(err * err, axis=0, keepdims=True)], d)

    tile = pl.BlockSpec((tm, d), lambda i: (i, 0))
    weight = pl.BlockSpec((d, d), lambda i: (0, 0))
    bf = jax.ShapeDtypeStruct((t, d), BF16)
    return pl.pallas_call(
        body, name=name, grid=(t // tm,),
        in_specs=[tile, tile, pl.BlockSpec((tm, d), lambda i: (i, 1)), pl.BlockSpec((tm, d), lambda i: (i, 2)),
                  pl.BlockSpec((1, d), lambda i: (0, 0)), pl.BlockSpec((1, d), lambda i: (0, 1)),
                  weight, weight, weight, tile, tile, pl.BlockSpec((1, d), lambda i: (0, 0))],
        out_specs=[tile, tile, tile, tile, tile, pl.BlockSpec((8, d), lambda i: (0, 0))],
        out_shape=[bf, bf, bf, bf, jax.ShapeDtypeStruct((t, d), F32), jax.ShapeDtypeStruct((8, d), F32)],
        compiler_params=_params("arbitrary"))(z_rnn, z_attn, ag_ml, ag_ml, b_gate, b_gate, w_rnn, w_attn, w_out, x, target, g_post)


def _branches_bwd(dout, br_rnn, br_attn, ag_ml, b_gate, w_rnn, w_attn, w_out, name, tm=512):
    t, d = br_rnn.shape
    tm = min(tm, t)

    def body(do_ref, r_ref, a_ref, lr_ref, la_ref, br_ref, ba_ref, wr_ref, wa_ref, wo_ref,
             dr_ref, da_ref, dl_ref, dzr_ref, dza_ref, st_ref, wt_ref):
        @pl.when(pl.program_id(0) == 0)
        def _():
            st_ref[...] = jnp.zeros_like(st_ref)
            wt_ref[0] = wo_ref[...].T
            wt_ref[1] = wr_ref[...].T
            wt_ref[2] = wa_ref[...].T

        dm = jnp.dot(do_ref[...], wt_ref[0], preferred_element_type=F32)
        g_rnn = _sigmoid(lr_ref[...] + br_ref[...])
        g_attn = _sigmoid(la_ref[...] + ba_ref[...])
        dbr_rnn = (dm * g_rnn).astype(BF16)
        dbr_attn = (dm * g_attn).astype(BF16)
        dr_ref[...] = dbr_rnn
        da_ref[...] = dbr_attn
        dl_rnn = dm * r_ref[...].astype(F32) * g_rnn * (1.0 - g_rnn)
        dl_attn = dm * a_ref[...].astype(F32) * g_attn * (1.0 - g_attn)
        dl_ref[:, 0:d] = dl_rnn.astype(BF16)
        dl_ref[:, d:2 * d] = dl_attn.astype(BF16)
        st_ref[...] += _rows8([jnp.sum(dl_rnn, axis=0, keepdims=True), jnp.sum(dl_attn, axis=0, keepdims=True)], d)
        dzr_ref[...] = jnp.dot(dbr_rnn, wt_ref[1], preferred_element_type=F32)
        dza_ref[...] = jnp.dot(dbr_attn, wt_ref[2], preferred_element_type=F32)

    tile = pl.BlockSpec((tm, d), lambda i: (i, 0))
    weight = pl.BlockSpec((d, d), lambda i: (0, 0))
    bf = jax.ShapeDtypeStruct((t, d), BF16)
    return pl.pallas_call(
        body, name=name, grid=(t // tm,),
        in_specs=[tile, tile, tile, pl.BlockSpec((tm, d), lambda i: (i, 1)), pl.BlockSpec((tm, d), lambda i: (i, 2)),
                  pl.BlockSpec((1, d), lambda i: (0, 0)), pl.BlockSpec((1, d), lambda i: (0, 1)), weight, weight, weight],
        out_specs=[tile, tile, pl.BlockSpec((tm, 2 * d), lambda i: (i, 0)), tile, tile, pl.BlockSpec((8, d), lambda i: (0, 0))],
        out_shape=[bf, bf, jax.ShapeDtypeStruct((t, 2 * d), BF16), jax.ShapeDtypeStruct((t, d), F32),
                   jax.ShapeDtypeStruct((t, d), F32), jax.ShapeDtypeStruct((8, d), F32)],
        scratch_shapes=[pltpu.VMEM((3, d, d), BF16)],
        compiler_params=_params("arbitrary"))(dout, br_rnn, br_attn, ag_ml, ag_ml, b_gate, b_gate, w_rnn, w_attn, w_out)


def _lru_gates(c, wa, ba, wx, bx, sp):
    cb = c.astype(BF16)
    r = _sigmoid(jnp.dot(cb, wa, preferred_element_type=F32) + ba)
    ig = _sigmoid(jnp.dot(cb, wx, preferred_element_type=F32) + bx)
    log_a = (-LRU_C) * r * sp
    a = jnp.exp(log_a)
    mult = jnp.sqrt(-jnp.tanh(log_a) * (a * a + 1.0))
    return cb, r, ig, a, mult


SUBLANES = 8


def _scan_fwd(a, u, carry, tt):
    w = a.shape[1]
    ng = tt // SUBLANES
    a3 = a.reshape(ng, SUBLANES, w)
    u3 = u.reshape(ng, SUBLANES, w)
    sub = lax.broadcasted_iota(jnp.int32, (ng, SUBLANES, w), 1)
    d = 1
    while d < SUBLANES:
        keep = sub >= d
        u3 = u3 + a3 * jnp.where(keep, pltpu.roll(u3, d, 1), 0.0)
        a3 = a3 * jnp.where(keep, pltpu.roll(a3, d, 1), 1.0)
        d *= 2
    out = []
    for g in range(ng):
        hg = u3[g] + a3[g] * carry
        out.append(hg)
        carry = hg[SUBLANES - 1:SUBLANES, :]
    return jnp.concatenate(out, axis=0)


def _scan_rev(b, g, carry, tt):
    w = b.shape[1]
    ng = tt // SUBLANES
    b3 = b.reshape(ng, SUBLANES, w)
    g3 = g.reshape(ng, SUBLANES, w)
    sub = lax.broadcasted_iota(jnp.int32, (ng, SUBLANES, w), 1)
    d = 1
    while d < SUBLANES:
        keep = sub < SUBLANES - d
        g3 = g3 + b3 * jnp.where(keep, pltpu.roll(g3, SUBLANES - d, 1), 0.0)
        b3 = b3 * jnp.where(keep, pltpu.roll(b3, SUBLANES - d, 1), 1.0)
        d *= 2
    out = [None] * ng
    for k in range(ng - 1, -1, -1):
        hk = g3[k] + b3[k] * carry
        out[k] = hk
        carry = hk[0:1, :]
    return jnp.concatenate(out, axis=0)


def _conv_taps(cw, bias, x, ext_ref, tt):
    x2 = ext_ref[7:7 + tt, :]
    x1 = ext_ref[6:6 + tt, :]
    x0 = ext_ref[5:5 + tt, :]
    c = bias + cw[3:4] * x + cw[2:3] * x2 + cw[1:2] * x1 + cw[0:1] * x0
    return c, x2, x1, x0


def _rnn_fwd(rx_rg, cw, cb, wa, ba, wx, bx, lam, name, tt=512):
    t = rx_rg.shape[0]
    tt = min(tt, t)
    w = GROUP_W

    def body(rx_ref, rg_ref, cw_ref, cb_ref, wa_ref, ba_ref, wx_ref, bx_ref, lam_ref, y_ref, z_ref, ext_ref, hc_ref):
        @pl.when(pl.program_id(1) == 0)
        def _():
            ext_ref[0:8, :] = jnp.zeros((8, w), F32)
            hc_ref[...] = jnp.zeros((8, w), F32)

        x = rx_ref[...]
        ext_ref[8:8 + tt, :] = x
        c, _, _, _ = _conv_taps(cw_ref[...], cb_ref[...], x, ext_ref, tt)
        ext_ref[0:8, :] = x[tt - 8:tt, :]
        sp = _softplus(-lam_ref[...])
        _, _, ig, a, mult = _lru_gates(c, wa_ref[...], ba_ref[...], wx_ref[...], bx_ref[...], sp)
        h = _scan_fwd(a, mult * (ig * c), hc_ref[7:8, :], tt)
        hc_ref[...] = h[tt - 8:tt, :]
        y_ref[...] = h
        rg = rg_ref[...]
        z_ref[...] = (h * rg * _sigmoid(rg)).astype(BF16)

    vec = pl.BlockSpec((1, w), lambda g, i: (0, g))
    mat = pl.BlockSpec((None, w, w), lambda g, i: (g, 0, 0))
    tile = pl.BlockSpec((tt, w), lambda g, i: (i, g))
    return pl.pallas_call(
        body, name=name, grid=(N_GROUPS, t // tt),
        in_specs=[tile, pl.BlockSpec((tt, w), lambda g, i: (i, N_GROUPS + g)),
                  pl.BlockSpec((4, w), lambda g, i: (0, g)), vec, mat, vec, mat, vec, vec],
        out_specs=[tile, tile],
        out_shape=[jax.ShapeDtypeStruct((t, D_RNN), F32), jax.ShapeDtypeStruct((t, D_RNN), BF16)],
        scratch_shapes=[pltpu.VMEM((tt + 8, w), F32), pltpu.VMEM((8, w), F32)],
        compiler_params=_params("parallel", "arbitrary"))(rx_rg, rx_rg, cw, cb, wa, ba, wx, bx, lam)


def _rnn_bwd(rx_rg, y, dz, cw, cb, wa, ba, wx, bx, lam, name, tt=512, ride=None):
    t = rx_rg.shape[0]
    tt = min(tt, t)
    nt = t // tt
    w = GROUP_W

    host = _Hosted(ride, 13, 5, 6)

    def body(*refs):
        host_refs, start, finish = host.split(refs)
        (rx_ref, rg_ref, rxt_ref, y_ref, yt_ref, dz_ref, cw_ref, cb_ref, wa_ref, ba_ref, wx_ref, bx_ref, lam_ref,
         drx_ref, drg_ref, st_ref, gda_ref, gdx_ref, ext_ref, dcx_ref, wcar_ref, acar_ref, dwa_ref, dwx_ref) = host_refs
        ii = pl.program_id(1)
        start((pl.program_id(0) == 0) & (ii == 0))

        @pl.when(ii == 0)
        def _():
            wcar_ref[...] = jnp.zeros((8, w), F32)
            acar_ref[...] = jnp.zeros((8, w), F32)
            dcx_ref[tt:tt + 8, :] = jnp.zeros((8, w), F32)
            st_ref[...] = jnp.zeros_like(st_ref)
            dwa_ref[...] = jnp.zeros_like(dwa_ref)
            dwx_ref[...] = jnp.zeros_like(dwx_ref)

        has_prev = jnp.where(ii == nt - 1, 0.0, 1.0)
        x = rx_ref[...]
        ext_ref[0:8, :] = rxt_ref[...] * has_prev
        ext_ref[8:8 + tt, :] = x
        cwv = cw_ref[...]
        c, x2, x1, x0 = _conv_taps(cwv, cb_ref[...], x, ext_ref, tt)
        lam = lam_ref[...]
        sp = _softplus(-lam)
        wa = wa_ref[...]
        wx = wx_ref[...]
        cb16, r, ig, a, mult = _lru_gates(c, wa, ba_ref[...], wx, bx_ref[...], sp)

        rg = rg_ref[...]
        sg = _sigmoid(rg)
        dz = dz_ref[...]
        yv = y_ref[...]
        drg_ref[...] = (dz * yv * (sg * (1.0 + rg * (1.0 - sg)))).astype(BF16)

        row = lax.broadcasted_iota(jnp.int32, (tt, w), 0)
        b = jnp.where(row < tt - 1, pltpu.roll(a, tt - 1, 0), acar_ref[0:1, :])
        dh = _scan_rev(b, dz * (rg * sg), wcar_ref[0:1, :], tt)
        wcar_ref[...] = dh[0:8, :]
        acar_ref[...] = a[0:8, :]

        hprev = jnp.where(row >= 1, pltpu.roll(yv, 1, 0), yt_ref[7:8, :] * has_prev)
        dmult = dh * (ig * c)
        dig = dh * mult * c
        dlog_a = dh * hprev * a - dmult * (a * a / mult)
        dpa = dlog_a * ((-LRU_C) * sp) * r * (1.0 - r)
        dpx = dig * ig * (1.0 - ig)
        dsp = jnp.sum(dlog_a * r, axis=0, keepdims=True) * (-LRU_C)
        dlam = dsp * (-_sigmoid(-lam))
        dpa16 = dpa.astype(BF16)
        dpx16 = dpx.astype(BF16)
        dwa_ref[...] += lax.dot_general(cb16, dpa16, TN_DIMS, preferred_element_type=F32)
        dwx_ref[...] += lax.dot_general(cb16, dpx16, TN_DIMS, preferred_element_type=F32)
        dc = (dh * mult * ig
              + lax.dot_general(dpa16, wa, NT_DIMS, preferred_element_type=F32)
              + lax.dot_general(dpx16, wx, NT_DIMS, preferred_element_type=F32))

        dcx_ref[0:tt, :] = dc
        drx = (cwv[3:4] * dc + cwv[2:3] * dcx_ref[1:1 + tt, :] + cwv[1:2] * dcx_ref[2:2 + tt, :]
               + cwv[0:1] * dcx_ref[3:3 + tt, :])
        drx_ref[...] = drx.astype(BF16)
        dcx_ref[tt:tt + 8, :] = dc[0:8, :]

        def colsum(v):
            return jnp.sum(v, axis=0, keepdims=True)

        st_ref[...] += _rows8([colsum(dc), colsum(dpa), colsum(dpx), dlam,
                               colsum(dc * x0), colsum(dc * x1), colsum(dc * x2), colsum(dc * x)], w)

        @pl.when(ii == nt - 1)
        def _():
            for blk in range(GROUP_W // RNN_BLOCK_W):
                rows = slice(blk * RNN_BLOCK_W, (blk + 1) * RNN_BLOCK_W)
                gda_ref[blk] = dwa_ref[rows, rows]
                gdx_ref[blk] = dwx_ref[rows, rows]

        finish((pl.program_id(0) == N_GROUPS - 1) & (ii == nt - 1))

    def rev(ii):
        return nt - 1 - ii

    def tail(g, ii):
        return (jnp.maximum(rev(ii) * (tt // 8) - 1, 0), g)

    vec = pl.BlockSpec((1, w), lambda g, ii: (0, g))
    mat = pl.BlockSpec((None, w, w), lambda g, ii: (g, 0, 0))
    tile = pl.BlockSpec((tt, w), lambda g, ii: (rev(ii), g))
    diag_shape = (N_GROUPS, GROUP_W // RNN_BLOCK_W, RNN_BLOCK_W, RNN_BLOCK_W)
    diag = pl.BlockSpec((None,) + diag_shape[1:], lambda g, ii: (g, 0, 0, 0))
    outs = pl.pallas_call(
        body, name=name, grid=(N_GROUPS, nt),
        in_specs=[tile, pl.BlockSpec((tt, w), lambda g, ii: (rev(ii), N_GROUPS + g)), pl.BlockSpec((8, w), tail),
                  tile, pl.BlockSpec((8, w), tail), tile,
                  pl.BlockSpec((4, w), lambda g, ii: (0, g)), vec, mat, vec, mat, vec, vec] + host.in_specs,
        out_specs=[tile, tile, pl.BlockSpec((8, w), lambda g, ii: (0, g)), diag, diag] + host.out_specs,
        out_shape=[jax.ShapeDtypeStruct((t, D_RNN), BF16), jax.ShapeDtypeStruct((t, D_RNN), BF16),
                   jax.ShapeDtypeStruct((8, D_RNN), F32),
                   jax.ShapeDtypeStruct(diag_shape, F32), jax.ShapeDtypeStruct(diag_shape, F32)] + host.out_shapes,
        scratch_shapes=[pltpu.VMEM((tt + 8, w), F32), pltpu.VMEM((tt + 8, w), F32), pltpu.VMEM((8, w), F32),
                        pltpu.VMEM((8, w), F32), pltpu.VMEM((w, w), F32), pltpu.VMEM((w, w), F32)] + host.scratch_shapes,
        compiler_params=_params("arbitrary" if ride else "parallel", "arbitrary"))(
            rx_rg, rx_rg, rx_rg, y, y, dz, cw, cb, wa, ba, wx, bx, lam, *host.arrays)
    res, landed = host.results(outs, 5)
    return (*res, landed) if ride else tuple(res)


def _half_mask(shape, half):
    lane = lax.broadcasted_iota(jnp.int32, shape, 1)
    return (lane >= HEAD_DIM) if half else (lane < HEAD_DIM)


def _dup_half(t, half):
    sel = jnp.where(_half_mask(t.shape, half), t, 0.0)
    return sel + pltpu.roll(sel, HEAD_DIM, 1)


def _band_geometry(n):
    qi = lax.broadcasted_iota(jnp.int32, (BLOCK, 2 * BLOCK), 0)
    kj = lax.broadcasted_iota(jnp.int32, (BLOCK, 2 * BLOCK), 1)
    dist = BLOCK + qi - kj
    first_key = jnp.where(n > 0, 0, BLOCK)
    valid = (dist >= 0) & (dist < BLOCK) & (kj >= first_key)
    return dist.astype(F32), valid


GROUP = 4


def _kv_dup(prev_ref, cur_ref, hk, scale=1.0):
    tile = hk // 2
    kt = jnp.concatenate([prev_ref[:, tile * LANE:(tile + 1) * LANE], cur_ref[:, tile * LANE:(tile + 1) * LANE]], axis=0)
    return (_dup_half(kt.astype(F32), hk % 2) * scale).astype(BF16)


def _fill_bias(bias_ref, sm_ref, n):
    distf, valid = _band_geometry(n)
    for head in range(N_Q_HEADS):
        bias_ref[head] = jnp.where(valid, -sm_ref[1, head] * distf, MASKED)


def _head_scores(q2s, half, kdup, bias):
    qm = jnp.where(_half_mask(q2s.shape, half), q2s, jnp.zeros_like(q2s))
    return qm, lax.dot_general(qm, kdup, NT_DIMS, preferred_element_type=F32) + bias


def _attn_specs(nb, clamp_last):
    def blk(n):
        return jnp.minimum(n, nb - 1) if clamp_last else n

    q_spec = pl.BlockSpec((BLOCK, D_MODEL), lambda n: (blk(n), 0))
    k_prev = pl.BlockSpec((BLOCK, D_KV), lambda n: (jnp.maximum(blk(n) - 1, 0), D_MODEL // D_KV))
    k_cur = pl.BlockSpec((BLOCK, D_KV), lambda n: (blk(n), D_MODEL // D_KV))
    v_prev = pl.BlockSpec((BLOCK, D_KV), lambda n: (jnp.maximum(blk(n) - 1, 0), D_MODEL // D_KV + 1))
    v_cur = pl.BlockSpec((BLOCK, D_KV), lambda n: (blk(n), D_MODEL // D_KV + 1))
    return q_spec, k_prev, k_cur, v_prev, v_cur


def _attn_fwd(sm, qkv, ag_ml, name, ride=None):
    t = qkv.shape[0]
    nb = t // BLOCK

    host = _Hosted(ride, 7, 3, 1)

    def body(*refs):
        (sm_ref, q_ref, kp_ref, kc_ref, vp_ref, vc_ref, ag_ref, y_ref, z_ref, lse_ref, bias_ref), start, finish = host.split(refs)
        n = pl.program_id(0)
        start(n == 0)

        @pl.when(n <= 1)
        def _():
            _fill_bias(bias_ref, sm_ref, n)

        lane = lax.broadcasted_iota(jnp.int32, (BLOCK, LANE), 1)
        low = lane < HEAD_DIM
        lse = jnp.zeros((BLOCK, LANE), F32)
        for hk in range(N_Q_HEADS // GROUP):
            kdup = _kv_dup(kp_ref, kc_ref, hk)
            vdup = _kv_dup(vp_ref, vc_ref, hk)
            for k in (0, 1):
                c = slice((2 * hk + k) * LANE, (2 * hk + k + 1) * LANE)
                q2s = q_ref[:, c] * ATTN_SCALE
                outs = []
                for half in (0, 1):
                    head = GROUP * hk + 2 * k + half
                    _, s = _head_scores(q2s, half, kdup, bias_ref[head])
                    sink = sm_ref[0, head]
                    m = jnp.maximum(jnp.max(s, axis=1, keepdims=True), sink)
                    e = jnp.exp(s - m)
                    l = jnp.sum(e, axis=1, keepdims=True) + jnp.exp(sink - m)
                    outs.append(jnp.dot((e * (1.0 / l)).astype(BF16), vdup, preferred_element_type=F32))
                    lse = jnp.where(lane == head, m + jnp.log(l), lse)
                yt = jnp.where(low, outs[0], outs[1])
                y_ref[:, c] = yt
                ag = ag_ref[:, c]
                z_ref[:, c] = (yt * ag * _sigmoid(ag)).astype(BF16)
        lse_ref[...] = lse
        finish(n == nb - 1)

    q_spec, k_prev, k_cur, v_prev, v_cur = _attn_specs(nb, False)
    wide = pl.BlockSpec((BLOCK, D_MODEL), lambda n: (n, 0))
    outs = pl.pallas_call(
        body, name=name, grid=(nb,),
        in_specs=[pl.BlockSpec(memory_space=pltpu.SMEM), q_spec, k_prev, k_cur, v_prev, v_cur, wide] + host.in_specs,
        out_specs=[wide, wide, pl.BlockSpec((BLOCK, LANE), lambda n: (n, 0))] + host.out_specs,
        out_shape=[jax.ShapeDtypeStruct((t, D_MODEL), F32), jax.ShapeDtypeStruct((t, D_MODEL), BF16),
                   jax.ShapeDtypeStruct((t, LANE), F32)] + host.out_shapes,
        scratch_shapes=[pltpu.VMEM((N_Q_HEADS, BLOCK, 2 * BLOCK), F32)] + host.scratch_shapes,
        compiler_params=_params("arbitrary"))(sm, qkv, qkv, qkv, qkv, qkv, ag_ml, *host.arrays)
    res, landed = host.results(outs, 3)
    return (*res, landed) if ride else tuple(res)


def _attn_bwd(sm, qkv, ag_ml, y, lse, dz, name, ride=None):
    t = qkv.shape[0]
    nb = t // BLOCK
    host = _Hosted(ride, 10, 4, 3)

    def body(*refs):
        host_refs, start, finish = host.split(refs)
        (sm_ref, q_ref, kp_ref, kc_ref, vp_ref, vc_ref, ag_ref, y_ref, lse_ref, dz_ref,
         dq_ref, dkv_ref, dag_ref, ds_ref, ck_ref, cv_ref, bias_ref) = host_refs
        n = pl.program_id(0)
        start(n == 0)

        @pl.when(n == 0)
        def _():
            ck_ref[...] = jnp.zeros_like(ck_ref)
            cv_ref[...] = jnp.zeros_like(cv_ref)
            ds_ref[...] = jnp.zeros_like(ds_ref)

        @pl.when(n <= 1)
        def _():
            _fill_bias(bias_ref, sm_ref, n)

        @pl.when(n < nb)
        def _():
            lane8 = lax.broadcasted_iota(jnp.int32, (8, LANE), 1)
            row8 = lax.broadcasted_iota(jnp.int32, (8, LANE), 0)
            dsink = jnp.zeros((8, LANE), F32)
            dk_heads, dv_heads = [], []
            lse_tile = lse_ref[...]
            for hk in range(N_Q_HEADS // GROUP):
                kdup = _kv_dup(kp_ref, kc_ref, hk)
                ks = _kv_dup(kp_ref, kc_ref, hk, ATTN_SCALE)
                vdup = _kv_dup(vp_ref, vc_ref, hk)
                dk_acc = jnp.zeros((2 * BLOCK, LANE), F32)
                dv_acc = jnp.zeros((2 * BLOCK, LANE), F32)
                for k in (0, 1):
                    c = slice((2 * hk + k) * LANE, (2 * hk + k + 1) * LANE)
                    ag = ag_ref[:, c]
                    sg = _sigmoid(ag)
                    dzt = dz_ref[:, c]
                    yt = y_ref[:, c]
                    dag_ref[:, c] = (dzt * yt * (sg * (1.0 + ag * (1.0 - sg)))).astype(BF16)
                    dyt = dzt * (ag * sg)
                    q2s = q_ref[:, c] * ATTN_SCALE
                    dq_halves = []
                    for half in (0, 1):
                        head = GROUP * hk + 2 * k + half
                        qm, s = _head_scores(q2s, half, kdup, bias_ref[head])
                        lh = lse_tile[:, head:head + 1]
                        probs = jnp.exp(s - lh)
                        psink = jnp.exp(sm_ref[0, head] - lh)
                        dyh = jnp.where(_half_mask(dyt.shape, half), dyt, 0.0)
                        delta = jnp.sum(dyh * yt, axis=1, keepdims=True)
                        dyh16 = dyh.astype(BF16)
                        dp = lax.dot_general(dyh16, vdup, NT_DIMS, preferred_element_type=F32)
                        ds16 = (probs * (dp - delta)).astype(BF16)
                        dsink = dsink + jnp.where((row8 == 0) & (lane8 == head),
                                                  -jnp.sum(psink * delta, axis=0, keepdims=True), 0.0)
                        dq_halves.append(jnp.dot(ds16, ks, preferred_element_type=F32))
                        dk_acc = dk_acc + lax.dot_general(ds16, qm, TN_DIMS, preferred_element_type=F32)
                        dv_acc = dv_acc + lax.dot_general(probs.astype(BF16), dyh16, TN_DIMS, preferred_element_type=F32)
                    dq_ref[:, c] = jnp.where(_half_mask((BLOCK, LANE), 0), dq_halves[0], dq_halves[1]).astype(BF16)
                dk_heads.append(dk_acc + pltpu.roll(dk_acc, HEAD_DIM, 1))
                dv_heads.append(dv_acc + pltpu.roll(dv_acc, HEAD_DIM, 1))
            ds_ref[...] += dsink
            low = _half_mask((2 * BLOCK, LANE), 0)
            for tile in range(2):
                cols = slice(tile * LANE, (tile + 1) * LANE)
                dkt = jnp.where(low, dk_heads[2 * tile], dk_heads[2 * tile + 1])
                dvt = jnp.where(low, dv_heads[2 * tile], dv_heads[2 * tile + 1])
                dkv_ref[:, cols] = (ck_ref[:, cols] + dkt[0:BLOCK, :]).astype(BF16)
                dkv_ref[:, D_KV + tile * LANE:D_KV + (tile + 1) * LANE] = (cv_ref[:, cols] + dvt[0:BLOCK, :]).astype(BF16)
                ck_ref[:, cols] = dkt[BLOCK:2 * BLOCK, :]
                cv_ref[:, cols] = dvt[BLOCK:2 * BLOCK, :]

        @pl.when(n == nb)
        def _():
            dkv_ref[:, 0:D_KV] = ck_ref[...].astype(BF16)
            dkv_ref[:, D_KV:2 * D_KV] = cv_ref[...].astype(BF16)

        finish(n == nb)

    q_spec, k_prev, k_cur, v_prev, v_cur = _attn_specs(nb, True)
    wide = pl.BlockSpec((BLOCK, D_MODEL), lambda n: (jnp.minimum(n, nb - 1), 0))
    outs = pl.pallas_call(
        body, name=name, grid=(nb + 1,),
        in_specs=[pl.BlockSpec(memory_space=pltpu.SMEM), q_spec, k_prev, k_cur, v_prev, v_cur, wide, wide,
                  pl.BlockSpec((BLOCK, LANE), lambda n: (jnp.minimum(n, nb - 1), 0)), wide] + host.in_specs,
        out_specs=[wide, pl.BlockSpec((BLOCK, 2 * D_KV), lambda n: (jnp.maximum(n - 1, 0), 0)), wide,
                   pl.BlockSpec((8, LANE), lambda n: (0, 0))] + host.out_specs,
        out_shape=[jax.ShapeDtypeStruct((t, D_MODEL), BF16), jax.ShapeDtypeStruct((t, 2 * D_KV), BF16),
                   jax.ShapeDtypeStruct((t, D_MODEL), BF16), jax.ShapeDtypeStruct((8, LANE), F32)] + host.out_shapes,
        scratch_shapes=[pltpu.VMEM((BLOCK, D_KV), F32), pltpu.VMEM((BLOCK, D_KV), F32),
                        pltpu.VMEM((N_Q_HEADS, BLOCK, 2 * BLOCK), F32)] + host.scratch_shapes,
        compiler_params=_params("arbitrary"))(sm, qkv, qkv, qkv, qkv, qkv, ag_ml, y, lse, dz, *host.arrays)
    res, landed = host.results(outs, 4)
    return (*res, landed) if ride else tuple(res)


def _local_grads(x, target, p, project, late_weights=None, reduce_out=None, reduce_in=None):
    h, rx_rg, qkv, ag_ml, wt = project(x, p["pre_g"])
    if late_weights is None:
        y_attn, z_attn, lse = _attn_fwd(p["sm"], qkv, ag_ml, "attn_fwd")
    else:
        y_attn, z_attn, lse, landed = _attn_fwd(p["sm"], qkv, ag_ml, "attn_fwd", ride=late_weights[0])
        p = {**p, **late_weights[1](landed)}
    lru = (p["cw"], p["cb"], p["wbd_a"], p["b_a"], p["wbd_x"], p["b_x"], p["lam"])
    y_rnn, z_rnn = _rnn_fwd(rx_rg, *lru, "rnn_fwd")
    br_rnn, br_attn, merged, dout, dy, st_post = _branches_fwd(
        z_rnn, z_attn, ag_ml, p["b_gate"], p["w_rnn"], p["w_attn"], p["w_out"], x, target, p["post_g"], "branches_fwd")

    dbr_rnn, dbr_attn, d_ml, dz_rnn, dz_attn, st_merge = _branches_bwd(
        dout, br_rnn, br_attn, ag_ml, p["b_gate"], p["w_rnn"], p["w_attn"], p["w_out"], "branches_bwd")
    gw_out = _mm_tn(merged, dout, "gw_out")
    gw_rnn = _mm_tn(z_rnn, dbr_rnn, "gw_rnn")
    gw_attn = _mm_tn(z_attn, dbr_attn, "gw_attn")
    red_out = red_in = None
    if reduce_out is None:
        d_rx, d_rg, st_rnn, g_rg_a, g_rg_x = _rnn_bwd(rx_rg, y_rnn, dz_rnn, *lru, "rnn_bwd")
        dq, dkv, d_ag, st_sink = _attn_bwd(p["sm"], qkv, ag_ml, y_attn, lse, dz_attn, "attn_bwd")
    else:
        d_rx, d_rg, st_rnn, g_rg_a, g_rg_x, from_sibling = _rnn_bwd(
            rx_rg, y_rnn, dz_rnn, *lru, "rnn_bwd", ride=reduce_out[0](gw_rnn, gw_attn, gw_out))
        dq, dkv, d_ag, st_sink, red_out = _attn_bwd(p["sm"], qkv, ag_ml, y_attn, lse, dz_attn, "attn_bwd",
                                                    ride=reduce_out[1](from_sibling, g_rg_a, g_rg_x))

    segs = [d_rx, d_rg, dq, dkv, d_ag, d_ml]
    gwt = _mm_tn_seg(segs, h, "gw_in")
    if reduce_in is None:
        grad_x, st_pre = _input_grad(segs, wt, x, p["pre_g"], dy, "input_grad")
    else:
        grad_x, st_pre, red_in = _input_grad(segs, wt, x, p["pre_g"], dy, "input_grad", ride=reduce_in(gwt))
    return dict(grad_x=grad_x, gwt=gwt, gw_rnn=gw_rnn, gw_attn=gw_attn, gw_out=gw_out,
                st_post=st_post, st_merge=st_merge, st_rnn=st_rnn, st_sink=st_sink, st_pre=st_pre,
                g_rg_a=g_rg_a, g_rg_x=g_rg_x, red_out=red_out, red_in=red_in)


def _place():
    x, y, c = lax.axis_index("x"), lax.axis_index("y"), lax.axis_index("c")
    return x, y, c


def _gather_ride(shards):
    n = len(shards)

    def copies(ins, outs, sems):
        send_sems, recv_sems, local_sems = sems
        x, y, c = _place()
        me, sibling = (x, y, c), (x, y, 1 - c)
        chips = [(1 - x, y), (x, 1 - y), (1 - x, 1 - y)]

        def slot(a, dev):
            return outs[a].at[4 * dev[0] + 2 * dev[1] + dev[2]]

        def copy(a, k, block, to, src=None):
            return pltpu.make_async_remote_copy(
                src_ref=slot(a, block) if src is None else src, dst_ref=slot(a, block),
                send_sem=send_sems.at[a, k], recv_sem=recv_sems.at[a, k], device_id=to, device_id_type=MESH)

        mine = [pltpu.make_async_copy(ins[a], slot(a, me), local_sems.at[a]) for a in range(n)]
        first = []
        for a in range(n):
            first.append(copy(a, 0, me, sibling, src=ins[a]))
            first += [copy(a, 1 + j, me, (*chip, c), src=ins[a]) for j, chip in enumerate(chips)]
        return me, sibling, chips, c, copy, mine, first

    def start(ins, outs, sems):
        *_, mine, first = copies(ins, outs, sems)
        for cp in mine + first:
            cp.start()

    def finish(ins, outs, sems):
        me, sibling, chips, c, copy, mine, first = copies(ins, outs, sems)
        passed = []
        for j, chip in enumerate(chips):
            for a in range(n):
                copy(a, 1 + j, (*chip, c), me).wait_recv()
                cp = copy(a, 4 + j, (*chip, c), sibling)
                cp.start()
                passed.append(cp)
        for a in range(n):
            copy(a, 0, sibling, me).wait_recv()
            for j, chip in enumerate(chips):
                copy(a, 4 + j, (*chip, 1 - c), me).wait_recv()
        for cp in first + passed:
            cp.wait_send()
        for cp in mine:
            cp.wait()

    return _Ride(
        shards, [jax.ShapeDtypeStruct((N_DEV, *s.shape), s.dtype) for s in shards],
        [pltpu.SemaphoreType.DMA((n, 7)), pltpu.SemaphoreType.DMA((n, 7)), pltpu.SemaphoreType.DMA((n,))],
        start, finish)


def _sibling_ride(scatter, whole):
    ns, nw = len(scatter), len(whole)

    def copies(ins, outs, sems):
        send_sems, recv_sems = sems
        x, y, c = _place()
        cps = [pltpu.make_async_remote_copy(
            src_ref=ins[a].at[2 * chip + (1 - c)], dst_ref=outs[a].at[chip],
            send_sem=send_sems.at[a * N_CHIP + chip], recv_sem=recv_sems.at[a * N_CHIP + chip],
            device_id=(x, y, 1 - c), device_id_type=MESH) for a in range(ns) for chip in range(N_CHIP)]
        cps += [pltpu.make_async_remote_copy(
            src_ref=ins[ns + a], dst_ref=outs[ns + a],
            send_sem=send_sems.at[ns * N_CHIP + a], recv_sem=recv_sems.at[ns * N_CHIP + a],
            device_id=(x, y, 1 - c), device_id_type=MESH) for a in range(nw)]
        return cps

    def start(ins, outs, sems):
        for cp in copies(ins, outs, sems):
            cp.start()

    def finish(ins, outs, sems):
        for cp in copies(ins, outs, sems):
            cp.wait()

    n_sem = ns * N_CHIP + nw
    return _Ride(
        list(scatter) + list(whole),
        [jax.ShapeDtypeStruct((N_CHIP, *s.shape[1:]), s.dtype) for s in scatter]
        + [jax.ShapeDtypeStruct(s.shape, s.dtype) for s in whole],
        [pltpu.SemaphoreType.DMA((n_sem,)), pltpu.SemaphoreType.DMA((n_sem,))], start, finish)


def _chips_ride(scatter, whole):
    ns, nw = len(scatter), len(whole)
    n = ns + nw

    def copies(ins, outs, sems):
        send_sems, recv_sems, local_sems = sems
        x, y, c = _place()
        own = 2 * x + y
        chips = [(1 - x, y), (x, 1 - y), (1 - x, 1 - y)]

        def src(a, chip_idx):
            return ins[a].at[chip_idx] if a < ns else ins[a]

        local = [pltpu.make_async_copy(src(a, own), outs[a].at[own], local_sems.at[a]) for a in range(n)]
        sent = [pltpu.make_async_remote_copy(
            src_ref=src(a, 2 * chip[0] + chip[1]), dst_ref=outs[a].at[own],
            send_sem=send_sems.at[a, j], recv_sem=recv_sems.at[a, own], device_id=(*chip, c), device_id_type=MESH)
            for a in range(n) for j, chip in enumerate(chips)]
        return chips, c, local, sent

    def start(ins, outs, sems):
        _, _, local, sent = copies(ins, outs, sems)
        for cp in local + sent:
            cp.start()

    def finish(ins, outs, sems):
        send_sems, recv_sems, _ = sems
        chips, c, local, sent = copies(ins, outs, sems)
        for a in range(n):
            for chip in chips:
                k = 2 * chip[0] + chip[1]
                pltpu.make_async_remote_copy(
                    src_ref=outs[a].at[k], dst_ref=outs[a].at[k], send_sem=send_sems.at[a, 0],
                    recv_sem=recv_sems.at[a, k], device_id=(*chip, c), device_id_type=MESH).wait_recv()
        for cp in sent:
            cp.wait_send()
        for cp in local:
            cp.wait()

    return _Ride(
        list(scatter) + list(whole),
        [jax.ShapeDtypeStruct(s.shape, s.dtype) for s in scatter]
        + [jax.ShapeDtypeStruct((N_CHIP, *s.shape), s.dtype) for s in whole],
        [pltpu.SemaphoreType.DMA((n, 3)), pltpu.SemaphoreType.DMA((n, N_CHIP)), pltpu.SemaphoreType.DMA((n,))],
        start, finish)


def _pair_sum_scatter(parts, recvs, core, name):
    na = len(parts)
    _, r, cdim = parts[0].shape
    tr = min(r, 416 if r % 416 == 0 else 128)

    def body(core_ref, *refs):
        del core_ref
        for a in range(na):
            refs[2 * na + a][...] = (refs[a][...] + refs[na + a][...]).astype(BF16)

    blk = (None, tr, cdim)
    mine = pl.BlockSpec(blk, lambda k, i, core_ref: (2 * k + core_ref[0], i, 0))
    slot = pl.BlockSpec(blk, lambda k, i, core_ref: (k, i, 0))
    return pl.pallas_call(
        body, name=name,
        grid_spec=pltpu.PrefetchScalarGridSpec(
            num_scalar_prefetch=1, grid=(N_CHIP, r // tr),
            in_specs=[mine] * na + [slot] * na, out_specs=[slot] * na),
        out_shape=[jax.ShapeDtypeStruct((N_CHIP, r, cdim), BF16)] * na,
        compiler_params=_params("parallel", "parallel"))(core, *parts, *recvs)


def _pair_sum_whole(mine, recvs, name):
    na = len(mine)

    def body(*refs):
        for a in range(na):
            refs[2 * na + a][...] = refs[a][...] + refs[na + a][...]

    return pl.pallas_call(body, name=name, out_shape=[jax.ShapeDtypeStruct(m.shape, F32) for m in mine])(*mine, *recvs)


def _allreduce_small(pack, name):
    shape = pack.shape

    def body(x_ref, o_ref, sib_ref, chip_ref, send_sems, recv_sems):
        x, y, c = _place()
        own = 2 * x + y
        chips = [(1 - x, y), (x, 1 - y), (1 - x, 1 - y)]
        to_sibling = pltpu.make_async_remote_copy(
            src_ref=x_ref, dst_ref=sib_ref, send_sem=send_sems.at[0], recv_sem=recv_sems.at[0],
            device_id=(x, y, 1 - c), device_id_type=MESH)
        to_sibling.start()
        to_sibling.wait()
        chip_ref[own] = x_ref[...] + sib_ref[...]
        sent = [pltpu.make_async_remote_copy(
            src_ref=chip_ref.at[own], dst_ref=chip_ref.at[own], send_sem=send_sems.at[1 + j],
            recv_sem=recv_sems.at[1 + own], device_id=(*chip, c), device_id_type=MESH) for j, chip in enumerate(chips)]
        for cp in sent:
            cp.start()
        for chip in chips:
            k = 2 * chip[0] + chip[1]
            pltpu.make_async_remote_copy(
                src_ref=chip_ref.at[k], dst_ref=chip_ref.at[k], send_sem=send_sems.at[1],
                recv_sem=recv_sems.at[1 + k], device_id=(*chip, c), device_id_type=MESH).wait_recv()
        for cp in sent:
            cp.wait_send()
        o_ref[...] = (chip_ref[0] + chip_ref[1]) + (chip_ref[2] + chip_ref[3])

    return pl.pallas_call(
        body, name=name, out_shape=jax.ShapeDtypeStruct(shape, F32),
        in_specs=[pl.BlockSpec(memory_space=pltpu.VMEM)], out_specs=pl.BlockSpec(memory_space=pltpu.VMEM),
        scratch_shapes=[pltpu.VMEM(shape, F32), pltpu.VMEM((N_CHIP, *shape), F32),
                        pltpu.SemaphoreType.DMA((4,)), pltpu.SemaphoreType.DMA((1 + N_CHIP,))],
    )(pack)


def _adamw(g, w, m, v):
    m = ADAM_B1 * m + (1.0 - ADAM_B1) * g
    v = ADAM_B2 * v + (1.0 - ADAM_B2) * (g * g)
    m_hat = m / (1.0 - ADAM_B1 ** ADAM_STEP)
    v_hat = v / (1.0 - ADAM_B2 ** ADAM_STEP)
    delta = -ADAM_LR * (m_hat / (jnp.sqrt(v_hat) + ADAM_EPS) + ADAM_WD * w)
    return delta, m, v


def _adam_parts(parts, w, m, v, name, tr=None):
    npart, r, c = parts.shape
    tr = r if tr is None else min(tr, r)

    def body(p_ref, w_ref, m_ref, v_ref, g_ref, d_ref, nm_ref, nv_ref):
        g = p_ref[0].astype(F32)
        for k in range(1, npart):
            g = g + p_ref[k].astype(F32)
        g_ref[...] = g
        d_ref[...], nm_ref[...], nv_ref[...] = _adamw(g, w_ref[...], m_ref[...], v_ref[...])

    tile = pl.BlockSpec((tr, c), lambda i: (i, 0))
    return pl.pallas_call(
        body, name=name, grid=(r // tr,),
        in_specs=[pl.BlockSpec((npart, tr, c), lambda i: (0, i, 0)), tile, tile, tile],
        out_specs=[tile] * 4, out_shape=[jax.ShapeDtypeStruct((r, c), F32)] * 4,
        compiler_params=_params("parallel"))(parts, w, m, v)


def _block_diag(w):
    w4 = w.reshape(N_GROUPS, 4, RNN_BLOCK_W, RNN_BLOCK_W)
    eye = jnp.eye(4, dtype=w.dtype)
    return jnp.einsum("gbij,bc->gbicj", w4, eye).reshape(N_GROUPS, GROUP_W, GROUP_W).astype(BF16)


SMALL_ROWS = 16
ROW_PRE_G, ROW_BGATE, ROW_CONV_B, ROW_B_A, ROW_B_X, ROW_LAM, ROW_POST_G, ROW_LOSS, ROW_SINKS, ROW_CONV_W = 0, 1, 3, 4, 5, 6, 7, 8, 9, 10


def _pack_stats(st_pre, st_merge, st_rnn, st_post, st_sink, name):
    d = D_MODEL

    def body(pre_ref, mg_ref, rnn_ref, post_ref, sink_ref, o_ref):
        rnn = rnn_ref[...]
        sinks = jnp.concatenate([sink_ref[0:1, :], jnp.zeros((1, d - LANE), F32)], axis=1)
        o_ref[0:8, :] = _rows8([pre_ref[0:1, :], mg_ref[0:1, :], mg_ref[1:2, :], rnn[0:1], rnn[1:2], rnn[2:3], rnn[3:4],
                                post_ref[0:1, :]], d)
        o_ref[8:16, :] = _rows8([post_ref[1:2, :], sinks, rnn[4:5], rnn[5:6], rnn[6:7], rnn[7:8]], d)

    return pl.pallas_call(body, name=name, out_shape=jax.ShapeDtypeStruct((SMALL_ROWS, d), F32))(
        st_pre, st_merge, st_rnn, st_post, st_sink)


def _adam_small(total, w, m, v, name):
    d = D_MODEL
    n_in = len(w)

    def pack(refs):
        pre, bg, cbias, ba, bx, lam, post, sinks = [r[...] for r in refs]
        top = _rows8([pre, bg[:, 0:d], bg[:, d:2 * d], cbias, ba, bx, lam, post], d)
        return jnp.concatenate([top, _rows8([jnp.zeros((1, d), F32), sinks], d)], axis=0)

    def body(*refs):
        p_ref = refs[0]
        w_refs, m_refs, v_refs = (refs[1 + k * n_in:1 + (k + 1) * n_in] for k in range(3))
        outs = refs[1 + 3 * n_in:]
        g = p_ref[...]
        res = (g,) + _adamw(g, pack(w_refs), pack(m_refs), pack(v_refs))
        for k in range(4):
            outs[k][...] = res[k]
            outs[4 + k][...] = jnp.concatenate([res[k][ROW_BGATE:ROW_BGATE + 1], res[k][ROW_BGATE + 1:ROW_BGATE + 2]], axis=1)

    return pl.pallas_call(
        body, name=name,
        out_shape=[jax.ShapeDtypeStruct((SMALL_ROWS, d), F32)] * 4 + [jax.ShapeDtypeStruct((1, 2 * d), F32)] * 4,
    )(total, *w, *m, *v)


def kernel(x, pre_norm_g, w_in, b_gate, conv_w, conv_b, w_rg_a, b_rg_a, w_rg_x, b_rg_x, lru_lambda, attn_sinks, w_rnn_out, w_attn_out, w_out, post_norm_g, loss_target, m_pre_norm_g, m_w_in, m_b_gate, m_conv_w, m_conv_b, m_w_rg_a, m_b_rg_a, m_w_rg_x, m_b_rg_x, m_lru_lambda, m_attn_sinks, m_w_rnn_out, m_w_attn_out, m_w_out, m_post_norm_g, v_pre_norm_g, v_w_in, v_b_gate, v_conv_w, v_conv_b, v_w_rg_a, v_b_rg_a, v_w_rg_x, v_b_rg_x, v_lru_lambda, v_attn_sinks, v_w_rnn_out, v_w_attn_out, v_w_out, v_post_norm_g):
    cx, cy, cc = _place()
    dev = 4 * cx + 2 * cy + cc
    core = jnp.reshape(cc, (1,)).astype(jnp.int32)

    w_in_t, m_in_t, v_in_t = (jnp.transpose(a[0]) for a in (w_in, m_w_in, v_w_in))
    wt_shard = w_in_t.astype(BF16)

    def project(xs, pre_g):
        h, rx_rg, qkv, ag_ml, wt_all = _gather_project(xs, pre_g, wt_shard, "gather_project")
        return h, rx_rg, qkv, ag_ml, wt_all.reshape(D_IN, D_MODEL)

    heads = jnp.arange(1, N_Q_HEADS + 1, dtype=F32)
    slopes = jnp.exp2(-ALIBI_MAX_BIAS * heads / N_Q_HEADS)
    b_a = b_rg_a.reshape(1, D_RNN)
    b_x = b_rg_x.reshape(1, D_RNN)
    p = dict(
        pre_g=pre_norm_g, post_g=post_norm_g, b_gate=b_gate, cb=conv_b,
        wbd_a=_block_diag(w_rg_a[0]), b_a=b_a, wbd_x=_block_diag(w_rg_x[0]), b_x=b_x, lam=lru_lambda,
        sm=jnp.pad(attn_sinks, ((0, 1), (0, 0))) + jnp.pad(slopes[None, :], ((1, 0), (0, 0))))

    def late_unpack(landed):
        w_rnn_all, w_attn_all, w_out_all, cw_all = landed
        return dict(w_rnn=w_rnn_all.reshape(D_RNN, D_MODEL), w_attn=w_attn_all.reshape(D_MODEL, D_MODEL),
                    w_out=w_out_all.reshape(D_MODEL, D_MODEL), cw=jnp.transpose(cw_all, (1, 0, 2)).reshape(4, D_RNN))

    late_weights = (_gather_ride([w_rnn_out[0].astype(BF16), w_attn_out[0].astype(BF16), w_out[0].astype(BF16), conv_w[0]]),
                    late_unpack)
    flat = (RNN_BLOCKS * RNN_BLOCK_W, RNN_BLOCK_W)

    out_scatter = []

    def out_sibling(gw_rnn, gw_attn, gw_out):
        out_scatter.extend(gw.reshape(N_DEV, SHARD_OUT, D_MODEL) for gw in (gw_rnn, gw_attn, gw_out))
        return _sibling_ride(out_scatter, [])

    def out_chips(from_sibling, g_rg_a, g_rg_x):
        whole = [g_rg_a.reshape(flat), g_rg_x.reshape(flat)]
        rg_sibling = _run_ride(_sibling_ride([], whole), "sibling_rg")
        return _chips_ride(_pair_sum_scatter(out_scatter, from_sibling, core, "pair_out"),
                           _pair_sum_whole(whole, rg_sibling, "pair_rg"))

    def reduce_in(gwt):
        scatter = [gwt.reshape(N_DEV, SHARD_IN, D_MODEL)]
        from_sibling = _run_ride(_sibling_ride(scatter, []), "sibling_in")
        return _chips_ride(_pair_sum_scatter(scatter, from_sibling, core, "pair_in"), [])

    g = _local_grads(x[0], loss_target[0], p, project, late_weights, (out_sibling, out_chips), reduce_in)
    small = _allreduce_small(
        _pack_stats(g["st_pre"], g["st_merge"], g["st_rnn"], g["st_post"], g["st_sink"], "pack_stats"), "allreduce_small")

    out = {}
    red = g["red_out"]
    out["w_in"] = [jnp.transpose(o) for o in _adam_parts(g["red_in"][0], w_in_t, m_in_t, v_in_t, "adam_w_in", tr=SHARD_IN // 2)]
    out["w_rnn_out"] = _adam_parts(red[0], w_rnn_out[0], m_w_rnn_out[0], v_w_rnn_out[0], "adam_w_rnn_out")
    out["w_attn_out"] = _adam_parts(red[1], w_attn_out[0], m_w_attn_out[0], v_w_attn_out[0], "adam_w_attn_out")
    out["w_out"] = _adam_parts(red[2], w_out[0], m_w_out[0], v_w_out[0], "adam_w_out")
    out["w_rg_a"] = _adam_parts(red[3], w_rg_a.reshape(flat), m_w_rg_a.reshape(flat), v_w_rg_a.reshape(flat), "adam_w_rg_a", tr=256)
    out["w_rg_x"] = _adam_parts(red[4], w_rg_x.reshape(flat), m_w_rg_x.reshape(flat), v_w_rg_x.reshape(flat), "adam_w_rg_x", tr=256)

    def rows(pre, bg, cbias, ba, bx, lam, post, sinks):
        return (pre, bg, cbias, ba.reshape(1, D_RNN), bx.reshape(1, D_RNN), lam, post,
                jnp.pad(sinks, ((0, 0), (0, D_MODEL - N_Q_HEADS))))

    small_out = _adam_small(
        small,
        rows(pre_norm_g, b_gate, conv_b, b_rg_a, b_rg_x, lru_lambda, post_norm_g, attn_sinks),
        rows(m_pre_norm_g, m_b_gate, m_conv_b, m_b_rg_a, m_b_rg_x, m_lru_lambda, m_post_norm_g, m_attn_sinks),
        rows(v_pre_norm_g, v_b_gate, v_conv_b, v_b_rg_a, v_b_rg_x, v_lru_lambda, v_post_norm_g, v_attn_sinks),
        "adam_small")
    packed, bgate_out = small_out[:4], small_out[4:]
    g_cw = lax.dynamic_slice(packed[0][ROW_CONV_W:ROW_CONV_W + 4], (0, dev * SHARD_OUT), (4, SHARD_OUT))
    out["conv_w"] = _adam_parts(g_cw[None], conv_w[0], m_conv_w[0], v_conv_w[0], "adam_conv_w")

    def unpack(kind, name):
        if name == "b_gate":
            return bgate_out[kind]
        row = dict(pre_norm_g=ROW_PRE_G, conv_b=ROW_CONV_B, b_rg_a=ROW_B_A, b_rg_x=ROW_B_X, lru_lambda=ROW_LAM,
                   post_norm_g=ROW_POST_G, attn_sinks=ROW_SINKS)[name]
        r = packed[kind][row:row + 1]
        if name == "attn_sinks":
            return r[:, 0:N_Q_HEADS]
        if name in ("b_rg_a", "b_rg_x"):
            return r.reshape(1, RNN_BLOCKS, RNN_BLOCK_W)
        return r

    shapes = dict(w_in=(1, D_MODEL, SHARD_IN), w_rnn_out=(1, SHARD_OUT, D_MODEL), w_attn_out=(1, SHARD_OUT, D_MODEL),
                  w_out=(1, SHARD_OUT, D_MODEL), w_rg_a=(1, RNN_BLOCKS, RNN_BLOCK_W, RNN_BLOCK_W),
                  w_rg_x=(1, RNN_BLOCKS, RNN_BLOCK_W, RNN_BLOCK_W), conv_w=(1, 4, SHARD_OUT))
    weights = ["pre_norm_g", "w_in", "b_gate", "conv_w", "conv_b", "w_rg_a", "b_rg_a", "w_rg_x", "b_rg_x",
               "lru_lambda", "attn_sinks", "w_rnn_out", "w_attn_out", "w_out", "post_norm_g"]
    results = []
    for kind in range(4):
        for name in weights:
            if name in out:
                results.append(out[name][kind].reshape(shapes[name]))
            else:
                results.append(unpack(kind, name))
    loss = 0.5 / D_MODEL * jnp.sum(packed[0][ROW_LOSS])
    return (loss, g["grad_x"][None], *results)
```

```python
import functools

import jax
import jax.numpy as jnp
from jax import lax
from jax.experimental import pallas as pl
from jax.experimental.pallas import tpu as pltpu

F32 = jnp.float32
BF16 = jnp.bfloat16

D_MODEL = 1024
D_RNN = 1024
RNN_BLOCKS = 16
RNN_BLOCK_W = 64
LRU_C = 8.0
N_Q_HEADS = 16
HEAD_DIM = 64
D_KV = 256
BLOCK = 128
ALIBI_MAX_BIAS = 8.0
EPS = 1e-6
D_IN = 6656
N_DEV = 8
N_CHIP = 4
SHARD_IN = D_IN // N_DEV
SHARD_OUT = D_MODEL // N_DEV
ATTN_SCALE = HEAD_DIM ** -0.5
MASKED = -1e30

ADAM_LR = 0.001
ADAM_B1 = 0.9
ADAM_B2 = 0.999
ADAM_EPS = 1e-08
ADAM_WD = 0.01
ADAM_STEP = 10

VMEM_LIMIT_BYTES = 52 * 1024 * 1024
LANE = 128
GROUP_W = 256
N_GROUPS = D_RNN // GROUP_W
SEG_CHUNK = 512

NT_DIMS = (((1,), (1,)), ((), ()))
TN_DIMS = (((0,), (0,)), ((), ()))
MESH = pl.DeviceIdType.MESH
ANY = pl.BlockSpec(memory_space=pl.ANY)


def _params(*semantics):
    return pltpu.CompilerParams(dimension_semantics=semantics, vmem_limit_bytes=VMEM_LIMIT_BYTES)


def _sigmoid(x):
    return 0.5 * jnp.tanh(0.5 * x) + 0.5


def _log1p(e):
    u = 1.0 + e
    den = jnp.where(u == 1.0, 1.0, u - 1.0)
    return jnp.where(u == 1.0, e, jnp.log(u) * (e / den))


def _softplus(z):
    return jnp.maximum(z, 0.0) + _log1p(jnp.exp(-jnp.abs(z)))


def _rows8(rows, width):
    idx = lax.broadcasted_iota(jnp.int32, (8, width), 0)
    out = jnp.zeros((8, width), F32)
    for r, v in enumerate(rows):
        out = jnp.where(idx == r, v, out)
    return out


class _Ride:
    def __init__(self, arrays, out_shapes, scratch_shapes, start, finish, middle=None):
        self.arrays, self.out_shapes, self.scratch_shapes = list(arrays), list(out_shapes), list(scratch_shapes)
        self.start, self.finish, self.middle = start, finish, middle


class _Hosted:
    def __init__(self, ride, n_in, n_out, n_scratch=0):
        self.ride = ride
        self.sizes = (n_in, len(ride.arrays) if ride else 0, n_out, len(ride.out_shapes) if ride else 0, n_scratch)
        self.arrays = ride.arrays if ride else []
        self.in_specs = [ANY] * len(self.arrays)
        self.out_shapes = ride.out_shapes if ride else []
        self.out_specs = [ANY] * len(self.out_shapes)
        self.scratch_shapes = ride.scratch_shapes if ride else []

    def split(self, refs):
        n_in, r_in, n_out, r_out, n_scr = self.sizes
        cuts = [0, n_in, n_in + r_in, n_in + r_in + n_out, n_in + r_in + n_out + r_out, n_in + r_in + n_out + r_out + n_scr]
        host_in, ride_in, host_out, ride_out, host_scr = (refs[cuts[k]:cuts[k + 1]] for k in range(5))
        ride_scr = refs[cuts[5]:]

        def start(when):
            if self.ride is not None:
                pl.when(when)(lambda: self.ride.start(ride_in, ride_out, ride_scr))

        def finish(when):
            if self.ride is not None:
                pl.when(when)(lambda: self.ride.finish(ride_in, ride_out, ride_scr))

        def middle(when):
            if self.ride is not None and self.ride.middle is not None:
                pl.when(when)(lambda: self.ride.middle(ride_in, ride_out, ride_scr))

        start.middle = middle
        return tuple(host_in) + tuple(host_out) + tuple(host_scr), start, finish

    def results(self, outs, n_out):
        outs = list(outs) if isinstance(outs, (list, tuple)) else [outs]
        return outs[:n_out], outs[n_out:]


def _run_ride(ride, name):
    n_in, n_out = len(ride.arrays), len(ride.out_shapes)

    def body(*refs):
        ins, outs, sems = refs[:n_in], refs[n_in:n_in + n_out], refs[n_in + n_out:]
        ride.start(ins, outs, sems)
        ride.finish(ins, outs, sems)

    return pl.pallas_call(
        body, name=name, in_specs=[ANY] * n_in, out_specs=[ANY] * n_out,
        out_shape=ride.out_shapes, scratch_shapes=ride.scratch_shapes)(*ride.arrays)


def _load_resident(w_hbm, w_vmem, sems, first):
    def piece(c):
        rows = pl.ds(c * SEG_CHUNK, SEG_CHUNK)
        return pltpu.make_async_copy(w_hbm.at[rows], w_vmem.at[rows], sems.at[c])

    @pl.when(first)
    def _():
        for c in range(w_vmem.shape[0] // SEG_CHUNK):
            piece(c).start()

    def ready(c):
        @pl.when(first)
        def _():
            piece(c).wait()

    return ready


CHIP_ROWS = 2 * SHARD_IN
PROJ_WIDTHS = (2 * D_RNN, D_MODEL + 2 * D_KV, 3 * D_MODEL)
PROJ_DTYPES = (F32, BF16, F32)


def _chip_pieces():
    starts = [0, PROJ_WIDTHS[0], PROJ_WIDTHS[0] + PROJ_WIDTHS[1], D_IN]
    pieces = []
    for k in range(N_CHIP):
        lo, hi = k * CHIP_ROWS, (k + 1) * CHIP_ROWS
        cur = []
        for a in range(len(PROJ_WIDTHS)):
            s0, s1 = max(lo, starts[a]), min(hi, starts[a + 1])
            if s0 < s1:
                cur.append((a, s0 - starts[a], s1 - s0, s0 - lo))
        pieces.append(cur)
    return pieces


def _gather_project(x, g, wt_shard, name, tm=512):
    t, k = x.shape
    tm = min(tm, t)
    nt = t // tm
    pieces = _chip_pieces()
    w16 = max(w for cur in pieces for a, _, w, _ in cur if PROJ_DTYPES[a] == BF16)

    def body(x_ref, g_ref, shard_ref, h_out, rx_ref, qkv_ref, ag_ref, wt_all,
             w_c, stage, o32, o16, h_all, send_sems, recv_sems, local_sem, stage_sems, out_sems, h_sems):
        s, ti = pl.program_id(0), pl.program_id(1)
        px, py, pc = _place()
        me, sibling = (px, py, pc), (px, py, 1 - pc)
        chips = [(px, py), (1 - px, py), (px, 1 - py), (1 - px, 1 - py)]
        outs = (rx_ref, qkv_ref, ag_ref)

        def slot(dev):
            return wt_all.at[4 * dev[0] + 2 * dev[1] + dev[2]]

        def copy(kk, block, to, src=None):
            return pltpu.make_async_remote_copy(
                src_ref=slot(block) if src is None else src, dst_ref=slot(block),
                send_sem=send_sems.at[kk], recv_sem=recv_sems.at[kk], device_id=to, device_id_type=MESH)

        mine = pltpu.make_async_copy(shard_ref, slot(me), local_sem)
        first = [copy(0, me, sibling, src=shard_ref)] + [copy(1 + j, me, (*chips[1 + j], pc), src=shard_ref) for j in range(3)]
        passed = [copy(4 + j, (*chips[1 + j], pc), sibling) for j in range(3)]

        @pl.when((s == 0) & (ti == 0))
        def _():
            mine.start()
            for cp in first:
                cp.start()

        for step in range(N_CHIP):
            @pl.when((s == step) & (ti == 0))
            def _(step=step):
                chip = chips[step]
                if step == 0:
                    mine.wait()
                    copy(0, sibling, me).wait_recv()
                else:
                    copy(step, (*chip, pc), me).wait_recv()
                    passed[step - 1].start()
                    copy(3 + step, (*chip, 1 - pc), me).wait_recv()
                loads = [pltpu.make_async_copy(slot((*chip, core)), stage.at[pl.ds(core * SHARD_IN, SHARD_IN)], stage_sems.at[core])
                         for core in (0, 1)]
                for cp in loads:
                    cp.start()
                for cp in loads:
                    cp.wait()
                w_c[...] = stage[...].T

        rows = pl.ds(pl.multiple_of(ti * tm, tm), tm)

        @pl.when(s == 0)
        def _():
            xv = x_ref[...]
            h_all[rows, :] = (xv * lax.rsqrt(jnp.mean(xv * xv, axis=-1, keepdims=True) + EPS) * g_ref[...]).astype(BF16)

        res = jnp.dot(h_all[rows, :], w_c[...], preferred_element_type=F32)

        n = s * nt + ti
        buf = lax.rem(n, 2)
        chip_idx = [2 * cx + cy for cx, cy in chips]

        def chip_at(step):
            return jnp.where(step == 0, chip_idx[0], jnp.where(step == 1, chip_idx[1], jnp.where(step == 2, chip_idx[2], chip_idx[3])))

        def h_write(b, tile):
            return pltpu.make_async_copy(h_all.at[pl.ds(tile * tm, tm)], h_out.at[pl.ds(tile * tm, tm)], h_sems.at[b])

        def writes(kchip, b, tile):
            cps = []
            for idx, (a, col, w, src) in enumerate(pieces[kchip]):
                staged = o16.at[b, :, pl.ds(0, w)] if PROJ_DTYPES[a] == BF16 else o32.at[b, :, pl.ds(src, w)]
                cps.append(pltpu.make_async_copy(staged, outs[a].at[pl.ds(tile * tm, tm), pl.ds(col, w)], out_sems.at[b, idx]))
            return cps

        o32[buf] = res
        kcur = chip_at(s)
        for kchip in range(N_CHIP):
            for a, _, w, src in pieces[kchip]:
                if PROJ_DTYPES[a] == BF16:
                    @pl.when(kcur == kchip)
                    def _(w=w, src=src):
                        o16[buf, :, 0:w] = res[:, src:src + w].astype(BF16)

        @pl.when(n > 0)
        def _():
            kprev = chip_at(lax.div(n - 1, nt))
            for kchip in range(N_CHIP):
                @pl.when(kprev == kchip)
                def _(kchip=kchip):
                    for cp in writes(kchip, 1 - buf, lax.rem(n - 1, nt)):
                        cp.wait()

            @pl.when(n <= nt)
            def _():
                h_write(1 - buf, n - 1).wait()

        @pl.when(s == 0)
        def _():
            h_write(buf, ti).start()

        for kchip in range(N_CHIP):
            @pl.when(kcur == kchip)
            def _(kchip=kchip):
                for cp in writes(kchip, buf, ti):
                    cp.start()

        @pl.when(n == N_CHIP * nt - 1)
        def _():
            for kchip in range(N_CHIP):
                @pl.when(kcur == kchip)
                def _(kchip=kchip):
                    for cp in writes(kchip, buf, ti):
                        cp.wait()
            for cp in first + passed:
                cp.wait_send()

    tile = pl.BlockSpec((tm, k), lambda s, ti: (jnp.where(s == 0, ti, nt - 1), 0))
    return pl.pallas_call(
        body, name=name, grid=(N_CHIP, nt),
        in_specs=[tile, pl.BlockSpec((1, k), lambda s, ti: (0, 0)), ANY],
        out_specs=[ANY, ANY, ANY, ANY, ANY],
        out_shape=[jax.ShapeDtypeStruct((t, k), BF16)]
        + [jax.ShapeDtypeStruct((t, w), dt) for w, dt in zip(PROJ_WIDTHS, PROJ_DTYPES)]
        + [jax.ShapeDtypeStruct((N_DEV, SHARD_IN, k), BF16)],
        scratch_shapes=[pltpu.VMEM((k, CHIP_ROWS), BF16), pltpu.VMEM((CHIP_ROWS, k), BF16),
                        pltpu.VMEM((2, tm, CHIP_ROWS), F32), pltpu.VMEM((2, tm, w16), BF16), pltpu.VMEM((t, k), BF16),
                        pltpu.SemaphoreType.DMA((7,)), pltpu.SemaphoreType.DMA((7,)), pltpu.SemaphoreType.DMA,
                        pltpu.SemaphoreType.DMA((2,)), pltpu.SemaphoreType.DMA((2, 2)), pltpu.SemaphoreType.DMA((2,))],
        compiler_params=_params("arbitrary", "arbitrary"))(x, g, wt_shard)


def _mm_tn(a, b, name, tm=512, tk=4096):
    ktok, m = a.shape
    n = b.shape[1]
    tk = min(tk, ktok)

    def body(a_ref, b_ref, o_ref):
        @pl.when(pl.program_id(1) == 0)
        def _():
            o_ref[...] = jnp.zeros_like(o_ref)

        o_ref[...] += lax.dot_general(a_ref[...], b_ref[...], TN_DIMS, preferred_element_type=F32)

    return pl.pallas_call(
        body, name=name, grid=(m // tm, ktok // tk),
        in_specs=[pl.BlockSpec((tk, tm), lambda i, kk: (kk, i)), pl.BlockSpec((tk, n), lambda i, kk: (kk, 0))],
        out_specs=pl.BlockSpec((tm, n), lambda i, kk: (i, 0)),
        out_shape=jax.ShapeDtypeStruct((m, n), F32),
        compiler_params=_params("parallel", "arbitrary"))(a, b)


def _segment_chunks(segs):
    bounds = [0]
    for s in segs:
        bounds.append(bounds[-1] + s.shape[1] // SEG_CHUNK)
    return bounds


def _input_grad(segs, wt, x, g, dy, name, tm=512, ride=None):
    m = segs[0].shape[0]
    rows, n = wt.shape
    tm = min(tm, m)
    bounds = _segment_chunks(segs)
    n_seg = len(segs)
    ni = m // tm
    host = _Hosted(ride, n_seg + 4, 2, 2)

    def body(*refs):
        host_refs, start, finish = host.split(refs)
        a_refs = host_refs[:n_seg]
        wt_hbm, x_ref, g_ref, dy_ref, gx_ref, st_ref, wt_vmem, sems = host_refs[n_seg:]
        i = pl.program_id(0)
        start(i == 0)

        @pl.when(i == 0)
        def _():
            st_ref[...] = jnp.zeros_like(st_ref)

        ready = _load_resident(wt_hbm, wt_vmem, sems, i == 0)
        dh = None
        for s in range(n_seg):
            for c in range(bounds[s], bounds[s + 1]):
                ready(c)
            part = jnp.dot(a_refs[s][...], wt_vmem[bounds[s] * SEG_CHUNK:bounds[s + 1] * SEG_CHUNK, :], preferred_element_type=F32)
            dh = part if dh is None else dh + part
        xv = x_ref[...]
        r = lax.rsqrt(jnp.mean(xv * xv, axis=-1, keepdims=True) + EPS)
        xn = xv * r
        dxn = dh * g_ref[...]
        gx_ref[...] = dy_ref[...] + r * (dxn - xn * jnp.mean(dxn * xn, axis=-1, keepdims=True))
        st_ref[...] += _rows8([jnp.sum(dh * xn, axis=0, keepdims=True)], n)
        finish(i == ni - 1)

    tile = pl.BlockSpec((tm, n), lambda i: (i, 0))
    outs = pl.pallas_call(
        body, name=name, grid=(ni,),
        in_specs=[pl.BlockSpec((tm, sg.shape[1]), lambda i: (i, 0)) for sg in segs]
        + [ANY, tile, pl.BlockSpec((1, n), lambda i: (0, 0)), tile] + host.in_specs,
        out_specs=[tile, pl.BlockSpec((8, n), lambda i: (0, 0))] + host.out_specs,
        out_shape=[jax.ShapeDtypeStruct((m, n), F32), jax.ShapeDtypeStruct((8, n), F32)] + host.out_shapes,
        scratch_shapes=[pltpu.VMEM((rows, n), wt.dtype), pltpu.SemaphoreType.DMA((rows // SEG_CHUNK,))] + host.scratch_shapes,
        compiler_params=_params("arbitrary"))(*segs, wt, x, g, dy, *host.arrays)
    res, landed = host.results(outs, 2)
    return (*res, landed) if ride else tuple(res)


def _mm_tn_seg(segs, b, name):
    ktok = segs[0].shape[0]
    n = b.shape[1]
    bounds = _segment_chunks(segs)
    n_seg = len(segs)
    nc = bounds[-1]
    seg_of = [s for s in range(n_seg) for _ in range(bounds[s], bounds[s + 1])]

    def body(*refs):
        a_hbm, b_hbm, o_ref = refs[:n_seg], refs[n_seg], refs[n_seg + 1]
        a_buf, b_vmem, a_sems, b_sem = refs[n_seg + 2:]
        c = pl.program_id(0)

        def fetch(cc):
            s = seg_of[cc]
            cols = pl.ds((cc - bounds[s]) * SEG_CHUNK, SEG_CHUNK)
            return pltpu.make_async_copy(a_hbm[s].at[:, cols], a_buf.at[cc % 2], a_sems.at[cc % 2])

        @pl.when(c == 0)
        def _():
            whole = pltpu.make_async_copy(b_hbm, b_vmem, b_sem)
            whole.start()
            fetch(0).start()
            whole.wait()

        for cc in range(nc):
            @pl.when(c == cc)
            def _(cc=cc):
                if cc + 1 < nc:
                    fetch(cc + 1).start()
                fetch(cc).wait()

        o_ref[...] = lax.dot_general(a_buf[c % 2], b_vmem[...], TN_DIMS, preferred_element_type=F32)

    return pl.pallas_call(
        body, name=name, grid=(nc,),
        in_specs=[ANY] * (n_seg + 1), out_specs=pl.BlockSpec((SEG_CHUNK, n), lambda c: (c, 0)),
        out_shape=jax.ShapeDtypeStruct((nc * SEG_CHUNK, n), F32),
        scratch_shapes=[pltpu.VMEM((2, ktok, SEG_CHUNK), segs[0].dtype), pltpu.VMEM((ktok, n), b.dtype),
                        pltpu.SemaphoreType.DMA((2,)), pltpu.SemaphoreType.DMA],
        compiler_params=_params("arbitrary"))(*segs, b)


def _branches_fwd(z_rnn, z_attn, ag_ml, b_gate, w_rnn, w_attn, w_out, x, target, g_post, name, tm=512):
    t, d = x.shape
    tm = min(tm, t)

    def body(zr_ref, za_ref, lr_ref, la_ref, br_ref, ba_ref, wr_ref, wa_ref, wo_ref, x_ref, t_ref, g_ref,
             brr_ref, bra_ref, mg_ref, do_ref, dy_ref, st_ref):
        @pl.when(pl.program_id(0) == 0)
        def _():
            st_ref[...] = jnp.zeros_like(st_ref)

        br_rnn = jnp.dot(zr_ref[...], wr_ref[...], preferred_element_type=F32)
        br_attn = jnp.dot(za_ref[...], wa_ref[...], preferred_element_type=F32)
        brr_ref[...] = br_rnn.astype(BF16)
        bra_ref[...] = br_attn.astype(BF16)
        g_rnn = _sigmoid(lr_ref[...] + br_ref[...])
        g_attn = _sigmoid(la_ref[...] + ba_ref[...])
        merged = (g_rnn * br_rnn + g_attn * br_attn).astype(BF16)
        mg_ref[...] = merged
        o = jnp.dot(merged, wo_ref[...], preferred_element_type=F32)
        g = g_ref[...]
        r = lax.rsqrt(jnp.mean(o * o, axis=-1, keepdims=True) + EPS)
        nrm = o * r
        err = x_ref[...] + nrm * g - t_ref[...]
        dy = err * (1.0 / d)
        dy_ref[...] = dy
        dn = dy * g
        do_ref[...] = (r * (dn - nrm * jnp.mean(dn * nrm, axis=-1, keepdims=True))).astype(BF16)
        st_ref[...] += _rows8([jnp.sum(dy * nrm, axis=0, keepdims=True), jnp.sum(err * err, axis=0, keepdims=True)], d)

    tile = pl.BlockSpec((tm, d), lambda i: (i, 0))
    weight = pl.BlockSpec((d, d), lambda i: (0, 0))
    bf = jax.ShapeDtypeStruct((t, d), BF16)
    return pl.pallas_call(
        body, name=name, grid=(t // tm,),
        in_specs=[tile, tile, pl.BlockSpec((tm, d), lambda i: (i, 1)), pl.BlockSpec((tm, d), lambda i: (i, 2)),
                  pl.BlockSpec((1, d), lambda i: (0, 0)), pl.BlockSpec((1, d), lambda i: (0, 1)),
                  weight, weight, weight, tile, tile, pl.BlockSpec((1, d), lambda i: (0, 0))],
        out_specs=[tile, tile, tile, tile, tile, pl.BlockSpec((8, d), lambda i: (0, 0))],
        out_shape=[bf, bf, bf, bf, jax.ShapeDtypeStruct((t, d), F32), jax.ShapeDtypeStruct((8, d), F32)],
        compiler_params=_params("arbitrary"))(z_rnn, z_attn, ag_ml, ag_ml, b_gate, b_gate, w_rnn, w_attn, w_out, x, target, g_post)


def _branches_bwd(dout, br_rnn, br_attn, ag_ml, b_gate, w_rnn, w_attn, w_out, name, tm=512):
    t, d = br_rnn.shape
    tm = min(tm, t)

    def body(do_ref, r_ref, a_ref, lr_ref, la_ref, br_ref, ba_ref, wr_ref, wa_ref, wo_ref,
             dr_ref, da_ref, dl_ref, dzr_ref, dza_ref, st_ref, wt_ref):
        @pl.when(pl.program_id(0) == 0)
        def _():
            st_ref[...] = jnp.zeros_like(st_ref)
            wt_ref[0] = wo_ref[...].T
            wt_ref[1] = wr_ref[...].T
            wt_ref[2] = wa_ref[...].T

        dm = jnp.dot(do_ref[...], wt_ref[0], preferred_element_type=F32)
        g_rnn = _sigmoid(lr_ref[...] + br_ref[...])
        g_attn = _sigmoid(la_ref[...] + ba_ref[...])
        dbr_rnn = (dm * g_rnn).astype(BF16)
        dbr_attn = (dm * g_attn).astype(BF16)
        dr_ref[...] = dbr_rnn
        da_ref[...] = dbr_attn
        dl_rnn = dm * r_ref[...].astype(F32) * g_rnn * (1.0 - g_rnn)
        dl_attn = dm * a_ref[...].astype(F32) * g_attn * (1.0 - g_attn)
        dl_ref[:, 0:d] = dl_rnn.astype(BF16)
        dl_ref[:, d:2 * d] = dl_attn.astype(BF16)
        st_ref[...] += _rows8([jnp.sum(dl_rnn, axis=0, keepdims=True), jnp.sum(dl_attn, axis=0, keepdims=True)], d)
        dzr_ref[...] = jnp.dot(dbr_rnn, wt_ref[1], preferred_element_type=F32)
        dza_ref[...] = jnp.dot(dbr_attn, wt_ref[2], preferred_element_type=F32)

    tile = pl.BlockSpec((tm, d), lambda i: (i, 0))
    weight = pl.BlockSpec((d, d), lambda i: (0, 0))
    bf = jax.ShapeDtypeStruct((t, d), BF16)
    return pl.pallas_call(
        body, name=name, grid=(t // tm,),
        in_specs=[tile, tile, tile, pl.BlockSpec((tm, d), lambda i: (i, 1)), pl.BlockSpec((tm, d), lambda i: (i, 2)),
                  pl.BlockSpec((1, d), lambda i: (0, 0)), pl.BlockSpec((1, d), lambda i: (0, 1)), weight, weight, weight],
        out_specs=[tile, tile, pl.BlockSpec((tm, 2 * d), lambda i: (i, 0)), tile, tile, pl.BlockSpec((8, d), lambda i: (0, 0))],
        out_shape=[bf, bf, jax.ShapeDtypeStruct((t, 2 * d), BF16), jax.ShapeDtypeStruct((t, d), F32),
                   jax.ShapeDtypeStruct((t, d), F32), jax.ShapeDtypeStruct((8, d), F32)],
        scratch_shapes=[pltpu.VMEM((3, d, d), BF16)],
        compiler_params=_params("arbitrary"))(dout, br_rnn, br_attn, ag_ml, ag_ml, b_gate, b_gate, w_rnn, w_attn, w_out)


def _lru_gates(c, wa, ba, wx, bx, sp):
    cb = c.astype(BF16)
    r = _sigmoid(jnp.dot(cb, wa, preferred_element_type=F32) + ba)
    ig = _sigmoid(jnp.dot(cb, wx, preferred_element_type=F32) + bx)
    log_a = (-LRU_C) * r * sp
    a = jnp.exp(log_a)
    mult = jnp.sqrt(-jnp.tanh(log_a) * (a * a + 1.0))
    return cb, r, ig, a, mult


SUBLANES = 8


def _scan_fwd(a, u, carry, tt):
    w = a.shape[1]
    ng = tt // SUBLANES
    a3 = a.reshape(ng, SUBLANES, w)
    u3 = u.reshape(ng, SUBLANES, w)
    sub = lax.broadcasted_iota(jnp.int32, (ng, SUBLANES, w), 1)
    d = 1
    while d < SUBLANES:
        keep = sub >= d
        u3 = u3 + a3 * jnp.where(keep, pltpu.roll(u3, d, 1), 0.0)
        a3 = a3 * jnp.where(keep, pltpu.roll(a3, d, 1), 1.0)
        d *= 2
    out = []
    for g in range(ng):
        hg = u3[g] + a3[g] * carry
        out.append(hg)
        carry = hg[SUBLANES - 1:SUBLANES, :]
    return jnp.concatenate(out, axis=0)


def _scan_rev(b, g, carry, tt):
    w = b.shape[1]
    ng = tt // SUBLANES
    b3 = b.reshape(ng, SUBLANES, w)
    g3 = g.reshape(ng, SUBLANES, w)
    sub = lax.broadcasted_iota(jnp.int32, (ng, SUBLANES, w), 1)
    d = 1
    while d < SUBLANES:
        keep = sub < SUBLANES - d
        g3 = g3 + b3 * jnp.where(keep, pltpu.roll(g3, SUBLANES - d, 1), 0.0)
        b3 = b3 * jnp.where(keep, pltpu.roll(b3, SUBLANES - d, 1), 1.0)
        d *= 2
    out = [None] * ng
    for k in range(ng - 1, -1, -1):
        hk = g3[k] + b3[k] * carry
        out[k] = hk
        carry = hk[0:1, :]
    return jnp.concatenate(out, axis=0)


def _conv_taps(cw, bias, x, ext_ref, tt):
    x2 = ext_ref[7:7 + tt, :]
    x1 = ext_ref[6:6 + tt, :]
    x0 = ext_ref[5:5 + tt, :]
    c = bias + cw[3:4] * x + cw[2:3] * x2 + cw[1:2] * x1 + cw[0:1] * x0
    return c, x2, x1, x0


def _rnn_fwd(rx_rg, cw, cb, wa, ba, wx, bx, lam, name, tt=512):
    t = rx_rg.shape[0]
    tt = min(tt, t)
    w = GROUP_W

    def body(rx_ref, rg_ref, cw_ref, cb_ref, wa_ref, ba_ref, wx_ref, bx_ref, lam_ref, y_ref, z_ref, ext_ref, hc_ref):
        @pl.when(pl.program_id(1) == 0)
        def _():
            ext_ref[0:8, :] = jnp.zeros((8, w), F32)
            hc_ref[...] = jnp.zeros((8, w), F32)

        x = rx_ref[...]
        ext_ref[8:8 + tt, :] = x
        c, _, _, _ = _conv_taps(cw_ref[...], cb_ref[...], x, ext_ref, tt)
        ext_ref[0:8, :] = x[tt - 8:tt, :]
        sp = _softplus(-lam_ref[...])
        _, _, ig, a, mult = _lru_gates(c, wa_ref[...], ba_ref[...], wx_ref[...], bx_ref[...], sp)
        h = _scan_fwd(a, mult * (ig * c), hc_ref[7:8, :], tt)
        hc_ref[...] = h[tt - 8:tt, :]
        y_ref[...] = h
        rg = rg_ref[...]
        z_ref[...] = (h * rg * _sigmoid(rg)).astype(BF16)

    vec = pl.BlockSpec((1, w), lambda g, i: (0, g))
    mat = pl.BlockSpec((None, w, w), lambda g, i: (g, 0, 0))
    tile = pl.BlockSpec((tt, w), lambda g, i: (i, g))
    return pl.pallas_call(
        body, name=name, grid=(N_GROUPS, t // tt),
        in_specs=[tile, pl.BlockSpec((tt, w), lambda g, i: (i, N_GROUPS + g)),
                  pl.BlockSpec((4, w), lambda g, i: (0, g)), vec, mat, vec, mat, vec, vec],
        out_specs=[tile, tile],
        out_shape=[jax.ShapeDtypeStruct((t, D_RNN), F32), jax.ShapeDtypeStruct((t, D_RNN), BF16)],
        scratch_shapes=[pltpu.VMEM((tt + 8, w), F32), pltpu.VMEM((8, w), F32)],
        compiler_params=_params("parallel", "arbitrary"))(rx_rg, rx_rg, cw, cb, wa, ba, wx, bx, lam)


def _rnn_bwd(rx_rg, y, dz, cw, cb, wa, ba, wx, bx, lam, name, tt=512, ride=None):
    t = rx_rg.shape[0]
    tt = min(tt, t)
    nt = t // tt
    w = GROUP_W

    host = _Hosted(ride, 13, 5, 6)

    def body(*refs):
        host_refs, start, finish = host.split(refs)
        (rx_ref, rg_ref, rxt_ref, y_ref, yt_ref, dz_ref, cw_ref, cb_ref, wa_ref, ba_ref, wx_ref, bx_ref, lam_ref,
         drx_ref, drg_ref, st_ref, gda_ref, gdx_ref, ext_ref, dcx_ref, wcar_ref, acar_ref, dwa_ref, dwx_ref) = host_refs
        ii = pl.program_id(1)
        start((pl.program_id(0) == 0) & (ii == 0))

        @pl.when(ii == 0)
        def _():
            wcar_ref[...] = jnp.zeros((8, w), F32)
            acar_ref[...] = jnp.zeros((8, w), F32)
            dcx_ref[tt:tt + 8, :] = jnp.zeros((8, w), F32)
            st_ref[...] = jnp.zeros_like(st_ref)
            dwa_ref[...] = jnp.zeros_like(dwa_ref)
            dwx_ref[...] = jnp.zeros_like(dwx_ref)

        has_prev = jnp.where(ii == nt - 1, 0.0, 1.0)
        x = rx_ref[...]
        ext_ref[0:8, :] = rxt_ref[...] * has_prev
        ext_ref[8:8 + tt, :] = x
        cwv = cw_ref[...]
        c, x2, x1, x0 = _conv_taps(cwv, cb_ref[...], x, ext_ref, tt)
        lam = lam_ref[...]
        sp = _softplus(-lam)
        wa = wa_ref[...]
        wx = wx_ref[...]
        cb16, r, ig, a, mult = _lru_gates(c, wa, ba_ref[...], wx, bx_ref[...], sp)

        rg = rg_ref[...]
        sg = _sigmoid(rg)
        dz = dz_ref[...]
        yv = y_ref[...]
        drg_ref[...] = (dz * yv * (sg * (1.0 + rg * (1.0 - sg)))).astype(BF16)

        row = lax.broadcasted_iota(jnp.int32, (tt, w), 0)
        b = jnp.where(row < tt - 1, pltpu.roll(a, tt - 1, 0), acar_ref[0:1, :])
        dh = _scan_rev(b, dz * (rg * sg), wcar_ref[0:1, :], tt)
        wcar_ref[...] = dh[0:8, :]
        acar_ref[...] = a[0:8, :]

        hprev = jnp.where(row >= 1, pltpu.roll(yv, 1, 0), yt_ref[7:8, :] * has_prev)
        dmult = dh * (ig * c)
        dig = dh * mult * c
        dlog_a = dh * hprev * a - dmult * (a * a / mult)
        dpa = dlog_a * ((-LRU_C) * sp) * r * (1.0 - r)
        dpx = dig * ig * (1.0 - ig)
        dsp = jnp.sum(dlog_a * r, axis=0, keepdims=True) * (-LRU_C)
        dlam = dsp * (-_sigmoid(-lam))
        dpa16 = dpa.astype(BF16)
        dpx16 = dpx.astype(BF16)
        dwa_ref[...] += lax.dot_general(cb16, dpa16, TN_DIMS, preferred_element_type=F32)
        dwx_ref[...] += lax.dot_general(cb16, dpx16, TN_DIMS, preferred_element_type=F32)
        dc = (dh * mult * ig
              + lax.dot_general(dpa16, wa, NT_DIMS, preferred_element_type=F32)
              + lax.dot_general(dpx16, wx, NT_DIMS, preferred_element_type=F32))

        dcx_ref[0:tt, :] = dc
        drx = (cwv[3:4] * dc + cwv[2:3] * dcx_ref[1:1 + tt, :] + cwv[1:2] * dcx_ref[2:2 + tt, :]
               + cwv[0:1] * dcx_ref[3:3 + tt, :])
        drx_ref[...] = drx.astype(BF16)
        dcx_ref[tt:tt + 8, :] = dc[0:8, :]

        def colsum(v):
            return jnp.sum(v, axis=0, keepdims=True)

        st_ref[...] += _rows8([colsum(dc), colsum(dpa), colsum(dpx), dlam,
                               colsum(dc * x0), colsum(dc * x1), colsum(dc * x2), colsum(dc * x)], w)

        @pl.when(ii == nt - 1)
        def _():
            for blk in range(GROUP_W // RNN_BLOCK_W):
                rows = slice(blk * RNN_BLOCK_W, (blk + 1) * RNN_BLOCK_W)
                gda_ref[blk] = dwa_ref[rows, rows]
                gdx_ref[blk] = dwx_ref[rows, rows]

        finish((pl.program_id(0) == N_GROUPS - 1) & (ii == nt - 1))

    def rev(ii):
        return nt - 1 - ii

    def tail(g, ii):
        return (jnp.maximum(rev(ii) * (tt // 8) - 1, 0), g)

    vec = pl.BlockSpec((1, w), lambda g, ii: (0, g))
    mat = pl.BlockSpec((None, w, w), lambda g, ii: (g, 0, 0))
    tile = pl.BlockSpec((tt, w), lambda g, ii: (rev(ii), g))
    diag_shape = (N_GROUPS, GROUP_W // RNN_BLOCK_W, RNN_BLOCK_W, RNN_BLOCK_W)
    diag = pl.BlockSpec((None,) + diag_shape[1:], lambda g, ii: (g, 0, 0, 0))
    outs = pl.pallas_call(
        body, name=name, grid=(N_GROUPS, nt),
        in_specs=[tile, pl.BlockSpec((tt, w), lambda g, ii: (rev(ii), N_GROUPS + g)), pl.BlockSpec((8, w), tail),
                  tile, pl.BlockSpec((8, w), tail), tile,
                  pl.BlockSpec((4, w), lambda g, ii: (0, g)), vec, mat, vec, mat, vec, vec] + host.in_specs,
        out_specs=[tile, tile, pl.BlockSpec((8, w), lambda g, ii: (0, g)), diag, diag] + host.out_specs,
        out_shape=[jax.ShapeDtypeStruct((t, D_RNN), BF16), jax.ShapeDtypeStruct((t, D_RNN), BF16),
                   jax.ShapeDtypeStruct((8, D_RNN), F32),
                   jax.ShapeDtypeStruct(diag_shape, F32), jax.ShapeDtypeStruct(diag_shape, F32)] + host.out_shapes,
        scratch_shapes=[pltpu.VMEM((tt + 8, w), F32), pltpu.VMEM((tt + 8, w), F32), pltpu.VMEM((8, w), F32),
                        pltpu.VMEM((8, w), F32), pltpu.VMEM((w, w), F32), pltpu.VMEM((w, w), F32)] + host.scratch_shapes,
        compiler_params=_params("arbitrary" if ride else "parallel", "arbitrary"))(
            rx_rg, rx_rg, rx_rg, y, y, dz, cw, cb, wa, ba, wx, bx, lam, *host.arrays)
    res, landed = host.results(outs, 5)
    return (*res, landed) if ride else tuple(res)


def _half_mask(shape, half):
    lane = lax.broadcasted_iota(jnp.int32, shape, 1)
    return (lane >= HEAD_DIM) if half else (lane < HEAD_DIM)


def _dup_half(t, half):
    sel = jnp.where(_half_mask(t.shape, half), t, 0.0)
    return sel + pltpu.roll(sel, HEAD_DIM, 1)


def _band_geometry(n):
    qi = lax.broadcasted_iota(jnp.int32, (BLOCK, 2 * BLOCK), 0)
    kj = lax.broadcasted_iota(jnp.int32, (BLOCK, 2 * BLOCK), 1)
    dist = BLOCK + qi - kj
    first_key = jnp.where(n > 0, 0, BLOCK)
    valid = (dist >= 0) & (dist < BLOCK) & (kj >= first_key)
    return dist.astype(F32), valid


GROUP = 4


def _kv_dup(prev_ref, cur_ref, hk, scale=1.0):
    tile = hk // 2
    kt = jnp.concatenate([prev_ref[:, tile * LANE:(tile + 1) * LANE], cur_ref[:, tile * LANE:(tile + 1) * LANE]], axis=0)
    return (_dup_half(kt.astype(F32), hk % 2) * scale).astype(BF16)


def _fill_bias(bias_ref, sm_ref, n):
    distf, valid = _band_geometry(n)
    for head in range(N_Q_HEADS):
        bias_ref[head] = jnp.where(valid, -sm_ref[1, head] * distf, MASKED)


def _head_scores(q2s, half, kdup, bias):
    qm = jnp.where(_half_mask(q2s.shape, half), q2s, jnp.zeros_like(q2s))
    return qm, lax.dot_general(qm, kdup, NT_DIMS, preferred_element_type=F32) + bias


def _attn_specs(nb, clamp_last):
    def blk(n):
        return jnp.minimum(n, nb - 1) if clamp_last else n

    q_spec = pl.BlockSpec((BLOCK, D_MODEL), lambda n: (blk(n), 0))
    k_prev = pl.BlockSpec((BLOCK, D_KV), lambda n: (jnp.maximum(blk(n) - 1, 0), D_MODEL // D_KV))
    k_cur = pl.BlockSpec((BLOCK, D_KV), lambda n: (blk(n), D_MODEL // D_KV))
    v_prev = pl.BlockSpec((BLOCK, D_KV), lambda n: (jnp.maximum(blk(n) - 1, 0), D_MODEL // D_KV + 1))
    v_cur = pl.BlockSpec((BLOCK, D_KV), lambda n: (blk(n), D_MODEL // D_KV + 1))
    return q_spec, k_prev, k_cur, v_prev, v_cur


def _attn_fwd(sm, qkv, ag_ml, name, ride=None):
    t = qkv.shape[0]
    nb = t // BLOCK

    host = _Hosted(ride, 7, 3, 1)

    def body(*refs):
        (sm_ref, q_ref, kp_ref, kc_ref, vp_ref, vc_ref, ag_ref, y_ref, z_ref, lse_ref, bias_ref), start, finish = host.split(refs)
        n = pl.program_id(0)
        start(n == 0)
        start.middle(n == (3 * nb) // 4)

        @pl.when(n <= 1)
        def _():
            _fill_bias(bias_ref, sm_ref, n)

        lane = lax.broadcasted_iota(jnp.int32, (BLOCK, LANE), 1)
        low = lane < HEAD_DIM
        lse = jnp.zeros((BLOCK, LANE), F32)
        for hk in range(N_Q_HEADS // GROUP):
            kdup = _kv_dup(kp_ref, kc_ref, hk)
            vdup = _kv_dup(vp_ref, vc_ref, hk)
            for k in (0, 1):
                c = slice((2 * hk + k) * LANE, (2 * hk + k + 1) * LANE)
                q2s = q_ref[:, c] * ATTN_SCALE
                outs = []
                for half in (0, 1):
                    head = GROUP * hk + 2 * k + half
                    _, s = _head_scores(q2s, half, kdup, bias_ref[head])
                    sink = sm_ref[0, head]
                    m = jnp.maximum(jnp.max(s, axis=1, keepdims=True), sink)
                    e = jnp.exp(s - m)
                    l = jnp.sum(e, axis=1, keepdims=True) + jnp.exp(sink - m)
                    outs.append(jnp.dot((e * (1.0 / l)).astype(BF16), vdup, preferred_element_type=F32))
                    lse = jnp.where(lane == head, m + jnp.log(l), lse)
                yt = jnp.where(low, outs[0], outs[1])
                y_ref[:, c] = yt
                ag = ag_ref[:, c]
                z_ref[:, c] = (yt * ag * _sigmoid(ag)).astype(BF16)
        lse_ref[...] = lse
        finish(n == nb - 1)

    q_spec, k_prev, k_cur, v_prev, v_cur = _attn_specs(nb, False)
    wide = pl.BlockSpec((BLOCK, D_MODEL), lambda n: (n, 0))
    outs = pl.pallas_call(
        body, name=name, grid=(nb,),
        in_specs=[pl.BlockSpec(memory_space=pltpu.SMEM), q_spec, k_prev, k_cur, v_prev, v_cur, wide] + host.in_specs,
        out_specs=[wide, wide, pl.BlockSpec((BLOCK, LANE), lambda n: (n, 0))] + host.out_specs,
        out_shape=[jax.ShapeDtypeStruct((t, D_MODEL), F32), jax.ShapeDtypeStruct((t, D_MODEL), BF16),
                   jax.ShapeDtypeStruct((t, LANE), F32)] + host.out_shapes,
        scratch_shapes=[pltpu.VMEM((N_Q_HEADS, BLOCK, 2 * BLOCK), F32)] + host.scratch_shapes,
        compiler_params=_params("arbitrary"))(sm, qkv, qkv, qkv, qkv, qkv, ag_ml, *host.arrays)
    res, landed = host.results(outs, 3)
    return (*res, landed) if ride else tuple(res)


def _attn_bwd(sm, qkv, ag_ml, y, lse, dz, name, ride=None):
    t = qkv.shape[0]
    nb = t // BLOCK
    host = _Hosted(ride, 10, 4, 3)

    def body(*refs):
        host_refs, start, finish = host.split(refs)
        (sm_ref, q_ref, kp_ref, kc_ref, vp_ref, vc_ref, ag_ref, y_ref, lse_ref, dz_ref,
         dq_ref, dkv_ref, dag_ref, ds_ref, ck_ref, cv_ref, bias_ref) = host_refs
        n = pl.program_id(0)
        start(n == 0)

        @pl.when(n == 0)
        def _():
            ck_ref[...] = jnp.zeros_like(ck_ref)
            cv_ref[...] = jnp.zeros_like(cv_ref)
            ds_ref[...] = jnp.zeros_like(ds_ref)

        @pl.when(n <= 1)
        def _():
            _fill_bias(bias_ref, sm_ref, n)

        @pl.when(n < nb)
        def _():
            lane8 = lax.broadcasted_iota(jnp.int32, (8, LANE), 1)
            row8 = lax.broadcasted_iota(jnp.int32, (8, LANE), 0)
            dsink = jnp.zeros((8, LANE), F32)
            dk_heads, dv_heads = [], []
            lse_tile = lse_ref[...]
            for hk in range(N_Q_HEADS // GROUP):
                kdup = _kv_dup(kp_ref, kc_ref, hk)
                ks = _kv_dup(kp_ref, kc_ref, hk, ATTN_SCALE)
                vdup = _kv_dup(vp_ref, vc_ref, hk)
                dk_acc = jnp.zeros((2 * BLOCK, LANE), F32)
                dv_acc = jnp.zeros((2 * BLOCK, LANE), F32)
                for k in (0, 1):
                    c = slice((2 * hk + k) * LANE, (2 * hk + k + 1) * LANE)
                    ag = ag_ref[:, c]
                    sg = _sigmoid(ag)
                    dzt = dz_ref[:, c]
                    yt = y_ref[:, c]
                    dag_ref[:, c] = (dzt * yt * (sg * (1.0 + ag * (1.0 - sg)))).astype(BF16)
                    dyt = dzt * (ag * sg)
                    q2s = q_ref[:, c] * ATTN_SCALE
                    dq_halves = []
                    for half in (0, 1):
                        head = GROUP * hk + 2 * k + half
                        qm, s = _head_scores(q2s, half, kdup, bias_ref[head])
                        lh = lse_tile[:, head:head + 1]
                        probs = jnp.exp(s - lh)
                        psink = jnp.exp(sm_ref[0, head] - lh)
                        dyh = jnp.where(_half_mask(dyt.shape, half), dyt, 0.0)
                        delta = jnp.sum(dyh * yt, axis=1, keepdims=True)
                        dyh16 = dyh.astype(BF16)
                        dp = lax.dot_general(dyh16, vdup, NT_DIMS, preferred_element_type=F32)
                        ds16 = (probs * (dp - delta)).astype(BF16)
                        dsink = dsink + jnp.where((row8 == 0) & (lane8 == head),
                                                  -jnp.sum(psink * delta, axis=0, keepdims=True), 0.0)
                        dq_halves.append(jnp.dot(ds16, ks, preferred_element_type=F32))
                        dk_acc = dk_acc + lax.dot_general(ds16, qm, TN_DIMS, preferred_element_type=F32)
                        dv_acc = dv_acc + lax.dot_general(probs.astype(BF16), dyh16, TN_DIMS, preferred_element_type=F32)
                    dq_ref[:, c] = jnp.where(_half_mask((BLOCK, LANE), 0), dq_halves[0], dq_halves[1]).astype(BF16)
                dk_heads.append(dk_acc + pltpu.roll(dk_acc, HEAD_DIM, 1))
                dv_heads.append(dv_acc + pltpu.roll(dv_acc, HEAD_DIM, 1))
            ds_ref[...] += dsink
            low = _half_mask((2 * BLOCK, LANE), 0)
            for tile in range(2):
                cols = slice(tile * LANE, (tile + 1) * LANE)
                dkt = jnp.where(low, dk_heads[2 * tile], dk_heads[2 * tile + 1])
                dvt = jnp.where(low, dv_heads[2 * tile], dv_heads[2 * tile + 1])
                dkv_ref[:, cols] = (ck_ref[:, cols] + dkt[0:BLOCK, :]).astype(BF16)
                dkv_ref[:, D_KV + tile * LANE:D_KV + (tile + 1) * LANE] = (cv_ref[:, cols] + dvt[0:BLOCK, :]).astype(BF16)
                ck_ref[:, cols] = dkt[BLOCK:2 * BLOCK, :]
                cv_ref[:, cols] = dvt[BLOCK:2 * BLOCK, :]

        @pl.when(n == nb)
        def _():
            dkv_ref[:, 0:D_KV] = ck_ref[...].astype(BF16)
            dkv_ref[:, D_KV:2 * D_KV] = cv_ref[...].astype(BF16)

        finish(n == nb)

    q_spec, k_prev, k_cur, v_prev, v_cur = _attn_specs(nb, True)
    wide = pl.BlockSpec((BLOCK, D_MODEL), lambda n: (jnp.minimum(n, nb - 1), 0))
    outs = pl.pallas_call(
        body, name=name, grid=(nb + 1,),
        in_specs=[pl.BlockSpec(memory_space=pltpu.SMEM), q_spec, k_prev, k_cur, v_prev, v_cur, wide, wide,
                  pl.BlockSpec((BLOCK, LANE), lambda n: (jnp.minimum(n, nb - 1), 0)), wide] + host.in_specs,
        out_specs=[wide, pl.BlockSpec((BLOCK, 2 * D_KV), lambda n: (jnp.maximum(n - 1, 0), 0)), wide,
                   pl.BlockSpec((8, LANE), lambda n: (0, 0))] + host.out_specs,
        out_shape=[jax.ShapeDtypeStruct((t, D_MODEL), BF16), jax.ShapeDtypeStruct((t, 2 * D_KV), BF16),
                   jax.ShapeDtypeStruct((t, D_MODEL), BF16), jax.ShapeDtypeStruct((8, LANE), F32)] + host.out_shapes,
        scratch_shapes=[pltpu.VMEM((BLOCK, D_KV), F32), pltpu.VMEM((BLOCK, D_KV), F32),
                        pltpu.VMEM((N_Q_HEADS, BLOCK, 2 * BLOCK), F32)] + host.scratch_shapes,
        compiler_params=_params("arbitrary"))(sm, qkv, qkv, qkv, qkv, qkv, ag_ml, y, lse, dz, *host.arrays)
    res, landed = host.results(outs, 4)
    return (*res, landed) if ride else tuple(res)


def _local_grads(x, target, p, project, late_weights=None, reduce_out=None, reduce_in=None):
    h, rx_rg, qkv, ag_ml, wt = project(x, p["pre_g"])
    if late_weights is None:
        y_attn, z_attn, lse = _attn_fwd(p["sm"], qkv, ag_ml, "attn_fwd")
    else:
        y_attn, z_attn, lse, landed = _attn_fwd(p["sm"], qkv, ag_ml, "attn_fwd", ride=late_weights[0])
        p = {**p, **late_weights[1](landed)}
    lru = (p["cw"], p["cb"], p["wbd_a"], p["b_a"], p["wbd_x"], p["b_x"], p["lam"])
    y_rnn, z_rnn = _rnn_fwd(rx_rg, *lru, "rnn_fwd")
    br_rnn, br_attn, merged, dout, dy, st_post = _branches_fwd(
        z_rnn, z_attn, ag_ml, p["b_gate"], p["w_rnn"], p["w_attn"], p["w_out"], x, target, p["post_g"], "branches_fwd")

    dbr_rnn, dbr_attn, d_ml, dz_rnn, dz_attn, st_merge = _branches_bwd(
        dout, br_rnn, br_attn, ag_ml, p["b_gate"], p["w_rnn"], p["w_attn"], p["w_out"], "branches_bwd")
    gw_out = _mm_tn(merged, dout, "gw_out")
    gw_rnn = _mm_tn(z_rnn, dbr_rnn, "gw_rnn")
    gw_attn = _mm_tn(z_attn, dbr_attn, "gw_attn")
    red_out = red_in = None
    if reduce_out is None:
        d_rx, d_rg, st_rnn, g_rg_a, g_rg_x = _rnn_bwd(rx_rg, y_rnn, dz_rnn, *lru, "rnn_bwd")
        dq, dkv, d_ag, st_sink = _attn_bwd(p["sm"], qkv, ag_ml, y_attn, lse, dz_attn, "attn_bwd")
    else:
        d_rx, d_rg, st_rnn, g_rg_a, g_rg_x, from_sibling = _rnn_bwd(
            rx_rg, y_rnn, dz_rnn, *lru, "rnn_bwd", ride=reduce_out[0](gw_rnn, gw_attn, gw_out))
        dq, dkv, d_ag, st_sink, red_out = _attn_bwd(p["sm"], qkv, ag_ml, y_attn, lse, dz_attn, "attn_bwd",
                                                    ride=reduce_out[1](from_sibling, g_rg_a, g_rg_x))

    segs = [d_rx, d_rg, dq, dkv, d_ag, d_ml]
    gwt = _mm_tn_seg(segs, h, "gw_in")
    if reduce_in is None:
        grad_x, st_pre = _input_grad(segs, wt, x, p["pre_g"], dy, "input_grad")
    else:
        grad_x, st_pre, red_in = _input_grad(segs, wt, x, p["pre_g"], dy, "input_grad", ride=reduce_in(gwt))
    return dict(grad_x=grad_x, gwt=gwt, gw_rnn=gw_rnn, gw_attn=gw_attn, gw_out=gw_out,
                st_post=st_post, st_merge=st_merge, st_rnn=st_rnn, st_sink=st_sink, st_pre=st_pre,
                g_rg_a=g_rg_a, g_rg_x=g_rg_x, red_out=red_out, red_in=red_in)


def _place():
    x, y, c = lax.axis_index("x"), lax.axis_index("y"), lax.axis_index("c")
    return x, y, c


def _gather_ride(shards):
    n = len(shards)

    def copies(ins, outs, sems):
        send_sems, recv_sems, local_sems = sems
        x, y, c = _place()
        me, sibling = (x, y, c), (x, y, 1 - c)
        chips = [(1 - x, y), (x, 1 - y), (1 - x, 1 - y)]

        def slot(a, dev):
            return outs[a].at[4 * dev[0] + 2 * dev[1] + dev[2]]

        def copy(a, k, block, to, src=None):
            return pltpu.make_async_remote_copy(
                src_ref=slot(a, block) if src is None else src, dst_ref=slot(a, block),
                send_sem=send_sems.at[a, k], recv_sem=recv_sems.at[a, k], device_id=to, device_id_type=MESH)

        mine = [pltpu.make_async_copy(ins[a], slot(a, me), local_sems.at[a]) for a in range(n)]
        first = []
        for a in range(n):
            first.append(copy(a, 0, me, sibling, src=ins[a]))
            first += [copy(a, 1 + j, me, (*chip, c), src=ins[a]) for j, chip in enumerate(chips)]
        return me, sibling, chips, c, copy, mine, first

    def start(ins, outs, sems):
        *_, mine, first = copies(ins, outs, sems)
        for cp in mine + first:
            cp.start()

    def middle(ins, outs, sems):
        me, sibling, chips, c, copy, _, _ = copies(ins, outs, sems)
        for j, chip in enumerate(chips):
            for a in range(n):
                copy(a, 1 + j, (*chip, c), me).wait_recv()
                copy(a, 4 + j, (*chip, c), sibling).start()

    def finish(ins, outs, sems):
        me, sibling, chips, c, copy, mine, first = copies(ins, outs, sems)
        passed = [copy(a, 4 + j, (*chip, c), sibling) for j, chip in enumerate(chips) for a in range(n)]
        for a in range(n):
            copy(a, 0, sibling, me).wait_recv()
            for j, chip in enumerate(chips):
                copy(a, 4 + j, (*chip, 1 - c), me).wait_recv()
        for cp in first + passed:
            cp.wait_send()
        for cp in mine:
            cp.wait()

    return _Ride(
        shards, [jax.ShapeDtypeStruct((N_DEV, *s.shape), s.dtype) for s in shards],
        [pltpu.SemaphoreType.DMA((n, 7)), pltpu.SemaphoreType.DMA((n, 7)), pltpu.SemaphoreType.DMA((n,))],
        start, finish, middle)


def _sibling_ride(scatter, whole):
    ns, nw = len(scatter), len(whole)

    def copies(ins, outs, sems):
        send_sems, recv_sems = sems
        x, y, c = _place()
        cps = [pltpu.make_async_remote_copy(
            src_ref=ins[a].at[2 * chip + (1 - c)], dst_ref=outs[a].at[chip],
            send_sem=send_sems.at[a * N_CHIP + chip], recv_sem=recv_sems.at[a * N_CHIP + chip],
            device_id=(x, y, 1 - c), device_id_type=MESH) for a in range(ns) for chip in range(N_CHIP)]
        cps += [pltpu.make_async_remote_copy(
            src_ref=ins[ns + a], dst_ref=outs[ns + a],
            send_sem=send_sems.at[ns * N_CHIP + a], recv_sem=recv_sems.at[ns * N_CHIP + a],
            device_id=(x, y, 1 - c), device_id_type=MESH) for a in range(nw)]
        return cps

    def start(ins, outs, sems):
        for cp in copies(ins, outs, sems):
            cp.start()

    def finish(ins, outs, sems):
        for cp in copies(ins, outs, sems):
            cp.wait()

    n_sem = ns * N_CHIP + nw
    return _Ride(
        list(scatter) + list(whole),
        [jax.ShapeDtypeStruct((N_CHIP, *s.shape[1:]), s.dtype) for s in scatter]
        + [jax.ShapeDtypeStruct(s.shape, s.dtype) for s in whole],
        [pltpu.SemaphoreType.DMA((n_sem,)), pltpu.SemaphoreType.DMA((n_sem,))], start, finish)


def _chips_ride(scatter, whole):
    ns, nw = len(scatter), len(whole)
    n = ns + nw

    def copies(ins, outs, sems):
        send_sems, recv_sems, local_sems = sems
        x, y, c = _place()
        own = 2 * x + y
        chips = [(1 - x, y), (x, 1 - y), (1 - x, 1 - y)]

        def src(a, chip_idx):
            return ins[a].at[chip_idx] if a < ns else ins[a]

        local = [pltpu.make_async_copy(src(a, own), outs[a].at[own], local_sems.at[a]) for a in range(n)]
        sent = [pltpu.make_async_remote_copy(
            src_ref=src(a, 2 * chip[0] + chip[1]), dst_ref=outs[a].at[own],
            send_sem=send_sems.at[a, j], recv_sem=recv_sems.at[a, own], device_id=(*chip, c), device_id_type=MESH)
            for a in range(n) for j, chip in enumerate(chips)]
        return chips, c, local, sent

    def start(ins, outs, sems):
        _, _, local, sent = copies(ins, outs, sems)
        for cp in local + sent:
            cp.start()

    def finish(ins, outs, sems):
        send_sems, recv_sems, _ = sems
        chips, c, local, sent = copies(ins, outs, sems)
        for a in range(n):
            for chip in chips:
                k = 2 * chip[0] + chip[1]
                pltpu.make_async_remote_copy(
                    src_ref=outs[a].at[k], dst_ref=outs[a].at[k], send_sem=send_sems.at[a, 0],
                    recv_sem=recv_sems.at[a, k], device_id=(*chip, c), device_id_type=MESH).wait_recv()
        for cp in sent:
            cp.wait_send()
        for cp in local:
            cp.wait()

    return _Ride(
        list(scatter) + list(whole),
        [jax.ShapeDtypeStruct(s.shape, s.dtype) for s in scatter]
        + [jax.ShapeDtypeStruct((N_CHIP, *s.shape), s.dtype) for s in whole],
        [pltpu.SemaphoreType.DMA((n, 3)), pltpu.SemaphoreType.DMA((n, N_CHIP)), pltpu.SemaphoreType.DMA((n,))],
        start, finish)


def _pair_sum_scatter(parts, recvs, core, name):
    na = len(parts)
    _, r, cdim = parts[0].shape
    tr = min(r, 416 if r % 416 == 0 else 128)

    def body(core_ref, *refs):
        del core_ref
        for a in range(na):
            refs[2 * na + a][...] = (refs[a][...] + refs[na + a][...]).astype(BF16)

    blk = (None, tr, cdim)
    mine = pl.BlockSpec(blk, lambda k, i, core_ref: (2 * k + core_ref[0], i, 0))
    slot = pl.BlockSpec(blk, lambda k, i, core_ref: (k, i, 0))
    return pl.pallas_call(
        body, name=name,
        grid_spec=pltpu.PrefetchScalarGridSpec(
            num_scalar_prefetch=1, grid=(N_CHIP, r // tr),
            in_specs=[mine] * na + [slot] * na, out_specs=[slot] * na),
        out_shape=[jax.ShapeDtypeStruct((N_CHIP, r, cdim), BF16)] * na,
        compiler_params=_params("parallel", "parallel"))(core, *parts, *recvs)


def _pair_sum_whole(mine, recvs, name):
    na = len(mine)

    def body(*refs):
        for a in range(na):
            refs[2 * na + a][...] = refs[a][...] + refs[na + a][...]

    return pl.pallas_call(body, name=name, out_shape=[jax.ShapeDtypeStruct(m.shape, F32) for m in mine])(*mine, *recvs)


def _allreduce_small(pack, name):
    shape = pack.shape

    def body(x_ref, o_ref, sib_ref, chip_ref, send_sems, recv_sems):
        x, y, c = _place()
        own = 2 * x + y
        chips = [(1 - x, y), (x, 1 - y), (1 - x, 1 - y)]
        to_sibling = pltpu.make_async_remote_copy(
            src_ref=x_ref, dst_ref=sib_ref, send_sem=send_sems.at[0], recv_sem=recv_sems.at[0],
            device_id=(x, y, 1 - c), device_id_type=MESH)
        to_sibling.start()
        to_sibling.wait()
        chip_ref[own] = x_ref[...] + sib_ref[...]
        sent = [pltpu.make_async_remote_copy(
            src_ref=chip_ref.at[own], dst_ref=chip_ref.at[own], send_sem=send_sems.at[1 + j],
            recv_sem=recv_sems.at[1 + own], device_id=(*chip, c), device_id_type=MESH) for j, chip in enumerate(chips)]
        for cp in sent:
            cp.start()
        for chip in chips:
            k = 2 * chip[0] + chip[1]
            pltpu.make_async_remote_copy(
                src_ref=chip_ref.at[k], dst_ref=chip_ref.at[k], send_sem=send_sems.at[1],
                recv_sem=recv_sems.at[1 + k], device_id=(*chip, c), device_id_type=MESH).wait_recv()
        for cp in sent:
            cp.wait_send()
        o_ref[...] = (chip_ref[0] + chip_ref[1]) + (chip_ref[2] + chip_ref[3])

    return pl.pallas_call(
        body, name=name, out_shape=jax.ShapeDtypeStruct(shape, F32),
        in_specs=[pl.BlockSpec(memory_space=pltpu.VMEM)], out_specs=pl.BlockSpec(memory_space=pltpu.VMEM),
        scratch_shapes=[pltpu.VMEM(shape, F32), pltpu.VMEM((N_CHIP, *shape), F32),
                        pltpu.SemaphoreType.DMA((4,)), pltpu.SemaphoreType.DMA((1 + N_CHIP,))],
    )(pack)


def _adamw(g, w, m, v):
    m = ADAM_B1 * m + (1.0 - ADAM_B1) * g
    v = ADAM_B2 * v + (1.0 - ADAM_B2) * (g * g)
    m_hat = m / (1.0 - ADAM_B1 ** ADAM_STEP)
    v_hat = v / (1.0 - ADAM_B2 ** ADAM_STEP)
    delta = -ADAM_LR * (m_hat / (jnp.sqrt(v_hat) + ADAM_EPS) + ADAM_WD * w)
    return delta, m, v


def _adam_parts(parts, w, m, v, name, tr=None):
    npart, r, c = parts.shape
    tr = r if tr is None else min(tr, r)

    def body(p_ref, w_ref, m_ref, v_ref, g_ref, d_ref, nm_ref, nv_ref):
        g = p_ref[0].astype(F32)
        for k in range(1, npart):
            g = g + p_ref[k].astype(F32)
        g_ref[...] = g
        d_ref[...], nm_ref[...], nv_ref[...] = _adamw(g, w_ref[...], m_ref[...], v_ref[...])

    tile = pl.BlockSpec((tr, c), lambda i: (i, 0))
    return pl.pallas_call(
        body, name=name, grid=(r // tr,),
        in_specs=[pl.BlockSpec((npart, tr, c), lambda i: (0, i, 0)), tile, tile, tile],
        out_specs=[tile] * 4, out_shape=[jax.ShapeDtypeStruct((r, c), F32)] * 4,
        compiler_params=_params("parallel"))(parts, w, m, v)


def _block_diag(w):
    w4 = w.reshape(N_GROUPS, 4, RNN_BLOCK_W, RNN_BLOCK_W)
    eye = jnp.eye(4, dtype=w.dtype)
    return jnp.einsum("gbij,bc->gbicj", w4, eye).reshape(N_GROUPS, GROUP_W, GROUP_W).astype(BF16)


SMALL_ROWS = 16
ROW_PRE_G, ROW_BGATE, ROW_CONV_B, ROW_B_A, ROW_B_X, ROW_LAM, ROW_POST_G, ROW_LOSS, ROW_SINKS, ROW_CONV_W = 0, 1, 3, 4, 5, 6, 7, 8, 9, 10


def _pack_stats(st_pre, st_merge, st_rnn, st_post, st_sink, name):
    d = D_MODEL

    def body(pre_ref, mg_ref, rnn_ref, post_ref, sink_ref, o_ref):
        rnn = rnn_ref[...]
        sinks = jnp.concatenate([sink_ref[0:1, :], jnp.zeros((1, d - LANE), F32)], axis=1)
        o_ref[0:8, :] = _rows8([pre_ref[0:1, :], mg_ref[0:1, :], mg_ref[1:2, :], rnn[0:1], rnn[1:2], rnn[2:3], rnn[3:4],
                                post_ref[0:1, :]], d)
        o_ref[8:16, :] = _rows8([post_ref[1:2, :], sinks, rnn[4:5], rnn[5:6], rnn[6:7], rnn[7:8]], d)

    return pl.pallas_call(body, name=name, out_shape=jax.ShapeDtypeStruct((SMALL_ROWS, d), F32))(
        st_pre, st_merge, st_rnn, st_post, st_sink)


def _adam_small(total, w, m, v, name):
    d = D_MODEL
    n_in = len(w)

    def pack(refs):
        pre, bg, cbias, ba, bx, lam, post, sinks = [r[...] for r in refs]
        top = _rows8([pre, bg[:, 0:d], bg[:, d:2 * d], cbias, ba, bx, lam, post], d)
        return jnp.concatenate([top, _rows8([jnp.zeros((1, d), F32), sinks], d)], axis=0)

    def body(*refs):
        p_ref = refs[0]
        w_refs, m_refs, v_refs = (refs[1 + k * n_in:1 + (k + 1) * n_in] for k in range(3))
        outs = refs[1 + 3 * n_in:]
        g = p_ref[...]
        res = (g,) + _adamw(g, pack(w_refs), pack(m_refs), pack(v_refs))
        for k in range(4):
            outs[k][...] = res[k]
            outs[4 + k][...] = jnp.concatenate([res[k][ROW_BGATE:ROW_BGATE + 1], res[k][ROW_BGATE + 1:ROW_BGATE + 2]], axis=1)

    return pl.pallas_call(
        body, name=name,
        out_shape=[jax.ShapeDtypeStruct((SMALL_ROWS, d), F32)] * 4 + [jax.ShapeDtypeStruct((1, 2 * d), F32)] * 4,
    )(total, *w, *m, *v)


def kernel(x, pre_norm_g, w_in, b_gate, conv_w, conv_b, w_rg_a, b_rg_a, w_rg_x, b_rg_x, lru_lambda, attn_sinks, w_rnn_out, w_attn_out, w_out, post_norm_g, loss_target, m_pre_norm_g, m_w_in, m_b_gate, m_conv_w, m_conv_b, m_w_rg_a, m_b_rg_a, m_w_rg_x, m_b_rg_x, m_lru_lambda, m_attn_sinks, m_w_rnn_out, m_w_attn_out, m_w_out, m_post_norm_g, v_pre_norm_g, v_w_in, v_b_gate, v_conv_w, v_conv_b, v_w_rg_a, v_b_rg_a, v_w_rg_x, v_b_rg_x, v_lru_lambda, v_attn_sinks, v_w_rnn_out, v_w_attn_out, v_w_out, v_post_norm_g):
    cx, cy, cc = _place()
    dev = 4 * cx + 2 * cy + cc
    core = jnp.reshape(cc, (1,)).astype(jnp.int32)

    w_in_t, m_in_t, v_in_t = (jnp.transpose(a[0]) for a in (w_in, m_w_in, v_w_in))
    wt_shard = w_in_t.astype(BF16)

    def project(xs, pre_g):
        h, rx_rg, qkv, ag_ml, wt_all = _gather_project(xs, pre_g, wt_shard, "gather_project")
        return h, rx_rg, qkv, ag_ml, wt_all.reshape(D_IN, D_MODEL)

    heads = jnp.arange(1, N_Q_HEADS + 1, dtype=F32)
    slopes = jnp.exp2(-ALIBI_MAX_BIAS * heads / N_Q_HEADS)
    b_a = b_rg_a.reshape(1, D_RNN)
    b_x = b_rg_x.reshape(1, D_RNN)
    p = dict(
        pre_g=pre_norm_g, post_g=post_norm_g, b_gate=b_gate, cb=conv_b,
        wbd_a=_block_diag(w_rg_a[0]), b_a=b_a, wbd_x=_block_diag(w_rg_x[0]), b_x=b_x, lam=lru_lambda,
        sm=jnp.pad(attn_sinks, ((0, 1), (0, 0))) + jnp.pad(slopes[None, :], ((1, 0), (0, 0))))

    def late_unpack(landed):
        w_rnn_all, w_attn_all, w_out_all, cw_all = landed
        return dict(w_rnn=w_rnn_all.reshape(D_RNN, D_MODEL), w_attn=w_attn_all.reshape(D_MODEL, D_MODEL),
                    w_out=w_out_all.reshape(D_MODEL, D_MODEL), cw=jnp.transpose(cw_all, (1, 0, 2)).reshape(4, D_RNN))

    late_weights = (_gather_ride([w_rnn_out[0].astype(BF16), w_attn_out[0].astype(BF16), w_out[0].astype(BF16), conv_w[0]]),
                    late_unpack)
    flat = (RNN_BLOCKS * RNN_BLOCK_W, RNN_BLOCK_W)

    out_scatter = []

    def out_sibling(gw_rnn, gw_attn, gw_out):
        out_scatter.extend(gw.reshape(N_DEV, SHARD_OUT, D_MODEL) for gw in (gw_rnn, gw_attn, gw_out))
        return _sibling_ride(out_scatter, [])

    def out_chips(from_sibling, g_rg_a, g_rg_x):
        whole = [g_rg_a.reshape(flat), g_rg_x.reshape(flat)]
        rg_sibling = _run_ride(_sibling_ride([], whole), "sibling_rg")
        return _chips_ride(_pair_sum_scatter(out_scatter, from_sibling, core, "pair_out"),
                           _pair_sum_whole(whole, rg_sibling, "pair_rg"))

    def reduce_in(gwt):
        scatter = [gwt.reshape(N_DEV, SHARD_IN, D_MODEL)]
        from_sibling = _run_ride(_sibling_ride(scatter, []), "sibling_in")
        return _chips_ride(_pair_sum_scatter(scatter, from_sibling, core, "pair_in"), [])

    g = _local_grads(x[0], loss_target[0], p, project, late_weights, (out_sibling, out_chips), reduce_in)
    small = _allreduce_small(
        _pack_stats(g["st_pre"], g["st_merge"], g["st_rnn"], g["st_post"], g["st_sink"], "pack_stats"), "allreduce_small")

    out = {}
    red = g["red_out"]
    out["w_in"] = [jnp.transpose(o) for o in _adam_parts(g["red_in"][0], w_in_t, m_in_t, v_in_t, "adam_w_in", tr=SHARD_IN // 2)]
    out["w_rnn_out"] = _adam_parts(red[0], w_rnn_out[0], m_w_rnn_out[0], v_w_rnn_out[0], "adam_w_rnn_out")
    out["w_attn_out"] = _adam_parts(red[1], w_attn_out[0], m_w_attn_out[0], v_w_attn_out[0], "adam_w_attn_out")
    out["w_out"] = _adam_parts(red[2], w_out[0], m_w_out[0], v_w_out[0], "adam_w_out")
    out["w_rg_a"] = _adam_parts(red[3], w_rg_a.reshape(flat), m_w_rg_a.reshape(flat), v_w_rg_a.reshape(flat), "adam_w_rg_a", tr=256)
    out["w_rg_x"] = _adam_parts(red[4], w_rg_x.reshape(flat), m_w_rg_x.reshape(flat), v_w_rg_x.reshape(flat), "adam_w_rg_x", tr=256)

    def rows(pre, bg, cbias, ba, bx, lam, post, sinks):
        return (pre, bg, cbias, ba.reshape(1, D_RNN), bx.reshape(1, D_RNN), lam, post,
                jnp.pad(sinks, ((0, 0), (0, D_MODEL - N_Q_HEADS))))

    small_out = _adam_small(
        small,
        rows(pre_norm_g, b_gate, conv_b, b_rg_a, b_rg_x, lru_lambda, post_norm_g, attn_sinks),
        rows(m_pre_norm_g, m_b_gate, m_conv_b, m_b_rg_a, m_b_rg_x, m_lru_lambda, m_post_norm_g, m_attn_sinks),
        rows(v_pre_norm_g, v_b_gate, v_conv_b, v_b_rg_a, v_b_rg_x, v_lru_lambda, v_post_norm_g, v_attn_sinks),
        "adam_small")
    packed, bgate_out = small_out[:4], small_out[4:]
    g_cw = lax.dynamic_slice(packed[0][ROW_CONV_W:ROW_CONV_W + 4], (0, dev * SHARD_OUT), (4, SHARD_OUT))
    out["conv_w"] = _adam_parts(g_cw[None], conv_w[0], m_conv_w[0], v_conv_w[0], "adam_conv_w")

    def unpack(kind, name):
        if name == "b_gate":
            return bgate_out[kind]
        row = dict(pre_norm_g=ROW_PRE_G, conv_b=ROW_CONV_B, b_rg_a=ROW_B_A, b_rg_x=ROW_B_X, lru_lambda=ROW_LAM,
                   post_norm_g=ROW_POST_G, attn_sinks=ROW_SINKS)[name]
        r = packed[kind][row:row + 1]
        if name == "attn_sinks":
            return r[:, 0:N_Q_HEADS]
        if name in ("b_rg_a", "b_rg_x"):
            return r.reshape(1, RNN_BLOCKS, RNN_BLOCK_W)
        return r

    shapes = dict(w_in=(1, D_MODEL, SHARD_IN), w_rnn_out=(1, SHARD_OUT, D_MODEL), w_attn_out=(1, SHARD_OUT, D_MODEL),
                  w_out=(1, SHARD_OUT, D_MODEL), w_rg_a=(1, RNN_BLOCKS, RNN_BLOCK_W, RNN_BLOCK_W),
                  w_rg_x=(1, RNN_BLOCKS, RNN_BLOCK_W, RNN_BLOCK_W), conv_w=(1, 4, SHARD_OUT))
    weights = ["pre_norm_g", "w_in", "b_gate", "conv_w", "conv_b", "w_rg_a", "b_rg_a", "w_rg_x", "b_rg_x",
               "lru_lambda", "attn_sinks", "w_rnn_out", "w_attn_out", "w_out", "post_norm_g"]
    results = []
    for kind in range(4):
        for name in weights:
            if name in out:
                results.append(out[name][kind].reshape(shapes[name]))
            else:
                results.append(unpack(kind, name))
    loss = 0.5 / D_MODEL * jnp.sum(packed[0][ROW_LOSS])
    return (loss, g["grad_x"][None], *results)
```

```python
import functools

import jax
import jax.numpy as jnp
from jax import lax
from jax.experimental import pallas as pl
from jax.experimental.pallas import tpu as pltpu

F32 = jnp.float32
BF16 = jnp.bfloat16

D_MODEL = 1024
D_RNN = 1024
RNN_BLOCKS = 16
RNN_BLOCK_W = 64
LRU_C = 8.0
N_Q_HEADS = 16
HEAD_DIM = 64
D_KV = 256
BLOCK = 128
ALIBI_MAX_BIAS = 8.0
EPS = 1e-6
D_IN = 6656
N_DEV = 8
N_CHIP = 4
SHARD_IN = D_IN // N_DEV
SHARD_OUT = D_MODEL // N_DEV
ATTN_SCALE = HEAD_DIM ** -0.5
MASKED = -1e30

ADAM_LR = 0.001
ADAM_B1 = 0.9
ADAM_B2 = 0.999
ADAM_EPS = 1e-08
ADAM_WD = 0.01
ADAM_STEP = 10

VMEM_LIMIT_BYTES = 52 * 1024 * 1024
LANE = 128
GROUP_W = 256
N_GROUPS = D_RNN // GROUP_W
SEG_CHUNK = 512

NT_DIMS = (((1,), (1,)), ((), ()))
TN_DIMS = (((0,), (0,)), ((), ()))
MESH = pl.DeviceIdType.MESH
ANY = pl.BlockSpec(memory_space=pl.ANY)


def _params(*semantics):
    return pltpu.CompilerParams(dimension_semantics=semantics, vmem_limit_bytes=VMEM_LIMIT_BYTES)


def _sigmoid(x):
    return 0.5 * jnp.tanh(0.5 * x) + 0.5


def _log1p(e):
    u = 1.0 + e
    den = jnp.where(u == 1.0, 1.0, u - 1.0)
    return jnp.where(u == 1.0, e, jnp.log(u) * (e / den))


def _softplus(z):
    return jnp.maximum(z, 0.0) + _log1p(jnp.exp(-jnp.abs(z)))


def _rows8(rows, width):
    idx = lax.broadcasted_iota(jnp.int32, (8, width), 0)
    out = jnp.zeros((8, width), F32)
    for r, v in enumerate(rows):
        out = jnp.where(idx == r, v, out)
    return out


class _Ride:
    def __init__(self, arrays, out_shapes, scratch_shapes, start, finish, middle=None):
        self.arrays, self.out_shapes, self.scratch_shapes = list(arrays), list(out_shapes), list(scratch_shapes)
        self.start, self.finish, self.middle = start, finish, middle


class _Hosted:
    def __init__(self, ride, n_in, n_out, n_scratch=0):
        self.ride = ride
        self.sizes = (n_in, len(ride.arrays) if ride else 0, n_out, len(ride.out_shapes) if ride else 0, n_scratch)
        self.arrays = ride.arrays if ride else []
        self.in_specs = [ANY] * len(self.arrays)
        self.out_shapes = ride.out_shapes if ride else []
        self.out_specs = [ANY] * len(self.out_shapes)
        self.scratch_shapes = ride.scratch_shapes if ride else []

    def split(self, refs):
        n_in, r_in, n_out, r_out, n_scr = self.sizes
        cuts = [0, n_in, n_in + r_in, n_in + r_in + n_out, n_in + r_in + n_out + r_out, n_in + r_in + n_out + r_out + n_scr]
        host_in, ride_in, host_out, ride_out, host_scr = (refs[cuts[k]:cuts[k + 1]] for k in range(5))
        ride_scr = refs[cuts[5]:]

        def start(when):
            if self.ride is not None:
                pl.when(when)(lambda: self.ride.start(ride_in, ride_out, ride_scr))

        def finish(when):
            if self.ride is not None:
                pl.when(when)(lambda: self.ride.finish(ride_in, ride_out, ride_scr))

        def middle(when):
            if self.ride is not None and self.ride.middle is not None:
                pl.when(when)(lambda: self.ride.middle(ride_in, ride_out, ride_scr))

        start.middle = middle
        return tuple(host_in) + tuple(host_out) + tuple(host_scr), start, finish

    def results(self, outs, n_out):
        outs = list(outs) if isinstance(outs, (list, tuple)) else [outs]
        return outs[:n_out], outs[n_out:]


def _run_ride(ride, name):
    n_in, n_out = len(ride.arrays), len(ride.out_shapes)

    def body(*refs):
        ins, outs, sems = refs[:n_in], refs[n_in:n_in + n_out], refs[n_in + n_out:]
        ride.start(ins, outs, sems)
        ride.finish(ins, outs, sems)

    return pl.pallas_call(
        body, name=name, in_specs=[ANY] * n_in, out_specs=[ANY] * n_out,
        out_shape=ride.out_shapes, scratch_shapes=ride.scratch_shapes)(*ride.arrays)


def _load_resident(w_hbm, w_vmem, sems, first):
    def piece(c):
        rows = pl.ds(c * SEG_CHUNK, SEG_CHUNK)
        return pltpu.make_async_copy(w_hbm.at[rows], w_vmem.at[rows], sems.at[c])

    @pl.when(first)
    def _():
        for c in range(w_vmem.shape[0] // SEG_CHUNK):
            piece(c).start()

    def ready(c):
        @pl.when(first)
        def _():
            piece(c).wait()

    return ready


CHIP_ROWS = 2 * SHARD_IN
PROJ_WIDTHS = (2 * D_RNN, D_MODEL + 2 * D_KV, 3 * D_MODEL)
PROJ_DTYPES = (F32, BF16, BF16)


def _chip_pieces():
    starts = [0, PROJ_WIDTHS[0], PROJ_WIDTHS[0] + PROJ_WIDTHS[1], D_IN]
    pieces = []
    for k in range(N_CHIP):
        lo, hi = k * CHIP_ROWS, (k + 1) * CHIP_ROWS
        cur = []
        for a in range(len(PROJ_WIDTHS)):
            s0, s1 = max(lo, starts[a]), min(hi, starts[a + 1])
            if s0 < s1:
                cur.append((a, s0 - starts[a], s1 - s0, s0 - lo))
        pieces.append(cur)
    return pieces


def _gather_project(x, g, wt_shard, name, tm=512):
    t, k = x.shape
    tm = min(tm, t)
    nt = t // tm
    pieces = _chip_pieces()

    def body(x_ref, g_ref, shard_ref, h_out, rx_ref, qkv_ref, ag_ref, wt_all,
             w_c, stage, o32, o16, h_all, send_sems, recv_sems, local_sem, stage_sems, out_sems, h_sems):
        s, ti = pl.program_id(0), pl.program_id(1)
        px, py, pc = _place()
        me, sibling = (px, py, pc), (px, py, 1 - pc)
        chips = [(px, py), (1 - px, py), (px, 1 - py), (1 - px, 1 - py)]
        outs = (rx_ref, qkv_ref, ag_ref)

        def slot(dev):
            return wt_all.at[4 * dev[0] + 2 * dev[1] + dev[2]]

        def copy(kk, block, to, src=None):
            return pltpu.make_async_remote_copy(
                src_ref=slot(block) if src is None else src, dst_ref=slot(block),
                send_sem=send_sems.at[kk], recv_sem=recv_sems.at[kk], device_id=to, device_id_type=MESH)

        mine = pltpu.make_async_copy(shard_ref, slot(me), local_sem)
        first = [copy(0, me, sibling, src=shard_ref)] + [copy(1 + j, me, (*chips[1 + j], pc), src=shard_ref) for j in range(3)]
        passed = [copy(4 + j, (*chips[1 + j], pc), sibling) for j in range(3)]

        @pl.when((s == 0) & (ti == 0))
        def _():
            mine.start()
            for cp in first:
                cp.start()

        for step in range(N_CHIP):
            @pl.when((s == step) & (ti == 0))
            def _(step=step):
                chip = chips[step]
                if step == 0:
                    mine.wait()
                    copy(0, sibling, me).wait_recv()
                else:
                    copy(step, (*chip, pc), me).wait_recv()
                    passed[step - 1].start()
                    copy(3 + step, (*chip, 1 - pc), me).wait_recv()
                loads = [pltpu.make_async_copy(slot((*chip, core)), stage.at[pl.ds(core * SHARD_IN, SHARD_IN)], stage_sems.at[core])
                         for core in (0, 1)]
                for cp in loads:
                    cp.start()
                for cp in loads:
                    cp.wait()
                w_c[...] = stage[...].T

        rows = pl.ds(pl.multiple_of(ti * tm, tm), tm)

        @pl.when(s == 0)
        def _():
            xv = x_ref[...]
            h_all[rows, :] = (xv * lax.rsqrt(jnp.mean(xv * xv, axis=-1, keepdims=True) + EPS) * g_ref[...]).astype(BF16)

        res = jnp.dot(h_all[rows, :], w_c[...], preferred_element_type=F32)

        n = s * nt + ti
        buf = lax.rem(n, 2)
        chip_idx = [2 * cx + cy for cx, cy in chips]

        def chip_at(step):
            return jnp.where(step == 0, chip_idx[0], jnp.where(step == 1, chip_idx[1], jnp.where(step == 2, chip_idx[2], chip_idx[3])))

        def h_write(b, tile):
            return pltpu.make_async_copy(h_all.at[pl.ds(tile * tm, tm)], h_out.at[pl.ds(tile * tm, tm)], h_sems.at[b])

        def writes(kchip, b, tile):
            cps = []
            for idx, (a, col, w, src) in enumerate(pieces[kchip]):
                staged = (o16 if PROJ_DTYPES[a] == BF16 else o32).at[b, :, pl.ds(src, w)]
                cps.append(pltpu.make_async_copy(staged, outs[a].at[pl.ds(tile * tm, tm), pl.ds(col, w)], out_sems.at[b, idx]))
            return cps

        o32[buf] = res
        o16[buf] = res.astype(BF16)
        kcur = chip_at(s)

        @pl.when(n > 0)
        def _():
            kprev = chip_at(lax.div(n - 1, nt))
            for kchip in range(N_CHIP):
                @pl.when(kprev == kchip)
                def _(kchip=kchip):
                    for cp in writes(kchip, 1 - buf, lax.rem(n - 1, nt)):
                        cp.wait()

            @pl.when(n <= nt)
            def _():
                h_write(1 - buf, n - 1).wait()

        @pl.when(s == 0)
        def _():
            h_write(buf, ti).start()

        for kchip in range(N_CHIP):
            @pl.when(kcur == kchip)
            def _(kchip=kchip):
                for cp in writes(kchip, buf, ti):
                    cp.start()

        @pl.when(n == N_CHIP * nt - 1)
        def _():
            for kchip in range(N_CHIP):
                @pl.when(kcur == kchip)
                def _(kchip=kchip):
                    for cp in writes(kchip, buf, ti):
                        cp.wait()
            for cp in first + passed:
                cp.wait_send()

    tile = pl.BlockSpec((tm, k), lambda s, ti: (jnp.where(s == 0, ti, nt - 1), 0))
    return pl.pallas_call(
        body, name=name, grid=(N_CHIP, nt),
        in_specs=[tile, pl.BlockSpec((1, k), lambda s, ti: (0, 0)), ANY],
        out_specs=[ANY, ANY, ANY, ANY, ANY],
        out_shape=[jax.ShapeDtypeStruct((t, k), BF16)]
        + [jax.ShapeDtypeStruct((t, w), dt) for w, dt in zip(PROJ_WIDTHS, PROJ_DTYPES)]
        + [jax.ShapeDtypeStruct((N_DEV, SHARD_IN, k), BF16)],
        scratch_shapes=[pltpu.VMEM((k, CHIP_ROWS), BF16), pltpu.VMEM((CHIP_ROWS, k), BF16),
                        pltpu.VMEM((2, tm, CHIP_ROWS), F32), pltpu.VMEM((2, tm, CHIP_ROWS), BF16), pltpu.VMEM((t, k), BF16),
                        pltpu.SemaphoreType.DMA((7,)), pltpu.SemaphoreType.DMA((7,)), pltpu.SemaphoreType.DMA,
                        pltpu.SemaphoreType.DMA((2,)), pltpu.SemaphoreType.DMA((2, 2)), pltpu.SemaphoreType.DMA((2,))],
        compiler_params=_params("arbitrary", "arbitrary"))(x, g, wt_shard)


def _mm_tn(a, b, name, tm=512, tk=4096):
    ktok, m = a.shape
    n = b.shape[1]
    tk = min(tk, ktok)

    def body(a_ref, b_ref, o_ref):
        @pl.when(pl.program_id(1) == 0)
        def _():
            o_ref[...] = jnp.zeros_like(o_ref)

        o_ref[...] += lax.dot_general(a_ref[...], b_ref[...], TN_DIMS, preferred_element_type=F32)

    return pl.pallas_call(
        body, name=name, grid=(m // tm, ktok // tk),
        in_specs=[pl.BlockSpec((tk, tm), lambda i, kk: (kk, i)), pl.BlockSpec((tk, n), lambda i, kk: (kk, 0))],
        out_specs=pl.BlockSpec((tm, n), lambda i, kk: (i, 0)),
        out_shape=jax.ShapeDtypeStruct((m, n), F32),
        compiler_params=_params("parallel", "arbitrary"))(a, b)


def _segment_chunks(segs):
    bounds = [0]
    for s in segs:
        bounds.append(bounds[-1] + s.shape[1] // SEG_CHUNK)
    return bounds


def _input_grad(segs, wt, x, g, dy, name, tm=512, ride=None):
    m = segs[0].shape[0]
    rows, n = wt.shape
    tm = min(tm, m)
    bounds = _segment_chunks(segs)
    n_seg = len(segs)
    ni = m // tm
    host = _Hosted(ride, n_seg + 4, 2, 2)

    def body(*refs):
        host_refs, start, finish = host.split(refs)
        a_refs = host_refs[:n_seg]
        wt_hbm, x_ref, g_ref, dy_ref, gx_ref, st_ref, wt_vmem, sems = host_refs[n_seg:]
        i = pl.program_id(0)
        start(i == 0)

        @pl.when(i == 0)
        def _():
            st_ref[...] = jnp.zeros_like(st_ref)

        ready = _load_resident(wt_hbm, wt_vmem, sems, i == 0)
        dh = None
        for s in range(n_seg):
            for c in range(bounds[s], bounds[s + 1]):
                ready(c)
            part = jnp.dot(a_refs[s][...], wt_vmem[bounds[s] * SEG_CHUNK:bounds[s + 1] * SEG_CHUNK, :], preferred_element_type=F32)
            dh = part if dh is None else dh + part
        xv = x_ref[...]
        r = lax.rsqrt(jnp.mean(xv * xv, axis=-1, keepdims=True) + EPS)
        xn = xv * r
        dxn = dh * g_ref[...]
        gx_ref[...] = dy_ref[...] + r * (dxn - xn * jnp.mean(dxn * xn, axis=-1, keepdims=True))
        st_ref[...] += _rows8([jnp.sum(dh * xn, axis=0, keepdims=True)], n)
        finish(i == ni - 1)

    tile = pl.BlockSpec((tm, n), lambda i: (i, 0))
    outs = pl.pallas_call(
        body, name=name, grid=(ni,),
        in_specs=[pl.BlockSpec((tm, sg.shape[1]), lambda i: (i, 0)) for sg in segs]
        + [ANY, tile, pl.BlockSpec((1, n), lambda i: (0, 0)), tile] + host.in_specs,
        out_specs=[tile, pl.BlockSpec((8, n), lambda i: (0, 0))] + host.out_specs,
        out_shape=[jax.ShapeDtypeStruct((m, n), F32), jax.ShapeDtypeStruct((8, n), F32)] + host.out_shapes,
        scratch_shapes=[pltpu.VMEM((rows, n), wt.dtype), pltpu.SemaphoreType.DMA((rows // SEG_CHUNK,))] + host.scratch_shapes,
        compiler_params=_params("arbitrary"))(*segs, wt, x, g, dy, *host.arrays)
    res, landed = host.results(outs, 2)
    return (*res, landed) if ride else tuple(res)


def _mm_tn_seg(segs, b, name):
    ktok = segs[0].shape[0]
    n = b.shape[1]
    bounds = _segment_chunks(segs)
    n_seg = len(segs)
    nc = bounds[-1]
    seg_of = [s for s in range(n_seg) for _ in range(bounds[s], bounds[s + 1])]

    def body(*refs):
        a_hbm, b_hbm, o_ref = refs[:n_seg], refs[n_seg], refs[n_seg + 1]
        a_buf, b_vmem, a_sems, b_sem = refs[n_seg + 2:]
        c = pl.program_id(0)

        def fetch(cc):
            s = seg_of[cc]
            cols = pl.ds((cc - bounds[s]) * SEG_CHUNK, SEG_CHUNK)
            return pltpu.make_async_copy(a_hbm[s].at[:, cols], a_buf.at[cc % 2], a_sems.at[cc % 2])

        @pl.when(c == 0)
        def _():
            whole = pltpu.make_async_copy(b_hbm, b_vmem, b_sem)
            whole.start()
            fetch(0).start()
            whole.wait()

        for cc in range(nc):
            @pl.when(c == cc)
            def _(cc=cc):
                if cc + 1 < nc:
                    fetch(cc + 1).start()
                fetch(cc).wait()

        o_ref[...] = lax.dot_general(a_buf[c % 2], b_vmem[...], TN_DIMS, preferred_element_type=F32)

    return pl.pallas_call(
        body, name=name, grid=(nc,),
        in_specs=[ANY] * (n_seg + 1), out_specs=pl.BlockSpec((SEG_CHUNK, n), lambda c: (c, 0)),
        out_shape=jax.ShapeDtypeStruct((nc * SEG_CHUNK, n), F32),
        scratch_shapes=[pltpu.VMEM((2, ktok, SEG_CHUNK), segs[0].dtype), pltpu.VMEM((ktok, n), b.dtype),
                        pltpu.SemaphoreType.DMA((2,)), pltpu.SemaphoreType.DMA],
        compiler_params=_params("arbitrary"))(*segs, b)


def _branches_fwd(z_rnn, z_attn, ag_ml, b_gate, w_rnn, w_attn, w_out, x, target, g_post, name, tm=512):
    t, d = x.shape
    tm = min(tm, t)

    def body(zr_ref, za_ref, lr_ref, la_ref, br_ref, ba_ref, wr_ref, wa_ref, wo_ref, x_ref, t_ref, g_ref,
             brr_ref, bra_ref, mg_ref, do_ref, dy_ref, st_ref):
        @pl.when(pl.program_id(0) == 0)
        def _():
            st_ref[...] = jnp.zeros_like(st_ref)

        br_rnn = jnp.dot(zr_ref[...], wr_ref[...], preferred_element_type=F32)
        br_attn = jnp.dot(za_ref[...], wa_ref[...], preferred_element_type=F32)
        brr_ref[...] = br_rnn.astype(BF16)
        bra_ref[...] = br_attn.astype(BF16)
        g_rnn = _sigmoid(lr_ref[...].astype(F32) + br_ref[...])
        g_attn = _sigmoid(la_ref[...].astype(F32) + ba_ref[...])
        merged = (g_rnn * br_rnn + g_attn * br_attn).astype(BF16)
        mg_ref[...] = merged
        o = jnp.dot(merged, wo_ref[...], preferred_element_type=F32)
        g = g_ref[...]
        r = lax.rsqrt(jnp.mean(o * o, axis=-1, keepdims=True) + EPS)
        nrm = o * r
        err = x_ref[...] + nrm * g - t_ref[...]
        dy = err * (1.0 / d)
        dy_ref[...] = dy
        dn = dy * g
        do_ref[...] = (r * (dn - nrm * jnp.mean(dn * nrm, axis=-1, keepdims=True))).astype(BF16)
        st_ref[...] += _rows8([jnp.sum(dy * nrm, axis=0, keepdims=True), jnp.sum(err * err, axis=0, keepdims=True)], d)

    tile = pl.BlockSpec((tm, d), lambda i: (i, 0))
    weight = pl.BlockSpec((d, d), lambda i: (0, 0))
    bf = jax.ShapeDtypeStruct((t, d), BF16)
    return pl.pallas_call(
        body, name=name, grid=(t // tm,),
        in_specs=[tile, tile, pl.BlockSpec((tm, d), lambda i: (i, 1)), pl.BlockSpec((tm, d), lambda i: (i, 2)),
                  pl.BlockSpec((1, d), lambda i: (0, 0)), pl.BlockSpec((1, d), lambda i: (0, 1)),
                  weight, weight, weight, tile, tile, pl.BlockSpec((1, d), lambda i: (0, 0))],
        out_specs=[tile, tile, tile, tile, tile, pl.BlockSpec((8, d), lambda i: (0, 0))],
        out_shape=[bf, bf, bf, bf, jax.ShapeDtypeStruct((t, d), F32), jax.ShapeDtypeStruct((8, d), F32)],
        compiler_params=_params("arbitrary"))(z_rnn, z_attn, ag_ml, ag_ml, b_gate, b_gate, w_rnn, w_attn, w_out, x, target, g_post)


def _branches_bwd(dout, br_rnn, br_attn, ag_ml, b_gate, w_rnn, w_attn, w_out, name, tm=512):
    t, d = br_rnn.shape
    tm = min(tm, t)

    def body(do_ref, r_ref, a_ref, lr_ref, la_ref, br_ref, ba_ref, wr_ref, wa_ref, wo_ref,
             dr_ref, da_ref, dl_ref, dzr_ref, dza_ref, st_ref, wt_ref):
        @pl.when(pl.program_id(0) == 0)
        def _():
            st_ref[...] = jnp.zeros_like(st_ref)
            wt_ref[0] = wo_ref[...].T
            wt_ref[1] = wr_ref[...].T
            wt_ref[2] = wa_ref[...].T

        dm = jnp.dot(do_ref[...], wt_ref[0], preferred_element_type=F32)
        g_rnn = _sigmoid(lr_ref[...].astype(F32) + br_ref[...])
        g_attn = _sigmoid(la_ref[...].astype(F32) + ba_ref[...])
        dbr_rnn = (dm * g_rnn).astype(BF16)
        dbr_attn = (dm * g_attn).astype(BF16)
        dr_ref[...] = dbr_rnn
        da_ref[...] = dbr_attn
        dl_rnn = dm * r_ref[...].astype(F32) * g_rnn * (1.0 - g_rnn)
        dl_attn = dm * a_ref[...].astype(F32) * g_attn * (1.0 - g_attn)
        dl_ref[:, 0:d] = dl_rnn.astype(BF16)
        dl_ref[:, d:2 * d] = dl_attn.astype(BF16)
        st_ref[...] += _rows8([jnp.sum(dl_rnn, axis=0, keepdims=True), jnp.sum(dl_attn, axis=0, keepdims=True)], d)
        dzr_ref[...] = jnp.dot(dbr_rnn, wt_ref[1], preferred_element_type=F32).astype(BF16)
        dza_ref[...] = jnp.dot(dbr_attn, wt_ref[2], preferred_element_type=F32).astype(BF16)

    tile = pl.BlockSpec((tm, d), lambda i: (i, 0))
    weight = pl.BlockSpec((d, d), lambda i: (0, 0))
    bf = jax.ShapeDtypeStruct((t, d), BF16)
    return pl.pallas_call(
        body, name=name, grid=(t // tm,),
        in_specs=[tile, tile, tile, pl.BlockSpec((tm, d), lambda i: (i, 1)), pl.BlockSpec((tm, d), lambda i: (i, 2)),
                  pl.BlockSpec((1, d), lambda i: (0, 0)), pl.BlockSpec((1, d), lambda i: (0, 1)), weight, weight, weight],
        out_specs=[tile, tile, pl.BlockSpec((tm, 2 * d), lambda i: (i, 0)), tile, tile, pl.BlockSpec((8, d), lambda i: (0, 0))],
        out_shape=[bf, bf, jax.ShapeDtypeStruct((t, 2 * d), BF16), bf, bf, jax.ShapeDtypeStruct((8, d), F32)],
        scratch_shapes=[pltpu.VMEM((3, d, d), BF16)],
        compiler_params=_params("arbitrary"))(dout, br_rnn, br_attn, ag_ml, ag_ml, b_gate, b_gate, w_rnn, w_attn, w_out)


def _lru_gates(c, wa, ba, wx, bx, sp):
    cb = c.astype(BF16)
    r = _sigmoid(jnp.dot(cb, wa, preferred_element_type=F32) + ba)
    ig = _sigmoid(jnp.dot(cb, wx, preferred_element_type=F32) + bx)
    log_a = (-LRU_C) * r * sp
    a = jnp.exp(log_a)
    mult = jnp.sqrt(-jnp.tanh(log_a) * (a * a + 1.0))
    return cb, r, ig, a, mult


SUBLANES = 8


def _scan_fwd(a, u, carry, tt):
    w = a.shape[1]
    ng = tt // SUBLANES
    a3 = a.reshape(ng, SUBLANES, w)
    u3 = u.reshape(ng, SUBLANES, w)
    sub = lax.broadcasted_iota(jnp.int32, (ng, SUBLANES, w), 1)
    d = 1
    while d < SUBLANES:
        keep = sub >= d
        u3 = u3 + a3 * jnp.where(keep, pltpu.roll(u3, d, 1), 0.0)
        a3 = a3 * jnp.where(keep, pltpu.roll(a3, d, 1), 1.0)
        d *= 2
    out = []
    for g in range(ng):
        hg = u3[g] + a3[g] * carry
        out.append(hg)
        carry = hg[SUBLANES - 1:SUBLANES, :]
    return jnp.concatenate(out, axis=0)


def _scan_rev(b, g, carry, tt):
    w = b.shape[1]
    ng = tt // SUBLANES
    b3 = b.reshape(ng, SUBLANES, w)
    g3 = g.reshape(ng, SUBLANES, w)
    sub = lax.broadcasted_iota(jnp.int32, (ng, SUBLANES, w), 1)
    d = 1
    while d < SUBLANES:
        keep = sub < SUBLANES - d
        g3 = g3 + b3 * jnp.where(keep, pltpu.roll(g3, SUBLANES - d, 1), 0.0)
        b3 = b3 * jnp.where(keep, pltpu.roll(b3, SUBLANES - d, 1), 1.0)
        d *= 2
    out = [None] * ng
    for k in range(ng - 1, -1, -1):
        hk = g3[k] + b3[k] * carry
        out[k] = hk
        carry = hk[0:1, :]
    return jnp.concatenate(out, axis=0)


def _conv_taps(cw, bias, x, ext_ref, tt):
    x2 = ext_ref[7:7 + tt, :]
    x1 = ext_ref[6:6 + tt, :]
    x0 = ext_ref[5:5 + tt, :]
    c = bias + cw[3:4] * x + cw[2:3] * x2 + cw[1:2] * x1 + cw[0:1] * x0
    return c, x2, x1, x0


def _rnn_fwd(rx_rg, cw, cb, wa, ba, wx, bx, lam, name, tt=512):
    t = rx_rg.shape[0]
    tt = min(tt, t)
    w = GROUP_W

    def body(rx_ref, rg_ref, cw_ref, cb_ref, wa_ref, ba_ref, wx_ref, bx_ref, lam_ref, y_ref, z_ref, ext_ref, hc_ref):
        @pl.when(pl.program_id(1) == 0)
        def _():
            ext_ref[0:8, :] = jnp.zeros((8, w), F32)
            hc_ref[...] = jnp.zeros((8, w), F32)

        x = rx_ref[...]
        ext_ref[8:8 + tt, :] = x
        c, _, _, _ = _conv_taps(cw_ref[...], cb_ref[...], x, ext_ref, tt)
        ext_ref[0:8, :] = x[tt - 8:tt, :]
        sp = _softplus(-lam_ref[...])
        _, _, ig, a, mult = _lru_gates(c, wa_ref[...], ba_ref[...], wx_ref[...], bx_ref[...], sp)
        h = _scan_fwd(a, mult * (ig * c), hc_ref[7:8, :], tt)
        hc_ref[...] = h[tt - 8:tt, :]
        y_ref[...] = h
        rg = rg_ref[...]
        z_ref[...] = (h * rg * _sigmoid(rg)).astype(BF16)

    vec = pl.BlockSpec((1, w), lambda g, i: (0, g))
    mat = pl.BlockSpec((None, w, w), lambda g, i: (g, 0, 0))
    tile = pl.BlockSpec((tt, w), lambda g, i: (i, g))
    return pl.pallas_call(
        body, name=name, grid=(N_GROUPS, t // tt),
        in_specs=[tile, pl.BlockSpec((tt, w), lambda g, i: (i, N_GROUPS + g)),
                  pl.BlockSpec((4, w), lambda g, i: (0, g)), vec, mat, vec, mat, vec, vec],
        out_specs=[tile, tile],
        out_shape=[jax.ShapeDtypeStruct((t, D_RNN), F32), jax.ShapeDtypeStruct((t, D_RNN), BF16)],
        scratch_shapes=[pltpu.VMEM((tt + 8, w), F32), pltpu.VMEM((8, w), F32)],
        compiler_params=_params("parallel", "arbitrary"))(rx_rg, rx_rg, cw, cb, wa, ba, wx, bx, lam)


def _rnn_bwd(rx_rg, y, dz, cw, cb, wa, ba, wx, bx, lam, name, tt=512, ride=None):
    t = rx_rg.shape[0]
    tt = min(tt, t)
    nt = t // tt
    w = GROUP_W

    host = _Hosted(ride, 13, 5, 6)

    def body(*refs):
        host_refs, start, finish = host.split(refs)
        (rx_ref, rg_ref, rxt_ref, y_ref, yt_ref, dz_ref, cw_ref, cb_ref, wa_ref, ba_ref, wx_ref, bx_ref, lam_ref,
         drx_ref, drg_ref, st_ref, gda_ref, gdx_ref, ext_ref, dcx_ref, wcar_ref, acar_ref, dwa_ref, dwx_ref) = host_refs
        ii = pl.program_id(1)
        start((pl.program_id(0) == 0) & (ii == 0))

        @pl.when(ii == 0)
        def _():
            wcar_ref[...] = jnp.zeros((8, w), F32)
            acar_ref[...] = jnp.zeros((8, w), F32)
            dcx_ref[tt:tt + 8, :] = jnp.zeros((8, w), F32)
            st_ref[...] = jnp.zeros_like(st_ref)
            dwa_ref[...] = jnp.zeros_like(dwa_ref)
            dwx_ref[...] = jnp.zeros_like(dwx_ref)

        has_prev = jnp.where(ii == nt - 1, 0.0, 1.0)
        x = rx_ref[...]
        ext_ref[0:8, :] = rxt_ref[...] * has_prev
        ext_ref[8:8 + tt, :] = x
        cwv = cw_ref[...]
        c, x2, x1, x0 = _conv_taps(cwv, cb_ref[...], x, ext_ref, tt)
        lam = lam_ref[...]
        sp = _softplus(-lam)
        wa = wa_ref[...]
        wx = wx_ref[...]
        cb16, r, ig, a, mult = _lru_gates(c, wa, ba_ref[...], wx, bx_ref[...], sp)

        rg = rg_ref[...]
        sg = _sigmoid(rg)
        dz = dz_ref[...].astype(F32)
        yv = y_ref[...]
        drg_ref[...] = (dz * yv * (sg * (1.0 + rg * (1.0 - sg)))).astype(BF16)

        row = lax.broadcasted_iota(jnp.int32, (tt, w), 0)
        b = jnp.where(row < tt - 1, pltpu.roll(a, tt - 1, 0), acar_ref[0:1, :])
        dh = _scan_rev(b, dz * (rg * sg), wcar_ref[0:1, :], tt)
        wcar_ref[...] = dh[0:8, :]
        acar_ref[...] = a[0:8, :]

        hprev = jnp.where(row >= 1, pltpu.roll(yv, 1, 0), yt_ref[7:8, :] * has_prev)
        dmult = dh * (ig * c)
        dig = dh * mult * c
        dlog_a = dh * hprev * a - dmult * (a * a / mult)
        dpa = dlog_a * ((-LRU_C) * sp) * r * (1.0 - r)
        dpx = dig * ig * (1.0 - ig)
        dsp = jnp.sum(dlog_a * r, axis=0, keepdims=True) * (-LRU_C)
        dlam = dsp * (-_sigmoid(-lam))
        dpa16 = dpa.astype(BF16)
        dpx16 = dpx.astype(BF16)
        dwa_ref[...] += lax.dot_general(cb16, dpa16, TN_DIMS, preferred_element_type=F32)
        dwx_ref[...] += lax.dot_general(cb16, dpx16, TN_DIMS, preferred_element_type=F32)
        dc = (dh * mult * ig
              + lax.dot_general(dpa16, wa, NT_DIMS, preferred_element_type=F32)
              + lax.dot_general(dpx16, wx, NT_DIMS, preferred_element_type=F32))

        dcx_ref[0:tt, :] = dc
        drx = (cwv[3:4] * dc + cwv[2:3] * dcx_ref[1:1 + tt, :] + cwv[1:2] * dcx_ref[2:2 + tt, :]
               + cwv[0:1] * dcx_ref[3:3 + tt, :])
        drx_ref[...] = drx.astype(BF16)
        dcx_ref[tt:tt + 8, :] = dc[0:8, :]

        def colsum(v):
            return jnp.sum(v, axis=0, keepdims=True)

        st_ref[...] += _rows8([colsum(dc), colsum(dpa), colsum(dpx), dlam,
                               colsum(dc * x0), colsum(dc * x1), colsum(dc * x2), colsum(dc * x)], w)

        @pl.when(ii == nt - 1)
        def _():
            for blk in range(GROUP_W // RNN_BLOCK_W):
                rows = slice(blk * RNN_BLOCK_W, (blk + 1) * RNN_BLOCK_W)
                gda_ref[blk] = dwa_ref[rows, rows]
                gdx_ref[blk] = dwx_ref[rows, rows]

        finish((pl.program_id(0) == N_GROUPS - 1) & (ii == nt - 1))

    def rev(ii):
        return nt - 1 - ii

    def tail(g, ii):
        return (jnp.maximum(rev(ii) * (tt // 8) - 1, 0), g)

    vec = pl.BlockSpec((1, w), lambda g, ii: (0, g))
    mat = pl.BlockSpec((None, w, w), lambda g, ii: (g, 0, 0))
    tile = pl.BlockSpec((tt, w), lambda g, ii: (rev(ii), g))
    diag_shape = (N_GROUPS, GROUP_W // RNN_BLOCK_W, RNN_BLOCK_W, RNN_BLOCK_W)
    diag = pl.BlockSpec((None,) + diag_shape[1:], lambda g, ii: (g, 0, 0, 0))
    outs = pl.pallas_call(
        body, name=name, grid=(N_GROUPS, nt),
        in_specs=[tile, pl.BlockSpec((tt, w), lambda g, ii: (rev(ii), N_GROUPS + g)), pl.BlockSpec((8, w), tail),
                  tile, pl.BlockSpec((8, w), tail), tile,
                  pl.BlockSpec((4, w), lambda g, ii: (0, g)), vec, mat, vec, mat, vec, vec] + host.in_specs,
        out_specs=[tile, tile, pl.BlockSpec((8, w), lambda g, ii: (0, g)), diag, diag] + host.out_specs,
        out_shape=[jax.ShapeDtypeStruct((t, D_RNN), BF16), jax.ShapeDtypeStruct((t, D_RNN), BF16),
                   jax.ShapeDtypeStruct((8, D_RNN), F32),
                   jax.ShapeDtypeStruct(diag_shape, F32), jax.ShapeDtypeStruct(diag_shape, F32)] + host.out_shapes,
        scratch_shapes=[pltpu.VMEM((tt + 8, w), F32), pltpu.VMEM((tt + 8, w), F32), pltpu.VMEM((8, w), F32),
                        pltpu.VMEM((8, w), F32), pltpu.VMEM((w, w), F32), pltpu.VMEM((w, w), F32)] + host.scratch_shapes,
        compiler_params=_params("arbitrary" if ride else "parallel", "arbitrary"))(
            rx_rg, rx_rg, rx_rg, y, y, dz, cw, cb, wa, ba, wx, bx, lam, *host.arrays)
    res, landed = host.results(outs, 5)
    return (*res, landed) if ride else tuple(res)


def _half_mask(shape, half):
    lane = lax.broadcasted_iota(jnp.int32, shape, 1)
    return (lane >= HEAD_DIM) if half else (lane < HEAD_DIM)


def _dup_half(t, half):
    sel = jnp.where(_half_mask(t.shape, half), t, 0.0)
    return sel + pltpu.roll(sel, HEAD_DIM, 1)


def _band_geometry(n):
    qi = lax.broadcasted_iota(jnp.int32, (BLOCK, 2 * BLOCK), 0)
    kj = lax.broadcasted_iota(jnp.int32, (BLOCK, 2 * BLOCK), 1)
    dist = BLOCK + qi - kj
    first_key = jnp.where(n > 0, 0, BLOCK)
    valid = (dist >= 0) & (dist < BLOCK) & (kj >= first_key)
    return dist.astype(F32), valid


GROUP = 4


def _kv_dup(prev_ref, cur_ref, hk, scale=1.0):
    tile = hk // 2
    kt = jnp.concatenate([prev_ref[:, tile * LANE:(tile + 1) * LANE], cur_ref[:, tile * LANE:(tile + 1) * LANE]], axis=0)
    return (_dup_half(kt.astype(F32), hk % 2) * scale).astype(BF16)


def _fill_bias(bias_ref, sm_ref, n):
    distf, valid = _band_geometry(n)
    for head in range(N_Q_HEADS):
        bias_ref[head] = jnp.where(valid, -sm_ref[1, head] * distf, MASKED)


def _head_scores(q2s, half, kdup, bias):
    qm = jnp.where(_half_mask(q2s.shape, half), q2s, jnp.zeros_like(q2s))
    return qm, lax.dot_general(qm, kdup, NT_DIMS, preferred_element_type=F32) + bias


def _attn_specs(nb, clamp_last):
    def blk(n):
        return jnp.minimum(n, nb - 1) if clamp_last else n

    q_spec = pl.BlockSpec((BLOCK, D_MODEL), lambda n: (blk(n), 0))
    k_prev = pl.BlockSpec((BLOCK, D_KV), lambda n: (jnp.maximum(blk(n) - 1, 0), D_MODEL // D_KV))
    k_cur = pl.BlockSpec((BLOCK, D_KV), lambda n: (blk(n), D_MODEL // D_KV))
    v_prev = pl.BlockSpec((BLOCK, D_KV), lambda n: (jnp.maximum(blk(n) - 1, 0), D_MODEL // D_KV + 1))
    v_cur = pl.BlockSpec((BLOCK, D_KV), lambda n: (blk(n), D_MODEL // D_KV + 1))
    return q_spec, k_prev, k_cur, v_prev, v_cur


def _attn_fwd(sm, qkv, ag_ml, name, ride=None):
    t = qkv.shape[0]
    nb = t // BLOCK

    host = _Hosted(ride, 7, 3, 1)

    def body(*refs):
        (sm_ref, q_ref, kp_ref, kc_ref, vp_ref, vc_ref, ag_ref, y_ref, z_ref, lse_ref, bias_ref), start, finish = host.split(refs)
        n = pl.program_id(0)
        start(n == 0)
        start.middle(n == (3 * nb) // 4)

        @pl.when(n <= 1)
        def _():
            _fill_bias(bias_ref, sm_ref, n)

        lane = lax.broadcasted_iota(jnp.int32, (BLOCK, LANE), 1)
        low = lane < HEAD_DIM
        lse = jnp.zeros((BLOCK, LANE), F32)
        for hk in range(N_Q_HEADS // GROUP):
            kdup = _kv_dup(kp_ref, kc_ref, hk)
            vdup = _kv_dup(vp_ref, vc_ref, hk)
            for k in (0, 1):
                c = slice((2 * hk + k) * LANE, (2 * hk + k + 1) * LANE)
                q2s = q_ref[:, c] * ATTN_SCALE
                outs = []
                for half in (0, 1):
                    head = GROUP * hk + 2 * k + half
                    _, s = _head_scores(q2s, half, kdup, bias_ref[head])
                    sink = sm_ref[0, head]
                    m = jnp.maximum(jnp.max(s, axis=1, keepdims=True), sink)
                    e = jnp.exp(s - m)
                    l = jnp.sum(e, axis=1, keepdims=True) + jnp.exp(sink - m)
                    outs.append(jnp.dot((e * (1.0 / l)).astype(BF16), vdup, preferred_element_type=F32))
                    lse = jnp.where(lane == head, m + jnp.log(l), lse)
                yt = jnp.where(low, outs[0], outs[1])
                y_ref[:, c] = yt
                ag = ag_ref[:, c].astype(F32)
                z_ref[:, c] = (yt * ag * _sigmoid(ag)).astype(BF16)
        lse_ref[...] = lse
        finish(n == nb - 1)

    q_spec, k_prev, k_cur, v_prev, v_cur = _attn_specs(nb, False)
    wide = pl.BlockSpec((BLOCK, D_MODEL), lambda n: (n, 0))
    outs = pl.pallas_call(
        body, name=name, grid=(nb,),
        in_specs=[pl.BlockSpec(memory_space=pltpu.SMEM), q_spec, k_prev, k_cur, v_prev, v_cur, wide] + host.in_specs,
        out_specs=[wide, wide, pl.BlockSpec((BLOCK, LANE), lambda n: (n, 0))] + host.out_specs,
        out_shape=[jax.ShapeDtypeStruct((t, D_MODEL), F32), jax.ShapeDtypeStruct((t, D_MODEL), BF16),
                   jax.ShapeDtypeStruct((t, LANE), F32)] + host.out_shapes,
        scratch_shapes=[pltpu.VMEM((N_Q_HEADS, BLOCK, 2 * BLOCK), F32)] + host.scratch_shapes,
        compiler_params=_params("arbitrary"))(sm, qkv, qkv, qkv, qkv, qkv, ag_ml, *host.arrays)
    res, landed = host.results(outs, 3)
    return (*res, landed) if ride else tuple(res)


def _attn_bwd(sm, qkv, ag_ml, y, lse, dz, name, ride=None):
    t = qkv.shape[0]
    nb = t // BLOCK
    host = _Hosted(ride, 10, 4, 3)

    def body(*refs):
        host_refs, start, finish = host.split(refs)
        (sm_ref, q_ref, kp_ref, kc_ref, vp_ref, vc_ref, ag_ref, y_ref, lse_ref, dz_ref,
         dq_ref, dkv_ref, dag_ref, ds_ref, ck_ref, cv_ref, bias_ref) = host_refs
        n = pl.program_id(0)
        start(n == 0)

        @pl.when(n == 0)
        def _():
            ck_ref[...] = jnp.zeros_like(ck_ref)
            cv_ref[...] = jnp.zeros_like(cv_ref)
            ds_ref[...] = jnp.zeros_like(ds_ref)

        @pl.when(n <= 1)
        def _():
            _fill_bias(bias_ref, sm_ref, n)

        @pl.when(n < nb)
        def _():
            lane8 = lax.broadcasted_iota(jnp.int32, (8, LANE), 1)
            row8 = lax.broadcasted_iota(jnp.int32, (8, LANE), 0)
            dsink = jnp.zeros((8, LANE), F32)
            dk_heads, dv_heads = [], []
            lse_tile = lse_ref[...]
            for hk in range(N_Q_HEADS // GROUP):
                kdup = _kv_dup(kp_ref, kc_ref, hk)
                ks = _kv_dup(kp_ref, kc_ref, hk, ATTN_SCALE)
                vdup = _kv_dup(vp_ref, vc_ref, hk)
                qms, dyhs, y_rows = [], [], []
                for k in (0, 1):
                    c = slice((2 * hk + k) * LANE, (2 * hk + k + 1) * LANE)
                    ag = ag_ref[:, c].astype(F32)
                    sg = _sigmoid(ag)
                    dzt = dz_ref[:, c].astype(F32)
                    yt = y_ref[:, c]
                    dag_ref[:, c] = (dzt * yt * (sg * (1.0 + ag * (1.0 - sg)))).astype(BF16)
                    dyt = dzt * (ag * sg)
                    q2s = q_ref[:, c] * ATTN_SCALE
                    for half in (0, 1):
                        hm = _half_mask(q2s.shape, half)
                        qms.append(jnp.where(hm, q2s, jnp.zeros_like(q2s)))
                        dyhs.append(jnp.where(hm, dyt, 0.0))
                        y_rows.append(yt)
                qm4 = jnp.concatenate(qms, axis=0)
                dy4 = jnp.concatenate(dyhs, axis=0)
                dy4_16 = dy4.astype(BF16)
                s4 = lax.dot_general(qm4, kdup, NT_DIMS, preferred_element_type=F32)
                dp4 = lax.dot_general(dy4_16, vdup, NT_DIMS, preferred_element_type=F32)
                probs16, ds16 = [], []
                for r in range(GROUP):
                    head = GROUP * hk + r
                    rows = slice(r * BLOCK, (r + 1) * BLOCK)
                    lh = lse_tile[:, head:head + 1]
                    probs = jnp.exp(s4[rows] + bias_ref[head] - lh)
                    psink = jnp.exp(sm_ref[0, head] - lh)
                    delta = jnp.sum(dyhs[r] * y_rows[r], axis=1, keepdims=True)
                    ds16.append((probs * (dp4[rows] - delta)).astype(BF16))
                    probs16.append(probs.astype(BF16))
                    dsink = dsink + jnp.where((row8 == 0) & (lane8 == head),
                                              -jnp.sum(psink * delta, axis=0, keepdims=True), 0.0)
                ds4 = jnp.concatenate(ds16, axis=0)
                p4 = jnp.concatenate(probs16, axis=0)
                dq4 = jnp.dot(ds4, ks, preferred_element_type=F32)
                low = _half_mask((BLOCK, LANE), 0)
                for k in (0, 1):
                    c = slice((2 * hk + k) * LANE, (2 * hk + k + 1) * LANE)
                    dq_ref[:, c] = jnp.where(low, dq4[2 * k * BLOCK:(2 * k + 1) * BLOCK],
                                             dq4[(2 * k + 1) * BLOCK:(2 * k + 2) * BLOCK]).astype(BF16)
                dk_acc = lax.dot_general(ds4, qm4, TN_DIMS, preferred_element_type=F32)
                dv_acc = lax.dot_general(p4, dy4_16, TN_DIMS, preferred_element_type=F32)
                dk_heads.append(dk_acc + pltpu.roll(dk_acc, HEAD_DIM, 1))
                dv_heads.append(dv_acc + pltpu.roll(dv_acc, HEAD_DIM, 1))
            ds_ref[...] += dsink
            low = _half_mask((2 * BLOCK, LANE), 0)
            for tile in range(2):
                cols = slice(tile * LANE, (tile + 1) * LANE)
                dkt = jnp.where(low, dk_heads[2 * tile], dk_heads[2 * tile + 1])
                dvt = jnp.where(low, dv_heads[2 * tile], dv_heads[2 * tile + 1])
                dkv_ref[:, cols] = (ck_ref[:, cols] + dkt[0:BLOCK, :]).astype(BF16)
                dkv_ref[:, D_KV + tile * LANE:D_KV + (tile + 1) * LANE] = (cv_ref[:, cols] + dvt[0:BLOCK, :]).astype(BF16)
                ck_ref[:, cols] = dkt[BLOCK:2 * BLOCK, :]
                cv_ref[:, cols] = dvt[BLOCK:2 * BLOCK, :]

        @pl.when(n == nb)
        def _():
            dkv_ref[:, 0:D_KV] = ck_ref[...].astype(BF16)
            dkv_ref[:, D_KV:2 * D_KV] = cv_ref[...].astype(BF16)

        finish(n == nb)

    q_spec, k_prev, k_cur, v_prev, v_cur = _attn_specs(nb, True)
    wide = pl.BlockSpec((BLOCK, D_MODEL), lambda n: (jnp.minimum(n, nb - 1), 0))
    outs = pl.pallas_call(
        body, name=name, grid=(nb + 1,),
        in_specs=[pl.BlockSpec(memory_space=pltpu.SMEM), q_spec, k_prev, k_cur, v_prev, v_cur, wide, wide,
                  pl.BlockSpec((BLOCK, LANE), lambda n: (jnp.minimum(n, nb - 1), 0)), wide] + host.in_specs,
        out_specs=[wide, pl.BlockSpec((BLOCK, 2 * D_KV), lambda n: (jnp.maximum(n - 1, 0), 0)), wide,
                   pl.BlockSpec((8, LANE), lambda n: (0, 0))] + host.out_specs,
        out_shape=[jax.ShapeDtypeStruct((t, D_MODEL), BF16), jax.ShapeDtypeStruct((t, 2 * D_KV), BF16),
                   jax.ShapeDtypeStruct((t, D_MODEL), BF16), jax.ShapeDtypeStruct((8, LANE), F32)] + host.out_shapes,
        scratch_shapes=[pltpu.VMEM((BLOCK, D_KV), F32), pltpu.VMEM((BLOCK, D_KV), F32),
                        pltpu.VMEM((N_Q_HEADS, BLOCK, 2 * BLOCK), F32)] + host.scratch_shapes,
        compiler_params=_params("arbitrary"))(sm, qkv, qkv, qkv, qkv, qkv, ag_ml, y, lse, dz, *host.arrays)
    res, landed = host.results(outs, 4)
    return (*res, landed) if ride else tuple(res)


def _local_grads(x, target, p, project, late_weights=None, reduce_out=None, reduce_in=None):
    h, rx_rg, qkv, ag_ml, wt = project(x, p["pre_g"])
    if late_weights is None:
        y_attn, z_attn, lse = _attn_fwd(p["sm"], qkv, ag_ml, "attn_fwd")
    else:
        y_attn, z_attn, lse, landed = _attn_fwd(p["sm"], qkv, ag_ml, "attn_fwd", ride=late_weights[0])
        p = {**p, **late_weights[1](landed)}
    lru = (p["cw"], p["cb"], p["wbd_a"], p["b_a"], p["wbd_x"], p["b_x"], p["lam"])
    y_rnn, z_rnn = _rnn_fwd(rx_rg, *lru, "rnn_fwd")
    br_rnn, br_attn, merged, dout, dy, st_post = _branches_fwd(
        z_rnn, z_attn, ag_ml, p["b_gate"], p["w_rnn"], p["w_attn"], p["w_out"], x, target, p["post_g"], "branches_fwd")

    dbr_rnn, dbr_attn, d_ml, dz_rnn, dz_attn, st_merge = _branches_bwd(
        dout, br_rnn, br_attn, ag_ml, p["b_gate"], p["w_rnn"], p["w_attn"], p["w_out"], "branches_bwd")
    gw_out = _mm_tn(merged, dout, "gw_out")
    gw_rnn = _mm_tn(z_rnn, dbr_rnn, "gw_rnn")
    gw_attn = _mm_tn(z_attn, dbr_attn, "gw_attn")
    red_out = red_in = None
    if reduce_out is None:
        d_rx, d_rg, st_rnn, g_rg_a, g_rg_x = _rnn_bwd(rx_rg, y_rnn, dz_rnn, *lru, "rnn_bwd")
        dq, dkv, d_ag, st_sink = _attn_bwd(p["sm"], qkv, ag_ml, y_attn, lse, dz_attn, "attn_bwd")
    else:
        d_rx, d_rg, st_rnn, g_rg_a, g_rg_x, from_sibling = _rnn_bwd(
            rx_rg, y_rnn, dz_rnn, *lru, "rnn_bwd", ride=reduce_out[0](gw_rnn, gw_attn, gw_out))
        dq, dkv, d_ag, st_sink, red_out = _attn_bwd(p["sm"], qkv, ag_ml, y_attn, lse, dz_attn, "attn_bwd",
                                                    ride=reduce_out[1](from_sibling, g_rg_a, g_rg_x))

    segs = [d_rx, d_rg, dq, dkv, d_ag, d_ml]
    gwt = _mm_tn_seg(segs, h, "gw_in")
    if reduce_in is None:
        grad_x, st_pre = _input_grad(segs, wt, x, p["pre_g"], dy, "input_grad")
    else:
        grad_x, st_pre, red_in = _input_grad(segs, wt, x, p["pre_g"], dy, "input_grad", ride=reduce_in(gwt))
    return dict(grad_x=grad_x, gwt=gwt, gw_rnn=gw_rnn, gw_attn=gw_attn, gw_out=gw_out,
                st_post=st_post, st_merge=st_merge, st_rnn=st_rnn, st_sink=st_sink, st_pre=st_pre,
                g_rg_a=g_rg_a, g_rg_x=g_rg_x, red_out=red_out, red_in=red_in)


def _place():
    x, y, c = lax.axis_index("x"), lax.axis_index("y"), lax.axis_index("c")
    return x, y, c


def _gather_ride(shards):
    n = len(shards)

    def copies(ins, outs, sems):
        send_sems, recv_sems, local_sems = sems
        x, y, c = _place()
        me, sibling = (x, y, c), (x, y, 1 - c)
        chips = [(1 - x, y), (x, 1 - y), (1 - x, 1 - y)]

        def slot(a, dev):
            return outs[a].at[4 * dev[0] + 2 * dev[1] + dev[2]]

        def copy(a, k, block, to, src=None):
            return pltpu.make_async_remote_copy(
                src_ref=slot(a, block) if src is None else src, dst_ref=slot(a, block),
                send_sem=send_sems.at[a, k], recv_sem=recv_sems.at[a, k], device_id=to, device_id_type=MESH)

        mine = [pltpu.make_async_copy(ins[a], slot(a, me), local_sems.at[a]) for a in range(n)]
        first = []
        for a in range(n):
            first.append(copy(a, 0, me, sibling, src=ins[a]))
            first += [copy(a, 1 + j, me, (*chip, c), src=ins[a]) for j, chip in enumerate(chips)]
        return me, sibling, chips, c, copy, mine, first

    def start(ins, outs, sems):
        *_, mine, first = copies(ins, outs, sems)
        for cp in mine + first:
            cp.start()

    def middle(ins, outs, sems):
        me, sibling, chips, c, copy, _, _ = copies(ins, outs, sems)
        for j, chip in enumerate(chips):
            for a in range(n):
                copy(a, 1 + j, (*chip, c), me).wait_recv()
                copy(a, 4 + j, (*chip, c), sibling).start()

    def finish(ins, outs, sems):
        me, sibling, chips, c, copy, mine, first = copies(ins, outs, sems)
        passed = [copy(a, 4 + j, (*chip, c), sibling) for j, chip in enumerate(chips) for a in range(n)]
        for a in range(n):
            copy(a, 0, sibling, me).wait_recv()
            for j, chip in enumerate(chips):
                copy(a, 4 + j, (*chip, 1 - c), me).wait_recv()
        for cp in first + passed:
            cp.wait_send()
        for cp in mine:
            cp.wait()

    return _Ride(
        shards, [jax.ShapeDtypeStruct((N_DEV, *s.shape), s.dtype) for s in shards],
        [pltpu.SemaphoreType.DMA((n, 7)), pltpu.SemaphoreType.DMA((n, 7)), pltpu.SemaphoreType.DMA((n,))],
        start, finish, middle)


def _sibling_ride(scatter, whole):
    ns, nw = len(scatter), len(whole)

    def copies(ins, outs, sems):
        send_sems, recv_sems = sems
        x, y, c = _place()
        cps = [pltpu.make_async_remote_copy(
            src_ref=ins[a].at[2 * chip + (1 - c)], dst_ref=outs[a].at[chip],
            send_sem=send_sems.at[a * N_CHIP + chip], recv_sem=recv_sems.at[a * N_CHIP + chip],
            device_id=(x, y, 1 - c), device_id_type=MESH) for a in range(ns) for chip in range(N_CHIP)]
        cps += [pltpu.make_async_remote_copy(
            src_ref=ins[ns + a], dst_ref=outs[ns + a],
            send_sem=send_sems.at[ns * N_CHIP + a], recv_sem=recv_sems.at[ns * N_CHIP + a],
            device_id=(x, y, 1 - c), device_id_type=MESH) for a in range(nw)]
        return cps

    def start(ins, outs, sems):
        for cp in copies(ins, outs, sems):
            cp.start()

    def finish(ins, outs, sems):
        for cp in copies(ins, outs, sems):
            cp.wait()

    n_sem = ns * N_CHIP + nw
    return _Ride(
        list(scatter) + list(whole),
        [jax.ShapeDtypeStruct((N_CHIP, *s.shape[1:]), s.dtype) for s in scatter]
        + [jax.ShapeDtypeStruct(s.shape, s.dtype) for s in whole],
        [pltpu.SemaphoreType.DMA((n_sem,)), pltpu.SemaphoreType.DMA((n_sem,))], start, finish)


def _chips_ride(scatter, whole):
    ns, nw = len(scatter), len(whole)
    n = ns + nw

    def copies(ins, outs, sems):
        send_sems, recv_sems, local_sems = sems
        x, y, c = _place()
        own = 2 * x + y
        chips = [(1 - x, y), (x, 1 - y), (1 - x, 1 - y)]

        def src(a, chip_idx):
            return ins[a].at[chip_idx] if a < ns else ins[a]

        local = [pltpu.make_async_copy(src(a, own), outs[a].at[own], local_sems.at[a]) for a in range(n)]
        sent = [pltpu.make_async_remote_copy(
            src_ref=src(a, 2 * chip[0] + chip[1]), dst_ref=outs[a].at[own],
            send_sem=send_sems.at[a, j], recv_sem=recv_sems.at[a, own], device_id=(*chip, c), device_id_type=MESH)
            for a in range(n) for j, chip in enumerate(chips)]
        return chips, c, local, sent

    def start(ins, outs, sems):
        _, _, local, sent = copies(ins, outs, sems)
        for cp in local + sent:
            cp.start()

    def finish(ins, outs, sems):
        send_sems, recv_sems, _ = sems
        chips, c, local, sent = copies(ins, outs, sems)
        for a in range(n):
            for chip in chips:
                k = 2 * chip[0] + chip[1]
                pltpu.make_async_remote_copy(
                    src_ref=outs[a].at[k], dst_ref=outs[a].at[k], send_sem=send_sems.at[a, 0],
                    recv_sem=recv_sems.at[a, k], device_id=(*chip, c), device_id_type=MESH).wait_recv()
        for cp in sent:
            cp.wait_send()
        for cp in local:
            cp.wait()

    return _Ride(
        list(scatter) + list(whole),
        [jax.ShapeDtypeStruct(s.shape, s.dtype) for s in scatter]
        + [jax.ShapeDtypeStruct((N_CHIP, *s.shape), s.dtype) for s in whole],
        [pltpu.SemaphoreType.DMA((n, 3)), pltpu.SemaphoreType.DMA((n, N_CHIP)), pltpu.SemaphoreType.DMA((n,))],
        start, finish)


def _pair_sum_scatter(parts, recvs, core, name):
    na = len(parts)
    _, r, cdim = parts[0].shape
    tr = min(r, 416 if r % 416 == 0 else 128)

    def body(core_ref, *refs):
        del core_ref
        for a in range(na):
            refs[2 * na + a][...] = (refs[a][...] + refs[na + a][...]).astype(BF16)

    blk = (None, tr, cdim)
    mine = pl.BlockSpec(blk, lambda k, i, core_ref: (2 * k + core_ref[0], i, 0))
    slot = pl.BlockSpec(blk, lambda k, i, core_ref: (k, i, 0))
    return pl.pallas_call(
        body, name=name,
        grid_spec=pltpu.PrefetchScalarGridSpec(
            num_scalar_prefetch=1, grid=(N_CHIP, r // tr),
            in_specs=[mine] * na + [slot] * na, out_specs=[slot] * na),
        out_shape=[jax.ShapeDtypeStruct((N_CHIP, r, cdim), BF16)] * na,
        compiler_params=_params("parallel", "parallel"))(core, *parts, *recvs)


def _pair_sum_whole(mine, recvs, name):
    na = len(mine)

    def body(*refs):
        for a in range(na):
            refs[2 * na + a][...] = refs[a][...] + refs[na + a][...]

    return pl.pallas_call(body, name=name, out_shape=[jax.ShapeDtypeStruct(m.shape, F32) for m in mine])(*mine, *recvs)


def _allreduce_small(pack, name):
    shape = pack.shape

    def body(x_ref, o_ref, sib_ref, chip_ref, send_sems, recv_sems):
        x, y, c = _place()
        own = 2 * x + y
        chips = [(1 - x, y), (x, 1 - y), (1 - x, 1 - y)]
        to_sibling = pltpu.make_async_remote_copy(
            src_ref=x_ref, dst_ref=sib_ref, send_sem=send_sems.at[0], recv_sem=recv_sems.at[0],
            device_id=(x, y, 1 - c), device_id_type=MESH)
        to_sibling.start()
        to_sibling.wait()
        chip_ref[own] = x_ref[...] + sib_ref[...]
        sent = [pltpu.make_async_remote_copy(
            src_ref=chip_ref.at[own], dst_ref=chip_ref.at[own], send_sem=send_sems.at[1 + j],
            recv_sem=recv_sems.at[1 + own], device_id=(*chip, c), device_id_type=MESH) for j, chip in enumerate(chips)]
        for cp in sent:
            cp.start()
        for chip in chips:
            k = 2 * chip[0] + chip[1]
            pltpu.make_async_remote_copy(
                src_ref=chip_ref.at[k], dst_ref=chip_ref.at[k], send_sem=send_sems.at[1],
                recv_sem=recv_sems.at[1 + k], device_id=(*chip, c), device_id_type=MESH).wait_recv()
        for cp in sent:
            cp.wait_send()
        o_ref[...] = (chip_ref[0] + chip_ref[1]) + (chip_ref[2] + chip_ref[3])

    return pl.pallas_call(
        body, name=name, out_shape=jax.ShapeDtypeStruct(shape, F32),
        in_specs=[pl.BlockSpec(memory_space=pltpu.VMEM)], out_specs=pl.BlockSpec(memory_space=pltpu.VMEM),
        scratch_shapes=[pltpu.VMEM(shape, F32), pltpu.VMEM((N_CHIP, *shape), F32),
                        pltpu.SemaphoreType.DMA((4,)), pltpu.SemaphoreType.DMA((1 + N_CHIP,))],
    )(pack)


def _adamw(g, w, m, v):
    m = ADAM_B1 * m + (1.0 - ADAM_B1) * g
    v = ADAM_B2 * v + (1.0 - ADAM_B2) * (g * g)
    m_hat = m / (1.0 - ADAM_B1 ** ADAM_STEP)
    v_hat = v / (1.0 - ADAM_B2 ** ADAM_STEP)
    delta = -ADAM_LR * (m_hat / (jnp.sqrt(v_hat) + ADAM_EPS) + ADAM_WD * w)
    return delta, m, v


def _adam_parts(parts, w, m, v, name, tr=None):
    npart, r, c = parts.shape
    tr = r if tr is None else min(tr, r)

    def body(p_ref, w_ref, m_ref, v_ref, g_ref, d_ref, nm_ref, nv_ref):
        g = p_ref[0].astype(F32)
        for k in range(1, npart):
            g = g + p_ref[k].astype(F32)
        g_ref[...] = g
        d_ref[...], nm_ref[...], nv_ref[...] = _adamw(g, w_ref[...], m_ref[...], v_ref[...])

    tile = pl.BlockSpec((tr, c), lambda i: (i, 0))
    return pl.pallas_call(
        body, name=name, grid=(r // tr,),
        in_specs=[pl.BlockSpec((npart, tr, c), lambda i: (0, i, 0)), tile, tile, tile],
        out_specs=[tile] * 4, out_shape=[jax.ShapeDtypeStruct((r, c), F32)] * 4,
        compiler_params=_params("parallel"))(parts, w, m, v)


def _block_diag(w):
    w4 = w.reshape(N_GROUPS, 4, RNN_BLOCK_W, RNN_BLOCK_W)
    eye = jnp.eye(4, dtype=w.dtype)
    return jnp.einsum("gbij,bc->gbicj", w4, eye).reshape(N_GROUPS, GROUP_W, GROUP_W).astype(BF16)


SMALL_ROWS = 16
ROW_PRE_G, ROW_BGATE, ROW_CONV_B, ROW_B_A, ROW_B_X, ROW_LAM, ROW_POST_G, ROW_LOSS, ROW_SINKS, ROW_CONV_W = 0, 1, 3, 4, 5, 6, 7, 8, 9, 10


def _pack_stats(st_pre, st_merge, st_rnn, st_post, st_sink, name):
    d = D_MODEL

    def body(pre_ref, mg_ref, rnn_ref, post_ref, sink_ref, o_ref):
        rnn = rnn_ref[...]
        sinks = jnp.concatenate([sink_ref[0:1, :], jnp.zeros((1, d - LANE), F32)], axis=1)
        o_ref[0:8, :] = _rows8([pre_ref[0:1, :], mg_ref[0:1, :], mg_ref[1:2, :], rnn[0:1], rnn[1:2], rnn[2:3], rnn[3:4],
                                post_ref[0:1, :]], d)
        o_ref[8:16, :] = _rows8([post_ref[1:2, :], sinks, rnn[4:5], rnn[5:6], rnn[6:7], rnn[7:8]], d)

    return pl.pallas_call(body, name=name, out_shape=jax.ShapeDtypeStruct((SMALL_ROWS, d), F32))(
        st_pre, st_merge, st_rnn, st_post, st_sink)


def _adam_small(total, w, m, v, name):
    d = D_MODEL
    n_in = len(w)

    def pack(refs):
        pre, bg, cbias, ba, bx, lam, post, sinks = [r[...] for r in refs]
        top = _rows8([pre, bg[:, 0:d], bg[:, d:2 * d], cbias, ba, bx, lam, post], d)
        return jnp.concatenate([top, _rows8([jnp.zeros((1, d), F32), sinks], d)], axis=0)

    def body(*refs):
        p_ref = refs[0]
        w_refs, m_refs, v_refs = (refs[1 + k * n_in:1 + (k + 1) * n_in] for k in range(3))
        outs = refs[1 + 3 * n_in:]
        g = p_ref[...]
        res = (g,) + _adamw(g, pack(w_refs), pack(m_refs), pack(v_refs))
        for k in range(4):
            outs[k][...] = res[k]
            outs[4 + k][...] = jnp.concatenate([res[k][ROW_BGATE:ROW_BGATE + 1], res[k][ROW_BGATE + 1:ROW_BGATE + 2]], axis=1)

    return pl.pallas_call(
        body, name=name,
        out_shape=[jax.ShapeDtypeStruct((SMALL_ROWS, d), F32)] * 4 + [jax.ShapeDtypeStruct((1, 2 * d), F32)] * 4,
    )(total, *w, *m, *v)


def kernel(x, pre_norm_g, w_in, b_gate, conv_w, conv_b, w_rg_a, b_rg_a, w_rg_x, b_rg_x, lru_lambda, attn_sinks, w_rnn_out, w_attn_out, w_out, post_norm_g, loss_target, m_pre_norm_g, m_w_in, m_b_gate, m_conv_w, m_conv_b, m_w_rg_a, m_b_rg_a, m_w_rg_x, m_b_rg_x, m_lru_lambda, m_attn_sinks, m_w_rnn_out, m_w_attn_out, m_w_out, m_post_norm_g, v_pre_norm_g, v_w_in, v_b_gate, v_conv_w, v_conv_b, v_w_rg_a, v_b_rg_a, v_w_rg_x, v_b_rg_x, v_lru_lambda, v_attn_sinks, v_w_rnn_out, v_w_attn_out, v_w_out, v_post_norm_g):
    cx, cy, cc = _place()
    dev = 4 * cx + 2 * cy + cc
    core = jnp.reshape(cc, (1,)).astype(jnp.int32)

    w_in_t, m_in_t, v_in_t = (jnp.transpose(a[0]) for a in (w_in, m_w_in, v_w_in))
    wt_shard = w_in_t.astype(BF16)

    def project(xs, pre_g):
        h, rx_rg, qkv, ag_ml, wt_all = _gather_project(xs, pre_g, wt_shard, "gather_project")
        return h, rx_rg, qkv, ag_ml, wt_all.reshape(D_IN, D_MODEL)

    heads = jnp.arange(1, N_Q_HEADS + 1, dtype=F32)
    slopes = jnp.exp2(-ALIBI_MAX_BIAS * heads / N_Q_HEADS)
    b_a = b_rg_a.reshape(1, D_RNN)
    b_x = b_rg_x.reshape(1, D_RNN)
    p = dict(
        pre_g=pre_norm_g, post_g=post_norm_g, b_gate=b_gate, cb=conv_b,
        wbd_a=_block_diag(w_rg_a[0]), b_a=b_a, wbd_x=_block_diag(w_rg_x[0]), b_x=b_x, lam=lru_lambda,
        sm=jnp.pad(attn_sinks, ((0, 1), (0, 0))) + jnp.pad(slopes[None, :], ((1, 0), (0, 0))))

    def late_unpack(landed):
        w_rnn_all, w_attn_all, w_out_all, cw_all = landed
        return dict(w_rnn=w_rnn_all.reshape(D_RNN, D_MODEL), w_attn=w_attn_all.reshape(D_MODEL, D_MODEL),
                    w_out=w_out_all.reshape(D_MODEL, D_MODEL), cw=jnp.transpose(cw_all, (1, 0, 2)).reshape(4, D_RNN))

    late_weights = (_gather_ride([w_rnn_out[0].astype(BF16), w_attn_out[0].astype(BF16), w_out[0].astype(BF16), conv_w[0]]),
                    late_unpack)
    flat = (RNN_BLOCKS * RNN_BLOCK_W, RNN_BLOCK_W)

    out_scatter = []

    def out_sibling(gw_rnn, gw_attn, gw_out):
        out_scatter.extend(gw.reshape(N_DEV, SHARD_OUT, D_MODEL) for gw in (gw_rnn, gw_attn, gw_out))
        return _sibling_ride(out_scatter, [])

    def out_chips(from_sibling, g_rg_a, g_rg_x):
        whole = [g_rg_a.reshape(flat), g_rg_x.reshape(flat)]
        rg_sibling = _run_ride(_sibling_ride([], whole), "sibling_rg")
        return _chips_ride(_pair_sum_scatter(out_scatter, from_sibling, core, "pair_out"),
                           _pair_sum_whole(whole, rg_sibling, "pair_rg"))

    def reduce_in(gwt):
        scatter = [gwt.reshape(N_DEV, SHARD_IN, D_MODEL)]
        from_sibling = _run_ride(_sibling_ride(scatter, []), "sibling_in")
        return _chips_ride(_pair_sum_scatter(scatter, from_sibling, core, "pair_in"), [])

    g = _local_grads(x[0], loss_target[0], p, project, late_weights, (out_sibling, out_chips), reduce_in)
    small = _allreduce_small(
        _pack_stats(g["st_pre"], g["st_merge"], g["st_rnn"], g["st_post"], g["st_sink"], "pack_stats"), "allreduce_small")

    out = {}
    red = g["red_out"]
    out["w_in"] = [jnp.transpose(o) for o in _adam_parts(g["red_in"][0], w_in_t, m_in_t, v_in_t, "adam_w_in", tr=SHARD_IN // 2)]
    out["w_rnn_out"] = _adam_parts(red[0], w_rnn_out[0], m_w_rnn_out[0], v_w_rnn_out[0], "adam_w_rnn_out")
    out["w_attn_out"] = _adam_parts(red[1], w_attn_out[0], m_w_attn_out[0], v_w_attn_out[0], "adam_w_attn_out")
    out["w_out"] = _adam_parts(red[2], w_out[0], m_w_out[0], v_w_out[0], "adam_w_out")
    out["w_rg_a"] = _adam_parts(red[3], w_rg_a.reshape(flat), m_w_rg_a.reshape(flat), v_w_rg_a.reshape(flat), "adam_w_rg_a", tr=256)
    out["w_rg_x"] = _adam_parts(red[4], w_rg_x.reshape(flat), m_w_rg_x.reshape(flat), v_w_rg_x.reshape(flat), "adam_w_rg_x", tr=256)

    def rows(pre, bg, cbias, ba, bx, lam, post, sinks):
        return (pre, bg, cbias, ba.reshape(1, D_RNN), bx.reshape(1, D_RNN), lam, post,
                jnp.pad(sinks, ((0, 0), (0, D_MODEL - N_Q_HEADS))))

    small_out = _adam_small(
        small,
        rows(pre_norm_g, b_gate, conv_b, b_rg_a, b_rg_x, lru_lambda, post_norm_g, attn_sinks),
        rows(m_pre_norm_g, m_b_gate, m_conv_b, m_b_rg_a, m_b_rg_x, m_lru_lambda, m_post_norm_g, m_attn_sinks),
        rows(v_pre_norm_g, v_b_gate, v_conv_b, v_b_rg_a, v_b_rg_x, v_lru_lambda, v_post_norm_g, v_attn_sinks),
        "adam_small")
    packed, bgate_out = small_out[:4], small_out[4:]
    g_cw = lax.dynamic_slice(packed[0][ROW_CONV_W:ROW_CONV_W + 4], (0, dev * SHARD_OUT), (4, SHARD_OUT))
    out["conv_w"] = _adam_parts(g_cw[None], conv_w[0], m_conv_w[0], v_conv_w[0], "adam_conv_w")

    def unpack(kind, name):
        if name == "b_gate":
            return bgate_out[kind]
        row = dict(pre_norm_g=ROW_PRE_G, conv_b=ROW_CONV_B, b_rg_a=ROW_B_A, b_rg_x=ROW_B_X, lru_lambda=ROW_LAM,
                   post_norm_g=ROW_POST_G, attn_sinks=ROW_SINKS)[name]
        r = packed[kind][row:row + 1]
        if name == "attn_sinks":
            return r[:, 0:N_Q_HEADS]
        if name in ("b_rg_a", "b_rg_x"):
            return r.reshape(1, RNN_BLOCKS, RNN_BLOCK_W)
        return r

    shapes = dict(w_in=(1, D_MODEL, SHARD_IN), w_rnn_out=(1, SHARD_OUT, D_MODEL), w_attn_out=(1, SHARD_OUT, D_MODEL),
                  w_out=(1, SHARD_OUT, D_MODEL), w_rg_a=(1, RNN_BLOCKS, RNN_BLOCK_W, RNN_BLOCK_W),
                  w_rg_x=(1, RNN_BLOCKS, RNN_BLOCK_W, RNN_BLOCK_W), conv_w=(1, 4, SHARD_OUT))
    weights = ["pre_norm_g", "w_in", "b_gate", "conv_w", "conv_b", "w_rg_a", "b_rg_a", "w_rg_x", "b_rg_x",
               "lru_lambda", "attn_sinks", "w_rnn_out", "w_attn_out", "w_out", "post_norm_g"]
    results = []
    for kind in range(4):
        for name in weights:
            if name in out:
                results.append(out[name][kind].reshape(shapes[name]))
            else:
                results.append(unpack(kind, name))
    loss = 0.5 / D_MODEL * jnp.sum(packed[0][ROW_LOSS])
    return (loss, g["grad_x"][None], *results)
```

```python
import functools

import jax
import jax.numpy as jnp
from jax import lax
from jax.experimental import pallas as pl
from jax.experimental.pallas import tpu as pltpu

F32 = jnp.float32
BF16 = jnp.bfloat16

D_MODEL = 1024
D_RNN = 1024
RNN_BLOCKS = 16
RNN_BLOCK_W = 64
LRU_C = 8.0
N_Q_HEADS = 16
HEAD_DIM = 64
D_KV = 256
BLOCK = 128
ALIBI_MAX_BIAS = 8.0
EPS = 1e-6
D_IN = 6656
N_DEV = 8
N_CHIP = 4
SHARD_IN = D_IN // N_DEV
SHARD_OUT = D_MODEL // N_DEV
ATTN_SCALE = HEAD_DIM ** -0.5
MASKED = -1e30

ADAM_LR = 0.001
ADAM_B1 = 0.9
ADAM_B2 = 0.999
ADAM_EPS = 1e-08
ADAM_WD = 0.01
ADAM_STEP = 10

VMEM_LIMIT_BYTES = 52 * 1024 * 1024
LANE = 128
GROUP_W = 256
N_GROUPS = D_RNN // GROUP_W
SEG_CHUNK = 512

NT_DIMS = (((1,), (1,)), ((), ()))
TN_DIMS = (((0,), (0,)), ((), ()))
MESH = pl.DeviceIdType.MESH
ANY = pl.BlockSpec(memory_space=pl.ANY)


def _params(*semantics):
    return pltpu.CompilerParams(dimension_semantics=semantics, vmem_limit_bytes=VMEM_LIMIT_BYTES)


def _sigmoid(x):
    return 0.5 * jnp.tanh(0.5 * x) + 0.5


def _log1p(e):
    u = 1.0 + e
    den = jnp.where(u == 1.0, 1.0, u - 1.0)
    return jnp.where(u == 1.0, e, jnp.log(u) * (e / den))


def _softplus(z):
    return jnp.maximum(z, 0.0) + _log1p(jnp.exp(-jnp.abs(z)))


def _rows8(rows, width):
    idx = lax.broadcasted_iota(jnp.int32, (8, width), 0)
    out = jnp.zeros((8, width), F32)
    for r, v in enumerate(rows):
        out = jnp.where(idx == r, v, out)
    return out


class _Ride:
    def __init__(self, arrays, out_shapes, scratch_shapes, start, finish, middle=None):
        self.arrays, self.out_shapes, self.scratch_shapes = list(arrays), list(out_shapes), list(scratch_shapes)
        self.start, self.finish, self.middle = start, finish, middle


class _Hosted:
    def __init__(self, ride, n_in, n_out, n_scratch=0):
        self.ride = ride
        self.sizes = (n_in, len(ride.arrays) if ride else 0, n_out, len(ride.out_shapes) if ride else 0, n_scratch)
        self.arrays = ride.arrays if ride else []
        self.in_specs = [ANY] * len(self.arrays)
        self.out_shapes = ride.out_shapes if ride else []
        self.out_specs = [ANY] * len(self.out_shapes)
        self.scratch_shapes = ride.scratch_shapes if ride else []

    def split(self, refs):
        n_in, r_in, n_out, r_out, n_scr = self.sizes
        cuts = [0, n_in, n_in + r_in, n_in + r_in + n_out, n_in + r_in + n_out + r_out, n_in + r_in + n_out + r_out + n_scr]
        host_in, ride_in, host_out, ride_out, host_scr = (refs[cuts[k]:cuts[k + 1]] for k in range(5))
        ride_scr = refs[cuts[5]:]

        def start(when):
            if self.ride is not None:
                pl.when(when)(lambda: self.ride.start(ride_in, ride_out, ride_scr))

        def finish(when):
            if self.ride is not None:
                pl.when(when)(lambda: self.ride.finish(ride_in, ride_out, ride_scr))

        def middle(when):
            if self.ride is not None and self.ride.middle is not None:
                pl.when(when)(lambda: self.ride.middle(ride_in, ride_out, ride_scr))

        start.middle = middle
        return tuple(host_in) + tuple(host_out) + tuple(host_scr), start, finish

    def results(self, outs, n_out):
        outs = list(outs) if isinstance(outs, (list, tuple)) else [outs]
        return outs[:n_out], outs[n_out:]


def _run_ride(ride, name):
    n_in, n_out = len(ride.arrays), len(ride.out_shapes)

    def body(*refs):
        ins, outs, sems = refs[:n_in], refs[n_in:n_in + n_out], refs[n_in + n_out:]
        ride.start(ins, outs, sems)
        ride.finish(ins, outs, sems)

    return pl.pallas_call(
        body, name=name, in_specs=[ANY] * n_in, out_specs=[ANY] * n_out,
        out_shape=ride.out_shapes, scratch_shapes=ride.scratch_shapes)(*ride.arrays)


def _load_resident(w_hbm, w_vmem, sems, first):
    def piece(c):
        rows = pl.ds(c * SEG_CHUNK, SEG_CHUNK)
        return pltpu.make_async_copy(w_hbm.at[rows], w_vmem.at[rows], sems.at[c])

    @pl.when(first)
    def _():
        for c in range(w_vmem.shape[0] // SEG_CHUNK):
            piece(c).start()

    def ready(c):
        @pl.when(first)
        def _():
            piece(c).wait()

    return ready


CHIP_ROWS = 2 * SHARD_IN
PROJ_WIDTHS = (2 * D_RNN, D_MODEL + 2 * D_KV, 3 * D_MODEL)
PROJ_DTYPES = (F32, BF16, BF16)


def _chip_pieces():
    starts = [0, PROJ_WIDTHS[0], PROJ_WIDTHS[0] + PROJ_WIDTHS[1], D_IN]
    pieces = []
    for k in range(N_CHIP):
        lo, hi = k * CHIP_ROWS, (k + 1) * CHIP_ROWS
        cur = []
        for a in range(len(PROJ_WIDTHS)):
            s0, s1 = max(lo, starts[a]), min(hi, starts[a + 1])
            if s0 < s1:
                cur.append((a, s0 - starts[a], s1 - s0, s0 - lo))
        pieces.append(cur)
    return pieces


def _gather_project(x, g, wt_shard, name, tm=512):
    t, k = x.shape
    tm = min(tm, t)
    nt = t // tm
    pieces = _chip_pieces()

    def body(x_ref, g_ref, shard_ref, h_out, rx_ref, qkv_ref, ag_ref, wt_all,
             w_c, stage, o32, o16, h_all, send_sems, recv_sems, local_sem, stage_sems, out_sems, h_sems):
        s, ti = pl.program_id(0), pl.program_id(1)
        px, py, pc = _place()
        me, sibling = (px, py, pc), (px, py, 1 - pc)
        chips = [(px, py), (1 - px, py), (px, 1 - py), (1 - px, 1 - py)]
        outs = (rx_ref, qkv_ref, ag_ref)

        def slot(dev):
            return wt_all.at[4 * dev[0] + 2 * dev[1] + dev[2]]

        def copy(kk, block, to, src=None):
            return pltpu.make_async_remote_copy(
                src_ref=slot(block) if src is None else src, dst_ref=slot(block),
                send_sem=send_sems.at[kk], recv_sem=recv_sems.at[kk], device_id=to, device_id_type=MESH)

        mine = pltpu.make_async_copy(shard_ref, slot(me), local_sem)
        first = [copy(0, me, sibling, src=shard_ref)] + [copy(1 + j, me, (*chips[1 + j], pc), src=shard_ref) for j in range(3)]
        passed = [copy(4 + j, (*chips[1 + j], pc), sibling) for j in range(3)]

        @pl.when((s == 0) & (ti == 0))
        def _():
            mine.start()
            for cp in first:
                cp.start()

        for step in range(N_CHIP):
            @pl.when((s == step) & (ti == 0))
            def _(step=step):
                chip = chips[step]
                if step == 0:
                    mine.wait()
                    copy(0, sibling, me).wait_recv()
                else:
                    copy(step, (*chip, pc), me).wait_recv()
                    passed[step - 1].start()
                    copy(3 + step, (*chip, 1 - pc), me).wait_recv()
                loads = [pltpu.make_async_copy(slot((*chip, core)), stage.at[pl.ds(core * SHARD_IN, SHARD_IN)], stage_sems.at[core])
                         for core in (0, 1)]
                for cp in loads:
                    cp.start()
                for cp in loads:
                    cp.wait()
                w_c[...] = stage[...].T

        rows = pl.ds(pl.multiple_of(ti * tm, tm), tm)

        @pl.when(s == 0)
        def _():
            xv = x_ref[...]
            h_all[rows, :] = (xv * lax.rsqrt(jnp.mean(xv * xv, axis=-1, keepdims=True) + EPS) * g_ref[...]).astype(BF16)

        res = jnp.dot(h_all[rows, :], w_c[...], preferred_element_type=F32)

        n = s * nt + ti
        buf = lax.rem(n, 2)
        chip_idx = [2 * cx + cy for cx, cy in chips]

        def chip_at(step):
            return jnp.where(step == 0, chip_idx[0], jnp.where(step == 1, chip_idx[1], jnp.where(step == 2, chip_idx[2], chip_idx[3])))

        def h_write(b, tile):
            return pltpu.make_async_copy(h_all.at[pl.ds(tile * tm, tm)], h_out.at[pl.ds(tile * tm, tm)], h_sems.at[b])

        def writes(kchip, b, tile):
            cps = []
            for idx, (a, col, w, src) in enumerate(pieces[kchip]):
                staged = (o16 if PROJ_DTYPES[a] == BF16 else o32).at[b, :, pl.ds(src, w)]
                cps.append(pltpu.make_async_copy(staged, outs[a].at[pl.ds(tile * tm, tm), pl.ds(col, w)], out_sems.at[b, idx]))
            return cps

        o32[buf] = res
        o16[buf] = res.astype(BF16)
        kcur = chip_at(s)

        @pl.when(n > 0)
        def _():
            kprev = chip_at(lax.div(n - 1, nt))
            for kchip in range(N_CHIP):
                @pl.when(kprev == kchip)
                def _(kchip=kchip):
                    for cp in writes(kchip, 1 - buf, lax.rem(n - 1, nt)):
                        cp.wait()

            @pl.when(n <= nt)
            def _():
                h_write(1 - buf, n - 1).wait()

        @pl.when(s == 0)
        def _():
            h_write(buf, ti).start()

        for kchip in range(N_CHIP):
            @pl.when(kcur == kchip)
            def _(kchip=kchip):
                for cp in writes(kchip, buf, ti):
                    cp.start()

        @pl.when(n == N_CHIP * nt - 1)
        def _():
            for kchip in range(N_CHIP):
                @pl.when(kcur == kchip)
                def _(kchip=kchip):
                    for cp in writes(kchip, buf, ti):
                        cp.wait()
            for cp in first + passed:
                cp.wait_send()

    tile = pl.BlockSpec((tm, k), lambda s, ti: (jnp.where(s == 0, ti, nt - 1), 0))
    return pl.pallas_call(
        body, name=name, grid=(N_CHIP, nt),
        in_specs=[tile, pl.BlockSpec((1, k), lambda s, ti: (0, 0)), ANY],
        out_specs=[ANY, ANY, ANY, ANY, ANY],
        out_shape=[jax.ShapeDtypeStruct((t, k), BF16)]
        + [jax.ShapeDtypeStruct((t, w), dt) for w, dt in zip(PROJ_WIDTHS, PROJ_DTYPES)]
        + [jax.ShapeDtypeStruct((N_DEV, SHARD_IN, k), BF16)],
        scratch_shapes=[pltpu.VMEM((k, CHIP_ROWS), BF16), pltpu.VMEM((CHIP_ROWS, k), BF16),
                        pltpu.VMEM((2, tm, CHIP_ROWS), F32), pltpu.VMEM((2, tm, CHIP_ROWS), BF16), pltpu.VMEM((t, k), BF16),
                        pltpu.SemaphoreType.DMA((7,)), pltpu.SemaphoreType.DMA((7,)), pltpu.SemaphoreType.DMA,
                        pltpu.SemaphoreType.DMA((2,)), pltpu.SemaphoreType.DMA((2, 2)), pltpu.SemaphoreType.DMA((2,))],
        compiler_params=_params("arbitrary", "arbitrary"))(x, g, wt_shard)


def _mm_tn(a, b, name, tm=512, tk=4096):
    ktok, m = a.shape
    n = b.shape[1]
    tk = min(tk, ktok)

    def body(a_ref, b_ref, o_ref):
        @pl.when(pl.program_id(1) == 0)
        def _():
            o_ref[...] = jnp.zeros_like(o_ref)

        o_ref[...] += lax.dot_general(a_ref[...], b_ref[...], TN_DIMS, preferred_element_type=F32)

    return pl.pallas_call(
        body, name=name, grid=(m // tm, ktok // tk),
        in_specs=[pl.BlockSpec((tk, tm), lambda i, kk: (kk, i)), pl.BlockSpec((tk, n), lambda i, kk: (kk, 0))],
        out_specs=pl.BlockSpec((tm, n), lambda i, kk: (i, 0)),
        out_shape=jax.ShapeDtypeStruct((m, n), F32),
        compiler_params=_params("parallel", "arbitrary"))(a, b)


def _segment_chunks(segs):
    bounds = [0]
    for s in segs:
        bounds.append(bounds[-1] + s.shape[1] // SEG_CHUNK)
    return bounds


def _input_grad(segs, wt, x, g, dy, name, tm=512, ride=None):
    m = segs[0].shape[0]
    rows, n = wt.shape
    tm = min(tm, m)
    bounds = _segment_chunks(segs)
    n_seg = len(segs)
    ni = m // tm
    host = _Hosted(ride, n_seg + 4, 2, 2)

    def body(*refs):
        host_refs, start, finish = host.split(refs)
        a_refs = host_refs[:n_seg]
        wt_hbm, x_ref, g_ref, dy_ref, gx_ref, st_ref, wt_vmem, sems = host_refs[n_seg:]
        i = pl.program_id(0)
        start(i == 0)

        @pl.when(i == 0)
        def _():
            st_ref[...] = jnp.zeros_like(st_ref)

        ready = _load_resident(wt_hbm, wt_vmem, sems, i == 0)
        dh = None
        for s in range(n_seg):
            for c in range(bounds[s], bounds[s + 1]):
                ready(c)
            part = jnp.dot(a_refs[s][...], wt_vmem[bounds[s] * SEG_CHUNK:bounds[s + 1] * SEG_CHUNK, :], preferred_element_type=F32)
            dh = part if dh is None else dh + part
        xv = x_ref[...]
        r = lax.rsqrt(jnp.mean(xv * xv, axis=-1, keepdims=True) + EPS)
        xn = xv * r
        dxn = dh * g_ref[...]
        gx_ref[...] = dy_ref[...] + r * (dxn - xn * jnp.mean(dxn * xn, axis=-1, keepdims=True))
        st_ref[...] += _rows8([jnp.sum(dh * xn, axis=0, keepdims=True)], n)
        finish(i == ni - 1)

    tile = pl.BlockSpec((tm, n), lambda i: (i, 0))
    outs = pl.pallas_call(
        body, name=name, grid=(ni,),
        in_specs=[pl.BlockSpec((tm, sg.shape[1]), lambda i: (i, 0)) for sg in segs]
        + [ANY, tile, pl.BlockSpec((1, n), lambda i: (0, 0)), tile] + host.in_specs,
        out_specs=[tile, pl.BlockSpec((8, n), lambda i: (0, 0))] + host.out_specs,
        out_shape=[jax.ShapeDtypeStruct((m, n), F32), jax.ShapeDtypeStruct((8, n), F32)] + host.out_shapes,
        scratch_shapes=[pltpu.VMEM((rows, n), wt.dtype), pltpu.SemaphoreType.DMA((rows // SEG_CHUNK,))] + host.scratch_shapes,
        compiler_params=_params("arbitrary"))(*segs, wt, x, g, dy, *host.arrays)
    res, landed = host.results(outs, 2)
    return (*res, landed) if ride else tuple(res)


def _mm_tn_seg(segs, b, name):
    ktok = segs[0].shape[0]
    n = b.shape[1]
    bounds = _segment_chunks(segs)
    n_seg = len(segs)
    nc = bounds[-1]
    seg_of = [s for s in range(n_seg) for _ in range(bounds[s], bounds[s + 1])]

    def body(*refs):
        a_hbm, b_hbm, o_ref = refs[:n_seg], refs[n_seg], refs[n_seg + 1]
        a_buf, b_vmem, a_sems, b_sem = refs[n_seg + 2:]
        c = pl.program_id(0)

        def fetch(cc):
            s = seg_of[cc]
            cols = pl.ds((cc - bounds[s]) * SEG_CHUNK, SEG_CHUNK)
            return pltpu.make_async_copy(a_hbm[s].at[:, cols], a_buf.at[cc % 2], a_sems.at[cc % 2])

        @pl.when(c == 0)
        def _():
            whole = pltpu.make_async_copy(b_hbm, b_vmem, b_sem)
            whole.start()
            fetch(0).start()
            whole.wait()

        for cc in range(nc):
            @pl.when(c == cc)
            def _(cc=cc):
                if cc + 1 < nc:
                    fetch(cc + 1).start()
                fetch(cc).wait()

        o_ref[...] = lax.dot_general(a_buf[c % 2], b_vmem[...], TN_DIMS, preferred_element_type=F32)

    return pl.pallas_call(
        body, name=name, grid=(nc,),
        in_specs=[ANY] * (n_seg + 1), out_specs=pl.BlockSpec((SEG_CHUNK, n), lambda c: (c, 0)),
        out_shape=jax.ShapeDtypeStruct((nc * SEG_CHUNK, n), F32),
        scratch_shapes=[pltpu.VMEM((2, ktok, SEG_CHUNK), segs[0].dtype), pltpu.VMEM((ktok, n), b.dtype),
                        pltpu.SemaphoreType.DMA((2,)), pltpu.SemaphoreType.DMA],
        compiler_params=_params("arbitrary"))(*segs, b)


def _branches_fwd(z_rnn, z_attn, ag_ml, b_gate, w_rnn, w_attn, w_out, x, target, g_post, name, tm=512):
    t, d = x.shape
    tm = min(tm, t)

    def body(zr_ref, za_ref, lr_ref, la_ref, br_ref, ba_ref, wr_ref, wa_ref, wo_ref, x_ref, t_ref, g_ref,
             brr_ref, bra_ref, mg_ref, do_ref, dy_ref, st_ref):
        @pl.when(pl.program_id(0) == 0)
        def _():
            st_ref[...] = jnp.zeros_like(st_ref)

        br_rnn = jnp.dot(zr_ref[...], wr_ref[...], preferred_element_type=F32)
        br_attn = jnp.dot(za_ref[...], wa_ref[...], preferred_element_type=F32)
        brr_ref[...] = br_rnn.astype(BF16)
        bra_ref[...] = br_attn.astype(BF16)
        g_rnn = _sigmoid(lr_ref[...].astype(F32) + br_ref[...])
        g_attn = _sigmoid(la_ref[...].astype(F32) + ba_ref[...])
        merged = (g_rnn * br_rnn + g_attn * br_attn).astype(BF16)
        mg_ref[...] = merged
        o = jnp.dot(merged, wo_ref[...], preferred_element_type=F32)
        g = g_ref[...]
        r = lax.rsqrt(jnp.mean(o * o, axis=-1, keepdims=True) + EPS)
        nrm = o * r
        err = x_ref[...] + nrm * g - t_ref[...]
        dy = err * (1.0 / d)
        dy_ref[...] = dy
        dn = dy * g
        do_ref[...] = (r * (dn - nrm * jnp.mean(dn * nrm, axis=-1, keepdims=True))).astype(BF16)
        st_ref[...] += _rows8([jnp.sum(dy * nrm, axis=0, keepdims=True), jnp.sum(err * err, axis=0, keepdims=True)], d)

    tile = pl.BlockSpec((tm, d), lambda i: (i, 0))
    weight = pl.BlockSpec((d, d), lambda i: (0, 0))
    bf = jax.ShapeDtypeStruct((t, d), BF16)
    return pl.pallas_call(
        body, name=name, grid=(t // tm,),
        in_specs=[tile, tile, pl.BlockSpec((tm, d), lambda i: (i, 1)), pl.BlockSpec((tm, d), lambda i: (i, 2)),
                  pl.BlockSpec((1, d), lambda i: (0, 0)), pl.BlockSpec((1, d), lambda i: (0, 1)),
                  weight, weight, weight, tile, tile, pl.BlockSpec((1, d), lambda i: (0, 0))],
        out_specs=[tile, tile, tile, tile, tile, pl.BlockSpec((8, d), lambda i: (0, 0))],
        out_shape=[bf, bf, bf, bf, jax.ShapeDtypeStruct((t, d), F32), jax.ShapeDtypeStruct((8, d), F32)],
        compiler_params=_params("arbitrary"))(z_rnn, z_attn, ag_ml, ag_ml, b_gate, b_gate, w_rnn, w_attn, w_out, x, target, g_post)


def _branches_bwd(dout, br_rnn, br_attn, ag_ml, b_gate, w_rnn, w_attn, w_out, name, tm=512):
    t, d = br_rnn.shape
    tm = min(tm, t)

    def body(do_ref, r_ref, a_ref, lr_ref, la_ref, br_ref, ba_ref, wr_ref, wa_ref, wo_ref,
             dr_ref, da_ref, dl_ref, dzr_ref, dza_ref, st_ref, wt_ref):
        @pl.when(pl.program_id(0) == 0)
        def _():
            st_ref[...] = jnp.zeros_like(st_ref)
            wt_ref[0] = wo_ref[...].T
            wt_ref[1] = wr_ref[...].T
            wt_ref[2] = wa_ref[...].T

        dm = jnp.dot(do_ref[...], wt_ref[0], preferred_element_type=F32)
        g_rnn = _sigmoid(lr_ref[...].astype(F32) + br_ref[...])
        g_attn = _sigmoid(la_ref[...].astype(F32) + ba_ref[...])
        dbr_rnn = (dm * g_rnn).astype(BF16)
        dbr_attn = (dm * g_attn).astype(BF16)
        dr_ref[...] = dbr_rnn
        da_ref[...] = dbr_attn
        dl_rnn = dm * r_ref[...].astype(F32) * g_rnn * (1.0 - g_rnn)
        dl_attn = dm * a_ref[...].astype(F32) * g_attn * (1.0 - g_attn)
        dl_ref[:, 0:d] = dl_rnn.astype(BF16)
        dl_ref[:, d:2 * d] = dl_attn.astype(BF16)
        st_ref[...] += _rows8([jnp.sum(dl_rnn, axis=0, keepdims=True), jnp.sum(dl_attn, axis=0, keepdims=True)], d)
        dzr_ref[...] = jnp.dot(dbr_rnn, wt_ref[1], preferred_element_type=F32).astype(BF16)
        dza_ref[...] = jnp.dot(dbr_attn, wt_ref[2], preferred_element_type=F32).astype(BF16)

    tile = pl.BlockSpec((tm, d), lambda i: (i, 0))
    weight = pl.BlockSpec((d, d), lambda i: (0, 0))
    bf = jax.ShapeDtypeStruct((t, d), BF16)
    return pl.pallas_call(
        body, name=name, grid=(t // tm,),
        in_specs=[tile, tile, tile, pl.BlockSpec((tm, d), lambda i: (i, 1)), pl.BlockSpec((tm, d), lambda i: (i, 2)),
                  pl.BlockSpec((1, d), lambda i: (0, 0)), pl.BlockSpec((1, d), lambda i: (0, 1)), weight, weight, weight],
        out_specs=[tile, tile, pl.BlockSpec((tm, 2 * d), lambda i: (i, 0)), tile, tile, pl.BlockSpec((8, d), lambda i: (0, 0))],
        out_shape=[bf, bf, jax.ShapeDtypeStruct((t, 2 * d), BF16), bf, bf, jax.ShapeDtypeStruct((8, d), F32)],
        scratch_shapes=[pltpu.VMEM((3, d, d), BF16)],
        compiler_params=_params("arbitrary"))(dout, br_rnn, br_attn, ag_ml, ag_ml, b_gate, b_gate, w_rnn, w_attn, w_out)


def _lru_gates(c, wa, ba, wx, bx, sp):
    cb = c.astype(BF16)
    r = _sigmoid(jnp.dot(cb, wa, preferred_element_type=F32) + ba)
    ig = _sigmoid(jnp.dot(cb, wx, preferred_element_type=F32) + bx)
    log_a = (-LRU_C) * r * sp
    a = jnp.exp(log_a)
    mult = jnp.sqrt(-jnp.tanh(log_a) * (a * a + 1.0))
    return cb, r, ig, a, mult


SUBLANES = 8


def _scan_fwd(a, u, carry, tt):
    w = a.shape[1]
    ng = tt // SUBLANES
    a3 = a.reshape(ng, SUBLANES, w)
    u3 = u.reshape(ng, SUBLANES, w)
    sub = lax.broadcasted_iota(jnp.int32, (ng, SUBLANES, w), 1)
    d = 1
    while d < SUBLANES:
        keep = sub >= d
        u3 = u3 + a3 * jnp.where(keep, pltpu.roll(u3, d, 1), 0.0)
        a3 = a3 * jnp.where(keep, pltpu.roll(a3, d, 1), 1.0)
        d *= 2
    out = []
    for g in range(ng):
        hg = u3[g] + a3[g] * carry
        out.append(hg)
        carry = hg[SUBLANES - 1:SUBLANES, :]
    return jnp.concatenate(out, axis=0)


def _scan_rev(b, g, carry, tt):
    w = b.shape[1]
    ng = tt // SUBLANES
    b3 = b.reshape(ng, SUBLANES, w)
    g3 = g.reshape(ng, SUBLANES, w)
    sub = lax.broadcasted_iota(jnp.int32, (ng, SUBLANES, w), 1)
    d = 1
    while d < SUBLANES:
        keep = sub < SUBLANES - d
        g3 = g3 + b3 * jnp.where(keep, pltpu.roll(g3, SUBLANES - d, 1), 0.0)
        b3 = b3 * jnp.where(keep, pltpu.roll(b3, SUBLANES - d, 1), 1.0)
        d *= 2
    out = [None] * ng
    for k in range(ng - 1, -1, -1):
        hk = g3[k] + b3[k] * carry
        out[k] = hk
        carry = hk[0:1, :]
    return jnp.concatenate(out, axis=0)


def _conv_taps(cw, bias, x, ext_ref, tt):
    x2 = ext_ref[7:7 + tt, :]
    x1 = ext_ref[6:6 + tt, :]
    x0 = ext_ref[5:5 + tt, :]
    c = bias + cw[3:4] * x + cw[2:3] * x2 + cw[1:2] * x1 + cw[0:1] * x0
    return c, x2, x1, x0


def _rnn_fwd(rx_rg, cw, cb, wa, ba, wx, bx, lam, name, tt=512, ride=None):
    t = rx_rg.shape[0]
    tt = min(tt, t)
    w = GROUP_W

    nt = t // tt
    host = _Hosted(ride, 9, 2, 2)

    def body(*refs):
        (rx_ref, rg_ref, cw_ref, cb_ref, wa_ref, ba_ref, wx_ref, bx_ref, lam_ref, y_ref, z_ref, ext_ref, hc_ref), start, finish = host.split(refs)
        start((pl.program_id(0) == 0) & (pl.program_id(1) == 0))
        start.middle((pl.program_id(0) == N_GROUPS - 1) & (pl.program_id(1) == 0))

        @pl.when(pl.program_id(1) == 0)
        def _():
            ext_ref[0:8, :] = jnp.zeros((8, w), F32)
            hc_ref[...] = jnp.zeros((8, w), F32)

        x = rx_ref[...]
        ext_ref[8:8 + tt, :] = x
        c, _, _, _ = _conv_taps(cw_ref[...], cb_ref[...], x, ext_ref, tt)
        ext_ref[0:8, :] = x[tt - 8:tt, :]
        sp = _softplus(-lam_ref[...])
        _, _, ig, a, mult = _lru_gates(c, wa_ref[...], ba_ref[...], wx_ref[...], bx_ref[...], sp)
        h = _scan_fwd(a, mult * (ig * c), hc_ref[7:8, :], tt)
        hc_ref[...] = h[tt - 8:tt, :]
        y_ref[...] = h
        rg = rg_ref[...]
        z_ref[...] = (h * rg * _sigmoid(rg)).astype(BF16)
        finish((pl.program_id(0) == N_GROUPS - 1) & (pl.program_id(1) == nt - 1))

    vec = pl.BlockSpec((1, w), lambda g, i: (0, g))
    mat = pl.BlockSpec((None, w, w), lambda g, i: (g, 0, 0))
    tile = pl.BlockSpec((tt, w), lambda g, i: (i, g))
    outs = pl.pallas_call(
        body, name=name, grid=(N_GROUPS, nt),
        in_specs=[tile, pl.BlockSpec((tt, w), lambda g, i: (i, N_GROUPS + g)),
                  pl.BlockSpec((4, w), lambda g, i: (0, g)), vec, mat, vec, mat, vec, vec] + host.in_specs,
        out_specs=[tile, tile] + host.out_specs,
        out_shape=[jax.ShapeDtypeStruct((t, D_RNN), F32), jax.ShapeDtypeStruct((t, D_RNN), BF16)] + host.out_shapes,
        scratch_shapes=[pltpu.VMEM((tt + 8, w), F32), pltpu.VMEM((8, w), F32)] + host.scratch_shapes,
        compiler_params=_params("arbitrary" if ride else "parallel", "arbitrary"))(
            rx_rg, rx_rg, cw, cb, wa, ba, wx, bx, lam, *host.arrays)
    res, landed = host.results(outs, 2)
    return (*res, landed) if ride else tuple(res)


def _rnn_bwd(rx_rg, y, dz, cw, cb, wa, ba, wx, bx, lam, name, tt=512, ride=None):
    t = rx_rg.shape[0]
    tt = min(tt, t)
    nt = t // tt
    w = GROUP_W

    host = _Hosted(ride, 13, 5, 6)

    def body(*refs):
        host_refs, start, finish = host.split(refs)
        (rx_ref, rg_ref, rxt_ref, y_ref, yt_ref, dz_ref, cw_ref, cb_ref, wa_ref, ba_ref, wx_ref, bx_ref, lam_ref,
         drx_ref, drg_ref, st_ref, gda_ref, gdx_ref, ext_ref, dcx_ref, wcar_ref, acar_ref, dwa_ref, dwx_ref) = host_refs
        ii = pl.program_id(1)
        start((pl.program_id(0) == 0) & (ii == 0))

        @pl.when(ii == 0)
        def _():
            wcar_ref[...] = jnp.zeros((8, w), F32)
            acar_ref[...] = jnp.zeros((8, w), F32)
            dcx_ref[tt:tt + 8, :] = jnp.zeros((8, w), F32)
            st_ref[...] = jnp.zeros_like(st_ref)
            dwa_ref[...] = jnp.zeros_like(dwa_ref)
            dwx_ref[...] = jnp.zeros_like(dwx_ref)

        has_prev = jnp.where(ii == nt - 1, 0.0, 1.0)
        x = rx_ref[...]
        ext_ref[0:8, :] = rxt_ref[...] * has_prev
        ext_ref[8:8 + tt, :] = x
        cwv = cw_ref[...]
        c, x2, x1, x0 = _conv_taps(cwv, cb_ref[...], x, ext_ref, tt)
        lam = lam_ref[...]
        sp = _softplus(-lam)
        wa = wa_ref[...]
        wx = wx_ref[...]
        cb16, r, ig, a, mult = _lru_gates(c, wa, ba_ref[...], wx, bx_ref[...], sp)

        rg = rg_ref[...]
        sg = _sigmoid(rg)
        dz = dz_ref[...].astype(F32)
        yv = y_ref[...]
        drg_ref[...] = (dz * yv * (sg * (1.0 + rg * (1.0 - sg)))).astype(BF16)

        row = lax.broadcasted_iota(jnp.int32, (tt, w), 0)
        b = jnp.where(row < tt - 1, pltpu.roll(a, tt - 1, 0), acar_ref[0:1, :])
        dh = _scan_rev(b, dz * (rg * sg), wcar_ref[0:1, :], tt)
        wcar_ref[...] = dh[0:8, :]
        acar_ref[...] = a[0:8, :]

        hprev = jnp.where(row >= 1, pltpu.roll(yv, 1, 0), yt_ref[7:8, :] * has_prev)
        dmult = dh * (ig * c)
        dig = dh * mult * c
        dlog_a = dh * hprev * a - dmult * (a * a / mult)
        dpa = dlog_a * ((-LRU_C) * sp) * r * (1.0 - r)
        dpx = dig * ig * (1.0 - ig)
        dsp = jnp.sum(dlog_a * r, axis=0, keepdims=True) * (-LRU_C)
        dlam = dsp * (-_sigmoid(-lam))
        dpa16 = dpa.astype(BF16)
        dpx16 = dpx.astype(BF16)
        dwa_ref[...] += lax.dot_general(cb16, dpa16, TN_DIMS, preferred_element_type=F32)
        dwx_ref[...] += lax.dot_general(cb16, dpx16, TN_DIMS, preferred_element_type=F32)
        dc = (dh * mult * ig
              + lax.dot_general(dpa16, wa, NT_DIMS, preferred_element_type=F32)
              + lax.dot_general(dpx16, wx, NT_DIMS, preferred_element_type=F32))

        dcx_ref[0:tt, :] = dc
        drx = (cwv[3:4] * dc + cwv[2:3] * dcx_ref[1:1 + tt, :] + cwv[1:2] * dcx_ref[2:2 + tt, :]
               + cwv[0:1] * dcx_ref[3:3 + tt, :])
        drx_ref[...] = drx.astype(BF16)
        dcx_ref[tt:tt + 8, :] = dc[0:8, :]

        def colsum(v):
            return jnp.sum(v, axis=0, keepdims=True)

        st_ref[...] += _rows8([colsum(dc), colsum(dpa), colsum(dpx), dlam,
                               colsum(dc * x0), colsum(dc * x1), colsum(dc * x2), colsum(dc * x)], w)

        @pl.when(ii == nt - 1)
        def _():
            for blk in range(GROUP_W // RNN_BLOCK_W):
                rows = slice(blk * RNN_BLOCK_W, (blk + 1) * RNN_BLOCK_W)
                gda_ref[blk] = dwa_ref[rows, rows]
                gdx_ref[blk] = dwx_ref[rows, rows]

        finish((pl.program_id(0) == N_GROUPS - 1) & (ii == nt - 1))

    def rev(ii):
        return nt - 1 - ii

    def tail(g, ii):
        return (jnp.maximum(rev(ii) * (tt // 8) - 1, 0), g)

    vec = pl.BlockSpec((1, w), lambda g, ii: (0, g))
    mat = pl.BlockSpec((None, w, w), lambda g, ii: (g, 0, 0))
    tile = pl.BlockSpec((tt, w), lambda g, ii: (rev(ii), g))
    diag_shape = (N_GROUPS, GROUP_W // RNN_BLOCK_W, RNN_BLOCK_W, RNN_BLOCK_W)
    diag = pl.BlockSpec((None,) + diag_shape[1:], lambda g, ii: (g, 0, 0, 0))
    outs = pl.pallas_call(
        body, name=name, grid=(N_GROUPS, nt),
        in_specs=[tile, pl.BlockSpec((tt, w), lambda g, ii: (rev(ii), N_GROUPS + g)), pl.BlockSpec((8, w), tail),
                  tile, pl.BlockSpec((8, w), tail), tile,
                  pl.BlockSpec((4, w), lambda g, ii: (0, g)), vec, mat, vec, mat, vec, vec] + host.in_specs,
        out_specs=[tile, tile, pl.BlockSpec((8, w), lambda g, ii: (0, g)), diag, diag] + host.out_specs,
        out_shape=[jax.ShapeDtypeStruct((t, D_RNN), BF16), jax.ShapeDtypeStruct((t, D_RNN), BF16),
                   jax.ShapeDtypeStruct((8, D_RNN), F32),
                   jax.ShapeDtypeStruct(diag_shape, F32), jax.ShapeDtypeStruct(diag_shape, F32)] + host.out_shapes,
        scratch_shapes=[pltpu.VMEM((tt + 8, w), F32), pltpu.VMEM((tt + 8, w), F32), pltpu.VMEM((8, w), F32),
                        pltpu.VMEM((8, w), F32), pltpu.VMEM((w, w), F32), pltpu.VMEM((w, w), F32)] + host.scratch_shapes,
        compiler_params=_params("arbitrary" if ride else "parallel", "arbitrary"))(
            rx_rg, rx_rg, rx_rg, y, y, dz, cw, cb, wa, ba, wx, bx, lam, *host.arrays)
    res, landed = host.results(outs, 5)
    return (*res, landed) if ride else tuple(res)


def _half_mask(shape, half):
    lane = lax.broadcasted_iota(jnp.int32, shape, 1)
    return (lane >= HEAD_DIM) if half else (lane < HEAD_DIM)


def _dup_half(t, half):
    sel = jnp.where(_half_mask(t.shape, half), t, 0.0)
    return sel + pltpu.roll(sel, HEAD_DIM, 1)


def _band_geometry(n):
    qi = lax.broadcasted_iota(jnp.int32, (BLOCK, 2 * BLOCK), 0)
    kj = lax.broadcasted_iota(jnp.int32, (BLOCK, 2 * BLOCK), 1)
    dist = BLOCK + qi - kj
    first_key = jnp.where(n > 0, 0, BLOCK)
    valid = (dist >= 0) & (dist < BLOCK) & (kj >= first_key)
    return dist.astype(F32), valid


GROUP = 4


def _kv_dup(prev_ref, cur_ref, hk, scale=1.0):
    tile = hk // 2
    kt = jnp.concatenate([prev_ref[:, tile * LANE:(tile + 1) * LANE], cur_ref[:, tile * LANE:(tile + 1) * LANE]], axis=0)
    return (_dup_half(kt.astype(F32), hk % 2) * scale).astype(BF16)


def _fill_bias(bias_ref, sm_ref, n):
    distf, valid = _band_geometry(n)
    for head in range(N_Q_HEADS):
        bias_ref[head] = jnp.where(valid, -sm_ref[1, head] * distf, MASKED)


def _head_scores(q2s, half, kdup, bias):
    qm = jnp.where(_half_mask(q2s.shape, half), q2s, jnp.zeros_like(q2s))
    return qm, lax.dot_general(qm, kdup, NT_DIMS, preferred_element_type=F32) + bias


def _attn_specs(nb, clamp_last):
    def blk(n):
        return jnp.minimum(n, nb - 1) if clamp_last else n

    q_spec = pl.BlockSpec((BLOCK, D_MODEL), lambda n: (blk(n), 0))
    k_prev = pl.BlockSpec((BLOCK, D_KV), lambda n: (jnp.maximum(blk(n) - 1, 0), D_MODEL // D_KV))
    k_cur = pl.BlockSpec((BLOCK, D_KV), lambda n: (blk(n), D_MODEL // D_KV))
    v_prev = pl.BlockSpec((BLOCK, D_KV), lambda n: (jnp.maximum(blk(n) - 1, 0), D_MODEL // D_KV + 1))
    v_cur = pl.BlockSpec((BLOCK, D_KV), lambda n: (blk(n), D_MODEL // D_KV + 1))
    return q_spec, k_prev, k_cur, v_prev, v_cur


def _attn_fwd(sm, qkv, ag_ml, name, ride=None):
    t = qkv.shape[0]
    nb = t // BLOCK

    host = _Hosted(ride, 7, 3, 1)

    def body(*refs):
        (sm_ref, q_ref, kp_ref, kc_ref, vp_ref, vc_ref, ag_ref, y_ref, z_ref, lse_ref, bias_ref), start, finish = host.split(refs)
        n = pl.program_id(0)
        start(n == 0)
        start.middle(n == (3 * nb) // 4)

        @pl.when(n <= 1)
        def _():
            _fill_bias(bias_ref, sm_ref, n)

        lane = lax.broadcasted_iota(jnp.int32, (BLOCK, LANE), 1)
        low = lane < HEAD_DIM
        lse = jnp.zeros((BLOCK, LANE), F32)
        for hk in range(N_Q_HEADS // GROUP):
            kdup = _kv_dup(kp_ref, kc_ref, hk)
            vdup = _kv_dup(vp_ref, vc_ref, hk)
            for k in (0, 1):
                c = slice((2 * hk + k) * LANE, (2 * hk + k + 1) * LANE)
                q2s = q_ref[:, c] * ATTN_SCALE
                outs = []
                for half in (0, 1):
                    head = GROUP * hk + 2 * k + half
                    _, s = _head_scores(q2s, half, kdup, bias_ref[head])
                    sink = sm_ref[0, head]
                    m = jnp.maximum(jnp.max(s, axis=1, keepdims=True), sink)
                    e = jnp.exp(s - m)
                    l = jnp.sum(e, axis=1, keepdims=True) + jnp.exp(sink - m)
                    outs.append(jnp.dot((e * (1.0 / l)).astype(BF16), vdup, preferred_element_type=F32))
                    lse = jnp.where(lane == head, m + jnp.log(l), lse)
                yt = jnp.where(low, outs[0], outs[1])
                y_ref[:, c] = yt
                ag = ag_ref[:, c].astype(F32)
                z_ref[:, c] = (yt * ag * _sigmoid(ag)).astype(BF16)
        lse_ref[...] = lse
        finish(n == nb - 1)

    q_spec, k_prev, k_cur, v_prev, v_cur = _attn_specs(nb, False)
    wide = pl.BlockSpec((BLOCK, D_MODEL), lambda n: (n, 0))
    outs = pl.pallas_call(
        body, name=name, grid=(nb,),
        in_specs=[pl.BlockSpec(memory_space=pltpu.SMEM), q_spec, k_prev, k_cur, v_prev, v_cur, wide] + host.in_specs,
        out_specs=[wide, wide, pl.BlockSpec((BLOCK, LANE), lambda n: (n, 0))] + host.out_specs,
        out_shape=[jax.ShapeDtypeStruct((t, D_MODEL), F32), jax.ShapeDtypeStruct((t, D_MODEL), BF16),
                   jax.ShapeDtypeStruct((t, LANE), F32)] + host.out_shapes,
        scratch_shapes=[pltpu.VMEM((N_Q_HEADS, BLOCK, 2 * BLOCK), F32)] + host.scratch_shapes,
        compiler_params=_params("arbitrary"))(sm, qkv, qkv, qkv, qkv, qkv, ag_ml, *host.arrays)
    res, landed = host.results(outs, 3)
    return (*res, landed) if ride else tuple(res)


def _attn_bwd(sm, qkv, ag_ml, y, lse, dz, name, ride=None):
    t = qkv.shape[0]
    nb = t // BLOCK
    host = _Hosted(ride, 10, 4, 3)

    def body(*refs):
        host_refs, start, finish = host.split(refs)
        (sm_ref, q_ref, kp_ref, kc_ref, vp_ref, vc_ref, ag_ref, y_ref, lse_ref, dz_ref,
         dq_ref, dkv_ref, dag_ref, ds_ref, ck_ref, cv_ref, bias_ref) = host_refs
        n = pl.program_id(0)
        start(n == 0)

        @pl.when(n == 0)
        def _():
            ck_ref[...] = jnp.zeros_like(ck_ref)
            cv_ref[...] = jnp.zeros_like(cv_ref)
            ds_ref[...] = jnp.zeros_like(ds_ref)

        @pl.when(n <= 1)
        def _():
            _fill_bias(bias_ref, sm_ref, n)

        @pl.when(n < nb)
        def _():
            lane8 = lax.broadcasted_iota(jnp.int32, (8, LANE), 1)
            row8 = lax.broadcasted_iota(jnp.int32, (8, LANE), 0)
            dsink = jnp.zeros((8, LANE), F32)
            dk_heads, dv_heads = [], []
            lse_tile = lse_ref[...]
            for hk in range(N_Q_HEADS // GROUP):
                kdup = _kv_dup(kp_ref, kc_ref, hk)
                ks = _kv_dup(kp_ref, kc_ref, hk, ATTN_SCALE)
                vdup = _kv_dup(vp_ref, vc_ref, hk)
                qms, dyhs, y_rows = [], [], []
                for k in (0, 1):
                    c = slice((2 * hk + k) * LANE, (2 * hk + k + 1) * LANE)
                    ag = ag_ref[:, c].astype(F32)
                    sg = _sigmoid(ag)
                    dzt = dz_ref[:, c].astype(F32)
                    yt = y_ref[:, c]
                    dag_ref[:, c] = (dzt * yt * (sg * (1.0 + ag * (1.0 - sg)))).astype(BF16)
                    dyt = dzt * (ag * sg)
                    q2s = q_ref[:, c] * ATTN_SCALE
                    for half in (0, 1):
                        hm = _half_mask(q2s.shape, half)
                        qms.append(jnp.where(hm, q2s, jnp.zeros_like(q2s)))
                        dyhs.append(jnp.where(hm, dyt, 0.0))
                        y_rows.append(yt)
                qm4 = jnp.concatenate(qms, axis=0)
                dy4 = jnp.concatenate(dyhs, axis=0)
                dy4_16 = dy4.astype(BF16)
                s4 = lax.dot_general(qm4, kdup, NT_DIMS, preferred_element_type=F32)
                dp4 = lax.dot_general(dy4_16, vdup, NT_DIMS, preferred_element_type=F32)
                probs16, ds16 = [], []
                for r in range(GROUP):
                    head = GROUP * hk + r
                    rows = slice(r * BLOCK, (r + 1) * BLOCK)
                    lh = lse_tile[:, head:head + 1]
                    probs = jnp.exp(s4[rows] + bias_ref[head] - lh)
                    psink = jnp.exp(sm_ref[0, head] - lh)
                    delta = jnp.sum(dyhs[r] * y_rows[r], axis=1, keepdims=True)
                    ds16.append((probs * (dp4[rows] - delta)).astype(BF16))
                    probs16.append(probs.astype(BF16))
                    dsink = dsink + jnp.where((row8 == 0) & (lane8 == head),
                                              -jnp.sum(psink * delta, axis=0, keepdims=True), 0.0)
                ds4 = jnp.concatenate(ds16, axis=0)
                p4 = jnp.concatenate(probs16, axis=0)
                dq4 = jnp.dot(ds4, ks, preferred_element_type=F32)
                low = _half_mask((BLOCK, LANE), 0)
                for k in (0, 1):
                    c = slice((2 * hk + k) * LANE, (2 * hk + k + 1) * LANE)
                    dq_ref[:, c] = jnp.where(low, dq4[2 * k * BLOCK:(2 * k + 1) * BLOCK],
                                             dq4[(2 * k + 1) * BLOCK:(2 * k + 2) * BLOCK]).astype(BF16)
                dk_acc = lax.dot_general(ds4, qm4, TN_DIMS, preferred_element_type=F32)
                dv_acc = lax.dot_general(p4, dy4_16, TN_DIMS, preferred_element_type=F32)
                dk_heads.append(dk_acc + pltpu.roll(dk_acc, HEAD_DIM, 1))
                dv_heads.append(dv_acc + pltpu.roll(dv_acc, HEAD_DIM, 1))
            ds_ref[...] += dsink
            low = _half_mask((2 * BLOCK, LANE), 0)
            for tile in range(2):
                cols = slice(tile * LANE, (tile + 1) * LANE)
                dkt = jnp.where(low, dk_heads[2 * tile], dk_heads[2 * tile + 1])
                dvt = jnp.where(low, dv_heads[2 * tile], dv_heads[2 * tile + 1])
                dkv_ref[:, cols] = (ck_ref[:, cols] + dkt[0:BLOCK, :]).astype(BF16)
                dkv_ref[:, D_KV + tile * LANE:D_KV + (tile + 1) * LANE] = (cv_ref[:, cols] + dvt[0:BLOCK, :]).astype(BF16)
                ck_ref[:, cols] = dkt[BLOCK:2 * BLOCK, :]
                cv_ref[:, cols] = dvt[BLOCK:2 * BLOCK, :]

        @pl.when(n == nb)
        def _():
            dkv_ref[:, 0:D_KV] = ck_ref[...].astype(BF16)
            dkv_ref[:, D_KV:2 * D_KV] = cv_ref[...].astype(BF16)

        finish(n == nb)

    q_spec, k_prev, k_cur, v_prev, v_cur = _attn_specs(nb, True)
    wide = pl.BlockSpec((BLOCK, D_MODEL), lambda n: (jnp.minimum(n, nb - 1), 0))
    outs = pl.pallas_call(
        body, name=name, grid=(nb + 1,),
        in_specs=[pl.BlockSpec(memory_space=pltpu.SMEM), q_spec, k_prev, k_cur, v_prev, v_cur, wide, wide,
                  pl.BlockSpec((BLOCK, LANE), lambda n: (jnp.minimum(n, nb - 1), 0)), wide] + host.in_specs,
        out_specs=[wide, pl.BlockSpec((BLOCK, 2 * D_KV), lambda n: (jnp.maximum(n - 1, 0), 0)), wide,
                   pl.BlockSpec((8, LANE), lambda n: (0, 0))] + host.out_specs,
        out_shape=[jax.ShapeDtypeStruct((t, D_MODEL), BF16), jax.ShapeDtypeStruct((t, 2 * D_KV), BF16),
                   jax.ShapeDtypeStruct((t, D_MODEL), BF16), jax.ShapeDtypeStruct((8, LANE), F32)] + host.out_shapes,
        scratch_shapes=[pltpu.VMEM((BLOCK, D_KV), F32), pltpu.VMEM((BLOCK, D_KV), F32),
                        pltpu.VMEM((N_Q_HEADS, BLOCK, 2 * BLOCK), F32)] + host.scratch_shapes,
        compiler_params=_params("arbitrary"))(sm, qkv, qkv, qkv, qkv, qkv, ag_ml, y, lse, dz, *host.arrays)
    res, landed = host.results(outs, 4)
    return (*res, landed) if ride else tuple(res)


def _local_grads(x, target, p, project, late_weights=None, reduce_out=None, reduce_in=None):
    h, rx_rg, qkv, ag_ml, wt = project(x, p["pre_g"])
    if late_weights is None:
        y_attn, z_attn, lse = _attn_fwd(p["sm"], qkv, ag_ml, "attn_fwd")
        lru = (p["cw"], p["cb"], p["wbd_a"], p["b_a"], p["wbd_x"], p["b_x"], p["lam"])
        y_rnn, z_rnn = _rnn_fwd(rx_rg, *lru, "rnn_fwd")
    else:
        (conv_ride, conv_unpack), (out_ride, out_unpack) = late_weights
        y_attn, z_attn, lse, landed = _attn_fwd(p["sm"], qkv, ag_ml, "attn_fwd", ride=conv_ride)
        p = {**p, **conv_unpack(landed)}
        lru = (p["cw"], p["cb"], p["wbd_a"], p["b_a"], p["wbd_x"], p["b_x"], p["lam"])
        y_rnn, z_rnn, landed = _rnn_fwd(rx_rg, *lru, "rnn_fwd", ride=out_ride)
        p = {**p, **out_unpack(landed)}
    br_rnn, br_attn, merged, dout, dy, st_post = _branches_fwd(
        z_rnn, z_attn, ag_ml, p["b_gate"], p["w_rnn"], p["w_attn"], p["w_out"], x, target, p["post_g"], "branches_fwd")

    dbr_rnn, dbr_attn, d_ml, dz_rnn, dz_attn, st_merge = _branches_bwd(
        dout, br_rnn, br_attn, ag_ml, p["b_gate"], p["w_rnn"], p["w_attn"], p["w_out"], "branches_bwd")
    gw_out = _mm_tn(merged, dout, "gw_out")
    gw_rnn = _mm_tn(z_rnn, dbr_rnn, "gw_rnn")
    gw_attn = _mm_tn(z_attn, dbr_attn, "gw_attn")
    red_out = red_in = None
    if reduce_out is None:
        d_rx, d_rg, st_rnn, g_rg_a, g_rg_x = _rnn_bwd(rx_rg, y_rnn, dz_rnn, *lru, "rnn_bwd")
        dq, dkv, d_ag, st_sink = _attn_bwd(p["sm"], qkv, ag_ml, y_attn, lse, dz_attn, "attn_bwd")
    else:
        d_rx, d_rg, st_rnn, g_rg_a, g_rg_x, from_sibling = _rnn_bwd(
            rx_rg, y_rnn, dz_rnn, *lru, "rnn_bwd", ride=reduce_out[0](gw_rnn, gw_attn, gw_out))
        dq, dkv, d_ag, st_sink, red_out = _attn_bwd(p["sm"], qkv, ag_ml, y_attn, lse, dz_attn, "attn_bwd",
                                                    ride=reduce_out[1](from_sibling, g_rg_a, g_rg_x))

    segs = [d_rx, d_rg, dq, dkv, d_ag, d_ml]
    gwt = _mm_tn_seg(segs, h, "gw_in")
    if reduce_in is None:
        grad_x, st_pre = _input_grad(segs, wt, x, p["pre_g"], dy, "input_grad")
    else:
        grad_x, st_pre, red_in = _input_grad(segs, wt, x, p["pre_g"], dy, "input_grad", ride=reduce_in(gwt))
    return dict(grad_x=grad_x, gwt=gwt, gw_rnn=gw_rnn, gw_attn=gw_attn, gw_out=gw_out,
                st_post=st_post, st_merge=st_merge, st_rnn=st_rnn, st_sink=st_sink, st_pre=st_pre,
                g_rg_a=g_rg_a, g_rg_x=g_rg_x, red_out=red_out, red_in=red_in)


def _place():
    x, y, c = lax.axis_index("x"), lax.axis_index("y"), lax.axis_index("c")
    return x, y, c


def _gather_ride(shards):
    n = len(shards)

    def copies(ins, outs, sems):
        send_sems, recv_sems, local_sems = sems
        x, y, c = _place()
        me, sibling = (x, y, c), (x, y, 1 - c)
        chips = [(1 - x, y), (x, 1 - y), (1 - x, 1 - y)]

        def slot(a, dev):
            return outs[a].at[4 * dev[0] + 2 * dev[1] + dev[2]]

        def copy(a, k, block, to, src=None):
            return pltpu.make_async_remote_copy(
                src_ref=slot(a, block) if src is None else src, dst_ref=slot(a, block),
                send_sem=send_sems.at[a, k], recv_sem=recv_sems.at[a, k], device_id=to, device_id_type=MESH)

        mine = [pltpu.make_async_copy(ins[a], slot(a, me), local_sems.at[a]) for a in range(n)]
        first = []
        for a in range(n):
            first.append(copy(a, 0, me, sibling, src=ins[a]))
            first += [copy(a, 1 + j, me, (*chip, c), src=ins[a]) for j, chip in enumerate(chips)]
        return me, sibling, chips, c, copy, mine, first

    def start(ins, outs, sems):
        *_, mine, first = copies(ins, outs, sems)
        for cp in mine + first:
            cp.start()

    def middle(ins, outs, sems):
        me, sibling, chips, c, copy, _, _ = copies(ins, outs, sems)
        for j, chip in enumerate(chips):
            for a in range(n):
                copy(a, 1 + j, (*chip, c), me).wait_recv()
                copy(a, 4 + j, (*chip, c), sibling).start()

    def finish(ins, outs, sems):
        me, sibling, chips, c, copy, mine, first = copies(ins, outs, sems)
        passed = [copy(a, 4 + j, (*chip, c), sibling) for j, chip in enumerate(chips) for a in range(n)]
        for a in range(n):
            copy(a, 0, sibling, me).wait_recv()
            for j, chip in enumerate(chips):
                copy(a, 4 + j, (*chip, 1 - c), me).wait_recv()
        for cp in first + passed:
            cp.wait_send()
        for cp in mine:
            cp.wait()

    return _Ride(
        shards, [jax.ShapeDtypeStruct((N_DEV, *s.shape), s.dtype) for s in shards],
        [pltpu.SemaphoreType.DMA((n, 7)), pltpu.SemaphoreType.DMA((n, 7)), pltpu.SemaphoreType.DMA((n,))],
        start, finish, middle)


def _sibling_ride(scatter, whole):
    ns, nw = len(scatter), len(whole)

    def copies(ins, outs, sems):
        send_sems, recv_sems = sems
        x, y, c = _place()
        cps = [pltpu.make_async_remote_copy(
            src_ref=ins[a].at[2 * chip + (1 - c)], dst_ref=outs[a].at[chip],
            send_sem=send_sems.at[a * N_CHIP + chip], recv_sem=recv_sems.at[a * N_CHIP + chip],
            device_id=(x, y, 1 - c), device_id_type=MESH) for a in range(ns) for chip in range(N_CHIP)]
        cps += [pltpu.make_async_remote_copy(
            src_ref=ins[ns + a], dst_ref=outs[ns + a],
            send_sem=send_sems.at[ns * N_CHIP + a], recv_sem=recv_sems.at[ns * N_CHIP + a],
            device_id=(x, y, 1 - c), device_id_type=MESH) for a in range(nw)]
        return cps

    def start(ins, outs, sems):
        for cp in copies(ins, outs, sems):
            cp.start()

    def finish(ins, outs, sems):
        for cp in copies(ins, outs, sems):
            cp.wait()

    n_sem = ns * N_CHIP + nw
    return _Ride(
        list(scatter) + list(whole),
        [jax.ShapeDtypeStruct((N_CHIP, *s.shape[1:]), s.dtype) for s in scatter]
        + [jax.ShapeDtypeStruct(s.shape, s.dtype) for s in whole],
        [pltpu.SemaphoreType.DMA((n_sem,)), pltpu.SemaphoreType.DMA((n_sem,))], start, finish)


def _chips_ride(scatter, whole):
    ns, nw = len(scatter), len(whole)
    n = ns + nw

    def copies(ins, outs, sems):
        send_sems, recv_sems, local_sems = sems
        x, y, c = _place()
        own = 2 * x + y
        chips = [(1 - x, y), (x, 1 - y), (1 - x, 1 - y)]

        def src(a, chip_idx):
            return ins[a].at[chip_idx] if a < ns else ins[a]

        local = [pltpu.make_async_copy(src(a, own), outs[a].at[own], local_sems.at[a]) for a in range(n)]
        sent = [pltpu.make_async_remote_copy(
            src_ref=src(a, 2 * chip[0] + chip[1]), dst_ref=outs[a].at[own],
            send_sem=send_sems.at[a, j], recv_sem=recv_sems.at[a, own], device_id=(*chip, c), device_id_type=MESH)
            for a in range(n) for j, chip in enumerate(chips)]
        return chips, c, local, sent

    def start(ins, outs, sems):
        _, _, local, sent = copies(ins, outs, sems)
        for cp in local + sent:
            cp.start()

    def finish(ins, outs, sems):
        send_sems, recv_sems, _ = sems
        chips, c, local, sent = copies(ins, outs, sems)
        for a in range(n):
            for chip in chips:
                k = 2 * chip[0] + chip[1]
                pltpu.make_async_remote_copy(
                    src_ref=outs[a].at[k], dst_ref=outs[a].at[k], send_sem=send_sems.at[a, 0],
                    recv_sem=recv_sems.at[a, k], device_id=(*chip, c), device_id_type=MESH).wait_recv()
        for cp in sent:
            cp.wait_send()
        for cp in local:
            cp.wait()

    return _Ride(
        list(scatter) + list(whole),
        [jax.ShapeDtypeStruct(s.shape, s.dtype) for s in scatter]
        + [jax.ShapeDtypeStruct((N_CHIP, *s.shape), s.dtype) for s in whole],
        [pltpu.SemaphoreType.DMA((n, 3)), pltpu.SemaphoreType.DMA((n, N_CHIP)), pltpu.SemaphoreType.DMA((n,))],
        start, finish)


def _pair_sum_scatter(parts, recvs, core, name):
    na = len(parts)
    _, r, cdim = parts[0].shape
    tr = min(r, 416 if r % 416 == 0 else 128)

    def body(core_ref, *refs):
        del core_ref
        for a in range(na):
            refs[2 * na + a][...] = (refs[a][...] + refs[na + a][...]).astype(BF16)

    blk = (None, tr, cdim)
    mine = pl.BlockSpec(blk, lambda k, i, core_ref: (2 * k + core_ref[0], i, 0))
    slot = pl.BlockSpec(blk, lambda k, i, core_ref: (k, i, 0))
    return pl.pallas_call(
        body, name=name,
        grid_spec=pltpu.PrefetchScalarGridSpec(
            num_scalar_prefetch=1, grid=(N_CHIP, r // tr),
            in_specs=[mine] * na + [slot] * na, out_specs=[slot] * na),
        out_shape=[jax.ShapeDtypeStruct((N_CHIP, r, cdim), BF16)] * na,
        compiler_params=_params("parallel", "parallel"))(core, *parts, *recvs)


def _pair_sum_whole(mine, recvs, name):
    na = len(mine)

    def body(*refs):
        for a in range(na):
            refs[2 * na + a][...] = refs[a][...] + refs[na + a][...]

    return pl.pallas_call(body, name=name, out_shape=[jax.ShapeDtypeStruct(m.shape, F32) for m in mine])(*mine, *recvs)


def _allreduce_small(pack, name):
    shape = pack.shape

    def body(x_ref, o_ref, sib_ref, chip_ref, send_sems, recv_sems):
        x, y, c = _place()
        own = 2 * x + y
        chips = [(1 - x, y), (x, 1 - y), (1 - x, 1 - y)]
        to_sibling = pltpu.make_async_remote_copy(
            src_ref=x_ref, dst_ref=sib_ref, send_sem=send_sems.at[0], recv_sem=recv_sems.at[0],
            device_id=(x, y, 1 - c), device_id_type=MESH)
        to_sibling.start()
        to_sibling.wait()
        chip_ref[own] = x_ref[...] + sib_ref[...]
        sent = [pltpu.make_async_remote_copy(
            src_ref=chip_ref.at[own], dst_ref=chip_ref.at[own], send_sem=send_sems.at[1 + j],
            recv_sem=recv_sems.at[1 + own], device_id=(*chip, c), device_id_type=MESH) for j, chip in enumerate(chips)]
        for cp in sent:
            cp.start()
        for chip in chips:
            k = 2 * chip[0] + chip[1]
            pltpu.make_async_remote_copy(
                src_ref=chip_ref.at[k], dst_ref=chip_ref.at[k], send_sem=send_sems.at[1],
                recv_sem=recv_sems.at[1 + k], device_id=(*chip, c), device_id_type=MESH).wait_recv()
        for cp in sent:
            cp.wait_send()
        o_ref[...] = (chip_ref[0] + chip_ref[1]) + (chip_ref[2] + chip_ref[3])

    return pl.pallas_call(
        body, name=name, out_shape=jax.ShapeDtypeStruct(shape, F32),
        in_specs=[pl.BlockSpec(memory_space=pltpu.VMEM)], out_specs=pl.BlockSpec(memory_space=pltpu.VMEM),
        scratch_shapes=[pltpu.VMEM(shape, F32), pltpu.VMEM((N_CHIP, *shape), F32),
                        pltpu.SemaphoreType.DMA((4,)), pltpu.SemaphoreType.DMA((1 + N_CHIP,))],
    )(pack)


def _adamw(g, w, m, v):
    m = ADAM_B1 * m + (1.0 - ADAM_B1) * g
    v = ADAM_B2 * v + (1.0 - ADAM_B2) * (g * g)
    m_hat = m / (1.0 - ADAM_B1 ** ADAM_STEP)
    v_hat = v / (1.0 - ADAM_B2 ** ADAM_STEP)
    delta = -ADAM_LR * (m_hat / (jnp.sqrt(v_hat) + ADAM_EPS) + ADAM_WD * w)
    return delta, m, v


def _adam_parts(parts, w, m, v, name, tr=None):
    npart, r, c = parts.shape
    tr = r if tr is None else min(tr, r)

    def body(p_ref, w_ref, m_ref, v_ref, g_ref, d_ref, nm_ref, nv_ref):
        g = p_ref[0].astype(F32)
        for k in range(1, npart):
            g = g + p_ref[k].astype(F32)
        g_ref[...] = g
        d_ref[...], nm_ref[...], nv_ref[...] = _adamw(g, w_ref[...], m_ref[...], v_ref[...])

    tile = pl.BlockSpec((tr, c), lambda i: (i, 0))
    return pl.pallas_call(
        body, name=name, grid=(r // tr,),
        in_specs=[pl.BlockSpec((npart, tr, c), lambda i: (0, i, 0)), tile, tile, tile],
        out_specs=[tile] * 4, out_shape=[jax.ShapeDtypeStruct((r, c), F32)] * 4,
        compiler_params=_params("parallel"))(parts, w, m, v)


def _block_diag(w):
    w4 = w.reshape(N_GROUPS, 4, RNN_BLOCK_W, RNN_BLOCK_W)
    eye = jnp.eye(4, dtype=w.dtype)
    return jnp.einsum("gbij,bc->gbicj", w4, eye).reshape(N_GROUPS, GROUP_W, GROUP_W).astype(BF16)


SMALL_ROWS = 16
ROW_PRE_G, ROW_BGATE, ROW_CONV_B, ROW_B_A, ROW_B_X, ROW_LAM, ROW_POST_G, ROW_LOSS, ROW_SINKS, ROW_CONV_W = 0, 1, 3, 4, 5, 6, 7, 8, 9, 10


def _pack_stats(st_pre, st_merge, st_rnn, st_post, st_sink, name):
    d = D_MODEL

    def body(pre_ref, mg_ref, rnn_ref, post_ref, sink_ref, o_ref):
        rnn = rnn_ref[...]
        sinks = jnp.concatenate([sink_ref[0:1, :], jnp.zeros((1, d - LANE), F32)], axis=1)
        o_ref[0:8, :] = _rows8([pre_ref[0:1, :], mg_ref[0:1, :], mg_ref[1:2, :], rnn[0:1], rnn[1:2], rnn[2:3], rnn[3:4],
                                post_ref[0:1, :]], d)
        o_ref[8:16, :] = _rows8([post_ref[1:2, :], sinks, rnn[4:5], rnn[5:6], rnn[6:7], rnn[7:8]], d)

    return pl.pallas_call(body, name=name, out_shape=jax.ShapeDtypeStruct((SMALL_ROWS, d), F32))(
        st_pre, st_merge, st_rnn, st_post, st_sink)


def _adam_small(total, w, m, v, name):
    d = D_MODEL
    n_in = len(w)

    def pack(refs):
        pre, bg, cbias, ba, bx, lam, post, sinks = [r[...] for r in refs]
        top = _rows8([pre, bg[:, 0:d], bg[:, d:2 * d], cbias, ba, bx, lam, post], d)
        return jnp.concatenate([top, _rows8([jnp.zeros((1, d), F32), sinks], d)], axis=0)

    def body(*refs):
        p_ref = refs[0]
        w_refs, m_refs, v_refs = (refs[1 + k * n_in:1 + (k + 1) * n_in] for k in range(3))
        outs = refs[1 + 3 * n_in:]
        g = p_ref[...]
        res = (g,) + _adamw(g, pack(w_refs), pack(m_refs), pack(v_refs))
        for k in range(4):
            outs[k][...] = res[k]
            outs[4 + k][...] = jnp.concatenate([res[k][ROW_BGATE:ROW_BGATE + 1], res[k][ROW_BGATE + 1:ROW_BGATE + 2]], axis=1)

    return pl.pallas_call(
        body, name=name,
        out_shape=[jax.ShapeDtypeStruct((SMALL_ROWS, d), F32)] * 4 + [jax.ShapeDtypeStruct((1, 2 * d), F32)] * 4,
    )(total, *w, *m, *v)


def kernel(x, pre_norm_g, w_in, b_gate, conv_w, conv_b, w_rg_a, b_rg_a, w_rg_x, b_rg_x, lru_lambda, attn_sinks, w_rnn_out, w_attn_out, w_out, post_norm_g, loss_target, m_pre_norm_g, m_w_in, m_b_gate, m_conv_w, m_conv_b, m_w_rg_a, m_b_rg_a, m_w_rg_x, m_b_rg_x, m_lru_lambda, m_attn_sinks, m_w_rnn_out, m_w_attn_out, m_w_out, m_post_norm_g, v_pre_norm_g, v_w_in, v_b_gate, v_conv_w, v_conv_b, v_w_rg_a, v_b_rg_a, v_w_rg_x, v_b_rg_x, v_lru_lambda, v_attn_sinks, v_w_rnn_out, v_w_attn_out, v_w_out, v_post_norm_g):
    cx, cy, cc = _place()
    dev = 4 * cx + 2 * cy + cc
    core = jnp.reshape(cc, (1,)).astype(jnp.int32)

    w_in_t, m_in_t, v_in_t = (jnp.transpose(a[0]) for a in (w_in, m_w_in, v_w_in))
    wt_shard = w_in_t.astype(BF16)

    def project(xs, pre_g):
        h, rx_rg, qkv, ag_ml, wt_all = _gather_project(xs, pre_g, wt_shard, "gather_project")
        return h, rx_rg, qkv, ag_ml, wt_all.reshape(D_IN, D_MODEL)

    heads = jnp.arange(1, N_Q_HEADS + 1, dtype=F32)
    slopes = jnp.exp2(-ALIBI_MAX_BIAS * heads / N_Q_HEADS)
    b_a = b_rg_a.reshape(1, D_RNN)
    b_x = b_rg_x.reshape(1, D_RNN)
    p = dict(
        pre_g=pre_norm_g, post_g=post_norm_g, b_gate=b_gate, cb=conv_b,
        wbd_a=_block_diag(w_rg_a[0]), b_a=b_a, wbd_x=_block_diag(w_rg_x[0]), b_x=b_x, lam=lru_lambda,
        sm=jnp.pad(attn_sinks, ((0, 1), (0, 0))) + jnp.pad(slopes[None, :], ((1, 0), (0, 0))))

    def conv_unpack(landed):
        return dict(cw=jnp.transpose(landed[0], (1, 0, 2)).reshape(4, D_RNN))

    def out_unpack(landed):
        w_rnn_all, w_attn_all, w_out_all = landed
        return dict(w_rnn=w_rnn_all.reshape(D_RNN, D_MODEL), w_attn=w_attn_all.reshape(D_MODEL, D_MODEL),
                    w_out=w_out_all.reshape(D_MODEL, D_MODEL))

    late_weights = ((_gather_ride([conv_w[0]]), conv_unpack),
                    (_gather_ride([w_rnn_out[0].astype(BF16), w_attn_out[0].astype(BF16), w_out[0].astype(BF16)]), out_unpack))
    flat = (RNN_BLOCKS * RNN_BLOCK_W, RNN_BLOCK_W)

    out_scatter = []

    def out_sibling(gw_rnn, gw_attn, gw_out):
        out_scatter.extend(gw.reshape(N_DEV, SHARD_OUT, D_MODEL) for gw in (gw_rnn, gw_attn, gw_out))
        return _sibling_ride(out_scatter, [])

    def out_chips(from_sibling, g_rg_a, g_rg_x):
        whole = [g_rg_a.reshape(flat), g_rg_x.reshape(flat)]
        rg_sibling = _run_ride(_sibling_ride([], whole), "sibling_rg")
        return _chips_ride(_pair_sum_scatter(out_scatter, from_sibling, core, "pair_out"),
                           _pair_sum_whole(whole, rg_sibling, "pair_rg"))

    def reduce_in(gwt):
        scatter = [gwt.reshape(N_DEV, SHARD_IN, D_MODEL)]
        from_sibling = _run_ride(_sibling_ride(scatter, []), "sibling_in")
        return _chips_ride(_pair_sum_scatter(scatter, from_sibling, core, "pair_in"), [])

    g = _local_grads(x[0], loss_target[0], p, project, late_weights, (out_sibling, out_chips), reduce_in)
    small = _allreduce_small(
        _pack_stats(g["st_pre"], g["st_merge"], g["st_rnn"], g["st_post"], g["st_sink"], "pack_stats"), "allreduce_small")

    out = {}
    red = g["red_out"]
    out["w_in"] = [jnp.transpose(o) for o in _adam_parts(g["red_in"][0], w_in_t, m_in_t, v_in_t, "adam_w_in", tr=SHARD_IN // 2)]
    out["w_rnn_out"] = _adam_parts(red[0], w_rnn_out[0], m_w_rnn_out[0], v_w_rnn_out[0], "adam_w_rnn_out")
    out["w_attn_out"] = _adam_parts(red[1], w_attn_out[0], m_w_attn_out[0], v_w_attn_out[0], "adam_w_attn_out")
    out["w_out"] = _adam_parts(red[2], w_out[0], m_w_out[0], v_w_out[0], "adam_w_out")
    out["w_rg_a"] = _adam_parts(red[3], w_rg_a.reshape(flat), m_w_rg_a.reshape(flat), v_w_rg_a.reshape(flat), "adam_w_rg_a", tr=256)
    out["w_rg_x"] = _adam_parts(red[4], w_rg_x.reshape(flat), m_w_rg_x.reshape(flat), v_w_rg_x.reshape(flat), "adam_w_rg_x", tr=256)

    def rows(pre, bg, cbias, ba, bx, lam, post, sinks):
        return (pre, bg, cbias, ba.reshape(1, D_RNN), bx.reshape(1, D_RNN), lam, post,
                jnp.pad(sinks, ((0, 0), (0, D_MODEL - N_Q_HEADS))))

    small_out = _adam_small(
        small,
        rows(pre_norm_g, b_gate, conv_b, b_rg_a, b_rg_x, lru_lambda, post_norm_g, attn_sinks),
        rows(m_pre_norm_g, m_b_gate, m_conv_b, m_b_rg_a, m_b_rg_x, m_lru_lambda, m_post_norm_g, m_attn_sinks),
        rows(v_pre_norm_g, v_b_gate, v_conv_b, v_b_rg_a, v_b_rg_x, v_lru_lambda, v_post_norm_g, v_attn_sinks),
        "adam_small")
    packed, bgate_out = small_out[:4], small_out[4:]
    g_cw = lax.dynamic_slice(packed[0][ROW_CONV_W:ROW_CONV_W + 4], (0, dev * SHARD_OUT), (4, SHARD_OUT))
    out["conv_w"] = _adam_parts(g_cw[None], conv_w[0], m_conv_w[0], v_conv_w[0], "adam_conv_w")

    def unpack(kind, name):
        if name == "b_gate":
            return bgate_out[kind]
        row = dict(pre_norm_g=ROW_PRE_G, conv_b=ROW_CONV_B, b_rg_a=ROW_B_A, b_rg_x=ROW_B_X, lru_lambda=ROW_LAM,
                   post_norm_g=ROW_POST_G, attn_sinks=ROW_SINKS)[name]
        r = packed[kind][row:row + 1]
        if name == "attn_sinks":
            return r[:, 0:N_Q_HEADS]
        if name in ("b_rg_a", "b_rg_x"):
            return r.reshape(1, RNN_BLOCKS, RNN_BLOCK_W)
        return r

    shapes = dict(w_in=(1, D_MODEL, SHARD_IN), w_rnn_out=(1, SHARD_OUT, D_MODEL), w_attn_out=(1, SHARD_OUT, D_MODEL),
                  w_out=(1, SHARD_OUT, D_MODEL), w_rg_a=(1, RNN_BLOCKS, RNN_BLOCK_W, RNN_BLOCK_W),
                  w_rg_x=(1, RNN_BLOCKS, RNN_BLOCK_W, RNN_BLOCK_W), conv_w=(1, 4, SHARD_OUT))
    weights = ["pre_norm_g", "w_in", "b_gate", "conv_w", "conv_b", "w_rg_a", "b_rg_a", "w_rg_x", "b_rg_x",
               "lru_lambda", "attn_sinks", "w_rnn_out", "w_attn_out", "w_out", "post_norm_g"]
    results = []
    for kind in range(4):
        for name in weights:
            if name in out:
                results.append(out[name][kind].reshape(shapes[name]))
            else:
                results.append(unpack(kind, name))
    loss = 0.5 / D_MODEL * jnp.sum(packed[0][ROW_LOSS])
    return (loss, g["grad_x"][None], *results)
```

```python
import functools

import jax
import jax.numpy as jnp
from jax import lax
from jax.experimental import pallas as pl
from jax.experimental.pallas import tpu as pltpu

F32 = jnp.float32
BF16 = jnp.bfloat16

D_MODEL = 1024
D_RNN = 1024
RNN_BLOCKS = 16
RNN_BLOCK_W = 64
LRU_C = 8.0
N_Q_HEADS = 16
HEAD_DIM = 64
D_KV = 256
BLOCK = 128
ALIBI_MAX_BIAS = 8.0
EPS = 1e-6
D_IN = 6656
N_DEV = 8
N_CHIP = 4
SHARD_IN = D_IN // N_DEV
SHARD_OUT = D_MODEL // N_DEV
ATTN_SCALE = HEAD_DIM ** -0.5
MASKED = -1e30

ADAM_LR = 0.001
ADAM_B1 = 0.9
ADAM_B2 = 0.999
ADAM_EPS = 1e-08
ADAM_WD = 0.01
ADAM_STEP = 10

VMEM_LIMIT_BYTES = 52 * 1024 * 1024
LANE = 128
GROUP_W = 256
N_GROUPS = D_RNN // GROUP_W
SEG_CHUNK = 512

NT_DIMS = (((1,), (1,)), ((), ()))
TN_DIMS = (((0,), (0,)), ((), ()))
MESH = pl.DeviceIdType.MESH
ANY = pl.BlockSpec(memory_space=pl.ANY)


def _params(*semantics):
    return pltpu.CompilerParams(dimension_semantics=semantics, vmem_limit_bytes=VMEM_LIMIT_BYTES)


def _sigmoid(x):
    return 0.5 * jnp.tanh(0.5 * x) + 0.5


def _log1p(e):
    u = 1.0 + e
    den = jnp.where(u == 1.0, 1.0, u - 1.0)
    return jnp.where(u == 1.0, e, jnp.log(u) * (e / den))


def _softplus(z):
    return jnp.maximum(z, 0.0) + _log1p(jnp.exp(-jnp.abs(z)))


def _rows8(rows, width):
    idx = lax.broadcasted_iota(jnp.int32, (8, width), 0)
    out = jnp.zeros((8, width), F32)
    for r, v in enumerate(rows):
        out = jnp.where(idx == r, v, out)
    return out


class _Ride:
    def __init__(self, arrays, out_shapes, scratch_shapes, start, finish, middle=None):
        self.arrays, self.out_shapes, self.scratch_shapes = list(arrays), list(out_shapes), list(scratch_shapes)
        self.start, self.finish, self.middle = start, finish, middle


class _Hosted:
    def __init__(self, ride, n_in, n_out, n_scratch=0):
        self.ride = ride
        self.sizes = (n_in, len(ride.arrays) if ride else 0, n_out, len(ride.out_shapes) if ride else 0, n_scratch)
        self.arrays = ride.arrays if ride else []
        self.in_specs = [ANY] * len(self.arrays)
        self.out_shapes = ride.out_shapes if ride else []
        self.out_specs = [ANY] * len(self.out_shapes)
        self.scratch_shapes = ride.scratch_shapes if ride else []

    def split(self, refs):
        n_in, r_in, n_out, r_out, n_scr = self.sizes
        cuts = [0, n_in, n_in + r_in, n_in + r_in + n_out, n_in + r_in + n_out + r_out, n_in + r_in + n_out + r_out + n_scr]
        host_in, ride_in, host_out, ride_out, host_scr = (refs[cuts[k]:cuts[k + 1]] for k in range(5))
        ride_scr = refs[cuts[5]:]

        def start(when):
            if self.ride is not None:
                pl.when(when)(lambda: self.ride.start(ride_in, ride_out, ride_scr))

        def finish(when):
            if self.ride is not None:
                pl.when(when)(lambda: self.ride.finish(ride_in, ride_out, ride_scr))

        def middle(when):
            if self.ride is not None and self.ride.middle is not None:
                pl.when(when)(lambda: self.ride.middle(ride_in, ride_out, ride_scr))

        start.middle = middle
        return tuple(host_in) + tuple(host_out) + tuple(host_scr), start, finish

    def results(self, outs, n_out):
        outs = list(outs) if isinstance(outs, (list, tuple)) else [outs]
        return outs[:n_out], outs[n_out:]


def _join_rides(a, b):
    na, nb_ = len(a.arrays), len(b.arrays)
    oa = len(a.out_shapes)
    sa = len(a.scratch_shapes)

    def both(fa, fb):
        def run(ins, outs, sems):
            if fa is not None:
                fa(ins[:na], outs[:oa], sems[:sa])
            if fb is not None:
                fb(ins[na:na + nb_], outs[oa:], sems[sa:])
        return run

    middle = both(a.middle, b.middle) if (a.middle or b.middle) else None
    return _Ride(a.arrays + b.arrays, a.out_shapes + b.out_shapes, a.scratch_shapes + b.scratch_shapes,
                 both(a.start, b.start), both(a.finish, b.finish), middle)


def _run_ride(ride, name):
    n_in, n_out = len(ride.arrays), len(ride.out_shapes)

    def body(*refs):
        ins, outs, sems = refs[:n_in], refs[n_in:n_in + n_out], refs[n_in + n_out:]
        ride.start(ins, outs, sems)
        ride.finish(ins, outs, sems)

    return pl.pallas_call(
        body, name=name, in_specs=[ANY] * n_in, out_specs=[ANY] * n_out,
        out_shape=ride.out_shapes, scratch_shapes=ride.scratch_shapes)(*ride.arrays)


def _load_resident(w_hbm, w_vmem, sems, first):
    def piece(c):
        rows = pl.ds(c * SEG_CHUNK, SEG_CHUNK)
        return pltpu.make_async_copy(w_hbm.at[rows], w_vmem.at[rows], sems.at[c])

    @pl.when(first)
    def _():
        for c in range(w_vmem.shape[0] // SEG_CHUNK):
            piece(c).start()

    def ready(c):
        @pl.when(first)
        def _():
            piece(c).wait()

    return ready


CHIP_ROWS = 2 * SHARD_IN
PROJ_WIDTHS = (2 * D_RNN, D_MODEL + 2 * D_KV, 3 * D_MODEL)
PROJ_DTYPES = (F32, BF16, BF16)


def _chip_pieces():
    starts = [0, PROJ_WIDTHS[0], PROJ_WIDTHS[0] + PROJ_WIDTHS[1], D_IN]
    pieces = []
    for k in range(N_CHIP):
        lo, hi = k * CHIP_ROWS, (k + 1) * CHIP_ROWS
        cur = []
        for a in range(len(PROJ_WIDTHS)):
            s0, s1 = max(lo, starts[a]), min(hi, starts[a + 1])
            if s0 < s1:
                cur.append((a, s0 - starts[a], s1 - s0, s0 - lo))
        pieces.append(cur)
    return pieces


def _gather_project(x, g, wt_shard, name, tm=512, ride=None):
    t, k = x.shape
    tm = min(tm, t)
    nt = t // tm
    pieces = _chip_pieces()
    host = _Hosted(ride, 3, 5, 11)

    def body(*refs):
        host_refs, ride_start, ride_finish = host.split(refs)
        (x_ref, g_ref, shard_ref, h_out, rx_ref, qkv_ref, ag_ref, wt_all,
         w_c, stage, o32, o16, h_all, send_sems, recv_sems, local_sem, stage_sems, out_sems, h_sems) = host_refs
        s, ti = pl.program_id(0), pl.program_id(1)
        px, py, pc = _place()
        me, sibling = (px, py, pc), (px, py, 1 - pc)
        chips = [(px, py), (1 - px, py), (px, 1 - py), (1 - px, 1 - py)]
        outs = (rx_ref, qkv_ref, ag_ref)

        def slot(dev):
            return wt_all.at[4 * dev[0] + 2 * dev[1] + dev[2]]

        def copy(kk, block, to, src=None):
            return pltpu.make_async_remote_copy(
                src_ref=slot(block) if src is None else src, dst_ref=slot(block),
                send_sem=send_sems.at[kk], recv_sem=recv_sems.at[kk], device_id=to, device_id_type=MESH)

        mine = pltpu.make_async_copy(shard_ref, slot(me), local_sem)
        first = [copy(0, me, sibling, src=shard_ref)] + [copy(1 + j, me, (*chips[1 + j], pc), src=shard_ref) for j in range(3)]
        passed = [copy(4 + j, (*chips[1 + j], pc), sibling) for j in range(3)]

        @pl.when((s == 0) & (ti == 0))
        def _():
            mine.start()
            for cp in first:
                cp.start()

        ride_start((s == 0) & (ti == 0))
        ride_start.middle((s == N_CHIP - 1) & (ti == nt // 2))

        for step in range(N_CHIP):
            @pl.when((s == step) & (ti == 0))
            def _(step=step):
                chip = chips[step]
                if step == 0:
                    mine.wait()
                    copy(0, sibling, me).wait_recv()
                else:
                    copy(step, (*chip, pc), me).wait_recv()
                    passed[step - 1].start()
                    copy(3 + step, (*chip, 1 - pc), me).wait_recv()
                loads = [pltpu.make_async_copy(slot((*chip, core)), stage.at[pl.ds(core * SHARD_IN, SHARD_IN)], stage_sems.at[core])
                         for core in (0, 1)]
                for cp in loads:
                    cp.start()
                for cp in loads:
                    cp.wait()
                w_c[...] = stage[...].T

        rows = pl.ds(pl.multiple_of(ti * tm, tm), tm)

        @pl.when(s == 0)
        def _():
            xv = x_ref[...]
            h_all[rows, :] = (xv * lax.rsqrt(jnp.mean(xv * xv, axis=-1, keepdims=True) + EPS) * g_ref[...]).astype(BF16)

        res = jnp.dot(h_all[rows, :], w_c[...], preferred_element_type=F32)

        n = s * nt + ti
        buf = lax.rem(n, 2)
        chip_idx = [2 * cx + cy for cx, cy in chips]

        def chip_at(step):
            return jnp.where(step == 0, chip_idx[0], jnp.where(step == 1, chip_idx[1], jnp.where(step == 2, chip_idx[2], chip_idx[3])))

        def h_write(b, tile):
            return pltpu.make_async_copy(h_all.at[pl.ds(tile * tm, tm)], h_out.at[pl.ds(tile * tm, tm)], h_sems.at[b])

        def writes(kchip, b, tile):
            cps = []
            for idx, (a, col, w, src) in enumerate(pieces[kchip]):
                staged = (o16 if PROJ_DTYPES[a] == BF16 else o32).at[b, :, pl.ds(src, w)]
                cps.append(pltpu.make_async_copy(staged, outs[a].at[pl.ds(tile * tm, tm), pl.ds(col, w)], out_sems.at[b, idx]))
            return cps

        o32[buf] = res
        o16[buf] = res.astype(BF16)
        kcur = chip_at(s)

        @pl.when(n > 0)
        def _():
            kprev = chip_at(lax.div(n - 1, nt))
            for kchip in range(N_CHIP):
                @pl.when(kprev == kchip)
                def _(kchip=kchip):
                    for cp in writes(kchip, 1 - buf, lax.rem(n - 1, nt)):
                        cp.wait()

            @pl.when(n <= nt)
            def _():
                h_write(1 - buf, n - 1).wait()

        @pl.when(s == 0)
        def _():
            h_write(buf, ti).start()

        for kchip in range(N_CHIP):
            @pl.when(kcur == kchip)
            def _(kchip=kchip):
                for cp in writes(kchip, buf, ti):
                    cp.start()

        @pl.when(n == N_CHIP * nt - 1)
        def _():
            for kchip in range(N_CHIP):
                @pl.when(kcur == kchip)
                def _(kchip=kchip):
                    for cp in writes(kchip, buf, ti):
                        cp.wait()
            for cp in first + passed:
                cp.wait_send()

        ride_finish(n == N_CHIP * nt - 1)

    tile = pl.BlockSpec((tm, k), lambda s, ti: (jnp.where(s == 0, ti, nt - 1), 0))
    outs = pl.pallas_call(
        body, name=name, grid=(N_CHIP, nt),
        in_specs=[tile, pl.BlockSpec((1, k), lambda s, ti: (0, 0)), ANY] + host.in_specs,
        out_specs=[ANY, ANY, ANY, ANY, ANY] + host.out_specs,
        out_shape=[jax.ShapeDtypeStruct((t, k), BF16)]
        + [jax.ShapeDtypeStruct((t, w), dt) for w, dt in zip(PROJ_WIDTHS, PROJ_DTYPES)]
        + [jax.ShapeDtypeStruct((N_DEV, SHARD_IN, k), BF16)] + host.out_shapes,
        scratch_shapes=[pltpu.VMEM((k, CHIP_ROWS), BF16), pltpu.VMEM((CHIP_ROWS, k), BF16),
                        pltpu.VMEM((2, tm, CHIP_ROWS), F32), pltpu.VMEM((2, tm, CHIP_ROWS), BF16), pltpu.VMEM((t, k), BF16),
                        pltpu.SemaphoreType.DMA((7,)), pltpu.SemaphoreType.DMA((7,)), pltpu.SemaphoreType.DMA,
                        pltpu.SemaphoreType.DMA((2,)), pltpu.SemaphoreType.DMA((2, 2)), pltpu.SemaphoreType.DMA((2,))]
        + host.scratch_shapes,
        compiler_params=_params("arbitrary", "arbitrary"))(x, g, wt_shard, *host.arrays)
    res, landed = host.results(outs, 5)
    return (*res, landed) if ride else tuple(res)


def _mm_tn(a, b, name, tm=512, tk=4096):
    ktok, m = a.shape
    n = b.shape[1]
    tk = min(tk, ktok)

    def body(a_ref, b_ref, o_ref):
        @pl.when(pl.program_id(1) == 0)
        def _():
            o_ref[...] = jnp.zeros_like(o_ref)

        o_ref[...] += lax.dot_general(a_ref[...], b_ref[...], TN_DIMS, preferred_element_type=F32)

    return pl.pallas_call(
        body, name=name, grid=(m // tm, ktok // tk),
        in_specs=[pl.BlockSpec((tk, tm), lambda i, kk: (kk, i)), pl.BlockSpec((tk, n), lambda i, kk: (kk, 0))],
        out_specs=pl.BlockSpec((tm, n), lambda i, kk: (i, 0)),
        out_shape=jax.ShapeDtypeStruct((m, n), F32),
        compiler_params=_params("parallel", "arbitrary"))(a, b)


def _segment_chunks(segs):
    bounds = [0]
    for s in segs:
        bounds.append(bounds[-1] + s.shape[1] // SEG_CHUNK)
    return bounds


def _input_grad(segs, wt, x, g, dy, name, tm=512, ride=None):
    m = segs[0].shape[0]
    rows, n = wt.shape
    tm = min(tm, m)
    bounds = _segment_chunks(segs)
    n_seg = len(segs)
    ni = m // tm
    host = _Hosted(ride, n_seg + 4, 2, 2)

    def body(*refs):
        host_refs, start, finish = host.split(refs)
        a_refs = host_refs[:n_seg]
        wt_hbm, x_ref, g_ref, dy_ref, gx_ref, st_ref, wt_vmem, sems = host_refs[n_seg:]
        i = pl.program_id(0)
        start(i == 0)

        @pl.when(i == 0)
        def _():
            st_ref[...] = jnp.zeros_like(st_ref)

        ready = _load_resident(wt_hbm, wt_vmem, sems, i == 0)
        dh = None
        for s in range(n_seg):
            for c in range(bounds[s], bounds[s + 1]):
                ready(c)
            part = jnp.dot(a_refs[s][...], wt_vmem[bounds[s] * SEG_CHUNK:bounds[s + 1] * SEG_CHUNK, :], preferred_element_type=F32)
            dh = part if dh is None else dh + part
        xv = x_ref[...]
        r = lax.rsqrt(jnp.mean(xv * xv, axis=-1, keepdims=True) + EPS)
        xn = xv * r
        dxn = dh * g_ref[...]
        gx_ref[...] = dy_ref[...] + r * (dxn - xn * jnp.mean(dxn * xn, axis=-1, keepdims=True))
        st_ref[...] += _rows8([jnp.sum(dh * xn, axis=0, keepdims=True)], n)
        finish(i == ni - 1)

    tile = pl.BlockSpec((tm, n), lambda i: (i, 0))
    outs = pl.pallas_call(
        body, name=name, grid=(ni,),
        in_specs=[pl.BlockSpec((tm, sg.shape[1]), lambda i: (i, 0)) for sg in segs]
        + [ANY, tile, pl.BlockSpec((1, n), lambda i: (0, 0)), tile] + host.in_specs,
        out_specs=[tile, pl.BlockSpec((8, n), lambda i: (0, 0))] + host.out_specs,
        out_shape=[jax.ShapeDtypeStruct((m, n), F32), jax.ShapeDtypeStruct((8, n), F32)] + host.out_shapes,
        scratch_shapes=[pltpu.VMEM((rows, n), wt.dtype), pltpu.SemaphoreType.DMA((rows // SEG_CHUNK,))] + host.scratch_shapes,
        compiler_params=_params("arbitrary"))(*segs, wt, x, g, dy, *host.arrays)
    res, landed = host.results(outs, 2)
    return (*res, landed) if ride else tuple(res)


def _mm_tn_seg(segs, b, name):
    ktok = segs[0].shape[0]
    n = b.shape[1]
    bounds = _segment_chunks(segs)
    n_seg = len(segs)
    nc = bounds[-1]
    seg_of = [s for s in range(n_seg) for _ in range(bounds[s], bounds[s + 1])]

    def body(*refs):
        a_hbm, b_hbm, o_ref = refs[:n_seg], refs[n_seg], refs[n_seg + 1]
        a_buf, b_vmem, a_sems, b_sem = refs[n_seg + 2:]
        c = pl.program_id(0)

        def fetch(cc):
            s = seg_of[cc]
            cols = pl.ds((cc - bounds[s]) * SEG_CHUNK, SEG_CHUNK)
            return pltpu.make_async_copy(a_hbm[s].at[:, cols], a_buf.at[cc % 2], a_sems.at[cc % 2])

        @pl.when(c == 0)
        def _():
            whole = pltpu.make_async_copy(b_hbm, b_vmem, b_sem)
            whole.start()
            fetch(0).start()
            whole.wait()

        for cc in range(nc):
            @pl.when(c == cc)
            def _(cc=cc):
                if cc + 1 < nc:
                    fetch(cc + 1).start()
                fetch(cc).wait()

        o_ref[...] = lax.dot_general(a_buf[c % 2], b_vmem[...], TN_DIMS, preferred_element_type=F32)

    return pl.pallas_call(
        body, name=name, grid=(nc,),
        in_specs=[ANY] * (n_seg + 1), out_specs=pl.BlockSpec((SEG_CHUNK, n), lambda c: (c, 0)),
        out_shape=jax.ShapeDtypeStruct((nc * SEG_CHUNK, n), F32),
        scratch_shapes=[pltpu.VMEM((2, ktok, SEG_CHUNK), segs[0].dtype), pltpu.VMEM((ktok, n), b.dtype),
                        pltpu.SemaphoreType.DMA((2,)), pltpu.SemaphoreType.DMA],
        compiler_params=_params("arbitrary"))(*segs, b)


def _branches_fwd(z_rnn, z_attn, ag_ml, b_gate, w_rnn, w_attn, w_out, x, target, g_post, name, tm=512):
    t, d = x.shape
    tm = min(tm, t)

    def body(zr_ref, za_ref, lr_ref, la_ref, br_ref, ba_ref, wr_ref, wa_ref, wo_ref, x_ref, t_ref, g_ref,
             brr_ref, bra_ref, mg_ref, do_ref, dy_ref, st_ref):
        @pl.when(pl.program_id(0) == 0)
        def _():
            st_ref[...] = jnp.zeros_like(st_ref)

        br_rnn = jnp.dot(zr_ref[...], wr_ref[...], preferred_element_type=F32)
        br_attn = jnp.dot(za_ref[...], wa_ref[...], preferred_element_type=F32)
        brr_ref[...] = br_rnn.astype(BF16)
        bra_ref[...] = br_attn.astype(BF16)
        g_rnn = _sigmoid(lr_ref[...].astype(F32) + br_ref[...])
        g_attn = _sigmoid(la_ref[...].astype(F32) + ba_ref[...])
        merged = (g_rnn * br_rnn + g_attn * br_attn).astype(BF16)
        mg_ref[...] = merged
        o = jnp.dot(merged, wo_ref[...], preferred_element_type=F32)
        g = g_ref[...]
        r = lax.rsqrt(jnp.mean(o * o, axis=-1, keepdims=True) + EPS)
        nrm = o * r
        err = x_ref[...] + nrm * g - t_ref[...]
        dy = err * (1.0 / d)
        dy_ref[...] = dy
        dn = dy * g
        do_ref[...] = (r * (dn - nrm * jnp.mean(dn * nrm, axis=-1, keepdims=True))).astype(BF16)
        st_ref[...] += _rows8([jnp.sum(dy * nrm, axis=0, keepdims=True), jnp.sum(err * err, axis=0, keepdims=True)], d)

    tile = pl.BlockSpec((tm, d), lambda i: (i, 0))
    weight = pl.BlockSpec((d, d), lambda i: (0, 0))
    bf = jax.ShapeDtypeStruct((t, d), BF16)
    return pl.pallas_call(
        body, name=name, grid=(t // tm,),
        in_specs=[tile, tile, pl.BlockSpec((tm, d), lambda i: (i, 1)), pl.BlockSpec((tm, d), lambda i: (i, 2)),
                  pl.BlockSpec((1, d), lambda i: (0, 0)), pl.BlockSpec((1, d), lambda i: (0, 1)),
                  weight, weight, weight, tile, tile, pl.BlockSpec((1, d), lambda i: (0, 0))],
        out_specs=[tile, tile, tile, tile, tile, pl.BlockSpec((8, d), lambda i: (0, 0))],
        out_shape=[bf, bf, bf, bf, jax.ShapeDtypeStruct((t, d), F32), jax.ShapeDtypeStruct((8, d), F32)],
        compiler_params=_params("arbitrary"))(z_rnn, z_attn, ag_ml, ag_ml, b_gate, b_gate, w_rnn, w_attn, w_out, x, target, g_post)


def _branches_bwd(dout, br_rnn, br_attn, ag_ml, b_gate, w_rnn, w_attn, w_out, name, tm=512):
    t, d = br_rnn.shape
    tm = min(tm, t)

    def body(do_ref, r_ref, a_ref, lr_ref, la_ref, br_ref, ba_ref, wr_ref, wa_ref, wo_ref,
             dr_ref, da_ref, dl_ref, dzr_ref, dza_ref, st_ref, wt_ref):
        @pl.when(pl.program_id(0) == 0)
        def _():
            st_ref[...] = jnp.zeros_like(st_ref)
            wt_ref[0] = wo_ref[...].T
            wt_ref[1] = wr_ref[...].T
            wt_ref[2] = wa_ref[...].T

        dm = jnp.dot(do_ref[...], wt_ref[0], preferred_element_type=F32)
        g_rnn = _sigmoid(lr_ref[...].astype(F32) + br_ref[...])
        g_attn = _sigmoid(la_ref[...].astype(F32) + ba_ref[...])
        dbr_rnn = (dm * g_rnn).astype(BF16)
        dbr_attn = (dm * g_attn).astype(BF16)
        dr_ref[...] = dbr_rnn
        da_ref[...] = dbr_attn
        dl_rnn = dm * r_ref[...].astype(F32) * g_rnn * (1.0 - g_rnn)
        dl_attn = dm * a_ref[...].astype(F32) * g_attn * (1.0 - g_attn)
        dl_ref[:, 0:d] = dl_rnn.astype(BF16)
        dl_ref[:, d:2 * d] = dl_attn.astype(BF16)
        st_ref[...] += _rows8([jnp.sum(dl_rnn, axis=0, keepdims=True), jnp.sum(dl_attn, axis=0, keepdims=True)], d)
        dzr_ref[...] = jnp.dot(dbr_rnn, wt_ref[1], preferred_element_type=F32).astype(BF16)
        dza_ref[...] = jnp.dot(dbr_attn, wt_ref[2], preferred_element_type=F32).astype(BF16)

    tile = pl.BlockSpec((tm, d), lambda i: (i, 0))
    weight = pl.BlockSpec((d, d), lambda i: (0, 0))
    bf = jax.ShapeDtypeStruct((t, d), BF16)
    return pl.pallas_call(
        body, name=name, grid=(t // tm,),
        in_specs=[tile, tile, tile, pl.BlockSpec((tm, d), lambda i: (i, 1)), pl.BlockSpec((tm, d), lambda i: (i, 2)),
                  pl.BlockSpec((1, d), lambda i: (0, 0)), pl.BlockSpec((1, d), lambda i: (0, 1)), weight, weight, weight],
        out_specs=[tile, tile, pl.BlockSpec((tm, 2 * d), lambda i: (i, 0)), tile, tile, pl.BlockSpec((8, d), lambda i: (0, 0))],
        out_shape=[bf, bf, jax.ShapeDtypeStruct((t, 2 * d), BF16), bf, bf, jax.ShapeDtypeStruct((8, d), F32)],
        scratch_shapes=[pltpu.VMEM((3, d, d), BF16)],
        compiler_params=_params("arbitrary"))(dout, br_rnn, br_attn, ag_ml, ag_ml, b_gate, b_gate, w_rnn, w_attn, w_out)


def _lru_gates(c, wa, ba, wx, bx, sp):
    cb = c.astype(BF16)
    r = _sigmoid(jnp.dot(cb, wa, preferred_element_type=F32) + ba)
    ig = _sigmoid(jnp.dot(cb, wx, preferred_element_type=F32) + bx)
    log_a = (-LRU_C) * r * sp
    a = jnp.exp(log_a)
    mult = jnp.sqrt(-jnp.tanh(log_a) * (a * a + 1.0))
    return cb, r, ig, a, mult


SUBLANES = 8


def _scan_fwd(a, u, carry, tt):
    w = a.shape[1]
    ng = tt // SUBLANES
    a3 = a.reshape(ng, SUBLANES, w)
    u3 = u.reshape(ng, SUBLANES, w)
    sub = lax.broadcasted_iota(jnp.int32, (ng, SUBLANES, w), 1)
    d = 1
    while d < SUBLANES:
        keep = sub >= d
        u3 = u3 + a3 * jnp.where(keep, pltpu.roll(u3, d, 1), 0.0)
        a3 = a3 * jnp.where(keep, pltpu.roll(a3, d, 1), 1.0)
        d *= 2
    out = []
    for g in range(ng):
        hg = u3[g] + a3[g] * carry
        out.append(hg)
        carry = hg[SUBLANES - 1:SUBLANES, :]
    return jnp.concatenate(out, axis=0)


def _scan_rev(b, g, carry, tt):
    w = b.shape[1]
    ng = tt // SUBLANES
    b3 = b.reshape(ng, SUBLANES, w)
    g3 = g.reshape(ng, SUBLANES, w)
    sub = lax.broadcasted_iota(jnp.int32, (ng, SUBLANES, w), 1)
    d = 1
    while d < SUBLANES:
        keep = sub < SUBLANES - d
        g3 = g3 + b3 * jnp.where(keep, pltpu.roll(g3, SUBLANES - d, 1), 0.0)
        b3 = b3 * jnp.where(keep, pltpu.roll(b3, SUBLANES - d, 1), 1.0)
        d *= 2
    out = [None] * ng
    for k in range(ng - 1, -1, -1):
        hk = g3[k] + b3[k] * carry
        out[k] = hk
        carry = hk[0:1, :]
    return jnp.concatenate(out, axis=0)


def _conv_taps(cw, bias, x, ext_ref, tt):
    x2 = ext_ref[7:7 + tt, :]
    x1 = ext_ref[6:6 + tt, :]
    x0 = ext_ref[5:5 + tt, :]
    c = bias + cw[3:4] * x + cw[2:3] * x2 + cw[1:2] * x1 + cw[0:1] * x0
    return c, x2, x1, x0


def _rnn_fwd(rx_rg, cw, cb, wa, ba, wx, bx, lam, name, tt=512):
    t = rx_rg.shape[0]
    tt = min(tt, t)
    w = GROUP_W

    def body(rx_ref, rg_ref, cw_ref, cb_ref, wa_ref, ba_ref, wx_ref, bx_ref, lam_ref, y_ref, z_ref, ext_ref, hc_ref):
        @pl.when(pl.program_id(1) == 0)
        def _():
            ext_ref[0:8, :] = jnp.zeros((8, w), F32)
            hc_ref[...] = jnp.zeros((8, w), F32)

        x = rx_ref[...]
        ext_ref[8:8 + tt, :] = x
        c, _, _, _ = _conv_taps(cw_ref[...], cb_ref[...], x, ext_ref, tt)
        ext_ref[0:8, :] = x[tt - 8:tt, :]
        sp = _softplus(-lam_ref[...])
        _, _, ig, a, mult = _lru_gates(c, wa_ref[...], ba_ref[...], wx_ref[...], bx_ref[...], sp)
        h = _scan_fwd(a, mult * (ig * c), hc_ref[7:8, :], tt)
        hc_ref[...] = h[tt - 8:tt, :]
        y_ref[...] = h
        rg = rg_ref[...]
        z_ref[...] = (h * rg * _sigmoid(rg)).astype(BF16)

    vec = pl.BlockSpec((1, w), lambda g, i: (0, g))
    mat = pl.BlockSpec((None, w, w), lambda g, i: (g, 0, 0))
    tile = pl.BlockSpec((tt, w), lambda g, i: (i, g))
    return pl.pallas_call(
        body, name=name, grid=(N_GROUPS, t // tt),
        in_specs=[tile, pl.BlockSpec((tt, w), lambda g, i: (i, N_GROUPS + g)),
                  pl.BlockSpec((4, w), lambda g, i: (0, g)), vec, mat, vec, mat, vec, vec],
        out_specs=[tile, tile],
        out_shape=[jax.ShapeDtypeStruct((t, D_RNN), F32), jax.ShapeDtypeStruct((t, D_RNN), BF16)],
        scratch_shapes=[pltpu.VMEM((tt + 8, w), F32), pltpu.VMEM((8, w), F32)],
        compiler_params=_params("parallel", "arbitrary"))(rx_rg, rx_rg, cw, cb, wa, ba, wx, bx, lam)


def _rnn_bwd(rx_rg, y, dz, cw, cb, wa, ba, wx, bx, lam, name, tt=512, ride=None):
    t = rx_rg.shape[0]
    tt = min(tt, t)
    nt = t // tt
    w = GROUP_W

    host = _Hosted(ride, 13, 5, 6)

    def body(*refs):
        host_refs, start, finish = host.split(refs)
        (rx_ref, rg_ref, rxt_ref, y_ref, yt_ref, dz_ref, cw_ref, cb_ref, wa_ref, ba_ref, wx_ref, bx_ref, lam_ref,
         drx_ref, drg_ref, st_ref, gda_ref, gdx_ref, ext_ref, dcx_ref, wcar_ref, acar_ref, dwa_ref, dwx_ref) = host_refs
        ii = pl.program_id(1)
        start((pl.program_id(0) == 0) & (ii == 0))

        @pl.when(ii == 0)
        def _():
            wcar_ref[...] = jnp.zeros((8, w), F32)
            acar_ref[...] = jnp.zeros((8, w), F32)
            dcx_ref[tt:tt + 8, :] = jnp.zeros((8, w), F32)
            st_ref[...] = jnp.zeros_like(st_ref)
            dwa_ref[...] = jnp.zeros_like(dwa_ref)
            dwx_ref[...] = jnp.zeros_like(dwx_ref)

        has_prev = jnp.where(ii == nt - 1, 0.0, 1.0)
        x = rx_ref[...]
        ext_ref[0:8, :] = rxt_ref[...] * has_prev
        ext_ref[8:8 + tt, :] = x
        cwv = cw_ref[...]
        c, x2, x1, x0 = _conv_taps(cwv, cb_ref[...], x, ext_ref, tt)
        lam = lam_ref[...]
        sp = _softplus(-lam)
        wa = wa_ref[...]
        wx = wx_ref[...]
        cb16, r, ig, a, mult = _lru_gates(c, wa, ba_ref[...], wx, bx_ref[...], sp)

        rg = rg_ref[...]
        sg = _sigmoid(rg)
        dz = dz_ref[...].astype(F32)
        yv = y_ref[...]
        drg_ref[...] = (dz * yv * (sg * (1.0 + rg * (1.0 - sg)))).astype(BF16)

        row = lax.broadcasted_iota(jnp.int32, (tt, w), 0)
        b = jnp.where(row < tt - 1, pltpu.roll(a, tt - 1, 0), acar_ref[0:1, :])
        dh = _scan_rev(b, dz * (rg * sg), wcar_ref[0:1, :], tt)
        wcar_ref[...] = dh[0:8, :]
        acar_ref[...] = a[0:8, :]

        hprev = jnp.where(row >= 1, pltpu.roll(yv, 1, 0), yt_ref[7:8, :] * has_prev)
        dmult = dh * (ig * c)
        dig = dh * mult * c
        dlog_a = dh * hprev * a - dmult * (a * a / mult)
        dpa = dlog_a * ((-LRU_C) * sp) * r * (1.0 - r)
        dpx = dig * ig * (1.0 - ig)
        dsp = jnp.sum(dlog_a * r, axis=0, keepdims=True) * (-LRU_C)
        dlam = dsp * (-_sigmoid(-lam))
        dpa16 = dpa.astype(BF16)
        dpx16 = dpx.astype(BF16)
        dwa_ref[...] += lax.dot_general(cb16, dpa16, TN_DIMS, preferred_element_type=F32)
        dwx_ref[...] += lax.dot_general(cb16, dpx16, TN_DIMS, preferred_element_type=F32)
        dc = (dh * mult * ig
              + lax.dot_general(dpa16, wa, NT_DIMS, preferred_element_type=F32)
              + lax.dot_general(dpx16, wx, NT_DIMS, preferred_element_type=F32))

        dcx_ref[0:tt, :] = dc
        drx = (cwv[3:4] * dc + cwv[2:3] * dcx_ref[1:1 + tt, :] + cwv[1:2] * dcx_ref[2:2 + tt, :]
               + cwv[0:1] * dcx_ref[3:3 + tt, :])
        drx_ref[...] = drx.astype(BF16)
        dcx_ref[tt:tt + 8, :] = dc[0:8, :]

        def colsum(v):
            return jnp.sum(v, axis=0, keepdims=True)

        st_ref[...] += _rows8([colsum(dc), colsum(dpa), colsum(dpx), dlam,
                               colsum(dc * x0), colsum(dc * x1), colsum(dc * x2), colsum(dc * x)], w)

        @pl.when(ii == nt - 1)
        def _():
            for blk in range(GROUP_W // RNN_BLOCK_W):
                rows = slice(blk * RNN_BLOCK_W, (blk + 1) * RNN_BLOCK_W)
                gda_ref[blk] = dwa_ref[rows, rows]
                gdx_ref[blk] = dwx_ref[rows, rows]

        finish((pl.program_id(0) == N_GROUPS - 1) & (ii == nt - 1))

    def rev(ii):
        return nt - 1 - ii

    def tail(g, ii):
        return (jnp.maximum(rev(ii) * (tt // 8) - 1, 0), g)

    vec = pl.BlockSpec((1, w), lambda g, ii: (0, g))
    mat = pl.BlockSpec((None, w, w), lambda g, ii: (g, 0, 0))
    tile = pl.BlockSpec((tt, w), lambda g, ii: (rev(ii), g))
    diag_shape = (N_GROUPS, GROUP_W // RNN_BLOCK_W, RNN_BLOCK_W, RNN_BLOCK_W)
    diag = pl.BlockSpec((None,) + diag_shape[1:], lambda g, ii: (g, 0, 0, 0))
    outs = pl.pallas_call(
        body, name=name, grid=(N_GROUPS, nt),
        in_specs=[tile, pl.BlockSpec((tt, w), lambda g, ii: (rev(ii), N_GROUPS + g)), pl.BlockSpec((8, w), tail),
                  tile, pl.BlockSpec((8, w), tail), tile,
                  pl.BlockSpec((4, w), lambda g, ii: (0, g)), vec, mat, vec, mat, vec, vec] + host.in_specs,
        out_specs=[tile, tile, pl.BlockSpec((8, w), lambda g, ii: (0, g)), diag, diag] + host.out_specs,
        out_shape=[jax.ShapeDtypeStruct((t, D_RNN), BF16), jax.ShapeDtypeStruct((t, D_RNN), BF16),
                   jax.ShapeDtypeStruct((8, D_RNN), F32),
                   jax.ShapeDtypeStruct(diag_shape, F32), jax.ShapeDtypeStruct(diag_shape, F32)] + host.out_shapes,
        scratch_shapes=[pltpu.VMEM((tt + 8, w), F32), pltpu.VMEM((tt + 8, w), F32), pltpu.VMEM((8, w), F32),
                        pltpu.VMEM((8, w), F32), pltpu.VMEM((w, w), F32), pltpu.VMEM((w, w), F32)] + host.scratch_shapes,
        compiler_params=_params("arbitrary" if ride else "parallel", "arbitrary"))(
            rx_rg, rx_rg, rx_rg, y, y, dz, cw, cb, wa, ba, wx, bx, lam, *host.arrays)
    res, landed = host.results(outs, 5)
    return (*res, landed) if ride else tuple(res)


def _half_mask(shape, half):
    lane = lax.broadcasted_iota(jnp.int32, shape, 1)
    return (lane >= HEAD_DIM) if half else (lane < HEAD_DIM)


def _dup_half(t, half):
    sel = jnp.where(_half_mask(t.shape, half), t, 0.0)
    return sel + pltpu.roll(sel, HEAD_DIM, 1)


def _band_geometry(n):
    qi = lax.broadcasted_iota(jnp.int32, (BLOCK, 2 * BLOCK), 0)
    kj = lax.broadcasted_iota(jnp.int32, (BLOCK, 2 * BLOCK), 1)
    dist = BLOCK + qi - kj
    first_key = jnp.where(n > 0, 0, BLOCK)
    valid = (dist >= 0) & (dist < BLOCK) & (kj >= first_key)
    return dist.astype(F32), valid


GROUP = 4


def _kv_dup(prev_ref, cur_ref, hk, scale=1.0):
    tile = hk // 2
    kt = jnp.concatenate([prev_ref[:, tile * LANE:(tile + 1) * LANE], cur_ref[:, tile * LANE:(tile + 1) * LANE]], axis=0)
    return (_dup_half(kt.astype(F32), hk % 2) * scale).astype(BF16)


def _fill_bias(bias_ref, sm_ref, n):
    distf, valid = _band_geometry(n)
    for head in range(N_Q_HEADS):
        bias_ref[head] = jnp.where(valid, -sm_ref[1, head] * distf, MASKED)


def _head_scores(q2s, half, kdup, bias):
    qm = jnp.where(_half_mask(q2s.shape, half), q2s, jnp.zeros_like(q2s))
    return qm, lax.dot_general(qm, kdup, NT_DIMS, preferred_element_type=F32) + bias


def _attn_specs(nb, clamp_last):
    def blk(n):
        return jnp.minimum(n, nb - 1) if clamp_last else n

    q_spec = pl.BlockSpec((BLOCK, D_MODEL), lambda n: (blk(n), 0))
    k_prev = pl.BlockSpec((BLOCK, D_KV), lambda n: (jnp.maximum(blk(n) - 1, 0), D_MODEL // D_KV))
    k_cur = pl.BlockSpec((BLOCK, D_KV), lambda n: (blk(n), D_MODEL // D_KV))
    v_prev = pl.BlockSpec((BLOCK, D_KV), lambda n: (jnp.maximum(blk(n) - 1, 0), D_MODEL // D_KV + 1))
    v_cur = pl.BlockSpec((BLOCK, D_KV), lambda n: (blk(n), D_MODEL // D_KV + 1))
    return q_spec, k_prev, k_cur, v_prev, v_cur


def _attn_fwd(sm, qkv, ag_ml, name):
    t = qkv.shape[0]
    nb = t // BLOCK

    def body(sm_ref, q_ref, kp_ref, kc_ref, vp_ref, vc_ref, ag_ref, y_ref, z_ref, lse_ref, bias_ref):
        n = pl.program_id(0)

        @pl.when(n <= 1)
        def _():
            _fill_bias(bias_ref, sm_ref, n)

        lane = lax.broadcasted_iota(jnp.int32, (BLOCK, LANE), 1)
        low = lane < HEAD_DIM
        lse = jnp.zeros((BLOCK, LANE), F32)
        for hk in range(N_Q_HEADS // GROUP):
            kdup = _kv_dup(kp_ref, kc_ref, hk)
            vdup = _kv_dup(vp_ref, vc_ref, hk)
            for k in (0, 1):
                c = slice((2 * hk + k) * LANE, (2 * hk + k + 1) * LANE)
                q2s = q_ref[:, c] * ATTN_SCALE
                outs = []
                for half in (0, 1):
                    head = GROUP * hk + 2 * k + half
                    _, s = _head_scores(q2s, half, kdup, bias_ref[head])
                    sink = sm_ref[0, head]
                    m = jnp.maximum(jnp.max(s, axis=1, keepdims=True), sink)
                    e = jnp.exp(s - m)
                    l = jnp.sum(e, axis=1, keepdims=True) + jnp.exp(sink - m)
                    outs.append(jnp.dot((e * (1.0 / l)).astype(BF16), vdup, preferred_element_type=F32))
                    lse = jnp.where(lane == head, m + jnp.log(l), lse)
                yt = jnp.where(low, outs[0], outs[1])
                y_ref[:, c] = yt
                ag = ag_ref[:, c].astype(F32)
                z_ref[:, c] = (yt * ag * _sigmoid(ag)).astype(BF16)
        lse_ref[...] = lse

    q_spec, k_prev, k_cur, v_prev, v_cur = _attn_specs(nb, False)
    wide = pl.BlockSpec((BLOCK, D_MODEL), lambda n: (n, 0))
    return pl.pallas_call(
        body, name=name, grid=(nb,),
        in_specs=[pl.BlockSpec(memory_space=pltpu.SMEM), q_spec, k_prev, k_cur, v_prev, v_cur, wide],
        out_specs=[wide, wide, pl.BlockSpec((BLOCK, LANE), lambda n: (n, 0))],
        out_shape=[jax.ShapeDtypeStruct((t, D_MODEL), F32), jax.ShapeDtypeStruct((t, D_MODEL), BF16),
                   jax.ShapeDtypeStruct((t, LANE), F32)],
        scratch_shapes=[pltpu.VMEM((N_Q_HEADS, BLOCK, 2 * BLOCK), F32)],
        compiler_params=_params("arbitrary"))(sm, qkv, qkv, qkv, qkv, qkv, ag_ml)


def _attn_bwd(sm, qkv, ag_ml, y, lse, dz, name, ride=None):
    t = qkv.shape[0]
    nb = t // BLOCK
    host = _Hosted(ride, 10, 4, 3)

    def body(*refs):
        host_refs, start, finish = host.split(refs)
        (sm_ref, q_ref, kp_ref, kc_ref, vp_ref, vc_ref, ag_ref, y_ref, lse_ref, dz_ref,
         dq_ref, dkv_ref, dag_ref, ds_ref, ck_ref, cv_ref, bias_ref) = host_refs
        n = pl.program_id(0)
        start(n == 0)
        start.middle(n == (3 * nb) // 4)

        @pl.when(n == 0)
        def _():
            ck_ref[...] = jnp.zeros_like(ck_ref)
            cv_ref[...] = jnp.zeros_like(cv_ref)
            ds_ref[...] = jnp.zeros_like(ds_ref)

        @pl.when(n <= 1)
        def _():
            _fill_bias(bias_ref, sm_ref, n)

        @pl.when(n < nb)
        def _():
            lane8 = lax.broadcasted_iota(jnp.int32, (8, LANE), 1)
            row8 = lax.broadcasted_iota(jnp.int32, (8, LANE), 0)
            dsink = jnp.zeros((8, LANE), F32)
            dk_heads, dv_heads = [], []
            lse_tile = lse_ref[...]
            for hk in range(N_Q_HEADS // GROUP):
                kdup = _kv_dup(kp_ref, kc_ref, hk)
                ks = _kv_dup(kp_ref, kc_ref, hk, ATTN_SCALE)
                vdup = _kv_dup(vp_ref, vc_ref, hk)
                qms, dyhs, y_rows = [], [], []
                for k in (0, 1):
                    c = slice((2 * hk + k) * LANE, (2 * hk + k + 1) * LANE)
                    ag = ag_ref[:, c].astype(F32)
                    sg = _sigmoid(ag)
                    dzt = dz_ref[:, c].astype(F32)
                    yt = y_ref[:, c]
                    dag_ref[:, c] = (dzt * yt * (sg * (1.0 + ag * (1.0 - sg)))).astype(BF16)
                    dyt = dzt * (ag * sg)
                    q2s = q_ref[:, c] * ATTN_SCALE
                    for half in (0, 1):
                        hm = _half_mask(q2s.shape, half)
                        qms.append(jnp.where(hm, q2s, jnp.zeros_like(q2s)))
                        dyhs.append(jnp.where(hm, dyt, 0.0))
                        y_rows.append(yt)
                qm4 = jnp.concatenate(qms, axis=0)
                dy4 = jnp.concatenate(dyhs, axis=0)
                dy4_16 = dy4.astype(BF16)
                s4 = lax.dot_general(qm4, kdup, NT_DIMS, preferred_element_type=F32)
                dp4 = lax.dot_general(dy4_16, vdup, NT_DIMS, preferred_element_type=F32)
                probs16, ds16 = [], []
                for r in range(GROUP):
                    head = GROUP * hk + r
                    rows = slice(r * BLOCK, (r + 1) * BLOCK)
                    lh = lse_tile[:, head:head + 1]
                    probs = jnp.exp(s4[rows] + bias_ref[head] - lh)
                    psink = jnp.exp(sm_ref[0, head] - lh)
                    delta = jnp.sum(dyhs[r] * y_rows[r], axis=1, keepdims=True)
                    ds16.append((probs * (dp4[rows] - delta)).astype(BF16))
                    probs16.append(probs.astype(BF16))
                    dsink = dsink + jnp.where((row8 == 0) & (lane8 == head),
                                              -jnp.sum(psink * delta, axis=0, keepdims=True), 0.0)
                ds4 = jnp.concatenate(ds16, axis=0)
                p4 = jnp.concatenate(probs16, axis=0)
                dq4 = jnp.dot(ds4, ks, preferred_element_type=F32)
                low = _half_mask((BLOCK, LANE), 0)
                for k in (0, 1):
                    c = slice((2 * hk + k) * LANE, (2 * hk + k + 1) * LANE)
                    dq_ref[:, c] = jnp.where(low, dq4[2 * k * BLOCK:(2 * k + 1) * BLOCK],
                                             dq4[(2 * k + 1) * BLOCK:(2 * k + 2) * BLOCK]).astype(BF16)
                dk_acc = lax.dot_general(ds4, qm4, TN_DIMS, preferred_element_type=F32)
                dv_acc = lax.dot_general(p4, dy4_16, TN_DIMS, preferred_element_type=F32)
                dk_heads.append(dk_acc + pltpu.roll(dk_acc, HEAD_DIM, 1))
                dv_heads.append(dv_acc + pltpu.roll(dv_acc, HEAD_DIM, 1))
            ds_ref[...] += dsink
            low = _half_mask((2 * BLOCK, LANE), 0)
            for tile in range(2):
                cols = slice(tile * LANE, (tile + 1) * LANE)
                dkt = jnp.where(low, dk_heads[2 * tile], dk_heads[2 * tile + 1])
                dvt = jnp.where(low, dv_heads[2 * tile], dv_heads[2 * tile + 1])
                dkv_ref[:, cols] = (ck_ref[:, cols] + dkt[0:BLOCK, :]).astype(BF16)
                dkv_ref[:, D_KV + tile * LANE:D_KV + (tile + 1) * LANE] = (cv_ref[:, cols] + dvt[0:BLOCK, :]).astype(BF16)
                ck_ref[:, cols] = dkt[BLOCK:2 * BLOCK, :]
                cv_ref[:, cols] = dvt[BLOCK:2 * BLOCK, :]

        @pl.when(n == nb)
        def _():
            dkv_ref[:, 0:D_KV] = ck_ref[...].astype(BF16)
            dkv_ref[:, D_KV:2 * D_KV] = cv_ref[...].astype(BF16)

        finish(n == nb)

    q_spec, k_prev, k_cur, v_prev, v_cur = _attn_specs(nb, True)
    wide = pl.BlockSpec((BLOCK, D_MODEL), lambda n: (jnp.minimum(n, nb - 1), 0))
    outs = pl.pallas_call(
        body, name=name, grid=(nb + 1,),
        in_specs=[pl.BlockSpec(memory_space=pltpu.SMEM), q_spec, k_prev, k_cur, v_prev, v_cur, wide, wide,
                  pl.BlockSpec((BLOCK, LANE), lambda n: (jnp.minimum(n, nb - 1), 0)), wide] + host.in_specs,
        out_specs=[wide, pl.BlockSpec((BLOCK, 2 * D_KV), lambda n: (jnp.maximum(n - 1, 0), 0)), wide,
                   pl.BlockSpec((8, LANE), lambda n: (0, 0))] + host.out_specs,
        out_shape=[jax.ShapeDtypeStruct((t, D_MODEL), BF16), jax.ShapeDtypeStruct((t, 2 * D_KV), BF16),
                   jax.ShapeDtypeStruct((t, D_MODEL), BF16), jax.ShapeDtypeStruct((8, LANE), F32)] + host.out_shapes,
        scratch_shapes=[pltpu.VMEM((BLOCK, D_KV), F32), pltpu.VMEM((BLOCK, D_KV), F32),
                        pltpu.VMEM((N_Q_HEADS, BLOCK, 2 * BLOCK), F32)] + host.scratch_shapes,
        compiler_params=_params("arbitrary"))(sm, qkv, qkv, qkv, qkv, qkv, ag_ml, y, lse, dz, *host.arrays)
    res, landed = host.results(outs, 4)
    return (*res, landed) if ride else tuple(res)


def _local_grads(x, target, p, project, reduce_out=None, reduce_in=None):
    h, rx_rg, qkv, ag_ml, wt, gathered = project(x, p["pre_g"])
    p = {**p, **gathered}
    y_attn, z_attn, lse = _attn_fwd(p["sm"], qkv, ag_ml, "attn_fwd")
    lru = (p["cw"], p["cb"], p["wbd_a"], p["b_a"], p["wbd_x"], p["b_x"], p["lam"])
    y_rnn, z_rnn = _rnn_fwd(rx_rg, *lru, "rnn_fwd")
    br_rnn, br_attn, merged, dout, dy, st_post = _branches_fwd(
        z_rnn, z_attn, ag_ml, p["b_gate"], p["w_rnn"], p["w_attn"], p["w_out"], x, target, p["post_g"], "branches_fwd")

    dbr_rnn, dbr_attn, d_ml, dz_rnn, dz_attn, st_merge = _branches_bwd(
        dout, br_rnn, br_attn, ag_ml, p["b_gate"], p["w_rnn"], p["w_attn"], p["w_out"], "branches_bwd")
    gw_out = _mm_tn(merged, dout, "gw_out")
    gw_rnn = _mm_tn(z_rnn, dbr_rnn, "gw_rnn")
    gw_attn = _mm_tn(z_attn, dbr_attn, "gw_attn")
    red_out = red_in = None
    if reduce_out is None:
        d_rx, d_rg, st_rnn, g_rg_a, g_rg_x = _rnn_bwd(rx_rg, y_rnn, dz_rnn, *lru, "rnn_bwd")
        dq, dkv, d_ag, st_sink = _attn_bwd(p["sm"], qkv, ag_ml, y_attn, lse, dz_attn, "attn_bwd")
    else:
        d_rx, d_rg, st_rnn, g_rg_a, g_rg_x, from_sibling = _rnn_bwd(
            rx_rg, y_rnn, dz_rnn, *lru, "rnn_bwd", ride=reduce_out[0](gw_rnn, gw_attn, gw_out))
        dq, dkv, d_ag, st_sink, red_out = _attn_bwd(p["sm"], qkv, ag_ml, y_attn, lse, dz_attn, "attn_bwd",
                                                    ride=reduce_out[1](from_sibling, g_rg_a, g_rg_x))

    segs = [d_rx, d_rg, dq, dkv, d_ag, d_ml]
    gwt = _mm_tn_seg(segs, h, "gw_in")
    if reduce_in is None:
        grad_x, st_pre = _input_grad(segs, wt, x, p["pre_g"], dy, "input_grad")
    else:
        grad_x, st_pre, red_in = _input_grad(segs, wt, x, p["pre_g"], dy, "input_grad", ride=reduce_in(gwt))
    return dict(grad_x=grad_x, gwt=gwt, gw_rnn=gw_rnn, gw_attn=gw_attn, gw_out=gw_out,
                st_post=st_post, st_merge=st_merge, st_rnn=st_rnn, st_sink=st_sink, st_pre=st_pre,
                g_rg_a=g_rg_a, g_rg_x=g_rg_x, red_out=red_out, red_in=red_in)


def _place():
    x, y, c = lax.axis_index("x"), lax.axis_index("y"), lax.axis_index("c")
    return x, y, c


def _gather_ride(shards):
    n = len(shards)

    def copies(ins, outs, sems):
        send_sems, recv_sems, local_sems = sems
        x, y, c = _place()
        me, sibling = (x, y, c), (x, y, 1 - c)
        chips = [(1 - x, y), (x, 1 - y), (1 - x, 1 - y)]

        def slot(a, dev):
            return outs[a].at[4 * dev[0] + 2 * dev[1] + dev[2]]

        def copy(a, k, block, to, src=None):
            return pltpu.make_async_remote_copy(
                src_ref=slot(a, block) if src is None else src, dst_ref=slot(a, block),
                send_sem=send_sems.at[a, k], recv_sem=recv_sems.at[a, k], device_id=to, device_id_type=MESH)

        mine = [pltpu.make_async_copy(ins[a], slot(a, me), local_sems.at[a]) for a in range(n)]
        first = []
        for a in range(n):
            first.append(copy(a, 0, me, sibling, src=ins[a]))
            first += [copy(a, 1 + j, me, (*chip, c), src=ins[a]) for j, chip in enumerate(chips)]
        return me, sibling, chips, c, copy, mine, first

    def start(ins, outs, sems):
        *_, mine, first = copies(ins, outs, sems)
        for cp in mine + first:
            cp.start()

    def middle(ins, outs, sems):
        me, sibling, chips, c, copy, _, _ = copies(ins, outs, sems)
        for j, chip in enumerate(chips):
            for a in range(n):
                copy(a, 1 + j, (*chip, c), me).wait_recv()
                copy(a, 4 + j, (*chip, c), sibling).start()

    def finish(ins, outs, sems):
        me, sibling, chips, c, copy, mine, first = copies(ins, outs, sems)
        passed = [copy(a, 4 + j, (*chip, c), sibling) for j, chip in enumerate(chips) for a in range(n)]
        for a in range(n):
            copy(a, 0, sibling, me).wait_recv()
            for j, chip in enumerate(chips):
                copy(a, 4 + j, (*chip, 1 - c), me).wait_recv()
        for cp in first + passed:
            cp.wait_send()
        for cp in mine:
            cp.wait()

    return _Ride(
        shards, [jax.ShapeDtypeStruct((N_DEV, *s.shape), s.dtype) for s in shards],
        [pltpu.SemaphoreType.DMA((n, 7)), pltpu.SemaphoreType.DMA((n, 7)), pltpu.SemaphoreType.DMA((n,))],
        start, finish, middle)


def _sibling_ride(scatter, whole):
    ns, nw = len(scatter), len(whole)

    def copies(ins, outs, sems):
        send_sems, recv_sems = sems
        x, y, c = _place()
        cps = [pltpu.make_async_remote_copy(
            src_ref=ins[a].at[2 * chip + (1 - c)], dst_ref=outs[a].at[chip],
            send_sem=send_sems.at[a * N_CHIP + chip], recv_sem=recv_sems.at[a * N_CHIP + chip],
            device_id=(x, y, 1 - c), device_id_type=MESH) for a in range(ns) for chip in range(N_CHIP)]
        cps += [pltpu.make_async_remote_copy(
            src_ref=ins[ns + a], dst_ref=outs[ns + a],
            send_sem=send_sems.at[ns * N_CHIP + a], recv_sem=recv_sems.at[ns * N_CHIP + a],
            device_id=(x, y, 1 - c), device_id_type=MESH) for a in range(nw)]
        return cps

    def start(ins, outs, sems):
        for cp in copies(ins, outs, sems):
            cp.start()

    def finish(ins, outs, sems):
        for cp in copies(ins, outs, sems):
            cp.wait()

    n_sem = ns * N_CHIP + nw
    return _Ride(
        list(scatter) + list(whole),
        [jax.ShapeDtypeStruct((N_CHIP, *s.shape[1:]), s.dtype) for s in scatter]
        + [jax.ShapeDtypeStruct(s.shape, s.dtype) for s in whole],
        [pltpu.SemaphoreType.DMA((n_sem,)), pltpu.SemaphoreType.DMA((n_sem,))], start, finish)


def _chips_ride(scatter, whole):
    ns, nw = len(scatter), len(whole)
    n = ns + nw

    def copies(ins, outs, sems):
        send_sems, recv_sems, local_sems = sems
        x, y, c = _place()
        own = 2 * x + y
        chips = [(1 - x, y), (x, 1 - y), (1 - x, 1 - y)]

        def src(a, chip_idx):
            return ins[a].at[chip_idx] if a < ns else ins[a]

        local = [pltpu.make_async_copy(src(a, own), outs[a].at[own], local_sems.at[a]) for a in range(n)]
        sent = [pltpu.make_async_remote_copy(
            src_ref=src(a, 2 * chip[0] + chip[1]), dst_ref=outs[a].at[own],
            send_sem=send_sems.at[a, j], recv_sem=recv_sems.at[a, own], device_id=(*chip, c), device_id_type=MESH)
            for a in range(n) for j, chip in enumerate(chips)]
        return chips, c, local, sent

    def start(ins, outs, sems):
        _, _, local, sent = copies(ins, outs, sems)
        for cp in local + sent:
            cp.start()

    def finish(ins, outs, sems):
        send_sems, recv_sems, _ = sems
        chips, c, local, sent = copies(ins, outs, sems)
        for a in range(n):
            for chip in chips:
                k = 2 * chip[0] + chip[1]
                pltpu.make_async_remote_copy(
                    src_ref=outs[a].at[k], dst_ref=outs[a].at[k], send_sem=send_sems.at[a, 0],
                    recv_sem=recv_sems.at[a, k], device_id=(*chip, c), device_id_type=MESH).wait_recv()
        for cp in sent:
            cp.wait_send()
        for cp in local:
            cp.wait()

    return _Ride(
        list(scatter) + list(whole),
        [jax.ShapeDtypeStruct(s.shape, s.dtype) for s in scatter]
        + [jax.ShapeDtypeStruct((N_CHIP, *s.shape), s.dtype) for s in whole],
        [pltpu.SemaphoreType.DMA((n, 3)), pltpu.SemaphoreType.DMA((n, N_CHIP)), pltpu.SemaphoreType.DMA((n,))],
        start, finish)


def _pair_sum_scatter(parts, recvs, core, name):
    na = len(parts)
    _, r, cdim = parts[0].shape
    tr = min(r, 416 if r % 416 == 0 else 128)

    def body(core_ref, *refs):
        del core_ref
        for a in range(na):
            refs[2 * na + a][...] = (refs[a][...] + refs[na + a][...]).astype(BF16)

    blk = (None, tr, cdim)
    mine = pl.BlockSpec(blk, lambda k, i, core_ref: (2 * k + core_ref[0], i, 0))
    slot = pl.BlockSpec(blk, lambda k, i, core_ref: (k, i, 0))
    return pl.pallas_call(
        body, name=name,
        grid_spec=pltpu.PrefetchScalarGridSpec(
            num_scalar_prefetch=1, grid=(N_CHIP, r // tr),
            in_specs=[mine] * na + [slot] * na, out_specs=[slot] * na),
        out_shape=[jax.ShapeDtypeStruct((N_CHIP, r, cdim), BF16)] * na,
        compiler_params=_params("parallel", "parallel"))(core, *parts, *recvs)


def _allreduce_small(pack, name):
    shape = pack.shape

    def body(x_ref, o_ref, sib_ref, chip_ref, send_sems, recv_sems):
        x, y, c = _place()
        own = 2 * x + y
        chips = [(1 - x, y), (x, 1 - y), (1 - x, 1 - y)]
        to_sibling = pltpu.make_async_remote_copy(
            src_ref=x_ref, dst_ref=sib_ref, send_sem=send_sems.at[0], recv_sem=recv_sems.at[0],
            device_id=(x, y, 1 - c), device_id_type=MESH)
        to_sibling.start()
        to_sibling.wait()
        chip_ref[own] = x_ref[...] + sib_ref[...]
        sent = [pltpu.make_async_remote_copy(
            src_ref=chip_ref.at[own], dst_ref=chip_ref.at[own], send_sem=send_sems.at[1 + j],
            recv_sem=recv_sems.at[1 + own], device_id=(*chip, c), device_id_type=MESH) for j, chip in enumerate(chips)]
        for cp in sent:
            cp.start()
        for chip in chips:
            k = 2 * chip[0] + chip[1]
            pltpu.make_async_remote_copy(
                src_ref=chip_ref.at[k], dst_ref=chip_ref.at[k], send_sem=send_sems.at[1],
                recv_sem=recv_sems.at[1 + k], device_id=(*chip, c), device_id_type=MESH).wait_recv()
        for cp in sent:
            cp.wait_send()
        o_ref[...] = (chip_ref[0] + chip_ref[1]) + (chip_ref[2] + chip_ref[3])

    return pl.pallas_call(
        body, name=name, out_shape=jax.ShapeDtypeStruct(shape, F32),
        in_specs=[pl.BlockSpec(memory_space=pltpu.VMEM)], out_specs=pl.BlockSpec(memory_space=pltpu.VMEM),
        scratch_shapes=[pltpu.VMEM(shape, F32), pltpu.VMEM((N_CHIP, *shape), F32),
                        pltpu.SemaphoreType.DMA((4,)), pltpu.SemaphoreType.DMA((1 + N_CHIP,))],
    )(pack)


def _adamw(g, w, m, v):
    m = ADAM_B1 * m + (1.0 - ADAM_B1) * g
    v = ADAM_B2 * v + (1.0 - ADAM_B2) * (g * g)
    m_hat = m / (1.0 - ADAM_B1 ** ADAM_STEP)
    v_hat = v / (1.0 - ADAM_B2 ** ADAM_STEP)
    delta = -ADAM_LR * (m_hat / (jnp.sqrt(v_hat) + ADAM_EPS) + ADAM_WD * w)
    return delta, m, v


def _adam_parts(parts, w, m, v, name, tr=None):
    npart, r, c = parts.shape
    tr = r if tr is None else min(tr, r)

    def body(p_ref, w_ref, m_ref, v_ref, g_ref, d_ref, nm_ref, nv_ref):
        g = p_ref[0].astype(F32)
        for k in range(1, npart):
            g = g + p_ref[k].astype(F32)
        g_ref[...] = g
        d_ref[...], nm_ref[...], nv_ref[...] = _adamw(g, w_ref[...], m_ref[...], v_ref[...])

    tile = pl.BlockSpec((tr, c), lambda i: (i, 0))
    return pl.pallas_call(
        body, name=name, grid=(r // tr,),
        in_specs=[pl.BlockSpec((npart, tr, c), lambda i: (0, i, 0)), tile, tile, tile],
        out_specs=[tile] * 4, out_shape=[jax.ShapeDtypeStruct((r, c), F32)] * 4,
        compiler_params=_params("parallel"))(parts, w, m, v)


def _block_diag(w):
    w4 = w.reshape(N_GROUPS, 4, RNN_BLOCK_W, RNN_BLOCK_W)
    eye = jnp.eye(4, dtype=w.dtype)
    return jnp.einsum("gbij,bc->gbicj", w4, eye).reshape(N_GROUPS, GROUP_W, GROUP_W).astype(BF16)


SMALL_ROWS = 16
ROW_PRE_G, ROW_BGATE, ROW_CONV_B, ROW_B_A, ROW_B_X, ROW_LAM, ROW_POST_G, ROW_LOSS, ROW_SINKS, ROW_CONV_W = 0, 1, 3, 4, 5, 6, 7, 8, 9, 10


def _pack_stats(st_pre, st_merge, st_rnn, st_post, st_sink, name):
    d = D_MODEL

    def body(pre_ref, mg_ref, rnn_ref, post_ref, sink_ref, o_ref):
        rnn = rnn_ref[...]
        sinks = jnp.concatenate([sink_ref[0:1, :], jnp.zeros((1, d - LANE), F32)], axis=1)
        o_ref[0:8, :] = _rows8([pre_ref[0:1, :], mg_ref[0:1, :], mg_ref[1:2, :], rnn[0:1], rnn[1:2], rnn[2:3], rnn[3:4],
                                post_ref[0:1, :]], d)
        o_ref[8:16, :] = _rows8([post_ref[1:2, :], sinks, rnn[4:5], rnn[5:6], rnn[6:7], rnn[7:8]], d)

    return pl.pallas_call(body, name=name, out_shape=jax.ShapeDtypeStruct((SMALL_ROWS, d), F32))(
        st_pre, st_merge, st_rnn, st_post, st_sink)


def _adam_small(total, w, m, v, name):
    d = D_MODEL
    n_in = len(w)

    def pack(refs):
        pre, bg, cbias, ba, bx, lam, post, sinks = [r[...] for r in refs]
        top = _rows8([pre, bg[:, 0:d], bg[:, d:2 * d], cbias, ba, bx, lam, post], d)
        return jnp.concatenate([top, _rows8([jnp.zeros((1, d), F32), sinks], d)], axis=0)

    def body(*refs):
        p_ref = refs[0]
        w_refs, m_refs, v_refs = (refs[1 + k * n_in:1 + (k + 1) * n_in] for k in range(3))
        outs = refs[1 + 3 * n_in:]
        g = p_ref[...]
        res = (g,) + _adamw(g, pack(w_refs), pack(m_refs), pack(v_refs))
        for k in range(4):
            outs[k][...] = res[k]
            outs[4 + k][...] = jnp.concatenate([res[k][ROW_BGATE:ROW_BGATE + 1], res[k][ROW_BGATE + 1:ROW_BGATE + 2]], axis=1)

    return pl.pallas_call(
        body, name=name,
        out_shape=[jax.ShapeDtypeStruct((SMALL_ROWS, d), F32)] * 4 + [jax.ShapeDtypeStruct((1, 2 * d), F32)] * 4,
    )(total, *w, *m, *v)


def kernel(x, pre_norm_g, w_in, b_gate, conv_w, conv_b, w_rg_a, b_rg_a, w_rg_x, b_rg_x, lru_lambda, attn_sinks, w_rnn_out, w_attn_out, w_out, post_norm_g, loss_target, m_pre_norm_g, m_w_in, m_b_gate, m_conv_w, m_conv_b, m_w_rg_a, m_b_rg_a, m_w_rg_x, m_b_rg_x, m_lru_lambda, m_attn_sinks, m_w_rnn_out, m_w_attn_out, m_w_out, m_post_norm_g, v_pre_norm_g, v_w_in, v_b_gate, v_conv_w, v_conv_b, v_w_rg_a, v_b_rg_a, v_w_rg_x, v_b_rg_x, v_lru_lambda, v_attn_sinks, v_w_rnn_out, v_w_attn_out, v_w_out, v_post_norm_g):
    cx, cy, cc = _place()
    dev = 4 * cx + 2 * cy + cc
    core = jnp.reshape(cc, (1,)).astype(jnp.int32)

    w_in_t, m_in_t, v_in_t = (jnp.transpose(a[0]) for a in (w_in, m_w_in, v_w_in))
    wt_shard = w_in_t.astype(BF16)

    def project(xs, pre_g):
        late = _gather_ride([w_rnn_out[0].astype(BF16), w_attn_out[0].astype(BF16), w_out[0].astype(BF16), conv_w[0]])
        h, rx_rg, qkv, ag_ml, wt_all, (w_rnn_all, w_attn_all, w_out_all, cw_all) = _gather_project(
            xs, pre_g, wt_shard, "gather_project", ride=late)
        gathered = dict(w_rnn=w_rnn_all.reshape(D_RNN, D_MODEL), w_attn=w_attn_all.reshape(D_MODEL, D_MODEL),
                        w_out=w_out_all.reshape(D_MODEL, D_MODEL), cw=jnp.transpose(cw_all, (1, 0, 2)).reshape(4, D_RNN))
        return h, rx_rg, qkv, ag_ml, wt_all.reshape(D_IN, D_MODEL), gathered

    heads = jnp.arange(1, N_Q_HEADS + 1, dtype=F32)
    slopes = jnp.exp2(-ALIBI_MAX_BIAS * heads / N_Q_HEADS)
    b_a = b_rg_a.reshape(1, D_RNN)
    b_x = b_rg_x.reshape(1, D_RNN)
    p = dict(
        pre_g=pre_norm_g, post_g=post_norm_g, b_gate=b_gate, cb=conv_b,
        wbd_a=_block_diag(w_rg_a[0]), b_a=b_a, wbd_x=_block_diag(w_rg_x[0]), b_x=b_x, lam=lru_lambda,
        sm=jnp.pad(attn_sinks, ((0, 1), (0, 0))) + jnp.pad(slopes[None, :], ((1, 0), (0, 0))))

    flat = (RNN_BLOCKS * RNN_BLOCK_W, RNN_BLOCK_W)

    out_scatter = []

    def out_sibling(gw_rnn, gw_attn, gw_out):
        out_scatter.extend(gw.reshape(N_DEV, SHARD_OUT, D_MODEL) for gw in (gw_rnn, gw_attn, gw_out))
        return _sibling_ride(out_scatter, [])

    def out_chips(from_sibling, g_rg_a, g_rg_x):
        return _join_rides(_chips_ride(_pair_sum_scatter(out_scatter, from_sibling, core, "pair_out"), []),
                           _gather_ride([g_rg_a.reshape(flat), g_rg_x.reshape(flat)]))

    def reduce_in(gwt):
        scatter = [gwt.reshape(N_DEV, SHARD_IN, D_MODEL)]
        from_sibling = _run_ride(_sibling_ride(scatter, []), "sibling_in")
        return _chips_ride(_pair_sum_scatter(scatter, from_sibling, core, "pair_in"), [])

    g = _local_grads(x[0], loss_target[0], p, project, (out_sibling, out_chips), reduce_in)
    small = _allreduce_small(
        _pack_stats(g["st_pre"], g["st_merge"], g["st_rnn"], g["st_post"], g["st_sink"], "pack_stats"), "allreduce_small")

    out = {}
    red = g["red_out"]
    out["w_in"] = [jnp.transpose(o) for o in _adam_parts(g["red_in"][0], w_in_t, m_in_t, v_in_t, "adam_w_in", tr=SHARD_IN // 2)]
    out["w_rnn_out"] = _adam_parts(red[0], w_rnn_out[0], m_w_rnn_out[0], v_w_rnn_out[0], "adam_w_rnn_out")
    out["w_attn_out"] = _adam_parts(red[1], w_attn_out[0], m_w_attn_out[0], v_w_attn_out[0], "adam_w_attn_out")
    out["w_out"] = _adam_parts(red[2], w_out[0], m_w_out[0], v_w_out[0], "adam_w_out")
    out["w_rg_a"] = _adam_parts(red[3], w_rg_a.reshape(flat), m_w_rg_a.reshape(flat), v_w_rg_a.reshape(flat), "adam_w_rg_a", tr=256)
    out["w_rg_x"] = _adam_parts(red[4], w_rg_x.reshape(flat), m_w_rg_x.reshape(flat), v_w_rg_x.reshape(flat), "adam_w_rg_x", tr=256)

    def rows(pre, bg, cbias, ba, bx, lam, post, sinks):
        return (pre, bg, cbias, ba.reshape(1, D_RNN), bx.reshape(1, D_RNN), lam, post,
                jnp.pad(sinks, ((0, 0), (0, D_MODEL - N_Q_HEADS))))

    small_out = _adam_small(
        small,
        rows(pre_norm_g, b_gate, conv_b, b_rg_a, b_rg_x, lru_lambda, post_norm_g, attn_sinks),
        rows(m_pre_norm_g, m_b_gate, m_conv_b, m_b_rg_a, m_b_rg_x, m_lru_lambda, m_post_norm_g, m_attn_sinks),
        rows(v_pre_norm_g, v_b_gate, v_conv_b, v_b_rg_a, v_b_rg_x, v_lru_lambda, v_post_norm_g, v_attn_sinks),
        "adam_small")
    packed, bgate_out = small_out[:4], small_out[4:]
    g_cw = lax.dynamic_slice(packed[0][ROW_CONV_W:ROW_CONV_W + 4], (0, dev * SHARD_OUT), (4, SHARD_OUT))
    out["conv_w"] = _adam_parts(g_cw[None], conv_w[0], m_conv_w[0], v_conv_w[0], "adam_conv_w")

    def unpack(kind, name):
        if name == "b_gate":
            return bgate_out[kind]
        row = dict(pre_norm_g=ROW_PRE_G, conv_b=ROW_CONV_B, b_rg_a=ROW_B_A, b_rg_x=ROW_B_X, lru_lambda=ROW_LAM,
                   post_norm_g=ROW_POST_G, attn_sinks=ROW_SINKS)[name]
        r = packed[kind][row:row + 1]
        if name == "attn_sinks":
            return r[:, 0:N_Q_HEADS]
        if name in ("b_rg_a", "b_rg_x"):
            return r.reshape(1, RNN_BLOCKS, RNN_BLOCK_W)
        return r

    shapes = dict(w_in=(1, D_MODEL, SHARD_IN), w_rnn_out=(1, SHARD_OUT, D_MODEL), w_attn_out=(1, SHARD_OUT, D_MODEL),
                  w_out=(1, SHARD_OUT, D_MODEL), w_rg_a=(1, RNN_BLOCKS, RNN_BLOCK_W, RNN_BLOCK_W),
                  w_rg_x=(1, RNN_BLOCKS, RNN_BLOCK_W, RNN_BLOCK_W), conv_w=(1, 4, SHARD_OUT))
    weights = ["pre_norm_g", "w_in", "b_gate", "conv_w", "conv_b", "w_rg_a", "b_rg_a", "w_rg_x", "b_rg_x",
               "lru_lambda", "attn_sinks", "w_rnn_out", "w_attn_out", "w_out", "post_norm_g"]
    results = []
    for kind in range(4):
        for name in weights:
            if name in out:
                results.append(out[name][kind].reshape(shapes[name]))
            else:
                results.append(unpack(kind, name))
    loss = 0.5 / D_MODEL * jnp.sum(packed[0][ROW_LOSS])
    return (loss, g["grad_x"][None], *results)
```

```python
import functools

import jax
import jax.numpy as jnp
from jax import lax
from jax.experimental import pallas as pl
from jax.experimental.pallas import tpu as pltpu

F32 = jnp.float32
BF16 = jnp.bfloat16

D_MODEL = 1024
D_RNN = 1024
RNN_BLOCKS = 16
RNN_BLOCK_W = 64
LRU_C = 8.0
N_Q_HEADS = 16
HEAD_DIM = 64
D_KV = 256
BLOCK = 128
ALIBI_MAX_BIAS = 8.0
EPS = 1e-6
D_IN = 6656
N_DEV = 8
N_CHIP = 4
SHARD_IN = D_IN // N_DEV
SHARD_OUT = D_MODEL // N_DEV
ATTN_SCALE = HEAD_DIM ** -0.5
MASKED = -1e30

ADAM_LR = 0.001
ADAM_B1 = 0.9
ADAM_B2 = 0.999
ADAM_EPS = 1e-08
ADAM_WD = 0.01
ADAM_STEP = 10

VMEM_LIMIT_BYTES = 52 * 1024 * 1024
LANE = 128
GROUP_W = 256
N_GROUPS = D_RNN // GROUP_W
SEG_CHUNK = 512

NT_DIMS = (((1,), (1,)), ((), ()))
TN_DIMS = (((0,), (0,)), ((), ()))
MESH = pl.DeviceIdType.MESH
ANY = pl.BlockSpec(memory_space=pl.ANY)


def _params(*semantics):
    return pltpu.CompilerParams(dimension_semantics=semantics, vmem_limit_bytes=VMEM_LIMIT_BYTES)


def _sigmoid(x):
    return 0.5 * jnp.tanh(0.5 * x) + 0.5


def _log1p(e):
    u = 1.0 + e
    den = jnp.where(u == 1.0, 1.0, u - 1.0)
    return jnp.where(u == 1.0, e, jnp.log(u) * (e / den))


def _softplus(z):
    return jnp.maximum(z, 0.0) + _log1p(jnp.exp(-jnp.abs(z)))


def _rows8(rows, width):
    idx = lax.broadcasted_iota(jnp.int32, (8, width), 0)
    out = jnp.zeros((8, width), F32)
    for r, v in enumerate(rows):
        out = jnp.where(idx == r, v, out)
    return out


class _Ride:
    def __init__(self, arrays, out_shapes, scratch_shapes, start, finish, middle=None):
        self.arrays, self.out_shapes, self.scratch_shapes = list(arrays), list(out_shapes), list(scratch_shapes)
        self.start, self.finish, self.middle = start, finish, middle


class _Hosted:
    def __init__(self, ride, n_in, n_out, n_scratch=0):
        self.ride = ride
        self.sizes = (n_in, len(ride.arrays) if ride else 0, n_out, len(ride.out_shapes) if ride else 0, n_scratch)
        self.arrays = ride.arrays if ride else []
        self.in_specs = [ANY] * len(self.arrays)
        self.out_shapes = ride.out_shapes if ride else []
        self.out_specs = [ANY] * len(self.out_shapes)
        self.scratch_shapes = ride.scratch_shapes if ride else []

    def split(self, refs):
        n_in, r_in, n_out, r_out, n_scr = self.sizes
        cuts = [0, n_in, n_in + r_in, n_in + r_in + n_out, n_in + r_in + n_out + r_out, n_in + r_in + n_out + r_out + n_scr]
        host_in, ride_in, host_out, ride_out, host_scr = (refs[cuts[k]:cuts[k + 1]] for k in range(5))
        ride_scr = refs[cuts[5]:]

        def start(when):
            if self.ride is not None:
                pl.when(when)(lambda: self.ride.start(ride_in, ride_out, ride_scr))

        def finish(when):
            if self.ride is not None:
                pl.when(when)(lambda: self.ride.finish(ride_in, ride_out, ride_scr))

        def middle(when):
            if self.ride is not None and self.ride.middle is not None:
                pl.when(when)(lambda: self.ride.middle(ride_in, ride_out, ride_scr))

        start.middle = middle
        return tuple(host_in) + tuple(host_out) + tuple(host_scr), start, finish

    def results(self, outs, n_out):
        outs = list(outs) if isinstance(outs, (list, tuple)) else [outs]
        return outs[:n_out], outs[n_out:]


def _join_rides(a, b):
    na, nb_ = len(a.arrays), len(b.arrays)
    oa = len(a.out_shapes)
    sa = len(a.scratch_shapes)

    def both(fa, fb):
        def run(ins, outs, sems):
            if fa is not None:
                fa(ins[:na], outs[:oa], sems[:sa])
            if fb is not None:
                fb(ins[na:na + nb_], outs[oa:], sems[sa:])
        return run

    middle = both(a.middle, b.middle) if (a.middle or b.middle) else None
    return _Ride(a.arrays + b.arrays, a.out_shapes + b.out_shapes, a.scratch_shapes + b.scratch_shapes,
                 both(a.start, b.start), both(a.finish, b.finish), middle)


def _run_ride(ride, name):
    n_in, n_out = len(ride.arrays), len(ride.out_shapes)

    def body(*refs):
        ins, outs, sems = refs[:n_in], refs[n_in:n_in + n_out], refs[n_in + n_out:]
        ride.start(ins, outs, sems)
        ride.finish(ins, outs, sems)

    return pl.pallas_call(
        body, name=name, in_specs=[ANY] * n_in, out_specs=[ANY] * n_out,
        out_shape=ride.out_shapes, scratch_shapes=ride.scratch_shapes)(*ride.arrays)


def _load_resident(w_hbm, w_vmem, sems, first):
    def piece(c):
        rows = pl.ds(c * SEG_CHUNK, SEG_CHUNK)
        return pltpu.make_async_copy(w_hbm.at[rows], w_vmem.at[rows], sems.at[c])

    @pl.when(first)
    def _():
        for c in range(w_vmem.shape[0] // SEG_CHUNK):
            piece(c).start()

    def ready(c):
        @pl.when(first)
        def _():
            piece(c).wait()

    return ready


CHIP_ROWS = 2 * SHARD_IN
PROJ_WIDTHS = (2 * D_RNN, D_MODEL + 2 * D_KV, 3 * D_MODEL)
PROJ_DTYPES = (F32, BF16, BF16)


def _chip_pieces():
    starts = [0, PROJ_WIDTHS[0], PROJ_WIDTHS[0] + PROJ_WIDTHS[1], D_IN]
    pieces = []
    for k in range(N_CHIP):
        lo, hi = k * CHIP_ROWS, (k + 1) * CHIP_ROWS
        cur = []
        for a in range(len(PROJ_WIDTHS)):
            s0, s1 = max(lo, starts[a]), min(hi, starts[a + 1])
            if s0 < s1:
                cur.append((a, s0 - starts[a], s1 - s0, s0 - lo))
        pieces.append(cur)
    return pieces


def _gather_project(x, g, wt_shard, name, tm=512):
    t, k = x.shape
    tm = min(tm, t)
    nt = t // tm
    pieces = _chip_pieces()

    def body(x_ref, g_ref, shard_ref, h_out, rx_ref, qkv_ref, ag_ref, wt_all,
             w_c, stage, o32, o16, h_all, send_sems, recv_sems, local_sem, stage_sems, out_sems, h_sems):
        s, ti = pl.program_id(0), pl.program_id(1)
        px, py, pc = _place()
        me, sibling = (px, py, pc), (px, py, 1 - pc)
        chips = [(px, py), (1 - px, py), (px, 1 - py), (1 - px, 1 - py)]
        outs = (rx_ref, qkv_ref, ag_ref)

        def slot(dev):
            return wt_all.at[4 * dev[0] + 2 * dev[1] + dev[2]]

        def copy(kk, block, to, src=None):
            return pltpu.make_async_remote_copy(
                src_ref=slot(block) if src is None else src, dst_ref=slot(block),
                send_sem=send_sems.at[kk], recv_sem=recv_sems.at[kk], device_id=to, device_id_type=MESH)

        mine = pltpu.make_async_copy(shard_ref, slot(me), local_sem)
        first = [copy(0, me, sibling, src=shard_ref)] + [copy(1 + j, me, (*chips[1 + j], pc), src=shard_ref) for j in range(3)]
        passed = [copy(4 + j, (*chips[1 + j], pc), sibling) for j in range(3)]

        @pl.when((s == 0) & (ti == 0))
        def _():
            mine.start()
            for cp in first:
                cp.start()

        for step in range(N_CHIP):
            @pl.when((s == step) & (ti == 0))
            def _(step=step):
                chip = chips[step]
                if step == 0:
                    mine.wait()
                    copy(0, sibling, me).wait_recv()
                else:
                    copy(step, (*chip, pc), me).wait_recv()
                    passed[step - 1].start()
                    copy(3 + step, (*chip, 1 - pc), me).wait_recv()
                loads = [pltpu.make_async_copy(slot((*chip, core)), stage.at[pl.ds(core * SHARD_IN, SHARD_IN)], stage_sems.at[core])
                         for core in (0, 1)]
                for cp in loads:
                    cp.start()
                for cp in loads:
                    cp.wait()
                w_c[...] = stage[...].T

        rows = pl.ds(pl.multiple_of(ti * tm, tm), tm)

        @pl.when(s == 0)
        def _():
            xv = x_ref[...]
            h_all[rows, :] = (xv * lax.rsqrt(jnp.mean(xv * xv, axis=-1, keepdims=True) + EPS) * g_ref[...]).astype(BF16)

        res = jnp.dot(h_all[rows, :], w_c[...], preferred_element_type=F32)

        n = s * nt + ti
        buf = lax.rem(n, 2)
        chip_idx = [2 * cx + cy for cx, cy in chips]

        def chip_at(step):
            return jnp.where(step == 0, chip_idx[0], jnp.where(step == 1, chip_idx[1], jnp.where(step == 2, chip_idx[2], chip_idx[3])))

        def h_write(b, tile):
            return pltpu.make_async_copy(h_all.at[pl.ds(tile * tm, tm)], h_out.at[pl.ds(tile * tm, tm)], h_sems.at[b])

        def writes(kchip, b, tile):
            cps = []
            for idx, (a, col, w, src) in enumerate(pieces[kchip]):
                staged = (o16 if PROJ_DTYPES[a] == BF16 else o32).at[b, :, pl.ds(src, w)]
                cps.append(pltpu.make_async_copy(staged, outs[a].at[pl.ds(tile * tm, tm), pl.ds(col, w)], out_sems.at[b, idx]))
            return cps

        o32[buf] = res
        o16[buf] = res.astype(BF16)
        kcur = chip_at(s)

        @pl.when(n > 0)
        def _():
            kprev = chip_at(lax.div(n - 1, nt))
            for kchip in range(N_CHIP):
                @pl.when(kprev == kchip)
                def _(kchip=kchip):
                    for cp in writes(kchip, 1 - buf, lax.rem(n - 1, nt)):
                        cp.wait()

            @pl.when(n <= nt)
            def _():
                h_write(1 - buf, n - 1).wait()

        @pl.when(s == 0)
        def _():
            h_write(buf, ti).start()

        for kchip in range(N_CHIP):
            @pl.when(kcur == kchip)
            def _(kchip=kchip):
                for cp in writes(kchip, buf, ti):
                    cp.start()

        @pl.when(n == N_CHIP * nt - 1)
        def _():
            for kchip in range(N_CHIP):
                @pl.when(kcur == kchip)
                def _(kchip=kchip):
                    for cp in writes(kchip, buf, ti):
                        cp.wait()
            for cp in first + passed:
                cp.wait_send()

    tile = pl.BlockSpec((tm, k), lambda s, ti: (jnp.where(s == 0, ti, nt - 1), 0))
    return pl.pallas_call(
        body, name=name, grid=(N_CHIP, nt),
        in_specs=[tile, pl.BlockSpec((1, k), lambda s, ti: (0, 0)), ANY],
        out_specs=[ANY, ANY, ANY, ANY, ANY],
        out_shape=[jax.ShapeDtypeStruct((t, k), BF16)]
        + [jax.ShapeDtypeStruct((t, w), dt) for w, dt in zip(PROJ_WIDTHS, PROJ_DTYPES)]
        + [jax.ShapeDtypeStruct((N_DEV, SHARD_IN, k), BF16)],
        scratch_shapes=[pltpu.VMEM((k, CHIP_ROWS), BF16), pltpu.VMEM((CHIP_ROWS, k), BF16),
                        pltpu.VMEM((2, tm, CHIP_ROWS), F32), pltpu.VMEM((2, tm, CHIP_ROWS), BF16), pltpu.VMEM((t, k), BF16),
                        pltpu.SemaphoreType.DMA((7,)), pltpu.SemaphoreType.DMA((7,)), pltpu.SemaphoreType.DMA,
                        pltpu.SemaphoreType.DMA((2,)), pltpu.SemaphoreType.DMA((2, 2)), pltpu.SemaphoreType.DMA((2,))],
        compiler_params=_params("arbitrary", "arbitrary"))(x, g, wt_shard)


def _mm_tn(a, b, name, tm=512, tk=4096):
    ktok, m = a.shape
    n = b.shape[1]
    tk = min(tk, ktok)

    def body(a_ref, b_ref, o_ref):
        @pl.when(pl.program_id(1) == 0)
        def _():
            o_ref[...] = jnp.zeros_like(o_ref)

        o_ref[...] += lax.dot_general(a_ref[...], b_ref[...], TN_DIMS, preferred_element_type=F32)

    return pl.pallas_call(
        body, name=name, grid=(m // tm, ktok // tk),
        in_specs=[pl.BlockSpec((tk, tm), lambda i, kk: (kk, i)), pl.BlockSpec((tk, n), lambda i, kk: (kk, 0))],
        out_specs=pl.BlockSpec((tm, n), lambda i, kk: (i, 0)),
        out_shape=jax.ShapeDtypeStruct((m, n), F32),
        compiler_params=_params("parallel", "arbitrary"))(a, b)


def _segment_chunks(segs):
    bounds = [0]
    for s in segs:
        bounds.append(bounds[-1] + s.shape[1] // SEG_CHUNK)
    return bounds


def _input_grad(segs, wt, x, g, dy, name, tm=512, ride=None):
    m = segs[0].shape[0]
    rows, n = wt.shape
    tm = min(tm, m)
    bounds = _segment_chunks(segs)
    n_seg = len(segs)
    ni = m // tm
    host = _Hosted(ride, n_seg + 4, 2, 2)

    def body(*refs):
        host_refs, start, finish = host.split(refs)
        a_refs = host_refs[:n_seg]
        wt_hbm, x_ref, g_ref, dy_ref, gx_ref, st_ref, wt_vmem, sems = host_refs[n_seg:]
        i = pl.program_id(0)
        start(i == 0)

        @pl.when(i == 0)
        def _():
            st_ref[...] = jnp.zeros_like(st_ref)

        ready = _load_resident(wt_hbm, wt_vmem, sems, i == 0)
        dh = None
        for s in range(n_seg):
            for c in range(bounds[s], bounds[s + 1]):
                ready(c)
            part = jnp.dot(a_refs[s][...], wt_vmem[bounds[s] * SEG_CHUNK:bounds[s + 1] * SEG_CHUNK, :], preferred_element_type=F32)
            dh = part if dh is None else dh + part
        xv = x_ref[...]
        r = lax.rsqrt(jnp.mean(xv * xv, axis=-1, keepdims=True) + EPS)
        xn = xv * r
        dxn = dh * g_ref[...]
        gx_ref[...] = dy_ref[...] + r * (dxn - xn * jnp.mean(dxn * xn, axis=-1, keepdims=True))
        st_ref[...] += _rows8([jnp.sum(dh * xn, axis=0, keepdims=True)], n)
        finish(i == ni - 1)

    tile = pl.BlockSpec((tm, n), lambda i: (i, 0))
    outs = pl.pallas_call(
        body, name=name, grid=(ni,),
        in_specs=[pl.BlockSpec((tm, sg.shape[1]), lambda i: (i, 0)) for sg in segs]
        + [ANY, tile, pl.BlockSpec((1, n), lambda i: (0, 0)), tile] + host.in_specs,
        out_specs=[tile, pl.BlockSpec((8, n), lambda i: (0, 0))] + host.out_specs,
        out_shape=[jax.ShapeDtypeStruct((m, n), F32), jax.ShapeDtypeStruct((8, n), F32)] + host.out_shapes,
        scratch_shapes=[pltpu.VMEM((rows, n), wt.dtype), pltpu.SemaphoreType.DMA((rows // SEG_CHUNK,))] + host.scratch_shapes,
        compiler_params=_params("arbitrary"))(*segs, wt, x, g, dy, *host.arrays)
    res, landed = host.results(outs, 2)
    return (*res, landed) if ride else tuple(res)


def _mm_tn_seg(segs, b, name):
    ktok = segs[0].shape[0]
    n = b.shape[1]
    bounds = _segment_chunks(segs)
    n_seg = len(segs)
    nc = bounds[-1]
    seg_of = [s for s in range(n_seg) for _ in range(bounds[s], bounds[s + 1])]

    def body(*refs):
        a_hbm, b_hbm, o_ref = refs[:n_seg], refs[n_seg], refs[n_seg + 1]
        a_buf, b_vmem, a_sems, b_sem = refs[n_seg + 2:]
        c = pl.program_id(0)

        def fetch(cc):
            s = seg_of[cc]
            cols = pl.ds((cc - bounds[s]) * SEG_CHUNK, SEG_CHUNK)
            return pltpu.make_async_copy(a_hbm[s].at[:, cols], a_buf.at[cc % 2], a_sems.at[cc % 2])

        @pl.when(c == 0)
        def _():
            whole = pltpu.make_async_copy(b_hbm, b_vmem, b_sem)
            whole.start()
            fetch(0).start()
            whole.wait()

        for cc in range(nc):
            @pl.when(c == cc)
            def _(cc=cc):
                if cc + 1 < nc:
                    fetch(cc + 1).start()
                fetch(cc).wait()

        o_ref[...] = lax.dot_general(a_buf[c % 2], b_vmem[...], TN_DIMS, preferred_element_type=F32)

    return pl.pallas_call(
        body, name=name, grid=(nc,),
        in_specs=[ANY] * (n_seg + 1), out_specs=pl.BlockSpec((SEG_CHUNK, n), lambda c: (c, 0)),
        out_shape=jax.ShapeDtypeStruct((nc * SEG_CHUNK, n), F32),
        scratch_shapes=[pltpu.VMEM((2, ktok, SEG_CHUNK), segs[0].dtype), pltpu.VMEM((ktok, n), b.dtype),
                        pltpu.SemaphoreType.DMA((2,)), pltpu.SemaphoreType.DMA],
        compiler_params=_params("arbitrary"))(*segs, b)


def _branches_fwd(z_rnn, z_attn, ag_ml, b_gate, w_rnn, w_attn, w_out, x, target, g_post, name, tm=512):
    t, d = x.shape
    tm = min(tm, t)

    def body(zr_ref, za_ref, lr_ref, la_ref, br_ref, ba_ref, wr_ref, wa_ref, wo_ref, x_ref, t_ref, g_ref,
             brr_ref, bra_ref, mg_ref, do_ref, dy_ref, st_ref):
        @pl.when(pl.program_id(0) == 0)
        def _():
            st_ref[...] = jnp.zeros_like(st_ref)

        br_rnn = jnp.dot(zr_ref[...], wr_ref[...], preferred_element_type=F32)
        br_attn = jnp.dot(za_ref[...], wa_ref[...], preferred_element_type=F32)
        brr_ref[...] = br_rnn.astype(BF16)
        bra_ref[...] = br_attn.astype(BF16)
        g_rnn = _sigmoid(lr_ref[...].astype(F32) + br_ref[...])
        g_attn = _sigmoid(la_ref[...].astype(F32) + ba_ref[...])
        merged = (g_rnn * br_rnn + g_attn * br_attn).astype(BF16)
        mg_ref[...] = merged
        o = jnp.dot(merged, wo_ref[...], preferred_element_type=F32)
        g = g_ref[...]
        r = lax.rsqrt(jnp.mean(o * o, axis=-1, keepdims=True) + EPS)
        nrm = o * r
        err = x_ref[...] + nrm * g - t_ref[...]
        dy = err * (1.0 / d)
        dy_ref[...] = dy
        dn = dy * g
        do_ref[...] = (r * (dn - nrm * jnp.mean(dn * nrm, axis=-1, keepdims=True))).astype(BF16)
        st_ref[...] += _rows8([jnp.sum(dy * nrm, axis=0, keepdims=True), jnp.sum(err * err, axis=0, keepdims=True)], d)

    tile = pl.BlockSpec((tm, d), lambda i: (i, 0))
    weight = pl.BlockSpec((d, d), lambda i: (0, 0))
    bf = jax.ShapeDtypeStruct((t, d), BF16)
    return pl.pallas_call(
        body, name=name, grid=(t // tm,),
        in_specs=[tile, tile, pl.BlockSpec((tm, d), lambda i: (i, 1)), pl.BlockSpec((tm, d), lambda i: (i, 2)),
                  pl.BlockSpec((1, d), lambda i: (0, 0)), pl.BlockSpec((1, d), lambda i: (0, 1)),
                  weight, weight, weight, tile, tile, pl.BlockSpec((1, d), lambda i: (0, 0))],
        out_specs=[tile, tile, tile, tile, tile, pl.BlockSpec((8, d), lambda i: (0, 0))],
        out_shape=[bf, bf, bf, bf, jax.ShapeDtypeStruct((t, d), F32), jax.ShapeDtypeStruct((8, d), F32)],
        compiler_params=_params("arbitrary"))(z_rnn, z_attn, ag_ml, ag_ml, b_gate, b_gate, w_rnn, w_attn, w_out, x, target, g_post)


def _branches_bwd(dout, br_rnn, br_attn, ag_ml, b_gate, w_rnn, w_attn, w_out, name, tm=512):
    t, d = br_rnn.shape
    tm = min(tm, t)

    def body(do_ref, r_ref, a_ref, lr_ref, la_ref, br_ref, ba_ref, wr_ref, wa_ref, wo_ref,
             dr_ref, da_ref, dl_ref, dzr_ref, dza_ref, st_ref, wt_ref):
        @pl.when(pl.program_id(0) == 0)
        def _():
            st_ref[...] = jnp.zeros_like(st_ref)
            wt_ref[0] = wo_ref[...].T
            wt_ref[1] = wr_ref[...].T
            wt_ref[2] = wa_ref[...].T

        dm = jnp.dot(do_ref[...], wt_ref[0], preferred_element_type=F32)
        g_rnn = _sigmoid(lr_ref[...].astype(F32) + br_ref[...])
        g_attn = _sigmoid(la_ref[...].astype(F32) + ba_ref[...])
        dbr_rnn = (dm * g_rnn).astype(BF16)
        dbr_attn = (dm * g_attn).astype(BF16)
        dr_ref[...] = dbr_rnn
        da_ref[...] = dbr_attn
        dl_rnn = dm * r_ref[...].astype(F32) * g_rnn * (1.0 - g_rnn)
        dl_attn = dm * a_ref[...].astype(F32) * g_attn * (1.0 - g_attn)
        dl_ref[:, 0:d] = dl_rnn.astype(BF16)
        dl_ref[:, d:2 * d] = dl_attn.astype(BF16)
        st_ref[...] += _rows8([jnp.sum(dl_rnn, axis=0, keepdims=True), jnp.sum(dl_attn, axis=0, keepdims=True)], d)
        dzr_ref[...] = jnp.dot(dbr_rnn, wt_ref[1], preferred_element_type=F32).astype(BF16)
        dza_ref[...] = jnp.dot(dbr_attn, wt_ref[2], preferred_element_type=F32).astype(BF16)

    tile = pl.BlockSpec((tm, d), lambda i: (i, 0))
    weight = pl.BlockSpec((d, d), lambda i: (0, 0))
    bf = jax.ShapeDtypeStruct((t, d), BF16)
    return pl.pallas_call(
        body, name=name, grid=(t // tm,),
        in_specs=[tile, tile, tile, pl.BlockSpec((tm, d), lambda i: (i, 1)), pl.BlockSpec((tm, d), lambda i: (i, 2)),
                  pl.BlockSpec((1, d), lambda i: (0, 0)), pl.BlockSpec((1, d), lambda i: (0, 1)), weight, weight, weight],
        out_specs=[tile, tile, pl.BlockSpec((tm, 2 * d), lambda i: (i, 0)), tile, tile, pl.BlockSpec((8, d), lambda i: (0, 0))],
        out_shape=[bf, bf, jax.ShapeDtypeStruct((t, 2 * d), BF16), bf, bf, jax.ShapeDtypeStruct((8, d), F32)],
        scratch_shapes=[pltpu.VMEM((3, d, d), BF16)],
        compiler_params=_params("arbitrary"))(dout, br_rnn, br_attn, ag_ml, ag_ml, b_gate, b_gate, w_rnn, w_attn, w_out)


def _lru_gates(c, wa, ba, wx, bx, sp):
    cb = c.astype(BF16)
    r = _sigmoid(jnp.dot(cb, wa, preferred_element_type=F32) + ba)
    ig = _sigmoid(jnp.dot(cb, wx, preferred_element_type=F32) + bx)
    log_a = (-LRU_C) * r * sp
    a = jnp.exp(log_a)
    mult = jnp.sqrt(-jnp.tanh(log_a) * (a * a + 1.0))
    return cb, r, ig, a, mult


SUBLANES = 8


def _scan_fwd(a, u, carry, tt):
    w = a.shape[1]
    ng = tt // SUBLANES
    a3 = a.reshape(ng, SUBLANES, w)
    u3 = u.reshape(ng, SUBLANES, w)
    sub = lax.broadcasted_iota(jnp.int32, (ng, SUBLANES, w), 1)
    d = 1
    while d < SUBLANES:
        keep = sub >= d
        u3 = u3 + a3 * jnp.where(keep, pltpu.roll(u3, d, 1), 0.0)
        a3 = a3 * jnp.where(keep, pltpu.roll(a3, d, 1), 1.0)
        d *= 2
    out = []
    for g in range(ng):
        hg = u3[g] + a3[g] * carry
        out.append(hg)
        carry = hg[SUBLANES - 1:SUBLANES, :]
    return jnp.concatenate(out, axis=0)


def _scan_rev(b, g, carry, tt):
    w = b.shape[1]
    ng = tt // SUBLANES
    b3 = b.reshape(ng, SUBLANES, w)
    g3 = g.reshape(ng, SUBLANES, w)
    sub = lax.broadcasted_iota(jnp.int32, (ng, SUBLANES, w), 1)
    d = 1
    while d < SUBLANES:
        keep = sub < SUBLANES - d
        g3 = g3 + b3 * jnp.where(keep, pltpu.roll(g3, SUBLANES - d, 1), 0.0)
        b3 = b3 * jnp.where(keep, pltpu.roll(b3, SUBLANES - d, 1), 1.0)
        d *= 2
    out = [None] * ng
    for k in range(ng - 1, -1, -1):
        hk = g3[k] + b3[k] * carry
        out[k] = hk
        carry = hk[0:1, :]
    return jnp.concatenate(out, axis=0)


def _conv_taps(cw, bias, x, ext_ref, tt):
    x2 = ext_ref[7:7 + tt, :]
    x1 = ext_ref[6:6 + tt, :]
    x0 = ext_ref[5:5 + tt, :]
    c = bias + cw[3:4] * x + cw[2:3] * x2 + cw[1:2] * x1 + cw[0:1] * x0
    return c, x2, x1, x0


def _rnn_fwd(rx_rg, cw, cb, wa, ba, wx, bx, lam, name, tt=512):
    t = rx_rg.shape[0]
    tt = min(tt, t)
    w = GROUP_W

    def body(rx_ref, rg_ref, cw_ref, cb_ref, wa_ref, ba_ref, wx_ref, bx_ref, lam_ref, y_ref, z_ref, ext_ref, hc_ref):
        @pl.when(pl.program_id(1) == 0)
        def _():
            ext_ref[0:8, :] = jnp.zeros((8, w), F32)
            hc_ref[...] = jnp.zeros((8, w), F32)

        x = rx_ref[...]
        ext_ref[8:8 + tt, :] = x
        c, _, _, _ = _conv_taps(cw_ref[...], cb_ref[...], x, ext_ref, tt)
        ext_ref[0:8, :] = x[tt - 8:tt, :]
        sp = _softplus(-lam_ref[...])
        _, _, ig, a, mult = _lru_gates(c, wa_ref[...], ba_ref[...], wx_ref[...], bx_ref[...], sp)
        h = _scan_fwd(a, mult * (ig * c), hc_ref[7:8, :], tt)
        hc_ref[...] = h[tt - 8:tt, :]
        y_ref[...] = h
        rg = rg_ref[...]
        z_ref[...] = (h * rg * _sigmoid(rg)).astype(BF16)

    vec = pl.BlockSpec((1, w), lambda g, i: (0, g))
    mat = pl.BlockSpec((None, w, w), lambda g, i: (g, 0, 0))
    tile = pl.BlockSpec((tt, w), lambda g, i: (i, g))
    return pl.pallas_call(
        body, name=name, grid=(N_GROUPS, t // tt),
        in_specs=[tile, pl.BlockSpec((tt, w), lambda g, i: (i, N_GROUPS + g)),
                  pl.BlockSpec((4, w), lambda g, i: (0, g)), vec, mat, vec, mat, vec, vec],
        out_specs=[tile, tile],
        out_shape=[jax.ShapeDtypeStruct((t, D_RNN), F32), jax.ShapeDtypeStruct((t, D_RNN), BF16)],
        scratch_shapes=[pltpu.VMEM((tt + 8, w), F32), pltpu.VMEM((8, w), F32)],
        compiler_params=_params("parallel", "arbitrary"))(rx_rg, rx_rg, cw, cb, wa, ba, wx, bx, lam)


def _rnn_bwd(rx_rg, y, dz, cw, cb, wa, ba, wx, bx, lam, name, tt=512, ride=None):
    t = rx_rg.shape[0]
    tt = min(tt, t)
    nt = t // tt
    w = GROUP_W

    host = _Hosted(ride, 13, 5, 6)

    def body(*refs):
        host_refs, start, finish = host.split(refs)
        (rx_ref, rg_ref, rxt_ref, y_ref, yt_ref, dz_ref, cw_ref, cb_ref, wa_ref, ba_ref, wx_ref, bx_ref, lam_ref,
         drx_ref, drg_ref, st_ref, gda_ref, gdx_ref, ext_ref, dcx_ref, wcar_ref, acar_ref, dwa_ref, dwx_ref) = host_refs
        ii = pl.program_id(1)
        start((pl.program_id(0) == 0) & (ii == 0))

        @pl.when(ii == 0)
        def _():
            wcar_ref[...] = jnp.zeros((8, w), F32)
            acar_ref[...] = jnp.zeros((8, w), F32)
            dcx_ref[tt:tt + 8, :] = jnp.zeros((8, w), F32)
            st_ref[...] = jnp.zeros_like(st_ref)
            dwa_ref[...] = jnp.zeros_like(dwa_ref)
            dwx_ref[...] = jnp.zeros_like(dwx_ref)

        has_prev = jnp.where(ii == nt - 1, 0.0, 1.0)
        x = rx_ref[...]
        ext_ref[0:8, :] = rxt_ref[...] * has_prev
        ext_ref[8:8 + tt, :] = x
        cwv = cw_ref[...]
        c, x2, x1, x0 = _conv_taps(cwv, cb_ref[...], x, ext_ref, tt)
        lam = lam_ref[...]
        sp = _softplus(-lam)
        wa = wa_ref[...]
        wx = wx_ref[...]
        cb16, r, ig, a, mult = _lru_gates(c, wa, ba_ref[...], wx, bx_ref[...], sp)

        rg = rg_ref[...]
        sg = _sigmoid(rg)
        dz = dz_ref[...].astype(F32)
        yv = y_ref[...]
        drg_ref[...] = (dz * yv * (sg * (1.0 + rg * (1.0 - sg)))).astype(BF16)

        row = lax.broadcasted_iota(jnp.int32, (tt, w), 0)
        b = jnp.where(row < tt - 1, pltpu.roll(a, tt - 1, 0), acar_ref[0:1, :])
        dh = _scan_rev(b, dz * (rg * sg), wcar_ref[0:1, :], tt)
        wcar_ref[...] = dh[0:8, :]
        acar_ref[...] = a[0:8, :]

        hprev = jnp.where(row >= 1, pltpu.roll(yv, 1, 0), yt_ref[7:8, :] * has_prev)
        dmult = dh * (ig * c)
        dig = dh * mult * c
        dlog_a = dh * hprev * a - dmult * (a * a / mult)
        dpa = dlog_a * ((-LRU_C) * sp) * r * (1.0 - r)
        dpx = dig * ig * (1.0 - ig)
        dsp = jnp.sum(dlog_a * r, axis=0, keepdims=True) * (-LRU_C)
        dlam = dsp * (-_sigmoid(-lam))
        dpa16 = dpa.astype(BF16)
        dpx16 = dpx.astype(BF16)
        dwa_ref[...] += lax.dot_general(cb16, dpa16, TN_DIMS, preferred_element_type=F32)
        dwx_ref[...] += lax.dot_general(cb16, dpx16, TN_DIMS, preferred_element_type=F32)
        dc = (dh * mult * ig
              + lax.dot_general(dpa16, wa, NT_DIMS, preferred_element_type=F32)
              + lax.dot_general(dpx16, wx, NT_DIMS, preferred_element_type=F32))

        dcx_ref[0:tt, :] = dc
        drx = (cwv[3:4] * dc + cwv[2:3] * dcx_ref[1:1 + tt, :] + cwv[1:2] * dcx_ref[2:2 + tt, :]
               + cwv[0:1] * dcx_ref[3:3 + tt, :])
        drx_ref[...] = drx.astype(BF16)
        dcx_ref[tt:tt + 8, :] = dc[0:8, :]

        def colsum(v):
            return jnp.sum(v, axis=0, keepdims=True)

        st_ref[...] += _rows8([colsum(dc), colsum(dpa), colsum(dpx), dlam,
                               colsum(dc * x0), colsum(dc * x1), colsum(dc * x2), colsum(dc * x)], w)

        @pl.when(ii == nt - 1)
        def _():
            for blk in range(GROUP_W // RNN_BLOCK_W):
                rows = slice(blk * RNN_BLOCK_W, (blk + 1) * RNN_BLOCK_W)
                gda_ref[blk] = dwa_ref[rows, rows]
                gdx_ref[blk] = dwx_ref[rows, rows]

        finish((pl.program_id(0) == N_GROUPS - 1) & (ii == nt - 1))

    def rev(ii):
        return nt - 1 - ii

    def tail(g, ii):
        return (jnp.maximum(rev(ii) * (tt // 8) - 1, 0), g)

    vec = pl.BlockSpec((1, w), lambda g, ii: (0, g))
    mat = pl.BlockSpec((None, w, w), lambda g, ii: (g, 0, 0))
    tile = pl.BlockSpec((tt, w), lambda g, ii: (rev(ii), g))
    diag_shape = (N_GROUPS, GROUP_W // RNN_BLOCK_W, RNN_BLOCK_W, RNN_BLOCK_W)
    diag = pl.BlockSpec((None,) + diag_shape[1:], lambda g, ii: (g, 0, 0, 0))
    outs = pl.pallas_call(
        body, name=name, grid=(N_GROUPS, nt),
        in_specs=[tile, pl.BlockSpec((tt, w), lambda g, ii: (rev(ii), N_GROUPS + g)), pl.BlockSpec((8, w), tail),
                  tile, pl.BlockSpec((8, w), tail), tile,
                  pl.BlockSpec((4, w), lambda g, ii: (0, g)), vec, mat, vec, mat, vec, vec] + host.in_specs,
        out_specs=[tile, tile, pl.BlockSpec((8, w), lambda g, ii: (0, g)), diag, diag] + host.out_specs,
        out_shape=[jax.ShapeDtypeStruct((t, D_RNN), BF16), jax.ShapeDtypeStruct((t, D_RNN), BF16),
                   jax.ShapeDtypeStruct((8, D_RNN), F32),
                   jax.ShapeDtypeStruct(diag_shape, F32), jax.ShapeDtypeStruct(diag_shape, F32)] + host.out_shapes,
        scratch_shapes=[pltpu.VMEM((tt + 8, w), F32), pltpu.VMEM((tt + 8, w), F32), pltpu.VMEM((8, w), F32),
                        pltpu.VMEM((8, w), F32), pltpu.VMEM((w, w), F32), pltpu.VMEM((w, w), F32)] + host.scratch_shapes,
        compiler_params=_params("arbitrary" if ride else "parallel", "arbitrary"))(
            rx_rg, rx_rg, rx_rg, y, y, dz, cw, cb, wa, ba, wx, bx, lam, *host.arrays)
    res, landed = host.results(outs, 5)
    return (*res, landed) if ride else tuple(res)


def _half_mask(shape, half):
    lane = lax.broadcasted_iota(jnp.int32, shape, 1)
    return (lane >= HEAD_DIM) if half else (lane < HEAD_DIM)


def _dup_half(t, half):
    sel = jnp.where(_half_mask(t.shape, half), t, 0.0)
    return sel + pltpu.roll(sel, HEAD_DIM, 1)


def _band_geometry(n):
    qi = lax.broadcasted_iota(jnp.int32, (BLOCK, 2 * BLOCK), 0)
    kj = lax.broadcasted_iota(jnp.int32, (BLOCK, 2 * BLOCK), 1)
    dist = BLOCK + qi - kj
    first_key = jnp.where(n > 0, 0, BLOCK)
    valid = (dist >= 0) & (dist < BLOCK) & (kj >= first_key)
    return dist.astype(F32), valid


GROUP = 4


def _kv_dup(prev_ref, cur_ref, hk, scale=1.0):
    tile = hk // 2
    kt = jnp.concatenate([prev_ref[:, tile * LANE:(tile + 1) * LANE], cur_ref[:, tile * LANE:(tile + 1) * LANE]], axis=0)
    return (_dup_half(kt.astype(F32), hk % 2) * scale).astype(BF16)


def _fill_bias(bias_ref, sm_ref, n):
    distf, valid = _band_geometry(n)
    for head in range(N_Q_HEADS):
        bias_ref[head] = jnp.where(valid, -sm_ref[1, head] * distf, MASKED)


def _head_scores(q2s, half, kdup, bias):
    qm = jnp.where(_half_mask(q2s.shape, half), q2s, jnp.zeros_like(q2s))
    return qm, lax.dot_general(qm, kdup, NT_DIMS, preferred_element_type=F32) + bias


def _attn_specs(nb, clamp_last):
    def blk(n):
        return jnp.minimum(n, nb - 1) if clamp_last else n

    q_spec = pl.BlockSpec((BLOCK, D_MODEL), lambda n: (blk(n), 0))
    k_prev = pl.BlockSpec((BLOCK, D_KV), lambda n: (jnp.maximum(blk(n) - 1, 0), D_MODEL // D_KV))
    k_cur = pl.BlockSpec((BLOCK, D_KV), lambda n: (blk(n), D_MODEL // D_KV))
    v_prev = pl.BlockSpec((BLOCK, D_KV), lambda n: (jnp.maximum(blk(n) - 1, 0), D_MODEL // D_KV + 1))
    v_cur = pl.BlockSpec((BLOCK, D_KV), lambda n: (blk(n), D_MODEL // D_KV + 1))
    return q_spec, k_prev, k_cur, v_prev, v_cur


def _attn_fwd(sm, qkv, ag_ml, name, ride=None):
    t = qkv.shape[0]
    nb = t // BLOCK

    host = _Hosted(ride, 7, 3, 1)

    def body(*refs):
        (sm_ref, q_ref, kp_ref, kc_ref, vp_ref, vc_ref, ag_ref, y_ref, z_ref, lse_ref, bias_ref), start, finish = host.split(refs)
        n = pl.program_id(0)
        start(n == 0)
        start.middle(n == (3 * nb) // 4)

        @pl.when(n <= 1)
        def _():
            _fill_bias(bias_ref, sm_ref, n)

        lane = lax.broadcasted_iota(jnp.int32, (BLOCK, LANE), 1)
        low = lane < HEAD_DIM
        lse = jnp.zeros((BLOCK, LANE), F32)
        for hk in range(N_Q_HEADS // GROUP):
            kdup = _kv_dup(kp_ref, kc_ref, hk)
            vdup = _kv_dup(vp_ref, vc_ref, hk)
            for k in (0, 1):
                c = slice((2 * hk + k) * LANE, (2 * hk + k + 1) * LANE)
                q2s = q_ref[:, c] * ATTN_SCALE
                outs = []
                for half in (0, 1):
                    head = GROUP * hk + 2 * k + half
                    _, s = _head_scores(q2s, half, kdup, bias_ref[head])
                    sink = sm_ref[0, head]
                    m = jnp.maximum(jnp.max(s, axis=1, keepdims=True), sink)
                    e = jnp.exp(s - m)
                    l = jnp.sum(e, axis=1, keepdims=True) + jnp.exp(sink - m)
                    outs.append(jnp.dot((e * (1.0 / l)).astype(BF16), vdup, preferred_element_type=F32))
                    lse = jnp.where(lane == head, m + jnp.log(l), lse)
                yt = jnp.where(low, outs[0], outs[1])
                y_ref[:, c] = yt
                ag = ag_ref[:, c].astype(F32)
                z_ref[:, c] = (yt * ag * _sigmoid(ag)).astype(BF16)
        lse_ref[...] = lse
        finish(n == nb - 1)

    q_spec, k_prev, k_cur, v_prev, v_cur = _attn_specs(nb, False)
    wide = pl.BlockSpec((BLOCK, D_MODEL), lambda n: (n, 0))
    outs = pl.pallas_call(
        body, name=name, grid=(nb,),
        in_specs=[pl.BlockSpec(memory_space=pltpu.SMEM), q_spec, k_prev, k_cur, v_prev, v_cur, wide] + host.in_specs,
        out_specs=[wide, wide, pl.BlockSpec((BLOCK, LANE), lambda n: (n, 0))] + host.out_specs,
        out_shape=[jax.ShapeDtypeStruct((t, D_MODEL), F32), jax.ShapeDtypeStruct((t, D_MODEL), BF16),
                   jax.ShapeDtypeStruct((t, LANE), F32)] + host.out_shapes,
        scratch_shapes=[pltpu.VMEM((N_Q_HEADS, BLOCK, 2 * BLOCK), F32)] + host.scratch_shapes,
        compiler_params=_params("arbitrary"))(sm, qkv, qkv, qkv, qkv, qkv, ag_ml, *host.arrays)
    res, landed = host.results(outs, 3)
    return (*res, landed) if ride else tuple(res)


def _attn_bwd(sm, qkv, ag_ml, y, lse, dz, name, ride=None):
    t = qkv.shape[0]
    nb = t // BLOCK
    host = _Hosted(ride, 10, 4, 3)

    def body(*refs):
        host_refs, start, finish = host.split(refs)
        (sm_ref, q_ref, kp_ref, kc_ref, vp_ref, vc_ref, ag_ref, y_ref, lse_ref, dz_ref,
         dq_ref, dkv_ref, dag_ref, ds_ref, ck_ref, cv_ref, bias_ref) = host_refs
        n = pl.program_id(0)
        start(n == 0)
        start.middle(n == (3 * nb) // 4)

        @pl.when(n == 0)
        def _():
            ck_ref[...] = jnp.zeros_like(ck_ref)
            cv_ref[...] = jnp.zeros_like(cv_ref)
            ds_ref[...] = jnp.zeros_like(ds_ref)

        @pl.when(n <= 1)
        def _():
            _fill_bias(bias_ref, sm_ref, n)

        @pl.when(n < nb)
        def _():
            lane8 = lax.broadcasted_iota(jnp.int32, (8, LANE), 1)
            row8 = lax.broadcasted_iota(jnp.int32, (8, LANE), 0)
            dsink = jnp.zeros((8, LANE), F32)
            dk_heads, dv_heads = [], []
            lse_tile = lse_ref[...]
            for hk in range(N_Q_HEADS // GROUP):
                kdup = _kv_dup(kp_ref, kc_ref, hk)
                ks = _kv_dup(kp_ref, kc_ref, hk, ATTN_SCALE)
                vdup = _kv_dup(vp_ref, vc_ref, hk)
                qms, dyhs, y_rows = [], [], []
                for k in (0, 1):
                    c = slice((2 * hk + k) * LANE, (2 * hk + k + 1) * LANE)
                    ag = ag_ref[:, c].astype(F32)
                    sg = _sigmoid(ag)
                    dzt = dz_ref[:, c].astype(F32)
                    yt = y_ref[:, c]
                    dag_ref[:, c] = (dzt * yt * (sg * (1.0 + ag * (1.0 - sg)))).astype(BF16)
                    dyt = dzt * (ag * sg)
                    q2s = q_ref[:, c] * ATTN_SCALE
                    for half in (0, 1):
                        hm = _half_mask(q2s.shape, half)
                        qms.append(jnp.where(hm, q2s, jnp.zeros_like(q2s)))
                        dyhs.append(jnp.where(hm, dyt, 0.0))
                        y_rows.append(yt)
                qm4 = jnp.concatenate(qms, axis=0)
                dy4 = jnp.concatenate(dyhs, axis=0)
                dy4_16 = dy4.astype(BF16)
                s4 = lax.dot_general(qm4, kdup, NT_DIMS, preferred_element_type=F32)
                dp4 = lax.dot_general(dy4_16, vdup, NT_DIMS, preferred_element_type=F32)
                probs16, ds16 = [], []
                for r in range(GROUP):
                    head = GROUP * hk + r
                    rows = slice(r * BLOCK, (r + 1) * BLOCK)
                    lh = lse_tile[:, head:head + 1]
                    probs = jnp.exp(s4[rows] + bias_ref[head] - lh)
                    psink = jnp.exp(sm_ref[0, head] - lh)
                    delta = jnp.sum(dyhs[r] * y_rows[r], axis=1, keepdims=True)
                    ds16.append((probs * (dp4[rows] - delta)).astype(BF16))
                    probs16.append(probs.astype(BF16))
                    dsink = dsink + jnp.where((row8 == 0) & (lane8 == head),
                                              -jnp.sum(psink * delta, axis=0, keepdims=True), 0.0)
                ds4 = jnp.concatenate(ds16, axis=0)
                p4 = jnp.concatenate(probs16, axis=0)
                dq4 = jnp.dot(ds4, ks, preferred_element_type=F32)
                low = _half_mask((BLOCK, LANE), 0)
                for k in (0, 1):
                    c = slice((2 * hk + k) * LANE, (2 * hk + k + 1) * LANE)
                    dq_ref[:, c] = jnp.where(low, dq4[2 * k * BLOCK:(2 * k + 1) * BLOCK],
                                             dq4[(2 * k + 1) * BLOCK:(2 * k + 2) * BLOCK]).astype(BF16)
                dk_acc = lax.dot_general(ds4, qm4, TN_DIMS, preferred_element_type=F32)
                dv_acc = lax.dot_general(p4, dy4_16, TN_DIMS, preferred_element_type=F32)
                dk_heads.append(dk_acc + pltpu.roll(dk_acc, HEAD_DIM, 1))
                dv_heads.append(dv_acc + pltpu.roll(dv_acc, HEAD_DIM, 1))
            ds_ref[...] += dsink
            low = _half_mask((2 * BLOCK, LANE), 0)
            for tile in range(2):
                cols = slice(tile * LANE, (tile + 1) * LANE)
                dkt = jnp.where(low, dk_heads[2 * tile], dk_heads[2 * tile + 1])
                dvt = jnp.where(low, dv_heads[2 * tile], dv_heads[2 * tile + 1])
                dkv_ref[:, cols] = (ck_ref[:, cols] + dkt[0:BLOCK, :]).astype(BF16)
                dkv_ref[:, D_KV + tile * LANE:D_KV + (tile + 1) * LANE] = (cv_ref[:, cols] + dvt[0:BLOCK, :]).astype(BF16)
                ck_ref[:, cols] = dkt[BLOCK:2 * BLOCK, :]
                cv_ref[:, cols] = dvt[BLOCK:2 * BLOCK, :]

        @pl.when(n == nb)
        def _():
            dkv_ref[:, 0:D_KV] = ck_ref[...].astype(BF16)
            dkv_ref[:, D_KV:2 * D_KV] = cv_ref[...].astype(BF16)

        finish(n == nb)

    q_spec, k_prev, k_cur, v_prev, v_cur = _attn_specs(nb, True)
    wide = pl.BlockSpec((BLOCK, D_MODEL), lambda n: (jnp.minimum(n, nb - 1), 0))
    outs = pl.pallas_call(
        body, name=name, grid=(nb + 1,),
        in_specs=[pl.BlockSpec(memory_space=pltpu.SMEM), q_spec, k_prev, k_cur, v_prev, v_cur, wide, wide,
                  pl.BlockSpec((BLOCK, LANE), lambda n: (jnp.minimum(n, nb - 1), 0)), wide] + host.in_specs,
        out_specs=[wide, pl.BlockSpec((BLOCK, 2 * D_KV), lambda n: (jnp.maximum(n - 1, 0), 0)), wide,
                   pl.BlockSpec((8, LANE), lambda n: (0, 0))] + host.out_specs,
        out_shape=[jax.ShapeDtypeStruct((t, D_MODEL), BF16), jax.ShapeDtypeStruct((t, 2 * D_KV), BF16),
                   jax.ShapeDtypeStruct((t, D_MODEL), BF16), jax.ShapeDtypeStruct((8, LANE), F32)] + host.out_shapes,
        scratch_shapes=[pltpu.VMEM((BLOCK, D_KV), F32), pltpu.VMEM((BLOCK, D_KV), F32),
                        pltpu.VMEM((N_Q_HEADS, BLOCK, 2 * BLOCK), F32)] + host.scratch_shapes,
        compiler_params=_params("arbitrary"))(sm, qkv, qkv, qkv, qkv, qkv, ag_ml, y, lse, dz, *host.arrays)
    res, landed = host.results(outs, 4)
    return (*res, landed) if ride else tuple(res)


def _local_grads(x, target, p, project, late_weights=None, reduce_out=None, reduce_in=None):
    h, rx_rg, qkv, ag_ml, wt = project(x, p["pre_g"])
    if late_weights is None:
        y_attn, z_attn, lse = _attn_fwd(p["sm"], qkv, ag_ml, "attn_fwd")
    else:
        y_attn, z_attn, lse, landed = _attn_fwd(p["sm"], qkv, ag_ml, "attn_fwd", ride=late_weights[0])
        p = {**p, **late_weights[1](landed)}
    lru = (p["cw"], p["cb"], p["wbd_a"], p["b_a"], p["wbd_x"], p["b_x"], p["lam"])
    y_rnn, z_rnn = _rnn_fwd(rx_rg, *lru, "rnn_fwd")
    br_rnn, br_attn, merged, dout, dy, st_post = _branches_fwd(
        z_rnn, z_attn, ag_ml, p["b_gate"], p["w_rnn"], p["w_attn"], p["w_out"], x, target, p["post_g"], "branches_fwd")

    dbr_rnn, dbr_attn, d_ml, dz_rnn, dz_attn, st_merge = _branches_bwd(
        dout, br_rnn, br_attn, ag_ml, p["b_gate"], p["w_rnn"], p["w_attn"], p["w_out"], "branches_bwd")
    gw_out = _mm_tn(merged, dout, "gw_out")
    gw_rnn = _mm_tn(z_rnn, dbr_rnn, "gw_rnn")
    gw_attn = _mm_tn(z_attn, dbr_attn, "gw_attn")
    red_out = red_in = None
    if reduce_out is None:
        d_rx, d_rg, st_rnn, g_rg_a, g_rg_x = _rnn_bwd(rx_rg, y_rnn, dz_rnn, *lru, "rnn_bwd")
        dq, dkv, d_ag, st_sink = _attn_bwd(p["sm"], qkv, ag_ml, y_attn, lse, dz_attn, "attn_bwd")
    else:
        d_rx, d_rg, st_rnn, g_rg_a, g_rg_x, from_sibling = _rnn_bwd(
            rx_rg, y_rnn, dz_rnn, *lru, "rnn_bwd", ride=reduce_out[0](gw_rnn, gw_attn, gw_out))
        dq, dkv, d_ag, st_sink, red_out = _attn_bwd(p["sm"], qkv, ag_ml, y_attn, lse, dz_attn, "attn_bwd",
                                                    ride=reduce_out[1](from_sibling, g_rg_a, g_rg_x))

    segs = [d_rx, d_rg, dq, dkv, d_ag, d_ml]
    gwt = _mm_tn_seg(segs, h, "gw_in")
    if reduce_in is None:
        grad_x, st_pre = _input_grad(segs, wt, x, p["pre_g"], dy, "input_grad")
    else:
        grad_x, st_pre, red_in = _input_grad(segs, wt, x, p["pre_g"], dy, "input_grad", ride=reduce_in(gwt))
    return dict(grad_x=grad_x, gwt=gwt, gw_rnn=gw_rnn, gw_attn=gw_attn, gw_out=gw_out,
                st_post=st_post, st_merge=st_merge, st_rnn=st_rnn, st_sink=st_sink, st_pre=st_pre,
                g_rg_a=g_rg_a, g_rg_x=g_rg_x, red_out=red_out, red_in=red_in)


def _place():
    x, y, c = lax.axis_index("x"), lax.axis_index("y"), lax.axis_index("c")
    return x, y, c


def _gather_ride(shards):
    n = len(shards)

    def copies(ins, outs, sems):
        send_sems, recv_sems, local_sems = sems
        x, y, c = _place()
        me, sibling = (x, y, c), (x, y, 1 - c)
        chips = [(1 - x, y), (x, 1 - y), (1 - x, 1 - y)]

        def slot(a, dev):
            return outs[a].at[4 * dev[0] + 2 * dev[1] + dev[2]]

        def copy(a, k, block, to, src=None):
            return pltpu.make_async_remote_copy(
                src_ref=slot(a, block) if src is None else src, dst_ref=slot(a, block),
                send_sem=send_sems.at[a, k], recv_sem=recv_sems.at[a, k], device_id=to, device_id_type=MESH)

        mine = [pltpu.make_async_copy(ins[a], slot(a, me), local_sems.at[a]) for a in range(n)]
        first = []
        for a in range(n):
            first.append(copy(a, 0, me, sibling, src=ins[a]))
            first += [copy(a, 1 + j, me, (*chip, c), src=ins[a]) for j, chip in enumerate(chips)]
        return me, sibling, chips, c, copy, mine, first

    def start(ins, outs, sems):
        *_, mine, first = copies(ins, outs, sems)
        for cp in mine + first:
            cp.start()

    def middle(ins, outs, sems):
        me, sibling, chips, c, copy, _, _ = copies(ins, outs, sems)
        for j, chip in enumerate(chips):
            for a in range(n):
                copy(a, 1 + j, (*chip, c), me).wait_recv()
                copy(a, 4 + j, (*chip, c), sibling).start()

    def finish(ins, outs, sems):
        me, sibling, chips, c, copy, mine, first = copies(ins, outs, sems)
        passed = [copy(a, 4 + j, (*chip, c), sibling) for j, chip in enumerate(chips) for a in range(n)]
        for a in range(n):
            copy(a, 0, sibling, me).wait_recv()
            for j, chip in enumerate(chips):
                copy(a, 4 + j, (*chip, 1 - c), me).wait_recv()
        for cp in first + passed:
            cp.wait_send()
        for cp in mine:
            cp.wait()

    return _Ride(
        shards, [jax.ShapeDtypeStruct((N_DEV, *s.shape), s.dtype) for s in shards],
        [pltpu.SemaphoreType.DMA((n, 7)), pltpu.SemaphoreType.DMA((n, 7)), pltpu.SemaphoreType.DMA((n,))],
        start, finish, middle)


def _sibling_ride(scatter, whole):
    ns, nw = len(scatter), len(whole)

    def copies(ins, outs, sems):
        send_sems, recv_sems = sems
        x, y, c = _place()
        cps = [pltpu.make_async_remote_copy(
            src_ref=ins[a].at[2 * chip + (1 - c)], dst_ref=outs[a].at[chip],
            send_sem=send_sems.at[a * N_CHIP + chip], recv_sem=recv_sems.at[a * N_CHIP + chip],
            device_id=(x, y, 1 - c), device_id_type=MESH) for a in range(ns) for chip in range(N_CHIP)]
        cps += [pltpu.make_async_remote_copy(
            src_ref=ins[ns + a], dst_ref=outs[ns + a],
            send_sem=send_sems.at[ns * N_CHIP + a], recv_sem=recv_sems.at[ns * N_CHIP + a],
            device_id=(x, y, 1 - c), device_id_type=MESH) for a in range(nw)]
        return cps

    def start(ins, outs, sems):
        for cp in copies(ins, outs, sems):
            cp.start()

    def finish(ins, outs, sems):
        for cp in copies(ins, outs, sems):
            cp.wait()

    n_sem = ns * N_CHIP + nw
    return _Ride(
        list(scatter) + list(whole),
        [jax.ShapeDtypeStruct((N_CHIP, *s.shape[1:]), s.dtype) for s in scatter]
        + [jax.ShapeDtypeStruct(s.shape, s.dtype) for s in whole],
        [pltpu.SemaphoreType.DMA((n_sem,)), pltpu.SemaphoreType.DMA((n_sem,))], start, finish)


def _chips_ride(scatter, whole):
    ns, nw = len(scatter), len(whole)
    n = ns + nw

    def copies(ins, outs, sems):
        send_sems, recv_sems, local_sems = sems
        x, y, c = _place()
        own = 2 * x + y
        chips = [(1 - x, y), (x, 1 - y), (1 - x, 1 - y)]

        def src(a, chip_idx):
            return ins[a].at[chip_idx] if a < ns else ins[a]

        local = [pltpu.make_async_copy(src(a, own), outs[a].at[own], local_sems.at[a]) for a in range(n)]
        sent = [pltpu.make_async_remote_copy(
            src_ref=src(a, 2 * chip[0] + chip[1]), dst_ref=outs[a].at[own],
            send_sem=send_sems.at[a, j], recv_sem=recv_sems.at[a, own], device_id=(*chip, c), device_id_type=MESH)
            for a in range(n) for j, chip in enumerate(chips)]
        return chips, c, local, sent

    def start(ins, outs, sems):
        _, _, local, sent = copies(ins, outs, sems)
        for cp in local + sent:
            cp.start()

    def finish(ins, outs, sems):
        send_sems, recv_sems, _ = sems
        chips, c, local, sent = copies(ins, outs, sems)
        for a in range(n):
            for chip in chips:
                k = 2 * chip[0] + chip[1]
                pltpu.make_async_remote_copy(
                    src_ref=outs[a].at[k], dst_ref=outs[a].at[k], send_sem=send_sems.at[a, 0],
                    recv_sem=recv_sems.at[a, k], device_id=(*chip, c), device_id_type=MESH).wait_recv()
        for cp in sent:
            cp.wait_send()
        for cp in local:
            cp.wait()

    return _Ride(
        list(scatter) + list(whole),
        [jax.ShapeDtypeStruct(s.shape, s.dtype) for s in scatter]
        + [jax.ShapeDtypeStruct((N_CHIP, *s.shape), s.dtype) for s in whole],
        [pltpu.SemaphoreType.DMA((n, 3)), pltpu.SemaphoreType.DMA((n, N_CHIP)), pltpu.SemaphoreType.DMA((n,))],
        start, finish)


def _pair_sum_scatter(parts, recvs, core, name):
    na = len(parts)
    _, r, cdim = parts[0].shape
    tr = min(r, 416 if r % 416 == 0 else 128)

    def body(core_ref, *refs):
        del core_ref
        for a in range(na):
            refs[2 * na + a][...] = (refs[a][...] + refs[na + a][...]).astype(BF16)

    blk = (None, tr, cdim)
    mine = pl.BlockSpec(blk, lambda k, i, core_ref: (2 * k + core_ref[0], i, 0))
    slot = pl.BlockSpec(blk, lambda k, i, core_ref: (k, i, 0))
    return pl.pallas_call(
        body, name=name,
        grid_spec=pltpu.PrefetchScalarGridSpec(
            num_scalar_prefetch=1, grid=(N_CHIP, r // tr),
            in_specs=[mine] * na + [slot] * na, out_specs=[slot] * na),
        out_shape=[jax.ShapeDtypeStruct((N_CHIP, r, cdim), BF16)] * na,
        compiler_params=_params("parallel", "parallel"))(core, *parts, *recvs)


def _allreduce_small(pack, name):
    shape = pack.shape

    def body(x_ref, o_ref, sib_ref, chip_ref, send_sems, recv_sems):
        x, y, c = _place()
        own = 2 * x + y
        chips = [(1 - x, y), (x, 1 - y), (1 - x, 1 - y)]
        to_sibling = pltpu.make_async_remote_copy(
            src_ref=x_ref, dst_ref=sib_ref, send_sem=send_sems.at[0], recv_sem=recv_sems.at[0],
            device_id=(x, y, 1 - c), device_id_type=MESH)
        to_sibling.start()
        to_sibling.wait()
        chip_ref[own] = x_ref[...] + sib_ref[...]
        sent = [pltpu.make_async_remote_copy(
            src_ref=chip_ref.at[own], dst_ref=chip_ref.at[own], send_sem=send_sems.at[1 + j],
            recv_sem=recv_sems.at[1 + own], device_id=(*chip, c), device_id_type=MESH) for j, chip in enumerate(chips)]
        for cp in sent:
            cp.start()
        for chip in chips:
            k = 2 * chip[0] + chip[1]
            pltpu.make_async_remote_copy(
                src_ref=chip_ref.at[k], dst_ref=chip_ref.at[k], send_sem=send_sems.at[1],
                recv_sem=recv_sems.at[1 + k], device_id=(*chip, c), device_id_type=MESH).wait_recv()
        for cp in sent:
            cp.wait_send()
        o_ref[...] = (chip_ref[0] + chip_ref[1]) + (chip_ref[2] + chip_ref[3])

    return pl.pallas_call(
        body, name=name, out_shape=jax.ShapeDtypeStruct(shape, F32),
        in_specs=[pl.BlockSpec(memory_space=pltpu.VMEM)], out_specs=pl.BlockSpec(memory_space=pltpu.VMEM),
        scratch_shapes=[pltpu.VMEM(shape, F32), pltpu.VMEM((N_CHIP, *shape), F32),
                        pltpu.SemaphoreType.DMA((4,)), pltpu.SemaphoreType.DMA((1 + N_CHIP,))],
    )(pack)


def _adamw(g, w, m, v):
    m = ADAM_B1 * m + (1.0 - ADAM_B1) * g
    v = ADAM_B2 * v + (1.0 - ADAM_B2) * (g * g)
    m_hat = m / (1.0 - ADAM_B1 ** ADAM_STEP)
    v_hat = v / (1.0 - ADAM_B2 ** ADAM_STEP)
    delta = -ADAM_LR * (m_hat / (jnp.sqrt(v_hat) + ADAM_EPS) + ADAM_WD * w)
    return delta, m, v


def _adam_parts(parts, w, m, v, name, tr=None):
    npart, r, c = parts.shape
    tr = r if tr is None else min(tr, r)

    def body(p_ref, w_ref, m_ref, v_ref, g_ref, d_ref, nm_ref, nv_ref):
        g = p_ref[0].astype(F32)
        for k in range(1, npart):
            g = g + p_ref[k].astype(F32)
        g_ref[...] = g
        d_ref[...], nm_ref[...], nv_ref[...] = _adamw(g, w_ref[...], m_ref[...], v_ref[...])

    tile = pl.BlockSpec((tr, c), lambda i: (i, 0))
    return pl.pallas_call(
        body, name=name, grid=(r // tr,),
        in_specs=[pl.BlockSpec((npart, tr, c), lambda i: (0, i, 0)), tile, tile, tile],
        out_specs=[tile] * 4, out_shape=[jax.ShapeDtypeStruct((r, c), F32)] * 4,
        compiler_params=_params("parallel"))(parts, w, m, v)


def _block_diag(w):
    w4 = w.reshape(N_GROUPS, 4, RNN_BLOCK_W, RNN_BLOCK_W)
    eye = jnp.eye(4, dtype=w.dtype)
    return jnp.einsum("gbij,bc->gbicj", w4, eye).reshape(N_GROUPS, GROUP_W, GROUP_W).astype(BF16)


SMALL_ROWS = 16
ROW_PRE_G, ROW_BGATE, ROW_CONV_B, ROW_B_A, ROW_B_X, ROW_LAM, ROW_POST_G, ROW_LOSS, ROW_SINKS, ROW_CONV_W = 0, 1, 3, 4, 5, 6, 7, 8, 9, 10


def _pack_stats(st_pre, st_merge, st_rnn, st_post, st_sink, name):
    d = D_MODEL

    def body(pre_ref, mg_ref, rnn_ref, post_ref, sink_ref, o_ref):
        rnn = rnn_ref[...]
        sinks = jnp.concatenate([sink_ref[0:1, :], jnp.zeros((1, d - LANE), F32)], axis=1)
        o_ref[0:8, :] = _rows8([pre_ref[0:1, :], mg_ref[0:1, :], mg_ref[1:2, :], rnn[0:1], rnn[1:2], rnn[2:3], rnn[3:4],
                                post_ref[0:1, :]], d)
        o_ref[8:16, :] = _rows8([post_ref[1:2, :], sinks, rnn[4:5], rnn[5:6], rnn[6:7], rnn[7:8]], d)

    return pl.pallas_call(body, name=name, out_shape=jax.ShapeDtypeStruct((SMALL_ROWS, d), F32))(
        st_pre, st_merge, st_rnn, st_post, st_sink)


def _adam_small(total, w, m, v, name):
    d = D_MODEL
    n_in = len(w)

    def pack(refs):
        pre, bg, cbias, ba, bx, lam, post, sinks = [r[...] for r in refs]
        top = _rows8([pre, bg[:, 0:d], bg[:, d:2 * d], cbias, ba, bx, lam, post], d)
        return jnp.concatenate([top, _rows8([jnp.zeros((1, d), F32), sinks], d)], axis=0)

    def body(*refs):
        p_ref = refs[0]
        w_refs, m_refs, v_refs = (refs[1 + k * n_in:1 + (k + 1) * n_in] for k in range(3))
        outs = refs[1 + 3 * n_in:]
        g = p_ref[...]
        res = (g,) + _adamw(g, pack(w_refs), pack(m_refs), pack(v_refs))
        for k in range(4):
            outs[k][...] = res[k]
            outs[4 + k][...] = jnp.concatenate([res[k][ROW_BGATE:ROW_BGATE + 1], res[k][ROW_BGATE + 1:ROW_BGATE + 2]], axis=1)

    return pl.pallas_call(
        body, name=name,
        out_shape=[jax.ShapeDtypeStruct((SMALL_ROWS, d), F32)] * 4 + [jax.ShapeDtypeStruct((1, 2 * d), F32)] * 4,
    )(total, *w, *m, *v)


def kernel(x, pre_norm_g, w_in, b_gate, conv_w, conv_b, w_rg_a, b_rg_a, w_rg_x, b_rg_x, lru_lambda, attn_sinks, w_rnn_out, w_attn_out, w_out, post_norm_g, loss_target, m_pre_norm_g, m_w_in, m_b_gate, m_conv_w, m_conv_b, m_w_rg_a, m_b_rg_a, m_w_rg_x, m_b_rg_x, m_lru_lambda, m_attn_sinks, m_w_rnn_out, m_w_attn_out, m_w_out, m_post_norm_g, v_pre_norm_g, v_w_in, v_b_gate, v_conv_w, v_conv_b, v_w_rg_a, v_b_rg_a, v_w_rg_x, v_b_rg_x, v_lru_lambda, v_attn_sinks, v_w_rnn_out, v_w_attn_out, v_w_out, v_post_norm_g):
    cx, cy, cc = _place()
    dev = 4 * cx + 2 * cy + cc
    core = jnp.reshape(cc, (1,)).astype(jnp.int32)

    w_in_t, m_in_t, v_in_t = (jnp.transpose(a[0]) for a in (w_in, m_w_in, v_w_in))
    wt_shard = w_in_t.astype(BF16)

    def project(xs, pre_g):
        h, rx_rg, qkv, ag_ml, wt_all = _gather_project(xs, pre_g, wt_shard, "gather_project")
        return h, rx_rg, qkv, ag_ml, wt_all.reshape(D_IN, D_MODEL)

    def late_unpack(landed):
        w_rnn_all, w_attn_all, w_out_all, cw_all = landed
        return dict(w_rnn=w_rnn_all.reshape(D_RNN, D_MODEL), w_attn=w_attn_all.reshape(D_MODEL, D_MODEL),
                    w_out=w_out_all.reshape(D_MODEL, D_MODEL), cw=jnp.transpose(cw_all, (1, 0, 2)).reshape(4, D_RNN))

    late_weights = (_gather_ride([w_rnn_out[0].astype(BF16), w_attn_out[0].astype(BF16), w_out[0].astype(BF16), conv_w[0]]),
                    late_unpack)

    heads = jnp.arange(1, N_Q_HEADS + 1, dtype=F32)
    slopes = jnp.exp2(-ALIBI_MAX_BIAS * heads / N_Q_HEADS)
    b_a = b_rg_a.reshape(1, D_RNN)
    b_x = b_rg_x.reshape(1, D_RNN)
    p = dict(
        pre_g=pre_norm_g, post_g=post_norm_g, b_gate=b_gate, cb=conv_b,
        wbd_a=_block_diag(w_rg_a[0]), b_a=b_a, wbd_x=_block_diag(w_rg_x[0]), b_x=b_x, lam=lru_lambda,
        sm=jnp.pad(attn_sinks, ((0, 1), (0, 0))) + jnp.pad(slopes[None, :], ((1, 0), (0, 0))))

    flat = (RNN_BLOCKS * RNN_BLOCK_W, RNN_BLOCK_W)

    out_scatter = []

    def out_sibling(gw_rnn, gw_attn, gw_out):
        out_scatter.extend(gw.reshape(N_DEV, SHARD_OUT, D_MODEL) for gw in (gw_rnn, gw_attn, gw_out))
        return _sibling_ride(out_scatter, [])

    def out_chips(from_sibling, g_rg_a, g_rg_x):
        return _join_rides(_chips_ride(_pair_sum_scatter(out_scatter, from_sibling, core, "pair_out"), []),
                           _gather_ride([g_rg_a.reshape(flat), g_rg_x.reshape(flat)]))

    def reduce_in(gwt):
        scatter = [gwt.reshape(N_DEV, SHARD_IN, D_MODEL)]
        from_sibling = _run_ride(_sibling_ride(scatter, []), "sibling_in")
        return _chips_ride(_pair_sum_scatter(scatter, from_sibling, core, "pair_in"), [])

    g = _local_grads(x[0], loss_target[0], p, project, late_weights, (out_sibling, out_chips), reduce_in)
    small = _allreduce_small(
        _pack_stats(g["st_pre"], g["st_merge"], g["st_rnn"], g["st_post"], g["st_sink"], "pack_stats"), "allreduce_small")

    out = {}
    red = g["red_out"]
    out["w_in"] = [jnp.transpose(o) for o in _adam_parts(g["red_in"][0], w_in_t, m_in_t, v_in_t, "adam_w_in", tr=SHARD_IN // 2)]
    out["w_rnn_out"] = _adam_parts(red[0], w_rnn_out[0], m_w_rnn_out[0], v_w_rnn_out[0], "adam_w_rnn_out")
    out["w_attn_out"] = _adam_parts(red[1], w_attn_out[0], m_w_attn_out[0], v_w_attn_out[0], "adam_w_attn_out")
    out["w_out"] = _adam_parts(red[2], w_out[0], m_w_out[0], v_w_out[0], "adam_w_out")
    out["w_rg_a"] = _adam_parts(red[3], w_rg_a.reshape(flat), m_w_rg_a.reshape(flat), v_w_rg_a.reshape(flat), "adam_w_rg_a", tr=256)
    out["w_rg_x"] = _adam_parts(red[4], w_rg_x.reshape(flat), m_w_rg_x.reshape(flat), v_w_rg_x.reshape(flat), "adam_w_rg_x", tr=256)

    def rows(pre, bg, cbias, ba, bx, lam, post, sinks):
        return (pre, bg, cbias, ba.reshape(1, D_RNN), bx.reshape(1, D_RNN), lam, post,
                jnp.pad(sinks, ((0, 0), (0, D_MODEL - N_Q_HEADS))))

    small_out = _adam_small(
        small,
        rows(pre_norm_g, b_gate, conv_b, b_rg_a, b_rg_x, lru_lambda, post_norm_g, attn_sinks),
        rows(m_pre_norm_g, m_b_gate, m_conv_b, m_b_rg_a, m_b_rg_x, m_lru_lambda, m_post_norm_g, m_attn_sinks),
        rows(v_pre_norm_g, v_b_gate, v_conv_b, v_b_rg_a, v_b_rg_x, v_lru_lambda, v_post_norm_g, v_attn_sinks),
        "adam_small")
    packed, bgate_out = small_out[:4], small_out[4:]
    g_cw = lax.dynamic_slice(packed[0][ROW_CONV_W:ROW_CONV_W + 4], (0, dev * SHARD_OUT), (4, SHARD_OUT))
    out["conv_w"] = _adam_parts(g_cw[None], conv_w[0], m_conv_w[0], v_conv_w[0], "adam_conv_w")

    def unpack(kind, name):
        if name == "b_gate":
            return bgate_out[kind]
        row = dict(pre_norm_g=ROW_PRE_G, conv_b=ROW_CONV_B, b_rg_a=ROW_B_A, b_rg_x=ROW_B_X, lru_lambda=ROW_LAM,
                   post_norm_g=ROW_POST_G, attn_sinks=ROW_SINKS)[name]
        r = packed[kind][row:row + 1]
        if name == "attn_sinks":
            return r[:, 0:N_Q_HEADS]
        if name in ("b_rg_a", "b_rg_x"):
            return r.reshape(1, RNN_BLOCKS, RNN_BLOCK_W)
        return r

    shapes = dict(w_in=(1, D_MODEL, SHARD_IN), w_rnn_out=(1, SHARD_OUT, D_MODEL), w_attn_out=(1, SHARD_OUT, D_MODEL),
                  w_out=(1, SHARD_OUT, D_MODEL), w_rg_a=(1, RNN_BLOCKS, RNN_BLOCK_W, RNN_BLOCK_W),
                  w_rg_x=(1, RNN_BLOCKS, RNN_BLOCK_W, RNN_BLOCK_W), conv_w=(1, 4, SHARD_OUT))
    weights = ["pre_norm_g", "w_in", "b_gate", "conv_w", "conv_b", "w_rg_a", "b_rg_a", "w_rg_x", "b_rg_x",
               "lru_lambda", "attn_sinks", "w_rnn_out", "w_attn_out", "w_out", "post_norm_g"]
    results = []
    for kind in range(4):
        for name in weights:
            if name in out:
                results.append(out[name][kind].reshape(shapes[name]))
            else:
                results.append(unpack(kind, name))
    loss = 0.5 / D_MODEL * jnp.sum(packed[0][ROW_LOSS])
    return (loss, g["grad_x"][None], *results)
```

```python
import functools

import jax
import jax.numpy as jnp
from jax import lax
from jax.experimental import pallas as pl
from jax.experimental.pallas import tpu as pltpu

F32 = jnp.float32
BF16 = jnp.bfloat16

D_MODEL = 1024
D_RNN = 1024
RNN_BLOCKS = 16
RNN_BLOCK_W = 64
LRU_C = 8.0
N_Q_HEADS = 16
HEAD_DIM = 64
D_KV = 256
BLOCK = 128
ALIBI_MAX_BIAS = 8.0
EPS = 1e-6
D_IN = 6656
N_DEV = 8
N_CHIP = 4
SHARD_IN = D_IN // N_DEV
SHARD_OUT = D_MODEL // N_DEV
ATTN_SCALE = HEAD_DIM ** -0.5
MASKED = -1e30

ADAM_LR = 0.001
ADAM_B1 = 0.9
ADAM_B2 = 0.999
ADAM_EPS = 1e-08
ADAM_WD = 0.01
ADAM_STEP = 10

VMEM_LIMIT_BYTES = 52 * 1024 * 1024
LANE = 128
GROUP_W = 256
N_GROUPS = D_RNN // GROUP_W
SEG_CHUNK = 512

NT_DIMS = (((1,), (1,)), ((), ()))
TN_DIMS = (((0,), (0,)), ((), ()))
MESH = pl.DeviceIdType.MESH
ANY = pl.BlockSpec(memory_space=pl.ANY)


def _params(*semantics):
    return pltpu.CompilerParams(dimension_semantics=semantics, vmem_limit_bytes=VMEM_LIMIT_BYTES)


def _sigmoid(x):
    return 0.5 * jnp.tanh(0.5 * x) + 0.5


def _log1p(e):
    u = 1.0 + e
    den = jnp.where(u == 1.0, 1.0, u - 1.0)
    return jnp.where(u == 1.0, e, jnp.log(u) * (e / den))


def _softplus(z):
    return jnp.maximum(z, 0.0) + _log1p(jnp.exp(-jnp.abs(z)))


def _rows8(rows, width):
    idx = lax.broadcasted_iota(jnp.int32, (8, width), 0)
    out = jnp.zeros((8, width), F32)
    for r, v in enumerate(rows):
        out = jnp.where(idx == r, v, out)
    return out


class _Ride:
    def __init__(self, arrays, out_shapes, scratch_shapes, start, finish, middle=None):
        self.arrays, self.out_shapes, self.scratch_shapes = list(arrays), list(out_shapes), list(scratch_shapes)
        self.start, self.finish, self.middle = start, finish, middle


class _Hosted:
    def __init__(self, ride, n_in, n_out, n_scratch=0):
        self.ride = ride
        self.sizes = (n_in, len(ride.arrays) if ride else 0, n_out, len(ride.out_shapes) if ride else 0, n_scratch)
        self.arrays = ride.arrays if ride else []
        self.in_specs = [ANY] * len(self.arrays)
        self.out_shapes = ride.out_shapes if ride else []
        self.out_specs = [ANY] * len(self.out_shapes)
        self.scratch_shapes = ride.scratch_shapes if ride else []

    def split(self, refs):
        n_in, r_in, n_out, r_out, n_scr = self.sizes
        cuts = [0, n_in, n_in + r_in, n_in + r_in + n_out, n_in + r_in + n_out + r_out, n_in + r_in + n_out + r_out + n_scr]
        host_in, ride_in, host_out, ride_out, host_scr = (refs[cuts[k]:cuts[k + 1]] for k in range(5))
        ride_scr = refs[cuts[5]:]

        def start(when):
            if self.ride is not None:
                pl.when(when)(lambda: self.ride.start(ride_in, ride_out, ride_scr))

        def finish(when):
            if self.ride is not None:
                pl.when(when)(lambda: self.ride.finish(ride_in, ride_out, ride_scr))

        def middle(when):
            if self.ride is not None and self.ride.middle is not None:
                pl.when(when)(lambda: self.ride.middle(ride_in, ride_out, ride_scr))

        start.middle = middle
        return tuple(host_in) + tuple(host_out) + tuple(host_scr), start, finish

    def results(self, outs, n_out):
        outs = list(outs) if isinstance(outs, (list, tuple)) else [outs]
        return outs[:n_out], outs[n_out:]


def _join_rides(a, b):
    na, nb_ = len(a.arrays), len(b.arrays)
    oa = len(a.out_shapes)
    sa = len(a.scratch_shapes)

    def both(fa, fb):
        def run(ins, outs, sems):
            if fa is not None:
                fa(ins[:na], outs[:oa], sems[:sa])
            if fb is not None:
                fb(ins[na:na + nb_], outs[oa:], sems[sa:])
        return run

    middle = both(a.middle, b.middle) if (a.middle or b.middle) else None
    return _Ride(a.arrays + b.arrays, a.out_shapes + b.out_shapes, a.scratch_shapes + b.scratch_shapes,
                 both(a.start, b.start), both(a.finish, b.finish), middle)


def _load_resident(w_hbm, w_vmem, sems, first):
    def piece(c):
        rows = pl.ds(c * SEG_CHUNK, SEG_CHUNK)
        return pltpu.make_async_copy(w_hbm.at[rows], w_vmem.at[rows], sems.at[c])

    @pl.when(first)
    def _():
        for c in range(w_vmem.shape[0] // SEG_CHUNK):
            piece(c).start()

    def ready(c):
        @pl.when(first)
        def _():
            piece(c).wait()

    return ready


CHIP_ROWS = 2 * SHARD_IN
PROJ_WIDTHS = (2 * D_RNN, D_MODEL + 2 * D_KV, 3 * D_MODEL)
PROJ_DTYPES = (F32, BF16, BF16)


def _chip_pieces():
    starts = [0, PROJ_WIDTHS[0], PROJ_WIDTHS[0] + PROJ_WIDTHS[1], D_IN]
    pieces = []
    for k in range(N_CHIP):
        lo, hi = k * CHIP_ROWS, (k + 1) * CHIP_ROWS
        cur = []
        for a in range(len(PROJ_WIDTHS)):
            s0, s1 = max(lo, starts[a]), min(hi, starts[a + 1])
            if s0 < s1:
                cur.append((a, s0 - starts[a], s1 - s0, s0 - lo))
        pieces.append(cur)
    return pieces


def _gather_project(x, g, wt_shard, name, tm=512):
    t, k = x.shape
    tm = min(tm, t)
    nt = t // tm
    pieces = _chip_pieces()

    def body(x_ref, g_ref, shard_ref, h_out, rx_ref, qkv_ref, ag_ref, wt_all,
             w_c, stage, o32, o16, h_all, send_sems, recv_sems, local_sem, stage_sems, out_sems, h_sems):
        s, ti = pl.program_id(0), pl.program_id(1)
        px, py, pc = _place()
        me, sibling = (px, py, pc), (px, py, 1 - pc)
        chips = [(px, py), (1 - px, py), (px, 1 - py), (1 - px, 1 - py)]
        outs = (rx_ref, qkv_ref, ag_ref)

        def slot(dev):
            return wt_all.at[4 * dev[0] + 2 * dev[1] + dev[2]]

        def copy(kk, block, to, src=None):
            return pltpu.make_async_remote_copy(
                src_ref=slot(block) if src is None else src, dst_ref=slot(block),
                send_sem=send_sems.at[kk], recv_sem=recv_sems.at[kk], device_id=to, device_id_type=MESH)

        mine = pltpu.make_async_copy(shard_ref, slot(me), local_sem)
        first = [copy(0, me, sibling, src=shard_ref)] + [copy(1 + j, me, (*chips[1 + j], pc), src=shard_ref) for j in range(3)]
        passed = [copy(4 + j, (*chips[1 + j], pc), sibling) for j in range(3)]

        @pl.when((s == 0) & (ti == 0))
        def _():
            mine.start()
            for cp in first:
                cp.start()

        for step in range(N_CHIP):
            @pl.when((s == step) & (ti == 0))
            def _(step=step):
                chip = chips[step]
                if step == 0:
                    mine.wait()
                    copy(0, sibling, me).wait_recv()
                else:
                    copy(step, (*chip, pc), me).wait_recv()
                    passed[step - 1].start()
                    copy(3 + step, (*chip, 1 - pc), me).wait_recv()
                loads = [pltpu.make_async_copy(slot((*chip, core)), stage.at[pl.ds(core * SHARD_IN, SHARD_IN)], stage_sems.at[core])
                         for core in (0, 1)]
                for cp in loads:
                    cp.start()
                for cp in loads:
                    cp.wait()
                w_c[...] = stage[...].T

        rows = pl.ds(pl.multiple_of(ti * tm, tm), tm)

        @pl.when(s == 0)
        def _():
            xv = x_ref[...]
            h_all[rows, :] = (xv * lax.rsqrt(jnp.mean(xv * xv, axis=-1, keepdims=True) + EPS) * g_ref[...]).astype(BF16)

        res = jnp.dot(h_all[rows, :], w_c[...], preferred_element_type=F32)

        n = s * nt + ti
        buf = lax.rem(n, 2)
        chip_idx = [2 * cx + cy for cx, cy in chips]

        def chip_at(step):
            return jnp.where(step == 0, chip_idx[0], jnp.where(step == 1, chip_idx[1], jnp.where(step == 2, chip_idx[2], chip_idx[3])))

        def h_write(b, tile):
            return pltpu.make_async_copy(h_all.at[pl.ds(tile * tm, tm)], h_out.at[pl.ds(tile * tm, tm)], h_sems.at[b])

        def writes(kchip, b, tile):
            cps = []
            for idx, (a, col, w, src) in enumerate(pieces[kchip]):
                staged = (o16 if PROJ_DTYPES[a] == BF16 else o32).at[b, :, pl.ds(src, w)]
                cps.append(pltpu.make_async_copy(staged, outs[a].at[pl.ds(tile * tm, tm), pl.ds(col, w)], out_sems.at[b, idx]))
            return cps

        o32[buf] = res
        o16[buf] = res.astype(BF16)
        kcur = chip_at(s)

        @pl.when(n > 0)
        def _():
            kprev = chip_at(lax.div(n - 1, nt))
            for kchip in range(N_CHIP):
                @pl.when(kprev == kchip)
                def _(kchip=kchip):
                    for cp in writes(kchip, 1 - buf, lax.rem(n - 1, nt)):
                        cp.wait()

            @pl.when(n <= nt)
            def _():
                h_write(1 - buf, n - 1).wait()

        @pl.when(s == 0)
        def _():
            h_write(buf, ti).start()

        for kchip in range(N_CHIP):
            @pl.when(kcur == kchip)
            def _(kchip=kchip):
                for cp in writes(kchip, buf, ti):
                    cp.start()

        @pl.when(n == N_CHIP * nt - 1)
        def _():
            for kchip in range(N_CHIP):
                @pl.when(kcur == kchip)
                def _(kchip=kchip):
                    for cp in writes(kchip, buf, ti):
                        cp.wait()
            for cp in first + passed:
                cp.wait_send()

    tile = pl.BlockSpec((tm, k), lambda s, ti: (jnp.where(s == 0, ti, nt - 1), 0))
    return pl.pallas_call(
        body, name=name, grid=(N_CHIP, nt),
        in_specs=[tile, pl.BlockSpec((1, k), lambda s, ti: (0, 0)), ANY],
        out_specs=[ANY, ANY, ANY, ANY, ANY],
        out_shape=[jax.ShapeDtypeStruct((t, k), BF16)]
        + [jax.ShapeDtypeStruct((t, w), dt) for w, dt in zip(PROJ_WIDTHS, PROJ_DTYPES)]
        + [jax.ShapeDtypeStruct((N_DEV, SHARD_IN, k), BF16)],
        scratch_shapes=[pltpu.VMEM((k, CHIP_ROWS), BF16), pltpu.VMEM((CHIP_ROWS, k), BF16),
                        pltpu.VMEM((2, tm, CHIP_ROWS), F32), pltpu.VMEM((2, tm, CHIP_ROWS), BF16), pltpu.VMEM((t, k), BF16),
                        pltpu.SemaphoreType.DMA((7,)), pltpu.SemaphoreType.DMA((7,)), pltpu.SemaphoreType.DMA,
                        pltpu.SemaphoreType.DMA((2,)), pltpu.SemaphoreType.DMA((2, 2)), pltpu.SemaphoreType.DMA((2,))],
        compiler_params=_params("arbitrary", "arbitrary"))(x, g, wt_shard)


def _mm_tn(a, b, name, tm=512, tk=4096):
    ktok, m = a.shape
    n = b.shape[1]
    tk = min(tk, ktok)

    def body(a_ref, b_ref, o_ref):
        @pl.when(pl.program_id(1) == 0)
        def _():
            o_ref[...] = jnp.zeros_like(o_ref)

        o_ref[...] += lax.dot_general(a_ref[...], b_ref[...], TN_DIMS, preferred_element_type=F32)

    return pl.pallas_call(
        body, name=name, grid=(m // tm, ktok // tk),
        in_specs=[pl.BlockSpec((tk, tm), lambda i, kk: (kk, i)), pl.BlockSpec((tk, n), lambda i, kk: (kk, 0))],
        out_specs=pl.BlockSpec((tm, n), lambda i, kk: (i, 0)),
        out_shape=jax.ShapeDtypeStruct((m, n), F32),
        compiler_params=_params("parallel", "arbitrary"))(a, b)


def _segment_chunks(segs):
    bounds = [0]
    for s in segs:
        bounds.append(bounds[-1] + s.shape[1] // SEG_CHUNK)
    return bounds


def _input_grad(segs, wt, x, g, dy, name, tm=512, ride=None):
    m = segs[0].shape[0]
    rows, n = wt.shape
    tm = min(tm, m)
    bounds = _segment_chunks(segs)
    n_seg = len(segs)
    ni = m // tm
    host = _Hosted(ride, n_seg + 4, 2, 2)

    def body(*refs):
        host_refs, start, finish = host.split(refs)
        a_refs = host_refs[:n_seg]
        wt_hbm, x_ref, g_ref, dy_ref, gx_ref, st_ref, wt_vmem, sems = host_refs[n_seg:]
        i = pl.program_id(0)
        start(i == 0)

        @pl.when(i == 0)
        def _():
            st_ref[...] = jnp.zeros_like(st_ref)

        ready = _load_resident(wt_hbm, wt_vmem, sems, i == 0)
        dh = None
        for s in range(n_seg):
            for c in range(bounds[s], bounds[s + 1]):
                ready(c)
            part = jnp.dot(a_refs[s][...], wt_vmem[bounds[s] * SEG_CHUNK:bounds[s + 1] * SEG_CHUNK, :], preferred_element_type=F32)
            dh = part if dh is None else dh + part
        xv = x_ref[...]
        r = lax.rsqrt(jnp.mean(xv * xv, axis=-1, keepdims=True) + EPS)
        xn = xv * r
        dxn = dh * g_ref[...]
        gx_ref[...] = dy_ref[...] + r * (dxn - xn * jnp.mean(dxn * xn, axis=-1, keepdims=True))
        st_ref[...] += _rows8([jnp.sum(dh * xn, axis=0, keepdims=True)], n)
        finish(i == ni - 1)

    tile = pl.BlockSpec((tm, n), lambda i: (i, 0))
    outs = pl.pallas_call(
        body, name=name, grid=(ni,),
        in_specs=[pl.BlockSpec((tm, sg.shape[1]), lambda i: (i, 0)) for sg in segs]
        + [ANY, tile, pl.BlockSpec((1, n), lambda i: (0, 0)), tile] + host.in_specs,
        out_specs=[tile, pl.BlockSpec((8, n), lambda i: (0, 0))] + host.out_specs,
        out_shape=[jax.ShapeDtypeStruct((m, n), F32), jax.ShapeDtypeStruct((8, n), F32)] + host.out_shapes,
        scratch_shapes=[pltpu.VMEM((rows, n), wt.dtype), pltpu.SemaphoreType.DMA((rows // SEG_CHUNK,))] + host.scratch_shapes,
        compiler_params=_params("arbitrary"))(*segs, wt, x, g, dy, *host.arrays)
    res, landed = host.results(outs, 2)
    return (*res, landed) if ride else tuple(res)


def _mm_tn_seg(segs, b, name):
    ktok = segs[0].shape[0]
    n = b.shape[1]
    bounds = _segment_chunks(segs)
    n_seg = len(segs)
    nc = bounds[-1]
    seg_of = [s for s in range(n_seg) for _ in range(bounds[s], bounds[s + 1])]

    def body(*refs):
        a_hbm, b_hbm, o_ref = refs[:n_seg], refs[n_seg], refs[n_seg + 1]
        a_buf, b_vmem, a_sems, b_sem = refs[n_seg + 2:]
        c = pl.program_id(0)

        def fetch(cc):
            s = seg_of[cc]
            cols = pl.ds((cc - bounds[s]) * SEG_CHUNK, SEG_CHUNK)
            return pltpu.make_async_copy(a_hbm[s].at[:, cols], a_buf.at[cc % 2], a_sems.at[cc % 2])

        @pl.when(c == 0)
        def _():
            whole = pltpu.make_async_copy(b_hbm, b_vmem, b_sem)
            whole.start()
            fetch(0).start()
            whole.wait()

        for cc in range(nc):
            @pl.when(c == cc)
            def _(cc=cc):
                if cc + 1 < nc:
                    fetch(cc + 1).start()
                fetch(cc).wait()

        o_ref[...] = lax.dot_general(a_buf[c % 2], b_vmem[...], TN_DIMS, preferred_element_type=F32)

    return pl.pallas_call(
        body, name=name, grid=(nc,),
        in_specs=[ANY] * (n_seg + 1), out_specs=pl.BlockSpec((SEG_CHUNK, n), lambda c: (c, 0)),
        out_shape=jax.ShapeDtypeStruct((nc * SEG_CHUNK, n), F32),
        scratch_shapes=[pltpu.VMEM((2, ktok, SEG_CHUNK), segs[0].dtype), pltpu.VMEM((ktok, n), b.dtype),
                        pltpu.SemaphoreType.DMA((2,)), pltpu.SemaphoreType.DMA],
        compiler_params=_params("arbitrary"))(*segs, b)


def _side_pieces(n_chunks):
    table = []
    for core in (0, 1):
        sides = ([None] * n_chunks, [None] * n_chunks)
        for cc in range(n_chunks):
            g0 = cc * SEG_CHUNK
            for d in range(N_DEV):
                lo, hi = max(g0, d * SHARD_IN), min(g0 + SEG_CHUNK, (d + 1) * SHARD_IN)
                if lo < hi:
                    side = sides[0 if d % 2 == core else 1]
                    assert side[cc] is None
                    side[cc] = (lo - g0, hi - lo, (d // 2) * SHARD_IN + lo - d * SHARD_IN)
        table.append(sides)
    return table


def _mm_tn_seg_pair(segs, b, name):
    ktok = segs[0].shape[0]
    n = b.shape[1]
    bounds = _segment_chunks(segs)
    n_seg = len(segs)
    nc = bounds[-1]
    assert nc * SEG_CHUNK == D_IN
    seg_of = [s for s in range(n_seg) for _ in range(bounds[s], bounds[s + 1])]
    table = _side_pieces(nc)

    def body(*refs):
        a_hbm, b_hbm, pair_ref = refs[:n_seg], refs[n_seg], refs[n_seg + 1]
        a_buf, b_vmem, res_buf, recv_all, out_stage, a_sems, b_sem, send_sems, recv_sems, out_sems = refs[n_seg + 2:]
        c = pl.program_id(0)
        px, py, pc = _place()

        def fetch(cc):
            s = seg_of[cc]
            cols = pl.ds((cc - bounds[s]) * SEG_CHUNK, SEG_CHUNK)
            return pltpu.make_async_copy(a_hbm[s].at[:, cols], a_buf.at[cc % 2], a_sems.at[cc % 2])

        @pl.when(c == 0)
        def _():
            whole = pltpu.make_async_copy(b_hbm, b_vmem, b_sem)
            whole.start()
            fetch(0).start()
            whole.wait()

        for cc in range(nc):
            @pl.when(c == cc)
            def _(cc=cc):
                if cc + 1 < nc:
                    fetch(cc + 1).start()
                fetch(cc).wait()

        res_buf[c % 2] = lax.dot_general(a_buf[c % 2], b_vmem[...], TN_DIMS, preferred_element_type=F32)

        def crossing(cc, piece):
            off, rows, at = piece
            return pltpu.make_async_remote_copy(
                src_ref=res_buf.at[cc % 2, pl.ds(off, rows)], dst_ref=recv_all.at[pl.ds(at, rows)],
                send_sem=send_sems.at[cc], recv_sem=recv_sems.at[cc], device_id=(px, py, 1 - pc), device_id_type=MESH)

        def write(j, piece):
            _, rows, at = piece
            return pltpu.make_async_copy(out_stage.at[j % 2, pl.ds(0, rows)], pair_ref.at[pl.ds(at, rows)], out_sems.at[j % 2])

        for core in (0, 1):
            mine, theirs = table[core]
            kept = [cc for cc in range(nc) if mine[cc] is not None]

            def settle(cc, mine=mine, theirs=theirs, kept=kept):
                if theirs[cc] is not None:
                    crossing(cc, theirs[cc]).wait_send()
                if mine[cc] is not None:
                    j = kept.index(cc)
                    off, rows, at = mine[cc]
                    crossing(cc, mine[cc]).wait_recv()
                    if j >= 2:
                        write(j - 2, mine[kept[j - 2]]).wait()
                    out_stage[j % 2, 0:rows] = (res_buf[cc % 2, off:off + rows] + recv_all[at:at + rows]).astype(BF16)
                    write(j, mine[cc]).start()

            for cc in range(nc):
                @pl.when((pc == core) & (c == cc))
                def _(cc=cc, settle=settle, mine=mine, theirs=theirs, kept=kept):
                    if theirs[cc] is not None:
                        crossing(cc, theirs[cc]).start()
                    if cc > 0:
                        settle(cc - 1)
                    if cc == nc - 1:
                        settle(cc)
                        for j in range(max(0, len(kept) - 2), len(kept)):
                            write(j, mine[kept[j]]).wait()

    pair = pl.pallas_call(
        body, name=name, grid=(nc,),
        in_specs=[ANY] * (n_seg + 1), out_specs=ANY,
        out_shape=jax.ShapeDtypeStruct((N_CHIP * SHARD_IN, n), BF16),
        scratch_shapes=[pltpu.VMEM((2, ktok, SEG_CHUNK), segs[0].dtype), pltpu.VMEM((ktok, n), b.dtype),
                        pltpu.VMEM((2, SEG_CHUNK, n), F32), pltpu.VMEM((N_CHIP * SHARD_IN, n), F32),
                        pltpu.VMEM((2, SEG_CHUNK, n), BF16),
                        pltpu.SemaphoreType.DMA((2,)), pltpu.SemaphoreType.DMA,
                        pltpu.SemaphoreType.DMA((nc,)), pltpu.SemaphoreType.DMA((nc,)), pltpu.SemaphoreType.DMA((2,))],
        compiler_params=_params("arbitrary"))(*segs, b)
    return pair.reshape(N_CHIP, SHARD_IN, n)


def _branches_fwd(z_rnn, z_attn, ag_ml, b_gate, w_rnn, w_attn, w_out, x, target, g_post, name, tm=512):
    t, d = x.shape
    tm = min(tm, t)

    def body(zr_ref, za_ref, lr_ref, la_ref, br_ref, ba_ref, wr_ref, wa_ref, wo_ref, x_ref, t_ref, g_ref,
             brr_ref, bra_ref, mg_ref, do_ref, dy_ref, st_ref):
        @pl.when(pl.program_id(0) == 0)
        def _():
            st_ref[...] = jnp.zeros_like(st_ref)

        br_rnn = jnp.dot(zr_ref[...], wr_ref[...], preferred_element_type=F32)
        br_attn = jnp.dot(za_ref[...], wa_ref[...], preferred_element_type=F32)
        brr_ref[...] = br_rnn.astype(BF16)
        bra_ref[...] = br_attn.astype(BF16)
        g_rnn = _sigmoid(lr_ref[...].astype(F32) + br_ref[...])
        g_attn = _sigmoid(la_ref[...].astype(F32) + ba_ref[...])
        merged = (g_rnn * br_rnn + g_attn * br_attn).astype(BF16)
        mg_ref[...] = merged
        o = jnp.dot(merged, wo_ref[...], preferred_element_type=F32)
        g = g_ref[...]
        r = lax.rsqrt(jnp.mean(o * o, axis=-1, keepdims=True) + EPS)
        nrm = o * r
        err = x_ref[...] + nrm * g - t_ref[...]
        dy = err * (1.0 / d)
        dy_ref[...] = dy
        dn = dy * g
        do_ref[...] = (r * (dn - nrm * jnp.mean(dn * nrm, axis=-1, keepdims=True))).astype(BF16)
        st_ref[...] += _rows8([jnp.sum(dy * nrm, axis=0, keepdims=True), jnp.sum(err * err, axis=0, keepdims=True)], d)

    tile = pl.BlockSpec((tm, d), lambda i: (i, 0))
    weight = pl.BlockSpec((d, d), lambda i: (0, 0))
    bf = jax.ShapeDtypeStruct((t, d), BF16)
    return pl.pallas_call(
        body, name=name, grid=(t // tm,),
        in_specs=[tile, tile, pl.BlockSpec((tm, d), lambda i: (i, 1)), pl.BlockSpec((tm, d), lambda i: (i, 2)),
                  pl.BlockSpec((1, d), lambda i: (0, 0)), pl.BlockSpec((1, d), lambda i: (0, 1)),
                  weight, weight, weight, tile, tile, pl.BlockSpec((1, d), lambda i: (0, 0))],
        out_specs=[tile, tile, tile, tile, tile, pl.BlockSpec((8, d), lambda i: (0, 0))],
        out_shape=[bf, bf, bf, bf, jax.ShapeDtypeStruct((t, d), F32), jax.ShapeDtypeStruct((8, d), F32)],
        compiler_params=_params("arbitrary"))(z_rnn, z_attn, ag_ml, ag_ml, b_gate, b_gate, w_rnn, w_attn, w_out, x, target, g_post)


def _branches_bwd(dout, br_rnn, br_attn, ag_ml, b_gate, w_rnn, w_attn, w_out, name, tm=512):
    t, d = br_rnn.shape
    tm = min(tm, t)

    def body(do_ref, r_ref, a_ref, lr_ref, la_ref, br_ref, ba_ref, wr_ref, wa_ref, wo_ref,
             dr_ref, da_ref, dl_ref, dzr_ref, dza_ref, st_ref, wt_ref):
        @pl.when(pl.program_id(0) == 0)
        def _():
            st_ref[...] = jnp.zeros_like(st_ref)
            wt_ref[0] = wo_ref[...].T
            wt_ref[1] = wr_ref[...].T
            wt_ref[2] = wa_ref[...].T

        dm = jnp.dot(do_ref[...], wt_ref[0], preferred_element_type=F32)
        g_rnn = _sigmoid(lr_ref[...].astype(F32) + br_ref[...])
        g_attn = _sigmoid(la_ref[...].astype(F32) + ba_ref[...])
        dbr_rnn = (dm * g_rnn).astype(BF16)
        dbr_attn = (dm * g_attn).astype(BF16)
        dr_ref[...] = dbr_rnn
        da_ref[...] = dbr_attn
        dl_rnn = dm * r_ref[...].astype(F32) * g_rnn * (1.0 - g_rnn)
        dl_attn = dm * a_ref[...].astype(F32) * g_attn * (1.0 - g_attn)
        dl_ref[:, 0:d] = dl_rnn.astype(BF16)
        dl_ref[:, d:2 * d] = dl_attn.astype(BF16)
        st_ref[...] += _rows8([jnp.sum(dl_rnn, axis=0, keepdims=True), jnp.sum(dl_attn, axis=0, keepdims=True)], d)
        dzr_ref[...] = jnp.dot(dbr_rnn, wt_ref[1], preferred_element_type=F32).astype(BF16)
        dza_ref[...] = jnp.dot(dbr_attn, wt_ref[2], preferred_element_type=F32).astype(BF16)

    tile = pl.BlockSpec((tm, d), lambda i: (i, 0))
    weight = pl.BlockSpec((d, d), lambda i: (0, 0))
    bf = jax.ShapeDtypeStruct((t, d), BF16)
    return pl.pallas_call(
        body, name=name, grid=(t // tm,),
        in_specs=[tile, tile, tile, pl.BlockSpec((tm, d), lambda i: (i, 1)), pl.BlockSpec((tm, d), lambda i: (i, 2)),
                  pl.BlockSpec((1, d), lambda i: (0, 0)), pl.BlockSpec((1, d), lambda i: (0, 1)), weight, weight, weight],
        out_specs=[tile, tile, pl.BlockSpec((tm, 2 * d), lambda i: (i, 0)), tile, tile, pl.BlockSpec((8, d), lambda i: (0, 0))],
        out_shape=[bf, bf, jax.ShapeDtypeStruct((t, 2 * d), BF16), bf, bf, jax.ShapeDtypeStruct((8, d), F32)],
        scratch_shapes=[pltpu.VMEM((3, d, d), BF16)],
        compiler_params=_params("arbitrary"))(dout, br_rnn, br_attn, ag_ml, ag_ml, b_gate, b_gate, w_rnn, w_attn, w_out)


def _lru_gates(c, wa, ba, wx, bx, sp):
    cb = c.astype(BF16)
    r = _sigmoid(jnp.dot(cb, wa, preferred_element_type=F32) + ba)
    ig = _sigmoid(jnp.dot(cb, wx, preferred_element_type=F32) + bx)
    log_a = (-LRU_C) * r * sp
    a = jnp.exp(log_a)
    mult = jnp.sqrt(-jnp.tanh(log_a) * (a * a + 1.0))
    return cb, r, ig, a, mult


SUBLANES = 8


def _scan_fwd(a, u, carry, tt):
    w = a.shape[1]
    ng = tt // SUBLANES
    a3 = a.reshape(ng, SUBLANES, w)
    u3 = u.reshape(ng, SUBLANES, w)
    sub = lax.broadcasted_iota(jnp.int32, (ng, SUBLANES, w), 1)
    d = 1
    while d < SUBLANES:
        keep = sub >= d
        u3 = u3 + a3 * jnp.where(keep, pltpu.roll(u3, d, 1), 0.0)
        a3 = a3 * jnp.where(keep, pltpu.roll(a3, d, 1), 1.0)
        d *= 2
    out = []
    for g in range(ng):
        hg = u3[g] + a3[g] * carry
        out.append(hg)
        carry = hg[SUBLANES - 1:SUBLANES, :]
    return jnp.concatenate(out, axis=0)


def _scan_rev(b, g, carry, tt):
    w = b.shape[1]
    ng = tt // SUBLANES
    b3 = b.reshape(ng, SUBLANES, w)
    g3 = g.reshape(ng, SUBLANES, w)
    sub = lax.broadcasted_iota(jnp.int32, (ng, SUBLANES, w), 1)
    d = 1
    while d < SUBLANES:
        keep = sub < SUBLANES - d
        g3 = g3 + b3 * jnp.where(keep, pltpu.roll(g3, SUBLANES - d, 1), 0.0)
        b3 = b3 * jnp.where(keep, pltpu.roll(b3, SUBLANES - d, 1), 1.0)
        d *= 2
    out = [None] * ng
    for k in range(ng - 1, -1, -1):
        hk = g3[k] + b3[k] * carry
        out[k] = hk
        carry = hk[0:1, :]
    return jnp.concatenate(out, axis=0)


def _conv_taps(cw, bias, x, ext_ref, tt):
    x2 = ext_ref[7:7 + tt, :]
    x1 = ext_ref[6:6 + tt, :]
    x0 = ext_ref[5:5 + tt, :]
    c = bias + cw[3:4] * x + cw[2:3] * x2 + cw[1:2] * x1 + cw[0:1] * x0
    return c, x2, x1, x0


def _rnn_fwd(rx_rg, cw, cb, wa, ba, wx, bx, lam, name, tt=512):
    t = rx_rg.shape[0]
    tt = min(tt, t)
    w = GROUP_W

    def body(rx_ref, rg_ref, cw_ref, cb_ref, wa_ref, ba_ref, wx_ref, bx_ref, lam_ref, y_ref, z_ref, ext_ref, hc_ref):
        @pl.when(pl.program_id(1) == 0)
        def _():
            ext_ref[0:8, :] = jnp.zeros((8, w), F32)
            hc_ref[...] = jnp.zeros((8, w), F32)

        x = rx_ref[...]
        ext_ref[8:8 + tt, :] = x
        c, _, _, _ = _conv_taps(cw_ref[...], cb_ref[...], x, ext_ref, tt)
        ext_ref[0:8, :] = x[tt - 8:tt, :]
        sp = _softplus(-lam_ref[...])
        _, _, ig, a, mult = _lru_gates(c, wa_ref[...], ba_ref[...], wx_ref[...], bx_ref[...], sp)
        h = _scan_fwd(a, mult * (ig * c), hc_ref[7:8, :], tt)
        hc_ref[...] = h[tt - 8:tt, :]
        y_ref[...] = h
        rg = rg_ref[...]
        z_ref[...] = (h * rg * _sigmoid(rg)).astype(BF16)

    vec = pl.BlockSpec((1, w), lambda g, i: (0, g))
    mat = pl.BlockSpec((None, w, w), lambda g, i: (g, 0, 0))
    tile = pl.BlockSpec((tt, w), lambda g, i: (i, g))
    return pl.pallas_call(
        body, name=name, grid=(N_GROUPS, t // tt),
        in_specs=[tile, pl.BlockSpec((tt, w), lambda g, i: (i, N_GROUPS + g)),
                  pl.BlockSpec((4, w), lambda g, i: (0, g)), vec, mat, vec, mat, vec, vec],
        out_specs=[tile, tile],
        out_shape=[jax.ShapeDtypeStruct((t, D_RNN), F32), jax.ShapeDtypeStruct((t, D_RNN), BF16)],
        scratch_shapes=[pltpu.VMEM((tt + 8, w), F32), pltpu.VMEM((8, w), F32)],
        compiler_params=_params("parallel", "arbitrary"))(rx_rg, rx_rg, cw, cb, wa, ba, wx, bx, lam)


def _rnn_bwd(rx_rg, y, dz, cw, cb, wa, ba, wx, bx, lam, name, tt=512, ride=None):
    t = rx_rg.shape[0]
    tt = min(tt, t)
    nt = t // tt
    w = GROUP_W

    host = _Hosted(ride, 13, 5, 6)

    def body(*refs):
        host_refs, start, finish = host.split(refs)
        (rx_ref, rg_ref, rxt_ref, y_ref, yt_ref, dz_ref, cw_ref, cb_ref, wa_ref, ba_ref, wx_ref, bx_ref, lam_ref,
         drx_ref, drg_ref, st_ref, gda_ref, gdx_ref, ext_ref, dcx_ref, wcar_ref, acar_ref, dwa_ref, dwx_ref) = host_refs
        ii = pl.program_id(1)
        start((pl.program_id(0) == 0) & (ii == 0))

        @pl.when(ii == 0)
        def _():
            wcar_ref[...] = jnp.zeros((8, w), F32)
            acar_ref[...] = jnp.zeros((8, w), F32)
            dcx_ref[tt:tt + 8, :] = jnp.zeros((8, w), F32)
            st_ref[...] = jnp.zeros_like(st_ref)
            dwa_ref[...] = jnp.zeros_like(dwa_ref)
            dwx_ref[...] = jnp.zeros_like(dwx_ref)

        has_prev = jnp.where(ii == nt - 1, 0.0, 1.0)
        x = rx_ref[...]
        ext_ref[0:8, :] = rxt_ref[...] * has_prev
        ext_ref[8:8 + tt, :] = x
        cwv = cw_ref[...]
        c, x2, x1, x0 = _conv_taps(cwv, cb_ref[...], x, ext_ref, tt)
        lam = lam_ref[...]
        sp = _softplus(-lam)
        wa = wa_ref[...]
        wx = wx_ref[...]
        cb16, r, ig, a, mult = _lru_gates(c, wa, ba_ref[...], wx, bx_ref[...], sp)

        rg = rg_ref[...]
        sg = _sigmoid(rg)
        dz = dz_ref[...].astype(F32)
        yv = y_ref[...]
        drg_ref[...] = (dz * yv * (sg * (1.0 + rg * (1.0 - sg)))).astype(BF16)

        row = lax.broadcasted_iota(jnp.int32, (tt, w), 0)
        b = jnp.where(row < tt - 1, pltpu.roll(a, tt - 1, 0), acar_ref[0:1, :])
        dh = _scan_rev(b, dz * (rg * sg), wcar_ref[0:1, :], tt)
        wcar_ref[...] = dh[0:8, :]
        acar_ref[...] = a[0:8, :]

        hprev = jnp.where(row >= 1, pltpu.roll(yv, 1, 0), yt_ref[7:8, :] * has_prev)
        dmult = dh * (ig * c)
        dig = dh * mult * c
        dlog_a = dh * hprev * a - dmult * (a * a / mult)
        dpa = dlog_a * ((-LRU_C) * sp) * r * (1.0 - r)
        dpx = dig * ig * (1.0 - ig)
        dsp = jnp.sum(dlog_a * r, axis=0, keepdims=True) * (-LRU_C)
        dlam = dsp * (-_sigmoid(-lam))
        dpa16 = dpa.astype(BF16)
        dpx16 = dpx.astype(BF16)
        dwa_ref[...] += lax.dot_general(cb16, dpa16, TN_DIMS, preferred_element_type=F32)
        dwx_ref[...] += lax.dot_general(cb16, dpx16, TN_DIMS, preferred_element_type=F32)
        dc = (dh * mult * ig
              + lax.dot_general(dpa16, wa, NT_DIMS, preferred_element_type=F32)
              + lax.dot_general(dpx16, wx, NT_DIMS, preferred_element_type=F32))

        dcx_ref[0:tt, :] = dc
        drx = (cwv[3:4] * dc + cwv[2:3] * dcx_ref[1:1 + tt, :] + cwv[1:2] * dcx_ref[2:2 + tt, :]
               + cwv[0:1] * dcx_ref[3:3 + tt, :])
        drx_ref[...] = drx.astype(BF16)
        dcx_ref[tt:tt + 8, :] = dc[0:8, :]

        def colsum(v):
            return jnp.sum(v, axis=0, keepdims=True)

        st_ref[...] += _rows8([colsum(dc), colsum(dpa), colsum(dpx), dlam,
                               colsum(dc * x0), colsum(dc * x1), colsum(dc * x2), colsum(dc * x)], w)

        @pl.when(ii == nt - 1)
        def _():
            for blk in range(GROUP_W // RNN_BLOCK_W):
                rows = slice(blk * RNN_BLOCK_W, (blk + 1) * RNN_BLOCK_W)
                gda_ref[blk] = dwa_ref[rows, rows]
                gdx_ref[blk] = dwx_ref[rows, rows]

        finish((pl.program_id(0) == N_GROUPS - 1) & (ii == nt - 1))

    def rev(ii):
        return nt - 1 - ii

    def tail(g, ii):
        return (jnp.maximum(rev(ii) * (tt // 8) - 1, 0), g)

    vec = pl.BlockSpec((1, w), lambda g, ii: (0, g))
    mat = pl.BlockSpec((None, w, w), lambda g, ii: (g, 0, 0))
    tile = pl.BlockSpec((tt, w), lambda g, ii: (rev(ii), g))
    diag_shape = (N_GROUPS, GROUP_W // RNN_BLOCK_W, RNN_BLOCK_W, RNN_BLOCK_W)
    diag = pl.BlockSpec((None,) + diag_shape[1:], lambda g, ii: (g, 0, 0, 0))
    outs = pl.pallas_call(
        body, name=name, grid=(N_GROUPS, nt),
        in_specs=[tile, pl.BlockSpec((tt, w), lambda g, ii: (rev(ii), N_GROUPS + g)), pl.BlockSpec((8, w), tail),
                  tile, pl.BlockSpec((8, w), tail), tile,
                  pl.BlockSpec((4, w), lambda g, ii: (0, g)), vec, mat, vec, mat, vec, vec] + host.in_specs,
        out_specs=[tile, tile, pl.BlockSpec((8, w), lambda g, ii: (0, g)), diag, diag] + host.out_specs,
        out_shape=[jax.ShapeDtypeStruct((t, D_RNN), BF16), jax.ShapeDtypeStruct((t, D_RNN), BF16),
                   jax.ShapeDtypeStruct((8, D_RNN), F32),
                   jax.ShapeDtypeStruct(diag_shape, F32), jax.ShapeDtypeStruct(diag_shape, F32)] + host.out_shapes,
        scratch_shapes=[pltpu.VMEM((tt + 8, w), F32), pltpu.VMEM((tt + 8, w), F32), pltpu.VMEM((8, w), F32),
                        pltpu.VMEM((8, w), F32), pltpu.VMEM((w, w), F32), pltpu.VMEM((w, w), F32)] + host.scratch_shapes,
        compiler_params=_params("arbitrary" if ride else "parallel", "arbitrary"))(
            rx_rg, rx_rg, rx_rg, y, y, dz, cw, cb, wa, ba, wx, bx, lam, *host.arrays)
    res, landed = host.results(outs, 5)
    return (*res, landed) if ride else tuple(res)


def _half_mask(shape, half):
    lane = lax.broadcasted_iota(jnp.int32, shape, 1)
    return (lane >= HEAD_DIM) if half else (lane < HEAD_DIM)


def _dup_half(t, half):
    sel = jnp.where(_half_mask(t.shape, half), t, 0.0)
    return sel + pltpu.roll(sel, HEAD_DIM, 1)


def _band_geometry(n):
    qi = lax.broadcasted_iota(jnp.int32, (BLOCK, 2 * BLOCK), 0)
    kj = lax.broadcasted_iota(jnp.int32, (BLOCK, 2 * BLOCK), 1)
    dist = BLOCK + qi - kj
    first_key = jnp.where(n > 0, 0, BLOCK)
    valid = (dist >= 0) & (dist < BLOCK) & (kj >= first_key)
    return dist.astype(F32), valid


GROUP = 4


def _kv_dup(prev_ref, cur_ref, hk, scale=1.0):
    tile = hk // 2
    kt = jnp.concatenate([prev_ref[:, tile * LANE:(tile + 1) * LANE], cur_ref[:, tile * LANE:(tile + 1) * LANE]], axis=0)
    return (_dup_half(kt.astype(F32), hk % 2) * scale).astype(BF16)


def _fill_bias(bias_ref, sm_ref, n):
    distf, valid = _band_geometry(n)
    for head in range(N_Q_HEADS):
        bias_ref[head] = jnp.where(valid, -sm_ref[1, head] * distf, MASKED)


def _head_scores(q2s, half, kdup, bias):
    qm = jnp.where(_half_mask(q2s.shape, half), q2s, jnp.zeros_like(q2s))
    return qm, lax.dot_general(qm, kdup, NT_DIMS, preferred_element_type=F32) + bias


def _attn_specs(nb, clamp_last):
    def blk(n):
        return jnp.minimum(n, nb - 1) if clamp_last else n

    q_spec = pl.BlockSpec((BLOCK, D_MODEL), lambda n: (blk(n), 0))
    k_prev = pl.BlockSpec((BLOCK, D_KV), lambda n: (jnp.maximum(blk(n) - 1, 0), D_MODEL // D_KV))
    k_cur = pl.BlockSpec((BLOCK, D_KV), lambda n: (blk(n), D_MODEL // D_KV))
    v_prev = pl.BlockSpec((BLOCK, D_KV), lambda n: (jnp.maximum(blk(n) - 1, 0), D_MODEL // D_KV + 1))
    v_cur = pl.BlockSpec((BLOCK, D_KV), lambda n: (blk(n), D_MODEL // D_KV + 1))
    return q_spec, k_prev, k_cur, v_prev, v_cur


def _attn_fwd(sm, qkv, ag_ml, name, ride=None):
    t = qkv.shape[0]
    nb = t // BLOCK

    host = _Hosted(ride, 7, 3, 1)

    def body(*refs):
        (sm_ref, q_ref, kp_ref, kc_ref, vp_ref, vc_ref, ag_ref, y_ref, z_ref, lse_ref, bias_ref), start, finish = host.split(refs)
        n = pl.program_id(0)
        start(n == 0)
        start.middle(n == (3 * nb) // 4)

        @pl.when(n <= 1)
        def _():
            _fill_bias(bias_ref, sm_ref, n)

        lane = lax.broadcasted_iota(jnp.int32, (BLOCK, LANE), 1)
        low = lane < HEAD_DIM
        lse = jnp.zeros((BLOCK, LANE), F32)
        for hk in range(N_Q_HEADS // GROUP):
            kdup = _kv_dup(kp_ref, kc_ref, hk)
            vdup = _kv_dup(vp_ref, vc_ref, hk)
            for k in (0, 1):
                c = slice((2 * hk + k) * LANE, (2 * hk + k + 1) * LANE)
                q2s = q_ref[:, c] * ATTN_SCALE
                outs = []
                for half in (0, 1):
                    head = GROUP * hk + 2 * k + half
                    _, s = _head_scores(q2s, half, kdup, bias_ref[head])
                    sink = sm_ref[0, head]
                    m = jnp.maximum(jnp.max(s, axis=1, keepdims=True), sink)
                    e = jnp.exp(s - m)
                    l = jnp.sum(e, axis=1, keepdims=True) + jnp.exp(sink - m)
                    outs.append(jnp.dot((e * (1.0 / l)).astype(BF16), vdup, preferred_element_type=F32))
                    lse = jnp.where(lane == head, m + jnp.log(l), lse)
                yt = jnp.where(low, outs[0], outs[1])
                y_ref[:, c] = yt
                ag = ag_ref[:, c].astype(F32)
                z_ref[:, c] = (yt * ag * _sigmoid(ag)).astype(BF16)
        lse_ref[...] = lse
        finish(n == nb - 1)

    q_spec, k_prev, k_cur, v_prev, v_cur = _attn_specs(nb, False)
    wide = pl.BlockSpec((BLOCK, D_MODEL), lambda n: (n, 0))
    outs = pl.pallas_call(
        body, name=name, grid=(nb,),
        in_specs=[pl.BlockSpec(memory_space=pltpu.SMEM), q_spec, k_prev, k_cur, v_prev, v_cur, wide] + host.in_specs,
        out_specs=[wide, wide, pl.BlockSpec((BLOCK, LANE), lambda n: (n, 0))] + host.out_specs,
        out_shape=[jax.ShapeDtypeStruct((t, D_MODEL), F32), jax.ShapeDtypeStruct((t, D_MODEL), BF16),
                   jax.ShapeDtypeStruct((t, LANE), F32)] + host.out_shapes,
        scratch_shapes=[pltpu.VMEM((N_Q_HEADS, BLOCK, 2 * BLOCK), F32)] + host.scratch_shapes,
        compiler_params=_params("arbitrary"))(sm, qkv, qkv, qkv, qkv, qkv, ag_ml, *host.arrays)
    res, landed = host.results(outs, 3)
    return (*res, landed) if ride else tuple(res)


def _attn_bwd(sm, qkv, ag_ml, y, lse, dz, name, ride=None):
    t = qkv.shape[0]
    nb = t // BLOCK
    host = _Hosted(ride, 10, 4, 3)

    def body(*refs):
        host_refs, start, finish = host.split(refs)
        (sm_ref, q_ref, kp_ref, kc_ref, vp_ref, vc_ref, ag_ref, y_ref, lse_ref, dz_ref,
         dq_ref, dkv_ref, dag_ref, ds_ref, ck_ref, cv_ref, bias_ref) = host_refs
        n = pl.program_id(0)
        start(n == 0)
        start.middle(n == (3 * nb) // 4)

        @pl.when(n == 0)
        def _():
            ck_ref[...] = jnp.zeros_like(ck_ref)
            cv_ref[...] = jnp.zeros_like(cv_ref)
            ds_ref[...] = jnp.zeros_like(ds_ref)

        @pl.when(n <= 1)
        def _():
            _fill_bias(bias_ref, sm_ref, n)

        @pl.when(n < nb)
        def _():
            lane8 = lax.broadcasted_iota(jnp.int32, (8, LANE), 1)
            row8 = lax.broadcasted_iota(jnp.int32, (8, LANE), 0)
            dsink = jnp.zeros((8, LANE), F32)
            dk_heads, dv_heads = [], []
            lse_tile = lse_ref[...]
            for hk in range(N_Q_HEADS // GROUP):
                kdup = _kv_dup(kp_ref, kc_ref, hk)
                ks = _kv_dup(kp_ref, kc_ref, hk, ATTN_SCALE)
                vdup = _kv_dup(vp_ref, vc_ref, hk)
                qms, dyhs, y_rows = [], [], []
                for k in (0, 1):
                    c = slice((2 * hk + k) * LANE, (2 * hk + k + 1) * LANE)
                    ag = ag_ref[:, c].astype(F32)
                    sg = _sigmoid(ag)
                    dzt = dz_ref[:, c].astype(F32)
                    yt = y_ref[:, c]
                    dag_ref[:, c] = (dzt * yt * (sg * (1.0 + ag * (1.0 - sg)))).astype(BF16)
                    dyt = dzt * (ag * sg)
                    q2s = q_ref[:, c] * ATTN_SCALE
                    for half in (0, 1):
                        hm = _half_mask(q2s.shape, half)
                        qms.append(jnp.where(hm, q2s, jnp.zeros_like(q2s)))
                        dyhs.append(jnp.where(hm, dyt, 0.0))
                        y_rows.append(yt)
                qm4 = jnp.concatenate(qms, axis=0)
                dy4 = jnp.concatenate(dyhs, axis=0)
                dy4_16 = dy4.astype(BF16)
                s4 = lax.dot_general(qm4, kdup, NT_DIMS, preferred_element_type=F32)
                dp4 = lax.dot_general(dy4_16, vdup, NT_DIMS, preferred_element_type=F32)
                probs16, ds16 = [], []
                for r in range(GROUP):
                    head = GROUP * hk + r
                    rows = slice(r * BLOCK, (r + 1) * BLOCK)
                    lh = lse_tile[:, head:head + 1]
                    probs = jnp.exp(s4[rows] + bias_ref[head] - lh)
                    psink = jnp.exp(sm_ref[0, head] - lh)
                    delta = jnp.sum(dyhs[r] * y_rows[r], axis=1, keepdims=True)
                    ds16.append((probs * (dp4[rows] - delta)).astype(BF16))
                    probs16.append(probs.astype(BF16))
                    dsink = dsink + jnp.where((row8 == 0) & (lane8 == head),
                                              -jnp.sum(psink * delta, axis=0, keepdims=True), 0.0)
                ds4 = jnp.concatenate(ds16, axis=0)
                p4 = jnp.concatenate(probs16, axis=0)
                dq4 = jnp.dot(ds4, ks, preferred_element_type=F32)
                low = _half_mask((BLOCK, LANE), 0)
                for k in (0, 1):
                    c = slice((2 * hk + k) * LANE, (2 * hk + k + 1) * LANE)
                    dq_ref[:, c] = jnp.where(low, dq4[2 * k * BLOCK:(2 * k + 1) * BLOCK],
                                             dq4[(2 * k + 1) * BLOCK:(2 * k + 2) * BLOCK]).astype(BF16)
                dk_acc = lax.dot_general(ds4, qm4, TN_DIMS, preferred_element_type=F32)
                dv_acc = lax.dot_general(p4, dy4_16, TN_DIMS, preferred_element_type=F32)
                dk_heads.append(dk_acc + pltpu.roll(dk_acc, HEAD_DIM, 1))
                dv_heads.append(dv_acc + pltpu.roll(dv_acc, HEAD_DIM, 1))
            ds_ref[...] += dsink
            low = _half_mask((2 * BLOCK, LANE), 0)
            for tile in range(2):
                cols = slice(tile * LANE, (tile + 1) * LANE)
                dkt = jnp.where(low, dk_heads[2 * tile], dk_heads[2 * tile + 1])
                dvt = jnp.where(low, dv_heads[2 * tile], dv_heads[2 * tile + 1])
                dkv_ref[:, cols] = (ck_ref[:, cols] + dkt[0:BLOCK, :]).astype(BF16)
                dkv_ref[:, D_KV + tile * LANE:D_KV + (tile + 1) * LANE] = (cv_ref[:, cols] + dvt[0:BLOCK, :]).astype(BF16)
                ck_ref[:, cols] = dkt[BLOCK:2 * BLOCK, :]
                cv_ref[:, cols] = dvt[BLOCK:2 * BLOCK, :]

        @pl.when(n == nb)
        def _():
            dkv_ref[:, 0:D_KV] = ck_ref[...].astype(BF16)
            dkv_ref[:, D_KV:2 * D_KV] = cv_ref[...].astype(BF16)

        finish(n == nb)

    q_spec, k_prev, k_cur, v_prev, v_cur = _attn_specs(nb, True)
    wide = pl.BlockSpec((BLOCK, D_MODEL), lambda n: (jnp.minimum(n, nb - 1), 0))
    outs = pl.pallas_call(
        body, name=name, grid=(nb + 1,),
        in_specs=[pl.BlockSpec(memory_space=pltpu.SMEM), q_spec, k_prev, k_cur, v_prev, v_cur, wide, wide,
                  pl.BlockSpec((BLOCK, LANE), lambda n: (jnp.minimum(n, nb - 1), 0)), wide] + host.in_specs,
        out_specs=[wide, pl.BlockSpec((BLOCK, 2 * D_KV), lambda n: (jnp.maximum(n - 1, 0), 0)), wide,
                   pl.BlockSpec((8, LANE), lambda n: (0, 0))] + host.out_specs,
        out_shape=[jax.ShapeDtypeStruct((t, D_MODEL), BF16), jax.ShapeDtypeStruct((t, 2 * D_KV), BF16),
                   jax.ShapeDtypeStruct((t, D_MODEL), BF16), jax.ShapeDtypeStruct((8, LANE), F32)] + host.out_shapes,
        scratch_shapes=[pltpu.VMEM((BLOCK, D_KV), F32), pltpu.VMEM((BLOCK, D_KV), F32),
                        pltpu.VMEM((N_Q_HEADS, BLOCK, 2 * BLOCK), F32)] + host.scratch_shapes,
        compiler_params=_params("arbitrary"))(sm, qkv, qkv, qkv, qkv, qkv, ag_ml, y, lse, dz, *host.arrays)
    res, landed = host.results(outs, 4)
    return (*res, landed) if ride else tuple(res)


def _local_grads(x, target, p, project, late_weights=None, reduce_out=None, reduce_in=None):
    h, rx_rg, qkv, ag_ml, wt = project(x, p["pre_g"])
    if late_weights is None:
        y_attn, z_attn, lse = _attn_fwd(p["sm"], qkv, ag_ml, "attn_fwd")
    else:
        y_attn, z_attn, lse, landed = _attn_fwd(p["sm"], qkv, ag_ml, "attn_fwd", ride=late_weights[0])
        p = {**p, **late_weights[1](landed)}
    lru = (p["cw"], p["cb"], p["wbd_a"], p["b_a"], p["wbd_x"], p["b_x"], p["lam"])
    y_rnn, z_rnn = _rnn_fwd(rx_rg, *lru, "rnn_fwd")
    br_rnn, br_attn, merged, dout, dy, st_post = _branches_fwd(
        z_rnn, z_attn, ag_ml, p["b_gate"], p["w_rnn"], p["w_attn"], p["w_out"], x, target, p["post_g"], "branches_fwd")

    dbr_rnn, dbr_attn, d_ml, dz_rnn, dz_attn, st_merge = _branches_bwd(
        dout, br_rnn, br_attn, ag_ml, p["b_gate"], p["w_rnn"], p["w_attn"], p["w_out"], "branches_bwd")
    gw_out = _mm_tn(merged, dout, "gw_out")
    gw_rnn = _mm_tn(z_rnn, dbr_rnn, "gw_rnn")
    gw_attn = _mm_tn(z_attn, dbr_attn, "gw_attn")
    red_out = red_in = None
    if reduce_out is None:
        d_rx, d_rg, st_rnn, g_rg_a, g_rg_x = _rnn_bwd(rx_rg, y_rnn, dz_rnn, *lru, "rnn_bwd")
        dq, dkv, d_ag, st_sink = _attn_bwd(p["sm"], qkv, ag_ml, y_attn, lse, dz_attn, "attn_bwd")
    else:
        d_rx, d_rg, st_rnn, g_rg_a, g_rg_x, from_sibling = _rnn_bwd(
            rx_rg, y_rnn, dz_rnn, *lru, "rnn_bwd", ride=reduce_out[0](gw_rnn, gw_attn, gw_out))
        dq, dkv, d_ag, st_sink, red_out = _attn_bwd(p["sm"], qkv, ag_ml, y_attn, lse, dz_attn, "attn_bwd",
                                                    ride=reduce_out[1](from_sibling, g_rg_a, g_rg_x))

    segs = [d_rx, d_rg, dq, dkv, d_ag, d_ml]
    if reduce_in is None:
        gwt = _mm_tn_seg(segs, h, "gw_in")
        grad_x, st_pre = _input_grad(segs, wt, x, p["pre_g"], dy, "input_grad")
    else:
        gwt = None
        grad_x, st_pre, red_in = _input_grad(segs, wt, x, p["pre_g"], dy, "input_grad",
                                             ride=reduce_in(_mm_tn_seg_pair(segs, h, "gw_in")))
    return dict(grad_x=grad_x, gwt=gwt, gw_rnn=gw_rnn, gw_attn=gw_attn, gw_out=gw_out,
                st_post=st_post, st_merge=st_merge, st_rnn=st_rnn, st_sink=st_sink, st_pre=st_pre,
                g_rg_a=g_rg_a, g_rg_x=g_rg_x, red_out=red_out, red_in=red_in)


def _place():
    x, y, c = lax.axis_index("x"), lax.axis_index("y"), lax.axis_index("c")
    return x, y, c


def _gather_ride(shards):
    n = len(shards)

    def copies(ins, outs, sems):
        send_sems, recv_sems, local_sems = sems
        x, y, c = _place()
        me, sibling = (x, y, c), (x, y, 1 - c)
        chips = [(1 - x, y), (x, 1 - y), (1 - x, 1 - y)]

        def slot(a, dev):
            return outs[a].at[4 * dev[0] + 2 * dev[1] + dev[2]]

        def copy(a, k, block, to, src=None):
            return pltpu.make_async_remote_copy(
                src_ref=slot(a, block) if src is None else src, dst_ref=slot(a, block),
                send_sem=send_sems.at[a, k], recv_sem=recv_sems.at[a, k], device_id=to, device_id_type=MESH)

        mine = [pltpu.make_async_copy(ins[a], slot(a, me), local_sems.at[a]) for a in range(n)]
        first = []
        for a in range(n):
            first.append(copy(a, 0, me, sibling, src=ins[a]))
            first += [copy(a, 1 + j, me, (*chip, c), src=ins[a]) for j, chip in enumerate(chips)]
        return me, sibling, chips, c, copy, mine, first

    def start(ins, outs, sems):
        *_, mine, first = copies(ins, outs, sems)
        for cp in mine + first:
            cp.start()

    def middle(ins, outs, sems):
        me, sibling, chips, c, copy, _, _ = copies(ins, outs, sems)
        for j, chip in enumerate(chips):
            for a in range(n):
                copy(a, 1 + j, (*chip, c), me).wait_recv()
                copy(a, 4 + j, (*chip, c), sibling).start()

    def finish(ins, outs, sems):
        me, sibling, chips, c, copy, mine, first = copies(ins, outs, sems)
        passed = [copy(a, 4 + j, (*chip, c), sibling) for j, chip in enumerate(chips) for a in range(n)]
        for a in range(n):
            copy(a, 0, sibling, me).wait_recv()
            for j, chip in enumerate(chips):
                copy(a, 4 + j, (*chip, 1 - c), me).wait_recv()
        for cp in first + passed:
            cp.wait_send()
        for cp in mine:
            cp.wait()

    return _Ride(
        shards, [jax.ShapeDtypeStruct((N_DEV, *s.shape), s.dtype) for s in shards],
        [pltpu.SemaphoreType.DMA((n, 7)), pltpu.SemaphoreType.DMA((n, 7)), pltpu.SemaphoreType.DMA((n,))],
        start, finish, middle)


def _sibling_ride(scatter, whole):
    ns, nw = len(scatter), len(whole)

    def copies(ins, outs, sems):
        send_sems, recv_sems = sems
        x, y, c = _place()
        cps = [pltpu.make_async_remote_copy(
            src_ref=ins[a].at[2 * chip + (1 - c)], dst_ref=outs[a].at[chip],
            send_sem=send_sems.at[a * N_CHIP + chip], recv_sem=recv_sems.at[a * N_CHIP + chip],
            device_id=(x, y, 1 - c), device_id_type=MESH) for a in range(ns) for chip in range(N_CHIP)]
        cps += [pltpu.make_async_remote_copy(
            src_ref=ins[ns + a], dst_ref=outs[ns + a],
            send_sem=send_sems.at[ns * N_CHIP + a], recv_sem=recv_sems.at[ns * N_CHIP + a],
            device_id=(x, y, 1 - c), device_id_type=MESH) for a in range(nw)]
        return cps

    def start(ins, outs, sems):
        for cp in copies(ins, outs, sems):
            cp.start()

    def finish(ins, outs, sems):
        for cp in copies(ins, outs, sems):
            cp.wait()

    n_sem = ns * N_CHIP + nw
    return _Ride(
        list(scatter) + list(whole),
        [jax.ShapeDtypeStruct((N_CHIP, *s.shape[1:]), s.dtype) for s in scatter]
        + [jax.ShapeDtypeStruct(s.shape, s.dtype) for s in whole],
        [pltpu.SemaphoreType.DMA((n_sem,)), pltpu.SemaphoreType.DMA((n_sem,))], start, finish)


def _chips_ride(scatter, whole):
    ns, nw = len(scatter), len(whole)
    n = ns + nw

    def copies(ins, outs, sems):
        send_sems, recv_sems, local_sems = sems
        x, y, c = _place()
        own = 2 * x + y
        chips = [(1 - x, y), (x, 1 - y), (1 - x, 1 - y)]

        def src(a, chip_idx):
            return ins[a].at[chip_idx] if a < ns else ins[a]

        local = [pltpu.make_async_copy(src(a, own), outs[a].at[own], local_sems.at[a]) for a in range(n)]
        sent = [pltpu.make_async_remote_copy(
            src_ref=src(a, 2 * chip[0] + chip[1]), dst_ref=outs[a].at[own],
            send_sem=send_sems.at[a, j], recv_sem=recv_sems.at[a, own], device_id=(*chip, c), device_id_type=MESH)
            for a in range(n) for j, chip in enumerate(chips)]
        return chips, c, local, sent

    def start(ins, outs, sems):
        _, _, local, sent = copies(ins, outs, sems)
        for cp in local + sent:
            cp.start()

    def finish(ins, outs, sems):
        send_sems, recv_sems, _ = sems
        chips, c, local, sent = copies(ins, outs, sems)
        for a in range(n):
            for chip in chips:
                k = 2 * chip[0] + chip[1]
                pltpu.make_async_remote_copy(
                    src_ref=outs[a].at[k], dst_ref=outs[a].at[k], send_sem=send_sems.at[a, 0],
                    recv_sem=recv_sems.at[a, k], device_id=(*chip, c), device_id_type=MESH).wait_recv()
        for cp in sent:
            cp.wait_send()
        for cp in local:
            cp.wait()

    return _Ride(
        list(scatter) + list(whole),
        [jax.ShapeDtypeStruct(s.shape, s.dtype) for s in scatter]
        + [jax.ShapeDtypeStruct((N_CHIP, *s.shape), s.dtype) for s in whole],
        [pltpu.SemaphoreType.DMA((n, 3)), pltpu.SemaphoreType.DMA((n, N_CHIP)), pltpu.SemaphoreType.DMA((n,))],
        start, finish)


def _pair_sum_scatter(parts, recvs, core, name):
    na = len(parts)
    _, r, cdim = parts[0].shape
    tr = min(r, 416 if r % 416 == 0 else 128)

    def body(core_ref, *refs):
        del core_ref
        for a in range(na):
            refs[2 * na + a][...] = (refs[a][...] + refs[na + a][...]).astype(BF16)

    blk = (None, tr, cdim)
    mine = pl.BlockSpec(blk, lambda k, i, core_ref: (2 * k + core_ref[0], i, 0))
    slot = pl.BlockSpec(blk, lambda k, i, core_ref: (k, i, 0))
    return pl.pallas_call(
        body, name=name,
        grid_spec=pltpu.PrefetchScalarGridSpec(
            num_scalar_prefetch=1, grid=(N_CHIP, r // tr),
            in_specs=[mine] * na + [slot] * na, out_specs=[slot] * na),
        out_shape=[jax.ShapeDtypeStruct((N_CHIP, r, cdim), BF16)] * na,
        compiler_params=_params("parallel", "parallel"))(core, *parts, *recvs)


def _allreduce_small(pack, name):
    shape = pack.shape

    def body(x_ref, o_ref, sib_ref, chip_ref, send_sems, recv_sems):
        x, y, c = _place()
        own = 2 * x + y
        chips = [(1 - x, y), (x, 1 - y), (1 - x, 1 - y)]
        to_sibling = pltpu.make_async_remote_copy(
            src_ref=x_ref, dst_ref=sib_ref, send_sem=send_sems.at[0], recv_sem=recv_sems.at[0],
            device_id=(x, y, 1 - c), device_id_type=MESH)
        to_sibling.start()
        to_sibling.wait()
        chip_ref[own] = x_ref[...] + sib_ref[...]
        sent = [pltpu.make_async_remote_copy(
            src_ref=chip_ref.at[own], dst_ref=chip_ref.at[own], send_sem=send_sems.at[1 + j],
            recv_sem=recv_sems.at[1 + own], device_id=(*chip, c), device_id_type=MESH) for j, chip in enumerate(chips)]
        for cp in sent:
            cp.start()
        for chip in chips:
            k = 2 * chip[0] + chip[1]
            pltpu.make_async_remote_copy(
                src_ref=chip_ref.at[k], dst_ref=chip_ref.at[k], send_sem=send_sems.at[1],
                recv_sem=recv_sems.at[1 + k], device_id=(*chip, c), device_id_type=MESH).wait_recv()
        for cp in sent:
            cp.wait_send()
        o_ref[...] = (chip_ref[0] + chip_ref[1]) + (chip_ref[2] + chip_ref[3])

    return pl.pallas_call(
        body, name=name, out_shape=jax.ShapeDtypeStruct(shape, F32),
        in_specs=[pl.BlockSpec(memory_space=pltpu.VMEM)], out_specs=pl.BlockSpec(memory_space=pltpu.VMEM),
        scratch_shapes=[pltpu.VMEM(shape, F32), pltpu.VMEM((N_CHIP, *shape), F32),
                        pltpu.SemaphoreType.DMA((4,)), pltpu.SemaphoreType.DMA((1 + N_CHIP,))],
    )(pack)


def _adamw(g, w, m, v):
    m = ADAM_B1 * m + (1.0 - ADAM_B1) * g
    v = ADAM_B2 * v + (1.0 - ADAM_B2) * (g * g)
    m_hat = m / (1.0 - ADAM_B1 ** ADAM_STEP)
    v_hat = v / (1.0 - ADAM_B2 ** ADAM_STEP)
    delta = -ADAM_LR * (m_hat / (jnp.sqrt(v_hat) + ADAM_EPS) + ADAM_WD * w)
    return delta, m, v


def _adam_parts(parts, w, m, v, name, tr=None):
    npart, r, c = parts.shape
    tr = r if tr is None else min(tr, r)

    def body(p_ref, w_ref, m_ref, v_ref, g_ref, d_ref, nm_ref, nv_ref):
        g = p_ref[0].astype(F32)
        for k in range(1, npart):
            g = g + p_ref[k].astype(F32)
        g_ref[...] = g
        d_ref[...], nm_ref[...], nv_ref[...] = _adamw(g, w_ref[...], m_ref[...], v_ref[...])

    tile = pl.BlockSpec((tr, c), lambda i: (i, 0))
    return pl.pallas_call(
        body, name=name, grid=(r // tr,),
        in_specs=[pl.BlockSpec((npart, tr, c), lambda i: (0, i, 0)), tile, tile, tile],
        out_specs=[tile] * 4, out_shape=[jax.ShapeDtypeStruct((r, c), F32)] * 4,
        compiler_params=_params("parallel"))(parts, w, m, v)


def _block_diag(w):
    w4 = w.reshape(N_GROUPS, 4, RNN_BLOCK_W, RNN_BLOCK_W)
    eye = jnp.eye(4, dtype=w.dtype)
    return jnp.einsum("gbij,bc->gbicj", w4, eye).reshape(N_GROUPS, GROUP_W, GROUP_W).astype(BF16)


SMALL_ROWS = 16
ROW_PRE_G, ROW_BGATE, ROW_CONV_B, ROW_B_A, ROW_B_X, ROW_LAM, ROW_POST_G, ROW_LOSS, ROW_SINKS, ROW_CONV_W = 0, 1, 3, 4, 5, 6, 7, 8, 9, 10


def _pack_stats(st_pre, st_merge, st_rnn, st_post, st_sink, name):
    d = D_MODEL

    def body(pre_ref, mg_ref, rnn_ref, post_ref, sink_ref, o_ref):
        rnn = rnn_ref[...]
        sinks = jnp.concatenate([sink_ref[0:1, :], jnp.zeros((1, d - LANE), F32)], axis=1)
        o_ref[0:8, :] = _rows8([pre_ref[0:1, :], mg_ref[0:1, :], mg_ref[1:2, :], rnn[0:1], rnn[1:2], rnn[2:3], rnn[3:4],
                                post_ref[0:1, :]], d)
        o_ref[8:16, :] = _rows8([post_ref[1:2, :], sinks, rnn[4:5], rnn[5:6], rnn[6:7], rnn[7:8]], d)

    return pl.pallas_call(body, name=name, out_shape=jax.ShapeDtypeStruct((SMALL_ROWS, d), F32))(
        st_pre, st_merge, st_rnn, st_post, st_sink)


def _adam_small(total, w, m, v, name):
    d = D_MODEL
    n_in = len(w)

    def pack(refs):
        pre, bg, cbias, ba, bx, lam, post, sinks = [r[...] for r in refs]
        top = _rows8([pre, bg[:, 0:d], bg[:, d:2 * d], cbias, ba, bx, lam, post], d)
        return jnp.concatenate([top, _rows8([jnp.zeros((1, d), F32), sinks], d)], axis=0)

    def body(*refs):
        p_ref = refs[0]
        w_refs, m_refs, v_refs = (refs[1 + k * n_in:1 + (k + 1) * n_in] for k in range(3))
        outs = refs[1 + 3 * n_in:]
        g = p_ref[...]
        res = (g,) + _adamw(g, pack(w_refs), pack(m_refs), pack(v_refs))
        for k in range(4):
            outs[k][...] = res[k]
            outs[4 + k][...] = jnp.concatenate([res[k][ROW_BGATE:ROW_BGATE + 1], res[k][ROW_BGATE + 1:ROW_BGATE + 2]], axis=1)

    return pl.pallas_call(
        body, name=name,
        out_shape=[jax.ShapeDtypeStruct((SMALL_ROWS, d), F32)] * 4 + [jax.ShapeDtypeStruct((1, 2 * d), F32)] * 4,
    )(total, *w, *m, *v)


def kernel(x, pre_norm_g, w_in, b_gate, conv_w, conv_b, w_rg_a, b_rg_a, w_rg_x, b_rg_x, lru_lambda, attn_sinks, w_rnn_out, w_attn_out, w_out, post_norm_g, loss_target, m_pre_norm_g, m_w_in, m_b_gate, m_conv_w, m_conv_b, m_w_rg_a, m_b_rg_a, m_w_rg_x, m_b_rg_x, m_lru_lambda, m_attn_sinks, m_w_rnn_out, m_w_attn_out, m_w_out, m_post_norm_g, v_pre_norm_g, v_w_in, v_b_gate, v_conv_w, v_conv_b, v_w_rg_a, v_b_rg_a, v_w_rg_x, v_b_rg_x, v_lru_lambda, v_attn_sinks, v_w_rnn_out, v_w_attn_out, v_w_out, v_post_norm_g):
    cx, cy, cc = _place()
    dev = 4 * cx + 2 * cy + cc
    core = jnp.reshape(cc, (1,)).astype(jnp.int32)

    w_in_t, m_in_t, v_in_t = (jnp.transpose(a[0]) for a in (w_in, m_w_in, v_w_in))
    wt_shard = w_in_t.astype(BF16)

    def project(xs, pre_g):
        h, rx_rg, qkv, ag_ml, wt_all = _gather_project(xs, pre_g, wt_shard, "gather_project")
        return h, rx_rg, qkv, ag_ml, wt_all.reshape(D_IN, D_MODEL)

    def late_unpack(landed):
        w_rnn_all, w_attn_all, w_out_all, cw_all = landed
        return dict(w_rnn=w_rnn_all.reshape(D_RNN, D_MODEL), w_attn=w_attn_all.reshape(D_MODEL, D_MODEL),
                    w_out=w_out_all.reshape(D_MODEL, D_MODEL), cw=jnp.transpose(cw_all, (1, 0, 2)).reshape(4, D_RNN))

    late_weights = (_gather_ride([w_rnn_out[0].astype(BF16), w_attn_out[0].astype(BF16), w_out[0].astype(BF16), conv_w[0]]),
                    late_unpack)

    heads = jnp.arange(1, N_Q_HEADS + 1, dtype=F32)
    slopes = jnp.exp2(-ALIBI_MAX_BIAS * heads / N_Q_HEADS)
    b_a = b_rg_a.reshape(1, D_RNN)
    b_x = b_rg_x.reshape(1, D_RNN)
    p = dict(
        pre_g=pre_norm_g, post_g=post_norm_g, b_gate=b_gate, cb=conv_b,
        wbd_a=_block_diag(w_rg_a[0]), b_a=b_a, wbd_x=_block_diag(w_rg_x[0]), b_x=b_x, lam=lru_lambda,
        sm=jnp.pad(attn_sinks, ((0, 1), (0, 0))) + jnp.pad(slopes[None, :], ((1, 0), (0, 0))))

    flat = (RNN_BLOCKS * RNN_BLOCK_W, RNN_BLOCK_W)

    out_scatter = []

    def out_sibling(gw_rnn, gw_attn, gw_out):
        out_scatter.extend(gw.reshape(N_DEV, SHARD_OUT, D_MODEL) for gw in (gw_rnn, gw_attn, gw_out))
        return _sibling_ride(out_scatter, [])

    def out_chips(from_sibling, g_rg_a, g_rg_x):
        return _join_rides(_chips_ride(_pair_sum_scatter(out_scatter, from_sibling, core, "pair_out"), []),
                           _gather_ride([g_rg_a.reshape(flat), g_rg_x.reshape(flat)]))

    def reduce_in(pair_sums):
        return _chips_ride([pair_sums], [])

    g = _local_grads(x[0], loss_target[0], p, project, late_weights, (out_sibling, out_chips), reduce_in)
    small = _allreduce_small(
        _pack_stats(g["st_pre"], g["st_merge"], g["st_rnn"], g["st_post"], g["st_sink"], "pack_stats"), "allreduce_small")

    out = {}
    red = g["red_out"]
    out["w_in"] = [jnp.transpose(o) for o in _adam_parts(g["red_in"][0], w_in_t, m_in_t, v_in_t, "adam_w_in", tr=SHARD_IN // 2)]
    out["w_rnn_out"] = _adam_parts(red[0], w_rnn_out[0], m_w_rnn_out[0], v_w_rnn_out[0], "adam_w_rnn_out")
    out["w_attn_out"] = _adam_parts(red[1], w_attn_out[0], m_w_attn_out[0], v_w_attn_out[0], "adam_w_attn_out")
    out["w_out"] = _adam_parts(red[2], w_out[0], m_w_out[0], v_w_out[0], "adam_w_out")
    out["w_rg_a"] = _adam_parts(red[3], w_rg_a.reshape(flat), m_w_rg_a.reshape(flat), v_w_rg_a.reshape(flat), "adam_w_rg_a", tr=256)
    out["w_rg_x"] = _adam_parts(red[4], w_rg_x.reshape(flat), m_w_rg_x.reshape(flat), v_w_rg_x.reshape(flat), "adam_w_rg_x", tr=256)

    def rows(pre, bg, cbias, ba, bx, lam, post, sinks):
        return (pre, bg, cbias, ba.reshape(1, D_RNN), bx.reshape(1, D_RNN), lam, post,
                jnp.pad(sinks, ((0, 0), (0, D_MODEL - N_Q_HEADS))))

    small_out = _adam_small(
        small,
        rows(pre_norm_g, b_gate, conv_b, b_rg_a, b_rg_x, lru_lambda, post_norm_g, attn_sinks),
        rows(m_pre_norm_g, m_b_gate, m_conv_b, m_b_rg_a, m_b_rg_x, m_lru_lambda, m_post_norm_g, m_attn_sinks),
        rows(v_pre_norm_g, v_b_gate, v_conv_b, v_b_rg_a, v_b_rg_x, v_lru_lambda, v_post_norm_g, v_attn_sinks),
        "adam_small")
    packed, bgate_out = small_out[:4], small_out[4:]
    g_cw = lax.dynamic_slice(packed[0][ROW_CONV_W:ROW_CONV_W + 4], (0, dev * SHARD_OUT), (4, SHARD_OUT))
    out["conv_w"] = _adam_parts(g_cw[None], conv_w[0], m_conv_w[0], v_conv_w[0], "adam_conv_w")

    def unpack(kind, name):
        if name == "b_gate":
            return bgate_out[kind]
        row = dict(pre_norm_g=ROW_PRE_G, conv_b=ROW_CONV_B, b_rg_a=ROW_B_A, b_rg_x=ROW_B_X, lru_lambda=ROW_LAM,
                   post_norm_g=ROW_POST_G, attn_sinks=ROW_SINKS)[name]
        r = packed[kind][row:row + 1]
        if name == "attn_sinks":
            return r[:, 0:N_Q_HEADS]
        if name in ("b_rg_a", "b_rg_x"):
            return r.reshape(1, RNN_BLOCKS, RNN_BLOCK_W)
        return r

    shapes = dict(w_in=(1, D_MODEL, SHARD_IN), w_rnn_out=(1, SHARD_OUT, D_MODEL), w_attn_out=(1, SHARD_OUT, D_MODEL),
                  w_out=(1, SHARD_OUT, D_MODEL), w_rg_a=(1, RNN_BLOCKS, RNN_BLOCK_W, RNN_BLOCK_W),
                  w_rg_x=(1, RNN_BLOCKS, RNN_BLOCK_W, RNN_BLOCK_W), conv_w=(1, 4, SHARD_OUT))
    weights = ["pre_norm_g", "w_in", "b_gate", "conv_w", "conv_b", "w_rg_a", "b_rg_a", "w_rg_x", "b_rg_x",
               "lru_lambda", "attn_sinks", "w_rnn_out", "w_attn_out", "w_out", "post_norm_g"]
    results = []
    for kind in range(4):
        for name in weights:
            if name in out:
                results.append(out[name][kind].reshape(shapes[name]))
            else:
                results.append(unpack(kind, name))
    loss = 0.5 / D_MODEL * jnp.sum(packed[0][ROW_LOSS])
    return (loss, g["grad_x"][None], *results)
```

```python
import functools

import jax
import jax.numpy as jnp
from jax import lax
from jax.experimental import pallas as pl
from jax.experimental.pallas import tpu as pltpu

F32 = jnp.float32
BF16 = jnp.bfloat16

D_MODEL = 1024
D_RNN = 1024
RNN_BLOCKS = 16
RNN_BLOCK_W = 64
LRU_C = 8.0
N_Q_HEADS = 16
HEAD_DIM = 64
D_KV = 256
BLOCK = 128
ALIBI_MAX_BIAS = 8.0
EPS = 1e-6
D_IN = 6656
N_DEV = 8
N_CHIP = 4
SHARD_IN = D_IN // N_DEV
SHARD_OUT = D_MODEL // N_DEV
ATTN_SCALE = HEAD_DIM ** -0.5
MASKED = -1e30

ADAM_LR = 0.001
ADAM_B1 = 0.9
ADAM_B2 = 0.999
ADAM_EPS = 1e-08
ADAM_WD = 0.01
ADAM_STEP = 10

VMEM_LIMIT_BYTES = 52 * 1024 * 1024
LANE = 128
GROUP_W = 256
N_GROUPS = D_RNN // GROUP_W
SEG_CHUNK = 512

NT_DIMS = (((1,), (1,)), ((), ()))
TN_DIMS = (((0,), (0,)), ((), ()))
MESH = pl.DeviceIdType.MESH
ANY = pl.BlockSpec(memory_space=pl.ANY)


def _params(*semantics):
    return pltpu.CompilerParams(dimension_semantics=semantics, vmem_limit_bytes=VMEM_LIMIT_BYTES)


def _sigmoid(x):
    return 0.5 * jnp.tanh(0.5 * x) + 0.5


def _log1p(e):
    u = 1.0 + e
    den = jnp.where(u == 1.0, 1.0, u - 1.0)
    return jnp.where(u == 1.0, e, jnp.log(u) * (e / den))


def _softplus(z):
    return jnp.maximum(z, 0.0) + _log1p(jnp.exp(-jnp.abs(z)))


def _rows8(rows, width):
    idx = lax.broadcasted_iota(jnp.int32, (8, width), 0)
    out = jnp.zeros((8, width), F32)
    for r, v in enumerate(rows):
        out = jnp.where(idx == r, v, out)
    return out


class _Ride:
    def __init__(self, arrays, out_shapes, scratch_shapes, start, finish, middle=None, aliases=None):
        self.arrays, self.out_shapes, self.scratch_shapes = list(arrays), list(out_shapes), list(scratch_shapes)
        self.start, self.finish, self.middle = start, finish, middle
        self.aliases = dict(aliases or {})


class _Hosted:
    def __init__(self, ride, n_in, n_out, n_scratch=0, aliasing=False):
        self.ride = ride
        assert aliasing or not (ride and ride.aliases), "this host does not alias"
        self.aliases = {n_in + i: n_out + o for i, o in ride.aliases.items()} if ride else {}
        self.sizes = (n_in, len(ride.arrays) if ride else 0, n_out, len(ride.out_shapes) if ride else 0, n_scratch)
        self.arrays = ride.arrays if ride else []
        self.in_specs = [ANY] * len(self.arrays)
        self.out_shapes = ride.out_shapes if ride else []
        self.out_specs = [ANY] * len(self.out_shapes)
        self.scratch_shapes = ride.scratch_shapes if ride else []

    def split(self, refs):
        n_in, r_in, n_out, r_out, n_scr = self.sizes
        cuts = [0, n_in, n_in + r_in, n_in + r_in + n_out, n_in + r_in + n_out + r_out, n_in + r_in + n_out + r_out + n_scr]
        host_in, ride_in, host_out, ride_out, host_scr = (refs[cuts[k]:cuts[k + 1]] for k in range(5))
        ride_scr = refs[cuts[5]:]

        def start(when):
            if self.ride is not None:
                pl.when(when)(lambda: self.ride.start(ride_in, ride_out, ride_scr))

        def finish(when):
            if self.ride is not None:
                pl.when(when)(lambda: self.ride.finish(ride_in, ride_out, ride_scr))

        def middle(when):
            if self.ride is not None and self.ride.middle is not None:
                pl.when(when)(lambda: self.ride.middle(ride_in, ride_out, ride_scr))

        start.middle = middle
        return tuple(host_in) + tuple(host_out) + tuple(host_scr), start, finish

    def results(self, outs, n_out):
        outs = list(outs) if isinstance(outs, (list, tuple)) else [outs]
        return outs[:n_out], outs[n_out:]


def _join_rides(a, b):
    na, nb_ = len(a.arrays), len(b.arrays)
    oa = len(a.out_shapes)
    sa = len(a.scratch_shapes)

    def both(fa, fb):
        def run(ins, outs, sems):
            if fa is not None:
                fa(ins[:na], outs[:oa], sems[:sa])
            if fb is not None:
                fb(ins[na:na + nb_], outs[oa:], sems[sa:])
        return run

    middle = both(a.middle, b.middle) if (a.middle or b.middle) else None
    return _Ride(a.arrays + b.arrays, a.out_shapes + b.out_shapes, a.scratch_shapes + b.scratch_shapes,
                 both(a.start, b.start), both(a.finish, b.finish), middle)


def _load_resident(w_hbm, w_vmem, sems, first):
    def piece(c):
        rows = pl.ds(c * SEG_CHUNK, SEG_CHUNK)
        return pltpu.make_async_copy(w_hbm.at[rows], w_vmem.at[rows], sems.at[c])

    @pl.when(first)
    def _():
        for c in range(w_vmem.shape[0] // SEG_CHUNK):
            piece(c).start()

    def ready(c):
        @pl.when(first)
        def _():
            piece(c).wait()

    return ready


CHIP_ROWS = 2 * SHARD_IN
PROJ_WIDTHS = (2 * D_RNN, D_MODEL + 2 * D_KV, 3 * D_MODEL)
PROJ_DTYPES = (F32, BF16, BF16)


def _chip_pieces():
    starts = [0, PROJ_WIDTHS[0], PROJ_WIDTHS[0] + PROJ_WIDTHS[1], D_IN]
    pieces = []
    for k in range(N_CHIP):
        lo, hi = k * CHIP_ROWS, (k + 1) * CHIP_ROWS
        cur = []
        for a in range(len(PROJ_WIDTHS)):
            s0, s1 = max(lo, starts[a]), min(hi, starts[a + 1])
            if s0 < s1:
                cur.append((a, s0 - starts[a], s1 - s0, s0 - lo))
        pieces.append(cur)
    return pieces


def _gather_project(x, g, wt_shard, name, tm=512):
    t, k = x.shape
    tm = min(tm, t)
    nt = t // tm
    pieces = _chip_pieces()

    def body(x_ref, g_ref, shard_ref, h_out, rx_ref, qkv_ref, ag_ref, wt_all,
             w_c, stage, o32, o16, h_all, send_sems, recv_sems, local_sem, stage_sems, out_sems, h_sems):
        s, ti = pl.program_id(0), pl.program_id(1)
        px, py, pc = _place()
        me, sibling = (px, py, pc), (px, py, 1 - pc)
        chips = [(px, py), (1 - px, py), (px, 1 - py), (1 - px, 1 - py)]
        outs = (rx_ref, qkv_ref, ag_ref)

        def slot(dev):
            return wt_all.at[4 * dev[0] + 2 * dev[1] + dev[2]]

        def copy(kk, block, to, src=None):
            return pltpu.make_async_remote_copy(
                src_ref=slot(block) if src is None else src, dst_ref=slot(block),
                send_sem=send_sems.at[kk], recv_sem=recv_sems.at[kk], device_id=to, device_id_type=MESH)

        mine = pltpu.make_async_copy(shard_ref, slot(me), local_sem)
        first = [copy(0, me, sibling, src=shard_ref)] + [copy(1 + j, me, (*chips[1 + j], pc), src=shard_ref) for j in range(3)]
        passed = [copy(4 + j, (*chips[1 + j], pc), sibling) for j in range(3)]

        @pl.when((s == 0) & (ti == 0))
        def _():
            mine.start()
            for cp in first:
                cp.start()

        for step in range(N_CHIP):
            @pl.when((s == step) & (ti == 0))
            def _(step=step):
                chip = chips[step]
                if step == 0:
                    mine.wait()
                    copy(0, sibling, me).wait_recv()
                else:
                    copy(step, (*chip, pc), me).wait_recv()
                    passed[step - 1].start()
                    copy(3 + step, (*chip, 1 - pc), me).wait_recv()
                loads = [pltpu.make_async_copy(slot((*chip, core)), stage.at[pl.ds(core * SHARD_IN, SHARD_IN)], stage_sems.at[core])
                         for core in (0, 1)]
                for cp in loads:
                    cp.start()
                for cp in loads:
                    cp.wait()
                w_c[...] = stage[...].T

        rows = pl.ds(pl.multiple_of(ti * tm, tm), tm)

        @pl.when(s == 0)
        def _():
            xv = x_ref[...]
            h_all[rows, :] = (xv * lax.rsqrt(jnp.mean(xv * xv, axis=-1, keepdims=True) + EPS) * g_ref[...]).astype(BF16)

        res = jnp.dot(h_all[rows, :], w_c[...], preferred_element_type=F32)

        n = s * nt + ti
        buf = lax.rem(n, 2)
        chip_idx = [2 * cx + cy for cx, cy in chips]

        def chip_at(step):
            return jnp.where(step == 0, chip_idx[0], jnp.where(step == 1, chip_idx[1], jnp.where(step == 2, chip_idx[2], chip_idx[3])))

        def h_write(b, tile):
            return pltpu.make_async_copy(h_all.at[pl.ds(tile * tm, tm)], h_out.at[pl.ds(tile * tm, tm)], h_sems.at[b])

        def writes(kchip, b, tile):
            cps = []
            for idx, (a, col, w, src) in enumerate(pieces[kchip]):
                staged = (o16 if PROJ_DTYPES[a] == BF16 else o32).at[b, :, pl.ds(src, w)]
                cps.append(pltpu.make_async_copy(staged, outs[a].at[pl.ds(tile * tm, tm), pl.ds(col, w)], out_sems.at[b, idx]))
            return cps

        o32[buf] = res
        o16[buf] = res.astype(BF16)
        kcur = chip_at(s)

        @pl.when(n > 0)
        def _():
            kprev = chip_at(lax.div(n - 1, nt))
            for kchip in range(N_CHIP):
                @pl.when(kprev == kchip)
                def _(kchip=kchip):
                    for cp in writes(kchip, 1 - buf, lax.rem(n - 1, nt)):
                        cp.wait()

            @pl.when(n <= nt)
            def _():
                h_write(1 - buf, n - 1).wait()

        @pl.when(s == 0)
        def _():
            h_write(buf, ti).start()

        for kchip in range(N_CHIP):
            @pl.when(kcur == kchip)
            def _(kchip=kchip):
                for cp in writes(kchip, buf, ti):
                    cp.start()

        @pl.when(n == N_CHIP * nt - 1)
        def _():
            for kchip in range(N_CHIP):
                @pl.when(kcur == kchip)
                def _(kchip=kchip):
                    for cp in writes(kchip, buf, ti):
                        cp.wait()
            for cp in first + passed:
                cp.wait_send()

    tile = pl.BlockSpec((tm, k), lambda s, ti: (jnp.where(s == 0, ti, nt - 1), 0))
    return pl.pallas_call(
        body, name=name, grid=(N_CHIP, nt),
        in_specs=[tile, pl.BlockSpec((1, k), lambda s, ti: (0, 0)), ANY],
        out_specs=[ANY, ANY, ANY, ANY, ANY],
        out_shape=[jax.ShapeDtypeStruct((t, k), BF16)]
        + [jax.ShapeDtypeStruct((t, w), dt) for w, dt in zip(PROJ_WIDTHS, PROJ_DTYPES)]
        + [jax.ShapeDtypeStruct((N_DEV, SHARD_IN, k), BF16)],
        scratch_shapes=[pltpu.VMEM((k, CHIP_ROWS), BF16), pltpu.VMEM((CHIP_ROWS, k), BF16),
                        pltpu.VMEM((2, tm, CHIP_ROWS), F32), pltpu.VMEM((2, tm, CHIP_ROWS), BF16), pltpu.VMEM((t, k), BF16),
                        pltpu.SemaphoreType.DMA((7,)), pltpu.SemaphoreType.DMA((7,)), pltpu.SemaphoreType.DMA,
                        pltpu.SemaphoreType.DMA((2,)), pltpu.SemaphoreType.DMA((2, 2)), pltpu.SemaphoreType.DMA((2,))],
        compiler_params=_params("arbitrary", "arbitrary"))(x, g, wt_shard)


def _mm_tn(a, b, name, tm=512, tk=4096):
    ktok, m = a.shape
    n = b.shape[1]
    tk = min(tk, ktok)

    def body(a_ref, b_ref, o_ref):
        @pl.when(pl.program_id(1) == 0)
        def _():
            o_ref[...] = jnp.zeros_like(o_ref)

        o_ref[...] += lax.dot_general(a_ref[...], b_ref[...], TN_DIMS, preferred_element_type=F32)

    return pl.pallas_call(
        body, name=name, grid=(m // tm, ktok // tk),
        in_specs=[pl.BlockSpec((tk, tm), lambda i, kk: (kk, i)), pl.BlockSpec((tk, n), lambda i, kk: (kk, 0))],
        out_specs=pl.BlockSpec((tm, n), lambda i, kk: (i, 0)),
        out_shape=jax.ShapeDtypeStruct((m, n), F32),
        compiler_params=_params("parallel", "arbitrary"))(a, b)


def _segment_chunks(segs):
    bounds = [0]
    for s in segs:
        bounds.append(bounds[-1] + s.shape[1] // SEG_CHUNK)
    return bounds


def _input_grad(segs, wt, x, g, dy, name, tm=512, ride=None):
    m = segs[0].shape[0]
    rows, n = wt.shape
    tm = min(tm, m)
    bounds = _segment_chunks(segs)
    n_seg = len(segs)
    ni = m // tm
    host = _Hosted(ride, n_seg + 4, 2, 2, aliasing=True)

    def body(*refs):
        host_refs, start, finish = host.split(refs)
        a_refs = host_refs[:n_seg]
        wt_hbm, x_ref, g_ref, dy_ref, gx_ref, st_ref, wt_vmem, sems = host_refs[n_seg:]
        i = pl.program_id(0)
        start(i == 0)

        @pl.when(i == 0)
        def _():
            st_ref[...] = jnp.zeros_like(st_ref)

        ready = _load_resident(wt_hbm, wt_vmem, sems, i == 0)
        dh = None
        for s in range(n_seg):
            for c in range(bounds[s], bounds[s + 1]):
                ready(c)
            part = jnp.dot(a_refs[s][...], wt_vmem[bounds[s] * SEG_CHUNK:bounds[s + 1] * SEG_CHUNK, :], preferred_element_type=F32)
            dh = part if dh is None else dh + part
        xv = x_ref[...]
        r = lax.rsqrt(jnp.mean(xv * xv, axis=-1, keepdims=True) + EPS)
        xn = xv * r
        dxn = dh * g_ref[...]
        gx_ref[...] = dy_ref[...] + r * (dxn - xn * jnp.mean(dxn * xn, axis=-1, keepdims=True))
        st_ref[...] += _rows8([jnp.sum(dh * xn, axis=0, keepdims=True)], n)
        finish(i == ni - 1)

    tile = pl.BlockSpec((tm, n), lambda i: (i, 0))
    outs = pl.pallas_call(
        body, name=name, grid=(ni,),
        in_specs=[pl.BlockSpec((tm, sg.shape[1]), lambda i: (i, 0)) for sg in segs]
        + [ANY, tile, pl.BlockSpec((1, n), lambda i: (0, 0)), tile] + host.in_specs,
        out_specs=[tile, pl.BlockSpec((8, n), lambda i: (0, 0))] + host.out_specs,
        out_shape=[jax.ShapeDtypeStruct((m, n), F32), jax.ShapeDtypeStruct((8, n), F32)] + host.out_shapes,
        scratch_shapes=[pltpu.VMEM((rows, n), wt.dtype), pltpu.SemaphoreType.DMA((rows // SEG_CHUNK,))] + host.scratch_shapes,
        input_output_aliases=host.aliases,
        compiler_params=_params("arbitrary"))(*segs, wt, x, g, dy, *host.arrays)
    res, landed = host.results(outs, 2)
    return (*res, landed) if ride else tuple(res)


def _mm_tn_seg(segs, b, name):
    ktok = segs[0].shape[0]
    n = b.shape[1]
    bounds = _segment_chunks(segs)
    n_seg = len(segs)
    nc = bounds[-1]
    seg_of = [s for s in range(n_seg) for _ in range(bounds[s], bounds[s + 1])]

    def body(*refs):
        a_hbm, b_hbm, o_ref = refs[:n_seg], refs[n_seg], refs[n_seg + 1]
        a_buf, b_vmem, a_sems, b_sem = refs[n_seg + 2:]
        c = pl.program_id(0)

        def fetch(cc):
            s = seg_of[cc]
            cols = pl.ds((cc - bounds[s]) * SEG_CHUNK, SEG_CHUNK)
            return pltpu.make_async_copy(a_hbm[s].at[:, cols], a_buf.at[cc % 2], a_sems.at[cc % 2])

        @pl.when(c == 0)
        def _():
            whole = pltpu.make_async_copy(b_hbm, b_vmem, b_sem)
            whole.start()
            fetch(0).start()
            whole.wait()

        for cc in range(nc):
            @pl.when(c == cc)
            def _(cc=cc):
                if cc + 1 < nc:
                    fetch(cc + 1).start()
                fetch(cc).wait()

        o_ref[...] = lax.dot_general(a_buf[c % 2], b_vmem[...], TN_DIMS, preferred_element_type=F32)

    return pl.pallas_call(
        body, name=name, grid=(nc,),
        in_specs=[ANY] * (n_seg + 1), out_specs=pl.BlockSpec((SEG_CHUNK, n), lambda c: (c, 0)),
        out_shape=jax.ShapeDtypeStruct((nc * SEG_CHUNK, n), F32),
        scratch_shapes=[pltpu.VMEM((2, ktok, SEG_CHUNK), segs[0].dtype), pltpu.VMEM((ktok, n), b.dtype),
                        pltpu.SemaphoreType.DMA((2,)), pltpu.SemaphoreType.DMA],
        compiler_params=_params("arbitrary"))(*segs, b)


CHUNK_ORDER = (0, 4, 7, 10, 1, 5, 8, 11, 2, 6, 9, 12, 3)
EARLY_STEPS = 8


def _side_pieces():
    table = []
    for core in (0, 1):
        sides = ([None] * len(CHUNK_ORDER), [None] * len(CHUNK_ORDER))
        for s, cc in enumerate(CHUNK_ORDER):
            g0 = cc * SEG_CHUNK
            for d in range(N_DEV):
                lo, hi = max(g0, d * SHARD_IN), min(g0 + SEG_CHUNK, (d + 1) * SHARD_IN)
                if lo < hi:
                    side = sides[0 if d % 2 == core else 1]
                    assert side[s] is None
                    side[s] = (lo - g0, hi - lo, (d // 2) * SHARD_IN + lo - d * SHARD_IN)
        table.append(sides)
    return table


def _to_chip(src, red_ref, piece, s, send_sems, recv_sems, own, core):
    _, rows, at = piece
    k, r0 = divmod(at, SHARD_IN)
    return k, pltpu.make_async_remote_copy(
        src_ref=src, dst_ref=red_ref.at[own, pl.ds(r0, rows)], send_sem=send_sems.at[s], recv_sem=recv_sems.at[own, s],
        device_id=(k // 2, k % 2, core), device_id_type=MESH)


def _from_chip(src, red_ref, piece, s, send_sems, recv_sems, chip, me):
    _, rows, at = piece
    return pltpu.make_async_remote_copy(
        src_ref=src, dst_ref=red_ref.at[chip, pl.ds(at % SHARD_IN, rows)], send_sem=send_sems.at[s], recv_sem=recv_sems.at[chip, s],
        device_id=me, device_id_type=MESH)


def _mm_tn_seg_pair(segs, b, name):
    ktok = segs[0].shape[0]
    n = b.shape[1]
    bounds = _segment_chunks(segs)
    n_seg = len(segs)
    nc = bounds[-1]
    assert nc == len(CHUNK_ORDER)
    seg_of = [s for s in range(n_seg) for _ in range(bounds[s], bounds[s + 1])]
    table = _side_pieces()

    def body(*refs):
        a_hbm, b_hbm, pair_ref, red_ref = refs[:n_seg], refs[n_seg], refs[n_seg + 1], refs[n_seg + 2]
        (a_buf, b_vmem, res_buf, recv_all, pair_all, a_sems, b_sem, send_sems, recv_sems, out_sems,
         chip_send, chip_recv) = refs[n_seg + 3:]
        step = pl.program_id(0)
        px, py, pc = _place()
        own = 2 * px + py

        def fetch(s):
            sg = seg_of[CHUNK_ORDER[s]]
            cols = pl.ds((CHUNK_ORDER[s] - bounds[sg]) * SEG_CHUNK, SEG_CHUNK)
            return pltpu.make_async_copy(a_hbm[sg].at[:, cols], a_buf.at[s % 2], a_sems.at[s % 2])

        @pl.when(step == 0)
        def _():
            whole = pltpu.make_async_copy(b_hbm, b_vmem, b_sem)
            whole.start()
            fetch(0).start()
            whole.wait()

        for s in range(nc):
            @pl.when(step == s)
            def _(s=s):
                if s + 1 < nc:
                    fetch(s + 1).start()
                fetch(s).wait()

        res_buf[step % 2] = lax.dot_general(a_buf[step % 2], b_vmem[...], TN_DIMS, preferred_element_type=F32)

        def crossing(s, piece):
            off, rows, at = piece
            return pltpu.make_async_remote_copy(
                src_ref=res_buf.at[s % 2, pl.ds(off, rows)], dst_ref=recv_all.at[pl.ds(at, rows)],
                send_sem=send_sems.at[s], recv_sem=recv_sems.at[s], device_id=(px, py, 1 - pc), device_id_type=MESH)

        def write(s, piece):
            _, rows, at = piece
            return pltpu.make_async_copy(pair_all.at[pl.ds(at, rows)], pair_ref.at[pl.ds(at, rows)], out_sems.at[s])

        def to_chip(s, piece, core):
            return _to_chip(pair_all.at[pl.ds(piece[2], piece[1])], red_ref, piece, s, chip_send, chip_recv, own, core)

        for core in (0, 1):
            mine, theirs = table[core]

            def settle(s, mine=mine, theirs=theirs, core=core):
                if theirs[s] is not None:
                    crossing(s, theirs[s]).wait_send()
                if mine[s] is not None:
                    off, rows, at = mine[s]
                    crossing(s, mine[s]).wait_recv()
                    pair_all[at:at + rows] = (res_buf[s % 2, off:off + rows] + recv_all[at:at + rows]).astype(BF16)
                    write(s, mine[s]).start()
                    if s < EARLY_STEPS:
                        k, cp = to_chip(s, mine[s], core)
                        pl.when(own != k)(cp.start)

            for s in range(nc):
                @pl.when((pc == core) & (step == s))
                def _(s=s, settle=settle, mine=mine, theirs=theirs, core=core):
                    if theirs[s] is not None:
                        crossing(s, theirs[s]).start()
                    if s > 0:
                        settle(s - 1)
                    if s == nc - 1:
                        settle(s)
                        kept = [t for t in range(nc) if mine[t] is not None]
                        for t in kept:
                            write(t, mine[t]).wait()
                        early = [t for t in kept if t < EARLY_STEPS]
                        for t in early:
                            k, cp = to_chip(t, mine[t], core)
                            pl.when(own != k)(cp.wait_send)
                        for k in range(N_CHIP):
                            @pl.when(own == k)
                            def _(k=k):
                                for t in early:
                                    if mine[t][2] // SHARD_IN == k:
                                        for chip in range(N_CHIP):
                                            if chip != k:
                                                _from_chip(pair_all.at[pl.ds(mine[t][2], mine[t][1])], red_ref, mine[t], t,
                                                           chip_send, chip_recv, chip, (px, py, pc)).wait_recv()

    flat = jax.ShapeDtypeStruct((N_CHIP * SHARD_IN, n), BF16)
    pair, red = pl.pallas_call(
        body, name=name, grid=(nc,),
        in_specs=[ANY] * (n_seg + 1), out_specs=[ANY, ANY],
        out_shape=[flat, jax.ShapeDtypeStruct((N_CHIP, SHARD_IN, n), BF16)],
        scratch_shapes=[pltpu.VMEM((2, ktok, SEG_CHUNK), segs[0].dtype), pltpu.VMEM((ktok, n), b.dtype),
                        pltpu.VMEM((2, SEG_CHUNK, n), F32), pltpu.VMEM(flat.shape, F32), pltpu.VMEM(flat.shape, BF16),
                        pltpu.SemaphoreType.DMA((2,)), pltpu.SemaphoreType.DMA,
                        pltpu.SemaphoreType.DMA((nc,)), pltpu.SemaphoreType.DMA((nc,)), pltpu.SemaphoreType.DMA((nc,)),
                        pltpu.SemaphoreType.DMA((nc,)), pltpu.SemaphoreType.DMA((N_CHIP, nc))],
        compiler_params=_params("arbitrary"))(*segs, b)
    return pair.reshape(N_CHIP, SHARD_IN, n), red


def _branches_fwd(z_rnn, z_attn, ag_ml, b_gate, w_rnn, w_attn, w_out, x, target, g_post, name, tm=512):
    t, d = x.shape
    tm = min(tm, t)

    def body(zr_ref, za_ref, lr_ref, la_ref, br_ref, ba_ref, wr_ref, wa_ref, wo_ref, x_ref, t_ref, g_ref,
             brr_ref, bra_ref, mg_ref, do_ref, dy_ref, st_ref):
        @pl.when(pl.program_id(0) == 0)
        def _():
            st_ref[...] = jnp.zeros_like(st_ref)

        br_rnn = jnp.dot(zr_ref[...], wr_ref[...], preferred_element_type=F32)
        br_attn = jnp.dot(za_ref[...], wa_ref[...], preferred_element_type=F32)
        brr_ref[...] = br_rnn.astype(BF16)
        bra_ref[...] = br_attn.astype(BF16)
        g_rnn = _sigmoid(lr_ref[...].astype(F32) + br_ref[...])
        g_attn = _sigmoid(la_ref[...].astype(F32) + ba_ref[...])
        merged = (g_rnn * br_rnn + g_attn * br_attn).astype(BF16)
        mg_ref[...] = merged
        o = jnp.dot(merged, wo_ref[...], preferred_element_type=F32)
        g = g_ref[...]
        r = lax.rsqrt(jnp.mean(o * o, axis=-1, keepdims=True) + EPS)
        nrm = o * r
        err = x_ref[...] + nrm * g - t_ref[...]
        dy = err * (1.0 / d)
        dy_ref[...] = dy
        dn = dy * g
        do_ref[...] = (r * (dn - nrm * jnp.mean(dn * nrm, axis=-1, keepdims=True))).astype(BF16)
        st_ref[...] += _rows8([jnp.sum(dy * nrm, axis=0, keepdims=True), jnp.sum(err * err, axis=0, keepdims=True)], d)

    tile = pl.BlockSpec((tm, d), lambda i: (i, 0))
    weight = pl.BlockSpec((d, d), lambda i: (0, 0))
    bf = jax.ShapeDtypeStruct((t, d), BF16)
    return pl.pallas_call(
        body, name=name, grid=(t // tm,),
        in_specs=[tile, tile, pl.BlockSpec((tm, d), lambda i: (i, 1)), pl.BlockSpec((tm, d), lambda i: (i, 2)),
                  pl.BlockSpec((1, d), lambda i: (0, 0)), pl.BlockSpec((1, d), lambda i: (0, 1)),
                  weight, weight, weight, tile, tile, pl.BlockSpec((1, d), lambda i: (0, 0))],
        out_specs=[tile, tile, tile, tile, tile, pl.BlockSpec((8, d), lambda i: (0, 0))],
        out_shape=[bf, bf, bf, bf, jax.ShapeDtypeStruct((t, d), F32), jax.ShapeDtypeStruct((8, d), F32)],
        compiler_params=_params("arbitrary"))(z_rnn, z_attn, ag_ml, ag_ml, b_gate, b_gate, w_rnn, w_attn, w_out, x, target, g_post)


def _branches_bwd(dout, br_rnn, br_attn, ag_ml, b_gate, w_rnn, w_attn, w_out, name, tm=512):
    t, d = br_rnn.shape
    tm = min(tm, t)

    def body(do_ref, r_ref, a_ref, lr_ref, la_ref, br_ref, ba_ref, wr_ref, wa_ref, wo_ref,
             dr_ref, da_ref, dl_ref, dzr_ref, dza_ref, st_ref, wt_ref):
        @pl.when(pl.program_id(0) == 0)
        def _():
            st_ref[...] = jnp.zeros_like(st_ref)
            wt_ref[0] = wo_ref[...].T
            wt_ref[1] = wr_ref[...].T
            wt_ref[2] = wa_ref[...].T

        dm = jnp.dot(do_ref[...], wt_ref[0], preferred_element_type=F32)
        g_rnn = _sigmoid(lr_ref[...].astype(F32) + br_ref[...])
        g_attn = _sigmoid(la_ref[...].astype(F32) + ba_ref[...])
        dbr_rnn = (dm * g_rnn).astype(BF16)
        dbr_attn = (dm * g_attn).astype(BF16)
        dr_ref[...] = dbr_rnn
        da_ref[...] = dbr_attn
        dl_rnn = dm * r_ref[...].astype(F32) * g_rnn * (1.0 - g_rnn)
        dl_attn = dm * a_ref[...].astype(F32) * g_attn * (1.0 - g_attn)
        dl_ref[:, 0:d] = dl_rnn.astype(BF16)
        dl_ref[:, d:2 * d] = dl_attn.astype(BF16)
        st_ref[...] += _rows8([jnp.sum(dl_rnn, axis=0, keepdims=True), jnp.sum(dl_attn, axis=0, keepdims=True)], d)
        dzr_ref[...] = jnp.dot(dbr_rnn, wt_ref[1], preferred_element_type=F32).astype(BF16)
        dza_ref[...] = jnp.dot(dbr_attn, wt_ref[2], preferred_element_type=F32).astype(BF16)

    tile = pl.BlockSpec((tm, d), lambda i: (i, 0))
    weight = pl.BlockSpec((d, d), lambda i: (0, 0))
    bf = jax.ShapeDtypeStruct((t, d), BF16)
    return pl.pallas_call(
        body, name=name, grid=(t // tm,),
        in_specs=[tile, tile, tile, pl.BlockSpec((tm, d), lambda i: (i, 1)), pl.BlockSpec((tm, d), lambda i: (i, 2)),
                  pl.BlockSpec((1, d), lambda i: (0, 0)), pl.BlockSpec((1, d), lambda i: (0, 1)), weight, weight, weight],
        out_specs=[tile, tile, pl.BlockSpec((tm, 2 * d), lambda i: (i, 0)), tile, tile, pl.BlockSpec((8, d), lambda i: (0, 0))],
        out_shape=[bf, bf, jax.ShapeDtypeStruct((t, 2 * d), BF16), bf, bf, jax.ShapeDtypeStruct((8, d), F32)],
        scratch_shapes=[pltpu.VMEM((3, d, d), BF16)],
        compiler_params=_params("arbitrary"))(dout, br_rnn, br_attn, ag_ml, ag_ml, b_gate, b_gate, w_rnn, w_attn, w_out)


def _lru_gates(c, wa, ba, wx, bx, sp):
    cb = c.astype(BF16)
    r = _sigmoid(jnp.dot(cb, wa, preferred_element_type=F32) + ba)
    ig = _sigmoid(jnp.dot(cb, wx, preferred_element_type=F32) + bx)
    log_a = (-LRU_C) * r * sp
    a = jnp.exp(log_a)
    mult = jnp.sqrt(-jnp.tanh(log_a) * (a * a + 1.0))
    return cb, r, ig, a, mult


SUBLANES = 8


def _scan_fwd(a, u, carry, tt):
    w = a.shape[1]
    ng = tt // SUBLANES
    a3 = a.reshape(ng, SUBLANES, w)
    u3 = u.reshape(ng, SUBLANES, w)
    sub = lax.broadcasted_iota(jnp.int32, (ng, SUBLANES, w), 1)
    d = 1
    while d < SUBLANES:
        keep = sub >= d
        u3 = u3 + a3 * jnp.where(keep, pltpu.roll(u3, d, 1), 0.0)
        a3 = a3 * jnp.where(keep, pltpu.roll(a3, d, 1), 1.0)
        d *= 2
    out = []
    for g in range(ng):
        hg = u3[g] + a3[g] * carry
        out.append(hg)
        carry = hg[SUBLANES - 1:SUBLANES, :]
    return jnp.concatenate(out, axis=0)


def _scan_rev(b, g, carry, tt):
    w = b.shape[1]
    ng = tt // SUBLANES
    b3 = b.reshape(ng, SUBLANES, w)
    g3 = g.reshape(ng, SUBLANES, w)
    sub = lax.broadcasted_iota(jnp.int32, (ng, SUBLANES, w), 1)
    d = 1
    while d < SUBLANES:
        keep = sub < SUBLANES - d
        g3 = g3 + b3 * jnp.where(keep, pltpu.roll(g3, SUBLANES - d, 1), 0.0)
        b3 = b3 * jnp.where(keep, pltpu.roll(b3, SUBLANES - d, 1), 1.0)
        d *= 2
    out = [None] * ng
    for k in range(ng - 1, -1, -1):
        hk = g3[k] + b3[k] * carry
        out[k] = hk
        carry = hk[0:1, :]
    return jnp.concatenate(out, axis=0)


def _conv_taps(cw, bias, x, ext_ref, tt):
    x2 = ext_ref[7:7 + tt, :]
    x1 = ext_ref[6:6 + tt, :]
    x0 = ext_ref[5:5 + tt, :]
    c = bias + cw[3:4] * x + cw[2:3] * x2 + cw[1:2] * x1 + cw[0:1] * x0
    return c, x2, x1, x0


def _rnn_fwd(rx_rg, cw, cb, wa, ba, wx, bx, lam, name, tt=512):
    t = rx_rg.shape[0]
    tt = min(tt, t)
    w = GROUP_W

    def body(rx_ref, rg_ref, cw_ref, cb_ref, wa_ref, ba_ref, wx_ref, bx_ref, lam_ref, y_ref, z_ref, ext_ref, hc_ref):
        @pl.when(pl.program_id(1) == 0)
        def _():
            ext_ref[0:8, :] = jnp.zeros((8, w), F32)
            hc_ref[...] = jnp.zeros((8, w), F32)

        x = rx_ref[...]
        ext_ref[8:8 + tt, :] = x
        c, _, _, _ = _conv_taps(cw_ref[...], cb_ref[...], x, ext_ref, tt)
        ext_ref[0:8, :] = x[tt - 8:tt, :]
        sp = _softplus(-lam_ref[...])
        _, _, ig, a, mult = _lru_gates(c, wa_ref[...], ba_ref[...], wx_ref[...], bx_ref[...], sp)
        h = _scan_fwd(a, mult * (ig * c), hc_ref[7:8, :], tt)
        hc_ref[...] = h[tt - 8:tt, :]
        y_ref[...] = h
        rg = rg_ref[...]
        z_ref[...] = (h * rg * _sigmoid(rg)).astype(BF16)

    vec = pl.BlockSpec((1, w), lambda g, i: (0, g))
    mat = pl.BlockSpec((None, w, w), lambda g, i: (g, 0, 0))
    tile = pl.BlockSpec((tt, w), lambda g, i: (i, g))
    return pl.pallas_call(
        body, name=name, grid=(N_GROUPS, t // tt),
        in_specs=[tile, pl.BlockSpec((tt, w), lambda g, i: (i, N_GROUPS + g)),
                  pl.BlockSpec((4, w), lambda g, i: (0, g)), vec, mat, vec, mat, vec, vec],
        out_specs=[tile, tile],
        out_shape=[jax.ShapeDtypeStruct((t, D_RNN), F32), jax.ShapeDtypeStruct((t, D_RNN), BF16)],
        scratch_shapes=[pltpu.VMEM((tt + 8, w), F32), pltpu.VMEM((8, w), F32)],
        compiler_params=_params("parallel", "arbitrary"))(rx_rg, rx_rg, cw, cb, wa, ba, wx, bx, lam)


def _rnn_bwd(rx_rg, y, dz, cw, cb, wa, ba, wx, bx, lam, name, tt=512, ride=None):
    t = rx_rg.shape[0]
    tt = min(tt, t)
    nt = t // tt
    w = GROUP_W

    host = _Hosted(ride, 13, 5, 6)

    def body(*refs):
        host_refs, start, finish = host.split(refs)
        (rx_ref, rg_ref, rxt_ref, y_ref, yt_ref, dz_ref, cw_ref, cb_ref, wa_ref, ba_ref, wx_ref, bx_ref, lam_ref,
         drx_ref, drg_ref, st_ref, gda_ref, gdx_ref, ext_ref, dcx_ref, wcar_ref, acar_ref, dwa_ref, dwx_ref) = host_refs
        ii = pl.program_id(1)
        start((pl.program_id(0) == 0) & (ii == 0))

        @pl.when(ii == 0)
        def _():
            wcar_ref[...] = jnp.zeros((8, w), F32)
            acar_ref[...] = jnp.zeros((8, w), F32)
            dcx_ref[tt:tt + 8, :] = jnp.zeros((8, w), F32)
            st_ref[...] = jnp.zeros_like(st_ref)
            dwa_ref[...] = jnp.zeros_like(dwa_ref)
            dwx_ref[...] = jnp.zeros_like(dwx_ref)

        has_prev = jnp.where(ii == nt - 1, 0.0, 1.0)
        x = rx_ref[...]
        ext_ref[0:8, :] = rxt_ref[...] * has_prev
        ext_ref[8:8 + tt, :] = x
        cwv = cw_ref[...]
        c, x2, x1, x0 = _conv_taps(cwv, cb_ref[...], x, ext_ref, tt)
        lam = lam_ref[...]
        sp = _softplus(-lam)
        wa = wa_ref[...]
        wx = wx_ref[...]
        cb16, r, ig, a, mult = _lru_gates(c, wa, ba_ref[...], wx, bx_ref[...], sp)

        rg = rg_ref[...]
        sg = _sigmoid(rg)
        dz = dz_ref[...].astype(F32)
        yv = y_ref[...]
        drg_ref[...] = (dz * yv * (sg * (1.0 + rg * (1.0 - sg)))).astype(BF16)

        row = lax.broadcasted_iota(jnp.int32, (tt, w), 0)
        b = jnp.where(row < tt - 1, pltpu.roll(a, tt - 1, 0), acar_ref[0:1, :])
        dh = _scan_rev(b, dz * (rg * sg), wcar_ref[0:1, :], tt)
        wcar_ref[...] = dh[0:8, :]
        acar_ref[...] = a[0:8, :]

        hprev = jnp.where(row >= 1, pltpu.roll(yv, 1, 0), yt_ref[7:8, :] * has_prev)
        dmult = dh * (ig * c)
        dig = dh * mult * c
        dlog_a = dh * hprev * a - dmult * (a * a / mult)
        dpa = dlog_a * ((-LRU_C) * sp) * r * (1.0 - r)
        dpx = dig * ig * (1.0 - ig)
        dsp = jnp.sum(dlog_a * r, axis=0, keepdims=True) * (-LRU_C)
        dlam = dsp * (-_sigmoid(-lam))
        dpa16 = dpa.astype(BF16)
        dpx16 = dpx.astype(BF16)
        dwa_ref[...] += lax.dot_general(cb16, dpa16, TN_DIMS, preferred_element_type=F32)
        dwx_ref[...] += lax.dot_general(cb16, dpx16, TN_DIMS, preferred_element_type=F32)
        dc = (dh * mult * ig
              + lax.dot_general(dpa16, wa, NT_DIMS, preferred_element_type=F32)
              + lax.dot_general(dpx16, wx, NT_DIMS, preferred_element_type=F32))

        dcx_ref[0:tt, :] = dc
        drx = (cwv[3:4] * dc + cwv[2:3] * dcx_ref[1:1 + tt, :] + cwv[1:2] * dcx_ref[2:2 + tt, :]
               + cwv[0:1] * dcx_ref[3:3 + tt, :])
        drx_ref[...] = drx.astype(BF16)
        dcx_ref[tt:tt + 8, :] = dc[0:8, :]

        def colsum(v):
            return jnp.sum(v, axis=0, keepdims=True)

        st_ref[...] += _rows8([colsum(dc), colsum(dpa), colsum(dpx), dlam,
                               colsum(dc * x0), colsum(dc * x1), colsum(dc * x2), colsum(dc * x)], w)

        @pl.when(ii == nt - 1)
        def _():
            for blk in range(GROUP_W // RNN_BLOCK_W):
                rows = slice(blk * RNN_BLOCK_W, (blk + 1) * RNN_BLOCK_W)
                gda_ref[blk] = dwa_ref[rows, rows]
                gdx_ref[blk] = dwx_ref[rows, rows]

        finish((pl.program_id(0) == N_GROUPS - 1) & (ii == nt - 1))

    def rev(ii):
        return nt - 1 - ii

    def tail(g, ii):
        return (jnp.maximum(rev(ii) * (tt // 8) - 1, 0), g)

    vec = pl.BlockSpec((1, w), lambda g, ii: (0, g))
    mat = pl.BlockSpec((None, w, w), lambda g, ii: (g, 0, 0))
    tile = pl.BlockSpec((tt, w), lambda g, ii: (rev(ii), g))
    diag_shape = (N_GROUPS, GROUP_W // RNN_BLOCK_W, RNN_BLOCK_W, RNN_BLOCK_W)
    diag = pl.BlockSpec((None,) + diag_shape[1:], lambda g, ii: (g, 0, 0, 0))
    outs = pl.pallas_call(
        body, name=name, grid=(N_GROUPS, nt),
        in_specs=[tile, pl.BlockSpec((tt, w), lambda g, ii: (rev(ii), N_GROUPS + g)), pl.BlockSpec((8, w), tail),
                  tile, pl.BlockSpec((8, w), tail), tile,
                  pl.BlockSpec((4, w), lambda g, ii: (0, g)), vec, mat, vec, mat, vec, vec] + host.in_specs,
        out_specs=[tile, tile, pl.BlockSpec((8, w), lambda g, ii: (0, g)), diag, diag] + host.out_specs,
        out_shape=[jax.ShapeDtypeStruct((t, D_RNN), BF16), jax.ShapeDtypeStruct((t, D_RNN), BF16),
                   jax.ShapeDtypeStruct((8, D_RNN), F32),
                   jax.ShapeDtypeStruct(diag_shape, F32), jax.ShapeDtypeStruct(diag_shape, F32)] + host.out_shapes,
        scratch_shapes=[pltpu.VMEM((tt + 8, w), F32), pltpu.VMEM((tt + 8, w), F32), pltpu.VMEM((8, w), F32),
                        pltpu.VMEM((8, w), F32), pltpu.VMEM((w, w), F32), pltpu.VMEM((w, w), F32)] + host.scratch_shapes,
        compiler_params=_params("arbitrary" if ride else "parallel", "arbitrary"))(
            rx_rg, rx_rg, rx_rg, y, y, dz, cw, cb, wa, ba, wx, bx, lam, *host.arrays)
    res, landed = host.results(outs, 5)
    return (*res, landed) if ride else tuple(res)


def _half_mask(shape, half):
    lane = lax.broadcasted_iota(jnp.int32, shape, 1)
    return (lane >= HEAD_DIM) if half else (lane < HEAD_DIM)


def _dup_half(t, half):
    sel = jnp.where(_half_mask(t.shape, half), t, 0.0)
    return sel + pltpu.roll(sel, HEAD_DIM, 1)


def _band_geometry(n):
    qi = lax.broadcasted_iota(jnp.int32, (BLOCK, 2 * BLOCK), 0)
    kj = lax.broadcasted_iota(jnp.int32, (BLOCK, 2 * BLOCK), 1)
    dist = BLOCK + qi - kj
    first_key = jnp.where(n > 0, 0, BLOCK)
    valid = (dist >= 0) & (dist < BLOCK) & (kj >= first_key)
    return dist.astype(F32), valid


GROUP = 4


def _kv_dup(prev_ref, cur_ref, hk, scale=1.0):
    tile = hk // 2
    kt = jnp.concatenate([prev_ref[:, tile * LANE:(tile + 1) * LANE], cur_ref[:, tile * LANE:(tile + 1) * LANE]], axis=0)
    return (_dup_half(kt.astype(F32), hk % 2) * scale).astype(BF16)


def _fill_bias(bias_ref, sm_ref, n):
    distf, valid = _band_geometry(n)
    for head in range(N_Q_HEADS):
        bias_ref[head] = jnp.where(valid, -sm_ref[1, head] * distf, MASKED)


def _head_scores(q2s, half, kdup, bias):
    qm = jnp.where(_half_mask(q2s.shape, half), q2s, jnp.zeros_like(q2s))
    return qm, lax.dot_general(qm, kdup, NT_DIMS, preferred_element_type=F32) + bias


def _attn_specs(nb, clamp_last):
    def blk(n):
        return jnp.minimum(n, nb - 1) if clamp_last else n

    q_spec = pl.BlockSpec((BLOCK, D_MODEL), lambda n: (blk(n), 0))
    k_prev = pl.BlockSpec((BLOCK, D_KV), lambda n: (jnp.maximum(blk(n) - 1, 0), D_MODEL // D_KV))
    k_cur = pl.BlockSpec((BLOCK, D_KV), lambda n: (blk(n), D_MODEL // D_KV))
    v_prev = pl.BlockSpec((BLOCK, D_KV), lambda n: (jnp.maximum(blk(n) - 1, 0), D_MODEL // D_KV + 1))
    v_cur = pl.BlockSpec((BLOCK, D_KV), lambda n: (blk(n), D_MODEL // D_KV + 1))
    return q_spec, k_prev, k_cur, v_prev, v_cur


def _attn_fwd(sm, qkv, ag_ml, name, ride=None):
    t = qkv.shape[0]
    nb = t // BLOCK

    host = _Hosted(ride, 7, 3, 1)

    def body(*refs):
        (sm_ref, q_ref, kp_ref, kc_ref, vp_ref, vc_ref, ag_ref, y_ref, z_ref, lse_ref, bias_ref), start, finish = host.split(refs)
        n = pl.program_id(0)
        start(n == 0)
        start.middle(n == (3 * nb) // 4)

        @pl.when(n <= 1)
        def _():
            _fill_bias(bias_ref, sm_ref, n)

        lane = lax.broadcasted_iota(jnp.int32, (BLOCK, LANE), 1)
        low = lane < HEAD_DIM
        lse = jnp.zeros((BLOCK, LANE), F32)
        for hk in range(N_Q_HEADS // GROUP):
            kdup = _kv_dup(kp_ref, kc_ref, hk)
            vdup = _kv_dup(vp_ref, vc_ref, hk)
            for k in (0, 1):
                c = slice((2 * hk + k) * LANE, (2 * hk + k + 1) * LANE)
                q2s = q_ref[:, c] * ATTN_SCALE
                outs = []
                for half in (0, 1):
                    head = GROUP * hk + 2 * k + half
                    _, s = _head_scores(q2s, half, kdup, bias_ref[head])
                    sink = sm_ref[0, head]
                    m = jnp.maximum(jnp.max(s, axis=1, keepdims=True), sink)
                    e = jnp.exp(s - m)
                    l = jnp.sum(e, axis=1, keepdims=True) + jnp.exp(sink - m)
                    outs.append(jnp.dot((e * (1.0 / l)).astype(BF16), vdup, preferred_element_type=F32))
                    lse = jnp.where(lane == head, m + jnp.log(l), lse)
                yt = jnp.where(low, outs[0], outs[1])
                y_ref[:, c] = yt
                ag = ag_ref[:, c].astype(F32)
                z_ref[:, c] = (yt * ag * _sigmoid(ag)).astype(BF16)
        lse_ref[...] = lse
        finish(n == nb - 1)

    q_spec, k_prev, k_cur, v_prev, v_cur = _attn_specs(nb, False)
    wide = pl.BlockSpec((BLOCK, D_MODEL), lambda n: (n, 0))
    outs = pl.pallas_call(
        body, name=name, grid=(nb,),
        in_specs=[pl.BlockSpec(memory_space=pltpu.SMEM), q_spec, k_prev, k_cur, v_prev, v_cur, wide] + host.in_specs,
        out_specs=[wide, wide, pl.BlockSpec((BLOCK, LANE), lambda n: (n, 0))] + host.out_specs,
        out_shape=[jax.ShapeDtypeStruct((t, D_MODEL), F32), jax.ShapeDtypeStruct((t, D_MODEL), BF16),
                   jax.ShapeDtypeStruct((t, LANE), F32)] + host.out_shapes,
        scratch_shapes=[pltpu.VMEM((N_Q_HEADS, BLOCK, 2 * BLOCK), F32)] + host.scratch_shapes,
        compiler_params=_params("arbitrary"))(sm, qkv, qkv, qkv, qkv, qkv, ag_ml, *host.arrays)
    res, landed = host.results(outs, 3)
    return (*res, landed) if ride else tuple(res)


def _attn_bwd(sm, qkv, ag_ml, y, lse, dz, name, ride=None):
    t = qkv.shape[0]
    nb = t // BLOCK
    host = _Hosted(ride, 10, 4, 3)

    def body(*refs):
        host_refs, start, finish = host.split(refs)
        (sm_ref, q_ref, kp_ref, kc_ref, vp_ref, vc_ref, ag_ref, y_ref, lse_ref, dz_ref,
         dq_ref, dkv_ref, dag_ref, ds_ref, ck_ref, cv_ref, bias_ref) = host_refs
        n = pl.program_id(0)
        start(n == 0)
        start.middle(n == (3 * nb) // 4)

        @pl.when(n == 0)
        def _():
            ck_ref[...] = jnp.zeros_like(ck_ref)
            cv_ref[...] = jnp.zeros_like(cv_ref)
            ds_ref[...] = jnp.zeros_like(ds_ref)

        @pl.when(n <= 1)
        def _():
            _fill_bias(bias_ref, sm_ref, n)

        @pl.when(n < nb)
        def _():
            lane8 = lax.broadcasted_iota(jnp.int32, (8, LANE), 1)
            row8 = lax.broadcasted_iota(jnp.int32, (8, LANE), 0)
            dsink = jnp.zeros((8, LANE), F32)
            dk_heads, dv_heads = [], []
            lse_tile = lse_ref[...]
            for hk in range(N_Q_HEADS // GROUP):
                kdup = _kv_dup(kp_ref, kc_ref, hk)
                ks = _kv_dup(kp_ref, kc_ref, hk, ATTN_SCALE)
                vdup = _kv_dup(vp_ref, vc_ref, hk)
                qms, dyhs, y_rows = [], [], []
                for k in (0, 1):
                    c = slice((2 * hk + k) * LANE, (2 * hk + k + 1) * LANE)
                    ag = ag_ref[:, c].astype(F32)
                    sg = _sigmoid(ag)
                    dzt = dz_ref[:, c].astype(F32)
                    yt = y_ref[:, c]
                    dag_ref[:, c] = (dzt * yt * (sg * (1.0 + ag * (1.0 - sg)))).astype(BF16)
                    dyt = dzt * (ag * sg)
                    q2s = q_ref[:, c] * ATTN_SCALE
                    for half in (0, 1):
                        hm = _half_mask(q2s.shape, half)
                        qms.append(jnp.where(hm, q2s, jnp.zeros_like(q2s)))
                        dyhs.append(jnp.where(hm, dyt, 0.0))
                        y_rows.append(yt)
                qm4 = jnp.concatenate(qms, axis=0)
                dy4 = jnp.concatenate(dyhs, axis=0)
                dy4_16 = dy4.astype(BF16)
                s4 = lax.dot_general(qm4, kdup, NT_DIMS, preferred_element_type=F32)
                dp4 = lax.dot_general(dy4_16, vdup, NT_DIMS, preferred_element_type=F32)
                probs16, ds16 = [], []
                for r in range(GROUP):
                    head = GROUP * hk + r
                    rows = slice(r * BLOCK, (r + 1) * BLOCK)
                    lh = lse_tile[:, head:head + 1]
                    probs = jnp.exp(s4[rows] + bias_ref[head] - lh)
                    psink = jnp.exp(sm_ref[0, head] - lh)
                    delta = jnp.sum(dyhs[r] * y_rows[r], axis=1, keepdims=True)
                    ds16.append((probs * (dp4[rows] - delta)).astype(BF16))
                    probs16.append(probs.astype(BF16))
                    dsink = dsink + jnp.where((row8 == 0) & (lane8 == head),
                                              -jnp.sum(psink * delta, axis=0, keepdims=True), 0.0)
                ds4 = jnp.concatenate(ds16, axis=0)
                p4 = jnp.concatenate(probs16, axis=0)
                dq4 = jnp.dot(ds4, ks, preferred_element_type=F32)
                low = _half_mask((BLOCK, LANE), 0)
                for k in (0, 1):
                    c = slice((2 * hk + k) * LANE, (2 * hk + k + 1) * LANE)
                    dq_ref[:, c] = jnp.where(low, dq4[2 * k * BLOCK:(2 * k + 1) * BLOCK],
                                             dq4[(2 * k + 1) * BLOCK:(2 * k + 2) * BLOCK]).astype(BF16)
                dk_acc = lax.dot_general(ds4, qm4, TN_DIMS, preferred_element_type=F32)
                dv_acc = lax.dot_general(p4, dy4_16, TN_DIMS, preferred_element_type=F32)
                dk_heads.append(dk_acc + pltpu.roll(dk_acc, HEAD_DIM, 1))
                dv_heads.append(dv_acc + pltpu.roll(dv_acc, HEAD_DIM, 1))
            ds_ref[...] += dsink
            low = _half_mask((2 * BLOCK, LANE), 0)
            for tile in range(2):
                cols = slice(tile * LANE, (tile + 1) * LANE)
                dkt = jnp.where(low, dk_heads[2 * tile], dk_heads[2 * tile + 1])
                dvt = jnp.where(low, dv_heads[2 * tile], dv_heads[2 * tile + 1])
                dkv_ref[:, cols] = (ck_ref[:, cols] + dkt[0:BLOCK, :]).astype(BF16)
                dkv_ref[:, D_KV + tile * LANE:D_KV + (tile + 1) * LANE] = (cv_ref[:, cols] + dvt[0:BLOCK, :]).astype(BF16)
                ck_ref[:, cols] = dkt[BLOCK:2 * BLOCK, :]
                cv_ref[:, cols] = dvt[BLOCK:2 * BLOCK, :]

        @pl.when(n == nb)
        def _():
            dkv_ref[:, 0:D_KV] = ck_ref[...].astype(BF16)
            dkv_ref[:, D_KV:2 * D_KV] = cv_ref[...].astype(BF16)

        finish(n == nb)

    q_spec, k_prev, k_cur, v_prev, v_cur = _attn_specs(nb, True)
    wide = pl.BlockSpec((BLOCK, D_MODEL), lambda n: (jnp.minimum(n, nb - 1), 0))
    outs = pl.pallas_call(
        body, name=name, grid=(nb + 1,),
        in_specs=[pl.BlockSpec(memory_space=pltpu.SMEM), q_spec, k_prev, k_cur, v_prev, v_cur, wide, wide,
                  pl.BlockSpec((BLOCK, LANE), lambda n: (jnp.minimum(n, nb - 1), 0)), wide] + host.in_specs,
        out_specs=[wide, pl.BlockSpec((BLOCK, 2 * D_KV), lambda n: (jnp.maximum(n - 1, 0), 0)), wide,
                   pl.BlockSpec((8, LANE), lambda n: (0, 0))] + host.out_specs,
        out_shape=[jax.ShapeDtypeStruct((t, D_MODEL), BF16), jax.ShapeDtypeStruct((t, 2 * D_KV), BF16),
                   jax.ShapeDtypeStruct((t, D_MODEL), BF16), jax.ShapeDtypeStruct((8, LANE), F32)] + host.out_shapes,
        scratch_shapes=[pltpu.VMEM((BLOCK, D_KV), F32), pltpu.VMEM((BLOCK, D_KV), F32),
                        pltpu.VMEM((N_Q_HEADS, BLOCK, 2 * BLOCK), F32)] + host.scratch_shapes,
        compiler_params=_params("arbitrary"))(sm, qkv, qkv, qkv, qkv, qkv, ag_ml, y, lse, dz, *host.arrays)
    res, landed = host.results(outs, 4)
    return (*res, landed) if ride else tuple(res)


def _local_grads(x, target, p, project, late_weights=None, reduce_out=None, reduce_in=None):
    h, rx_rg, qkv, ag_ml, wt = project(x, p["pre_g"])
    if late_weights is None:
        y_attn, z_attn, lse = _attn_fwd(p["sm"], qkv, ag_ml, "attn_fwd")
    else:
        y_attn, z_attn, lse, landed = _attn_fwd(p["sm"], qkv, ag_ml, "attn_fwd", ride=late_weights[0])
        p = {**p, **late_weights[1](landed)}
    lru = (p["cw"], p["cb"], p["wbd_a"], p["b_a"], p["wbd_x"], p["b_x"], p["lam"])
    y_rnn, z_rnn = _rnn_fwd(rx_rg, *lru, "rnn_fwd")
    br_rnn, br_attn, merged, dout, dy, st_post = _branches_fwd(
        z_rnn, z_attn, ag_ml, p["b_gate"], p["w_rnn"], p["w_attn"], p["w_out"], x, target, p["post_g"], "branches_fwd")

    dbr_rnn, dbr_attn, d_ml, dz_rnn, dz_attn, st_merge = _branches_bwd(
        dout, br_rnn, br_attn, ag_ml, p["b_gate"], p["w_rnn"], p["w_attn"], p["w_out"], "branches_bwd")
    gw_out = _mm_tn(merged, dout, "gw_out")
    gw_rnn = _mm_tn(z_rnn, dbr_rnn, "gw_rnn")
    gw_attn = _mm_tn(z_attn, dbr_attn, "gw_attn")
    red_out = red_in = None
    if reduce_out is None:
        d_rx, d_rg, st_rnn, g_rg_a, g_rg_x = _rnn_bwd(rx_rg, y_rnn, dz_rnn, *lru, "rnn_bwd")
        dq, dkv, d_ag, st_sink = _attn_bwd(p["sm"], qkv, ag_ml, y_attn, lse, dz_attn, "attn_bwd")
    else:
        d_rx, d_rg, st_rnn, g_rg_a, g_rg_x, from_sibling = _rnn_bwd(
            rx_rg, y_rnn, dz_rnn, *lru, "rnn_bwd", ride=reduce_out[0](gw_rnn, gw_attn, gw_out))
        dq, dkv, d_ag, st_sink, red_out = _attn_bwd(p["sm"], qkv, ag_ml, y_attn, lse, dz_attn, "attn_bwd",
                                                    ride=reduce_out[1](from_sibling, g_rg_a, g_rg_x))

    segs = [d_rx, d_rg, dq, dkv, d_ag, d_ml]
    if reduce_in is None:
        gwt = _mm_tn_seg(segs, h, "gw_in")
        grad_x, st_pre = _input_grad(segs, wt, x, p["pre_g"], dy, "input_grad")
    else:
        gwt = None
        grad_x, st_pre, red_in = _input_grad(segs, wt, x, p["pre_g"], dy, "input_grad",
                                             ride=reduce_in(*_mm_tn_seg_pair(segs, h, "gw_in")))
    return dict(grad_x=grad_x, gwt=gwt, gw_rnn=gw_rnn, gw_attn=gw_attn, gw_out=gw_out,
                st_post=st_post, st_merge=st_merge, st_rnn=st_rnn, st_sink=st_sink, st_pre=st_pre,
                g_rg_a=g_rg_a, g_rg_x=g_rg_x, red_out=red_out, red_in=red_in)


def _place():
    x, y, c = lax.axis_index("x"), lax.axis_index("y"), lax.axis_index("c")
    return x, y, c


def _gather_ride(shards):
    n = len(shards)

    def copies(ins, outs, sems):
        send_sems, recv_sems, local_sems = sems
        x, y, c = _place()
        me, sibling = (x, y, c), (x, y, 1 - c)
        chips = [(1 - x, y), (x, 1 - y), (1 - x, 1 - y)]

        def slot(a, dev):
            return outs[a].at[4 * dev[0] + 2 * dev[1] + dev[2]]

        def copy(a, k, block, to, src=None):
            return pltpu.make_async_remote_copy(
                src_ref=slot(a, block) if src is None else src, dst_ref=slot(a, block),
                send_sem=send_sems.at[a, k], recv_sem=recv_sems.at[a, k], device_id=to, device_id_type=MESH)

        mine = [pltpu.make_async_copy(ins[a], slot(a, me), local_sems.at[a]) for a in range(n)]
        first = []
        for a in range(n):
            first.append(copy(a, 0, me, sibling, src=ins[a]))
            first += [copy(a, 1 + j, me, (*chip, c), src=ins[a]) for j, chip in enumerate(chips)]
        return me, sibling, chips, c, copy, mine, first

    def start(ins, outs, sems):
        *_, mine, first = copies(ins, outs, sems)
        for cp in mine + first:
            cp.start()

    def middle(ins, outs, sems):
        me, sibling, chips, c, copy, _, _ = copies(ins, outs, sems)
        for j, chip in enumerate(chips):
            for a in range(n):
                copy(a, 1 + j, (*chip, c), me).wait_recv()
                copy(a, 4 + j, (*chip, c), sibling).start()

    def finish(ins, outs, sems):
        me, sibling, chips, c, copy, mine, first = copies(ins, outs, sems)
        passed = [copy(a, 4 + j, (*chip, c), sibling) for j, chip in enumerate(chips) for a in range(n)]
        for a in range(n):
            copy(a, 0, sibling, me).wait_recv()
            for j, chip in enumerate(chips):
                copy(a, 4 + j, (*chip, 1 - c), me).wait_recv()
        for cp in first + passed:
            cp.wait_send()
        for cp in mine:
            cp.wait()

    return _Ride(
        shards, [jax.ShapeDtypeStruct((N_DEV, *s.shape), s.dtype) for s in shards],
        [pltpu.SemaphoreType.DMA((n, 7)), pltpu.SemaphoreType.DMA((n, 7)), pltpu.SemaphoreType.DMA((n,))],
        start, finish, middle)


def _sibling_ride(scatter, whole):
    ns, nw = len(scatter), len(whole)

    def copies(ins, outs, sems):
        send_sems, recv_sems = sems
        x, y, c = _place()
        cps = [pltpu.make_async_remote_copy(
            src_ref=ins[a].at[2 * chip + (1 - c)], dst_ref=outs[a].at[chip],
            send_sem=send_sems.at[a * N_CHIP + chip], recv_sem=recv_sems.at[a * N_CHIP + chip],
            device_id=(x, y, 1 - c), device_id_type=MESH) for a in range(ns) for chip in range(N_CHIP)]
        cps += [pltpu.make_async_remote_copy(
            src_ref=ins[ns + a], dst_ref=outs[ns + a],
            send_sem=send_sems.at[ns * N_CHIP + a], recv_sem=recv_sems.at[ns * N_CHIP + a],
            device_id=(x, y, 1 - c), device_id_type=MESH) for a in range(nw)]
        return cps

    def start(ins, outs, sems):
        for cp in copies(ins, outs, sems):
            cp.start()

    def finish(ins, outs, sems):
        for cp in copies(ins, outs, sems):
            cp.wait()

    n_sem = ns * N_CHIP + nw
    return _Ride(
        list(scatter) + list(whole),
        [jax.ShapeDtypeStruct((N_CHIP, *s.shape[1:]), s.dtype) for s in scatter]
        + [jax.ShapeDtypeStruct(s.shape, s.dtype) for s in whole],
        [pltpu.SemaphoreType.DMA((n_sem,)), pltpu.SemaphoreType.DMA((n_sem,))], start, finish)


def _chips_ride(scatter, whole):
    ns, nw = len(scatter), len(whole)
    n = ns + nw

    def copies(ins, outs, sems):
        send_sems, recv_sems, local_sems = sems
        x, y, c = _place()
        own = 2 * x + y
        chips = [(1 - x, y), (x, 1 - y), (1 - x, 1 - y)]

        def src(a, chip_idx):
            return ins[a].at[chip_idx] if a < ns else ins[a]

        local = [pltpu.make_async_copy(src(a, own), outs[a].at[own], local_sems.at[a]) for a in range(n)]
        sent = [pltpu.make_async_remote_copy(
            src_ref=src(a, 2 * chip[0] + chip[1]), dst_ref=outs[a].at[own],
            send_sem=send_sems.at[a, j], recv_sem=recv_sems.at[a, own], device_id=(*chip, c), device_id_type=MESH)
            for a in range(n) for j, chip in enumerate(chips)]
        return chips, c, local, sent

    def start(ins, outs, sems):
        _, _, local, sent = copies(ins, outs, sems)
        for cp in local + sent:
            cp.start()

    def finish(ins, outs, sems):
        send_sems, recv_sems, _ = sems
        chips, c, local, sent = copies(ins, outs, sems)
        for a in range(n):
            for chip in chips:
                k = 2 * chip[0] + chip[1]
                pltpu.make_async_remote_copy(
                    src_ref=outs[a].at[k], dst_ref=outs[a].at[k], send_sem=send_sems.at[a, 0],
                    recv_sem=recv_sems.at[a, k], device_id=(*chip, c), device_id_type=MESH).wait_recv()
        for cp in sent:
            cp.wait_send()
        for cp in local:
            cp.wait()

    return _Ride(
        list(scatter) + list(whole),
        [jax.ShapeDtypeStruct(s.shape, s.dtype) for s in scatter]
        + [jax.ShapeDtypeStruct((N_CHIP, *s.shape), s.dtype) for s in whole],
        [pltpu.SemaphoreType.DMA((n, 3)), pltpu.SemaphoreType.DMA((n, N_CHIP)), pltpu.SemaphoreType.DMA((n,))],
        start, finish)


def _chips_rest_ride(pair, red):
    table = _side_pieces()
    nc = len(CHUNK_ORDER)

    def each(ins, outs, sems, sending, landing):
        send_sems, recv_sems, _ = sems
        pair_ref, red_ref = ins[0], outs[0]
        x, y, c = _place()
        own = 2 * x + y
        for core in (0, 1):
            mine = table[core][0]
            rest = [s for s in range(EARLY_STEPS, nc) if mine[s] is not None]

            @pl.when(c == core)
            def _(mine=mine, rest=rest, core=core):
                for s in rest:
                    _, rows, at = mine[s]
                    k, cp = _to_chip(pair_ref.at[at // SHARD_IN, pl.ds(at % SHARD_IN, rows)], red_ref, mine[s], s,
                                     send_sems, recv_sems, own, core)
                    pl.when(own != k)(lambda cp=cp: sending(cp))
                if landing is not None:
                    for k in range(N_CHIP):
                        @pl.when(own == k)
                        def _(k=k):
                            for s in rest:
                                _, rows, at = mine[s]
                                if at // SHARD_IN == k:
                                    for chip in range(N_CHIP):
                                        if chip != k:
                                            landing(_from_chip(pair_ref.at[k, pl.ds(at % SHARD_IN, rows)], red_ref, mine[s], s,
                                                               send_sems, recv_sems, chip, (x, y, c)))

    def local(ins, outs, sems):
        x, y, _ = _place()
        return pltpu.make_async_copy(ins[0].at[2 * x + y], outs[0].at[2 * x + y], sems[2])

    def start(ins, outs, sems):
        local(ins, outs, sems).start()
        each(ins, outs, sems, lambda cp: cp.start(), None)

    def finish(ins, outs, sems):
        each(ins, outs, sems, lambda cp: cp.wait_send(), lambda cp: cp.wait_recv())
        local(ins, outs, sems).wait()

    return _Ride([pair, red], [jax.ShapeDtypeStruct(red.shape, red.dtype)],
                 [pltpu.SemaphoreType.DMA((nc,)), pltpu.SemaphoreType.DMA((N_CHIP, nc)), pltpu.SemaphoreType.DMA],
                 start, finish, aliases={1: 0})


def _pair_sum_scatter(parts, recvs, core, name):
    na = len(parts)
    _, r, cdim = parts[0].shape
    tr = min(r, 416 if r % 416 == 0 else 128)

    def body(core_ref, *refs):
        del core_ref
        for a in range(na):
            refs[2 * na + a][...] = (refs[a][...] + refs[na + a][...]).astype(BF16)

    blk = (None, tr, cdim)
    mine = pl.BlockSpec(blk, lambda k, i, core_ref: (2 * k + core_ref[0], i, 0))
    slot = pl.BlockSpec(blk, lambda k, i, core_ref: (k, i, 0))
    return pl.pallas_call(
        body, name=name,
        grid_spec=pltpu.PrefetchScalarGridSpec(
            num_scalar_prefetch=1, grid=(N_CHIP, r // tr),
            in_specs=[mine] * na + [slot] * na, out_specs=[slot] * na),
        out_shape=[jax.ShapeDtypeStruct((N_CHIP, r, cdim), BF16)] * na,
        compiler_params=_params("parallel", "parallel"))(core, *parts, *recvs)


def _allreduce_small(pack, name):
    shape = pack.shape

    def body(x_ref, o_ref, sib_ref, chip_ref, send_sems, recv_sems):
        x, y, c = _place()
        own = 2 * x + y
        chips = [(1 - x, y), (x, 1 - y), (1 - x, 1 - y)]
        to_sibling = pltpu.make_async_remote_copy(
            src_ref=x_ref, dst_ref=sib_ref, send_sem=send_sems.at[0], recv_sem=recv_sems.at[0],
            device_id=(x, y, 1 - c), device_id_type=MESH)
        to_sibling.start()
        to_sibling.wait()
        chip_ref[own] = x_ref[...] + sib_ref[...]
        sent = [pltpu.make_async_remote_copy(
            src_ref=chip_ref.at[own], dst_ref=chip_ref.at[own], send_sem=send_sems.at[1 + j],
            recv_sem=recv_sems.at[1 + own], device_id=(*chip, c), device_id_type=MESH) for j, chip in enumerate(chips)]
        for cp in sent:
            cp.start()
        for chip in chips:
            k = 2 * chip[0] + chip[1]
            pltpu.make_async_remote_copy(
                src_ref=chip_ref.at[k], dst_ref=chip_ref.at[k], send_sem=send_sems.at[1],
                recv_sem=recv_sems.at[1 + k], device_id=(*chip, c), device_id_type=MESH).wait_recv()
        for cp in sent:
            cp.wait_send()
        o_ref[...] = (chip_ref[0] + chip_ref[1]) + (chip_ref[2] + chip_ref[3])

    return pl.pallas_call(
        body, name=name, out_shape=jax.ShapeDtypeStruct(shape, F32),
        in_specs=[pl.BlockSpec(memory_space=pltpu.VMEM)], out_specs=pl.BlockSpec(memory_space=pltpu.VMEM),
        scratch_shapes=[pltpu.VMEM(shape, F32), pltpu.VMEM((N_CHIP, *shape), F32),
                        pltpu.SemaphoreType.DMA((4,)), pltpu.SemaphoreType.DMA((1 + N_CHIP,))],
    )(pack)


def _adamw(g, w, m, v):
    m = ADAM_B1 * m + (1.0 - ADAM_B1) * g
    v = ADAM_B2 * v + (1.0 - ADAM_B2) * (g * g)
    m_hat = m / (1.0 - ADAM_B1 ** ADAM_STEP)
    v_hat = v / (1.0 - ADAM_B2 ** ADAM_STEP)
    delta = -ADAM_LR * (m_hat / (jnp.sqrt(v_hat) + ADAM_EPS) + ADAM_WD * w)
    return delta, m, v


def _adam_parts(parts, w, m, v, name, tr=None):
    npart, r, c = parts.shape
    tr = r if tr is None else min(tr, r)

    def body(p_ref, w_ref, m_ref, v_ref, g_ref, d_ref, nm_ref, nv_ref):
        g = p_ref[0].astype(F32)
        for k in range(1, npart):
            g = g + p_ref[k].astype(F32)
        g_ref[...] = g
        d_ref[...], nm_ref[...], nv_ref[...] = _adamw(g, w_ref[...], m_ref[...], v_ref[...])

    tile = pl.BlockSpec((tr, c), lambda i: (i, 0))
    return pl.pallas_call(
        body, name=name, grid=(r // tr,),
        in_specs=[pl.BlockSpec((npart, tr, c), lambda i: (0, i, 0)), tile, tile, tile],
        out_specs=[tile] * 4, out_shape=[jax.ShapeDtypeStruct((r, c), F32)] * 4,
        compiler_params=_params("parallel"))(parts, w, m, v)


def _block_diag(w):
    w4 = w.reshape(N_GROUPS, 4, RNN_BLOCK_W, RNN_BLOCK_W)
    eye = jnp.eye(4, dtype=w.dtype)
    return jnp.einsum("gbij,bc->gbicj", w4, eye).reshape(N_GROUPS, GROUP_W, GROUP_W).astype(BF16)


SMALL_ROWS = 16
ROW_PRE_G, ROW_BGATE, ROW_CONV_B, ROW_B_A, ROW_B_X, ROW_LAM, ROW_POST_G, ROW_LOSS, ROW_SINKS, ROW_CONV_W = 0, 1, 3, 4, 5, 6, 7, 8, 9, 10


def _pack_stats(st_pre, st_merge, st_rnn, st_post, st_sink, name):
    d = D_MODEL

    def body(pre_ref, mg_ref, rnn_ref, post_ref, sink_ref, o_ref):
        rnn = rnn_ref[...]
        sinks = jnp.concatenate([sink_ref[0:1, :], jnp.zeros((1, d - LANE), F32)], axis=1)
        o_ref[0:8, :] = _rows8([pre_ref[0:1, :], mg_ref[0:1, :], mg_ref[1:2, :], rnn[0:1], rnn[1:2], rnn[2:3], rnn[3:4],
                                post_ref[0:1, :]], d)
        o_ref[8:16, :] = _rows8([post_ref[1:2, :], sinks, rnn[4:5], rnn[5:6], rnn[6:7], rnn[7:8]], d)

    return pl.pallas_call(body, name=name, out_shape=jax.ShapeDtypeStruct((SMALL_ROWS, d), F32))(
        st_pre, st_merge, st_rnn, st_post, st_sink)


def _adam_small(total, w, m, v, name):
    d = D_MODEL
    n_in = len(w)

    def pack(refs):
        pre, bg, cbias, ba, bx, lam, post, sinks = [r[...] for r in refs]
        top = _rows8([pre, bg[:, 0:d], bg[:, d:2 * d], cbias, ba, bx, lam, post], d)
        return jnp.concatenate([top, _rows8([jnp.zeros((1, d), F32), sinks], d)], axis=0)

    def body(*refs):
        p_ref = refs[0]
        w_refs, m_refs, v_refs = (refs[1 + k * n_in:1 + (k + 1) * n_in] for k in range(3))
        outs = refs[1 + 3 * n_in:]
        g = p_ref[...]
        res = (g,) + _adamw(g, pack(w_refs), pack(m_refs), pack(v_refs))
        for k in range(4):
            outs[k][...] = res[k]
            outs[4 + k][...] = jnp.concatenate([res[k][ROW_BGATE:ROW_BGATE + 1], res[k][ROW_BGATE + 1:ROW_BGATE + 2]], axis=1)

    return pl.pallas_call(
        body, name=name,
        out_shape=[jax.ShapeDtypeStruct((SMALL_ROWS, d), F32)] * 4 + [jax.ShapeDtypeStruct((1, 2 * d), F32)] * 4,
    )(total, *w, *m, *v)


def kernel(x, pre_norm_g, w_in, b_gate, conv_w, conv_b, w_rg_a, b_rg_a, w_rg_x, b_rg_x, lru_lambda, attn_sinks, w_rnn_out, w_attn_out, w_out, post_norm_g, loss_target, m_pre_norm_g, m_w_in, m_b_gate, m_conv_w, m_conv_b, m_w_rg_a, m_b_rg_a, m_w_rg_x, m_b_rg_x, m_lru_lambda, m_attn_sinks, m_w_rnn_out, m_w_attn_out, m_w_out, m_post_norm_g, v_pre_norm_g, v_w_in, v_b_gate, v_conv_w, v_conv_b, v_w_rg_a, v_b_rg_a, v_w_rg_x, v_b_rg_x, v_lru_lambda, v_attn_sinks, v_w_rnn_out, v_w_attn_out, v_w_out, v_post_norm_g):
    cx, cy, cc = _place()
    dev = 4 * cx + 2 * cy + cc
    core = jnp.reshape(cc, (1,)).astype(jnp.int32)

    w_in_t, m_in_t, v_in_t = (jnp.transpose(a[0]) for a in (w_in, m_w_in, v_w_in))
    wt_shard = w_in_t.astype(BF16)

    def project(xs, pre_g):
        h, rx_rg, qkv, ag_ml, wt_all = _gather_project(xs, pre_g, wt_shard, "gather_project")
        return h, rx_rg, qkv, ag_ml, wt_all.reshape(D_IN, D_MODEL)

    def late_unpack(landed):
        w_rnn_all, w_attn_all, w_out_all, cw_all = landed
        return dict(w_rnn=w_rnn_all.reshape(D_RNN, D_MODEL), w_attn=w_attn_all.reshape(D_MODEL, D_MODEL),
                    w_out=w_out_all.reshape(D_MODEL, D_MODEL), cw=jnp.transpose(cw_all, (1, 0, 2)).reshape(4, D_RNN))

    late_weights = (_gather_ride([w_rnn_out[0].astype(BF16), w_attn_out[0].astype(BF16), w_out[0].astype(BF16), conv_w[0]]),
                    late_unpack)

    heads = jnp.arange(1, N_Q_HEADS + 1, dtype=F32)
    slopes = jnp.exp2(-ALIBI_MAX_BIAS * heads / N_Q_HEADS)
    b_a = b_rg_a.reshape(1, D_RNN)
    b_x = b_rg_x.reshape(1, D_RNN)
    p = dict(
        pre_g=pre_norm_g, post_g=post_norm_g, b_gate=b_gate, cb=conv_b,
        wbd_a=_block_diag(w_rg_a[0]), b_a=b_a, wbd_x=_block_diag(w_rg_x[0]), b_x=b_x, lam=lru_lambda,
        sm=jnp.pad(attn_sinks, ((0, 1), (0, 0))) + jnp.pad(slopes[None, :], ((1, 0), (0, 0))))

    flat = (RNN_BLOCKS * RNN_BLOCK_W, RNN_BLOCK_W)

    out_scatter = []

    def out_sibling(gw_rnn, gw_attn, gw_out):
        out_scatter.extend(gw.reshape(N_DEV, SHARD_OUT, D_MODEL) for gw in (gw_rnn, gw_attn, gw_out))
        return _sibling_ride(out_scatter, [])

    def out_chips(from_sibling, g_rg_a, g_rg_x):
        return _join_rides(_chips_ride(_pair_sum_scatter(out_scatter, from_sibling, core, "pair_out"), []),
                           _gather_ride([g_rg_a.reshape(flat), g_rg_x.reshape(flat)]))

    reduce_in = _chips_rest_ride

    g = _local_grads(x[0], loss_target[0], p, project, late_weights, (out_sibling, out_chips), reduce_in)
    small = _allreduce_small(
        _pack_stats(g["st_pre"], g["st_merge"], g["st_rnn"], g["st_post"], g["st_sink"], "pack_stats"), "allreduce_small")

    out = {}
    red = g["red_out"]
    out["w_in"] = [jnp.transpose(o) for o in _adam_parts(g["red_in"][0], w_in_t, m_in_t, v_in_t, "adam_w_in", tr=SHARD_IN // 2)]
    out["w_rnn_out"] = _adam_parts(red[0], w_rnn_out[0], m_w_rnn_out[0], v_w_rnn_out[0], "adam_w_rnn_out")
    out["w_attn_out"] = _adam_parts(red[1], w_attn_out[0], m_w_attn_out[0], v_w_attn_out[0], "adam_w_attn_out")
    out["w_out"] = _adam_parts(red[2], w_out[0], m_w_out[0], v_w_out[0], "adam_w_out")
    out["w_rg_a"] = _adam_parts(red[3], w_rg_a.reshape(flat), m_w_rg_a.reshape(flat), v_w_rg_a.reshape(flat), "adam_w_rg_a", tr=256)
    out["w_rg_x"] = _adam_parts(red[4], w_rg_x.reshape(flat), m_w_rg_x.reshape(flat), v_w_rg_x.reshape(flat), "adam_w_rg_x", tr=256)

    def rows(pre, bg, cbias, ba, bx, lam, post, sinks):
        return (pre, bg, cbias, ba.reshape(1, D_RNN), bx.reshape(1, D_RNN), lam, post,
                jnp.pad(sinks, ((0, 0), (0, D_MODEL - N_Q_HEADS))))

    small_out = _adam_small(
        small,
        rows(pre_norm_g, b_gate, conv_b, b_rg_a, b_rg_x, lru_lambda, post_norm_g, attn_sinks),
        rows(m_pre_norm_g, m_b_gate, m_conv_b, m_b_rg_a, m_b_rg_x, m_lru_lambda, m_post_norm_g, m_attn_sinks),
        rows(v_pre_norm_g, v_b_gate, v_conv_b, v_b_rg_a, v_b_rg_x, v_lru_lambda, v_post_norm_g, v_attn_sinks),
        "adam_small")
    packed, bgate_out = small_out[:4], small_out[4:]
    g_cw = lax.dynamic_slice(packed[0][ROW_CONV_W:ROW_CONV_W + 4], (0, dev * SHARD_OUT), (4, SHARD_OUT))
    out["conv_w"] = _adam_parts(g_cw[None], conv_w[0], m_conv_w[0], v_conv_w[0], "adam_conv_w")

    def unpack(kind, name):
        if name == "b_gate":
            return bgate_out[kind]
        row = dict(pre_norm_g=ROW_PRE_G, conv_b=ROW_CONV_B, b_rg_a=ROW_B_A, b_rg_x=ROW_B_X, lru_lambda=ROW_LAM,
                   post_norm_g=ROW_POST_G, attn_sinks=ROW_SINKS)[name]
        r = packed[kind][row:row + 1]
        if name == "attn_sinks":
            return r[:, 0:N_Q_HEADS]
        if name in ("b_rg_a", "b_rg_x"):
            return r.reshape(1, RNN_BLOCKS, RNN_BLOCK_W)
        return r

    shapes = dict(w_in=(1, D_MODEL, SHARD_IN), w_rnn_out=(1, SHARD_OUT, D_MODEL), w_attn_out=(1, SHARD_OUT, D_MODEL),
                  w_out=(1, SHARD_OUT, D_MODEL), w_rg_a=(1, RNN_BLOCKS, RNN_BLOCK_W, RNN_BLOCK_W),
                  w_rg_x=(1, RNN_BLOCKS, RNN_BLOCK_W, RNN_BLOCK_W), conv_w=(1, 4, SHARD_OUT))
    weights = ["pre_norm_g", "w_in", "b_gate", "conv_w", "conv_b", "w_rg_a", "b_rg_a", "w_rg_x", "b_rg_x",
               "lru_lambda", "attn_sinks", "w_rnn_out", "w_attn_out", "w_out", "post_norm_g"]
    results = []
    for kind in range(4):
        for name in weights:
            if name in out:
                results.append(out[name][kind].reshape(shapes[name]))
            else:
                results.append(unpack(kind, name))
    loss = 0.5 / D_MODEL * jnp.sum(packed[0][ROW_LOSS])
    return (loss, g["grad_x"][None], *results)
```

```python
import functools

import jax
import jax.numpy as jnp
from jax import lax
from jax.experimental import pallas as pl
from jax.experimental.pallas import tpu as pltpu

F32 = jnp.float32
BF16 = jnp.bfloat16

D_MODEL = 1024
D_RNN = 1024
RNN_BLOCKS = 16
RNN_BLOCK_W = 64
LRU_C = 8.0
N_Q_HEADS = 16
HEAD_DIM = 64
D_KV = 256
BLOCK = 128
ALIBI_MAX_BIAS = 8.0
EPS = 1e-6
D_IN = 6656
N_DEV = 8
N_CHIP = 4
SHARD_IN = D_IN // N_DEV
SHARD_OUT = D_MODEL // N_DEV
ATTN_SCALE = HEAD_DIM ** -0.5
MASKED = -1e30

ADAM_LR = 0.001
ADAM_B1 = 0.9
ADAM_B2 = 0.999
ADAM_EPS = 1e-08
ADAM_WD = 0.01
ADAM_STEP = 10

VMEM_LIMIT_BYTES = 52 * 1024 * 1024
LANE = 128
GROUP_W = 256
N_GROUPS = D_RNN // GROUP_W
SEG_CHUNK = 512

NT_DIMS = (((1,), (1,)), ((), ()))
TN_DIMS = (((0,), (0,)), ((), ()))
MESH = pl.DeviceIdType.MESH
ANY = pl.BlockSpec(memory_space=pl.ANY)


def _params(*semantics):
    return pltpu.CompilerParams(dimension_semantics=semantics, vmem_limit_bytes=VMEM_LIMIT_BYTES)


def _sigmoid(x):
    return 0.5 * jnp.tanh(0.5 * x) + 0.5


def _log1p(e):
    u = 1.0 + e
    den = jnp.where(u == 1.0, 1.0, u - 1.0)
    return jnp.where(u == 1.0, e, jnp.log(u) * (e / den))


def _softplus(z):
    return jnp.maximum(z, 0.0) + _log1p(jnp.exp(-jnp.abs(z)))


def _rows8(rows, width):
    idx = lax.broadcasted_iota(jnp.int32, (8, width), 0)
    out = jnp.zeros((8, width), F32)
    for r, v in enumerate(rows):
        out = jnp.where(idx == r, v, out)
    return out


class _Ride:
    def __init__(self, arrays, out_shapes, scratch_shapes, start, finish, middle=None, aliases=None):
        self.arrays, self.out_shapes, self.scratch_shapes = list(arrays), list(out_shapes), list(scratch_shapes)
        self.start, self.finish, self.middle = start, finish, middle
        self.aliases = dict(aliases or {})


class _Hosted:
    def __init__(self, ride, n_in, n_out, n_scratch=0, aliasing=False):
        self.ride = ride
        assert aliasing or not (ride and ride.aliases), "this host does not alias"
        self.aliases = {n_in + i: n_out + o for i, o in ride.aliases.items()} if ride else {}
        self.sizes = (n_in, len(ride.arrays) if ride else 0, n_out, len(ride.out_shapes) if ride else 0, n_scratch)
        self.arrays = ride.arrays if ride else []
        self.in_specs = [ANY] * len(self.arrays)
        self.out_shapes = ride.out_shapes if ride else []
        self.out_specs = [ANY] * len(self.out_shapes)
        self.scratch_shapes = ride.scratch_shapes if ride else []

    def split(self, refs):
        n_in, r_in, n_out, r_out, n_scr = self.sizes
        cuts = [0, n_in, n_in + r_in, n_in + r_in + n_out, n_in + r_in + n_out + r_out, n_in + r_in + n_out + r_out + n_scr]
        host_in, ride_in, host_out, ride_out, host_scr = (refs[cuts[k]:cuts[k + 1]] for k in range(5))
        ride_scr = refs[cuts[5]:]

        def start(when):
            if self.ride is not None:
                pl.when(when)(lambda: self.ride.start(ride_in, ride_out, ride_scr))

        def finish(when):
            if self.ride is not None:
                pl.when(when)(lambda: self.ride.finish(ride_in, ride_out, ride_scr))

        def middle(when):
            if self.ride is not None and self.ride.middle is not None:
                pl.when(when)(lambda: self.ride.middle(ride_in, ride_out, ride_scr))

        start.middle = middle
        return tuple(host_in) + tuple(host_out) + tuple(host_scr), start, finish

    def results(self, outs, n_out):
        outs = list(outs) if isinstance(outs, (list, tuple)) else [outs]
        return outs[:n_out], outs[n_out:]


def _join_rides(a, b):
    na, nb_ = len(a.arrays), len(b.arrays)
    oa = len(a.out_shapes)
    sa = len(a.scratch_shapes)

    def both(fa, fb):
        def run(ins, outs, sems):
            if fa is not None:
                fa(ins[:na], outs[:oa], sems[:sa])
            if fb is not None:
                fb(ins[na:na + nb_], outs[oa:], sems[sa:])
        return run

    middle = both(a.middle, b.middle) if (a.middle or b.middle) else None
    return _Ride(a.arrays + b.arrays, a.out_shapes + b.out_shapes, a.scratch_shapes + b.scratch_shapes,
                 both(a.start, b.start), both(a.finish, b.finish), middle)


def _load_resident(w_hbm, w_vmem, sems, first):
    def piece(c):
        rows = pl.ds(c * SEG_CHUNK, SEG_CHUNK)
        return pltpu.make_async_copy(w_hbm.at[rows], w_vmem.at[rows], sems.at[c])

    @pl.when(first)
    def _():
        for c in range(w_vmem.shape[0] // SEG_CHUNK):
            piece(c).start()

    def ready(c):
        @pl.when(first)
        def _():
            piece(c).wait()

    return ready


CHIP_ROWS = 2 * SHARD_IN
PROJ_WIDTHS = (2 * D_RNN, D_MODEL + 2 * D_KV, 3 * D_MODEL)
PROJ_DTYPES = (F32, BF16, BF16)


def _chip_pieces():
    starts = [0, PROJ_WIDTHS[0], PROJ_WIDTHS[0] + PROJ_WIDTHS[1], D_IN]
    pieces = []
    for k in range(N_CHIP):
        lo, hi = k * CHIP_ROWS, (k + 1) * CHIP_ROWS
        cur = []
        for a in range(len(PROJ_WIDTHS)):
            s0, s1 = max(lo, starts[a]), min(hi, starts[a + 1])
            if s0 < s1:
                cur.append((a, s0 - starts[a], s1 - s0, s0 - lo))
        pieces.append(cur)
    return pieces


def _gather_project(x, g, wt_shard, name, tm=512):
    t, k = x.shape
    tm = min(tm, t)
    nt = t // tm
    pieces = _chip_pieces()

    def body(x_ref, g_ref, shard_ref, h_out, rx_ref, qkv_ref, ag_ref, wt_all,
             w_c, stage, o32, o16, h_all, send_sems, recv_sems, local_sem, stage_sems, out_sems, h_sems):
        s, ti = pl.program_id(0), pl.program_id(1)
        px, py, pc = _place()
        me, sibling = (px, py, pc), (px, py, 1 - pc)
        chips = [(px, py), (1 - px, py), (px, 1 - py), (1 - px, 1 - py)]
        outs = (rx_ref, qkv_ref, ag_ref)

        def slot(dev):
            return wt_all.at[4 * dev[0] + 2 * dev[1] + dev[2]]

        def copy(kk, block, to, src=None):
            return pltpu.make_async_remote_copy(
                src_ref=slot(block) if src is None else src, dst_ref=slot(block),
                send_sem=send_sems.at[kk], recv_sem=recv_sems.at[kk], device_id=to, device_id_type=MESH)

        mine = pltpu.make_async_copy(shard_ref, slot(me), local_sem)
        first = [copy(0, me, sibling, src=shard_ref)] + [copy(1 + j, me, (*chips[1 + j], pc), src=shard_ref) for j in range(3)]
        passed = [copy(4 + j, (*chips[1 + j], pc), sibling) for j in range(3)]

        @pl.when((s == 0) & (ti == 0))
        def _():
            mine.start()
            for cp in first[:3]:
                cp.start()

        for step in range(N_CHIP):
            @pl.when((s == step) & (ti == 0))
            def _(step=step):
                chip = chips[step]
                if step == 0:
                    mine.wait()
                    copy(0, sibling, me).wait_recv()
                else:
                    copy(step, (*chip, pc), me).wait_recv()
                    if step == 1:
                        first[3].start()
                    passed[step - 1].start()
                    copy(3 + step, (*chip, 1 - pc), me).wait_recv()
                loads = [pltpu.make_async_copy(slot((*chip, core)), stage.at[pl.ds(core * SHARD_IN, SHARD_IN)], stage_sems.at[core])
                         for core in (0, 1)]
                for cp in loads:
                    cp.start()
                for cp in loads:
                    cp.wait()
                w_c[...] = stage[...].T

        rows = pl.ds(pl.multiple_of(ti * tm, tm), tm)

        @pl.when(s == 0)
        def _():
            xv = x_ref[...]
            h_all[rows, :] = (xv * lax.rsqrt(jnp.mean(xv * xv, axis=-1, keepdims=True) + EPS) * g_ref[...]).astype(BF16)

        res = jnp.dot(h_all[rows, :], w_c[...], preferred_element_type=F32)

        n = s * nt + ti
        buf = lax.rem(n, 2)
        chip_idx = [2 * cx + cy for cx, cy in chips]

        def chip_at(step):
            return jnp.where(step == 0, chip_idx[0], jnp.where(step == 1, chip_idx[1], jnp.where(step == 2, chip_idx[2], chip_idx[3])))

        def h_write(b, tile):
            return pltpu.make_async_copy(h_all.at[pl.ds(tile * tm, tm)], h_out.at[pl.ds(tile * tm, tm)], h_sems.at[b])

        def writes(kchip, b, tile):
            cps = []
            for idx, (a, col, w, src) in enumerate(pieces[kchip]):
                staged = (o16 if PROJ_DTYPES[a] == BF16 else o32).at[b, :, pl.ds(src, w)]
                cps.append(pltpu.make_async_copy(staged, outs[a].at[pl.ds(tile * tm, tm), pl.ds(col, w)], out_sems.at[b, idx]))
            return cps

        o32[buf] = res
        o16[buf] = res.astype(BF16)
        kcur = chip_at(s)

        @pl.when(n > 0)
        def _():
            kprev = chip_at(lax.div(n - 1, nt))
            for kchip in range(N_CHIP):
                @pl.when(kprev == kchip)
                def _(kchip=kchip):
                    for cp in writes(kchip, 1 - buf, lax.rem(n - 1, nt)):
                        cp.wait()

            @pl.when(n <= nt)
            def _():
                h_write(1 - buf, n - 1).wait()

        @pl.when(s == 0)
        def _():
            h_write(buf, ti).start()

        for kchip in range(N_CHIP):
            @pl.when(kcur == kchip)
            def _(kchip=kchip):
                for cp in writes(kchip, buf, ti):
                    cp.start()

        @pl.when(n == N_CHIP * nt - 1)
        def _():
            for kchip in range(N_CHIP):
                @pl.when(kcur == kchip)
                def _(kchip=kchip):
                    for cp in writes(kchip, buf, ti):
                        cp.wait()
            for cp in first + passed:
                cp.wait_send()

    tile = pl.BlockSpec((tm, k), lambda s, ti: (jnp.where(s == 0, ti, nt - 1), 0))
    return pl.pallas_call(
        body, name=name, grid=(N_CHIP, nt),
        in_specs=[tile, pl.BlockSpec((1, k), lambda s, ti: (0, 0)), ANY],
        out_specs=[ANY, ANY, ANY, ANY, ANY],
        out_shape=[jax.ShapeDtypeStruct((t, k), BF16)]
        + [jax.ShapeDtypeStruct((t, w), dt) for w, dt in zip(PROJ_WIDTHS, PROJ_DTYPES)]
        + [jax.ShapeDtypeStruct((N_DEV, SHARD_IN, k), BF16)],
        scratch_shapes=[pltpu.VMEM((k, CHIP_ROWS), BF16), pltpu.VMEM((CHIP_ROWS, k), BF16),
                        pltpu.VMEM((2, tm, CHIP_ROWS), F32), pltpu.VMEM((2, tm, CHIP_ROWS), BF16), pltpu.VMEM((t, k), BF16),
                        pltpu.SemaphoreType.DMA((7,)), pltpu.SemaphoreType.DMA((7,)), pltpu.SemaphoreType.DMA,
                        pltpu.SemaphoreType.DMA((2,)), pltpu.SemaphoreType.DMA((2, 2)), pltpu.SemaphoreType.DMA((2,))],
        compiler_params=_params("arbitrary", "arbitrary"))(x, g, wt_shard)


def _mm_tn(a, b, name, tm=512, tk=4096):
    ktok, m = a.shape
    n = b.shape[1]
    tk = min(tk, ktok)

    def body(a_ref, b_ref, o_ref):
        @pl.when(pl.program_id(1) == 0)
        def _():
            o_ref[...] = jnp.zeros_like(o_ref)

        o_ref[...] += lax.dot_general(a_ref[...], b_ref[...], TN_DIMS, preferred_element_type=F32)

    return pl.pallas_call(
        body, name=name, grid=(m // tm, ktok // tk),
        in_specs=[pl.BlockSpec((tk, tm), lambda i, kk: (kk, i)), pl.BlockSpec((tk, n), lambda i, kk: (kk, 0))],
        out_specs=pl.BlockSpec((tm, n), lambda i, kk: (i, 0)),
        out_shape=jax.ShapeDtypeStruct((m, n), F32),
        compiler_params=_params("parallel", "arbitrary"))(a, b)


def _segment_chunks(segs):
    bounds = [0]
    for s in segs:
        bounds.append(bounds[-1] + s.shape[1] // SEG_CHUNK)
    return bounds


def _input_grad(segs, wt, x, g, dy, name, tm=512, ride=None):
    m = segs[0].shape[0]
    rows, n = wt.shape
    tm = min(tm, m)
    bounds = _segment_chunks(segs)
    n_seg = len(segs)
    ni = m // tm
    host = _Hosted(ride, n_seg + 4, 2, 2, aliasing=True)

    def body(*refs):
        host_refs, start, finish = host.split(refs)
        a_refs = host_refs[:n_seg]
        wt_hbm, x_ref, g_ref, dy_ref, gx_ref, st_ref, wt_vmem, sems = host_refs[n_seg:]
        i = pl.program_id(0)
        start(i == 0)

        @pl.when(i == 0)
        def _():
            st_ref[...] = jnp.zeros_like(st_ref)

        ready = _load_resident(wt_hbm, wt_vmem, sems, i == 0)
        dh = None
        for s in range(n_seg):
            for c in range(bounds[s], bounds[s + 1]):
                ready(c)
            part = jnp.dot(a_refs[s][...], wt_vmem[bounds[s] * SEG_CHUNK:bounds[s + 1] * SEG_CHUNK, :], preferred_element_type=F32)
            dh = part if dh is None else dh + part
        xv = x_ref[...]
        r = lax.rsqrt(jnp.mean(xv * xv, axis=-1, keepdims=True) + EPS)
        xn = xv * r
        dxn = dh * g_ref[...]
        gx_ref[...] = dy_ref[...] + r * (dxn - xn * jnp.mean(dxn * xn, axis=-1, keepdims=True))
        st_ref[...] += _rows8([jnp.sum(dh * xn, axis=0, keepdims=True)], n)
        finish(i == ni - 1)

    tile = pl.BlockSpec((tm, n), lambda i: (i, 0))
    outs = pl.pallas_call(
        body, name=name, grid=(ni,),
        in_specs=[pl.BlockSpec((tm, sg.shape[1]), lambda i: (i, 0)) for sg in segs]
        + [ANY, tile, pl.BlockSpec((1, n), lambda i: (0, 0)), tile] + host.in_specs,
        out_specs=[tile, pl.BlockSpec((8, n), lambda i: (0, 0))] + host.out_specs,
        out_shape=[jax.ShapeDtypeStruct((m, n), F32), jax.ShapeDtypeStruct((8, n), F32)] + host.out_shapes,
        scratch_shapes=[pltpu.VMEM((rows, n), wt.dtype), pltpu.SemaphoreType.DMA((rows // SEG_CHUNK,))] + host.scratch_shapes,
        input_output_aliases=host.aliases,
        compiler_params=_params("arbitrary"))(*segs, wt, x, g, dy, *host.arrays)
    res, landed = host.results(outs, 2)
    return (*res, landed) if ride else tuple(res)


def _mm_tn_seg(segs, b, name):
    ktok = segs[0].shape[0]
    n = b.shape[1]
    bounds = _segment_chunks(segs)
    n_seg = len(segs)
    nc = bounds[-1]
    seg_of = [s for s in range(n_seg) for _ in range(bounds[s], bounds[s + 1])]

    def body(*refs):
        a_hbm, b_hbm, o_ref = refs[:n_seg], refs[n_seg], refs[n_seg + 1]
        a_buf, b_vmem, a_sems, b_sem = refs[n_seg + 2:]
        c = pl.program_id(0)

        def fetch(cc):
            s = seg_of[cc]
            cols = pl.ds((cc - bounds[s]) * SEG_CHUNK, SEG_CHUNK)
            return pltpu.make_async_copy(a_hbm[s].at[:, cols], a_buf.at[cc % 2], a_sems.at[cc % 2])

        @pl.when(c == 0)
        def _():
            whole = pltpu.make_async_copy(b_hbm, b_vmem, b_sem)
            whole.start()
            fetch(0).start()
            whole.wait()

        for cc in range(nc):
            @pl.when(c == cc)
            def _(cc=cc):
                if cc + 1 < nc:
                    fetch(cc + 1).start()
                fetch(cc).wait()

        o_ref[...] = lax.dot_general(a_buf[c % 2], b_vmem[...], TN_DIMS, preferred_element_type=F32)

    return pl.pallas_call(
        body, name=name, grid=(nc,),
        in_specs=[ANY] * (n_seg + 1), out_specs=pl.BlockSpec((SEG_CHUNK, n), lambda c: (c, 0)),
        out_shape=jax.ShapeDtypeStruct((nc * SEG_CHUNK, n), F32),
        scratch_shapes=[pltpu.VMEM((2, ktok, SEG_CHUNK), segs[0].dtype), pltpu.VMEM((ktok, n), b.dtype),
                        pltpu.SemaphoreType.DMA((2,)), pltpu.SemaphoreType.DMA],
        compiler_params=_params("arbitrary"))(*segs, b)


CHUNK_ORDER = (0, 4, 7, 10, 1, 5, 8, 11, 2, 6, 9, 12, 3)
EARLY_STEPS = 8


def _side_pieces():
    table = []
    for core in (0, 1):
        sides = ([None] * len(CHUNK_ORDER), [None] * len(CHUNK_ORDER))
        for s, cc in enumerate(CHUNK_ORDER):
            g0 = cc * SEG_CHUNK
            for d in range(N_DEV):
                lo, hi = max(g0, d * SHARD_IN), min(g0 + SEG_CHUNK, (d + 1) * SHARD_IN)
                if lo < hi:
                    side = sides[0 if d % 2 == core else 1]
                    assert side[s] is None
                    side[s] = (lo - g0, hi - lo, (d // 2) * SHARD_IN + lo - d * SHARD_IN)
        table.append(sides)
    return table


def _to_chip(src, red_ref, piece, s, send_sems, recv_sems, own, core):
    _, rows, at = piece
    k, r0 = divmod(at, SHARD_IN)
    return k, pltpu.make_async_remote_copy(
        src_ref=src, dst_ref=red_ref.at[own, pl.ds(r0, rows)], send_sem=send_sems.at[s], recv_sem=recv_sems.at[own, s],
        device_id=(k // 2, k % 2, core), device_id_type=MESH)


def _from_chip(src, red_ref, piece, s, send_sems, recv_sems, chip, me):
    _, rows, at = piece
    return pltpu.make_async_remote_copy(
        src_ref=src, dst_ref=red_ref.at[chip, pl.ds(at % SHARD_IN, rows)], send_sem=send_sems.at[s], recv_sem=recv_sems.at[chip, s],
        device_id=me, device_id_type=MESH)


def _mm_tn_seg_pair(segs, b, name):
    ktok = segs[0].shape[0]
    n = b.shape[1]
    bounds = _segment_chunks(segs)
    n_seg = len(segs)
    nc = bounds[-1]
    assert nc == len(CHUNK_ORDER)
    seg_of = [s for s in range(n_seg) for _ in range(bounds[s], bounds[s + 1])]
    table = _side_pieces()

    def body(*refs):
        a_hbm, b_hbm, pair_ref, red_ref = refs[:n_seg], refs[n_seg], refs[n_seg + 1], refs[n_seg + 2]
        (a_buf, b_vmem, res_buf, recv_all, pair_all, a_sems, b_sem, send_sems, recv_sems, out_sems,
         chip_send, chip_recv) = refs[n_seg + 3:]
        step = pl.program_id(0)
        px, py, pc = _place()
        own = 2 * px + py

        def fetch(s):
            sg = seg_of[CHUNK_ORDER[s]]
            cols = pl.ds((CHUNK_ORDER[s] - bounds[sg]) * SEG_CHUNK, SEG_CHUNK)
            return pltpu.make_async_copy(a_hbm[sg].at[:, cols], a_buf.at[s % 2], a_sems.at[s % 2])

        @pl.when(step == 0)
        def _():
            whole = pltpu.make_async_copy(b_hbm, b_vmem, b_sem)
            whole.start()
            fetch(0).start()
            whole.wait()

        for s in range(nc):
            @pl.when(step == s)
            def _(s=s):
                if s + 1 < nc:
                    fetch(s + 1).start()
                fetch(s).wait()

        res_buf[step % 2] = lax.dot_general(a_buf[step % 2], b_vmem[...], TN_DIMS, preferred_element_type=F32)

        def crossing(s, piece):
            off, rows, at = piece
            return pltpu.make_async_remote_copy(
                src_ref=res_buf.at[s % 2, pl.ds(off, rows)], dst_ref=recv_all.at[pl.ds(at, rows)],
                send_sem=send_sems.at[s], recv_sem=recv_sems.at[s], device_id=(px, py, 1 - pc), device_id_type=MESH)

        def write(s, piece):
            _, rows, at = piece
            return pltpu.make_async_copy(pair_all.at[pl.ds(at, rows)], pair_ref.at[pl.ds(at, rows)], out_sems.at[s])

        def to_chip(s, piece, core):
            return _to_chip(pair_all.at[pl.ds(piece[2], piece[1])], red_ref, piece, s, chip_send, chip_recv, own, core)

        for core in (0, 1):
            mine, theirs = table[core]

            def settle(s, mine=mine, theirs=theirs, core=core):
                if theirs[s] is not None:
                    crossing(s, theirs[s]).wait_send()
                if mine[s] is not None:
                    off, rows, at = mine[s]
                    crossing(s, mine[s]).wait_recv()
                    pair_all[at:at + rows] = (res_buf[s % 2, off:off + rows] + recv_all[at:at + rows]).astype(BF16)
                    write(s, mine[s]).start()
                    if s < EARLY_STEPS:
                        k, cp = to_chip(s, mine[s], core)
                        pl.when(own != k)(cp.start)

            for s in range(nc):
                @pl.when((pc == core) & (step == s))
                def _(s=s, settle=settle, mine=mine, theirs=theirs, core=core):
                    if theirs[s] is not None:
                        crossing(s, theirs[s]).start()
                    if s > 0:
                        settle(s - 1)
                    if s == nc - 1:
                        settle(s)
                        kept = [t for t in range(nc) if mine[t] is not None]
                        for t in kept:
                            write(t, mine[t]).wait()
                        early = [t for t in kept if t < EARLY_STEPS]
                        for t in early:
                            k, cp = to_chip(t, mine[t], core)
                            pl.when(own != k)(cp.wait_send)
                        for k in range(N_CHIP):
                            @pl.when(own == k)
                            def _(k=k):
                                for t in early:
                                    if mine[t][2] // SHARD_IN == k:
                                        for chip in range(N_CHIP):
                                            if chip != k:
                                                _from_chip(pair_all.at[pl.ds(mine[t][2], mine[t][1])], red_ref, mine[t], t,
                                                           chip_send, chip_recv, chip, (px, py, pc)).wait_recv()

    flat = jax.ShapeDtypeStruct((N_CHIP * SHARD_IN, n), BF16)
    pair, red = pl.pallas_call(
        body, name=name, grid=(nc,),
        in_specs=[ANY] * (n_seg + 1), out_specs=[ANY, ANY],
        out_shape=[flat, jax.ShapeDtypeStruct((N_CHIP, SHARD_IN, n), BF16)],
        scratch_shapes=[pltpu.VMEM((2, ktok, SEG_CHUNK), segs[0].dtype), pltpu.VMEM((ktok, n), b.dtype),
                        pltpu.VMEM((2, SEG_CHUNK, n), F32), pltpu.VMEM(flat.shape, F32), pltpu.VMEM(flat.shape, BF16),
                        pltpu.SemaphoreType.DMA((2,)), pltpu.SemaphoreType.DMA,
                        pltpu.SemaphoreType.DMA((nc,)), pltpu.SemaphoreType.DMA((nc,)), pltpu.SemaphoreType.DMA((nc,)),
                        pltpu.SemaphoreType.DMA((nc,)), pltpu.SemaphoreType.DMA((N_CHIP, nc))],
        compiler_params=_params("arbitrary"))(*segs, b)
    return pair.reshape(N_CHIP, SHARD_IN, n), red


def _branches_fwd(z_rnn, z_attn, ag_ml, b_gate, w_rnn, w_attn, w_out, x, target, g_post, name, tm=512):
    t, d = x.shape
    tm = min(tm, t)

    def body(zr_ref, za_ref, lr_ref, la_ref, br_ref, ba_ref, wr_ref, wa_ref, wo_ref, x_ref, t_ref, g_ref,
             brr_ref, bra_ref, mg_ref, do_ref, dy_ref, st_ref):
        @pl.when(pl.program_id(0) == 0)
        def _():
            st_ref[...] = jnp.zeros_like(st_ref)

        br_rnn = jnp.dot(zr_ref[...], wr_ref[...], preferred_element_type=F32)
        br_attn = jnp.dot(za_ref[...], wa_ref[...], preferred_element_type=F32)
        brr_ref[...] = br_rnn.astype(BF16)
        bra_ref[...] = br_attn.astype(BF16)
        g_rnn = _sigmoid(lr_ref[...].astype(F32) + br_ref[...])
        g_attn = _sigmoid(la_ref[...].astype(F32) + ba_ref[...])
        merged = (g_rnn * br_rnn + g_attn * br_attn).astype(BF16)
        mg_ref[...] = merged
        o = jnp.dot(merged, wo_ref[...], preferred_element_type=F32)
        g = g_ref[...]
        r = lax.rsqrt(jnp.mean(o * o, axis=-1, keepdims=True) + EPS)
        nrm = o * r
        err = x_ref[...] + nrm * g - t_ref[...]
        dy = err * (1.0 / d)
        dy_ref[...] = dy
        dn = dy * g
        do_ref[...] = (r * (dn - nrm * jnp.mean(dn * nrm, axis=-1, keepdims=True))).astype(BF16)
        st_ref[...] += _rows8([jnp.sum(dy * nrm, axis=0, keepdims=True), jnp.sum(err * err, axis=0, keepdims=True)], d)

    tile = pl.BlockSpec((tm, d), lambda i: (i, 0))
    weight = pl.BlockSpec((d, d), lambda i: (0, 0))
    bf = jax.ShapeDtypeStruct((t, d), BF16)
    return pl.pallas_call(
        body, name=name, grid=(t // tm,),
        in_specs=[tile, tile, pl.BlockSpec((tm, d), lambda i: (i, 1)), pl.BlockSpec((tm, d), lambda i: (i, 2)),
                  pl.BlockSpec((1, d), lambda i: (0, 0)), pl.BlockSpec((1, d), lambda i: (0, 1)),
                  weight, weight, weight, tile, tile, pl.BlockSpec((1, d), lambda i: (0, 0))],
        out_specs=[tile, tile, tile, tile, tile, pl.BlockSpec((8, d), lambda i: (0, 0))],
        out_shape=[bf, bf, bf, bf, jax.ShapeDtypeStruct((t, d), F32), jax.ShapeDtypeStruct((8, d), F32)],
        compiler_params=_params("arbitrary"))(z_rnn, z_attn, ag_ml, ag_ml, b_gate, b_gate, w_rnn, w_attn, w_out, x, target, g_post)


def _branches_bwd(dout, br_rnn, br_attn, ag_ml, b_gate, w_rnn, w_attn, w_out, name, tm=512):
    t, d = br_rnn.shape
    tm = min(tm, t)

    def body(do_ref, r_ref, a_ref, lr_ref, la_ref, br_ref, ba_ref, wr_ref, wa_ref, wo_ref,
             dr_ref, da_ref, dl_ref, dzr_ref, dza_ref, st_ref, wt_ref):
        @pl.when(pl.program_id(0) == 0)
        def _():
            st_ref[...] = jnp.zeros_like(st_ref)
            wt_ref[0] = wo_ref[...].T
            wt_ref[1] = wr_ref[...].T
            wt_ref[2] = wa_ref[...].T

        dm = jnp.dot(do_ref[...], wt_ref[0], preferred_element_type=F32)
        g_rnn = _sigmoid(lr_ref[...].astype(F32) + br_ref[...])
        g_attn = _sigmoid(la_ref[...].astype(F32) + ba_ref[...])
        dbr_rnn = (dm * g_rnn).astype(BF16)
        dbr_attn = (dm * g_attn).astype(BF16)
        dr_ref[...] = dbr_rnn
        da_ref[...] = dbr_attn
        dl_rnn = dm * r_ref[...].astype(F32) * g_rnn * (1.0 - g_rnn)
        dl_attn = dm * a_ref[...].astype(F32) * g_attn * (1.0 - g_attn)
        dl_ref[:, 0:d] = dl_rnn.astype(BF16)
        dl_ref[:, d:2 * d] = dl_attn.astype(BF16)
        st_ref[...] += _rows8([jnp.sum(dl_rnn, axis=0, keepdims=True), jnp.sum(dl_attn, axis=0, keepdims=True)], d)
        dzr_ref[...] = jnp.dot(dbr_rnn, wt_ref[1], preferred_element_type=F32).astype(BF16)
        dza_ref[...] = jnp.dot(dbr_attn, wt_ref[2], preferred_element_type=F32).astype(BF16)

    tile = pl.BlockSpec((tm, d), lambda i: (i, 0))
    weight = pl.BlockSpec((d, d), lambda i: (0, 0))
    bf = jax.ShapeDtypeStruct((t, d), BF16)
    return pl.pallas_call(
        body, name=name, grid=(t // tm,),
        in_specs=[tile, tile, tile, pl.BlockSpec((tm, d), lambda i: (i, 1)), pl.BlockSpec((tm, d), lambda i: (i, 2)),
                  pl.BlockSpec((1, d), lambda i: (0, 0)), pl.BlockSpec((1, d), lambda i: (0, 1)), weight, weight, weight],
        out_specs=[tile, tile, pl.BlockSpec((tm, 2 * d), lambda i: (i, 0)), tile, tile, pl.BlockSpec((8, d), lambda i: (0, 0))],
        out_shape=[bf, bf, jax.ShapeDtypeStruct((t, 2 * d), BF16), bf, bf, jax.ShapeDtypeStruct((8, d), F32)],
        scratch_shapes=[pltpu.VMEM((3, d, d), BF16)],
        compiler_params=_params("arbitrary"))(dout, br_rnn, br_attn, ag_ml, ag_ml, b_gate, b_gate, w_rnn, w_attn, w_out)


def _lru_gates(c, wa, ba, wx, bx, sp):
    cb = c.astype(BF16)
    r = _sigmoid(jnp.dot(cb, wa, preferred_element_type=F32) + ba)
    ig = _sigmoid(jnp.dot(cb, wx, preferred_element_type=F32) + bx)
    log_a = (-LRU_C) * r * sp
    a = jnp.exp(log_a)
    mult = jnp.sqrt(-jnp.tanh(log_a) * (a * a + 1.0))
    return cb, r, ig, a, mult


SUBLANES = 8


def _scan_fwd(a, u, carry, tt):
    w = a.shape[1]
    ng = tt // SUBLANES
    a3 = a.reshape(ng, SUBLANES, w)
    u3 = u.reshape(ng, SUBLANES, w)
    sub = lax.broadcasted_iota(jnp.int32, (ng, SUBLANES, w), 1)
    d = 1
    while d < SUBLANES:
        keep = sub >= d
        u3 = u3 + a3 * jnp.where(keep, pltpu.roll(u3, d, 1), 0.0)
        a3 = a3 * jnp.where(keep, pltpu.roll(a3, d, 1), 1.0)
        d *= 2
    out = []
    for g in range(ng):
        hg = u3[g] + a3[g] * carry
        out.append(hg)
        carry = hg[SUBLANES - 1:SUBLANES, :]
    return jnp.concatenate(out, axis=0)


def _scan_rev(b, g, carry, tt):
    w = b.shape[1]
    ng = tt // SUBLANES
    b3 = b.reshape(ng, SUBLANES, w)
    g3 = g.reshape(ng, SUBLANES, w)
    sub = lax.broadcasted_iota(jnp.int32, (ng, SUBLANES, w), 1)
    d = 1
    while d < SUBLANES:
        keep = sub < SUBLANES - d
        g3 = g3 + b3 * jnp.where(keep, pltpu.roll(g3, SUBLANES - d, 1), 0.0)
        b3 = b3 * jnp.where(keep, pltpu.roll(b3, SUBLANES - d, 1), 1.0)
        d *= 2
    out = [None] * ng
    for k in range(ng - 1, -1, -1):
        hk = g3[k] + b3[k] * carry
        out[k] = hk
        carry = hk[0:1, :]
    return jnp.concatenate(out, axis=0)


def _conv_taps(cw, bias, x, ext_ref, tt):
    x2 = ext_ref[7:7 + tt, :]
    x1 = ext_ref[6:6 + tt, :]
    x0 = ext_ref[5:5 + tt, :]
    c = bias + cw[3:4] * x + cw[2:3] * x2 + cw[1:2] * x1 + cw[0:1] * x0
    return c, x2, x1, x0


def _rnn_fwd(rx_rg, cw, cb, wa, ba, wx, bx, lam, name, tt=512):
    t = rx_rg.shape[0]
    tt = min(tt, t)
    w = GROUP_W

    def body(rx_ref, rg_ref, cw_ref, cb_ref, wa_ref, ba_ref, wx_ref, bx_ref, lam_ref, y_ref, z_ref, ext_ref, hc_ref):
        @pl.when(pl.program_id(1) == 0)
        def _():
            ext_ref[0:8, :] = jnp.zeros((8, w), F32)
            hc_ref[...] = jnp.zeros((8, w), F32)

        x = rx_ref[...]
        ext_ref[8:8 + tt, :] = x
        c, _, _, _ = _conv_taps(cw_ref[...], cb_ref[...], x, ext_ref, tt)
        ext_ref[0:8, :] = x[tt - 8:tt, :]
        sp = _softplus(-lam_ref[...])
        _, _, ig, a, mult = _lru_gates(c, wa_ref[...], ba_ref[...], wx_ref[...], bx_ref[...], sp)
        h = _scan_fwd(a, mult * (ig * c), hc_ref[7:8, :], tt)
        hc_ref[...] = h[tt - 8:tt, :]
        y_ref[...] = h
        rg = rg_ref[...]
        z_ref[...] = (h * rg * _sigmoid(rg)).astype(BF16)

    vec = pl.BlockSpec((1, w), lambda g, i: (0, g))
    mat = pl.BlockSpec((None, w, w), lambda g, i: (g, 0, 0))
    tile = pl.BlockSpec((tt, w), lambda g, i: (i, g))
    return pl.pallas_call(
        body, name=name, grid=(N_GROUPS, t // tt),
        in_specs=[tile, pl.BlockSpec((tt, w), lambda g, i: (i, N_GROUPS + g)),
                  pl.BlockSpec((4, w), lambda g, i: (0, g)), vec, mat, vec, mat, vec, vec],
        out_specs=[tile, tile],
        out_shape=[jax.ShapeDtypeStruct((t, D_RNN), F32), jax.ShapeDtypeStruct((t, D_RNN), BF16)],
        scratch_shapes=[pltpu.VMEM((tt + 8, w), F32), pltpu.VMEM((8, w), F32)],
        compiler_params=_params("parallel", "arbitrary"))(rx_rg, rx_rg, cw, cb, wa, ba, wx, bx, lam)


def _rnn_bwd(rx_rg, y, dz, cw, cb, wa, ba, wx, bx, lam, name, tt=512, ride=None):
    t = rx_rg.shape[0]
    tt = min(tt, t)
    nt = t // tt
    w = GROUP_W

    host = _Hosted(ride, 13, 5, 6)

    def body(*refs):
        host_refs, start, finish = host.split(refs)
        (rx_ref, rg_ref, rxt_ref, y_ref, yt_ref, dz_ref, cw_ref, cb_ref, wa_ref, ba_ref, wx_ref, bx_ref, lam_ref,
         drx_ref, drg_ref, st_ref, gda_ref, gdx_ref, ext_ref, dcx_ref, wcar_ref, acar_ref, dwa_ref, dwx_ref) = host_refs
        ii = pl.program_id(1)
        start((pl.program_id(0) == 0) & (ii == 0))

        @pl.when(ii == 0)
        def _():
            wcar_ref[...] = jnp.zeros((8, w), F32)
            acar_ref[...] = jnp.zeros((8, w), F32)
            dcx_ref[tt:tt + 8, :] = jnp.zeros((8, w), F32)
            st_ref[...] = jnp.zeros_like(st_ref)
            dwa_ref[...] = jnp.zeros_like(dwa_ref)
            dwx_ref[...] = jnp.zeros_like(dwx_ref)

        has_prev = jnp.where(ii == nt - 1, 0.0, 1.0)
        x = rx_ref[...]
        ext_ref[0:8, :] = rxt_ref[...] * has_prev
        ext_ref[8:8 + tt, :] = x
        cwv = cw_ref[...]
        c, x2, x1, x0 = _conv_taps(cwv, cb_ref[...], x, ext_ref, tt)
        lam = lam_ref[...]
        sp = _softplus(-lam)
        wa = wa_ref[...]
        wx = wx_ref[...]
        cb16, r, ig, a, mult = _lru_gates(c, wa, ba_ref[...], wx, bx_ref[...], sp)

        rg = rg_ref[...]
        sg = _sigmoid(rg)
        dz = dz_ref[...].astype(F32)
        yv = y_ref[...]
        drg_ref[...] = (dz * yv * (sg * (1.0 + rg * (1.0 - sg)))).astype(BF16)

        row = lax.broadcasted_iota(jnp.int32, (tt, w), 0)
        b = jnp.where(row < tt - 1, pltpu.roll(a, tt - 1, 0), acar_ref[0:1, :])
        dh = _scan_rev(b, dz * (rg * sg), wcar_ref[0:1, :], tt)
        wcar_ref[...] = dh[0:8, :]
        acar_ref[...] = a[0:8, :]

        hprev = jnp.where(row >= 1, pltpu.roll(yv, 1, 0), yt_ref[7:8, :] * has_prev)
        dmult = dh * (ig * c)
        dig = dh * mult * c
        dlog_a = dh * hprev * a - dmult * (a * a / mult)
        dpa = dlog_a * ((-LRU_C) * sp) * r * (1.0 - r)
        dpx = dig * ig * (1.0 - ig)
        dsp = jnp.sum(dlog_a * r, axis=0, keepdims=True) * (-LRU_C)
        dlam = dsp * (-_sigmoid(-lam))
        dpa16 = dpa.astype(BF16)
        dpx16 = dpx.astype(BF16)
        dwa_ref[...] += lax.dot_general(cb16, dpa16, TN_DIMS, preferred_element_type=F32)
        dwx_ref[...] += lax.dot_general(cb16, dpx16, TN_DIMS, preferred_element_type=F32)
        dc = (dh * mult * ig
              + lax.dot_general(dpa16, wa, NT_DIMS, preferred_element_type=F32)
              + lax.dot_general(dpx16, wx, NT_DIMS, preferred_element_type=F32))

        dcx_ref[0:tt, :] = dc
        drx = (cwv[3:4] * dc + cwv[2:3] * dcx_ref[1:1 + tt, :] + cwv[1:2] * dcx_ref[2:2 + tt, :]
               + cwv[0:1] * dcx_ref[3:3 + tt, :])
        drx_ref[...] = drx.astype(BF16)
        dcx_ref[tt:tt + 8, :] = dc[0:8, :]

        def colsum(v):
            return jnp.sum(v, axis=0, keepdims=True)

        st_ref[...] += _rows8([colsum(dc), colsum(dpa), colsum(dpx), dlam,
                               colsum(dc * x0), colsum(dc * x1), colsum(dc * x2), colsum(dc * x)], w)

        @pl.when(ii == nt - 1)
        def _():
            for blk in range(GROUP_W // RNN_BLOCK_W):
                rows = slice(blk * RNN_BLOCK_W, (blk + 1) * RNN_BLOCK_W)
                gda_ref[blk] = dwa_ref[rows, rows]
                gdx_ref[blk] = dwx_ref[rows, rows]

        finish((pl.program_id(0) == N_GROUPS - 1) & (ii == nt - 1))

    def rev(ii):
        return nt - 1 - ii

    def tail(g, ii):
        return (jnp.maximum(rev(ii) * (tt // 8) - 1, 0), g)

    vec = pl.BlockSpec((1, w), lambda g, ii: (0, g))
    mat = pl.BlockSpec((None, w, w), lambda g, ii: (g, 0, 0))
    tile = pl.BlockSpec((tt, w), lambda g, ii: (rev(ii), g))
    diag_shape = (N_GROUPS, GROUP_W // RNN_BLOCK_W, RNN_BLOCK_W, RNN_BLOCK_W)
    diag = pl.BlockSpec((None,) + diag_shape[1:], lambda g, ii: (g, 0, 0, 0))
    outs = pl.pallas_call(
        body, name=name, grid=(N_GROUPS, nt),
        in_specs=[tile, pl.BlockSpec((tt, w), lambda g, ii: (rev(ii), N_GROUPS + g)), pl.BlockSpec((8, w), tail),
                  tile, pl.BlockSpec((8, w), tail), tile,
                  pl.BlockSpec((4, w), lambda g, ii: (0, g)), vec, mat, vec, mat, vec, vec] + host.in_specs,
        out_specs=[tile, tile, pl.BlockSpec((8, w), lambda g, ii: (0, g)), diag, diag] + host.out_specs,
        out_shape=[jax.ShapeDtypeStruct((t, D_RNN), BF16), jax.ShapeDtypeStruct((t, D_RNN), BF16),
                   jax.ShapeDtypeStruct((8, D_RNN), F32),
                   jax.ShapeDtypeStruct(diag_shape, F32), jax.ShapeDtypeStruct(diag_shape, F32)] + host.out_shapes,
        scratch_shapes=[pltpu.VMEM((tt + 8, w), F32), pltpu.VMEM((tt + 8, w), F32), pltpu.VMEM((8, w), F32),
                        pltpu.VMEM((8, w), F32), pltpu.VMEM((w, w), F32), pltpu.VMEM((w, w), F32)] + host.scratch_shapes,
        compiler_params=_params("arbitrary" if ride else "parallel", "arbitrary"))(
            rx_rg, rx_rg, rx_rg, y, y, dz, cw, cb, wa, ba, wx, bx, lam, *host.arrays)
    res, landed = host.results(outs, 5)
    return (*res, landed) if ride else tuple(res)


def _half_mask(shape, half):
    lane = lax.broadcasted_iota(jnp.int32, shape, 1)
    return (lane >= HEAD_DIM) if half else (lane < HEAD_DIM)


def _dup_half(t, half):
    sel = jnp.where(_half_mask(t.shape, half), t, 0.0)
    return sel + pltpu.roll(sel, HEAD_DIM, 1)


def _band_geometry(n):
    qi = lax.broadcasted_iota(jnp.int32, (BLOCK, 2 * BLOCK), 0)
    kj = lax.broadcasted_iota(jnp.int32, (BLOCK, 2 * BLOCK), 1)
    dist = BLOCK + qi - kj
    first_key = jnp.where(n > 0, 0, BLOCK)
    valid = (dist >= 0) & (dist < BLOCK) & (kj >= first_key)
    return dist.astype(F32), valid


GROUP = 4


def _kv_dup(prev_ref, cur_ref, hk, scale=1.0):
    tile = hk // 2
    kt = jnp.concatenate([prev_ref[:, tile * LANE:(tile + 1) * LANE], cur_ref[:, tile * LANE:(tile + 1) * LANE]], axis=0)
    return (_dup_half(kt.astype(F32), hk % 2) * scale).astype(BF16)


def _fill_bias(bias_ref, sm_ref, n):
    distf, valid = _band_geometry(n)
    for head in range(N_Q_HEADS):
        bias_ref[head] = jnp.where(valid, -sm_ref[1, head] * distf, MASKED)


def _head_scores(q2s, half, kdup, bias):
    qm = jnp.where(_half_mask(q2s.shape, half), q2s, jnp.zeros_like(q2s))
    return qm, lax.dot_general(qm, kdup, NT_DIMS, preferred_element_type=F32) + bias


def _attn_specs(nb, clamp_last):
    def blk(n):
        return jnp.minimum(n, nb - 1) if clamp_last else n

    q_spec = pl.BlockSpec((BLOCK, D_MODEL), lambda n: (blk(n), 0))
    k_prev = pl.BlockSpec((BLOCK, D_KV), lambda n: (jnp.maximum(blk(n) - 1, 0), D_MODEL // D_KV))
    k_cur = pl.BlockSpec((BLOCK, D_KV), lambda n: (blk(n), D_MODEL // D_KV))
    v_prev = pl.BlockSpec((BLOCK, D_KV), lambda n: (jnp.maximum(blk(n) - 1, 0), D_MODEL // D_KV + 1))
    v_cur = pl.BlockSpec((BLOCK, D_KV), lambda n: (blk(n), D_MODEL // D_KV + 1))
    return q_spec, k_prev, k_cur, v_prev, v_cur


def _attn_fwd(sm, qkv, ag_ml, name, ride=None):
    t = qkv.shape[0]
    nb = t // BLOCK

    host = _Hosted(ride, 7, 3, 1)

    def body(*refs):
        (sm_ref, q_ref, kp_ref, kc_ref, vp_ref, vc_ref, ag_ref, y_ref, z_ref, lse_ref, bias_ref), start, finish = host.split(refs)
        n = pl.program_id(0)
        start(n == 0)
        start.middle(n == (3 * nb) // 4)

        @pl.when(n <= 1)
        def _():
            _fill_bias(bias_ref, sm_ref, n)

        lane = lax.broadcasted_iota(jnp.int32, (BLOCK, LANE), 1)
        low = lane < HEAD_DIM
        lse = jnp.zeros((BLOCK, LANE), F32)
        for hk in range(N_Q_HEADS // GROUP):
            kdup = _kv_dup(kp_ref, kc_ref, hk)
            vdup = _kv_dup(vp_ref, vc_ref, hk)
            for k in (0, 1):
                c = slice((2 * hk + k) * LANE, (2 * hk + k + 1) * LANE)
                q2s = q_ref[:, c] * ATTN_SCALE
                outs = []
                for half in (0, 1):
                    head = GROUP * hk + 2 * k + half
                    _, s = _head_scores(q2s, half, kdup, bias_ref[head])
                    sink = sm_ref[0, head]
                    m = jnp.maximum(jnp.max(s, axis=1, keepdims=True), sink)
                    e = jnp.exp(s - m)
                    l = jnp.sum(e, axis=1, keepdims=True) + jnp.exp(sink - m)
                    outs.append(jnp.dot((e * (1.0 / l)).astype(BF16), vdup, preferred_element_type=F32))
                    lse = jnp.where(lane == head, m + jnp.log(l), lse)
                yt = jnp.where(low, outs[0], outs[1])
                y_ref[:, c] = yt
                ag = ag_ref[:, c].astype(F32)
                z_ref[:, c] = (yt * ag * _sigmoid(ag)).astype(BF16)
        lse_ref[...] = lse
        finish(n == nb - 1)

    q_spec, k_prev, k_cur, v_prev, v_cur = _attn_specs(nb, False)
    wide = pl.BlockSpec((BLOCK, D_MODEL), lambda n: (n, 0))
    outs = pl.pallas_call(
        body, name=name, grid=(nb,),
        in_specs=[pl.BlockSpec(memory_space=pltpu.SMEM), q_spec, k_prev, k_cur, v_prev, v_cur, wide] + host.in_specs,
        out_specs=[wide, wide, pl.BlockSpec((BLOCK, LANE), lambda n: (n, 0))] + host.out_specs,
        out_shape=[jax.ShapeDtypeStruct((t, D_MODEL), F32), jax.ShapeDtypeStruct((t, D_MODEL), BF16),
                   jax.ShapeDtypeStruct((t, LANE), F32)] + host.out_shapes,
        scratch_shapes=[pltpu.VMEM((N_Q_HEADS, BLOCK, 2 * BLOCK), F32)] + host.scratch_shapes,
        compiler_params=_params("arbitrary"))(sm, qkv, qkv, qkv, qkv, qkv, ag_ml, *host.arrays)
    res, landed = host.results(outs, 3)
    return (*res, landed) if ride else tuple(res)


def _attn_bwd(sm, qkv, ag_ml, y, lse, dz, name, ride=None):
    t = qkv.shape[0]
    nb = t // BLOCK
    host = _Hosted(ride, 10, 4, 3)

    def body(*refs):
        host_refs, start, finish = host.split(refs)
        (sm_ref, q_ref, kp_ref, kc_ref, vp_ref, vc_ref, ag_ref, y_ref, lse_ref, dz_ref,
         dq_ref, dkv_ref, dag_ref, ds_ref, ck_ref, cv_ref, bias_ref) = host_refs
        n = pl.program_id(0)
        start(n == 0)
        start.middle(n == (3 * nb) // 4)

        @pl.when(n == 0)
        def _():
            ck_ref[...] = jnp.zeros_like(ck_ref)
            cv_ref[...] = jnp.zeros_like(cv_ref)
            ds_ref[...] = jnp.zeros_like(ds_ref)

        @pl.when(n <= 1)
        def _():
            _fill_bias(bias_ref, sm_ref, n)

        @pl.when(n < nb)
        def _():
            lane8 = lax.broadcasted_iota(jnp.int32, (8, LANE), 1)
            row8 = lax.broadcasted_iota(jnp.int32, (8, LANE), 0)
            dsink = jnp.zeros((8, LANE), F32)
            dk_heads, dv_heads = [], []
            lse_tile = lse_ref[...]
            for hk in range(N_Q_HEADS // GROUP):
                kdup = _kv_dup(kp_ref, kc_ref, hk)
                ks = _kv_dup(kp_ref, kc_ref, hk, ATTN_SCALE)
                vdup = _kv_dup(vp_ref, vc_ref, hk)
                qms, dyhs, y_rows = [], [], []
                for k in (0, 1):
                    c = slice((2 * hk + k) * LANE, (2 * hk + k + 1) * LANE)
                    ag = ag_ref[:, c].astype(F32)
                    sg = _sigmoid(ag)
                    dzt = dz_ref[:, c].astype(F32)
                    yt = y_ref[:, c]
                    dag_ref[:, c] = (dzt * yt * (sg * (1.0 + ag * (1.0 - sg)))).astype(BF16)
                    dyt = dzt * (ag * sg)
                    q2s = q_ref[:, c] * ATTN_SCALE
                    for half in (0, 1):
                        hm = _half_mask(q2s.shape, half)
                        qms.append(jnp.where(hm, q2s, jnp.zeros_like(q2s)))
                        dyhs.append(jnp.where(hm, dyt, 0.0))
                        y_rows.append(yt)
                qm4 = jnp.concatenate(qms, axis=0)
                dy4 = jnp.concatenate(dyhs, axis=0)
                dy4_16 = dy4.astype(BF16)
                s4 = lax.dot_general(qm4, kdup, NT_DIMS, preferred_element_type=F32)
                dp4 = lax.dot_general(dy4_16, vdup, NT_DIMS, preferred_element_type=F32)
                probs16, ds16 = [], []
                for r in range(GROUP):
                    head = GROUP * hk + r
                    rows = slice(r * BLOCK, (r + 1) * BLOCK)
                    lh = lse_tile[:, head:head + 1]
                    probs = jnp.exp(s4[rows] + bias_ref[head] - lh)
                    psink = jnp.exp(sm_ref[0, head] - lh)
                    delta = jnp.sum(dyhs[r] * y_rows[r], axis=1, keepdims=True)
                    ds16.append((probs * (dp4[rows] - delta)).astype(BF16))
                    probs16.append(probs.astype(BF16))
                    dsink = dsink + jnp.where((row8 == 0) & (lane8 == head),
                                              -jnp.sum(psink * delta, axis=0, keepdims=True), 0.0)
                ds4 = jnp.concatenate(ds16, axis=0)
                p4 = jnp.concatenate(probs16, axis=0)
                dq4 = jnp.dot(ds4, ks, preferred_element_type=F32)
                low = _half_mask((BLOCK, LANE), 0)
                for k in (0, 1):
                    c = slice((2 * hk + k) * LANE, (2 * hk + k + 1) * LANE)
                    dq_ref[:, c] = jnp.where(low, dq4[2 * k * BLOCK:(2 * k + 1) * BLOCK],
                                             dq4[(2 * k + 1) * BLOCK:(2 * k + 2) * BLOCK]).astype(BF16)
                dk_acc = lax.dot_general(ds4, qm4, TN_DIMS, preferred_element_type=F32)
                dv_acc = lax.dot_general(p4, dy4_16, TN_DIMS, preferred_element_type=F32)
                dk_heads.append(dk_acc + pltpu.roll(dk_acc, HEAD_DIM, 1))
                dv_heads.append(dv_acc + pltpu.roll(dv_acc, HEAD_DIM, 1))
            ds_ref[...] += dsink
            low = _half_mask((2 * BLOCK, LANE), 0)
            for tile in range(2):
                cols = slice(tile * LANE, (tile + 1) * LANE)
                dkt = jnp.where(low, dk_heads[2 * tile], dk_heads[2 * tile + 1])
                dvt = jnp.where(low, dv_heads[2 * tile], dv_heads[2 * tile + 1])
                dkv_ref[:, cols] = (ck_ref[:, cols] + dkt[0:BLOCK, :]).astype(BF16)
                dkv_ref[:, D_KV + tile * LANE:D_KV + (tile + 1) * LANE] = (cv_ref[:, cols] + dvt[0:BLOCK, :]).astype(BF16)
                ck_ref[:, cols] = dkt[BLOCK:2 * BLOCK, :]
                cv_ref[:, cols] = dvt[BLOCK:2 * BLOCK, :]

        @pl.when(n == nb)
        def _():
            dkv_ref[:, 0:D_KV] = ck_ref[...].astype(BF16)
            dkv_ref[:, D_KV:2 * D_KV] = cv_ref[...].astype(BF16)

        finish(n == nb)

    q_spec, k_prev, k_cur, v_prev, v_cur = _attn_specs(nb, True)
    wide = pl.BlockSpec((BLOCK, D_MODEL), lambda n: (jnp.minimum(n, nb - 1), 0))
    outs = pl.pallas_call(
        body, name=name, grid=(nb + 1,),
        in_specs=[pl.BlockSpec(memory_space=pltpu.SMEM), q_spec, k_prev, k_cur, v_prev, v_cur, wide, wide,
                  pl.BlockSpec((BLOCK, LANE), lambda n: (jnp.minimum(n, nb - 1), 0)), wide] + host.in_specs,
        out_specs=[wide, pl.BlockSpec((BLOCK, 2 * D_KV), lambda n: (jnp.maximum(n - 1, 0), 0)), wide,
                   pl.BlockSpec((8, LANE), lambda n: (0, 0))] + host.out_specs,
        out_shape=[jax.ShapeDtypeStruct((t, D_MODEL), BF16), jax.ShapeDtypeStruct((t, 2 * D_KV), BF16),
                   jax.ShapeDtypeStruct((t, D_MODEL), BF16), jax.ShapeDtypeStruct((8, LANE), F32)] + host.out_shapes,
        scratch_shapes=[pltpu.VMEM((BLOCK, D_KV), F32), pltpu.VMEM((BLOCK, D_KV), F32),
                        pltpu.VMEM((N_Q_HEADS, BLOCK, 2 * BLOCK), F32)] + host.scratch_shapes,
        compiler_params=_params("arbitrary"))(sm, qkv, qkv, qkv, qkv, qkv, ag_ml, y, lse, dz, *host.arrays)
    res, landed = host.results(outs, 4)
    return (*res, landed) if ride else tuple(res)


def _local_grads(x, target, p, project, late_weights=None, reduce_out=None, reduce_in=None):
    h, rx_rg, qkv, ag_ml, wt = project(x, p["pre_g"])
    if late_weights is None:
        y_attn, z_attn, lse = _attn_fwd(p["sm"], qkv, ag_ml, "attn_fwd")
    else:
        y_attn, z_attn, lse, landed = _attn_fwd(p["sm"], qkv, ag_ml, "attn_fwd", ride=late_weights[0])
        p = {**p, **late_weights[1](landed)}
    lru = (p["cw"], p["cb"], p["wbd_a"], p["b_a"], p["wbd_x"], p["b_x"], p["lam"])
    y_rnn, z_rnn = _rnn_fwd(rx_rg, *lru, "rnn_fwd")
    br_rnn, br_attn, merged, dout, dy, st_post = _branches_fwd(
        z_rnn, z_attn, ag_ml, p["b_gate"], p["w_rnn"], p["w_attn"], p["w_out"], x, target, p["post_g"], "branches_fwd")

    dbr_rnn, dbr_attn, d_ml, dz_rnn, dz_attn, st_merge = _branches_bwd(
        dout, br_rnn, br_attn, ag_ml, p["b_gate"], p["w_rnn"], p["w_attn"], p["w_out"], "branches_bwd")
    gw_out = _mm_tn(merged, dout, "gw_out")
    gw_rnn = _mm_tn(z_rnn, dbr_rnn, "gw_rnn")
    gw_attn = _mm_tn(z_attn, dbr_attn, "gw_attn")
    red_out = red_in = None
    if reduce_out is None:
        d_rx, d_rg, st_rnn, g_rg_a, g_rg_x = _rnn_bwd(rx_rg, y_rnn, dz_rnn, *lru, "rnn_bwd")
        dq, dkv, d_ag, st_sink = _attn_bwd(p["sm"], qkv, ag_ml, y_attn, lse, dz_attn, "attn_bwd")
    else:
        d_rx, d_rg, st_rnn, g_rg_a, g_rg_x, from_sibling = _rnn_bwd(
            rx_rg, y_rnn, dz_rnn, *lru, "rnn_bwd", ride=reduce_out[0](gw_rnn, gw_attn, gw_out))
        dq, dkv, d_ag, st_sink, red_out = _attn_bwd(p["sm"], qkv, ag_ml, y_attn, lse, dz_attn, "attn_bwd",
                                                    ride=reduce_out[1](from_sibling, g_rg_a, g_rg_x))

    segs = [d_rx, d_rg, dq, dkv, d_ag, d_ml]
    if reduce_in is None:
        gwt = _mm_tn_seg(segs, h, "gw_in")
        grad_x, st_pre = _input_grad(segs, wt, x, p["pre_g"], dy, "input_grad")
    else:
        gwt = None
        grad_x, st_pre, red_in = _input_grad(segs, wt, x, p["pre_g"], dy, "input_grad",
                                             ride=reduce_in(*_mm_tn_seg_pair(segs, h, "gw_in")))
    return dict(grad_x=grad_x, gwt=gwt, gw_rnn=gw_rnn, gw_attn=gw_attn, gw_out=gw_out,
                st_post=st_post, st_merge=st_merge, st_rnn=st_rnn, st_sink=st_sink, st_pre=st_pre,
                g_rg_a=g_rg_a, g_rg_x=g_rg_x, red_out=red_out, red_in=red_in)


def _place():
    x, y, c = lax.axis_index("x"), lax.axis_index("y"), lax.axis_index("c")
    return x, y, c


def _gather_ride(shards):
    n = len(shards)

    def copies(ins, outs, sems):
        send_sems, recv_sems, local_sems = sems
        x, y, c = _place()
        me, sibling = (x, y, c), (x, y, 1 - c)
        chips = [(1 - x, y), (x, 1 - y), (1 - x, 1 - y)]

        def slot(a, dev):
            return outs[a].at[4 * dev[0] + 2 * dev[1] + dev[2]]

        def copy(a, k, block, to, src=None):
            return pltpu.make_async_remote_copy(
                src_ref=slot(a, block) if src is None else src, dst_ref=slot(a, block),
                send_sem=send_sems.at[a, k], recv_sem=recv_sems.at[a, k], device_id=to, device_id_type=MESH)

        mine = [pltpu.make_async_copy(ins[a], slot(a, me), local_sems.at[a]) for a in range(n)]
        first = []
        for a in range(n):
            first.append(copy(a, 0, me, sibling, src=ins[a]))
            first += [copy(a, 1 + j, me, (*chip, c), src=ins[a]) for j, chip in enumerate(chips)]
        return me, sibling, chips, c, copy, mine, first

    def start(ins, outs, sems):
        *_, mine, first = copies(ins, outs, sems)
        for cp in mine + first:
            cp.start()

    def middle(ins, outs, sems):
        me, sibling, chips, c, copy, _, _ = copies(ins, outs, sems)
        for j, chip in enumerate(chips):
            for a in range(n):
                copy(a, 1 + j, (*chip, c), me).wait_recv()
                copy(a, 4 + j, (*chip, c), sibling).start()

    def finish(ins, outs, sems):
        me, sibling, chips, c, copy, mine, first = copies(ins, outs, sems)
        passed = [copy(a, 4 + j, (*chip, c), sibling) for j, chip in enumerate(chips) for a in range(n)]
        for a in range(n):
            copy(a, 0, sibling, me).wait_recv()
            for j, chip in enumerate(chips):
                copy(a, 4 + j, (*chip, 1 - c), me).wait_recv()
        for cp in first + passed:
            cp.wait_send()
        for cp in mine:
            cp.wait()

    return _Ride(
        shards, [jax.ShapeDtypeStruct((N_DEV, *s.shape), s.dtype) for s in shards],
        [pltpu.SemaphoreType.DMA((n, 7)), pltpu.SemaphoreType.DMA((n, 7)), pltpu.SemaphoreType.DMA((n,))],
        start, finish, middle)


def _sibling_ride(scatter, whole):
    ns, nw = len(scatter), len(whole)

    def copies(ins, outs, sems):
        send_sems, recv_sems = sems
        x, y, c = _place()
        cps = [pltpu.make_async_remote_copy(
            src_ref=ins[a].at[2 * chip + (1 - c)], dst_ref=outs[a].at[chip],
            send_sem=send_sems.at[a * N_CHIP + chip], recv_sem=recv_sems.at[a * N_CHIP + chip],
            device_id=(x, y, 1 - c), device_id_type=MESH) for a in range(ns) for chip in range(N_CHIP)]
        cps += [pltpu.make_async_remote_copy(
            src_ref=ins[ns + a], dst_ref=outs[ns + a],
            send_sem=send_sems.at[ns * N_CHIP + a], recv_sem=recv_sems.at[ns * N_CHIP + a],
            device_id=(x, y, 1 - c), device_id_type=MESH) for a in range(nw)]
        return cps

    def start(ins, outs, sems):
        for cp in copies(ins, outs, sems):
            cp.start()

    def finish(ins, outs, sems):
        for cp in copies(ins, outs, sems):
            cp.wait()

    n_sem = ns * N_CHIP + nw
    return _Ride(
        list(scatter) + list(whole),
        [jax.ShapeDtypeStruct((N_CHIP, *s.shape[1:]), s.dtype) for s in scatter]
        + [jax.ShapeDtypeStruct(s.shape, s.dtype) for s in whole],
        [pltpu.SemaphoreType.DMA((n_sem,)), pltpu.SemaphoreType.DMA((n_sem,))], start, finish)


def _chips_ride(scatter, whole):
    ns, nw = len(scatter), len(whole)
    n = ns + nw

    def copies(ins, outs, sems):
        send_sems, recv_sems, local_sems = sems
        x, y, c = _place()
        own = 2 * x + y
        chips = [(1 - x, y), (x, 1 - y), (1 - x, 1 - y)]

        def src(a, chip_idx):
            return ins[a].at[chip_idx] if a < ns else ins[a]

        local = [pltpu.make_async_copy(src(a, own), outs[a].at[own], local_sems.at[a]) for a in range(n)]
        sent = [pltpu.make_async_remote_copy(
            src_ref=src(a, 2 * chip[0] + chip[1]), dst_ref=outs[a].at[own],
            send_sem=send_sems.at[a, j], recv_sem=recv_sems.at[a, own], device_id=(*chip, c), device_id_type=MESH)
            for a in range(n) for j, chip in enumerate(chips)]
        return chips, c, local, sent

    def start(ins, outs, sems):
        _, _, local, sent = copies(ins, outs, sems)
        for cp in local + sent:
            cp.start()

    def finish(ins, outs, sems):
        send_sems, recv_sems, _ = sems
        chips, c, local, sent = copies(ins, outs, sems)
        for a in range(n):
            for chip in chips:
                k = 2 * chip[0] + chip[1]
                pltpu.make_async_remote_copy(
                    src_ref=outs[a].at[k], dst_ref=outs[a].at[k], send_sem=send_sems.at[a, 0],
                    recv_sem=recv_sems.at[a, k], device_id=(*chip, c), device_id_type=MESH).wait_recv()
        for cp in sent:
            cp.wait_send()
        for cp in local:
            cp.wait()

    return _Ride(
        list(scatter) + list(whole),
        [jax.ShapeDtypeStruct(s.shape, s.dtype) for s in scatter]
        + [jax.ShapeDtypeStruct((N_CHIP, *s.shape), s.dtype) for s in whole],
        [pltpu.SemaphoreType.DMA((n, 3)), pltpu.SemaphoreType.DMA((n, N_CHIP)), pltpu.SemaphoreType.DMA((n,))],
        start, finish)


def _chips_rest_ride(pair, red):
    table = _side_pieces()
    nc = len(CHUNK_ORDER)

    def each(ins, outs, sems, sending, landing):
        send_sems, recv_sems, _ = sems
        pair_ref, red_ref = ins[0], outs[0]
        x, y, c = _place()
        own = 2 * x + y
        for core in (0, 1):
            mine = table[core][0]
            rest = [s for s in range(EARLY_STEPS, nc) if mine[s] is not None]

            @pl.when(c == core)
            def _(mine=mine, rest=rest, core=core):
                for s in rest:
                    _, rows, at = mine[s]
                    k, cp = _to_chip(pair_ref.at[at // SHARD_IN, pl.ds(at % SHARD_IN, rows)], red_ref, mine[s], s,
                                     send_sems, recv_sems, own, core)
                    pl.when(own != k)(lambda cp=cp: sending(cp))
                if landing is not None:
                    for k in range(N_CHIP):
                        @pl.when(own == k)
                        def _(k=k):
                            for s in rest:
                                _, rows, at = mine[s]
                                if at // SHARD_IN == k:
                                    for chip in range(N_CHIP):
                                        if chip != k:
                                            landing(_from_chip(pair_ref.at[k, pl.ds(at % SHARD_IN, rows)], red_ref, mine[s], s,
                                                               send_sems, recv_sems, chip, (x, y, c)))

    def local(ins, outs, sems):
        x, y, _ = _place()
        return pltpu.make_async_copy(ins[0].at[2 * x + y], outs[0].at[2 * x + y], sems[2])

    def start(ins, outs, sems):
        local(ins, outs, sems).start()
        each(ins, outs, sems, lambda cp: cp.start(), None)

    def finish(ins, outs, sems):
        each(ins, outs, sems, lambda cp: cp.wait_send(), lambda cp: cp.wait_recv())
        local(ins, outs, sems).wait()

    return _Ride([pair, red], [jax.ShapeDtypeStruct(red.shape, red.dtype)],
                 [pltpu.SemaphoreType.DMA((nc,)), pltpu.SemaphoreType.DMA((N_CHIP, nc)), pltpu.SemaphoreType.DMA],
                 start, finish, aliases={1: 0})


def _pair_sum_scatter(parts, recvs, core, name):
    na = len(parts)
    _, r, cdim = parts[0].shape
    tr = min(r, 416 if r % 416 == 0 else 128)

    def body(core_ref, *refs):
        del core_ref
        for a in range(na):
            refs[2 * na + a][...] = (refs[a][...] + refs[na + a][...]).astype(BF16)

    blk = (None, tr, cdim)
    mine = pl.BlockSpec(blk, lambda k, i, core_ref: (2 * k + core_ref[0], i, 0))
    slot = pl.BlockSpec(blk, lambda k, i, core_ref: (k, i, 0))
    return pl.pallas_call(
        body, name=name,
        grid_spec=pltpu.PrefetchScalarGridSpec(
            num_scalar_prefetch=1, grid=(N_CHIP, r // tr),
            in_specs=[mine] * na + [slot] * na, out_specs=[slot] * na),
        out_shape=[jax.ShapeDtypeStruct((N_CHIP, r, cdim), BF16)] * na,
        compiler_params=_params("parallel", "parallel"))(core, *parts, *recvs)


def _allreduce_small(pack, name):
    shape = pack.shape

    def body(x_ref, o_ref, sib_ref, chip_ref, send_sems, recv_sems):
        x, y, c = _place()
        own = 2 * x + y
        chips = [(1 - x, y), (x, 1 - y), (1 - x, 1 - y)]
        to_sibling = pltpu.make_async_remote_copy(
            src_ref=x_ref, dst_ref=sib_ref, send_sem=send_sems.at[0], recv_sem=recv_sems.at[0],
            device_id=(x, y, 1 - c), device_id_type=MESH)
        to_sibling.start()
        to_sibling.wait()
        chip_ref[own] = x_ref[...] + sib_ref[...]
        sent = [pltpu.make_async_remote_copy(
            src_ref=chip_ref.at[own], dst_ref=chip_ref.at[own], send_sem=send_sems.at[1 + j],
            recv_sem=recv_sems.at[1 + own], device_id=(*chip, c), device_id_type=MESH) for j, chip in enumerate(chips)]
        for cp in sent:
            cp.start()
        for chip in chips:
            k = 2 * chip[0] + chip[1]
            pltpu.make_async_remote_copy(
                src_ref=chip_ref.at[k], dst_ref=chip_ref.at[k], send_sem=send_sems.at[1],
                recv_sem=recv_sems.at[1 + k], device_id=(*chip, c), device_id_type=MESH).wait_recv()
        for cp in sent:
            cp.wait_send()
        o_ref[...] = (chip_ref[0] + chip_ref[1]) + (chip_ref[2] + chip_ref[3])

    return pl.pallas_call(
        body, name=name, out_shape=jax.ShapeDtypeStruct(shape, F32),
        in_specs=[pl.BlockSpec(memory_space=pltpu.VMEM)], out_specs=pl.BlockSpec(memory_space=pltpu.VMEM),
        scratch_shapes=[pltpu.VMEM(shape, F32), pltpu.VMEM((N_CHIP, *shape), F32),
                        pltpu.SemaphoreType.DMA((4,)), pltpu.SemaphoreType.DMA((1 + N_CHIP,))],
    )(pack)


def _adamw(g, w, m, v):
    m = ADAM_B1 * m + (1.0 - ADAM_B1) * g
    v = ADAM_B2 * v + (1.0 - ADAM_B2) * (g * g)
    m_hat = m / (1.0 - ADAM_B1 ** ADAM_STEP)
    v_hat = v / (1.0 - ADAM_B2 ** ADAM_STEP)
    delta = -ADAM_LR * (m_hat / (jnp.sqrt(v_hat) + ADAM_EPS) + ADAM_WD * w)
    return delta, m, v


def _adam_parts(parts, w, m, v, name, tr=None):
    npart, r, c = parts.shape
    tr = r if tr is None else min(tr, r)

    def body(p_ref, w_ref, m_ref, v_ref, g_ref, d_ref, nm_ref, nv_ref):
        g = p_ref[0].astype(F32)
        for k in range(1, npart):
            g = g + p_ref[k].astype(F32)
        g_ref[...] = g
        d_ref[...], nm_ref[...], nv_ref[...] = _adamw(g, w_ref[...], m_ref[...], v_ref[...])

    tile = pl.BlockSpec((tr, c), lambda i: (i, 0))
    return pl.pallas_call(
        body, name=name, grid=(r // tr,),
        in_specs=[pl.BlockSpec((npart, tr, c), lambda i: (0, i, 0)), tile, tile, tile],
        out_specs=[tile] * 4, out_shape=[jax.ShapeDtypeStruct((r, c), F32)] * 4,
        compiler_params=_params("parallel"))(parts, w, m, v)


def _block_diag(w):
    w4 = w.reshape(N_GROUPS, 4, RNN_BLOCK_W, RNN_BLOCK_W)
    eye = jnp.eye(4, dtype=w.dtype)
    return jnp.einsum("gbij,bc->gbicj", w4, eye).reshape(N_GROUPS, GROUP_W, GROUP_W).astype(BF16)


SMALL_ROWS = 16
ROW_PRE_G, ROW_BGATE, ROW_CONV_B, ROW_B_A, ROW_B_X, ROW_LAM, ROW_POST_G, ROW_LOSS, ROW_SINKS, ROW_CONV_W = 0, 1, 3, 4, 5, 6, 7, 8, 9, 10


def _pack_stats(st_pre, st_merge, st_rnn, st_post, st_sink, name):
    d = D_MODEL

    def body(pre_ref, mg_ref, rnn_ref, post_ref, sink_ref, o_ref):
        rnn = rnn_ref[...]
        sinks = jnp.concatenate([sink_ref[0:1, :], jnp.zeros((1, d - LANE), F32)], axis=1)
        o_ref[0:8, :] = _rows8([pre_ref[0:1, :], mg_ref[0:1, :], mg_ref[1:2, :], rnn[0:1], rnn[1:2], rnn[2:3], rnn[3:4],
                                post_ref[0:1, :]], d)
        o_ref[8:16, :] = _rows8([post_ref[1:2, :], sinks, rnn[4:5], rnn[5:6], rnn[6:7], rnn[7:8]], d)

    return pl.pallas_call(body, name=name, out_shape=jax.ShapeDtypeStruct((SMALL_ROWS, d), F32))(
        st_pre, st_merge, st_rnn, st_post, st_sink)


def _adam_small(total, w, m, v, name):
    d = D_MODEL
    n_in = len(w)

    def pack(refs):
        pre, bg, cbias, ba, bx, lam, post, sinks = [r[...] for r in refs]
        top = _rows8([pre, bg[:, 0:d], bg[:, d:2 * d], cbias, ba, bx, lam, post], d)
        return jnp.concatenate([top, _rows8([jnp.zeros((1, d), F32), sinks], d)], axis=0)

    def body(*refs):
        p_ref = refs[0]
        w_refs, m_refs, v_refs = (refs[1 + k * n_in:1 + (k + 1) * n_in] for k in range(3))
        outs = refs[1 + 3 * n_in:]
        g = p_ref[...]
        res = (g,) + _adamw(g, pack(w_refs), pack(m_refs), pack(v_refs))
        for k in range(4):
            outs[k][...] = res[k]
            outs[4 + k][...] = jnp.concatenate([res[k][ROW_BGATE:ROW_BGATE + 1], res[k][ROW_BGATE + 1:ROW_BGATE + 2]], axis=1)

    return pl.pallas_call(
        body, name=name,
        out_shape=[jax.ShapeDtypeStruct((SMALL_ROWS, d), F32)] * 4 + [jax.ShapeDtypeStruct((1, 2 * d), F32)] * 4,
    )(total, *w, *m, *v)


def kernel(x, pre_norm_g, w_in, b_gate, conv_w, conv_b, w_rg_a, b_rg_a, w_rg_x, b_rg_x, lru_lambda, attn_sinks, w_rnn_out, w_attn_out, w_out, post_norm_g, loss_target, m_pre_norm_g, m_w_in, m_b_gate, m_conv_w, m_conv_b, m_w_rg_a, m_b_rg_a, m_w_rg_x, m_b_rg_x, m_lru_lambda, m_attn_sinks, m_w_rnn_out, m_w_attn_out, m_w_out, m_post_norm_g, v_pre_norm_g, v_w_in, v_b_gate, v_conv_w, v_conv_b, v_w_rg_a, v_b_rg_a, v_w_rg_x, v_b_rg_x, v_lru_lambda, v_attn_sinks, v_w_rnn_out, v_w_attn_out, v_w_out, v_post_norm_g):
    cx, cy, cc = _place()
    dev = 4 * cx + 2 * cy + cc
    core = jnp.reshape(cc, (1,)).astype(jnp.int32)

    w_in_t, m_in_t, v_in_t = (jnp.transpose(a[0]) for a in (w_in, m_w_in, v_w_in))
    wt_shard = w_in_t.astype(BF16)

    def project(xs, pre_g):
        h, rx_rg, qkv, ag_ml, wt_all = _gather_project(xs, pre_g, wt_shard, "gather_project")
        return h, rx_rg, qkv, ag_ml, wt_all.reshape(D_IN, D_MODEL)

    def late_unpack(landed):
        w_rnn_all, w_attn_all, w_out_all, cw_all = landed
        return dict(w_rnn=w_rnn_all.reshape(D_RNN, D_MODEL), w_attn=w_attn_all.reshape(D_MODEL, D_MODEL),
                    w_out=w_out_all.reshape(D_MODEL, D_MODEL), cw=jnp.transpose(cw_all, (1, 0, 2)).reshape(4, D_RNN))

    late_weights = (_gather_ride([w_rnn_out[0].astype(BF16), w_attn_out[0].astype(BF16), w_out[0].astype(BF16), conv_w[0]]),
                    late_unpack)

    heads = jnp.arange(1, N_Q_HEADS + 1, dtype=F32)
    slopes = jnp.exp2(-ALIBI_MAX_BIAS * heads / N_Q_HEADS)
    b_a = b_rg_a.reshape(1, D_RNN)
    b_x = b_rg_x.reshape(1, D_RNN)
    p = dict(
        pre_g=pre_norm_g, post_g=post_norm_g, b_gate=b_gate, cb=conv_b,
        wbd_a=_block_diag(w_rg_a[0]), b_a=b_a, wbd_x=_block_diag(w_rg_x[0]), b_x=b_x, lam=lru_lambda,
        sm=jnp.pad(attn_sinks, ((0, 1), (0, 0))) + jnp.pad(slopes[None, :], ((1, 0), (0, 0))))

    flat = (RNN_BLOCKS * RNN_BLOCK_W, RNN_BLOCK_W)

    out_scatter = []

    def out_sibling(gw_rnn, gw_attn, gw_out):
        out_scatter.extend(gw.reshape(N_DEV, SHARD_OUT, D_MODEL) for gw in (gw_rnn, gw_attn, gw_out))
        return _sibling_ride(out_scatter, [])

    def out_chips(from_sibling, g_rg_a, g_rg_x):
        return _join_rides(_chips_ride(_pair_sum_scatter(out_scatter, from_sibling, core, "pair_out"), []),
                           _gather_ride([g_rg_a.reshape(flat), g_rg_x.reshape(flat)]))

    reduce_in = _chips_rest_ride

    g = _local_grads(x[0], loss_target[0], p, project, late_weights, (out_sibling, out_chips), reduce_in)
    small = _allreduce_small(
        _pack_stats(g["st_pre"], g["st_merge"], g["st_rnn"], g["st_post"], g["st_sink"], "pack_stats"), "allreduce_small")

    out = {}
    red = g["red_out"]
    out["w_in"] = [jnp.transpose(o) for o in _adam_parts(g["red_in"][0], w_in_t, m_in_t, v_in_t, "adam_w_in", tr=SHARD_IN // 2)]
    out["w_rnn_out"] = _adam_parts(red[0], w_rnn_out[0], m_w_rnn_out[0], v_w_rnn_out[0], "adam_w_rnn_out")
    out["w_attn_out"] = _adam_parts(red[1], w_attn_out[0], m_w_attn_out[0], v_w_attn_out[0], "adam_w_attn_out")
    out["w_out"] = _adam_parts(red[2], w_out[0], m_w_out[0], v_w_out[0], "adam_w_out")
    out["w_rg_a"] = _adam_parts(red[3], w_rg_a.reshape(flat), m_w_rg_a.reshape(flat), v_w_rg_a.reshape(flat), "adam_w_rg_a", tr=256)
    out["w_rg_x"] = _adam_parts(red[4], w_rg_x.reshape(flat), m_w_rg_x.reshape(flat), v_w_rg_x.reshape(flat), "adam_w_rg_x", tr=256)

    def rows(pre, bg, cbias, ba, bx, lam, post, sinks):
        return (pre, bg, cbias, ba.reshape(1, D_RNN), bx.reshape(1, D_RNN), lam, post,
                jnp.pad(sinks, ((0, 0), (0, D_MODEL - N_Q_HEADS))))

    small_out = _adam_small(
        small,
        rows(pre_norm_g, b_gate, conv_b, b_rg_a, b_rg_x, lru_lambda, post_norm_g, attn_sinks),
        rows(m_pre_norm_g, m_b_gate, m_conv_b, m_b_rg_a, m_b_rg_x, m_lru_lambda, m_post_norm_g, m_attn_sinks),
        rows(v_pre_norm_g, v_b_gate, v_conv_b, v_b_rg_a, v_b_rg_x, v_lru_lambda, v_post_norm_g, v_attn_sinks),
        "adam_small")
    packed, bgate_out = small_out[:4], small_out[4:]
    g_cw = lax.dynamic_slice(packed[0][ROW_CONV_W:ROW_CONV_W + 4], (0, dev * SHARD_OUT), (4, SHARD_OUT))
    out["conv_w"] = _adam_parts(g_cw[None], conv_w[0], m_conv_w[0], v_conv_w[0], "adam_conv_w")

    def unpack(kind, name):
        if name == "b_gate":
            return bgate_out[kind]
        row = dict(pre_norm_g=ROW_PRE_G, conv_b=ROW_CONV_B, b_rg_a=ROW_B_A, b_rg_x=ROW_B_X, lru_lambda=ROW_LAM,
                   post_norm_g=ROW_POST_G, attn_sinks=ROW_SINKS)[name]
        r = packed[kind][row:row + 1]
        if name == "attn_sinks":
            return r[:, 0:N_Q_HEADS]
        if name in ("b_rg_a", "b_rg_x"):
            return r.reshape(1, RNN_BLOCKS, RNN_BLOCK_W)
        return r

    shapes = dict(w_in=(1, D_MODEL, SHARD_IN), w_rnn_out=(1, SHARD_OUT, D_MODEL), w_attn_out=(1, SHARD_OUT, D_MODEL),
                  w_out=(1, SHARD_OUT, D_MODEL), w_rg_a=(1, RNN_BLOCKS, RNN_BLOCK_W, RNN_BLOCK_W),
                  w_rg_x=(1, RNN_BLOCKS, RNN_BLOCK_W, RNN_BLOCK_W), conv_w=(1, 4, SHARD_OUT))
    weights = ["pre_norm_g", "w_in", "b_gate", "conv_w", "conv_b", "w_rg_a", "b_rg_a", "w_rg_x", "b_rg_x",
               "lru_lambda", "attn_sinks", "w_rnn_out", "w_attn_out", "w_out", "post_norm_g"]
    results = []
    for kind in range(4):
        for name in weights:
            if name in out:
                results.append(out[name][kind].reshape(shapes[name]))
            else:
                results.append(unpack(kind, name))
    loss = 0.5 / D_MODEL * jnp.sum(packed[0][ROW_LOSS])
    return (loss, g["grad_x"][None], *results)
```

```python
import functools

import jax
import jax.numpy as jnp
from jax import lax
from jax.experimental import pallas as pl
from jax.experimental.pallas import tpu as pltpu

F32 = jnp.float32
BF16 = jnp.bfloat16

D_MODEL = 1024
D_RNN = 1024
RNN_BLOCKS = 16
RNN_BLOCK_W = 64
LRU_C = 8.0
N_Q_HEADS = 16
HEAD_DIM = 64
D_KV = 256
BLOCK = 128
ALIBI_MAX_BIAS = 8.0
EPS = 1e-6
D_IN = 6656
N_DEV = 8
N_CHIP = 4
SHARD_IN = D_IN // N_DEV
SHARD_OUT = D_MODEL // N_DEV
ATTN_SCALE = HEAD_DIM ** -0.5
MASKED = -1e30

ADAM_LR = 0.001
ADAM_B1 = 0.9
ADAM_B2 = 0.999
ADAM_EPS = 1e-08
ADAM_WD = 0.01
ADAM_STEP = 10

VMEM_LIMIT_BYTES = 52 * 1024 * 1024
LANE = 128
GROUP_W = 256
N_GROUPS = D_RNN // GROUP_W
SEG_CHUNK = 512

NT_DIMS = (((1,), (1,)), ((), ()))
TN_DIMS = (((0,), (0,)), ((), ()))
MESH = pl.DeviceIdType.MESH
ANY = pl.BlockSpec(memory_space=pl.ANY)


def _params(*semantics):
    return pltpu.CompilerParams(dimension_semantics=semantics, vmem_limit_bytes=VMEM_LIMIT_BYTES)


def _sigmoid(x):
    return 0.5 * jnp.tanh(0.5 * x) + 0.5


def _log1p(e):
    u = 1.0 + e
    den = jnp.where(u == 1.0, 1.0, u - 1.0)
    return jnp.where(u == 1.0, e, jnp.log(u) * (e / den))


def _softplus(z):
    return jnp.maximum(z, 0.0) + _log1p(jnp.exp(-jnp.abs(z)))


def _rows8(rows, width):
    idx = lax.broadcasted_iota(jnp.int32, (8, width), 0)
    out = jnp.zeros((8, width), F32)
    for r, v in enumerate(rows):
        out = jnp.where(idx == r, v, out)
    return out


class _Ride:
    def __init__(self, arrays, out_shapes, scratch_shapes, start, finish, middle=None, aliases=None):
        self.arrays, self.out_shapes, self.scratch_shapes = list(arrays), list(out_shapes), list(scratch_shapes)
        self.start, self.finish, self.middle = start, finish, middle
        self.aliases = dict(aliases or {})


class _Hosted:
    def __init__(self, ride, n_in, n_out, n_scratch=0, aliasing=False):
        self.ride = ride
        assert aliasing or not (ride and ride.aliases), "this host does not alias"
        self.aliases = {n_in + i: n_out + o for i, o in ride.aliases.items()} if ride else {}
        self.sizes = (n_in, len(ride.arrays) if ride else 0, n_out, len(ride.out_shapes) if ride else 0, n_scratch)
        self.arrays = ride.arrays if ride else []
        self.in_specs = [ANY] * len(self.arrays)
        self.out_shapes = ride.out_shapes if ride else []
        self.out_specs = [ANY] * len(self.out_shapes)
        self.scratch_shapes = ride.scratch_shapes if ride else []

    def split(self, refs):
        n_in, r_in, n_out, r_out, n_scr = self.sizes
        cuts = [0, n_in, n_in + r_in, n_in + r_in + n_out, n_in + r_in + n_out + r_out, n_in + r_in + n_out + r_out + n_scr]
        host_in, ride_in, host_out, ride_out, host_scr = (refs[cuts[k]:cuts[k + 1]] for k in range(5))
        ride_scr = refs[cuts[5]:]

        def start(when):
            if self.ride is not None:
                pl.when(when)(lambda: self.ride.start(ride_in, ride_out, ride_scr))

        def finish(when):
            if self.ride is not None:
                pl.when(when)(lambda: self.ride.finish(ride_in, ride_out, ride_scr))

        def middle(when):
            if self.ride is not None and self.ride.middle is not None:
                pl.when(when)(lambda: self.ride.middle(ride_in, ride_out, ride_scr))

        start.middle = middle
        return tuple(host_in) + tuple(host_out) + tuple(host_scr), start, finish

    def results(self, outs, n_out):
        outs = list(outs) if isinstance(outs, (list, tuple)) else [outs]
        return outs[:n_out], outs[n_out:]


def _join_rides(a, b):
    na, nb_ = len(a.arrays), len(b.arrays)
    oa = len(a.out_shapes)
    sa = len(a.scratch_shapes)

    def both(fa, fb):
        def run(ins, outs, sems):
            if fa is not None:
                fa(ins[:na], outs[:oa], sems[:sa])
            if fb is not None:
                fb(ins[na:na + nb_], outs[oa:], sems[sa:])
        return run

    middle = both(a.middle, b.middle) if (a.middle or b.middle) else None
    return _Ride(a.arrays + b.arrays, a.out_shapes + b.out_shapes, a.scratch_shapes + b.scratch_shapes,
                 both(a.start, b.start), both(a.finish, b.finish), middle)


def _load_resident(w_hbm, w_vmem, sems, first):
    def piece(c):
        rows = pl.ds(c * SEG_CHUNK, SEG_CHUNK)
        return pltpu.make_async_copy(w_hbm.at[rows], w_vmem.at[rows], sems.at[c])

    @pl.when(first)
    def _():
        for c in range(w_vmem.shape[0] // SEG_CHUNK):
            piece(c).start()

    def ready(c):
        @pl.when(first)
        def _():
            piece(c).wait()

    return ready


CHIP_ROWS = 2 * SHARD_IN
PROJ_WIDTHS = (2 * D_RNN, D_MODEL + 2 * D_KV, 3 * D_MODEL)
PROJ_DTYPES = (F32, BF16, BF16)


def _chip_pieces():
    starts = [0, PROJ_WIDTHS[0], PROJ_WIDTHS[0] + PROJ_WIDTHS[1], D_IN]
    pieces = []
    for k in range(N_CHIP):
        lo, hi = k * CHIP_ROWS, (k + 1) * CHIP_ROWS
        cur = []
        for a in range(len(PROJ_WIDTHS)):
            s0, s1 = max(lo, starts[a]), min(hi, starts[a + 1])
            if s0 < s1:
                cur.append((a, s0 - starts[a], s1 - s0, s0 - lo))
        pieces.append(cur)
    return pieces


def _gather_project(x, g, wt_shard, name, tm=512):
    t, k = x.shape
    tm = min(tm, t)
    nt = t // tm
    pieces = _chip_pieces()

    def body(x_ref, g_ref, shard_ref, h_out, rx_ref, qkv_ref, ag_ref, wt_all,
             w_c, stage, o32, o16, h_all, send_sems, recv_sems, local_sem, stage_sems, out_sems, h_sems):
        s, ti = pl.program_id(0), pl.program_id(1)
        px, py, pc = _place()
        me, sibling = (px, py, pc), (px, py, 1 - pc)
        chips = [(px, py), (1 - px, py), (px, 1 - py), (1 - px, 1 - py)]
        outs = (rx_ref, qkv_ref, ag_ref)

        def slot(dev):
            return wt_all.at[4 * dev[0] + 2 * dev[1] + dev[2]]

        def copy(kk, block, to, src=None):
            return pltpu.make_async_remote_copy(
                src_ref=slot(block) if src is None else src, dst_ref=slot(block),
                send_sem=send_sems.at[kk], recv_sem=recv_sems.at[kk], device_id=to, device_id_type=MESH)

        mine = pltpu.make_async_copy(shard_ref, slot(me), local_sem)
        first = [copy(0, me, sibling, src=shard_ref)] + [copy(1 + j, me, (*chips[1 + j], pc), src=shard_ref) for j in range(3)]
        passed = [copy(4 + j, (*chips[1 + j], pc), sibling) for j in range(3)]

        @pl.when((s == 0) & (ti == 0))
        def _():
            mine.start()
            for cp in first[:3]:
                cp.start()

        def pass_on(step):
            copy(step, (*chips[step], pc), me).wait_recv()
            passed[step - 1].start()

        @pl.when((s == 2) & (ti == nt - 1))
        def _():
            pass_on(3)

        for step in range(N_CHIP):
            @pl.when((s == step) & (ti == 0))
            def _(step=step):
                chip = chips[step]
                if step == 0:
                    mine.wait()
                    copy(0, sibling, me).wait_recv()
                else:
                    if step == 1:
                        pass_on(1)
                        first[3].start()
                        pass_on(2)
                    copy(3 + step, (*chip, 1 - pc), me).wait_recv()
                loads = [pltpu.make_async_copy(slot((*chip, core)), stage.at[pl.ds(core * SHARD_IN, SHARD_IN)], stage_sems.at[core])
                         for core in (0, 1)]
                for cp in loads:
                    cp.start()
                for cp in loads:
                    cp.wait()
                w_c[...] = stage[...].T

        rows = pl.ds(pl.multiple_of(ti * tm, tm), tm)

        @pl.when(s == 0)
        def _():
            xv = x_ref[...]
            h_all[rows, :] = (xv * lax.rsqrt(jnp.mean(xv * xv, axis=-1, keepdims=True) + EPS) * g_ref[...]).astype(BF16)

        res = jnp.dot(h_all[rows, :], w_c[...], preferred_element_type=F32)

        n = s * nt + ti
        buf = lax.rem(n, 2)
        chip_idx = [2 * cx + cy for cx, cy in chips]

        def chip_at(step):
            return jnp.where(step == 0, chip_idx[0], jnp.where(step == 1, chip_idx[1], jnp.where(step == 2, chip_idx[2], chip_idx[3])))

        def h_write(b, tile):
            return pltpu.make_async_copy(h_all.at[pl.ds(tile * tm, tm)], h_out.at[pl.ds(tile * tm, tm)], h_sems.at[b])

        def writes(kchip, b, tile):
            cps = []
            for idx, (a, col, w, src) in enumerate(pieces[kchip]):
                staged = (o16 if PROJ_DTYPES[a] == BF16 else o32).at[b, :, pl.ds(src, w)]
                cps.append(pltpu.make_async_copy(staged, outs[a].at[pl.ds(tile * tm, tm), pl.ds(col, w)], out_sems.at[b, idx]))
            return cps

        o32[buf] = res
        o16[buf] = res.astype(BF16)
        kcur = chip_at(s)

        @pl.when(n > 0)
        def _():
            kprev = chip_at(lax.div(n - 1, nt))
            for kchip in range(N_CHIP):
                @pl.when(kprev == kchip)
                def _(kchip=kchip):
                    for cp in writes(kchip, 1 - buf, lax.rem(n - 1, nt)):
                        cp.wait()

            @pl.when(n <= nt)
            def _():
                h_write(1 - buf, n - 1).wait()

        @pl.when(s == 0)
        def _():
            h_write(buf, ti).start()

        for kchip in range(N_CHIP):
            @pl.when(kcur == kchip)
            def _(kchip=kchip):
                for cp in writes(kchip, buf, ti):
                    cp.start()

        @pl.when(n == N_CHIP * nt - 1)
        def _():
            for kchip in range(N_CHIP):
                @pl.when(kcur == kchip)
                def _(kchip=kchip):
                    for cp in writes(kchip, buf, ti):
                        cp.wait()
            for cp in first + passed:
                cp.wait_send()

    tile = pl.BlockSpec((tm, k), lambda s, ti: (jnp.where(s == 0, ti, nt - 1), 0))
    return pl.pallas_call(
        body, name=name, grid=(N_CHIP, nt),
        in_specs=[tile, pl.BlockSpec((1, k), lambda s, ti: (0, 0)), ANY],
        out_specs=[ANY, ANY, ANY, ANY, ANY],
        out_shape=[jax.ShapeDtypeStruct((t, k), BF16)]
        + [jax.ShapeDtypeStruct((t, w), dt) for w, dt in zip(PROJ_WIDTHS, PROJ_DTYPES)]
        + [jax.ShapeDtypeStruct((N_DEV, SHARD_IN, k), BF16)],
        scratch_shapes=[pltpu.VMEM((k, CHIP_ROWS), BF16), pltpu.VMEM((CHIP_ROWS, k), BF16),
                        pltpu.VMEM((2, tm, CHIP_ROWS), F32), pltpu.VMEM((2, tm, CHIP_ROWS), BF16), pltpu.VMEM((t, k), BF16),
                        pltpu.SemaphoreType.DMA((7,)), pltpu.SemaphoreType.DMA((7,)), pltpu.SemaphoreType.DMA,
                        pltpu.SemaphoreType.DMA((2,)), pltpu.SemaphoreType.DMA((2, 2)), pltpu.SemaphoreType.DMA((2,))],
        compiler_params=_params("arbitrary", "arbitrary"))(x, g, wt_shard)


def _mm_tn(a, b, name, tm=512, tk=4096):
    ktok, m = a.shape
    n = b.shape[1]
    tk = min(tk, ktok)

    def body(a_ref, b_ref, o_ref):
        @pl.when(pl.program_id(1) == 0)
        def _():
            o_ref[...] = jnp.zeros_like(o_ref)

        o_ref[...] += lax.dot_general(a_ref[...], b_ref[...], TN_DIMS, preferred_element_type=F32)

    return pl.pallas_call(
        body, name=name, grid=(m // tm, ktok // tk),
        in_specs=[pl.BlockSpec((tk, tm), lambda i, kk: (kk, i)), pl.BlockSpec((tk, n), lambda i, kk: (kk, 0))],
        out_specs=pl.BlockSpec((tm, n), lambda i, kk: (i, 0)),
        out_shape=jax.ShapeDtypeStruct((m, n), F32),
        compiler_params=_params("parallel", "arbitrary"))(a, b)


def _segment_chunks(segs):
    bounds = [0]
    for s in segs:
        bounds.append(bounds[-1] + s.shape[1] // SEG_CHUNK)
    return bounds


def _input_grad(segs, wt, x, g, dy, name, tm=512, ride=None):
    m = segs[0].shape[0]
    rows, n = wt.shape
    tm = min(tm, m)
    bounds = _segment_chunks(segs)
    n_seg = len(segs)
    ni = m // tm
    host = _Hosted(ride, n_seg + 4, 2, 2, aliasing=True)

    def body(*refs):
        host_refs, start, finish = host.split(refs)
        a_refs = host_refs[:n_seg]
        wt_hbm, x_ref, g_ref, dy_ref, gx_ref, st_ref, wt_vmem, sems = host_refs[n_seg:]
        i = pl.program_id(0)
        start(i == 0)

        @pl.when(i == 0)
        def _():
            st_ref[...] = jnp.zeros_like(st_ref)

        ready = _load_resident(wt_hbm, wt_vmem, sems, i == 0)
        dh = None
        for s in range(n_seg):
            for c in range(bounds[s], bounds[s + 1]):
                ready(c)
            part = jnp.dot(a_refs[s][...], wt_vmem[bounds[s] * SEG_CHUNK:bounds[s + 1] * SEG_CHUNK, :], preferred_element_type=F32)
            dh = part if dh is None else dh + part
        xv = x_ref[...]
        r = lax.rsqrt(jnp.mean(xv * xv, axis=-1, keepdims=True) + EPS)
        xn = xv * r
        dxn = dh * g_ref[...]
        gx_ref[...] = dy_ref[...] + r * (dxn - xn * jnp.mean(dxn * xn, axis=-1, keepdims=True))
        st_ref[...] += _rows8([jnp.sum(dh * xn, axis=0, keepdims=True)], n)
        finish(i == ni - 1)

    tile = pl.BlockSpec((tm, n), lambda i: (i, 0))
    outs = pl.pallas_call(
        body, name=name, grid=(ni,),
        in_specs=[pl.BlockSpec((tm, sg.shape[1]), lambda i: (i, 0)) for sg in segs]
        + [ANY, tile, pl.BlockSpec((1, n), lambda i: (0, 0)), tile] + host.in_specs,
        out_specs=[tile, pl.BlockSpec((8, n), lambda i: (0, 0))] + host.out_specs,
        out_shape=[jax.ShapeDtypeStruct((m, n), F32), jax.ShapeDtypeStruct((8, n), F32)] + host.out_shapes,
        scratch_shapes=[pltpu.VMEM((rows, n), wt.dtype), pltpu.SemaphoreType.DMA((rows // SEG_CHUNK,))] + host.scratch_shapes,
        input_output_aliases=host.aliases,
        compiler_params=_params("arbitrary"))(*segs, wt, x, g, dy, *host.arrays)
    res, landed = host.results(outs, 2)
    return (*res, landed) if ride else tuple(res)


def _mm_tn_seg(segs, b, name):
    ktok = segs[0].shape[0]
    n = b.shape[1]
    bounds = _segment_chunks(segs)
    n_seg = len(segs)
    nc = bounds[-1]
    seg_of = [s for s in range(n_seg) for _ in range(bounds[s], bounds[s + 1])]

    def body(*refs):
        a_hbm, b_hbm, o_ref = refs[:n_seg], refs[n_seg], refs[n_seg + 1]
        a_buf, b_vmem, a_sems, b_sem = refs[n_seg + 2:]
        c = pl.program_id(0)

        def fetch(cc):
            s = seg_of[cc]
            cols = pl.ds((cc - bounds[s]) * SEG_CHUNK, SEG_CHUNK)
            return pltpu.make_async_copy(a_hbm[s].at[:, cols], a_buf.at[cc % 2], a_sems.at[cc % 2])

        @pl.when(c == 0)
        def _():
            whole = pltpu.make_async_copy(b_hbm, b_vmem, b_sem)
            whole.start()
            fetch(0).start()
            whole.wait()

        for cc in range(nc):
            @pl.when(c == cc)
            def _(cc=cc):
                if cc + 1 < nc:
                    fetch(cc + 1).start()
                fetch(cc).wait()

        o_ref[...] = lax.dot_general(a_buf[c % 2], b_vmem[...], TN_DIMS, preferred_element_type=F32)

    return pl.pallas_call(
        body, name=name, grid=(nc,),
        in_specs=[ANY] * (n_seg + 1), out_specs=pl.BlockSpec((SEG_CHUNK, n), lambda c: (c, 0)),
        out_shape=jax.ShapeDtypeStruct((nc * SEG_CHUNK, n), F32),
        scratch_shapes=[pltpu.VMEM((2, ktok, SEG_CHUNK), segs[0].dtype), pltpu.VMEM((ktok, n), b.dtype),
                        pltpu.SemaphoreType.DMA((2,)), pltpu.SemaphoreType.DMA],
        compiler_params=_params("arbitrary"))(*segs, b)


CHUNK_ORDER = (0, 4, 7, 10, 1, 5, 8, 11, 2, 6, 9, 12, 3)
EARLY_STEPS = 8


def _side_pieces():
    table = []
    for core in (0, 1):
        sides = ([None] * len(CHUNK_ORDER), [None] * len(CHUNK_ORDER))
        for s, cc in enumerate(CHUNK_ORDER):
            g0 = cc * SEG_CHUNK
            for d in range(N_DEV):
                lo, hi = max(g0, d * SHARD_IN), min(g0 + SEG_CHUNK, (d + 1) * SHARD_IN)
                if lo < hi:
                    side = sides[0 if d % 2 == core else 1]
                    assert side[s] is None
                    side[s] = (lo - g0, hi - lo, (d // 2) * SHARD_IN + lo - d * SHARD_IN)
        table.append(sides)
    return table


def _to_chip(src, red_ref, piece, s, send_sems, recv_sems, own, core):
    _, rows, at = piece
    k, r0 = divmod(at, SHARD_IN)
    return k, pltpu.make_async_remote_copy(
        src_ref=src, dst_ref=red_ref.at[own, pl.ds(r0, rows)], send_sem=send_sems.at[s], recv_sem=recv_sems.at[own, s],
        device_id=(k // 2, k % 2, core), device_id_type=MESH)


def _from_chip(src, red_ref, piece, s, send_sems, recv_sems, chip, me):
    _, rows, at = piece
    return pltpu.make_async_remote_copy(
        src_ref=src, dst_ref=red_ref.at[chip, pl.ds(at % SHARD_IN, rows)], send_sem=send_sems.at[s], recv_sem=recv_sems.at[chip, s],
        device_id=me, device_id_type=MESH)


def _mm_tn_seg_pair(segs, b, name):
    ktok = segs[0].shape[0]
    n = b.shape[1]
    bounds = _segment_chunks(segs)
    n_seg = len(segs)
    nc = bounds[-1]
    assert nc == len(CHUNK_ORDER)
    seg_of = [s for s in range(n_seg) for _ in range(bounds[s], bounds[s + 1])]
    table = _side_pieces()

    def body(*refs):
        a_hbm, b_hbm, pair_ref, red_ref = refs[:n_seg], refs[n_seg], refs[n_seg + 1], refs[n_seg + 2]
        (a_buf, b_vmem, res_buf, recv_all, pair_all, a_sems, b_sem, send_sems, recv_sems, out_sems,
         chip_send, chip_recv) = refs[n_seg + 3:]
        step = pl.program_id(0)
        px, py, pc = _place()
        own = 2 * px + py

        def fetch(s):
            sg = seg_of[CHUNK_ORDER[s]]
            cols = pl.ds((CHUNK_ORDER[s] - bounds[sg]) * SEG_CHUNK, SEG_CHUNK)
            return pltpu.make_async_copy(a_hbm[sg].at[:, cols], a_buf.at[s % 2], a_sems.at[s % 2])

        @pl.when(step == 0)
        def _():
            whole = pltpu.make_async_copy(b_hbm, b_vmem, b_sem)
            whole.start()
            fetch(0).start()
            whole.wait()

        for s in range(nc):
            @pl.when(step == s)
            def _(s=s):
                if s + 1 < nc:
                    fetch(s + 1).start()
                fetch(s).wait()

        res_buf[step % 2] = lax.dot_general(a_buf[step % 2], b_vmem[...], TN_DIMS, preferred_element_type=F32)

        def crossing(s, piece):
            off, rows, at = piece
            return pltpu.make_async_remote_copy(
                src_ref=res_buf.at[s % 2, pl.ds(off, rows)], dst_ref=recv_all.at[pl.ds(at, rows)],
                send_sem=send_sems.at[s], recv_sem=recv_sems.at[s], device_id=(px, py, 1 - pc), device_id_type=MESH)

        def write(s, piece):
            _, rows, at = piece
            return pltpu.make_async_copy(pair_all.at[pl.ds(at, rows)], pair_ref.at[pl.ds(at, rows)], out_sems.at[s])

        def to_chip(s, piece, core):
            return _to_chip(pair_all.at[pl.ds(piece[2], piece[1])], red_ref, piece, s, chip_send, chip_recv, own, core)

        for core in (0, 1):
            mine, theirs = table[core]

            def settle(s, mine=mine, theirs=theirs, core=core):
                if theirs[s] is not None:
                    crossing(s, theirs[s]).wait_send()
                if mine[s] is not None:
                    off, rows, at = mine[s]
                    crossing(s, mine[s]).wait_recv()
                    pair_all[at:at + rows] = (res_buf[s % 2, off:off + rows] + recv_all[at:at + rows]).astype(BF16)
                    write(s, mine[s]).start()
                    if s < EARLY_STEPS:
                        k, cp = to_chip(s, mine[s], core)
                        pl.when(own != k)(cp.start)

            for s in range(nc):
                @pl.when((pc == core) & (step == s))
                def _(s=s, settle=settle, mine=mine, theirs=theirs, core=core):
                    if theirs[s] is not None:
                        crossing(s, theirs[s]).start()
                    if s > 0:
                        settle(s - 1)
                    if s == nc - 1:
                        settle(s)
                        kept = [t for t in range(nc) if mine[t] is not None]
                        for t in kept:
                            write(t, mine[t]).wait()
                        early = [t for t in kept if t < EARLY_STEPS]
                        for t in early:
                            k, cp = to_chip(t, mine[t], core)
                            pl.when(own != k)(cp.wait_send)
                        for k in range(N_CHIP):
                            @pl.when(own == k)
                            def _(k=k):
                                for t in early:
                                    if mine[t][2] // SHARD_IN == k:
                                        for chip in range(N_CHIP):
                                            if chip != k:
                                                _from_chip(pair_all.at[pl.ds(mine[t][2], mine[t][1])], red_ref, mine[t], t,
                                                           chip_send, chip_recv, chip, (px, py, pc)).wait_recv()

    flat = jax.ShapeDtypeStruct((N_CHIP * SHARD_IN, n), BF16)
    pair, red = pl.pallas_call(
        body, name=name, grid=(nc,),
        in_specs=[ANY] * (n_seg + 1), out_specs=[ANY, ANY],
        out_shape=[flat, jax.ShapeDtypeStruct((N_CHIP, SHARD_IN, n), BF16)],
        scratch_shapes=[pltpu.VMEM((2, ktok, SEG_CHUNK), segs[0].dtype), pltpu.VMEM((ktok, n), b.dtype),
                        pltpu.VMEM((2, SEG_CHUNK, n), F32), pltpu.VMEM(flat.shape, F32), pltpu.VMEM(flat.shape, BF16),
                        pltpu.SemaphoreType.DMA((2,)), pltpu.SemaphoreType.DMA,
                        pltpu.SemaphoreType.DMA((nc,)), pltpu.SemaphoreType.DMA((nc,)), pltpu.SemaphoreType.DMA((nc,)),
                        pltpu.SemaphoreType.DMA((nc,)), pltpu.SemaphoreType.DMA((N_CHIP, nc))],
        compiler_params=_params("arbitrary"))(*segs, b)
    return pair.reshape(N_CHIP, SHARD_IN, n), red


def _branches_fwd(z_rnn, z_attn, ag_ml, b_gate, w_rnn, w_attn, w_out, x, target, g_post, name, tm=512):
    t, d = x.shape
    tm = min(tm, t)

    def body(zr_ref, za_ref, lr_ref, la_ref, br_ref, ba_ref, wr_ref, wa_ref, wo_ref, x_ref, t_ref, g_ref,
             brr_ref, bra_ref, mg_ref, do_ref, dy_ref, st_ref):
        @pl.when(pl.program_id(0) == 0)
        def _():
            st_ref[...] = jnp.zeros_like(st_ref)

        br_rnn = jnp.dot(zr_ref[...], wr_ref[...], preferred_element_type=F32)
        br_attn = jnp.dot(za_ref[...], wa_ref[...], preferred_element_type=F32)
        brr_ref[...] = br_rnn.astype(BF16)
        bra_ref[...] = br_attn.astype(BF16)
        g_rnn = _sigmoid(lr_ref[...].astype(F32) + br_ref[...])
        g_attn = _sigmoid(la_ref[...].astype(F32) + ba_ref[...])
        merged = (g_rnn * br_rnn + g_attn * br_attn).astype(BF16)
        mg_ref[...] = merged
        o = jnp.dot(merged, wo_ref[...], preferred_element_type=F32)
        g = g_ref[...]
        r = lax.rsqrt(jnp.mean(o * o, axis=-1, keepdims=True) + EPS)
        nrm = o * r
        err = x_ref[...] + nrm * g - t_ref[...]
        dy = err * (1.0 / d)
        dy_ref[...] = dy
        dn = dy * g
        do_ref[...] = (r * (dn - nrm * jnp.mean(dn * nrm, axis=-1, keepdims=True))).astype(BF16)
        st_ref[...] += _rows8([jnp.sum(dy * nrm, axis=0, keepdims=True), jnp.sum(err * err, axis=0, keepdims=True)], d)

    tile = pl.BlockSpec((tm, d), lambda i: (i, 0))
    weight = pl.BlockSpec((d, d), lambda i: (0, 0))
    bf = jax.ShapeDtypeStruct((t, d), BF16)
    return pl.pallas_call(
        body, name=name, grid=(t // tm,),
        in_specs=[tile, tile, pl.BlockSpec((tm, d), lambda i: (i, 1)), pl.BlockSpec((tm, d), lambda i: (i, 2)),
                  pl.BlockSpec((1, d), lambda i: (0, 0)), pl.BlockSpec((1, d), lambda i: (0, 1)),
                  weight, weight, weight, tile, tile, pl.BlockSpec((1, d), lambda i: (0, 0))],
        out_specs=[tile, tile, tile, tile, tile, pl.BlockSpec((8, d), lambda i: (0, 0))],
        out_shape=[bf, bf, bf, bf, jax.ShapeDtypeStruct((t, d), F32), jax.ShapeDtypeStruct((8, d), F32)],
        compiler_params=_params("arbitrary"))(z_rnn, z_attn, ag_ml, ag_ml, b_gate, b_gate, w_rnn, w_attn, w_out, x, target, g_post)


def _branches_bwd(dout, br_rnn, br_attn, ag_ml, b_gate, w_rnn, w_attn, w_out, name, tm=512):
    t, d = br_rnn.shape
    tm = min(tm, t)

    def body(do_ref, r_ref, a_ref, lr_ref, la_ref, br_ref, ba_ref, wr_ref, wa_ref, wo_ref,
             dr_ref, da_ref, dl_ref, dzr_ref, dza_ref, st_ref, wt_ref):
        @pl.when(pl.program_id(0) == 0)
        def _():
            st_ref[...] = jnp.zeros_like(st_ref)
            wt_ref[0] = wo_ref[...].T
            wt_ref[1] = wr_ref[...].T
            wt_ref[2] = wa_ref[...].T

        dm = jnp.dot(do_ref[...], wt_ref[0], preferred_element_type=F32)
        g_rnn = _sigmoid(lr_ref[...].astype(F32) + br_ref[...])
        g_attn = _sigmoid(la_ref[...].astype(F32) + ba_ref[...])
        dbr_rnn = (dm * g_rnn).astype(BF16)
        dbr_attn = (dm * g_attn).astype(BF16)
        dr_ref[...] = dbr_rnn
        da_ref[...] = dbr_attn
        dl_rnn = dm * r_ref[...].astype(F32) * g_rnn * (1.0 - g_rnn)
        dl_attn = dm * a_ref[...].astype(F32) * g_attn * (1.0 - g_attn)
        dl_ref[:, 0:d] = dl_rnn.astype(BF16)
        dl_ref[:, d:2 * d] = dl_attn.astype(BF16)
        st_ref[...] += _rows8([jnp.sum(dl_rnn, axis=0, keepdims=True), jnp.sum(dl_attn, axis=0, keepdims=True)], d)
        dzr_ref[...] = jnp.dot(dbr_rnn, wt_ref[1], preferred_element_type=F32).astype(BF16)
        dza_ref[...] = jnp.dot(dbr_attn, wt_ref[2], preferred_element_type=F32).astype(BF16)

    tile = pl.BlockSpec((tm, d), lambda i: (i, 0))
    weight = pl.BlockSpec((d, d), lambda i: (0, 0))
    bf = jax.ShapeDtypeStruct((t, d), BF16)
    return pl.pallas_call(
        body, name=name, grid=(t // tm,),
        in_specs=[tile, tile, tile, pl.BlockSpec((tm, d), lambda i: (i, 1)), pl.BlockSpec((tm, d), lambda i: (i, 2)),
                  pl.BlockSpec((1, d), lambda i: (0, 0)), pl.BlockSpec((1, d), lambda i: (0, 1)), weight, weight, weight],
        out_specs=[tile, tile, pl.BlockSpec((tm, 2 * d), lambda i: (i, 0)), tile, tile, pl.BlockSpec((8, d), lambda i: (0, 0))],
        out_shape=[bf, bf, jax.ShapeDtypeStruct((t, 2 * d), BF16), bf, bf, jax.ShapeDtypeStruct((8, d), F32)],
        scratch_shapes=[pltpu.VMEM((3, d, d), BF16)],
        compiler_params=_params("arbitrary"))(dout, br_rnn, br_attn, ag_ml, ag_ml, b_gate, b_gate, w_rnn, w_attn, w_out)


def _lru_gates(c, wa, ba, wx, bx, sp):
    cb = c.astype(BF16)
    r = _sigmoid(jnp.dot(cb, wa, preferred_element_type=F32) + ba)
    ig = _sigmoid(jnp.dot(cb, wx, preferred_element_type=F32) + bx)
    log_a = (-LRU_C) * r * sp
    a = jnp.exp(log_a)
    mult = jnp.sqrt(-jnp.tanh(log_a) * (a * a + 1.0))
    return cb, r, ig, a, mult


SUBLANES = 8


def _scan_fwd(a, u, carry, tt):
    w = a.shape[1]
    ng = tt // SUBLANES
    a3 = a.reshape(ng, SUBLANES, w)
    u3 = u.reshape(ng, SUBLANES, w)
    sub = lax.broadcasted_iota(jnp.int32, (ng, SUBLANES, w), 1)
    d = 1
    while d < SUBLANES:
        keep = sub >= d
        u3 = u3 + a3 * jnp.where(keep, pltpu.roll(u3, d, 1), 0.0)
        a3 = a3 * jnp.where(keep, pltpu.roll(a3, d, 1), 1.0)
        d *= 2
    out = []
    for g in range(ng):
        hg = u3[g] + a3[g] * carry
        out.append(hg)
        carry = hg[SUBLANES - 1:SUBLANES, :]
    return jnp.concatenate(out, axis=0)


def _scan_rev(b, g, carry, tt):
    w = b.shape[1]
    ng = tt // SUBLANES
    b3 = b.reshape(ng, SUBLANES, w)
    g3 = g.reshape(ng, SUBLANES, w)
    sub = lax.broadcasted_iota(jnp.int32, (ng, SUBLANES, w), 1)
    d = 1
    while d < SUBLANES:
        keep = sub < SUBLANES - d
        g3 = g3 + b3 * jnp.where(keep, pltpu.roll(g3, SUBLANES - d, 1), 0.0)
        b3 = b3 * jnp.where(keep, pltpu.roll(b3, SUBLANES - d, 1), 1.0)
        d *= 2
    out = [None] * ng
    for k in range(ng - 1, -1, -1):
        hk = g3[k] + b3[k] * carry
        out[k] = hk
        carry = hk[0:1, :]
    return jnp.concatenate(out, axis=0)


def _conv_taps(cw, bias, x, ext_ref, tt):
    x2 = ext_ref[7:7 + tt, :]
    x1 = ext_ref[6:6 + tt, :]
    x0 = ext_ref[5:5 + tt, :]
    c = bias + cw[3:4] * x + cw[2:3] * x2 + cw[1:2] * x1 + cw[0:1] * x0
    return c, x2, x1, x0


def _rnn_fwd(rx_rg, cw, cb, wa, ba, wx, bx, lam, name, tt=512):
    t = rx_rg.shape[0]
    tt = min(tt, t)
    w = GROUP_W

    def body(rx_ref, rg_ref, cw_ref, cb_ref, wa_ref, ba_ref, wx_ref, bx_ref, lam_ref, y_ref, z_ref, ext_ref, hc_ref):
        @pl.when(pl.program_id(1) == 0)
        def _():
            ext_ref[0:8, :] = jnp.zeros((8, w), F32)
            hc_ref[...] = jnp.zeros((8, w), F32)

        x = rx_ref[...]
        ext_ref[8:8 + tt, :] = x
        c, _, _, _ = _conv_taps(cw_ref[...], cb_ref[...], x, ext_ref, tt)
        ext_ref[0:8, :] = x[tt - 8:tt, :]
        sp = _softplus(-lam_ref[...])
        _, _, ig, a, mult = _lru_gates(c, wa_ref[...], ba_ref[...], wx_ref[...], bx_ref[...], sp)
        h = _scan_fwd(a, mult * (ig * c), hc_ref[7:8, :], tt)
        hc_ref[...] = h[tt - 8:tt, :]
        y_ref[...] = h
        rg = rg_ref[...]
        z_ref[...] = (h * rg * _sigmoid(rg)).astype(BF16)

    vec = pl.BlockSpec((1, w), lambda g, i: (0, g))
    mat = pl.BlockSpec((None, w, w), lambda g, i: (g, 0, 0))
    tile = pl.BlockSpec((tt, w), lambda g, i: (i, g))
    return pl.pallas_call(
        body, name=name, grid=(N_GROUPS, t // tt),
        in_specs=[tile, pl.BlockSpec((tt, w), lambda g, i: (i, N_GROUPS + g)),
                  pl.BlockSpec((4, w), lambda g, i: (0, g)), vec, mat, vec, mat, vec, vec],
        out_specs=[tile, tile],
        out_shape=[jax.ShapeDtypeStruct((t, D_RNN), F32), jax.ShapeDtypeStruct((t, D_RNN), BF16)],
        scratch_shapes=[pltpu.VMEM((tt + 8, w), F32), pltpu.VMEM((8, w), F32)],
        compiler_params=_params("parallel", "arbitrary"))(rx_rg, rx_rg, cw, cb, wa, ba, wx, bx, lam)


def _rnn_bwd(rx_rg, y, dz, cw, cb, wa, ba, wx, bx, lam, name, tt=512, ride=None):
    t = rx_rg.shape[0]
    tt = min(tt, t)
    nt = t // tt
    w = GROUP_W

    host = _Hosted(ride, 13, 5, 6)

    def body(*refs):
        host_refs, start, finish = host.split(refs)
        (rx_ref, rg_ref, rxt_ref, y_ref, yt_ref, dz_ref, cw_ref, cb_ref, wa_ref, ba_ref, wx_ref, bx_ref, lam_ref,
         drx_ref, drg_ref, st_ref, gda_ref, gdx_ref, ext_ref, dcx_ref, wcar_ref, acar_ref, dwa_ref, dwx_ref) = host_refs
        ii = pl.program_id(1)
        start((pl.program_id(0) == 0) & (ii == 0))

        @pl.when(ii == 0)
        def _():
            wcar_ref[...] = jnp.zeros((8, w), F32)
            acar_ref[...] = jnp.zeros((8, w), F32)
            dcx_ref[tt:tt + 8, :] = jnp.zeros((8, w), F32)
            st_ref[...] = jnp.zeros_like(st_ref)
            dwa_ref[...] = jnp.zeros_like(dwa_ref)
            dwx_ref[...] = jnp.zeros_like(dwx_ref)

        has_prev = jnp.where(ii == nt - 1, 0.0, 1.0)
        x = rx_ref[...]
        ext_ref[0:8, :] = rxt_ref[...] * has_prev
        ext_ref[8:8 + tt, :] = x
        cwv = cw_ref[...]
        c, x2, x1, x0 = _conv_taps(cwv, cb_ref[...], x, ext_ref, tt)
        lam = lam_ref[...]
        sp = _softplus(-lam)
        wa = wa_ref[...]
        wx = wx_ref[...]
        cb16, r, ig, a, mult = _lru_gates(c, wa, ba_ref[...], wx, bx_ref[...], sp)

        rg = rg_ref[...]
        sg = _sigmoid(rg)
        dz = dz_ref[...].astype(F32)
        yv = y_ref[...]
        drg_ref[...] = (dz * yv * (sg * (1.0 + rg * (1.0 - sg)))).astype(BF16)

        row = lax.broadcasted_iota(jnp.int32, (tt, w), 0)
        b = jnp.where(row < tt - 1, pltpu.roll(a, tt - 1, 0), acar_ref[0:1, :])
        dh = _scan_rev(b, dz * (rg * sg), wcar_ref[0:1, :], tt)
        wcar_ref[...] = dh[0:8, :]
        acar_ref[...] = a[0:8, :]

        hprev = jnp.where(row >= 1, pltpu.roll(yv, 1, 0), yt_ref[7:8, :] * has_prev)
        dmult = dh * (ig * c)
        dig = dh * mult * c
        dlog_a = dh * hprev * a - dmult * (a * a / mult)
        dpa = dlog_a * ((-LRU_C) * sp) * r * (1.0 - r)
        dpx = dig * ig * (1.0 - ig)
        dsp = jnp.sum(dlog_a * r, axis=0, keepdims=True) * (-LRU_C)
        dlam = dsp * (-_sigmoid(-lam))
        dpa16 = dpa.astype(BF16)
        dpx16 = dpx.astype(BF16)
        dwa_ref[...] += lax.dot_general(cb16, dpa16, TN_DIMS, preferred_element_type=F32)
        dwx_ref[...] += lax.dot_general(cb16, dpx16, TN_DIMS, preferred_element_type=F32)
        dc = (dh * mult * ig
              + lax.dot_general(dpa16, wa, NT_DIMS, preferred_element_type=F32)
              + lax.dot_general(dpx16, wx, NT_DIMS, preferred_element_type=F32))

        dcx_ref[0:tt, :] = dc
        drx = (cwv[3:4] * dc + cwv[2:3] * dcx_ref[1:1 + tt, :] + cwv[1:2] * dcx_ref[2:2 + tt, :]
               + cwv[0:1] * dcx_ref[3:3 + tt, :])
        drx_ref[...] = drx.astype(BF16)
        dcx_ref[tt:tt + 8, :] = dc[0:8, :]

        def colsum(v):
            return jnp.sum(v, axis=0, keepdims=True)

        st_ref[...] += _rows8([colsum(dc), colsum(dpa), colsum(dpx), dlam,
                               colsum(dc * x0), colsum(dc * x1), colsum(dc * x2), colsum(dc * x)], w)

        @pl.when(ii == nt - 1)
        def _():
            for blk in range(GROUP_W // RNN_BLOCK_W):
                rows = slice(blk * RNN_BLOCK_W, (blk + 1) * RNN_BLOCK_W)
                gda_ref[blk] = dwa_ref[rows, rows]
                gdx_ref[blk] = dwx_ref[rows, rows]

        finish((pl.program_id(0) == N_GROUPS - 1) & (ii == nt - 1))

    def rev(ii):
        return nt - 1 - ii

    def tail(g, ii):
        return (jnp.maximum(rev(ii) * (tt // 8) - 1, 0), g)

    vec = pl.BlockSpec((1, w), lambda g, ii: (0, g))
    mat = pl.BlockSpec((None, w, w), lambda g, ii: (g, 0, 0))
    tile = pl.BlockSpec((tt, w), lambda g, ii: (rev(ii), g))
    diag_shape = (N_GROUPS, GROUP_W // RNN_BLOCK_W, RNN_BLOCK_W, RNN_BLOCK_W)
    diag = pl.BlockSpec((None,) + diag_shape[1:], lambda g, ii: (g, 0, 0, 0))
    outs = pl.pallas_call(
        body, name=name, grid=(N_GROUPS, nt),
        in_specs=[tile, pl.BlockSpec((tt, w), lambda g, ii: (rev(ii), N_GROUPS + g)), pl.BlockSpec((8, w), tail),
                  tile, pl.BlockSpec((8, w), tail), tile,
                  pl.BlockSpec((4, w), lambda g, ii: (0, g)), vec, mat, vec, mat, vec, vec] + host.in_specs,
        out_specs=[tile, tile, pl.BlockSpec((8, w), lambda g, ii: (0, g)), diag, diag] + host.out_specs,
        out_shape=[jax.ShapeDtypeStruct((t, D_RNN), BF16), jax.ShapeDtypeStruct((t, D_RNN), BF16),
                   jax.ShapeDtypeStruct((8, D_RNN), F32),
                   jax.ShapeDtypeStruct(diag_shape, F32), jax.ShapeDtypeStruct(diag_shape, F32)] + host.out_shapes,
        scratch_shapes=[pltpu.VMEM((tt + 8, w), F32), pltpu.VMEM((tt + 8, w), F32), pltpu.VMEM((8, w), F32),
                        pltpu.VMEM((8, w), F32), pltpu.VMEM((w, w), F32), pltpu.VMEM((w, w), F32)] + host.scratch_shapes,
        compiler_params=_params("arbitrary" if ride else "parallel", "arbitrary"))(
            rx_rg, rx_rg, rx_rg, y, y, dz, cw, cb, wa, ba, wx, bx, lam, *host.arrays)
    res, landed = host.results(outs, 5)
    return (*res, landed) if ride else tuple(res)


def _half_mask(shape, half):
    lane = lax.broadcasted_iota(jnp.int32, shape, 1)
    return (lane >= HEAD_DIM) if half else (lane < HEAD_DIM)


def _dup_half(t, half):
    sel = jnp.where(_half_mask(t.shape, half), t, 0.0)
    return sel + pltpu.roll(sel, HEAD_DIM, 1)


def _band_geometry(n):
    qi = lax.broadcasted_iota(jnp.int32, (BLOCK, 2 * BLOCK), 0)
    kj = lax.broadcasted_iota(jnp.int32, (BLOCK, 2 * BLOCK), 1)
    dist = BLOCK + qi - kj
    first_key = jnp.where(n > 0, 0, BLOCK)
    valid = (dist >= 0) & (dist < BLOCK) & (kj >= first_key)
    return dist.astype(F32), valid


GROUP = 4


def _kv_dup(prev_ref, cur_ref, hk, scale=1.0):
    tile = hk // 2
    kt = jnp.concatenate([prev_ref[:, tile * LANE:(tile + 1) * LANE], cur_ref[:, tile * LANE:(tile + 1) * LANE]], axis=0)
    return (_dup_half(kt.astype(F32), hk % 2) * scale).astype(BF16)


def _fill_bias(bias_ref, sm_ref, n):
    distf, valid = _band_geometry(n)
    for head in range(N_Q_HEADS):
        bias_ref[head] = jnp.where(valid, -sm_ref[1, head] * distf, MASKED)


def _head_scores(q2s, half, kdup, bias):
    qm = jnp.where(_half_mask(q2s.shape, half), q2s, jnp.zeros_like(q2s))
    return qm, lax.dot_general(qm, kdup, NT_DIMS, preferred_element_type=F32) + bias


def _attn_specs(nb, clamp_last):
    def blk(n):
        return jnp.minimum(n, nb - 1) if clamp_last else n

    q_spec = pl.BlockSpec((BLOCK, D_MODEL), lambda n: (blk(n), 0))
    k_prev = pl.BlockSpec((BLOCK, D_KV), lambda n: (jnp.maximum(blk(n) - 1, 0), D_MODEL // D_KV))
    k_cur = pl.BlockSpec((BLOCK, D_KV), lambda n: (blk(n), D_MODEL // D_KV))
    v_prev = pl.BlockSpec((BLOCK, D_KV), lambda n: (jnp.maximum(blk(n) - 1, 0), D_MODEL // D_KV + 1))
    v_cur = pl.BlockSpec((BLOCK, D_KV), lambda n: (blk(n), D_MODEL // D_KV + 1))
    return q_spec, k_prev, k_cur, v_prev, v_cur


def _attn_fwd(sm, qkv, ag_ml, name, ride=None):
    t = qkv.shape[0]
    nb = t // BLOCK

    host = _Hosted(ride, 7, 3, 1)

    def body(*refs):
        (sm_ref, q_ref, kp_ref, kc_ref, vp_ref, vc_ref, ag_ref, y_ref, z_ref, lse_ref, bias_ref), start, finish = host.split(refs)
        n = pl.program_id(0)
        start(n == 0)
        start.middle(n == (3 * nb) // 4)

        @pl.when(n <= 1)
        def _():
            _fill_bias(bias_ref, sm_ref, n)

        lane = lax.broadcasted_iota(jnp.int32, (BLOCK, LANE), 1)
        low = lane < HEAD_DIM
        lse = jnp.zeros((BLOCK, LANE), F32)
        for hk in range(N_Q_HEADS // GROUP):
            kdup = _kv_dup(kp_ref, kc_ref, hk)
            vdup = _kv_dup(vp_ref, vc_ref, hk)
            for k in (0, 1):
                c = slice((2 * hk + k) * LANE, (2 * hk + k + 1) * LANE)
                q2s = q_ref[:, c] * ATTN_SCALE
                outs = []
                for half in (0, 1):
                    head = GROUP * hk + 2 * k + half
                    _, s = _head_scores(q2s, half, kdup, bias_ref[head])
                    sink = sm_ref[0, head]
                    m = jnp.maximum(jnp.max(s, axis=1, keepdims=True), sink)
                    e = jnp.exp(s - m)
                    l = jnp.sum(e, axis=1, keepdims=True) + jnp.exp(sink - m)
                    outs.append(jnp.dot((e * (1.0 / l)).astype(BF16), vdup, preferred_element_type=F32))
                    lse = jnp.where(lane == head, m + jnp.log(l), lse)
                yt = jnp.where(low, outs[0], outs[1])
                y_ref[:, c] = yt
                ag = ag_ref[:, c].astype(F32)
                z_ref[:, c] = (yt * ag * _sigmoid(ag)).astype(BF16)
        lse_ref[...] = lse
        finish(n == nb - 1)

    q_spec, k_prev, k_cur, v_prev, v_cur = _attn_specs(nb, False)
    wide = pl.BlockSpec((BLOCK, D_MODEL), lambda n: (n, 0))
    outs = pl.pallas_call(
        body, name=name, grid=(nb,),
        in_specs=[pl.BlockSpec(memory_space=pltpu.SMEM), q_spec, k_prev, k_cur, v_prev, v_cur, wide] + host.in_specs,
        out_specs=[wide, wide, pl.BlockSpec((BLOCK, LANE), lambda n: (n, 0))] + host.out_specs,
        out_shape=[jax.ShapeDtypeStruct((t, D_MODEL), F32), jax.ShapeDtypeStruct((t, D_MODEL), BF16),
                   jax.ShapeDtypeStruct((t, LANE), F32)] + host.out_shapes,
        scratch_shapes=[pltpu.VMEM((N_Q_HEADS, BLOCK, 2 * BLOCK), F32)] + host.scratch_shapes,
        compiler_params=_params("arbitrary"))(sm, qkv, qkv, qkv, qkv, qkv, ag_ml, *host.arrays)
    res, landed = host.results(outs, 3)
    return (*res, landed) if ride else tuple(res)


def _attn_bwd(sm, qkv, ag_ml, y, lse, dz, name, ride=None):
    t = qkv.shape[0]
    nb = t // BLOCK
    host = _Hosted(ride, 10, 4, 3)

    def body(*refs):
        host_refs, start, finish = host.split(refs)
        (sm_ref, q_ref, kp_ref, kc_ref, vp_ref, vc_ref, ag_ref, y_ref, lse_ref, dz_ref,
         dq_ref, dkv_ref, dag_ref, ds_ref, ck_ref, cv_ref, bias_ref) = host_refs
        n = pl.program_id(0)
        start(n == 0)
        start.middle(n == (3 * nb) // 4)

        @pl.when(n == 0)
        def _():
            ck_ref[...] = jnp.zeros_like(ck_ref)
            cv_ref[...] = jnp.zeros_like(cv_ref)
            ds_ref[...] = jnp.zeros_like(ds_ref)

        @pl.when(n <= 1)
        def _():
            _fill_bias(bias_ref, sm_ref, n)

        @pl.when(n < nb)
        def _():
            lane8 = lax.broadcasted_iota(jnp.int32, (8, LANE), 1)
            row8 = lax.broadcasted_iota(jnp.int32, (8, LANE), 0)
            dsink = jnp.zeros((8, LANE), F32)
            dk_heads, dv_heads = [], []
            lse_tile = lse_ref[...]
            for hk in range(N_Q_HEADS // GROUP):
                kdup = _kv_dup(kp_ref, kc_ref, hk)
                ks = _kv_dup(kp_ref, kc_ref, hk, ATTN_SCALE)
                vdup = _kv_dup(vp_ref, vc_ref, hk)
                qms, dyhs, y_rows = [], [], []
                for k in (0, 1):
                    c = slice((2 * hk + k) * LANE, (2 * hk + k + 1) * LANE)
                    ag = ag_ref[:, c].astype(F32)
                    sg = _sigmoid(ag)
                    dzt = dz_ref[:, c].astype(F32)
                    yt = y_ref[:, c]
                    dag_ref[:, c] = (dzt * yt * (sg * (1.0 + ag * (1.0 - sg)))).astype(BF16)
                    dyt = dzt * (ag * sg)
                    q2s = q_ref[:, c] * ATTN_SCALE
                    for half in (0, 1):
                        hm = _half_mask(q2s.shape, half)
                        qms.append(jnp.where(hm, q2s, jnp.zeros_like(q2s)))
                        dyhs.append(jnp.where(hm, dyt, 0.0))
                        y_rows.append(yt)
                qm4 = jnp.concatenate(qms, axis=0)
                dy4 = jnp.concatenate(dyhs, axis=0)
                dy4_16 = dy4.astype(BF16)
                s4 = lax.dot_general(qm4, kdup, NT_DIMS, preferred_element_type=F32)
                dp4 = lax.dot_general(dy4_16, vdup, NT_DIMS, preferred_element_type=F32)
                probs16, ds16 = [], []
                for r in range(GROUP):
                    head = GROUP * hk + r
                    rows = slice(r * BLOCK, (r + 1) * BLOCK)
                    lh = lse_tile[:, head:head + 1]
                    probs = jnp.exp(s4[rows] + bias_ref[head] - lh)
                    psink = jnp.exp(sm_ref[0, head] - lh)
                    delta = jnp.sum(dyhs[r] * y_rows[r], axis=1, keepdims=True)
                    ds16.append((probs * (dp4[rows] - delta)).astype(BF16))
                    probs16.append(probs.astype(BF16))
                    dsink = dsink + jnp.where((row8 == 0) & (lane8 == head),
                                              -jnp.sum(psink * delta, axis=0, keepdims=True), 0.0)
                ds4 = jnp.concatenate(ds16, axis=0)
                p4 = jnp.concatenate(probs16, axis=0)
                dq4 = jnp.dot(ds4, ks, preferred_element_type=F32)
                low = _half_mask((BLOCK, LANE), 0)
                for k in (0, 1):
                    c = slice((2 * hk + k) * LANE, (2 * hk + k + 1) * LANE)
                    dq_ref[:, c] = jnp.where(low, dq4[2 * k * BLOCK:(2 * k + 1) * BLOCK],
                                             dq4[(2 * k + 1) * BLOCK:(2 * k + 2) * BLOCK]).astype(BF16)
                dk_acc = lax.dot_general(ds4, qm4, TN_DIMS, preferred_element_type=F32)
                dv_acc = lax.dot_general(p4, dy4_16, TN_DIMS, preferred_element_type=F32)
                dk_heads.append(dk_acc + pltpu.roll(dk_acc, HEAD_DIM, 1))
                dv_heads.append(dv_acc + pltpu.roll(dv_acc, HEAD_DIM, 1))
            ds_ref[...] += dsink
            low = _half_mask((2 * BLOCK, LANE), 0)
            for tile in range(2):
                cols = slice(tile * LANE, (tile + 1) * LANE)
                dkt = jnp.where(low, dk_heads[2 * tile], dk_heads[2 * tile + 1])
                dvt = jnp.where(low, dv_heads[2 * tile], dv_heads[2 * tile + 1])
                dkv_ref[:, cols] = (ck_ref[:, cols] + dkt[0:BLOCK, :]).astype(BF16)
                dkv_ref[:, D_KV + tile * LANE:D_KV + (tile + 1) * LANE] = (cv_ref[:, cols] + dvt[0:BLOCK, :]).astype(BF16)
                ck_ref[:, cols] = dkt[BLOCK:2 * BLOCK, :]
                cv_ref[:, cols] = dvt[BLOCK:2 * BLOCK, :]

        @pl.when(n == nb)
        def _():
            dkv_ref[:, 0:D_KV] = ck_ref[...].astype(BF16)
            dkv_ref[:, D_KV:2 * D_KV] = cv_ref[...].astype(BF16)

        finish(n == nb)

    q_spec, k_prev, k_cur, v_prev, v_cur = _attn_specs(nb, True)
    wide = pl.BlockSpec((BLOCK, D_MODEL), lambda n: (jnp.minimum(n, nb - 1), 0))
    outs = pl.pallas_call(
        body, name=name, grid=(nb + 1,),
        in_specs=[pl.BlockSpec(memory_space=pltpu.SMEM), q_spec, k_prev, k_cur, v_prev, v_cur, wide, wide,
                  pl.BlockSpec((BLOCK, LANE), lambda n: (jnp.minimum(n, nb - 1), 0)), wide] + host.in_specs,
        out_specs=[wide, pl.BlockSpec((BLOCK, 2 * D_KV), lambda n: (jnp.maximum(n - 1, 0), 0)), wide,
                   pl.BlockSpec((8, LANE), lambda n: (0, 0))] + host.out_specs,
        out_shape=[jax.ShapeDtypeStruct((t, D_MODEL), BF16), jax.ShapeDtypeStruct((t, 2 * D_KV), BF16),
                   jax.ShapeDtypeStruct((t, D_MODEL), BF16), jax.ShapeDtypeStruct((8, LANE), F32)] + host.out_shapes,
        scratch_shapes=[pltpu.VMEM((BLOCK, D_KV), F32), pltpu.VMEM((BLOCK, D_KV), F32),
                        pltpu.VMEM((N_Q_HEADS, BLOCK, 2 * BLOCK), F32)] + host.scratch_shapes,
        compiler_params=_params("arbitrary"))(sm, qkv, qkv, qkv, qkv, qkv, ag_ml, y, lse, dz, *host.arrays)
    res, landed = host.results(outs, 4)
    return (*res, landed) if ride else tuple(res)


def _local_grads(x, target, p, project, late_weights=None, reduce_out=None, reduce_in=None):
    h, rx_rg, qkv, ag_ml, wt = project(x, p["pre_g"])
    if late_weights is None:
        y_attn, z_attn, lse = _attn_fwd(p["sm"], qkv, ag_ml, "attn_fwd")
    else:
        y_attn, z_attn, lse, landed = _attn_fwd(p["sm"], qkv, ag_ml, "attn_fwd", ride=late_weights[0])
        p = {**p, **late_weights[1](landed)}
    lru = (p["cw"], p["cb"], p["wbd_a"], p["b_a"], p["wbd_x"], p["b_x"], p["lam"])
    y_rnn, z_rnn = _rnn_fwd(rx_rg, *lru, "rnn_fwd")
    br_rnn, br_attn, merged, dout, dy, st_post = _branches_fwd(
        z_rnn, z_attn, ag_ml, p["b_gate"], p["w_rnn"], p["w_attn"], p["w_out"], x, target, p["post_g"], "branches_fwd")

    dbr_rnn, dbr_attn, d_ml, dz_rnn, dz_attn, st_merge = _branches_bwd(
        dout, br_rnn, br_attn, ag_ml, p["b_gate"], p["w_rnn"], p["w_attn"], p["w_out"], "branches_bwd")
    gw_out = _mm_tn(merged, dout, "gw_out")
    gw_rnn = _mm_tn(z_rnn, dbr_rnn, "gw_rnn")
    gw_attn = _mm_tn(z_attn, dbr_attn, "gw_attn")
    red_out = red_in = None
    if reduce_out is None:
        d_rx, d_rg, st_rnn, g_rg_a, g_rg_x = _rnn_bwd(rx_rg, y_rnn, dz_rnn, *lru, "rnn_bwd")
        dq, dkv, d_ag, st_sink = _attn_bwd(p["sm"], qkv, ag_ml, y_attn, lse, dz_attn, "attn_bwd")
    else:
        d_rx, d_rg, st_rnn, g_rg_a, g_rg_x, from_sibling = _rnn_bwd(
            rx_rg, y_rnn, dz_rnn, *lru, "rnn_bwd", ride=reduce_out[0](gw_rnn, gw_attn, gw_out))
        dq, dkv, d_ag, st_sink, red_out = _attn_bwd(p["sm"], qkv, ag_ml, y_attn, lse, dz_attn, "attn_bwd",
                                                    ride=reduce_out[1](from_sibling, g_rg_a, g_rg_x))

    segs = [d_rx, d_rg, dq, dkv, d_ag, d_ml]
    if reduce_in is None:
        gwt = _mm_tn_seg(segs, h, "gw_in")
        grad_x, st_pre = _input_grad(segs, wt, x, p["pre_g"], dy, "input_grad")
    else:
        gwt = None
        grad_x, st_pre, red_in = _input_grad(segs, wt, x, p["pre_g"], dy, "input_grad",
                                             ride=reduce_in(*_mm_tn_seg_pair(segs, h, "gw_in")))
    return dict(grad_x=grad_x, gwt=gwt, gw_rnn=gw_rnn, gw_attn=gw_attn, gw_out=gw_out,
                st_post=st_post, st_merge=st_merge, st_rnn=st_rnn, st_sink=st_sink, st_pre=st_pre,
                g_rg_a=g_rg_a, g_rg_x=g_rg_x, red_out=red_out, red_in=red_in)


def _place():
    x, y, c = lax.axis_index("x"), lax.axis_index("y"), lax.axis_index("c")
    return x, y, c


def _gather_ride(shards):
    n = len(shards)

    def copies(ins, outs, sems):
        send_sems, recv_sems, local_sems = sems
        x, y, c = _place()
        me, sibling = (x, y, c), (x, y, 1 - c)
        chips = [(1 - x, y), (x, 1 - y), (1 - x, 1 - y)]

        def slot(a, dev):
            return outs[a].at[4 * dev[0] + 2 * dev[1] + dev[2]]

        def copy(a, k, block, to, src=None):
            return pltpu.make_async_remote_copy(
                src_ref=slot(a, block) if src is None else src, dst_ref=slot(a, block),
                send_sem=send_sems.at[a, k], recv_sem=recv_sems.at[a, k], device_id=to, device_id_type=MESH)

        mine = [pltpu.make_async_copy(ins[a], slot(a, me), local_sems.at[a]) for a in range(n)]
        first = []
        for a in range(n):
            first.append(copy(a, 0, me, sibling, src=ins[a]))
            first += [copy(a, 1 + j, me, (*chip, c), src=ins[a]) for j, chip in enumerate(chips)]
        return me, sibling, chips, c, copy, mine, first

    def start(ins, outs, sems):
        *_, mine, first = copies(ins, outs, sems)
        for cp in mine + first:
            cp.start()

    def middle(ins, outs, sems):
        me, sibling, chips, c, copy, _, _ = copies(ins, outs, sems)
        for j, chip in enumerate(chips):
            for a in range(n):
                copy(a, 1 + j, (*chip, c), me).wait_recv()
                copy(a, 4 + j, (*chip, c), sibling).start()

    def finish(ins, outs, sems):
        me, sibling, chips, c, copy, mine, first = copies(ins, outs, sems)
        passed = [copy(a, 4 + j, (*chip, c), sibling) for j, chip in enumerate(chips) for a in range(n)]
        for a in range(n):
            copy(a, 0, sibling, me).wait_recv()
            for j, chip in enumerate(chips):
                copy(a, 4 + j, (*chip, 1 - c), me).wait_recv()
        for cp in first + passed:
            cp.wait_send()
        for cp in mine:
            cp.wait()

    return _Ride(
        shards, [jax.ShapeDtypeStruct((N_DEV, *s.shape), s.dtype) for s in shards],
        [pltpu.SemaphoreType.DMA((n, 7)), pltpu.SemaphoreType.DMA((n, 7)), pltpu.SemaphoreType.DMA((n,))],
        start, finish, middle)


def _sibling_ride(scatter, whole):
    ns, nw = len(scatter), len(whole)

    def copies(ins, outs, sems):
        send_sems, recv_sems = sems
        x, y, c = _place()
        cps = [pltpu.make_async_remote_copy(
            src_ref=ins[a].at[2 * chip + (1 - c)], dst_ref=outs[a].at[chip],
            send_sem=send_sems.at[a * N_CHIP + chip], recv_sem=recv_sems.at[a * N_CHIP + chip],
            device_id=(x, y, 1 - c), device_id_type=MESH) for a in range(ns) for chip in range(N_CHIP)]
        cps += [pltpu.make_async_remote_copy(
            src_ref=ins[ns + a], dst_ref=outs[ns + a],
            send_sem=send_sems.at[ns * N_CHIP + a], recv_sem=recv_sems.at[ns * N_CHIP + a],
            device_id=(x, y, 1 - c), device_id_type=MESH) for a in range(nw)]
        return cps

    def start(ins, outs, sems):
        for cp in copies(ins, outs, sems):
            cp.start()

    def finish(ins, outs, sems):
        for cp in copies(ins, outs, sems):
            cp.wait()

    n_sem = ns * N_CHIP + nw
    return _Ride(
        list(scatter) + list(whole),
        [jax.ShapeDtypeStruct((N_CHIP, *s.shape[1:]), s.dtype) for s in scatter]
        + [jax.ShapeDtypeStruct(s.shape, s.dtype) for s in whole],
        [pltpu.SemaphoreType.DMA((n_sem,)), pltpu.SemaphoreType.DMA((n_sem,))], start, finish)


def _chips_ride(scatter, whole):
    ns, nw = len(scatter), len(whole)
    n = ns + nw

    def copies(ins, outs, sems):
        send_sems, recv_sems, local_sems = sems
        x, y, c = _place()
        own = 2 * x + y
        chips = [(1 - x, y), (x, 1 - y), (1 - x, 1 - y)]

        def src(a, chip_idx):
            return ins[a].at[chip_idx] if a < ns else ins[a]

        local = [pltpu.make_async_copy(src(a, own), outs[a].at[own], local_sems.at[a]) for a in range(n)]
        sent = [pltpu.make_async_remote_copy(
            src_ref=src(a, 2 * chip[0] + chip[1]), dst_ref=outs[a].at[own],
            send_sem=send_sems.at[a, j], recv_sem=recv_sems.at[a, own], device_id=(*chip, c), device_id_type=MESH)
            for a in range(n) for j, chip in enumerate(chips)]
        return chips, c, local, sent

    def start(ins, outs, sems):
        _, _, local, sent = copies(ins, outs, sems)
        for cp in local + sent:
            cp.start()

    def finish(ins, outs, sems):
        send_sems, recv_sems, _ = sems
        chips, c, local, sent = copies(ins, outs, sems)
        for a in range(n):
            for chip in chips:
                k = 2 * chip[0] + chip[1]
                pltpu.make_async_remote_copy(
                    src_ref=outs[a].at[k], dst_ref=outs[a].at[k], send_sem=send_sems.at[a, 0],
                    recv_sem=recv_sems.at[a, k], device_id=(*chip, c), device_id_type=MESH).wait_recv()
        for cp in sent:
            cp.wait_send()
        for cp in local:
            cp.wait()

    return _Ride(
        list(scatter) + list(whole),
        [jax.ShapeDtypeStruct(s.shape, s.dtype) for s in scatter]
        + [jax.ShapeDtypeStruct((N_CHIP, *s.shape), s.dtype) for s in whole],
        [pltpu.SemaphoreType.DMA((n, 3)), pltpu.SemaphoreType.DMA((n, N_CHIP)), pltpu.SemaphoreType.DMA((n,))],
        start, finish)


def _chips_rest_ride(pair, red):
    table = _side_pieces()
    nc = len(CHUNK_ORDER)

    def each(ins, outs, sems, sending, landing):
        send_sems, recv_sems, _ = sems
        pair_ref, red_ref = ins[0], outs[0]
        x, y, c = _place()
        own = 2 * x + y
        for core in (0, 1):
            mine = table[core][0]
            rest = [s for s in range(EARLY_STEPS, nc) if mine[s] is not None]

            @pl.when(c == core)
            def _(mine=mine, rest=rest, core=core):
                for s in rest:
                    _, rows, at = mine[s]
                    k, cp = _to_chip(pair_ref.at[at // SHARD_IN, pl.ds(at % SHARD_IN, rows)], red_ref, mine[s], s,
                                     send_sems, recv_sems, own, core)
                    pl.when(own != k)(lambda cp=cp: sending(cp))
                if landing is not None:
                    for k in range(N_CHIP):
                        @pl.when(own == k)
                        def _(k=k):
                            for s in rest:
                                _, rows, at = mine[s]
                                if at // SHARD_IN == k:
                                    for chip in range(N_CHIP):
                                        if chip != k:
                                            landing(_from_chip(pair_ref.at[k, pl.ds(at % SHARD_IN, rows)], red_ref, mine[s], s,
                                                               send_sems, recv_sems, chip, (x, y, c)))

    def local(ins, outs, sems):
        x, y, _ = _place()
        return pltpu.make_async_copy(ins[0].at[2 * x + y], outs[0].at[2 * x + y], sems[2])

    def start(ins, outs, sems):
        local(ins, outs, sems).start()
        each(ins, outs, sems, lambda cp: cp.start(), None)

    def finish(ins, outs, sems):
        each(ins, outs, sems, lambda cp: cp.wait_send(), lambda cp: cp.wait_recv())
        local(ins, outs, sems).wait()

    return _Ride([pair, red], [jax.ShapeDtypeStruct(red.shape, red.dtype)],
                 [pltpu.SemaphoreType.DMA((nc,)), pltpu.SemaphoreType.DMA((N_CHIP, nc)), pltpu.SemaphoreType.DMA],
                 start, finish, aliases={1: 0})


def _pair_sum_scatter(parts, recvs, core, name):
    na = len(parts)
    _, r, cdim = parts[0].shape
    tr = min(r, 416 if r % 416 == 0 else 128)

    def body(core_ref, *refs):
        del core_ref
        for a in range(na):
            refs[2 * na + a][...] = (refs[a][...] + refs[na + a][...]).astype(BF16)

    blk = (None, tr, cdim)
    mine = pl.BlockSpec(blk, lambda k, i, core_ref: (2 * k + core_ref[0], i, 0))
    slot = pl.BlockSpec(blk, lambda k, i, core_ref: (k, i, 0))
    return pl.pallas_call(
        body, name=name,
        grid_spec=pltpu.PrefetchScalarGridSpec(
            num_scalar_prefetch=1, grid=(N_CHIP, r // tr),
            in_specs=[mine] * na + [slot] * na, out_specs=[slot] * na),
        out_shape=[jax.ShapeDtypeStruct((N_CHIP, r, cdim), BF16)] * na,
        compiler_params=_params("parallel", "parallel"))(core, *parts, *recvs)


def _allreduce_small(pack, name):
    shape = pack.shape

    def body(x_ref, o_ref, sib_ref, chip_ref, send_sems, recv_sems):
        x, y, c = _place()
        own = 2 * x + y
        chips = [(1 - x, y), (x, 1 - y), (1 - x, 1 - y)]
        to_sibling = pltpu.make_async_remote_copy(
            src_ref=x_ref, dst_ref=sib_ref, send_sem=send_sems.at[0], recv_sem=recv_sems.at[0],
            device_id=(x, y, 1 - c), device_id_type=MESH)
        to_sibling.start()
        to_sibling.wait()
        chip_ref[own] = x_ref[...] + sib_ref[...]
        sent = [pltpu.make_async_remote_copy(
            src_ref=chip_ref.at[own], dst_ref=chip_ref.at[own], send_sem=send_sems.at[1 + j],
            recv_sem=recv_sems.at[1 + own], device_id=(*chip, c), device_id_type=MESH) for j, chip in enumerate(chips)]
        for cp in sent:
            cp.start()
        for chip in chips:
            k = 2 * chip[0] + chip[1]
            pltpu.make_async_remote_copy(
                src_ref=chip_ref.at[k], dst_ref=chip_ref.at[k], send_sem=send_sems.at[1],
                recv_sem=recv_sems.at[1 + k], device_id=(*chip, c), device_id_type=MESH).wait_recv()
        for cp in sent:
            cp.wait_send()
        o_ref[...] = (chip_ref[0] + chip_ref[1]) + (chip_ref[2] + chip_ref[3])

    return pl.pallas_call(
        body, name=name, out_shape=jax.ShapeDtypeStruct(shape, F32),
        in_specs=[pl.BlockSpec(memory_space=pltpu.VMEM)], out_specs=pl.BlockSpec(memory_space=pltpu.VMEM),
        scratch_shapes=[pltpu.VMEM(shape, F32), pltpu.VMEM((N_CHIP, *shape), F32),
                        pltpu.SemaphoreType.DMA((4,)), pltpu.SemaphoreType.DMA((1 + N_CHIP,))],
    )(pack)


def _adamw(g, w, m, v):
    m = ADAM_B1 * m + (1.0 - ADAM_B1) * g
    v = ADAM_B2 * v + (1.0 - ADAM_B2) * (g * g)
    m_hat = m / (1.0 - ADAM_B1 ** ADAM_STEP)
    v_hat = v / (1.0 - ADAM_B2 ** ADAM_STEP)
    delta = -ADAM_LR * (m_hat / (jnp.sqrt(v_hat) + ADAM_EPS) + ADAM_WD * w)
    return delta, m, v


def _adam_parts(parts, w, m, v, name, tr=None):
    npart, r, c = parts.shape
    tr = r if tr is None else min(tr, r)

    def body(p_ref, w_ref, m_ref, v_ref, g_ref, d_ref, nm_ref, nv_ref):
        g = p_ref[0].astype(F32)
        for k in range(1, npart):
            g = g + p_ref[k].astype(F32)
        g_ref[...] = g
        d_ref[...], nm_ref[...], nv_ref[...] = _adamw(g, w_ref[...], m_ref[...], v_ref[...])

    tile = pl.BlockSpec((tr, c), lambda i: (i, 0))
    return pl.pallas_call(
        body, name=name, grid=(r // tr,),
        in_specs=[pl.BlockSpec((npart, tr, c), lambda i: (0, i, 0)), tile, tile, tile],
        out_specs=[tile] * 4, out_shape=[jax.ShapeDtypeStruct((r, c), F32)] * 4,
        compiler_params=_params("parallel"))(parts, w, m, v)


def _block_diag(w):
    w4 = w.reshape(N_GROUPS, 4, RNN_BLOCK_W, RNN_BLOCK_W)
    eye = jnp.eye(4, dtype=w.dtype)
    return jnp.einsum("gbij,bc->gbicj", w4, eye).reshape(N_GROUPS, GROUP_W, GROUP_W).astype(BF16)


SMALL_ROWS = 16
ROW_PRE_G, ROW_BGATE, ROW_CONV_B, ROW_B_A, ROW_B_X, ROW_LAM, ROW_POST_G, ROW_LOSS, ROW_SINKS, ROW_CONV_W = 0, 1, 3, 4, 5, 6, 7, 8, 9, 10


def _pack_stats(st_pre, st_merge, st_rnn, st_post, st_sink, name):
    d = D_MODEL

    def body(pre_ref, mg_ref, rnn_ref, post_ref, sink_ref, o_ref):
        rnn = rnn_ref[...]
        sinks = jnp.concatenate([sink_ref[0:1, :], jnp.zeros((1, d - LANE), F32)], axis=1)
        o_ref[0:8, :] = _rows8([pre_ref[0:1, :], mg_ref[0:1, :], mg_ref[1:2, :], rnn[0:1], rnn[1:2], rnn[2:3], rnn[3:4],
                                post_ref[0:1, :]], d)
        o_ref[8:16, :] = _rows8([post_ref[1:2, :], sinks, rnn[4:5], rnn[5:6], rnn[6:7], rnn[7:8]], d)

    return pl.pallas_call(body, name=name, out_shape=jax.ShapeDtypeStruct((SMALL_ROWS, d), F32))(
        st_pre, st_merge, st_rnn, st_post, st_sink)


def _adam_small(total, w, m, v, name):
    d = D_MODEL
    n_in = len(w)

    def pack(refs):
        pre, bg, cbias, ba, bx, lam, post, sinks = [r[...] for r in refs]
        top = _rows8([pre, bg[:, 0:d], bg[:, d:2 * d], cbias, ba, bx, lam, post], d)
        return jnp.concatenate([top, _rows8([jnp.zeros((1, d), F32), sinks], d)], axis=0)

    def body(*refs):
        p_ref = refs[0]
        w_refs, m_refs, v_refs = (refs[1 + k * n_in:1 + (k + 1) * n_in] for k in range(3))
        outs = refs[1 + 3 * n_in:]
        g = p_ref[...]
        res = (g,) + _adamw(g, pack(w_refs), pack(m_refs), pack(v_refs))
        for k in range(4):
            outs[k][...] = res[k]
            outs[4 + k][...] = jnp.concatenate([res[k][ROW_BGATE:ROW_BGATE + 1], res[k][ROW_BGATE + 1:ROW_BGATE + 2]], axis=1)

    return pl.pallas_call(
        body, name=name,
        out_shape=[jax.ShapeDtypeStruct((SMALL_ROWS, d), F32)] * 4 + [jax.ShapeDtypeStruct((1, 2 * d), F32)] * 4,
    )(total, *w, *m, *v)


def kernel(x, pre_norm_g, w_in, b_gate, conv_w, conv_b, w_rg_a, b_rg_a, w_rg_x, b_rg_x, lru_lambda, attn_sinks, w_rnn_out, w_attn_out, w_out, post_norm_g, loss_target, m_pre_norm_g, m_w_in, m_b_gate, m_conv_w, m_conv_b, m_w_rg_a, m_b_rg_a, m_w_rg_x, m_b_rg_x, m_lru_lambda, m_attn_sinks, m_w_rnn_out, m_w_attn_out, m_w_out, m_post_norm_g, v_pre_norm_g, v_w_in, v_b_gate, v_conv_w, v_conv_b, v_w_rg_a, v_b_rg_a, v_w_rg_x, v_b_rg_x, v_lru_lambda, v_attn_sinks, v_w_rnn_out, v_w_attn_out, v_w_out, v_post_norm_g):
    cx, cy, cc = _place()
    dev = 4 * cx + 2 * cy + cc
    core = jnp.reshape(cc, (1,)).astype(jnp.int32)

    w_in_t, m_in_t, v_in_t = (jnp.transpose(a[0]) for a in (w_in, m_w_in, v_w_in))
    wt_shard = w_in_t.astype(BF16)

    def project(xs, pre_g):
        h, rx_rg, qkv, ag_ml, wt_all = _gather_project(xs, pre_g, wt_shard, "gather_project")
        return h, rx_rg, qkv, ag_ml, wt_all.reshape(D_IN, D_MODEL)

    def late_unpack(landed):
        w_rnn_all, w_attn_all, w_out_all, cw_all = landed
        return dict(w_rnn=w_rnn_all.reshape(D_RNN, D_MODEL), w_attn=w_attn_all.reshape(D_MODEL, D_MODEL),
                    w_out=w_out_all.reshape(D_MODEL, D_MODEL), cw=jnp.transpose(cw_all, (1, 0, 2)).reshape(4, D_RNN))

    late_weights = (_gather_ride([w_rnn_out[0].astype(BF16), w_attn_out[0].astype(BF16), w_out[0].astype(BF16), conv_w[0]]),
                    late_unpack)

    heads = jnp.arange(1, N_Q_HEADS + 1, dtype=F32)
    slopes = jnp.exp2(-ALIBI_MAX_BIAS * heads / N_Q_HEADS)
    b_a = b_rg_a.reshape(1, D_RNN)
    b_x = b_rg_x.reshape(1, D_RNN)
    p = dict(
        pre_g=pre_norm_g, post_g=post_norm_g, b_gate=b_gate, cb=conv_b,
        wbd_a=_block_diag(w_rg_a[0]), b_a=b_a, wbd_x=_block_diag(w_rg_x[0]), b_x=b_x, lam=lru_lambda,
        sm=jnp.pad(attn_sinks, ((0, 1), (0, 0))) + jnp.pad(slopes[None, :], ((1, 0), (0, 0))))

    flat = (RNN_BLOCKS * RNN_BLOCK_W, RNN_BLOCK_W)

    out_scatter = []

    def out_sibling(gw_rnn, gw_attn, gw_out):
        out_scatter.extend(gw.reshape(N_DEV, SHARD_OUT, D_MODEL) for gw in (gw_rnn, gw_attn, gw_out))
        return _sibling_ride(out_scatter, [])

    def out_chips(from_sibling, g_rg_a, g_rg_x):
        return _join_rides(_chips_ride(_pair_sum_scatter(out_scatter, from_sibling, core, "pair_out"), []),
                           _gather_ride([g_rg_a.reshape(flat), g_rg_x.reshape(flat)]))

    reduce_in = _chips_rest_ride

    g = _local_grads(x[0], loss_target[0], p, project, late_weights, (out_sibling, out_chips), reduce_in)
    small = _allreduce_small(
        _pack_stats(g["st_pre"], g["st_merge"], g["st_rnn"], g["st_post"], g["st_sink"], "pack_stats"), "allreduce_small")

    out = {}
    red = g["red_out"]
    out["w_in"] = [jnp.transpose(o) for o in _adam_parts(g["red_in"][0], w_in_t, m_in_t, v_in_t, "adam_w_in", tr=SHARD_IN // 2)]
    out["w_rnn_out"] = _adam_parts(red[0], w_rnn_out[0], m_w_rnn_out[0], v_w_rnn_out[0], "adam_w_rnn_out")
    out["w_attn_out"] = _adam_parts(red[1], w_attn_out[0], m_w_attn_out[0], v_w_attn_out[0], "adam_w_attn_out")
    out["w_out"] = _adam_parts(red[2], w_out[0], m_w_out[0], v_w_out[0], "adam_w_out")
    out["w_rg_a"] = _adam_parts(red[3], w_rg_a.reshape(flat), m_w_rg_a.reshape(flat), v_w_rg_a.reshape(flat), "adam_w_rg_a", tr=256)
    out["w_rg_x"] = _adam_parts(red[4], w_rg_x.reshape(flat), m_w_rg_x.reshape(flat), v_w_rg_x.reshape(flat), "adam_w_rg_x", tr=256)

    def rows(pre, bg, cbias, ba, bx, lam, post, sinks):
        return (pre, bg, cbias, ba.reshape(1, D_RNN), bx.reshape(1, D_RNN), lam, post,
                jnp.pad(sinks, ((0, 0), (0, D_MODEL - N_Q_HEADS))))

    small_out = _adam_small(
        small,
        rows(pre_norm_g, b_gate, conv_b, b_rg_a, b_rg_x, lru_lambda, post_norm_g, attn_sinks),
        rows(m_pre_norm_g, m_b_gate, m_conv_b, m_b_rg_a, m_b_rg_x, m_lru_lambda, m_post_norm_g, m_attn_sinks),
        rows(v_pre_norm_g, v_b_gate, v_conv_b, v_b_rg_a, v_b_rg_x, v_lru_lambda, v_post_norm_g, v_attn_sinks),
        "adam_small")
    packed, bgate_out = small_out[:4], small_out[4:]
    g_cw = lax.dynamic_slice(packed[0][ROW_CONV_W:ROW_CONV_W + 4], (0, dev * SHARD_OUT), (4, SHARD_OUT))
    out["conv_w"] = _adam_parts(g_cw[None], conv_w[0], m_conv_w[0], v_conv_w[0], "adam_conv_w")

    def unpack(kind, name):
        if name == "b_gate":
            return bgate_out[kind]
        row = dict(pre_norm_g=ROW_PRE_G, conv_b=ROW_CONV_B, b_rg_a=ROW_B_A, b_rg_x=ROW_B_X, lru_lambda=ROW_LAM,
                   post_norm_g=ROW_POST_G, attn_sinks=ROW_SINKS)[name]
        r = packed[kind][row:row + 1]
        if name == "attn_sinks":
            return r[:, 0:N_Q_HEADS]
        if name in ("b_rg_a", "b_rg_x"):
            return r.reshape(1, RNN_BLOCKS, RNN_BLOCK_W)
        return r

    shapes = dict(w_in=(1, D_MODEL, SHARD_IN), w_rnn_out=(1, SHARD_OUT, D_MODEL), w_attn_out=(1, SHARD_OUT, D_MODEL),
                  w_out=(1, SHARD_OUT, D_MODEL), w_rg_a=(1, RNN_BLOCKS, RNN_BLOCK_W, RNN_BLOCK_W),
                  w_rg_x=(1, RNN_BLOCKS, RNN_BLOCK_W, RNN_BLOCK_W), conv_w=(1, 4, SHARD_OUT))
    weights = ["pre_norm_g", "w_in", "b_gate", "conv_w", "conv_b", "w_rg_a", "b_rg_a", "w_rg_x", "b_rg_x",
               "lru_lambda", "attn_sinks", "w_rnn_out", "w_attn_out", "w_out", "post_norm_g"]
    results = []
    for kind in range(4):
        for name in weights:
            if name in out:
                results.append(out[name][kind].reshape(shapes[name]))
            else:
                results.append(unpack(kind, name))
    loss = 0.5 / D_MODEL * jnp.sum(packed[0][ROW_LOSS])
    return (loss, g["grad_x"][None], *results)
```

```python
import functools

import jax
import jax.numpy as jnp
from jax import lax
from jax.experimental import pallas as pl
from jax.experimental.pallas import tpu as pltpu

F32 = jnp.float32
BF16 = jnp.bfloat16

D_MODEL = 1024
D_RNN = 1024
RNN_BLOCKS = 16
RNN_BLOCK_W = 64
LRU_C = 8.0
N_Q_HEADS = 16
HEAD_DIM = 64
D_KV = 256
BLOCK = 128
ALIBI_MAX_BIAS = 8.0
EPS = 1e-6
D_IN = 6656
N_DEV = 8
N_CHIP = 4
SHARD_IN = D_IN // N_DEV
SHARD_OUT = D_MODEL // N_DEV
ATTN_SCALE = HEAD_DIM ** -0.5
MASKED = -1e30

ADAM_LR = 0.001
ADAM_B1 = 0.9
ADAM_B2 = 0.999
ADAM_EPS = 1e-08
ADAM_WD = 0.01
ADAM_STEP = 10

VMEM_LIMIT_BYTES = 52 * 1024 * 1024
LANE = 128
GROUP_W = 256
N_GROUPS = D_RNN // GROUP_W
SEG_CHUNK = 512

NT_DIMS = (((1,), (1,)), ((), ()))
TN_DIMS = (((0,), (0,)), ((), ()))
MESH = pl.DeviceIdType.MESH
ANY = pl.BlockSpec(memory_space=pl.ANY)


def _params(*semantics):
    return pltpu.CompilerParams(dimension_semantics=semantics, vmem_limit_bytes=VMEM_LIMIT_BYTES)


def _sigmoid(x):
    return 0.5 * jnp.tanh(0.5 * x) + 0.5


def _log1p(e):
    u = 1.0 + e
    den = jnp.where(u == 1.0, 1.0, u - 1.0)
    return jnp.where(u == 1.0, e, jnp.log(u) * (e / den))


def _softplus(z):
    return jnp.maximum(z, 0.0) + _log1p(jnp.exp(-jnp.abs(z)))


def _rows8(rows, width):
    idx = lax.broadcasted_iota(jnp.int32, (8, width), 0)
    out = jnp.zeros((8, width), F32)
    for r, v in enumerate(rows):
        out = jnp.where(idx == r, v, out)
    return out


class _Ride:
    def __init__(self, arrays, out_shapes, scratch_shapes, start, finish, middle=None, aliases=None):
        self.arrays, self.out_shapes, self.scratch_shapes = list(arrays), list(out_shapes), list(scratch_shapes)
        self.start, self.finish, self.middle = start, finish, middle
        self.aliases = dict(aliases or {})


class _Hosted:
    def __init__(self, ride, n_in, n_out, n_scratch=0, aliasing=False):
        self.ride = ride
        assert aliasing or not (ride and ride.aliases), "this host does not alias"
        self.aliases = {n_in + i: n_out + o for i, o in ride.aliases.items()} if ride else {}
        self.sizes = (n_in, len(ride.arrays) if ride else 0, n_out, len(ride.out_shapes) if ride else 0, n_scratch)
        self.arrays = ride.arrays if ride else []
        self.in_specs = [ANY] * len(self.arrays)
        self.out_shapes = ride.out_shapes if ride else []
        self.out_specs = [ANY] * len(self.out_shapes)
        self.scratch_shapes = ride.scratch_shapes if ride else []

    def split(self, refs):
        n_in, r_in, n_out, r_out, n_scr = self.sizes
        cuts = [0, n_in, n_in + r_in, n_in + r_in + n_out, n_in + r_in + n_out + r_out, n_in + r_in + n_out + r_out + n_scr]
        host_in, ride_in, host_out, ride_out, host_scr = (refs[cuts[k]:cuts[k + 1]] for k in range(5))
        ride_scr = refs[cuts[5]:]

        def start(when):
            if self.ride is not None:
                pl.when(when)(lambda: self.ride.start(ride_in, ride_out, ride_scr))

        def finish(when):
            if self.ride is not None:
                pl.when(when)(lambda: self.ride.finish(ride_in, ride_out, ride_scr))

        def middle(when):
            if self.ride is not None and self.ride.middle is not None:
                pl.when(when)(lambda: self.ride.middle(ride_in, ride_out, ride_scr))

        start.middle = middle
        return tuple(host_in) + tuple(host_out) + tuple(host_scr), start, finish

    def results(self, outs, n_out):
        outs = list(outs) if isinstance(outs, (list, tuple)) else [outs]
        return outs[:n_out], outs[n_out:]


def _join_rides(a, b):
    na, nb_ = len(a.arrays), len(b.arrays)
    oa = len(a.out_shapes)
    sa = len(a.scratch_shapes)

    def both(fa, fb):
        def run(ins, outs, sems):
            if fa is not None:
                fa(ins[:na], outs[:oa], sems[:sa])
            if fb is not None:
                fb(ins[na:na + nb_], outs[oa:], sems[sa:])
        return run

    middle = both(a.middle, b.middle) if (a.middle or b.middle) else None
    return _Ride(a.arrays + b.arrays, a.out_shapes + b.out_shapes, a.scratch_shapes + b.scratch_shapes,
                 both(a.start, b.start), both(a.finish, b.finish), middle)


def _load_resident(w_hbm, w_vmem, sems, first):
    def piece(c):
        rows = pl.ds(c * SEG_CHUNK, SEG_CHUNK)
        return pltpu.make_async_copy(w_hbm.at[rows], w_vmem.at[rows], sems.at[c])

    @pl.when(first)
    def _():
        for c in range(w_vmem.shape[0] // SEG_CHUNK):
            piece(c).start()

    def ready(c):
        @pl.when(first)
        def _():
            piece(c).wait()

    return ready


CHIP_ROWS = 2 * SHARD_IN
PROJ_WIDTHS = (2 * D_RNN, D_MODEL + 2 * D_KV, 3 * D_MODEL)
PROJ_DTYPES = (F32, BF16, BF16)


def _chip_pieces():
    starts = [0, PROJ_WIDTHS[0], PROJ_WIDTHS[0] + PROJ_WIDTHS[1], D_IN]
    pieces = []
    for k in range(N_CHIP):
        lo, hi = k * CHIP_ROWS, (k + 1) * CHIP_ROWS
        cur = []
        for a in range(len(PROJ_WIDTHS)):
            s0, s1 = max(lo, starts[a]), min(hi, starts[a + 1])
            if s0 < s1:
                cur.append((a, s0 - starts[a], s1 - s0, s0 - lo))
        pieces.append(cur)
    return pieces


def _gather_project(x, g, wt_shard, name, tm=1024):
    t, k = x.shape
    tm = min(tm, t)
    nt = t // tm
    pieces = _chip_pieces()

    def body(x_ref, g_ref, shard_ref, h_out, rx_ref, qkv_ref, ag_ref, wt_all,
             w_c, stage, o32, o16, h_all, send_sems, recv_sems, local_sem, stage_sems, out_sems, h_sems):
        s, ti = pl.program_id(0), pl.program_id(1)
        px, py, pc = _place()
        me, sibling = (px, py, pc), (px, py, 1 - pc)
        chips = [(px, py), (1 - px, py), (px, 1 - py), (1 - px, 1 - py)]
        outs = (rx_ref, qkv_ref, ag_ref)

        def slot(dev):
            return wt_all.at[4 * dev[0] + 2 * dev[1] + dev[2]]

        def copy(kk, block, to, src=None):
            return pltpu.make_async_remote_copy(
                src_ref=slot(block) if src is None else src, dst_ref=slot(block),
                send_sem=send_sems.at[kk], recv_sem=recv_sems.at[kk], device_id=to, device_id_type=MESH)

        mine = pltpu.make_async_copy(shard_ref, slot(me), local_sem)
        first = [copy(0, me, sibling, src=shard_ref)] + [copy(1 + j, me, (*chips[1 + j], pc), src=shard_ref) for j in range(3)]
        passed = [copy(4 + j, (*chips[1 + j], pc), sibling) for j in range(3)]

        @pl.when((s == 0) & (ti == 0))
        def _():
            mine.start()
            for cp in first[:3]:
                cp.start()

        def pass_on(step):
            copy(step, (*chips[step], pc), me).wait_recv()
            passed[step - 1].start()

        @pl.when((s == 2) & (ti == nt - 1))
        def _():
            pass_on(3)

        for step in range(N_CHIP):
            @pl.when((s == step) & (ti == 0))
            def _(step=step):
                chip = chips[step]
                if step == 0:
                    mine.wait()
                    copy(0, sibling, me).wait_recv()
                else:
                    if step == 1:
                        pass_on(1)
                        first[3].start()
                        pass_on(2)
                    copy(3 + step, (*chip, 1 - pc), me).wait_recv()
                loads = [pltpu.make_async_copy(slot((*chip, core)), stage.at[pl.ds(core * SHARD_IN, SHARD_IN)], stage_sems.at[core])
                         for core in (0, 1)]
                for cp in loads:
                    cp.start()
                for cp in loads:
                    cp.wait()
                w_c[...] = stage[...].T

        rows = pl.ds(pl.multiple_of(ti * tm, tm), tm)

        @pl.when(s == 0)
        def _():
            xv = x_ref[...]
            h_all[rows, :] = (xv * lax.rsqrt(jnp.mean(xv * xv, axis=-1, keepdims=True) + EPS) * g_ref[...]).astype(BF16)

        res = jnp.dot(h_all[rows, :], w_c[...], preferred_element_type=F32)

        n = s * nt + ti
        buf = lax.rem(n, 2)
        chip_idx = [2 * cx + cy for cx, cy in chips]

        def chip_at(step):
            return jnp.where(step == 0, chip_idx[0], jnp.where(step == 1, chip_idx[1], jnp.where(step == 2, chip_idx[2], chip_idx[3])))

        def h_write(b, tile):
            return pltpu.make_async_copy(h_all.at[pl.ds(tile * tm, tm)], h_out.at[pl.ds(tile * tm, tm)], h_sems.at[b])

        def writes(kchip, b, tile):
            cps = []
            for idx, (a, col, w, src) in enumerate(pieces[kchip]):
                staged = (o16 if PROJ_DTYPES[a] == BF16 else o32).at[b, :, pl.ds(src, w)]
                cps.append(pltpu.make_async_copy(staged, outs[a].at[pl.ds(tile * tm, tm), pl.ds(col, w)], out_sems.at[b, idx]))
            return cps

        o32[buf] = res
        o16[buf] = res.astype(BF16)
        kcur = chip_at(s)

        @pl.when(n > 0)
        def _():
            kprev = chip_at(lax.div(n - 1, nt))
            for kchip in range(N_CHIP):
                @pl.when(kprev == kchip)
                def _(kchip=kchip):
                    for cp in writes(kchip, 1 - buf, lax.rem(n - 1, nt)):
                        cp.wait()

            @pl.when(n <= nt)
            def _():
                h_write(1 - buf, n - 1).wait()

        @pl.when(s == 0)
        def _():
            h_write(buf, ti).start()

        for kchip in range(N_CHIP):
            @pl.when(kcur == kchip)
            def _(kchip=kchip):
                for cp in writes(kchip, buf, ti):
                    cp.start()

        @pl.when(n == N_CHIP * nt - 1)
        def _():
            for kchip in range(N_CHIP):
                @pl.when(kcur == kchip)
                def _(kchip=kchip):
                    for cp in writes(kchip, buf, ti):
                        cp.wait()
            for cp in first + passed:
                cp.wait_send()

    tile = pl.BlockSpec((tm, k), lambda s, ti: (jnp.where(s == 0, ti, nt - 1), 0))
    return pl.pallas_call(
        body, name=name, grid=(N_CHIP, nt),
        in_specs=[tile, pl.BlockSpec((1, k), lambda s, ti: (0, 0)), ANY],
        out_specs=[ANY, ANY, ANY, ANY, ANY],
        out_shape=[jax.ShapeDtypeStruct((t, k), BF16)]
        + [jax.ShapeDtypeStruct((t, w), dt) for w, dt in zip(PROJ_WIDTHS, PROJ_DTYPES)]
        + [jax.ShapeDtypeStruct((N_DEV, SHARD_IN, k), BF16)],
        scratch_shapes=[pltpu.VMEM((k, CHIP_ROWS), BF16), pltpu.VMEM((CHIP_ROWS, k), BF16),
                        pltpu.VMEM((2, tm, CHIP_ROWS), F32), pltpu.VMEM((2, tm, CHIP_ROWS), BF16), pltpu.VMEM((t, k), BF16),
                        pltpu.SemaphoreType.DMA((7,)), pltpu.SemaphoreType.DMA((7,)), pltpu.SemaphoreType.DMA,
                        pltpu.SemaphoreType.DMA((2,)), pltpu.SemaphoreType.DMA((2, 2)), pltpu.SemaphoreType.DMA((2,))],
        compiler_params=_params("arbitrary", "arbitrary"))(x, g, wt_shard)


def _mm_tn(a, b, name, tm=512, tk=4096):
    ktok, m = a.shape
    n = b.shape[1]
    tk = min(tk, ktok)

    def body(a_ref, b_ref, o_ref):
        @pl.when(pl.program_id(1) == 0)
        def _():
            o_ref[...] = jnp.zeros_like(o_ref)

        o_ref[...] += lax.dot_general(a_ref[...], b_ref[...], TN_DIMS, preferred_element_type=F32)

    return pl.pallas_call(
        body, name=name, grid=(m // tm, ktok // tk),
        in_specs=[pl.BlockSpec((tk, tm), lambda i, kk: (kk, i)), pl.BlockSpec((tk, n), lambda i, kk: (kk, 0))],
        out_specs=pl.BlockSpec((tm, n), lambda i, kk: (i, 0)),
        out_shape=jax.ShapeDtypeStruct((m, n), F32),
        compiler_params=_params("parallel", "arbitrary"))(a, b)


def _segment_chunks(segs):
    bounds = [0]
    for s in segs:
        bounds.append(bounds[-1] + s.shape[1] // SEG_CHUNK)
    return bounds


def _input_grad(segs, wt, x, g, dy, name, tm=512, ride=None):
    m = segs[0].shape[0]
    rows, n = wt.shape
    tm = min(tm, m)
    bounds = _segment_chunks(segs)
    n_seg = len(segs)
    ni = m // tm
    host = _Hosted(ride, n_seg + 4, 2, 2, aliasing=True)

    def body(*refs):
        host_refs, start, finish = host.split(refs)
        a_refs = host_refs[:n_seg]
        wt_hbm, x_ref, g_ref, dy_ref, gx_ref, st_ref, wt_vmem, sems = host_refs[n_seg:]
        i = pl.program_id(0)
        start(i == 0)

        @pl.when(i == 0)
        def _():
            st_ref[...] = jnp.zeros_like(st_ref)

        ready = _load_resident(wt_hbm, wt_vmem, sems, i == 0)
        dh = None
        for s in range(n_seg):
            for c in range(bounds[s], bounds[s + 1]):
                ready(c)
            part = jnp.dot(a_refs[s][...], wt_vmem[bounds[s] * SEG_CHUNK:bounds[s + 1] * SEG_CHUNK, :], preferred_element_type=F32)
            dh = part if dh is None else dh + part
        xv = x_ref[...]
        r = lax.rsqrt(jnp.mean(xv * xv, axis=-1, keepdims=True) + EPS)
        xn = xv * r
        dxn = dh * g_ref[...]
        gx_ref[...] = dy_ref[...] + r * (dxn - xn * jnp.mean(dxn * xn, axis=-1, keepdims=True))
        st_ref[...] += _rows8([jnp.sum(dh * xn, axis=0, keepdims=True)], n)
        finish(i == ni - 1)

    tile = pl.BlockSpec((tm, n), lambda i: (i, 0))
    outs = pl.pallas_call(
        body, name=name, grid=(ni,),
        in_specs=[pl.BlockSpec((tm, sg.shape[1]), lambda i: (i, 0)) for sg in segs]
        + [ANY, tile, pl.BlockSpec((1, n), lambda i: (0, 0)), tile] + host.in_specs,
        out_specs=[tile, pl.BlockSpec((8, n), lambda i: (0, 0))] + host.out_specs,
        out_shape=[jax.ShapeDtypeStruct((m, n), F32), jax.ShapeDtypeStruct((8, n), F32)] + host.out_shapes,
        scratch_shapes=[pltpu.VMEM((rows, n), wt.dtype), pltpu.SemaphoreType.DMA((rows // SEG_CHUNK,))] + host.scratch_shapes,
        input_output_aliases=host.aliases,
        compiler_params=_params("arbitrary"))(*segs, wt, x, g, dy, *host.arrays)
    res, landed = host.results(outs, 2)
    return (*res, landed) if ride else tuple(res)


def _mm_tn_seg(segs, b, name):
    ktok = segs[0].shape[0]
    n = b.shape[1]
    bounds = _segment_chunks(segs)
    n_seg = len(segs)
    nc = bounds[-1]
    seg_of = [s for s in range(n_seg) for _ in range(bounds[s], bounds[s + 1])]

    def body(*refs):
        a_hbm, b_hbm, o_ref = refs[:n_seg], refs[n_seg], refs[n_seg + 1]
        a_buf, b_vmem, a_sems, b_sem = refs[n_seg + 2:]
        c = pl.program_id(0)

        def fetch(cc):
            s = seg_of[cc]
            cols = pl.ds((cc - bounds[s]) * SEG_CHUNK, SEG_CHUNK)
            return pltpu.make_async_copy(a_hbm[s].at[:, cols], a_buf.at[cc % 2], a_sems.at[cc % 2])

        @pl.when(c == 0)
        def _():
            whole = pltpu.make_async_copy(b_hbm, b_vmem, b_sem)
            whole.start()
            fetch(0).start()
            whole.wait()

        for cc in range(nc):
            @pl.when(c == cc)
            def _(cc=cc):
                if cc + 1 < nc:
                    fetch(cc + 1).start()
                fetch(cc).wait()

        o_ref[...] = lax.dot_general(a_buf[c % 2], b_vmem[...], TN_DIMS, preferred_element_type=F32)

    return pl.pallas_call(
        body, name=name, grid=(nc,),
        in_specs=[ANY] * (n_seg + 1), out_specs=pl.BlockSpec((SEG_CHUNK, n), lambda c: (c, 0)),
        out_shape=jax.ShapeDtypeStruct((nc * SEG_CHUNK, n), F32),
        scratch_shapes=[pltpu.VMEM((2, ktok, SEG_CHUNK), segs[0].dtype), pltpu.VMEM((ktok, n), b.dtype),
                        pltpu.SemaphoreType.DMA((2,)), pltpu.SemaphoreType.DMA],
        compiler_params=_params("arbitrary"))(*segs, b)


CHUNK_ORDER = (0, 4, 7, 10, 1, 5, 8, 11, 2, 6, 9, 12, 3)
EARLY_STEPS = 8


def _side_pieces():
    table = []
    for core in (0, 1):
        sides = ([None] * len(CHUNK_ORDER), [None] * len(CHUNK_ORDER))
        for s, cc in enumerate(CHUNK_ORDER):
            g0 = cc * SEG_CHUNK
            for d in range(N_DEV):
                lo, hi = max(g0, d * SHARD_IN), min(g0 + SEG_CHUNK, (d + 1) * SHARD_IN)
                if lo < hi:
                    side = sides[0 if d % 2 == core else 1]
                    assert side[s] is None
                    side[s] = (lo - g0, hi - lo, (d // 2) * SHARD_IN + lo - d * SHARD_IN)
        table.append(sides)
    return table


def _to_chip(src, red_ref, piece, s, send_sems, recv_sems, own, core):
    _, rows, at = piece
    k, r0 = divmod(at, SHARD_IN)
    return k, pltpu.make_async_remote_copy(
        src_ref=src, dst_ref=red_ref.at[own, pl.ds(r0, rows)], send_sem=send_sems.at[s], recv_sem=recv_sems.at[own, s],
        device_id=(k // 2, k % 2, core), device_id_type=MESH)


def _from_chip(src, red_ref, piece, s, send_sems, recv_sems, chip, me):
    _, rows, at = piece
    return pltpu.make_async_remote_copy(
        src_ref=src, dst_ref=red_ref.at[chip, pl.ds(at % SHARD_IN, rows)], send_sem=send_sems.at[s], recv_sem=recv_sems.at[chip, s],
        device_id=me, device_id_type=MESH)


def _mm_tn_seg_pair(segs, b, name):
    ktok = segs[0].shape[0]
    n = b.shape[1]
    bounds = _segment_chunks(segs)
    n_seg = len(segs)
    nc = bounds[-1]
    assert nc == len(CHUNK_ORDER)
    seg_of = [s for s in range(n_seg) for _ in range(bounds[s], bounds[s + 1])]
    table = _side_pieces()

    def body(*refs):
        a_hbm, b_hbm, pair_ref, red_ref = refs[:n_seg], refs[n_seg], refs[n_seg + 1], refs[n_seg + 2]
        (a_buf, b_vmem, res_buf, recv_all, pair_all, a_sems, b_sem, send_sems, recv_sems, out_sems,
         chip_send, chip_recv) = refs[n_seg + 3:]
        step = pl.program_id(0)
        px, py, pc = _place()
        own = 2 * px + py

        def fetch(s):
            sg = seg_of[CHUNK_ORDER[s]]
            cols = pl.ds((CHUNK_ORDER[s] - bounds[sg]) * SEG_CHUNK, SEG_CHUNK)
            return pltpu.make_async_copy(a_hbm[sg].at[:, cols], a_buf.at[s % 2], a_sems.at[s % 2])

        @pl.when(step == 0)
        def _():
            whole = pltpu.make_async_copy(b_hbm, b_vmem, b_sem)
            whole.start()
            fetch(0).start()
            whole.wait()

        for s in range(nc):
            @pl.when(step == s)
            def _(s=s):
                if s + 1 < nc:
                    fetch(s + 1).start()
                fetch(s).wait()

        res_buf[step % 2] = lax.dot_general(a_buf[step % 2], b_vmem[...], TN_DIMS, preferred_element_type=F32)

        def crossing(s, piece):
            off, rows, at = piece
            return pltpu.make_async_remote_copy(
                src_ref=res_buf.at[s % 2, pl.ds(off, rows)], dst_ref=recv_all.at[pl.ds(at, rows)],
                send_sem=send_sems.at[s], recv_sem=recv_sems.at[s], device_id=(px, py, 1 - pc), device_id_type=MESH)

        def write(s, piece):
            _, rows, at = piece
            return pltpu.make_async_copy(pair_all.at[pl.ds(at, rows)], pair_ref.at[pl.ds(at, rows)], out_sems.at[s])

        def to_chip(s, piece, core):
            return _to_chip(pair_all.at[pl.ds(piece[2], piece[1])], red_ref, piece, s, chip_send, chip_recv, own, core)

        for core in (0, 1):
            mine, theirs = table[core]

            def settle(s, mine=mine, theirs=theirs, core=core):
                if theirs[s] is not None:
                    crossing(s, theirs[s]).wait_send()
                if mine[s] is not None:
                    off, rows, at = mine[s]
                    crossing(s, mine[s]).wait_recv()
                    pair_all[at:at + rows] = (res_buf[s % 2, off:off + rows] + recv_all[at:at + rows]).astype(BF16)
                    write(s, mine[s]).start()
                    if s < EARLY_STEPS:
                        k, cp = to_chip(s, mine[s], core)
                        pl.when(own != k)(cp.start)

            for s in range(nc):
                @pl.when((pc == core) & (step == s))
                def _(s=s, settle=settle, mine=mine, theirs=theirs, core=core):
                    if theirs[s] is not None:
                        crossing(s, theirs[s]).start()
                    if s > 0:
                        settle(s - 1)
                    if s == nc - 1:
                        settle(s)
                        kept = [t for t in range(nc) if mine[t] is not None]
                        for t in kept:
                            write(t, mine[t]).wait()
                        early = [t for t in kept if t < EARLY_STEPS]
                        for t in early:
                            k, cp = to_chip(t, mine[t], core)
                            pl.when(own != k)(cp.wait_send)
                        for k in range(N_CHIP):
                            @pl.when(own == k)
                            def _(k=k):
                                for t in early:
                                    if mine[t][2] // SHARD_IN == k:
                                        for chip in range(N_CHIP):
                                            if chip != k:
                                                _from_chip(pair_all.at[pl.ds(mine[t][2], mine[t][1])], red_ref, mine[t], t,
                                                           chip_send, chip_recv, chip, (px, py, pc)).wait_recv()

    flat = jax.ShapeDtypeStruct((N_CHIP * SHARD_IN, n), BF16)
    pair, red = pl.pallas_call(
        body, name=name, grid=(nc,),
        in_specs=[ANY] * (n_seg + 1), out_specs=[ANY, ANY],
        out_shape=[flat, jax.ShapeDtypeStruct((N_CHIP, SHARD_IN, n), BF16)],
        scratch_shapes=[pltpu.VMEM((2, ktok, SEG_CHUNK), segs[0].dtype), pltpu.VMEM((ktok, n), b.dtype),
                        pltpu.VMEM((2, SEG_CHUNK, n), F32), pltpu.VMEM(flat.shape, F32), pltpu.VMEM(flat.shape, BF16),
                        pltpu.SemaphoreType.DMA((2,)), pltpu.SemaphoreType.DMA,
                        pltpu.SemaphoreType.DMA((nc,)), pltpu.SemaphoreType.DMA((nc,)), pltpu.SemaphoreType.DMA((nc,)),
                        pltpu.SemaphoreType.DMA((nc,)), pltpu.SemaphoreType.DMA((N_CHIP, nc))],
        compiler_params=_params("arbitrary"))(*segs, b)
    return pair.reshape(N_CHIP, SHARD_IN, n), red


def _branches_fwd(z_rnn, z_attn, ag_ml, b_gate, w_rnn, w_attn, w_out, x, target, g_post, name, tm=512):
    t, d = x.shape
    tm = min(tm, t)

    def body(zr_ref, za_ref, lr_ref, la_ref, br_ref, ba_ref, wr_ref, wa_ref, wo_ref, x_ref, t_ref, g_ref,
             brr_ref, bra_ref, mg_ref, do_ref, dy_ref, st_ref):
        @pl.when(pl.program_id(0) == 0)
        def _():
            st_ref[...] = jnp.zeros_like(st_ref)

        br_rnn = jnp.dot(zr_ref[...], wr_ref[...], preferred_element_type=F32)
        br_attn = jnp.dot(za_ref[...], wa_ref[...], preferred_element_type=F32)
        brr_ref[...] = br_rnn.astype(BF16)
        bra_ref[...] = br_attn.astype(BF16)
        g_rnn = _sigmoid(lr_ref[...].astype(F32) + br_ref[...])
        g_attn = _sigmoid(la_ref[...].astype(F32) + ba_ref[...])
        merged = (g_rnn * br_rnn + g_attn * br_attn).astype(BF16)
        mg_ref[...] = merged
        o = jnp.dot(merged, wo_ref[...], preferred_element_type=F32)
        g = g_ref[...]
        r = lax.rsqrt(jnp.mean(o * o, axis=-1, keepdims=True) + EPS)
        nrm = o * r
        err = x_ref[...] + nrm * g - t_ref[...]
        dy = err * (1.0 / d)
        dy_ref[...] = dy
        dn = dy * g
        do_ref[...] = (r * (dn - nrm * jnp.mean(dn * nrm, axis=-1, keepdims=True))).astype(BF16)
        st_ref[...] += _rows8([jnp.sum(dy * nrm, axis=0, keepdims=True), jnp.sum(err * err, axis=0, keepdims=True)], d)

    tile = pl.BlockSpec((tm, d), lambda i: (i, 0))
    weight = pl.BlockSpec((d, d), lambda i: (0, 0))
    bf = jax.ShapeDtypeStruct((t, d), BF16)
    return pl.pallas_call(
        body, name=name, grid=(t // tm,),
        in_specs=[tile, tile, pl.BlockSpec((tm, d), lambda i: (i, 1)), pl.BlockSpec((tm, d), lambda i: (i, 2)),
                  pl.BlockSpec((1, d), lambda i: (0, 0)), pl.BlockSpec((1, d), lambda i: (0, 1)),
                  weight, weight, weight, tile, tile, pl.BlockSpec((1, d), lambda i: (0, 0))],
        out_specs=[tile, tile, tile, tile, tile, pl.BlockSpec((8, d), lambda i: (0, 0))],
        out_shape=[bf, bf, bf, bf, jax.ShapeDtypeStruct((t, d), F32), jax.ShapeDtypeStruct((8, d), F32)],
        compiler_params=_params("arbitrary"))(z_rnn, z_attn, ag_ml, ag_ml, b_gate, b_gate, w_rnn, w_attn, w_out, x, target, g_post)


def _branches_bwd(dout, br_rnn, br_attn, ag_ml, b_gate, w_rnn, w_attn, w_out, name, tm=512):
    t, d = br_rnn.shape
    tm = min(tm, t)

    def body(do_ref, r_ref, a_ref, lr_ref, la_ref, br_ref, ba_ref, wr_ref, wa_ref, wo_ref,
             dr_ref, da_ref, dl_ref, dzr_ref, dza_ref, st_ref, wt_ref):
        @pl.when(pl.program_id(0) == 0)
        def _():
            st_ref[...] = jnp.zeros_like(st_ref)
            wt_ref[0] = wo_ref[...].T
            wt_ref[1] = wr_ref[...].T
            wt_ref[2] = wa_ref[...].T

        dm = jnp.dot(do_ref[...], wt_ref[0], preferred_element_type=F32)
        g_rnn = _sigmoid(lr_ref[...].astype(F32) + br_ref[...])
        g_attn = _sigmoid(la_ref[...].astype(F32) + ba_ref[...])
        dbr_rnn = (dm * g_rnn).astype(BF16)
        dbr_attn = (dm * g_attn).astype(BF16)
        dr_ref[...] = dbr_rnn
        da_ref[...] = dbr_attn
        dl_rnn = dm * r_ref[...].astype(F32) * g_rnn * (1.0 - g_rnn)
        dl_attn = dm * a_ref[...].astype(F32) * g_attn * (1.0 - g_attn)
        dl_ref[:, 0:d] = dl_rnn.astype(BF16)
        dl_ref[:, d:2 * d] = dl_attn.astype(BF16)
        st_ref[...] += _rows8([jnp.sum(dl_rnn, axis=0, keepdims=True), jnp.sum(dl_attn, axis=0, keepdims=True)], d)
        dzr_ref[...] = jnp.dot(dbr_rnn, wt_ref[1], preferred_element_type=F32).astype(BF16)
        dza_ref[...] = jnp.dot(dbr_attn, wt_ref[2], preferred_element_type=F32).astype(BF16)

    tile = pl.BlockSpec((tm, d), lambda i: (i, 0))
    weight = pl.BlockSpec((d, d), lambda i: (0, 0))
    bf = jax.ShapeDtypeStruct((t, d), BF16)
    return pl.pallas_call(
        body, name=name, grid=(t // tm,),
        in_specs=[tile, tile, tile, pl.BlockSpec((tm, d), lambda i: (i, 1)), pl.BlockSpec((tm, d), lambda i: (i, 2)),
                  pl.BlockSpec((1, d), lambda i: (0, 0)), pl.BlockSpec((1, d), lambda i: (0, 1)), weight, weight, weight],
        out_specs=[tile, tile, pl.BlockSpec((tm, 2 * d), lambda i: (i, 0)), tile, tile, pl.BlockSpec((8, d), lambda i: (0, 0))],
        out_shape=[bf, bf, jax.ShapeDtypeStruct((t, 2 * d), BF16), bf, bf, jax.ShapeDtypeStruct((8, d), F32)],
        scratch_shapes=[pltpu.VMEM((3, d, d), BF16)],
        compiler_params=_params("arbitrary"))(dout, br_rnn, br_attn, ag_ml, ag_ml, b_gate, b_gate, w_rnn, w_attn, w_out)


def _lru_gates(c, wa, ba, wx, bx, sp):
    cb = c.astype(BF16)
    r = _sigmoid(jnp.dot(cb, wa, preferred_element_type=F32) + ba)
    ig = _sigmoid(jnp.dot(cb, wx, preferred_element_type=F32) + bx)
    log_a = (-LRU_C) * r * sp
    a = jnp.exp(log_a)
    mult = jnp.sqrt(-jnp.tanh(log_a) * (a * a + 1.0))
    return cb, r, ig, a, mult


SUBLANES = 8


def _scan_fwd(a, u, carry, tt):
    w = a.shape[1]
    ng = tt // SUBLANES
    a3 = a.reshape(ng, SUBLANES, w)
    u3 = u.reshape(ng, SUBLANES, w)
    sub = lax.broadcasted_iota(jnp.int32, (ng, SUBLANES, w), 1)
    d = 1
    while d < SUBLANES:
        keep = sub >= d
        u3 = u3 + a3 * jnp.where(keep, pltpu.roll(u3, d, 1), 0.0)
        a3 = a3 * jnp.where(keep, pltpu.roll(a3, d, 1), 1.0)
        d *= 2
    out = []
    for g in range(ng):
        hg = u3[g] + a3[g] * carry
        out.append(hg)
        carry = hg[SUBLANES - 1:SUBLANES, :]
    return jnp.concatenate(out, axis=0)


def _scan_rev(b, g, carry, tt):
    w = b.shape[1]
    ng = tt // SUBLANES
    b3 = b.reshape(ng, SUBLANES, w)
    g3 = g.reshape(ng, SUBLANES, w)
    sub = lax.broadcasted_iota(jnp.int32, (ng, SUBLANES, w), 1)
    d = 1
    while d < SUBLANES:
        keep = sub < SUBLANES - d
        g3 = g3 + b3 * jnp.where(keep, pltpu.roll(g3, SUBLANES - d, 1), 0.0)
        b3 = b3 * jnp.where(keep, pltpu.roll(b3, SUBLANES - d, 1), 1.0)
        d *= 2
    out = [None] * ng
    for k in range(ng - 1, -1, -1):
        hk = g3[k] + b3[k] * carry
        out[k] = hk
        carry = hk[0:1, :]
    return jnp.concatenate(out, axis=0)


def _conv_taps(cw, bias, x, ext_ref, tt):
    x2 = ext_ref[7:7 + tt, :]
    x1 = ext_ref[6:6 + tt, :]
    x0 = ext_ref[5:5 + tt, :]
    c = bias + cw[3:4] * x + cw[2:3] * x2 + cw[1:2] * x1 + cw[0:1] * x0
    return c, x2, x1, x0


def _rnn_fwd(rx_rg, cw, cb, wa, ba, wx, bx, lam, name, tt=512):
    t = rx_rg.shape[0]
    tt = min(tt, t)
    w = GROUP_W

    def body(rx_ref, rg_ref, cw_ref, cb_ref, wa_ref, ba_ref, wx_ref, bx_ref, lam_ref, y_ref, z_ref, ext_ref, hc_ref):
        @pl.when(pl.program_id(1) == 0)
        def _():
            ext_ref[0:8, :] = jnp.zeros((8, w), F32)
            hc_ref[...] = jnp.zeros((8, w), F32)

        x = rx_ref[...]
        ext_ref[8:8 + tt, :] = x
        c, _, _, _ = _conv_taps(cw_ref[...], cb_ref[...], x, ext_ref, tt)
        ext_ref[0:8, :] = x[tt - 8:tt, :]
        sp = _softplus(-lam_ref[...])
        _, _, ig, a, mult = _lru_gates(c, wa_ref[...], ba_ref[...], wx_ref[...], bx_ref[...], sp)
        h = _scan_fwd(a, mult * (ig * c), hc_ref[7:8, :], tt)
        hc_ref[...] = h[tt - 8:tt, :]
        y_ref[...] = h
        rg = rg_ref[...]
        z_ref[...] = (h * rg * _sigmoid(rg)).astype(BF16)

    vec = pl.BlockSpec((1, w), lambda g, i: (0, g))
    mat = pl.BlockSpec((None, w, w), lambda g, i: (g, 0, 0))
    tile = pl.BlockSpec((tt, w), lambda g, i: (i, g))
    return pl.pallas_call(
        body, name=name, grid=(N_GROUPS, t // tt),
        in_specs=[tile, pl.BlockSpec((tt, w), lambda g, i: (i, N_GROUPS + g)),
                  pl.BlockSpec((4, w), lambda g, i: (0, g)), vec, mat, vec, mat, vec, vec],
        out_specs=[tile, tile],
        out_shape=[jax.ShapeDtypeStruct((t, D_RNN), F32), jax.ShapeDtypeStruct((t, D_RNN), BF16)],
        scratch_shapes=[pltpu.VMEM((tt + 8, w), F32), pltpu.VMEM((8, w), F32)],
        compiler_params=_params("parallel", "arbitrary"))(rx_rg, rx_rg, cw, cb, wa, ba, wx, bx, lam)


def _rnn_bwd(rx_rg, y, dz, cw, cb, wa, ba, wx, bx, lam, name, tt=512, ride=None):
    t = rx_rg.shape[0]
    tt = min(tt, t)
    nt = t // tt
    w = GROUP_W

    host = _Hosted(ride, 13, 5, 6)

    def body(*refs):
        host_refs, start, finish = host.split(refs)
        (rx_ref, rg_ref, rxt_ref, y_ref, yt_ref, dz_ref, cw_ref, cb_ref, wa_ref, ba_ref, wx_ref, bx_ref, lam_ref,
         drx_ref, drg_ref, st_ref, gda_ref, gdx_ref, ext_ref, dcx_ref, wcar_ref, acar_ref, dwa_ref, dwx_ref) = host_refs
        ii = pl.program_id(1)
        start((pl.program_id(0) == 0) & (ii == 0))

        @pl.when(ii == 0)
        def _():
            wcar_ref[...] = jnp.zeros((8, w), F32)
            acar_ref[...] = jnp.zeros((8, w), F32)
            dcx_ref[tt:tt + 8, :] = jnp.zeros((8, w), F32)
            st_ref[...] = jnp.zeros_like(st_ref)
            dwa_ref[...] = jnp.zeros_like(dwa_ref)
            dwx_ref[...] = jnp.zeros_like(dwx_ref)

        has_prev = jnp.where(ii == nt - 1, 0.0, 1.0)
        x = rx_ref[...]
        ext_ref[0:8, :] = rxt_ref[...] * has_prev
        ext_ref[8:8 + tt, :] = x
        cwv = cw_ref[...]
        c, x2, x1, x0 = _conv_taps(cwv, cb_ref[...], x, ext_ref, tt)
        lam = lam_ref[...]
        sp = _softplus(-lam)
        wa = wa_ref[...]
        wx = wx_ref[...]
        cb16, r, ig, a, mult = _lru_gates(c, wa, ba_ref[...], wx, bx_ref[...], sp)

        rg = rg_ref[...]
        sg = _sigmoid(rg)
        dz = dz_ref[...].astype(F32)
        yv = y_ref[...]
        drg_ref[...] = (dz * yv * (sg * (1.0 + rg * (1.0 - sg)))).astype(BF16)

        row = lax.broadcasted_iota(jnp.int32, (tt, w), 0)
        b = jnp.where(row < tt - 1, pltpu.roll(a, tt - 1, 0), acar_ref[0:1, :])
        dh = _scan_rev(b, dz * (rg * sg), wcar_ref[0:1, :], tt)
        wcar_ref[...] = dh[0:8, :]
        acar_ref[...] = a[0:8, :]

        hprev = jnp.where(row >= 1, pltpu.roll(yv, 1, 0), yt_ref[7:8, :] * has_prev)
        dmult = dh * (ig * c)
        dig = dh * mult * c
        dlog_a = dh * hprev * a - dmult * (a * a / mult)
        dpa = dlog_a * ((-LRU_C) * sp) * r * (1.0 - r)
        dpx = dig * ig * (1.0 - ig)
        dsp = jnp.sum(dlog_a * r, axis=0, keepdims=True) * (-LRU_C)
        dlam = dsp * (-_sigmoid(-lam))
        dpa16 = dpa.astype(BF16)
        dpx16 = dpx.astype(BF16)
        dwa_ref[...] += lax.dot_general(cb16, dpa16, TN_DIMS, preferred_element_type=F32)
        dwx_ref[...] += lax.dot_general(cb16, dpx16, TN_DIMS, preferred_element_type=F32)
        dc = (dh * mult * ig
              + lax.dot_general(dpa16, wa, NT_DIMS, preferred_element_type=F32)
              + lax.dot_general(dpx16, wx, NT_DIMS, preferred_element_type=F32))

        dcx_ref[0:tt, :] = dc
        drx = (cwv[3:4] * dc + cwv[2:3] * dcx_ref[1:1 + tt, :] + cwv[1:2] * dcx_ref[2:2 + tt, :]
               + cwv[0:1] * dcx_ref[3:3 + tt, :])
        drx_ref[...] = drx.astype(BF16)
        dcx_ref[tt:tt + 8, :] = dc[0:8, :]

        def colsum(v):
            return jnp.sum(v, axis=0, keepdims=True)

        st_ref[...] += _rows8([colsum(dc), colsum(dpa), colsum(dpx), dlam,
                               colsum(dc * x0), colsum(dc * x1), colsum(dc * x2), colsum(dc * x)], w)

        @pl.when(ii == nt - 1)
        def _():
            for blk in range(GROUP_W // RNN_BLOCK_W):
                rows = slice(blk * RNN_BLOCK_W, (blk + 1) * RNN_BLOCK_W)
                gda_ref[blk] = dwa_ref[rows, rows]
                gdx_ref[blk] = dwx_ref[rows, rows]

        finish((pl.program_id(0) == N_GROUPS - 1) & (ii == nt - 1))

    def rev(ii):
        return nt - 1 - ii

    def tail(g, ii):
        return (jnp.maximum(rev(ii) * (tt // 8) - 1, 0), g)

    vec = pl.BlockSpec((1, w), lambda g, ii: (0, g))
    mat = pl.BlockSpec((None, w, w), lambda g, ii: (g, 0, 0))
    tile = pl.BlockSpec((tt, w), lambda g, ii: (rev(ii), g))
    diag_shape = (N_GROUPS, GROUP_W // RNN_BLOCK_W, RNN_BLOCK_W, RNN_BLOCK_W)
    diag = pl.BlockSpec((None,) + diag_shape[1:], lambda g, ii: (g, 0, 0, 0))
    outs = pl.pallas_call(
        body, name=name, grid=(N_GROUPS, nt),
        in_specs=[tile, pl.BlockSpec((tt, w), lambda g, ii: (rev(ii), N_GROUPS + g)), pl.BlockSpec((8, w), tail),
                  tile, pl.BlockSpec((8, w), tail), tile,
                  pl.BlockSpec((4, w), lambda g, ii: (0, g)), vec, mat, vec, mat, vec, vec] + host.in_specs,
        out_specs=[tile, tile, pl.BlockSpec((8, w), lambda g, ii: (0, g)), diag, diag] + host.out_specs,
        out_shape=[jax.ShapeDtypeStruct((t, D_RNN), BF16), jax.ShapeDtypeStruct((t, D_RNN), BF16),
                   jax.ShapeDtypeStruct((8, D_RNN), F32),
                   jax.ShapeDtypeStruct(diag_shape, F32), jax.ShapeDtypeStruct(diag_shape, F32)] + host.out_shapes,
        scratch_shapes=[pltpu.VMEM((tt + 8, w), F32), pltpu.VMEM((tt + 8, w), F32), pltpu.VMEM((8, w), F32),
                        pltpu.VMEM((8, w), F32), pltpu.VMEM((w, w), F32), pltpu.VMEM((w, w), F32)] + host.scratch_shapes,
        compiler_params=_params("arbitrary" if ride else "parallel", "arbitrary"))(
            rx_rg, rx_rg, rx_rg, y, y, dz, cw, cb, wa, ba, wx, bx, lam, *host.arrays)
    res, landed = host.results(outs, 5)
    return (*res, landed) if ride else tuple(res)


def _half_mask(shape, half):
    lane = lax.broadcasted_iota(jnp.int32, shape, 1)
    return (lane >= HEAD_DIM) if half else (lane < HEAD_DIM)


def _dup_half(t, half):
    sel = jnp.where(_half_mask(t.shape, half), t, 0.0)
    return sel + pltpu.roll(sel, HEAD_DIM, 1)


def _band_geometry(n):
    qi = lax.broadcasted_iota(jnp.int32, (BLOCK, 2 * BLOCK), 0)
    kj = lax.broadcasted_iota(jnp.int32, (BLOCK, 2 * BLOCK), 1)
    dist = BLOCK + qi - kj
    first_key = jnp.where(n > 0, 0, BLOCK)
    valid = (dist >= 0) & (dist < BLOCK) & (kj >= first_key)
    return dist.astype(F32), valid


GROUP = 4


def _kv_dup(prev_ref, cur_ref, hk, scale=1.0):
    tile = hk // 2
    kt = jnp.concatenate([prev_ref[:, tile * LANE:(tile + 1) * LANE], cur_ref[:, tile * LANE:(tile + 1) * LANE]], axis=0)
    return (_dup_half(kt.astype(F32), hk % 2) * scale).astype(BF16)


def _fill_bias(bias_ref, sm_ref, n):
    distf, valid = _band_geometry(n)
    for head in range(N_Q_HEADS):
        bias_ref[head] = jnp.where(valid, -sm_ref[1, head] * distf, MASKED)


def _head_scores(q2s, half, kdup, bias):
    qm = jnp.where(_half_mask(q2s.shape, half), q2s, jnp.zeros_like(q2s))
    return qm, lax.dot_general(qm, kdup, NT_DIMS, preferred_element_type=F32) + bias


def _attn_specs(nb, clamp_last):
    def blk(n):
        return jnp.minimum(n, nb - 1) if clamp_last else n

    q_spec = pl.BlockSpec((BLOCK, D_MODEL), lambda n: (blk(n), 0))
    k_prev = pl.BlockSpec((BLOCK, D_KV), lambda n: (jnp.maximum(blk(n) - 1, 0), D_MODEL // D_KV))
    k_cur = pl.BlockSpec((BLOCK, D_KV), lambda n: (blk(n), D_MODEL // D_KV))
    v_prev = pl.BlockSpec((BLOCK, D_KV), lambda n: (jnp.maximum(blk(n) - 1, 0), D_MODEL // D_KV + 1))
    v_cur = pl.BlockSpec((BLOCK, D_KV), lambda n: (blk(n), D_MODEL // D_KV + 1))
    return q_spec, k_prev, k_cur, v_prev, v_cur


def _attn_fwd(sm, qkv, ag_ml, name, ride=None):
    t = qkv.shape[0]
    nb = t // BLOCK

    host = _Hosted(ride, 7, 3, 1)

    def body(*refs):
        (sm_ref, q_ref, kp_ref, kc_ref, vp_ref, vc_ref, ag_ref, y_ref, z_ref, lse_ref, bias_ref), start, finish = host.split(refs)
        n = pl.program_id(0)
        start(n == 0)
        start.middle(n == (3 * nb) // 4)

        @pl.when(n <= 1)
        def _():
            _fill_bias(bias_ref, sm_ref, n)

        lane = lax.broadcasted_iota(jnp.int32, (BLOCK, LANE), 1)
        low = lane < HEAD_DIM
        lse = jnp.zeros((BLOCK, LANE), F32)
        for hk in range(N_Q_HEADS // GROUP):
            kdup = _kv_dup(kp_ref, kc_ref, hk)
            vdup = _kv_dup(vp_ref, vc_ref, hk)
            for k in (0, 1):
                c = slice((2 * hk + k) * LANE, (2 * hk + k + 1) * LANE)
                q2s = q_ref[:, c] * ATTN_SCALE
                outs = []
                for half in (0, 1):
                    head = GROUP * hk + 2 * k + half
                    _, s = _head_scores(q2s, half, kdup, bias_ref[head])
                    sink = sm_ref[0, head]
                    m = jnp.maximum(jnp.max(s, axis=1, keepdims=True), sink)
                    e = jnp.exp(s - m)
                    l = jnp.sum(e, axis=1, keepdims=True) + jnp.exp(sink - m)
                    outs.append(jnp.dot((e * (1.0 / l)).astype(BF16), vdup, preferred_element_type=F32))
                    lse = jnp.where(lane == head, m + jnp.log(l), lse)
                yt = jnp.where(low, outs[0], outs[1])
                y_ref[:, c] = yt
                ag = ag_ref[:, c].astype(F32)
                z_ref[:, c] = (yt * ag * _sigmoid(ag)).astype(BF16)
        lse_ref[...] = lse
        finish(n == nb - 1)

    q_spec, k_prev, k_cur, v_prev, v_cur = _attn_specs(nb, False)
    wide = pl.BlockSpec((BLOCK, D_MODEL), lambda n: (n, 0))
    outs = pl.pallas_call(
        body, name=name, grid=(nb,),
        in_specs=[pl.BlockSpec(memory_space=pltpu.SMEM), q_spec, k_prev, k_cur, v_prev, v_cur, wide] + host.in_specs,
        out_specs=[wide, wide, pl.BlockSpec((BLOCK, LANE), lambda n: (n, 0))] + host.out_specs,
        out_shape=[jax.ShapeDtypeStruct((t, D_MODEL), F32), jax.ShapeDtypeStruct((t, D_MODEL), BF16),
                   jax.ShapeDtypeStruct((t, LANE), F32)] + host.out_shapes,
        scratch_shapes=[pltpu.VMEM((N_Q_HEADS, BLOCK, 2 * BLOCK), F32)] + host.scratch_shapes,
        compiler_params=_params("arbitrary"))(sm, qkv, qkv, qkv, qkv, qkv, ag_ml, *host.arrays)
    res, landed = host.results(outs, 3)
    return (*res, landed) if ride else tuple(res)


def _attn_bwd(sm, qkv, ag_ml, y, lse, dz, name, ride=None):
    t = qkv.shape[0]
    nb = t // BLOCK
    host = _Hosted(ride, 10, 4, 3)

    def body(*refs):
        host_refs, start, finish = host.split(refs)
        (sm_ref, q_ref, kp_ref, kc_ref, vp_ref, vc_ref, ag_ref, y_ref, lse_ref, dz_ref,
         dq_ref, dkv_ref, dag_ref, ds_ref, ck_ref, cv_ref, bias_ref) = host_refs
        n = pl.program_id(0)
        start(n == 0)
        start.middle(n == (3 * nb) // 4)

        @pl.when(n == 0)
        def _():
            ck_ref[...] = jnp.zeros_like(ck_ref)
            cv_ref[...] = jnp.zeros_like(cv_ref)
            ds_ref[...] = jnp.zeros_like(ds_ref)

        @pl.when(n <= 1)
        def _():
            _fill_bias(bias_ref, sm_ref, n)

        @pl.when(n < nb)
        def _():
            lane8 = lax.broadcasted_iota(jnp.int32, (8, LANE), 1)
            row8 = lax.broadcasted_iota(jnp.int32, (8, LANE), 0)
            dsink = jnp.zeros((8, LANE), F32)
            dk_heads, dv_heads = [], []
            lse_tile = lse_ref[...]
            for hk in range(N_Q_HEADS // GROUP):
                kdup = _kv_dup(kp_ref, kc_ref, hk)
                ks = _kv_dup(kp_ref, kc_ref, hk, ATTN_SCALE)
                vdup = _kv_dup(vp_ref, vc_ref, hk)
                qms, dyhs, y_rows = [], [], []
                for k in (0, 1):
                    c = slice((2 * hk + k) * LANE, (2 * hk + k + 1) * LANE)
                    ag = ag_ref[:, c].astype(F32)
                    sg = _sigmoid(ag)
                    dzt = dz_ref[:, c].astype(F32)
                    yt = y_ref[:, c]
                    dag_ref[:, c] = (dzt * yt * (sg * (1.0 + ag * (1.0 - sg)))).astype(BF16)
                    dyt = dzt * (ag * sg)
                    q2s = q_ref[:, c] * ATTN_SCALE
                    for half in (0, 1):
                        hm = _half_mask(q2s.shape, half)
                        qms.append(jnp.where(hm, q2s, jnp.zeros_like(q2s)))
                        dyhs.append(jnp.where(hm, dyt, 0.0))
                        y_rows.append(yt)
                qm4 = jnp.concatenate(qms, axis=0)
                dy4 = jnp.concatenate(dyhs, axis=0)
                dy4_16 = dy4.astype(BF16)
                s4 = lax.dot_general(qm4, kdup, NT_DIMS, preferred_element_type=F32)
                dp4 = lax.dot_general(dy4_16, vdup, NT_DIMS, preferred_element_type=F32)
                probs16, ds16 = [], []
                for r in range(GROUP):
                    head = GROUP * hk + r
                    rows = slice(r * BLOCK, (r + 1) * BLOCK)
                    lh = lse_tile[:, head:head + 1]
                    probs = jnp.exp(s4[rows] + bias_ref[head] - lh)
                    psink = jnp.exp(sm_ref[0, head] - lh)
                    delta = jnp.sum(dyhs[r] * y_rows[r], axis=1, keepdims=True)
                    ds16.append((probs * (dp4[rows] - delta)).astype(BF16))
                    probs16.append(probs.astype(BF16))
                    dsink = dsink + jnp.where((row8 == 0) & (lane8 == head),
                                              -jnp.sum(psink * delta, axis=0, keepdims=True), 0.0)
                ds4 = jnp.concatenate(ds16, axis=0)
                p4 = jnp.concatenate(probs16, axis=0)
                dq4 = jnp.dot(ds4, ks, preferred_element_type=F32)
                low = _half_mask((BLOCK, LANE), 0)
                for k in (0, 1):
                    c = slice((2 * hk + k) * LANE, (2 * hk + k + 1) * LANE)
                    dq_ref[:, c] = jnp.where(low, dq4[2 * k * BLOCK:(2 * k + 1) * BLOCK],
                                             dq4[(2 * k + 1) * BLOCK:(2 * k + 2) * BLOCK]).astype(BF16)
                dk_acc = lax.dot_general(ds4, qm4, TN_DIMS, preferred_element_type=F32)
                dv_acc = lax.dot_general(p4, dy4_16, TN_DIMS, preferred_element_type=F32)
                dk_heads.append(dk_acc + pltpu.roll(dk_acc, HEAD_DIM, 1))
                dv_heads.append(dv_acc + pltpu.roll(dv_acc, HEAD_DIM, 1))
            ds_ref[...] += dsink
            low = _half_mask((2 * BLOCK, LANE), 0)
            for tile in range(2):
                cols = slice(tile * LANE, (tile + 1) * LANE)
                dkt = jnp.where(low, dk_heads[2 * tile], dk_heads[2 * tile + 1])
                dvt = jnp.where(low, dv_heads[2 * tile], dv_heads[2 * tile + 1])
                dkv_ref[:, cols] = (ck_ref[:, cols] + dkt[0:BLOCK, :]).astype(BF16)
                dkv_ref[:, D_KV + tile * LANE:D_KV + (tile + 1) * LANE] = (cv_ref[:, cols] + dvt[0:BLOCK, :]).astype(BF16)
                ck_ref[:, cols] = dkt[BLOCK:2 * BLOCK, :]
                cv_ref[:, cols] = dvt[BLOCK:2 * BLOCK, :]

        @pl.when(n == nb)
        def _():
            dkv_ref[:, 0:D_KV] = ck_ref[...].astype(BF16)
            dkv_ref[:, D_KV:2 * D_KV] = cv_ref[...].astype(BF16)

        finish(n == nb)

    q_spec, k_prev, k_cur, v_prev, v_cur = _attn_specs(nb, True)
    wide = pl.BlockSpec((BLOCK, D_MODEL), lambda n: (jnp.minimum(n, nb - 1), 0))
    outs = pl.pallas_call(
        body, name=name, grid=(nb + 1,),
        in_specs=[pl.BlockSpec(memory_space=pltpu.SMEM), q_spec, k_prev, k_cur, v_prev, v_cur, wide, wide,
                  pl.BlockSpec((BLOCK, LANE), lambda n: (jnp.minimum(n, nb - 1), 0)), wide] + host.in_specs,
        out_specs=[wide, pl.BlockSpec((BLOCK, 2 * D_KV), lambda n: (jnp.maximum(n - 1, 0), 0)), wide,
                   pl.BlockSpec((8, LANE), lambda n: (0, 0))] + host.out_specs,
        out_shape=[jax.ShapeDtypeStruct((t, D_MODEL), BF16), jax.ShapeDtypeStruct((t, 2 * D_KV), BF16),
                   jax.ShapeDtypeStruct((t, D_MODEL), BF16), jax.ShapeDtypeStruct((8, LANE), F32)] + host.out_shapes,
        scratch_shapes=[pltpu.VMEM((BLOCK, D_KV), F32), pltpu.VMEM((BLOCK, D_KV), F32),
                        pltpu.VMEM((N_Q_HEADS, BLOCK, 2 * BLOCK), F32)] + host.scratch_shapes,
        compiler_params=_params("arbitrary"))(sm, qkv, qkv, qkv, qkv, qkv, ag_ml, y, lse, dz, *host.arrays)
    res, landed = host.results(outs, 4)
    return (*res, landed) if ride else tuple(res)


def _local_grads(x, target, p, project, late_weights=None, reduce_out=None, reduce_in=None):
    h, rx_rg, qkv, ag_ml, wt = project(x, p["pre_g"])
    if late_weights is None:
        y_attn, z_attn, lse = _attn_fwd(p["sm"], qkv, ag_ml, "attn_fwd")
    else:
        y_attn, z_attn, lse, landed = _attn_fwd(p["sm"], qkv, ag_ml, "attn_fwd", ride=late_weights[0])
        p = {**p, **late_weights[1](landed)}
    lru = (p["cw"], p["cb"], p["wbd_a"], p["b_a"], p["wbd_x"], p["b_x"], p["lam"])
    y_rnn, z_rnn = _rnn_fwd(rx_rg, *lru, "rnn_fwd")
    br_rnn, br_attn, merged, dout, dy, st_post = _branches_fwd(
        z_rnn, z_attn, ag_ml, p["b_gate"], p["w_rnn"], p["w_attn"], p["w_out"], x, target, p["post_g"], "branches_fwd")

    dbr_rnn, dbr_attn, d_ml, dz_rnn, dz_attn, st_merge = _branches_bwd(
        dout, br_rnn, br_attn, ag_ml, p["b_gate"], p["w_rnn"], p["w_attn"], p["w_out"], "branches_bwd")
    gw_out = _mm_tn(merged, dout, "gw_out")
    gw_rnn = _mm_tn(z_rnn, dbr_rnn, "gw_rnn")
    gw_attn = _mm_tn(z_attn, dbr_attn, "gw_attn")
    red_out = red_in = None
    if reduce_out is None:
        d_rx, d_rg, st_rnn, g_rg_a, g_rg_x = _rnn_bwd(rx_rg, y_rnn, dz_rnn, *lru, "rnn_bwd")
        dq, dkv, d_ag, st_sink = _attn_bwd(p["sm"], qkv, ag_ml, y_attn, lse, dz_attn, "attn_bwd")
    else:
        d_rx, d_rg, st_rnn, g_rg_a, g_rg_x, from_sibling = _rnn_bwd(
            rx_rg, y_rnn, dz_rnn, *lru, "rnn_bwd", ride=reduce_out[0](gw_rnn, gw_attn, gw_out))
        dq, dkv, d_ag, st_sink, red_out = _attn_bwd(p["sm"], qkv, ag_ml, y_attn, lse, dz_attn, "attn_bwd",
                                                    ride=reduce_out[1](from_sibling, g_rg_a, g_rg_x))

    segs = [d_rx, d_rg, dq, dkv, d_ag, d_ml]
    if reduce_in is None:
        gwt = _mm_tn_seg(segs, h, "gw_in")
        grad_x, st_pre = _input_grad(segs, wt, x, p["pre_g"], dy, "input_grad")
    else:
        gwt = None
        grad_x, st_pre, red_in = _input_grad(segs, wt, x, p["pre_g"], dy, "input_grad",
                                             ride=reduce_in(*_mm_tn_seg_pair(segs, h, "gw_in")))
    return dict(grad_x=grad_x, gwt=gwt, gw_rnn=gw_rnn, gw_attn=gw_attn, gw_out=gw_out,
                st_post=st_post, st_merge=st_merge, st_rnn=st_rnn, st_sink=st_sink, st_pre=st_pre,
                g_rg_a=g_rg_a, g_rg_x=g_rg_x, red_out=red_out, red_in=red_in)


def _place():
    x, y, c = lax.axis_index("x"), lax.axis_index("y"), lax.axis_index("c")
    return x, y, c


def _gather_ride(shards):
    n = len(shards)

    def copies(ins, outs, sems):
        send_sems, recv_sems, local_sems = sems
        x, y, c = _place()
        me, sibling = (x, y, c), (x, y, 1 - c)
        chips = [(1 - x, y), (x, 1 - y), (1 - x, 1 - y)]

        def slot(a, dev):
            return outs[a].at[4 * dev[0] + 2 * dev[1] + dev[2]]

        def copy(a, k, block, to, src=None):
            return pltpu.make_async_remote_copy(
                src_ref=slot(a, block) if src is None else src, dst_ref=slot(a, block),
                send_sem=send_sems.at[a, k], recv_sem=recv_sems.at[a, k], device_id=to, device_id_type=MESH)

        mine = [pltpu.make_async_copy(ins[a], slot(a, me), local_sems.at[a]) for a in range(n)]
        first = []
        for a in range(n):
            first.append(copy(a, 0, me, sibling, src=ins[a]))
            first += [copy(a, 1 + j, me, (*chip, c), src=ins[a]) for j, chip in enumerate(chips)]
        return me, sibling, chips, c, copy, mine, first

    def start(ins, outs, sems):
        *_, mine, first = copies(ins, outs, sems)
        for cp in mine + first:
            cp.start()

    def middle(ins, outs, sems):
        me, sibling, chips, c, copy, _, _ = copies(ins, outs, sems)
        for j, chip in enumerate(chips):
            for a in range(n):
                copy(a, 1 + j, (*chip, c), me).wait_recv()
                copy(a, 4 + j, (*chip, c), sibling).start()

    def finish(ins, outs, sems):
        me, sibling, chips, c, copy, mine, first = copies(ins, outs, sems)
        passed = [copy(a, 4 + j, (*chip, c), sibling) for j, chip in enumerate(chips) for a in range(n)]
        for a in range(n):
            copy(a, 0, sibling, me).wait_recv()
            for j, chip in enumerate(chips):
                copy(a, 4 + j, (*chip, 1 - c), me).wait_recv()
        for cp in first + passed:
            cp.wait_send()
        for cp in mine:
            cp.wait()

    return _Ride(
        shards, [jax.ShapeDtypeStruct((N_DEV, *s.shape), s.dtype) for s in shards],
        [pltpu.SemaphoreType.DMA((n, 7)), pltpu.SemaphoreType.DMA((n, 7)), pltpu.SemaphoreType.DMA((n,))],
        start, finish, middle)


def _sibling_ride(scatter, whole):
    ns, nw = len(scatter), len(whole)

    def copies(ins, outs, sems):
        send_sems, recv_sems = sems
        x, y, c = _place()
        cps = [pltpu.make_async_remote_copy(
            src_ref=ins[a].at[2 * chip + (1 - c)], dst_ref=outs[a].at[chip],
            send_sem=send_sems.at[a * N_CHIP + chip], recv_sem=recv_sems.at[a * N_CHIP + chip],
            device_id=(x, y, 1 - c), device_id_type=MESH) for a in range(ns) for chip in range(N_CHIP)]
        cps += [pltpu.make_async_remote_copy(
            src_ref=ins[ns + a], dst_ref=outs[ns + a],
            send_sem=send_sems.at[ns * N_CHIP + a], recv_sem=recv_sems.at[ns * N_CHIP + a],
            device_id=(x, y, 1 - c), device_id_type=MESH) for a in range(nw)]
        return cps

    def start(ins, outs, sems):
        for cp in copies(ins, outs, sems):
            cp.start()

    def finish(ins, outs, sems):
        for cp in copies(ins, outs, sems):
            cp.wait()

    n_sem = ns * N_CHIP + nw
    return _Ride(
        list(scatter) + list(whole),
        [jax.ShapeDtypeStruct((N_CHIP, *s.shape[1:]), s.dtype) for s in scatter]
        + [jax.ShapeDtypeStruct(s.shape, s.dtype) for s in whole],
        [pltpu.SemaphoreType.DMA((n_sem,)), pltpu.SemaphoreType.DMA((n_sem,))], start, finish)


def _chips_ride(scatter, whole):
    ns, nw = len(scatter), len(whole)
    n = ns + nw

    def copies(ins, outs, sems):
        send_sems, recv_sems, local_sems = sems
        x, y, c = _place()
        own = 2 * x + y
        chips = [(1 - x, y), (x, 1 - y), (1 - x, 1 - y)]

        def src(a, chip_idx):
            return ins[a].at[chip_idx] if a < ns else ins[a]

        local = [pltpu.make_async_copy(src(a, own), outs[a].at[own], local_sems.at[a]) for a in range(n)]
        sent = [pltpu.make_async_remote_copy(
            src_ref=src(a, 2 * chip[0] + chip[1]), dst_ref=outs[a].at[own],
            send_sem=send_sems.at[a, j], recv_sem=recv_sems.at[a, own], device_id=(*chip, c), device_id_type=MESH)
            for a in range(n) for j, chip in enumerate(chips)]
        return chips, c, local, sent

    def start(ins, outs, sems):
        _, _, local, sent = copies(ins, outs, sems)
        for cp in local + sent:
            cp.start()

    def finish(ins, outs, sems):
        send_sems, recv_sems, _ = sems
        chips, c, local, sent = copies(ins, outs, sems)
        for a in range(n):
            for chip in chips:
                k = 2 * chip[0] + chip[1]
                pltpu.make_async_remote_copy(
                    src_ref=outs[a].at[k], dst_ref=outs[a].at[k], send_sem=send_sems.at[a, 0],
                    recv_sem=recv_sems.at[a, k], device_id=(*chip, c), device_id_type=MESH).wait_recv()
        for cp in sent:
            cp.wait_send()
        for cp in local:
            cp.wait()

    return _Ride(
        list(scatter) + list(whole),
        [jax.ShapeDtypeStruct(s.shape, s.dtype) for s in scatter]
        + [jax.ShapeDtypeStruct((N_CHIP, *s.shape), s.dtype) for s in whole],
        [pltpu.SemaphoreType.DMA((n, 3)), pltpu.SemaphoreType.DMA((n, N_CHIP)), pltpu.SemaphoreType.DMA((n,))],
        start, finish)


def _chips_rest_ride(pair, red):
    table = _side_pieces()
    nc = len(CHUNK_ORDER)

    def each(ins, outs, sems, sending, landing):
        send_sems, recv_sems, _ = sems
        pair_ref, red_ref = ins[0], outs[0]
        x, y, c = _place()
        own = 2 * x + y
        for core in (0, 1):
            mine = table[core][0]
            rest = [s for s in range(EARLY_STEPS, nc) if mine[s] is not None]

            @pl.when(c == core)
            def _(mine=mine, rest=rest, core=core):
                for s in rest:
                    _, rows, at = mine[s]
                    k, cp = _to_chip(pair_ref.at[at // SHARD_IN, pl.ds(at % SHARD_IN, rows)], red_ref, mine[s], s,
                                     send_sems, recv_sems, own, core)
                    pl.when(own != k)(lambda cp=cp: sending(cp))
                if landing is not None:
                    for k in range(N_CHIP):
                        @pl.when(own == k)
                        def _(k=k):
                            for s in rest:
                                _, rows, at = mine[s]
                                if at // SHARD_IN == k:
                                    for chip in range(N_CHIP):
                                        if chip != k:
                                            landing(_from_chip(pair_ref.at[k, pl.ds(at % SHARD_IN, rows)], red_ref, mine[s], s,
                                                               send_sems, recv_sems, chip, (x, y, c)))

    def local(ins, outs, sems):
        x, y, _ = _place()
        return pltpu.make_async_copy(ins[0].at[2 * x + y], outs[0].at[2 * x + y], sems[2])

    def start(ins, outs, sems):
        local(ins, outs, sems).start()
        each(ins, outs, sems, lambda cp: cp.start(), None)

    def finish(ins, outs, sems):
        each(ins, outs, sems, lambda cp: cp.wait_send(), lambda cp: cp.wait_recv())
        local(ins, outs, sems).wait()

    return _Ride([pair, red], [jax.ShapeDtypeStruct(red.shape, red.dtype)],
                 [pltpu.SemaphoreType.DMA((nc,)), pltpu.SemaphoreType.DMA((N_CHIP, nc)), pltpu.SemaphoreType.DMA],
                 start, finish, aliases={1: 0})


def _pair_sum_scatter(parts, recvs, core, name):
    na = len(parts)
    _, r, cdim = parts[0].shape
    tr = min(r, 416 if r % 416 == 0 else 128)

    def body(core_ref, *refs):
        del core_ref
        for a in range(na):
            refs[2 * na + a][...] = (refs[a][...] + refs[na + a][...]).astype(BF16)

    blk = (None, tr, cdim)
    mine = pl.BlockSpec(blk, lambda k, i, core_ref: (2 * k + core_ref[0], i, 0))
    slot = pl.BlockSpec(blk, lambda k, i, core_ref: (k, i, 0))
    return pl.pallas_call(
        body, name=name,
        grid_spec=pltpu.PrefetchScalarGridSpec(
            num_scalar_prefetch=1, grid=(N_CHIP, r // tr),
            in_specs=[mine] * na + [slot] * na, out_specs=[slot] * na),
        out_shape=[jax.ShapeDtypeStruct((N_CHIP, r, cdim), BF16)] * na,
        compiler_params=_params("parallel", "parallel"))(core, *parts, *recvs)


def _allreduce_small(pack, name):
    shape = pack.shape

    def body(x_ref, o_ref, sib_ref, chip_ref, send_sems, recv_sems):
        x, y, c = _place()
        own = 2 * x + y
        chips = [(1 - x, y), (x, 1 - y), (1 - x, 1 - y)]
        to_sibling = pltpu.make_async_remote_copy(
            src_ref=x_ref, dst_ref=sib_ref, send_sem=send_sems.at[0], recv_sem=recv_sems.at[0],
            device_id=(x, y, 1 - c), device_id_type=MESH)
        to_sibling.start()
        to_sibling.wait()
        chip_ref[own] = x_ref[...] + sib_ref[...]
        sent = [pltpu.make_async_remote_copy(
            src_ref=chip_ref.at[own], dst_ref=chip_ref.at[own], send_sem=send_sems.at[1 + j],
            recv_sem=recv_sems.at[1 + own], device_id=(*chip, c), device_id_type=MESH) for j, chip in enumerate(chips)]
        for cp in sent:
            cp.start()
        for chip in chips:
            k = 2 * chip[0] + chip[1]
            pltpu.make_async_remote_copy(
                src_ref=chip_ref.at[k], dst_ref=chip_ref.at[k], send_sem=send_sems.at[1],
                recv_sem=recv_sems.at[1 + k], device_id=(*chip, c), device_id_type=MESH).wait_recv()
        for cp in sent:
            cp.wait_send()
        o_ref[...] = (chip_ref[0] + chip_ref[1]) + (chip_ref[2] + chip_ref[3])

    return pl.pallas_call(
        body, name=name, out_shape=jax.ShapeDtypeStruct(shape, F32),
        in_specs=[pl.BlockSpec(memory_space=pltpu.VMEM)], out_specs=pl.BlockSpec(memory_space=pltpu.VMEM),
        scratch_shapes=[pltpu.VMEM(shape, F32), pltpu.VMEM((N_CHIP, *shape), F32),
                        pltpu.SemaphoreType.DMA((4,)), pltpu.SemaphoreType.DMA((1 + N_CHIP,))],
    )(pack)


def _adamw(g, w, m, v):
    m = ADAM_B1 * m + (1.0 - ADAM_B1) * g
    v = ADAM_B2 * v + (1.0 - ADAM_B2) * (g * g)
    m_hat = m / (1.0 - ADAM_B1 ** ADAM_STEP)
    v_hat = v / (1.0 - ADAM_B2 ** ADAM_STEP)
    delta = -ADAM_LR * (m_hat / (jnp.sqrt(v_hat) + ADAM_EPS) + ADAM_WD * w)
    return delta, m, v


def _adam_parts(parts, w, m, v, name, tr=None):
    npart, r, c = parts.shape
    tr = r if tr is None else min(tr, r)

    def body(p_ref, w_ref, m_ref, v_ref, g_ref, d_ref, nm_ref, nv_ref):
        g = p_ref[0].astype(F32)
        for k in range(1, npart):
            g = g + p_ref[k].astype(F32)
        g_ref[...] = g
        d_ref[...], nm_ref[...], nv_ref[...] = _adamw(g, w_ref[...], m_ref[...], v_ref[...])

    tile = pl.BlockSpec((tr, c), lambda i: (i, 0))
    return pl.pallas_call(
        body, name=name, grid=(r // tr,),
        in_specs=[pl.BlockSpec((npart, tr, c), lambda i: (0, i, 0)), tile, tile, tile],
        out_specs=[tile] * 4, out_shape=[jax.ShapeDtypeStruct((r, c), F32)] * 4,
        compiler_params=_params("parallel"))(parts, w, m, v)


def _block_diag(w):
    w4 = w.reshape(N_GROUPS, 4, RNN_BLOCK_W, RNN_BLOCK_W)
    eye = jnp.eye(4, dtype=w.dtype)
    return jnp.einsum("gbij,bc->gbicj", w4, eye).reshape(N_GROUPS, GROUP_W, GROUP_W).astype(BF16)


SMALL_ROWS = 16
ROW_PRE_G, ROW_BGATE, ROW_CONV_B, ROW_B_A, ROW_B_X, ROW_LAM, ROW_POST_G, ROW_LOSS, ROW_SINKS, ROW_CONV_W = 0, 1, 3, 4, 5, 6, 7, 8, 9, 10


def _pack_stats(st_pre, st_merge, st_rnn, st_post, st_sink, name):
    d = D_MODEL

    def body(pre_ref, mg_ref, rnn_ref, post_ref, sink_ref, o_ref):
        rnn = rnn_ref[...]
        sinks = jnp.concatenate([sink_ref[0:1, :], jnp.zeros((1, d - LANE), F32)], axis=1)
        o_ref[0:8, :] = _rows8([pre_ref[0:1, :], mg_ref[0:1, :], mg_ref[1:2, :], rnn[0:1], rnn[1:2], rnn[2:3], rnn[3:4],
                                post_ref[0:1, :]], d)
        o_ref[8:16, :] = _rows8([post_ref[1:2, :], sinks, rnn[4:5], rnn[5:6], rnn[6:7], rnn[7:8]], d)

    return pl.pallas_call(body, name=name, out_shape=jax.ShapeDtypeStruct((SMALL_ROWS, d), F32))(
        st_pre, st_merge, st_rnn, st_post, st_sink)


def _adam_small(total, w, m, v, name):
    d = D_MODEL
    n_in = len(w)

    def pack(refs):
        pre, bg, cbias, ba, bx, lam, post, sinks = [r[...] for r in refs]
        top = _rows8([pre, bg[:, 0:d], bg[:, d:2 * d], cbias, ba, bx, lam, post], d)
        return jnp.concatenate([top, _rows8([jnp.zeros((1, d), F32), sinks], d)], axis=0)

    def body(*refs):
        p_ref = refs[0]
        w_refs, m_refs, v_refs = (refs[1 + k * n_in:1 + (k + 1) * n_in] for k in range(3))
        outs = refs[1 + 3 * n_in:]
        g = p_ref[...]
        res = (g,) + _adamw(g, pack(w_refs), pack(m_refs), pack(v_refs))
        for k in range(4):
            outs[k][...] = res[k]
            outs[4 + k][...] = jnp.concatenate([res[k][ROW_BGATE:ROW_BGATE + 1], res[k][ROW_BGATE + 1:ROW_BGATE + 2]], axis=1)

    return pl.pallas_call(
        body, name=name,
        out_shape=[jax.ShapeDtypeStruct((SMALL_ROWS, d), F32)] * 4 + [jax.ShapeDtypeStruct((1, 2 * d), F32)] * 4,
    )(total, *w, *m, *v)


def kernel(x, pre_norm_g, w_in, b_gate, conv_w, conv_b, w_rg_a, b_rg_a, w_rg_x, b_rg_x, lru_lambda, attn_sinks, w_rnn_out, w_attn_out, w_out, post_norm_g, loss_target, m_pre_norm_g, m_w_in, m_b_gate, m_conv_w, m_conv_b, m_w_rg_a, m_b_rg_a, m_w_rg_x, m_b_rg_x, m_lru_lambda, m_attn_sinks, m_w_rnn_out, m_w_attn_out, m_w_out, m_post_norm_g, v_pre_norm_g, v_w_in, v_b_gate, v_conv_w, v_conv_b, v_w_rg_a, v_b_rg_a, v_w_rg_x, v_b_rg_x, v_lru_lambda, v_attn_sinks, v_w_rnn_out, v_w_attn_out, v_w_out, v_post_norm_g):
    cx, cy, cc = _place()
    dev = 4 * cx + 2 * cy + cc
    core = jnp.reshape(cc, (1,)).astype(jnp.int32)

    w_in_t, m_in_t, v_in_t = (jnp.transpose(a[0]) for a in (w_in, m_w_in, v_w_in))
    wt_shard = w_in_t.astype(BF16)

    def project(xs, pre_g):
        h, rx_rg, qkv, ag_ml, wt_all = _gather_project(xs, pre_g, wt_shard, "gather_project")
        return h, rx_rg, qkv, ag_ml, wt_all.reshape(D_IN, D_MODEL)

    def late_unpack(landed):
        w_rnn_all, w_attn_all, w_out_all, cw_all = landed
        return dict(w_rnn=w_rnn_all.reshape(D_RNN, D_MODEL), w_attn=w_attn_all.reshape(D_MODEL, D_MODEL),
                    w_out=w_out_all.reshape(D_MODEL, D_MODEL), cw=jnp.transpose(cw_all, (1, 0, 2)).reshape(4, D_RNN))

    late_weights = (_gather_ride([w_rnn_out[0].astype(BF16), w_attn_out[0].astype(BF16), w_out[0].astype(BF16), conv_w[0]]),
                    late_unpack)

    heads = jnp.arange(1, N_Q_HEADS + 1, dtype=F32)
    slopes = jnp.exp2(-ALIBI_MAX_BIAS * heads / N_Q_HEADS)
    b_a = b_rg_a.reshape(1, D_RNN)
    b_x = b_rg_x.reshape(1, D_RNN)
    p = dict(
        pre_g=pre_norm_g, post_g=post_norm_g, b_gate=b_gate, cb=conv_b,
        wbd_a=_block_diag(w_rg_a[0]), b_a=b_a, wbd_x=_block_diag(w_rg_x[0]), b_x=b_x, lam=lru_lambda,
        sm=jnp.pad(attn_sinks, ((0, 1), (0, 0))) + jnp.pad(slopes[None, :], ((1, 0), (0, 0))))

    flat = (RNN_BLOCKS * RNN_BLOCK_W, RNN_BLOCK_W)

    out_scatter = []

    def out_sibling(gw_rnn, gw_attn, gw_out):
        out_scatter.extend(gw.reshape(N_DEV, SHARD_OUT, D_MODEL) for gw in (gw_rnn, gw_attn, gw_out))
        return _sibling_ride(out_scatter, [])

    def out_chips(from_sibling, g_rg_a, g_rg_x):
        return _join_rides(_chips_ride(_pair_sum_scatter(out_scatter, from_sibling, core, "pair_out"), []),
                           _gather_ride([g_rg_a.reshape(flat), g_rg_x.reshape(flat)]))

    reduce_in = _chips_rest_ride

    g = _local_grads(x[0], loss_target[0], p, project, late_weights, (out_sibling, out_chips), reduce_in)
    small = _allreduce_small(
        _pack_stats(g["st_pre"], g["st_merge"], g["st_rnn"], g["st_post"], g["st_sink"], "pack_stats"), "allreduce_small")

    out = {}
    red = g["red_out"]
    out["w_in"] = [jnp.transpose(o) for o in _adam_parts(g["red_in"][0], w_in_t, m_in_t, v_in_t, "adam_w_in", tr=SHARD_IN // 2)]
    out["w_rnn_out"] = _adam_parts(red[0], w_rnn_out[0], m_w_rnn_out[0], v_w_rnn_out[0], "adam_w_rnn_out")
    out["w_attn_out"] = _adam_parts(red[1], w_attn_out[0], m_w_attn_out[0], v_w_attn_out[0], "adam_w_attn_out")
    out["w_out"] = _adam_parts(red[2], w_out[0], m_w_out[0], v_w_out[0], "adam_w_out")
    out["w_rg_a"] = _adam_parts(red[3], w_rg_a.reshape(flat), m_w_rg_a.reshape(flat), v_w_rg_a.reshape(flat), "adam_w_rg_a", tr=256)
    out["w_rg_x"] = _adam_parts(red[4], w_rg_x.reshape(flat), m_w_rg_x.reshape(flat), v_w_rg_x.reshape(flat), "adam_w_rg_x", tr=256)

    def rows(pre, bg, cbias, ba, bx, lam, post, sinks):
        return (pre, bg, cbias, ba.reshape(1, D_RNN), bx.reshape(1, D_RNN), lam, post,
                jnp.pad(sinks, ((0, 0), (0, D_MODEL - N_Q_HEADS))))

    small_out = _adam_small(
        small,
        rows(pre_norm_g, b_gate, conv_b, b_rg_a, b_rg_x, lru_lambda, post_norm_g, attn_sinks),
        rows(m_pre_norm_g, m_b_gate, m_conv_b, m_b_rg_a, m_b_rg_x, m_lru_lambda, m_post_norm_g, m_attn_sinks),
        rows(v_pre_norm_g, v_b_gate, v_conv_b, v_b_rg_a, v_b_rg_x, v_lru_lambda, v_post_norm_g, v_attn_sinks),
        "adam_small")
    packed, bgate_out = small_out[:4], small_out[4:]
    g_cw = lax.dynamic_slice(packed[0][ROW_CONV_W:ROW_CONV_W + 4], (0, dev * SHARD_OUT), (4, SHARD_OUT))
    out["conv_w"] = _adam_parts(g_cw[None], conv_w[0], m_conv_w[0], v_conv_w[0], "adam_conv_w")

    def unpack(kind, name):
        if name == "b_gate":
            return bgate_out[kind]
        row = dict(pre_norm_g=ROW_PRE_G, conv_b=ROW_CONV_B, b_rg_a=ROW_B_A, b_rg_x=ROW_B_X, lru_lambda=ROW_LAM,
                   post_norm_g=ROW_POST_G, attn_sinks=ROW_SINKS)[name]
        r = packed[kind][row:row + 1]
        if name == "attn_sinks":
            return r[:, 0:N_Q_HEADS]
        if name in ("b_rg_a", "b_rg_x"):
            return r.reshape(1, RNN_BLOCKS, RNN_BLOCK_W)
        return r

    shapes = dict(w_in=(1, D_MODEL, SHARD_IN), w_rnn_out=(1, SHARD_OUT, D_MODEL), w_attn_out=(1, SHARD_OUT, D_MODEL),
                  w_out=(1, SHARD_OUT, D_MODEL), w_rg_a=(1, RNN_BLOCKS, RNN_BLOCK_W, RNN_BLOCK_W),
                  w_rg_x=(1, RNN_BLOCKS, RNN_BLOCK_W, RNN_BLOCK_W), conv_w=(1, 4, SHARD_OUT))
    weights = ["pre_norm_g", "w_in", "b_gate", "conv_w", "conv_b", "w_rg_a", "b_rg_a", "w_rg_x", "b_rg_x",
               "lru_lambda", "attn_sinks", "w_rnn_out", "w_attn_out", "w_out", "post_norm_g"]
    results = []
    for kind in range(4):
        for name in weights:
            if name in out:
                results.append(out[name][kind].reshape(shapes[name]))
            else:
                results.append(unpack(kind, name))
    loss = 0.5 / D_MODEL * jnp.sum(packed[0][ROW_LOSS])
    return (loss, g["grad_x"][None], *results)
```

```python
import functools

import jax
import jax.numpy as jnp
from jax import lax
from jax.experimental import pallas as pl
from jax.experimental.pallas import tpu as pltpu

F32 = jnp.float32
BF16 = jnp.bfloat16

D_MODEL = 1024
D_RNN = 1024
RNN_BLOCKS = 16
RNN_BLOCK_W = 64
LRU_C = 8.0
N_Q_HEADS = 16
HEAD_DIM = 64
D_KV = 256
BLOCK = 128
ALIBI_MAX_BIAS = 8.0
EPS = 1e-6
D_IN = 6656
N_DEV = 8
N_CHIP = 4
SHARD_IN = D_IN // N_DEV
SHARD_OUT = D_MODEL // N_DEV
ATTN_SCALE = HEAD_DIM ** -0.5
MASKED = -1e30

ADAM_LR = 0.001
ADAM_B1 = 0.9
ADAM_B2 = 0.999
ADAM_EPS = 1e-08
ADAM_WD = 0.01
ADAM_STEP = 10

VMEM_LIMIT_BYTES = 52 * 1024 * 1024
LANE = 128
GROUP_W = 256
N_GROUPS = D_RNN // GROUP_W
SEG_CHUNK = 512

NT_DIMS = (((1,), (1,)), ((), ()))
TN_DIMS = (((0,), (0,)), ((), ()))
MESH = pl.DeviceIdType.MESH
ANY = pl.BlockSpec(memory_space=pl.ANY)


def _params(*semantics):
    return pltpu.CompilerParams(dimension_semantics=semantics, vmem_limit_bytes=VMEM_LIMIT_BYTES)


def _sigmoid(x):
    return 0.5 * jnp.tanh(0.5 * x) + 0.5


def _log1p(e):
    u = 1.0 + e
    den = jnp.where(u == 1.0, 1.0, u - 1.0)
    return jnp.where(u == 1.0, e, jnp.log(u) * (e / den))


def _softplus(z):
    return jnp.maximum(z, 0.0) + _log1p(jnp.exp(-jnp.abs(z)))


def _rows8(rows, width):
    idx = lax.broadcasted_iota(jnp.int32, (8, width), 0)
    out = jnp.zeros((8, width), F32)
    for r, v in enumerate(rows):
        out = jnp.where(idx == r, v, out)
    return out


class _Ride:
    def __init__(self, arrays, out_shapes, scratch_shapes, start, finish, middle=None, aliases=None):
        self.arrays, self.out_shapes, self.scratch_shapes = list(arrays), list(out_shapes), list(scratch_shapes)
        self.start, self.finish, self.middle = start, finish, middle
        self.aliases = dict(aliases or {})


class _Hosted:
    def __init__(self, ride, n_in, n_out, n_scratch=0, aliasing=False):
        self.ride = ride
        assert aliasing or not (ride and ride.aliases), "this host does not alias"
        self.aliases = {n_in + i: n_out + o for i, o in ride.aliases.items()} if ride else {}
        self.sizes = (n_in, len(ride.arrays) if ride else 0, n_out, len(ride.out_shapes) if ride else 0, n_scratch)
        self.arrays = ride.arrays if ride else []
        self.in_specs = [ANY] * len(self.arrays)
        self.out_shapes = ride.out_shapes if ride else []
        self.out_specs = [ANY] * len(self.out_shapes)
        self.scratch_shapes = ride.scratch_shapes if ride else []

    def split(self, refs):
        n_in, r_in, n_out, r_out, n_scr = self.sizes
        cuts = [0, n_in, n_in + r_in, n_in + r_in + n_out, n_in + r_in + n_out + r_out, n_in + r_in + n_out + r_out + n_scr]
        host_in, ride_in, host_out, ride_out, host_scr = (refs[cuts[k]:cuts[k + 1]] for k in range(5))
        ride_scr = refs[cuts[5]:]

        def start(when):
            if self.ride is not None:
                pl.when(when)(lambda: self.ride.start(ride_in, ride_out, ride_scr))

        def finish(when):
            if self.ride is not None:
                pl.when(when)(lambda: self.ride.finish(ride_in, ride_out, ride_scr))

        def middle(when):
            if self.ride is not None and self.ride.middle is not None:
                pl.when(when)(lambda: self.ride.middle(ride_in, ride_out, ride_scr))

        start.middle = middle
        return tuple(host_in) + tuple(host_out) + tuple(host_scr), start, finish

    def results(self, outs, n_out):
        outs = list(outs) if isinstance(outs, (list, tuple)) else [outs]
        return outs[:n_out], outs[n_out:]


def _join_rides(a, b):
    na, nb_ = len(a.arrays), len(b.arrays)
    oa = len(a.out_shapes)
    sa = len(a.scratch_shapes)

    def both(fa, fb):
        def run(ins, outs, sems):
            if fa is not None:
                fa(ins[:na], outs[:oa], sems[:sa])
            if fb is not None:
                fb(ins[na:na + nb_], outs[oa:], sems[sa:])
        return run

    middle = both(a.middle, b.middle) if (a.middle or b.middle) else None
    return _Ride(a.arrays + b.arrays, a.out_shapes + b.out_shapes, a.scratch_shapes + b.scratch_shapes,
                 both(a.start, b.start), both(a.finish, b.finish), middle)


def _load_resident(w_hbm, w_vmem, sems, first):
    def piece(c):
        rows = pl.ds(c * SEG_CHUNK, SEG_CHUNK)
        return pltpu.make_async_copy(w_hbm.at[rows], w_vmem.at[rows], sems.at[c])

    @pl.when(first)
    def _():
        for c in range(w_vmem.shape[0] // SEG_CHUNK):
            piece(c).start()

    def ready(c):
        @pl.when(first)
        def _():
            piece(c).wait()

    return ready


CHIP_ROWS = 2 * SHARD_IN
PROJ_WIDTHS = (2 * D_RNN, D_MODEL + 2 * D_KV, 3 * D_MODEL)
PROJ_DTYPES = (F32, BF16, BF16)


def _chip_pieces():
    starts = [0, PROJ_WIDTHS[0], PROJ_WIDTHS[0] + PROJ_WIDTHS[1], D_IN]
    pieces = []
    for k in range(N_CHIP):
        lo, hi = k * CHIP_ROWS, (k + 1) * CHIP_ROWS
        cur = []
        for a in range(len(PROJ_WIDTHS)):
            s0, s1 = max(lo, starts[a]), min(hi, starts[a + 1])
            if s0 < s1:
                cur.append((a, s0 - starts[a], s1 - s0, s0 - lo))
        pieces.append(cur)
    return pieces


def _gather_project(x, g, wt_shard, name, tm=1024):
    t, k = x.shape
    tm = min(tm, t)
    nt = t // tm
    pieces = _chip_pieces()

    def body(x_ref, g_ref, shard_ref, h_out, rx_ref, qkv_ref, ag_ref, wt_all,
             w_c, stage, o32, o16, h_all, send_sems, recv_sems, local_sem, stage_sems, out_sems, h_sems):
        s, ti = pl.program_id(0), pl.program_id(1)
        px, py, pc = _place()
        me, sibling = (px, py, pc), (px, py, 1 - pc)
        chips = [(px, py), (1 - px, py), (px, 1 - py), (1 - px, 1 - py)]
        outs = (rx_ref, qkv_ref, ag_ref)

        def slot(dev):
            return wt_all.at[4 * dev[0] + 2 * dev[1] + dev[2]]

        def copy(kk, block, to, src=None):
            return pltpu.make_async_remote_copy(
                src_ref=slot(block) if src is None else src, dst_ref=slot(block),
                send_sem=send_sems.at[kk], recv_sem=recv_sems.at[kk], device_id=to, device_id_type=MESH)

        mine = pltpu.make_async_copy(shard_ref, slot(me), local_sem)
        first = [copy(0, me, sibling, src=shard_ref)] + [copy(1 + j, me, (*chips[1 + j], pc), src=shard_ref) for j in range(3)]
        passed = [copy(4 + j, (*chips[1 + j], pc), sibling) for j in range(3)]

        @pl.when((s == 0) & (ti == 0))
        def _():
            mine.start()
            for cp in first[:3]:
                cp.start()

        def pass_on(step):
            copy(step, (*chips[step], pc), me).wait_recv()
            passed[step - 1].start()

        @pl.when((s == 2) & (ti == nt - 1))
        def _():
            pass_on(3)

        for step in range(N_CHIP):
            @pl.when((s == step) & (ti == 0))
            def _(step=step):
                chip = chips[step]
                if step == 0:
                    mine.wait()
                    copy(0, sibling, me).wait_recv()
                else:
                    if step == 1:
                        pass_on(1)
                        first[3].start()
                        pass_on(2)
                    copy(3 + step, (*chip, 1 - pc), me).wait_recv()
                loads = [pltpu.make_async_copy(slot((*chip, core)), stage.at[pl.ds(core * SHARD_IN, SHARD_IN)], stage_sems.at[core])
                         for core in (0, 1)]
                for cp in loads:
                    cp.start()
                for cp in loads:
                    cp.wait()
                w_c[...] = stage[...].T

        rows = pl.ds(pl.multiple_of(ti * tm, tm), tm)

        @pl.when(s == 0)
        def _():
            xv = x_ref[...]
            h_all[rows, :] = (xv * lax.rsqrt(jnp.mean(xv * xv, axis=-1, keepdims=True) + EPS) * g_ref[...]).astype(BF16)

        res = jnp.dot(h_all[rows, :], w_c[...], preferred_element_type=F32)

        n = s * nt + ti
        buf = lax.rem(n, 2)
        chip_idx = [2 * cx + cy for cx, cy in chips]

        def chip_at(step):
            return jnp.where(step == 0, chip_idx[0], jnp.where(step == 1, chip_idx[1], jnp.where(step == 2, chip_idx[2], chip_idx[3])))

        def h_write(b, tile):
            return pltpu.make_async_copy(h_all.at[pl.ds(tile * tm, tm)], h_out.at[pl.ds(tile * tm, tm)], h_sems.at[b])

        def writes(kchip, b, tile):
            cps = []
            for idx, (a, col, w, src) in enumerate(pieces[kchip]):
                staged = (o16 if PROJ_DTYPES[a] == BF16 else o32).at[b, :, pl.ds(src, w)]
                cps.append(pltpu.make_async_copy(staged, outs[a].at[pl.ds(tile * tm, tm), pl.ds(col, w)], out_sems.at[b, idx]))
            return cps

        o32[buf] = res
        o16[buf] = res.astype(BF16)
        kcur = chip_at(s)

        @pl.when(n > 0)
        def _():
            kprev = chip_at(lax.div(n - 1, nt))
            for kchip in range(N_CHIP):
                @pl.when(kprev == kchip)
                def _(kchip=kchip):
                    for cp in writes(kchip, 1 - buf, lax.rem(n - 1, nt)):
                        cp.wait()

            @pl.when(n <= nt)
            def _():
                h_write(1 - buf, n - 1).wait()

        @pl.when(s == 0)
        def _():
            h_write(buf, ti).start()

        for kchip in range(N_CHIP):
            @pl.when(kcur == kchip)
            def _(kchip=kchip):
                for cp in writes(kchip, buf, ti):
                    cp.start()

        @pl.when(n == N_CHIP * nt - 1)
        def _():
            for kchip in range(N_CHIP):
                @pl.when(kcur == kchip)
                def _(kchip=kchip):
                    for cp in writes(kchip, buf, ti):
                        cp.wait()
            for cp in first + passed:
                cp.wait_send()

    tile = pl.BlockSpec((tm, k), lambda s, ti: (jnp.where(s == 0, ti, nt - 1), 0))
    return pl.pallas_call(
        body, name=name, grid=(N_CHIP, nt),
        in_specs=[tile, pl.BlockSpec((1, k), lambda s, ti: (0, 0)), ANY],
        out_specs=[ANY, ANY, ANY, ANY, ANY],
        out_shape=[jax.ShapeDtypeStruct((t, k), BF16)]
        + [jax.ShapeDtypeStruct((t, w), dt) for w, dt in zip(PROJ_WIDTHS, PROJ_DTYPES)]
        + [jax.ShapeDtypeStruct((N_DEV, SHARD_IN, k), BF16)],
        scratch_shapes=[pltpu.VMEM((k, CHIP_ROWS), BF16), pltpu.VMEM((CHIP_ROWS, k), BF16),
                        pltpu.VMEM((2, tm, CHIP_ROWS), F32), pltpu.VMEM((2, tm, CHIP_ROWS), BF16), pltpu.VMEM((t, k), BF16),
                        pltpu.SemaphoreType.DMA((7,)), pltpu.SemaphoreType.DMA((7,)), pltpu.SemaphoreType.DMA,
                        pltpu.SemaphoreType.DMA((2,)), pltpu.SemaphoreType.DMA((2, 2)), pltpu.SemaphoreType.DMA((2,))],
        compiler_params=_params("arbitrary", "arbitrary"))(x, g, wt_shard)


def _mm_tn(a, b, name, tm=512, tk=4096):
    ktok, m = a.shape
    n = b.shape[1]
    tk = min(tk, ktok)

    def body(a_ref, b_ref, o_ref):
        @pl.when(pl.program_id(1) == 0)
        def _():
            o_ref[...] = jnp.zeros_like(o_ref)

        o_ref[...] += lax.dot_general(a_ref[...], b_ref[...], TN_DIMS, preferred_element_type=F32)

    return pl.pallas_call(
        body, name=name, grid=(m // tm, ktok // tk),
        in_specs=[pl.BlockSpec((tk, tm), lambda i, kk: (kk, i)), pl.BlockSpec((tk, n), lambda i, kk: (kk, 0))],
        out_specs=pl.BlockSpec((tm, n), lambda i, kk: (i, 0)),
        out_shape=jax.ShapeDtypeStruct((m, n), F32),
        compiler_params=_params("parallel", "arbitrary"))(a, b)


def _mm_tn_pairs(prods, name):
    n_prod = len(prods)
    ktok, m = prods[0][0].shape
    n = prods[0][1].shape[1]
    half, blk = m // 2, m // N_DEV
    ns = 2 * n_prod
    kept = N_CHIP * blk

    def side_pieces(s, side):
        i, hf = divmod(s, 2)
        first = hf * N_DEV // 2
        return [((d - first) * blk, i * kept + (d // 2) * blk) for d in range(first, first + N_DEV // 2) if d % 2 == side]

    def body(*refs):
        a_hbm, b_hbm, pair_ref = refs[:n_prod], refs[n_prod:2 * n_prod], refs[2 * n_prod]
        a_buf, b_buf, res_buf, recv_all, pair_all, a_sems, b_sems, send_sems, recv_sems, out_sems = refs[2 * n_prod + 1:]
        step = pl.program_id(0)
        px, py, pc = _place()

        def fetch_a(s):
            return pltpu.make_async_copy(a_hbm[s // 2].at[:, pl.ds((s % 2) * half, half)], a_buf.at[s % 2], a_sems.at[s % 2])

        def fetch_b(i):
            return pltpu.make_async_copy(b_hbm[i], b_buf.at[i % 2], b_sems.at[i % 2])

        @pl.when(step == 0)
        def _():
            fetch_a(0).start()
            fetch_b(0).start()

        for s in range(ns):
            @pl.when(step == s)
            def _(s=s):
                if s + 1 < ns:
                    fetch_a(s + 1).start()
                if s % 2 == 0 and s // 2 + 1 < n_prod:
                    fetch_b(s // 2 + 1).start()
                fetch_a(s).wait()
                if s % 2 == 0:
                    fetch_b(s // 2).wait()

        res_buf[step % 2] = lax.dot_general(a_buf[step % 2], b_buf[(step // 2) % 2], TN_DIMS, preferred_element_type=F32)

        def crossing(s, j, piece):
            off, at = piece
            return pltpu.make_async_remote_copy(
                src_ref=res_buf.at[s % 2, pl.ds(off, blk)], dst_ref=recv_all.at[pl.ds(at, blk)],
                send_sem=send_sems.at[s, j], recv_sem=recv_sems.at[s, j], device_id=(px, py, 1 - pc), device_id_type=MESH)

        def write(s, core):
            at = side_pieces(s, core)[0][1]
            return pltpu.make_async_copy(pair_all.at[pl.ds(at, 2 * blk)], pair_ref.at[pl.ds(at, 2 * blk)], out_sems.at[s])

        for core in (0, 1):
            def settle(s, core=core):
                for j, piece in enumerate(side_pieces(s, 1 - core)):
                    crossing(s, j, piece).wait_send()
                for j, (off, at) in enumerate(side_pieces(s, core)):
                    crossing(s, j, (off, at)).wait_recv()
                    pair_all[at:at + blk] = (res_buf[s % 2, off:off + blk] + recv_all[at:at + blk]).astype(BF16)
                write(s, core).start()

            for s in range(ns):
                @pl.when((pc == core) & (step == s))
                def _(s=s, settle=settle, core=core):
                    for j, piece in enumerate(side_pieces(s, 1 - core)):
                        crossing(s, j, piece).start()
                    if s > 0:
                        settle(s - 1)
                    if s == ns - 1:
                        settle(s)
                        for t in range(ns):
                            write(t, core).wait()

    pair = pl.pallas_call(
        body, name=name, grid=(ns,),
        in_specs=[ANY] * (2 * n_prod), out_specs=ANY,
        out_shape=jax.ShapeDtypeStruct((n_prod * kept, n), BF16),
        scratch_shapes=[pltpu.VMEM((2, ktok, half), prods[0][0].dtype), pltpu.VMEM((2, ktok, n), prods[0][1].dtype),
                        pltpu.VMEM((2, half, n), F32), pltpu.VMEM((n_prod * kept, n), F32), pltpu.VMEM((n_prod * kept, n), BF16),
                        pltpu.SemaphoreType.DMA((2,)), pltpu.SemaphoreType.DMA((2,)),
                        pltpu.SemaphoreType.DMA((ns, 2)), pltpu.SemaphoreType.DMA((ns, 2)), pltpu.SemaphoreType.DMA((ns,))],
        compiler_params=_params("arbitrary"))(*[a for a, _ in prods], *[b for _, b in prods])
    pair = pair.reshape(n_prod, N_CHIP, blk, n)
    return [pair[i] for i in range(n_prod)]


def _segment_chunks(segs):
    bounds = [0]
    for s in segs:
        bounds.append(bounds[-1] + s.shape[1] // SEG_CHUNK)
    return bounds


def _input_grad(segs, wt, x, g, dy, name, tm=512, ride=None):
    m = segs[0].shape[0]
    rows, n = wt.shape
    tm = min(tm, m)
    bounds = _segment_chunks(segs)
    n_seg = len(segs)
    ni = m // tm
    host = _Hosted(ride, n_seg + 4, 2, 2, aliasing=True)

    def body(*refs):
        host_refs, start, finish = host.split(refs)
        a_refs = host_refs[:n_seg]
        wt_hbm, x_ref, g_ref, dy_ref, gx_ref, st_ref, wt_vmem, sems = host_refs[n_seg:]
        i = pl.program_id(0)
        start(i == 0)

        @pl.when(i == 0)
        def _():
            st_ref[...] = jnp.zeros_like(st_ref)

        ready = _load_resident(wt_hbm, wt_vmem, sems, i == 0)
        dh = None
        for s in range(n_seg):
            for c in range(bounds[s], bounds[s + 1]):
                ready(c)
            part = jnp.dot(a_refs[s][...], wt_vmem[bounds[s] * SEG_CHUNK:bounds[s + 1] * SEG_CHUNK, :], preferred_element_type=F32)
            dh = part if dh is None else dh + part
        xv = x_ref[...]
        r = lax.rsqrt(jnp.mean(xv * xv, axis=-1, keepdims=True) + EPS)
        xn = xv * r
        dxn = dh * g_ref[...]
        gx_ref[...] = dy_ref[...] + r * (dxn - xn * jnp.mean(dxn * xn, axis=-1, keepdims=True))
        st_ref[...] += _rows8([jnp.sum(dh * xn, axis=0, keepdims=True)], n)
        finish(i == ni - 1)

    tile = pl.BlockSpec((tm, n), lambda i: (i, 0))
    outs = pl.pallas_call(
        body, name=name, grid=(ni,),
        in_specs=[pl.BlockSpec((tm, sg.shape[1]), lambda i: (i, 0)) for sg in segs]
        + [ANY, tile, pl.BlockSpec((1, n), lambda i: (0, 0)), tile] + host.in_specs,
        out_specs=[tile, pl.BlockSpec((8, n), lambda i: (0, 0))] + host.out_specs,
        out_shape=[jax.ShapeDtypeStruct((m, n), F32), jax.ShapeDtypeStruct((8, n), F32)] + host.out_shapes,
        scratch_shapes=[pltpu.VMEM((rows, n), wt.dtype), pltpu.SemaphoreType.DMA((rows // SEG_CHUNK,))] + host.scratch_shapes,
        input_output_aliases=host.aliases,
        compiler_params=_params("arbitrary"))(*segs, wt, x, g, dy, *host.arrays)
    res, landed = host.results(outs, 2)
    return (*res, landed) if ride else tuple(res)


def _mm_tn_seg(segs, b, name):
    ktok = segs[0].shape[0]
    n = b.shape[1]
    bounds = _segment_chunks(segs)
    n_seg = len(segs)
    nc = bounds[-1]
    seg_of = [s for s in range(n_seg) for _ in range(bounds[s], bounds[s + 1])]

    def body(*refs):
        a_hbm, b_hbm, o_ref = refs[:n_seg], refs[n_seg], refs[n_seg + 1]
        a_buf, b_vmem, a_sems, b_sem = refs[n_seg + 2:]
        c = pl.program_id(0)

        def fetch(cc):
            s = seg_of[cc]
            cols = pl.ds((cc - bounds[s]) * SEG_CHUNK, SEG_CHUNK)
            return pltpu.make_async_copy(a_hbm[s].at[:, cols], a_buf.at[cc % 2], a_sems.at[cc % 2])

        @pl.when(c == 0)
        def _():
            whole = pltpu.make_async_copy(b_hbm, b_vmem, b_sem)
            whole.start()
            fetch(0).start()
            whole.wait()

        for cc in range(nc):
            @pl.when(c == cc)
            def _(cc=cc):
                if cc + 1 < nc:
                    fetch(cc + 1).start()
                fetch(cc).wait()

        o_ref[...] = lax.dot_general(a_buf[c % 2], b_vmem[...], TN_DIMS, preferred_element_type=F32)

    return pl.pallas_call(
        body, name=name, grid=(nc,),
        in_specs=[ANY] * (n_seg + 1), out_specs=pl.BlockSpec((SEG_CHUNK, n), lambda c: (c, 0)),
        out_shape=jax.ShapeDtypeStruct((nc * SEG_CHUNK, n), F32),
        scratch_shapes=[pltpu.VMEM((2, ktok, SEG_CHUNK), segs[0].dtype), pltpu.VMEM((ktok, n), b.dtype),
                        pltpu.SemaphoreType.DMA((2,)), pltpu.SemaphoreType.DMA],
        compiler_params=_params("arbitrary"))(*segs, b)


CHUNK_ORDER = (0, 4, 7, 10, 1, 5, 8, 11, 2, 6, 9, 12, 3)
EARLY_STEPS = 8


def _side_pieces():
    table = []
    for core in (0, 1):
        sides = ([None] * len(CHUNK_ORDER), [None] * len(CHUNK_ORDER))
        for s, cc in enumerate(CHUNK_ORDER):
            g0 = cc * SEG_CHUNK
            for d in range(N_DEV):
                lo, hi = max(g0, d * SHARD_IN), min(g0 + SEG_CHUNK, (d + 1) * SHARD_IN)
                if lo < hi:
                    side = sides[0 if d % 2 == core else 1]
                    assert side[s] is None
                    side[s] = (lo - g0, hi - lo, (d // 2) * SHARD_IN + lo - d * SHARD_IN)
        table.append(sides)
    return table


def _to_chip(src, red_ref, piece, s, send_sems, recv_sems, own, core):
    _, rows, at = piece
    k, r0 = divmod(at, SHARD_IN)
    return k, pltpu.make_async_remote_copy(
        src_ref=src, dst_ref=red_ref.at[own, pl.ds(r0, rows)], send_sem=send_sems.at[s], recv_sem=recv_sems.at[own, s],
        device_id=(k // 2, k % 2, core), device_id_type=MESH)


def _from_chip(src, red_ref, piece, s, send_sems, recv_sems, chip, me):
    _, rows, at = piece
    return pltpu.make_async_remote_copy(
        src_ref=src, dst_ref=red_ref.at[chip, pl.ds(at % SHARD_IN, rows)], send_sem=send_sems.at[s], recv_sem=recv_sems.at[chip, s],
        device_id=me, device_id_type=MESH)


def _mm_tn_seg_pair(segs, b, name):
    ktok = segs[0].shape[0]
    n = b.shape[1]
    bounds = _segment_chunks(segs)
    n_seg = len(segs)
    nc = bounds[-1]
    assert nc == len(CHUNK_ORDER)
    seg_of = [s for s in range(n_seg) for _ in range(bounds[s], bounds[s + 1])]
    table = _side_pieces()

    def body(*refs):
        a_hbm, b_hbm, pair_ref, red_ref = refs[:n_seg], refs[n_seg], refs[n_seg + 1], refs[n_seg + 2]
        (a_buf, b_vmem, res_buf, recv_all, pair_all, a_sems, b_sem, send_sems, recv_sems, out_sems,
         chip_send, chip_recv) = refs[n_seg + 3:]
        step = pl.program_id(0)
        px, py, pc = _place()
        own = 2 * px + py

        def fetch(s):
            sg = seg_of[CHUNK_ORDER[s]]
            cols = pl.ds((CHUNK_ORDER[s] - bounds[sg]) * SEG_CHUNK, SEG_CHUNK)
            return pltpu.make_async_copy(a_hbm[sg].at[:, cols], a_buf.at[s % 2], a_sems.at[s % 2])

        @pl.when(step == 0)
        def _():
            whole = pltpu.make_async_copy(b_hbm, b_vmem, b_sem)
            whole.start()
            fetch(0).start()
            whole.wait()

        for s in range(nc):
            @pl.when(step == s)
            def _(s=s):
                if s + 1 < nc:
                    fetch(s + 1).start()
                fetch(s).wait()

        res_buf[step % 2] = lax.dot_general(a_buf[step % 2], b_vmem[...], TN_DIMS, preferred_element_type=F32)

        def crossing(s, piece):
            off, rows, at = piece
            return pltpu.make_async_remote_copy(
                src_ref=res_buf.at[s % 2, pl.ds(off, rows)], dst_ref=recv_all.at[pl.ds(at, rows)],
                send_sem=send_sems.at[s], recv_sem=recv_sems.at[s], device_id=(px, py, 1 - pc), device_id_type=MESH)

        def write(s, piece):
            _, rows, at = piece
            return pltpu.make_async_copy(pair_all.at[pl.ds(at, rows)], pair_ref.at[pl.ds(at, rows)], out_sems.at[s])

        def to_chip(s, piece, core):
            return _to_chip(pair_all.at[pl.ds(piece[2], piece[1])], red_ref, piece, s, chip_send, chip_recv, own, core)

        for core in (0, 1):
            mine, theirs = table[core]

            def settle(s, mine=mine, theirs=theirs, core=core):
                if theirs[s] is not None:
                    crossing(s, theirs[s]).wait_send()
                if mine[s] is not None:
                    off, rows, at = mine[s]
                    crossing(s, mine[s]).wait_recv()
                    pair_all[at:at + rows] = (res_buf[s % 2, off:off + rows] + recv_all[at:at + rows]).astype(BF16)
                    write(s, mine[s]).start()
                    if s < EARLY_STEPS:
                        k, cp = to_chip(s, mine[s], core)
                        pl.when(own != k)(cp.start)

            for s in range(nc):
                @pl.when((pc == core) & (step == s))
                def _(s=s, settle=settle, mine=mine, theirs=theirs, core=core):
                    if theirs[s] is not None:
                        crossing(s, theirs[s]).start()
                    if s > 0:
                        settle(s - 1)
                    if s == nc - 1:
                        settle(s)
                        kept = [t for t in range(nc) if mine[t] is not None]
                        for t in kept:
                            write(t, mine[t]).wait()
                        early = [t for t in kept if t < EARLY_STEPS]
                        for t in early:
                            k, cp = to_chip(t, mine[t], core)
                            pl.when(own != k)(cp.wait_send)
                        for k in range(N_CHIP):
                            @pl.when(own == k)
                            def _(k=k):
                                for t in early:
                                    if mine[t][2] // SHARD_IN == k:
                                        for chip in range(N_CHIP):
                                            if chip != k:
                                                _from_chip(pair_all.at[pl.ds(mine[t][2], mine[t][1])], red_ref, mine[t], t,
                                                           chip_send, chip_recv, chip, (px, py, pc)).wait_recv()

    flat = jax.ShapeDtypeStruct((N_CHIP * SHARD_IN, n), BF16)
    pair, red = pl.pallas_call(
        body, name=name, grid=(nc,),
        in_specs=[ANY] * (n_seg + 1), out_specs=[ANY, ANY],
        out_shape=[flat, jax.ShapeDtypeStruct((N_CHIP, SHARD_IN, n), BF16)],
        scratch_shapes=[pltpu.VMEM((2, ktok, SEG_CHUNK), segs[0].dtype), pltpu.VMEM((ktok, n), b.dtype),
                        pltpu.VMEM((2, SEG_CHUNK, n), F32), pltpu.VMEM(flat.shape, F32), pltpu.VMEM(flat.shape, BF16),
                        pltpu.SemaphoreType.DMA((2,)), pltpu.SemaphoreType.DMA,
                        pltpu.SemaphoreType.DMA((nc,)), pltpu.SemaphoreType.DMA((nc,)), pltpu.SemaphoreType.DMA((nc,)),
                        pltpu.SemaphoreType.DMA((nc,)), pltpu.SemaphoreType.DMA((N_CHIP, nc))],
        compiler_params=_params("arbitrary"))(*segs, b)
    return pair.reshape(N_CHIP, SHARD_IN, n), red


def _branches_fwd(z_rnn, z_attn, ag_ml, b_gate, w_rnn, w_attn, w_out, x, target, g_post, name, tm=512):
    t, d = x.shape
    tm = min(tm, t)

    def body(zr_ref, za_ref, lr_ref, la_ref, br_ref, ba_ref, wr_ref, wa_ref, wo_ref, x_ref, t_ref, g_ref,
             brr_ref, bra_ref, mg_ref, do_ref, dy_ref, st_ref):
        @pl.when(pl.program_id(0) == 0)
        def _():
            st_ref[...] = jnp.zeros_like(st_ref)

        br_rnn = jnp.dot(zr_ref[...], wr_ref[...], preferred_element_type=F32)
        br_attn = jnp.dot(za_ref[...], wa_ref[...], preferred_element_type=F32)
        brr_ref[...] = br_rnn.astype(BF16)
        bra_ref[...] = br_attn.astype(BF16)
        g_rnn = _sigmoid(lr_ref[...].astype(F32) + br_ref[...])
        g_attn = _sigmoid(la_ref[...].astype(F32) + ba_ref[...])
        merged = (g_rnn * br_rnn + g_attn * br_attn).astype(BF16)
        mg_ref[...] = merged
        o = jnp.dot(merged, wo_ref[...], preferred_element_type=F32)
        g = g_ref[...]
        r = lax.rsqrt(jnp.mean(o * o, axis=-1, keepdims=True) + EPS)
        nrm = o * r
        err = x_ref[...] + nrm * g - t_ref[...]
        dy = err * (1.0 / d)
        dy_ref[...] = dy
        dn = dy * g
        do_ref[...] = (r * (dn - nrm * jnp.mean(dn * nrm, axis=-1, keepdims=True))).astype(BF16)
        st_ref[...] += _rows8([jnp.sum(dy * nrm, axis=0, keepdims=True), jnp.sum(err * err, axis=0, keepdims=True)], d)

    tile = pl.BlockSpec((tm, d), lambda i: (i, 0))
    weight = pl.BlockSpec((d, d), lambda i: (0, 0))
    bf = jax.ShapeDtypeStruct((t, d), BF16)
    return pl.pallas_call(
        body, name=name, grid=(t // tm,),
        in_specs=[tile, tile, pl.BlockSpec((tm, d), lambda i: (i, 1)), pl.BlockSpec((tm, d), lambda i: (i, 2)),
                  pl.BlockSpec((1, d), lambda i: (0, 0)), pl.BlockSpec((1, d), lambda i: (0, 1)),
                  weight, weight, weight, tile, tile, pl.BlockSpec((1, d), lambda i: (0, 0))],
        out_specs=[tile, tile, tile, tile, tile, pl.BlockSpec((8, d), lambda i: (0, 0))],
        out_shape=[bf, bf, bf, bf, jax.ShapeDtypeStruct((t, d), F32), jax.ShapeDtypeStruct((8, d), F32)],
        compiler_params=_params("arbitrary"))(z_rnn, z_attn, ag_ml, ag_ml, b_gate, b_gate, w_rnn, w_attn, w_out, x, target, g_post)


def _branches_bwd(dout, br_rnn, br_attn, ag_ml, b_gate, w_rnn, w_attn, w_out, name, tm=512):
    t, d = br_rnn.shape
    tm = min(tm, t)

    def body(do_ref, r_ref, a_ref, lr_ref, la_ref, br_ref, ba_ref, wr_ref, wa_ref, wo_ref,
             dr_ref, da_ref, dl_ref, dzr_ref, dza_ref, st_ref, wt_ref):
        @pl.when(pl.program_id(0) == 0)
        def _():
            st_ref[...] = jnp.zeros_like(st_ref)
            wt_ref[0] = wo_ref[...].T
            wt_ref[1] = wr_ref[...].T
            wt_ref[2] = wa_ref[...].T

        dm = jnp.dot(do_ref[...], wt_ref[0], preferred_element_type=F32)
        g_rnn = _sigmoid(lr_ref[...].astype(F32) + br_ref[...])
        g_attn = _sigmoid(la_ref[...].astype(F32) + ba_ref[...])
        dbr_rnn = (dm * g_rnn).astype(BF16)
        dbr_attn = (dm * g_attn).astype(BF16)
        dr_ref[...] = dbr_rnn
        da_ref[...] = dbr_attn
        dl_rnn = dm * r_ref[...].astype(F32) * g_rnn * (1.0 - g_rnn)
        dl_attn = dm * a_ref[...].astype(F32) * g_attn * (1.0 - g_attn)
        dl_ref[:, 0:d] = dl_rnn.astype(BF16)
        dl_ref[:, d:2 * d] = dl_attn.astype(BF16)
        st_ref[...] += _rows8([jnp.sum(dl_rnn, axis=0, keepdims=True), jnp.sum(dl_attn, axis=0, keepdims=True)], d)
        dzr_ref[...] = jnp.dot(dbr_rnn, wt_ref[1], preferred_element_type=F32).astype(BF16)
        dza_ref[...] = jnp.dot(dbr_attn, wt_ref[2], preferred_element_type=F32).astype(BF16)

    tile = pl.BlockSpec((tm, d), lambda i: (i, 0))
    weight = pl.BlockSpec((d, d), lambda i: (0, 0))
    bf = jax.ShapeDtypeStruct((t, d), BF16)
    return pl.pallas_call(
        body, name=name, grid=(t // tm,),
        in_specs=[tile, tile, tile, pl.BlockSpec((tm, d), lambda i: (i, 1)), pl.BlockSpec((tm, d), lambda i: (i, 2)),
                  pl.BlockSpec((1, d), lambda i: (0, 0)), pl.BlockSpec((1, d), lambda i: (0, 1)), weight, weight, weight],
        out_specs=[tile, tile, pl.BlockSpec((tm, 2 * d), lambda i: (i, 0)), tile, tile, pl.BlockSpec((8, d), lambda i: (0, 0))],
        out_shape=[bf, bf, jax.ShapeDtypeStruct((t, 2 * d), BF16), bf, bf, jax.ShapeDtypeStruct((8, d), F32)],
        scratch_shapes=[pltpu.VMEM((3, d, d), BF16)],
        compiler_params=_params("arbitrary"))(dout, br_rnn, br_attn, ag_ml, ag_ml, b_gate, b_gate, w_rnn, w_attn, w_out)


def _lru_gates(c, wa, ba, wx, bx, sp):
    cb = c.astype(BF16)
    r = _sigmoid(jnp.dot(cb, wa, preferred_element_type=F32) + ba)
    ig = _sigmoid(jnp.dot(cb, wx, preferred_element_type=F32) + bx)
    log_a = (-LRU_C) * r * sp
    a = jnp.exp(log_a)
    mult = jnp.sqrt(-jnp.tanh(log_a) * (a * a + 1.0))
    return cb, r, ig, a, mult


SUBLANES = 8


def _scan_fwd(a, u, carry, tt):
    w = a.shape[1]
    ng = tt // SUBLANES
    a3 = a.reshape(ng, SUBLANES, w)
    u3 = u.reshape(ng, SUBLANES, w)
    sub = lax.broadcasted_iota(jnp.int32, (ng, SUBLANES, w), 1)
    d = 1
    while d < SUBLANES:
        keep = sub >= d
        u3 = u3 + a3 * jnp.where(keep, pltpu.roll(u3, d, 1), 0.0)
        a3 = a3 * jnp.where(keep, pltpu.roll(a3, d, 1), 1.0)
        d *= 2
    out = []
    for g in range(ng):
        hg = u3[g] + a3[g] * carry
        out.append(hg)
        carry = hg[SUBLANES - 1:SUBLANES, :]
    return jnp.concatenate(out, axis=0)


def _scan_rev(b, g, carry, tt):
    w = b.shape[1]
    ng = tt // SUBLANES
    b3 = b.reshape(ng, SUBLANES, w)
    g3 = g.reshape(ng, SUBLANES, w)
    sub = lax.broadcasted_iota(jnp.int32, (ng, SUBLANES, w), 1)
    d = 1
    while d < SUBLANES:
        keep = sub < SUBLANES - d
        g3 = g3 + b3 * jnp.where(keep, pltpu.roll(g3, SUBLANES - d, 1), 0.0)
        b3 = b3 * jnp.where(keep, pltpu.roll(b3, SUBLANES - d, 1), 1.0)
        d *= 2
    out = [None] * ng
    for k in range(ng - 1, -1, -1):
        hk = g3[k] + b3[k] * carry
        out[k] = hk
        carry = hk[0:1, :]
    return jnp.concatenate(out, axis=0)


def _conv_taps(cw, bias, x, ext_ref, tt):
    x2 = ext_ref[7:7 + tt, :]
    x1 = ext_ref[6:6 + tt, :]
    x0 = ext_ref[5:5 + tt, :]
    c = bias + cw[3:4] * x + cw[2:3] * x2 + cw[1:2] * x1 + cw[0:1] * x0
    return c, x2, x1, x0


def _rnn_fwd(rx_rg, cw, cb, wa, ba, wx, bx, lam, name, tt=512):
    t = rx_rg.shape[0]
    tt = min(tt, t)
    w = GROUP_W

    def body(rx_ref, rg_ref, cw_ref, cb_ref, wa_ref, ba_ref, wx_ref, bx_ref, lam_ref, y_ref, z_ref, ext_ref, hc_ref):
        @pl.when(pl.program_id(1) == 0)
        def _():
            ext_ref[0:8, :] = jnp.zeros((8, w), F32)
            hc_ref[...] = jnp.zeros((8, w), F32)

        x = rx_ref[...]
        ext_ref[8:8 + tt, :] = x
        c, _, _, _ = _conv_taps(cw_ref[...], cb_ref[...], x, ext_ref, tt)
        ext_ref[0:8, :] = x[tt - 8:tt, :]
        sp = _softplus(-lam_ref[...])
        _, _, ig, a, mult = _lru_gates(c, wa_ref[...], ba_ref[...], wx_ref[...], bx_ref[...], sp)
        h = _scan_fwd(a, mult * (ig * c), hc_ref[7:8, :], tt)
        hc_ref[...] = h[tt - 8:tt, :]
        y_ref[...] = h
        rg = rg_ref[...]
        z_ref[...] = (h * rg * _sigmoid(rg)).astype(BF16)

    vec = pl.BlockSpec((1, w), lambda g, i: (0, g))
    mat = pl.BlockSpec((None, w, w), lambda g, i: (g, 0, 0))
    tile = pl.BlockSpec((tt, w), lambda g, i: (i, g))
    return pl.pallas_call(
        body, name=name, grid=(N_GROUPS, t // tt),
        in_specs=[tile, pl.BlockSpec((tt, w), lambda g, i: (i, N_GROUPS + g)),
                  pl.BlockSpec((4, w), lambda g, i: (0, g)), vec, mat, vec, mat, vec, vec],
        out_specs=[tile, tile],
        out_shape=[jax.ShapeDtypeStruct((t, D_RNN), F32), jax.ShapeDtypeStruct((t, D_RNN), BF16)],
        scratch_shapes=[pltpu.VMEM((tt + 8, w), F32), pltpu.VMEM((8, w), F32)],
        compiler_params=_params("parallel", "arbitrary"))(rx_rg, rx_rg, cw, cb, wa, ba, wx, bx, lam)


def _rnn_bwd(rx_rg, y, dz, cw, cb, wa, ba, wx, bx, lam, name, tt=512, ride=None):
    t = rx_rg.shape[0]
    tt = min(tt, t)
    nt = t // tt
    w = GROUP_W

    host = _Hosted(ride, 13, 5, 6)

    def body(*refs):
        host_refs, start, finish = host.split(refs)
        (rx_ref, rg_ref, rxt_ref, y_ref, yt_ref, dz_ref, cw_ref, cb_ref, wa_ref, ba_ref, wx_ref, bx_ref, lam_ref,
         drx_ref, drg_ref, st_ref, gda_ref, gdx_ref, ext_ref, dcx_ref, wcar_ref, acar_ref, dwa_ref, dwx_ref) = host_refs
        ii = pl.program_id(1)
        start((pl.program_id(0) == 0) & (ii == 0))

        @pl.when(ii == 0)
        def _():
            wcar_ref[...] = jnp.zeros((8, w), F32)
            acar_ref[...] = jnp.zeros((8, w), F32)
            dcx_ref[tt:tt + 8, :] = jnp.zeros((8, w), F32)
            st_ref[...] = jnp.zeros_like(st_ref)
            dwa_ref[...] = jnp.zeros_like(dwa_ref)
            dwx_ref[...] = jnp.zeros_like(dwx_ref)

        has_prev = jnp.where(ii == nt - 1, 0.0, 1.0)
        x = rx_ref[...]
        ext_ref[0:8, :] = rxt_ref[...] * has_prev
        ext_ref[8:8 + tt, :] = x
        cwv = cw_ref[...]
        c, x2, x1, x0 = _conv_taps(cwv, cb_ref[...], x, ext_ref, tt)
        lam = lam_ref[...]
        sp = _softplus(-lam)
        wa = wa_ref[...]
        wx = wx_ref[...]
        cb16, r, ig, a, mult = _lru_gates(c, wa, ba_ref[...], wx, bx_ref[...], sp)

        rg = rg_ref[...]
        sg = _sigmoid(rg)
        dz = dz_ref[...].astype(F32)
        yv = y_ref[...]
        drg_ref[...] = (dz * yv * (sg * (1.0 + rg * (1.0 - sg)))).astype(BF16)

        row = lax.broadcasted_iota(jnp.int32, (tt, w), 0)
        b = jnp.where(row < tt - 1, pltpu.roll(a, tt - 1, 0), acar_ref[0:1, :])
        dh = _scan_rev(b, dz * (rg * sg), wcar_ref[0:1, :], tt)
        wcar_ref[...] = dh[0:8, :]
        acar_ref[...] = a[0:8, :]

        hprev = jnp.where(row >= 1, pltpu.roll(yv, 1, 0), yt_ref[7:8, :] * has_prev)
        dmult = dh * (ig * c)
        dig = dh * mult * c
        dlog_a = dh * hprev * a - dmult * (a * a / mult)
        dpa = dlog_a * ((-LRU_C) * sp) * r * (1.0 - r)
        dpx = dig * ig * (1.0 - ig)
        dsp = jnp.sum(dlog_a * r, axis=0, keepdims=True) * (-LRU_C)
        dlam = dsp * (-_sigmoid(-lam))
        dpa16 = dpa.astype(BF16)
        dpx16 = dpx.astype(BF16)
        dwa_ref[...] += lax.dot_general(cb16, dpa16, TN_DIMS, preferred_element_type=F32)
        dwx_ref[...] += lax.dot_general(cb16, dpx16, TN_DIMS, preferred_element_type=F32)
        dc = (dh * mult * ig
              + lax.dot_general(dpa16, wa, NT_DIMS, preferred_element_type=F32)
              + lax.dot_general(dpx16, wx, NT_DIMS, preferred_element_type=F32))

        dcx_ref[0:tt, :] = dc
        drx = (cwv[3:4] * dc + cwv[2:3] * dcx_ref[1:1 + tt, :] + cwv[1:2] * dcx_ref[2:2 + tt, :]
               + cwv[0:1] * dcx_ref[3:3 + tt, :])
        drx_ref[...] = drx.astype(BF16)
        dcx_ref[tt:tt + 8, :] = dc[0:8, :]

        def colsum(v):
            return jnp.sum(v, axis=0, keepdims=True)

        st_ref[...] += _rows8([colsum(dc), colsum(dpa), colsum(dpx), dlam,
                               colsum(dc * x0), colsum(dc * x1), colsum(dc * x2), colsum(dc * x)], w)

        @pl.when(ii == nt - 1)
        def _():
            for blk in range(GROUP_W // RNN_BLOCK_W):
                rows = slice(blk * RNN_BLOCK_W, (blk + 1) * RNN_BLOCK_W)
                gda_ref[blk] = dwa_ref[rows, rows]
                gdx_ref[blk] = dwx_ref[rows, rows]

        finish((pl.program_id(0) == N_GROUPS - 1) & (ii == nt - 1))

    def rev(ii):
        return nt - 1 - ii

    def tail(g, ii):
        return (jnp.maximum(rev(ii) * (tt // 8) - 1, 0), g)

    vec = pl.BlockSpec((1, w), lambda g, ii: (0, g))
    mat = pl.BlockSpec((None, w, w), lambda g, ii: (g, 0, 0))
    tile = pl.BlockSpec((tt, w), lambda g, ii: (rev(ii), g))
    diag_shape = (N_GROUPS, GROUP_W // RNN_BLOCK_W, RNN_BLOCK_W, RNN_BLOCK_W)
    diag = pl.BlockSpec((None,) + diag_shape[1:], lambda g, ii: (g, 0, 0, 0))
    outs = pl.pallas_call(
        body, name=name, grid=(N_GROUPS, nt),
        in_specs=[tile, pl.BlockSpec((tt, w), lambda g, ii: (rev(ii), N_GROUPS + g)), pl.BlockSpec((8, w), tail),
                  tile, pl.BlockSpec((8, w), tail), tile,
                  pl.BlockSpec((4, w), lambda g, ii: (0, g)), vec, mat, vec, mat, vec, vec] + host.in_specs,
        out_specs=[tile, tile, pl.BlockSpec((8, w), lambda g, ii: (0, g)), diag, diag] + host.out_specs,
        out_shape=[jax.ShapeDtypeStruct((t, D_RNN), BF16), jax.ShapeDtypeStruct((t, D_RNN), BF16),
                   jax.ShapeDtypeStruct((8, D_RNN), F32),
                   jax.ShapeDtypeStruct(diag_shape, F32), jax.ShapeDtypeStruct(diag_shape, F32)] + host.out_shapes,
        scratch_shapes=[pltpu.VMEM((tt + 8, w), F32), pltpu.VMEM((tt + 8, w), F32), pltpu.VMEM((8, w), F32),
                        pltpu.VMEM((8, w), F32), pltpu.VMEM((w, w), F32), pltpu.VMEM((w, w), F32)] + host.scratch_shapes,
        compiler_params=_params("arbitrary" if ride else "parallel", "arbitrary"))(
            rx_rg, rx_rg, rx_rg, y, y, dz, cw, cb, wa, ba, wx, bx, lam, *host.arrays)
    res, landed = host.results(outs, 5)
    return (*res, landed) if ride else tuple(res)


def _half_mask(shape, half):
    lane = lax.broadcasted_iota(jnp.int32, shape, 1)
    return (lane >= HEAD_DIM) if half else (lane < HEAD_DIM)


def _dup_half(t, half):
    sel = jnp.where(_half_mask(t.shape, half), t, 0.0)
    return sel + pltpu.roll(sel, HEAD_DIM, 1)


def _band_geometry(n):
    qi = lax.broadcasted_iota(jnp.int32, (BLOCK, 2 * BLOCK), 0)
    kj = lax.broadcasted_iota(jnp.int32, (BLOCK, 2 * BLOCK), 1)
    dist = BLOCK + qi - kj
    first_key = jnp.where(n > 0, 0, BLOCK)
    valid = (dist >= 0) & (dist < BLOCK) & (kj >= first_key)
    return dist.astype(F32), valid


GROUP = 4


def _kv_dup(prev_ref, cur_ref, hk, scale=1.0):
    tile = hk // 2
    kt = jnp.concatenate([prev_ref[:, tile * LANE:(tile + 1) * LANE], cur_ref[:, tile * LANE:(tile + 1) * LANE]], axis=0)
    return (_dup_half(kt.astype(F32), hk % 2) * scale).astype(BF16)


def _fill_bias(bias_ref, sm_ref, n):
    distf, valid = _band_geometry(n)
    for head in range(N_Q_HEADS):
        bias_ref[head] = jnp.where(valid, -sm_ref[1, head] * distf, MASKED)


def _head_scores(q2s, half, kdup, bias):
    qm = jnp.where(_half_mask(q2s.shape, half), q2s, jnp.zeros_like(q2s))
    return qm, lax.dot_general(qm, kdup, NT_DIMS, preferred_element_type=F32) + bias


def _attn_specs(nb, clamp_last):
    def blk(n):
        return jnp.minimum(n, nb - 1) if clamp_last else n

    q_spec = pl.BlockSpec((BLOCK, D_MODEL), lambda n: (blk(n), 0))
    k_prev = pl.BlockSpec((BLOCK, D_KV), lambda n: (jnp.maximum(blk(n) - 1, 0), D_MODEL // D_KV))
    k_cur = pl.BlockSpec((BLOCK, D_KV), lambda n: (blk(n), D_MODEL // D_KV))
    v_prev = pl.BlockSpec((BLOCK, D_KV), lambda n: (jnp.maximum(blk(n) - 1, 0), D_MODEL // D_KV + 1))
    v_cur = pl.BlockSpec((BLOCK, D_KV), lambda n: (blk(n), D_MODEL // D_KV + 1))
    return q_spec, k_prev, k_cur, v_prev, v_cur


def _attn_fwd(sm, qkv, ag_ml, name, ride=None):
    t = qkv.shape[0]
    nb = t // BLOCK

    host = _Hosted(ride, 7, 3, 1)

    def body(*refs):
        (sm_ref, q_ref, kp_ref, kc_ref, vp_ref, vc_ref, ag_ref, y_ref, z_ref, lse_ref, bias_ref), start, finish = host.split(refs)
        n = pl.program_id(0)
        start(n == 0)
        start.middle(n == (3 * nb) // 4)

        @pl.when(n <= 1)
        def _():
            _fill_bias(bias_ref, sm_ref, n)

        lane = lax.broadcasted_iota(jnp.int32, (BLOCK, LANE), 1)
        low = lane < HEAD_DIM
        lse = jnp.zeros((BLOCK, LANE), F32)
        for hk in range(N_Q_HEADS // GROUP):
            kdup = _kv_dup(kp_ref, kc_ref, hk)
            vdup = _kv_dup(vp_ref, vc_ref, hk)
            for k in (0, 1):
                c = slice((2 * hk + k) * LANE, (2 * hk + k + 1) * LANE)
                q2s = q_ref[:, c] * ATTN_SCALE
                outs = []
                for half in (0, 1):
                    head = GROUP * hk + 2 * k + half
                    _, s = _head_scores(q2s, half, kdup, bias_ref[head])
                    sink = sm_ref[0, head]
                    m = jnp.maximum(jnp.max(s, axis=1, keepdims=True), sink)
                    e = jnp.exp(s - m)
                    l = jnp.sum(e, axis=1, keepdims=True) + jnp.exp(sink - m)
                    outs.append(jnp.dot((e * (1.0 / l)).astype(BF16), vdup, preferred_element_type=F32))
                    lse = jnp.where(lane == head, m + jnp.log(l), lse)
                yt = jnp.where(low, outs[0], outs[1])
                y_ref[:, c] = yt
                ag = ag_ref[:, c].astype(F32)
                z_ref[:, c] = (yt * ag * _sigmoid(ag)).astype(BF16)
        lse_ref[...] = lse
        finish(n == nb - 1)

    q_spec, k_prev, k_cur, v_prev, v_cur = _attn_specs(nb, False)
    wide = pl.BlockSpec((BLOCK, D_MODEL), lambda n: (n, 0))
    outs = pl.pallas_call(
        body, name=name, grid=(nb,),
        in_specs=[pl.BlockSpec(memory_space=pltpu.SMEM), q_spec, k_prev, k_cur, v_prev, v_cur, wide] + host.in_specs,
        out_specs=[wide, wide, pl.BlockSpec((BLOCK, LANE), lambda n: (n, 0))] + host.out_specs,
        out_shape=[jax.ShapeDtypeStruct((t, D_MODEL), F32), jax.ShapeDtypeStruct((t, D_MODEL), BF16),
                   jax.ShapeDtypeStruct((t, LANE), F32)] + host.out_shapes,
        scratch_shapes=[pltpu.VMEM((N_Q_HEADS, BLOCK, 2 * BLOCK), F32)] + host.scratch_shapes,
        compiler_params=_params("arbitrary"))(sm, qkv, qkv, qkv, qkv, qkv, ag_ml, *host.arrays)
    res, landed = host.results(outs, 3)
    return (*res, landed) if ride else tuple(res)


def _attn_bwd(sm, qkv, ag_ml, y, lse, dz, name, ride=None):
    t = qkv.shape[0]
    nb = t // BLOCK
    host = _Hosted(ride, 10, 4, 3)

    def body(*refs):
        host_refs, start, finish = host.split(refs)
        (sm_ref, q_ref, kp_ref, kc_ref, vp_ref, vc_ref, ag_ref, y_ref, lse_ref, dz_ref,
         dq_ref, dkv_ref, dag_ref, ds_ref, ck_ref, cv_ref, bias_ref) = host_refs
        n = pl.program_id(0)
        start(n == 0)
        start.middle(n == (3 * nb) // 4)

        @pl.when(n == 0)
        def _():
            ck_ref[...] = jnp.zeros_like(ck_ref)
            cv_ref[...] = jnp.zeros_like(cv_ref)
            ds_ref[...] = jnp.zeros_like(ds_ref)

        @pl.when(n <= 1)
        def _():
            _fill_bias(bias_ref, sm_ref, n)

        @pl.when(n < nb)
        def _():
            lane8 = lax.broadcasted_iota(jnp.int32, (8, LANE), 1)
            row8 = lax.broadcasted_iota(jnp.int32, (8, LANE), 0)
            dsink = jnp.zeros((8, LANE), F32)
            dk_heads, dv_heads = [], []
            lse_tile = lse_ref[...]
            for hk in range(N_Q_HEADS // GROUP):
                kdup = _kv_dup(kp_ref, kc_ref, hk)
                ks = _kv_dup(kp_ref, kc_ref, hk, ATTN_SCALE)
                vdup = _kv_dup(vp_ref, vc_ref, hk)
                qms, dyhs, y_rows = [], [], []
                for k in (0, 1):
                    c = slice((2 * hk + k) * LANE, (2 * hk + k + 1) * LANE)
                    ag = ag_ref[:, c].astype(F32)
                    sg = _sigmoid(ag)
                    dzt = dz_ref[:, c].astype(F32)
                    yt = y_ref[:, c]
                    dag_ref[:, c] = (dzt * yt * (sg * (1.0 + ag * (1.0 - sg)))).astype(BF16)
                    dyt = dzt * (ag * sg)
                    q2s = q_ref[:, c] * ATTN_SCALE
                    for half in (0, 1):
                        hm = _half_mask(q2s.shape, half)
                        qms.append(jnp.where(hm, q2s, jnp.zeros_like(q2s)))
                        dyhs.append(jnp.where(hm, dyt, 0.0))
                        y_rows.append(yt)
                qm4 = jnp.concatenate(qms, axis=0)
                dy4 = jnp.concatenate(dyhs, axis=0)
                dy4_16 = dy4.astype(BF16)
                s4 = lax.dot_general(qm4, kdup, NT_DIMS, preferred_element_type=F32)
                dp4 = lax.dot_general(dy4_16, vdup, NT_DIMS, preferred_element_type=F32)
                probs16, ds16 = [], []
                for r in range(GROUP):
                    head = GROUP * hk + r
                    rows = slice(r * BLOCK, (r + 1) * BLOCK)
                    lh = lse_tile[:, head:head + 1]
                    probs = jnp.exp(s4[rows] + bias_ref[head] - lh)
                    psink = jnp.exp(sm_ref[0, head] - lh)
                    delta = jnp.sum(dyhs[r] * y_rows[r], axis=1, keepdims=True)
                    ds16.append((probs * (dp4[rows] - delta)).astype(BF16))
                    probs16.append(probs.astype(BF16))
                    dsink = dsink + jnp.where((row8 == 0) & (lane8 == head),
                                              -jnp.sum(psink * delta, axis=0, keepdims=True), 0.0)
                ds4 = jnp.concatenate(ds16, axis=0)
                p4 = jnp.concatenate(probs16, axis=0)
                dq4 = jnp.dot(ds4, ks, preferred_element_type=F32)
                low = _half_mask((BLOCK, LANE), 0)
                for k in (0, 1):
                    c = slice((2 * hk + k) * LANE, (2 * hk + k + 1) * LANE)
                    dq_ref[:, c] = jnp.where(low, dq4[2 * k * BLOCK:(2 * k + 1) * BLOCK],
                                             dq4[(2 * k + 1) * BLOCK:(2 * k + 2) * BLOCK]).astype(BF16)
                dk_acc = lax.dot_general(ds4, qm4, TN_DIMS, preferred_element_type=F32)
                dv_acc = lax.dot_general(p4, dy4_16, TN_DIMS, preferred_element_type=F32)
                dk_heads.append(dk_acc + pltpu.roll(dk_acc, HEAD_DIM, 1))
                dv_heads.append(dv_acc + pltpu.roll(dv_acc, HEAD_DIM, 1))
            ds_ref[...] += dsink
            low = _half_mask((2 * BLOCK, LANE), 0)
            for tile in range(2):
                cols = slice(tile * LANE, (tile + 1) * LANE)
                dkt = jnp.where(low, dk_heads[2 * tile], dk_heads[2 * tile + 1])
                dvt = jnp.where(low, dv_heads[2 * tile], dv_heads[2 * tile + 1])
                dkv_ref[:, cols] = (ck_ref[:, cols] + dkt[0:BLOCK, :]).astype(BF16)
                dkv_ref[:, D_KV + tile * LANE:D_KV + (tile + 1) * LANE] = (cv_ref[:, cols] + dvt[0:BLOCK, :]).astype(BF16)
                ck_ref[:, cols] = dkt[BLOCK:2 * BLOCK, :]
                cv_ref[:, cols] = dvt[BLOCK:2 * BLOCK, :]

        @pl.when(n == nb)
        def _():
            dkv_ref[:, 0:D_KV] = ck_ref[...].astype(BF16)
            dkv_ref[:, D_KV:2 * D_KV] = cv_ref[...].astype(BF16)

        finish(n == nb)

    q_spec, k_prev, k_cur, v_prev, v_cur = _attn_specs(nb, True)
    wide = pl.BlockSpec((BLOCK, D_MODEL), lambda n: (jnp.minimum(n, nb - 1), 0))
    outs = pl.pallas_call(
        body, name=name, grid=(nb + 1,),
        in_specs=[pl.BlockSpec(memory_space=pltpu.SMEM), q_spec, k_prev, k_cur, v_prev, v_cur, wide, wide,
                  pl.BlockSpec((BLOCK, LANE), lambda n: (jnp.minimum(n, nb - 1), 0)), wide] + host.in_specs,
        out_specs=[wide, pl.BlockSpec((BLOCK, 2 * D_KV), lambda n: (jnp.maximum(n - 1, 0), 0)), wide,
                   pl.BlockSpec((8, LANE), lambda n: (0, 0))] + host.out_specs,
        out_shape=[jax.ShapeDtypeStruct((t, D_MODEL), BF16), jax.ShapeDtypeStruct((t, 2 * D_KV), BF16),
                   jax.ShapeDtypeStruct((t, D_MODEL), BF16), jax.ShapeDtypeStruct((8, LANE), F32)] + host.out_shapes,
        scratch_shapes=[pltpu.VMEM((BLOCK, D_KV), F32), pltpu.VMEM((BLOCK, D_KV), F32),
                        pltpu.VMEM((N_Q_HEADS, BLOCK, 2 * BLOCK), F32)] + host.scratch_shapes,
        compiler_params=_params("arbitrary"))(sm, qkv, qkv, qkv, qkv, qkv, ag_ml, y, lse, dz, *host.arrays)
    res, landed = host.results(outs, 4)
    return (*res, landed) if ride else tuple(res)


def _local_grads(x, target, p, project, late_weights=None, reduce_out=None, reduce_in=None):
    h, rx_rg, qkv, ag_ml, wt = project(x, p["pre_g"])
    if late_weights is None:
        y_attn, z_attn, lse = _attn_fwd(p["sm"], qkv, ag_ml, "attn_fwd")
    else:
        y_attn, z_attn, lse, landed = _attn_fwd(p["sm"], qkv, ag_ml, "attn_fwd", ride=late_weights[0])
        p = {**p, **late_weights[1](landed)}
    lru = (p["cw"], p["cb"], p["wbd_a"], p["b_a"], p["wbd_x"], p["b_x"], p["lam"])
    y_rnn, z_rnn = _rnn_fwd(rx_rg, *lru, "rnn_fwd")
    br_rnn, br_attn, merged, dout, dy, st_post = _branches_fwd(
        z_rnn, z_attn, ag_ml, p["b_gate"], p["w_rnn"], p["w_attn"], p["w_out"], x, target, p["post_g"], "branches_fwd")

    dbr_rnn, dbr_attn, d_ml, dz_rnn, dz_attn, st_merge = _branches_bwd(
        dout, br_rnn, br_attn, ag_ml, p["b_gate"], p["w_rnn"], p["w_attn"], p["w_out"], "branches_bwd")
    out_prods = [(z_rnn, dbr_rnn), (z_attn, dbr_attn), (merged, dout)]
    gw_rnn = gw_attn = gw_out = red_out = red_in = None
    if reduce_out is None:
        gw_rnn, gw_attn, gw_out = (_mm_tn(a, b, nm) for (a, b), nm in zip(out_prods, ("gw_rnn", "gw_attn", "gw_out")))
    else:
        out_pairs = _mm_tn_pairs(out_prods, "gw_outs")
    d_rx, d_rg, st_rnn, g_rg_a, g_rg_x = _rnn_bwd(rx_rg, y_rnn, dz_rnn, *lru, "rnn_bwd")
    if reduce_out is None:
        dq, dkv, d_ag, st_sink = _attn_bwd(p["sm"], qkv, ag_ml, y_attn, lse, dz_attn, "attn_bwd")
    else:
        dq, dkv, d_ag, st_sink, red_out = _attn_bwd(p["sm"], qkv, ag_ml, y_attn, lse, dz_attn, "attn_bwd",
                                                    ride=reduce_out(out_pairs, g_rg_a, g_rg_x))

    segs = [d_rx, d_rg, dq, dkv, d_ag, d_ml]
    if reduce_in is None:
        gwt = _mm_tn_seg(segs, h, "gw_in")
        grad_x, st_pre = _input_grad(segs, wt, x, p["pre_g"], dy, "input_grad")
    else:
        gwt = None
        grad_x, st_pre, red_in = _input_grad(segs, wt, x, p["pre_g"], dy, "input_grad",
                                             ride=reduce_in(*_mm_tn_seg_pair(segs, h, "gw_in")))
    return dict(grad_x=grad_x, gwt=gwt, gw_rnn=gw_rnn, gw_attn=gw_attn, gw_out=gw_out,
                st_post=st_post, st_merge=st_merge, st_rnn=st_rnn, st_sink=st_sink, st_pre=st_pre,
                g_rg_a=g_rg_a, g_rg_x=g_rg_x, red_out=red_out, red_in=red_in)


def _place():
    x, y, c = lax.axis_index("x"), lax.axis_index("y"), lax.axis_index("c")
    return x, y, c


def _gather_ride(shards):
    n = len(shards)

    def copies(ins, outs, sems):
        send_sems, recv_sems, local_sems = sems
        x, y, c = _place()
        me, sibling = (x, y, c), (x, y, 1 - c)
        chips = [(1 - x, y), (x, 1 - y), (1 - x, 1 - y)]

        def slot(a, dev):
            return outs[a].at[4 * dev[0] + 2 * dev[1] + dev[2]]

        def copy(a, k, block, to, src=None):
            return pltpu.make_async_remote_copy(
                src_ref=slot(a, block) if src is None else src, dst_ref=slot(a, block),
                send_sem=send_sems.at[a, k], recv_sem=recv_sems.at[a, k], device_id=to, device_id_type=MESH)

        mine = [pltpu.make_async_copy(ins[a], slot(a, me), local_sems.at[a]) for a in range(n)]
        first = []
        for a in range(n):
            first.append(copy(a, 0, me, sibling, src=ins[a]))
            first += [copy(a, 1 + j, me, (*chip, c), src=ins[a]) for j, chip in enumerate(chips)]
        return me, sibling, chips, c, copy, mine, first

    def start(ins, outs, sems):
        *_, mine, first = copies(ins, outs, sems)
        for cp in mine + first:
            cp.start()

    def middle(ins, outs, sems):
        me, sibling, chips, c, copy, _, _ = copies(ins, outs, sems)
        for j, chip in enumerate(chips):
            for a in range(n):
                copy(a, 1 + j, (*chip, c), me).wait_recv()
                copy(a, 4 + j, (*chip, c), sibling).start()

    def finish(ins, outs, sems):
        me, sibling, chips, c, copy, mine, first = copies(ins, outs, sems)
        passed = [copy(a, 4 + j, (*chip, c), sibling) for j, chip in enumerate(chips) for a in range(n)]
        for a in range(n):
            copy(a, 0, sibling, me).wait_recv()
            for j, chip in enumerate(chips):
                copy(a, 4 + j, (*chip, 1 - c), me).wait_recv()
        for cp in first + passed:
            cp.wait_send()
        for cp in mine:
            cp.wait()

    return _Ride(
        shards, [jax.ShapeDtypeStruct((N_DEV, *s.shape), s.dtype) for s in shards],
        [pltpu.SemaphoreType.DMA((n, 7)), pltpu.SemaphoreType.DMA((n, 7)), pltpu.SemaphoreType.DMA((n,))],
        start, finish, middle)


def _chips_ride(scatter, whole):
    ns, nw = len(scatter), len(whole)
    n = ns + nw

    def copies(ins, outs, sems):
        send_sems, recv_sems, local_sems = sems
        x, y, c = _place()
        own = 2 * x + y
        chips = [(1 - x, y), (x, 1 - y), (1 - x, 1 - y)]

        def src(a, chip_idx):
            return ins[a].at[chip_idx] if a < ns else ins[a]

        local = [pltpu.make_async_copy(src(a, own), outs[a].at[own], local_sems.at[a]) for a in range(n)]
        sent = [pltpu.make_async_remote_copy(
            src_ref=src(a, 2 * chip[0] + chip[1]), dst_ref=outs[a].at[own],
            send_sem=send_sems.at[a, j], recv_sem=recv_sems.at[a, own], device_id=(*chip, c), device_id_type=MESH)
            for a in range(n) for j, chip in enumerate(chips)]
        return chips, c, local, sent

    def start(ins, outs, sems):
        _, _, local, sent = copies(ins, outs, sems)
        for cp in local + sent:
            cp.start()

    def finish(ins, outs, sems):
        send_sems, recv_sems, _ = sems
        chips, c, local, sent = copies(ins, outs, sems)
        for a in range(n):
            for chip in chips:
                k = 2 * chip[0] + chip[1]
                pltpu.make_async_remote_copy(
                    src_ref=outs[a].at[k], dst_ref=outs[a].at[k], send_sem=send_sems.at[a, 0],
                    recv_sem=recv_sems.at[a, k], device_id=(*chip, c), device_id_type=MESH).wait_recv()
        for cp in sent:
            cp.wait_send()
        for cp in local:
            cp.wait()

    return _Ride(
        list(scatter) + list(whole),
        [jax.ShapeDtypeStruct(s.shape, s.dtype) for s in scatter]
        + [jax.ShapeDtypeStruct((N_CHIP, *s.shape), s.dtype) for s in whole],
        [pltpu.SemaphoreType.DMA((n, 3)), pltpu.SemaphoreType.DMA((n, N_CHIP)), pltpu.SemaphoreType.DMA((n,))],
        start, finish)


def _chips_rest_ride(pair, red):
    table = _side_pieces()
    nc = len(CHUNK_ORDER)

    def each(ins, outs, sems, sending, landing):
        send_sems, recv_sems, _ = sems
        pair_ref, red_ref = ins[0], outs[0]
        x, y, c = _place()
        own = 2 * x + y
        for core in (0, 1):
            mine = table[core][0]
            rest = [s for s in range(EARLY_STEPS, nc) if mine[s] is not None]

            @pl.when(c == core)
            def _(mine=mine, rest=rest, core=core):
                for s in rest:
                    _, rows, at = mine[s]
                    k, cp = _to_chip(pair_ref.at[at // SHARD_IN, pl.ds(at % SHARD_IN, rows)], red_ref, mine[s], s,
                                     send_sems, recv_sems, own, core)
                    pl.when(own != k)(lambda cp=cp: sending(cp))
                if landing is not None:
                    for k in range(N_CHIP):
                        @pl.when(own == k)
                        def _(k=k):
                            for s in rest:
                                _, rows, at = mine[s]
                                if at // SHARD_IN == k:
                                    for chip in range(N_CHIP):
                                        if chip != k:
                                            landing(_from_chip(pair_ref.at[k, pl.ds(at % SHARD_IN, rows)], red_ref, mine[s], s,
                                                               send_sems, recv_sems, chip, (x, y, c)))

    def local(ins, outs, sems):
        x, y, _ = _place()
        return pltpu.make_async_copy(ins[0].at[2 * x + y], outs[0].at[2 * x + y], sems[2])

    def start(ins, outs, sems):
        local(ins, outs, sems).start()
        each(ins, outs, sems, lambda cp: cp.start(), None)

    def finish(ins, outs, sems):
        each(ins, outs, sems, lambda cp: cp.wait_send(), lambda cp: cp.wait_recv())
        local(ins, outs, sems).wait()

    return _Ride([pair, red], [jax.ShapeDtypeStruct(red.shape, red.dtype)],
                 [pltpu.SemaphoreType.DMA((nc,)), pltpu.SemaphoreType.DMA((N_CHIP, nc)), pltpu.SemaphoreType.DMA],
                 start, finish, aliases={1: 0})


def _allreduce_small(pack, name):
    shape = pack.shape

    def body(x_ref, o_ref, sib_ref, chip_ref, send_sems, recv_sems):
        x, y, c = _place()
        own = 2 * x + y
        chips = [(1 - x, y), (x, 1 - y), (1 - x, 1 - y)]
        to_sibling = pltpu.make_async_remote_copy(
            src_ref=x_ref, dst_ref=sib_ref, send_sem=send_sems.at[0], recv_sem=recv_sems.at[0],
            device_id=(x, y, 1 - c), device_id_type=MESH)
        to_sibling.start()
        to_sibling.wait()
        chip_ref[own] = x_ref[...] + sib_ref[...]
        sent = [pltpu.make_async_remote_copy(
            src_ref=chip_ref.at[own], dst_ref=chip_ref.at[own], send_sem=send_sems.at[1 + j],
            recv_sem=recv_sems.at[1 + own], device_id=(*chip, c), device_id_type=MESH) for j, chip in enumerate(chips)]
        for cp in sent:
            cp.start()
        for chip in chips:
            k = 2 * chip[0] + chip[1]
            pltpu.make_async_remote_copy(
                src_ref=chip_ref.at[k], dst_ref=chip_ref.at[k], send_sem=send_sems.at[1],
                recv_sem=recv_sems.at[1 + k], device_id=(*chip, c), device_id_type=MESH).wait_recv()
        for cp in sent:
            cp.wait_send()
        o_ref[...] = (chip_ref[0] + chip_ref[1]) + (chip_ref[2] + chip_ref[3])

    return pl.pallas_call(
        body, name=name, out_shape=jax.ShapeDtypeStruct(shape, F32),
        in_specs=[pl.BlockSpec(memory_space=pltpu.VMEM)], out_specs=pl.BlockSpec(memory_space=pltpu.VMEM),
        scratch_shapes=[pltpu.VMEM(shape, F32), pltpu.VMEM((N_CHIP, *shape), F32),
                        pltpu.SemaphoreType.DMA((4,)), pltpu.SemaphoreType.DMA((1 + N_CHIP,))],
    )(pack)


def _adamw(g, w, m, v):
    m = ADAM_B1 * m + (1.0 - ADAM_B1) * g
    v = ADAM_B2 * v + (1.0 - ADAM_B2) * (g * g)
    m_hat = m / (1.0 - ADAM_B1 ** ADAM_STEP)
    v_hat = v / (1.0 - ADAM_B2 ** ADAM_STEP)
    delta = -ADAM_LR * (m_hat / (jnp.sqrt(v_hat) + ADAM_EPS) + ADAM_WD * w)
    return delta, m, v


def _adam_parts(parts, w, m, v, name, tr=None):
    npart, r, c = parts.shape
    tr = r if tr is None else min(tr, r)

    def body(p_ref, w_ref, m_ref, v_ref, g_ref, d_ref, nm_ref, nv_ref):
        g = p_ref[0].astype(F32)
        for k in range(1, npart):
            g = g + p_ref[k].astype(F32)
        g_ref[...] = g
        d_ref[...], nm_ref[...], nv_ref[...] = _adamw(g, w_ref[...], m_ref[...], v_ref[...])

    tile = pl.BlockSpec((tr, c), lambda i: (i, 0))
    return pl.pallas_call(
        body, name=name, grid=(r // tr,),
        in_specs=[pl.BlockSpec((npart, tr, c), lambda i: (0, i, 0)), tile, tile, tile],
        out_specs=[tile] * 4, out_shape=[jax.ShapeDtypeStruct((r, c), F32)] * 4,
        compiler_params=_params("parallel"))(parts, w, m, v)


def _block_diag(w):
    w4 = w.reshape(N_GROUPS, 4, RNN_BLOCK_W, RNN_BLOCK_W)
    eye = jnp.eye(4, dtype=w.dtype)
    return jnp.einsum("gbij,bc->gbicj", w4, eye).reshape(N_GROUPS, GROUP_W, GROUP_W).astype(BF16)


SMALL_ROWS = 16
ROW_PRE_G, ROW_BGATE, ROW_CONV_B, ROW_B_A, ROW_B_X, ROW_LAM, ROW_POST_G, ROW_LOSS, ROW_SINKS, ROW_CONV_W = 0, 1, 3, 4, 5, 6, 7, 8, 9, 10


def _pack_stats(st_pre, st_merge, st_rnn, st_post, st_sink, name):
    d = D_MODEL

    def body(pre_ref, mg_ref, rnn_ref, post_ref, sink_ref, o_ref):
        rnn = rnn_ref[...]
        sinks = jnp.concatenate([sink_ref[0:1, :], jnp.zeros((1, d - LANE), F32)], axis=1)
        o_ref[0:8, :] = _rows8([pre_ref[0:1, :], mg_ref[0:1, :], mg_ref[1:2, :], rnn[0:1], rnn[1:2], rnn[2:3], rnn[3:4],
                                post_ref[0:1, :]], d)
        o_ref[8:16, :] = _rows8([post_ref[1:2, :], sinks, rnn[4:5], rnn[5:6], rnn[6:7], rnn[7:8]], d)

    return pl.pallas_call(body, name=name, out_shape=jax.ShapeDtypeStruct((SMALL_ROWS, d), F32))(
        st_pre, st_merge, st_rnn, st_post, st_sink)


def _adam_small(total, w, m, v, name):
    d = D_MODEL
    n_in = len(w)

    def pack(refs):
        pre, bg, cbias, ba, bx, lam, post, sinks = [r[...] for r in refs]
        top = _rows8([pre, bg[:, 0:d], bg[:, d:2 * d], cbias, ba, bx, lam, post], d)
        return jnp.concatenate([top, _rows8([jnp.zeros((1, d), F32), sinks], d)], axis=0)

    def body(*refs):
        p_ref = refs[0]
        w_refs, m_refs, v_refs = (refs[1 + k * n_in:1 + (k + 1) * n_in] for k in range(3))
        outs = refs[1 + 3 * n_in:]
        g = p_ref[...]
        res = (g,) + _adamw(g, pack(w_refs), pack(m_refs), pack(v_refs))
        for k in range(4):
            outs[k][...] = res[k]
            outs[4 + k][...] = jnp.concatenate([res[k][ROW_BGATE:ROW_BGATE + 1], res[k][ROW_BGATE + 1:ROW_BGATE + 2]], axis=1)

    return pl.pallas_call(
        body, name=name,
        out_shape=[jax.ShapeDtypeStruct((SMALL_ROWS, d), F32)] * 4 + [jax.ShapeDtypeStruct((1, 2 * d), F32)] * 4,
    )(total, *w, *m, *v)


def kernel(x, pre_norm_g, w_in, b_gate, conv_w, conv_b, w_rg_a, b_rg_a, w_rg_x, b_rg_x, lru_lambda, attn_sinks, w_rnn_out, w_attn_out, w_out, post_norm_g, loss_target, m_pre_norm_g, m_w_in, m_b_gate, m_conv_w, m_conv_b, m_w_rg_a, m_b_rg_a, m_w_rg_x, m_b_rg_x, m_lru_lambda, m_attn_sinks, m_w_rnn_out, m_w_attn_out, m_w_out, m_post_norm_g, v_pre_norm_g, v_w_in, v_b_gate, v_conv_w, v_conv_b, v_w_rg_a, v_b_rg_a, v_w_rg_x, v_b_rg_x, v_lru_lambda, v_attn_sinks, v_w_rnn_out, v_w_attn_out, v_w_out, v_post_norm_g):
    cx, cy, cc = _place()
    dev = 4 * cx + 2 * cy + cc

    w_in_t, m_in_t, v_in_t = (jnp.transpose(a[0]) for a in (w_in, m_w_in, v_w_in))
    wt_shard = w_in_t.astype(BF16)

    def project(xs, pre_g):
        h, rx_rg, qkv, ag_ml, wt_all = _gather_project(xs, pre_g, wt_shard, "gather_project")
        return h, rx_rg, qkv, ag_ml, wt_all.reshape(D_IN, D_MODEL)

    def late_unpack(landed):
        w_rnn_all, w_attn_all, w_out_all, cw_all = landed
        return dict(w_rnn=w_rnn_all.reshape(D_RNN, D_MODEL), w_attn=w_attn_all.reshape(D_MODEL, D_MODEL),
                    w_out=w_out_all.reshape(D_MODEL, D_MODEL), cw=jnp.transpose(cw_all, (1, 0, 2)).reshape(4, D_RNN))

    late_weights = (_gather_ride([w_rnn_out[0].astype(BF16), w_attn_out[0].astype(BF16), w_out[0].astype(BF16), conv_w[0]]),
                    late_unpack)

    heads = jnp.arange(1, N_Q_HEADS + 1, dtype=F32)
    slopes = jnp.exp2(-ALIBI_MAX_BIAS * heads / N_Q_HEADS)
    b_a = b_rg_a.reshape(1, D_RNN)
    b_x = b_rg_x.reshape(1, D_RNN)
    p = dict(
        pre_g=pre_norm_g, post_g=post_norm_g, b_gate=b_gate, cb=conv_b,
        wbd_a=_block_diag(w_rg_a[0]), b_a=b_a, wbd_x=_block_diag(w_rg_x[0]), b_x=b_x, lam=lru_lambda,
        sm=jnp.pad(attn_sinks, ((0, 1), (0, 0))) + jnp.pad(slopes[None, :], ((1, 0), (0, 0))))

    flat = (RNN_BLOCKS * RNN_BLOCK_W, RNN_BLOCK_W)

    def reduce_out(out_pairs, g_rg_a, g_rg_x):
        return _join_rides(_chips_ride(out_pairs, []), _gather_ride([g_rg_a.reshape(flat), g_rg_x.reshape(flat)]))

    reduce_in = _chips_rest_ride

    g = _local_grads(x[0], loss_target[0], p, project, late_weights, reduce_out, reduce_in)
    small = _allreduce_small(
        _pack_stats(g["st_pre"], g["st_merge"], g["st_rnn"], g["st_post"], g["st_sink"], "pack_stats"), "allreduce_small")

    out = {}
    red = g["red_out"]
    out["w_in"] = [jnp.transpose(o) for o in _adam_parts(g["red_in"][0], w_in_t, m_in_t, v_in_t, "adam_w_in", tr=SHARD_IN // 2)]
    out["w_rnn_out"] = _adam_parts(red[0], w_rnn_out[0], m_w_rnn_out[0], v_w_rnn_out[0], "adam_w_rnn_out")
    out["w_attn_out"] = _adam_parts(red[1], w_attn_out[0], m_w_attn_out[0], v_w_attn_out[0], "adam_w_attn_out")
    out["w_out"] = _adam_parts(red[2], w_out[0], m_w_out[0], v_w_out[0], "adam_w_out")
    out["w_rg_a"] = _adam_parts(red[3], w_rg_a.reshape(flat), m_w_rg_a.reshape(flat), v_w_rg_a.reshape(flat), "adam_w_rg_a", tr=256)
    out["w_rg_x"] = _adam_parts(red[4], w_rg_x.reshape(flat), m_w_rg_x.reshape(flat), v_w_rg_x.reshape(flat), "adam_w_rg_x", tr=256)

    def rows(pre, bg, cbias, ba, bx, lam, post, sinks):
        return (pre, bg, cbias, ba.reshape(1, D_RNN), bx.reshape(1, D_RNN), lam, post,
                jnp.pad(sinks, ((0, 0), (0, D_MODEL - N_Q_HEADS))))

    small_out = _adam_small(
        small,
        rows(pre_norm_g, b_gate, conv_b, b_rg_a, b_rg_x, lru_lambda, post_norm_g, attn_sinks),
        rows(m_pre_norm_g, m_b_gate, m_conv_b, m_b_rg_a, m_b_rg_x, m_lru_lambda, m_post_norm_g, m_attn_sinks),
        rows(v_pre_norm_g, v_b_gate, v_conv_b, v_b_rg_a, v_b_rg_x, v_lru_lambda, v_post_norm_g, v_attn_sinks),
        "adam_small")
    packed, bgate_out = small_out[:4], small_out[4:]
    g_cw = lax.dynamic_slice(packed[0][ROW_CONV_W:ROW_CONV_W + 4], (0, dev * SHARD_OUT), (4, SHARD_OUT))
    out["conv_w"] = _adam_parts(g_cw[None], conv_w[0], m_conv_w[0], v_conv_w[0], "adam_conv_w")

    def unpack(kind, name):
        if name == "b_gate":
            return bgate_out[kind]
        row = dict(pre_norm_g=ROW_PRE_G, conv_b=ROW_CONV_B, b_rg_a=ROW_B_A, b_rg_x=ROW_B_X, lru_lambda=ROW_LAM,
                   post_norm_g=ROW_POST_G, attn_sinks=ROW_SINKS)[name]
        r = packed[kind][row:row + 1]
        if name == "attn_sinks":
            return r[:, 0:N_Q_HEADS]
        if name in ("b_rg_a", "b_rg_x"):
            return r.reshape(1, RNN_BLOCKS, RNN_BLOCK_W)
        return r

    shapes = dict(w_in=(1, D_MODEL, SHARD_IN), w_rnn_out=(1, SHARD_OUT, D_MODEL), w_attn_out=(1, SHARD_OUT, D_MODEL),
                  w_out=(1, SHARD_OUT, D_MODEL), w_rg_a=(1, RNN_BLOCKS, RNN_BLOCK_W, RNN_BLOCK_W),
                  w_rg_x=(1, RNN_BLOCKS, RNN_BLOCK_W, RNN_BLOCK_W), conv_w=(1, 4, SHARD_OUT))
    weights = ["pre_norm_g", "w_in", "b_gate", "conv_w", "conv_b", "w_rg_a", "b_rg_a", "w_rg_x", "b_rg_x",
               "lru_lambda", "attn_sinks", "w_rnn_out", "w_attn_out", "w_out", "post_norm_g"]
    results = []
    for kind in range(4):
        for name in weights:
            if name in out:
                results.append(out[name][kind].reshape(shapes[name]))
            else:
                results.append(unpack(kind, name))
    loss = 0.5 / D_MODEL * jnp.sum(packed[0][ROW_LOSS])
    return (loss, g["grad_x"][None], *results)
```

```python
import functools

import jax
import jax.numpy as jnp
from jax import lax
from jax.experimental import pallas as pl
from jax.experimental.pallas import tpu as pltpu

F32 = jnp.float32
BF16 = jnp.bfloat16

D_MODEL = 1024
D_RNN = 1024
RNN_BLOCKS = 16
RNN_BLOCK_W = 64
LRU_C = 8.0
N_Q_HEADS = 16
HEAD_DIM = 64
D_KV = 256
BLOCK = 128
ALIBI_MAX_BIAS = 8.0
EPS = 1e-6
D_IN = 6656
N_DEV = 8
N_CHIP = 4
SHARD_IN = D_IN // N_DEV
SHARD_OUT = D_MODEL // N_DEV
ATTN_SCALE = HEAD_DIM ** -0.5
MASKED = -1e30

ADAM_LR = 0.001
ADAM_B1 = 0.9
ADAM_B2 = 0.999
ADAM_EPS = 1e-08
ADAM_WD = 0.01
ADAM_STEP = 10

VMEM_LIMIT_BYTES = 52 * 1024 * 1024
LANE = 128
GROUP_W = 256
N_GROUPS = D_RNN // GROUP_W
SEG_CHUNK = 512

NT_DIMS = (((1,), (1,)), ((), ()))
TN_DIMS = (((0,), (0,)), ((), ()))
MESH = pl.DeviceIdType.MESH
ANY = pl.BlockSpec(memory_space=pl.ANY)


def _params(*semantics):
    return pltpu.CompilerParams(dimension_semantics=semantics, vmem_limit_bytes=VMEM_LIMIT_BYTES)


def _sigmoid(x):
    return 0.5 * jnp.tanh(0.5 * x) + 0.5


def _log1p(e):
    u = 1.0 + e
    den = jnp.where(u == 1.0, 1.0, u - 1.0)
    return jnp.where(u == 1.0, e, jnp.log(u) * (e / den))


def _softplus(z):
    return jnp.maximum(z, 0.0) + _log1p(jnp.exp(-jnp.abs(z)))


def _rows8(rows, width):
    idx = lax.broadcasted_iota(jnp.int32, (8, width), 0)
    out = jnp.zeros((8, width), F32)
    for r, v in enumerate(rows):
        out = jnp.where(idx == r, v, out)
    return out


class _Ride:
    def __init__(self, arrays, out_shapes, scratch_shapes, start, finish, middle=None, aliases=None):
        self.arrays, self.out_shapes, self.scratch_shapes = list(arrays), list(out_shapes), list(scratch_shapes)
        self.start, self.finish, self.middle = start, finish, middle
        self.aliases = dict(aliases or {})


class _Hosted:
    def __init__(self, ride, n_in, n_out, n_scratch=0, aliasing=False):
        self.ride = ride
        assert aliasing or not (ride and ride.aliases), "this host does not alias"
        self.aliases = {n_in + i: n_out + o for i, o in ride.aliases.items()} if ride else {}
        self.sizes = (n_in, len(ride.arrays) if ride else 0, n_out, len(ride.out_shapes) if ride else 0, n_scratch)
        self.arrays = ride.arrays if ride else []
        self.in_specs = [ANY] * len(self.arrays)
        self.out_shapes = ride.out_shapes if ride else []
        self.out_specs = [ANY] * len(self.out_shapes)
        self.scratch_shapes = ride.scratch_shapes if ride else []

    def split(self, refs):
        n_in, r_in, n_out, r_out, n_scr = self.sizes
        cuts = [0, n_in, n_in + r_in, n_in + r_in + n_out, n_in + r_in + n_out + r_out, n_in + r_in + n_out + r_out + n_scr]
        host_in, ride_in, host_out, ride_out, host_scr = (refs[cuts[k]:cuts[k + 1]] for k in range(5))
        ride_scr = refs[cuts[5]:]

        def start(when):
            if self.ride is not None:
                pl.when(when)(lambda: self.ride.start(ride_in, ride_out, ride_scr))

        def finish(when):
            if self.ride is not None:
                pl.when(when)(lambda: self.ride.finish(ride_in, ride_out, ride_scr))

        def middle(when):
            if self.ride is not None and self.ride.middle is not None:
                pl.when(when)(lambda: self.ride.middle(ride_in, ride_out, ride_scr))

        start.middle = middle
        return tuple(host_in) + tuple(host_out) + tuple(host_scr), start, finish

    def results(self, outs, n_out):
        outs = list(outs) if isinstance(outs, (list, tuple)) else [outs]
        return outs[:n_out], outs[n_out:]


def _join_rides(a, b):
    na, nb_ = len(a.arrays), len(b.arrays)
    oa = len(a.out_shapes)
    sa = len(a.scratch_shapes)

    def both(fa, fb):
        def run(ins, outs, sems):
            if fa is not None:
                fa(ins[:na], outs[:oa], sems[:sa])
            if fb is not None:
                fb(ins[na:na + nb_], outs[oa:], sems[sa:])
        return run

    middle = both(a.middle, b.middle) if (a.middle or b.middle) else None
    return _Ride(a.arrays + b.arrays, a.out_shapes + b.out_shapes, a.scratch_shapes + b.scratch_shapes,
                 both(a.start, b.start), both(a.finish, b.finish), middle)


def _load_resident(w_hbm, w_vmem, sems, first):
    def piece(c):
        rows = pl.ds(c * SEG_CHUNK, SEG_CHUNK)
        return pltpu.make_async_copy(w_hbm.at[rows], w_vmem.at[rows], sems.at[c])

    @pl.when(first)
    def _():
        for c in range(w_vmem.shape[0] // SEG_CHUNK):
            piece(c).start()

    def ready(c):
        @pl.when(first)
        def _():
            piece(c).wait()

    return ready


CHIP_ROWS = 2 * SHARD_IN
PROJ_WIDTHS = (2 * D_RNN, D_MODEL + 2 * D_KV, 3 * D_MODEL)
PROJ_DTYPES = (F32, BF16, BF16)


def _chip_pieces():
    starts = [0, PROJ_WIDTHS[0], PROJ_WIDTHS[0] + PROJ_WIDTHS[1], D_IN]
    pieces = []
    for k in range(N_CHIP):
        lo, hi = k * CHIP_ROWS, (k + 1) * CHIP_ROWS
        cur = []
        for a in range(len(PROJ_WIDTHS)):
            s0, s1 = max(lo, starts[a]), min(hi, starts[a + 1])
            if s0 < s1:
                cur.append((a, s0 - starts[a], s1 - s0, s0 - lo))
        pieces.append(cur)
    return pieces


def _gather_project(x, g, wt_shard, name, tm=1024):
    t, k = x.shape
    tm = min(tm, t)
    nt = t // tm
    pieces = _chip_pieces()

    def body(x_ref, g_ref, shard_ref, h_out, rx_ref, qkv_ref, ag_ref, wt_all,
             w_c, stage, o32, o16, h_all, send_sems, recv_sems, local_sem, stage_sems, out_sems, h_sems):
        s, ti = pl.program_id(0), pl.program_id(1)
        px, py, pc = _place()
        me, sibling = (px, py, pc), (px, py, 1 - pc)
        chips = [(px, py), (1 - px, py), (px, 1 - py), (1 - px, 1 - py)]
        outs = (rx_ref, qkv_ref, ag_ref)

        def slot(dev):
            return wt_all.at[4 * dev[0] + 2 * dev[1] + dev[2]]

        def copy(kk, block, to, src=None):
            return pltpu.make_async_remote_copy(
                src_ref=slot(block) if src is None else src, dst_ref=slot(block),
                send_sem=send_sems.at[kk], recv_sem=recv_sems.at[kk], device_id=to, device_id_type=MESH)

        mine = pltpu.make_async_copy(shard_ref, slot(me), local_sem)
        first = [copy(0, me, sibling, src=shard_ref)] + [copy(1 + j, me, (*chips[1 + j], pc), src=shard_ref) for j in range(3)]
        passed = [copy(4 + j, (*chips[1 + j], pc), sibling) for j in range(3)]

        @pl.when((s == 0) & (ti == 0))
        def _():
            mine.start()
            for cp in first[:3]:
                cp.start()

        def pass_on(step):
            copy(step, (*chips[step], pc), me).wait_recv()
            passed[step - 1].start()

        @pl.when((s == 2) & (ti == nt - 1))
        def _():
            pass_on(3)

        for step in range(N_CHIP):
            @pl.when((s == step) & (ti == 0))
            def _(step=step):
                chip = chips[step]
                if step == 0:
                    mine.wait()
                    copy(0, sibling, me).wait_recv()
                else:
                    if step == 1:
                        pass_on(1)
                        first[3].start()
                        pass_on(2)
                    copy(3 + step, (*chip, 1 - pc), me).wait_recv()
                loads = [pltpu.make_async_copy(slot((*chip, core)), stage.at[pl.ds(core * SHARD_IN, SHARD_IN)], stage_sems.at[core])
                         for core in (0, 1)]
                for cp in loads:
                    cp.start()
                for cp in loads:
                    cp.wait()
                w_c[...] = stage[...].T

        rows = pl.ds(pl.multiple_of(ti * tm, tm), tm)

        @pl.when(s == 0)
        def _():
            xv = x_ref[...]
            h_all[rows, :] = (xv * lax.rsqrt(jnp.mean(xv * xv, axis=-1, keepdims=True) + EPS) * g_ref[...]).astype(BF16)

        res = jnp.dot(h_all[rows, :], w_c[...], preferred_element_type=F32)

        n = s * nt + ti
        buf = lax.rem(n, 2)
        chip_idx = [2 * cx + cy for cx, cy in chips]

        def chip_at(step):
            return jnp.where(step == 0, chip_idx[0], jnp.where(step == 1, chip_idx[1], jnp.where(step == 2, chip_idx[2], chip_idx[3])))

        def h_write(b, tile):
            return pltpu.make_async_copy(h_all.at[pl.ds(tile * tm, tm)], h_out.at[pl.ds(tile * tm, tm)], h_sems.at[b])

        def writes(kchip, b, tile):
            cps = []
            for idx, (a, col, w, src) in enumerate(pieces[kchip]):
                staged = (o16 if PROJ_DTYPES[a] == BF16 else o32).at[b, :, pl.ds(src, w)]
                cps.append(pltpu.make_async_copy(staged, outs[a].at[pl.ds(tile * tm, tm), pl.ds(col, w)], out_sems.at[b, idx]))
            return cps

        o32[buf] = res
        o16[buf] = res.astype(BF16)
        kcur = chip_at(s)

        @pl.when(n > 0)
        def _():
            kprev = chip_at(lax.div(n - 1, nt))
            for kchip in range(N_CHIP):
                @pl.when(kprev == kchip)
                def _(kchip=kchip):
                    for cp in writes(kchip, 1 - buf, lax.rem(n - 1, nt)):
                        cp.wait()

            @pl.when(n <= nt)
            def _():
                h_write(1 - buf, n - 1).wait()

        @pl.when(s == 0)
        def _():
            h_write(buf, ti).start()

        for kchip in range(N_CHIP):
            @pl.when(kcur == kchip)
            def _(kchip=kchip):
                for cp in writes(kchip, buf, ti):
                    cp.start()

        @pl.when(n == N_CHIP * nt - 1)
        def _():
            for kchip in range(N_CHIP):
                @pl.when(kcur == kchip)
                def _(kchip=kchip):
                    for cp in writes(kchip, buf, ti):
                        cp.wait()
            for cp in first + passed:
                cp.wait_send()

    tile = pl.BlockSpec((tm, k), lambda s, ti: (jnp.where(s == 0, ti, nt - 1), 0))
    return pl.pallas_call(
        body, name=name, grid=(N_CHIP, nt),
        in_specs=[tile, pl.BlockSpec((1, k), lambda s, ti: (0, 0)), ANY],
        out_specs=[ANY, ANY, ANY, ANY, ANY],
        out_shape=[jax.ShapeDtypeStruct((t, k), BF16)]
        + [jax.ShapeDtypeStruct((t, w), dt) for w, dt in zip(PROJ_WIDTHS, PROJ_DTYPES)]
        + [jax.ShapeDtypeStruct((N_DEV, SHARD_IN, k), BF16)],
        scratch_shapes=[pltpu.VMEM((k, CHIP_ROWS), BF16), pltpu.VMEM((CHIP_ROWS, k), BF16),
                        pltpu.VMEM((2, tm, CHIP_ROWS), F32), pltpu.VMEM((2, tm, CHIP_ROWS), BF16), pltpu.VMEM((t, k), BF16),
                        pltpu.SemaphoreType.DMA((7,)), pltpu.SemaphoreType.DMA((7,)), pltpu.SemaphoreType.DMA,
                        pltpu.SemaphoreType.DMA((2,)), pltpu.SemaphoreType.DMA((2, 2)), pltpu.SemaphoreType.DMA((2,))],
        compiler_params=_params("arbitrary", "arbitrary"))(x, g, wt_shard)


def _mm_tn(a, b, name, tm=512, tk=4096):
    ktok, m = a.shape
    n = b.shape[1]
    tk = min(tk, ktok)

    def body(a_ref, b_ref, o_ref):
        @pl.when(pl.program_id(1) == 0)
        def _():
            o_ref[...] = jnp.zeros_like(o_ref)

        o_ref[...] += lax.dot_general(a_ref[...], b_ref[...], TN_DIMS, preferred_element_type=F32)

    return pl.pallas_call(
        body, name=name, grid=(m // tm, ktok // tk),
        in_specs=[pl.BlockSpec((tk, tm), lambda i, kk: (kk, i)), pl.BlockSpec((tk, n), lambda i, kk: (kk, 0))],
        out_specs=pl.BlockSpec((tm, n), lambda i, kk: (i, 0)),
        out_shape=jax.ShapeDtypeStruct((m, n), F32),
        compiler_params=_params("parallel", "arbitrary"))(a, b)


def _mm_tn_pairs(prods, name):
    n_prod = len(prods)
    ktok, m = prods[0][0].shape
    n = prods[0][1].shape[1]
    half, blk = m // 2, m // N_DEV
    ns = 2 * n_prod
    kept = N_CHIP * blk

    def side_pieces(s, side):
        i, hf = divmod(s, 2)
        first = hf * N_DEV // 2
        return [((d - first) * blk, i * kept + (d // 2) * blk) for d in range(first, first + N_DEV // 2) if d % 2 == side]

    def body(*refs):
        a_hbm, b_hbm, pair_refs = refs[:n_prod], refs[n_prod:2 * n_prod], refs[2 * n_prod:3 * n_prod]
        a_buf, b_buf, res_buf, recv_all, pair_all, a_sems, b_sems, send_sems, recv_sems, out_sems = refs[3 * n_prod:]
        step = pl.program_id(0)
        px, py, pc = _place()

        def fetch_a(s):
            return pltpu.make_async_copy(a_hbm[s // 2].at[:, pl.ds((s % 2) * half, half)], a_buf.at[s % 2], a_sems.at[s % 2])

        def fetch_b(i):
            return pltpu.make_async_copy(b_hbm[i], b_buf.at[i % 2], b_sems.at[i % 2])

        @pl.when(step == 0)
        def _():
            fetch_a(0).start()
            fetch_b(0).start()

        for s in range(ns):
            @pl.when(step == s)
            def _(s=s):
                if s + 1 < ns:
                    fetch_a(s + 1).start()
                if s % 2 == 0 and s // 2 + 1 < n_prod:
                    fetch_b(s // 2 + 1).start()
                fetch_a(s).wait()
                if s % 2 == 0:
                    fetch_b(s // 2).wait()

        res_buf[step % 2] = lax.dot_general(a_buf[step % 2], b_buf[(step // 2) % 2], TN_DIMS, preferred_element_type=F32)

        def crossing(s, j, piece):
            off, at = piece
            return pltpu.make_async_remote_copy(
                src_ref=res_buf.at[s % 2, pl.ds(off, blk)], dst_ref=recv_all.at[pl.ds(at, blk)],
                send_sem=send_sems.at[s, j], recv_sem=recv_sems.at[s, j], device_id=(px, py, 1 - pc), device_id_type=MESH)

        def write(s, core):
            at = side_pieces(s, core)[0][1]
            return pltpu.make_async_copy(pair_all.at[pl.ds(at, 2 * blk)], pair_refs[s // 2].at[pl.ds(at % kept, 2 * blk)], out_sems.at[s])

        for core in (0, 1):
            def settle(s, core=core):
                for j, piece in enumerate(side_pieces(s, 1 - core)):
                    crossing(s, j, piece).wait_send()
                for j, (off, at) in enumerate(side_pieces(s, core)):
                    crossing(s, j, (off, at)).wait_recv()
                    pair_all[at:at + blk] = (res_buf[s % 2, off:off + blk] + recv_all[at:at + blk]).astype(BF16)
                write(s, core).start()

            for s in range(ns):
                @pl.when((pc == core) & (step == s))
                def _(s=s, settle=settle, core=core):
                    for j, piece in enumerate(side_pieces(s, 1 - core)):
                        crossing(s, j, piece).start()
                    if s > 0:
                        settle(s - 1)
                    if s == ns - 1:
                        settle(s)
                        for t in range(ns):
                            write(t, core).wait()

    pairs = pl.pallas_call(
        body, name=name, grid=(ns,),
        in_specs=[ANY] * (2 * n_prod), out_specs=[ANY] * n_prod,
        out_shape=[jax.ShapeDtypeStruct((kept, n), BF16)] * n_prod,
        scratch_shapes=[pltpu.VMEM((2, ktok, half), prods[0][0].dtype), pltpu.VMEM((2, ktok, n), prods[0][1].dtype),
                        pltpu.VMEM((2, half, n), F32), pltpu.VMEM((n_prod * kept, n), F32), pltpu.VMEM((n_prod * kept, n), BF16),
                        pltpu.SemaphoreType.DMA((2,)), pltpu.SemaphoreType.DMA((2,)),
                        pltpu.SemaphoreType.DMA((ns, 2)), pltpu.SemaphoreType.DMA((ns, 2)), pltpu.SemaphoreType.DMA((ns,))],
        compiler_params=_params("arbitrary"))(*[a for a, _ in prods], *[b for _, b in prods])
    return [pair.reshape(N_CHIP, blk, n) for pair in pairs]


def _segment_chunks(segs):
    bounds = [0]
    for s in segs:
        bounds.append(bounds[-1] + s.shape[1] // SEG_CHUNK)
    return bounds


def _input_grad(segs, wt, x, g, dy, name, tm=512, ride=None):
    m = segs[0].shape[0]
    rows, n = wt.shape
    tm = min(tm, m)
    bounds = _segment_chunks(segs)
    n_seg = len(segs)
    ni = m // tm
    host = _Hosted(ride, n_seg + 4, 2, 2, aliasing=True)

    def body(*refs):
        host_refs, start, finish = host.split(refs)
        a_refs = host_refs[:n_seg]
        wt_hbm, x_ref, g_ref, dy_ref, gx_ref, st_ref, wt_vmem, sems = host_refs[n_seg:]
        i = pl.program_id(0)
        start(i == 0)

        @pl.when(i == 0)
        def _():
            st_ref[...] = jnp.zeros_like(st_ref)

        ready = _load_resident(wt_hbm, wt_vmem, sems, i == 0)
        dh = None
        for s in range(n_seg):
            for c in range(bounds[s], bounds[s + 1]):
                ready(c)
            part = jnp.dot(a_refs[s][...], wt_vmem[bounds[s] * SEG_CHUNK:bounds[s + 1] * SEG_CHUNK, :], preferred_element_type=F32)
            dh = part if dh is None else dh + part
        xv = x_ref[...]
        r = lax.rsqrt(jnp.mean(xv * xv, axis=-1, keepdims=True) + EPS)
        xn = xv * r
        dxn = dh * g_ref[...]
        gx_ref[...] = dy_ref[...] + r * (dxn - xn * jnp.mean(dxn * xn, axis=-1, keepdims=True))
        st_ref[...] += _rows8([jnp.sum(dh * xn, axis=0, keepdims=True)], n)
        finish(i == ni - 1)

    tile = pl.BlockSpec((tm, n), lambda i: (i, 0))
    outs = pl.pallas_call(
        body, name=name, grid=(ni,),
        in_specs=[pl.BlockSpec((tm, sg.shape[1]), lambda i: (i, 0)) for sg in segs]
        + [ANY, tile, pl.BlockSpec((1, n), lambda i: (0, 0)), tile] + host.in_specs,
        out_specs=[tile, pl.BlockSpec((8, n), lambda i: (0, 0))] + host.out_specs,
        out_shape=[jax.ShapeDtypeStruct((m, n), F32), jax.ShapeDtypeStruct((8, n), F32)] + host.out_shapes,
        scratch_shapes=[pltpu.VMEM((rows, n), wt.dtype), pltpu.SemaphoreType.DMA((rows // SEG_CHUNK,))] + host.scratch_shapes,
        input_output_aliases=host.aliases,
        compiler_params=_params("arbitrary"))(*segs, wt, x, g, dy, *host.arrays)
    res, landed = host.results(outs, 2)
    return (*res, landed) if ride else tuple(res)


def _mm_tn_seg(segs, b, name):
    ktok = segs[0].shape[0]
    n = b.shape[1]
    bounds = _segment_chunks(segs)
    n_seg = len(segs)
    nc = bounds[-1]
    seg_of = [s for s in range(n_seg) for _ in range(bounds[s], bounds[s + 1])]

    def body(*refs):
        a_hbm, b_hbm, o_ref = refs[:n_seg], refs[n_seg], refs[n_seg + 1]
        a_buf, b_vmem, a_sems, b_sem = refs[n_seg + 2:]
        c = pl.program_id(0)

        def fetch(cc):
            s = seg_of[cc]
            cols = pl.ds((cc - bounds[s]) * SEG_CHUNK, SEG_CHUNK)
            return pltpu.make_async_copy(a_hbm[s].at[:, cols], a_buf.at[cc % 2], a_sems.at[cc % 2])

        @pl.when(c == 0)
        def _():
            whole = pltpu.make_async_copy(b_hbm, b_vmem, b_sem)
            whole.start()
            fetch(0).start()
            whole.wait()

        for cc in range(nc):
            @pl.when(c == cc)
            def _(cc=cc):
                if cc + 1 < nc:
                    fetch(cc + 1).start()
                fetch(cc).wait()

        o_ref[...] = lax.dot_general(a_buf[c % 2], b_vmem[...], TN_DIMS, preferred_element_type=F32)

    return pl.pallas_call(
        body, name=name, grid=(nc,),
        in_specs=[ANY] * (n_seg + 1), out_specs=pl.BlockSpec((SEG_CHUNK, n), lambda c: (c, 0)),
        out_shape=jax.ShapeDtypeStruct((nc * SEG_CHUNK, n), F32),
        scratch_shapes=[pltpu.VMEM((2, ktok, SEG_CHUNK), segs[0].dtype), pltpu.VMEM((ktok, n), b.dtype),
                        pltpu.SemaphoreType.DMA((2,)), pltpu.SemaphoreType.DMA],
        compiler_params=_params("arbitrary"))(*segs, b)


CHUNK_ORDER = (0, 4, 7, 10, 1, 5, 8, 11, 2, 6, 9, 12, 3)
EARLY_STEPS = 8


def _side_pieces():
    table = []
    for core in (0, 1):
        sides = ([None] * len(CHUNK_ORDER), [None] * len(CHUNK_ORDER))
        for s, cc in enumerate(CHUNK_ORDER):
            g0 = cc * SEG_CHUNK
            for d in range(N_DEV):
                lo, hi = max(g0, d * SHARD_IN), min(g0 + SEG_CHUNK, (d + 1) * SHARD_IN)
                if lo < hi:
                    side = sides[0 if d % 2 == core else 1]
                    assert side[s] is None
                    side[s] = (lo - g0, hi - lo, (d // 2) * SHARD_IN + lo - d * SHARD_IN)
        table.append(sides)
    return table


def _to_chip(src, red_ref, piece, s, send_sems, recv_sems, own, core):
    _, rows, at = piece
    k, r0 = divmod(at, SHARD_IN)
    return k, pltpu.make_async_remote_copy(
        src_ref=src, dst_ref=red_ref.at[own, pl.ds(r0, rows)], send_sem=send_sems.at[s], recv_sem=recv_sems.at[own, s],
        device_id=(k // 2, k % 2, core), device_id_type=MESH)


def _from_chip(src, red_ref, piece, s, send_sems, recv_sems, chip, me):
    _, rows, at = piece
    return pltpu.make_async_remote_copy(
        src_ref=src, dst_ref=red_ref.at[chip, pl.ds(at % SHARD_IN, rows)], send_sem=send_sems.at[s], recv_sem=recv_sems.at[chip, s],
        device_id=me, device_id_type=MESH)


def _mm_tn_seg_pair(segs, b, name):
    ktok = segs[0].shape[0]
    n = b.shape[1]
    bounds = _segment_chunks(segs)
    n_seg = len(segs)
    nc = bounds[-1]
    assert nc == len(CHUNK_ORDER)
    seg_of = [s for s in range(n_seg) for _ in range(bounds[s], bounds[s + 1])]
    table = _side_pieces()

    def body(*refs):
        a_hbm, b_hbm, pair_ref, red_ref = refs[:n_seg], refs[n_seg], refs[n_seg + 1], refs[n_seg + 2]
        (a_buf, b_vmem, res_buf, recv_all, pair_all, a_sems, b_sem, send_sems, recv_sems, out_sems,
         chip_send, chip_recv) = refs[n_seg + 3:]
        step = pl.program_id(0)
        px, py, pc = _place()
        own = 2 * px + py

        def fetch(s):
            sg = seg_of[CHUNK_ORDER[s]]
            cols = pl.ds((CHUNK_ORDER[s] - bounds[sg]) * SEG_CHUNK, SEG_CHUNK)
            return pltpu.make_async_copy(a_hbm[sg].at[:, cols], a_buf.at[s % 2], a_sems.at[s % 2])

        @pl.when(step == 0)
        def _():
            whole = pltpu.make_async_copy(b_hbm, b_vmem, b_sem)
            whole.start()
            fetch(0).start()
            whole.wait()

        for s in range(nc):
            @pl.when(step == s)
            def _(s=s):
                if s + 1 < nc:
                    fetch(s + 1).start()
                fetch(s).wait()

        res_buf[step % 2] = lax.dot_general(a_buf[step % 2], b_vmem[...], TN_DIMS, preferred_element_type=F32)

        def crossing(s, piece):
            off, rows, at = piece
            return pltpu.make_async_remote_copy(
                src_ref=res_buf.at[s % 2, pl.ds(off, rows)], dst_ref=recv_all.at[pl.ds(at, rows)],
                send_sem=send_sems.at[s], recv_sem=recv_sems.at[s], device_id=(px, py, 1 - pc), device_id_type=MESH)

        def write(s, piece):
            _, rows, at = piece
            return pltpu.make_async_copy(pair_all.at[pl.ds(at, rows)], pair_ref.at[pl.ds(at, rows)], out_sems.at[s])

        def to_chip(s, piece, core):
            return _to_chip(pair_all.at[pl.ds(piece[2], piece[1])], red_ref, piece, s, chip_send, chip_recv, own, core)

        for core in (0, 1):
            mine, theirs = table[core]

            def settle(s, mine=mine, theirs=theirs, core=core):
                if theirs[s] is not None:
                    crossing(s, theirs[s]).wait_send()
                if mine[s] is not None:
                    off, rows, at = mine[s]
                    crossing(s, mine[s]).wait_recv()
                    pair_all[at:at + rows] = (res_buf[s % 2, off:off + rows] + recv_all[at:at + rows]).astype(BF16)
                    write(s, mine[s]).start()
                    if s < EARLY_STEPS:
                        k, cp = to_chip(s, mine[s], core)
                        pl.when(own != k)(cp.start)

            for s in range(nc):
                @pl.when((pc == core) & (step == s))
                def _(s=s, settle=settle, mine=mine, theirs=theirs, core=core):
                    if theirs[s] is not None:
                        crossing(s, theirs[s]).start()
                    if s > 0:
                        settle(s - 1)
                    if s == nc - 1:
                        settle(s)
                        kept = [t for t in range(nc) if mine[t] is not None]
                        for t in kept:
                            write(t, mine[t]).wait()
                        early = [t for t in kept if t < EARLY_STEPS]
                        for t in early:
                            k, cp = to_chip(t, mine[t], core)
                            pl.when(own != k)(cp.wait_send)
                        for k in range(N_CHIP):
                            @pl.when(own == k)
                            def _(k=k):
                                for t in early:
                                    if mine[t][2] // SHARD_IN == k:
                                        for chip in range(N_CHIP):
                                            if chip != k:
                                                _from_chip(pair_all.at[pl.ds(mine[t][2], mine[t][1])], red_ref, mine[t], t,
                                                           chip_send, chip_recv, chip, (px, py, pc)).wait_recv()

    flat = jax.ShapeDtypeStruct((N_CHIP * SHARD_IN, n), BF16)
    pair, red = pl.pallas_call(
        body, name=name, grid=(nc,),
        in_specs=[ANY] * (n_seg + 1), out_specs=[ANY, ANY],
        out_shape=[flat, jax.ShapeDtypeStruct((N_CHIP, SHARD_IN, n), BF16)],
        scratch_shapes=[pltpu.VMEM((2, ktok, SEG_CHUNK), segs[0].dtype), pltpu.VMEM((ktok, n), b.dtype),
                        pltpu.VMEM((2, SEG_CHUNK, n), F32), pltpu.VMEM(flat.shape, F32), pltpu.VMEM(flat.shape, BF16),
                        pltpu.SemaphoreType.DMA((2,)), pltpu.SemaphoreType.DMA,
                        pltpu.SemaphoreType.DMA((nc,)), pltpu.SemaphoreType.DMA((nc,)), pltpu.SemaphoreType.DMA((nc,)),
                        pltpu.SemaphoreType.DMA((nc,)), pltpu.SemaphoreType.DMA((N_CHIP, nc))],
        compiler_params=_params("arbitrary"))(*segs, b)
    return pair.reshape(N_CHIP, SHARD_IN, n), red


def _branches_fwd(z_rnn, z_attn, ag_ml, b_gate, w_rnn, w_attn, w_out, x, target, g_post, name, tm=512):
    t, d = x.shape
    tm = min(tm, t)

    def body(zr_ref, za_ref, lr_ref, la_ref, br_ref, ba_ref, wr_ref, wa_ref, wo_ref, x_ref, t_ref, g_ref,
             brr_ref, bra_ref, mg_ref, do_ref, dy_ref, st_ref):
        @pl.when(pl.program_id(0) == 0)
        def _():
            st_ref[...] = jnp.zeros_like(st_ref)

        br_rnn = jnp.dot(zr_ref[...], wr_ref[...], preferred_element_type=F32)
        br_attn = jnp.dot(za_ref[...], wa_ref[...], preferred_element_type=F32)
        brr_ref[...] = br_rnn.astype(BF16)
        bra_ref[...] = br_attn.astype(BF16)
        g_rnn = _sigmoid(lr_ref[...].astype(F32) + br_ref[...])
        g_attn = _sigmoid(la_ref[...].astype(F32) + ba_ref[...])
        merged = (g_rnn * br_rnn + g_attn * br_attn).astype(BF16)
        mg_ref[...] = merged
        o = jnp.dot(merged, wo_ref[...], preferred_element_type=F32)
        g = g_ref[...]
        r = lax.rsqrt(jnp.mean(o * o, axis=-1, keepdims=True) + EPS)
        nrm = o * r
        err = x_ref[...] + nrm * g - t_ref[...]
        dy = err * (1.0 / d)
        dy_ref[...] = dy
        dn = dy * g
        do_ref[...] = (r * (dn - nrm * jnp.mean(dn * nrm, axis=-1, keepdims=True))).astype(BF16)
        st_ref[...] += _rows8([jnp.sum(dy * nrm, axis=0, keepdims=True), jnp.sum(err * err, axis=0, keepdims=True)], d)

    tile = pl.BlockSpec((tm, d), lambda i: (i, 0))
    weight = pl.BlockSpec((d, d), lambda i: (0, 0))
    bf = jax.ShapeDtypeStruct((t, d), BF16)
    return pl.pallas_call(
        body, name=name, grid=(t // tm,),
        in_specs=[tile, tile, pl.BlockSpec((tm, d), lambda i: (i, 1)), pl.BlockSpec((tm, d), lambda i: (i, 2)),
                  pl.BlockSpec((1, d), lambda i: (0, 0)), pl.BlockSpec((1, d), lambda i: (0, 1)),
                  weight, weight, weight, tile, tile, pl.BlockSpec((1, d), lambda i: (0, 0))],
        out_specs=[tile, tile, tile, tile, tile, pl.BlockSpec((8, d), lambda i: (0, 0))],
        out_shape=[bf, bf, bf, bf, jax.ShapeDtypeStruct((t, d), F32), jax.ShapeDtypeStruct((8, d), F32)],
        compiler_params=_params("arbitrary"))(z_rnn, z_attn, ag_ml, ag_ml, b_gate, b_gate, w_rnn, w_attn, w_out, x, target, g_post)


def _branches_bwd(dout, br_rnn, br_attn, ag_ml, b_gate, w_rnn, w_attn, w_out, name, tm=512):
    t, d = br_rnn.shape
    tm = min(tm, t)

    def body(do_ref, r_ref, a_ref, lr_ref, la_ref, br_ref, ba_ref, wr_ref, wa_ref, wo_ref,
             dr_ref, da_ref, dl_ref, dzr_ref, dza_ref, st_ref, wt_ref):
        @pl.when(pl.program_id(0) == 0)
        def _():
            st_ref[...] = jnp.zeros_like(st_ref)
            wt_ref[0] = wo_ref[...].T
            wt_ref[1] = wr_ref[...].T
            wt_ref[2] = wa_ref[...].T

        dm = jnp.dot(do_ref[...], wt_ref[0], preferred_element_type=F32)
        g_rnn = _sigmoid(lr_ref[...].astype(F32) + br_ref[...])
        g_attn = _sigmoid(la_ref[...].astype(F32) + ba_ref[...])
        dbr_rnn = (dm * g_rnn).astype(BF16)
        dbr_attn = (dm * g_attn).astype(BF16)
        dr_ref[...] = dbr_rnn
        da_ref[...] = dbr_attn
        dl_rnn = dm * r_ref[...].astype(F32) * g_rnn * (1.0 - g_rnn)
        dl_attn = dm * a_ref[...].astype(F32) * g_attn * (1.0 - g_attn)
        dl_ref[:, 0:d] = dl_rnn.astype(BF16)
        dl_ref[:, d:2 * d] = dl_attn.astype(BF16)
        st_ref[...] += _rows8([jnp.sum(dl_rnn, axis=0, keepdims=True), jnp.sum(dl_attn, axis=0, keepdims=True)], d)
        dzr_ref[...] = jnp.dot(dbr_rnn, wt_ref[1], preferred_element_type=F32).astype(BF16)
        dza_ref[...] = jnp.dot(dbr_attn, wt_ref[2], preferred_element_type=F32).astype(BF16)

    tile = pl.BlockSpec((tm, d), lambda i: (i, 0))
    weight = pl.BlockSpec((d, d), lambda i: (0, 0))
    bf = jax.ShapeDtypeStruct((t, d), BF16)
    return pl.pallas_call(
        body, name=name, grid=(t // tm,),
        in_specs=[tile, tile, tile, pl.BlockSpec((tm, d), lambda i: (i, 1)), pl.BlockSpec((tm, d), lambda i: (i, 2)),
                  pl.BlockSpec((1, d), lambda i: (0, 0)), pl.BlockSpec((1, d), lambda i: (0, 1)), weight, weight, weight],
        out_specs=[tile, tile, pl.BlockSpec((tm, 2 * d), lambda i: (i, 0)), tile, tile, pl.BlockSpec((8, d), lambda i: (0, 0))],
        out_shape=[bf, bf, jax.ShapeDtypeStruct((t, 2 * d), BF16), bf, bf, jax.ShapeDtypeStruct((8, d), F32)],
        scratch_shapes=[pltpu.VMEM((3, d, d), BF16)],
        compiler_params=_params("arbitrary"))(dout, br_rnn, br_attn, ag_ml, ag_ml, b_gate, b_gate, w_rnn, w_attn, w_out)


def _lru_gates(c, wa, ba, wx, bx, sp):
    cb = c.astype(BF16)
    r = _sigmoid(jnp.dot(cb, wa, preferred_element_type=F32) + ba)
    ig = _sigmoid(jnp.dot(cb, wx, preferred_element_type=F32) + bx)
    log_a = (-LRU_C) * r * sp
    a = jnp.exp(log_a)
    mult = jnp.sqrt(-jnp.tanh(log_a) * (a * a + 1.0))
    return cb, r, ig, a, mult


SUBLANES = 8


def _scan_fwd(a, u, carry, tt):
    w = a.shape[1]
    ng = tt // SUBLANES
    a3 = a.reshape(ng, SUBLANES, w)
    u3 = u.reshape(ng, SUBLANES, w)
    sub = lax.broadcasted_iota(jnp.int32, (ng, SUBLANES, w), 1)
    d = 1
    while d < SUBLANES:
        keep = sub >= d
        u3 = u3 + a3 * jnp.where(keep, pltpu.roll(u3, d, 1), 0.0)
        a3 = a3 * jnp.where(keep, pltpu.roll(a3, d, 1), 1.0)
        d *= 2
    out = []
    for g in range(ng):
        hg = u3[g] + a3[g] * carry
        out.append(hg)
        carry = hg[SUBLANES - 1:SUBLANES, :]
    return jnp.concatenate(out, axis=0)


def _scan_rev(b, g, carry, tt):
    w = b.shape[1]
    ng = tt // SUBLANES
    b3 = b.reshape(ng, SUBLANES, w)
    g3 = g.reshape(ng, SUBLANES, w)
    sub = lax.broadcasted_iota(jnp.int32, (ng, SUBLANES, w), 1)
    d = 1
    while d < SUBLANES:
        keep = sub < SUBLANES - d
        g3 = g3 + b3 * jnp.where(keep, pltpu.roll(g3, SUBLANES - d, 1), 0.0)
        b3 = b3 * jnp.where(keep, pltpu.roll(b3, SUBLANES - d, 1), 1.0)
        d *= 2
    out = [None] * ng
    for k in range(ng - 1, -1, -1):
        hk = g3[k] + b3[k] * carry
        out[k] = hk
        carry = hk[0:1, :]
    return jnp.concatenate(out, axis=0)


def _conv_taps(cw, bias, x, ext_ref, tt):
    x2 = ext_ref[7:7 + tt, :]
    x1 = ext_ref[6:6 + tt, :]
    x0 = ext_ref[5:5 + tt, :]
    c = bias + cw[3:4] * x + cw[2:3] * x2 + cw[1:2] * x1 + cw[0:1] * x0
    return c, x2, x1, x0


def _rnn_fwd(rx_rg, cw, cb, wa, ba, wx, bx, lam, name, tt=512):
    t = rx_rg.shape[0]
    tt = min(tt, t)
    w = GROUP_W

    def body(rx_ref, rg_ref, cw_ref, cb_ref, wa_ref, ba_ref, wx_ref, bx_ref, lam_ref, y_ref, z_ref, ext_ref, hc_ref):
        @pl.when(pl.program_id(1) == 0)
        def _():
            ext_ref[0:8, :] = jnp.zeros((8, w), F32)
            hc_ref[...] = jnp.zeros((8, w), F32)

        x = rx_ref[...]
        ext_ref[8:8 + tt, :] = x
        c, _, _, _ = _conv_taps(cw_ref[...], cb_ref[...], x, ext_ref, tt)
        ext_ref[0:8, :] = x[tt - 8:tt, :]
        sp = _softplus(-lam_ref[...])
        _, _, ig, a, mult = _lru_gates(c, wa_ref[...], ba_ref[...], wx_ref[...], bx_ref[...], sp)
        h = _scan_fwd(a, mult * (ig * c), hc_ref[7:8, :], tt)
        hc_ref[...] = h[tt - 8:tt, :]
        y_ref[...] = h
        rg = rg_ref[...]
        z_ref[...] = (h * rg * _sigmoid(rg)).astype(BF16)

    vec = pl.BlockSpec((1, w), lambda g, i: (0, g))
    mat = pl.BlockSpec((None, w, w), lambda g, i: (g, 0, 0))
    tile = pl.BlockSpec((tt, w), lambda g, i: (i, g))
    return pl.pallas_call(
        body, name=name, grid=(N_GROUPS, t // tt),
        in_specs=[tile, pl.BlockSpec((tt, w), lambda g, i: (i, N_GROUPS + g)),
                  pl.BlockSpec((4, w), lambda g, i: (0, g)), vec, mat, vec, mat, vec, vec],
        out_specs=[tile, tile],
        out_shape=[jax.ShapeDtypeStruct((t, D_RNN), F32), jax.ShapeDtypeStruct((t, D_RNN), BF16)],
        scratch_shapes=[pltpu.VMEM((tt + 8, w), F32), pltpu.VMEM((8, w), F32)],
        compiler_params=_params("parallel", "arbitrary"))(rx_rg, rx_rg, cw, cb, wa, ba, wx, bx, lam)


def _rnn_bwd(rx_rg, y, dz, cw, cb, wa, ba, wx, bx, lam, name, tt=512, ride=None):
    t = rx_rg.shape[0]
    tt = min(tt, t)
    nt = t // tt
    w = GROUP_W

    host = _Hosted(ride, 13, 5, 6)

    def body(*refs):
        host_refs, start, finish = host.split(refs)
        (rx_ref, rg_ref, rxt_ref, y_ref, yt_ref, dz_ref, cw_ref, cb_ref, wa_ref, ba_ref, wx_ref, bx_ref, lam_ref,
         drx_ref, drg_ref, st_ref, gda_ref, gdx_ref, ext_ref, dcx_ref, wcar_ref, acar_ref, dwa_ref, dwx_ref) = host_refs
        ii = pl.program_id(1)
        start((pl.program_id(0) == 0) & (ii == 0))

        @pl.when(ii == 0)
        def _():
            wcar_ref[...] = jnp.zeros((8, w), F32)
            acar_ref[...] = jnp.zeros((8, w), F32)
            dcx_ref[tt:tt + 8, :] = jnp.zeros((8, w), F32)
            st_ref[...] = jnp.zeros_like(st_ref)
            dwa_ref[...] = jnp.zeros_like(dwa_ref)
            dwx_ref[...] = jnp.zeros_like(dwx_ref)

        has_prev = jnp.where(ii == nt - 1, 0.0, 1.0)
        x = rx_ref[...]
        ext_ref[0:8, :] = rxt_ref[...] * has_prev
        ext_ref[8:8 + tt, :] = x
        cwv = cw_ref[...]
        c, x2, x1, x0 = _conv_taps(cwv, cb_ref[...], x, ext_ref, tt)
        lam = lam_ref[...]
        sp = _softplus(-lam)
        wa = wa_ref[...]
        wx = wx_ref[...]
        cb16, r, ig, a, mult = _lru_gates(c, wa, ba_ref[...], wx, bx_ref[...], sp)

        rg = rg_ref[...]
        sg = _sigmoid(rg)
        dz = dz_ref[...].astype(F32)
        yv = y_ref[...]
        drg_ref[...] = (dz * yv * (sg * (1.0 + rg * (1.0 - sg)))).astype(BF16)

        row = lax.broadcasted_iota(jnp.int32, (tt, w), 0)
        b = jnp.where(row < tt - 1, pltpu.roll(a, tt - 1, 0), acar_ref[0:1, :])
        dh = _scan_rev(b, dz * (rg * sg), wcar_ref[0:1, :], tt)
        wcar_ref[...] = dh[0:8, :]
        acar_ref[...] = a[0:8, :]

        hprev = jnp.where(row >= 1, pltpu.roll(yv, 1, 0), yt_ref[7:8, :] * has_prev)
        dmult = dh * (ig * c)
        dig = dh * mult * c
        dlog_a = dh * hprev * a - dmult * (a * a / mult)
        dpa = dlog_a * ((-LRU_C) * sp) * r * (1.0 - r)
        dpx = dig * ig * (1.0 - ig)
        dsp = jnp.sum(dlog_a * r, axis=0, keepdims=True) * (-LRU_C)
        dlam = dsp * (-_sigmoid(-lam))
        dpa16 = dpa.astype(BF16)
        dpx16 = dpx.astype(BF16)
        dwa_ref[...] += lax.dot_general(cb16, dpa16, TN_DIMS, preferred_element_type=F32)
        dwx_ref[...] += lax.dot_general(cb16, dpx16, TN_DIMS, preferred_element_type=F32)
        dc = (dh * mult * ig
              + lax.dot_general(dpa16, wa, NT_DIMS, preferred_element_type=F32)
              + lax.dot_general(dpx16, wx, NT_DIMS, preferred_element_type=F32))

        dcx_ref[0:tt, :] = dc
        drx = (cwv[3:4] * dc + cwv[2:3] * dcx_ref[1:1 + tt, :] + cwv[1:2] * dcx_ref[2:2 + tt, :]
               + cwv[0:1] * dcx_ref[3:3 + tt, :])
        drx_ref[...] = drx.astype(BF16)
        dcx_ref[tt:tt + 8, :] = dc[0:8, :]

        def colsum(v):
            return jnp.sum(v, axis=0, keepdims=True)

        st_ref[...] += _rows8([colsum(dc), colsum(dpa), colsum(dpx), dlam,
                               colsum(dc * x0), colsum(dc * x1), colsum(dc * x2), colsum(dc * x)], w)

        @pl.when(ii == nt - 1)
        def _():
            for blk in range(GROUP_W // RNN_BLOCK_W):
                rows = slice(blk * RNN_BLOCK_W, (blk + 1) * RNN_BLOCK_W)
                gda_ref[blk] = dwa_ref[rows, rows]
                gdx_ref[blk] = dwx_ref[rows, rows]

        finish((pl.program_id(0) == N_GROUPS - 1) & (ii == nt - 1))

    def rev(ii):
        return nt - 1 - ii

    def tail(g, ii):
        return (jnp.maximum(rev(ii) * (tt // 8) - 1, 0), g)

    vec = pl.BlockSpec((1, w), lambda g, ii: (0, g))
    mat = pl.BlockSpec((None, w, w), lambda g, ii: (g, 0, 0))
    tile = pl.BlockSpec((tt, w), lambda g, ii: (rev(ii), g))
    diag_shape = (N_GROUPS, GROUP_W // RNN_BLOCK_W, RNN_BLOCK_W, RNN_BLOCK_W)
    diag = pl.BlockSpec((None,) + diag_shape[1:], lambda g, ii: (g, 0, 0, 0))
    outs = pl.pallas_call(
        body, name=name, grid=(N_GROUPS, nt),
        in_specs=[tile, pl.BlockSpec((tt, w), lambda g, ii: (rev(ii), N_GROUPS + g)), pl.BlockSpec((8, w), tail),
                  tile, pl.BlockSpec((8, w), tail), tile,
                  pl.BlockSpec((4, w), lambda g, ii: (0, g)), vec, mat, vec, mat, vec, vec] + host.in_specs,
        out_specs=[tile, tile, pl.BlockSpec((8, w), lambda g, ii: (0, g)), diag, diag] + host.out_specs,
        out_shape=[jax.ShapeDtypeStruct((t, D_RNN), BF16), jax.ShapeDtypeStruct((t, D_RNN), BF16),
                   jax.ShapeDtypeStruct((8, D_RNN), F32),
                   jax.ShapeDtypeStruct(diag_shape, F32), jax.ShapeDtypeStruct(diag_shape, F32)] + host.out_shapes,
        scratch_shapes=[pltpu.VMEM((tt + 8, w), F32), pltpu.VMEM((tt + 8, w), F32), pltpu.VMEM((8, w), F32),
                        pltpu.VMEM((8, w), F32), pltpu.VMEM((w, w), F32), pltpu.VMEM((w, w), F32)] + host.scratch_shapes,
        compiler_params=_params("arbitrary" if ride else "parallel", "arbitrary"))(
            rx_rg, rx_rg, rx_rg, y, y, dz, cw, cb, wa, ba, wx, bx, lam, *host.arrays)
    res, landed = host.results(outs, 5)
    return (*res, landed) if ride else tuple(res)


def _half_mask(shape, half):
    lane = lax.broadcasted_iota(jnp.int32, shape, 1)
    return (lane >= HEAD_DIM) if half else (lane < HEAD_DIM)


def _dup_half(t, half):
    sel = jnp.where(_half_mask(t.shape, half), t, 0.0)
    return sel + pltpu.roll(sel, HEAD_DIM, 1)


def _band_geometry(n):
    qi = lax.broadcasted_iota(jnp.int32, (BLOCK, 2 * BLOCK), 0)
    kj = lax.broadcasted_iota(jnp.int32, (BLOCK, 2 * BLOCK), 1)
    dist = BLOCK + qi - kj
    first_key = jnp.where(n > 0, 0, BLOCK)
    valid = (dist >= 0) & (dist < BLOCK) & (kj >= first_key)
    return dist.astype(F32), valid


GROUP = 4


def _kv_dup(prev_ref, cur_ref, hk, scale=1.0):
    tile = hk // 2
    kt = jnp.concatenate([prev_ref[:, tile * LANE:(tile + 1) * LANE], cur_ref[:, tile * LANE:(tile + 1) * LANE]], axis=0)
    return (_dup_half(kt.astype(F32), hk % 2) * scale).astype(BF16)


def _fill_bias(bias_ref, sm_ref, n):
    distf, valid = _band_geometry(n)
    for head in range(N_Q_HEADS):
        bias_ref[head] = jnp.where(valid, -sm_ref[1, head] * distf, MASKED)


def _head_scores(q2s, half, kdup, bias):
    qm = jnp.where(_half_mask(q2s.shape, half), q2s, jnp.zeros_like(q2s))
    return qm, lax.dot_general(qm, kdup, NT_DIMS, preferred_element_type=F32) + bias


def _attn_specs(nb, clamp_last):
    def blk(n):
        return jnp.minimum(n, nb - 1) if clamp_last else n

    q_spec = pl.BlockSpec((BLOCK, D_MODEL), lambda n: (blk(n), 0))
    k_prev = pl.BlockSpec((BLOCK, D_KV), lambda n: (jnp.maximum(blk(n) - 1, 0), D_MODEL // D_KV))
    k_cur = pl.BlockSpec((BLOCK, D_KV), lambda n: (blk(n), D_MODEL // D_KV))
    v_prev = pl.BlockSpec((BLOCK, D_KV), lambda n: (jnp.maximum(blk(n) - 1, 0), D_MODEL // D_KV + 1))
    v_cur = pl.BlockSpec((BLOCK, D_KV), lambda n: (blk(n), D_MODEL // D_KV + 1))
    return q_spec, k_prev, k_cur, v_prev, v_cur


def _attn_fwd(sm, qkv, ag_ml, name, ride=None):
    t = qkv.shape[0]
    nb = t // BLOCK

    host = _Hosted(ride, 7, 3, 1)

    def body(*refs):
        (sm_ref, q_ref, kp_ref, kc_ref, vp_ref, vc_ref, ag_ref, y_ref, z_ref, lse_ref, bias_ref), start, finish = host.split(refs)
        n = pl.program_id(0)
        start(n == 0)
        start.middle(n == (3 * nb) // 4)

        @pl.when(n <= 1)
        def _():
            _fill_bias(bias_ref, sm_ref, n)

        lane = lax.broadcasted_iota(jnp.int32, (BLOCK, LANE), 1)
        low = lane < HEAD_DIM
        lse = jnp.zeros((BLOCK, LANE), F32)
        for hk in range(N_Q_HEADS // GROUP):
            kdup = _kv_dup(kp_ref, kc_ref, hk)
            vdup = _kv_dup(vp_ref, vc_ref, hk)
            for k in (0, 1):
                c = slice((2 * hk + k) * LANE, (2 * hk + k + 1) * LANE)
                q2s = q_ref[:, c] * ATTN_SCALE
                outs = []
                for half in (0, 1):
                    head = GROUP * hk + 2 * k + half
                    _, s = _head_scores(q2s, half, kdup, bias_ref[head])
                    sink = sm_ref[0, head]
                    m = jnp.maximum(jnp.max(s, axis=1, keepdims=True), sink)
                    e = jnp.exp(s - m)
                    l = jnp.sum(e, axis=1, keepdims=True) + jnp.exp(sink - m)
                    outs.append(jnp.dot((e * (1.0 / l)).astype(BF16), vdup, preferred_element_type=F32))
                    lse = jnp.where(lane == head, m + jnp.log(l), lse)
                yt = jnp.where(low, outs[0], outs[1])
                y_ref[:, c] = yt
                ag = ag_ref[:, c].astype(F32)
                z_ref[:, c] = (yt * ag * _sigmoid(ag)).astype(BF16)
        lse_ref[...] = lse
        finish(n == nb - 1)

    q_spec, k_prev, k_cur, v_prev, v_cur = _attn_specs(nb, False)
    wide = pl.BlockSpec((BLOCK, D_MODEL), lambda n: (n, 0))
    outs = pl.pallas_call(
        body, name=name, grid=(nb,),
        in_specs=[pl.BlockSpec(memory_space=pltpu.SMEM), q_spec, k_prev, k_cur, v_prev, v_cur, wide] + host.in_specs,
        out_specs=[wide, wide, pl.BlockSpec((BLOCK, LANE), lambda n: (n, 0))] + host.out_specs,
        out_shape=[jax.ShapeDtypeStruct((t, D_MODEL), F32), jax.ShapeDtypeStruct((t, D_MODEL), BF16),
                   jax.ShapeDtypeStruct((t, LANE), F32)] + host.out_shapes,
        scratch_shapes=[pltpu.VMEM((N_Q_HEADS, BLOCK, 2 * BLOCK), F32)] + host.scratch_shapes,
        compiler_params=_params("arbitrary"))(sm, qkv, qkv, qkv, qkv, qkv, ag_ml, *host.arrays)
    res, landed = host.results(outs, 3)
    return (*res, landed) if ride else tuple(res)


def _attn_bwd(sm, qkv, ag_ml, y, lse, dz, name, ride=None):
    t = qkv.shape[0]
    nb = t // BLOCK
    host = _Hosted(ride, 10, 4, 3)

    def body(*refs):
        host_refs, start, finish = host.split(refs)
        (sm_ref, q_ref, kp_ref, kc_ref, vp_ref, vc_ref, ag_ref, y_ref, lse_ref, dz_ref,
         dq_ref, dkv_ref, dag_ref, ds_ref, ck_ref, cv_ref, bias_ref) = host_refs
        n = pl.program_id(0)
        start(n == 0)
        start.middle(n == (3 * nb) // 4)

        @pl.when(n == 0)
        def _():
            ck_ref[...] = jnp.zeros_like(ck_ref)
            cv_ref[...] = jnp.zeros_like(cv_ref)
            ds_ref[...] = jnp.zeros_like(ds_ref)

        @pl.when(n <= 1)
        def _():
            _fill_bias(bias_ref, sm_ref, n)

        @pl.when(n < nb)
        def _():
            lane8 = lax.broadcasted_iota(jnp.int32, (8, LANE), 1)
            row8 = lax.broadcasted_iota(jnp.int32, (8, LANE), 0)
            dsink = jnp.zeros((8, LANE), F32)
            dk_heads, dv_heads = [], []
            lse_tile = lse_ref[...]
            for hk in range(N_Q_HEADS // GROUP):
                kdup = _kv_dup(kp_ref, kc_ref, hk)
                ks = _kv_dup(kp_ref, kc_ref, hk, ATTN_SCALE)
                vdup = _kv_dup(vp_ref, vc_ref, hk)
                qms, dyhs, y_rows = [], [], []
                for k in (0, 1):
                    c = slice((2 * hk + k) * LANE, (2 * hk + k + 1) * LANE)
                    ag = ag_ref[:, c].astype(F32)
                    sg = _sigmoid(ag)
                    dzt = dz_ref[:, c].astype(F32)
                    yt = y_ref[:, c]
                    dag_ref[:, c] = (dzt * yt * (sg * (1.0 + ag * (1.0 - sg)))).astype(BF16)
                    dyt = dzt * (ag * sg)
                    q2s = q_ref[:, c] * ATTN_SCALE
                    for half in (0, 1):
                        hm = _half_mask(q2s.shape, half)
                        qms.append(jnp.where(hm, q2s, jnp.zeros_like(q2s)))
                        dyhs.append(jnp.where(hm, dyt, 0.0))
                        y_rows.append(yt)
                qm4 = jnp.concatenate(qms, axis=0)
                dy4 = jnp.concatenate(dyhs, axis=0)
                dy4_16 = dy4.astype(BF16)
                s4 = lax.dot_general(qm4, kdup, NT_DIMS, preferred_element_type=F32)
                dp4 = lax.dot_general(dy4_16, vdup, NT_DIMS, preferred_element_type=F32)
                probs16, ds16 = [], []
                for r in range(GROUP):
                    head = GROUP * hk + r
                    rows = slice(r * BLOCK, (r + 1) * BLOCK)
                    lh = lse_tile[:, head:head + 1]
                    probs = jnp.exp(s4[rows] + bias_ref[head] - lh)
                    psink = jnp.exp(sm_ref[0, head] - lh)
                    delta = jnp.sum(dyhs[r] * y_rows[r], axis=1, keepdims=True)
                    ds16.append((probs * (dp4[rows] - delta)).astype(BF16))
                    probs16.append(probs.astype(BF16))
                    dsink = dsink + jnp.where((row8 == 0) & (lane8 == head),
                                              -jnp.sum(psink * delta, axis=0, keepdims=True), 0.0)
                ds4 = jnp.concatenate(ds16, axis=0)
                p4 = jnp.concatenate(probs16, axis=0)
                dq4 = jnp.dot(ds4, ks, preferred_element_type=F32)
                low = _half_mask((BLOCK, LANE), 0)
                for k in (0, 1):
                    c = slice((2 * hk + k) * LANE, (2 * hk + k + 1) * LANE)
                    dq_ref[:, c] = jnp.where(low, dq4[2 * k * BLOCK:(2 * k + 1) * BLOCK],
                                             dq4[(2 * k + 1) * BLOCK:(2 * k + 2) * BLOCK]).astype(BF16)
                dk_acc = lax.dot_general(ds4, qm4, TN_DIMS, preferred_element_type=F32)
                dv_acc = lax.dot_general(p4, dy4_16, TN_DIMS, preferred_element_type=F32)
                dk_heads.append(dk_acc + pltpu.roll(dk_acc, HEAD_DIM, 1))
                dv_heads.append(dv_acc + pltpu.roll(dv_acc, HEAD_DIM, 1))
            ds_ref[...] += dsink
            low = _half_mask((2 * BLOCK, LANE), 0)
            for tile in range(2):
                cols = slice(tile * LANE, (tile + 1) * LANE)
                dkt = jnp.where(low, dk_heads[2 * tile], dk_heads[2 * tile + 1])
                dvt = jnp.where(low, dv_heads[2 * tile], dv_heads[2 * tile + 1])
                dkv_ref[:, cols] = (ck_ref[:, cols] + dkt[0:BLOCK, :]).astype(BF16)
                dkv_ref[:, D_KV + tile * LANE:D_KV + (tile + 1) * LANE] = (cv_ref[:, cols] + dvt[0:BLOCK, :]).astype(BF16)
                ck_ref[:, cols] = dkt[BLOCK:2 * BLOCK, :]
                cv_ref[:, cols] = dvt[BLOCK:2 * BLOCK, :]

        @pl.when(n == nb)
        def _():
            dkv_ref[:, 0:D_KV] = ck_ref[...].astype(BF16)
            dkv_ref[:, D_KV:2 * D_KV] = cv_ref[...].astype(BF16)

        finish(n == nb)

    q_spec, k_prev, k_cur, v_prev, v_cur = _attn_specs(nb, True)
    wide = pl.BlockSpec((BLOCK, D_MODEL), lambda n: (jnp.minimum(n, nb - 1), 0))
    outs = pl.pallas_call(
        body, name=name, grid=(nb + 1,),
        in_specs=[pl.BlockSpec(memory_space=pltpu.SMEM), q_spec, k_prev, k_cur, v_prev, v_cur, wide, wide,
                  pl.BlockSpec((BLOCK, LANE), lambda n: (jnp.minimum(n, nb - 1), 0)), wide] + host.in_specs,
        out_specs=[wide, pl.BlockSpec((BLOCK, 2 * D_KV), lambda n: (jnp.maximum(n - 1, 0), 0)), wide,
                   pl.BlockSpec((8, LANE), lambda n: (0, 0))] + host.out_specs,
        out_shape=[jax.ShapeDtypeStruct((t, D_MODEL), BF16), jax.ShapeDtypeStruct((t, 2 * D_KV), BF16),
                   jax.ShapeDtypeStruct((t, D_MODEL), BF16), jax.ShapeDtypeStruct((8, LANE), F32)] + host.out_shapes,
        scratch_shapes=[pltpu.VMEM((BLOCK, D_KV), F32), pltpu.VMEM((BLOCK, D_KV), F32),
                        pltpu.VMEM((N_Q_HEADS, BLOCK, 2 * BLOCK), F32)] + host.scratch_shapes,
        compiler_params=_params("arbitrary"))(sm, qkv, qkv, qkv, qkv, qkv, ag_ml, y, lse, dz, *host.arrays)
    res, landed = host.results(outs, 4)
    return (*res, landed) if ride else tuple(res)


def _local_grads(x, target, p, project, late_weights=None, reduce_out=None, reduce_in=None):
    h, rx_rg, qkv, ag_ml, wt = project(x, p["pre_g"])
    if late_weights is None:
        y_attn, z_attn, lse = _attn_fwd(p["sm"], qkv, ag_ml, "attn_fwd")
    else:
        y_attn, z_attn, lse, landed = _attn_fwd(p["sm"], qkv, ag_ml, "attn_fwd", ride=late_weights[0])
        p = {**p, **late_weights[1](landed)}
    lru = (p["cw"], p["cb"], p["wbd_a"], p["b_a"], p["wbd_x"], p["b_x"], p["lam"])
    y_rnn, z_rnn = _rnn_fwd(rx_rg, *lru, "rnn_fwd")
    br_rnn, br_attn, merged, dout, dy, st_post = _branches_fwd(
        z_rnn, z_attn, ag_ml, p["b_gate"], p["w_rnn"], p["w_attn"], p["w_out"], x, target, p["post_g"], "branches_fwd")

    dbr_rnn, dbr_attn, d_ml, dz_rnn, dz_attn, st_merge = _branches_bwd(
        dout, br_rnn, br_attn, ag_ml, p["b_gate"], p["w_rnn"], p["w_attn"], p["w_out"], "branches_bwd")
    out_prods = [(z_rnn, dbr_rnn), (z_attn, dbr_attn), (merged, dout)]
    gw_rnn = gw_attn = gw_out = red_out = red_in = None
    if reduce_out is None:
        gw_rnn, gw_attn, gw_out = (_mm_tn(a, b, nm) for (a, b), nm in zip(out_prods, ("gw_rnn", "gw_attn", "gw_out")))
    else:
        out_pairs = _mm_tn_pairs(out_prods, "gw_outs")
    d_rx, d_rg, st_rnn, g_rg_a, g_rg_x = _rnn_bwd(rx_rg, y_rnn, dz_rnn, *lru, "rnn_bwd")
    if reduce_out is None:
        dq, dkv, d_ag, st_sink = _attn_bwd(p["sm"], qkv, ag_ml, y_attn, lse, dz_attn, "attn_bwd")
    else:
        dq, dkv, d_ag, st_sink, red_out = _attn_bwd(p["sm"], qkv, ag_ml, y_attn, lse, dz_attn, "attn_bwd",
                                                    ride=reduce_out(out_pairs, g_rg_a, g_rg_x))

    segs = [d_rx, d_rg, dq, dkv, d_ag, d_ml]
    if reduce_in is None:
        gwt = _mm_tn_seg(segs, h, "gw_in")
        grad_x, st_pre = _input_grad(segs, wt, x, p["pre_g"], dy, "input_grad")
    else:
        gwt = None
        grad_x, st_pre, red_in = _input_grad(segs, wt, x, p["pre_g"], dy, "input_grad",
                                             ride=reduce_in(*_mm_tn_seg_pair(segs, h, "gw_in")))
    return dict(grad_x=grad_x, gwt=gwt, gw_rnn=gw_rnn, gw_attn=gw_attn, gw_out=gw_out,
                st_post=st_post, st_merge=st_merge, st_rnn=st_rnn, st_sink=st_sink, st_pre=st_pre,
                g_rg_a=g_rg_a, g_rg_x=g_rg_x, red_out=red_out, red_in=red_in)


def _place():
    x, y, c = lax.axis_index("x"), lax.axis_index("y"), lax.axis_index("c")
    return x, y, c


def _gather_ride(shards):
    n = len(shards)

    def copies(ins, outs, sems):
        send_sems, recv_sems, local_sems = sems
        x, y, c = _place()
        me, sibling = (x, y, c), (x, y, 1 - c)
        chips = [(1 - x, y), (x, 1 - y), (1 - x, 1 - y)]

        def slot(a, dev):
            return outs[a].at[4 * dev[0] + 2 * dev[1] + dev[2]]

        def copy(a, k, block, to, src=None):
            return pltpu.make_async_remote_copy(
                src_ref=slot(a, block) if src is None else src, dst_ref=slot(a, block),
                send_sem=send_sems.at[a, k], recv_sem=recv_sems.at[a, k], device_id=to, device_id_type=MESH)

        mine = [pltpu.make_async_copy(ins[a], slot(a, me), local_sems.at[a]) for a in range(n)]
        first = []
        for a in range(n):
            first.append(copy(a, 0, me, sibling, src=ins[a]))
            first += [copy(a, 1 + j, me, (*chip, c), src=ins[a]) for j, chip in enumerate(chips)]
        return me, sibling, chips, c, copy, mine, first

    def start(ins, outs, sems):
        *_, mine, first = copies(ins, outs, sems)
        for cp in mine + first:
            cp.start()

    def middle(ins, outs, sems):
        me, sibling, chips, c, copy, _, _ = copies(ins, outs, sems)
        for j, chip in enumerate(chips):
            for a in range(n):
                copy(a, 1 + j, (*chip, c), me).wait_recv()
                copy(a, 4 + j, (*chip, c), sibling).start()

    def finish(ins, outs, sems):
        me, sibling, chips, c, copy, mine, first = copies(ins, outs, sems)
        passed = [copy(a, 4 + j, (*chip, c), sibling) for j, chip in enumerate(chips) for a in range(n)]
        for a in range(n):
            copy(a, 0, sibling, me).wait_recv()
            for j, chip in enumerate(chips):
                copy(a, 4 + j, (*chip, 1 - c), me).wait_recv()
        for cp in first + passed:
            cp.wait_send()
        for cp in mine:
            cp.wait()

    return _Ride(
        shards, [jax.ShapeDtypeStruct((N_DEV, *s.shape), s.dtype) for s in shards],
        [pltpu.SemaphoreType.DMA((n, 7)), pltpu.SemaphoreType.DMA((n, 7)), pltpu.SemaphoreType.DMA((n,))],
        start, finish, middle)


def _chips_ride(scatter, whole):
    ns, nw = len(scatter), len(whole)
    n = ns + nw

    def copies(ins, outs, sems):
        send_sems, recv_sems, local_sems = sems
        x, y, c = _place()
        own = 2 * x + y
        chips = [(1 - x, y), (x, 1 - y), (1 - x, 1 - y)]

        def src(a, chip_idx):
            return ins[a].at[chip_idx] if a < ns else ins[a]

        local = [pltpu.make_async_copy(src(a, own), outs[a].at[own], local_sems.at[a]) for a in range(n)]
        sent = [pltpu.make_async_remote_copy(
            src_ref=src(a, 2 * chip[0] + chip[1]), dst_ref=outs[a].at[own],
            send_sem=send_sems.at[a, j], recv_sem=recv_sems.at[a, own], device_id=(*chip, c), device_id_type=MESH)
            for a in range(n) for j, chip in enumerate(chips)]
        return chips, c, local, sent

    def start(ins, outs, sems):
        _, _, local, sent = copies(ins, outs, sems)
        for cp in local + sent:
            cp.start()

    def finish(ins, outs, sems):
        send_sems, recv_sems, _ = sems
        chips, c, local, sent = copies(ins, outs, sems)
        for a in range(n):
            for chip in chips:
                k = 2 * chip[0] + chip[1]
                pltpu.make_async_remote_copy(
                    src_ref=outs[a].at[k], dst_ref=outs[a].at[k], send_sem=send_sems.at[a, 0],
                    recv_sem=recv_sems.at[a, k], device_id=(*chip, c), device_id_type=MESH).wait_recv()
        for cp in sent:
            cp.wait_send()
        for cp in local:
            cp.wait()

    return _Ride(
        list(scatter) + list(whole),
        [jax.ShapeDtypeStruct(s.shape, s.dtype) for s in scatter]
        + [jax.ShapeDtypeStruct((N_CHIP, *s.shape), s.dtype) for s in whole],
        [pltpu.SemaphoreType.DMA((n, 3)), pltpu.SemaphoreType.DMA((n, N_CHIP)), pltpu.SemaphoreType.DMA((n,))],
        start, finish)


def _chips_rest_ride(pair, red):
    table = _side_pieces()
    nc = len(CHUNK_ORDER)

    def each(ins, outs, sems, sending, landing):
        send_sems, recv_sems, _ = sems
        pair_ref, red_ref = ins[0], outs[0]
        x, y, c = _place()
        own = 2 * x + y
        for core in (0, 1):
            mine = table[core][0]
            rest = [s for s in range(EARLY_STEPS, nc) if mine[s] is not None]

            @pl.when(c == core)
            def _(mine=mine, rest=rest, core=core):
                for s in rest:
                    _, rows, at = mine[s]
                    k, cp = _to_chip(pair_ref.at[at // SHARD_IN, pl.ds(at % SHARD_IN, rows)], red_ref, mine[s], s,
                                     send_sems, recv_sems, own, core)
                    pl.when(own != k)(lambda cp=cp: sending(cp))
                if landing is not None:
                    for k in range(N_CHIP):
                        @pl.when(own == k)
                        def _(k=k):
                            for s in rest:
                                _, rows, at = mine[s]
                                if at // SHARD_IN == k:
                                    for chip in range(N_CHIP):
                                        if chip != k:
                                            landing(_from_chip(pair_ref.at[k, pl.ds(at % SHARD_IN, rows)], red_ref, mine[s], s,
                                                               send_sems, recv_sems, chip, (x, y, c)))

    def local(ins, outs, sems):
        x, y, _ = _place()
        return pltpu.make_async_copy(ins[0].at[2 * x + y], outs[0].at[2 * x + y], sems[2])

    def start(ins, outs, sems):
        local(ins, outs, sems).start()
        each(ins, outs, sems, lambda cp: cp.start(), None)

    def finish(ins, outs, sems):
        each(ins, outs, sems, lambda cp: cp.wait_send(), lambda cp: cp.wait_recv())
        local(ins, outs, sems).wait()

    return _Ride([pair, red], [jax.ShapeDtypeStruct(red.shape, red.dtype)],
                 [pltpu.SemaphoreType.DMA((nc,)), pltpu.SemaphoreType.DMA((N_CHIP, nc)), pltpu.SemaphoreType.DMA],
                 start, finish, aliases={1: 0})


def _allreduce_small(pack, name):
    shape = pack.shape

    def body(x_ref, o_ref, sib_ref, chip_ref, send_sems, recv_sems):
        x, y, c = _place()
        own = 2 * x + y
        chips = [(1 - x, y), (x, 1 - y), (1 - x, 1 - y)]
        to_sibling = pltpu.make_async_remote_copy(
            src_ref=x_ref, dst_ref=sib_ref, send_sem=send_sems.at[0], recv_sem=recv_sems.at[0],
            device_id=(x, y, 1 - c), device_id_type=MESH)
        to_sibling.start()
        to_sibling.wait()
        chip_ref[own] = x_ref[...] + sib_ref[...]
        sent = [pltpu.make_async_remote_copy(
            src_ref=chip_ref.at[own], dst_ref=chip_ref.at[own], send_sem=send_sems.at[1 + j],
            recv_sem=recv_sems.at[1 + own], device_id=(*chip, c), device_id_type=MESH) for j, chip in enumerate(chips)]
        for cp in sent:
            cp.start()
        for chip in chips:
            k = 2 * chip[0] + chip[1]
            pltpu.make_async_remote_copy(
                src_ref=chip_ref.at[k], dst_ref=chip_ref.at[k], send_sem=send_sems.at[1],
                recv_sem=recv_sems.at[1 + k], device_id=(*chip, c), device_id_type=MESH).wait_recv()
        for cp in sent:
            cp.wait_send()
        o_ref[...] = (chip_ref[0] + chip_ref[1]) + (chip_ref[2] + chip_ref[3])

    return pl.pallas_call(
        body, name=name, out_shape=jax.ShapeDtypeStruct(shape, F32),
        in_specs=[pl.BlockSpec(memory_space=pltpu.VMEM)], out_specs=pl.BlockSpec(memory_space=pltpu.VMEM),
        scratch_shapes=[pltpu.VMEM(shape, F32), pltpu.VMEM((N_CHIP, *shape), F32),
                        pltpu.SemaphoreType.DMA((4,)), pltpu.SemaphoreType.DMA((1 + N_CHIP,))],
    )(pack)


def _adamw(g, w, m, v):
    m = ADAM_B1 * m + (1.0 - ADAM_B1) * g
    v = ADAM_B2 * v + (1.0 - ADAM_B2) * (g * g)
    m_hat = m / (1.0 - ADAM_B1 ** ADAM_STEP)
    v_hat = v / (1.0 - ADAM_B2 ** ADAM_STEP)
    delta = -ADAM_LR * (m_hat / (jnp.sqrt(v_hat) + ADAM_EPS) + ADAM_WD * w)
    return delta, m, v


def _adam_parts(items, name, tr=None):
    ni = len(items)
    npart, r, c = items[0][0].shape
    tr = r if tr is None else min(tr, r)

    def body(*refs):
        for a in range(ni):
            p_ref, w_ref, m_ref, v_ref = refs[4 * a:4 * a + 4]
            g_ref, d_ref, nm_ref, nv_ref = refs[4 * (ni + a):4 * (ni + a) + 4]
            g = p_ref[0].astype(F32)
            for k in range(1, npart):
                g = g + p_ref[k].astype(F32)
            g_ref[...] = g
            d_ref[...], nm_ref[...], nv_ref[...] = _adamw(g, w_ref[...], m_ref[...], v_ref[...])

    tile = pl.BlockSpec((tr, c), lambda i: (i, 0))
    outs = pl.pallas_call(
        body, name=name, grid=(r // tr,),
        in_specs=[pl.BlockSpec((npart, tr, c), lambda i: (0, i, 0)), tile, tile, tile] * ni,
        out_specs=[tile] * (4 * ni), out_shape=[jax.ShapeDtypeStruct((r, c), F32)] * (4 * ni),
        compiler_params=_params("parallel"))(*[arr for item in items for arr in item])
    return [outs[4 * a:4 * a + 4] for a in range(ni)]


def _block_diag(w):
    w4 = w.reshape(N_GROUPS, 4, RNN_BLOCK_W, RNN_BLOCK_W)
    eye = jnp.eye(4, dtype=w.dtype)
    return jnp.einsum("gbij,bc->gbicj", w4, eye).reshape(N_GROUPS, GROUP_W, GROUP_W).astype(BF16)


SMALL_ROWS = 16
ROW_PRE_G, ROW_BGATE, ROW_CONV_B, ROW_B_A, ROW_B_X, ROW_LAM, ROW_POST_G, ROW_LOSS, ROW_SINKS, ROW_CONV_W = 0, 1, 3, 4, 5, 6, 7, 8, 9, 10


def _pack_stats(st_pre, st_merge, st_rnn, st_post, st_sink, name):
    d = D_MODEL

    def body(pre_ref, mg_ref, rnn_ref, post_ref, sink_ref, o_ref):
        rnn = rnn_ref[...]
        sinks = jnp.concatenate([sink_ref[0:1, :], jnp.zeros((1, d - LANE), F32)], axis=1)
        o_ref[0:8, :] = _rows8([pre_ref[0:1, :], mg_ref[0:1, :], mg_ref[1:2, :], rnn[0:1], rnn[1:2], rnn[2:3], rnn[3:4],
                                post_ref[0:1, :]], d)
        o_ref[8:16, :] = _rows8([post_ref[1:2, :], sinks, rnn[4:5], rnn[5:6], rnn[6:7], rnn[7:8]], d)

    return pl.pallas_call(body, name=name, out_shape=jax.ShapeDtypeStruct((SMALL_ROWS, d), F32))(
        st_pre, st_merge, st_rnn, st_post, st_sink)


def _adam_small(total, w, m, v, name):
    d = D_MODEL
    n_in = len(w)

    def pack(refs):
        pre, bg, cbias, ba, bx, lam, post, sinks = [r[...] for r in refs]
        top = _rows8([pre, bg[:, 0:d], bg[:, d:2 * d], cbias, ba, bx, lam, post], d)
        return jnp.concatenate([top, _rows8([jnp.zeros((1, d), F32), sinks], d)], axis=0)

    def body(*refs):
        p_ref = refs[0]
        w_refs, m_refs, v_refs = (refs[1 + k * n_in:1 + (k + 1) * n_in] for k in range(3))
        outs = refs[1 + 3 * n_in:]
        g = p_ref[...]
        res = (g,) + _adamw(g, pack(w_refs), pack(m_refs), pack(v_refs))
        for k in range(4):
            outs[k][...] = res[k]
            outs[4 + k][...] = jnp.concatenate([res[k][ROW_BGATE:ROW_BGATE + 1], res[k][ROW_BGATE + 1:ROW_BGATE + 2]], axis=1)

    return pl.pallas_call(
        body, name=name,
        out_shape=[jax.ShapeDtypeStruct((SMALL_ROWS, d), F32)] * 4 + [jax.ShapeDtypeStruct((1, 2 * d), F32)] * 4,
    )(total, *w, *m, *v)


def kernel(x, pre_norm_g, w_in, b_gate, conv_w, conv_b, w_rg_a, b_rg_a, w_rg_x, b_rg_x, lru_lambda, attn_sinks, w_rnn_out, w_attn_out, w_out, post_norm_g, loss_target, m_pre_norm_g, m_w_in, m_b_gate, m_conv_w, m_conv_b, m_w_rg_a, m_b_rg_a, m_w_rg_x, m_b_rg_x, m_lru_lambda, m_attn_sinks, m_w_rnn_out, m_w_attn_out, m_w_out, m_post_norm_g, v_pre_norm_g, v_w_in, v_b_gate, v_conv_w, v_conv_b, v_w_rg_a, v_b_rg_a, v_w_rg_x, v_b_rg_x, v_lru_lambda, v_attn_sinks, v_w_rnn_out, v_w_attn_out, v_w_out, v_post_norm_g):
    cx, cy, cc = _place()
    dev = 4 * cx + 2 * cy + cc

    w_in_t, m_in_t, v_in_t = (jnp.transpose(a[0]) for a in (w_in, m_w_in, v_w_in))
    wt_shard = w_in_t.astype(BF16)

    def project(xs, pre_g):
        h, rx_rg, qkv, ag_ml, wt_all = _gather_project(xs, pre_g, wt_shard, "gather_project")
        return h, rx_rg, qkv, ag_ml, wt_all.reshape(D_IN, D_MODEL)

    def late_unpack(landed):
        w_rnn_all, w_attn_all, w_out_all, cw_all = landed
        return dict(w_rnn=w_rnn_all.reshape(D_RNN, D_MODEL), w_attn=w_attn_all.reshape(D_MODEL, D_MODEL),
                    w_out=w_out_all.reshape(D_MODEL, D_MODEL), cw=jnp.transpose(cw_all, (1, 0, 2)).reshape(4, D_RNN))

    late_weights = (_gather_ride([w_rnn_out[0].astype(BF16), w_attn_out[0].astype(BF16), w_out[0].astype(BF16), conv_w[0]]),
                    late_unpack)

    heads = jnp.arange(1, N_Q_HEADS + 1, dtype=F32)
    slopes = jnp.exp2(-ALIBI_MAX_BIAS * heads / N_Q_HEADS)
    b_a = b_rg_a.reshape(1, D_RNN)
    b_x = b_rg_x.reshape(1, D_RNN)
    p = dict(
        pre_g=pre_norm_g, post_g=post_norm_g, b_gate=b_gate, cb=conv_b,
        wbd_a=_block_diag(w_rg_a[0]), b_a=b_a, wbd_x=_block_diag(w_rg_x[0]), b_x=b_x, lam=lru_lambda,
        sm=jnp.pad(attn_sinks, ((0, 1), (0, 0))) + jnp.pad(slopes[None, :], ((1, 0), (0, 0))))

    flat = (RNN_BLOCKS * RNN_BLOCK_W, RNN_BLOCK_W)

    def reduce_out(out_pairs, g_rg_a, g_rg_x):
        return _join_rides(_chips_ride(out_pairs, []), _gather_ride([g_rg_a.reshape(flat), g_rg_x.reshape(flat)]))

    reduce_in = _chips_rest_ride

    g = _local_grads(x[0], loss_target[0], p, project, late_weights, reduce_out, reduce_in)
    small = _allreduce_small(
        _pack_stats(g["st_pre"], g["st_merge"], g["st_rnn"], g["st_post"], g["st_sink"], "pack_stats"), "allreduce_small")

    out = {}
    red = g["red_out"]
    w_in_out, = _adam_parts([(g["red_in"][0], w_in_t, m_in_t, v_in_t)], "adam_w_in", tr=SHARD_IN // 2)
    out["w_in"] = [jnp.transpose(o) for o in w_in_out]
    out["w_rnn_out"], out["w_attn_out"], out["w_out"] = _adam_parts(
        [(red[0], w_rnn_out[0], m_w_rnn_out[0], v_w_rnn_out[0]), (red[1], w_attn_out[0], m_w_attn_out[0], v_w_attn_out[0]),
         (red[2], w_out[0], m_w_out[0], v_w_out[0])], "adam_w_outs")
    out["w_rg_a"], out["w_rg_x"] = _adam_parts(
        [(red[3], w_rg_a.reshape(flat), m_w_rg_a.reshape(flat), v_w_rg_a.reshape(flat)),
         (red[4], w_rg_x.reshape(flat), m_w_rg_x.reshape(flat), v_w_rg_x.reshape(flat))], "adam_w_rg", tr=256)

    def rows(pre, bg, cbias, ba, bx, lam, post, sinks):
        return (pre, bg, cbias, ba.reshape(1, D_RNN), bx.reshape(1, D_RNN), lam, post,
                jnp.pad(sinks, ((0, 0), (0, D_MODEL - N_Q_HEADS))))

    small_out = _adam_small(
        small,
        rows(pre_norm_g, b_gate, conv_b, b_rg_a, b_rg_x, lru_lambda, post_norm_g, attn_sinks),
        rows(m_pre_norm_g, m_b_gate, m_conv_b, m_b_rg_a, m_b_rg_x, m_lru_lambda, m_post_norm_g, m_attn_sinks),
        rows(v_pre_norm_g, v_b_gate, v_conv_b, v_b_rg_a, v_b_rg_x, v_lru_lambda, v_post_norm_g, v_attn_sinks),
        "adam_small")
    packed, bgate_out = small_out[:4], small_out[4:]
    g_cw = lax.dynamic_slice(packed[0][ROW_CONV_W:ROW_CONV_W + 4], (0, dev * SHARD_OUT), (4, SHARD_OUT))
    out["conv_w"], = _adam_parts([(g_cw[None], conv_w[0], m_conv_w[0], v_conv_w[0])], "adam_conv_w")

    def unpack(kind, name):
        if name == "b_gate":
            return bgate_out[kind]
        row = dict(pre_norm_g=ROW_PRE_G, conv_b=ROW_CONV_B, b_rg_a=ROW_B_A, b_rg_x=ROW_B_X, lru_lambda=ROW_LAM,
                   post_norm_g=ROW_POST_G, attn_sinks=ROW_SINKS)[name]
        r = packed[kind][row:row + 1]
        if name == "attn_sinks":
            return r[:, 0:N_Q_HEADS]
        if name in ("b_rg_a", "b_rg_x"):
            return r.reshape(1, RNN_BLOCKS, RNN_BLOCK_W)
        return r

    shapes = dict(w_in=(1, D_MODEL, SHARD_IN), w_rnn_out=(1, SHARD_OUT, D_MODEL), w_attn_out=(1, SHARD_OUT, D_MODEL),
                  w_out=(1, SHARD_OUT, D_MODEL), w_rg_a=(1, RNN_BLOCKS, RNN_BLOCK_W, RNN_BLOCK_W),
                  w_rg_x=(1, RNN_BLOCKS, RNN_BLOCK_W, RNN_BLOCK_W), conv_w=(1, 4, SHARD_OUT))
    weights = ["pre_norm_g", "w_in", "b_gate", "conv_w", "conv_b", "w_rg_a", "b_rg_a", "w_rg_x", "b_rg_x",
               "lru_lambda", "attn_sinks", "w_rnn_out", "w_attn_out", "w_out", "post_norm_g"]
    results = []
    for kind in range(4):
        for name in weights:
            if name in out:
                results.append(out[name][kind].reshape(shapes[name]))
            else:
                results.append(unpack(kind, name))
    loss = 0.5 / D_MODEL * jnp.sum(packed[0][ROW_LOSS])
    return (loss, g["grad_x"][None], *results)
```

```python
import functools

import jax
import jax.numpy as jnp
from jax import lax
from jax.experimental import pallas as pl
from jax.experimental.pallas import tpu as pltpu

F32 = jnp.float32
BF16 = jnp.bfloat16

D_MODEL = 1024
D_RNN = 1024
RNN_BLOCKS = 16
RNN_BLOCK_W = 64
LRU_C = 8.0
N_Q_HEADS = 16
HEAD_DIM = 64
D_KV = 256
BLOCK = 128
ALIBI_MAX_BIAS = 8.0
EPS = 1e-6
D_IN = 6656
N_DEV = 8
N_CHIP = 4
SHARD_IN = D_IN // N_DEV
SHARD_OUT = D_MODEL // N_DEV
ATTN_SCALE = HEAD_DIM ** -0.5
MASKED = -1e30

ADAM_LR = 0.001
ADAM_B1 = 0.9
ADAM_B2 = 0.999
ADAM_EPS = 1e-08
ADAM_WD = 0.01
ADAM_STEP = 10

VMEM_LIMIT_BYTES = 52 * 1024 * 1024
LANE = 128
GROUP_W = 256
N_GROUPS = D_RNN // GROUP_W
SEG_CHUNK = 512

NT_DIMS = (((1,), (1,)), ((), ()))
TN_DIMS = (((0,), (0,)), ((), ()))
MESH = pl.DeviceIdType.MESH
ANY = pl.BlockSpec(memory_space=pl.ANY)


def _params(*semantics):
    return pltpu.CompilerParams(dimension_semantics=semantics, vmem_limit_bytes=VMEM_LIMIT_BYTES)


def _sigmoid(x):
    return 0.5 * jnp.tanh(0.5 * x) + 0.5


def _log1p(e):
    u = 1.0 + e
    den = jnp.where(u == 1.0, 1.0, u - 1.0)
    return jnp.where(u == 1.0, e, jnp.log(u) * (e / den))


def _softplus(z):
    return jnp.maximum(z, 0.0) + _log1p(jnp.exp(-jnp.abs(z)))


def _rows8(rows, width):
    idx = lax.broadcasted_iota(jnp.int32, (8, width), 0)
    out = jnp.zeros((8, width), F32)
    for r, v in enumerate(rows):
        out = jnp.where(idx == r, v, out)
    return out


class _Ride:
    def __init__(self, arrays, out_shapes, scratch_shapes, start, finish, middle=None, aliases=None):
        self.arrays, self.out_shapes, self.scratch_shapes = list(arrays), list(out_shapes), list(scratch_shapes)
        self.start, self.finish, self.middle = start, finish, middle
        self.aliases = dict(aliases or {})


class _Hosted:
    def __init__(self, ride, n_in, n_out, n_scratch=0, aliasing=False):
        self.ride = ride
        assert aliasing or not (ride and ride.aliases), "this host does not alias"
        self.aliases = {n_in + i: n_out + o for i, o in ride.aliases.items()} if ride else {}
        self.sizes = (n_in, len(ride.arrays) if ride else 0, n_out, len(ride.out_shapes) if ride else 0, n_scratch)
        self.arrays = ride.arrays if ride else []
        self.in_specs = [ANY] * len(self.arrays)
        self.out_shapes = ride.out_shapes if ride else []
        self.out_specs = [ANY] * len(self.out_shapes)
        self.scratch_shapes = ride.scratch_shapes if ride else []

    def split(self, refs):
        n_in, r_in, n_out, r_out, n_scr = self.sizes
        cuts = [0, n_in, n_in + r_in, n_in + r_in + n_out, n_in + r_in + n_out + r_out, n_in + r_in + n_out + r_out + n_scr]
        host_in, ride_in, host_out, ride_out, host_scr = (refs[cuts[k]:cuts[k + 1]] for k in range(5))
        ride_scr = refs[cuts[5]:]

        def start(when):
            if self.ride is not None:
                pl.when(when)(lambda: self.ride.start(ride_in, ride_out, ride_scr))

        def finish(when):
            if self.ride is not None:
                pl.when(when)(lambda: self.ride.finish(ride_in, ride_out, ride_scr))

        def middle(when):
            if self.ride is not None and self.ride.middle is not None:
                pl.when(when)(lambda: self.ride.middle(ride_in, ride_out, ride_scr))

        start.middle = middle
        return tuple(host_in) + tuple(host_out) + tuple(host_scr), start, finish

    def results(self, outs, n_out):
        outs = list(outs) if isinstance(outs, (list, tuple)) else [outs]
        return outs[:n_out], outs[n_out:]


def _join_rides(a, b):
    na, nb_ = len(a.arrays), len(b.arrays)
    oa = len(a.out_shapes)
    sa = len(a.scratch_shapes)

    def both(fa, fb):
        def run(ins, outs, sems):
            if fa is not None:
                fa(ins[:na], outs[:oa], sems[:sa])
            if fb is not None:
                fb(ins[na:na + nb_], outs[oa:], sems[sa:])
        return run

    middle = both(a.middle, b.middle) if (a.middle or b.middle) else None
    return _Ride(a.arrays + b.arrays, a.out_shapes + b.out_shapes, a.scratch_shapes + b.scratch_shapes,
                 both(a.start, b.start), both(a.finish, b.finish), middle)


def _load_resident(w_hbm, w_vmem, sems, first):
    def piece(c):
        rows = pl.ds(c * SEG_CHUNK, SEG_CHUNK)
        return pltpu.make_async_copy(w_hbm.at[rows], w_vmem.at[rows], sems.at[c])

    @pl.when(first)
    def _():
        for c in range(w_vmem.shape[0] // SEG_CHUNK):
            piece(c).start()

    def ready(c):
        @pl.when(first)
        def _():
            piece(c).wait()

    return ready


CHIP_ROWS = 2 * SHARD_IN
PROJ_WIDTHS = (2 * D_RNN, D_MODEL + 2 * D_KV, 3 * D_MODEL)
PROJ_DTYPES = (F32, BF16, BF16)


def _chip_pieces():
    starts = [0, PROJ_WIDTHS[0], PROJ_WIDTHS[0] + PROJ_WIDTHS[1], D_IN]
    pieces = []
    for k in range(N_CHIP):
        lo, hi = k * CHIP_ROWS, (k + 1) * CHIP_ROWS
        cur = []
        for a in range(len(PROJ_WIDTHS)):
            s0, s1 = max(lo, starts[a]), min(hi, starts[a + 1])
            if s0 < s1:
                cur.append((a, s0 - starts[a], s1 - s0, s0 - lo))
        pieces.append(cur)
    return pieces


def _gather_project(x, g, wt_shard, name, tm=1024):
    t, k = x.shape
    tm = min(tm, t)
    nt = t // tm
    pieces = _chip_pieces()

    def body(x_ref, g_ref, shard_ref, h_out, rx_ref, qkv_ref, ag_ref, wt_all,
             w_c, stage, o32, o16, h_all, send_sems, recv_sems, local_sem, stage_sems, out_sems, h_sems):
        s, ti = pl.program_id(0), pl.program_id(1)
        px, py, pc = _place()
        me, sibling = (px, py, pc), (px, py, 1 - pc)
        chips = [(px, py), (1 - px, py), (px, 1 - py), (1 - px, 1 - py)]
        outs = (rx_ref, qkv_ref, ag_ref)

        def slot(dev):
            return wt_all.at[4 * dev[0] + 2 * dev[1] + dev[2]]

        def copy(kk, block, to, src=None):
            return pltpu.make_async_remote_copy(
                src_ref=slot(block) if src is None else src, dst_ref=slot(block),
                send_sem=send_sems.at[kk], recv_sem=recv_sems.at[kk], device_id=to, device_id_type=MESH)

        mine = pltpu.make_async_copy(shard_ref, slot(me), local_sem)
        first = [copy(0, me, sibling, src=shard_ref)] + [copy(1 + j, me, (*chips[1 + j], pc), src=shard_ref) for j in range(3)]
        passed = [copy(4 + j, (*chips[1 + j], pc), sibling) for j in range(3)]

        @pl.when((s == 0) & (ti == 0))
        def _():
            mine.start()
            for cp in first[:3]:
                cp.start()

        def pass_on(step):
            copy(step, (*chips[step], pc), me).wait_recv()
            passed[step - 1].start()

        @pl.when((s == 2) & (ti == nt - 1))
        def _():
            pass_on(3)

        for step in range(N_CHIP):
            @pl.when((s == step) & (ti == 0))
            def _(step=step):
                chip = chips[step]
                if step == 0:
                    mine.wait()
                    copy(0, sibling, me).wait_recv()
                else:
                    if step == 1:
                        pass_on(1)
                        first[3].start()
                        pass_on(2)
                    copy(3 + step, (*chip, 1 - pc), me).wait_recv()
                loads = [pltpu.make_async_copy(slot((*chip, core)), stage.at[pl.ds(core * SHARD_IN, SHARD_IN)], stage_sems.at[core])
                         for core in (0, 1)]
                for cp in loads:
                    cp.start()
                for cp in loads:
                    cp.wait()
                w_c[...] = stage[...].T

        rows = pl.ds(pl.multiple_of(ti * tm, tm), tm)

        @pl.when(s == 0)
        def _():
            xv = x_ref[...]
            h_all[rows, :] = (xv * lax.rsqrt(jnp.mean(xv * xv, axis=-1, keepdims=True) + EPS) * g_ref[...]).astype(BF16)

        res = jnp.dot(h_all[rows, :], w_c[...], preferred_element_type=F32)

        n = s * nt + ti
        buf = lax.rem(n, 2)
        chip_idx = [2 * cx + cy for cx, cy in chips]

        def chip_at(step):
            return jnp.where(step == 0, chip_idx[0], jnp.where(step == 1, chip_idx[1], jnp.where(step == 2, chip_idx[2], chip_idx[3])))

        def h_write(b, tile):
            return pltpu.make_async_copy(h_all.at[pl.ds(tile * tm, tm)], h_out.at[pl.ds(tile * tm, tm)], h_sems.at[b])

        def writes(kchip, b, tile):
            cps = []
            for idx, (a, col, w, src) in enumerate(pieces[kchip]):
                staged = (o16 if PROJ_DTYPES[a] == BF16 else o32).at[b, :, pl.ds(src, w)]
                cps.append(pltpu.make_async_copy(staged, outs[a].at[pl.ds(tile * tm, tm), pl.ds(col, w)], out_sems.at[b, idx]))
            return cps

        o32[buf] = res
        o16[buf] = res.astype(BF16)
        kcur = chip_at(s)

        @pl.when(n > 0)
        def _():
            kprev = chip_at(lax.div(n - 1, nt))
            for kchip in range(N_CHIP):
                @pl.when(kprev == kchip)
                def _(kchip=kchip):
                    for cp in writes(kchip, 1 - buf, lax.rem(n - 1, nt)):
                        cp.wait()

            @pl.when(n <= nt)
            def _():
                h_write(1 - buf, n - 1).wait()

        @pl.when(s == 0)
        def _():
            h_write(buf, ti).start()

        for kchip in range(N_CHIP):
            @pl.when(kcur == kchip)
            def _(kchip=kchip):
                for cp in writes(kchip, buf, ti):
                    cp.start()

        @pl.when(n == N_CHIP * nt - 1)
        def _():
            for kchip in range(N_CHIP):
                @pl.when(kcur == kchip)
                def _(kchip=kchip):
                    for cp in writes(kchip, buf, ti):
                        cp.wait()
            for cp in first + passed:
                cp.wait_send()

    tile = pl.BlockSpec((tm, k), lambda s, ti: (jnp.where(s == 0, ti, nt - 1), 0))
    return pl.pallas_call(
        body, name=name, grid=(N_CHIP, nt),
        in_specs=[tile, pl.BlockSpec((1, k), lambda s, ti: (0, 0)), ANY],
        out_specs=[ANY, ANY, ANY, ANY, ANY],
        out_shape=[jax.ShapeDtypeStruct((t, k), BF16)]
        + [jax.ShapeDtypeStruct((t, w), dt) for w, dt in zip(PROJ_WIDTHS, PROJ_DTYPES)]
        + [jax.ShapeDtypeStruct((N_DEV, SHARD_IN, k), BF16)],
        scratch_shapes=[pltpu.VMEM((k, CHIP_ROWS), BF16), pltpu.VMEM((CHIP_ROWS, k), BF16),
                        pltpu.VMEM((2, tm, CHIP_ROWS), F32), pltpu.VMEM((2, tm, CHIP_ROWS), BF16), pltpu.VMEM((t, k), BF16),
                        pltpu.SemaphoreType.DMA((7,)), pltpu.SemaphoreType.DMA((7,)), pltpu.SemaphoreType.DMA,
                        pltpu.SemaphoreType.DMA((2,)), pltpu.SemaphoreType.DMA((2, 2)), pltpu.SemaphoreType.DMA((2,))],
        compiler_params=_params("arbitrary", "arbitrary"))(x, g, wt_shard)


def _mm_tn(a, b, name, tm=512, tk=4096):
    ktok, m = a.shape
    n = b.shape[1]
    tk = min(tk, ktok)

    def body(a_ref, b_ref, o_ref):
        @pl.when(pl.program_id(1) == 0)
        def _():
            o_ref[...] = jnp.zeros_like(o_ref)

        o_ref[...] += lax.dot_general(a_ref[...], b_ref[...], TN_DIMS, preferred_element_type=F32)

    return pl.pallas_call(
        body, name=name, grid=(m // tm, ktok // tk),
        in_specs=[pl.BlockSpec((tk, tm), lambda i, kk: (kk, i)), pl.BlockSpec((tk, n), lambda i, kk: (kk, 0))],
        out_specs=pl.BlockSpec((tm, n), lambda i, kk: (i, 0)),
        out_shape=jax.ShapeDtypeStruct((m, n), F32),
        compiler_params=_params("parallel", "arbitrary"))(a, b)


def _mm_tn_pairs(prods, name):
    n_prod = len(prods)
    ktok, m = prods[0][0].shape
    n = prods[0][1].shape[1]
    half, blk = m // 2, m // N_DEV
    ns = 2 * n_prod
    kept = N_CHIP * blk

    def side_pieces(s, side):
        i, hf = divmod(s, 2)
        first = hf * N_DEV // 2
        return [((d - first) * blk, i * kept + (d // 2) * blk) for d in range(first, first + N_DEV // 2) if d % 2 == side]

    def body(*refs):
        a_hbm, b_hbm, pair_refs = refs[:n_prod], refs[n_prod:2 * n_prod], refs[2 * n_prod:3 * n_prod]
        a_buf, b_buf, res_buf, recv_all, pair_all, a_sems, b_sems, send_sems, recv_sems, out_sems = refs[3 * n_prod:]
        step = pl.program_id(0)
        px, py, pc = _place()

        def fetch_a(s):
            return pltpu.make_async_copy(a_hbm[s // 2].at[:, pl.ds((s % 2) * half, half)], a_buf.at[s % 2], a_sems.at[s % 2])

        def fetch_b(i):
            return pltpu.make_async_copy(b_hbm[i], b_buf.at[i % 2], b_sems.at[i % 2])

        @pl.when(step == 0)
        def _():
            fetch_a(0).start()
            fetch_b(0).start()

        for s in range(ns):
            @pl.when(step == s)
            def _(s=s):
                if s + 1 < ns:
                    fetch_a(s + 1).start()
                if s % 2 == 0 and s // 2 + 1 < n_prod:
                    fetch_b(s // 2 + 1).start()
                fetch_a(s).wait()
                if s % 2 == 0:
                    fetch_b(s // 2).wait()

        res_buf[step % 2] = lax.dot_general(a_buf[step % 2], b_buf[(step // 2) % 2], TN_DIMS, preferred_element_type=F32)

        def crossing(s, j, piece):
            off, at = piece
            return pltpu.make_async_remote_copy(
                src_ref=res_buf.at[s % 2, pl.ds(off, blk)], dst_ref=recv_all.at[pl.ds(at, blk)],
                send_sem=send_sems.at[s, j], recv_sem=recv_sems.at[s, j], device_id=(px, py, 1 - pc), device_id_type=MESH)

        def write(s, core):
            at = side_pieces(s, core)[0][1]
            return pltpu.make_async_copy(pair_all.at[pl.ds(at, 2 * blk)], pair_refs[s // 2].at[pl.ds(at % kept, 2 * blk)], out_sems.at[s])

        for core in (0, 1):
            def settle(s, core=core):
                for j, piece in enumerate(side_pieces(s, 1 - core)):
                    crossing(s, j, piece).wait_send()
                for j, (off, at) in enumerate(side_pieces(s, core)):
                    crossing(s, j, (off, at)).wait_recv()
                    pair_all[at:at + blk] = (res_buf[s % 2, off:off + blk] + recv_all[at:at + blk]).astype(BF16)
                write(s, core).start()

            for s in range(ns):
                @pl.when((pc == core) & (step == s))
                def _(s=s, settle=settle, core=core):
                    for j, piece in enumerate(side_pieces(s, 1 - core)):
                        crossing(s, j, piece).start()
                    if s > 0:
                        settle(s - 1)
                    if s == ns - 1:
                        settle(s)
                        for t in range(ns):
                            write(t, core).wait()

    pairs = pl.pallas_call(
        body, name=name, grid=(ns,),
        in_specs=[ANY] * (2 * n_prod), out_specs=[ANY] * n_prod,
        out_shape=[jax.ShapeDtypeStruct((kept, n), BF16)] * n_prod,
        scratch_shapes=[pltpu.VMEM((2, ktok, half), prods[0][0].dtype), pltpu.VMEM((2, ktok, n), prods[0][1].dtype),
                        pltpu.VMEM((2, half, n), F32), pltpu.VMEM((n_prod * kept, n), F32), pltpu.VMEM((n_prod * kept, n), BF16),
                        pltpu.SemaphoreType.DMA((2,)), pltpu.SemaphoreType.DMA((2,)),
                        pltpu.SemaphoreType.DMA((ns, 2)), pltpu.SemaphoreType.DMA((ns, 2)), pltpu.SemaphoreType.DMA((ns,))],
        compiler_params=_params("arbitrary"))(*[a for a, _ in prods], *[b for _, b in prods])
    return [pair.reshape(N_CHIP, blk, n) for pair in pairs]


def _segment_chunks(segs):
    bounds = [0]
    for s in segs:
        bounds.append(bounds[-1] + s.shape[1] // SEG_CHUNK)
    return bounds


def _input_grad(segs, wt, x, g, dy, name, tm=512, ride=None):
    m = segs[0].shape[0]
    rows, n = wt.shape
    tm = min(tm, m)
    bounds = _segment_chunks(segs)
    n_seg = len(segs)
    ni = m // tm
    host = _Hosted(ride, n_seg + 4, 2, 2, aliasing=True)

    def body(*refs):
        host_refs, start, finish = host.split(refs)
        a_refs = host_refs[:n_seg]
        wt_hbm, x_ref, g_ref, dy_ref, gx_ref, st_ref, wt_vmem, sems = host_refs[n_seg:]
        i = pl.program_id(0)
        start(i == 0)

        @pl.when(i == 0)
        def _():
            st_ref[...] = jnp.zeros_like(st_ref)

        ready = _load_resident(wt_hbm, wt_vmem, sems, i == 0)
        dh = None
        for s in range(n_seg):
            for c in range(bounds[s], bounds[s + 1]):
                ready(c)
            part = jnp.dot(a_refs[s][...], wt_vmem[bounds[s] * SEG_CHUNK:bounds[s + 1] * SEG_CHUNK, :], preferred_element_type=F32)
            dh = part if dh is None else dh + part
        xv = x_ref[...]
        r = lax.rsqrt(jnp.mean(xv * xv, axis=-1, keepdims=True) + EPS)
        xn = xv * r
        dxn = dh * g_ref[...]
        gx_ref[...] = dy_ref[...] + r * (dxn - xn * jnp.mean(dxn * xn, axis=-1, keepdims=True))
        st_ref[...] += _rows8([jnp.sum(dh * xn, axis=0, keepdims=True)], n)
        finish(i == ni - 1)

    tile = pl.BlockSpec((tm, n), lambda i: (i, 0))
    outs = pl.pallas_call(
        body, name=name, grid=(ni,),
        in_specs=[pl.BlockSpec((tm, sg.shape[1]), lambda i: (i, 0)) for sg in segs]
        + [ANY, tile, pl.BlockSpec((1, n), lambda i: (0, 0)), tile] + host.in_specs,
        out_specs=[tile, pl.BlockSpec((8, n), lambda i: (0, 0))] + host.out_specs,
        out_shape=[jax.ShapeDtypeStruct((m, n), F32), jax.ShapeDtypeStruct((8, n), F32)] + host.out_shapes,
        scratch_shapes=[pltpu.VMEM((rows, n), wt.dtype), pltpu.SemaphoreType.DMA((rows // SEG_CHUNK,))] + host.scratch_shapes,
        input_output_aliases=host.aliases,
        compiler_params=_params("arbitrary"))(*segs, wt, x, g, dy, *host.arrays)
    res, landed = host.results(outs, 2)
    return (*res, landed) if ride else tuple(res)


def _mm_tn_seg(segs, b, name):
    ktok = segs[0].shape[0]
    n = b.shape[1]
    bounds = _segment_chunks(segs)
    n_seg = len(segs)
    nc = bounds[-1]
    seg_of = [s for s in range(n_seg) for _ in range(bounds[s], bounds[s + 1])]

    def body(*refs):
        a_hbm, b_hbm, o_ref = refs[:n_seg], refs[n_seg], refs[n_seg + 1]
        a_buf, b_vmem, a_sems, b_sem = refs[n_seg + 2:]
        c = pl.program_id(0)

        def fetch(cc):
            s = seg_of[cc]
            cols = pl.ds((cc - bounds[s]) * SEG_CHUNK, SEG_CHUNK)
            return pltpu.make_async_copy(a_hbm[s].at[:, cols], a_buf.at[cc % 2], a_sems.at[cc % 2])

        @pl.when(c == 0)
        def _():
            whole = pltpu.make_async_copy(b_hbm, b_vmem, b_sem)
            whole.start()
            fetch(0).start()
            whole.wait()

        for cc in range(nc):
            @pl.when(c == cc)
            def _(cc=cc):
                if cc + 1 < nc:
                    fetch(cc + 1).start()
                fetch(cc).wait()

        o_ref[...] = lax.dot_general(a_buf[c % 2], b_vmem[...], TN_DIMS, preferred_element_type=F32)

    return pl.pallas_call(
        body, name=name, grid=(nc,),
        in_specs=[ANY] * (n_seg + 1), out_specs=pl.BlockSpec((SEG_CHUNK, n), lambda c: (c, 0)),
        out_shape=jax.ShapeDtypeStruct((nc * SEG_CHUNK, n), F32),
        scratch_shapes=[pltpu.VMEM((2, ktok, SEG_CHUNK), segs[0].dtype), pltpu.VMEM((ktok, n), b.dtype),
                        pltpu.SemaphoreType.DMA((2,)), pltpu.SemaphoreType.DMA],
        compiler_params=_params("arbitrary"))(*segs, b)


CHUNK_ORDER = (0, 4, 7, 10, 1, 5, 8, 11, 2, 6, 9, 12, 3)
EARLY_STEPS = 13


def _side_pieces():
    table = []
    for core in (0, 1):
        sides = ([None] * len(CHUNK_ORDER), [None] * len(CHUNK_ORDER))
        for s, cc in enumerate(CHUNK_ORDER):
            g0 = cc * SEG_CHUNK
            for d in range(N_DEV):
                lo, hi = max(g0, d * SHARD_IN), min(g0 + SEG_CHUNK, (d + 1) * SHARD_IN)
                if lo < hi:
                    side = sides[0 if d % 2 == core else 1]
                    assert side[s] is None
                    side[s] = (lo - g0, hi - lo, (d // 2) * SHARD_IN + lo - d * SHARD_IN)
        table.append(sides)
    return table


def _to_chip(src, red_ref, piece, s, send_sems, recv_sems, own, core):
    _, rows, at = piece
    k, r0 = divmod(at, SHARD_IN)
    return k, pltpu.make_async_remote_copy(
        src_ref=src, dst_ref=red_ref.at[own, pl.ds(r0, rows)], send_sem=send_sems.at[s], recv_sem=recv_sems.at[own, s],
        device_id=(k // 2, k % 2, core), device_id_type=MESH)


def _from_chip(src, red_ref, piece, s, send_sems, recv_sems, chip, me):
    _, rows, at = piece
    return pltpu.make_async_remote_copy(
        src_ref=src, dst_ref=red_ref.at[chip, pl.ds(at % SHARD_IN, rows)], send_sem=send_sems.at[s], recv_sem=recv_sems.at[chip, s],
        device_id=me, device_id_type=MESH)


def _mm_tn_seg_pair(segs, b, name):
    ktok = segs[0].shape[0]
    n = b.shape[1]
    bounds = _segment_chunks(segs)
    n_seg = len(segs)
    nc = bounds[-1]
    assert nc == len(CHUNK_ORDER)
    seg_of = [s for s in range(n_seg) for _ in range(bounds[s], bounds[s + 1])]
    table = _side_pieces()

    def body(*refs):
        a_hbm, b_hbm, pair_ref, red_ref = refs[:n_seg], refs[n_seg], refs[n_seg + 1], refs[n_seg + 2]
        (a_buf, b_vmem, res_buf, recv_all, pair_all, a_sems, b_sem, send_sems, recv_sems, out_sems,
         chip_send, chip_recv, local_sem) = refs[n_seg + 3:]
        step = pl.program_id(0)
        px, py, pc = _place()
        own = 2 * px + py

        def fetch(s):
            sg = seg_of[CHUNK_ORDER[s]]
            cols = pl.ds((CHUNK_ORDER[s] - bounds[sg]) * SEG_CHUNK, SEG_CHUNK)
            return pltpu.make_async_copy(a_hbm[sg].at[:, cols], a_buf.at[s % 2], a_sems.at[s % 2])

        @pl.when(step == 0)
        def _():
            whole = pltpu.make_async_copy(b_hbm, b_vmem, b_sem)
            whole.start()
            fetch(0).start()
            whole.wait()

        for s in range(nc):
            @pl.when(step == s)
            def _(s=s):
                if s + 1 < nc:
                    fetch(s + 1).start()
                fetch(s).wait()

        res_buf[step % 2] = lax.dot_general(a_buf[step % 2], b_vmem[...], TN_DIMS, preferred_element_type=F32)

        def crossing(s, piece):
            off, rows, at = piece
            return pltpu.make_async_remote_copy(
                src_ref=res_buf.at[s % 2, pl.ds(off, rows)], dst_ref=recv_all.at[pl.ds(at, rows)],
                send_sem=send_sems.at[s], recv_sem=recv_sems.at[s], device_id=(px, py, 1 - pc), device_id_type=MESH)

        def write(s, piece):
            _, rows, at = piece
            return pltpu.make_async_copy(pair_all.at[pl.ds(at, rows)], pair_ref.at[pl.ds(at, rows)], out_sems.at[s])

        def to_chip(s, piece, core):
            return _to_chip(pair_all.at[pl.ds(piece[2], piece[1])], red_ref, piece, s, chip_send, chip_recv, own, core)

        for core in (0, 1):
            mine, theirs = table[core]

            def settle(s, mine=mine, theirs=theirs, core=core):
                if theirs[s] is not None:
                    crossing(s, theirs[s]).wait_send()
                if mine[s] is not None:
                    off, rows, at = mine[s]
                    crossing(s, mine[s]).wait_recv()
                    pair_all[at:at + rows] = (res_buf[s % 2, off:off + rows] + recv_all[at:at + rows]).astype(BF16)
                    write(s, mine[s]).start()
                    if s < EARLY_STEPS:
                        k, cp = to_chip(s, mine[s], core)
                        pl.when(own != k)(cp.start)

            for s in range(nc):
                @pl.when((pc == core) & (step == s))
                def _(s=s, settle=settle, mine=mine, theirs=theirs, core=core):
                    if theirs[s] is not None:
                        crossing(s, theirs[s]).start()
                    if s > 0:
                        settle(s - 1)
                    if s == nc - 1:
                        settle(s)
                        kept = [t for t in range(nc) if mine[t] is not None]
                        for t in kept:
                            write(t, mine[t]).wait()
                        early = [t for t in kept if t < EARLY_STEPS]
                        for t in early:
                            k, cp = to_chip(t, mine[t], core)
                            pl.when(own != k)(cp.wait_send)
                        for k in range(N_CHIP):
                            @pl.when(own == k)
                            def _(k=k):
                                for t in early:
                                    if mine[t][2] // SHARD_IN == k:
                                        for chip in range(N_CHIP):
                                            if chip != k:
                                                _from_chip(pair_all.at[pl.ds(mine[t][2], mine[t][1])], red_ref, mine[t], t,
                                                           chip_send, chip_recv, chip, (px, py, pc)).wait_recv()
                        own_rows = pl.ds(pl.multiple_of(own * SHARD_IN, SHARD_IN), SHARD_IN)
                        keep = pltpu.make_async_copy(pair_all.at[own_rows], red_ref.at[own], local_sem)
                        keep.start()
                        keep.wait()

    flat = jax.ShapeDtypeStruct((N_CHIP * SHARD_IN, n), BF16)
    pair, red = pl.pallas_call(
        body, name=name, grid=(nc,),
        in_specs=[ANY] * (n_seg + 1), out_specs=[ANY, ANY],
        out_shape=[flat, jax.ShapeDtypeStruct((N_CHIP, SHARD_IN, n), BF16)],
        scratch_shapes=[pltpu.VMEM((2, ktok, SEG_CHUNK), segs[0].dtype), pltpu.VMEM((ktok, n), b.dtype),
                        pltpu.VMEM((2, SEG_CHUNK, n), F32), pltpu.VMEM(flat.shape, F32), pltpu.VMEM(flat.shape, BF16),
                        pltpu.SemaphoreType.DMA((2,)), pltpu.SemaphoreType.DMA,
                        pltpu.SemaphoreType.DMA((nc,)), pltpu.SemaphoreType.DMA((nc,)), pltpu.SemaphoreType.DMA((nc,)),
                        pltpu.SemaphoreType.DMA((nc,)), pltpu.SemaphoreType.DMA((N_CHIP, nc)), pltpu.SemaphoreType.DMA],
        compiler_params=_params("arbitrary"))(*segs, b)
    return pair.reshape(N_CHIP, SHARD_IN, n), red


def _branches_fwd(z_rnn, z_attn, ag_ml, b_gate, w_rnn, w_attn, w_out, x, target, g_post, name, tm=512):
    t, d = x.shape
    tm = min(tm, t)

    def body(zr_ref, za_ref, lr_ref, la_ref, br_ref, ba_ref, wr_ref, wa_ref, wo_ref, x_ref, t_ref, g_ref,
             brr_ref, bra_ref, mg_ref, do_ref, dy_ref, st_ref):
        @pl.when(pl.program_id(0) == 0)
        def _():
            st_ref[...] = jnp.zeros_like(st_ref)

        br_rnn = jnp.dot(zr_ref[...], wr_ref[...], preferred_element_type=F32)
        br_attn = jnp.dot(za_ref[...], wa_ref[...], preferred_element_type=F32)
        brr_ref[...] = br_rnn.astype(BF16)
        bra_ref[...] = br_attn.astype(BF16)
        g_rnn = _sigmoid(lr_ref[...].astype(F32) + br_ref[...])
        g_attn = _sigmoid(la_ref[...].astype(F32) + ba_ref[...])
        merged = (g_rnn * br_rnn + g_attn * br_attn).astype(BF16)
        mg_ref[...] = merged
        o = jnp.dot(merged, wo_ref[...], preferred_element_type=F32)
        g = g_ref[...]
        r = lax.rsqrt(jnp.mean(o * o, axis=-1, keepdims=True) + EPS)
        nrm = o * r
        err = x_ref[...] + nrm * g - t_ref[...]
        dy = err * (1.0 / d)
        dy_ref[...] = dy
        dn = dy * g
        do_ref[...] = (r * (dn - nrm * jnp.mean(dn * nrm, axis=-1, keepdims=True))).astype(BF16)
        st_ref[...] += _rows8([jnp.sum(dy * nrm, axis=0, keepdims=True), jnp.sum(err * err, axis=0, keepdims=True)], d)

    tile = pl.BlockSpec((tm, d), lambda i: (i, 0))
    weight = pl.BlockSpec((d, d), lambda i: (0, 0))
    bf = jax.ShapeDtypeStruct((t, d), BF16)
    return pl.pallas_call(
        body, name=name, grid=(t // tm,),
        in_specs=[tile, tile, pl.BlockSpec((tm, d), lambda i: (i, 1)), pl.BlockSpec((tm, d), lambda i: (i, 2)),
                  pl.BlockSpec((1, d), lambda i: (0, 0)), pl.BlockSpec((1, d), lambda i: (0, 1)),
                  weight, weight, weight, tile, tile, pl.BlockSpec((1, d), lambda i: (0, 0))],
        out_specs=[tile, tile, tile, tile, tile, pl.BlockSpec((8, d), lambda i: (0, 0))],
        out_shape=[bf, bf, bf, bf, jax.ShapeDtypeStruct((t, d), F32), jax.ShapeDtypeStruct((8, d), F32)],
        compiler_params=_params("arbitrary"))(z_rnn, z_attn, ag_ml, ag_ml, b_gate, b_gate, w_rnn, w_attn, w_out, x, target, g_post)


def _branches_bwd(dout, br_rnn, br_attn, ag_ml, b_gate, w_rnn, w_attn, w_out, name, tm=512):
    t, d = br_rnn.shape
    tm = min(tm, t)

    def body(do_ref, r_ref, a_ref, lr_ref, la_ref, br_ref, ba_ref, wr_ref, wa_ref, wo_ref,
             dr_ref, da_ref, dl_ref, dzr_ref, dza_ref, st_ref, wt_ref):
        @pl.when(pl.program_id(0) == 0)
        def _():
            st_ref[...] = jnp.zeros_like(st_ref)
            wt_ref[0] = wo_ref[...].T
            wt_ref[1] = wr_ref[...].T
            wt_ref[2] = wa_ref[...].T

        dm = jnp.dot(do_ref[...], wt_ref[0], preferred_element_type=F32)
        g_rnn = _sigmoid(lr_ref[...].astype(F32) + br_ref[...])
        g_attn = _sigmoid(la_ref[...].astype(F32) + ba_ref[...])
        dbr_rnn = (dm * g_rnn).astype(BF16)
        dbr_attn = (dm * g_attn).astype(BF16)
        dr_ref[...] = dbr_rnn
        da_ref[...] = dbr_attn
        dl_rnn = dm * r_ref[...].astype(F32) * g_rnn * (1.0 - g_rnn)
        dl_attn = dm * a_ref[...].astype(F32) * g_attn * (1.0 - g_attn)
        dl_ref[:, 0:d] = dl_rnn.astype(BF16)
        dl_ref[:, d:2 * d] = dl_attn.astype(BF16)
        st_ref[...] += _rows8([jnp.sum(dl_rnn, axis=0, keepdims=True), jnp.sum(dl_attn, axis=0, keepdims=True)], d)
        dzr_ref[...] = jnp.dot(dbr_rnn, wt_ref[1], preferred_element_type=F32).astype(BF16)
        dza_ref[...] = jnp.dot(dbr_attn, wt_ref[2], preferred_element_type=F32).astype(BF16)

    tile = pl.BlockSpec((tm, d), lambda i: (i, 0))
    weight = pl.BlockSpec((d, d), lambda i: (0, 0))
    bf = jax.ShapeDtypeStruct((t, d), BF16)
    return pl.pallas_call(
        body, name=name, grid=(t // tm,),
        in_specs=[tile, tile, tile, pl.BlockSpec((tm, d), lambda i: (i, 1)), pl.BlockSpec((tm, d), lambda i: (i, 2)),
                  pl.BlockSpec((1, d), lambda i: (0, 0)), pl.BlockSpec((1, d), lambda i: (0, 1)), weight, weight, weight],
        out_specs=[tile, tile, pl.BlockSpec((tm, 2 * d), lambda i: (i, 0)), tile, tile, pl.BlockSpec((8, d), lambda i: (0, 0))],
        out_shape=[bf, bf, jax.ShapeDtypeStruct((t, 2 * d), BF16), bf, bf, jax.ShapeDtypeStruct((8, d), F32)],
        scratch_shapes=[pltpu.VMEM((3, d, d), BF16)],
        compiler_params=_params("arbitrary"))(dout, br_rnn, br_attn, ag_ml, ag_ml, b_gate, b_gate, w_rnn, w_attn, w_out)


def _lru_gates(c, wa, ba, wx, bx, sp):
    cb = c.astype(BF16)
    r = _sigmoid(jnp.dot(cb, wa, preferred_element_type=F32) + ba)
    ig = _sigmoid(jnp.dot(cb, wx, preferred_element_type=F32) + bx)
    log_a = (-LRU_C) * r * sp
    a = jnp.exp(log_a)
    mult = jnp.sqrt(-jnp.tanh(log_a) * (a * a + 1.0))
    return cb, r, ig, a, mult


SUBLANES = 8


def _scan_fwd(a, u, carry, tt):
    w = a.shape[1]
    ng = tt // SUBLANES
    a3 = a.reshape(ng, SUBLANES, w)
    u3 = u.reshape(ng, SUBLANES, w)
    sub = lax.broadcasted_iota(jnp.int32, (ng, SUBLANES, w), 1)
    d = 1
    while d < SUBLANES:
        keep = sub >= d
        u3 = u3 + a3 * jnp.where(keep, pltpu.roll(u3, d, 1), 0.0)
        a3 = a3 * jnp.where(keep, pltpu.roll(a3, d, 1), 1.0)
        d *= 2
    out = []
    for g in range(ng):
        hg = u3[g] + a3[g] * carry
        out.append(hg)
        carry = hg[SUBLANES - 1:SUBLANES, :]
    return jnp.concatenate(out, axis=0)


def _scan_rev(b, g, carry, tt):
    w = b.shape[1]
    ng = tt // SUBLANES
    b3 = b.reshape(ng, SUBLANES, w)
    g3 = g.reshape(ng, SUBLANES, w)
    sub = lax.broadcasted_iota(jnp.int32, (ng, SUBLANES, w), 1)
    d = 1
    while d < SUBLANES:
        keep = sub < SUBLANES - d
        g3 = g3 + b3 * jnp.where(keep, pltpu.roll(g3, SUBLANES - d, 1), 0.0)
        b3 = b3 * jnp.where(keep, pltpu.roll(b3, SUBLANES - d, 1), 1.0)
        d *= 2
    out = [None] * ng
    for k in range(ng - 1, -1, -1):
        hk = g3[k] + b3[k] * carry
        out[k] = hk
        carry = hk[0:1, :]
    return jnp.concatenate(out, axis=0)


def _conv_taps(cw, bias, x, ext_ref, tt):
    x2 = ext_ref[7:7 + tt, :]
    x1 = ext_ref[6:6 + tt, :]
    x0 = ext_ref[5:5 + tt, :]
    c = bias + cw[3:4] * x + cw[2:3] * x2 + cw[1:2] * x1 + cw[0:1] * x0
    return c, x2, x1, x0


def _rnn_fwd(rx_rg, cw, cb, wa, ba, wx, bx, lam, name, tt=512):
    t = rx_rg.shape[0]
    tt = min(tt, t)
    w = GROUP_W

    def body(rx_ref, rg_ref, cw_ref, cb_ref, wa_ref, ba_ref, wx_ref, bx_ref, lam_ref, y_ref, z_ref, ext_ref, hc_ref):
        @pl.when(pl.program_id(1) == 0)
        def _():
            ext_ref[0:8, :] = jnp.zeros((8, w), F32)
            hc_ref[...] = jnp.zeros((8, w), F32)

        x = rx_ref[...]
        ext_ref[8:8 + tt, :] = x
        c, _, _, _ = _conv_taps(cw_ref[...], cb_ref[...], x, ext_ref, tt)
        ext_ref[0:8, :] = x[tt - 8:tt, :]
        sp = _softplus(-lam_ref[...])
        _, _, ig, a, mult = _lru_gates(c, wa_ref[...], ba_ref[...], wx_ref[...], bx_ref[...], sp)
        h = _scan_fwd(a, mult * (ig * c), hc_ref[7:8, :], tt)
        hc_ref[...] = h[tt - 8:tt, :]
        y_ref[...] = h
        rg = rg_ref[...]
        z_ref[...] = (h * rg * _sigmoid(rg)).astype(BF16)

    vec = pl.BlockSpec((1, w), lambda g, i: (0, g))
    mat = pl.BlockSpec((None, w, w), lambda g, i: (g, 0, 0))
    tile = pl.BlockSpec((tt, w), lambda g, i: (i, g))
    return pl.pallas_call(
        body, name=name, grid=(N_GROUPS, t // tt),
        in_specs=[tile, pl.BlockSpec((tt, w), lambda g, i: (i, N_GROUPS + g)),
                  pl.BlockSpec((4, w), lambda g, i: (0, g)), vec, mat, vec, mat, vec, vec],
        out_specs=[tile, tile],
        out_shape=[jax.ShapeDtypeStruct((t, D_RNN), F32), jax.ShapeDtypeStruct((t, D_RNN), BF16)],
        scratch_shapes=[pltpu.VMEM((tt + 8, w), F32), pltpu.VMEM((8, w), F32)],
        compiler_params=_params("parallel", "arbitrary"))(rx_rg, rx_rg, cw, cb, wa, ba, wx, bx, lam)


def _rnn_bwd(rx_rg, y, dz, cw, cb, wa, ba, wx, bx, lam, name, tt=512, ride=None):
    t = rx_rg.shape[0]
    tt = min(tt, t)
    nt = t // tt
    w = GROUP_W

    host = _Hosted(ride, 13, 5, 6)

    def body(*refs):
        host_refs, start, finish = host.split(refs)
        (rx_ref, rg_ref, rxt_ref, y_ref, yt_ref, dz_ref, cw_ref, cb_ref, wa_ref, ba_ref, wx_ref, bx_ref, lam_ref,
         drx_ref, drg_ref, st_ref, gda_ref, gdx_ref, ext_ref, dcx_ref, wcar_ref, acar_ref, dwa_ref, dwx_ref) = host_refs
        ii = pl.program_id(1)
        start((pl.program_id(0) == 0) & (ii == 0))

        @pl.when(ii == 0)
        def _():
            wcar_ref[...] = jnp.zeros((8, w), F32)
            acar_ref[...] = jnp.zeros((8, w), F32)
            dcx_ref[tt:tt + 8, :] = jnp.zeros((8, w), F32)
            st_ref[...] = jnp.zeros_like(st_ref)
            dwa_ref[...] = jnp.zeros_like(dwa_ref)
            dwx_ref[...] = jnp.zeros_like(dwx_ref)

        has_prev = jnp.where(ii == nt - 1, 0.0, 1.0)
        x = rx_ref[...]
        ext_ref[0:8, :] = rxt_ref[...] * has_prev
        ext_ref[8:8 + tt, :] = x
        cwv = cw_ref[...]
        c, x2, x1, x0 = _conv_taps(cwv, cb_ref[...], x, ext_ref, tt)
        lam = lam_ref[...]
        sp = _softplus(-lam)
        wa = wa_ref[...]
        wx = wx_ref[...]
        cb16, r, ig, a, mult = _lru_gates(c, wa, ba_ref[...], wx, bx_ref[...], sp)

        rg = rg_ref[...]
        sg = _sigmoid(rg)
        dz = dz_ref[...].astype(F32)
        yv = y_ref[...]
        drg_ref[...] = (dz * yv * (sg * (1.0 + rg * (1.0 - sg)))).astype(BF16)

        row = lax.broadcasted_iota(jnp.int32, (tt, w), 0)
        b = jnp.where(row < tt - 1, pltpu.roll(a, tt - 1, 0), acar_ref[0:1, :])
        dh = _scan_rev(b, dz * (rg * sg), wcar_ref[0:1, :], tt)
        wcar_ref[...] = dh[0:8, :]
        acar_ref[...] = a[0:8, :]

        hprev = jnp.where(row >= 1, pltpu.roll(yv, 1, 0), yt_ref[7:8, :] * has_prev)
        dmult = dh * (ig * c)
        dig = dh * mult * c
        dlog_a = dh * hprev * a - dmult * (a * a / mult)
        dpa = dlog_a * ((-LRU_C) * sp) * r * (1.0 - r)
        dpx = dig * ig * (1.0 - ig)
        dsp = jnp.sum(dlog_a * r, axis=0, keepdims=True) * (-LRU_C)
        dlam = dsp * (-_sigmoid(-lam))
        dpa16 = dpa.astype(BF16)
        dpx16 = dpx.astype(BF16)
        dwa_ref[...] += lax.dot_general(cb16, dpa16, TN_DIMS, preferred_element_type=F32)
        dwx_ref[...] += lax.dot_general(cb16, dpx16, TN_DIMS, preferred_element_type=F32)
        dc = (dh * mult * ig
              + lax.dot_general(dpa16, wa, NT_DIMS, preferred_element_type=F32)
              + lax.dot_general(dpx16, wx, NT_DIMS, preferred_element_type=F32))

        dcx_ref[0:tt, :] = dc
        drx = (cwv[3:4] * dc + cwv[2:3] * dcx_ref[1:1 + tt, :] + cwv[1:2] * dcx_ref[2:2 + tt, :]
               + cwv[0:1] * dcx_ref[3:3 + tt, :])
        drx_ref[...] = drx.astype(BF16)
        dcx_ref[tt:tt + 8, :] = dc[0:8, :]

        def colsum(v):
            return jnp.sum(v, axis=0, keepdims=True)

        st_ref[...] += _rows8([colsum(dc), colsum(dpa), colsum(dpx), dlam,
                               colsum(dc * x0), colsum(dc * x1), colsum(dc * x2), colsum(dc * x)], w)

        @pl.when(ii == nt - 1)
        def _():
            for blk in range(GROUP_W // RNN_BLOCK_W):
                rows = slice(blk * RNN_BLOCK_W, (blk + 1) * RNN_BLOCK_W)
                gda_ref[blk] = dwa_ref[rows, rows]
                gdx_ref[blk] = dwx_ref[rows, rows]

        finish((pl.program_id(0) == N_GROUPS - 1) & (ii == nt - 1))

    def rev(ii):
        return nt - 1 - ii

    def tail(g, ii):
        return (jnp.maximum(rev(ii) * (tt // 8) - 1, 0), g)

    vec = pl.BlockSpec((1, w), lambda g, ii: (0, g))
    mat = pl.BlockSpec((None, w, w), lambda g, ii: (g, 0, 0))
    tile = pl.BlockSpec((tt, w), lambda g, ii: (rev(ii), g))
    diag_shape = (N_GROUPS, GROUP_W // RNN_BLOCK_W, RNN_BLOCK_W, RNN_BLOCK_W)
    diag = pl.BlockSpec((None,) + diag_shape[1:], lambda g, ii: (g, 0, 0, 0))
    outs = pl.pallas_call(
        body, name=name, grid=(N_GROUPS, nt),
        in_specs=[tile, pl.BlockSpec((tt, w), lambda g, ii: (rev(ii), N_GROUPS + g)), pl.BlockSpec((8, w), tail),
                  tile, pl.BlockSpec((8, w), tail), tile,
                  pl.BlockSpec((4, w), lambda g, ii: (0, g)), vec, mat, vec, mat, vec, vec] + host.in_specs,
        out_specs=[tile, tile, pl.BlockSpec((8, w), lambda g, ii: (0, g)), diag, diag] + host.out_specs,
        out_shape=[jax.ShapeDtypeStruct((t, D_RNN), BF16), jax.ShapeDtypeStruct((t, D_RNN), BF16),
                   jax.ShapeDtypeStruct((8, D_RNN), F32),
                   jax.ShapeDtypeStruct(diag_shape, F32), jax.ShapeDtypeStruct(diag_shape, F32)] + host.out_shapes,
        scratch_shapes=[pltpu.VMEM((tt + 8, w), F32), pltpu.VMEM((tt + 8, w), F32), pltpu.VMEM((8, w), F32),
                        pltpu.VMEM((8, w), F32), pltpu.VMEM((w, w), F32), pltpu.VMEM((w, w), F32)] + host.scratch_shapes,
        compiler_params=_params("arbitrary" if ride else "parallel", "arbitrary"))(
            rx_rg, rx_rg, rx_rg, y, y, dz, cw, cb, wa, ba, wx, bx, lam, *host.arrays)
    res, landed = host.results(outs, 5)
    return (*res, landed) if ride else tuple(res)


def _half_mask(shape, half):
    lane = lax.broadcasted_iota(jnp.int32, shape, 1)
    return (lane >= HEAD_DIM) if half else (lane < HEAD_DIM)


def _dup_half(t, half):
    sel = jnp.where(_half_mask(t.shape, half), t, 0.0)
    return sel + pltpu.roll(sel, HEAD_DIM, 1)


def _band_geometry(n):
    qi = lax.broadcasted_iota(jnp.int32, (BLOCK, 2 * BLOCK), 0)
    kj = lax.broadcasted_iota(jnp.int32, (BLOCK, 2 * BLOCK), 1)
    dist = BLOCK + qi - kj
    first_key = jnp.where(n > 0, 0, BLOCK)
    valid = (dist >= 0) & (dist < BLOCK) & (kj >= first_key)
    return dist.astype(F32), valid


GROUP = 4


def _kv_dup(prev_ref, cur_ref, hk, scale=1.0):
    tile = hk // 2
    kt = jnp.concatenate([prev_ref[:, tile * LANE:(tile + 1) * LANE], cur_ref[:, tile * LANE:(tile + 1) * LANE]], axis=0)
    return (_dup_half(kt.astype(F32), hk % 2) * scale).astype(BF16)


def _fill_bias(bias_ref, sm_ref, n):
    distf, valid = _band_geometry(n)
    for head in range(N_Q_HEADS):
        bias_ref[head] = jnp.where(valid, -sm_ref[1, head] * distf, MASKED)


def _head_scores(q2s, half, kdup, bias):
    qm = jnp.where(_half_mask(q2s.shape, half), q2s, jnp.zeros_like(q2s))
    return qm, lax.dot_general(qm, kdup, NT_DIMS, preferred_element_type=F32) + bias


def _attn_specs(nb, clamp_last):
    def blk(n):
        return jnp.minimum(n, nb - 1) if clamp_last else n

    q_spec = pl.BlockSpec((BLOCK, D_MODEL), lambda n: (blk(n), 0))
    k_prev = pl.BlockSpec((BLOCK, D_KV), lambda n: (jnp.maximum(blk(n) - 1, 0), D_MODEL // D_KV))
    k_cur = pl.BlockSpec((BLOCK, D_KV), lambda n: (blk(n), D_MODEL // D_KV))
    v_prev = pl.BlockSpec((BLOCK, D_KV), lambda n: (jnp.maximum(blk(n) - 1, 0), D_MODEL // D_KV + 1))
    v_cur = pl.BlockSpec((BLOCK, D_KV), lambda n: (blk(n), D_MODEL // D_KV + 1))
    return q_spec, k_prev, k_cur, v_prev, v_cur


def _attn_fwd(sm, qkv, ag_ml, name, ride=None):
    t = qkv.shape[0]
    nb = t // BLOCK

    host = _Hosted(ride, 7, 3, 1)

    def body(*refs):
        (sm_ref, q_ref, kp_ref, kc_ref, vp_ref, vc_ref, ag_ref, y_ref, z_ref, lse_ref, bias_ref), start, finish = host.split(refs)
        n = pl.program_id(0)
        start(n == 0)
        start.middle(n == (3 * nb) // 4)

        @pl.when(n <= 1)
        def _():
            _fill_bias(bias_ref, sm_ref, n)

        lane = lax.broadcasted_iota(jnp.int32, (BLOCK, LANE), 1)
        low = lane < HEAD_DIM
        lse = jnp.zeros((BLOCK, LANE), F32)
        for hk in range(N_Q_HEADS // GROUP):
            kdup = _kv_dup(kp_ref, kc_ref, hk)
            vdup = _kv_dup(vp_ref, vc_ref, hk)
            for k in (0, 1):
                c = slice((2 * hk + k) * LANE, (2 * hk + k + 1) * LANE)
                q2s = q_ref[:, c] * ATTN_SCALE
                outs = []
                for half in (0, 1):
                    head = GROUP * hk + 2 * k + half
                    _, s = _head_scores(q2s, half, kdup, bias_ref[head])
                    sink = sm_ref[0, head]
                    m = jnp.maximum(jnp.max(s, axis=1, keepdims=True), sink)
                    e = jnp.exp(s - m)
                    l = jnp.sum(e, axis=1, keepdims=True) + jnp.exp(sink - m)
                    outs.append(jnp.dot((e * (1.0 / l)).astype(BF16), vdup, preferred_element_type=F32))
                    lse = jnp.where(lane == head, m + jnp.log(l), lse)
                yt = jnp.where(low, outs[0], outs[1])
                y_ref[:, c] = yt
                ag = ag_ref[:, c].astype(F32)
                z_ref[:, c] = (yt * ag * _sigmoid(ag)).astype(BF16)
        lse_ref[...] = lse
        finish(n == nb - 1)

    q_spec, k_prev, k_cur, v_prev, v_cur = _attn_specs(nb, False)
    wide = pl.BlockSpec((BLOCK, D_MODEL), lambda n: (n, 0))
    outs = pl.pallas_call(
        body, name=name, grid=(nb,),
        in_specs=[pl.BlockSpec(memory_space=pltpu.SMEM), q_spec, k_prev, k_cur, v_prev, v_cur, wide] + host.in_specs,
        out_specs=[wide, wide, pl.BlockSpec((BLOCK, LANE), lambda n: (n, 0))] + host.out_specs,
        out_shape=[jax.ShapeDtypeStruct((t, D_MODEL), F32), jax.ShapeDtypeStruct((t, D_MODEL), BF16),
                   jax.ShapeDtypeStruct((t, LANE), F32)] + host.out_shapes,
        scratch_shapes=[pltpu.VMEM((N_Q_HEADS, BLOCK, 2 * BLOCK), F32)] + host.scratch_shapes,
        compiler_params=_params("arbitrary"))(sm, qkv, qkv, qkv, qkv, qkv, ag_ml, *host.arrays)
    res, landed = host.results(outs, 3)
    return (*res, landed) if ride else tuple(res)


def _attn_bwd(sm, qkv, ag_ml, y, lse, dz, name, ride=None):
    t = qkv.shape[0]
    nb = t // BLOCK
    host = _Hosted(ride, 10, 4, 3)

    def body(*refs):
        host_refs, start, finish = host.split(refs)
        (sm_ref, q_ref, kp_ref, kc_ref, vp_ref, vc_ref, ag_ref, y_ref, lse_ref, dz_ref,
         dq_ref, dkv_ref, dag_ref, ds_ref, ck_ref, cv_ref, bias_ref) = host_refs
        n = pl.program_id(0)
        start(n == 0)
        start.middle(n == (3 * nb) // 4)

        @pl.when(n == 0)
        def _():
            ck_ref[...] = jnp.zeros_like(ck_ref)
            cv_ref[...] = jnp.zeros_like(cv_ref)
            ds_ref[...] = jnp.zeros_like(ds_ref)

        @pl.when(n <= 1)
        def _():
            _fill_bias(bias_ref, sm_ref, n)

        @pl.when(n < nb)
        def _():
            lane8 = lax.broadcasted_iota(jnp.int32, (8, LANE), 1)
            row8 = lax.broadcasted_iota(jnp.int32, (8, LANE), 0)
            dsink = jnp.zeros((8, LANE), F32)
            dk_heads, dv_heads = [], []
            lse_tile = lse_ref[...]
            for hk in range(N_Q_HEADS // GROUP):
                kdup = _kv_dup(kp_ref, kc_ref, hk)
                ks = _kv_dup(kp_ref, kc_ref, hk, ATTN_SCALE)
                vdup = _kv_dup(vp_ref, vc_ref, hk)
                qms, dyhs, y_rows = [], [], []
                for k in (0, 1):
                    c = slice((2 * hk + k) * LANE, (2 * hk + k + 1) * LANE)
                    ag = ag_ref[:, c].astype(F32)
                    sg = _sigmoid(ag)
                    dzt = dz_ref[:, c].astype(F32)
                    yt = y_ref[:, c]
                    dag_ref[:, c] = (dzt * yt * (sg * (1.0 + ag * (1.0 - sg)))).astype(BF16)
                    dyt = dzt * (ag * sg)
                    q2s = q_ref[:, c] * ATTN_SCALE
                    for half in (0, 1):
                        hm = _half_mask(q2s.shape, half)
                        qms.append(jnp.where(hm, q2s, jnp.zeros_like(q2s)))
                        dyhs.append(jnp.where(hm, dyt, 0.0))
                        y_rows.append(yt)
                qm4 = jnp.concatenate(qms, axis=0)
                dy4 = jnp.concatenate(dyhs, axis=0)
                dy4_16 = dy4.astype(BF16)
                s4 = lax.dot_general(qm4, kdup, NT_DIMS, preferred_element_type=F32)
                dp4 = lax.dot_general(dy4_16, vdup, NT_DIMS, preferred_element_type=F32)
                probs16, ds16 = [], []
                for r in range(GROUP):
                    head = GROUP * hk + r
                    rows = slice(r * BLOCK, (r + 1) * BLOCK)
                    lh = lse_tile[:, head:head + 1]
                    probs = jnp.exp(s4[rows] + bias_ref[head] - lh)
                    psink = jnp.exp(sm_ref[0, head] - lh)
                    delta = jnp.sum(dyhs[r] * y_rows[r], axis=1, keepdims=True)
                    ds16.append((probs * (dp4[rows] - delta)).astype(BF16))
                    probs16.append(probs.astype(BF16))
                    dsink = dsink + jnp.where((row8 == 0) & (lane8 == head),
                                              -jnp.sum(psink * delta, axis=0, keepdims=True), 0.0)
                ds4 = jnp.concatenate(ds16, axis=0)
                p4 = jnp.concatenate(probs16, axis=0)
                dq4 = jnp.dot(ds4, ks, preferred_element_type=F32)
                low = _half_mask((BLOCK, LANE), 0)
                for k in (0, 1):
                    c = slice((2 * hk + k) * LANE, (2 * hk + k + 1) * LANE)
                    dq_ref[:, c] = jnp.where(low, dq4[2 * k * BLOCK:(2 * k + 1) * BLOCK],
                                             dq4[(2 * k + 1) * BLOCK:(2 * k + 2) * BLOCK]).astype(BF16)
                dk_acc = lax.dot_general(ds4, qm4, TN_DIMS, preferred_element_type=F32)
                dv_acc = lax.dot_general(p4, dy4_16, TN_DIMS, preferred_element_type=F32)
                dk_heads.append(dk_acc + pltpu.roll(dk_acc, HEAD_DIM, 1))
                dv_heads.append(dv_acc + pltpu.roll(dv_acc, HEAD_DIM, 1))
            ds_ref[...] += dsink
            low = _half_mask((2 * BLOCK, LANE), 0)
            for tile in range(2):
                cols = slice(tile * LANE, (tile + 1) * LANE)
                dkt = jnp.where(low, dk_heads[2 * tile], dk_heads[2 * tile + 1])
                dvt = jnp.where(low, dv_heads[2 * tile], dv_heads[2 * tile + 1])
                dkv_ref[:, cols] = (ck_ref[:, cols] + dkt[0:BLOCK, :]).astype(BF16)
                dkv_ref[:, D_KV + tile * LANE:D_KV + (tile + 1) * LANE] = (cv_ref[:, cols] + dvt[0:BLOCK, :]).astype(BF16)
                ck_ref[:, cols] = dkt[BLOCK:2 * BLOCK, :]
                cv_ref[:, cols] = dvt[BLOCK:2 * BLOCK, :]

        @pl.when(n == nb)
        def _():
            dkv_ref[:, 0:D_KV] = ck_ref[...].astype(BF16)
            dkv_ref[:, D_KV:2 * D_KV] = cv_ref[...].astype(BF16)

        finish(n == nb)

    q_spec, k_prev, k_cur, v_prev, v_cur = _attn_specs(nb, True)
    wide = pl.BlockSpec((BLOCK, D_MODEL), lambda n: (jnp.minimum(n, nb - 1), 0))
    outs = pl.pallas_call(
        body, name=name, grid=(nb + 1,),
        in_specs=[pl.BlockSpec(memory_space=pltpu.SMEM), q_spec, k_prev, k_cur, v_prev, v_cur, wide, wide,
                  pl.BlockSpec((BLOCK, LANE), lambda n: (jnp.minimum(n, nb - 1), 0)), wide] + host.in_specs,
        out_specs=[wide, pl.BlockSpec((BLOCK, 2 * D_KV), lambda n: (jnp.maximum(n - 1, 0), 0)), wide,
                   pl.BlockSpec((8, LANE), lambda n: (0, 0))] + host.out_specs,
        out_shape=[jax.ShapeDtypeStruct((t, D_MODEL), BF16), jax.ShapeDtypeStruct((t, 2 * D_KV), BF16),
                   jax.ShapeDtypeStruct((t, D_MODEL), BF16), jax.ShapeDtypeStruct((8, LANE), F32)] + host.out_shapes,
        scratch_shapes=[pltpu.VMEM((BLOCK, D_KV), F32), pltpu.VMEM((BLOCK, D_KV), F32),
                        pltpu.VMEM((N_Q_HEADS, BLOCK, 2 * BLOCK), F32)] + host.scratch_shapes,
        compiler_params=_params("arbitrary"))(sm, qkv, qkv, qkv, qkv, qkv, ag_ml, y, lse, dz, *host.arrays)
    res, landed = host.results(outs, 4)
    return (*res, landed) if ride else tuple(res)


def _local_grads(x, target, p, project, late_weights=None, reduce_out=None, reduce_in=None):
    h, rx_rg, qkv, ag_ml, wt = project(x, p["pre_g"])
    if late_weights is None:
        y_attn, z_attn, lse = _attn_fwd(p["sm"], qkv, ag_ml, "attn_fwd")
    else:
        y_attn, z_attn, lse, landed = _attn_fwd(p["sm"], qkv, ag_ml, "attn_fwd", ride=late_weights[0])
        p = {**p, **late_weights[1](landed)}
    lru = (p["cw"], p["cb"], p["wbd_a"], p["b_a"], p["wbd_x"], p["b_x"], p["lam"])
    y_rnn, z_rnn = _rnn_fwd(rx_rg, *lru, "rnn_fwd")
    br_rnn, br_attn, merged, dout, dy, st_post = _branches_fwd(
        z_rnn, z_attn, ag_ml, p["b_gate"], p["w_rnn"], p["w_attn"], p["w_out"], x, target, p["post_g"], "branches_fwd")

    dbr_rnn, dbr_attn, d_ml, dz_rnn, dz_attn, st_merge = _branches_bwd(
        dout, br_rnn, br_attn, ag_ml, p["b_gate"], p["w_rnn"], p["w_attn"], p["w_out"], "branches_bwd")
    out_prods = [(z_rnn, dbr_rnn), (z_attn, dbr_attn), (merged, dout)]
    gw_rnn = gw_attn = gw_out = red_out = red_in = None
    if reduce_out is None:
        gw_rnn, gw_attn, gw_out = (_mm_tn(a, b, nm) for (a, b), nm in zip(out_prods, ("gw_rnn", "gw_attn", "gw_out")))
    else:
        out_pairs = _mm_tn_pairs(out_prods, "gw_outs")
    d_rx, d_rg, st_rnn, g_rg_a, g_rg_x = _rnn_bwd(rx_rg, y_rnn, dz_rnn, *lru, "rnn_bwd")
    if reduce_out is None:
        dq, dkv, d_ag, st_sink = _attn_bwd(p["sm"], qkv, ag_ml, y_attn, lse, dz_attn, "attn_bwd")
    else:
        dq, dkv, d_ag, st_sink, red_out = _attn_bwd(p["sm"], qkv, ag_ml, y_attn, lse, dz_attn, "attn_bwd",
                                                    ride=reduce_out(out_pairs, g_rg_a, g_rg_x))

    segs = [d_rx, d_rg, dq, dkv, d_ag, d_ml]
    if reduce_in is None:
        gwt = _mm_tn_seg(segs, h, "gw_in")
        grad_x, st_pre = _input_grad(segs, wt, x, p["pre_g"], dy, "input_grad")
    else:
        gwt = None
        grad_x, st_pre = _input_grad(segs, wt, x, p["pre_g"], dy, "input_grad")
        red_in = [_mm_tn_seg_pair(segs, h, "gw_in")[1]]
    return dict(grad_x=grad_x, gwt=gwt, gw_rnn=gw_rnn, gw_attn=gw_attn, gw_out=gw_out,
                st_post=st_post, st_merge=st_merge, st_rnn=st_rnn, st_sink=st_sink, st_pre=st_pre,
                g_rg_a=g_rg_a, g_rg_x=g_rg_x, red_out=red_out, red_in=red_in)


def _place():
    x, y, c = lax.axis_index("x"), lax.axis_index("y"), lax.axis_index("c")
    return x, y, c


def _gather_ride(shards):
    n = len(shards)

    def copies(ins, outs, sems):
        send_sems, recv_sems, local_sems = sems
        x, y, c = _place()
        me, sibling = (x, y, c), (x, y, 1 - c)
        chips = [(1 - x, y), (x, 1 - y), (1 - x, 1 - y)]

        def slot(a, dev):
            return outs[a].at[4 * dev[0] + 2 * dev[1] + dev[2]]

        def copy(a, k, block, to, src=None):
            return pltpu.make_async_remote_copy(
                src_ref=slot(a, block) if src is None else src, dst_ref=slot(a, block),
                send_sem=send_sems.at[a, k], recv_sem=recv_sems.at[a, k], device_id=to, device_id_type=MESH)

        mine = [pltpu.make_async_copy(ins[a], slot(a, me), local_sems.at[a]) for a in range(n)]
        first = []
        for a in range(n):
            first.append(copy(a, 0, me, sibling, src=ins[a]))
            first += [copy(a, 1 + j, me, (*chip, c), src=ins[a]) for j, chip in enumerate(chips)]
        return me, sibling, chips, c, copy, mine, first

    def start(ins, outs, sems):
        *_, mine, first = copies(ins, outs, sems)
        for cp in mine + first:
            cp.start()

    def middle(ins, outs, sems):
        me, sibling, chips, c, copy, _, _ = copies(ins, outs, sems)
        for j, chip in enumerate(chips):
            for a in range(n):
                copy(a, 1 + j, (*chip, c), me).wait_recv()
                copy(a, 4 + j, (*chip, c), sibling).start()

    def finish(ins, outs, sems):
        me, sibling, chips, c, copy, mine, first = copies(ins, outs, sems)
        passed = [copy(a, 4 + j, (*chip, c), sibling) for j, chip in enumerate(chips) for a in range(n)]
        for a in range(n):
            copy(a, 0, sibling, me).wait_recv()
            for j, chip in enumerate(chips):
                copy(a, 4 + j, (*chip, 1 - c), me).wait_recv()
        for cp in first + passed:
            cp.wait_send()
        for cp in mine:
            cp.wait()

    return _Ride(
        shards, [jax.ShapeDtypeStruct((N_DEV, *s.shape), s.dtype) for s in shards],
        [pltpu.SemaphoreType.DMA((n, 7)), pltpu.SemaphoreType.DMA((n, 7)), pltpu.SemaphoreType.DMA((n,))],
        start, finish, middle)


def _chips_ride(scatter, whole):
    ns, nw = len(scatter), len(whole)
    n = ns + nw

    def copies(ins, outs, sems):
        send_sems, recv_sems, local_sems = sems
        x, y, c = _place()
        own = 2 * x + y
        chips = [(1 - x, y), (x, 1 - y), (1 - x, 1 - y)]

        def src(a, chip_idx):
            return ins[a].at[chip_idx] if a < ns else ins[a]

        local = [pltpu.make_async_copy(src(a, own), outs[a].at[own], local_sems.at[a]) for a in range(n)]
        sent = [pltpu.make_async_remote_copy(
            src_ref=src(a, 2 * chip[0] + chip[1]), dst_ref=outs[a].at[own],
            send_sem=send_sems.at[a, j], recv_sem=recv_sems.at[a, own], device_id=(*chip, c), device_id_type=MESH)
            for a in range(n) for j, chip in enumerate(chips)]
        return chips, c, local, sent

    def start(ins, outs, sems):
        _, _, local, sent = copies(ins, outs, sems)
        for cp in local + sent:
            cp.start()

    def finish(ins, outs, sems):
        send_sems, recv_sems, _ = sems
        chips, c, local, sent = copies(ins, outs, sems)
        for a in range(n):
            for chip in chips:
                k = 2 * chip[0] + chip[1]
                pltpu.make_async_remote_copy(
                    src_ref=outs[a].at[k], dst_ref=outs[a].at[k], send_sem=send_sems.at[a, 0],
                    recv_sem=recv_sems.at[a, k], device_id=(*chip, c), device_id_type=MESH).wait_recv()
        for cp in sent:
            cp.wait_send()
        for cp in local:
            cp.wait()

    return _Ride(
        list(scatter) + list(whole),
        [jax.ShapeDtypeStruct(s.shape, s.dtype) for s in scatter]
        + [jax.ShapeDtypeStruct((N_CHIP, *s.shape), s.dtype) for s in whole],
        [pltpu.SemaphoreType.DMA((n, 3)), pltpu.SemaphoreType.DMA((n, N_CHIP)), pltpu.SemaphoreType.DMA((n,))],
        start, finish)


def _chips_rest_ride(pair, red):
    table = _side_pieces()
    nc = len(CHUNK_ORDER)

    def each(ins, outs, sems, sending, landing):
        send_sems, recv_sems, _ = sems
        pair_ref, red_ref = ins[0], outs[0]
        x, y, c = _place()
        own = 2 * x + y
        for core in (0, 1):
            mine = table[core][0]
            rest = [s for s in range(EARLY_STEPS, nc) if mine[s] is not None]

            @pl.when(c == core)
            def _(mine=mine, rest=rest, core=core):
                for s in rest:
                    _, rows, at = mine[s]
                    k, cp = _to_chip(pair_ref.at[at // SHARD_IN, pl.ds(at % SHARD_IN, rows)], red_ref, mine[s], s,
                                     send_sems, recv_sems, own, core)
                    pl.when(own != k)(lambda cp=cp: sending(cp))
                if landing is not None:
                    for k in range(N_CHIP):
                        @pl.when(own == k)
                        def _(k=k):
                            for s in rest:
                                _, rows, at = mine[s]
                                if at // SHARD_IN == k:
                                    for chip in range(N_CHIP):
                                        if chip != k:
                                            landing(_from_chip(pair_ref.at[k, pl.ds(at % SHARD_IN, rows)], red_ref, mine[s], s,
                                                               send_sems, recv_sems, chip, (x, y, c)))

    def local(ins, outs, sems):
        x, y, _ = _place()
        return pltpu.make_async_copy(ins[0].at[2 * x + y], outs[0].at[2 * x + y], sems[2])

    def start(ins, outs, sems):
        local(ins, outs, sems).start()
        each(ins, outs, sems, lambda cp: cp.start(), None)

    def finish(ins, outs, sems):
        each(ins, outs, sems, lambda cp: cp.wait_send(), lambda cp: cp.wait_recv())
        local(ins, outs, sems).wait()

    return _Ride([pair, red], [jax.ShapeDtypeStruct(red.shape, red.dtype)],
                 [pltpu.SemaphoreType.DMA((nc,)), pltpu.SemaphoreType.DMA((N_CHIP, nc)), pltpu.SemaphoreType.DMA],
                 start, finish, aliases={1: 0})


def _allreduce_small(pack, name):
    shape = pack.shape

    def body(x_ref, o_ref, sib_ref, chip_ref, send_sems, recv_sems):
        x, y, c = _place()
        own = 2 * x + y
        chips = [(1 - x, y), (x, 1 - y), (1 - x, 1 - y)]
        to_sibling = pltpu.make_async_remote_copy(
            src_ref=x_ref, dst_ref=sib_ref, send_sem=send_sems.at[0], recv_sem=recv_sems.at[0],
            device_id=(x, y, 1 - c), device_id_type=MESH)
        to_sibling.start()
        to_sibling.wait()
        chip_ref[own] = x_ref[...] + sib_ref[...]
        sent = [pltpu.make_async_remote_copy(
            src_ref=chip_ref.at[own], dst_ref=chip_ref.at[own], send_sem=send_sems.at[1 + j],
            recv_sem=recv_sems.at[1 + own], device_id=(*chip, c), device_id_type=MESH) for j, chip in enumerate(chips)]
        for cp in sent:
            cp.start()
        for chip in chips:
            k = 2 * chip[0] + chip[1]
            pltpu.make_async_remote_copy(
                src_ref=chip_ref.at[k], dst_ref=chip_ref.at[k], send_sem=send_sems.at[1],
                recv_sem=recv_sems.at[1 + k], device_id=(*chip, c), device_id_type=MESH).wait_recv()
        for cp in sent:
            cp.wait_send()
        o_ref[...] = (chip_ref[0] + chip_ref[1]) + (chip_ref[2] + chip_ref[3])

    return pl.pallas_call(
        body, name=name, out_shape=jax.ShapeDtypeStruct(shape, F32),
        in_specs=[pl.BlockSpec(memory_space=pltpu.VMEM)], out_specs=pl.BlockSpec(memory_space=pltpu.VMEM),
        scratch_shapes=[pltpu.VMEM(shape, F32), pltpu.VMEM((N_CHIP, *shape), F32),
                        pltpu.SemaphoreType.DMA((4,)), pltpu.SemaphoreType.DMA((1 + N_CHIP,))],
    )(pack)


def _adamw(g, w, m, v):
    m = ADAM_B1 * m + (1.0 - ADAM_B1) * g
    v = ADAM_B2 * v + (1.0 - ADAM_B2) * (g * g)
    m_hat = m / (1.0 - ADAM_B1 ** ADAM_STEP)
    v_hat = v / (1.0 - ADAM_B2 ** ADAM_STEP)
    delta = -ADAM_LR * (m_hat / (jnp.sqrt(v_hat) + ADAM_EPS) + ADAM_WD * w)
    return delta, m, v


def _adam_parts(items, name, tr=None):
    ni = len(items)
    npart, r, c = items[0][0].shape
    tr = r if tr is None else min(tr, r)

    def body(*refs):
        for a in range(ni):
            p_ref, w_ref, m_ref, v_ref = refs[4 * a:4 * a + 4]
            g_ref, d_ref, nm_ref, nv_ref = refs[4 * (ni + a):4 * (ni + a) + 4]
            g = p_ref[0].astype(F32)
            for k in range(1, npart):
                g = g + p_ref[k].astype(F32)
            g_ref[...] = g
            d_ref[...], nm_ref[...], nv_ref[...] = _adamw(g, w_ref[...], m_ref[...], v_ref[...])

    tile = pl.BlockSpec((tr, c), lambda i: (i, 0))
    outs = pl.pallas_call(
        body, name=name, grid=(r // tr,),
        in_specs=[pl.BlockSpec((npart, tr, c), lambda i: (0, i, 0)), tile, tile, tile] * ni,
        out_specs=[tile] * (4 * ni), out_shape=[jax.ShapeDtypeStruct((r, c), F32)] * (4 * ni),
        compiler_params=_params("parallel"))(*[arr for item in items for arr in item])
    return [outs[4 * a:4 * a + 4] for a in range(ni)]


def _block_diag(w):
    w4 = w.reshape(N_GROUPS, 4, RNN_BLOCK_W, RNN_BLOCK_W)
    eye = jnp.eye(4, dtype=w.dtype)
    return jnp.einsum("gbij,bc->gbicj", w4, eye).reshape(N_GROUPS, GROUP_W, GROUP_W).astype(BF16)


SMALL_ROWS = 16
ROW_PRE_G, ROW_BGATE, ROW_CONV_B, ROW_B_A, ROW_B_X, ROW_LAM, ROW_POST_G, ROW_LOSS, ROW_SINKS, ROW_CONV_W = 0, 1, 3, 4, 5, 6, 7, 8, 9, 10


def _pack_stats(st_pre, st_merge, st_rnn, st_post, st_sink, name):
    d = D_MODEL

    def body(pre_ref, mg_ref, rnn_ref, post_ref, sink_ref, o_ref):
        rnn = rnn_ref[...]
        sinks = jnp.concatenate([sink_ref[0:1, :], jnp.zeros((1, d - LANE), F32)], axis=1)
        o_ref[0:8, :] = _rows8([pre_ref[0:1, :], mg_ref[0:1, :], mg_ref[1:2, :], rnn[0:1], rnn[1:2], rnn[2:3], rnn[3:4],
                                post_ref[0:1, :]], d)
        o_ref[8:16, :] = _rows8([post_ref[1:2, :], sinks, rnn[4:5], rnn[5:6], rnn[6:7], rnn[7:8]], d)

    return pl.pallas_call(body, name=name, out_shape=jax.ShapeDtypeStruct((SMALL_ROWS, d), F32))(
        st_pre, st_merge, st_rnn, st_post, st_sink)


def _adam_small(total, w, m, v, name):
    d = D_MODEL
    n_in = len(w)

    def pack(refs):
        pre, bg, cbias, ba, bx, lam, post, sinks = [r[...] for r in refs]
        top = _rows8([pre, bg[:, 0:d], bg[:, d:2 * d], cbias, ba, bx, lam, post], d)
        return jnp.concatenate([top, _rows8([jnp.zeros((1, d), F32), sinks], d)], axis=0)

    def body(*refs):
        p_ref = refs[0]
        w_refs, m_refs, v_refs = (refs[1 + k * n_in:1 + (k + 1) * n_in] for k in range(3))
        outs = refs[1 + 3 * n_in:]
        g = p_ref[...]
        res = (g,) + _adamw(g, pack(w_refs), pack(m_refs), pack(v_refs))
        for k in range(4):
            outs[k][...] = res[k]
            outs[4 + k][...] = jnp.concatenate([res[k][ROW_BGATE:ROW_BGATE + 1], res[k][ROW_BGATE + 1:ROW_BGATE + 2]], axis=1)

    return pl.pallas_call(
        body, name=name,
        out_shape=[jax.ShapeDtypeStruct((SMALL_ROWS, d), F32)] * 4 + [jax.ShapeDtypeStruct((1, 2 * d), F32)] * 4,
    )(total, *w, *m, *v)


def kernel(x, pre_norm_g, w_in, b_gate, conv_w, conv_b, w_rg_a, b_rg_a, w_rg_x, b_rg_x, lru_lambda, attn_sinks, w_rnn_out, w_attn_out, w_out, post_norm_g, loss_target, m_pre_norm_g, m_w_in, m_b_gate, m_conv_w, m_conv_b, m_w_rg_a, m_b_rg_a, m_w_rg_x, m_b_rg_x, m_lru_lambda, m_attn_sinks, m_w_rnn_out, m_w_attn_out, m_w_out, m_post_norm_g, v_pre_norm_g, v_w_in, v_b_gate, v_conv_w, v_conv_b, v_w_rg_a, v_b_rg_a, v_w_rg_x, v_b_rg_x, v_lru_lambda, v_attn_sinks, v_w_rnn_out, v_w_attn_out, v_w_out, v_post_norm_g):
    cx, cy, cc = _place()
    dev = 4 * cx + 2 * cy + cc

    w_in_t, m_in_t, v_in_t = (jnp.transpose(a[0]) for a in (w_in, m_w_in, v_w_in))
    wt_shard = w_in_t.astype(BF16)

    def project(xs, pre_g):
        h, rx_rg, qkv, ag_ml, wt_all = _gather_project(xs, pre_g, wt_shard, "gather_project")
        return h, rx_rg, qkv, ag_ml, wt_all.reshape(D_IN, D_MODEL)

    def late_unpack(landed):
        w_rnn_all, w_attn_all, w_out_all, cw_all = landed
        return dict(w_rnn=w_rnn_all.reshape(D_RNN, D_MODEL), w_attn=w_attn_all.reshape(D_MODEL, D_MODEL),
                    w_out=w_out_all.reshape(D_MODEL, D_MODEL), cw=jnp.transpose(cw_all, (1, 0, 2)).reshape(4, D_RNN))

    late_weights = (_gather_ride([w_rnn_out[0].astype(BF16), w_attn_out[0].astype(BF16), w_out[0].astype(BF16), conv_w[0]]),
                    late_unpack)

    heads = jnp.arange(1, N_Q_HEADS + 1, dtype=F32)
    slopes = jnp.exp2(-ALIBI_MAX_BIAS * heads / N_Q_HEADS)
    b_a = b_rg_a.reshape(1, D_RNN)
    b_x = b_rg_x.reshape(1, D_RNN)
    p = dict(
        pre_g=pre_norm_g, post_g=post_norm_g, b_gate=b_gate, cb=conv_b,
        wbd_a=_block_diag(w_rg_a[0]), b_a=b_a, wbd_x=_block_diag(w_rg_x[0]), b_x=b_x, lam=lru_lambda,
        sm=jnp.pad(attn_sinks, ((0, 1), (0, 0))) + jnp.pad(slopes[None, :], ((1, 0), (0, 0))))

    flat = (RNN_BLOCKS * RNN_BLOCK_W, RNN_BLOCK_W)

    def reduce_out(out_pairs, g_rg_a, g_rg_x):
        return _join_rides(_chips_ride(out_pairs, []), _gather_ride([g_rg_a.reshape(flat), g_rg_x.reshape(flat)]))

    reduce_in = _chips_rest_ride

    g = _local_grads(x[0], loss_target[0], p, project, late_weights, reduce_out, reduce_in)
    small = _allreduce_small(
        _pack_stats(g["st_pre"], g["st_merge"], g["st_rnn"], g["st_post"], g["st_sink"], "pack_stats"), "allreduce_small")

    out = {}
    red = g["red_out"]
    w_in_out, = _adam_parts([(g["red_in"][0], w_in_t, m_in_t, v_in_t)], "adam_w_in", tr=SHARD_IN // 2)
    out["w_in"] = [jnp.transpose(o) for o in w_in_out]
    out["w_rnn_out"], out["w_attn_out"], out["w_out"] = _adam_parts(
        [(red[0], w_rnn_out[0], m_w_rnn_out[0], v_w_rnn_out[0]), (red[1], w_attn_out[0], m_w_attn_out[0], v_w_attn_out[0]),
         (red[2], w_out[0], m_w_out[0], v_w_out[0])], "adam_w_outs")
    out["w_rg_a"], out["w_rg_x"] = _adam_parts(
        [(red[3], w_rg_a.reshape(flat), m_w_rg_a.reshape(flat), v_w_rg_a.reshape(flat)),
         (red[4], w_rg_x.reshape(flat), m_w_rg_x.reshape(flat), v_w_rg_x.reshape(flat))], "adam_w_rg", tr=256)

    def rows(pre, bg, cbias, ba, bx, lam, post, sinks):
        return (pre, bg, cbias, ba.reshape(1, D_RNN), bx.reshape(1, D_RNN), lam, post,
                jnp.pad(sinks, ((0, 0), (0, D_MODEL - N_Q_HEADS))))

    small_out = _adam_small(
        small,
        rows(pre_norm_g, b_gate, conv_b, b_rg_a, b_rg_x, lru_lambda, post_norm_g, attn_sinks),
        rows(m_pre_norm_g, m_b_gate, m_conv_b, m_b_rg_a, m_b_rg_x, m_lru_lambda, m_post_norm_g, m_attn_sinks),
        rows(v_pre_norm_g, v_b_gate, v_conv_b, v_b_rg_a, v_b_rg_x, v_lru_lambda, v_post_norm_g, v_attn_sinks),
        "adam_small")
    packed, bgate_out = small_out[:4], small_out[4:]
    g_cw = lax.dynamic_slice(packed[0][ROW_CONV_W:ROW_CONV_W + 4], (0, dev * SHARD_OUT), (4, SHARD_OUT))
    out["conv_w"], = _adam_parts([(g_cw[None], conv_w[0], m_conv_w[0], v_conv_w[0])], "adam_conv_w")

    def unpack(kind, name):
        if name == "b_gate":
            return bgate_out[kind]
        row = dict(pre_norm_g=ROW_PRE_G, conv_b=ROW_CONV_B, b_rg_a=ROW_B_A, b_rg_x=ROW_B_X, lru_lambda=ROW_LAM,
                   post_norm_g=ROW_POST_G, attn_sinks=ROW_SINKS)[name]
        r = packed[kind][row:row + 1]
        if name == "attn_sinks":
            return r[:, 0:N_Q_HEADS]
        if name in ("b_rg_a", "b_rg_x"):
            return r.reshape(1, RNN_BLOCKS, RNN_BLOCK_W)
        return r

    shapes = dict(w_in=(1, D_MODEL, SHARD_IN), w_rnn_out=(1, SHARD_OUT, D_MODEL), w_attn_out=(1, SHARD_OUT, D_MODEL),
                  w_out=(1, SHARD_OUT, D_MODEL), w_rg_a=(1, RNN_BLOCKS, RNN_BLOCK_W, RNN_BLOCK_W),
                  w_rg_x=(1, RNN_BLOCKS, RNN_BLOCK_W, RNN_BLOCK_W), conv_w=(1, 4, SHARD_OUT))
    weights = ["pre_norm_g", "w_in", "b_gate", "conv_w", "conv_b", "w_rg_a", "b_rg_a", "w_rg_x", "b_rg_x",
               "lru_lambda", "attn_sinks", "w_rnn_out", "w_attn_out", "w_out", "post_norm_g"]
    results = []
    for kind in range(4):
        for name in weights:
            if name in out:
                results.append(out[name][kind].reshape(shapes[name]))
            else:
                results.append(unpack(kind, name))
    loss = 0.5 / D_MODEL * jnp.sum(packed[0][ROW_LOSS])
    return (loss, g["grad_x"][None], *results)
```

```python
import functools

import jax
import jax.numpy as jnp
from jax import lax
from jax.experimental import pallas as pl
from jax.experimental.pallas import tpu as pltpu

F32 = jnp.float32
BF16 = jnp.bfloat16

D_MODEL = 1024
D_RNN = 1024
RNN_BLOCKS = 16
RNN_BLOCK_W = 64
LRU_C = 8.0
N_Q_HEADS = 16
HEAD_DIM = 64
D_KV = 256
BLOCK = 128
ALIBI_MAX_BIAS = 8.0
EPS = 1e-6
D_IN = 6656
N_DEV = 8
N_CHIP = 4
SHARD_IN = D_IN // N_DEV
SHARD_OUT = D_MODEL // N_DEV
ATTN_SCALE = HEAD_DIM ** -0.5
MASKED = -1e30

ADAM_LR = 0.001
ADAM_B1 = 0.9
ADAM_B2 = 0.999
ADAM_EPS = 1e-08
ADAM_WD = 0.01
ADAM_STEP = 10

VMEM_LIMIT_BYTES = 52 * 1024 * 1024
LANE = 128
GROUP_W = 256
N_GROUPS = D_RNN // GROUP_W
SEG_CHUNK = 512

NT_DIMS = (((1,), (1,)), ((), ()))
TN_DIMS = (((0,), (0,)), ((), ()))
MESH = pl.DeviceIdType.MESH
ANY = pl.BlockSpec(memory_space=pl.ANY)


def _params(*semantics):
    return pltpu.CompilerParams(dimension_semantics=semantics, vmem_limit_bytes=VMEM_LIMIT_BYTES)


def _sigmoid(x):
    return 0.5 * jnp.tanh(0.5 * x) + 0.5


def _log1p(e):
    u = 1.0 + e
    den = jnp.where(u == 1.0, 1.0, u - 1.0)
    return jnp.where(u == 1.0, e, jnp.log(u) * (e / den))


def _softplus(z):
    return jnp.maximum(z, 0.0) + _log1p(jnp.exp(-jnp.abs(z)))


def _rows8(rows, width):
    idx = lax.broadcasted_iota(jnp.int32, (8, width), 0)
    out = jnp.zeros((8, width), F32)
    for r, v in enumerate(rows):
        out = jnp.where(idx == r, v, out)
    return out


class _Ride:
    def __init__(self, arrays, out_shapes, scratch_shapes, start, finish, middle=None, aliases=None):
        self.arrays, self.out_shapes, self.scratch_shapes = list(arrays), list(out_shapes), list(scratch_shapes)
        self.start, self.finish, self.middle = start, finish, middle
        self.aliases = dict(aliases or {})


class _Hosted:
    def __init__(self, ride, n_in, n_out, n_scratch=0, aliasing=False):
        self.ride = ride
        assert aliasing or not (ride and ride.aliases), "this host does not alias"
        self.aliases = {n_in + i: n_out + o for i, o in ride.aliases.items()} if ride else {}
        self.sizes = (n_in, len(ride.arrays) if ride else 0, n_out, len(ride.out_shapes) if ride else 0, n_scratch)
        self.arrays = ride.arrays if ride else []
        self.in_specs = [ANY] * len(self.arrays)
        self.out_shapes = ride.out_shapes if ride else []
        self.out_specs = [ANY] * len(self.out_shapes)
        self.scratch_shapes = ride.scratch_shapes if ride else []

    def split(self, refs):
        n_in, r_in, n_out, r_out, n_scr = self.sizes
        cuts = [0, n_in, n_in + r_in, n_in + r_in + n_out, n_in + r_in + n_out + r_out, n_in + r_in + n_out + r_out + n_scr]
        host_in, ride_in, host_out, ride_out, host_scr = (refs[cuts[k]:cuts[k + 1]] for k in range(5))
        ride_scr = refs[cuts[5]:]

        def start(when):
            if self.ride is not None:
                pl.when(when)(lambda: self.ride.start(ride_in, ride_out, ride_scr))

        def finish(when):
            if self.ride is not None:
                pl.when(when)(lambda: self.ride.finish(ride_in, ride_out, ride_scr))

        def middle(when):
            if self.ride is not None and self.ride.middle is not None:
                pl.when(when)(lambda: self.ride.middle(ride_in, ride_out, ride_scr))

        start.middle = middle
        return tuple(host_in) + tuple(host_out) + tuple(host_scr), start, finish

    def results(self, outs, n_out):
        outs = list(outs) if isinstance(outs, (list, tuple)) else [outs]
        return outs[:n_out], outs[n_out:]


def _join_rides(a, b):
    na, nb_ = len(a.arrays), len(b.arrays)
    oa = len(a.out_shapes)
    sa = len(a.scratch_shapes)

    def both(fa, fb):
        def run(ins, outs, sems):
            if fa is not None:
                fa(ins[:na], outs[:oa], sems[:sa])
            if fb is not None:
                fb(ins[na:na + nb_], outs[oa:], sems[sa:])
        return run

    middle = both(a.middle, b.middle) if (a.middle or b.middle) else None
    return _Ride(a.arrays + b.arrays, a.out_shapes + b.out_shapes, a.scratch_shapes + b.scratch_shapes,
                 both(a.start, b.start), both(a.finish, b.finish), middle)


def _load_resident(w_hbm, w_vmem, sems, first):
    def piece(c):
        rows = pl.ds(c * SEG_CHUNK, SEG_CHUNK)
        return pltpu.make_async_copy(w_hbm.at[rows], w_vmem.at[rows], sems.at[c])

    @pl.when(first)
    def _():
        for c in range(w_vmem.shape[0] // SEG_CHUNK):
            piece(c).start()

    def ready(c):
        @pl.when(first)
        def _():
            piece(c).wait()

    return ready


CHIP_ROWS = 2 * SHARD_IN
PROJ_WIDTHS = (2 * D_RNN, D_MODEL + 2 * D_KV, 3 * D_MODEL)
PROJ_DTYPES = (F32, BF16, BF16)


def _chip_pieces():
    starts = [0, PROJ_WIDTHS[0], PROJ_WIDTHS[0] + PROJ_WIDTHS[1], D_IN]
    pieces = []
    for k in range(N_CHIP):
        lo, hi = k * CHIP_ROWS, (k + 1) * CHIP_ROWS
        cur = []
        for a in range(len(PROJ_WIDTHS)):
            s0, s1 = max(lo, starts[a]), min(hi, starts[a + 1])
            if s0 < s1:
                cur.append((a, s0 - starts[a], s1 - s0, s0 - lo))
        pieces.append(cur)
    return pieces


def _gather_project(x, g, wt_shard, name, tm=1024):
    t, k = x.shape
    tm = min(tm, t)
    nt = t // tm
    pieces = _chip_pieces()

    def body(x_ref, g_ref, shard_ref, h_out, rx_ref, qkv_ref, ag_ref, wt_all,
             w_c, stage, o32, o16, h_all, send_sems, recv_sems, local_sem, stage_sems, out_sems, h_sems):
        s, ti = pl.program_id(0), pl.program_id(1)
        px, py, pc = _place()
        me, sibling = (px, py, pc), (px, py, 1 - pc)
        chips = [(px, py), (1 - px, py), (px, 1 - py), (1 - px, 1 - py)]
        outs = (rx_ref, qkv_ref, ag_ref)

        def slot(dev):
            return wt_all.at[4 * dev[0] + 2 * dev[1] + dev[2]]

        def copy(kk, block, to, src=None):
            return pltpu.make_async_remote_copy(
                src_ref=slot(block) if src is None else src, dst_ref=slot(block),
                send_sem=send_sems.at[kk], recv_sem=recv_sems.at[kk], device_id=to, device_id_type=MESH)

        mine = pltpu.make_async_copy(shard_ref, slot(me), local_sem)
        first = [copy(0, me, sibling, src=shard_ref)] + [copy(1 + j, me, (*chips[1 + j], pc), src=shard_ref) for j in range(3)]
        passed = [copy(4 + j, (*chips[1 + j], pc), sibling) for j in range(3)]

        @pl.when((s == 0) & (ti == 0))
        def _():
            mine.start()
            for cp in first[:3]:
                cp.start()

        def pass_on(step):
            copy(step, (*chips[step], pc), me).wait_recv()
            passed[step - 1].start()

        @pl.when((s == 2) & (ti == nt - 1))
        def _():
            pass_on(3)

        for step in range(N_CHIP):
            @pl.when((s == step) & (ti == 0))
            def _(step=step):
                chip = chips[step]
                if step == 0:
                    mine.wait()
                    copy(0, sibling, me).wait_recv()
                else:
                    if step == 1:
                        pass_on(1)
                        first[3].start()
                        pass_on(2)
                    copy(3 + step, (*chip, 1 - pc), me).wait_recv()
                loads = [pltpu.make_async_copy(slot((*chip, core)), stage.at[pl.ds(core * SHARD_IN, SHARD_IN)], stage_sems.at[core])
                         for core in (0, 1)]
                for cp in loads:
                    cp.start()
                for cp in loads:
                    cp.wait()
                w_c[...] = stage[...].T

        rows = pl.ds(pl.multiple_of(ti * tm, tm), tm)

        @pl.when(s == 0)
        def _():
            xv = x_ref[...]
            h_all[rows, :] = (xv * lax.rsqrt(jnp.mean(xv * xv, axis=-1, keepdims=True) + EPS) * g_ref[...]).astype(BF16)

        res = jnp.dot(h_all[rows, :], w_c[...], preferred_element_type=F32)

        n = s * nt + ti
        buf = lax.rem(n, 2)
        chip_idx = [2 * cx + cy for cx, cy in chips]

        def chip_at(step):
            return jnp.where(step == 0, chip_idx[0], jnp.where(step == 1, chip_idx[1], jnp.where(step == 2, chip_idx[2], chip_idx[3])))

        def h_write(b, tile):
            return pltpu.make_async_copy(h_all.at[pl.ds(tile * tm, tm)], h_out.at[pl.ds(tile * tm, tm)], h_sems.at[b])

        def writes(kchip, b, tile):
            cps = []
            for idx, (a, col, w, src) in enumerate(pieces[kchip]):
                staged = (o16 if PROJ_DTYPES[a] == BF16 else o32).at[b, :, pl.ds(src, w)]
                cps.append(pltpu.make_async_copy(staged, outs[a].at[pl.ds(tile * tm, tm), pl.ds(col, w)], out_sems.at[b, idx]))
            return cps

        o32[buf] = res
        o16[buf] = res.astype(BF16)
        kcur = chip_at(s)

        @pl.when(n > 0)
        def _():
            kprev = chip_at(lax.div(n - 1, nt))
            for kchip in range(N_CHIP):
                @pl.when(kprev == kchip)
                def _(kchip=kchip):
                    for cp in writes(kchip, 1 - buf, lax.rem(n - 1, nt)):
                        cp.wait()

            @pl.when(n <= nt)
            def _():
                h_write(1 - buf, n - 1).wait()

        @pl.when(s == 0)
        def _():
            h_write(buf, ti).start()

        for kchip in range(N_CHIP):
            @pl.when(kcur == kchip)
            def _(kchip=kchip):
                for cp in writes(kchip, buf, ti):
                    cp.start()

        @pl.when(n == N_CHIP * nt - 1)
        def _():
            for kchip in range(N_CHIP):
                @pl.when(kcur == kchip)
                def _(kchip=kchip):
                    for cp in writes(kchip, buf, ti):
                        cp.wait()
            for cp in first + passed:
                cp.wait_send()

    tile = pl.BlockSpec((tm, k), lambda s, ti: (jnp.where(s == 0, ti, nt - 1), 0))
    return pl.pallas_call(
        body, name=name, grid=(N_CHIP, nt),
        in_specs=[tile, pl.BlockSpec((1, k), lambda s, ti: (0, 0)), ANY],
        out_specs=[ANY, ANY, ANY, ANY, ANY],
        out_shape=[jax.ShapeDtypeStruct((t, k), BF16)]
        + [jax.ShapeDtypeStruct((t, w), dt) for w, dt in zip(PROJ_WIDTHS, PROJ_DTYPES)]
        + [jax.ShapeDtypeStruct((N_DEV, SHARD_IN, k), BF16)],
        scratch_shapes=[pltpu.VMEM((k, CHIP_ROWS), BF16), pltpu.VMEM((CHIP_ROWS, k), BF16),
                        pltpu.VMEM((2, tm, CHIP_ROWS), F32), pltpu.VMEM((2, tm, CHIP_ROWS), BF16), pltpu.VMEM((t, k), BF16),
                        pltpu.SemaphoreType.DMA((7,)), pltpu.SemaphoreType.DMA((7,)), pltpu.SemaphoreType.DMA,
                        pltpu.SemaphoreType.DMA((2,)), pltpu.SemaphoreType.DMA((2, 2)), pltpu.SemaphoreType.DMA((2,))],
        compiler_params=_params("arbitrary", "arbitrary"))(x, g, wt_shard)


def _mm_tn(a, b, name, tm=512, tk=4096):
    ktok, m = a.shape
    n = b.shape[1]
    tk = min(tk, ktok)

    def body(a_ref, b_ref, o_ref):
        @pl.when(pl.program_id(1) == 0)
        def _():
            o_ref[...] = jnp.zeros_like(o_ref)

        o_ref[...] += lax.dot_general(a_ref[...], b_ref[...], TN_DIMS, preferred_element_type=F32)

    return pl.pallas_call(
        body, name=name, grid=(m // tm, ktok // tk),
        in_specs=[pl.BlockSpec((tk, tm), lambda i, kk: (kk, i)), pl.BlockSpec((tk, n), lambda i, kk: (kk, 0))],
        out_specs=pl.BlockSpec((tm, n), lambda i, kk: (i, 0)),
        out_shape=jax.ShapeDtypeStruct((m, n), F32),
        compiler_params=_params("parallel", "arbitrary"))(a, b)


def _mm_tn_pairs(prods, name):
    n_prod = len(prods)
    ktok, m = prods[0][0].shape
    n = prods[0][1].shape[1]
    half, blk = m // 2, m // N_DEV
    ns = 2 * n_prod
    kept = N_CHIP * blk

    def side_pieces(s, side):
        i, hf = divmod(s, 2)
        first = hf * N_DEV // 2
        return [((d - first) * blk, i * kept + (d // 2) * blk) for d in range(first, first + N_DEV // 2) if d % 2 == side]

    def body(*refs):
        a_hbm, b_hbm, pair_refs = refs[:n_prod], refs[n_prod:2 * n_prod], refs[2 * n_prod:3 * n_prod]
        a_buf, b_buf, res_buf, recv_all, pair_all, a_sems, b_sems, send_sems, recv_sems, out_sems = refs[3 * n_prod:]
        step = pl.program_id(0)
        px, py, pc = _place()

        def fetch_a(s):
            return pltpu.make_async_copy(a_hbm[s // 2].at[:, pl.ds((s % 2) * half, half)], a_buf.at[s % 2], a_sems.at[s % 2])

        def fetch_b(i):
            return pltpu.make_async_copy(b_hbm[i], b_buf.at[i % 2], b_sems.at[i % 2])

        @pl.when(step == 0)
        def _():
            fetch_a(0).start()
            fetch_b(0).start()

        for s in range(ns):
            @pl.when(step == s)
            def _(s=s):
                if s + 1 < ns:
                    fetch_a(s + 1).start()
                if s % 2 == 0 and s // 2 + 1 < n_prod:
                    fetch_b(s // 2 + 1).start()
                fetch_a(s).wait()
                if s % 2 == 0:
                    fetch_b(s // 2).wait()

        res_buf[step % 2] = lax.dot_general(a_buf[step % 2], b_buf[(step // 2) % 2], TN_DIMS, preferred_element_type=F32)

        def crossing(s, j, piece):
            off, at = piece
            return pltpu.make_async_remote_copy(
                src_ref=res_buf.at[s % 2, pl.ds(off, blk)], dst_ref=recv_all.at[pl.ds(at, blk)],
                send_sem=send_sems.at[s, j], recv_sem=recv_sems.at[s, j], device_id=(px, py, 1 - pc), device_id_type=MESH)

        def write(s, core):
            at = side_pieces(s, core)[0][1]
            return pltpu.make_async_copy(pair_all.at[pl.ds(at, 2 * blk)], pair_refs[s // 2].at[pl.ds(at % kept, 2 * blk)], out_sems.at[s])

        for core in (0, 1):
            def settle(s, core=core):
                for j, piece in enumerate(side_pieces(s, 1 - core)):
                    crossing(s, j, piece).wait_send()
                for j, (off, at) in enumerate(side_pieces(s, core)):
                    crossing(s, j, (off, at)).wait_recv()
                    pair_all[at:at + blk] = (res_buf[s % 2, off:off + blk] + recv_all[at:at + blk]).astype(BF16)
                write(s, core).start()

            for s in range(ns):
                @pl.when((pc == core) & (step == s))
                def _(s=s, settle=settle, core=core):
                    for j, piece in enumerate(side_pieces(s, 1 - core)):
                        crossing(s, j, piece).start()
                    if s > 0:
                        settle(s - 1)
                    if s == ns - 1:
                        settle(s)
                        for t in range(ns):
                            write(t, core).wait()

    pairs = pl.pallas_call(
        body, name=name, grid=(ns,),
        in_specs=[ANY] * (2 * n_prod), out_specs=[ANY] * n_prod,
        out_shape=[jax.ShapeDtypeStruct((kept, n), BF16)] * n_prod,
        scratch_shapes=[pltpu.VMEM((2, ktok, half), prods[0][0].dtype), pltpu.VMEM((2, ktok, n), prods[0][1].dtype),
                        pltpu.VMEM((2, half, n), F32), pltpu.VMEM((n_prod * kept, n), F32), pltpu.VMEM((n_prod * kept, n), BF16),
                        pltpu.SemaphoreType.DMA((2,)), pltpu.SemaphoreType.DMA((2,)),
                        pltpu.SemaphoreType.DMA((ns, 2)), pltpu.SemaphoreType.DMA((ns, 2)), pltpu.SemaphoreType.DMA((ns,))],
        compiler_params=_params("arbitrary"))(*[a for a, _ in prods], *[b for _, b in prods])
    return [pair.reshape(N_CHIP, blk, n) for pair in pairs]


def _segment_chunks(segs):
    bounds = [0]
    for s in segs:
        bounds.append(bounds[-1] + s.shape[1] // SEG_CHUNK)
    return bounds


def _input_grad(segs, wt, x, g, dy, name, tm=512, ride=None):
    m = segs[0].shape[0]
    rows, n = wt.shape
    tm = min(tm, m)
    bounds = _segment_chunks(segs)
    n_seg = len(segs)
    ni = m // tm
    host = _Hosted(ride, n_seg + 4, 2, 2, aliasing=True)

    def body(*refs):
        host_refs, start, finish = host.split(refs)
        a_refs = host_refs[:n_seg]
        wt_hbm, x_ref, g_ref, dy_ref, gx_ref, st_ref, wt_vmem, sems = host_refs[n_seg:]
        i = pl.program_id(0)
        start(i == 0)

        @pl.when(i == 0)
        def _():
            st_ref[...] = jnp.zeros_like(st_ref)

        ready = _load_resident(wt_hbm, wt_vmem, sems, i == 0)
        dh = None
        for s in range(n_seg):
            for c in range(bounds[s], bounds[s + 1]):
                ready(c)
            part = jnp.dot(a_refs[s][...], wt_vmem[bounds[s] * SEG_CHUNK:bounds[s + 1] * SEG_CHUNK, :], preferred_element_type=F32)
            dh = part if dh is None else dh + part
        xv = x_ref[...]
        r = lax.rsqrt(jnp.mean(xv * xv, axis=-1, keepdims=True) + EPS)
        xn = xv * r
        dxn = dh * g_ref[...]
        gx_ref[...] = dy_ref[...] + r * (dxn - xn * jnp.mean(dxn * xn, axis=-1, keepdims=True))
        st_ref[...] += _rows8([jnp.sum(dh * xn, axis=0, keepdims=True)], n)
        finish(i == ni - 1)

    tile = pl.BlockSpec((tm, n), lambda i: (i, 0))
    outs = pl.pallas_call(
        body, name=name, grid=(ni,),
        in_specs=[pl.BlockSpec((tm, sg.shape[1]), lambda i: (i, 0)) for sg in segs]
        + [ANY, tile, pl.BlockSpec((1, n), lambda i: (0, 0)), tile] + host.in_specs,
        out_specs=[tile, pl.BlockSpec((8, n), lambda i: (0, 0))] + host.out_specs,
        out_shape=[jax.ShapeDtypeStruct((m, n), F32), jax.ShapeDtypeStruct((8, n), F32)] + host.out_shapes,
        scratch_shapes=[pltpu.VMEM((rows, n), wt.dtype), pltpu.SemaphoreType.DMA((rows // SEG_CHUNK,))] + host.scratch_shapes,
        input_output_aliases=host.aliases,
        compiler_params=_params("arbitrary"))(*segs, wt, x, g, dy, *host.arrays)
    res, landed = host.results(outs, 2)
    return (*res, landed) if ride else tuple(res)


def _mm_tn_seg(segs, b, name):
    ktok = segs[0].shape[0]
    n = b.shape[1]
    bounds = _segment_chunks(segs)
    n_seg = len(segs)
    nc = bounds[-1]
    seg_of = [s for s in range(n_seg) for _ in range(bounds[s], bounds[s + 1])]

    def body(*refs):
        a_hbm, b_hbm, o_ref = refs[:n_seg], refs[n_seg], refs[n_seg + 1]
        a_buf, b_vmem, a_sems, b_sem = refs[n_seg + 2:]
        c = pl.program_id(0)

        def fetch(cc):
            s = seg_of[cc]
            cols = pl.ds((cc - bounds[s]) * SEG_CHUNK, SEG_CHUNK)
            return pltpu.make_async_copy(a_hbm[s].at[:, cols], a_buf.at[cc % 2], a_sems.at[cc % 2])

        @pl.when(c == 0)
        def _():
            whole = pltpu.make_async_copy(b_hbm, b_vmem, b_sem)
            whole.start()
            fetch(0).start()
            whole.wait()

        for cc in range(nc):
            @pl.when(c == cc)
            def _(cc=cc):
                if cc + 1 < nc:
                    fetch(cc + 1).start()
                fetch(cc).wait()

        o_ref[...] = lax.dot_general(a_buf[c % 2], b_vmem[...], TN_DIMS, preferred_element_type=F32)

    return pl.pallas_call(
        body, name=name, grid=(nc,),
        in_specs=[ANY] * (n_seg + 1), out_specs=pl.BlockSpec((SEG_CHUNK, n), lambda c: (c, 0)),
        out_shape=jax.ShapeDtypeStruct((nc * SEG_CHUNK, n), F32),
        scratch_shapes=[pltpu.VMEM((2, ktok, SEG_CHUNK), segs[0].dtype), pltpu.VMEM((ktok, n), b.dtype),
                        pltpu.SemaphoreType.DMA((2,)), pltpu.SemaphoreType.DMA],
        compiler_params=_params("arbitrary"))(*segs, b)


CHUNK_ORDER = (0, 4, 7, 10, 1, 5, 8, 11, 2, 6, 9, 12, 3)
EARLY_STEPS = 8


def _side_pieces():
    table = []
    for core in (0, 1):
        sides = ([None] * len(CHUNK_ORDER), [None] * len(CHUNK_ORDER))
        for s, cc in enumerate(CHUNK_ORDER):
            g0 = cc * SEG_CHUNK
            for d in range(N_DEV):
                lo, hi = max(g0, d * SHARD_IN), min(g0 + SEG_CHUNK, (d + 1) * SHARD_IN)
                if lo < hi:
                    side = sides[0 if d % 2 == core else 1]
                    assert side[s] is None
                    side[s] = (lo - g0, hi - lo, (d // 2) * SHARD_IN + lo - d * SHARD_IN)
        table.append(sides)
    return table


def _to_chip(src, red_ref, piece, s, send_sems, recv_sems, own, core):
    _, rows, at = piece
    k, r0 = divmod(at, SHARD_IN)
    return k, pltpu.make_async_remote_copy(
        src_ref=src, dst_ref=red_ref.at[own, pl.ds(r0, rows)], send_sem=send_sems.at[s], recv_sem=recv_sems.at[own, s],
        device_id=(k // 2, k % 2, core), device_id_type=MESH)


def _from_chip(src, red_ref, piece, s, send_sems, recv_sems, chip, me):
    _, rows, at = piece
    return pltpu.make_async_remote_copy(
        src_ref=src, dst_ref=red_ref.at[chip, pl.ds(at % SHARD_IN, rows)], send_sem=send_sems.at[s], recv_sem=recv_sems.at[chip, s],
        device_id=me, device_id_type=MESH)


def _mm_tn_seg_pair(segs, b, name):
    ktok = segs[0].shape[0]
    n = b.shape[1]
    bounds = _segment_chunks(segs)
    n_seg = len(segs)
    nc = bounds[-1]
    assert nc == len(CHUNK_ORDER)
    seg_of = [s for s in range(n_seg) for _ in range(bounds[s], bounds[s + 1])]
    table = _side_pieces()

    def body(*refs):
        a_hbm, b_hbm, pair_ref, red_ref = refs[:n_seg], refs[n_seg], refs[n_seg + 1], refs[n_seg + 2]
        (a_buf, b_vmem, res_buf, recv_all, pair_all, a_sems, b_sem, send_sems, recv_sems, out_sems,
         chip_send, chip_recv) = refs[n_seg + 3:]
        step = pl.program_id(0)
        px, py, pc = _place()
        own = 2 * px + py

        def fetch(s):
            sg = seg_of[CHUNK_ORDER[s]]
            cols = pl.ds((CHUNK_ORDER[s] - bounds[sg]) * SEG_CHUNK, SEG_CHUNK)
            return pltpu.make_async_copy(a_hbm[sg].at[:, cols], a_buf.at[s % 2], a_sems.at[s % 2])

        @pl.when(step == 0)
        def _():
            whole = pltpu.make_async_copy(b_hbm, b_vmem, b_sem)
            whole.start()
            fetch(0).start()
            whole.wait()

        for s in range(nc):
            @pl.when(step == s)
            def _(s=s):
                if s + 1 < nc:
                    fetch(s + 1).start()
                fetch(s).wait()

        res_buf[step % 2] = lax.dot_general(a_buf[step % 2], b_vmem[...], TN_DIMS, preferred_element_type=F32)

        def crossing(s, piece):
            off, rows, at = piece
            return pltpu.make_async_remote_copy(
                src_ref=res_buf.at[s % 2, pl.ds(off, rows)], dst_ref=recv_all.at[pl.ds(at, rows)],
                send_sem=send_sems.at[s], recv_sem=recv_sems.at[s], device_id=(px, py, 1 - pc), device_id_type=MESH)

        def write(s, piece):
            _, rows, at = piece
            return pltpu.make_async_copy(pair_all.at[pl.ds(at, rows)], pair_ref.at[pl.ds(at, rows)], out_sems.at[s])

        def to_chip(s, piece, core):
            return _to_chip(pair_all.at[pl.ds(piece[2], piece[1])], red_ref, piece, s, chip_send, chip_recv, own, core)

        for core in (0, 1):
            mine, theirs = table[core]

            def settle(s, mine=mine, theirs=theirs, core=core):
                if theirs[s] is not None:
                    crossing(s, theirs[s]).wait_send()
                if mine[s] is not None:
                    off, rows, at = mine[s]
                    crossing(s, mine[s]).wait_recv()
                    pair_all[at:at + rows] = (res_buf[s % 2, off:off + rows] + recv_all[at:at + rows]).astype(BF16)
                    write(s, mine[s]).start()
                    if s < EARLY_STEPS:
                        k, cp = to_chip(s, mine[s], core)
                        pl.when(own != k)(cp.start)

            for s in range(nc):
                @pl.when((pc == core) & (step == s))
                def _(s=s, settle=settle, mine=mine, theirs=theirs, core=core):
                    if theirs[s] is not None:
                        crossing(s, theirs[s]).start()
                    if s > 0:
                        settle(s - 1)
                    if s == nc - 1:
                        settle(s)
                        kept = [t for t in range(nc) if mine[t] is not None]
                        for t in kept:
                            write(t, mine[t]).wait()
                        early = [t for t in kept if t < EARLY_STEPS]
                        for t in early:
                            k, cp = to_chip(t, mine[t], core)
                            pl.when(own != k)(cp.wait_send)
                        for k in range(N_CHIP):
                            @pl.when(own == k)
                            def _(k=k):
                                for t in early:
                                    if mine[t][2] // SHARD_IN == k:
                                        for chip in range(N_CHIP):
                                            if chip != k:
                                                _from_chip(pair_all.at[pl.ds(mine[t][2], mine[t][1])], red_ref, mine[t], t,
                                                           chip_send, chip_recv, chip, (px, py, pc)).wait_recv()

    flat = jax.ShapeDtypeStruct((N_CHIP * SHARD_IN, n), BF16)
    pair, red = pl.pallas_call(
        body, name=name, grid=(nc,),
        in_specs=[ANY] * (n_seg + 1), out_specs=[ANY, ANY],
        out_shape=[flat, jax.ShapeDtypeStruct((N_CHIP, SHARD_IN, n), BF16)],
        scratch_shapes=[pltpu.VMEM((2, ktok, SEG_CHUNK), segs[0].dtype), pltpu.VMEM((ktok, n), b.dtype),
                        pltpu.VMEM((2, SEG_CHUNK, n), F32), pltpu.VMEM(flat.shape, F32), pltpu.VMEM(flat.shape, BF16),
                        pltpu.SemaphoreType.DMA((2,)), pltpu.SemaphoreType.DMA,
                        pltpu.SemaphoreType.DMA((nc,)), pltpu.SemaphoreType.DMA((nc,)), pltpu.SemaphoreType.DMA((nc,)),
                        pltpu.SemaphoreType.DMA((nc,)), pltpu.SemaphoreType.DMA((N_CHIP, nc))],
        compiler_params=_params("arbitrary"))(*segs, b)
    return pair.reshape(N_CHIP, SHARD_IN, n), red


def _branches_fwd(z_rnn, z_attn, ag_ml, b_gate, w_rnn, w_attn, w_out, x, target, g_post, name, tm=512):
    t, d = x.shape
    tm = min(tm, t)

    def body(zr_ref, za_ref, lr_ref, la_ref, br_ref, ba_ref, wr_ref, wa_ref, wo_ref, x_ref, t_ref, g_ref,
             brr_ref, bra_ref, mg_ref, do_ref, dy_ref, st_ref):
        @pl.when(pl.program_id(0) == 0)
        def _():
            st_ref[...] = jnp.zeros_like(st_ref)

        br_rnn = jnp.dot(zr_ref[...], wr_ref[...], preferred_element_type=F32)
        br_attn = jnp.dot(za_ref[...], wa_ref[...], preferred_element_type=F32)
        brr_ref[...] = br_rnn.astype(BF16)
        bra_ref[...] = br_attn.astype(BF16)
        g_rnn = _sigmoid(lr_ref[...].astype(F32) + br_ref[...])
        g_attn = _sigmoid(la_ref[...].astype(F32) + ba_ref[...])
        merged = (g_rnn * br_rnn + g_attn * br_attn).astype(BF16)
        mg_ref[...] = merged
        o = jnp.dot(merged, wo_ref[...], preferred_element_type=F32)
        g = g_ref[...]
        r = lax.rsqrt(jnp.mean(o * o, axis=-1, keepdims=True) + EPS)
        nrm = o * r
        err = x_ref[...] + nrm * g - t_ref[...]
        dy = err * (1.0 / d)
        dy_ref[...] = dy
        dn = dy * g
        do_ref[...] = (r * (dn - nrm * jnp.mean(dn * nrm, axis=-1, keepdims=True))).astype(BF16)
        st_ref[...] += _rows8([jnp.sum(dy * nrm, axis=0, keepdims=True), jnp.sum(err * err, axis=0, keepdims=True)], d)

    tile = pl.BlockSpec((tm, d), lambda i: (i, 0))
    weight = pl.BlockSpec((d, d), lambda i: (0, 0))
    bf = jax.ShapeDtypeStruct((t, d), BF16)
    return pl.pallas_call(
        body, name=name, grid=(t // tm,),
        in_specs=[tile, tile, pl.BlockSpec((tm, d), lambda i: (i, 1)), pl.BlockSpec((tm, d), lambda i: (i, 2)),
                  pl.BlockSpec((1, d), lambda i: (0, 0)), pl.BlockSpec((1, d), lambda i: (0, 1)),
                  weight, weight, weight, tile, tile, pl.BlockSpec((1, d), lambda i: (0, 0))],
        out_specs=[tile, tile, tile, tile, tile, pl.BlockSpec((8, d), lambda i: (0, 0))],
        out_shape=[bf, bf, bf, bf, jax.ShapeDtypeStruct((t, d), F32), jax.ShapeDtypeStruct((8, d), F32)],
        compiler_params=_params("arbitrary"))(z_rnn, z_attn, ag_ml, ag_ml, b_gate, b_gate, w_rnn, w_attn, w_out, x, target, g_post)


def _branches_bwd(dout, br_rnn, br_attn, ag_ml, b_gate, w_rnn, w_attn, w_out, name, tm=512):
    t, d = br_rnn.shape
    tm = min(tm, t)

    def body(do_ref, r_ref, a_ref, lr_ref, la_ref, br_ref, ba_ref, wr_ref, wa_ref, wo_ref,
             dr_ref, da_ref, dl_ref, dzr_ref, dza_ref, st_ref, wt_ref):
        @pl.when(pl.program_id(0) == 0)
        def _():
            st_ref[...] = jnp.zeros_like(st_ref)
            wt_ref[0] = wo_ref[...].T
            wt_ref[1] = wr_ref[...].T
            wt_ref[2] = wa_ref[...].T

        dm = jnp.dot(do_ref[...], wt_ref[0], preferred_element_type=F32)
        g_rnn = _sigmoid(lr_ref[...].astype(F32) + br_ref[...])
        g_attn = _sigmoid(la_ref[...].astype(F32) + ba_ref[...])
        dbr_rnn = (dm * g_rnn).astype(BF16)
        dbr_attn = (dm * g_attn).astype(BF16)
        dr_ref[...] = dbr_rnn
        da_ref[...] = dbr_attn
        dl_rnn = dm * r_ref[...].astype(F32) * g_rnn * (1.0 - g_rnn)
        dl_attn = dm * a_ref[...].astype(F32) * g_attn * (1.0 - g_attn)
        dl_ref[:, 0:d] = dl_rnn.astype(BF16)
        dl_ref[:, d:2 * d] = dl_attn.astype(BF16)
        st_ref[...] += _rows8([jnp.sum(dl_rnn, axis=0, keepdims=True), jnp.sum(dl_attn, axis=0, keepdims=True)], d)
        dzr_ref[...] = jnp.dot(dbr_rnn, wt_ref[1], preferred_element_type=F32).astype(BF16)
        dza_ref[...] = jnp.dot(dbr_attn, wt_ref[2], preferred_element_type=F32).astype(BF16)

    tile = pl.BlockSpec((tm, d), lambda i: (i, 0))
    weight = pl.BlockSpec((d, d), lambda i: (0, 0))
    bf = jax.ShapeDtypeStruct((t, d), BF16)
    return pl.pallas_call(
        body, name=name, grid=(t // tm,),
        in_specs=[tile, tile, tile, pl.BlockSpec((tm, d), lambda i: (i, 1)), pl.BlockSpec((tm, d), lambda i: (i, 2)),
                  pl.BlockSpec((1, d), lambda i: (0, 0)), pl.BlockSpec((1, d), lambda i: (0, 1)), weight, weight, weight],
        out_specs=[tile, tile, pl.BlockSpec((tm, 2 * d), lambda i: (i, 0)), tile, tile, pl.BlockSpec((8, d), lambda i: (0, 0))],
        out_shape=[bf, bf, jax.ShapeDtypeStruct((t, 2 * d), BF16), bf, bf, jax.ShapeDtypeStruct((8, d), F32)],
        scratch_shapes=[pltpu.VMEM((3, d, d), BF16)],
        compiler_params=_params("arbitrary"))(dout, br_rnn, br_attn, ag_ml, ag_ml, b_gate, b_gate, w_rnn, w_attn, w_out)


def _lru_gates(c, wa, ba, wx, bx, sp):
    cb = c.astype(BF16)
    r = _sigmoid(jnp.dot(cb, wa, preferred_element_type=F32) + ba)
    ig = _sigmoid(jnp.dot(cb, wx, preferred_element_type=F32) + bx)
    log_a = (-LRU_C) * r * sp
    a = jnp.exp(log_a)
    mult = jnp.sqrt(-jnp.tanh(log_a) * (a * a + 1.0))
    return cb, r, ig, a, mult


SUBLANES = 8


def _scan_fwd(a, u, carry, tt):
    w = a.shape[1]
    ng = tt // SUBLANES
    a3 = a.reshape(ng, SUBLANES, w)
    u3 = u.reshape(ng, SUBLANES, w)
    sub = lax.broadcasted_iota(jnp.int32, (ng, SUBLANES, w), 1)
    d = 1
    while d < SUBLANES:
        keep = sub >= d
        u3 = u3 + a3 * jnp.where(keep, pltpu.roll(u3, d, 1), 0.0)
        a3 = a3 * jnp.where(keep, pltpu.roll(a3, d, 1), 1.0)
        d *= 2
    out = []
    for g in range(ng):
        hg = u3[g] + a3[g] * carry
        out.append(hg)
        carry = hg[SUBLANES - 1:SUBLANES, :]
    return jnp.concatenate(out, axis=0)


def _scan_rev(b, g, carry, tt):
    w = b.shape[1]
    ng = tt // SUBLANES
    b3 = b.reshape(ng, SUBLANES, w)
    g3 = g.reshape(ng, SUBLANES, w)
    sub = lax.broadcasted_iota(jnp.int32, (ng, SUBLANES, w), 1)
    d = 1
    while d < SUBLANES:
        keep = sub < SUBLANES - d
        g3 = g3 + b3 * jnp.where(keep, pltpu.roll(g3, SUBLANES - d, 1), 0.0)
        b3 = b3 * jnp.where(keep, pltpu.roll(b3, SUBLANES - d, 1), 1.0)
        d *= 2
    out = [None] * ng
    for k in range(ng - 1, -1, -1):
        hk = g3[k] + b3[k] * carry
        out[k] = hk
        carry = hk[0:1, :]
    return jnp.concatenate(out, axis=0)


def _conv_taps(cw, bias, x, ext_ref, tt):
    x2 = ext_ref[7:7 + tt, :]
    x1 = ext_ref[6:6 + tt, :]
    x0 = ext_ref[5:5 + tt, :]
    c = bias + cw[3:4] * x + cw[2:3] * x2 + cw[1:2] * x1 + cw[0:1] * x0
    return c, x2, x1, x0


def _rnn_fwd(rx_rg, cw, cb, wa, ba, wx, bx, lam, name, tt=512):
    t = rx_rg.shape[0]
    tt = min(tt, t)
    w = GROUP_W

    def body(rx_ref, rg_ref, cw_ref, cb_ref, wa_ref, ba_ref, wx_ref, bx_ref, lam_ref, y_ref, z_ref, ext_ref, hc_ref):
        @pl.when(pl.program_id(1) == 0)
        def _():
            ext_ref[0:8, :] = jnp.zeros((8, w), F32)
            hc_ref[...] = jnp.zeros((8, w), F32)

        x = rx_ref[...]
        ext_ref[8:8 + tt, :] = x
        c, _, _, _ = _conv_taps(cw_ref[...], cb_ref[...], x, ext_ref, tt)
        ext_ref[0:8, :] = x[tt - 8:tt, :]
        sp = _softplus(-lam_ref[...])
        _, _, ig, a, mult = _lru_gates(c, wa_ref[...], ba_ref[...], wx_ref[...], bx_ref[...], sp)
        h = _scan_fwd(a, mult * (ig * c), hc_ref[7:8, :], tt)
        hc_ref[...] = h[tt - 8:tt, :]
        y_ref[...] = h
        rg = rg_ref[...]
        z_ref[...] = (h * rg * _sigmoid(rg)).astype(BF16)

    vec = pl.BlockSpec((1, w), lambda g, i: (0, g))
    mat = pl.BlockSpec((None, w, w), lambda g, i: (g, 0, 0))
    tile = pl.BlockSpec((tt, w), lambda g, i: (i, g))
    return pl.pallas_call(
        body, name=name, grid=(N_GROUPS, t // tt),
        in_specs=[tile, pl.BlockSpec((tt, w), lambda g, i: (i, N_GROUPS + g)),
                  pl.BlockSpec((4, w), lambda g, i: (0, g)), vec, mat, vec, mat, vec, vec],
        out_specs=[tile, tile],
        out_shape=[jax.ShapeDtypeStruct((t, D_RNN), F32), jax.ShapeDtypeStruct((t, D_RNN), BF16)],
        scratch_shapes=[pltpu.VMEM((tt + 8, w), F32), pltpu.VMEM((8, w), F32)],
        compiler_params=_params("parallel", "arbitrary"))(rx_rg, rx_rg, cw, cb, wa, ba, wx, bx, lam)


def _rnn_bwd(rx_rg, y, dz, cw, cb, wa, ba, wx, bx, lam, name, tt=512, ride=None):
    t = rx_rg.shape[0]
    tt = min(tt, t)
    nt = t // tt
    w = GROUP_W

    host = _Hosted(ride, 13, 5, 6)

    def body(*refs):
        host_refs, start, finish = host.split(refs)
        (rx_ref, rg_ref, rxt_ref, y_ref, yt_ref, dz_ref, cw_ref, cb_ref, wa_ref, ba_ref, wx_ref, bx_ref, lam_ref,
         drx_ref, drg_ref, st_ref, gda_ref, gdx_ref, ext_ref, dcx_ref, wcar_ref, acar_ref, dwa_ref, dwx_ref) = host_refs
        ii = pl.program_id(1)
        start((pl.program_id(0) == 0) & (ii == 0))

        @pl.when(ii == 0)
        def _():
            wcar_ref[...] = jnp.zeros((8, w), F32)
            acar_ref[...] = jnp.zeros((8, w), F32)
            dcx_ref[tt:tt + 8, :] = jnp.zeros((8, w), F32)
            st_ref[...] = jnp.zeros_like(st_ref)
            dwa_ref[...] = jnp.zeros_like(dwa_ref)
            dwx_ref[...] = jnp.zeros_like(dwx_ref)

        has_prev = jnp.where(ii == nt - 1, 0.0, 1.0)
        x = rx_ref[...]
        ext_ref[0:8, :] = rxt_ref[...] * has_prev
        ext_ref[8:8 + tt, :] = x
        cwv = cw_ref[...]
        c, x2, x1, x0 = _conv_taps(cwv, cb_ref[...], x, ext_ref, tt)
        lam = lam_ref[...]
        sp = _softplus(-lam)
        wa = wa_ref[...]
        wx = wx_ref[...]
        cb16, r, ig, a, mult = _lru_gates(c, wa, ba_ref[...], wx, bx_ref[...], sp)

        rg = rg_ref[...]
        sg = _sigmoid(rg)
        dz = dz_ref[...].astype(F32)
        yv = y_ref[...]
        drg_ref[...] = (dz * yv * (sg * (1.0 + rg * (1.0 - sg)))).astype(BF16)

        row = lax.broadcasted_iota(jnp.int32, (tt, w), 0)
        b = jnp.where(row < tt - 1, pltpu.roll(a, tt - 1, 0), acar_ref[0:1, :])
        dh = _scan_rev(b, dz * (rg * sg), wcar_ref[0:1, :], tt)
        wcar_ref[...] = dh[0:8, :]
        acar_ref[...] = a[0:8, :]

        hprev = jnp.where(row >= 1, pltpu.roll(yv, 1, 0), yt_ref[7:8, :] * has_prev)
        dmult = dh * (ig * c)
        dig = dh * mult * c
        dlog_a = dh * hprev * a - dmult * (a * a / mult)
        dpa = dlog_a * ((-LRU_C) * sp) * r * (1.0 - r)
        dpx = dig * ig * (1.0 - ig)
        dsp = jnp.sum(dlog_a * r, axis=0, keepdims=True) * (-LRU_C)
        dlam = dsp * (-_sigmoid(-lam))
        dpa16 = dpa.astype(BF16)
        dpx16 = dpx.astype(BF16)
        dwa_ref[...] += lax.dot_general(cb16, dpa16, TN_DIMS, preferred_element_type=F32)
        dwx_ref[...] += lax.dot_general(cb16, dpx16, TN_DIMS, preferred_element_type=F32)
        dc = (dh * mult * ig
              + lax.dot_general(dpa16, wa, NT_DIMS, preferred_element_type=F32)
              + lax.dot_general(dpx16, wx, NT_DIMS, preferred_element_type=F32))

        dcx_ref[0:tt, :] = dc
        drx = (cwv[3:4] * dc + cwv[2:3] * dcx_ref[1:1 + tt, :] + cwv[1:2] * dcx_ref[2:2 + tt, :]
               + cwv[0:1] * dcx_ref[3:3 + tt, :])
        drx_ref[...] = drx.astype(BF16)
        dcx_ref[tt:tt + 8, :] = dc[0:8, :]

        def colsum(v):
            return jnp.sum(v, axis=0, keepdims=True)

        st_ref[...] += _rows8([colsum(dc), colsum(dpa), colsum(dpx), dlam,
                               colsum(dc * x0), colsum(dc * x1), colsum(dc * x2), colsum(dc * x)], w)

        @pl.when(ii == nt - 1)
        def _():
            for blk in range(GROUP_W // RNN_BLOCK_W):
                rows = slice(blk * RNN_BLOCK_W, (blk + 1) * RNN_BLOCK_W)
                gda_ref[blk] = dwa_ref[rows, rows]
                gdx_ref[blk] = dwx_ref[rows, rows]

        finish((pl.program_id(0) == N_GROUPS - 1) & (ii == nt - 1))

    def rev(ii):
        return nt - 1 - ii

    def tail(g, ii):
        return (jnp.maximum(rev(ii) * (tt // 8) - 1, 0), g)

    vec = pl.BlockSpec((1, w), lambda g, ii: (0, g))
    mat = pl.BlockSpec((None, w, w), lambda g, ii: (g, 0, 0))
    tile = pl.BlockSpec((tt, w), lambda g, ii: (rev(ii), g))
    diag_shape = (N_GROUPS, GROUP_W // RNN_BLOCK_W, RNN_BLOCK_W, RNN_BLOCK_W)
    diag = pl.BlockSpec((None,) + diag_shape[1:], lambda g, ii: (g, 0, 0, 0))
    outs = pl.pallas_call(
        body, name=name, grid=(N_GROUPS, nt),
        in_specs=[tile, pl.BlockSpec((tt, w), lambda g, ii: (rev(ii), N_GROUPS + g)), pl.BlockSpec((8, w), tail),
                  tile, pl.BlockSpec((8, w), tail), tile,
                  pl.BlockSpec((4, w), lambda g, ii: (0, g)), vec, mat, vec, mat, vec, vec] + host.in_specs,
        out_specs=[tile, tile, pl.BlockSpec((8, w), lambda g, ii: (0, g)), diag, diag] + host.out_specs,
        out_shape=[jax.ShapeDtypeStruct((t, D_RNN), BF16), jax.ShapeDtypeStruct((t, D_RNN), BF16),
                   jax.ShapeDtypeStruct((8, D_RNN), F32),
                   jax.ShapeDtypeStruct(diag_shape, F32), jax.ShapeDtypeStruct(diag_shape, F32)] + host.out_shapes,
        scratch_shapes=[pltpu.VMEM((tt + 8, w), F32), pltpu.VMEM((tt + 8, w), F32), pltpu.VMEM((8, w), F32),
                        pltpu.VMEM((8, w), F32), pltpu.VMEM((w, w), F32), pltpu.VMEM((w, w), F32)] + host.scratch_shapes,
        compiler_params=_params("arbitrary" if ride else "parallel", "arbitrary"))(
            rx_rg, rx_rg, rx_rg, y, y, dz, cw, cb, wa, ba, wx, bx, lam, *host.arrays)
    res, landed = host.results(outs, 5)
    return (*res, landed) if ride else tuple(res)


def _half_mask(shape, half):
    lane = lax.broadcasted_iota(jnp.int32, shape, 1)
    return (lane >= HEAD_DIM) if half else (lane < HEAD_DIM)


def _dup_half(t, half):
    sel = jnp.where(_half_mask(t.shape, half), t, 0.0)
    return sel + pltpu.roll(sel, HEAD_DIM, 1)


def _band_geometry(n):
    qi = lax.broadcasted_iota(jnp.int32, (BLOCK, 2 * BLOCK), 0)
    kj = lax.broadcasted_iota(jnp.int32, (BLOCK, 2 * BLOCK), 1)
    dist = BLOCK + qi - kj
    first_key = jnp.where(n > 0, 0, BLOCK)
    valid = (dist >= 0) & (dist < BLOCK) & (kj >= first_key)
    return dist.astype(F32), valid


GROUP = 4


def _kv_dup(prev_ref, cur_ref, hk, scale=1.0):
    tile = hk // 2
    kt = jnp.concatenate([prev_ref[:, tile * LANE:(tile + 1) * LANE], cur_ref[:, tile * LANE:(tile + 1) * LANE]], axis=0)
    return (_dup_half(kt.astype(F32), hk % 2) * scale).astype(BF16)


def _fill_bias(bias_ref, sm_ref, n):
    distf, valid = _band_geometry(n)
    for head in range(N_Q_HEADS):
        bias_ref[head] = jnp.where(valid, -sm_ref[1, head] * distf, MASKED)


def _head_scores(q2s, half, kdup, bias):
    qm = jnp.where(_half_mask(q2s.shape, half), q2s, jnp.zeros_like(q2s))
    return qm, lax.dot_general(qm, kdup, NT_DIMS, preferred_element_type=F32) + bias


def _attn_specs(nb, clamp_last):
    def blk(n):
        return jnp.minimum(n, nb - 1) if clamp_last else n

    q_spec = pl.BlockSpec((BLOCK, D_MODEL), lambda n: (blk(n), 0))
    k_prev = pl.BlockSpec((BLOCK, D_KV), lambda n: (jnp.maximum(blk(n) - 1, 0), D_MODEL // D_KV))
    k_cur = pl.BlockSpec((BLOCK, D_KV), lambda n: (blk(n), D_MODEL // D_KV))
    v_prev = pl.BlockSpec((BLOCK, D_KV), lambda n: (jnp.maximum(blk(n) - 1, 0), D_MODEL // D_KV + 1))
    v_cur = pl.BlockSpec((BLOCK, D_KV), lambda n: (blk(n), D_MODEL // D_KV + 1))
    return q_spec, k_prev, k_cur, v_prev, v_cur


def _attn_fwd(sm, qkv, ag_ml, name, ride=None):
    t = qkv.shape[0]
    nb = t // BLOCK

    host = _Hosted(ride, 7, 3, 1)

    def body(*refs):
        (sm_ref, q_ref, kp_ref, kc_ref, vp_ref, vc_ref, ag_ref, y_ref, z_ref, lse_ref, bias_ref), start, finish = host.split(refs)
        n = pl.program_id(0)
        start(n == 0)
        start.middle(n == (3 * nb) // 4)

        @pl.when(n <= 1)
        def _():
            _fill_bias(bias_ref, sm_ref, n)

        lane = lax.broadcasted_iota(jnp.int32, (BLOCK, LANE), 1)
        low = lane < HEAD_DIM
        lse = jnp.zeros((BLOCK, LANE), F32)
        for hk in range(N_Q_HEADS // GROUP):
            kdup = _kv_dup(kp_ref, kc_ref, hk)
            vdup = _kv_dup(vp_ref, vc_ref, hk)
            for k in (0, 1):
                c = slice((2 * hk + k) * LANE, (2 * hk + k + 1) * LANE)
                q2s = q_ref[:, c] * ATTN_SCALE
                outs = []
                for half in (0, 1):
                    head = GROUP * hk + 2 * k + half
                    _, s = _head_scores(q2s, half, kdup, bias_ref[head])
                    sink = sm_ref[0, head]
                    m = jnp.maximum(jnp.max(s, axis=1, keepdims=True), sink)
                    e = jnp.exp(s - m)
                    l = jnp.sum(e, axis=1, keepdims=True) + jnp.exp(sink - m)
                    outs.append(jnp.dot((e * (1.0 / l)).astype(BF16), vdup, preferred_element_type=F32))
                    lse = jnp.where(lane == head, m + jnp.log(l), lse)
                yt = jnp.where(low, outs[0], outs[1])
                y_ref[:, c] = yt
                ag = ag_ref[:, c].astype(F32)
                z_ref[:, c] = (yt * ag * _sigmoid(ag)).astype(BF16)
        lse_ref[...] = lse
        finish(n == nb - 1)

    q_spec, k_prev, k_cur, v_prev, v_cur = _attn_specs(nb, False)
    wide = pl.BlockSpec((BLOCK, D_MODEL), lambda n: (n, 0))
    outs = pl.pallas_call(
        body, name=name, grid=(nb,),
        in_specs=[pl.BlockSpec(memory_space=pltpu.SMEM), q_spec, k_prev, k_cur, v_prev, v_cur, wide] + host.in_specs,
        out_specs=[wide, wide, pl.BlockSpec((BLOCK, LANE), lambda n: (n, 0))] + host.out_specs,
        out_shape=[jax.ShapeDtypeStruct((t, D_MODEL), F32), jax.ShapeDtypeStruct((t, D_MODEL), BF16),
                   jax.ShapeDtypeStruct((t, LANE), F32)] + host.out_shapes,
        scratch_shapes=[pltpu.VMEM((N_Q_HEADS, BLOCK, 2 * BLOCK), F32)] + host.scratch_shapes,
        compiler_params=_params("arbitrary"))(sm, qkv, qkv, qkv, qkv, qkv, ag_ml, *host.arrays)
    res, landed = host.results(outs, 3)
    return (*res, landed) if ride else tuple(res)


def _attn_bwd(sm, qkv, ag_ml, y, lse, dz, name, ride=None):
    t = qkv.shape[0]
    nb = t // BLOCK
    host = _Hosted(ride, 10, 4, 3)

    def body(*refs):
        host_refs, start, finish = host.split(refs)
        (sm_ref, q_ref, kp_ref, kc_ref, vp_ref, vc_ref, ag_ref, y_ref, lse_ref, dz_ref,
         dq_ref, dkv_ref, dag_ref, ds_ref, ck_ref, cv_ref, bias_ref) = host_refs
        n = pl.program_id(0)
        start(n == 0)
        start.middle(n == (3 * nb) // 4)

        @pl.when(n == 0)
        def _():
            ck_ref[...] = jnp.zeros_like(ck_ref)
            cv_ref[...] = jnp.zeros_like(cv_ref)
            ds_ref[...] = jnp.zeros_like(ds_ref)

        @pl.when(n <= 1)
        def _():
            _fill_bias(bias_ref, sm_ref, n)

        @pl.when(n < nb)
        def _():
            lane8 = lax.broadcasted_iota(jnp.int32, (8, LANE), 1)
            row8 = lax.broadcasted_iota(jnp.int32, (8, LANE), 0)
            dsink = jnp.zeros((8, LANE), F32)
            dk_heads, dv_heads = [], []
            lse_tile = lse_ref[...]
            for hk in range(N_Q_HEADS // GROUP):
                kdup = _kv_dup(kp_ref, kc_ref, hk)
                ks = _kv_dup(kp_ref, kc_ref, hk, ATTN_SCALE)
                vdup = _kv_dup(vp_ref, vc_ref, hk)
                qms, dyhs, y_rows = [], [], []
                for k in (0, 1):
                    c = slice((2 * hk + k) * LANE, (2 * hk + k + 1) * LANE)
                    ag = ag_ref[:, c].astype(F32)
                    sg = _sigmoid(ag)
                    dzt = dz_ref[:, c].astype(F32)
                    yt = y_ref[:, c]
                    dag_ref[:, c] = (dzt * yt * (sg * (1.0 + ag * (1.0 - sg)))).astype(BF16)
                    dyt = dzt * (ag * sg)
                    q2s = q_ref[:, c] * ATTN_SCALE
                    for half in (0, 1):
                        hm = _half_mask(q2s.shape, half)
                        qms.append(jnp.where(hm, q2s, jnp.zeros_like(q2s)))
                        dyhs.append(jnp.where(hm, dyt, 0.0))
                        y_rows.append(yt)
                qm4 = jnp.concatenate(qms, axis=0)
                dy4 = jnp.concatenate(dyhs, axis=0)
                dy4_16 = dy4.astype(BF16)
                s4 = lax.dot_general(qm4, kdup, NT_DIMS, preferred_element_type=F32)
                dp4 = lax.dot_general(dy4_16, vdup, NT_DIMS, preferred_element_type=F32)
                probs16, ds16 = [], []
                for r in range(GROUP):
                    head = GROUP * hk + r
                    rows = slice(r * BLOCK, (r + 1) * BLOCK)
                    lh = lse_tile[:, head:head + 1]
                    probs = jnp.exp(s4[rows] + bias_ref[head] - lh)
                    psink = jnp.exp(sm_ref[0, head] - lh)
                    delta = jnp.sum(dyhs[r] * y_rows[r], axis=1, keepdims=True)
                    ds16.append((probs * (dp4[rows] - delta)).astype(BF16))
                    probs16.append(probs.astype(BF16))
                    dsink = dsink + jnp.where((row8 == 0) & (lane8 == head),
                                              -jnp.sum(psink * delta, axis=0, keepdims=True), 0.0)
                ds4 = jnp.concatenate(ds16, axis=0)
                p4 = jnp.concatenate(probs16, axis=0)
                dq4 = jnp.dot(ds4, ks, preferred_element_type=F32)
                low = _half_mask((BLOCK, LANE), 0)
                for k in (0, 1):
                    c = slice((2 * hk + k) * LANE, (2 * hk + k + 1) * LANE)
                    dq_ref[:, c] = jnp.where(low, dq4[2 * k * BLOCK:(2 * k + 1) * BLOCK],
                                             dq4[(2 * k + 1) * BLOCK:(2 * k + 2) * BLOCK]).astype(BF16)
                dk_acc = lax.dot_general(ds4, qm4, TN_DIMS, preferred_element_type=F32)
                dv_acc = lax.dot_general(p4, dy4_16, TN_DIMS, preferred_element_type=F32)
                dk_heads.append(dk_acc + pltpu.roll(dk_acc, HEAD_DIM, 1))
                dv_heads.append(dv_acc + pltpu.roll(dv_acc, HEAD_DIM, 1))
            ds_ref[...] += dsink
            low = _half_mask((2 * BLOCK, LANE), 0)
            for tile in range(2):
                cols = slice(tile * LANE, (tile + 1) * LANE)
                dkt = jnp.where(low, dk_heads[2 * tile], dk_heads[2 * tile + 1])
                dvt = jnp.where(low, dv_heads[2 * tile], dv_heads[2 * tile + 1])
                dkv_ref[:, cols] = (ck_ref[:, cols] + dkt[0:BLOCK, :]).astype(BF16)
                dkv_ref[:, D_KV + tile * LANE:D_KV + (tile + 1) * LANE] = (cv_ref[:, cols] + dvt[0:BLOCK, :]).astype(BF16)
                ck_ref[:, cols] = dkt[BLOCK:2 * BLOCK, :]
                cv_ref[:, cols] = dvt[BLOCK:2 * BLOCK, :]

        @pl.when(n == nb)
        def _():
            dkv_ref[:, 0:D_KV] = ck_ref[...].astype(BF16)
            dkv_ref[:, D_KV:2 * D_KV] = cv_ref[...].astype(BF16)

        finish(n == nb)

    q_spec, k_prev, k_cur, v_prev, v_cur = _attn_specs(nb, True)
    wide = pl.BlockSpec((BLOCK, D_MODEL), lambda n: (jnp.minimum(n, nb - 1), 0))
    outs = pl.pallas_call(
        body, name=name, grid=(nb + 1,),
        in_specs=[pl.BlockSpec(memory_space=pltpu.SMEM), q_spec, k_prev, k_cur, v_prev, v_cur, wide, wide,
                  pl.BlockSpec((BLOCK, LANE), lambda n: (jnp.minimum(n, nb - 1), 0)), wide] + host.in_specs,
        out_specs=[wide, pl.BlockSpec((BLOCK, 2 * D_KV), lambda n: (jnp.maximum(n - 1, 0), 0)), wide,
                   pl.BlockSpec((8, LANE), lambda n: (0, 0))] + host.out_specs,
        out_shape=[jax.ShapeDtypeStruct((t, D_MODEL), BF16), jax.ShapeDtypeStruct((t, 2 * D_KV), BF16),
                   jax.ShapeDtypeStruct((t, D_MODEL), BF16), jax.ShapeDtypeStruct((8, LANE), F32)] + host.out_shapes,
        scratch_shapes=[pltpu.VMEM((BLOCK, D_KV), F32), pltpu.VMEM((BLOCK, D_KV), F32),
                        pltpu.VMEM((N_Q_HEADS, BLOCK, 2 * BLOCK), F32)] + host.scratch_shapes,
        compiler_params=_params("arbitrary"))(sm, qkv, qkv, qkv, qkv, qkv, ag_ml, y, lse, dz, *host.arrays)
    res, landed = host.results(outs, 4)
    return (*res, landed) if ride else tuple(res)


def _local_grads(x, target, p, project, late_weights=None, reduce_out=None, reduce_in=None):
    h, rx_rg, qkv, ag_ml, wt = project(x, p["pre_g"])
    if late_weights is None:
        y_attn, z_attn, lse = _attn_fwd(p["sm"], qkv, ag_ml, "attn_fwd")
    else:
        y_attn, z_attn, lse, landed = _attn_fwd(p["sm"], qkv, ag_ml, "attn_fwd", ride=late_weights[0])
        p = {**p, **late_weights[1](landed)}
    lru = (p["cw"], p["cb"], p["wbd_a"], p["b_a"], p["wbd_x"], p["b_x"], p["lam"])
    y_rnn, z_rnn = _rnn_fwd(rx_rg, *lru, "rnn_fwd")
    br_rnn, br_attn, merged, dout, dy, st_post = _branches_fwd(
        z_rnn, z_attn, ag_ml, p["b_gate"], p["w_rnn"], p["w_attn"], p["w_out"], x, target, p["post_g"], "branches_fwd")

    dbr_rnn, dbr_attn, d_ml, dz_rnn, dz_attn, st_merge = _branches_bwd(
        dout, br_rnn, br_attn, ag_ml, p["b_gate"], p["w_rnn"], p["w_attn"], p["w_out"], "branches_bwd")
    out_prods = [(z_rnn, dbr_rnn), (z_attn, dbr_attn), (merged, dout)]
    gw_rnn = gw_attn = gw_out = red_out = red_in = None
    if reduce_out is None:
        gw_rnn, gw_attn, gw_out = (_mm_tn(a, b, nm) for (a, b), nm in zip(out_prods, ("gw_rnn", "gw_attn", "gw_out")))
    else:
        out_pairs = _mm_tn_pairs(out_prods, "gw_outs")
    d_rx, d_rg, st_rnn, g_rg_a, g_rg_x = _rnn_bwd(rx_rg, y_rnn, dz_rnn, *lru, "rnn_bwd")
    if reduce_out is None:
        dq, dkv, d_ag, st_sink = _attn_bwd(p["sm"], qkv, ag_ml, y_attn, lse, dz_attn, "attn_bwd")
    else:
        dq, dkv, d_ag, st_sink, red_out = _attn_bwd(p["sm"], qkv, ag_ml, y_attn, lse, dz_attn, "attn_bwd",
                                                    ride=reduce_out(out_pairs, g_rg_a, g_rg_x))

    segs = [d_rx, d_rg, dq, dkv, d_ag, d_ml]
    if reduce_in is None:
        gwt = _mm_tn_seg(segs, h, "gw_in")
        grad_x, st_pre = _input_grad(segs, wt, x, p["pre_g"], dy, "input_grad")
    else:
        gwt = None
        grad_x, st_pre, red_in = _input_grad(segs, wt, x, p["pre_g"], dy, "input_grad",
                                             ride=reduce_in(*_mm_tn_seg_pair(segs, h, "gw_in")))
    return dict(grad_x=grad_x, gwt=gwt, gw_rnn=gw_rnn, gw_attn=gw_attn, gw_out=gw_out,
                st_post=st_post, st_merge=st_merge, st_rnn=st_rnn, st_sink=st_sink, st_pre=st_pre,
                g_rg_a=g_rg_a, g_rg_x=g_rg_x, red_out=red_out, red_in=red_in)


def _place():
    x, y, c = lax.axis_index("x"), lax.axis_index("y"), lax.axis_index("c")
    return x, y, c


def _gather_ride(shards):
    n = len(shards)

    def copies(ins, outs, sems):
        send_sems, recv_sems, local_sems = sems
        x, y, c = _place()
        me, sibling = (x, y, c), (x, y, 1 - c)
        chips = [(1 - x, y), (x, 1 - y), (1 - x, 1 - y)]

        def slot(a, dev):
            return outs[a].at[4 * dev[0] + 2 * dev[1] + dev[2]]

        def copy(a, k, block, to, src=None):
            return pltpu.make_async_remote_copy(
                src_ref=slot(a, block) if src is None else src, dst_ref=slot(a, block),
                send_sem=send_sems.at[a, k], recv_sem=recv_sems.at[a, k], device_id=to, device_id_type=MESH)

        mine = [pltpu.make_async_copy(ins[a], slot(a, me), local_sems.at[a]) for a in range(n)]
        first = []
        for a in range(n):
            first.append(copy(a, 0, me, sibling, src=ins[a]))
            first += [copy(a, 1 + j, me, (*chip, c), src=ins[a]) for j, chip in enumerate(chips)]
        return me, sibling, chips, c, copy, mine, first

    def start(ins, outs, sems):
        *_, mine, first = copies(ins, outs, sems)
        for cp in mine + first:
            cp.start()

    def middle(ins, outs, sems):
        me, sibling, chips, c, copy, _, _ = copies(ins, outs, sems)
        for j, chip in enumerate(chips):
            for a in range(n):
                copy(a, 1 + j, (*chip, c), me).wait_recv()
                copy(a, 4 + j, (*chip, c), sibling).start()

    def finish(ins, outs, sems):
        me, sibling, chips, c, copy, mine, first = copies(ins, outs, sems)
        passed = [copy(a, 4 + j, (*chip, c), sibling) for j, chip in enumerate(chips) for a in range(n)]
        for a in range(n):
            copy(a, 0, sibling, me).wait_recv()
            for j, chip in enumerate(chips):
                copy(a, 4 + j, (*chip, 1 - c), me).wait_recv()
        for cp in first + passed:
            cp.wait_send()
        for cp in mine:
            cp.wait()

    return _Ride(
        shards, [jax.ShapeDtypeStruct((N_DEV, *s.shape), s.dtype) for s in shards],
        [pltpu.SemaphoreType.DMA((n, 7)), pltpu.SemaphoreType.DMA((n, 7)), pltpu.SemaphoreType.DMA((n,))],
        start, finish, middle)


def _chips_ride(scatter, whole):
    ns, nw = len(scatter), len(whole)
    n = ns + nw

    def copies(ins, outs, sems):
        send_sems, recv_sems, local_sems = sems
        x, y, c = _place()
        own = 2 * x + y
        chips = [(1 - x, y), (x, 1 - y), (1 - x, 1 - y)]

        def src(a, chip_idx):
            return ins[a].at[chip_idx] if a < ns else ins[a]

        local = [pltpu.make_async_copy(src(a, own), outs[a].at[own], local_sems.at[a]) for a in range(n)]
        sent = [pltpu.make_async_remote_copy(
            src_ref=src(a, 2 * chip[0] + chip[1]), dst_ref=outs[a].at[own],
            send_sem=send_sems.at[a, j], recv_sem=recv_sems.at[a, own], device_id=(*chip, c), device_id_type=MESH)
            for a in range(n) for j, chip in enumerate(chips)]
        return chips, c, local, sent

    def start(ins, outs, sems):
        _, _, local, sent = copies(ins, outs, sems)
        for cp in local + sent:
            cp.start()

    def finish(ins, outs, sems):
        send_sems, recv_sems, _ = sems
        chips, c, local, sent = copies(ins, outs, sems)
        for a in range(n):
            for chip in chips:
                k = 2 * chip[0] + chip[1]
                pltpu.make_async_remote_copy(
                    src_ref=outs[a].at[k], dst_ref=outs[a].at[k], send_sem=send_sems.at[a, 0],
                    recv_sem=recv_sems.at[a, k], device_id=(*chip, c), device_id_type=MESH).wait_recv()
        for cp in sent:
            cp.wait_send()
        for cp in local:
            cp.wait()

    return _Ride(
        list(scatter) + list(whole),
        [jax.ShapeDtypeStruct(s.shape, s.dtype) for s in scatter]
        + [jax.ShapeDtypeStruct((N_CHIP, *s.shape), s.dtype) for s in whole],
        [pltpu.SemaphoreType.DMA((n, 3)), pltpu.SemaphoreType.DMA((n, N_CHIP)), pltpu.SemaphoreType.DMA((n,))],
        start, finish)


def _chips_rest_ride(pair, red):
    table = _side_pieces()
    nc = len(CHUNK_ORDER)

    def each(ins, outs, sems, sending, landing):
        send_sems, recv_sems, _ = sems
        pair_ref, red_ref = ins[0], outs[0]
        x, y, c = _place()
        own = 2 * x + y
        for core in (0, 1):
            mine = table[core][0]
            rest = [s for s in range(EARLY_STEPS, nc) if mine[s] is not None]

            @pl.when(c == core)
            def _(mine=mine, rest=rest, core=core):
                for s in rest:
                    _, rows, at = mine[s]
                    k, cp = _to_chip(pair_ref.at[at // SHARD_IN, pl.ds(at % SHARD_IN, rows)], red_ref, mine[s], s,
                                     send_sems, recv_sems, own, core)
                    pl.when(own != k)(lambda cp=cp: sending(cp))
                if landing is not None:
                    for k in range(N_CHIP):
                        @pl.when(own == k)
                        def _(k=k):
                            for s in rest:
                                _, rows, at = mine[s]
                                if at // SHARD_IN == k:
                                    for chip in range(N_CHIP):
                                        if chip != k:
                                            landing(_from_chip(pair_ref.at[k, pl.ds(at % SHARD_IN, rows)], red_ref, mine[s], s,
                                                               send_sems, recv_sems, chip, (x, y, c)))

    def local(ins, outs, sems):
        x, y, _ = _place()
        return pltpu.make_async_copy(ins[0].at[2 * x + y], outs[0].at[2 * x + y], sems[2])

    def start(ins, outs, sems):
        local(ins, outs, sems).start()
        each(ins, outs, sems, lambda cp: cp.start(), None)

    def finish(ins, outs, sems):
        each(ins, outs, sems, lambda cp: cp.wait_send(), lambda cp: cp.wait_recv())
        local(ins, outs, sems).wait()

    return _Ride([pair, red], [jax.ShapeDtypeStruct(red.shape, red.dtype)],
                 [pltpu.SemaphoreType.DMA((nc,)), pltpu.SemaphoreType.DMA((N_CHIP, nc)), pltpu.SemaphoreType.DMA],
                 start, finish, aliases={1: 0})


def _allreduce_small(pack, name):
    shape = pack.shape

    def body(x_ref, o_ref, sib_ref, chip_ref, send_sems, recv_sems):
        x, y, c = _place()
        own = 2 * x + y
        chips = [(1 - x, y), (x, 1 - y), (1 - x, 1 - y)]
        to_sibling = pltpu.make_async_remote_copy(
            src_ref=x_ref, dst_ref=sib_ref, send_sem=send_sems.at[0], recv_sem=recv_sems.at[0],
            device_id=(x, y, 1 - c), device_id_type=MESH)
        to_sibling.start()
        to_sibling.wait()
        chip_ref[own] = x_ref[...] + sib_ref[...]
        sent = [pltpu.make_async_remote_copy(
            src_ref=chip_ref.at[own], dst_ref=chip_ref.at[own], send_sem=send_sems.at[1 + j],
            recv_sem=recv_sems.at[1 + own], device_id=(*chip, c), device_id_type=MESH) for j, chip in enumerate(chips)]
        for cp in sent:
            cp.start()
        for chip in chips:
            k = 2 * chip[0] + chip[1]
            pltpu.make_async_remote_copy(
                src_ref=chip_ref.at[k], dst_ref=chip_ref.at[k], send_sem=send_sems.at[1],
                recv_sem=recv_sems.at[1 + k], device_id=(*chip, c), device_id_type=MESH).wait_recv()
        for cp in sent:
            cp.wait_send()
        o_ref[...] = (chip_ref[0] + chip_ref[1]) + (chip_ref[2] + chip_ref[3])

    return pl.pallas_call(
        body, name=name, out_shape=jax.ShapeDtypeStruct(shape, F32),
        in_specs=[pl.BlockSpec(memory_space=pltpu.VMEM)], out_specs=pl.BlockSpec(memory_space=pltpu.VMEM),
        scratch_shapes=[pltpu.VMEM(shape, F32), pltpu.VMEM((N_CHIP, *shape), F32),
                        pltpu.SemaphoreType.DMA((4,)), pltpu.SemaphoreType.DMA((1 + N_CHIP,))],
    )(pack)


def _adamw(g, w, m, v):
    m = ADAM_B1 * m + (1.0 - ADAM_B1) * g
    v = ADAM_B2 * v + (1.0 - ADAM_B2) * (g * g)
    m_hat = m / (1.0 - ADAM_B1 ** ADAM_STEP)
    v_hat = v / (1.0 - ADAM_B2 ** ADAM_STEP)
    delta = -ADAM_LR * (m_hat / (jnp.sqrt(v_hat) + ADAM_EPS) + ADAM_WD * w)
    return delta, m, v


def _adam_parts(items, name, tr=None):
    ni = len(items)
    npart, r, c = items[0][0].shape
    tr = r if tr is None else min(tr, r)

    def body(*refs):
        for a in range(ni):
            p_ref, w_ref, m_ref, v_ref = refs[4 * a:4 * a + 4]
            g_ref, d_ref, nm_ref, nv_ref = refs[4 * (ni + a):4 * (ni + a) + 4]
            g = p_ref[0].astype(F32)
            for k in range(1, npart):
                g = g + p_ref[k].astype(F32)
            g_ref[...] = g
            d_ref[...], nm_ref[...], nv_ref[...] = _adamw(g, w_ref[...], m_ref[...], v_ref[...])

    tile = pl.BlockSpec((tr, c), lambda i: (i, 0))
    outs = pl.pallas_call(
        body, name=name, grid=(r // tr,),
        in_specs=[pl.BlockSpec((npart, tr, c), lambda i: (0, i, 0)), tile, tile, tile] * ni,
        out_specs=[tile] * (4 * ni), out_shape=[jax.ShapeDtypeStruct((r, c), F32)] * (4 * ni),
        compiler_params=_params("parallel"))(*[arr for item in items for arr in item])
    return [outs[4 * a:4 * a + 4] for a in range(ni)]


def _block_diag(w):
    w4 = w.reshape(N_GROUPS, 4, RNN_BLOCK_W, RNN_BLOCK_W)
    eye = jnp.eye(4, dtype=w.dtype)
    return jnp.einsum("gbij,bc->gbicj", w4, eye).reshape(N_GROUPS, GROUP_W, GROUP_W).astype(BF16)


SMALL_ROWS = 16
ROW_PRE_G, ROW_BGATE, ROW_CONV_B, ROW_B_A, ROW_B_X, ROW_LAM, ROW_POST_G, ROW_LOSS, ROW_SINKS, ROW_CONV_W = 0, 1, 3, 4, 5, 6, 7, 8, 9, 10


def _pack_stats(st_pre, st_merge, st_rnn, st_post, st_sink, name):
    d = D_MODEL

    def body(pre_ref, mg_ref, rnn_ref, post_ref, sink_ref, o_ref):
        rnn = rnn_ref[...]
        sinks = jnp.concatenate([sink_ref[0:1, :], jnp.zeros((1, d - LANE), F32)], axis=1)
        o_ref[0:8, :] = _rows8([pre_ref[0:1, :], mg_ref[0:1, :], mg_ref[1:2, :], rnn[0:1], rnn[1:2], rnn[2:3], rnn[3:4],
                                post_ref[0:1, :]], d)
        o_ref[8:16, :] = _rows8([post_ref[1:2, :], sinks, rnn[4:5], rnn[5:6], rnn[6:7], rnn[7:8]], d)

    return pl.pallas_call(body, name=name, out_shape=jax.ShapeDtypeStruct((SMALL_ROWS, d), F32))(
        st_pre, st_merge, st_rnn, st_post, st_sink)


def _adam_small(total, w, m, v, name):
    d = D_MODEL
    n_in = len(w)
    rows = (ROW_PRE_G, ROW_BGATE, ROW_CONV_B, ROW_B_A, ROW_B_X, ROW_LAM, ROW_POST_G, ROW_SINKS)

    def grad_of(p_ref, row, shape):
        if shape == (1, 2 * d):
            return jnp.concatenate([p_ref[row:row + 1, :], p_ref[row + 1:row + 2, :]], axis=1)
        if shape == (1, RNN_BLOCKS, RNN_BLOCK_W):
            r = p_ref[row:row + 1, :]
            return jnp.concatenate([r[:, b * RNN_BLOCK_W:(b + 1) * RNN_BLOCK_W] for b in range(RNN_BLOCKS)], axis=0)[None]
        return p_ref[row:row + 1, 0:shape[1]]

    def body(*refs):
        p_ref = refs[0]
        w_refs, m_refs, v_refs = (refs[1 + k * n_in:1 + (k + 1) * n_in] for k in range(3))
        outs = refs[1 + 3 * n_in:]
        for k in range(n_in):
            g = grad_of(p_ref, rows[k], w[k].shape)
            res = (g,) + _adamw(g, w_refs[k][...], m_refs[k][...], v_refs[k][...])
            for kind in range(4):
                outs[kind * n_in + k][...] = res[kind]

    outs = pl.pallas_call(
        body, name=name, out_shape=[jax.ShapeDtypeStruct(a.shape, F32) for _ in range(4) for a in w])(total, *w, *m, *v)
    return [outs[kind * n_in:(kind + 1) * n_in] for kind in range(4)]


def kernel(x, pre_norm_g, w_in, b_gate, conv_w, conv_b, w_rg_a, b_rg_a, w_rg_x, b_rg_x, lru_lambda, attn_sinks, w_rnn_out, w_attn_out, w_out, post_norm_g, loss_target, m_pre_norm_g, m_w_in, m_b_gate, m_conv_w, m_conv_b, m_w_rg_a, m_b_rg_a, m_w_rg_x, m_b_rg_x, m_lru_lambda, m_attn_sinks, m_w_rnn_out, m_w_attn_out, m_w_out, m_post_norm_g, v_pre_norm_g, v_w_in, v_b_gate, v_conv_w, v_conv_b, v_w_rg_a, v_b_rg_a, v_w_rg_x, v_b_rg_x, v_lru_lambda, v_attn_sinks, v_w_rnn_out, v_w_attn_out, v_w_out, v_post_norm_g):
    cx, cy, cc = _place()
    dev = 4 * cx + 2 * cy + cc

    w_in_t, m_in_t, v_in_t = (jnp.transpose(a[0]) for a in (w_in, m_w_in, v_w_in))
    wt_shard = w_in_t.astype(BF16)

    def project(xs, pre_g):
        h, rx_rg, qkv, ag_ml, wt_all = _gather_project(xs, pre_g, wt_shard, "gather_project")
        return h, rx_rg, qkv, ag_ml, wt_all.reshape(D_IN, D_MODEL)

    def late_unpack(landed):
        w_rnn_all, w_attn_all, w_out_all, cw_all = landed
        return dict(w_rnn=w_rnn_all.reshape(D_RNN, D_MODEL), w_attn=w_attn_all.reshape(D_MODEL, D_MODEL),
                    w_out=w_out_all.reshape(D_MODEL, D_MODEL), cw=jnp.transpose(cw_all, (1, 0, 2)).reshape(4, D_RNN))

    late_weights = (_gather_ride([w_rnn_out[0].astype(BF16), w_attn_out[0].astype(BF16), w_out[0].astype(BF16), conv_w[0]]),
                    late_unpack)

    heads = jnp.arange(1, N_Q_HEADS + 1, dtype=F32)
    slopes = jnp.exp2(-ALIBI_MAX_BIAS * heads / N_Q_HEADS)
    b_a = b_rg_a.reshape(1, D_RNN)
    b_x = b_rg_x.reshape(1, D_RNN)
    p = dict(
        pre_g=pre_norm_g, post_g=post_norm_g, b_gate=b_gate, cb=conv_b,
        wbd_a=_block_diag(w_rg_a[0]), b_a=b_a, wbd_x=_block_diag(w_rg_x[0]), b_x=b_x, lam=lru_lambda,
        sm=jnp.pad(attn_sinks, ((0, 1), (0, 0))) + jnp.pad(slopes[None, :], ((1, 0), (0, 0))))

    flat = (RNN_BLOCKS * RNN_BLOCK_W, RNN_BLOCK_W)

    def reduce_out(out_pairs, g_rg_a, g_rg_x):
        return _join_rides(_chips_ride(out_pairs, []), _gather_ride([g_rg_a.reshape(flat), g_rg_x.reshape(flat)]))

    reduce_in = _chips_rest_ride

    g = _local_grads(x[0], loss_target[0], p, project, late_weights, reduce_out, reduce_in)
    small = _allreduce_small(
        _pack_stats(g["st_pre"], g["st_merge"], g["st_rnn"], g["st_post"], g["st_sink"], "pack_stats"), "allreduce_small")

    out = {}
    red = g["red_out"]
    w_in_out, = _adam_parts([(g["red_in"][0], w_in_t, m_in_t, v_in_t)], "adam_w_in", tr=SHARD_IN // 2)
    out["w_in"] = [jnp.transpose(o) for o in w_in_out]
    out["w_rnn_out"], out["w_attn_out"], out["w_out"] = _adam_parts(
        [(red[0], w_rnn_out[0], m_w_rnn_out[0], v_w_rnn_out[0]), (red[1], w_attn_out[0], m_w_attn_out[0], v_w_attn_out[0]),
         (red[2], w_out[0], m_w_out[0], v_w_out[0])], "adam_w_outs")
    out["w_rg_a"], out["w_rg_x"] = _adam_parts(
        [(red[3], w_rg_a.reshape(flat), m_w_rg_a.reshape(flat), v_w_rg_a.reshape(flat)),
         (red[4], w_rg_x.reshape(flat), m_w_rg_x.reshape(flat), v_w_rg_x.reshape(flat))], "adam_w_rg", tr=256)

    small_names = ("pre_norm_g", "b_gate", "conv_b", "b_rg_a", "b_rg_x", "lru_lambda", "post_norm_g", "attn_sinks")
    small_out = _adam_small(
        small,
        (pre_norm_g, b_gate, conv_b, b_rg_a, b_rg_x, lru_lambda, post_norm_g, attn_sinks),
        (m_pre_norm_g, m_b_gate, m_conv_b, m_b_rg_a, m_b_rg_x, m_lru_lambda, m_post_norm_g, m_attn_sinks),
        (v_pre_norm_g, v_b_gate, v_conv_b, v_b_rg_a, v_b_rg_x, v_lru_lambda, v_post_norm_g, v_attn_sinks),
        "adam_small")
    g_cw = lax.dynamic_slice(small[ROW_CONV_W:ROW_CONV_W + 4], (0, dev * SHARD_OUT), (4, SHARD_OUT))
    out["conv_w"], = _adam_parts([(g_cw[None], conv_w[0], m_conv_w[0], v_conv_w[0])], "adam_conv_w")

    shapes = dict(w_in=(1, D_MODEL, SHARD_IN), w_rnn_out=(1, SHARD_OUT, D_MODEL), w_attn_out=(1, SHARD_OUT, D_MODEL),
                  w_out=(1, SHARD_OUT, D_MODEL), w_rg_a=(1, RNN_BLOCKS, RNN_BLOCK_W, RNN_BLOCK_W),
                  w_rg_x=(1, RNN_BLOCKS, RNN_BLOCK_W, RNN_BLOCK_W), conv_w=(1, 4, SHARD_OUT))
    weights = ["pre_norm_g", "w_in", "b_gate", "conv_w", "conv_b", "w_rg_a", "b_rg_a", "w_rg_x", "b_rg_x",
               "lru_lambda", "attn_sinks", "w_rnn_out", "w_attn_out", "w_out", "post_norm_g"]
    results = []
    for kind in range(4):
        for name in weights:
            if name in out:
                results.append(out[name][kind].reshape(shapes[name]))
            else:
                results.append(small_out[kind][small_names.index(name)])
    loss = 0.5 / D_MODEL * jnp.sum(small[ROW_LOSS])
    return (loss, g["grad_x"][None], *results)
```

```python
import functools

import jax
import jax.numpy as jnp
from jax import lax
from jax.experimental import pallas as pl
from jax.experimental.pallas import tpu as pltpu

F32 = jnp.float32
BF16 = jnp.bfloat16

D_MODEL = 1024
D_RNN = 1024
RNN_BLOCKS = 16
RNN_BLOCK_W = 64
LRU_C = 8.0
N_Q_HEADS = 16
HEAD_DIM = 64
D_KV = 256
BLOCK = 128
ALIBI_MAX_BIAS = 8.0
EPS = 1e-6
D_IN = 6656
N_DEV = 8
N_CHIP = 4
SHARD_IN = D_IN // N_DEV
SHARD_OUT = D_MODEL // N_DEV
ATTN_SCALE = HEAD_DIM ** -0.5
MASKED = -1e30

ADAM_LR = 0.001
ADAM_B1 = 0.9
ADAM_B2 = 0.999
ADAM_EPS = 1e-08
ADAM_WD = 0.01
ADAM_STEP = 10

VMEM_LIMIT_BYTES = 52 * 1024 * 1024
LANE = 128
GROUP_W = 256
N_GROUPS = D_RNN // GROUP_W
SEG_CHUNK = 512

NT_DIMS = (((1,), (1,)), ((), ()))
TN_DIMS = (((0,), (0,)), ((), ()))
MESH = pl.DeviceIdType.MESH
ANY = pl.BlockSpec(memory_space=pl.ANY)


def _params(*semantics):
    return pltpu.CompilerParams(dimension_semantics=semantics, vmem_limit_bytes=VMEM_LIMIT_BYTES)


def _sigmoid(x):
    return 0.5 * jnp.tanh(0.5 * x) + 0.5


def _log1p(e):
    u = 1.0 + e
    den = jnp.where(u == 1.0, 1.0, u - 1.0)
    return jnp.where(u == 1.0, e, jnp.log(u) * (e / den))


def _softplus(z):
    return jnp.maximum(z, 0.0) + _log1p(jnp.exp(-jnp.abs(z)))


def _rows8(rows, width):
    idx = lax.broadcasted_iota(jnp.int32, (8, width), 0)
    out = jnp.zeros((8, width), F32)
    for r, v in enumerate(rows):
        out = jnp.where(idx == r, v, out)
    return out


class _Ride:
    def __init__(self, arrays, out_shapes, scratch_shapes, start, finish, middle=None, aliases=None):
        self.arrays, self.out_shapes, self.scratch_shapes = list(arrays), list(out_shapes), list(scratch_shapes)
        self.start, self.finish, self.middle = start, finish, middle
        self.aliases = dict(aliases or {})


class _Hosted:
    def __init__(self, ride, n_in, n_out, n_scratch=0, aliasing=False):
        self.ride = ride
        assert aliasing or not (ride and ride.aliases), "this host does not alias"
        self.aliases = {n_in + i: n_out + o for i, o in ride.aliases.items()} if ride else {}
        self.sizes = (n_in, len(ride.arrays) if ride else 0, n_out, len(ride.out_shapes) if ride else 0, n_scratch)
        self.arrays = ride.arrays if ride else []
        self.in_specs = [ANY] * len(self.arrays)
        self.out_shapes = ride.out_shapes if ride else []
        self.out_specs = [ANY] * len(self.out_shapes)
        self.scratch_shapes = ride.scratch_shapes if ride else []

    def split(self, refs):
        n_in, r_in, n_out, r_out, n_scr = self.sizes
        cuts = [0, n_in, n_in + r_in, n_in + r_in + n_out, n_in + r_in + n_out + r_out, n_in + r_in + n_out + r_out + n_scr]
        host_in, ride_in, host_out, ride_out, host_scr = (refs[cuts[k]:cuts[k + 1]] for k in range(5))
        ride_scr = refs[cuts[5]:]

        def start(when):
            if self.ride is not None:
                pl.when(when)(lambda: self.ride.start(ride_in, ride_out, ride_scr))

        def finish(when):
            if self.ride is not None:
                pl.when(when)(lambda: self.ride.finish(ride_in, ride_out, ride_scr))

        def middle(when):
            if self.ride is not None and self.ride.middle is not None:
                pl.when(when)(lambda: self.ride.middle(ride_in, ride_out, ride_scr))

        start.middle = middle
        return tuple(host_in) + tuple(host_out) + tuple(host_scr), start, finish

    def results(self, outs, n_out):
        outs = list(outs) if isinstance(outs, (list, tuple)) else [outs]
        return outs[:n_out], outs[n_out:]


def _join_rides(a, b):
    na, nb_ = len(a.arrays), len(b.arrays)
    oa = len(a.out_shapes)
    sa = len(a.scratch_shapes)

    def both(fa, fb):
        def run(ins, outs, sems):
            if fa is not None:
                fa(ins[:na], outs[:oa], sems[:sa])
            if fb is not None:
                fb(ins[na:na + nb_], outs[oa:], sems[sa:])
        return run

    middle = both(a.middle, b.middle) if (a.middle or b.middle) else None
    return _Ride(a.arrays + b.arrays, a.out_shapes + b.out_shapes, a.scratch_shapes + b.scratch_shapes,
                 both(a.start, b.start), both(a.finish, b.finish), middle)


def _load_resident(w_hbm, w_vmem, sems, first):
    def piece(c):
        rows = pl.ds(c * SEG_CHUNK, SEG_CHUNK)
        return pltpu.make_async_copy(w_hbm.at[rows], w_vmem.at[rows], sems.at[c])

    @pl.when(first)
    def _():
        for c in range(w_vmem.shape[0] // SEG_CHUNK):
            piece(c).start()

    def ready(c):
        @pl.when(first)
        def _():
            piece(c).wait()

    return ready


CHIP_ROWS = 2 * SHARD_IN
PROJ_WIDTHS = (2 * D_RNN, D_MODEL + 2 * D_KV, 3 * D_MODEL)
PROJ_DTYPES = (F32, BF16, BF16)


def _chip_pieces():
    starts = [0, PROJ_WIDTHS[0], PROJ_WIDTHS[0] + PROJ_WIDTHS[1], D_IN]
    pieces = []
    for k in range(N_CHIP):
        lo, hi = k * CHIP_ROWS, (k + 1) * CHIP_ROWS
        cur = []
        for a in range(len(PROJ_WIDTHS)):
            s0, s1 = max(lo, starts[a]), min(hi, starts[a + 1])
            if s0 < s1:
                cur.append((a, s0 - starts[a], s1 - s0, s0 - lo))
        pieces.append(cur)
    return pieces


def _gather_project(x, g, wt_shard, name, tm=1024):
    t, k = x.shape
    tm = min(tm, t)
    nt = t // tm
    pieces = _chip_pieces()

    def body(x_ref, g_ref, shard_ref, h_out, rx_ref, qkv_ref, ag_ref, wt_all,
             w_c, stage, o32, o16, h_all, send_sems, recv_sems, local_sem, stage_sems, out_sems, h_sems):
        s, ti = pl.program_id(0), pl.program_id(1)
        px, py, pc = _place()
        me, sibling = (px, py, pc), (px, py, 1 - pc)
        chips = [(px, py), (1 - px, py), (px, 1 - py), (1 - px, 1 - py)]
        outs = (rx_ref, qkv_ref, ag_ref)

        def slot(dev):
            return wt_all.at[4 * dev[0] + 2 * dev[1] + dev[2]]

        def copy(kk, block, to, src=None):
            return pltpu.make_async_remote_copy(
                src_ref=slot(block) if src is None else src, dst_ref=slot(block),
                send_sem=send_sems.at[kk], recv_sem=recv_sems.at[kk], device_id=to, device_id_type=MESH)

        mine = pltpu.make_async_copy(shard_ref, slot(me), local_sem)
        first = [copy(0, me, sibling, src=shard_ref)] + [copy(1 + j, me, (*chips[1 + j], pc), src=shard_ref) for j in range(3)]
        passed = [copy(4 + j, (*chips[1 + j], pc), sibling) for j in range(3)]

        @pl.when((s == 0) & (ti == 0))
        def _():
            mine.start()
            for cp in first[:3]:
                cp.start()

        def pass_on(step):
            copy(step, (*chips[step], pc), me).wait_recv()
            passed[step - 1].start()

        @pl.when((s == 2) & (ti == nt - 1))
        def _():
            pass_on(3)

        for step in range(N_CHIP):
            @pl.when((s == step) & (ti == 0))
            def _(step=step):
                chip = chips[step]
                if step == 0:
                    mine.wait()
                    copy(0, sibling, me).wait_recv()
                else:
                    if step == 1:
                        pass_on(1)
                        first[3].start()
                        pass_on(2)
                    copy(3 + step, (*chip, 1 - pc), me).wait_recv()
                loads = [pltpu.make_async_copy(slot((*chip, core)), stage.at[pl.ds(core * SHARD_IN, SHARD_IN)], stage_sems.at[core])
                         for core in (0, 1)]
                for cp in loads:
                    cp.start()
                for cp in loads:
                    cp.wait()
                w_c[...] = stage[...].T

        rows = pl.ds(pl.multiple_of(ti * tm, tm), tm)

        @pl.when(s == 0)
        def _():
            xv = x_ref[...]
            h_all[rows, :] = (xv * lax.rsqrt(jnp.mean(xv * xv, axis=-1, keepdims=True) + EPS) * g_ref[...]).astype(BF16)

        res = jnp.dot(h_all[rows, :], w_c[...], preferred_element_type=F32)

        n = s * nt + ti
        buf = lax.rem(n, 2)
        chip_idx = [2 * cx + cy for cx, cy in chips]

        def chip_at(step):
            return jnp.where(step == 0, chip_idx[0], jnp.where(step == 1, chip_idx[1], jnp.where(step == 2, chip_idx[2], chip_idx[3])))

        def h_write(b, tile):
            return pltpu.make_async_copy(h_all.at[pl.ds(tile * tm, tm)], h_out.at[pl.ds(tile * tm, tm)], h_sems.at[b])

        def writes(kchip, b, tile):
            cps = []
            for idx, (a, col, w, src) in enumerate(pieces[kchip]):
                staged = (o16 if PROJ_DTYPES[a] == BF16 else o32).at[b, :, pl.ds(src, w)]
                cps.append(pltpu.make_async_copy(staged, outs[a].at[pl.ds(tile * tm, tm), pl.ds(col, w)], out_sems.at[b, idx]))
            return cps

        o32[buf] = res
        o16[buf] = res.astype(BF16)
        kcur = chip_at(s)

        @pl.when(n > 0)
        def _():
            kprev = chip_at(lax.div(n - 1, nt))
            for kchip in range(N_CHIP):
                @pl.when(kprev == kchip)
                def _(kchip=kchip):
                    for cp in writes(kchip, 1 - buf, lax.rem(n - 1, nt)):
                        cp.wait()

            @pl.when(n <= nt)
            def _():
                h_write(1 - buf, n - 1).wait()

        @pl.when(s == 0)
        def _():
            h_write(buf, ti).start()

        for kchip in range(N_CHIP):
            @pl.when(kcur == kchip)
            def _(kchip=kchip):
                for cp in writes(kchip, buf, ti):
                    cp.start()

        @pl.when(n == N_CHIP * nt - 1)
        def _():
            for kchip in range(N_CHIP):
                @pl.when(kcur == kchip)
                def _(kchip=kchip):
                    for cp in writes(kchip, buf, ti):
                        cp.wait()
            for cp in first + passed:
                cp.wait_send()

    tile = pl.BlockSpec((tm, k), lambda s, ti: (jnp.where(s == 0, ti, nt - 1), 0))
    return pl.pallas_call(
        body, name=name, grid=(N_CHIP, nt),
        in_specs=[tile, pl.BlockSpec((1, k), lambda s, ti: (0, 0)), ANY],
        out_specs=[ANY, ANY, ANY, ANY, ANY],
        out_shape=[jax.ShapeDtypeStruct((t, k), BF16)]
        + [jax.ShapeDtypeStruct((t, w), dt) for w, dt in zip(PROJ_WIDTHS, PROJ_DTYPES)]
        + [jax.ShapeDtypeStruct((N_DEV, SHARD_IN, k), BF16)],
        scratch_shapes=[pltpu.VMEM((k, CHIP_ROWS), BF16), pltpu.VMEM((CHIP_ROWS, k), BF16),
                        pltpu.VMEM((2, tm, CHIP_ROWS), F32), pltpu.VMEM((2, tm, CHIP_ROWS), BF16), pltpu.VMEM((t, k), BF16),
                        pltpu.SemaphoreType.DMA((7,)), pltpu.SemaphoreType.DMA((7,)), pltpu.SemaphoreType.DMA,
                        pltpu.SemaphoreType.DMA((2,)), pltpu.SemaphoreType.DMA((2, 2)), pltpu.SemaphoreType.DMA((2,))],
        compiler_params=_params("arbitrary", "arbitrary"))(x, g, wt_shard)


def _mm_tn(a, b, name, tm=512, tk=4096):
    ktok, m = a.shape
    n = b.shape[1]
    tk = min(tk, ktok)

    def body(a_ref, b_ref, o_ref):
        @pl.when(pl.program_id(1) == 0)
        def _():
            o_ref[...] = jnp.zeros_like(o_ref)

        o_ref[...] += lax.dot_general(a_ref[...], b_ref[...], TN_DIMS, preferred_element_type=F32)

    return pl.pallas_call(
        body, name=name, grid=(m // tm, ktok // tk),
        in_specs=[pl.BlockSpec((tk, tm), lambda i, kk: (kk, i)), pl.BlockSpec((tk, n), lambda i, kk: (kk, 0))],
        out_specs=pl.BlockSpec((tm, n), lambda i, kk: (i, 0)),
        out_shape=jax.ShapeDtypeStruct((m, n), F32),
        compiler_params=_params("parallel", "arbitrary"))(a, b)


def _mm_tn_pairs(prods, name):
    n_prod = len(prods)
    ktok, m = prods[0][0].shape
    n = prods[0][1].shape[1]
    half, blk = m // 2, m // N_DEV
    ns = 2 * n_prod
    kept = N_CHIP * blk

    def side_pieces(s, side):
        i, hf = divmod(s, 2)
        first = hf * N_DEV // 2
        return [((d - first) * blk, i * kept + (d // 2) * blk) for d in range(first, first + N_DEV // 2) if d % 2 == side]

    def body(*refs):
        a_hbm, b_hbm, pair_refs = refs[:n_prod], refs[n_prod:2 * n_prod], refs[2 * n_prod:3 * n_prod]
        a_buf, b_buf, res_buf, recv_all, pair_all, a_sems, b_sems, send_sems, recv_sems, out_sems = refs[3 * n_prod:]
        step = pl.program_id(0)
        px, py, pc = _place()

        def fetch_a(s):
            return pltpu.make_async_copy(a_hbm[s // 2].at[:, pl.ds((s % 2) * half, half)], a_buf.at[s % 2], a_sems.at[s % 2])

        def fetch_b(i):
            return pltpu.make_async_copy(b_hbm[i], b_buf.at[i % 2], b_sems.at[i % 2])

        @pl.when(step == 0)
        def _():
            fetch_a(0).start()
            fetch_b(0).start()

        for s in range(ns):
            @pl.when(step == s)
            def _(s=s):
                if s + 1 < ns:
                    fetch_a(s + 1).start()
                if s % 2 == 0 and s // 2 + 1 < n_prod:
                    fetch_b(s // 2 + 1).start()
                fetch_a(s).wait()
                if s % 2 == 0:
                    fetch_b(s // 2).wait()

        res_buf[step % 2] = lax.dot_general(a_buf[step % 2], b_buf[(step // 2) % 2], TN_DIMS, preferred_element_type=F32)

        def crossing(s, j, piece):
            off, at = piece
            return pltpu.make_async_remote_copy(
                src_ref=res_buf.at[s % 2, pl.ds(off, blk)], dst_ref=recv_all.at[pl.ds(at, blk)],
                send_sem=send_sems.at[s, j], recv_sem=recv_sems.at[s, j], device_id=(px, py, 1 - pc), device_id_type=MESH)

        def write(s, core):
            at = side_pieces(s, core)[0][1]
            return pltpu.make_async_copy(pair_all.at[pl.ds(at, 2 * blk)], pair_refs[s // 2].at[pl.ds(at % kept, 2 * blk)], out_sems.at[s])

        for core in (0, 1):
            def settle(s, core=core):
                for j, piece in enumerate(side_pieces(s, 1 - core)):
                    crossing(s, j, piece).wait_send()
                for j, (off, at) in enumerate(side_pieces(s, core)):
                    crossing(s, j, (off, at)).wait_recv()
                    pair_all[at:at + blk] = (res_buf[s % 2, off:off + blk] + recv_all[at:at + blk]).astype(BF16)
                write(s, core).start()

            for s in range(ns):
                @pl.when((pc == core) & (step == s))
                def _(s=s, settle=settle, core=core):
                    for j, piece in enumerate(side_pieces(s, 1 - core)):
                        crossing(s, j, piece).start()
                    if s > 0:
                        settle(s - 1)
                    if s == ns - 1:
                        settle(s)
                        for t in range(ns):
                            write(t, core).wait()

    pairs = pl.pallas_call(
        body, name=name, grid=(ns,),
        in_specs=[ANY] * (2 * n_prod), out_specs=[ANY] * n_prod,
        out_shape=[jax.ShapeDtypeStruct((kept, n), BF16)] * n_prod,
        scratch_shapes=[pltpu.VMEM((2, ktok, half), prods[0][0].dtype), pltpu.VMEM((2, ktok, n), prods[0][1].dtype),
                        pltpu.VMEM((2, half, n), F32), pltpu.VMEM((n_prod * kept, n), F32), pltpu.VMEM((n_prod * kept, n), BF16),
                        pltpu.SemaphoreType.DMA((2,)), pltpu.SemaphoreType.DMA((2,)),
                        pltpu.SemaphoreType.DMA((ns, 2)), pltpu.SemaphoreType.DMA((ns, 2)), pltpu.SemaphoreType.DMA((ns,))],
        compiler_params=_params("arbitrary"))(*[a for a, _ in prods], *[b for _, b in prods])
    return [pair.reshape(N_CHIP, blk, n) for pair in pairs]


def _segment_chunks(segs):
    bounds = [0]
    for s in segs:
        bounds.append(bounds[-1] + s.shape[1] // SEG_CHUNK)
    return bounds


def _input_grad(segs, wt, x, g, dy, name, tm=512, ride=None):
    m = segs[0].shape[0]
    rows, n = wt.shape
    tm = min(tm, m)
    bounds = _segment_chunks(segs)
    n_seg = len(segs)
    ni = m // tm
    host = _Hosted(ride, n_seg + 4, 2, 2, aliasing=True)

    def body(*refs):
        host_refs, start, finish = host.split(refs)
        a_refs = host_refs[:n_seg]
        wt_hbm, x_ref, g_ref, dy_ref, gx_ref, st_ref, wt_vmem, sems = host_refs[n_seg:]
        i = pl.program_id(0)
        start(i == 0)

        @pl.when(i == 0)
        def _():
            st_ref[...] = jnp.zeros_like(st_ref)

        ready = _load_resident(wt_hbm, wt_vmem, sems, i == 0)
        dh = None
        for s in range(n_seg):
            for c in range(bounds[s], bounds[s + 1]):
                ready(c)
            part = jnp.dot(a_refs[s][...], wt_vmem[bounds[s] * SEG_CHUNK:bounds[s + 1] * SEG_CHUNK, :], preferred_element_type=F32)
            dh = part if dh is None else dh + part
        xv = x_ref[...]
        r = lax.rsqrt(jnp.mean(xv * xv, axis=-1, keepdims=True) + EPS)
        xn = xv * r
        dxn = dh * g_ref[...]
        gx_ref[...] = dy_ref[...] + r * (dxn - xn * jnp.mean(dxn * xn, axis=-1, keepdims=True))
        st_ref[...] += _rows8([jnp.sum(dh * xn, axis=0, keepdims=True)], n)
        finish(i == ni - 1)

    tile = pl.BlockSpec((tm, n), lambda i: (i, 0))
    outs = pl.pallas_call(
        body, name=name, grid=(ni,),
        in_specs=[pl.BlockSpec((tm, sg.shape[1]), lambda i: (i, 0)) for sg in segs]
        + [ANY, tile, pl.BlockSpec((1, n), lambda i: (0, 0)), tile] + host.in_specs,
        out_specs=[tile, pl.BlockSpec((8, n), lambda i: (0, 0))] + host.out_specs,
        out_shape=[jax.ShapeDtypeStruct((m, n), F32), jax.ShapeDtypeStruct((8, n), F32)] + host.out_shapes,
        scratch_shapes=[pltpu.VMEM((rows, n), wt.dtype), pltpu.SemaphoreType.DMA((rows // SEG_CHUNK,))] + host.scratch_shapes,
        input_output_aliases=host.aliases,
        compiler_params=_params("arbitrary"))(*segs, wt, x, g, dy, *host.arrays)
    res, landed = host.results(outs, 2)
    return (*res, landed) if ride else tuple(res)


def _mm_tn_seg(segs, b, name):
    ktok = segs[0].shape[0]
    n = b.shape[1]
    bounds = _segment_chunks(segs)
    n_seg = len(segs)
    nc = bounds[-1]
    seg_of = [s for s in range(n_seg) for _ in range(bounds[s], bounds[s + 1])]

    def body(*refs):
        a_hbm, b_hbm, o_ref = refs[:n_seg], refs[n_seg], refs[n_seg + 1]
        a_buf, b_vmem, a_sems, b_sem = refs[n_seg + 2:]
        c = pl.program_id(0)

        def fetch(cc):
            s = seg_of[cc]
            cols = pl.ds((cc - bounds[s]) * SEG_CHUNK, SEG_CHUNK)
            return pltpu.make_async_copy(a_hbm[s].at[:, cols], a_buf.at[cc % 2], a_sems.at[cc % 2])

        @pl.when(c == 0)
        def _():
            whole = pltpu.make_async_copy(b_hbm, b_vmem, b_sem)
            whole.start()
            fetch(0).start()
            whole.wait()

        for cc in range(nc):
            @pl.when(c == cc)
            def _(cc=cc):
                if cc + 1 < nc:
                    fetch(cc + 1).start()
                fetch(cc).wait()

        o_ref[...] = lax.dot_general(a_buf[c % 2], b_vmem[...], TN_DIMS, preferred_element_type=F32)

    return pl.pallas_call(
        body, name=name, grid=(nc,),
        in_specs=[ANY] * (n_seg + 1), out_specs=pl.BlockSpec((SEG_CHUNK, n), lambda c: (c, 0)),
        out_shape=jax.ShapeDtypeStruct((nc * SEG_CHUNK, n), F32),
        scratch_shapes=[pltpu.VMEM((2, ktok, SEG_CHUNK), segs[0].dtype), pltpu.VMEM((ktok, n), b.dtype),
                        pltpu.SemaphoreType.DMA((2,)), pltpu.SemaphoreType.DMA],
        compiler_params=_params("arbitrary"))(*segs, b)


CHUNK_ORDER = (0, 4, 7, 10, 1, 5, 8, 11, 2, 6, 9, 12, 3)
EARLY_STEPS = 8


def _side_pieces():
    table = []
    for core in (0, 1):
        sides = ([None] * len(CHUNK_ORDER), [None] * len(CHUNK_ORDER))
        for s, cc in enumerate(CHUNK_ORDER):
            g0 = cc * SEG_CHUNK
            for d in range(N_DEV):
                lo, hi = max(g0, d * SHARD_IN), min(g0 + SEG_CHUNK, (d + 1) * SHARD_IN)
                if lo < hi:
                    side = sides[0 if d % 2 == core else 1]
                    assert side[s] is None
                    side[s] = (lo - g0, hi - lo, (d // 2) * SHARD_IN + lo - d * SHARD_IN)
        table.append(sides)
    return table


def _to_chip(src, red_ref, piece, s, send_sems, recv_sems, own, core):
    _, rows, at = piece
    k, r0 = divmod(at, SHARD_IN)
    return k, pltpu.make_async_remote_copy(
        src_ref=src, dst_ref=red_ref.at[own, pl.ds(r0, rows)], send_sem=send_sems.at[s], recv_sem=recv_sems.at[own, s],
        device_id=(k // 2, k % 2, core), device_id_type=MESH)


def _from_chip(src, red_ref, piece, s, send_sems, recv_sems, chip, me):
    _, rows, at = piece
    return pltpu.make_async_remote_copy(
        src_ref=src, dst_ref=red_ref.at[chip, pl.ds(at % SHARD_IN, rows)], send_sem=send_sems.at[s], recv_sem=recv_sems.at[chip, s],
        device_id=me, device_id_type=MESH)


def _mm_tn_seg_pair(segs, b, name):
    ktok = segs[0].shape[0]
    n = b.shape[1]
    bounds = _segment_chunks(segs)
    n_seg = len(segs)
    nc = bounds[-1]
    assert nc == len(CHUNK_ORDER)
    seg_of = [s for s in range(n_seg) for _ in range(bounds[s], bounds[s + 1])]
    table = _side_pieces()

    def body(*refs):
        a_hbm, b_hbm, pair_ref, red_ref = refs[:n_seg], refs[n_seg], refs[n_seg + 1], refs[n_seg + 2]
        (a_buf, b_vmem, res_buf, recv_all, pair_all, a_sems, b_sem, send_sems, recv_sems, out_sems,
         chip_send, chip_recv) = refs[n_seg + 3:]
        step = pl.program_id(0)
        px, py, pc = _place()
        own = 2 * px + py

        def fetch(s):
            sg = seg_of[CHUNK_ORDER[s]]
            cols = pl.ds((CHUNK_ORDER[s] - bounds[sg]) * SEG_CHUNK, SEG_CHUNK)
            return pltpu.make_async_copy(a_hbm[sg].at[:, cols], a_buf.at[s % 2], a_sems.at[s % 2])

        @pl.when(step == 0)
        def _():
            whole = pltpu.make_async_copy(b_hbm, b_vmem, b_sem)
            whole.start()
            fetch(0).start()
            whole.wait()

        for s in range(nc):
            @pl.when(step == s)
            def _(s=s):
                if s + 1 < nc:
                    fetch(s + 1).start()
                fetch(s).wait()

        res_buf[step % 2] = lax.dot_general(a_buf[step % 2], b_vmem[...], TN_DIMS, preferred_element_type=F32)

        def crossing(s, piece):
            off, rows, at = piece
            return pltpu.make_async_remote_copy(
                src_ref=res_buf.at[s % 2, pl.ds(off, rows)], dst_ref=recv_all.at[pl.ds(at, rows)],
                send_sem=send_sems.at[s], recv_sem=recv_sems.at[s], device_id=(px, py, 1 - pc), device_id_type=MESH)

        def write(s, piece):
            _, rows, at = piece
            return pltpu.make_async_copy(pair_all.at[pl.ds(at, rows)], pair_ref.at[pl.ds(at, rows)], out_sems.at[s])

        def to_chip(s, piece, core):
            return _to_chip(pair_all.at[pl.ds(piece[2], piece[1])], red_ref, piece, s, chip_send, chip_recv, own, core)

        for core in (0, 1):
            mine, theirs = table[core]

            def settle(s, mine=mine, theirs=theirs, core=core):
                if theirs[s] is not None:
                    crossing(s, theirs[s]).wait_send()
                if mine[s] is not None:
                    off, rows, at = mine[s]
                    crossing(s, mine[s]).wait_recv()
                    pair_all[at:at + rows] = (res_buf[s % 2, off:off + rows] + recv_all[at:at + rows]).astype(BF16)
                    write(s, mine[s]).start()
                    if s < EARLY_STEPS:
                        k, cp = to_chip(s, mine[s], core)
                        pl.when(own != k)(cp.start)

            for s in range(nc):
                @pl.when((pc == core) & (step == s))
                def _(s=s, settle=settle, mine=mine, theirs=theirs, core=core):
                    if theirs[s] is not None:
                        crossing(s, theirs[s]).start()
                    if s > 0:
                        settle(s - 1)
                    if s == nc - 1:
                        settle(s)
                        kept = [t for t in range(nc) if mine[t] is not None]
                        for t in kept:
                            write(t, mine[t]).wait()
                        early = [t for t in kept if t < EARLY_STEPS]
                        for t in early:
                            k, cp = to_chip(t, mine[t], core)
                            pl.when(own != k)(cp.wait_send)
                        for k in range(N_CHIP):
                            @pl.when(own == k)
                            def _(k=k):
                                for t in early:
                                    if mine[t][2] // SHARD_IN == k:
                                        for chip in range(N_CHIP):
                                            if chip != k:
                                                _from_chip(pair_all.at[pl.ds(mine[t][2], mine[t][1])], red_ref, mine[t], t,
                                                           chip_send, chip_recv, chip, (px, py, pc)).wait_recv()

    flat = jax.ShapeDtypeStruct((N_CHIP * SHARD_IN, n), BF16)
    pair, red = pl.pallas_call(
        body, name=name, grid=(nc,),
        in_specs=[ANY] * (n_seg + 1), out_specs=[ANY, ANY],
        out_shape=[flat, jax.ShapeDtypeStruct((N_CHIP, SHARD_IN, n), BF16)],
        scratch_shapes=[pltpu.VMEM((2, ktok, SEG_CHUNK), segs[0].dtype), pltpu.VMEM((ktok, n), b.dtype),
                        pltpu.VMEM((2, SEG_CHUNK, n), F32), pltpu.VMEM(flat.shape, F32), pltpu.VMEM(flat.shape, BF16),
                        pltpu.SemaphoreType.DMA((2,)), pltpu.SemaphoreType.DMA,
                        pltpu.SemaphoreType.DMA((nc,)), pltpu.SemaphoreType.DMA((nc,)), pltpu.SemaphoreType.DMA((nc,)),
                        pltpu.SemaphoreType.DMA((nc,)), pltpu.SemaphoreType.DMA((N_CHIP, nc))],
        compiler_params=_params("arbitrary"))(*segs, b)
    return pair.reshape(N_CHIP, SHARD_IN, n), red


def _branches_fwd(z_rnn, z_attn, ag_ml, b_gate, w_rnn, w_attn, w_out, x, target, g_post, name, tm=512):
    t, d = x.shape
    tm = min(tm, t)

    def body(zr_ref, za_ref, lr_ref, la_ref, br_ref, ba_ref, wr_ref, wa_ref, wo_ref, x_ref, t_ref, g_ref,
             brr_ref, bra_ref, mg_ref, do_ref, dy_ref, st_ref):
        @pl.when(pl.program_id(0) == 0)
        def _():
            st_ref[...] = jnp.zeros_like(st_ref)

        br_rnn = jnp.dot(zr_ref[...], wr_ref[...], preferred_element_type=F32)
        br_attn = jnp.dot(za_ref[...], wa_ref[...], preferred_element_type=F32)
        brr_ref[...] = br_rnn.astype(BF16)
        bra_ref[...] = br_attn.astype(BF16)
        g_rnn = _sigmoid(lr_ref[...].astype(F32) + br_ref[...])
        g_attn = _sigmoid(la_ref[...].astype(F32) + ba_ref[...])
        merged = (g_rnn * br_rnn + g_attn * br_attn).astype(BF16)
        mg_ref[...] = merged
        o = jnp.dot(merged, wo_ref[...], preferred_element_type=F32)
        g = g_ref[...]
        r = lax.rsqrt(jnp.mean(o * o, axis=-1, keepdims=True) + EPS)
        nrm = o * r
        err = x_ref[...] + nrm * g - t_ref[...]
        dy = err * (1.0 / d)
        dy_ref[...] = dy
        dn = dy * g
        do_ref[...] = (r * (dn - nrm * jnp.mean(dn * nrm, axis=-1, keepdims=True))).astype(BF16)
        st_ref[...] += _rows8([jnp.sum(dy * nrm, axis=0, keepdims=True), jnp.sum(err * err, axis=0, keepdims=True)], d)

    tile = pl.BlockSpec((tm, d), lambda i: (i, 0))
    weight = pl.BlockSpec((d, d), lambda i: (0, 0))
    bf = jax.ShapeDtypeStruct((t, d), BF16)
    return pl.pallas_call(
        body, name=name, grid=(t // tm,),
        in_specs=[tile, tile, pl.BlockSpec((tm, d), lambda i: (i, 1)), pl.BlockSpec((tm, d), lambda i: (i, 2)),
                  pl.BlockSpec((1, d), lambda i: (0, 0)), pl.BlockSpec((1, d), lambda i: (0, 1)),
                  weight, weight, weight, tile, tile, pl.BlockSpec((1, d), lambda i: (0, 0))],
        out_specs=[tile, tile, tile, tile, tile, pl.BlockSpec((8, d), lambda i: (0, 0))],
        out_shape=[bf, bf, bf, bf, jax.ShapeDtypeStruct((t, d), F32), jax.ShapeDtypeStruct((8, d), F32)],
        compiler_params=_params("arbitrary"))(z_rnn, z_attn, ag_ml, ag_ml, b_gate, b_gate, w_rnn, w_attn, w_out, x, target, g_post)


def _branches_bwd(dout, br_rnn, br_attn, ag_ml, b_gate, w_rnn, w_attn, w_out, name, tm=512):
    t, d = br_rnn.shape
    tm = min(tm, t)

    def body(do_ref, r_ref, a_ref, lr_ref, la_ref, br_ref, ba_ref, wr_ref, wa_ref, wo_ref,
             dr_ref, da_ref, dl_ref, dzr_ref, dza_ref, st_ref, wt_ref):
        @pl.when(pl.program_id(0) == 0)
        def _():
            st_ref[...] = jnp.zeros_like(st_ref)
            wt_ref[0] = wo_ref[...].T
            wt_ref[1] = wr_ref[...].T
            wt_ref[2] = wa_ref[...].T

        dm = jnp.dot(do_ref[...], wt_ref[0], preferred_element_type=F32)
        g_rnn = _sigmoid(lr_ref[...].astype(F32) + br_ref[...])
        g_attn = _sigmoid(la_ref[...].astype(F32) + ba_ref[...])
        dbr_rnn = (dm * g_rnn).astype(BF16)
        dbr_attn = (dm * g_attn).astype(BF16)
        dr_ref[...] = dbr_rnn
        da_ref[...] = dbr_attn
        dl_rnn = dm * r_ref[...].astype(F32) * g_rnn * (1.0 - g_rnn)
        dl_attn = dm * a_ref[...].astype(F32) * g_attn * (1.0 - g_attn)
        dl_ref[:, 0:d] = dl_rnn.astype(BF16)
        dl_ref[:, d:2 * d] = dl_attn.astype(BF16)
        st_ref[...] += _rows8([jnp.sum(dl_rnn, axis=0, keepdims=True), jnp.sum(dl_attn, axis=0, keepdims=True)], d)
        dzr_ref[...] = jnp.dot(dbr_rnn, wt_ref[1], preferred_element_type=F32).astype(BF16)
        dza_ref[...] = jnp.dot(dbr_attn, wt_ref[2], preferred_element_type=F32).astype(BF16)

    tile = pl.BlockSpec((tm, d), lambda i: (i, 0))
    weight = pl.BlockSpec((d, d), lambda i: (0, 0))
    bf = jax.ShapeDtypeStruct((t, d), BF16)
    return pl.pallas_call(
        body, name=name, grid=(t // tm,),
        in_specs=[tile, tile, tile, pl.BlockSpec((tm, d), lambda i: (i, 1)), pl.BlockSpec((tm, d), lambda i: (i, 2)),
                  pl.BlockSpec((1, d), lambda i: (0, 0)), pl.BlockSpec((1, d), lambda i: (0, 1)), weight, weight, weight],
        out_specs=[tile, tile, pl.BlockSpec((tm, 2 * d), lambda i: (i, 0)), tile, tile, pl.BlockSpec((8, d), lambda i: (0, 0))],
        out_shape=[bf, bf, jax.ShapeDtypeStruct((t, 2 * d), BF16), bf, bf, jax.ShapeDtypeStruct((8, d), F32)],
        scratch_shapes=[pltpu.VMEM((3, d, d), BF16)],
        compiler_params=_params("arbitrary"))(dout, br_rnn, br_attn, ag_ml, ag_ml, b_gate, b_gate, w_rnn, w_attn, w_out)


def _lru_decay(r, sp):
    log_a = (-LRU_C) * r * sp
    return log_a, jnp.exp(log_a)


def _lru_gates(c, wa, ba, wx, bx, sp):
    cb = c.astype(BF16)
    r = _sigmoid(jnp.dot(cb, wa, preferred_element_type=F32) + ba)
    ig = _sigmoid(jnp.dot(cb, wx, preferred_element_type=F32) + bx)
    log_a, a = _lru_decay(r, sp)
    mult = jnp.sqrt(-jnp.tanh(log_a) * (a * a + 1.0))
    return cb, r, ig, a, mult


SUBLANES = 8


def _scan_fwd(a, u, carry, tt):
    w = a.shape[1]
    ng = tt // SUBLANES
    a3 = a.reshape(ng, SUBLANES, w)
    u3 = u.reshape(ng, SUBLANES, w)
    sub = lax.broadcasted_iota(jnp.int32, (ng, SUBLANES, w), 1)
    d = 1
    while d < SUBLANES:
        keep = sub >= d
        u3 = u3 + a3 * jnp.where(keep, pltpu.roll(u3, d, 1), 0.0)
        a3 = a3 * jnp.where(keep, pltpu.roll(a3, d, 1), 1.0)
        d *= 2
    out = []
    for g in range(ng):
        hg = u3[g] + a3[g] * carry
        out.append(hg)
        carry = hg[SUBLANES - 1:SUBLANES, :]
    return jnp.concatenate(out, axis=0)


def _scan_rev(b, g, carry, tt):
    w = b.shape[1]
    ng = tt // SUBLANES
    b3 = b.reshape(ng, SUBLANES, w)
    g3 = g.reshape(ng, SUBLANES, w)
    sub = lax.broadcasted_iota(jnp.int32, (ng, SUBLANES, w), 1)
    d = 1
    while d < SUBLANES:
        keep = sub < SUBLANES - d
        g3 = g3 + b3 * jnp.where(keep, pltpu.roll(g3, SUBLANES - d, 1), 0.0)
        b3 = b3 * jnp.where(keep, pltpu.roll(b3, SUBLANES - d, 1), 1.0)
        d *= 2
    out = [None] * ng
    for k in range(ng - 1, -1, -1):
        hk = g3[k] + b3[k] * carry
        out[k] = hk
        carry = hk[0:1, :]
    return jnp.concatenate(out, axis=0)


def _conv_taps(cw, bias, x, ext_ref, tt):
    x2 = ext_ref[7:7 + tt, :]
    x1 = ext_ref[6:6 + tt, :]
    x0 = ext_ref[5:5 + tt, :]
    return bias + cw[3:4] * x + cw[2:3] * x2 + cw[1:2] * x1 + cw[0:1] * x0


def _rnn_fwd(rx_rg, cw, cb, wa, ba, wx, bx, lam, name, tt=512):
    t = rx_rg.shape[0]
    tt = min(tt, t)
    w = GROUP_W

    def body(rx_ref, rg_ref, cw_ref, cb_ref, wa_ref, ba_ref, wx_ref, bx_ref, lam_ref,
             y_ref, z_ref, c_ref, r_ref, ig_ref, mult_ref, ext_ref, hc_ref):
        @pl.when(pl.program_id(1) == 0)
        def _():
            ext_ref[0:8, :] = jnp.zeros((8, w), F32)
            hc_ref[...] = jnp.zeros((8, w), F32)

        x = rx_ref[...]
        ext_ref[8:8 + tt, :] = x
        c = _conv_taps(cw_ref[...], cb_ref[...], x, ext_ref, tt)
        ext_ref[0:8, :] = x[tt - 8:tt, :]
        sp = _softplus(-lam_ref[...])
        _, r, ig, a, mult = _lru_gates(c, wa_ref[...], ba_ref[...], wx_ref[...], bx_ref[...], sp)
        c_ref[...] = c
        r_ref[...] = r
        ig_ref[...] = ig
        mult_ref[...] = mult
        h = _scan_fwd(a, mult * (ig * c), hc_ref[7:8, :], tt)
        hc_ref[...] = h[tt - 8:tt, :]
        y_ref[...] = h
        rg = rg_ref[...]
        z_ref[...] = (h * rg * _sigmoid(rg)).astype(BF16)

    vec = pl.BlockSpec((1, w), lambda g, i: (0, g))
    mat = pl.BlockSpec((None, w, w), lambda g, i: (g, 0, 0))
    tile = pl.BlockSpec((tt, w), lambda g, i: (i, g))
    return pl.pallas_call(
        body, name=name, grid=(N_GROUPS, t // tt),
        in_specs=[tile, pl.BlockSpec((tt, w), lambda g, i: (i, N_GROUPS + g)),
                  pl.BlockSpec((4, w), lambda g, i: (0, g)), vec, mat, vec, mat, vec, vec],
        out_specs=[tile] * 6,
        out_shape=[jax.ShapeDtypeStruct((t, D_RNN), F32), jax.ShapeDtypeStruct((t, D_RNN), BF16)]
        + [jax.ShapeDtypeStruct((t, D_RNN), F32)] * 4,
        scratch_shapes=[pltpu.VMEM((tt + 8, w), F32), pltpu.VMEM((8, w), F32)],
        compiler_params=_params("parallel", "arbitrary"))(rx_rg, rx_rg, cw, cb, wa, ba, wx, bx, lam)


def _rnn_bwd(rx_rg, y, dz, c, r, ig, mult, cw, wa, wx, lam, name, tt=512):
    t = rx_rg.shape[0]
    tt = min(tt, t)
    nt = t // tt
    w = GROUP_W

    def body(rx_ref, rg_ref, y_ref, yt_ref, dz_ref, c_ref, r_ref, ig_ref, mult_ref, cw_ref, wa_ref, wx_ref, lam_ref,
             drx_ref, drg_ref, st_ref, gda_ref, gdx_ref, dcx_ref, wcar_ref, acar_ref, dwa_ref, dwx_ref):
        ii = pl.program_id(1)

        @pl.when(ii == 0)
        def _():
            wcar_ref[...] = jnp.zeros((8, w), F32)
            acar_ref[...] = jnp.zeros((8, w), F32)
            dcx_ref[tt:tt + 8, :] = jnp.zeros((8, w), F32)
            st_ref[...] = jnp.zeros_like(st_ref)
            dwa_ref[...] = jnp.zeros_like(dwa_ref)
            dwx_ref[...] = jnp.zeros_like(dwx_ref)

        has_prev = jnp.where(ii == nt - 1, 0.0, 1.0)
        cwv = cw_ref[...]
        lam = lam_ref[...]
        sp = _softplus(-lam)
        wa = wa_ref[...]
        wx = wx_ref[...]
        c = c_ref[...]
        r = r_ref[...]
        ig = ig_ref[...]
        mult = mult_ref[...]
        _, a = _lru_decay(r, sp)
        cb16 = c.astype(BF16)

        rg = rg_ref[...]
        sg = _sigmoid(rg)
        dz = dz_ref[...].astype(F32)
        yv = y_ref[...]
        drg_ref[...] = (dz * yv * (sg * (1.0 + rg * (1.0 - sg)))).astype(BF16)

        row = lax.broadcasted_iota(jnp.int32, (tt, w), 0)
        b = jnp.where(row < tt - 1, pltpu.roll(a, tt - 1, 0), acar_ref[0:1, :])
        dh = _scan_rev(b, dz * (rg * sg), wcar_ref[0:1, :], tt)
        wcar_ref[...] = dh[0:8, :]
        acar_ref[...] = a[0:8, :]

        hprev = jnp.where(row >= 1, pltpu.roll(yv, 1, 0), yt_ref[7:8, :] * has_prev)
        dmult = dh * (ig * c)
        dig = dh * mult * c
        dlog_a = dh * hprev * a - dmult * (a * a / mult)
        dpa = dlog_a * ((-LRU_C) * sp) * r * (1.0 - r)
        dpx = dig * ig * (1.0 - ig)
        dsp = jnp.sum(dlog_a * r, axis=0, keepdims=True) * (-LRU_C)
        dlam = dsp * (-_sigmoid(-lam))
        dpa16 = dpa.astype(BF16)
        dpx16 = dpx.astype(BF16)
        dwa_ref[...] += lax.dot_general(cb16, dpa16, TN_DIMS, preferred_element_type=F32)
        dwx_ref[...] += lax.dot_general(cb16, dpx16, TN_DIMS, preferred_element_type=F32)
        dc = (dh * mult * ig
              + lax.dot_general(dpa16, wa, NT_DIMS, preferred_element_type=F32)
              + lax.dot_general(dpx16, wx, NT_DIMS, preferred_element_type=F32))

        dcx_ref[0:tt, :] = dc
        dc1 = dcx_ref[1:1 + tt, :]
        dc2 = dcx_ref[2:2 + tt, :]
        dc3 = dcx_ref[3:3 + tt, :]
        drx_ref[...] = (cwv[3:4] * dc + cwv[2:3] * dc1 + cwv[1:2] * dc2 + cwv[0:1] * dc3).astype(BF16)
        dcx_ref[tt:tt + 8, :] = dc[0:8, :]

        def colsum(v):
            return jnp.sum(v, axis=0, keepdims=True)

        x = rx_ref[...]
        st_ref[...] += _rows8([colsum(dc), colsum(dpa), colsum(dpx), dlam,
                               colsum(dc3 * x), colsum(dc2 * x), colsum(dc1 * x), colsum(dc * x)], w)

        @pl.when(ii == nt - 1)
        def _():
            for blk in range(GROUP_W // RNN_BLOCK_W):
                rows = slice(blk * RNN_BLOCK_W, (blk + 1) * RNN_BLOCK_W)
                gda_ref[blk] = dwa_ref[rows, rows]
                gdx_ref[blk] = dwx_ref[rows, rows]

    def rev(ii):
        return nt - 1 - ii

    def tail(g, ii):
        return (jnp.maximum(rev(ii) * (tt // 8) - 1, 0), g)

    vec = pl.BlockSpec((1, w), lambda g, ii: (0, g))
    mat = pl.BlockSpec((None, w, w), lambda g, ii: (g, 0, 0))
    tile = pl.BlockSpec((tt, w), lambda g, ii: (rev(ii), g))
    diag_shape = (N_GROUPS, GROUP_W // RNN_BLOCK_W, RNN_BLOCK_W, RNN_BLOCK_W)
    diag = pl.BlockSpec((None,) + diag_shape[1:], lambda g, ii: (g, 0, 0, 0))
    return pl.pallas_call(
        body, name=name, grid=(N_GROUPS, nt),
        in_specs=[tile, pl.BlockSpec((tt, w), lambda g, ii: (rev(ii), N_GROUPS + g)),
                  tile, pl.BlockSpec((8, w), tail), tile, tile, tile, tile, tile,
                  pl.BlockSpec((4, w), lambda g, ii: (0, g)), mat, mat, vec],
        out_specs=[tile, tile, pl.BlockSpec((8, w), lambda g, ii: (0, g)), diag, diag],
        out_shape=[jax.ShapeDtypeStruct((t, D_RNN), BF16), jax.ShapeDtypeStruct((t, D_RNN), BF16),
                   jax.ShapeDtypeStruct((8, D_RNN), F32),
                   jax.ShapeDtypeStruct(diag_shape, F32), jax.ShapeDtypeStruct(diag_shape, F32)],
        scratch_shapes=[pltpu.VMEM((tt + 8, w), F32), pltpu.VMEM((8, w), F32),
                        pltpu.VMEM((8, w), F32), pltpu.VMEM((w, w), F32), pltpu.VMEM((w, w), F32)],
        compiler_params=_params("parallel", "arbitrary"))(rx_rg, rx_rg, y, y, dz, c, r, ig, mult, cw, wa, wx, lam)


def _half_mask(shape, half):
    lane = lax.broadcasted_iota(jnp.int32, shape, 1)
    return (lane >= HEAD_DIM) if half else (lane < HEAD_DIM)


def _dup_half(t, half):
    sel = jnp.where(_half_mask(t.shape, half), t, 0.0)
    return sel + pltpu.roll(sel, HEAD_DIM, 1)


def _band_geometry(n):
    qi = lax.broadcasted_iota(jnp.int32, (BLOCK, 2 * BLOCK), 0)
    kj = lax.broadcasted_iota(jnp.int32, (BLOCK, 2 * BLOCK), 1)
    dist = BLOCK + qi - kj
    first_key = jnp.where(n > 0, 0, BLOCK)
    valid = (dist >= 0) & (dist < BLOCK) & (kj >= first_key)
    return dist.astype(F32), valid


GROUP = 4


def _kv_dup(prev_ref, cur_ref, hk, scale=1.0):
    tile = hk // 2
    kt = jnp.concatenate([prev_ref[:, tile * LANE:(tile + 1) * LANE], cur_ref[:, tile * LANE:(tile + 1) * LANE]], axis=0)
    return (_dup_half(kt.astype(F32), hk % 2) * scale).astype(BF16)


def _fill_bias(bias_ref, sm_ref, n):
    distf, valid = _band_geometry(n)
    for head in range(N_Q_HEADS):
        bias_ref[head] = jnp.where(valid, -sm_ref[1, head] * distf, MASKED)


def _head_scores(q2s, half, kdup, bias):
    qm = jnp.where(_half_mask(q2s.shape, half), q2s, jnp.zeros_like(q2s))
    return qm, lax.dot_general(qm, kdup, NT_DIMS, preferred_element_type=F32) + bias


def _attn_specs(nb, clamp_last):
    def blk(n):
        return jnp.minimum(n, nb - 1) if clamp_last else n

    q_spec = pl.BlockSpec((BLOCK, D_MODEL), lambda n: (blk(n), 0))
    k_prev = pl.BlockSpec((BLOCK, D_KV), lambda n: (jnp.maximum(blk(n) - 1, 0), D_MODEL // D_KV))
    k_cur = pl.BlockSpec((BLOCK, D_KV), lambda n: (blk(n), D_MODEL // D_KV))
    v_prev = pl.BlockSpec((BLOCK, D_KV), lambda n: (jnp.maximum(blk(n) - 1, 0), D_MODEL // D_KV + 1))
    v_cur = pl.BlockSpec((BLOCK, D_KV), lambda n: (blk(n), D_MODEL // D_KV + 1))
    return q_spec, k_prev, k_cur, v_prev, v_cur


def _attn_fwd(sm, qkv, ag_ml, name, ride=None):
    t = qkv.shape[0]
    nb = t // BLOCK

    host = _Hosted(ride, 7, 3, 1)

    def body(*refs):
        (sm_ref, q_ref, kp_ref, kc_ref, vp_ref, vc_ref, ag_ref, y_ref, z_ref, lse_ref, bias_ref), start, finish = host.split(refs)
        n = pl.program_id(0)
        start(n == 0)
        start.middle(n == (3 * nb) // 4)

        @pl.when(n <= 1)
        def _():
            _fill_bias(bias_ref, sm_ref, n)

        lane = lax.broadcasted_iota(jnp.int32, (BLOCK, LANE), 1)
        low = lane < HEAD_DIM
        lse = jnp.zeros((BLOCK, LANE), F32)
        for hk in range(N_Q_HEADS // GROUP):
            kdup = _kv_dup(kp_ref, kc_ref, hk)
            vdup = _kv_dup(vp_ref, vc_ref, hk)
            for k in (0, 1):
                c = slice((2 * hk + k) * LANE, (2 * hk + k + 1) * LANE)
                q2s = q_ref[:, c] * ATTN_SCALE
                outs = []
                for half in (0, 1):
                    head = GROUP * hk + 2 * k + half
                    _, s = _head_scores(q2s, half, kdup, bias_ref[head])
                    sink = sm_ref[0, head]
                    m = jnp.maximum(jnp.max(s, axis=1, keepdims=True), sink)
                    e = jnp.exp(s - m)
                    l = jnp.sum(e, axis=1, keepdims=True) + jnp.exp(sink - m)
                    outs.append(jnp.dot((e * (1.0 / l)).astype(BF16), vdup, preferred_element_type=F32))
                    lse = jnp.where(lane == head, m + jnp.log(l), lse)
                yt = jnp.where(low, outs[0], outs[1])
                y_ref[:, c] = yt
                ag = ag_ref[:, c].astype(F32)
                z_ref[:, c] = (yt * ag * _sigmoid(ag)).astype(BF16)
        lse_ref[...] = lse
        finish(n == nb - 1)

    q_spec, k_prev, k_cur, v_prev, v_cur = _attn_specs(nb, False)
    wide = pl.BlockSpec((BLOCK, D_MODEL), lambda n: (n, 0))
    outs = pl.pallas_call(
        body, name=name, grid=(nb,),
        in_specs=[pl.BlockSpec(memory_space=pltpu.SMEM), q_spec, k_prev, k_cur, v_prev, v_cur, wide] + host.in_specs,
        out_specs=[wide, wide, pl.BlockSpec((BLOCK, LANE), lambda n: (n, 0))] + host.out_specs,
        out_shape=[jax.ShapeDtypeStruct((t, D_MODEL), F32), jax.ShapeDtypeStruct((t, D_MODEL), BF16),
                   jax.ShapeDtypeStruct((t, LANE), F32)] + host.out_shapes,
        scratch_shapes=[pltpu.VMEM((N_Q_HEADS, BLOCK, 2 * BLOCK), F32)] + host.scratch_shapes,
        compiler_params=_params("arbitrary"))(sm, qkv, qkv, qkv, qkv, qkv, ag_ml, *host.arrays)
    res, landed = host.results(outs, 3)
    return (*res, landed) if ride else tuple(res)


def _attn_bwd(sm, qkv, ag_ml, y, lse, dz, name, ride=None):
    t = qkv.shape[0]
    nb = t // BLOCK
    host = _Hosted(ride, 10, 4, 3)

    def body(*refs):
        host_refs, start, finish = host.split(refs)
        (sm_ref, q_ref, kp_ref, kc_ref, vp_ref, vc_ref, ag_ref, y_ref, lse_ref, dz_ref,
         dq_ref, dkv_ref, dag_ref, ds_ref, ck_ref, cv_ref, bias_ref) = host_refs
        n = pl.program_id(0)
        start(n == 0)
        start.middle(n == (3 * nb) // 4)

        @pl.when(n == 0)
        def _():
            ck_ref[...] = jnp.zeros_like(ck_ref)
            cv_ref[...] = jnp.zeros_like(cv_ref)
            ds_ref[...] = jnp.zeros_like(ds_ref)

        @pl.when(n <= 1)
        def _():
            _fill_bias(bias_ref, sm_ref, n)

        @pl.when(n < nb)
        def _():
            lane8 = lax.broadcasted_iota(jnp.int32, (8, LANE), 1)
            row8 = lax.broadcasted_iota(jnp.int32, (8, LANE), 0)
            dsink = jnp.zeros((8, LANE), F32)
            dk_heads, dv_heads = [], []
            lse_tile = lse_ref[...]
            for hk in range(N_Q_HEADS // GROUP):
                kdup = _kv_dup(kp_ref, kc_ref, hk)
                ks = _kv_dup(kp_ref, kc_ref, hk, ATTN_SCALE)
                vdup = _kv_dup(vp_ref, vc_ref, hk)
                qms, dyhs, y_rows = [], [], []
                for k in (0, 1):
                    c = slice((2 * hk + k) * LANE, (2 * hk + k + 1) * LANE)
                    ag = ag_ref[:, c].astype(F32)
                    sg = _sigmoid(ag)
                    dzt = dz_ref[:, c].astype(F32)
                    yt = y_ref[:, c]
                    dag_ref[:, c] = (dzt * yt * (sg * (1.0 + ag * (1.0 - sg)))).astype(BF16)
                    dyt = dzt * (ag * sg)
                    q2s = q_ref[:, c] * ATTN_SCALE
                    for half in (0, 1):
                        hm = _half_mask(q2s.shape, half)
                        qms.append(jnp.where(hm, q2s, jnp.zeros_like(q2s)))
                        dyhs.append(jnp.where(hm, dyt, 0.0))
                        y_rows.append(yt)
                qm4 = jnp.concatenate(qms, axis=0)
                dy4 = jnp.concatenate(dyhs, axis=0)
                dy4_16 = dy4.astype(BF16)
                s4 = lax.dot_general(qm4, kdup, NT_DIMS, preferred_element_type=F32)
                dp4 = lax.dot_general(dy4_16, vdup, NT_DIMS, preferred_element_type=F32)
                probs16, ds16 = [], []
                for r in range(GROUP):
                    head = GROUP * hk + r
                    rows = slice(r * BLOCK, (r + 1) * BLOCK)
                    lh = lse_tile[:, head:head + 1]
                    probs = jnp.exp(s4[rows] + bias_ref[head] - lh)
                    psink = jnp.exp(sm_ref[0, head] - lh)
                    delta = jnp.sum(dyhs[r] * y_rows[r], axis=1, keepdims=True)
                    ds16.append((probs * (dp4[rows] - delta)).astype(BF16))
                    probs16.append(probs.astype(BF16))
                    dsink = dsink + jnp.where((row8 == 0) & (lane8 == head),
                                              -jnp.sum(psink * delta, axis=0, keepdims=True), 0.0)
                ds4 = jnp.concatenate(ds16, axis=0)
                p4 = jnp.concatenate(probs16, axis=0)
                dq4 = jnp.dot(ds4, ks, preferred_element_type=F32)
                low = _half_mask((BLOCK, LANE), 0)
                for k in (0, 1):
                    c = slice((2 * hk + k) * LANE, (2 * hk + k + 1) * LANE)
                    dq_ref[:, c] = jnp.where(low, dq4[2 * k * BLOCK:(2 * k + 1) * BLOCK],
                                             dq4[(2 * k + 1) * BLOCK:(2 * k + 2) * BLOCK]).astype(BF16)
                dk_acc = lax.dot_general(ds4, qm4, TN_DIMS, preferred_element_type=F32)
                dv_acc = lax.dot_general(p4, dy4_16, TN_DIMS, preferred_element_type=F32)
                dk_heads.append(dk_acc + pltpu.roll(dk_acc, HEAD_DIM, 1))
                dv_heads.append(dv_acc + pltpu.roll(dv_acc, HEAD_DIM, 1))
            ds_ref[...] += dsink
            low = _half_mask((2 * BLOCK, LANE), 0)
            for tile in range(2):
                cols = slice(tile * LANE, (tile + 1) * LANE)
                dkt = jnp.where(low, dk_heads[2 * tile], dk_heads[2 * tile + 1])
                dvt = jnp.where(low, dv_heads[2 * tile], dv_heads[2 * tile + 1])
                dkv_ref[:, cols] = (ck_ref[:, cols] + dkt[0:BLOCK, :]).astype(BF16)
                dkv_ref[:, D_KV + tile * LANE:D_KV + (tile + 1) * LANE] = (cv_ref[:, cols] + dvt[0:BLOCK, :]).astype(BF16)
                ck_ref[:, cols] = dkt[BLOCK:2 * BLOCK, :]
                cv_ref[:, cols] = dvt[BLOCK:2 * BLOCK, :]

        @pl.when(n == nb)
        def _():
            dkv_ref[:, 0:D_KV] = ck_ref[...].astype(BF16)
            dkv_ref[:, D_KV:2 * D_KV] = cv_ref[...].astype(BF16)

        finish(n == nb)

    q_spec, k_prev, k_cur, v_prev, v_cur = _attn_specs(nb, True)
    wide = pl.BlockSpec((BLOCK, D_MODEL), lambda n: (jnp.minimum(n, nb - 1), 0))
    outs = pl.pallas_call(
        body, name=name, grid=(nb + 1,),
        in_specs=[pl.BlockSpec(memory_space=pltpu.SMEM), q_spec, k_prev, k_cur, v_prev, v_cur, wide, wide,
                  pl.BlockSpec((BLOCK, LANE), lambda n: (jnp.minimum(n, nb - 1), 0)), wide] + host.in_specs,
        out_specs=[wide, pl.BlockSpec((BLOCK, 2 * D_KV), lambda n: (jnp.maximum(n - 1, 0), 0)), wide,
                   pl.BlockSpec((8, LANE), lambda n: (0, 0))] + host.out_specs,
        out_shape=[jax.ShapeDtypeStruct((t, D_MODEL), BF16), jax.ShapeDtypeStruct((t, 2 * D_KV), BF16),
                   jax.ShapeDtypeStruct((t, D_MODEL), BF16), jax.ShapeDtypeStruct((8, LANE), F32)] + host.out_shapes,
        scratch_shapes=[pltpu.VMEM((BLOCK, D_KV), F32), pltpu.VMEM((BLOCK, D_KV), F32),
                        pltpu.VMEM((N_Q_HEADS, BLOCK, 2 * BLOCK), F32)] + host.scratch_shapes,
        compiler_params=_params("arbitrary"))(sm, qkv, qkv, qkv, qkv, qkv, ag_ml, y, lse, dz, *host.arrays)
    res, landed = host.results(outs, 4)
    return (*res, landed) if ride else tuple(res)


def _local_grads(x, target, p, project, late_weights=None, reduce_out=None, reduce_in=None):
    h, rx_rg, qkv, ag_ml, wt = project(x, p["pre_g"])
    if late_weights is None:
        y_attn, z_attn, lse = _attn_fwd(p["sm"], qkv, ag_ml, "attn_fwd")
    else:
        y_attn, z_attn, lse, landed = _attn_fwd(p["sm"], qkv, ag_ml, "attn_fwd", ride=late_weights[0])
        p = {**p, **late_weights[1](landed)}
    lru = (p["cw"], p["cb"], p["wbd_a"], p["b_a"], p["wbd_x"], p["b_x"], p["lam"])
    y_rnn, z_rnn, *kept_rnn = _rnn_fwd(rx_rg, *lru, "rnn_fwd")
    br_rnn, br_attn, merged, dout, dy, st_post = _branches_fwd(
        z_rnn, z_attn, ag_ml, p["b_gate"], p["w_rnn"], p["w_attn"], p["w_out"], x, target, p["post_g"], "branches_fwd")

    dbr_rnn, dbr_attn, d_ml, dz_rnn, dz_attn, st_merge = _branches_bwd(
        dout, br_rnn, br_attn, ag_ml, p["b_gate"], p["w_rnn"], p["w_attn"], p["w_out"], "branches_bwd")
    out_prods = [(z_rnn, dbr_rnn), (z_attn, dbr_attn), (merged, dout)]
    gw_rnn = gw_attn = gw_out = red_out = red_in = None
    if reduce_out is None:
        gw_rnn, gw_attn, gw_out = (_mm_tn(a, b, nm) for (a, b), nm in zip(out_prods, ("gw_rnn", "gw_attn", "gw_out")))
    else:
        out_pairs = _mm_tn_pairs(out_prods, "gw_outs")
    d_rx, d_rg, st_rnn, g_rg_a, g_rg_x = _rnn_bwd(
        rx_rg, y_rnn, dz_rnn, *kept_rnn, p["cw"], p["wbd_a"], p["wbd_x"], p["lam"], "rnn_bwd")
    if reduce_out is None:
        dq, dkv, d_ag, st_sink = _attn_bwd(p["sm"], qkv, ag_ml, y_attn, lse, dz_attn, "attn_bwd")
    else:
        dq, dkv, d_ag, st_sink, red_out = _attn_bwd(p["sm"], qkv, ag_ml, y_attn, lse, dz_attn, "attn_bwd",
                                                    ride=reduce_out(out_pairs, g_rg_a, g_rg_x))

    segs = [d_rx, d_rg, dq, dkv, d_ag, d_ml]
    if reduce_in is None:
        gwt = _mm_tn_seg(segs, h, "gw_in")
        grad_x, st_pre = _input_grad(segs, wt, x, p["pre_g"], dy, "input_grad")
    else:
        gwt = None
        grad_x, st_pre, red_in = _input_grad(segs, wt, x, p["pre_g"], dy, "input_grad",
                                             ride=reduce_in(*_mm_tn_seg_pair(segs, h, "gw_in")))
    return dict(grad_x=grad_x, gwt=gwt, gw_rnn=gw_rnn, gw_attn=gw_attn, gw_out=gw_out,
                st_post=st_post, st_merge=st_merge, st_rnn=st_rnn, st_sink=st_sink, st_pre=st_pre,
                g_rg_a=g_rg_a, g_rg_x=g_rg_x, red_out=red_out, red_in=red_in)


def _place():
    x, y, c = lax.axis_index("x"), lax.axis_index("y"), lax.axis_index("c")
    return x, y, c


def _gather_ride(shards):
    n = len(shards)

    def copies(ins, outs, sems):
        send_sems, recv_sems, local_sems = sems
        x, y, c = _place()
        me, sibling = (x, y, c), (x, y, 1 - c)
        chips = [(1 - x, y), (x, 1 - y), (1 - x, 1 - y)]

        def slot(a, dev):
            return outs[a].at[4 * dev[0] + 2 * dev[1] + dev[2]]

        def copy(a, k, block, to, src=None):
            return pltpu.make_async_remote_copy(
                src_ref=slot(a, block) if src is None else src, dst_ref=slot(a, block),
                send_sem=send_sems.at[a, k], recv_sem=recv_sems.at[a, k], device_id=to, device_id_type=MESH)

        mine = [pltpu.make_async_copy(ins[a], slot(a, me), local_sems.at[a]) for a in range(n)]
        first = []
        for a in range(n):
            first.append(copy(a, 0, me, sibling, src=ins[a]))
            first += [copy(a, 1 + j, me, (*chip, c), src=ins[a]) for j, chip in enumerate(chips)]
        return me, sibling, chips, c, copy, mine, first

    def start(ins, outs, sems):
        *_, mine, first = copies(ins, outs, sems)
        for cp in mine + first:
            cp.start()

    def middle(ins, outs, sems):
        me, sibling, chips, c, copy, _, _ = copies(ins, outs, sems)
        for j, chip in enumerate(chips):
            for a in range(n):
                copy(a, 1 + j, (*chip, c), me).wait_recv()
                copy(a, 4 + j, (*chip, c), sibling).start()

    def finish(ins, outs, sems):
        me, sibling, chips, c, copy, mine, first = copies(ins, outs, sems)
        passed = [copy(a, 4 + j, (*chip, c), sibling) for j, chip in enumerate(chips) for a in range(n)]
        for a in range(n):
            copy(a, 0, sibling, me).wait_recv()
            for j, chip in enumerate(chips):
                copy(a, 4 + j, (*chip, 1 - c), me).wait_recv()
        for cp in first + passed:
            cp.wait_send()
        for cp in mine:
            cp.wait()

    return _Ride(
        shards, [jax.ShapeDtypeStruct((N_DEV, *s.shape), s.dtype) for s in shards],
        [pltpu.SemaphoreType.DMA((n, 7)), pltpu.SemaphoreType.DMA((n, 7)), pltpu.SemaphoreType.DMA((n,))],
        start, finish, middle)


def _chips_ride(scatter, whole):
    ns, nw = len(scatter), len(whole)
    n = ns + nw

    def copies(ins, outs, sems):
        send_sems, recv_sems, local_sems = sems
        x, y, c = _place()
        own = 2 * x + y
        chips = [(1 - x, y), (x, 1 - y), (1 - x, 1 - y)]

        def src(a, chip_idx):
            return ins[a].at[chip_idx] if a < ns else ins[a]

        local = [pltpu.make_async_copy(src(a, own), outs[a].at[own], local_sems.at[a]) for a in range(n)]
        sent = [pltpu.make_async_remote_copy(
            src_ref=src(a, 2 * chip[0] + chip[1]), dst_ref=outs[a].at[own],
            send_sem=send_sems.at[a, j], recv_sem=recv_sems.at[a, own], device_id=(*chip, c), device_id_type=MESH)
            for a in range(n) for j, chip in enumerate(chips)]
        return chips, c, local, sent

    def start(ins, outs, sems):
        _, _, local, sent = copies(ins, outs, sems)
        for cp in local + sent:
            cp.start()

    def finish(ins, outs, sems):
        send_sems, recv_sems, _ = sems
        chips, c, local, sent = copies(ins, outs, sems)
        for a in range(n):
            for chip in chips:
                k = 2 * chip[0] + chip[1]
                pltpu.make_async_remote_copy(
                    src_ref=outs[a].at[k], dst_ref=outs[a].at[k], send_sem=send_sems.at[a, 0],
                    recv_sem=recv_sems.at[a, k], device_id=(*chip, c), device_id_type=MESH).wait_recv()
        for cp in sent:
            cp.wait_send()
        for cp in local:
            cp.wait()

    return _Ride(
        list(scatter) + list(whole),
        [jax.ShapeDtypeStruct(s.shape, s.dtype) for s in scatter]
        + [jax.ShapeDtypeStruct((N_CHIP, *s.shape), s.dtype) for s in whole],
        [pltpu.SemaphoreType.DMA((n, 3)), pltpu.SemaphoreType.DMA((n, N_CHIP)), pltpu.SemaphoreType.DMA((n,))],
        start, finish)


def _chips_rest_ride(pair, red):
    table = _side_pieces()
    nc = len(CHUNK_ORDER)

    def each(ins, outs, sems, sending, landing):
        send_sems, recv_sems, _ = sems
        pair_ref, red_ref = ins[0], outs[0]
        x, y, c = _place()
        own = 2 * x + y
        for core in (0, 1):
            mine = table[core][0]
            rest = [s for s in range(EARLY_STEPS, nc) if mine[s] is not None]

            @pl.when(c == core)
            def _(mine=mine, rest=rest, core=core):
                for s in rest:
                    _, rows, at = mine[s]
                    k, cp = _to_chip(pair_ref.at[at // SHARD_IN, pl.ds(at % SHARD_IN, rows)], red_ref, mine[s], s,
                                     send_sems, recv_sems, own, core)
                    pl.when(own != k)(lambda cp=cp: sending(cp))
                if landing is not None:
                    for k in range(N_CHIP):
                        @pl.when(own == k)
                        def _(k=k):
                            for s in rest:
                                _, rows, at = mine[s]
                                if at // SHARD_IN == k:
                                    for chip in range(N_CHIP):
                                        if chip != k:
                                            landing(_from_chip(pair_ref.at[k, pl.ds(at % SHARD_IN, rows)], red_ref, mine[s], s,
                                                               send_sems, recv_sems, chip, (x, y, c)))

    def local(ins, outs, sems):
        x, y, _ = _place()
        return pltpu.make_async_copy(ins[0].at[2 * x + y], outs[0].at[2 * x + y], sems[2])

    def start(ins, outs, sems):
        local(ins, outs, sems).start()
        each(ins, outs, sems, lambda cp: cp.start(), None)

    def finish(ins, outs, sems):
        each(ins, outs, sems, lambda cp: cp.wait_send(), lambda cp: cp.wait_recv())
        local(ins, outs, sems).wait()

    return _Ride([pair, red], [jax.ShapeDtypeStruct(red.shape, red.dtype)],
                 [pltpu.SemaphoreType.DMA((nc,)), pltpu.SemaphoreType.DMA((N_CHIP, nc)), pltpu.SemaphoreType.DMA],
                 start, finish, aliases={1: 0})


def _allreduce_small(pack, name):
    shape = pack.shape

    def body(x_ref, o_ref, sib_ref, chip_ref, send_sems, recv_sems):
        x, y, c = _place()
        own = 2 * x + y
        chips = [(1 - x, y), (x, 1 - y), (1 - x, 1 - y)]
        to_sibling = pltpu.make_async_remote_copy(
            src_ref=x_ref, dst_ref=sib_ref, send_sem=send_sems.at[0], recv_sem=recv_sems.at[0],
            device_id=(x, y, 1 - c), device_id_type=MESH)
        to_sibling.start()
        to_sibling.wait()
        chip_ref[own] = x_ref[...] + sib_ref[...]
        sent = [pltpu.make_async_remote_copy(
            src_ref=chip_ref.at[own], dst_ref=chip_ref.at[own], send_sem=send_sems.at[1 + j],
            recv_sem=recv_sems.at[1 + own], device_id=(*chip, c), device_id_type=MESH) for j, chip in enumerate(chips)]
        for cp in sent:
            cp.start()
        for chip in chips:
            k = 2 * chip[0] + chip[1]
            pltpu.make_async_remote_copy(
                src_ref=chip_ref.at[k], dst_ref=chip_ref.at[k], send_sem=send_sems.at[1],
                recv_sem=recv_sems.at[1 + k], device_id=(*chip, c), device_id_type=MESH).wait_recv()
        for cp in sent:
            cp.wait_send()
        o_ref[...] = (chip_ref[0] + chip_ref[1]) + (chip_ref[2] + chip_ref[3])

    return pl.pallas_call(
        body, name=name, out_shape=jax.ShapeDtypeStruct(shape, F32),
        in_specs=[pl.BlockSpec(memory_space=pltpu.VMEM)], out_specs=pl.BlockSpec(memory_space=pltpu.VMEM),
        scratch_shapes=[pltpu.VMEM(shape, F32), pltpu.VMEM((N_CHIP, *shape), F32),
                        pltpu.SemaphoreType.DMA((4,)), pltpu.SemaphoreType.DMA((1 + N_CHIP,))],
    )(pack)


def _adamw(g, w, m, v):
    m = ADAM_B1 * m + (1.0 - ADAM_B1) * g
    v = ADAM_B2 * v + (1.0 - ADAM_B2) * (g * g)
    m_hat = m / (1.0 - ADAM_B1 ** ADAM_STEP)
    v_hat = v / (1.0 - ADAM_B2 ** ADAM_STEP)
    delta = -ADAM_LR * (m_hat / (jnp.sqrt(v_hat) + ADAM_EPS) + ADAM_WD * w)
    return delta, m, v


def _adam_parts(items, name, tr=None):
    ni = len(items)
    npart, r, c = items[0][0].shape
    tr = r if tr is None else min(tr, r)

    def body(*refs):
        for a in range(ni):
            p_ref, w_ref, m_ref, v_ref = refs[4 * a:4 * a + 4]
            g_ref, d_ref, nm_ref, nv_ref = refs[4 * (ni + a):4 * (ni + a) + 4]
            g = p_ref[0].astype(F32)
            for k in range(1, npart):
                g = g + p_ref[k].astype(F32)
            g_ref[...] = g
            d_ref[...], nm_ref[...], nv_ref[...] = _adamw(g, w_ref[...], m_ref[...], v_ref[...])

    tile = pl.BlockSpec((tr, c), lambda i: (i, 0))
    outs = pl.pallas_call(
        body, name=name, grid=(r // tr,),
        in_specs=[pl.BlockSpec((npart, tr, c), lambda i: (0, i, 0)), tile, tile, tile] * ni,
        out_specs=[tile] * (4 * ni), out_shape=[jax.ShapeDtypeStruct((r, c), F32)] * (4 * ni),
        compiler_params=_params("parallel"))(*[arr for item in items for arr in item])
    return [outs[4 * a:4 * a + 4] for a in range(ni)]


def _block_diag(w):
    w4 = w.reshape(N_GROUPS, 4, RNN_BLOCK_W, RNN_BLOCK_W)
    eye = jnp.eye(4, dtype=w.dtype)
    return jnp.einsum("gbij,bc->gbicj", w4, eye).reshape(N_GROUPS, GROUP_W, GROUP_W).astype(BF16)


SMALL_ROWS = 16
ROW_PRE_G, ROW_BGATE, ROW_CONV_B, ROW_B_A, ROW_B_X, ROW_LAM, ROW_POST_G, ROW_LOSS, ROW_SINKS, ROW_CONV_W = 0, 1, 3, 4, 5, 6, 7, 8, 9, 10


def _pack_stats(st_pre, st_merge, st_rnn, st_post, st_sink, name):
    d = D_MODEL

    def body(pre_ref, mg_ref, rnn_ref, post_ref, sink_ref, o_ref):
        rnn = rnn_ref[...]
        sinks = jnp.concatenate([sink_ref[0:1, :], jnp.zeros((1, d - LANE), F32)], axis=1)
        o_ref[0:8, :] = _rows8([pre_ref[0:1, :], mg_ref[0:1, :], mg_ref[1:2, :], rnn[0:1], rnn[1:2], rnn[2:3], rnn[3:4],
                                post_ref[0:1, :]], d)
        o_ref[8:16, :] = _rows8([post_ref[1:2, :], sinks, rnn[4:5], rnn[5:6], rnn[6:7], rnn[7:8]], d)

    return pl.pallas_call(body, name=name, out_shape=jax.ShapeDtypeStruct((SMALL_ROWS, d), F32))(
        st_pre, st_merge, st_rnn, st_post, st_sink)


def _adam_small(total, w, m, v, name):
    d = D_MODEL
    n_in = len(w)
    rows = (ROW_PRE_G, ROW_BGATE, ROW_CONV_B, ROW_B_A, ROW_B_X, ROW_LAM, ROW_POST_G, ROW_SINKS)

    def grad_of(p_ref, row, shape):
        if shape == (1, 2 * d):
            return jnp.concatenate([p_ref[row:row + 1, :], p_ref[row + 1:row + 2, :]], axis=1)
        if shape == (1, RNN_BLOCKS, RNN_BLOCK_W):
            r = p_ref[row:row + 1, :]
            return jnp.concatenate([r[:, b * RNN_BLOCK_W:(b + 1) * RNN_BLOCK_W] for b in range(RNN_BLOCKS)], axis=0)[None]
        return p_ref[row:row + 1, 0:shape[1]]

    def body(*refs):
        p_ref = refs[0]
        w_refs, m_refs, v_refs = (refs[1 + k * n_in:1 + (k + 1) * n_in] for k in range(3))
        outs = refs[1 + 3 * n_in:]
        for k in range(n_in):
            g = grad_of(p_ref, rows[k], w[k].shape)
            res = (g,) + _adamw(g, w_refs[k][...], m_refs[k][...], v_refs[k][...])
            for kind in range(4):
                outs[kind * n_in + k][...] = res[kind]

    outs = pl.pallas_call(
        body, name=name, out_shape=[jax.ShapeDtypeStruct(a.shape, F32) for _ in range(4) for a in w])(total, *w, *m, *v)
    return [outs[kind * n_in:(kind + 1) * n_in] for kind in range(4)]


def kernel(x, pre_norm_g, w_in, b_gate, conv_w, conv_b, w_rg_a, b_rg_a, w_rg_x, b_rg_x, lru_lambda, attn_sinks, w_rnn_out, w_attn_out, w_out, post_norm_g, loss_target, m_pre_norm_g, m_w_in, m_b_gate, m_conv_w, m_conv_b, m_w_rg_a, m_b_rg_a, m_w_rg_x, m_b_rg_x, m_lru_lambda, m_attn_sinks, m_w_rnn_out, m_w_attn_out, m_w_out, m_post_norm_g, v_pre_norm_g, v_w_in, v_b_gate, v_conv_w, v_conv_b, v_w_rg_a, v_b_rg_a, v_w_rg_x, v_b_rg_x, v_lru_lambda, v_attn_sinks, v_w_rnn_out, v_w_attn_out, v_w_out, v_post_norm_g):
    cx, cy, cc = _place()
    dev = 4 * cx + 2 * cy + cc

    w_in_t, m_in_t, v_in_t = (jnp.transpose(a[0]) for a in (w_in, m_w_in, v_w_in))
    wt_shard = w_in_t.astype(BF16)

    def project(xs, pre_g):
        h, rx_rg, qkv, ag_ml, wt_all = _gather_project(xs, pre_g, wt_shard, "gather_project")
        return h, rx_rg, qkv, ag_ml, wt_all.reshape(D_IN, D_MODEL)

    def late_unpack(landed):
        w_rnn_all, w_attn_all, w_out_all, cw_all = landed
        return dict(w_rnn=w_rnn_all.reshape(D_RNN, D_MODEL), w_attn=w_attn_all.reshape(D_MODEL, D_MODEL),
                    w_out=w_out_all.reshape(D_MODEL, D_MODEL), cw=jnp.transpose(cw_all, (1, 0, 2)).reshape(4, D_RNN))

    late_weights = (_gather_ride([w_rnn_out[0].astype(BF16), w_attn_out[0].astype(BF16), w_out[0].astype(BF16), conv_w[0]]),
                    late_unpack)

    heads = jnp.arange(1, N_Q_HEADS + 1, dtype=F32)
    slopes = jnp.exp2(-ALIBI_MAX_BIAS * heads / N_Q_HEADS)
    b_a = b_rg_a.reshape(1, D_RNN)
    b_x = b_rg_x.reshape(1, D_RNN)
    p = dict(
        pre_g=pre_norm_g, post_g=post_norm_g, b_gate=b_gate, cb=conv_b,
        wbd_a=_block_diag(w_rg_a[0]), b_a=b_a, wbd_x=_block_diag(w_rg_x[0]), b_x=b_x, lam=lru_lambda,
        sm=jnp.pad(attn_sinks, ((0, 1), (0, 0))) + jnp.pad(slopes[None, :], ((1, 0), (0, 0))))

    flat = (RNN_BLOCKS * RNN_BLOCK_W, RNN_BLOCK_W)

    def reduce_out(out_pairs, g_rg_a, g_rg_x):
        return _join_rides(_chips_ride(out_pairs, []), _gather_ride([g_rg_a.reshape(flat), g_rg_x.reshape(flat)]))

    reduce_in = _chips_rest_ride

    g = _local_grads(x[0], loss_target[0], p, project, late_weights, reduce_out, reduce_in)
    small = _allreduce_small(
        _pack_stats(g["st_pre"], g["st_merge"], g["st_rnn"], g["st_post"], g["st_sink"], "pack_stats"), "allreduce_small")

    out = {}
    red = g["red_out"]
    w_in_out, = _adam_parts([(g["red_in"][0], w_in_t, m_in_t, v_in_t)], "adam_w_in", tr=SHARD_IN // 2)
    out["w_in"] = [jnp.transpose(o) for o in w_in_out]
    out["w_rnn_out"], out["w_attn_out"], out["w_out"] = _adam_parts(
        [(red[0], w_rnn_out[0], m_w_rnn_out[0], v_w_rnn_out[0]), (red[1], w_attn_out[0], m_w_attn_out[0], v_w_attn_out[0]),
         (red[2], w_out[0], m_w_out[0], v_w_out[0])], "adam_w_outs")
    out["w_rg_a"], out["w_rg_x"] = _adam_parts(
        [(red[3], w_rg_a.reshape(flat), m_w_rg_a.reshape(flat), v_w_rg_a.reshape(flat)),
         (red[4], w_rg_x.reshape(flat), m_w_rg_x.reshape(flat), v_w_rg_x.reshape(flat))], "adam_w_rg", tr=256)

    small_names = ("pre_norm_g", "b_gate", "conv_b", "b_rg_a", "b_rg_x", "lru_lambda", "post_norm_g", "attn_sinks")
    small_out = _adam_small(
        small,
        (pre_norm_g, b_gate, conv_b, b_rg_a, b_rg_x, lru_lambda, post_norm_g, attn_sinks),
        (m_pre_norm_g, m_b_gate, m_conv_b, m_b_rg_a, m_b_rg_x, m_lru_lambda, m_post_norm_g, m_attn_sinks),
        (v_pre_norm_g, v_b_gate, v_conv_b, v_b_rg_a, v_b_rg_x, v_lru_lambda, v_post_norm_g, v_attn_sinks),
        "adam_small")
    g_cw = lax.dynamic_slice(small[ROW_CONV_W:ROW_CONV_W + 4], (0, dev * SHARD_OUT), (4, SHARD_OUT))
    out["conv_w"], = _adam_parts([(g_cw[None], conv_w[0], m_conv_w[0], v_conv_w[0])], "adam_conv_w")

    shapes = dict(w_in=(1, D_MODEL, SHARD_IN), w_rnn_out=(1, SHARD_OUT, D_MODEL), w_attn_out=(1, SHARD_OUT, D_MODEL),
                  w_out=(1, SHARD_OUT, D_MODEL), w_rg_a=(1, RNN_BLOCKS, RNN_BLOCK_W, RNN_BLOCK_W),
                  w_rg_x=(1, RNN_BLOCKS, RNN_BLOCK_W, RNN_BLOCK_W), conv_w=(1, 4, SHARD_OUT))
    weights = ["pre_norm_g", "w_in", "b_gate", "conv_w", "conv_b", "w_rg_a", "b_rg_a", "w_rg_x", "b_rg_x",
               "lru_lambda", "attn_sinks", "w_rnn_out", "w_attn_out", "w_out", "post_norm_g"]
    results = []
    for kind in range(4):
        for name in weights:
            if name in out:
                results.append(out[name][kind].reshape(shapes[name]))
            else:
                results.append(small_out[kind][small_names.index(name)])
    loss = 0.5 / D_MODEL * jnp.sum(small[ROW_LOSS])
    return (loss, g["grad_x"][None], *results)
```

```python
import functools

import jax
import jax.numpy as jnp
from jax import lax
from jax.experimental import pallas as pl
from jax.experimental.pallas import tpu as pltpu

F32 = jnp.float32
BF16 = jnp.bfloat16

D_MODEL = 1024
D_RNN = 1024
RNN_BLOCKS = 16
RNN_BLOCK_W = 64
LRU_C = 8.0
N_Q_HEADS = 16
HEAD_DIM = 64
D_KV = 256
BLOCK = 128
ALIBI_MAX_BIAS = 8.0
EPS = 1e-6
D_IN = 6656
N_DEV = 8
N_CHIP = 4
SHARD_IN = D_IN // N_DEV
SHARD_OUT = D_MODEL // N_DEV
ATTN_SCALE = HEAD_DIM ** -0.5
MASKED = -1e30

ADAM_LR = 0.001
ADAM_B1 = 0.9
ADAM_B2 = 0.999
ADAM_EPS = 1e-08
ADAM_WD = 0.01
ADAM_STEP = 10

VMEM_LIMIT_BYTES = 52 * 1024 * 1024
LANE = 128
GROUP_W = 256
N_GROUPS = D_RNN // GROUP_W
SEG_CHUNK = 512

NT_DIMS = (((1,), (1,)), ((), ()))
TN_DIMS = (((0,), (0,)), ((), ()))
MESH = pl.DeviceIdType.MESH
ANY = pl.BlockSpec(memory_space=pl.ANY)


def _params(*semantics):
    return pltpu.CompilerParams(dimension_semantics=semantics, vmem_limit_bytes=VMEM_LIMIT_BYTES)


def _sigmoid(x):
    return 0.5 * jnp.tanh(0.5 * x) + 0.5


def _log1p(e):
    u = 1.0 + e
    den = jnp.where(u == 1.0, 1.0, u - 1.0)
    return jnp.where(u == 1.0, e, jnp.log(u) * (e / den))


def _softplus(z):
    return jnp.maximum(z, 0.0) + _log1p(jnp.exp(-jnp.abs(z)))


def _rows8(rows, width):
    idx = lax.broadcasted_iota(jnp.int32, (8, width), 0)
    out = jnp.zeros((8, width), F32)
    for r, v in enumerate(rows):
        out = jnp.where(idx == r, v, out)
    return out


class _Ride:
    def __init__(self, arrays, out_shapes, scratch_shapes, start, finish, middle=None, aliases=None):
        self.arrays, self.out_shapes, self.scratch_shapes = list(arrays), list(out_shapes), list(scratch_shapes)
        self.start, self.finish, self.middle = start, finish, middle
        self.aliases = dict(aliases or {})


class _Hosted:
    def __init__(self, ride, n_in, n_out, n_scratch=0, aliasing=False):
        self.ride = ride
        assert aliasing or not (ride and ride.aliases), "this host does not alias"
        self.aliases = {n_in + i: n_out + o for i, o in ride.aliases.items()} if ride else {}
        self.sizes = (n_in, len(ride.arrays) if ride else 0, n_out, len(ride.out_shapes) if ride else 0, n_scratch)
        self.arrays = ride.arrays if ride else []
        self.in_specs = [ANY] * len(self.arrays)
        self.out_shapes = ride.out_shapes if ride else []
        self.out_specs = [ANY] * len(self.out_shapes)
        self.scratch_shapes = ride.scratch_shapes if ride else []

    def split(self, refs):
        n_in, r_in, n_out, r_out, n_scr = self.sizes
        cuts = [0, n_in, n_in + r_in, n_in + r_in + n_out, n_in + r_in + n_out + r_out, n_in + r_in + n_out + r_out + n_scr]
        host_in, ride_in, host_out, ride_out, host_scr = (refs[cuts[k]:cuts[k + 1]] for k in range(5))
        ride_scr = refs[cuts[5]:]

        def start(when):
            if self.ride is not None:
                pl.when(when)(lambda: self.ride.start(ride_in, ride_out, ride_scr))

        def finish(when):
            if self.ride is not None:
                pl.when(when)(lambda: self.ride.finish(ride_in, ride_out, ride_scr))

        def middle(when):
            if self.ride is not None and self.ride.middle is not None:
                pl.when(when)(lambda: self.ride.middle(ride_in, ride_out, ride_scr))

        start.middle = middle
        return tuple(host_in) + tuple(host_out) + tuple(host_scr), start, finish

    def results(self, outs, n_out):
        outs = list(outs) if isinstance(outs, (list, tuple)) else [outs]
        return outs[:n_out], outs[n_out:]


def _join_rides(a, b):
    na, nb_ = len(a.arrays), len(b.arrays)
    oa = len(a.out_shapes)
    sa = len(a.scratch_shapes)

    def both(fa, fb):
        def run(ins, outs, sems):
            if fa is not None:
                fa(ins[:na], outs[:oa], sems[:sa])
            if fb is not None:
                fb(ins[na:na + nb_], outs[oa:], sems[sa:])
        return run

    middle = both(a.middle, b.middle) if (a.middle or b.middle) else None
    return _Ride(a.arrays + b.arrays, a.out_shapes + b.out_shapes, a.scratch_shapes + b.scratch_shapes,
                 both(a.start, b.start), both(a.finish, b.finish), middle)


def _load_resident(w_hbm, w_vmem, sems, first):
    def piece(c):
        rows = pl.ds(c * SEG_CHUNK, SEG_CHUNK)
        return pltpu.make_async_copy(w_hbm.at[rows], w_vmem.at[rows], sems.at[c])

    @pl.when(first)
    def _():
        for c in range(w_vmem.shape[0] // SEG_CHUNK):
            piece(c).start()

    def ready(c):
        @pl.when(first)
        def _():
            piece(c).wait()

    return ready


CHIP_ROWS = 2 * SHARD_IN
PROJ_WIDTHS = (2 * D_RNN, D_MODEL + 2 * D_KV, 3 * D_MODEL)
PROJ_DTYPES = (F32, BF16, BF16)


def _chip_pieces():
    starts = [0, PROJ_WIDTHS[0], PROJ_WIDTHS[0] + PROJ_WIDTHS[1], D_IN]
    pieces = []
    for k in range(N_CHIP):
        lo, hi = k * CHIP_ROWS, (k + 1) * CHIP_ROWS
        cur = []
        for a in range(len(PROJ_WIDTHS)):
            s0, s1 = max(lo, starts[a]), min(hi, starts[a + 1])
            if s0 < s1:
                cur.append((a, s0 - starts[a], s1 - s0, s0 - lo))
        pieces.append(cur)
    return pieces


def _gather_project(x, g, wt_shard, name, tm=1024):
    t, k = x.shape
    tm = min(tm, t)
    nt = t // tm
    pieces = _chip_pieces()

    def body(x_ref, g_ref, shard_ref, h_out, rx_ref, qkv_ref, ag_ref, wt_all,
             w_c, stage, o32, o16, h_all, send_sems, recv_sems, local_sem, stage_sems, out_sems, h_sems):
        s, ti = pl.program_id(0), pl.program_id(1)
        px, py, pc = _place()
        me, sibling = (px, py, pc), (px, py, 1 - pc)
        chips = [(px, py), (1 - px, py), (px, 1 - py), (1 - px, 1 - py)]
        outs = (rx_ref, qkv_ref, ag_ref)

        def slot(dev):
            return wt_all.at[4 * dev[0] + 2 * dev[1] + dev[2]]

        def copy(kk, block, to, src=None):
            return pltpu.make_async_remote_copy(
                src_ref=slot(block) if src is None else src, dst_ref=slot(block),
                send_sem=send_sems.at[kk], recv_sem=recv_sems.at[kk], device_id=to, device_id_type=MESH)

        mine = pltpu.make_async_copy(shard_ref, slot(me), local_sem)
        first = [copy(0, me, sibling, src=shard_ref)] + [copy(1 + j, me, (*chips[1 + j], pc), src=shard_ref) for j in range(3)]
        passed = [copy(4 + j, (*chips[1 + j], pc), sibling) for j in range(3)]

        @pl.when((s == 0) & (ti == 0))
        def _():
            mine.start()
            for cp in first[:3]:
                cp.start()

        def pass_on(step):
            copy(step, (*chips[step], pc), me).wait_recv()
            passed[step - 1].start()

        @pl.when((s == 2) & (ti == nt - 1))
        def _():
            pass_on(3)

        for step in range(N_CHIP):
            @pl.when((s == step) & (ti == 0))
            def _(step=step):
                chip = chips[step]
                if step == 0:
                    mine.wait()
                    copy(0, sibling, me).wait_recv()
                else:
                    if step == 1:
                        pass_on(1)
                        first[3].start()
                        pass_on(2)
                    copy(3 + step, (*chip, 1 - pc), me).wait_recv()
                loads = [pltpu.make_async_copy(slot((*chip, core)), stage.at[pl.ds(core * SHARD_IN, SHARD_IN)], stage_sems.at[core])
                         for core in (0, 1)]
                for cp in loads:
                    cp.start()
                for cp in loads:
                    cp.wait()
                w_c[...] = stage[...].T

        rows = pl.ds(pl.multiple_of(ti * tm, tm), tm)

        @pl.when(s == 0)
        def _():
            xv = x_ref[...]
            h_all[rows, :] = (xv * lax.rsqrt(jnp.mean(xv * xv, axis=-1, keepdims=True) + EPS) * g_ref[...]).astype(BF16)

        res = jnp.dot(h_all[rows, :], w_c[...], preferred_element_type=F32)

        n = s * nt + ti
        buf = lax.rem(n, 2)
        chip_idx = [2 * cx + cy for cx, cy in chips]

        def chip_at(step):
            return jnp.where(step == 0, chip_idx[0], jnp.where(step == 1, chip_idx[1], jnp.where(step == 2, chip_idx[2], chip_idx[3])))

        def h_write(b, tile):
            return pltpu.make_async_copy(h_all.at[pl.ds(tile * tm, tm)], h_out.at[pl.ds(tile * tm, tm)], h_sems.at[b])

        def writes(kchip, b, tile):
            cps = []
            for idx, (a, col, w, src) in enumerate(pieces[kchip]):
                staged = (o16 if PROJ_DTYPES[a] == BF16 else o32).at[b, :, pl.ds(src, w)]
                cps.append(pltpu.make_async_copy(staged, outs[a].at[pl.ds(tile * tm, tm), pl.ds(col, w)], out_sems.at[b, idx]))
            return cps

        o32[buf] = res
        o16[buf] = res.astype(BF16)
        kcur = chip_at(s)

        @pl.when(n > 0)
        def _():
            kprev = chip_at(lax.div(n - 1, nt))
            for kchip in range(N_CHIP):
                @pl.when(kprev == kchip)
                def _(kchip=kchip):
                    for cp in writes(kchip, 1 - buf, lax.rem(n - 1, nt)):
                        cp.wait()

            @pl.when(n <= nt)
            def _():
                h_write(1 - buf, n - 1).wait()

        @pl.when(s == 0)
        def _():
            h_write(buf, ti).start()

        for kchip in range(N_CHIP):
            @pl.when(kcur == kchip)
            def _(kchip=kchip):
                for cp in writes(kchip, buf, ti):
                    cp.start()

        @pl.when(n == N_CHIP * nt - 1)
        def _():
            for kchip in range(N_CHIP):
                @pl.when(kcur == kchip)
                def _(kchip=kchip):
                    for cp in writes(kchip, buf, ti):
                        cp.wait()
            for cp in first + passed:
                cp.wait_send()

    tile = pl.BlockSpec((tm, k), lambda s, ti: (jnp.where(s == 0, ti, nt - 1), 0))
    return pl.pallas_call(
        body, name=name, grid=(N_CHIP, nt),
        in_specs=[tile, pl.BlockSpec((1, k), lambda s, ti: (0, 0)), ANY],
        out_specs=[ANY, ANY, ANY, ANY, ANY],
        out_shape=[jax.ShapeDtypeStruct((t, k), BF16)]
        + [jax.ShapeDtypeStruct((t, w), dt) for w, dt in zip(PROJ_WIDTHS, PROJ_DTYPES)]
        + [jax.ShapeDtypeStruct((N_DEV, SHARD_IN, k), BF16)],
        scratch_shapes=[pltpu.VMEM((k, CHIP_ROWS), BF16), pltpu.VMEM((CHIP_ROWS, k), BF16),
                        pltpu.VMEM((2, tm, CHIP_ROWS), F32), pltpu.VMEM((2, tm, CHIP_ROWS), BF16), pltpu.VMEM((t, k), BF16),
                        pltpu.SemaphoreType.DMA((7,)), pltpu.SemaphoreType.DMA((7,)), pltpu.SemaphoreType.DMA,
                        pltpu.SemaphoreType.DMA((2,)), pltpu.SemaphoreType.DMA((2, 2)), pltpu.SemaphoreType.DMA((2,))],
        compiler_params=_params("arbitrary", "arbitrary"))(x, g, wt_shard)


def _mm_tn(a, b, name, tm=512, tk=4096):
    ktok, m = a.shape
    n = b.shape[1]
    tk = min(tk, ktok)

    def body(a_ref, b_ref, o_ref):
        @pl.when(pl.program_id(1) == 0)
        def _():
            o_ref[...] = jnp.zeros_like(o_ref)

        o_ref[...] += lax.dot_general(a_ref[...], b_ref[...], TN_DIMS, preferred_element_type=F32)

    return pl.pallas_call(
        body, name=name, grid=(m // tm, ktok // tk),
        in_specs=[pl.BlockSpec((tk, tm), lambda i, kk: (kk, i)), pl.BlockSpec((tk, n), lambda i, kk: (kk, 0))],
        out_specs=pl.BlockSpec((tm, n), lambda i, kk: (i, 0)),
        out_shape=jax.ShapeDtypeStruct((m, n), F32),
        compiler_params=_params("parallel", "arbitrary"))(a, b)


def _mm_tn_pairs(prods, name):
    n_prod = len(prods)
    ktok, m = prods[0][0].shape
    n = prods[0][1].shape[1]
    half, blk = m // 2, m // N_DEV
    ns = 2 * n_prod
    kept = N_CHIP * blk

    def side_pieces(s, side):
        i, hf = divmod(s, 2)
        first = hf * N_DEV // 2
        return [((d - first) * blk, i * kept + (d // 2) * blk) for d in range(first, first + N_DEV // 2) if d % 2 == side]

    def body(*refs):
        a_hbm, b_hbm, pair_refs = refs[:n_prod], refs[n_prod:2 * n_prod], refs[2 * n_prod:3 * n_prod]
        a_buf, b_buf, res_buf, recv_all, pair_all, a_sems, b_sems, send_sems, recv_sems, out_sems = refs[3 * n_prod:]
        step = pl.program_id(0)
        px, py, pc = _place()

        def fetch_a(s):
            return pltpu.make_async_copy(a_hbm[s // 2].at[:, pl.ds((s % 2) * half, half)], a_buf.at[s % 2], a_sems.at[s % 2])

        def fetch_b(i):
            return pltpu.make_async_copy(b_hbm[i], b_buf.at[i % 2], b_sems.at[i % 2])

        @pl.when(step == 0)
        def _():
            fetch_a(0).start()
            fetch_b(0).start()

        for s in range(ns):
            @pl.when(step == s)
            def _(s=s):
                if s + 1 < ns:
                    fetch_a(s + 1).start()
                if s % 2 == 0 and s // 2 + 1 < n_prod:
                    fetch_b(s // 2 + 1).start()
                fetch_a(s).wait()
                if s % 2 == 0:
                    fetch_b(s // 2).wait()

        res_buf[step % 2] = lax.dot_general(a_buf[step % 2], b_buf[(step // 2) % 2], TN_DIMS, preferred_element_type=F32)

        def crossing(s, j, piece):
            off, at = piece
            return pltpu.make_async_remote_copy(
                src_ref=res_buf.at[s % 2, pl.ds(off, blk)], dst_ref=recv_all.at[pl.ds(at, blk)],
                send_sem=send_sems.at[s, j], recv_sem=recv_sems.at[s, j], device_id=(px, py, 1 - pc), device_id_type=MESH)

        def write(s, core):
            at = side_pieces(s, core)[0][1]
            return pltpu.make_async_copy(pair_all.at[pl.ds(at, 2 * blk)], pair_refs[s // 2].at[pl.ds(at % kept, 2 * blk)], out_sems.at[s])

        for core in (0, 1):
            def settle(s, core=core):
                for j, piece in enumerate(side_pieces(s, 1 - core)):
                    crossing(s, j, piece).wait_send()
                for j, (off, at) in enumerate(side_pieces(s, core)):
                    crossing(s, j, (off, at)).wait_recv()
                    pair_all[at:at + blk] = (res_buf[s % 2, off:off + blk] + recv_all[at:at + blk]).astype(BF16)
                write(s, core).start()

            for s in range(ns):
                @pl.when((pc == core) & (step == s))
                def _(s=s, settle=settle, core=core):
                    for j, piece in enumerate(side_pieces(s, 1 - core)):
                        crossing(s, j, piece).start()
                    if s > 0:
                        settle(s - 1)
                    if s == ns - 1:
                        settle(s)
                        for t in range(ns):
                            write(t, core).wait()

    pairs = pl.pallas_call(
        body, name=name, grid=(ns,),
        in_specs=[ANY] * (2 * n_prod), out_specs=[ANY] * n_prod,
        out_shape=[jax.ShapeDtypeStruct((kept, n), BF16)] * n_prod,
        scratch_shapes=[pltpu.VMEM((2, ktok, half), prods[0][0].dtype), pltpu.VMEM((2, ktok, n), prods[0][1].dtype),
                        pltpu.VMEM((2, half, n), F32), pltpu.VMEM((n_prod * kept, n), F32), pltpu.VMEM((n_prod * kept, n), BF16),
                        pltpu.SemaphoreType.DMA((2,)), pltpu.SemaphoreType.DMA((2,)),
                        pltpu.SemaphoreType.DMA((ns, 2)), pltpu.SemaphoreType.DMA((ns, 2)), pltpu.SemaphoreType.DMA((ns,))],
        compiler_params=_params("arbitrary"))(*[a for a, _ in prods], *[b for _, b in prods])
    return [pair.reshape(N_CHIP, blk, n) for pair in pairs]


def _segment_chunks(segs):
    bounds = [0]
    for s in segs:
        bounds.append(bounds[-1] + s.shape[1] // SEG_CHUNK)
    return bounds


def _input_grad(segs, wt, x, g, dy, name, tm=512, ride=None):
    m = segs[0].shape[0]
    rows, n = wt.shape
    tm = min(tm, m)
    bounds = _segment_chunks(segs)
    n_seg = len(segs)
    ni = m // tm
    host = _Hosted(ride, n_seg + 4, 2, 2, aliasing=True)

    def body(*refs):
        host_refs, start, finish = host.split(refs)
        a_refs = host_refs[:n_seg]
        wt_hbm, x_ref, g_ref, dy_ref, gx_ref, st_ref, wt_vmem, sems = host_refs[n_seg:]
        i = pl.program_id(0)
        start(i == 0)

        @pl.when(i == 0)
        def _():
            st_ref[...] = jnp.zeros_like(st_ref)

        ready = _load_resident(wt_hbm, wt_vmem, sems, i == 0)
        dh = None
        for s in range(n_seg):
            for c in range(bounds[s], bounds[s + 1]):
                ready(c)
            part = jnp.dot(a_refs[s][...], wt_vmem[bounds[s] * SEG_CHUNK:bounds[s + 1] * SEG_CHUNK, :], preferred_element_type=F32)
            dh = part if dh is None else dh + part
        xv = x_ref[...]
        r = lax.rsqrt(jnp.mean(xv * xv, axis=-1, keepdims=True) + EPS)
        xn = xv * r
        dxn = dh * g_ref[...]
        gx_ref[...] = dy_ref[...] + r * (dxn - xn * jnp.mean(dxn * xn, axis=-1, keepdims=True))
        st_ref[...] += _rows8([jnp.sum(dh * xn, axis=0, keepdims=True)], n)
        finish(i == ni - 1)

    tile = pl.BlockSpec((tm, n), lambda i: (i, 0))
    outs = pl.pallas_call(
        body, name=name, grid=(ni,),
        in_specs=[pl.BlockSpec((tm, sg.shape[1]), lambda i: (i, 0)) for sg in segs]
        + [ANY, tile, pl.BlockSpec((1, n), lambda i: (0, 0)), tile] + host.in_specs,
        out_specs=[tile, pl.BlockSpec((8, n), lambda i: (0, 0))] + host.out_specs,
        out_shape=[jax.ShapeDtypeStruct((m, n), F32), jax.ShapeDtypeStruct((8, n), F32)] + host.out_shapes,
        scratch_shapes=[pltpu.VMEM((rows, n), wt.dtype), pltpu.SemaphoreType.DMA((rows // SEG_CHUNK,))] + host.scratch_shapes,
        input_output_aliases=host.aliases,
        compiler_params=_params("arbitrary"))(*segs, wt, x, g, dy, *host.arrays)
    res, landed = host.results(outs, 2)
    return (*res, landed) if ride else tuple(res)


def _mm_tn_seg(segs, b, name):
    ktok = segs[0].shape[0]
    n = b.shape[1]
    bounds = _segment_chunks(segs)
    n_seg = len(segs)
    nc = bounds[-1]
    seg_of = [s for s in range(n_seg) for _ in range(bounds[s], bounds[s + 1])]

    def body(*refs):
        a_hbm, b_hbm, o_ref = refs[:n_seg], refs[n_seg], refs[n_seg + 1]
        a_buf, b_vmem, a_sems, b_sem = refs[n_seg + 2:]
        c = pl.program_id(0)

        def fetch(cc):
            s = seg_of[cc]
            cols = pl.ds((cc - bounds[s]) * SEG_CHUNK, SEG_CHUNK)
            return pltpu.make_async_copy(a_hbm[s].at[:, cols], a_buf.at[cc % 2], a_sems.at[cc % 2])

        @pl.when(c == 0)
        def _():
            whole = pltpu.make_async_copy(b_hbm, b_vmem, b_sem)
            whole.start()
            fetch(0).start()
            whole.wait()

        for cc in range(nc):
            @pl.when(c == cc)
            def _(cc=cc):
                if cc + 1 < nc:
                    fetch(cc + 1).start()
                fetch(cc).wait()

        o_ref[...] = lax.dot_general(a_buf[c % 2], b_vmem[...], TN_DIMS, preferred_element_type=F32)

    return pl.pallas_call(
        body, name=name, grid=(nc,),
        in_specs=[ANY] * (n_seg + 1), out_specs=pl.BlockSpec((SEG_CHUNK, n), lambda c: (c, 0)),
        out_shape=jax.ShapeDtypeStruct((nc * SEG_CHUNK, n), F32),
        scratch_shapes=[pltpu.VMEM((2, ktok, SEG_CHUNK), segs[0].dtype), pltpu.VMEM((ktok, n), b.dtype),
                        pltpu.SemaphoreType.DMA((2,)), pltpu.SemaphoreType.DMA],
        compiler_params=_params("arbitrary"))(*segs, b)


CHUNK_ORDER = (0, 4, 7, 10, 1, 5, 8, 11, 2, 6, 9, 12, 3)
EARLY_STEPS = 8


def _side_pieces():
    table = []
    for core in (0, 1):
        sides = ([None] * len(CHUNK_ORDER), [None] * len(CHUNK_ORDER))
        for s, cc in enumerate(CHUNK_ORDER):
            g0 = cc * SEG_CHUNK
            for d in range(N_DEV):
                lo, hi = max(g0, d * SHARD_IN), min(g0 + SEG_CHUNK, (d + 1) * SHARD_IN)
                if lo < hi:
                    side = sides[0 if d % 2 == core else 1]
                    assert side[s] is None
                    side[s] = (lo - g0, hi - lo, (d // 2) * SHARD_IN + lo - d * SHARD_IN)
        table.append(sides)
    return table


def _to_chip(src, red_ref, piece, s, send_sems, recv_sems, own, core):
    _, rows, at = piece
    k, r0 = divmod(at, SHARD_IN)
    return k, pltpu.make_async_remote_copy(
        src_ref=src, dst_ref=red_ref.at[own, pl.ds(r0, rows)], send_sem=send_sems.at[s], recv_sem=recv_sems.at[own, s],
        device_id=(k // 2, k % 2, core), device_id_type=MESH)


def _from_chip(src, red_ref, piece, s, send_sems, recv_sems, chip, me):
    _, rows, at = piece
    return pltpu.make_async_remote_copy(
        src_ref=src, dst_ref=red_ref.at[chip, pl.ds(at % SHARD_IN, rows)], send_sem=send_sems.at[s], recv_sem=recv_sems.at[chip, s],
        device_id=me, device_id_type=MESH)


def _mm_tn_seg_pair(segs, b, name):
    ktok = segs[0].shape[0]
    n = b.shape[1]
    bounds = _segment_chunks(segs)
    n_seg = len(segs)
    nc = bounds[-1]
    assert nc == len(CHUNK_ORDER)
    seg_of = [s for s in range(n_seg) for _ in range(bounds[s], bounds[s + 1])]
    table = _side_pieces()

    def body(*refs):
        a_hbm, b_hbm, pair_ref, red_ref = refs[:n_seg], refs[n_seg], refs[n_seg + 1], refs[n_seg + 2]
        (a_buf, b_vmem, res_buf, recv_all, pair_all, a_sems, b_sem, send_sems, recv_sems, out_sems,
         chip_send, chip_recv) = refs[n_seg + 3:]
        step = pl.program_id(0)
        px, py, pc = _place()
        own = 2 * px + py

        def fetch(s):
            sg = seg_of[CHUNK_ORDER[s]]
            cols = pl.ds((CHUNK_ORDER[s] - bounds[sg]) * SEG_CHUNK, SEG_CHUNK)
            return pltpu.make_async_copy(a_hbm[sg].at[:, cols], a_buf.at[s % 2], a_sems.at[s % 2])

        @pl.when(step == 0)
        def _():
            whole = pltpu.make_async_copy(b_hbm, b_vmem, b_sem)
            whole.start()
            fetch(0).start()
            whole.wait()

        for s in range(nc):
            @pl.when(step == s)
            def _(s=s):
                if s + 1 < nc:
                    fetch(s + 1).start()
                fetch(s).wait()

        res_buf[step % 2] = lax.dot_general(a_buf[step % 2], b_vmem[...], TN_DIMS, preferred_element_type=F32)

        def crossing(s, piece):
            off, rows, at = piece
            return pltpu.make_async_remote_copy(
                src_ref=res_buf.at[s % 2, pl.ds(off, rows)], dst_ref=recv_all.at[pl.ds(at, rows)],
                send_sem=send_sems.at[s], recv_sem=recv_sems.at[s], device_id=(px, py, 1 - pc), device_id_type=MESH)

        def write(s, piece):
            _, rows, at = piece
            return pltpu.make_async_copy(pair_all.at[pl.ds(at, rows)], pair_ref.at[pl.ds(at, rows)], out_sems.at[s])

        def to_chip(s, piece, core):
            return _to_chip(pair_all.at[pl.ds(piece[2], piece[1])], red_ref, piece, s, chip_send, chip_recv, own, core)

        for core in (0, 1):
            mine, theirs = table[core]

            def settle(s, mine=mine, theirs=theirs, core=core):
                if theirs[s] is not None:
                    crossing(s, theirs[s]).wait_send()
                if mine[s] is not None:
                    off, rows, at = mine[s]
                    crossing(s, mine[s]).wait_recv()
                    pair_all[at:at + rows] = (res_buf[s % 2, off:off + rows] + recv_all[at:at + rows]).astype(BF16)
                    write(s, mine[s]).start()
                    if s < EARLY_STEPS:
                        k, cp = to_chip(s, mine[s], core)
                        pl.when(own != k)(cp.start)

            for s in range(nc):
                @pl.when((pc == core) & (step == s))
                def _(s=s, settle=settle, mine=mine, theirs=theirs, core=core):
                    if theirs[s] is not None:
                        crossing(s, theirs[s]).start()
                    if s > 0:
                        settle(s - 1)
                    if s == nc - 1:
                        settle(s)
                        kept = [t for t in range(nc) if mine[t] is not None]
                        for t in kept:
                            write(t, mine[t]).wait()
                        early = [t for t in kept if t < EARLY_STEPS]
                        for t in early:
                            k, cp = to_chip(t, mine[t], core)
                            pl.when(own != k)(cp.wait_send)
                        for k in range(N_CHIP):
                            @pl.when(own == k)
                            def _(k=k):
                                for t in early:
                                    if mine[t][2] // SHARD_IN == k:
                                        for chip in range(N_CHIP):
                                            if chip != k:
                                                _from_chip(pair_all.at[pl.ds(mine[t][2], mine[t][1])], red_ref, mine[t], t,
                                                           chip_send, chip_recv, chip, (px, py, pc)).wait_recv()

    flat = jax.ShapeDtypeStruct((N_CHIP * SHARD_IN, n), BF16)
    pair, red = pl.pallas_call(
        body, name=name, grid=(nc,),
        in_specs=[ANY] * (n_seg + 1), out_specs=[ANY, ANY],
        out_shape=[flat, jax.ShapeDtypeStruct((N_CHIP, SHARD_IN, n), BF16)],
        scratch_shapes=[pltpu.VMEM((2, ktok, SEG_CHUNK), segs[0].dtype), pltpu.VMEM((ktok, n), b.dtype),
                        pltpu.VMEM((2, SEG_CHUNK, n), F32), pltpu.VMEM(flat.shape, F32), pltpu.VMEM(flat.shape, BF16),
                        pltpu.SemaphoreType.DMA((2,)), pltpu.SemaphoreType.DMA,
                        pltpu.SemaphoreType.DMA((nc,)), pltpu.SemaphoreType.DMA((nc,)), pltpu.SemaphoreType.DMA((nc,)),
                        pltpu.SemaphoreType.DMA((nc,)), pltpu.SemaphoreType.DMA((N_CHIP, nc))],
        compiler_params=_params("arbitrary"))(*segs, b)
    return pair.reshape(N_CHIP, SHARD_IN, n), red


def _branches_fwd(z_rnn, z_attn, ag_ml, b_gate, w_rnn, w_attn, w_out, x, target, g_post, name, tm=512):
    t, d = x.shape
    tm = min(tm, t)

    def body(zr_ref, za_ref, lr_ref, la_ref, br_ref, ba_ref, wr_ref, wa_ref, wo_ref, x_ref, t_ref, g_ref,
             brr_ref, bra_ref, mg_ref, do_ref, dy_ref, st_ref):
        @pl.when(pl.program_id(0) == 0)
        def _():
            st_ref[...] = jnp.zeros_like(st_ref)

        br_rnn = jnp.dot(zr_ref[...], wr_ref[...], preferred_element_type=F32)
        br_attn = jnp.dot(za_ref[...], wa_ref[...], preferred_element_type=F32)
        brr_ref[...] = br_rnn.astype(BF16)
        bra_ref[...] = br_attn.astype(BF16)
        g_rnn = _sigmoid(lr_ref[...].astype(F32) + br_ref[...])
        g_attn = _sigmoid(la_ref[...].astype(F32) + ba_ref[...])
        merged = (g_rnn * br_rnn + g_attn * br_attn).astype(BF16)
        mg_ref[...] = merged
        o = jnp.dot(merged, wo_ref[...], preferred_element_type=F32)
        g = g_ref[...]
        r = lax.rsqrt(jnp.mean(o * o, axis=-1, keepdims=True) + EPS)
        nrm = o * r
        err = x_ref[...] + nrm * g - t_ref[...]
        dy = err * (1.0 / d)
        dy_ref[...] = dy
        dn = dy * g
        do_ref[...] = (r * (dn - nrm * jnp.mean(dn * nrm, axis=-1, keepdims=True))).astype(BF16)
        st_ref[...] += _rows8([jnp.sum(dy * nrm, axis=0, keepdims=True), jnp.sum(err * err, axis=0, keepdims=True)], d)

    tile = pl.BlockSpec((tm, d), lambda i: (i, 0))
    weight = pl.BlockSpec((d, d), lambda i: (0, 0))
    bf = jax.ShapeDtypeStruct((t, d), BF16)
    return pl.pallas_call(
        body, name=name, grid=(t // tm,),
        in_specs=[tile, tile, pl.BlockSpec((tm, d), lambda i: (i, 1)), pl.BlockSpec((tm, d), lambda i: (i, 2)),
                  pl.BlockSpec((1, d), lambda i: (0, 0)), pl.BlockSpec((1, d), lambda i: (0, 1)),
                  weight, weight, weight, tile, tile, pl.BlockSpec((1, d), lambda i: (0, 0))],
        out_specs=[tile, tile, tile, tile, tile, pl.BlockSpec((8, d), lambda i: (0, 0))],
        out_shape=[bf, bf, bf, bf, jax.ShapeDtypeStruct((t, d), F32), jax.ShapeDtypeStruct((8, d), F32)],
        compiler_params=_params("arbitrary"))(z_rnn, z_attn, ag_ml, ag_ml, b_gate, b_gate, w_rnn, w_attn, w_out, x, target, g_post)


def _branches_bwd(dout, br_rnn, br_attn, ag_ml, b_gate, w_rnn, w_attn, w_out, name, tm=512):
    t, d = br_rnn.shape
    tm = min(tm, t)

    def body(do_ref, r_ref, a_ref, lr_ref, la_ref, br_ref, ba_ref, wr_ref, wa_ref, wo_ref,
             dr_ref, da_ref, dl_ref, dzr_ref, dza_ref, st_ref, wt_ref):
        @pl.when(pl.program_id(0) == 0)
        def _():
            st_ref[...] = jnp.zeros_like(st_ref)
            wt_ref[0] = wo_ref[...].T
            wt_ref[1] = wr_ref[...].T
            wt_ref[2] = wa_ref[...].T

        dm = jnp.dot(do_ref[...], wt_ref[0], preferred_element_type=F32)
        g_rnn = _sigmoid(lr_ref[...].astype(F32) + br_ref[...])
        g_attn = _sigmoid(la_ref[...].astype(F32) + ba_ref[...])
        dbr_rnn = (dm * g_rnn).astype(BF16)
        dbr_attn = (dm * g_attn).astype(BF16)
        dr_ref[...] = dbr_rnn
        da_ref[...] = dbr_attn
        dl_rnn = dm * r_ref[...].astype(F32) * g_rnn * (1.0 - g_rnn)
        dl_attn = dm * a_ref[...].astype(F32) * g_attn * (1.0 - g_attn)
        dl_ref[:, 0:d] = dl_rnn.astype(BF16)
        dl_ref[:, d:2 * d] = dl_attn.astype(BF16)
        st_ref[...] += _rows8([jnp.sum(dl_rnn, axis=0, keepdims=True), jnp.sum(dl_attn, axis=0, keepdims=True)], d)
        dzr_ref[...] = jnp.dot(dbr_rnn, wt_ref[1], preferred_element_type=F32).astype(BF16)
        dza_ref[...] = jnp.dot(dbr_attn, wt_ref[2], preferred_element_type=F32).astype(BF16)

    tile = pl.BlockSpec((tm, d), lambda i: (i, 0))
    weight = pl.BlockSpec((d, d), lambda i: (0, 0))
    bf = jax.ShapeDtypeStruct((t, d), BF16)
    return pl.pallas_call(
        body, name=name, grid=(t // tm,),
        in_specs=[tile, tile, tile, pl.BlockSpec((tm, d), lambda i: (i, 1)), pl.BlockSpec((tm, d), lambda i: (i, 2)),
                  pl.BlockSpec((1, d), lambda i: (0, 0)), pl.BlockSpec((1, d), lambda i: (0, 1)), weight, weight, weight],
        out_specs=[tile, tile, pl.BlockSpec((tm, 2 * d), lambda i: (i, 0)), tile, tile, pl.BlockSpec((8, d), lambda i: (0, 0))],
        out_shape=[bf, bf, jax.ShapeDtypeStruct((t, 2 * d), BF16), bf, bf, jax.ShapeDtypeStruct((8, d), F32)],
        scratch_shapes=[pltpu.VMEM((3, d, d), BF16)],
        compiler_params=_params("arbitrary"))(dout, br_rnn, br_attn, ag_ml, ag_ml, b_gate, b_gate, w_rnn, w_attn, w_out)


def _lru_decay(r, sp):
    log_a = (-LRU_C) * r * sp
    return log_a, jnp.exp(log_a)


def _lru_gates(c, wa, ba, wx, bx, sp):
    cb = c.astype(BF16)
    r = _sigmoid(jnp.dot(cb, wa, preferred_element_type=F32) + ba)
    ig = _sigmoid(jnp.dot(cb, wx, preferred_element_type=F32) + bx)
    log_a, a = _lru_decay(r, sp)
    mult = jnp.sqrt(-jnp.tanh(log_a) * (a * a + 1.0))
    return cb, r, ig, a, mult


SUBLANES = 8


def _scan_fwd(a, u, carry, tt):
    w = a.shape[1]
    ng = tt // SUBLANES
    a3 = a.reshape(ng, SUBLANES, w)
    u3 = u.reshape(ng, SUBLANES, w)
    sub = lax.broadcasted_iota(jnp.int32, (ng, SUBLANES, w), 1)
    d = 1
    while d < SUBLANES:
        keep = sub >= d
        u3 = u3 + a3 * jnp.where(keep, pltpu.roll(u3, d, 1), 0.0)
        a3 = a3 * jnp.where(keep, pltpu.roll(a3, d, 1), 1.0)
        d *= 2
    out = []
    for g in range(ng):
        hg = u3[g] + a3[g] * carry
        out.append(hg)
        carry = hg[SUBLANES - 1:SUBLANES, :]
    return jnp.concatenate(out, axis=0)


def _scan_rev(b, g, carry, tt):
    w = b.shape[1]
    ng = tt // SUBLANES
    b3 = b.reshape(ng, SUBLANES, w)
    g3 = g.reshape(ng, SUBLANES, w)
    sub = lax.broadcasted_iota(jnp.int32, (ng, SUBLANES, w), 1)
    d = 1
    while d < SUBLANES:
        keep = sub < SUBLANES - d
        g3 = g3 + b3 * jnp.where(keep, pltpu.roll(g3, SUBLANES - d, 1), 0.0)
        b3 = b3 * jnp.where(keep, pltpu.roll(b3, SUBLANES - d, 1), 1.0)
        d *= 2
    out = [None] * ng
    for k in range(ng - 1, -1, -1):
        hk = g3[k] + b3[k] * carry
        out[k] = hk
        carry = hk[0:1, :]
    return jnp.concatenate(out, axis=0)


def _conv_taps(cw, bias, x, ext_ref, tt):
    x2 = ext_ref[7:7 + tt, :]
    x1 = ext_ref[6:6 + tt, :]
    x0 = ext_ref[5:5 + tt, :]
    return bias + cw[3:4] * x + cw[2:3] * x2 + cw[1:2] * x1 + cw[0:1] * x0


def _rnn_fwd(rx_rg, cw, cb, wa, ba, wx, bx, lam, name, tt=512):
    t = rx_rg.shape[0]
    tt = min(tt, t)
    w = GROUP_W

    def body(rx_ref, rg_ref, cw_ref, cb_ref, wa_ref, ba_ref, wx_ref, bx_ref, lam_ref,
             y_ref, z_ref, c_ref, r_ref, ig_ref, mult_ref, ext_ref, hc_ref):
        @pl.when(pl.program_id(1) == 0)
        def _():
            ext_ref[0:8, :] = jnp.zeros((8, w), F32)
            hc_ref[...] = jnp.zeros((8, w), F32)

        x = rx_ref[...]
        ext_ref[8:8 + tt, :] = x
        c = _conv_taps(cw_ref[...], cb_ref[...], x, ext_ref, tt)
        ext_ref[0:8, :] = x[tt - 8:tt, :]
        sp = _softplus(-lam_ref[...])
        _, r, ig, a, mult = _lru_gates(c, wa_ref[...], ba_ref[...], wx_ref[...], bx_ref[...], sp)
        c_ref[...] = c
        r_ref[...] = r
        ig_ref[...] = ig
        mult_ref[...] = mult
        h = _scan_fwd(a, mult * (ig * c), hc_ref[7:8, :], tt)
        hc_ref[...] = h[tt - 8:tt, :]
        y_ref[...] = h
        rg = rg_ref[...]
        z_ref[...] = (h * rg * _sigmoid(rg)).astype(BF16)

    vec = pl.BlockSpec((1, w), lambda g, i: (0, g))
    mat = pl.BlockSpec((None, w, w), lambda g, i: (g, 0, 0))
    tile = pl.BlockSpec((tt, w), lambda g, i: (i, g))
    return pl.pallas_call(
        body, name=name, grid=(N_GROUPS, t // tt),
        in_specs=[tile, pl.BlockSpec((tt, w), lambda g, i: (i, N_GROUPS + g)),
                  pl.BlockSpec((4, w), lambda g, i: (0, g)), vec, mat, vec, mat, vec, vec],
        out_specs=[tile, tile] + [pl.BlockSpec((None, tt, w), lambda g, i: (g, i, 0))] * 4,
        out_shape=[jax.ShapeDtypeStruct((t, D_RNN), F32), jax.ShapeDtypeStruct((t, D_RNN), BF16)]
        + [jax.ShapeDtypeStruct((N_GROUPS, t, w), F32)] * 4,
        scratch_shapes=[pltpu.VMEM((tt + 8, w), F32), pltpu.VMEM((8, w), F32)],
        compiler_params=_params("parallel", "arbitrary"))(rx_rg, rx_rg, cw, cb, wa, ba, wx, bx, lam)


def _rnn_bwd(rx_rg, y, dz, c, r, ig, mult, cw, wa, wx, lam, name, tt=512):
    t = rx_rg.shape[0]
    tt = min(tt, t)
    nt = t // tt
    w = GROUP_W

    def body(rx_ref, rg_ref, y_ref, yt_ref, dz_ref, c_ref, r_ref, ig_ref, mult_ref, cw_ref, wa_ref, wx_ref, lam_ref,
             drx_ref, drg_ref, st_ref, gda_ref, gdx_ref, dcx_ref, wcar_ref, acar_ref, dwa_ref, dwx_ref):
        ii = pl.program_id(1)

        @pl.when(ii == 0)
        def _():
            wcar_ref[...] = jnp.zeros((8, w), F32)
            acar_ref[...] = jnp.zeros((8, w), F32)
            dcx_ref[tt:tt + 8, :] = jnp.zeros((8, w), F32)
            st_ref[...] = jnp.zeros_like(st_ref)
            dwa_ref[...] = jnp.zeros_like(dwa_ref)
            dwx_ref[...] = jnp.zeros_like(dwx_ref)

        has_prev = jnp.where(ii == nt - 1, 0.0, 1.0)
        cwv = cw_ref[...]
        lam = lam_ref[...]
        sp = _softplus(-lam)
        wa = wa_ref[...]
        wx = wx_ref[...]
        c = c_ref[...]
        r = r_ref[...]
        ig = ig_ref[...]
        mult = mult_ref[...]
        _, a = _lru_decay(r, sp)
        cb16 = c.astype(BF16)

        rg = rg_ref[...]
        sg = _sigmoid(rg)
        dz = dz_ref[...].astype(F32)
        yv = y_ref[...]
        drg_ref[...] = (dz * yv * (sg * (1.0 + rg * (1.0 - sg)))).astype(BF16)

        row = lax.broadcasted_iota(jnp.int32, (tt, w), 0)
        b = jnp.where(row < tt - 1, pltpu.roll(a, tt - 1, 0), acar_ref[0:1, :])
        dh = _scan_rev(b, dz * (rg * sg), wcar_ref[0:1, :], tt)
        wcar_ref[...] = dh[0:8, :]
        acar_ref[...] = a[0:8, :]

        hprev = jnp.where(row >= 1, pltpu.roll(yv, 1, 0), yt_ref[7:8, :] * has_prev)
        dmult = dh * (ig * c)
        dig = dh * mult * c
        dlog_a = dh * hprev * a - dmult * (a * a / mult)
        dpa = dlog_a * ((-LRU_C) * sp) * r * (1.0 - r)
        dpx = dig * ig * (1.0 - ig)
        dsp = jnp.sum(dlog_a * r, axis=0, keepdims=True) * (-LRU_C)
        dlam = dsp * (-_sigmoid(-lam))
        dpa16 = dpa.astype(BF16)
        dpx16 = dpx.astype(BF16)
        dwa_ref[...] += lax.dot_general(cb16, dpa16, TN_DIMS, preferred_element_type=F32)
        dwx_ref[...] += lax.dot_general(cb16, dpx16, TN_DIMS, preferred_element_type=F32)
        dc = (dh * mult * ig
              + lax.dot_general(dpa16, wa, NT_DIMS, preferred_element_type=F32)
              + lax.dot_general(dpx16, wx, NT_DIMS, preferred_element_type=F32))

        dcx_ref[0:tt, :] = dc
        dc1 = dcx_ref[1:1 + tt, :]
        dc2 = dcx_ref[2:2 + tt, :]
        dc3 = dcx_ref[3:3 + tt, :]
        drx_ref[...] = (cwv[3:4] * dc + cwv[2:3] * dc1 + cwv[1:2] * dc2 + cwv[0:1] * dc3).astype(BF16)
        dcx_ref[tt:tt + 8, :] = dc[0:8, :]

        def colsum(v):
            return jnp.sum(v, axis=0, keepdims=True)

        x = rx_ref[...]
        st_ref[...] += _rows8([colsum(dc), colsum(dpa), colsum(dpx), dlam,
                               colsum(dc3 * x), colsum(dc2 * x), colsum(dc1 * x), colsum(dc * x)], w)

        @pl.when(ii == nt - 1)
        def _():
            for blk in range(GROUP_W // RNN_BLOCK_W):
                rows = slice(blk * RNN_BLOCK_W, (blk + 1) * RNN_BLOCK_W)
                gda_ref[blk] = dwa_ref[rows, rows]
                gdx_ref[blk] = dwx_ref[rows, rows]

    def rev(ii):
        return nt - 1 - ii

    def tail(g, ii):
        return (jnp.maximum(rev(ii) * (tt // 8) - 1, 0), g)

    vec = pl.BlockSpec((1, w), lambda g, ii: (0, g))
    mat = pl.BlockSpec((None, w, w), lambda g, ii: (g, 0, 0))
    tile = pl.BlockSpec((tt, w), lambda g, ii: (rev(ii), g))
    diag_shape = (N_GROUPS, GROUP_W // RNN_BLOCK_W, RNN_BLOCK_W, RNN_BLOCK_W)
    diag = pl.BlockSpec((None,) + diag_shape[1:], lambda g, ii: (g, 0, 0, 0))
    kept = pl.BlockSpec((None, tt, w), lambda g, ii: (g, rev(ii), 0))
    return pl.pallas_call(
        body, name=name, grid=(N_GROUPS, nt),
        in_specs=[tile, pl.BlockSpec((tt, w), lambda g, ii: (rev(ii), N_GROUPS + g)),
                  tile, pl.BlockSpec((8, w), tail), tile, kept, kept, kept, kept,
                  pl.BlockSpec((4, w), lambda g, ii: (0, g)), mat, mat, vec],
        out_specs=[tile, tile, pl.BlockSpec((8, w), lambda g, ii: (0, g)), diag, diag],
        out_shape=[jax.ShapeDtypeStruct((t, D_RNN), BF16), jax.ShapeDtypeStruct((t, D_RNN), BF16),
                   jax.ShapeDtypeStruct((8, D_RNN), F32),
                   jax.ShapeDtypeStruct(diag_shape, F32), jax.ShapeDtypeStruct(diag_shape, F32)],
        scratch_shapes=[pltpu.VMEM((tt + 8, w), F32), pltpu.VMEM((8, w), F32),
                        pltpu.VMEM((8, w), F32), pltpu.VMEM((w, w), F32), pltpu.VMEM((w, w), F32)],
        compiler_params=_params("parallel", "arbitrary"))(rx_rg, rx_rg, y, y, dz, c, r, ig, mult, cw, wa, wx, lam)


def _half_mask(shape, half):
    lane = lax.broadcasted_iota(jnp.int32, shape, 1)
    return (lane >= HEAD_DIM) if half else (lane < HEAD_DIM)


def _dup_half(t, half):
    sel = jnp.where(_half_mask(t.shape, half), t, 0.0)
    return sel + pltpu.roll(sel, HEAD_DIM, 1)


def _band_geometry(n):
    qi = lax.broadcasted_iota(jnp.int32, (BLOCK, 2 * BLOCK), 0)
    kj = lax.broadcasted_iota(jnp.int32, (BLOCK, 2 * BLOCK), 1)
    dist = BLOCK + qi - kj
    first_key = jnp.where(n > 0, 0, BLOCK)
    valid = (dist >= 0) & (dist < BLOCK) & (kj >= first_key)
    return dist.astype(F32), valid


GROUP = 4


def _kv_dup(prev_ref, cur_ref, hk, scale=1.0):
    tile = hk // 2
    kt = jnp.concatenate([prev_ref[:, tile * LANE:(tile + 1) * LANE], cur_ref[:, tile * LANE:(tile + 1) * LANE]], axis=0)
    return (_dup_half(kt.astype(F32), hk % 2) * scale).astype(BF16)


def _fill_bias(bias_ref, sm_ref, n):
    distf, valid = _band_geometry(n)
    for head in range(N_Q_HEADS):
        bias_ref[head] = jnp.where(valid, -sm_ref[1, head] * distf, MASKED)


def _head_scores(q2s, half, kdup, bias):
    qm = jnp.where(_half_mask(q2s.shape, half), q2s, jnp.zeros_like(q2s))
    return qm, lax.dot_general(qm, kdup, NT_DIMS, preferred_element_type=F32) + bias


def _attn_specs(nb, clamp_last):
    def blk(n):
        return jnp.minimum(n, nb - 1) if clamp_last else n

    q_spec = pl.BlockSpec((BLOCK, D_MODEL), lambda n: (blk(n), 0))
    k_prev = pl.BlockSpec((BLOCK, D_KV), lambda n: (jnp.maximum(blk(n) - 1, 0), D_MODEL // D_KV))
    k_cur = pl.BlockSpec((BLOCK, D_KV), lambda n: (blk(n), D_MODEL // D_KV))
    v_prev = pl.BlockSpec((BLOCK, D_KV), lambda n: (jnp.maximum(blk(n) - 1, 0), D_MODEL // D_KV + 1))
    v_cur = pl.BlockSpec((BLOCK, D_KV), lambda n: (blk(n), D_MODEL // D_KV + 1))
    return q_spec, k_prev, k_cur, v_prev, v_cur


def _attn_fwd(sm, qkv, ag_ml, name, ride=None):
    t = qkv.shape[0]
    nb = t // BLOCK

    host = _Hosted(ride, 7, 3, 1)

    def body(*refs):
        (sm_ref, q_ref, kp_ref, kc_ref, vp_ref, vc_ref, ag_ref, y_ref, z_ref, lse_ref, bias_ref), start, finish = host.split(refs)
        n = pl.program_id(0)
        start(n == 0)
        start.middle(n == (3 * nb) // 4)

        @pl.when(n <= 1)
        def _():
            _fill_bias(bias_ref, sm_ref, n)

        lane = lax.broadcasted_iota(jnp.int32, (BLOCK, LANE), 1)
        low = lane < HEAD_DIM
        lse = jnp.zeros((BLOCK, LANE), F32)
        for hk in range(N_Q_HEADS // GROUP):
            kdup = _kv_dup(kp_ref, kc_ref, hk)
            vdup = _kv_dup(vp_ref, vc_ref, hk)
            for k in (0, 1):
                c = slice((2 * hk + k) * LANE, (2 * hk + k + 1) * LANE)
                q2s = q_ref[:, c] * ATTN_SCALE
                outs = []
                for half in (0, 1):
                    head = GROUP * hk + 2 * k + half
                    _, s = _head_scores(q2s, half, kdup, bias_ref[head])
                    sink = sm_ref[0, head]
                    m = jnp.maximum(jnp.max(s, axis=1, keepdims=True), sink)
                    e = jnp.exp(s - m)
                    l = jnp.sum(e, axis=1, keepdims=True) + jnp.exp(sink - m)
                    outs.append(jnp.dot((e * (1.0 / l)).astype(BF16), vdup, preferred_element_type=F32))
                    lse = jnp.where(lane == head, m + jnp.log(l), lse)
                yt = jnp.where(low, outs[0], outs[1])
                y_ref[:, c] = yt
                ag = ag_ref[:, c].astype(F32)
                z_ref[:, c] = (yt * ag * _sigmoid(ag)).astype(BF16)
        lse_ref[...] = lse
        finish(n == nb - 1)

    q_spec, k_prev, k_cur, v_prev, v_cur = _attn_specs(nb, False)
    wide = pl.BlockSpec((BLOCK, D_MODEL), lambda n: (n, 0))
    outs = pl.pallas_call(
        body, name=name, grid=(nb,),
        in_specs=[pl.BlockSpec(memory_space=pltpu.SMEM), q_spec, k_prev, k_cur, v_prev, v_cur, wide] + host.in_specs,
        out_specs=[wide, wide, pl.BlockSpec((BLOCK, LANE), lambda n: (n, 0))] + host.out_specs,
        out_shape=[jax.ShapeDtypeStruct((t, D_MODEL), F32), jax.ShapeDtypeStruct((t, D_MODEL), BF16),
                   jax.ShapeDtypeStruct((t, LANE), F32)] + host.out_shapes,
        scratch_shapes=[pltpu.VMEM((N_Q_HEADS, BLOCK, 2 * BLOCK), F32)] + host.scratch_shapes,
        compiler_params=_params("arbitrary"))(sm, qkv, qkv, qkv, qkv, qkv, ag_ml, *host.arrays)
    res, landed = host.results(outs, 3)
    return (*res, landed) if ride else tuple(res)


def _attn_bwd(sm, qkv, ag_ml, y, lse, dz, name, ride=None):
    t = qkv.shape[0]
    nb = t // BLOCK
    host = _Hosted(ride, 10, 4, 3)

    def body(*refs):
        host_refs, start, finish = host.split(refs)
        (sm_ref, q_ref, kp_ref, kc_ref, vp_ref, vc_ref, ag_ref, y_ref, lse_ref, dz_ref,
         dq_ref, dkv_ref, dag_ref, ds_ref, ck_ref, cv_ref, bias_ref) = host_refs
        n = pl.program_id(0)
        start(n == 0)
        start.middle(n == (3 * nb) // 4)

        @pl.when(n == 0)
        def _():
            ck_ref[...] = jnp.zeros_like(ck_ref)
            cv_ref[...] = jnp.zeros_like(cv_ref)
            ds_ref[...] = jnp.zeros_like(ds_ref)

        @pl.when(n <= 1)
        def _():
            _fill_bias(bias_ref, sm_ref, n)

        @pl.when(n < nb)
        def _():
            lane8 = lax.broadcasted_iota(jnp.int32, (8, LANE), 1)
            row8 = lax.broadcasted_iota(jnp.int32, (8, LANE), 0)
            dsink = jnp.zeros((8, LANE), F32)
            dk_heads, dv_heads = [], []
            lse_tile = lse_ref[...]
            for hk in range(N_Q_HEADS // GROUP):
                kdup = _kv_dup(kp_ref, kc_ref, hk)
                ks = _kv_dup(kp_ref, kc_ref, hk, ATTN_SCALE)
                vdup = _kv_dup(vp_ref, vc_ref, hk)
                qms, dyhs, y_rows = [], [], []
                for k in (0, 1):
                    c = slice((2 * hk + k) * LANE, (2 * hk + k + 1) * LANE)
                    ag = ag_ref[:, c].astype(F32)
                    sg = _sigmoid(ag)
                    dzt = dz_ref[:, c].astype(F32)
                    yt = y_ref[:, c]
                    dag_ref[:, c] = (dzt * yt * (sg * (1.0 + ag * (1.0 - sg)))).astype(BF16)
                    dyt = dzt * (ag * sg)
                    q2s = q_ref[:, c] * ATTN_SCALE
                    for half in (0, 1):
                        hm = _half_mask(q2s.shape, half)
                        qms.append(jnp.where(hm, q2s, jnp.zeros_like(q2s)))
                        dyhs.append(jnp.where(hm, dyt, 0.0))
                        y_rows.append(yt)
                qm4 = jnp.concatenate(qms, axis=0)
                dy4 = jnp.concatenate(dyhs, axis=0)
                dy4_16 = dy4.astype(BF16)
                s4 = lax.dot_general(qm4, kdup, NT_DIMS, preferred_element_type=F32)
                dp4 = lax.dot_general(dy4_16, vdup, NT_DIMS, preferred_element_type=F32)
                probs16, ds16 = [], []
                for r in range(GROUP):
                    head = GROUP * hk + r
                    rows = slice(r * BLOCK, (r + 1) * BLOCK)
                    lh = lse_tile[:, head:head + 1]
                    probs = jnp.exp(s4[rows] + bias_ref[head] - lh)
                    psink = jnp.exp(sm_ref[0, head] - lh)
                    delta = jnp.sum(dyhs[r] * y_rows[r], axis=1, keepdims=True)
                    ds16.append((probs * (dp4[rows] - delta)).astype(BF16))
                    probs16.append(probs.astype(BF16))
                    dsink = dsink + jnp.where((row8 == 0) & (lane8 == head),
                                              -jnp.sum(psink * delta, axis=0, keepdims=True), 0.0)
                ds4 = jnp.concatenate(ds16, axis=0)
                p4 = jnp.concatenate(probs16, axis=0)
                dq4 = jnp.dot(ds4, ks, preferred_element_type=F32)
                low = _half_mask((BLOCK, LANE), 0)
                for k in (0, 1):
                    c = slice((2 * hk + k) * LANE, (2 * hk + k + 1) * LANE)
                    dq_ref[:, c] = jnp.where(low, dq4[2 * k * BLOCK:(2 * k + 1) * BLOCK],
                                             dq4[(2 * k + 1) * BLOCK:(2 * k + 2) * BLOCK]).astype(BF16)
                dk_acc = lax.dot_general(ds4, qm4, TN_DIMS, preferred_element_type=F32)
                dv_acc = lax.dot_general(p4, dy4_16, TN_DIMS, preferred_element_type=F32)
                dk_heads.append(dk_acc + pltpu.roll(dk_acc, HEAD_DIM, 1))
                dv_heads.append(dv_acc + pltpu.roll(dv_acc, HEAD_DIM, 1))
            ds_ref[...] += dsink
            low = _half_mask((2 * BLOCK, LANE), 0)
            for tile in range(2):
                cols = slice(tile * LANE, (tile + 1) * LANE)
                dkt = jnp.where(low, dk_heads[2 * tile], dk_heads[2 * tile + 1])
                dvt = jnp.where(low, dv_heads[2 * tile], dv_heads[2 * tile + 1])
                dkv_ref[:, cols] = (ck_ref[:, cols] + dkt[0:BLOCK, :]).astype(BF16)
                dkv_ref[:, D_KV + tile * LANE:D_KV + (tile + 1) * LANE] = (cv_ref[:, cols] + dvt[0:BLOCK, :]).astype(BF16)
                ck_ref[:, cols] = dkt[BLOCK:2 * BLOCK, :]
                cv_ref[:, cols] = dvt[BLOCK:2 * BLOCK, :]

        @pl.when(n == nb)
        def _():
            dkv_ref[:, 0:D_KV] = ck_ref[...].astype(BF16)
            dkv_ref[:, D_KV:2 * D_KV] = cv_ref[...].astype(BF16)

        finish(n == nb)

    q_spec, k_prev, k_cur, v_prev, v_cur = _attn_specs(nb, True)
    wide = pl.BlockSpec((BLOCK, D_MODEL), lambda n: (jnp.minimum(n, nb - 1), 0))
    outs = pl.pallas_call(
        body, name=name, grid=(nb + 1,),
        in_specs=[pl.BlockSpec(memory_space=pltpu.SMEM), q_spec, k_prev, k_cur, v_prev, v_cur, wide, wide,
                  pl.BlockSpec((BLOCK, LANE), lambda n: (jnp.minimum(n, nb - 1), 0)), wide] + host.in_specs,
        out_specs=[wide, pl.BlockSpec((BLOCK, 2 * D_KV), lambda n: (jnp.maximum(n - 1, 0), 0)), wide,
                   pl.BlockSpec((8, LANE), lambda n: (0, 0))] + host.out_specs,
        out_shape=[jax.ShapeDtypeStruct((t, D_MODEL), BF16), jax.ShapeDtypeStruct((t, 2 * D_KV), BF16),
                   jax.ShapeDtypeStruct((t, D_MODEL), BF16), jax.ShapeDtypeStruct((8, LANE), F32)] + host.out_shapes,
        scratch_shapes=[pltpu.VMEM((BLOCK, D_KV), F32), pltpu.VMEM((BLOCK, D_KV), F32),
                        pltpu.VMEM((N_Q_HEADS, BLOCK, 2 * BLOCK), F32)] + host.scratch_shapes,
        compiler_params=_params("arbitrary"))(sm, qkv, qkv, qkv, qkv, qkv, ag_ml, y, lse, dz, *host.arrays)
    res, landed = host.results(outs, 4)
    return (*res, landed) if ride else tuple(res)


def _local_grads(x, target, p, project, late_weights=None, reduce_out=None, reduce_in=None):
    h, rx_rg, qkv, ag_ml, wt = project(x, p["pre_g"])
    if late_weights is None:
        y_attn, z_attn, lse = _attn_fwd(p["sm"], qkv, ag_ml, "attn_fwd")
    else:
        y_attn, z_attn, lse, landed = _attn_fwd(p["sm"], qkv, ag_ml, "attn_fwd", ride=late_weights[0])
        p = {**p, **late_weights[1](landed)}
    lru = (p["cw"], p["cb"], p["wbd_a"], p["b_a"], p["wbd_x"], p["b_x"], p["lam"])
    y_rnn, z_rnn, *kept_rnn = _rnn_fwd(rx_rg, *lru, "rnn_fwd")
    br_rnn, br_attn, merged, dout, dy, st_post = _branches_fwd(
        z_rnn, z_attn, ag_ml, p["b_gate"], p["w_rnn"], p["w_attn"], p["w_out"], x, target, p["post_g"], "branches_fwd")

    dbr_rnn, dbr_attn, d_ml, dz_rnn, dz_attn, st_merge = _branches_bwd(
        dout, br_rnn, br_attn, ag_ml, p["b_gate"], p["w_rnn"], p["w_attn"], p["w_out"], "branches_bwd")
    out_prods = [(z_rnn, dbr_rnn), (z_attn, dbr_attn), (merged, dout)]
    gw_rnn = gw_attn = gw_out = red_out = red_in = None
    if reduce_out is None:
        gw_rnn, gw_attn, gw_out = (_mm_tn(a, b, nm) for (a, b), nm in zip(out_prods, ("gw_rnn", "gw_attn", "gw_out")))
    else:
        out_pairs = _mm_tn_pairs(out_prods, "gw_outs")
    d_rx, d_rg, st_rnn, g_rg_a, g_rg_x = _rnn_bwd(
        rx_rg, y_rnn, dz_rnn, *kept_rnn, p["cw"], p["wbd_a"], p["wbd_x"], p["lam"], "rnn_bwd")
    if reduce_out is None:
        dq, dkv, d_ag, st_sink = _attn_bwd(p["sm"], qkv, ag_ml, y_attn, lse, dz_attn, "attn_bwd")
    else:
        dq, dkv, d_ag, st_sink, red_out = _attn_bwd(p["sm"], qkv, ag_ml, y_attn, lse, dz_attn, "attn_bwd",
                                                    ride=reduce_out(out_pairs, g_rg_a, g_rg_x))

    segs = [d_rx, d_rg, dq, dkv, d_ag, d_ml]
    if reduce_in is None:
        gwt = _mm_tn_seg(segs, h, "gw_in")
        grad_x, st_pre = _input_grad(segs, wt, x, p["pre_g"], dy, "input_grad")
    else:
        gwt = None
        grad_x, st_pre, red_in = _input_grad(segs, wt, x, p["pre_g"], dy, "input_grad",
                                             ride=reduce_in(*_mm_tn_seg_pair(segs, h, "gw_in")))
    return dict(grad_x=grad_x, gwt=gwt, gw_rnn=gw_rnn, gw_attn=gw_attn, gw_out=gw_out,
                st_post=st_post, st_merge=st_merge, st_rnn=st_rnn, st_sink=st_sink, st_pre=st_pre,
                g_rg_a=g_rg_a, g_rg_x=g_rg_x, red_out=red_out, red_in=red_in)


def _place():
    x, y, c = lax.axis_index("x"), lax.axis_index("y"), lax.axis_index("c")
    return x, y, c


def _gather_ride(shards):
    n = len(shards)

    def copies(ins, outs, sems):
        send_sems, recv_sems, local_sems = sems
        x, y, c = _place()
        me, sibling = (x, y, c), (x, y, 1 - c)
        chips = [(1 - x, y), (x, 1 - y), (1 - x, 1 - y)]

        def slot(a, dev):
            return outs[a].at[4 * dev[0] + 2 * dev[1] + dev[2]]

        def copy(a, k, block, to, src=None):
            return pltpu.make_async_remote_copy(
                src_ref=slot(a, block) if src is None else src, dst_ref=slot(a, block),
                send_sem=send_sems.at[a, k], recv_sem=recv_sems.at[a, k], device_id=to, device_id_type=MESH)

        mine = [pltpu.make_async_copy(ins[a], slot(a, me), local_sems.at[a]) for a in range(n)]
        first = []
        for a in range(n):
            first.append(copy(a, 0, me, sibling, src=ins[a]))
            first += [copy(a, 1 + j, me, (*chip, c), src=ins[a]) for j, chip in enumerate(chips)]
        return me, sibling, chips, c, copy, mine, first

    def start(ins, outs, sems):
        *_, mine, first = copies(ins, outs, sems)
        for cp in mine + first:
            cp.start()

    def middle(ins, outs, sems):
        me, sibling, chips, c, copy, _, _ = copies(ins, outs, sems)
        for j, chip in enumerate(chips):
            for a in range(n):
                copy(a, 1 + j, (*chip, c), me).wait_recv()
                copy(a, 4 + j, (*chip, c), sibling).start()

    def finish(ins, outs, sems):
        me, sibling, chips, c, copy, mine, first = copies(ins, outs, sems)
        passed = [copy(a, 4 + j, (*chip, c), sibling) for j, chip in enumerate(chips) for a in range(n)]
        for a in range(n):
            copy(a, 0, sibling, me).wait_recv()
            for j, chip in enumerate(chips):
                copy(a, 4 + j, (*chip, 1 - c), me).wait_recv()
        for cp in first + passed:
            cp.wait_send()
        for cp in mine:
            cp.wait()

    return _Ride(
        shards, [jax.ShapeDtypeStruct((N_DEV, *s.shape), s.dtype) for s in shards],
        [pltpu.SemaphoreType.DMA((n, 7)), pltpu.SemaphoreType.DMA((n, 7)), pltpu.SemaphoreType.DMA((n,))],
        start, finish, middle)


def _chips_ride(scatter, whole):
    ns, nw = len(scatter), len(whole)
    n = ns + nw

    def copies(ins, outs, sems):
        send_sems, recv_sems, local_sems = sems
        x, y, c = _place()
        own = 2 * x + y
        chips = [(1 - x, y), (x, 1 - y), (1 - x, 1 - y)]

        def src(a, chip_idx):
            return ins[a].at[chip_idx] if a < ns else ins[a]

        local = [pltpu.make_async_copy(src(a, own), outs[a].at[own], local_sems.at[a]) for a in range(n)]
        sent = [pltpu.make_async_remote_copy(
            src_ref=src(a, 2 * chip[0] + chip[1]), dst_ref=outs[a].at[own],
            send_sem=send_sems.at[a, j], recv_sem=recv_sems.at[a, own], device_id=(*chip, c), device_id_type=MESH)
            for a in range(n) for j, chip in enumerate(chips)]
        return chips, c, local, sent

    def start(ins, outs, sems):
        _, _, local, sent = copies(ins, outs, sems)
        for cp in local + sent:
            cp.start()

    def finish(ins, outs, sems):
        send_sems, recv_sems, _ = sems
        chips, c, local, sent = copies(ins, outs, sems)
        for a in range(n):
            for chip in chips:
                k = 2 * chip[0] + chip[1]
                pltpu.make_async_remote_copy(
                    src_ref=outs[a].at[k], dst_ref=outs[a].at[k], send_sem=send_sems.at[a, 0],
                    recv_sem=recv_sems.at[a, k], device_id=(*chip, c), device_id_type=MESH).wait_recv()
        for cp in sent:
            cp.wait_send()
        for cp in local:
            cp.wait()

    return _Ride(
        list(scatter) + list(whole),
        [jax.ShapeDtypeStruct(s.shape, s.dtype) for s in scatter]
        + [jax.ShapeDtypeStruct((N_CHIP, *s.shape), s.dtype) for s in whole],
        [pltpu.SemaphoreType.DMA((n, 3)), pltpu.SemaphoreType.DMA((n, N_CHIP)), pltpu.SemaphoreType.DMA((n,))],
        start, finish)


def _chips_rest_ride(pair, red):
    table = _side_pieces()
    nc = len(CHUNK_ORDER)

    def each(ins, outs, sems, sending, landing):
        send_sems, recv_sems, _ = sems
        pair_ref, red_ref = ins[0], outs[0]
        x, y, c = _place()
        own = 2 * x + y
        for core in (0, 1):
            mine = table[core][0]
            rest = [s for s in range(EARLY_STEPS, nc) if mine[s] is not None]

            @pl.when(c == core)
            def _(mine=mine, rest=rest, core=core):
                for s in rest:
                    _, rows, at = mine[s]
                    k, cp = _to_chip(pair_ref.at[at // SHARD_IN, pl.ds(at % SHARD_IN, rows)], red_ref, mine[s], s,
                                     send_sems, recv_sems, own, core)
                    pl.when(own != k)(lambda cp=cp: sending(cp))
                if landing is not None:
                    for k in range(N_CHIP):
                        @pl.when(own == k)
                        def _(k=k):
                            for s in rest:
                                _, rows, at = mine[s]
                                if at // SHARD_IN == k:
                                    for chip in range(N_CHIP):
                                        if chip != k:
                                            landing(_from_chip(pair_ref.at[k, pl.ds(at % SHARD_IN, rows)], red_ref, mine[s], s,
                                                               send_sems, recv_sems, chip, (x, y, c)))

    def local(ins, outs, sems):
        x, y, _ = _place()
        return pltpu.make_async_copy(ins[0].at[2 * x + y], outs[0].at[2 * x + y], sems[2])

    def start(ins, outs, sems):
        local(ins, outs, sems).start()
        each(ins, outs, sems, lambda cp: cp.start(), None)

    def finish(ins, outs, sems):
        each(ins, outs, sems, lambda cp: cp.wait_send(), lambda cp: cp.wait_recv())
        local(ins, outs, sems).wait()

    return _Ride([pair, red], [jax.ShapeDtypeStruct(red.shape, red.dtype)],
                 [pltpu.SemaphoreType.DMA((nc,)), pltpu.SemaphoreType.DMA((N_CHIP, nc)), pltpu.SemaphoreType.DMA],
                 start, finish, aliases={1: 0})


def _allreduce_small(pack, name):
    shape = pack.shape

    def body(x_ref, o_ref, sib_ref, chip_ref, send_sems, recv_sems):
        x, y, c = _place()
        own = 2 * x + y
        chips = [(1 - x, y), (x, 1 - y), (1 - x, 1 - y)]
        to_sibling = pltpu.make_async_remote_copy(
            src_ref=x_ref, dst_ref=sib_ref, send_sem=send_sems.at[0], recv_sem=recv_sems.at[0],
            device_id=(x, y, 1 - c), device_id_type=MESH)
        to_sibling.start()
        to_sibling.wait()
        chip_ref[own] = x_ref[...] + sib_ref[...]
        sent = [pltpu.make_async_remote_copy(
            src_ref=chip_ref.at[own], dst_ref=chip_ref.at[own], send_sem=send_sems.at[1 + j],
            recv_sem=recv_sems.at[1 + own], device_id=(*chip, c), device_id_type=MESH) for j, chip in enumerate(chips)]
        for cp in sent:
            cp.start()
        for chip in chips:
            k = 2 * chip[0] + chip[1]
            pltpu.make_async_remote_copy(
                src_ref=chip_ref.at[k], dst_ref=chip_ref.at[k], send_sem=send_sems.at[1],
                recv_sem=recv_sems.at[1 + k], device_id=(*chip, c), device_id_type=MESH).wait_recv()
        for cp in sent:
            cp.wait_send()
        o_ref[...] = (chip_ref[0] + chip_ref[1]) + (chip_ref[2] + chip_ref[3])

    return pl.pallas_call(
        body, name=name, out_shape=jax.ShapeDtypeStruct(shape, F32),
        in_specs=[pl.BlockSpec(memory_space=pltpu.VMEM)], out_specs=pl.BlockSpec(memory_space=pltpu.VMEM),
        scratch_shapes=[pltpu.VMEM(shape, F32), pltpu.VMEM((N_CHIP, *shape), F32),
                        pltpu.SemaphoreType.DMA((4,)), pltpu.SemaphoreType.DMA((1 + N_CHIP,))],
    )(pack)


def _adamw(g, w, m, v):
    m = ADAM_B1 * m + (1.0 - ADAM_B1) * g
    v = ADAM_B2 * v + (1.0 - ADAM_B2) * (g * g)
    m_hat = m / (1.0 - ADAM_B1 ** ADAM_STEP)
    v_hat = v / (1.0 - ADAM_B2 ** ADAM_STEP)
    delta = -ADAM_LR * (m_hat / (jnp.sqrt(v_hat) + ADAM_EPS) + ADAM_WD * w)
    return delta, m, v


def _adam_parts(items, name, tr=None):
    ni = len(items)
    npart, r, c = items[0][0].shape
    tr = r if tr is None else min(tr, r)

    def body(*refs):
        for a in range(ni):
            p_ref, w_ref, m_ref, v_ref = refs[4 * a:4 * a + 4]
            g_ref, d_ref, nm_ref, nv_ref = refs[4 * (ni + a):4 * (ni + a) + 4]
            g = p_ref[0].astype(F32)
            for k in range(1, npart):
                g = g + p_ref[k].astype(F32)
            g_ref[...] = g
            d_ref[...], nm_ref[...], nv_ref[...] = _adamw(g, w_ref[...], m_ref[...], v_ref[...])

    tile = pl.BlockSpec((tr, c), lambda i: (i, 0))
    outs = pl.pallas_call(
        body, name=name, grid=(r // tr,),
        in_specs=[pl.BlockSpec((npart, tr, c), lambda i: (0, i, 0)), tile, tile, tile] * ni,
        out_specs=[tile] * (4 * ni), out_shape=[jax.ShapeDtypeStruct((r, c), F32)] * (4 * ni),
        compiler_params=_params("parallel"))(*[arr for item in items for arr in item])
    return [outs[4 * a:4 * a + 4] for a in range(ni)]


def _block_diag(w):
    w4 = w.reshape(N_GROUPS, 4, RNN_BLOCK_W, RNN_BLOCK_W)
    eye = jnp.eye(4, dtype=w.dtype)
    return jnp.einsum("gbij,bc->gbicj", w4, eye).reshape(N_GROUPS, GROUP_W, GROUP_W).astype(BF16)


SMALL_ROWS = 16
ROW_PRE_G, ROW_BGATE, ROW_CONV_B, ROW_B_A, ROW_B_X, ROW_LAM, ROW_POST_G, ROW_LOSS, ROW_SINKS, ROW_CONV_W = 0, 1, 3, 4, 5, 6, 7, 8, 9, 10


def _pack_stats(st_pre, st_merge, st_rnn, st_post, st_sink, name):
    d = D_MODEL

    def body(pre_ref, mg_ref, rnn_ref, post_ref, sink_ref, o_ref):
        rnn = rnn_ref[...]
        sinks = jnp.concatenate([sink_ref[0:1, :], jnp.zeros((1, d - LANE), F32)], axis=1)
        o_ref[0:8, :] = _rows8([pre_ref[0:1, :], mg_ref[0:1, :], mg_ref[1:2, :], rnn[0:1], rnn[1:2], rnn[2:3], rnn[3:4],
                                post_ref[0:1, :]], d)
        o_ref[8:16, :] = _rows8([post_ref[1:2, :], sinks, rnn[4:5], rnn[5:6], rnn[6:7], rnn[7:8]], d)

    return pl.pallas_call(body, name=name, out_shape=jax.ShapeDtypeStruct((SMALL_ROWS, d), F32))(
        st_pre, st_merge, st_rnn, st_post, st_sink)


def _adam_small(total, w, m, v, name):
    d = D_MODEL
    n_in = len(w)
    rows = (ROW_PRE_G, ROW_BGATE, ROW_CONV_B, ROW_B_A, ROW_B_X, ROW_LAM, ROW_POST_G, ROW_SINKS)

    def grad_of(p_ref, row, shape):
        if shape == (1, 2 * d):
            return jnp.concatenate([p_ref[row:row + 1, :], p_ref[row + 1:row + 2, :]], axis=1)
        if shape == (1, RNN_BLOCKS, RNN_BLOCK_W):
            r = p_ref[row:row + 1, :]
            return jnp.concatenate([r[:, b * RNN_BLOCK_W:(b + 1) * RNN_BLOCK_W] for b in range(RNN_BLOCKS)], axis=0)[None]
        return p_ref[row:row + 1, 0:shape[1]]

    def body(*refs):
        p_ref = refs[0]
        w_refs, m_refs, v_refs = (refs[1 + k * n_in:1 + (k + 1) * n_in] for k in range(3))
        outs = refs[1 + 3 * n_in:]
        for k in range(n_in):
            g = grad_of(p_ref, rows[k], w[k].shape)
            res = (g,) + _adamw(g, w_refs[k][...], m_refs[k][...], v_refs[k][...])
            for kind in range(4):
                outs[kind * n_in + k][...] = res[kind]

    outs = pl.pallas_call(
        body, name=name, out_shape=[jax.ShapeDtypeStruct(a.shape, F32) for _ in range(4) for a in w])(total, *w, *m, *v)
    return [outs[kind * n_in:(kind + 1) * n_in] for kind in range(4)]


def kernel(x, pre_norm_g, w_in, b_gate, conv_w, conv_b, w_rg_a, b_rg_a, w_rg_x, b_rg_x, lru_lambda, attn_sinks, w_rnn_out, w_attn_out, w_out, post_norm_g, loss_target, m_pre_norm_g, m_w_in, m_b_gate, m_conv_w, m_conv_b, m_w_rg_a, m_b_rg_a, m_w_rg_x, m_b_rg_x, m_lru_lambda, m_attn_sinks, m_w_rnn_out, m_w_attn_out, m_w_out, m_post_norm_g, v_pre_norm_g, v_w_in, v_b_gate, v_conv_w, v_conv_b, v_w_rg_a, v_b_rg_a, v_w_rg_x, v_b_rg_x, v_lru_lambda, v_attn_sinks, v_w_rnn_out, v_w_attn_out, v_w_out, v_post_norm_g):
    cx, cy, cc = _place()
    dev = 4 * cx + 2 * cy + cc

    w_in_t, m_in_t, v_in_t = (jnp.transpose(a[0]) for a in (w_in, m_w_in, v_w_in))
    wt_shard = w_in_t.astype(BF16)

    def project(xs, pre_g):
        h, rx_rg, qkv, ag_ml, wt_all = _gather_project(xs, pre_g, wt_shard, "gather_project")
        return h, rx_rg, qkv, ag_ml, wt_all.reshape(D_IN, D_MODEL)

    def late_unpack(landed):
        w_rnn_all, w_attn_all, w_out_all, cw_all = landed
        return dict(w_rnn=w_rnn_all.reshape(D_RNN, D_MODEL), w_attn=w_attn_all.reshape(D_MODEL, D_MODEL),
                    w_out=w_out_all.reshape(D_MODEL, D_MODEL), cw=jnp.transpose(cw_all, (1, 0, 2)).reshape(4, D_RNN))

    late_weights = (_gather_ride([w_rnn_out[0].astype(BF16), w_attn_out[0].astype(BF16), w_out[0].astype(BF16), conv_w[0]]),
                    late_unpack)

    heads = jnp.arange(1, N_Q_HEADS + 1, dtype=F32)
    slopes = jnp.exp2(-ALIBI_MAX_BIAS * heads / N_Q_HEADS)
    b_a = b_rg_a.reshape(1, D_RNN)
    b_x = b_rg_x.reshape(1, D_RNN)
    p = dict(
        pre_g=pre_norm_g, post_g=post_norm_g, b_gate=b_gate, cb=conv_b,
        wbd_a=_block_diag(w_rg_a[0]), b_a=b_a, wbd_x=_block_diag(w_rg_x[0]), b_x=b_x, lam=lru_lambda,
        sm=jnp.pad(attn_sinks, ((0, 1), (0, 0))) + jnp.pad(slopes[None, :], ((1, 0), (0, 0))))

    flat = (RNN_BLOCKS * RNN_BLOCK_W, RNN_BLOCK_W)

    def reduce_out(out_pairs, g_rg_a, g_rg_x):
        return _join_rides(_chips_ride(out_pairs, []), _gather_ride([g_rg_a.reshape(flat), g_rg_x.reshape(flat)]))

    reduce_in = _chips_rest_ride

    g = _local_grads(x[0], loss_target[0], p, project, late_weights, reduce_out, reduce_in)
    small = _allreduce_small(
        _pack_stats(g["st_pre"], g["st_merge"], g["st_rnn"], g["st_post"], g["st_sink"], "pack_stats"), "allreduce_small")

    out = {}
    red = g["red_out"]
    w_in_out, = _adam_parts([(g["red_in"][0], w_in_t, m_in_t, v_in_t)], "adam_w_in", tr=SHARD_IN // 2)
    out["w_in"] = [jnp.transpose(o) for o in w_in_out]
    out["w_rnn_out"], out["w_attn_out"], out["w_out"] = _adam_parts(
        [(red[0], w_rnn_out[0], m_w_rnn_out[0], v_w_rnn_out[0]), (red[1], w_attn_out[0], m_w_attn_out[0], v_w_attn_out[0]),
         (red[2], w_out[0], m_w_out[0], v_w_out[0])], "adam_w_outs")
    out["w_rg_a"], out["w_rg_x"] = _adam_parts(
        [(red[3], w_rg_a.reshape(flat), m_w_rg_a.reshape(flat), v_w_rg_a.reshape(flat)),
         (red[4], w_rg_x.reshape(flat), m_w_rg_x.reshape(flat), v_w_rg_x.reshape(flat))], "adam_w_rg", tr=256)

    small_names = ("pre_norm_g", "b_gate", "conv_b", "b_rg_a", "b_rg_x", "lru_lambda", "post_norm_g", "attn_sinks")
    small_out = _adam_small(
        small,
        (pre_norm_g, b_gate, conv_b, b_rg_a, b_rg_x, lru_lambda, post_norm_g, attn_sinks),
        (m_pre_norm_g, m_b_gate, m_conv_b, m_b_rg_a, m_b_rg_x, m_lru_lambda, m_post_norm_g, m_attn_sinks),
        (v_pre_norm_g, v_b_gate, v_conv_b, v_b_rg_a, v_b_rg_x, v_lru_lambda, v_post_norm_g, v_attn_sinks),
        "adam_small")
    g_cw = lax.dynamic_slice(small[ROW_CONV_W:ROW_CONV_W + 4], (0, dev * SHARD_OUT), (4, SHARD_OUT))
    out["conv_w"], = _adam_parts([(g_cw[None], conv_w[0], m_conv_w[0], v_conv_w[0])], "adam_conv_w")

    shapes = dict(w_in=(1, D_MODEL, SHARD_IN), w_rnn_out=(1, SHARD_OUT, D_MODEL), w_attn_out=(1, SHARD_OUT, D_MODEL),
                  w_out=(1, SHARD_OUT, D_MODEL), w_rg_a=(1, RNN_BLOCKS, RNN_BLOCK_W, RNN_BLOCK_W),
                  w_rg_x=(1, RNN_BLOCKS, RNN_BLOCK_W, RNN_BLOCK_W), conv_w=(1, 4, SHARD_OUT))
    weights = ["pre_norm_g", "w_in", "b_gate", "conv_w", "conv_b", "w_rg_a", "b_rg_a", "w_rg_x", "b_rg_x",
               "lru_lambda", "attn_sinks", "w_rnn_out", "w_attn_out", "w_out", "post_norm_g"]
    results = []
    for kind in range(4):
        for name in weights:
            if name in out:
                results.append(out[name][kind].reshape(shapes[name]))
            else:
                results.append(small_out[kind][small_names.index(name)])
    loss = 0.5 / D_MODEL * jnp.sum(small[ROW_LOSS])
    return (loss, g["grad_x"][None], *results)
```

```python
import jax
import jax.numpy as jnp
from jax import lax
from jax.experimental import pallas as pl
from jax.experimental.pallas import tpu as pltpu

F32 = jnp.float32
BF16 = jnp.bfloat16

D_MODEL = 1024
D_RNN = 1024
RNN_BLOCKS = 16
RNN_BLOCK_W = 64
LRU_C = 8.0
N_Q_HEADS = 16
HEAD_DIM = 64
D_KV = 256
BLOCK = 128
ALIBI_MAX_BIAS = 8.0
EPS = 1e-6
D_IN = 6656
N_DEV = 8
N_CHIP = 4
SHARD_IN = D_IN // N_DEV
SHARD_OUT = D_MODEL // N_DEV
ATTN_SCALE = HEAD_DIM ** -0.5
MASKED = -1e30

ADAM_LR = 0.001
ADAM_B1 = 0.9
ADAM_B2 = 0.999
ADAM_EPS = 1e-08
ADAM_WD = 0.01
ADAM_STEP = 10

VMEM_LIMIT_BYTES = 52 * 1024 * 1024
LANE = 128
GROUP_W = 256
N_GROUPS = D_RNN // GROUP_W
SEG_CHUNK = 512

NT_DIMS = (((1,), (1,)), ((), ()))
TN_DIMS = (((0,), (0,)), ((), ()))
MESH = pl.DeviceIdType.MESH
ANY = pl.BlockSpec(memory_space=pl.ANY)


def _params(*semantics):
    return pltpu.CompilerParams(dimension_semantics=semantics, vmem_limit_bytes=VMEM_LIMIT_BYTES)


def _sigmoid(x):
    return 0.5 * jnp.tanh(0.5 * x) + 0.5


def _log1p(e):
    u = 1.0 + e
    den = jnp.where(u == 1.0, 1.0, u - 1.0)
    return jnp.where(u == 1.0, e, jnp.log(u) * (e / den))


def _softplus(z):
    return jnp.maximum(z, 0.0) + _log1p(jnp.exp(-jnp.abs(z)))


def _rows8(rows, width):
    idx = lax.broadcasted_iota(jnp.int32, (8, width), 0)
    out = jnp.zeros((8, width), F32)
    for r, v in enumerate(rows):
        out = jnp.where(idx == r, v, out)
    return out


class _Ride:
    def __init__(self, arrays, out_shapes, scratch_shapes, start, finish, middle=None, aliases=None):
        self.arrays, self.out_shapes, self.scratch_shapes = list(arrays), list(out_shapes), list(scratch_shapes)
        self.start, self.finish, self.middle = start, finish, middle
        self.aliases = dict(aliases or {})


class _Hosted:
    def __init__(self, ride, n_in, n_out, n_scratch=0, aliasing=False):
        self.ride = ride
        assert aliasing or not (ride and ride.aliases), "this host does not alias"
        self.aliases = {n_in + i: n_out + o for i, o in ride.aliases.items()} if ride else {}
        self.sizes = (n_in, len(ride.arrays) if ride else 0, n_out, len(ride.out_shapes) if ride else 0, n_scratch)
        self.arrays = ride.arrays if ride else []
        self.in_specs = [ANY] * len(self.arrays)
        self.out_shapes = ride.out_shapes if ride else []
        self.out_specs = [ANY] * len(self.out_shapes)
        self.scratch_shapes = ride.scratch_shapes if ride else []

    def split(self, refs):
        n_in, r_in, n_out, r_out, n_scr = self.sizes
        cuts = [0, n_in, n_in + r_in, n_in + r_in + n_out, n_in + r_in + n_out + r_out, n_in + r_in + n_out + r_out + n_scr]
        host_in, ride_in, host_out, ride_out, host_scr = (refs[cuts[k]:cuts[k + 1]] for k in range(5))
        ride_scr = refs[cuts[5]:]

        def start(when):
            if self.ride is not None:
                pl.when(when)(lambda: self.ride.start(ride_in, ride_out, ride_scr))

        def finish(when):
            if self.ride is not None:
                pl.when(when)(lambda: self.ride.finish(ride_in, ride_out, ride_scr))

        def middle(when):
            if self.ride is not None and self.ride.middle is not None:
                pl.when(when)(lambda: self.ride.middle(ride_in, ride_out, ride_scr))

        start.middle = middle
        return tuple(host_in) + tuple(host_out) + tuple(host_scr), start, finish

    def results(self, outs, n_out):
        outs = list(outs) if isinstance(outs, (list, tuple)) else [outs]
        return outs[:n_out], outs[n_out:]


def _join_rides(a, b):
    na, nb_ = len(a.arrays), len(b.arrays)
    oa = len(a.out_shapes)
    sa = len(a.scratch_shapes)

    def both(fa, fb):
        def run(ins, outs, sems):
            if fa is not None:
                fa(ins[:na], outs[:oa], sems[:sa])
            if fb is not None:
                fb(ins[na:na + nb_], outs[oa:], sems[sa:])
        return run

    middle = both(a.middle, b.middle) if (a.middle or b.middle) else None
    return _Ride(a.arrays + b.arrays, a.out_shapes + b.out_shapes, a.scratch_shapes + b.scratch_shapes,
                 both(a.start, b.start), both(a.finish, b.finish), middle)


def _load_resident(w_hbm, w_vmem, sems, first):
    def piece(c):
        rows = pl.ds(c * SEG_CHUNK, SEG_CHUNK)
        return pltpu.make_async_copy(w_hbm.at[rows], w_vmem.at[rows], sems.at[c])

    @pl.when(first)
    def _():
        for c in range(w_vmem.shape[0] // SEG_CHUNK):
            piece(c).start()

    def ready(c):
        @pl.when(first)
        def _():
            piece(c).wait()

    return ready


CHIP_ROWS = 2 * SHARD_IN
PROJ_WIDTHS = (2 * D_RNN, D_MODEL + 2 * D_KV, 3 * D_MODEL)
PROJ_DTYPES = (F32, BF16, BF16)


def _chip_pieces():
    starts = [0, PROJ_WIDTHS[0], PROJ_WIDTHS[0] + PROJ_WIDTHS[1], D_IN]
    pieces = []
    for k in range(N_CHIP):
        lo, hi = k * CHIP_ROWS, (k + 1) * CHIP_ROWS
        cur = []
        for a in range(len(PROJ_WIDTHS)):
            s0, s1 = max(lo, starts[a]), min(hi, starts[a + 1])
            if s0 < s1:
                cur.append((a, s0 - starts[a], s1 - s0, s0 - lo))
        pieces.append(cur)
    return pieces


def _gather_project(x, g, wt_shard, name, tm=1024):
    t, k = x.shape
    tm = min(tm, t)
    nt = t // tm
    pieces = _chip_pieces()

    def body(x_ref, g_ref, shard_ref, h_out, rx_ref, qkv_ref, ag_ref, wt_all,
             w_c, stage, o32, o16, h_all, send_sems, recv_sems, local_sem, stage_sems, out_sems, h_sems):
        s, ti = pl.program_id(0), pl.program_id(1)
        px, py, pc = _place()
        me, sibling = (px, py, pc), (px, py, 1 - pc)
        chips = [(px, py), (1 - px, py), (px, 1 - py), (1 - px, 1 - py)]
        outs = (rx_ref, qkv_ref, ag_ref)

        def slot(dev):
            return wt_all.at[4 * dev[0] + 2 * dev[1] + dev[2]]

        def copy(kk, block, to, src=None):
            return pltpu.make_async_remote_copy(
                src_ref=slot(block) if src is None else src, dst_ref=slot(block),
                send_sem=send_sems.at[kk], recv_sem=recv_sems.at[kk], device_id=to, device_id_type=MESH)

        mine = pltpu.make_async_copy(shard_ref, slot(me), local_sem)
        first = [copy(0, me, sibling, src=shard_ref)] + [copy(1 + j, me, (*chips[1 + j], pc), src=shard_ref) for j in range(3)]
        passed = [copy(4 + j, (*chips[1 + j], pc), sibling) for j in range(3)]

        @pl.when((s == 0) & (ti == 0))
        def _():
            mine.start()
            for cp in first[:3]:
                cp.start()

        def pass_on(step):
            copy(step, (*chips[step], pc), me).wait_recv()
            passed[step - 1].start()

        @pl.when((s == 2) & (ti == nt - 1))
        def _():
            pass_on(3)

        for step in range(N_CHIP):
            @pl.when((s == step) & (ti == 0))
            def _(step=step):
                chip = chips[step]
                if step == 0:
                    mine.wait()
                    copy(0, sibling, me).wait_recv()
                else:
                    if step == 1:
                        pass_on(1)
                        first[3].start()
                        pass_on(2)
                    copy(3 + step, (*chip, 1 - pc), me).wait_recv()
                loads = [pltpu.make_async_copy(slot((*chip, core)), stage.at[pl.ds(core * SHARD_IN, SHARD_IN)], stage_sems.at[core])
                         for core in (0, 1)]
                for cp in loads:
                    cp.start()
                for cp in loads:
                    cp.wait()
                w_c[...] = stage[...].T

        rows = pl.ds(pl.multiple_of(ti * tm, tm), tm)

        @pl.when(s == 0)
        def _():
            xv = x_ref[...]
            h_all[rows, :] = (xv * lax.rsqrt(jnp.mean(xv * xv, axis=-1, keepdims=True) + EPS) * g_ref[...]).astype(BF16)

        res = jnp.dot(h_all[rows, :], w_c[...], preferred_element_type=F32)

        n = s * nt + ti
        buf = lax.rem(n, 2)
        chip_idx = [2 * cx + cy for cx, cy in chips]

        def chip_at(step):
            return jnp.where(step == 0, chip_idx[0], jnp.where(step == 1, chip_idx[1], jnp.where(step == 2, chip_idx[2], chip_idx[3])))

        def h_write(b, tile):
            return pltpu.make_async_copy(h_all.at[pl.ds(tile * tm, tm)], h_out.at[pl.ds(tile * tm, tm)], h_sems.at[b])

        def writes(kchip, b, tile):
            cps = []
            for idx, (a, col, w, src) in enumerate(pieces[kchip]):
                staged = (o16 if PROJ_DTYPES[a] == BF16 else o32).at[b, :, pl.ds(src, w)]
                cps.append(pltpu.make_async_copy(staged, outs[a].at[pl.ds(tile * tm, tm), pl.ds(col, w)], out_sems.at[b, idx]))
            return cps

        o32[buf] = res
        o16[buf] = res.astype(BF16)
        kcur = chip_at(s)

        @pl.when(n > 0)
        def _():
            kprev = chip_at(lax.div(n - 1, nt))
            for kchip in range(N_CHIP):
                @pl.when(kprev == kchip)
                def _(kchip=kchip):
                    for cp in writes(kchip, 1 - buf, lax.rem(n - 1, nt)):
                        cp.wait()

            @pl.when(n <= nt)
            def _():
                h_write(1 - buf, n - 1).wait()

        @pl.when(s == 0)
        def _():
            h_write(buf, ti).start()

        for kchip in range(N_CHIP):
            @pl.when(kcur == kchip)
            def _(kchip=kchip):
                for cp in writes(kchip, buf, ti):
                    cp.start()

        @pl.when(n == N_CHIP * nt - 1)
        def _():
            for kchip in range(N_CHIP):
                @pl.when(kcur == kchip)
                def _(kchip=kchip):
                    for cp in writes(kchip, buf, ti):
                        cp.wait()
            for cp in first + passed:
                cp.wait_send()

    tile = pl.BlockSpec((tm, k), lambda s, ti: (jnp.where(s == 0, ti, nt - 1), 0))
    return pl.pallas_call(
        body, name=name, grid=(N_CHIP, nt),
        in_specs=[tile, pl.BlockSpec((1, k), lambda s, ti: (0, 0)), ANY],
        out_specs=[ANY, ANY, ANY, ANY, ANY],
        out_shape=[jax.ShapeDtypeStruct((t, k), BF16)]
        + [jax.ShapeDtypeStruct((t, w), dt) for w, dt in zip(PROJ_WIDTHS, PROJ_DTYPES)]
        + [jax.ShapeDtypeStruct((N_DEV, SHARD_IN, k), BF16)],
        scratch_shapes=[pltpu.VMEM((k, CHIP_ROWS), BF16), pltpu.VMEM((CHIP_ROWS, k), BF16),
                        pltpu.VMEM((2, tm, CHIP_ROWS), F32), pltpu.VMEM((2, tm, CHIP_ROWS), BF16), pltpu.VMEM((t, k), BF16),
                        pltpu.SemaphoreType.DMA((7,)), pltpu.SemaphoreType.DMA((7,)), pltpu.SemaphoreType.DMA,
                        pltpu.SemaphoreType.DMA((2,)), pltpu.SemaphoreType.DMA((2, 2)), pltpu.SemaphoreType.DMA((2,))],
        compiler_params=_params("arbitrary", "arbitrary"))(x, g, wt_shard)


def _mm_tn_pairs(prods, name):
    n_prod = len(prods)
    ktok, m = prods[0][0].shape
    n = prods[0][1].shape[1]
    half, blk = m // 2, m // N_DEV
    ns = 2 * n_prod
    kept = N_CHIP * blk

    def side_pieces(s, side):
        i, hf = divmod(s, 2)
        first = hf * N_DEV // 2
        return [((d - first) * blk, i * kept + (d // 2) * blk) for d in range(first, first + N_DEV // 2) if d % 2 == side]

    def body(*refs):
        a_hbm, b_hbm, pair_refs = refs[:n_prod], refs[n_prod:2 * n_prod], refs[2 * n_prod:3 * n_prod]
        a_buf, b_buf, res_buf, recv_all, pair_all, a_sems, b_sems, send_sems, recv_sems, out_sems = refs[3 * n_prod:]
        step = pl.program_id(0)
        px, py, pc = _place()

        def fetch_a(s):
            return pltpu.make_async_copy(a_hbm[s // 2].at[:, pl.ds((s % 2) * half, half)], a_buf.at[s % 2], a_sems.at[s % 2])

        def fetch_b(i):
            return pltpu.make_async_copy(b_hbm[i], b_buf.at[i % 2], b_sems.at[i % 2])

        @pl.when(step == 0)
        def _():
            fetch_a(0).start()
            fetch_b(0).start()

        for s in range(ns):
            @pl.when(step == s)
            def _(s=s):
                if s + 1 < ns:
                    fetch_a(s + 1).start()
                if s % 2 == 0 and s // 2 + 1 < n_prod:
                    fetch_b(s // 2 + 1).start()
                fetch_a(s).wait()
                if s % 2 == 0:
                    fetch_b(s // 2).wait()

        res_buf[step % 2] = lax.dot_general(a_buf[step % 2], b_buf[(step // 2) % 2], TN_DIMS, preferred_element_type=F32)

        def crossing(s, j, piece):
            off, at = piece
            return pltpu.make_async_remote_copy(
                src_ref=res_buf.at[s % 2, pl.ds(off, blk)], dst_ref=recv_all.at[pl.ds(at, blk)],
                send_sem=send_sems.at[s, j], recv_sem=recv_sems.at[s, j], device_id=(px, py, 1 - pc), device_id_type=MESH)

        def write(s, core):
            at = side_pieces(s, core)[0][1]
            return pltpu.make_async_copy(pair_all.at[pl.ds(at, 2 * blk)], pair_refs[s // 2].at[pl.ds(at % kept, 2 * blk)], out_sems.at[s])

        for core in (0, 1):
            def settle(s, core=core):
                for j, piece in enumerate(side_pieces(s, 1 - core)):
                    crossing(s, j, piece).wait_send()
                for j, (off, at) in enumerate(side_pieces(s, core)):
                    crossing(s, j, (off, at)).wait_recv()
                    pair_all[at:at + blk] = (res_buf[s % 2, off:off + blk] + recv_all[at:at + blk]).astype(BF16)
                write(s, core).start()

            for s in range(ns):
                @pl.when((pc == core) & (step == s))
                def _(s=s, settle=settle, core=core):
                    for j, piece in enumerate(side_pieces(s, 1 - core)):
                        crossing(s, j, piece).start()
                    if s > 0:
                        settle(s - 1)
                    if s == ns - 1:
                        settle(s)
                        for t in range(ns):
                            write(t, core).wait()

    pairs = pl.pallas_call(
        body, name=name, grid=(ns,),
        in_specs=[ANY] * (2 * n_prod), out_specs=[ANY] * n_prod,
        out_shape=[jax.ShapeDtypeStruct((kept, n), BF16)] * n_prod,
        scratch_shapes=[pltpu.VMEM((2, ktok, half), prods[0][0].dtype), pltpu.VMEM((2, ktok, n), prods[0][1].dtype),
                        pltpu.VMEM((2, half, n), F32), pltpu.VMEM((n_prod * kept, n), F32), pltpu.VMEM((n_prod * kept, n), BF16),
                        pltpu.SemaphoreType.DMA((2,)), pltpu.SemaphoreType.DMA((2,)),
                        pltpu.SemaphoreType.DMA((ns, 2)), pltpu.SemaphoreType.DMA((ns, 2)), pltpu.SemaphoreType.DMA((ns,))],
        compiler_params=_params("arbitrary"))(*[a for a, _ in prods], *[b for _, b in prods])
    return [pair.reshape(N_CHIP, blk, n) for pair in pairs]


def _segment_chunks(segs):
    bounds = [0]
    for s in segs:
        bounds.append(bounds[-1] + s.shape[1] // SEG_CHUNK)
    return bounds


def _input_grad(segs, wt, x, g, dy, name, tm=512, ride=None):
    m = segs[0].shape[0]
    rows, n = wt.shape
    tm = min(tm, m)
    bounds = _segment_chunks(segs)
    n_seg = len(segs)
    ni = m // tm
    host = _Hosted(ride, n_seg + 4, 2, 2, aliasing=True)

    def body(*refs):
        host_refs, start, finish = host.split(refs)
        a_refs = host_refs[:n_seg]
        wt_hbm, x_ref, g_ref, dy_ref, gx_ref, st_ref, wt_vmem, sems = host_refs[n_seg:]
        i = pl.program_id(0)
        start(i == 0)

        @pl.when(i == 0)
        def _():
            st_ref[...] = jnp.zeros_like(st_ref)

        ready = _load_resident(wt_hbm, wt_vmem, sems, i == 0)
        dh = None
        for s in range(n_seg):
            for c in range(bounds[s], bounds[s + 1]):
                ready(c)
            part = jnp.dot(a_refs[s][...], wt_vmem[bounds[s] * SEG_CHUNK:bounds[s + 1] * SEG_CHUNK, :], preferred_element_type=F32)
            dh = part if dh is None else dh + part
        xv = x_ref[...]
        r = lax.rsqrt(jnp.mean(xv * xv, axis=-1, keepdims=True) + EPS)
        xn = xv * r
        dxn = dh * g_ref[...]
        gx_ref[...] = dy_ref[...] + r * (dxn - xn * jnp.mean(dxn * xn, axis=-1, keepdims=True))
        st_ref[...] += _rows8([jnp.sum(dh * xn, axis=0, keepdims=True)], n)
        finish(i == ni - 1)

    tile = pl.BlockSpec((tm, n), lambda i: (i, 0))
    outs = pl.pallas_call(
        body, name=name, grid=(ni,),
        in_specs=[pl.BlockSpec((tm, sg.shape[1]), lambda i: (i, 0)) for sg in segs]
        + [ANY, tile, pl.BlockSpec((1, n), lambda i: (0, 0)), tile] + host.in_specs,
        out_specs=[tile, pl.BlockSpec((8, n), lambda i: (0, 0))] + host.out_specs,
        out_shape=[jax.ShapeDtypeStruct((m, n), F32), jax.ShapeDtypeStruct((8, n), F32)] + host.out_shapes,
        scratch_shapes=[pltpu.VMEM((rows, n), wt.dtype), pltpu.SemaphoreType.DMA((rows // SEG_CHUNK,))] + host.scratch_shapes,
        input_output_aliases=host.aliases,
        compiler_params=_params("arbitrary"))(*segs, wt, x, g, dy, *host.arrays)
    res, landed = host.results(outs, 2)
    return (*res, landed) if ride else tuple(res)


CHUNK_ORDER = (0, 4, 7, 10, 1, 5, 8, 11, 2, 6, 9, 12, 3)
EARLY_STEPS = 8


def _side_pieces():
    table = []
    for core in (0, 1):
        sides = ([None] * len(CHUNK_ORDER), [None] * len(CHUNK_ORDER))
        for s, cc in enumerate(CHUNK_ORDER):
            g0 = cc * SEG_CHUNK
            for d in range(N_DEV):
                lo, hi = max(g0, d * SHARD_IN), min(g0 + SEG_CHUNK, (d + 1) * SHARD_IN)
                if lo < hi:
                    side = sides[0 if d % 2 == core else 1]
                    assert side[s] is None
                    side[s] = (lo - g0, hi - lo, (d // 2) * SHARD_IN + lo - d * SHARD_IN)
        table.append(sides)
    return table


def _to_chip(src, red_ref, piece, s, send_sems, recv_sems, own, core):
    _, rows, at = piece
    k, r0 = divmod(at, SHARD_IN)
    return k, pltpu.make_async_remote_copy(
        src_ref=src, dst_ref=red_ref.at[own, pl.ds(r0, rows)], send_sem=send_sems.at[s], recv_sem=recv_sems.at[own, s],
        device_id=(k // 2, k % 2, core), device_id_type=MESH)


def _from_chip(src, red_ref, piece, s, send_sems, recv_sems, chip, me):
    _, rows, at = piece
    return pltpu.make_async_remote_copy(
        src_ref=src, dst_ref=red_ref.at[chip, pl.ds(at % SHARD_IN, rows)], send_sem=send_sems.at[s], recv_sem=recv_sems.at[chip, s],
        device_id=me, device_id_type=MESH)


def _mm_tn_seg_pair(segs, b, name):
    ktok = segs[0].shape[0]
    n = b.shape[1]
    bounds = _segment_chunks(segs)
    n_seg = len(segs)
    nc = bounds[-1]
    assert nc == len(CHUNK_ORDER)
    seg_of = [s for s in range(n_seg) for _ in range(bounds[s], bounds[s + 1])]
    table = _side_pieces()

    def body(*refs):
        a_hbm, b_hbm, pair_ref, red_ref = refs[:n_seg], refs[n_seg], refs[n_seg + 1], refs[n_seg + 2]
        (a_buf, b_vmem, res_buf, recv_all, pair_all, a_sems, b_sem, send_sems, recv_sems, out_sems,
         chip_send, chip_recv) = refs[n_seg + 3:]
        step = pl.program_id(0)
        px, py, pc = _place()
        own = 2 * px + py

        def fetch(s):
            sg = seg_of[CHUNK_ORDER[s]]
            cols = pl.ds((CHUNK_ORDER[s] - bounds[sg]) * SEG_CHUNK, SEG_CHUNK)
            return pltpu.make_async_copy(a_hbm[sg].at[:, cols], a_buf.at[s % 2], a_sems.at[s % 2])

        @pl.when(step == 0)
        def _():
            whole = pltpu.make_async_copy(b_hbm, b_vmem, b_sem)
            whole.start()
            fetch(0).start()
            whole.wait()

        for s in range(nc):
            @pl.when(step == s)
            def _(s=s):
                if s + 1 < nc:
                    fetch(s + 1).start()
                fetch(s).wait()

        res_buf[step % 2] = lax.dot_general(a_buf[step % 2], b_vmem[...], TN_DIMS, preferred_element_type=F32)

        def crossing(s, piece):
            off, rows, at = piece
            return pltpu.make_async_remote_copy(
                src_ref=res_buf.at[s % 2, pl.ds(off, rows)], dst_ref=recv_all.at[pl.ds(at, rows)],
                send_sem=send_sems.at[s], recv_sem=recv_sems.at[s], device_id=(px, py, 1 - pc), device_id_type=MESH)

        def write(s, piece):
            _, rows, at = piece
            return pltpu.make_async_copy(pair_all.at[pl.ds(at, rows)], pair_ref.at[pl.ds(at, rows)], out_sems.at[s])

        def to_chip(s, piece, core):
            return _to_chip(pair_all.at[pl.ds(piece[2], piece[1])], red_ref, piece, s, chip_send, chip_recv, own, core)

        for core in (0, 1):
            mine, theirs = table[core]

            def settle(s, mine=mine, theirs=theirs, core=core):
                if theirs[s] is not None:
                    crossing(s, theirs[s]).wait_send()
                if mine[s] is not None:
                    off, rows, at = mine[s]
                    crossing(s, mine[s]).wait_recv()
                    pair_all[at:at + rows] = (res_buf[s % 2, off:off + rows] + recv_all[at:at + rows]).astype(BF16)
                    write(s, mine[s]).start()
                    if s < EARLY_STEPS:
                        k, cp = to_chip(s, mine[s], core)
                        pl.when(own != k)(cp.start)

            for s in range(nc):
                @pl.when((pc == core) & (step == s))
                def _(s=s, settle=settle, mine=mine, theirs=theirs, core=core):
                    if theirs[s] is not None:
                        crossing(s, theirs[s]).start()
                    if s > 0:
                        settle(s - 1)
                    if s == nc - 1:
                        settle(s)
                        kept = [t for t in range(nc) if mine[t] is not None]
                        for t in kept:
                            write(t, mine[t]).wait()
                        early = [t for t in kept if t < EARLY_STEPS]
                        for t in early:
                            k, cp = to_chip(t, mine[t], core)
                            pl.when(own != k)(cp.wait_send)
                        for k in range(N_CHIP):
                            @pl.when(own == k)
                            def _(k=k):
                                for t in early:
                                    if mine[t][2] // SHARD_IN == k:
                                        for chip in range(N_CHIP):
                                            if chip != k:
                                                _from_chip(pair_all.at[pl.ds(mine[t][2], mine[t][1])], red_ref, mine[t], t,
                                                           chip_send, chip_recv, chip, (px, py, pc)).wait_recv()

    flat = jax.ShapeDtypeStruct((N_CHIP * SHARD_IN, n), BF16)
    pair, red = pl.pallas_call(
        body, name=name, grid=(nc,),
        in_specs=[ANY] * (n_seg + 1), out_specs=[ANY, ANY],
        out_shape=[flat, jax.ShapeDtypeStruct((N_CHIP, SHARD_IN, n), BF16)],
        scratch_shapes=[pltpu.VMEM((2, ktok, SEG_CHUNK), segs[0].dtype), pltpu.VMEM((ktok, n), b.dtype),
                        pltpu.VMEM((2, SEG_CHUNK, n), F32), pltpu.VMEM(flat.shape, F32), pltpu.VMEM(flat.shape, BF16),
                        pltpu.SemaphoreType.DMA((2,)), pltpu.SemaphoreType.DMA,
                        pltpu.SemaphoreType.DMA((nc,)), pltpu.SemaphoreType.DMA((nc,)), pltpu.SemaphoreType.DMA((nc,)),
                        pltpu.SemaphoreType.DMA((nc,)), pltpu.SemaphoreType.DMA((N_CHIP, nc))],
        compiler_params=_params("arbitrary"))(*segs, b)
    return pair.reshape(N_CHIP, SHARD_IN, n), red


def _branches_fwd(z_rnn, z_attn, ag_ml, b_gate, w_rnn, w_attn, w_out, x, target, g_post, name, tm=512):
    t, d = x.shape
    tm = min(tm, t)

    def body(zr_ref, za_ref, lr_ref, la_ref, br_ref, ba_ref, wr_ref, wa_ref, wo_ref, x_ref, t_ref, g_ref,
             brr_ref, bra_ref, mg_ref, do_ref, dy_ref, st_ref):
        @pl.when(pl.program_id(0) == 0)
        def _():
            st_ref[...] = jnp.zeros_like(st_ref)

        br_rnn = jnp.dot(zr_ref[...], wr_ref[...], preferred_element_type=F32)
        br_attn = jnp.dot(za_ref[...], wa_ref[...], preferred_element_type=F32)
        brr_ref[...] = br_rnn.astype(BF16)
        bra_ref[...] = br_attn.astype(BF16)
        g_rnn = _sigmoid(lr_ref[...].astype(F32) + br_ref[...])
        g_attn = _sigmoid(la_ref[...].astype(F32) + ba_ref[...])
        merged = (g_rnn * br_rnn + g_attn * br_attn).astype(BF16)
        mg_ref[...] = merged
        o = jnp.dot(merged, wo_ref[...], preferred_element_type=F32)
        g = g_ref[...]
        r = lax.rsqrt(jnp.mean(o * o, axis=-1, keepdims=True) + EPS)
        nrm = o * r
        err = x_ref[...] + nrm * g - t_ref[...]
        dy = err * (1.0 / d)
        dy_ref[...] = dy
        dn = dy * g
        do_ref[...] = (r * (dn - nrm * jnp.mean(dn * nrm, axis=-1, keepdims=True))).astype(BF16)
        st_ref[...] += _rows8([jnp.sum(dy * nrm, axis=0, keepdims=True), jnp.sum(err * err, axis=0, keepdims=True)], d)

    tile = pl.BlockSpec((tm, d), lambda i: (i, 0))
    weight = pl.BlockSpec((d, d), lambda i: (0, 0))
    bf = jax.ShapeDtypeStruct((t, d), BF16)
    return pl.pallas_call(
        body, name=name, grid=(t // tm,),
        in_specs=[tile, tile, pl.BlockSpec((tm, d), lambda i: (i, 1)), pl.BlockSpec((tm, d), lambda i: (i, 2)),
                  pl.BlockSpec((1, d), lambda i: (0, 0)), pl.BlockSpec((1, d), lambda i: (0, 1)),
                  weight, weight, weight, tile, tile, pl.BlockSpec((1, d), lambda i: (0, 0))],
        out_specs=[tile, tile, tile, tile, tile, pl.BlockSpec((8, d), lambda i: (0, 0))],
        out_shape=[bf, bf, bf, bf, jax.ShapeDtypeStruct((t, d), F32), jax.ShapeDtypeStruct((8, d), F32)],
        compiler_params=_params("arbitrary"))(z_rnn, z_attn, ag_ml, ag_ml, b_gate, b_gate, w_rnn, w_attn, w_out, x, target, g_post)


def _branches_bwd(dout, br_rnn, br_attn, ag_ml, b_gate, w_rnn, w_attn, w_out, name, tm=512):
    t, d = br_rnn.shape
    tm = min(tm, t)

    def body(do_ref, r_ref, a_ref, lr_ref, la_ref, br_ref, ba_ref, wr_ref, wa_ref, wo_ref,
             dr_ref, da_ref, dl_ref, dzr_ref, dza_ref, st_ref, wt_ref):
        @pl.when(pl.program_id(0) == 0)
        def _():
            st_ref[...] = jnp.zeros_like(st_ref)
            wt_ref[0] = wo_ref[...].T
            wt_ref[1] = wr_ref[...].T
            wt_ref[2] = wa_ref[...].T

        dm = jnp.dot(do_ref[...], wt_ref[0], preferred_element_type=F32)
        g_rnn = _sigmoid(lr_ref[...].astype(F32) + br_ref[...])
        g_attn = _sigmoid(la_ref[...].astype(F32) + ba_ref[...])
        dbr_rnn = (dm * g_rnn).astype(BF16)
        dbr_attn = (dm * g_attn).astype(BF16)
        dr_ref[...] = dbr_rnn
        da_ref[...] = dbr_attn
        dl_rnn = dm * r_ref[...].astype(F32) * g_rnn * (1.0 - g_rnn)
        dl_attn = dm * a_ref[...].astype(F32) * g_attn * (1.0 - g_attn)
        dl_ref[:, 0:d] = dl_rnn.astype(BF16)
        dl_ref[:, d:2 * d] = dl_attn.astype(BF16)
        st_ref[...] += _rows8([jnp.sum(dl_rnn, axis=0, keepdims=True), jnp.sum(dl_attn, axis=0, keepdims=True)], d)
        dzr_ref[...] = jnp.dot(dbr_rnn, wt_ref[1], preferred_element_type=F32).astype(BF16)
        dza_ref[...] = jnp.dot(dbr_attn, wt_ref[2], preferred_element_type=F32).astype(BF16)

    tile = pl.BlockSpec((tm, d), lambda i: (i, 0))
    weight = pl.BlockSpec((d, d), lambda i: (0, 0))
    bf = jax.ShapeDtypeStruct((t, d), BF16)
    return pl.pallas_call(
        body, name=name, grid=(t // tm,),
        in_specs=[tile, tile, tile, pl.BlockSpec((tm, d), lambda i: (i, 1)), pl.BlockSpec((tm, d), lambda i: (i, 2)),
                  pl.BlockSpec((1, d), lambda i: (0, 0)), pl.BlockSpec((1, d), lambda i: (0, 1)), weight, weight, weight],
        out_specs=[tile, tile, pl.BlockSpec((tm, 2 * d), lambda i: (i, 0)), tile, tile, pl.BlockSpec((8, d), lambda i: (0, 0))],
        out_shape=[bf, bf, jax.ShapeDtypeStruct((t, 2 * d), BF16), bf, bf, jax.ShapeDtypeStruct((8, d), F32)],
        scratch_shapes=[pltpu.VMEM((3, d, d), BF16)],
        compiler_params=_params("arbitrary"))(dout, br_rnn, br_attn, ag_ml, ag_ml, b_gate, b_gate, w_rnn, w_attn, w_out)


def _lru_decay(r, sp):
    log_a = (-LRU_C) * r * sp
    return log_a, jnp.exp(log_a)


def _lru_gates(c, wa, ba, wx, bx, sp):
    cb = c.astype(BF16)
    r = _sigmoid(jnp.dot(cb, wa, preferred_element_type=F32) + ba)
    ig = _sigmoid(jnp.dot(cb, wx, preferred_element_type=F32) + bx)
    log_a, a = _lru_decay(r, sp)
    mult = jnp.sqrt(-jnp.tanh(log_a) * (a * a + 1.0))
    return cb, r, ig, a, mult


SUBLANES = 8


def _scan_fwd(a, u, carry, tt):
    w = a.shape[1]
    ng = tt // SUBLANES
    a3 = a.reshape(ng, SUBLANES, w)
    u3 = u.reshape(ng, SUBLANES, w)
    sub = lax.broadcasted_iota(jnp.int32, (ng, SUBLANES, w), 1)
    d = 1
    while d < SUBLANES:
        keep = sub >= d
        u3 = u3 + a3 * jnp.where(keep, pltpu.roll(u3, d, 1), 0.0)
        a3 = a3 * jnp.where(keep, pltpu.roll(a3, d, 1), 1.0)
        d *= 2
    out = []
    for g in range(ng):
        hg = u3[g] + a3[g] * carry
        out.append(hg)
        carry = hg[SUBLANES - 1:SUBLANES, :]
    return jnp.concatenate(out, axis=0)


def _scan_rev(b, g, carry, tt):
    w = b.shape[1]
    ng = tt // SUBLANES
    b3 = b.reshape(ng, SUBLANES, w)
    g3 = g.reshape(ng, SUBLANES, w)
    sub = lax.broadcasted_iota(jnp.int32, (ng, SUBLANES, w), 1)
    d = 1
    while d < SUBLANES:
        keep = sub < SUBLANES - d
        g3 = g3 + b3 * jnp.where(keep, pltpu.roll(g3, SUBLANES - d, 1), 0.0)
        b3 = b3 * jnp.where(keep, pltpu.roll(b3, SUBLANES - d, 1), 1.0)
        d *= 2
    out = [None] * ng
    for k in range(ng - 1, -1, -1):
        hk = g3[k] + b3[k] * carry
        out[k] = hk
        carry = hk[0:1, :]
    return jnp.concatenate(out, axis=0)


def _conv_taps(cw, bias, x, ext_ref, tt):
    x2 = ext_ref[7:7 + tt, :]
    x1 = ext_ref[6:6 + tt, :]
    x0 = ext_ref[5:5 + tt, :]
    return bias + cw[3:4] * x + cw[2:3] * x2 + cw[1:2] * x1 + cw[0:1] * x0


def _rnn_fwd(rx_rg, cw, cb, wa, ba, wx, bx, lam, name, tt=512):
    t = rx_rg.shape[0]
    tt = min(tt, t)
    w = GROUP_W

    def body(rx_ref, rg_ref, cw_ref, cb_ref, wa_ref, ba_ref, wx_ref, bx_ref, lam_ref,
             y_ref, z_ref, c_ref, r_ref, ig_ref, mult_ref, ext_ref, hc_ref):
        @pl.when(pl.program_id(1) == 0)
        def _():
            ext_ref[0:8, :] = jnp.zeros((8, w), F32)
            hc_ref[...] = jnp.zeros((8, w), F32)

        x = rx_ref[...]
        ext_ref[8:8 + tt, :] = x
        c = _conv_taps(cw_ref[...], cb_ref[...], x, ext_ref, tt)
        ext_ref[0:8, :] = x[tt - 8:tt, :]
        sp = _softplus(-lam_ref[...])
        _, r, ig, a, mult = _lru_gates(c, wa_ref[...], ba_ref[...], wx_ref[...], bx_ref[...], sp)
        c_ref[...] = c
        r_ref[...] = r
        ig_ref[...] = ig
        mult_ref[...] = mult
        h = _scan_fwd(a, mult * (ig * c), hc_ref[7:8, :], tt)
        hc_ref[...] = h[tt - 8:tt, :]
        y_ref[...] = h
        rg = rg_ref[...]
        z_ref[...] = (h * rg * _sigmoid(rg)).astype(BF16)

    vec = pl.BlockSpec((1, w), lambda g, i: (0, g))
    mat = pl.BlockSpec((None, w, w), lambda g, i: (g, 0, 0))
    tile = pl.BlockSpec((tt, w), lambda g, i: (i, g))
    return pl.pallas_call(
        body, name=name, grid=(N_GROUPS, t // tt),
        in_specs=[tile, pl.BlockSpec((tt, w), lambda g, i: (i, N_GROUPS + g)),
                  pl.BlockSpec((4, w), lambda g, i: (0, g)), vec, mat, vec, mat, vec, vec],
        out_specs=[tile, tile] + [pl.BlockSpec((None, tt, w), lambda g, i: (g, i, 0))] * 4,
        out_shape=[jax.ShapeDtypeStruct((t, D_RNN), F32), jax.ShapeDtypeStruct((t, D_RNN), BF16)]
        + [jax.ShapeDtypeStruct((N_GROUPS, t, w), F32)] * 4,
        scratch_shapes=[pltpu.VMEM((tt + 8, w), F32), pltpu.VMEM((8, w), F32)],
        compiler_params=_params("parallel", "arbitrary"))(rx_rg, rx_rg, cw, cb, wa, ba, wx, bx, lam)


def _rnn_bwd(rx_rg, y, dz, c, r, ig, mult, cw, wa, wx, lam, name, tt=512):
    t = rx_rg.shape[0]
    tt = min(tt, t)
    nt = t // tt
    w = GROUP_W

    def body(rx_ref, rg_ref, y_ref, yt_ref, dz_ref, c_ref, r_ref, ig_ref, mult_ref, cw_ref, wa_ref, wx_ref, lam_ref,
             drx_ref, drg_ref, st_ref, gda_ref, gdx_ref, dcx_ref, wcar_ref, acar_ref, dwa_ref, dwx_ref):
        ii = pl.program_id(1)

        @pl.when(ii == 0)
        def _():
            wcar_ref[...] = jnp.zeros((8, w), F32)
            acar_ref[...] = jnp.zeros((8, w), F32)
            dcx_ref[tt:tt + 8, :] = jnp.zeros((8, w), F32)
            st_ref[...] = jnp.zeros_like(st_ref)
            dwa_ref[...] = jnp.zeros_like(dwa_ref)
            dwx_ref[...] = jnp.zeros_like(dwx_ref)

        has_prev = jnp.where(ii == nt - 1, 0.0, 1.0)
        cwv = cw_ref[...]
        lam = lam_ref[...]
        sp = _softplus(-lam)
        wa = wa_ref[...]
        wx = wx_ref[...]
        c = c_ref[...]
        r = r_ref[...]
        ig = ig_ref[...]
        mult = mult_ref[...]
        _, a = _lru_decay(r, sp)
        cb16 = c.astype(BF16)

        rg = rg_ref[...]
        sg = _sigmoid(rg)
        dz = dz_ref[...].astype(F32)
        yv = y_ref[...]
        drg_ref[...] = (dz * yv * (sg * (1.0 + rg * (1.0 - sg)))).astype(BF16)

        row = lax.broadcasted_iota(jnp.int32, (tt, w), 0)
        b = jnp.where(row < tt - 1, pltpu.roll(a, tt - 1, 0), acar_ref[0:1, :])
        dh = _scan_rev(b, dz * (rg * sg), wcar_ref[0:1, :], tt)
        wcar_ref[...] = dh[0:8, :]
        acar_ref[...] = a[0:8, :]

        hprev = jnp.where(row >= 1, pltpu.roll(yv, 1, 0), yt_ref[7:8, :] * has_prev)
        dmult = dh * (ig * c)
        dig = dh * mult * c
        dlog_a = dh * hprev * a - dmult * (a * a / mult)
        dpa = dlog_a * ((-LRU_C) * sp) * r * (1.0 - r)
        dpx = dig * ig * (1.0 - ig)
        dsp = jnp.sum(dlog_a * r, axis=0, keepdims=True) * (-LRU_C)
        dlam = dsp * (-_sigmoid(-lam))
        dpa16 = dpa.astype(BF16)
        dpx16 = dpx.astype(BF16)
        dwa_ref[...] += lax.dot_general(cb16, dpa16, TN_DIMS, preferred_element_type=F32)
        dwx_ref[...] += lax.dot_general(cb16, dpx16, TN_DIMS, preferred_element_type=F32)
        dc = (dh * mult * ig
              + lax.dot_general(dpa16, wa, NT_DIMS, preferred_element_type=F32)
              + lax.dot_general(dpx16, wx, NT_DIMS, preferred_element_type=F32))

        dcx_ref[0:tt, :] = dc
        dc1 = dcx_ref[1:1 + tt, :]
        dc2 = dcx_ref[2:2 + tt, :]
        dc3 = dcx_ref[3:3 + tt, :]
        drx_ref[...] = (cwv[3:4] * dc + cwv[2:3] * dc1 + cwv[1:2] * dc2 + cwv[0:1] * dc3).astype(BF16)
        dcx_ref[tt:tt + 8, :] = dc[0:8, :]

        def colsum(v):
            return jnp.sum(v, axis=0, keepdims=True)

        x = rx_ref[...]
        st_ref[...] += _rows8([colsum(dc), colsum(dpa), colsum(dpx), dlam,
                               colsum(dc3 * x), colsum(dc2 * x), colsum(dc1 * x), colsum(dc * x)], w)

        @pl.when(ii == nt - 1)
        def _():
            for blk in range(GROUP_W // RNN_BLOCK_W):
                rows = slice(blk * RNN_BLOCK_W, (blk + 1) * RNN_BLOCK_W)
                gda_ref[blk] = dwa_ref[rows, rows]
                gdx_ref[blk] = dwx_ref[rows, rows]

    def rev(ii):
        return nt - 1 - ii

    def tail(g, ii):
        return (jnp.maximum(rev(ii) * (tt // 8) - 1, 0), g)

    vec = pl.BlockSpec((1, w), lambda g, ii: (0, g))
    mat = pl.BlockSpec((None, w, w), lambda g, ii: (g, 0, 0))
    tile = pl.BlockSpec((tt, w), lambda g, ii: (rev(ii), g))
    diag_shape = (N_GROUPS, GROUP_W // RNN_BLOCK_W, RNN_BLOCK_W, RNN_BLOCK_W)
    diag = pl.BlockSpec((None,) + diag_shape[1:], lambda g, ii: (g, 0, 0, 0))
    kept = pl.BlockSpec((None, tt, w), lambda g, ii: (g, rev(ii), 0))
    return pl.pallas_call(
        body, name=name, grid=(N_GROUPS, nt),
        in_specs=[tile, pl.BlockSpec((tt, w), lambda g, ii: (rev(ii), N_GROUPS + g)),
                  tile, pl.BlockSpec((8, w), tail), tile, kept, kept, kept, kept,
                  pl.BlockSpec((4, w), lambda g, ii: (0, g)), mat, mat, vec],
        out_specs=[tile, tile, pl.BlockSpec((8, w), lambda g, ii: (0, g)), diag, diag],
        out_shape=[jax.ShapeDtypeStruct((t, D_RNN), BF16), jax.ShapeDtypeStruct((t, D_RNN), BF16),
                   jax.ShapeDtypeStruct((8, D_RNN), F32),
                   jax.ShapeDtypeStruct(diag_shape, F32), jax.ShapeDtypeStruct(diag_shape, F32)],
        scratch_shapes=[pltpu.VMEM((tt + 8, w), F32), pltpu.VMEM((8, w), F32),
                        pltpu.VMEM((8, w), F32), pltpu.VMEM((w, w), F32), pltpu.VMEM((w, w), F32)],
        compiler_params=_params("parallel", "arbitrary"))(rx_rg, rx_rg, y, y, dz, c, r, ig, mult, cw, wa, wx, lam)


def _half_mask(shape, half):
    lane = lax.broadcasted_iota(jnp.int32, shape, 1)
    return (lane >= HEAD_DIM) if half else (lane < HEAD_DIM)


def _dup_half(t, half):
    sel = jnp.where(_half_mask(t.shape, half), t, 0.0)
    return sel + pltpu.roll(sel, HEAD_DIM, 1)


def _band_geometry(n):
    qi = lax.broadcasted_iota(jnp.int32, (BLOCK, 2 * BLOCK), 0)
    kj = lax.broadcasted_iota(jnp.int32, (BLOCK, 2 * BLOCK), 1)
    dist = BLOCK + qi - kj
    first_key = jnp.where(n > 0, 0, BLOCK)
    valid = (dist >= 0) & (dist < BLOCK) & (kj >= first_key)
    return dist.astype(F32), valid


GROUP = 4


def _kv_dup(prev_ref, cur_ref, hk, scale=1.0):
    tile = hk // 2
    kt = jnp.concatenate([prev_ref[:, tile * LANE:(tile + 1) * LANE], cur_ref[:, tile * LANE:(tile + 1) * LANE]], axis=0)
    return (_dup_half(kt.astype(F32), hk % 2) * scale).astype(BF16)


def _fill_bias(bias_ref, sm_ref, n):
    distf, valid = _band_geometry(n)
    for head in range(N_Q_HEADS):
        bias_ref[head] = jnp.where(valid, -sm_ref[1, head] * distf, MASKED)


def _head_scores(q2s, half, kdup, bias):
    qm = jnp.where(_half_mask(q2s.shape, half), q2s, jnp.zeros_like(q2s))
    return qm, lax.dot_general(qm, kdup, NT_DIMS, preferred_element_type=F32) + bias


def _attn_specs(nb, clamp_last):
    def blk(n):
        return jnp.minimum(n, nb - 1) if clamp_last else n

    q_spec = pl.BlockSpec((BLOCK, D_MODEL), lambda n: (blk(n), 0))
    k_prev = pl.BlockSpec((BLOCK, D_KV), lambda n: (jnp.maximum(blk(n) - 1, 0), D_MODEL // D_KV))
    k_cur = pl.BlockSpec((BLOCK, D_KV), lambda n: (blk(n), D_MODEL // D_KV))
    v_prev = pl.BlockSpec((BLOCK, D_KV), lambda n: (jnp.maximum(blk(n) - 1, 0), D_MODEL // D_KV + 1))
    v_cur = pl.BlockSpec((BLOCK, D_KV), lambda n: (blk(n), D_MODEL // D_KV + 1))
    return q_spec, k_prev, k_cur, v_prev, v_cur


def _attn_fwd(sm, qkv, ag_ml, name, ride=None):
    t = qkv.shape[0]
    nb = t // BLOCK

    host = _Hosted(ride, 7, 3, 1)

    def body(*refs):
        (sm_ref, q_ref, kp_ref, kc_ref, vp_ref, vc_ref, ag_ref, y_ref, z_ref, lse_ref, bias_ref), start, finish = host.split(refs)
        n = pl.program_id(0)
        start(n == 0)
        start.middle(n == (3 * nb) // 4)

        @pl.when(n <= 1)
        def _():
            _fill_bias(bias_ref, sm_ref, n)

        lane = lax.broadcasted_iota(jnp.int32, (BLOCK, LANE), 1)
        low = lane < HEAD_DIM
        lse = jnp.zeros((BLOCK, LANE), F32)
        for hk in range(N_Q_HEADS // GROUP):
            kdup = _kv_dup(kp_ref, kc_ref, hk)
            vdup = _kv_dup(vp_ref, vc_ref, hk)
            for k in (0, 1):
                c = slice((2 * hk + k) * LANE, (2 * hk + k + 1) * LANE)
                q2s = q_ref[:, c] * ATTN_SCALE
                outs = []
                for half in (0, 1):
                    head = GROUP * hk + 2 * k + half
                    _, s = _head_scores(q2s, half, kdup, bias_ref[head])
                    sink = sm_ref[0, head]
                    m = jnp.maximum(jnp.max(s, axis=1, keepdims=True), sink)
                    e = jnp.exp(s - m)
                    l = jnp.sum(e, axis=1, keepdims=True) + jnp.exp(sink - m)
                    outs.append(jnp.dot((e * (1.0 / l)).astype(BF16), vdup, preferred_element_type=F32))
                    lse = jnp.where(lane == head, m + jnp.log(l), lse)
                yt = jnp.where(low, outs[0], outs[1])
                y_ref[:, c] = yt
                ag = ag_ref[:, c].astype(F32)
                z_ref[:, c] = (yt * ag * _sigmoid(ag)).astype(BF16)
        lse_ref[...] = lse
        finish(n == nb - 1)

    q_spec, k_prev, k_cur, v_prev, v_cur = _attn_specs(nb, False)
    wide = pl.BlockSpec((BLOCK, D_MODEL), lambda n: (n, 0))
    outs = pl.pallas_call(
        body, name=name, grid=(nb,),
        in_specs=[pl.BlockSpec(memory_space=pltpu.SMEM), q_spec, k_prev, k_cur, v_prev, v_cur, wide] + host.in_specs,
        out_specs=[wide, wide, pl.BlockSpec((BLOCK, LANE), lambda n: (n, 0))] + host.out_specs,
        out_shape=[jax.ShapeDtypeStruct((t, D_MODEL), F32), jax.ShapeDtypeStruct((t, D_MODEL), BF16),
                   jax.ShapeDtypeStruct((t, LANE), F32)] + host.out_shapes,
        scratch_shapes=[pltpu.VMEM((N_Q_HEADS, BLOCK, 2 * BLOCK), F32)] + host.scratch_shapes,
        compiler_params=_params("arbitrary"))(sm, qkv, qkv, qkv, qkv, qkv, ag_ml, *host.arrays)
    res, landed = host.results(outs, 3)
    return (*res, landed) if ride else tuple(res)


def _attn_bwd(sm, qkv, ag_ml, y, lse, dz, name, ride=None):
    t = qkv.shape[0]
    nb = t // BLOCK
    host = _Hosted(ride, 10, 4, 3)

    def body(*refs):
        host_refs, start, finish = host.split(refs)
        (sm_ref, q_ref, kp_ref, kc_ref, vp_ref, vc_ref, ag_ref, y_ref, lse_ref, dz_ref,
         dq_ref, dkv_ref, dag_ref, ds_ref, ck_ref, cv_ref, bias_ref) = host_refs
        n = pl.program_id(0)
        start(n == 0)
        start.middle(n == (3 * nb) // 4)

        @pl.when(n == 0)
        def _():
            ck_ref[...] = jnp.zeros_like(ck_ref)
            cv_ref[...] = jnp.zeros_like(cv_ref)
            ds_ref[...] = jnp.zeros_like(ds_ref)

        @pl.when(n <= 1)
        def _():
            _fill_bias(bias_ref, sm_ref, n)

        @pl.when(n < nb)
        def _():
            lane8 = lax.broadcasted_iota(jnp.int32, (8, LANE), 1)
            row8 = lax.broadcasted_iota(jnp.int32, (8, LANE), 0)
            dsink = jnp.zeros((8, LANE), F32)
            dk_heads, dv_heads = [], []
            lse_tile = lse_ref[...]
            for hk in range(N_Q_HEADS // GROUP):
                kdup = _kv_dup(kp_ref, kc_ref, hk)
                ks = _kv_dup(kp_ref, kc_ref, hk, ATTN_SCALE)
                vdup = _kv_dup(vp_ref, vc_ref, hk)
                qms, dyhs, y_rows = [], [], []
                for k in (0, 1):
                    c = slice((2 * hk + k) * LANE, (2 * hk + k + 1) * LANE)
                    ag = ag_ref[:, c].astype(F32)
                    sg = _sigmoid(ag)
                    dzt = dz_ref[:, c].astype(F32)
                    yt = y_ref[:, c]
                    dag_ref[:, c] = (dzt * yt * (sg * (1.0 + ag * (1.0 - sg)))).astype(BF16)
                    dyt = dzt * (ag * sg)
                    q2s = q_ref[:, c] * ATTN_SCALE
                    for half in (0, 1):
                        hm = _half_mask(q2s.shape, half)
                        qms.append(jnp.where(hm, q2s, jnp.zeros_like(q2s)))
                        dyhs.append(jnp.where(hm, dyt, 0.0))
                        y_rows.append(yt)
                qm4 = jnp.concatenate(qms, axis=0)
                dy4 = jnp.concatenate(dyhs, axis=0)
                dy4_16 = dy4.astype(BF16)
                s4 = lax.dot_general(qm4, kdup, NT_DIMS, preferred_element_type=F32)
                dp4 = lax.dot_general(dy4_16, vdup, NT_DIMS, preferred_element_type=F32)
                probs16, ds16 = [], []
                for r in range(GROUP):
                    head = GROUP * hk + r
                    rows = slice(r * BLOCK, (r + 1) * BLOCK)
                    lh = lse_tile[:, head:head + 1]
                    probs = jnp.exp(s4[rows] + bias_ref[head] - lh)
                    psink = jnp.exp(sm_ref[0, head] - lh)
                    delta = jnp.sum(dyhs[r] * y_rows[r], axis=1, keepdims=True)
                    ds16.append((probs * (dp4[rows] - delta)).astype(BF16))
                    probs16.append(probs.astype(BF16))
                    dsink = dsink + jnp.where((row8 == 0) & (lane8 == head),
                                              -jnp.sum(psink * delta, axis=0, keepdims=True), 0.0)
                ds4 = jnp.concatenate(ds16, axis=0)
                p4 = jnp.concatenate(probs16, axis=0)
                dq4 = jnp.dot(ds4, ks, preferred_element_type=F32)
                low = _half_mask((BLOCK, LANE), 0)
                for k in (0, 1):
                    c = slice((2 * hk + k) * LANE, (2 * hk + k + 1) * LANE)
                    dq_ref[:, c] = jnp.where(low, dq4[2 * k * BLOCK:(2 * k + 1) * BLOCK],
                                             dq4[(2 * k + 1) * BLOCK:(2 * k + 2) * BLOCK]).astype(BF16)
                dk_acc = lax.dot_general(ds4, qm4, TN_DIMS, preferred_element_type=F32)
                dv_acc = lax.dot_general(p4, dy4_16, TN_DIMS, preferred_element_type=F32)
                dk_heads.append(dk_acc + pltpu.roll(dk_acc, HEAD_DIM, 1))
                dv_heads.append(dv_acc + pltpu.roll(dv_acc, HEAD_DIM, 1))
            ds_ref[...] += dsink
            low = _half_mask((2 * BLOCK, LANE), 0)
            for tile in range(2):
                cols = slice(tile * LANE, (tile + 1) * LANE)
                dkt = jnp.where(low, dk_heads[2 * tile], dk_heads[2 * tile + 1])
                dvt = jnp.where(low, dv_heads[2 * tile], dv_heads[2 * tile + 1])
                dkv_ref[:, cols] = (ck_ref[:, cols] + dkt[0:BLOCK, :]).astype(BF16)
                dkv_ref[:, D_KV + tile * LANE:D_KV + (tile + 1) * LANE] = (cv_ref[:, cols] + dvt[0:BLOCK, :]).astype(BF16)
                ck_ref[:, cols] = dkt[BLOCK:2 * BLOCK, :]
                cv_ref[:, cols] = dvt[BLOCK:2 * BLOCK, :]

        @pl.when(n == nb)
        def _():
            dkv_ref[:, 0:D_KV] = ck_ref[...].astype(BF16)
            dkv_ref[:, D_KV:2 * D_KV] = cv_ref[...].astype(BF16)

        finish(n == nb)

    q_spec, k_prev, k_cur, v_prev, v_cur = _attn_specs(nb, True)
    wide = pl.BlockSpec((BLOCK, D_MODEL), lambda n: (jnp.minimum(n, nb - 1), 0))
    outs = pl.pallas_call(
        body, name=name, grid=(nb + 1,),
        in_specs=[pl.BlockSpec(memory_space=pltpu.SMEM), q_spec, k_prev, k_cur, v_prev, v_cur, wide, wide,
                  pl.BlockSpec((BLOCK, LANE), lambda n: (jnp.minimum(n, nb - 1), 0)), wide] + host.in_specs,
        out_specs=[wide, pl.BlockSpec((BLOCK, 2 * D_KV), lambda n: (jnp.maximum(n - 1, 0), 0)), wide,
                   pl.BlockSpec((8, LANE), lambda n: (0, 0))] + host.out_specs,
        out_shape=[jax.ShapeDtypeStruct((t, D_MODEL), BF16), jax.ShapeDtypeStruct((t, 2 * D_KV), BF16),
                   jax.ShapeDtypeStruct((t, D_MODEL), BF16), jax.ShapeDtypeStruct((8, LANE), F32)] + host.out_shapes,
        scratch_shapes=[pltpu.VMEM((BLOCK, D_KV), F32), pltpu.VMEM((BLOCK, D_KV), F32),
                        pltpu.VMEM((N_Q_HEADS, BLOCK, 2 * BLOCK), F32)] + host.scratch_shapes,
        compiler_params=_params("arbitrary"))(sm, qkv, qkv, qkv, qkv, qkv, ag_ml, y, lse, dz, *host.arrays)
    res, landed = host.results(outs, 4)
    return (*res, landed) if ride else tuple(res)


def _local_grads(x, target, p, project, late_weights, reduce_out, reduce_in):
    h, rx_rg, qkv, ag_ml, wt = project(x, p["pre_g"])
    y_attn, z_attn, lse, landed = _attn_fwd(p["sm"], qkv, ag_ml, "attn_fwd", ride=late_weights[0])
    p = {**p, **late_weights[1](landed)}
    y_rnn, z_rnn, *kept_rnn = _rnn_fwd(
        rx_rg, p["cw"], p["cb"], p["wbd_a"], p["b_a"], p["wbd_x"], p["b_x"], p["lam"], "rnn_fwd")
    br_rnn, br_attn, merged, dout, dy, st_post = _branches_fwd(
        z_rnn, z_attn, ag_ml, p["b_gate"], p["w_rnn"], p["w_attn"], p["w_out"], x, target, p["post_g"], "branches_fwd")

    dbr_rnn, dbr_attn, d_ml, dz_rnn, dz_attn, st_merge = _branches_bwd(
        dout, br_rnn, br_attn, ag_ml, p["b_gate"], p["w_rnn"], p["w_attn"], p["w_out"], "branches_bwd")
    out_pairs = _mm_tn_pairs([(z_rnn, dbr_rnn), (z_attn, dbr_attn), (merged, dout)], "gw_outs")
    d_rx, d_rg, st_rnn, g_rg_a, g_rg_x = _rnn_bwd(
        rx_rg, y_rnn, dz_rnn, *kept_rnn, p["cw"], p["wbd_a"], p["wbd_x"], p["lam"], "rnn_bwd")
    dq, dkv, d_ag, st_sink, red_out = _attn_bwd(p["sm"], qkv, ag_ml, y_attn, lse, dz_attn, "attn_bwd",
                                                ride=reduce_out(out_pairs, g_rg_a, g_rg_x))

    segs = [d_rx, d_rg, dq, dkv, d_ag, d_ml]
    grad_x, st_pre, red_in = _input_grad(segs, wt, x, p["pre_g"], dy, "input_grad",
                                         ride=reduce_in(*_mm_tn_seg_pair(segs, h, "gw_in")))
    return dict(grad_x=grad_x, st_post=st_post, st_merge=st_merge, st_rnn=st_rnn, st_sink=st_sink, st_pre=st_pre,
                red_out=red_out, red_in=red_in)


def _place():
    x, y, c = lax.axis_index("x"), lax.axis_index("y"), lax.axis_index("c")
    return x, y, c


def _gather_ride(shards):
    n = len(shards)

    def copies(ins, outs, sems):
        send_sems, recv_sems, local_sems = sems
        x, y, c = _place()
        me, sibling = (x, y, c), (x, y, 1 - c)
        chips = [(1 - x, y), (x, 1 - y), (1 - x, 1 - y)]

        def slot(a, dev):
            return outs[a].at[4 * dev[0] + 2 * dev[1] + dev[2]]

        def copy(a, k, block, to, src=None):
            return pltpu.make_async_remote_copy(
                src_ref=slot(a, block) if src is None else src, dst_ref=slot(a, block),
                send_sem=send_sems.at[a, k], recv_sem=recv_sems.at[a, k], device_id=to, device_id_type=MESH)

        mine = [pltpu.make_async_copy(ins[a], slot(a, me), local_sems.at[a]) for a in range(n)]
        first = []
        for a in range(n):
            first.append(copy(a, 0, me, sibling, src=ins[a]))
            first += [copy(a, 1 + j, me, (*chip, c), src=ins[a]) for j, chip in enumerate(chips)]
        return me, sibling, chips, c, copy, mine, first

    def start(ins, outs, sems):
        *_, mine, first = copies(ins, outs, sems)
        for cp in mine + first:
            cp.start()

    def middle(ins, outs, sems):
        me, sibling, chips, c, copy, _, _ = copies(ins, outs, sems)
        for j, chip in enumerate(chips):
            for a in range(n):
                copy(a, 1 + j, (*chip, c), me).wait_recv()
                copy(a, 4 + j, (*chip, c), sibling).start()

    def finish(ins, outs, sems):
        me, sibling, chips, c, copy, mine, first = copies(ins, outs, sems)
        passed = [copy(a, 4 + j, (*chip, c), sibling) for j, chip in enumerate(chips) for a in range(n)]
        for a in range(n):
            copy(a, 0, sibling, me).wait_recv()
            for j, chip in enumerate(chips):
                copy(a, 4 + j, (*chip, 1 - c), me).wait_recv()
        for cp in first + passed:
            cp.wait_send()
        for cp in mine:
            cp.wait()

    return _Ride(
        shards, [jax.ShapeDtypeStruct((N_DEV, *s.shape), s.dtype) for s in shards],
        [pltpu.SemaphoreType.DMA((n, 7)), pltpu.SemaphoreType.DMA((n, 7)), pltpu.SemaphoreType.DMA((n,))],
        start, finish, middle)


def _chips_ride(scatter):
    n = len(scatter)

    def copies(ins, outs, sems):
        send_sems, recv_sems, local_sems = sems
        x, y, c = _place()
        own = 2 * x + y
        chips = [(1 - x, y), (x, 1 - y), (1 - x, 1 - y)]
        local = [pltpu.make_async_copy(ins[a].at[own], outs[a].at[own], local_sems.at[a]) for a in range(n)]
        sent = [pltpu.make_async_remote_copy(
            src_ref=ins[a].at[2 * chip[0] + chip[1]], dst_ref=outs[a].at[own],
            send_sem=send_sems.at[a, j], recv_sem=recv_sems.at[a, own], device_id=(*chip, c), device_id_type=MESH)
            for a in range(n) for j, chip in enumerate(chips)]
        return chips, c, local, sent

    def start(ins, outs, sems):
        _, _, local, sent = copies(ins, outs, sems)
        for cp in local + sent:
            cp.start()

    def finish(ins, outs, sems):
        send_sems, recv_sems, _ = sems
        chips, c, local, sent = copies(ins, outs, sems)
        for a in range(n):
            for chip in chips:
                k = 2 * chip[0] + chip[1]
                pltpu.make_async_remote_copy(
                    src_ref=outs[a].at[k], dst_ref=outs[a].at[k], send_sem=send_sems.at[a, 0],
                    recv_sem=recv_sems.at[a, k], device_id=(*chip, c), device_id_type=MESH).wait_recv()
        for cp in sent:
            cp.wait_send()
        for cp in local:
            cp.wait()

    return _Ride(
        list(scatter), [jax.ShapeDtypeStruct(s.shape, s.dtype) for s in scatter],
        [pltpu.SemaphoreType.DMA((n, 3)), pltpu.SemaphoreType.DMA((n, N_CHIP)), pltpu.SemaphoreType.DMA((n,))],
        start, finish)


def _chips_rest_ride(pair, red):
    table = _side_pieces()
    nc = len(CHUNK_ORDER)

    def each(ins, outs, sems, sending, landing):
        send_sems, recv_sems, _ = sems
        pair_ref, red_ref = ins[0], outs[0]
        x, y, c = _place()
        own = 2 * x + y
        for core in (0, 1):
            mine = table[core][0]
            rest = [s for s in range(EARLY_STEPS, nc) if mine[s] is not None]

            @pl.when(c == core)
            def _(mine=mine, rest=rest, core=core):
                for s in rest:
                    _, rows, at = mine[s]
                    k, cp = _to_chip(pair_ref.at[at // SHARD_IN, pl.ds(at % SHARD_IN, rows)], red_ref, mine[s], s,
                                     send_sems, recv_sems, own, core)
                    pl.when(own != k)(lambda cp=cp: sending(cp))
                if landing is not None:
                    for k in range(N_CHIP):
                        @pl.when(own == k)
                        def _(k=k):
                            for s in rest:
                                _, rows, at = mine[s]
                                if at // SHARD_IN == k:
                                    for chip in range(N_CHIP):
                                        if chip != k:
                                            landing(_from_chip(pair_ref.at[k, pl.ds(at % SHARD_IN, rows)], red_ref, mine[s], s,
                                                               send_sems, recv_sems, chip, (x, y, c)))

    def local(ins, outs, sems):
        x, y, _ = _place()
        return pltpu.make_async_copy(ins[0].at[2 * x + y], outs[0].at[2 * x + y], sems[2])

    def start(ins, outs, sems):
        local(ins, outs, sems).start()
        each(ins, outs, sems, lambda cp: cp.start(), None)

    def finish(ins, outs, sems):
        each(ins, outs, sems, lambda cp: cp.wait_send(), lambda cp: cp.wait_recv())
        local(ins, outs, sems).wait()

    return _Ride([pair, red], [jax.ShapeDtypeStruct(red.shape, red.dtype)],
                 [pltpu.SemaphoreType.DMA((nc,)), pltpu.SemaphoreType.DMA((N_CHIP, nc)), pltpu.SemaphoreType.DMA],
                 start, finish, aliases={1: 0})


def _allreduce_small(pack, name):
    shape = pack.shape

    def body(x_ref, o_ref, sib_ref, chip_ref, send_sems, recv_sems):
        x, y, c = _place()
        own = 2 * x + y
        chips = [(1 - x, y), (x, 1 - y), (1 - x, 1 - y)]
        to_sibling = pltpu.make_async_remote_copy(
            src_ref=x_ref, dst_ref=sib_ref, send_sem=send_sems.at[0], recv_sem=recv_sems.at[0],
            device_id=(x, y, 1 - c), device_id_type=MESH)
        to_sibling.start()
        to_sibling.wait()
        chip_ref[own] = x_ref[...] + sib_ref[...]
        sent = [pltpu.make_async_remote_copy(
            src_ref=chip_ref.at[own], dst_ref=chip_ref.at[own], send_sem=send_sems.at[1 + j],
            recv_sem=recv_sems.at[1 + own], device_id=(*chip, c), device_id_type=MESH) for j, chip in enumerate(chips)]
        for cp in sent:
            cp.start()
        for chip in chips:
            k = 2 * chip[0] + chip[1]
            pltpu.make_async_remote_copy(
                src_ref=chip_ref.at[k], dst_ref=chip_ref.at[k], send_sem=send_sems.at[1],
                recv_sem=recv_sems.at[1 + k], device_id=(*chip, c), device_id_type=MESH).wait_recv()
        for cp in sent:
            cp.wait_send()
        o_ref[...] = (chip_ref[0] + chip_ref[1]) + (chip_ref[2] + chip_ref[3])

    return pl.pallas_call(
        body, name=name, out_shape=jax.ShapeDtypeStruct(shape, F32),
        in_specs=[pl.BlockSpec(memory_space=pltpu.VMEM)], out_specs=pl.BlockSpec(memory_space=pltpu.VMEM),
        scratch_shapes=[pltpu.VMEM(shape, F32), pltpu.VMEM((N_CHIP, *shape), F32),
                        pltpu.SemaphoreType.DMA((4,)), pltpu.SemaphoreType.DMA((1 + N_CHIP,))],
    )(pack)


def _adamw(g, w, m, v):
    m = ADAM_B1 * m + (1.0 - ADAM_B1) * g
    v = ADAM_B2 * v + (1.0 - ADAM_B2) * (g * g)
    m_hat = m / (1.0 - ADAM_B1 ** ADAM_STEP)
    v_hat = v / (1.0 - ADAM_B2 ** ADAM_STEP)
    delta = -ADAM_LR * (m_hat / (jnp.sqrt(v_hat) + ADAM_EPS) + ADAM_WD * w)
    return delta, m, v


def _adam_parts(items, name, tr=None):
    ni = len(items)
    npart, r, c = items[0][0].shape
    tr = r if tr is None else min(tr, r)

    def body(*refs):
        for a in range(ni):
            p_ref, w_ref, m_ref, v_ref = refs[4 * a:4 * a + 4]
            g_ref, d_ref, nm_ref, nv_ref = refs[4 * (ni + a):4 * (ni + a) + 4]
            g = p_ref[0].astype(F32)
            for k in range(1, npart):
                g = g + p_ref[k].astype(F32)
            g_ref[...] = g
            d_ref[...], nm_ref[...], nv_ref[...] = _adamw(g, w_ref[...], m_ref[...], v_ref[...])

    tile = pl.BlockSpec((tr, c), lambda i: (i, 0))
    outs = pl.pallas_call(
        body, name=name, grid=(r // tr,),
        in_specs=[pl.BlockSpec((npart, tr, c), lambda i: (0, i, 0)), tile, tile, tile] * ni,
        out_specs=[tile] * (4 * ni), out_shape=[jax.ShapeDtypeStruct((r, c), F32)] * (4 * ni),
        compiler_params=_params("parallel"))(*[arr for item in items for arr in item])
    return [outs[4 * a:4 * a + 4] for a in range(ni)]


def _block_diag(w):
    w4 = w.reshape(N_GROUPS, 4, RNN_BLOCK_W, RNN_BLOCK_W)
    eye = jnp.eye(4, dtype=w.dtype)
    return jnp.einsum("gbij,bc->gbicj", w4, eye).reshape(N_GROUPS, GROUP_W, GROUP_W).astype(BF16)


SMALL_ROWS = 16
ROW_PRE_G, ROW_BGATE, ROW_CONV_B, ROW_B_A, ROW_B_X, ROW_LAM, ROW_POST_G, ROW_LOSS, ROW_SINKS, ROW_CONV_W = 0, 1, 3, 4, 5, 6, 7, 8, 9, 10


def _pack_stats(st_pre, st_merge, st_rnn, st_post, st_sink, name):
    d = D_MODEL

    def body(pre_ref, mg_ref, rnn_ref, post_ref, sink_ref, o_ref):
        rnn = rnn_ref[...]
        sinks = jnp.concatenate([sink_ref[0:1, :], jnp.zeros((1, d - LANE), F32)], axis=1)
        o_ref[0:8, :] = _rows8([pre_ref[0:1, :], mg_ref[0:1, :], mg_ref[1:2, :], rnn[0:1], rnn[1:2], rnn[2:3], rnn[3:4],
                                post_ref[0:1, :]], d)
        o_ref[8:16, :] = _rows8([post_ref[1:2, :], sinks, rnn[4:5], rnn[5:6], rnn[6:7], rnn[7:8]], d)

    return pl.pallas_call(body, name=name, out_shape=jax.ShapeDtypeStruct((SMALL_ROWS, d), F32))(
        st_pre, st_merge, st_rnn, st_post, st_sink)


def _adam_small(total, w, m, v, name):
    d = D_MODEL
    n_in = len(w)
    rows = (ROW_PRE_G, ROW_BGATE, ROW_CONV_B, ROW_B_A, ROW_B_X, ROW_LAM, ROW_POST_G, ROW_SINKS)

    def grad_of(p_ref, row, shape):
        if shape == (1, 2 * d):
            return jnp.concatenate([p_ref[row:row + 1, :], p_ref[row + 1:row + 2, :]], axis=1)
        if shape == (1, RNN_BLOCKS, RNN_BLOCK_W):
            r = p_ref[row:row + 1, :]
            return jnp.concatenate([r[:, b * RNN_BLOCK_W:(b + 1) * RNN_BLOCK_W] for b in range(RNN_BLOCKS)], axis=0)[None]
        return p_ref[row:row + 1, 0:shape[1]]

    def body(*refs):
        p_ref = refs[0]
        w_refs, m_refs, v_refs = (refs[1 + k * n_in:1 + (k + 1) * n_in] for k in range(3))
        outs = refs[1 + 3 * n_in:]
        for k in range(n_in):
            g = grad_of(p_ref, rows[k], w[k].shape)
            res = (g,) + _adamw(g, w_refs[k][...], m_refs[k][...], v_refs[k][...])
            for kind in range(4):
                outs[kind * n_in + k][...] = res[kind]

    outs = pl.pallas_call(
        body, name=name, out_shape=[jax.ShapeDtypeStruct(a.shape, F32) for _ in range(4) for a in w])(total, *w, *m, *v)
    return [outs[kind * n_in:(kind + 1) * n_in] for kind in range(4)]


def kernel(x, pre_norm_g, w_in, b_gate, conv_w, conv_b, w_rg_a, b_rg_a, w_rg_x, b_rg_x, lru_lambda, attn_sinks, w_rnn_out, w_attn_out, w_out, post_norm_g, loss_target, m_pre_norm_g, m_w_in, m_b_gate, m_conv_w, m_conv_b, m_w_rg_a, m_b_rg_a, m_w_rg_x, m_b_rg_x, m_lru_lambda, m_attn_sinks, m_w_rnn_out, m_w_attn_out, m_w_out, m_post_norm_g, v_pre_norm_g, v_w_in, v_b_gate, v_conv_w, v_conv_b, v_w_rg_a, v_b_rg_a, v_w_rg_x, v_b_rg_x, v_lru_lambda, v_attn_sinks, v_w_rnn_out, v_w_attn_out, v_w_out, v_post_norm_g):
    cx, cy, cc = _place()
    dev = 4 * cx + 2 * cy + cc

    w_in_t, m_in_t, v_in_t = (jnp.transpose(a[0]) for a in (w_in, m_w_in, v_w_in))
    wt_shard = w_in_t.astype(BF16)

    def project(xs, pre_g):
        h, rx_rg, qkv, ag_ml, wt_all = _gather_project(xs, pre_g, wt_shard, "gather_project")
        return h, rx_rg, qkv, ag_ml, wt_all.reshape(D_IN, D_MODEL)

    def late_unpack(landed):
        w_rnn_all, w_attn_all, w_out_all, cw_all = landed
        return dict(w_rnn=w_rnn_all.reshape(D_RNN, D_MODEL), w_attn=w_attn_all.reshape(D_MODEL, D_MODEL),
                    w_out=w_out_all.reshape(D_MODEL, D_MODEL), cw=jnp.transpose(cw_all, (1, 0, 2)).reshape(4, D_RNN))

    late_weights = (_gather_ride([w_rnn_out[0].astype(BF16), w_attn_out[0].astype(BF16), w_out[0].astype(BF16), conv_w[0]]),
                    late_unpack)

    heads = jnp.arange(1, N_Q_HEADS + 1, dtype=F32)
    slopes = jnp.exp2(-ALIBI_MAX_BIAS * heads / N_Q_HEADS)
    b_a = b_rg_a.reshape(1, D_RNN)
    b_x = b_rg_x.reshape(1, D_RNN)
    p = dict(
        pre_g=pre_norm_g, post_g=post_norm_g, b_gate=b_gate, cb=conv_b,
        wbd_a=_block_diag(w_rg_a[0]), b_a=b_a, wbd_x=_block_diag(w_rg_x[0]), b_x=b_x, lam=lru_lambda,
        sm=jnp.pad(attn_sinks, ((0, 1), (0, 0))) + jnp.pad(slopes[None, :], ((1, 0), (0, 0))))

    flat = (RNN_BLOCKS * RNN_BLOCK_W, RNN_BLOCK_W)

    def reduce_out(out_pairs, g_rg_a, g_rg_x):
        return _join_rides(_chips_ride(out_pairs), _gather_ride([g_rg_a.reshape(flat), g_rg_x.reshape(flat)]))

    reduce_in = _chips_rest_ride

    g = _local_grads(x[0], loss_target[0], p, project, late_weights, reduce_out, reduce_in)
    small = _allreduce_small(
        _pack_stats(g["st_pre"], g["st_merge"], g["st_rnn"], g["st_post"], g["st_sink"], "pack_stats"), "allreduce_small")

    out = {}
    red = g["red_out"]
    w_in_out, = _adam_parts([(g["red_in"][0], w_in_t, m_in_t, v_in_t)], "adam_w_in", tr=SHARD_IN // 2)
    out["w_in"] = [jnp.transpose(o) for o in w_in_out]
    out["w_rnn_out"], out["w_attn_out"], out["w_out"] = _adam_parts(
        [(red[0], w_rnn_out[0], m_w_rnn_out[0], v_w_rnn_out[0]), (red[1], w_attn_out[0], m_w_attn_out[0], v_w_attn_out[0]),
         (red[2], w_out[0], m_w_out[0], v_w_out[0])], "adam_w_outs")
    out["w_rg_a"], out["w_rg_x"] = _adam_parts(
        [(red[3], w_rg_a.reshape(flat), m_w_rg_a.reshape(flat), v_w_rg_a.reshape(flat)),
         (red[4], w_rg_x.reshape(flat), m_w_rg_x.reshape(flat), v_w_rg_x.reshape(flat))], "adam_w_rg")

    small_names = ("pre_norm_g", "b_gate", "conv_b", "b_rg_a", "b_rg_x", "lru_lambda", "post_norm_g", "attn_sinks")
    small_out = _adam_small(
        small,
        (pre_norm_g, b_gate, conv_b, b_rg_a, b_rg_x, lru_lambda, post_norm_g, attn_sinks),
        (m_pre_norm_g, m_b_gate, m_conv_b, m_b_rg_a, m_b_rg_x, m_lru_lambda, m_post_norm_g, m_attn_sinks),
        (v_pre_norm_g, v_b_gate, v_conv_b, v_b_rg_a, v_b_rg_x, v_lru_lambda, v_post_norm_g, v_attn_sinks),
        "adam_small")
    g_cw = lax.dynamic_slice(small[ROW_CONV_W:ROW_CONV_W + 4], (0, dev * SHARD_OUT), (4, SHARD_OUT))
    out["conv_w"], = _adam_parts([(g_cw[None], conv_w[0], m_conv_w[0], v_conv_w[0])], "adam_conv_w")

    shapes = dict(w_in=(1, D_MODEL, SHARD_IN), w_rnn_out=(1, SHARD_OUT, D_MODEL), w_attn_out=(1, SHARD_OUT, D_MODEL),
                  w_out=(1, SHARD_OUT, D_MODEL), w_rg_a=(1, RNN_BLOCKS, RNN_BLOCK_W, RNN_BLOCK_W),
                  w_rg_x=(1, RNN_BLOCKS, RNN_BLOCK_W, RNN_BLOCK_W), conv_w=(1, 4, SHARD_OUT))
    weights = ["pre_norm_g", "w_in", "b_gate", "conv_w", "conv_b", "w_rg_a", "b_rg_a", "w_rg_x", "b_rg_x",
               "lru_lambda", "attn_sinks", "w_rnn_out", "w_attn_out", "w_out", "post_norm_g"]
    results = []
    for kind in range(4):
        for name in weights:
            if name in out:
                results.append(out[name][kind].reshape(shapes[name]))
            else:
                results.append(small_out[kind][small_names.index(name)])
    loss = 0.5 / D_MODEL * jnp.sum(small[ROW_LOSS])
    return (loss, g["grad_x"][None], *results)
```

```python
import jax
import jax.numpy as jnp
from jax import lax
from jax.experimental import pallas as pl
from jax.experimental.pallas import tpu as pltpu

F32 = jnp.float32
BF16 = jnp.bfloat16

D_MODEL = 1024
D_RNN = 1024
RNN_BLOCKS = 16
RNN_BLOCK_W = 64
LRU_C = 8.0
N_Q_HEADS = 16
HEAD_DIM = 64
D_KV = 256
BLOCK = 128
ALIBI_MAX_BIAS = 8.0
EPS = 1e-6
D_IN = 6656
N_DEV = 8
N_CHIP = 4
SHARD_IN = D_IN // N_DEV
SHARD_OUT = D_MODEL // N_DEV
ATTN_SCALE = HEAD_DIM ** -0.5
MASKED = -1e30

ADAM_LR = 0.001
ADAM_B1 = 0.9
ADAM_B2 = 0.999
ADAM_EPS = 1e-08
ADAM_WD = 0.01
ADAM_STEP = 10

VMEM_LIMIT_BYTES = 52 * 1024 * 1024
LANE = 128
GROUP_W = 256
N_GROUPS = D_RNN // GROUP_W
SEG_CHUNK = 512

NT_DIMS = (((1,), (1,)), ((), ()))
TN_DIMS = (((0,), (0,)), ((), ()))
MESH = pl.DeviceIdType.MESH
ANY = pl.BlockSpec(memory_space=pl.ANY)


def _params(*semantics):
    return pltpu.CompilerParams(dimension_semantics=semantics, vmem_limit_bytes=VMEM_LIMIT_BYTES)


def _sigmoid(x):
    return 0.5 * jnp.tanh(0.5 * x) + 0.5


def _log1p(e):
    u = 1.0 + e
    den = jnp.where(u == 1.0, 1.0, u - 1.0)
    return jnp.where(u == 1.0, e, jnp.log(u) * (e / den))


def _softplus(z):
    return jnp.maximum(z, 0.0) + _log1p(jnp.exp(-jnp.abs(z)))


def _rows8(rows, width):
    idx = lax.broadcasted_iota(jnp.int32, (8, width), 0)
    out = jnp.zeros((8, width), F32)
    for r, v in enumerate(rows):
        out = jnp.where(idx == r, v, out)
    return out


class _Ride:
    def __init__(self, arrays, out_shapes, scratch_shapes, start, finish, middle=None, aliases=None):
        self.arrays, self.out_shapes, self.scratch_shapes = list(arrays), list(out_shapes), list(scratch_shapes)
        self.start, self.finish, self.middle = start, finish, middle
        self.aliases = dict(aliases or {})


class _Hosted:
    def __init__(self, ride, n_in, n_out, n_scratch=0, aliasing=False):
        self.ride = ride
        assert aliasing or not (ride and ride.aliases), "this host does not alias"
        self.aliases = {n_in + i: n_out + o for i, o in ride.aliases.items()} if ride else {}
        self.sizes = (n_in, len(ride.arrays) if ride else 0, n_out, len(ride.out_shapes) if ride else 0, n_scratch)
        self.arrays = ride.arrays if ride else []
        self.in_specs = [ANY] * len(self.arrays)
        self.out_shapes = ride.out_shapes if ride else []
        self.out_specs = [ANY] * len(self.out_shapes)
        self.scratch_shapes = ride.scratch_shapes if ride else []

    def split(self, refs):
        n_in, r_in, n_out, r_out, n_scr = self.sizes
        cuts = [0, n_in, n_in + r_in, n_in + r_in + n_out, n_in + r_in + n_out + r_out, n_in + r_in + n_out + r_out + n_scr]
        host_in, ride_in, host_out, ride_out, host_scr = (refs[cuts[k]:cuts[k + 1]] for k in range(5))
        ride_scr = refs[cuts[5]:]

        def start(when):
            if self.ride is not None:
                pl.when(when)(lambda: self.ride.start(ride_in, ride_out, ride_scr))

        def finish(when):
            if self.ride is not None:
                pl.when(when)(lambda: self.ride.finish(ride_in, ride_out, ride_scr))

        def middle(when):
            if self.ride is not None and self.ride.middle is not None:
                pl.when(when)(lambda: self.ride.middle(ride_in, ride_out, ride_scr))

        start.middle = middle
        return tuple(host_in) + tuple(host_out) + tuple(host_scr), start, finish

    def results(self, outs, n_out):
        outs = list(outs) if isinstance(outs, (list, tuple)) else [outs]
        return outs[:n_out], outs[n_out:]


def _join_rides(a, b):
    na, nb_ = len(a.arrays), len(b.arrays)
    oa = len(a.out_shapes)
    sa = len(a.scratch_shapes)

    def both(fa, fb):
        def run(ins, outs, sems):
            if fa is not None:
                fa(ins[:na], outs[:oa], sems[:sa])
            if fb is not None:
                fb(ins[na:na + nb_], outs[oa:], sems[sa:])
        return run

    middle = both(a.middle, b.middle) if (a.middle or b.middle) else None
    return _Ride(a.arrays + b.arrays, a.out_shapes + b.out_shapes, a.scratch_shapes + b.scratch_shapes,
                 both(a.start, b.start), both(a.finish, b.finish), middle)


def _load_resident(w_hbm, w_vmem, sems, first):
    def piece(c):
        rows = pl.ds(c * SEG_CHUNK, SEG_CHUNK)
        return pltpu.make_async_copy(w_hbm.at[rows], w_vmem.at[rows], sems.at[c])

    @pl.when(first)
    def _():
        for c in range(w_vmem.shape[0] // SEG_CHUNK):
            piece(c).start()

    def ready(c):
        @pl.when(first)
        def _():
            piece(c).wait()

    return ready


CHIP_ROWS = 2 * SHARD_IN
PROJ_WIDTHS = (2 * D_RNN, D_MODEL + 2 * D_KV, 3 * D_MODEL)
PROJ_DTYPES = (F32, BF16, BF16)


def _chip_pieces():
    starts = [0, PROJ_WIDTHS[0], PROJ_WIDTHS[0] + PROJ_WIDTHS[1], D_IN]
    pieces = []
    for k in range(N_CHIP):
        lo, hi = k * CHIP_ROWS, (k + 1) * CHIP_ROWS
        cur = []
        for a in range(len(PROJ_WIDTHS)):
            s0, s1 = max(lo, starts[a]), min(hi, starts[a + 1])
            if s0 < s1:
                cur.append((a, s0 - starts[a], s1 - s0, s0 - lo))
        pieces.append(cur)
    return pieces


def _gather_project(x, g, wt_shard, name, tm=1024):
    t, k = x.shape
    tm = min(tm, t)
    nt = t // tm
    pieces = _chip_pieces()

    def body(x_ref, g_ref, shard_ref, h_out, rx_ref, qkv_ref, ag_ref, wt_all,
             w_c, stage, o32, o16, h_all, send_sems, recv_sems, local_sem, stage_sems, out_sems, h_sems):
        s, ti = pl.program_id(0), pl.program_id(1)
        px, py, pc = _place()
        me, sibling = (px, py, pc), (px, py, 1 - pc)
        chips = [(px, py), (1 - px, py), (px, 1 - py), (1 - px, 1 - py)]
        outs = (rx_ref, qkv_ref, ag_ref)

        def slot(dev):
            return wt_all.at[4 * dev[0] + 2 * dev[1] + dev[2]]

        def copy(kk, block, to, src=None):
            return pltpu.make_async_remote_copy(
                src_ref=slot(block) if src is None else src, dst_ref=slot(block),
                send_sem=send_sems.at[kk], recv_sem=recv_sems.at[kk], device_id=to, device_id_type=MESH)

        mine = pltpu.make_async_copy(shard_ref, slot(me), local_sem)
        first = [copy(0, me, sibling, src=shard_ref)] + [copy(1 + j, me, (*chips[1 + j], pc), src=shard_ref) for j in range(3)]
        passed = [copy(4 + j, (*chips[1 + j], pc), sibling) for j in range(3)]

        @pl.when((s == 0) & (ti == 0))
        def _():
            mine.start()
            for cp in first[:3]:
                cp.start()

        def pass_on(step):
            copy(step, (*chips[step], pc), me).wait_recv()
            passed[step - 1].start()

        @pl.when((s == 2) & (ti == nt - 1))
        def _():
            pass_on(3)

        for step in range(N_CHIP):
            @pl.when((s == step) & (ti == 0))
            def _(step=step):
                chip = chips[step]
                if step == 0:
                    mine.wait()
                    copy(0, sibling, me).wait_recv()
                else:
                    if step == 1:
                        pass_on(1)
                        first[3].start()
                        pass_on(2)
                    copy(3 + step, (*chip, 1 - pc), me).wait_recv()
                loads = [pltpu.make_async_copy(slot((*chip, core)), stage.at[pl.ds(core * SHARD_IN, SHARD_IN)], stage_sems.at[core])
                         for core in (0, 1)]
                for cp in loads:
                    cp.start()
                for cp in loads:
                    cp.wait()
                w_c[...] = stage[...].T

        rows = pl.ds(pl.multiple_of(ti * tm, tm), tm)

        @pl.when(s == 0)
        def _():
            xv = x_ref[...]
            h_all[rows, :] = (xv * lax.rsqrt(jnp.mean(xv * xv, axis=-1, keepdims=True) + EPS) * g_ref[...]).astype(BF16)

        res = jnp.dot(h_all[rows, :], w_c[...], preferred_element_type=F32)

        n = s * nt + ti
        buf = lax.rem(n, 2)
        chip_idx = [2 * cx + cy for cx, cy in chips]

        def chip_at(step):
            return jnp.where(step == 0, chip_idx[0], jnp.where(step == 1, chip_idx[1], jnp.where(step == 2, chip_idx[2], chip_idx[3])))

        def h_write(b, tile):
            return pltpu.make_async_copy(h_all.at[pl.ds(tile * tm, tm)], h_out.at[pl.ds(tile * tm, tm)], h_sems.at[b])

        def writes(kchip, b, tile):
            cps = []
            for idx, (a, col, w, src) in enumerate(pieces[kchip]):
                staged = (o16 if PROJ_DTYPES[a] == BF16 else o32).at[b, :, pl.ds(src, w)]
                cps.append(pltpu.make_async_copy(staged, outs[a].at[pl.ds(tile * tm, tm), pl.ds(col, w)], out_sems.at[b, idx]))
            return cps

        o32[buf] = res
        o16[buf] = res.astype(BF16)
        kcur = chip_at(s)

        @pl.when(n > 0)
        def _():
            kprev = chip_at(lax.div(n - 1, nt))
            for kchip in range(N_CHIP):
                @pl.when(kprev == kchip)
                def _(kchip=kchip):
                    for cp in writes(kchip, 1 - buf, lax.rem(n - 1, nt)):
                        cp.wait()

            @pl.when(n <= nt)
            def _():
                h_write(1 - buf, n - 1).wait()

        @pl.when(s == 0)
        def _():
            h_write(buf, ti).start()

        for kchip in range(N_CHIP):
            @pl.when(kcur == kchip)
            def _(kchip=kchip):
                for cp in writes(kchip, buf, ti):
                    cp.start()

        @pl.when(n == N_CHIP * nt - 1)
        def _():
            for kchip in range(N_CHIP):
                @pl.when(kcur == kchip)
                def _(kchip=kchip):
                    for cp in writes(kchip, buf, ti):
                        cp.wait()
            for cp in first + passed:
                cp.wait_send()

    tile = pl.BlockSpec((tm, k), lambda s, ti: (jnp.where(s == 0, ti, nt - 1), 0))
    return pl.pallas_call(
        body, name=name, grid=(N_CHIP, nt),
        in_specs=[tile, pl.BlockSpec((1, k), lambda s, ti: (0, 0)), ANY],
        out_specs=[ANY, ANY, ANY, ANY, ANY],
        out_shape=[jax.ShapeDtypeStruct((t, k), BF16)]
        + [jax.ShapeDtypeStruct((t, w), dt) for w, dt in zip(PROJ_WIDTHS, PROJ_DTYPES)]
        + [jax.ShapeDtypeStruct((N_DEV, SHARD_IN, k), BF16)],
        scratch_shapes=[pltpu.VMEM((k, CHIP_ROWS), BF16), pltpu.VMEM((CHIP_ROWS, k), BF16),
                        pltpu.VMEM((2, tm, CHIP_ROWS), F32), pltpu.VMEM((2, tm, CHIP_ROWS), BF16), pltpu.VMEM((t, k), BF16),
                        pltpu.SemaphoreType.DMA((7,)), pltpu.SemaphoreType.DMA((7,)), pltpu.SemaphoreType.DMA,
                        pltpu.SemaphoreType.DMA((2,)), pltpu.SemaphoreType.DMA((2, 2)), pltpu.SemaphoreType.DMA((2,))],
        compiler_params=_params("arbitrary", "arbitrary"))(x, g, wt_shard)


def _mm_tn_pairs(prods, name):
    n_prod = len(prods)
    ktok, m = prods[0][0].shape
    n = prods[0][1].shape[1]
    half, blk = m // 2, m // N_DEV
    ns = 2 * n_prod
    kept = N_CHIP * blk

    def side_pieces(s, side):
        i, hf = divmod(s, 2)
        first = hf * N_DEV // 2
        return [((d - first) * blk, i * kept + (d // 2) * blk) for d in range(first, first + N_DEV // 2) if d % 2 == side]

    def body(*refs):
        a_hbm, b_hbm, pair_refs = refs[:n_prod], refs[n_prod:2 * n_prod], refs[2 * n_prod:3 * n_prod]
        a_buf, b_buf, res_buf, recv_all, pair_all, a_sems, b_sems, send_sems, recv_sems, out_sems = refs[3 * n_prod:]
        step = pl.program_id(0)
        px, py, pc = _place()

        def fetch_a(s):
            return pltpu.make_async_copy(a_hbm[s // 2].at[:, pl.ds((s % 2) * half, half)], a_buf.at[s % 2], a_sems.at[s % 2])

        def fetch_b(i):
            return pltpu.make_async_copy(b_hbm[i], b_buf.at[i % 2], b_sems.at[i % 2])

        @pl.when(step == 0)
        def _():
            fetch_a(0).start()
            fetch_b(0).start()

        for s in range(ns):
            @pl.when(step == s)
            def _(s=s):
                if s + 1 < ns:
                    fetch_a(s + 1).start()
                if s % 2 == 0 and s // 2 + 1 < n_prod:
                    fetch_b(s // 2 + 1).start()
                fetch_a(s).wait()
                if s % 2 == 0:
                    fetch_b(s // 2).wait()

        res_buf[step % 2] = lax.dot_general(a_buf[step % 2], b_buf[(step // 2) % 2], TN_DIMS, preferred_element_type=F32)

        def crossing(s, j, piece):
            off, at = piece
            return pltpu.make_async_remote_copy(
                src_ref=res_buf.at[s % 2, pl.ds(off, blk)], dst_ref=recv_all.at[pl.ds(at, blk)],
                send_sem=send_sems.at[s, j], recv_sem=recv_sems.at[s, j], device_id=(px, py, 1 - pc), device_id_type=MESH)

        def write(s, core):
            at = side_pieces(s, core)[0][1]
            return pltpu.make_async_copy(pair_all.at[pl.ds(at, 2 * blk)], pair_refs[s // 2].at[pl.ds(at % kept, 2 * blk)], out_sems.at[s])

        for core in (0, 1):
            def settle(s, core=core):
                for j, piece in enumerate(side_pieces(s, 1 - core)):
                    crossing(s, j, piece).wait_send()
                for j, (off, at) in enumerate(side_pieces(s, core)):
                    crossing(s, j, (off, at)).wait_recv()
                    pair_all[at:at + blk] = (res_buf[s % 2, off:off + blk] + recv_all[at:at + blk]).astype(BF16)
                write(s, core).start()

            for s in range(ns):
                @pl.when((pc == core) & (step == s))
                def _(s=s, settle=settle, core=core):
                    for j, piece in enumerate(side_pieces(s, 1 - core)):
                        crossing(s, j, piece).start()
                    if s > 0:
                        settle(s - 1)
                    if s == ns - 1:
                        settle(s)
                        for t in range(ns):
                            write(t, core).wait()

    pairs = pl.pallas_call(
        body, name=name, grid=(ns,),
        in_specs=[ANY] * (2 * n_prod), out_specs=[ANY] * n_prod,
        out_shape=[jax.ShapeDtypeStruct((kept, n), BF16)] * n_prod,
        scratch_shapes=[pltpu.VMEM((2, ktok, half), prods[0][0].dtype), pltpu.VMEM((2, ktok, n), prods[0][1].dtype),
                        pltpu.VMEM((2, half, n), F32), pltpu.VMEM((n_prod * kept, n), F32), pltpu.VMEM((n_prod * kept, n), BF16),
                        pltpu.SemaphoreType.DMA((2,)), pltpu.SemaphoreType.DMA((2,)),
                        pltpu.SemaphoreType.DMA((ns, 2)), pltpu.SemaphoreType.DMA((ns, 2)), pltpu.SemaphoreType.DMA((ns,))],
        compiler_params=_params("arbitrary"))(*[a for a, _ in prods], *[b for _, b in prods])
    return [pair.reshape(N_CHIP, blk, n) for pair in pairs]


def _segment_chunks(segs):
    bounds = [0]
    for s in segs:
        bounds.append(bounds[-1] + s.shape[1] // SEG_CHUNK)
    return bounds


def _input_grad(segs, wt, x, g, dy, name, tm=512, ride=None):
    m = segs[0].shape[0]
    rows, n = wt.shape
    tm = min(tm, m)
    bounds = _segment_chunks(segs)
    n_seg = len(segs)
    ni = m // tm
    host = _Hosted(ride, n_seg + 4, 2, 2, aliasing=True)

    def body(*refs):
        host_refs, start, finish = host.split(refs)
        a_refs = host_refs[:n_seg]
        wt_hbm, x_ref, g_ref, dy_ref, gx_ref, st_ref, wt_vmem, sems = host_refs[n_seg:]
        i = pl.program_id(0)
        start(i == 0)

        @pl.when(i == 0)
        def _():
            st_ref[...] = jnp.zeros_like(st_ref)

        ready = _load_resident(wt_hbm, wt_vmem, sems, i == 0)
        dh = None
        for s in range(n_seg):
            for c in range(bounds[s], bounds[s + 1]):
                ready(c)
            part = jnp.dot(a_refs[s][...], wt_vmem[bounds[s] * SEG_CHUNK:bounds[s + 1] * SEG_CHUNK, :], preferred_element_type=F32)
            dh = part if dh is None else dh + part
        xv = x_ref[...]
        r = lax.rsqrt(jnp.mean(xv * xv, axis=-1, keepdims=True) + EPS)
        xn = xv * r
        dxn = dh * g_ref[...]
        gx_ref[...] = dy_ref[...] + r * (dxn - xn * jnp.mean(dxn * xn, axis=-1, keepdims=True))
        st_ref[...] += _rows8([jnp.sum(dh * xn, axis=0, keepdims=True)], n)
        finish(i == ni - 1)

    tile = pl.BlockSpec((tm, n), lambda i: (i, 0))
    outs = pl.pallas_call(
        body, name=name, grid=(ni,),
        in_specs=[pl.BlockSpec((tm, sg.shape[1]), lambda i: (i, 0)) for sg in segs]
        + [ANY, tile, pl.BlockSpec((1, n), lambda i: (0, 0)), tile] + host.in_specs,
        out_specs=[tile, pl.BlockSpec((8, n), lambda i: (0, 0))] + host.out_specs,
        out_shape=[jax.ShapeDtypeStruct((m, n), F32), jax.ShapeDtypeStruct((8, n), F32)] + host.out_shapes,
        scratch_shapes=[pltpu.VMEM((rows, n), wt.dtype), pltpu.SemaphoreType.DMA((rows // SEG_CHUNK,))] + host.scratch_shapes,
        input_output_aliases=host.aliases,
        compiler_params=_params("arbitrary"))(*segs, wt, x, g, dy, *host.arrays)
    res, landed = host.results(outs, 2)
    return (*res, landed) if ride else tuple(res)


CHUNK_ORDER = (0, 4, 7, 10, 1, 5, 8, 11, 2, 6, 9, 12, 3)
EARLY_STEPS = 8


def _side_pieces():
    table = []
    for core in (0, 1):
        sides = ([None] * len(CHUNK_ORDER), [None] * len(CHUNK_ORDER))
        for s, cc in enumerate(CHUNK_ORDER):
            g0 = cc * SEG_CHUNK
            for d in range(N_DEV):
                lo, hi = max(g0, d * SHARD_IN), min(g0 + SEG_CHUNK, (d + 1) * SHARD_IN)
                if lo < hi:
                    side = sides[0 if d % 2 == core else 1]
                    assert side[s] is None
                    side[s] = (lo - g0, hi - lo, (d // 2) * SHARD_IN + lo - d * SHARD_IN)
        table.append(sides)
    return table


def _to_chip(src, red_ref, piece, s, send_sems, recv_sems, own, core):
    _, rows, at = piece
    k, r0 = divmod(at, SHARD_IN)
    return k, pltpu.make_async_remote_copy(
        src_ref=src, dst_ref=red_ref.at[own, pl.ds(r0, rows)], send_sem=send_sems.at[s], recv_sem=recv_sems.at[own, s],
        device_id=(k // 2, k % 2, core), device_id_type=MESH)


def _from_chip(src, red_ref, piece, s, send_sems, recv_sems, chip, me):
    _, rows, at = piece
    return pltpu.make_async_remote_copy(
        src_ref=src, dst_ref=red_ref.at[chip, pl.ds(at % SHARD_IN, rows)], send_sem=send_sems.at[s], recv_sem=recv_sems.at[chip, s],
        device_id=me, device_id_type=MESH)


def _mm_tn_seg_pair(segs, b, name):
    ktok = segs[0].shape[0]
    n = b.shape[1]
    bounds = _segment_chunks(segs)
    n_seg = len(segs)
    nc = bounds[-1]
    assert nc == len(CHUNK_ORDER)
    seg_of = [s for s in range(n_seg) for _ in range(bounds[s], bounds[s + 1])]
    table = _side_pieces()

    def body(*refs):
        a_hbm, b_hbm, pair_ref, red_ref = refs[:n_seg], refs[n_seg], refs[n_seg + 1], refs[n_seg + 2]
        (a_buf, b_vmem, res_buf, recv_all, pair_all, a_sems, b_sem, send_sems, recv_sems, out_sems,
         chip_send, chip_recv) = refs[n_seg + 3:]
        step = pl.program_id(0)
        px, py, pc = _place()
        own = 2 * px + py

        def fetch(s):
            sg = seg_of[CHUNK_ORDER[s]]
            cols = pl.ds((CHUNK_ORDER[s] - bounds[sg]) * SEG_CHUNK, SEG_CHUNK)
            return pltpu.make_async_copy(a_hbm[sg].at[:, cols], a_buf.at[s % 2], a_sems.at[s % 2])

        @pl.when(step == 0)
        def _():
            whole = pltpu.make_async_copy(b_hbm, b_vmem, b_sem)
            whole.start()
            fetch(0).start()
            whole.wait()

        for s in range(nc):
            @pl.when(step == s)
            def _(s=s):
                if s + 1 < nc:
                    fetch(s + 1).start()
                fetch(s).wait()

        res_buf[step % 2] = lax.dot_general(a_buf[step % 2], b_vmem[...], TN_DIMS, preferred_element_type=F32)

        def crossing(s, piece):
            off, rows, at = piece
            return pltpu.make_async_remote_copy(
                src_ref=res_buf.at[s % 2, pl.ds(off, rows)], dst_ref=recv_all.at[pl.ds(at, rows)],
                send_sem=send_sems.at[s], recv_sem=recv_sems.at[s], device_id=(px, py, 1 - pc), device_id_type=MESH)

        def write(s, piece):
            _, rows, at = piece
            return pltpu.make_async_copy(pair_all.at[pl.ds(at, rows)], pair_ref.at[pl.ds(at, rows)], out_sems.at[s])

        def to_chip(s, piece, core):
            return _to_chip(pair_all.at[pl.ds(piece[2], piece[1])], red_ref, piece, s, chip_send, chip_recv, own, core)

        for core in (0, 1):
            mine, theirs = table[core]

            def settle(s, mine=mine, theirs=theirs, core=core):
                if theirs[s] is not None:
                    crossing(s, theirs[s]).wait_send()
                if mine[s] is not None:
                    off, rows, at = mine[s]
                    crossing(s, mine[s]).wait_recv()
                    pair_all[at:at + rows] = (res_buf[s % 2, off:off + rows] + recv_all[at:at + rows]).astype(BF16)
                    write(s, mine[s]).start()
                    if s < EARLY_STEPS:
                        k, cp = to_chip(s, mine[s], core)
                        pl.when(own != k)(cp.start)

            for s in range(nc):
                @pl.when((pc == core) & (step == s))
                def _(s=s, settle=settle, mine=mine, theirs=theirs, core=core):
                    if theirs[s] is not None:
                        crossing(s, theirs[s]).start()
                    if s > 0:
                        settle(s - 1)
                    if s == nc - 1:
                        settle(s)
                        kept = [t for t in range(nc) if mine[t] is not None]
                        for t in kept:
                            write(t, mine[t]).wait()
                        early = [t for t in kept if t < EARLY_STEPS]
                        for t in early:
                            k, cp = to_chip(t, mine[t], core)
                            pl.when(own != k)(cp.wait_send)
                        for k in range(N_CHIP):
                            @pl.when(own == k)
                            def _(k=k):
                                for t in early:
                                    if mine[t][2] // SHARD_IN == k:
                                        for chip in range(N_CHIP):
                                            if chip != k:
                                                _from_chip(pair_all.at[pl.ds(mine[t][2], mine[t][1])], red_ref, mine[t], t,
                                                           chip_send, chip_recv, chip, (px, py, pc)).wait_recv()

    flat = jax.ShapeDtypeStruct((N_CHIP * SHARD_IN, n), BF16)
    pair, red = pl.pallas_call(
        body, name=name, grid=(nc,),
        in_specs=[ANY] * (n_seg + 1), out_specs=[ANY, ANY],
        out_shape=[flat, jax.ShapeDtypeStruct((N_CHIP, SHARD_IN, n), BF16)],
        scratch_shapes=[pltpu.VMEM((2, ktok, SEG_CHUNK), segs[0].dtype), pltpu.VMEM((ktok, n), b.dtype),
                        pltpu.VMEM((2, SEG_CHUNK, n), F32), pltpu.VMEM(flat.shape, F32), pltpu.VMEM(flat.shape, BF16),
                        pltpu.SemaphoreType.DMA((2,)), pltpu.SemaphoreType.DMA,
                        pltpu.SemaphoreType.DMA((nc,)), pltpu.SemaphoreType.DMA((nc,)), pltpu.SemaphoreType.DMA((nc,)),
                        pltpu.SemaphoreType.DMA((nc,)), pltpu.SemaphoreType.DMA((N_CHIP, nc))],
        compiler_params=_params("arbitrary"))(*segs, b)
    return pair.reshape(N_CHIP, SHARD_IN, n), red


def _branches_fwd(z_rnn, z_attn, ag_ml, b_gate, w_rnn, w_attn, w_out, x, target, g_post, name, tm=512):
    t, d = x.shape
    tm = min(tm, t)

    def body(zr_ref, za_ref, lr_ref, la_ref, br_ref, ba_ref, wr_ref, wa_ref, wo_ref, x_ref, t_ref, g_ref,
             brr_ref, bra_ref, mg_ref, do_ref, dy_ref, st_ref):
        @pl.when(pl.program_id(0) == 0)
        def _():
            st_ref[...] = jnp.zeros_like(st_ref)

        br_rnn = jnp.dot(zr_ref[...], wr_ref[...], preferred_element_type=F32)
        br_attn = jnp.dot(za_ref[...], wa_ref[...], preferred_element_type=F32)
        brr_ref[...] = br_rnn.astype(BF16)
        bra_ref[...] = br_attn.astype(BF16)
        g_rnn = _sigmoid(lr_ref[...].astype(F32) + br_ref[...])
        g_attn = _sigmoid(la_ref[...].astype(F32) + ba_ref[...])
        merged = (g_rnn * br_rnn + g_attn * br_attn).astype(BF16)
        mg_ref[...] = merged
        o = jnp.dot(merged, wo_ref[...], preferred_element_type=F32)
        g = g_ref[...]
        r = lax.rsqrt(jnp.mean(o * o, axis=-1, keepdims=True) + EPS)
        nrm = o * r
        err = x_ref[...] + nrm * g - t_ref[...]
        dy = err * (1.0 / d)
        dy_ref[...] = dy
        dn = dy * g
        do_ref[...] = (r * (dn - nrm * jnp.mean(dn * nrm, axis=-1, keepdims=True))).astype(BF16)
        st_ref[...] += _rows8([jnp.sum(dy * nrm, axis=0, keepdims=True), jnp.sum(err * err, axis=0, keepdims=True)], d)

    tile = pl.BlockSpec((tm, d), lambda i: (i, 0))
    weight = pl.BlockSpec((d, d), lambda i: (0, 0))
    bf = jax.ShapeDtypeStruct((t, d), BF16)
    return pl.pallas_call(
        body, name=name, grid=(t // tm,),
        in_specs=[tile, tile, pl.BlockSpec((tm, d), lambda i: (i, 1)), pl.BlockSpec((tm, d), lambda i: (i, 2)),
                  pl.BlockSpec((1, d), lambda i: (0, 0)), pl.BlockSpec((1, d), lambda i: (0, 1)),
                  weight, weight, weight, tile, tile, pl.BlockSpec((1, d), lambda i: (0, 0))],
        out_specs=[tile, tile, tile, tile, tile, pl.BlockSpec((8, d), lambda i: (0, 0))],
        out_shape=[bf, bf, bf, bf, jax.ShapeDtypeStruct((t, d), F32), jax.ShapeDtypeStruct((8, d), F32)],
        compiler_params=_params("arbitrary"))(z_rnn, z_attn, ag_ml, ag_ml, b_gate, b_gate, w_rnn, w_attn, w_out, x, target, g_post)


def _branches_bwd(dout, br_rnn, br_attn, ag_ml, b_gate, w_rnn, w_attn, w_out, name, tm=512):
    t, d = br_rnn.shape
    tm = min(tm, t)

    def body(do_ref, r_ref, a_ref, lr_ref, la_ref, br_ref, ba_ref, wr_ref, wa_ref, wo_ref,
             dr_ref, da_ref, dl_ref, dzr_ref, dza_ref, st_ref, wt_ref):
        @pl.when(pl.program_id(0) == 0)
        def _():
            st_ref[...] = jnp.zeros_like(st_ref)
            wt_ref[0] = wo_ref[...].T
            wt_ref[1] = wr_ref[...].T
            wt_ref[2] = wa_ref[...].T

        dm = jnp.dot(do_ref[...], wt_ref[0], preferred_element_type=F32)
        g_rnn = _sigmoid(lr_ref[...].astype(F32) + br_ref[...])
        g_attn = _sigmoid(la_ref[...].astype(F32) + ba_ref[...])
        dbr_rnn = (dm * g_rnn).astype(BF16)
        dbr_attn = (dm * g_attn).astype(BF16)
        dr_ref[...] = dbr_rnn
        da_ref[...] = dbr_attn
        dl_rnn = dm * r_ref[...].astype(F32) * g_rnn * (1.0 - g_rnn)
        dl_attn = dm * a_ref[...].astype(F32) * g_attn * (1.0 - g_attn)
        dl_ref[:, 0:d] = dl_rnn.astype(BF16)
        dl_ref[:, d:2 * d] = dl_attn.astype(BF16)
        st_ref[...] += _rows8([jnp.sum(dl_rnn, axis=0, keepdims=True), jnp.sum(dl_attn, axis=0, keepdims=True)], d)
        dzr_ref[...] = jnp.dot(dbr_rnn, wt_ref[1], preferred_element_type=F32).astype(BF16)
        dza_ref[...] = jnp.dot(dbr_attn, wt_ref[2], preferred_element_type=F32).astype(BF16)

    tile = pl.BlockSpec((tm, d), lambda i: (i, 0))
    weight = pl.BlockSpec((d, d), lambda i: (0, 0))
    bf = jax.ShapeDtypeStruct((t, d), BF16)
    return pl.pallas_call(
        body, name=name, grid=(t // tm,),
        in_specs=[tile, tile, tile, pl.BlockSpec((tm, d), lambda i: (i, 1)), pl.BlockSpec((tm, d), lambda i: (i, 2)),
                  pl.BlockSpec((1, d), lambda i: (0, 0)), pl.BlockSpec((1, d), lambda i: (0, 1)), weight, weight, weight],
        out_specs=[tile, tile, pl.BlockSpec((tm, 2 * d), lambda i: (i, 0)), tile, tile, pl.BlockSpec((8, d), lambda i: (0, 0))],
        out_shape=[bf, bf, jax.ShapeDtypeStruct((t, 2 * d), BF16), bf, bf, jax.ShapeDtypeStruct((8, d), F32)],
        scratch_shapes=[pltpu.VMEM((3, d, d), BF16)],
        compiler_params=_params("arbitrary"))(dout, br_rnn, br_attn, ag_ml, ag_ml, b_gate, b_gate, w_rnn, w_attn, w_out)


def _lru_decay(r, sp):
    log_a = (-LRU_C) * r * sp
    return log_a, jnp.exp(log_a)


def _lru_gates(c, wa, ba, wx, bx, sp):
    cb = c.astype(BF16)
    r = _sigmoid(jnp.dot(cb, wa, preferred_element_type=F32) + ba)
    ig = _sigmoid(jnp.dot(cb, wx, preferred_element_type=F32) + bx)
    log_a, a = _lru_decay(r, sp)
    mult = jnp.sqrt(-jnp.tanh(log_a) * (a * a + 1.0))
    return cb, r, ig, a, mult


SUBLANES = 8


def _scan_fwd(a, u, carry, tt):
    w = a.shape[1]
    ng = tt // SUBLANES
    a3 = a.reshape(ng, SUBLANES, w)
    u3 = u.reshape(ng, SUBLANES, w)
    sub = lax.broadcasted_iota(jnp.int32, (ng, SUBLANES, w), 1)
    d = 1
    while d < SUBLANES:
        keep = sub >= d
        u3 = u3 + a3 * jnp.where(keep, pltpu.roll(u3, d, 1), 0.0)
        a3 = a3 * jnp.where(keep, pltpu.roll(a3, d, 1), 1.0)
        d *= 2
    out = []
    for g in range(ng):
        hg = u3[g] + a3[g] * carry
        out.append(hg)
        carry = hg[SUBLANES - 1:SUBLANES, :]
    return jnp.concatenate(out, axis=0)


def _scan_rev(b, g, carry, tt):
    w = b.shape[1]
    ng = tt // SUBLANES
    b3 = b.reshape(ng, SUBLANES, w)
    g3 = g.reshape(ng, SUBLANES, w)
    sub = lax.broadcasted_iota(jnp.int32, (ng, SUBLANES, w), 1)
    d = 1
    while d < SUBLANES:
        keep = sub < SUBLANES - d
        g3 = g3 + b3 * jnp.where(keep, pltpu.roll(g3, SUBLANES - d, 1), 0.0)
        b3 = b3 * jnp.where(keep, pltpu.roll(b3, SUBLANES - d, 1), 1.0)
        d *= 2
    out = [None] * ng
    for k in range(ng - 1, -1, -1):
        hk = g3[k] + b3[k] * carry
        out[k] = hk
        carry = hk[0:1, :]
    return jnp.concatenate(out, axis=0)


def _conv_taps(cw, bias, x, ext_ref, tt):
    x2 = ext_ref[7:7 + tt, :]
    x1 = ext_ref[6:6 + tt, :]
    x0 = ext_ref[5:5 + tt, :]
    return bias + cw[3:4] * x + cw[2:3] * x2 + cw[1:2] * x1 + cw[0:1] * x0


def _rnn_fwd(rx_rg, cw, cb, wa, ba, wx, bx, lam, name, tt=512):
    t = rx_rg.shape[0]
    tt = min(tt, t)
    w = GROUP_W

    def body(rx_ref, rg_ref, cw_ref, cb_ref, wa_ref, ba_ref, wx_ref, bx_ref, lam_ref,
             y_ref, z_ref, c_ref, r_ref, ig_ref, mult_ref, ext_ref, hc_ref):
        @pl.when(pl.program_id(1) == 0)
        def _():
            ext_ref[0:8, :] = jnp.zeros((8, w), F32)
            hc_ref[...] = jnp.zeros((8, w), F32)

        x = rx_ref[...]
        ext_ref[8:8 + tt, :] = x
        c = _conv_taps(cw_ref[...], cb_ref[...], x, ext_ref, tt)
        ext_ref[0:8, :] = x[tt - 8:tt, :]
        sp = _softplus(-lam_ref[...])
        _, r, ig, a, mult = _lru_gates(c, wa_ref[...], ba_ref[...], wx_ref[...], bx_ref[...], sp)
        c_ref[...] = c
        r_ref[...] = r
        ig_ref[...] = ig
        mult_ref[...] = mult
        h = _scan_fwd(a, mult * (ig * c), hc_ref[7:8, :], tt)
        hc_ref[...] = h[tt - 8:tt, :]
        y_ref[...] = h
        rg = rg_ref[...]
        z_ref[...] = (h * rg * _sigmoid(rg)).astype(BF16)

    vec = pl.BlockSpec((1, w), lambda g, i: (0, g))
    mat = pl.BlockSpec((None, w, w), lambda g, i: (g, 0, 0))
    tile = pl.BlockSpec((tt, w), lambda g, i: (i, g))
    return pl.pallas_call(
        body, name=name, grid=(N_GROUPS, t // tt),
        in_specs=[tile, pl.BlockSpec((tt, w), lambda g, i: (i, N_GROUPS + g)),
                  pl.BlockSpec((4, w), lambda g, i: (0, g)), vec, mat, vec, mat, vec, vec],
        out_specs=[tile, tile] + [pl.BlockSpec((None, tt, w), lambda g, i: (g, i, 0))] * 4,
        out_shape=[jax.ShapeDtypeStruct((t, D_RNN), F32), jax.ShapeDtypeStruct((t, D_RNN), BF16)]
        + [jax.ShapeDtypeStruct((N_GROUPS, t, w), F32)] * 4,
        scratch_shapes=[pltpu.VMEM((tt + 8, w), F32), pltpu.VMEM((8, w), F32)],
        compiler_params=_params("parallel", "arbitrary"))(rx_rg, rx_rg, cw, cb, wa, ba, wx, bx, lam)


def _rnn_bwd(rx_rg, y, dz, c, r, ig, mult, cw, wa, wx, lam, name, tt=512):
    t = rx_rg.shape[0]
    tt = min(tt, t)
    nt = t // tt
    w = GROUP_W

    def body(rx_ref, rg_ref, y_ref, yt_ref, dz_ref, c_ref, r_ref, ig_ref, mult_ref, cw_ref, wa_ref, wx_ref, lam_ref,
             drx_ref, drg_ref, st_ref, gda_ref, gdx_ref, dcx_ref, wcar_ref, acar_ref, dwa_ref, dwx_ref):
        ii = pl.program_id(1)

        @pl.when(ii == 0)
        def _():
            wcar_ref[...] = jnp.zeros((8, w), F32)
            acar_ref[...] = jnp.zeros((8, w), F32)
            dcx_ref[tt:tt + 8, :] = jnp.zeros((8, w), F32)
            st_ref[...] = jnp.zeros_like(st_ref)
            dwa_ref[...] = jnp.zeros_like(dwa_ref)
            dwx_ref[...] = jnp.zeros_like(dwx_ref)

        has_prev = jnp.where(ii == nt - 1, 0.0, 1.0)
        cwv = cw_ref[...]
        lam = lam_ref[...]
        sp = _softplus(-lam)
        wa = wa_ref[...]
        wx = wx_ref[...]
        c = c_ref[...]
        r = r_ref[...]
        ig = ig_ref[...]
        mult = mult_ref[...]
        _, a = _lru_decay(r, sp)
        cb16 = c.astype(BF16)

        rg = rg_ref[...]
        sg = _sigmoid(rg)
        dz = dz_ref[...].astype(F32)
        yv = y_ref[...]
        drg_ref[...] = (dz * yv * (sg * (1.0 + rg * (1.0 - sg)))).astype(BF16)

        row = lax.broadcasted_iota(jnp.int32, (tt, w), 0)
        b = jnp.where(row < tt - 1, pltpu.roll(a, tt - 1, 0), acar_ref[0:1, :])
        dh = _scan_rev(b, dz * (rg * sg), wcar_ref[0:1, :], tt)
        wcar_ref[...] = dh[0:8, :]
        acar_ref[...] = a[0:8, :]

        hprev = jnp.where(row >= 1, pltpu.roll(yv, 1, 0), yt_ref[7:8, :] * has_prev)
        dmult = dh * (ig * c)
        dig = dh * mult * c
        dlog_a = dh * hprev * a - dmult * (a * a / mult)
        dpa = dlog_a * ((-LRU_C) * sp) * r * (1.0 - r)
        dpx = dig * ig * (1.0 - ig)
        dsp = jnp.sum(dlog_a * r, axis=0, keepdims=True) * (-LRU_C)
        dlam = dsp * (-_sigmoid(-lam))
        dpa16 = dpa.astype(BF16)
        dpx16 = dpx.astype(BF16)
        dwa_ref[...] += lax.dot_general(cb16, dpa16, TN_DIMS, preferred_element_type=F32)
        dwx_ref[...] += lax.dot_general(cb16, dpx16, TN_DIMS, preferred_element_type=F32)
        dc = (dh * mult * ig
              + lax.dot_general(dpa16, wa, NT_DIMS, preferred_element_type=F32)
              + lax.dot_general(dpx16, wx, NT_DIMS, preferred_element_type=F32))

        dcx_ref[0:tt, :] = dc
        dc1 = dcx_ref[1:1 + tt, :]
        dc2 = dcx_ref[2:2 + tt, :]
        dc3 = dcx_ref[3:3 + tt, :]
        drx_ref[...] = (cwv[3:4] * dc + cwv[2:3] * dc1 + cwv[1:2] * dc2 + cwv[0:1] * dc3).astype(BF16)
        dcx_ref[tt:tt + 8, :] = dc[0:8, :]

        def colsum(v):
            return jnp.sum(v, axis=0, keepdims=True)

        x = rx_ref[...]
        st_ref[...] += _rows8([colsum(dc), colsum(dpa), colsum(dpx), dlam,
                               colsum(dc3 * x), colsum(dc2 * x), colsum(dc1 * x), colsum(dc * x)], w)

        @pl.when(ii == nt - 1)
        def _():
            for blk in range(GROUP_W // RNN_BLOCK_W):
                rows = slice(blk * RNN_BLOCK_W, (blk + 1) * RNN_BLOCK_W)
                gda_ref[blk] = dwa_ref[rows, rows]
                gdx_ref[blk] = dwx_ref[rows, rows]

    def rev(ii):
        return nt - 1 - ii

    def tail(g, ii):
        return (jnp.maximum(rev(ii) * (tt // 8) - 1, 0), g)

    vec = pl.BlockSpec((1, w), lambda g, ii: (0, g))
    mat = pl.BlockSpec((None, w, w), lambda g, ii: (g, 0, 0))
    tile = pl.BlockSpec((tt, w), lambda g, ii: (rev(ii), g))
    diag_shape = (N_GROUPS, GROUP_W // RNN_BLOCK_W, RNN_BLOCK_W, RNN_BLOCK_W)
    diag = pl.BlockSpec((None,) + diag_shape[1:], lambda g, ii: (g, 0, 0, 0))
    kept = pl.BlockSpec((None, tt, w), lambda g, ii: (g, rev(ii), 0))
    return pl.pallas_call(
        body, name=name, grid=(N_GROUPS, nt),
        in_specs=[tile, pl.BlockSpec((tt, w), lambda g, ii: (rev(ii), N_GROUPS + g)),
                  tile, pl.BlockSpec((8, w), tail), tile, kept, kept, kept, kept,
                  pl.BlockSpec((4, w), lambda g, ii: (0, g)), mat, mat, vec],
        out_specs=[tile, tile, pl.BlockSpec((8, w), lambda g, ii: (0, g)), diag, diag],
        out_shape=[jax.ShapeDtypeStruct((t, D_RNN), BF16), jax.ShapeDtypeStruct((t, D_RNN), BF16),
                   jax.ShapeDtypeStruct((8, D_RNN), F32),
                   jax.ShapeDtypeStruct(diag_shape, F32), jax.ShapeDtypeStruct(diag_shape, F32)],
        scratch_shapes=[pltpu.VMEM((tt + 8, w), F32), pltpu.VMEM((8, w), F32),
                        pltpu.VMEM((8, w), F32), pltpu.VMEM((w, w), F32), pltpu.VMEM((w, w), F32)],
        compiler_params=_params("parallel", "arbitrary"))(rx_rg, rx_rg, y, y, dz, c, r, ig, mult, cw, wa, wx, lam)


def _half_mask(shape, half):
    lane = lax.broadcasted_iota(jnp.int32, shape, 1)
    return (lane >= HEAD_DIM) if half else (lane < HEAD_DIM)


def _dup_half(t, half):
    sel = jnp.where(_half_mask(t.shape, half), t, 0.0)
    return sel + pltpu.roll(sel, HEAD_DIM, 1)


def _band_geometry(n):
    qi = lax.broadcasted_iota(jnp.int32, (BLOCK, 2 * BLOCK), 0)
    kj = lax.broadcasted_iota(jnp.int32, (BLOCK, 2 * BLOCK), 1)
    dist = BLOCK + qi - kj
    first_key = jnp.where(n > 0, 0, BLOCK)
    valid = (dist >= 0) & (dist < BLOCK) & (kj >= first_key)
    return dist.astype(F32), valid


GROUP = 4


def _kv_dup(prev_ref, cur_ref, hk, scale=1.0):
    tile = hk // 2
    kt = jnp.concatenate([prev_ref[:, tile * LANE:(tile + 1) * LANE], cur_ref[:, tile * LANE:(tile + 1) * LANE]], axis=0)
    return (_dup_half(kt.astype(F32), hk % 2) * scale).astype(BF16)


def _fill_bias(bias_ref, sm_ref, n):
    distf, valid = _band_geometry(n)
    for head in range(N_Q_HEADS):
        bias_ref[head] = jnp.where(valid, -sm_ref[1, head] * distf, MASKED)


def _head_scores(q2s, half, kdup, bias):
    qm = jnp.where(_half_mask(q2s.shape, half), q2s, jnp.zeros_like(q2s))
    return qm, lax.dot_general(qm, kdup, NT_DIMS, preferred_element_type=F32) + bias


def _attn_specs(nb, clamp_last):
    def blk(n):
        return jnp.minimum(n, nb - 1) if clamp_last else n

    q_spec = pl.BlockSpec((BLOCK, D_MODEL), lambda n: (blk(n), 0))
    k_prev = pl.BlockSpec((BLOCK, D_KV), lambda n: (jnp.maximum(blk(n) - 1, 0), D_MODEL // D_KV))
    k_cur = pl.BlockSpec((BLOCK, D_KV), lambda n: (blk(n), D_MODEL // D_KV))
    v_prev = pl.BlockSpec((BLOCK, D_KV), lambda n: (jnp.maximum(blk(n) - 1, 0), D_MODEL // D_KV + 1))
    v_cur = pl.BlockSpec((BLOCK, D_KV), lambda n: (blk(n), D_MODEL // D_KV + 1))
    return q_spec, k_prev, k_cur, v_prev, v_cur


def _attn_fwd(sm, qkv, ag_ml, name, ride=None):
    t = qkv.shape[0]
    nb = t // BLOCK

    host = _Hosted(ride, 7, 3, 1)

    def body(*refs):
        (sm_ref, q_ref, kp_ref, kc_ref, vp_ref, vc_ref, ag_ref, y_ref, z_ref, lse_ref, bias_ref), start, finish = host.split(refs)
        n = pl.program_id(0)
        start(n == 0)
        start.middle(n == (3 * nb) // 4)

        @pl.when(n <= 1)
        def _():
            _fill_bias(bias_ref, sm_ref, n)

        lane = lax.broadcasted_iota(jnp.int32, (BLOCK, LANE), 1)
        low = lane < HEAD_DIM
        lse = jnp.zeros((BLOCK, LANE), F32)
        for hk in range(N_Q_HEADS // GROUP):
            kdup = _kv_dup(kp_ref, kc_ref, hk)
            vdup = _kv_dup(vp_ref, vc_ref, hk)
            for k in (0, 1):
                c = slice((2 * hk + k) * LANE, (2 * hk + k + 1) * LANE)
                q2s = q_ref[:, c] * ATTN_SCALE
                outs = []
                for half in (0, 1):
                    head = GROUP * hk + 2 * k + half
                    _, s = _head_scores(q2s, half, kdup, bias_ref[head])
                    sink = sm_ref[0, head]
                    m = jnp.maximum(jnp.max(s, axis=1, keepdims=True), sink)
                    e = jnp.exp(s - m)
                    l = jnp.sum(e, axis=1, keepdims=True) + jnp.exp(sink - m)
                    outs.append(jnp.dot((e * (1.0 / l)).astype(BF16), vdup, preferred_element_type=F32))
                    lse = jnp.where(lane == head, m + jnp.log(l), lse)
                yt = jnp.where(low, outs[0], outs[1])
                y_ref[:, c] = yt
                ag = ag_ref[:, c].astype(F32)
                z_ref[:, c] = (yt * ag * _sigmoid(ag)).astype(BF16)
        lse_ref[...] = lse
        finish(n == nb - 1)

    q_spec, k_prev, k_cur, v_prev, v_cur = _attn_specs(nb, False)
    wide = pl.BlockSpec((BLOCK, D_MODEL), lambda n: (n, 0))
    outs = pl.pallas_call(
        body, name=name, grid=(nb,),
        in_specs=[pl.BlockSpec(memory_space=pltpu.SMEM), q_spec, k_prev, k_cur, v_prev, v_cur, wide] + host.in_specs,
        out_specs=[wide, wide, pl.BlockSpec((BLOCK, LANE), lambda n: (n, 0))] + host.out_specs,
        out_shape=[jax.ShapeDtypeStruct((t, D_MODEL), F32), jax.ShapeDtypeStruct((t, D_MODEL), BF16),
                   jax.ShapeDtypeStruct((t, LANE), F32)] + host.out_shapes,
        scratch_shapes=[pltpu.VMEM((N_Q_HEADS, BLOCK, 2 * BLOCK), F32)] + host.scratch_shapes,
        compiler_params=_params("arbitrary"))(sm, qkv, qkv, qkv, qkv, qkv, ag_ml, *host.arrays)
    res, landed = host.results(outs, 3)
    return (*res, landed) if ride else tuple(res)


def _attn_bwd(sm, qkv, ag_ml, y, lse, dz, name, ride=None):
    t = qkv.shape[0]
    nb = t // BLOCK
    host = _Hosted(ride, 10, 4, 3)

    def body(*refs):
        host_refs, start, finish = host.split(refs)
        (sm_ref, q_ref, kp_ref, kc_ref, vp_ref, vc_ref, ag_ref, y_ref, lse_ref, dz_ref,
         dq_ref, dkv_ref, dag_ref, ds_ref, ck_ref, cv_ref, bias_ref) = host_refs
        n = pl.program_id(0)
        start(n == 0)
        start.middle(n == (3 * nb) // 4)

        @pl.when(n == 0)
        def _():
            ck_ref[...] = jnp.zeros_like(ck_ref)
            cv_ref[...] = jnp.zeros_like(cv_ref)
            ds_ref[...] = jnp.zeros_like(ds_ref)

        @pl.when(n <= 1)
        def _():
            _fill_bias(bias_ref, sm_ref, n)

        @pl.when(n < nb)
        def _():
            lane8 = lax.broadcasted_iota(jnp.int32, (8, LANE), 1)
            row8 = lax.broadcasted_iota(jnp.int32, (8, LANE), 0)
            dsink = jnp.zeros((8, LANE), F32)
            dk_heads, dv_heads = [], []
            lse_tile = lse_ref[...]
            for hk in range(N_Q_HEADS // GROUP):
                kdup = _kv_dup(kp_ref, kc_ref, hk)
                ks = _kv_dup(kp_ref, kc_ref, hk, ATTN_SCALE)
                vdup = _kv_dup(vp_ref, vc_ref, hk)
                qms, dyhs, y_rows = [], [], []
                for k in (0, 1):
                    c = slice((2 * hk + k) * LANE, (2 * hk + k + 1) * LANE)
                    ag = ag_ref[:, c].astype(F32)
                    sg = _sigmoid(ag)
                    dzt = dz_ref[:, c].astype(F32)
                    yt = y_ref[:, c]
                    dag_ref[:, c] = (dzt * yt * (sg * (1.0 + ag * (1.0 - sg)))).astype(BF16)
                    dyt = dzt * (ag * sg)
                    q2s = q_ref[:, c] * ATTN_SCALE
                    for half in (0, 1):
                        hm = _half_mask(q2s.shape, half)
                        qms.append(jnp.where(hm, q2s, jnp.zeros_like(q2s)))
                        dyhs.append(jnp.where(hm, dyt, 0.0))
                        y_rows.append(yt)
                qm4 = jnp.concatenate(qms, axis=0)
                dy4 = jnp.concatenate(dyhs, axis=0)
                dy4_16 = dy4.astype(BF16)
                s4 = lax.dot_general(qm4, kdup, NT_DIMS, preferred_element_type=F32)
                dp4 = lax.dot_general(dy4_16, vdup, NT_DIMS, preferred_element_type=F32)
                probs16, ds16 = [], []
                for r in range(GROUP):
                    head = GROUP * hk + r
                    rows = slice(r * BLOCK, (r + 1) * BLOCK)
                    lh = lse_tile[:, head:head + 1]
                    probs = jnp.exp(s4[rows] + bias_ref[head] - lh)
                    psink = jnp.exp(sm_ref[0, head] - lh)
                    delta = jnp.sum(dyhs[r] * y_rows[r], axis=1, keepdims=True)
                    ds16.append((probs * (dp4[rows] - delta)).astype(BF16))
                    probs16.append(probs.astype(BF16))
                    dsink = dsink + jnp.where((row8 == 0) & (lane8 == head),
                                              -jnp.sum(psink * delta, axis=0, keepdims=True), 0.0)
                ds4 = jnp.concatenate(ds16, axis=0)
                p4 = jnp.concatenate(probs16, axis=0)
                dq4 = jnp.dot(ds4, ks, preferred_element_type=F32)
                low = _half_mask((BLOCK, LANE), 0)
                for k in (0, 1):
                    c = slice((2 * hk + k) * LANE, (2 * hk + k + 1) * LANE)
                    dq_ref[:, c] = jnp.where(low, dq4[2 * k * BLOCK:(2 * k + 1) * BLOCK],
                                             dq4[(2 * k + 1) * BLOCK:(2 * k + 2) * BLOCK]).astype(BF16)
                dk_acc = lax.dot_general(ds4, qm4, TN_DIMS, preferred_element_type=F32)
                dv_acc = lax.dot_general(p4, dy4_16, TN_DIMS, preferred_element_type=F32)
                dk_heads.append(dk_acc + pltpu.roll(dk_acc, HEAD_DIM, 1))
                dv_heads.append(dv_acc + pltpu.roll(dv_acc, HEAD_DIM, 1))
            ds_ref[...] += dsink
            low = _half_mask((2 * BLOCK, LANE), 0)
            for tile in range(2):
                cols = slice(tile * LANE, (tile + 1) * LANE)
                dkt = jnp.where(low, dk_heads[2 * tile], dk_heads[2 * tile + 1])
                dvt = jnp.where(low, dv_heads[2 * tile], dv_heads[2 * tile + 1])
                dkv_ref[:, cols] = (ck_ref[:, cols] + dkt[0:BLOCK, :]).astype(BF16)
                dkv_ref[:, D_KV + tile * LANE:D_KV + (tile + 1) * LANE] = (cv_ref[:, cols] + dvt[0:BLOCK, :]).astype(BF16)
                ck_ref[:, cols] = dkt[BLOCK:2 * BLOCK, :]
                cv_ref[:, cols] = dvt[BLOCK:2 * BLOCK, :]

        @pl.when(n == nb)
        def _():
            dkv_ref[:, 0:D_KV] = ck_ref[...].astype(BF16)
            dkv_ref[:, D_KV:2 * D_KV] = cv_ref[...].astype(BF16)

        finish(n == nb)

    q_spec, k_prev, k_cur, v_prev, v_cur = _attn_specs(nb, True)
    wide = pl.BlockSpec((BLOCK, D_MODEL), lambda n: (jnp.minimum(n, nb - 1), 0))
    outs = pl.pallas_call(
        body, name=name, grid=(nb + 1,),
        in_specs=[pl.BlockSpec(memory_space=pltpu.SMEM), q_spec, k_prev, k_cur, v_prev, v_cur, wide, wide,
                  pl.BlockSpec((BLOCK, LANE), lambda n: (jnp.minimum(n, nb - 1), 0)), wide] + host.in_specs,
        out_specs=[wide, pl.BlockSpec((BLOCK, 2 * D_KV), lambda n: (jnp.maximum(n - 1, 0), 0)), wide,
                   pl.BlockSpec((8, LANE), lambda n: (0, 0))] + host.out_specs,
        out_shape=[jax.ShapeDtypeStruct((t, D_MODEL), BF16), jax.ShapeDtypeStruct((t, 2 * D_KV), BF16),
                   jax.ShapeDtypeStruct((t, D_MODEL), BF16), jax.ShapeDtypeStruct((8, LANE), F32)] + host.out_shapes,
        scratch_shapes=[pltpu.VMEM((BLOCK, D_KV), F32), pltpu.VMEM((BLOCK, D_KV), F32),
                        pltpu.VMEM((N_Q_HEADS, BLOCK, 2 * BLOCK), F32)] + host.scratch_shapes,
        compiler_params=_params("arbitrary"))(sm, qkv, qkv, qkv, qkv, qkv, ag_ml, y, lse, dz, *host.arrays)
    res, landed = host.results(outs, 4)
    return (*res, landed) if ride else tuple(res)


def _local_grads(x, target, p, project, late_weights, reduce_out, reduce_in):
    h, rx_rg, qkv, ag_ml, wt = project(x, p["pre_g"])
    y_attn, z_attn, lse, landed = _attn_fwd(p["sm"], qkv, ag_ml, "attn_fwd", ride=late_weights[0])
    p = {**p, **late_weights[1](landed)}
    y_rnn, z_rnn, *kept_rnn = _rnn_fwd(
        rx_rg, p["cw"], p["cb"], p["wbd_a"], p["b_a"], p["wbd_x"], p["b_x"], p["lam"], "rnn_fwd")
    br_rnn, br_attn, merged, dout, dy, st_post = _branches_fwd(
        z_rnn, z_attn, ag_ml, p["b_gate"], p["w_rnn"], p["w_attn"], p["w_out"], x, target, p["post_g"], "branches_fwd")

    dbr_rnn, dbr_attn, d_ml, dz_rnn, dz_attn, st_merge = _branches_bwd(
        dout, br_rnn, br_attn, ag_ml, p["b_gate"], p["w_rnn"], p["w_attn"], p["w_out"], "branches_bwd")
    out_pairs = _mm_tn_pairs([(z_rnn, dbr_rnn), (z_attn, dbr_attn), (merged, dout)], "gw_outs")
    d_rx, d_rg, st_rnn, g_rg_a, g_rg_x = _rnn_bwd(
        rx_rg, y_rnn, dz_rnn, *kept_rnn, p["cw"], p["wbd_a"], p["wbd_x"], p["lam"], "rnn_bwd")
    dq, dkv, d_ag, st_sink, red_out = _attn_bwd(p["sm"], qkv, ag_ml, y_attn, lse, dz_attn, "attn_bwd",
                                                ride=reduce_out(out_pairs, g_rg_a, g_rg_x))

    segs = [d_rx, d_rg, dq, dkv, d_ag, d_ml]
    grad_x, st_pre, red_in = _input_grad(segs, wt, x, p["pre_g"], dy, "input_grad",
                                         ride=reduce_in(*_mm_tn_seg_pair(segs, h, "gw_in")))
    return dict(grad_x=grad_x, st_post=st_post, st_merge=st_merge, st_rnn=st_rnn, st_sink=st_sink, st_pre=st_pre,
                red_out=red_out, red_in=red_in)


def _place():
    x, y, c = lax.axis_index("x"), lax.axis_index("y"), lax.axis_index("c")
    return x, y, c


def _gather_ride(shards):
    n = len(shards)

    def copies(ins, outs, sems):
        send_sems, recv_sems, local_sems = sems
        x, y, c = _place()
        me, sibling = (x, y, c), (x, y, 1 - c)
        chips = [(1 - x, y), (x, 1 - y), (1 - x, 1 - y)]

        def slot(a, dev):
            return outs[a].at[4 * dev[0] + 2 * dev[1] + dev[2]]

        def copy(a, k, block, to, src=None):
            return pltpu.make_async_remote_copy(
                src_ref=slot(a, block) if src is None else src, dst_ref=slot(a, block),
                send_sem=send_sems.at[a, k], recv_sem=recv_sems.at[a, k], device_id=to, device_id_type=MESH)

        mine = [pltpu.make_async_copy(ins[a], slot(a, me), local_sems.at[a]) for a in range(n)]
        first = []
        for a in range(n):
            first.append(copy(a, 0, me, sibling, src=ins[a]))
            first += [copy(a, 1 + j, me, (*chip, c), src=ins[a]) for j, chip in enumerate(chips)]
        return me, sibling, chips, c, copy, mine, first

    def start(ins, outs, sems):
        *_, mine, first = copies(ins, outs, sems)
        for cp in mine + first:
            cp.start()

    def middle(ins, outs, sems):
        me, sibling, chips, c, copy, _, _ = copies(ins, outs, sems)
        for j, chip in enumerate(chips):
            for a in range(n):
                copy(a, 1 + j, (*chip, c), me).wait_recv()
                copy(a, 4 + j, (*chip, c), sibling).start()

    def finish(ins, outs, sems):
        me, sibling, chips, c, copy, mine, first = copies(ins, outs, sems)
        passed = [copy(a, 4 + j, (*chip, c), sibling) for j, chip in enumerate(chips) for a in range(n)]
        for a in range(n):
            copy(a, 0, sibling, me).wait_recv()
            for j, chip in enumerate(chips):
                copy(a, 4 + j, (*chip, 1 - c), me).wait_recv()
        for cp in first + passed:
            cp.wait_send()
        for cp in mine:
            cp.wait()

    return _Ride(
        shards, [jax.ShapeDtypeStruct((N_DEV, *s.shape), s.dtype) for s in shards],
        [pltpu.SemaphoreType.DMA((n, 7)), pltpu.SemaphoreType.DMA((n, 7)), pltpu.SemaphoreType.DMA((n,))],
        start, finish, middle)


def _chips_ride(scatter):
    n = len(scatter)

    def copies(ins, outs, sems):
        send_sems, recv_sems, local_sems = sems
        x, y, c = _place()
        own = 2 * x + y
        chips = [(1 - x, y), (x, 1 - y), (1 - x, 1 - y)]
        local = [pltpu.make_async_copy(ins[a].at[own], outs[a].at[own], local_sems.at[a]) for a in range(n)]
        sent = [pltpu.make_async_remote_copy(
            src_ref=ins[a].at[2 * chip[0] + chip[1]], dst_ref=outs[a].at[own],
            send_sem=send_sems.at[a, j], recv_sem=recv_sems.at[a, own], device_id=(*chip, c), device_id_type=MESH)
            for a in range(n) for j, chip in enumerate(chips)]
        return chips, c, local, sent

    def start(ins, outs, sems):
        _, _, local, sent = copies(ins, outs, sems)
        for cp in local + sent:
            cp.start()

    def finish(ins, outs, sems):
        send_sems, recv_sems, _ = sems
        chips, c, local, sent = copies(ins, outs, sems)
        for a in range(n):
            for chip in chips:
                k = 2 * chip[0] + chip[1]
                pltpu.make_async_remote_copy(
                    src_ref=outs[a].at[k], dst_ref=outs[a].at[k], send_sem=send_sems.at[a, 0],
                    recv_sem=recv_sems.at[a, k], device_id=(*chip, c), device_id_type=MESH).wait_recv()
        for cp in sent:
            cp.wait_send()
        for cp in local:
            cp.wait()

    return _Ride(
        list(scatter), [jax.ShapeDtypeStruct(s.shape, s.dtype) for s in scatter],
        [pltpu.SemaphoreType.DMA((n, 3)), pltpu.SemaphoreType.DMA((n, N_CHIP)), pltpu.SemaphoreType.DMA((n,))],
        start, finish)


def _chips_rest_ride(pair, red):
    table = _side_pieces()
    nc = len(CHUNK_ORDER)

    def each(ins, outs, sems, sending, landing):
        send_sems, recv_sems, _ = sems
        pair_ref, red_ref = ins[0], outs[0]
        x, y, c = _place()
        own = 2 * x + y
        for core in (0, 1):
            mine = table[core][0]
            rest = [s for s in range(EARLY_STEPS, nc) if mine[s] is not None]

            @pl.when(c == core)
            def _(mine=mine, rest=rest, core=core):
                for s in rest:
                    _, rows, at = mine[s]
                    k, cp = _to_chip(pair_ref.at[at // SHARD_IN, pl.ds(at % SHARD_IN, rows)], red_ref, mine[s], s,
                                     send_sems, recv_sems, own, core)
                    pl.when(own != k)(lambda cp=cp: sending(cp))
                if landing is not None:
                    for k in range(N_CHIP):
                        @pl.when(own == k)
                        def _(k=k):
                            for s in rest:
                                _, rows, at = mine[s]
                                if at // SHARD_IN == k:
                                    for chip in range(N_CHIP):
                                        if chip != k:
                                            landing(_from_chip(pair_ref.at[k, pl.ds(at % SHARD_IN, rows)], red_ref, mine[s], s,
                                                               send_sems, recv_sems, chip, (x, y, c)))

    def local(ins, outs, sems):
        x, y, _ = _place()
        return pltpu.make_async_copy(ins[0].at[2 * x + y], outs[0].at[2 * x + y], sems[2])

    def start(ins, outs, sems):
        local(ins, outs, sems).start()
        each(ins, outs, sems, lambda cp: cp.start(), None)

    def finish(ins, outs, sems):
        each(ins, outs, sems, lambda cp: cp.wait_send(), lambda cp: cp.wait_recv())
        local(ins, outs, sems).wait()

    return _Ride([pair, red], [jax.ShapeDtypeStruct(red.shape, red.dtype)],
                 [pltpu.SemaphoreType.DMA((nc,)), pltpu.SemaphoreType.DMA((N_CHIP, nc)), pltpu.SemaphoreType.DMA],
                 start, finish, aliases={1: 0})


def _allreduce_small(st_pre, st_merge, st_rnn, st_post, st_sink, name):
    d = D_MODEL

    def body(pre_ref, mg_ref, rnn_ref, post_ref, sink_ref, o_ref, all_ref, send_sems, recv_sems):
        x, y, c = _place()
        me = 4 * x + 2 * y + c
        rnn = rnn_ref[...]
        sinks = jnp.concatenate([sink_ref[0:1, :], jnp.zeros((1, d - LANE), F32)], axis=1)
        all_ref[me, 0:8, :] = _rows8([pre_ref[0:1, :], mg_ref[0:1, :], mg_ref[1:2, :], rnn[0:1], rnn[1:2], rnn[2:3], rnn[3:4],
                                      post_ref[0:1, :]], d)
        all_ref[me, 8:16, :] = _rows8([post_ref[1:2, :], sinks, rnn[4:5], rnn[5:6], rnn[6:7], rnn[7:8]], d)

        def crossing(block, peer):
            return pltpu.make_async_remote_copy(
                src_ref=all_ref.at[block], dst_ref=all_ref.at[block], send_sem=send_sems.at[peer], recv_sem=recv_sems.at[block],
                device_id=(peer // 4, (peer // 2) % 2, peer % 2), device_id_type=MESH)

        for j in range(N_DEV):
            pl.when(me != j)(crossing(me, j).start)
        for j in range(N_DEV):
            pl.when(me != j)(crossing(j, j).wait_recv)
        for j in range(N_DEV):
            pl.when(me != j)(crossing(me, j).wait_send)
        o_ref[...] = (((all_ref[0] + all_ref[1]) + (all_ref[2] + all_ref[3]))
                      + ((all_ref[4] + all_ref[5]) + (all_ref[6] + all_ref[7])))

    vmem = pl.BlockSpec(memory_space=pltpu.VMEM)
    return pl.pallas_call(
        body, name=name, out_shape=jax.ShapeDtypeStruct((SMALL_ROWS, d), F32), in_specs=[vmem] * 5, out_specs=vmem,
        scratch_shapes=[pltpu.VMEM((N_DEV, SMALL_ROWS, d), F32), pltpu.SemaphoreType.DMA((N_DEV,)), pltpu.SemaphoreType.DMA((N_DEV,))],
    )(st_pre, st_merge, st_rnn, st_post, st_sink)


def _adamw(g, w, m, v):
    m = ADAM_B1 * m + (1.0 - ADAM_B1) * g
    v = ADAM_B2 * v + (1.0 - ADAM_B2) * (g * g)
    m_hat = m / (1.0 - ADAM_B1 ** ADAM_STEP)
    v_hat = v / (1.0 - ADAM_B2 ** ADAM_STEP)
    delta = -ADAM_LR * (m_hat / (jnp.sqrt(v_hat) + ADAM_EPS) + ADAM_WD * w)
    return delta, m, v


def _adam_parts(items, name, tr=None):
    ni = len(items)
    npart, r, c = items[0][0].shape
    tr = r if tr is None else min(tr, r)

    def body(*refs):
        for a in range(ni):
            p_ref, w_ref, m_ref, v_ref = refs[4 * a:4 * a + 4]
            g_ref, d_ref, nm_ref, nv_ref = refs[4 * (ni + a):4 * (ni + a) + 4]
            g = p_ref[0].astype(F32)
            for k in range(1, npart):
                g = g + p_ref[k].astype(F32)
            g_ref[...] = g
            d_ref[...], nm_ref[...], nv_ref[...] = _adamw(g, w_ref[...], m_ref[...], v_ref[...])

    tile = pl.BlockSpec((tr, c), lambda i: (i, 0))
    outs = pl.pallas_call(
        body, name=name, grid=(r // tr,),
        in_specs=[pl.BlockSpec((npart, tr, c), lambda i: (0, i, 0)), tile, tile, tile] * ni,
        out_specs=[tile] * (4 * ni), out_shape=[jax.ShapeDtypeStruct((r, c), F32)] * (4 * ni),
        compiler_params=_params("parallel"))(*[arr for item in items for arr in item])
    return [outs[4 * a:4 * a + 4] for a in range(ni)]


def _block_diag(w):
    w4 = w.reshape(N_GROUPS, 4, RNN_BLOCK_W, RNN_BLOCK_W)
    eye = jnp.eye(4, dtype=w.dtype)
    return jnp.einsum("gbij,bc->gbicj", w4, eye).reshape(N_GROUPS, GROUP_W, GROUP_W).astype(BF16)


SMALL_ROWS = 16
ROW_PRE_G, ROW_BGATE, ROW_CONV_B, ROW_B_A, ROW_B_X, ROW_LAM, ROW_POST_G, ROW_LOSS, ROW_SINKS, ROW_CONV_W = 0, 1, 3, 4, 5, 6, 7, 8, 9, 10


def _adam_small(total, w, m, v, name):
    d = D_MODEL
    n_in = len(w)
    rows = (ROW_PRE_G, ROW_BGATE, ROW_CONV_B, ROW_B_A, ROW_B_X, ROW_LAM, ROW_POST_G, ROW_SINKS)

    def grad_of(p_ref, row, shape):
        if shape == (1, 2 * d):
            return jnp.concatenate([p_ref[row:row + 1, :], p_ref[row + 1:row + 2, :]], axis=1)
        if shape == (1, RNN_BLOCKS, RNN_BLOCK_W):
            r = p_ref[row:row + 1, :]
            return jnp.concatenate([r[:, b * RNN_BLOCK_W:(b + 1) * RNN_BLOCK_W] for b in range(RNN_BLOCKS)], axis=0)[None]
        return p_ref[row:row + 1, 0:shape[1]]

    def body(*refs):
        p_ref = refs[0]
        w_refs, m_refs, v_refs = (refs[1 + k * n_in:1 + (k + 1) * n_in] for k in range(3))
        outs = refs[1 + 3 * n_in:]
        for k in range(n_in):
            g = grad_of(p_ref, rows[k], w[k].shape)
            res = (g,) + _adamw(g, w_refs[k][...], m_refs[k][...], v_refs[k][...])
            for kind in range(4):
                outs[kind * n_in + k][...] = res[kind]

    outs = pl.pallas_call(
        body, name=name, out_shape=[jax.ShapeDtypeStruct(a.shape, F32) for _ in range(4) for a in w])(total, *w, *m, *v)
    return [outs[kind * n_in:(kind + 1) * n_in] for kind in range(4)]


def kernel(x, pre_norm_g, w_in, b_gate, conv_w, conv_b, w_rg_a, b_rg_a, w_rg_x, b_rg_x, lru_lambda, attn_sinks, w_rnn_out, w_attn_out, w_out, post_norm_g, loss_target, m_pre_norm_g, m_w_in, m_b_gate, m_conv_w, m_conv_b, m_w_rg_a, m_b_rg_a, m_w_rg_x, m_b_rg_x, m_lru_lambda, m_attn_sinks, m_w_rnn_out, m_w_attn_out, m_w_out, m_post_norm_g, v_pre_norm_g, v_w_in, v_b_gate, v_conv_w, v_conv_b, v_w_rg_a, v_b_rg_a, v_w_rg_x, v_b_rg_x, v_lru_lambda, v_attn_sinks, v_w_rnn_out, v_w_attn_out, v_w_out, v_post_norm_g):
    cx, cy, cc = _place()
    dev = 4 * cx + 2 * cy + cc

    w_in_t, m_in_t, v_in_t = (jnp.transpose(a[0]) for a in (w_in, m_w_in, v_w_in))
    wt_shard = w_in_t.astype(BF16)

    def project(xs, pre_g):
        h, rx_rg, qkv, ag_ml, wt_all = _gather_project(xs, pre_g, wt_shard, "gather_project")
        return h, rx_rg, qkv, ag_ml, wt_all.reshape(D_IN, D_MODEL)

    def late_unpack(landed):
        w_rnn_all, w_attn_all, w_out_all, cw_all = landed
        return dict(w_rnn=w_rnn_all.reshape(D_RNN, D_MODEL), w_attn=w_attn_all.reshape(D_MODEL, D_MODEL),
                    w_out=w_out_all.reshape(D_MODEL, D_MODEL), cw=jnp.transpose(cw_all, (1, 0, 2)).reshape(4, D_RNN))

    late_weights = (_gather_ride([w_rnn_out[0].astype(BF16), w_attn_out[0].astype(BF16), w_out[0].astype(BF16), conv_w[0]]),
                    late_unpack)

    heads = jnp.arange(1, N_Q_HEADS + 1, dtype=F32)
    slopes = jnp.exp2(-ALIBI_MAX_BIAS * heads / N_Q_HEADS)
    b_a = b_rg_a.reshape(1, D_RNN)
    b_x = b_rg_x.reshape(1, D_RNN)
    p = dict(
        pre_g=pre_norm_g, post_g=post_norm_g, b_gate=b_gate, cb=conv_b,
        wbd_a=_block_diag(w_rg_a[0]), b_a=b_a, wbd_x=_block_diag(w_rg_x[0]), b_x=b_x, lam=lru_lambda,
        sm=jnp.pad(attn_sinks, ((0, 1), (0, 0))) + jnp.pad(slopes[None, :], ((1, 0), (0, 0))))

    flat = (RNN_BLOCKS * RNN_BLOCK_W, RNN_BLOCK_W)

    def reduce_out(out_pairs, g_rg_a, g_rg_x):
        return _join_rides(_chips_ride(out_pairs), _gather_ride([g_rg_a.reshape(flat), g_rg_x.reshape(flat)]))

    reduce_in = _chips_rest_ride

    g = _local_grads(x[0], loss_target[0], p, project, late_weights, reduce_out, reduce_in)
    small = _allreduce_small(g["st_pre"], g["st_merge"], g["st_rnn"], g["st_post"], g["st_sink"], "allreduce_small")

    out = {}
    red = g["red_out"]
    w_in_out, = _adam_parts([(g["red_in"][0], w_in_t, m_in_t, v_in_t)], "adam_w_in", tr=SHARD_IN // 2)
    out["w_in"] = [jnp.transpose(o) for o in w_in_out]
    out["w_rnn_out"], out["w_attn_out"], out["w_out"] = _adam_parts(
        [(red[0], w_rnn_out[0], m_w_rnn_out[0], v_w_rnn_out[0]), (red[1], w_attn_out[0], m_w_attn_out[0], v_w_attn_out[0]),
         (red[2], w_out[0], m_w_out[0], v_w_out[0])], "adam_w_outs")
    out["w_rg_a"], out["w_rg_x"] = _adam_parts(
        [(red[3], w_rg_a.reshape(flat), m_w_rg_a.reshape(flat), v_w_rg_a.reshape(flat)),
         (red[4], w_rg_x.reshape(flat), m_w_rg_x.reshape(flat), v_w_rg_x.reshape(flat))], "adam_w_rg")

    small_names = ("pre_norm_g", "b_gate", "conv_b", "b_rg_a", "b_rg_x", "lru_lambda", "post_norm_g", "attn_sinks")
    small_out = _adam_small(
        small,
        (pre_norm_g, b_gate, conv_b, b_rg_a, b_rg_x, lru_lambda, post_norm_g, attn_sinks),
        (m_pre_norm_g, m_b_gate, m_conv_b, m_b_rg_a, m_b_rg_x, m_lru_lambda, m_post_norm_g, m_attn_sinks),
        (v_pre_norm_g, v_b_gate, v_conv_b, v_b_rg_a, v_b_rg_x, v_lru_lambda, v_post_norm_g, v_attn_sinks),
        "adam_small")
    g_cw = lax.dynamic_slice(small[ROW_CONV_W:ROW_CONV_W + 4], (0, dev * SHARD_OUT), (4, SHARD_OUT))
    out["conv_w"], = _adam_parts([(g_cw[None], conv_w[0], m_conv_w[0], v_conv_w[0])], "adam_conv_w")

    shapes = dict(w_in=(1, D_MODEL, SHARD_IN), w_rnn_out=(1, SHARD_OUT, D_MODEL), w_attn_out=(1, SHARD_OUT, D_MODEL),
                  w_out=(1, SHARD_OUT, D_MODEL), w_rg_a=(1, RNN_BLOCKS, RNN_BLOCK_W, RNN_BLOCK_W),
                  w_rg_x=(1, RNN_BLOCKS, RNN_BLOCK_W, RNN_BLOCK_W), conv_w=(1, 4, SHARD_OUT))
    weights = ["pre_norm_g", "w_in", "b_gate", "conv_w", "conv_b", "w_rg_a", "b_rg_a", "w_rg_x", "b_rg_x",
               "lru_lambda", "attn_sinks", "w_rnn_out", "w_attn_out", "w_out", "post_norm_g"]
    results = []
    for kind in range(4):
        for name in weights:
            if name in out:
                results.append(out[name][kind].reshape(shapes[name]))
            else:
                results.append(small_out[kind][small_names.index(name)])
    loss = 0.5 / D_MODEL * jnp.sum(small[ROW_LOSS])
    return (loss, g["grad_x"][None], *results)
```

```python
import jax
import jax.numpy as jnp
from jax import lax
from jax.experimental import pallas as pl
from jax.experimental.pallas import tpu as pltpu

F32 = jnp.float32
BF16 = jnp.bfloat16

D_MODEL = 1024
D_RNN = 1024
RNN_BLOCKS = 16
RNN_BLOCK_W = 64
LRU_C = 8.0
N_Q_HEADS = 16
HEAD_DIM = 64
D_KV = 256
BLOCK = 128
ALIBI_MAX_BIAS = 8.0
EPS = 1e-6
D_IN = 6656
N_DEV = 8
N_CHIP = 4
SHARD_IN = D_IN // N_DEV
SHARD_OUT = D_MODEL // N_DEV
ATTN_SCALE = HEAD_DIM ** -0.5
MASKED = -1e30

ADAM_LR = 0.001
ADAM_B1 = 0.9
ADAM_B2 = 0.999
ADAM_EPS = 1e-08
ADAM_WD = 0.01
ADAM_STEP = 10

VMEM_LIMIT_BYTES = 52 * 1024 * 1024
LANE = 128
GROUP_W = 256
N_GROUPS = D_RNN // GROUP_W
SEG_CHUNK = 512

NT_DIMS = (((1,), (1,)), ((), ()))
TN_DIMS = (((0,), (0,)), ((), ()))
MESH = pl.DeviceIdType.MESH
ANY = pl.BlockSpec(memory_space=pl.ANY)


def _params(*semantics):
    return pltpu.CompilerParams(dimension_semantics=semantics, vmem_limit_bytes=VMEM_LIMIT_BYTES)


def _sigmoid(x):
    return 0.5 * jnp.tanh(0.5 * x) + 0.5


def _log1p(e):
    u = 1.0 + e
    den = jnp.where(u == 1.0, 1.0, u - 1.0)
    return jnp.where(u == 1.0, e, jnp.log(u) * (e / den))


def _softplus(z):
    return jnp.maximum(z, 0.0) + _log1p(jnp.exp(-jnp.abs(z)))


def _rows8(rows, width):
    idx = lax.broadcasted_iota(jnp.int32, (8, width), 0)
    out = jnp.zeros((8, width), F32)
    for r, v in enumerate(rows):
        out = jnp.where(idx == r, v, out)
    return out


class _Ride:
    def __init__(self, arrays, out_shapes, scratch_shapes, start, finish, middle=None, aliases=None):
        self.arrays, self.out_shapes, self.scratch_shapes = list(arrays), list(out_shapes), list(scratch_shapes)
        self.start, self.finish, self.middle = start, finish, middle
        self.aliases = dict(aliases or {})


class _Hosted:
    def __init__(self, ride, n_in, n_out, n_scratch=0, aliasing=False):
        self.ride = ride
        assert aliasing or not (ride and ride.aliases), "this host does not alias"
        self.aliases = {n_in + i: n_out + o for i, o in ride.aliases.items()} if ride else {}
        self.sizes = (n_in, len(ride.arrays) if ride else 0, n_out, len(ride.out_shapes) if ride else 0, n_scratch)
        self.arrays = ride.arrays if ride else []
        self.in_specs = [ANY] * len(self.arrays)
        self.out_shapes = ride.out_shapes if ride else []
        self.out_specs = [ANY] * len(self.out_shapes)
        self.scratch_shapes = ride.scratch_shapes if ride else []

    def split(self, refs):
        n_in, r_in, n_out, r_out, n_scr = self.sizes
        cuts = [0, n_in, n_in + r_in, n_in + r_in + n_out, n_in + r_in + n_out + r_out, n_in + r_in + n_out + r_out + n_scr]
        host_in, ride_in, host_out, ride_out, host_scr = (refs[cuts[k]:cuts[k + 1]] for k in range(5))
        ride_scr = refs[cuts[5]:]

        def start(when):
            if self.ride is not None:
                pl.when(when)(lambda: self.ride.start(ride_in, ride_out, ride_scr))

        def finish(when):
            if self.ride is not None:
                pl.when(when)(lambda: self.ride.finish(ride_in, ride_out, ride_scr))

        def middle(when):
            if self.ride is not None and self.ride.middle is not None:
                pl.when(when)(lambda: self.ride.middle(ride_in, ride_out, ride_scr))

        start.middle = middle
        return tuple(host_in) + tuple(host_out) + tuple(host_scr), start, finish

    def results(self, outs, n_out):
        outs = list(outs) if isinstance(outs, (list, tuple)) else [outs]
        return outs[:n_out], outs[n_out:]


def _join_rides(a, b):
    na, nb_ = len(a.arrays), len(b.arrays)
    oa = len(a.out_shapes)
    sa = len(a.scratch_shapes)

    def both(fa, fb):
        def run(ins, outs, sems):
            if fa is not None:
                fa(ins[:na], outs[:oa], sems[:sa])
            if fb is not None:
                fb(ins[na:na + nb_], outs[oa:], sems[sa:])
        return run

    middle = both(a.middle, b.middle) if (a.middle or b.middle) else None
    return _Ride(a.arrays + b.arrays, a.out_shapes + b.out_shapes, a.scratch_shapes + b.scratch_shapes,
                 both(a.start, b.start), both(a.finish, b.finish), middle)


def _load_resident(w_hbm, w_vmem, sems, first):
    def piece(c):
        rows = pl.ds(c * SEG_CHUNK, SEG_CHUNK)
        return pltpu.make_async_copy(w_hbm.at[rows], w_vmem.at[rows], sems.at[c])

    @pl.when(first)
    def _():
        for c in range(w_vmem.shape[0] // SEG_CHUNK):
            piece(c).start()

    def ready(c):
        @pl.when(first)
        def _():
            piece(c).wait()

    return ready


CHIP_ROWS = 2 * SHARD_IN
PROJ_WIDTHS = (2 * D_RNN, D_MODEL + 2 * D_KV, 3 * D_MODEL)
PROJ_DTYPES = (F32, BF16, BF16)


def _chip_pieces():
    starts = [0, PROJ_WIDTHS[0], PROJ_WIDTHS[0] + PROJ_WIDTHS[1], D_IN]
    pieces = []
    for k in range(N_CHIP):
        lo, hi = k * CHIP_ROWS, (k + 1) * CHIP_ROWS
        cur = []
        for a in range(len(PROJ_WIDTHS)):
            s0, s1 = max(lo, starts[a]), min(hi, starts[a + 1])
            if s0 < s1:
                cur.append((a, s0 - starts[a], s1 - s0, s0 - lo))
        pieces.append(cur)
    return pieces


def _gather_project(x, g, wt_shard, name, tm=1024):
    t, k = x.shape
    tm = min(tm, t)
    nt = t // tm
    pieces = _chip_pieces()

    def body(x_ref, g_ref, shard_ref, h_out, rx_ref, qkv_ref, ag_ref, wt_all,
             w_c, stage, o32, o16, h_all, send_sems, recv_sems, local_sem, stage_sems, out_sems, h_sems):
        s, ti = pl.program_id(0), pl.program_id(1)
        px, py, pc = _place()
        me, sibling = (px, py, pc), (px, py, 1 - pc)
        chips = [(px, py), (1 - px, py), (px, 1 - py), (1 - px, 1 - py)]
        outs = (rx_ref, qkv_ref, ag_ref)

        def slot(dev):
            return wt_all.at[4 * dev[0] + 2 * dev[1] + dev[2]]

        def copy(kk, block, to, src=None):
            return pltpu.make_async_remote_copy(
                src_ref=slot(block) if src is None else src, dst_ref=slot(block),
                send_sem=send_sems.at[kk], recv_sem=recv_sems.at[kk], device_id=to, device_id_type=MESH)

        mine = pltpu.make_async_copy(shard_ref, slot(me), local_sem)
        first = [copy(0, me, sibling, src=shard_ref)] + [copy(1 + j, me, (*chips[1 + j], pc), src=shard_ref) for j in range(3)]
        passed = [copy(4 + j, (*chips[1 + j], pc), sibling) for j in range(3)]

        @pl.when((s == 0) & (ti == 0))
        def _():
            mine.start()
            for cp in first[:3]:
                cp.start()

        def pass_on(step):
            copy(step, (*chips[step], pc), me).wait_recv()
            passed[step - 1].start()

        @pl.when((s == 2) & (ti == nt - 1))
        def _():
            pass_on(3)

        for step in range(N_CHIP):
            @pl.when((s == step) & (ti == 0))
            def _(step=step):
                chip = chips[step]
                if step == 0:
                    mine.wait()
                    copy(0, sibling, me).wait_recv()
                else:
                    if step == 1:
                        pass_on(1)
                        first[3].start()
                        pass_on(2)
                    copy(3 + step, (*chip, 1 - pc), me).wait_recv()
                loads = [pltpu.make_async_copy(slot((*chip, core)), stage.at[pl.ds(core * SHARD_IN, SHARD_IN)], stage_sems.at[core])
                         for core in (0, 1)]
                for cp in loads:
                    cp.start()
                for cp in loads:
                    cp.wait()
                w_c[...] = stage[...].T

        rows = pl.ds(pl.multiple_of(ti * tm, tm), tm)

        @pl.when(s == 0)
        def _():
            xv = x_ref[...]
            h_all[rows, :] = (xv * lax.rsqrt(jnp.mean(xv * xv, axis=-1, keepdims=True) + EPS) * g_ref[...]).astype(BF16)

        res = jnp.dot(h_all[rows, :], w_c[...], preferred_element_type=F32)

        n = s * nt + ti
        buf = lax.rem(n, 2)
        chip_idx = [2 * cx + cy for cx, cy in chips]

        def chip_at(step):
            return jnp.where(step == 0, chip_idx[0], jnp.where(step == 1, chip_idx[1], jnp.where(step == 2, chip_idx[2], chip_idx[3])))

        def h_write(b, tile):
            return pltpu.make_async_copy(h_all.at[pl.ds(tile * tm, tm)], h_out.at[pl.ds(tile * tm, tm)], h_sems.at[b])

        def writes(kchip, b, tile):
            cps = []
            for idx, (a, col, w, src) in enumerate(pieces[kchip]):
                staged = (o16 if PROJ_DTYPES[a] == BF16 else o32).at[b, :, pl.ds(src, w)]
                cps.append(pltpu.make_async_copy(staged, outs[a].at[pl.ds(tile * tm, tm), pl.ds(col, w)], out_sems.at[b, idx]))
            return cps

        o32[buf] = res
        o16[buf] = res.astype(BF16)
        kcur = chip_at(s)

        @pl.when(n > 0)
        def _():
            kprev = chip_at(lax.div(n - 1, nt))
            for kchip in range(N_CHIP):
                @pl.when(kprev == kchip)
                def _(kchip=kchip):
                    for cp in writes(kchip, 1 - buf, lax.rem(n - 1, nt)):
                        cp.wait()

            @pl.when(n <= nt)
            def _():
                h_write(1 - buf, n - 1).wait()

        @pl.when(s == 0)
        def _():
            h_write(buf, ti).start()

        for kchip in range(N_CHIP):
            @pl.when(kcur == kchip)
            def _(kchip=kchip):
                for cp in writes(kchip, buf, ti):
                    cp.start()

        @pl.when(n == N_CHIP * nt - 1)
        def _():
            for kchip in range(N_CHIP):
                @pl.when(kcur == kchip)
                def _(kchip=kchip):
                    for cp in writes(kchip, buf, ti):
                        cp.wait()
            for cp in first + passed:
                cp.wait_send()

    tile = pl.BlockSpec((tm, k), lambda s, ti: (jnp.where(s == 0, ti, nt - 1), 0))
    return pl.pallas_call(
        body, name=name, grid=(N_CHIP, nt),
        in_specs=[tile, pl.BlockSpec((1, k), lambda s, ti: (0, 0)), ANY],
        out_specs=[ANY, ANY, ANY, ANY, ANY],
        out_shape=[jax.ShapeDtypeStruct((t, k), BF16)]
        + [jax.ShapeDtypeStruct((t, w), dt) for w, dt in zip(PROJ_WIDTHS, PROJ_DTYPES)]
        + [jax.ShapeDtypeStruct((N_DEV, SHARD_IN, k), BF16)],
        scratch_shapes=[pltpu.VMEM((k, CHIP_ROWS), BF16), pltpu.VMEM((CHIP_ROWS, k), BF16),
                        pltpu.VMEM((2, tm, CHIP_ROWS), F32), pltpu.VMEM((2, tm, CHIP_ROWS), BF16), pltpu.VMEM((t, k), BF16),
                        pltpu.SemaphoreType.DMA((7,)), pltpu.SemaphoreType.DMA((7,)), pltpu.SemaphoreType.DMA,
                        pltpu.SemaphoreType.DMA((2,)), pltpu.SemaphoreType.DMA((2, 2)), pltpu.SemaphoreType.DMA((2,))],
        compiler_params=_params("arbitrary", "arbitrary"))(x, g, wt_shard)


def _mm_tn_pairs(prods, name):
    n_prod = len(prods)
    ktok, m = prods[0][0].shape
    n = prods[0][1].shape[1]
    half, blk = m // 2, m // N_DEV
    ns = 2 * n_prod
    kept = N_CHIP * blk

    def side_pieces(s, side):
        i, hf = divmod(s, 2)
        first = hf * N_DEV // 2
        return [((d - first) * blk, i * kept + (d // 2) * blk) for d in range(first, first + N_DEV // 2) if d % 2 == side]

    def body(*refs):
        a_hbm, b_hbm, pair_refs = refs[:n_prod], refs[n_prod:2 * n_prod], refs[2 * n_prod:3 * n_prod]
        a_buf, b_buf, res_buf, recv_all, pair_all, a_sems, b_sems, send_sems, recv_sems, out_sems = refs[3 * n_prod:]
        step = pl.program_id(0)
        px, py, pc = _place()

        def fetch_a(s):
            return pltpu.make_async_copy(a_hbm[s // 2].at[:, pl.ds((s % 2) * half, half)], a_buf.at[s % 2], a_sems.at[s % 2])

        def fetch_b(i):
            return pltpu.make_async_copy(b_hbm[i], b_buf.at[i % 2], b_sems.at[i % 2])

        @pl.when(step == 0)
        def _():
            fetch_a(0).start()
            fetch_b(0).start()

        for s in range(ns):
            @pl.when(step == s)
            def _(s=s):
                if s + 1 < ns:
                    fetch_a(s + 1).start()
                if s % 2 == 0 and s // 2 + 1 < n_prod:
                    fetch_b(s // 2 + 1).start()
                fetch_a(s).wait()
                if s % 2 == 0:
                    fetch_b(s // 2).wait()

        res_buf[step % 2] = lax.dot_general(a_buf[step % 2], b_buf[(step // 2) % 2], TN_DIMS, preferred_element_type=F32)

        def crossing(s, j, piece):
            off, at = piece
            return pltpu.make_async_remote_copy(
                src_ref=res_buf.at[s % 2, pl.ds(off, blk)], dst_ref=recv_all.at[pl.ds(at, blk)],
                send_sem=send_sems.at[s, j], recv_sem=recv_sems.at[s, j], device_id=(px, py, 1 - pc), device_id_type=MESH)

        def write(s, core):
            at = side_pieces(s, core)[0][1]
            return pltpu.make_async_copy(pair_all.at[pl.ds(at, 2 * blk)], pair_refs[s // 2].at[pl.ds(at % kept, 2 * blk)], out_sems.at[s])

        for core in (0, 1):
            def settle(s, core=core):
                for j, piece in enumerate(side_pieces(s, 1 - core)):
                    crossing(s, j, piece).wait_send()
                for j, (off, at) in enumerate(side_pieces(s, core)):
                    crossing(s, j, (off, at)).wait_recv()
                    pair_all[at:at + blk] = (res_buf[s % 2, off:off + blk] + recv_all[at:at + blk]).astype(BF16)
                write(s, core).start()

            for s in range(ns):
                @pl.when((pc == core) & (step == s))
                def _(s=s, settle=settle, core=core):
                    for j, piece in enumerate(side_pieces(s, 1 - core)):
                        crossing(s, j, piece).start()
                    if s > 0:
                        settle(s - 1)
                    if s == ns - 1:
                        settle(s)
                        for t in range(ns):
                            write(t, core).wait()

    pairs = pl.pallas_call(
        body, name=name, grid=(ns,),
        in_specs=[ANY] * (2 * n_prod), out_specs=[ANY] * n_prod,
        out_shape=[jax.ShapeDtypeStruct((kept, n), BF16)] * n_prod,
        scratch_shapes=[pltpu.VMEM((2, ktok, half), prods[0][0].dtype), pltpu.VMEM((2, ktok, n), prods[0][1].dtype),
                        pltpu.VMEM((2, half, n), F32), pltpu.VMEM((n_prod * kept, n), F32), pltpu.VMEM((n_prod * kept, n), BF16),
                        pltpu.SemaphoreType.DMA((2,)), pltpu.SemaphoreType.DMA((2,)),
                        pltpu.SemaphoreType.DMA((ns, 2)), pltpu.SemaphoreType.DMA((ns, 2)), pltpu.SemaphoreType.DMA((ns,))],
        compiler_params=_params("arbitrary"))(*[a for a, _ in prods], *[b for _, b in prods])
    return [pair.reshape(N_CHIP, blk, n) for pair in pairs]


def _segment_chunks(segs):
    bounds = [0]
    for s in segs:
        bounds.append(bounds[-1] + s.shape[1] // SEG_CHUNK)
    return bounds


def _input_grad(segs, wt, x, g, dy, name, tm=512, ride=None):
    m = segs[0].shape[0]
    rows, n = wt.shape
    tm = min(tm, m)
    bounds = _segment_chunks(segs)
    n_seg = len(segs)
    ni = m // tm
    host = _Hosted(ride, n_seg + 4, 2, 2, aliasing=True)

    def body(*refs):
        host_refs, start, finish = host.split(refs)
        a_refs = host_refs[:n_seg]
        wt_hbm, x_ref, g_ref, dy_ref, gx_ref, st_ref, wt_vmem, sems = host_refs[n_seg:]
        i = pl.program_id(0)
        start(i == 0)

        @pl.when(i == 0)
        def _():
            st_ref[...] = jnp.zeros_like(st_ref)

        ready = _load_resident(wt_hbm, wt_vmem, sems, i == 0)
        dh = None
        for s in range(n_seg):
            for c in range(bounds[s], bounds[s + 1]):
                ready(c)
            part = jnp.dot(a_refs[s][...], wt_vmem[bounds[s] * SEG_CHUNK:bounds[s + 1] * SEG_CHUNK, :], preferred_element_type=F32)
            dh = part if dh is None else dh + part
        xv = x_ref[...]
        r = lax.rsqrt(jnp.mean(xv * xv, axis=-1, keepdims=True) + EPS)
        xn = xv * r
        dxn = dh * g_ref[...]
        gx_ref[...] = dy_ref[...] + r * (dxn - xn * jnp.mean(dxn * xn, axis=-1, keepdims=True))
        st_ref[...] += _rows8([jnp.sum(dh * xn, axis=0, keepdims=True)], n)
        finish(i == ni - 1)

    tile = pl.BlockSpec((tm, n), lambda i: (i, 0))
    outs = pl.pallas_call(
        body, name=name, grid=(ni,),
        in_specs=[pl.BlockSpec((tm, sg.shape[1]), lambda i: (i, 0)) for sg in segs]
        + [ANY, tile, pl.BlockSpec((1, n), lambda i: (0, 0)), tile] + host.in_specs,
        out_specs=[tile, pl.BlockSpec((8, n), lambda i: (0, 0))] + host.out_specs,
        out_shape=[jax.ShapeDtypeStruct((m, n), F32), jax.ShapeDtypeStruct((8, n), F32)] + host.out_shapes,
        scratch_shapes=[pltpu.VMEM((rows, n), wt.dtype), pltpu.SemaphoreType.DMA((rows // SEG_CHUNK,))] + host.scratch_shapes,
        input_output_aliases=host.aliases,
        compiler_params=_params("arbitrary"))(*segs, wt, x, g, dy, *host.arrays)
    res, landed = host.results(outs, 2)
    return (*res, landed) if ride else tuple(res)


CHUNK_ORDER = (0, 4, 7, 10, 1, 5, 8, 11, 2, 6, 9, 12, 3)
EARLY_STEPS = 8


def _side_pieces():
    table = []
    for core in (0, 1):
        sides = ([None] * len(CHUNK_ORDER), [None] * len(CHUNK_ORDER))
        for s, cc in enumerate(CHUNK_ORDER):
            g0 = cc * SEG_CHUNK
            for d in range(N_DEV):
                lo, hi = max(g0, d * SHARD_IN), min(g0 + SEG_CHUNK, (d + 1) * SHARD_IN)
                if lo < hi:
                    side = sides[0 if d % 2 == core else 1]
                    assert side[s] is None
                    side[s] = (lo - g0, hi - lo, (d // 2) * SHARD_IN + lo - d * SHARD_IN)
        table.append(sides)
    return table


def _to_chip(src, red_ref, piece, s, send_sems, recv_sems, own, core):
    _, rows, at = piece
    k, r0 = divmod(at, SHARD_IN)
    return k, pltpu.make_async_remote_copy(
        src_ref=src, dst_ref=red_ref.at[own, pl.ds(r0, rows)], send_sem=send_sems.at[s], recv_sem=recv_sems.at[own, s],
        device_id=(k // 2, k % 2, core), device_id_type=MESH)


def _from_chip(src, red_ref, piece, s, send_sems, recv_sems, chip, me):
    _, rows, at = piece
    return pltpu.make_async_remote_copy(
        src_ref=src, dst_ref=red_ref.at[chip, pl.ds(at % SHARD_IN, rows)], send_sem=send_sems.at[s], recv_sem=recv_sems.at[chip, s],
        device_id=me, device_id_type=MESH)


def _mm_tn_seg_pair(segs, b, name):
    ktok = segs[0].shape[0]
    n = b.shape[1]
    bounds = _segment_chunks(segs)
    n_seg = len(segs)
    nc = bounds[-1]
    assert nc == len(CHUNK_ORDER)
    seg_of = [s for s in range(n_seg) for _ in range(bounds[s], bounds[s + 1])]
    table = _side_pieces()

    def body(*refs):
        a_hbm, b_hbm, pair_ref, red_ref = refs[:n_seg], refs[n_seg], refs[n_seg + 1], refs[n_seg + 2]
        (a_buf, b_vmem, res_buf, recv_all, pair_all, a_sems, b_sem, send_sems, recv_sems, out_sems,
         chip_send, chip_recv) = refs[n_seg + 3:]
        step = pl.program_id(0)
        px, py, pc = _place()
        own = 2 * px + py

        def fetch(s):
            sg = seg_of[CHUNK_ORDER[s]]
            cols = pl.ds((CHUNK_ORDER[s] - bounds[sg]) * SEG_CHUNK, SEG_CHUNK)
            return pltpu.make_async_copy(a_hbm[sg].at[:, cols], a_buf.at[s % 2], a_sems.at[s % 2])

        @pl.when(step == 0)
        def _():
            whole = pltpu.make_async_copy(b_hbm, b_vmem, b_sem)
            whole.start()
            fetch(0).start()
            whole.wait()

        for s in range(nc):
            @pl.when(step == s)
            def _(s=s):
                if s + 1 < nc:
                    fetch(s + 1).start()
                fetch(s).wait()

        res_buf[step % 2] = lax.dot_general(a_buf[step % 2], b_vmem[...], TN_DIMS, preferred_element_type=F32)

        def crossing(s, piece):
            off, rows, at = piece
            return pltpu.make_async_remote_copy(
                src_ref=res_buf.at[s % 2, pl.ds(off, rows)], dst_ref=recv_all.at[pl.ds(at, rows)],
                send_sem=send_sems.at[s], recv_sem=recv_sems.at[s], device_id=(px, py, 1 - pc), device_id_type=MESH)

        def write(s, piece):
            _, rows, at = piece
            return pltpu.make_async_copy(pair_all.at[pl.ds(at, rows)], pair_ref.at[pl.ds(at, rows)], out_sems.at[s])

        def to_chip(s, piece, core):
            return _to_chip(pair_all.at[pl.ds(piece[2], piece[1])], red_ref, piece, s, chip_send, chip_recv, own, core)

        for core in (0, 1):
            mine, theirs = table[core]

            def settle(s, mine=mine, theirs=theirs, core=core):
                if theirs[s] is not None:
                    crossing(s, theirs[s]).wait_send()
                if mine[s] is not None:
                    off, rows, at = mine[s]
                    crossing(s, mine[s]).wait_recv()
                    pair_all[at:at + rows] = (res_buf[s % 2, off:off + rows] + recv_all[at:at + rows]).astype(BF16)
                    write(s, mine[s]).start()
                    if s < EARLY_STEPS:
                        k, cp = to_chip(s, mine[s], core)
                        pl.when(own != k)(cp.start)

            for s in range(nc):
                @pl.when((pc == core) & (step == s))
                def _(s=s, settle=settle, mine=mine, theirs=theirs, core=core):
                    if theirs[s] is not None:
                        crossing(s, theirs[s]).start()
                    if s > 0:
                        settle(s - 1)
                    if s == nc - 1:
                        settle(s)
                        kept = [t for t in range(nc) if mine[t] is not None]
                        for t in kept:
                            write(t, mine[t]).wait()
                        early = [t for t in kept if t < EARLY_STEPS]
                        for t in early:
                            k, cp = to_chip(t, mine[t], core)
                            pl.when(own != k)(cp.wait_send)
                        for k in range(N_CHIP):
                            @pl.when(own == k)
                            def _(k=k):
                                for t in early:
                                    if mine[t][2] // SHARD_IN == k:
                                        for chip in range(N_CHIP):
                                            if chip != k:
                                                _from_chip(pair_all.at[pl.ds(mine[t][2], mine[t][1])], red_ref, mine[t], t,
                                                           chip_send, chip_recv, chip, (px, py, pc)).wait_recv()

    flat = jax.ShapeDtypeStruct((N_CHIP * SHARD_IN, n), BF16)
    pair, red = pl.pallas_call(
        body, name=name, grid=(nc,),
        in_specs=[ANY] * (n_seg + 1), out_specs=[ANY, ANY],
        out_shape=[flat, jax.ShapeDtypeStruct((N_CHIP, SHARD_IN, n), BF16)],
        scratch_shapes=[pltpu.VMEM((2, ktok, SEG_CHUNK), segs[0].dtype), pltpu.VMEM((ktok, n), b.dtype),
                        pltpu.VMEM((2, SEG_CHUNK, n), F32), pltpu.VMEM(flat.shape, F32), pltpu.VMEM(flat.shape, BF16),
                        pltpu.SemaphoreType.DMA((2,)), pltpu.SemaphoreType.DMA,
                        pltpu.SemaphoreType.DMA((nc,)), pltpu.SemaphoreType.DMA((nc,)), pltpu.SemaphoreType.DMA((nc,)),
                        pltpu.SemaphoreType.DMA((nc,)), pltpu.SemaphoreType.DMA((N_CHIP, nc))],
        compiler_params=_params("arbitrary"))(*segs, b)
    return pair.reshape(N_CHIP, SHARD_IN, n), red


def _branches_fwd(z_rnn, z_attn, ag_ml, b_gate, w_rnn, w_attn, w_out, x, target, g_post, name, tm=512):
    t, d = x.shape
    tm = min(tm, t)

    def body(zr_ref, za_ref, lr_ref, la_ref, br_ref, ba_ref, wr_ref, wa_ref, wo_ref, x_ref, t_ref, g_ref,
             brr_ref, bra_ref, mg_ref, do_ref, dy_ref, st_ref):
        @pl.when(pl.program_id(0) == 0)
        def _():
            st_ref[...] = jnp.zeros_like(st_ref)

        br_rnn = jnp.dot(zr_ref[...], wr_ref[...], preferred_element_type=F32)
        br_attn = jnp.dot(za_ref[...], wa_ref[...], preferred_element_type=F32)
        brr_ref[...] = br_rnn.astype(BF16)
        bra_ref[...] = br_attn.astype(BF16)
        g_rnn = _sigmoid(lr_ref[...].astype(F32) + br_ref[...])
        g_attn = _sigmoid(la_ref[...].astype(F32) + ba_ref[...])
        merged = (g_rnn * br_rnn + g_attn * br_attn).astype(BF16)
        mg_ref[...] = merged
        o = jnp.dot(merged, wo_ref[...], preferred_element_type=F32)
        g = g_ref[...]
        r = lax.rsqrt(jnp.mean(o * o, axis=-1, keepdims=True) + EPS)
        nrm = o * r
        err = x_ref[...] + nrm * g - t_ref[...]
        dy = err * (1.0 / d)
        dy_ref[...] = dy
        dn = dy * g
        do_ref[...] = (r * (dn - nrm * jnp.mean(dn * nrm, axis=-1, keepdims=True))).astype(BF16)
        st_ref[...] += _rows8([jnp.sum(dy * nrm, axis=0, keepdims=True), jnp.sum(err * err, axis=0, keepdims=True)], d)

    tile = pl.BlockSpec((tm, d), lambda i: (i, 0))
    weight = pl.BlockSpec((d, d), lambda i: (0, 0))
    bf = jax.ShapeDtypeStruct((t, d), BF16)
    return pl.pallas_call(
        body, name=name, grid=(t // tm,),
        in_specs=[tile, tile, pl.BlockSpec((tm, d), lambda i: (i, 1)), pl.BlockSpec((tm, d), lambda i: (i, 2)),
                  pl.BlockSpec((1, d), lambda i: (0, 0)), pl.BlockSpec((1, d), lambda i: (0, 1)),
                  weight, weight, weight, tile, tile, pl.BlockSpec((1, d), lambda i: (0, 0))],
        out_specs=[tile, tile, tile, tile, tile, pl.BlockSpec((8, d), lambda i: (0, 0))],
        out_shape=[bf, bf, bf, bf, jax.ShapeDtypeStruct((t, d), F32), jax.ShapeDtypeStruct((8, d), F32)],
        compiler_params=_params("arbitrary"))(z_rnn, z_attn, ag_ml, ag_ml, b_gate, b_gate, w_rnn, w_attn, w_out, x, target, g_post)


def _branches_bwd(dout, br_rnn, br_attn, ag_ml, b_gate, w_rnn, w_attn, w_out, name, tm=512):
    t, d = br_rnn.shape
    tm = min(tm, t)

    def body(do_ref, r_ref, a_ref, lr_ref, la_ref, br_ref, ba_ref, wr_ref, wa_ref, wo_ref,
             dr_ref, da_ref, dl_ref, dzr_ref, dza_ref, st_ref, wt_ref):
        @pl.when(pl.program_id(0) == 0)
        def _():
            st_ref[...] = jnp.zeros_like(st_ref)
            wt_ref[0] = wo_ref[...].T
            wt_ref[1] = wr_ref[...].T
            wt_ref[2] = wa_ref[...].T

        dm = jnp.dot(do_ref[...], wt_ref[0], preferred_element_type=F32)
        g_rnn = _sigmoid(lr_ref[...].astype(F32) + br_ref[...])
        g_attn = _sigmoid(la_ref[...].astype(F32) + ba_ref[...])
        dbr_rnn = (dm * g_rnn).astype(BF16)
        dbr_attn = (dm * g_attn).astype(BF16)
        dr_ref[...] = dbr_rnn
        da_ref[...] = dbr_attn
        dl_rnn = dm * r_ref[...].astype(F32) * g_rnn * (1.0 - g_rnn)
        dl_attn = dm * a_ref[...].astype(F32) * g_attn * (1.0 - g_attn)
        dl_ref[:, 0:d] = dl_rnn.astype(BF16)
        dl_ref[:, d:2 * d] = dl_attn.astype(BF16)
        st_ref[...] += _rows8([jnp.sum(dl_rnn, axis=0, keepdims=True), jnp.sum(dl_attn, axis=0, keepdims=True)], d)
        dzr_ref[...] = jnp.dot(dbr_rnn, wt_ref[1], preferred_element_type=F32).astype(BF16)
        dza_ref[...] = jnp.dot(dbr_attn, wt_ref[2], preferred_element_type=F32).astype(BF16)

    tile = pl.BlockSpec((tm, d), lambda i: (i, 0))
    weight = pl.BlockSpec((d, d), lambda i: (0, 0))
    bf = jax.ShapeDtypeStruct((t, d), BF16)
    return pl.pallas_call(
        body, name=name, grid=(t // tm,),
        in_specs=[tile, tile, tile, pl.BlockSpec((tm, d), lambda i: (i, 1)), pl.BlockSpec((tm, d), lambda i: (i, 2)),
                  pl.BlockSpec((1, d), lambda i: (0, 0)), pl.BlockSpec((1, d), lambda i: (0, 1)), weight, weight, weight],
        out_specs=[tile, tile, pl.BlockSpec((tm, 2 * d), lambda i: (i, 0)), tile, tile, pl.BlockSpec((8, d), lambda i: (0, 0))],
        out_shape=[bf, bf, jax.ShapeDtypeStruct((t, 2 * d), BF16), bf, bf, jax.ShapeDtypeStruct((8, d), F32)],
        scratch_shapes=[pltpu.VMEM((3, d, d), BF16)],
        compiler_params=_params("arbitrary"))(dout, br_rnn, br_attn, ag_ml, ag_ml, b_gate, b_gate, w_rnn, w_attn, w_out)


def _lru_decay(r, sp):
    log_a = (-LRU_C) * r * sp
    return log_a, jnp.exp(log_a)


def _lru_gates(c, wa, ba, wx, bx, sp):
    cb = c.astype(BF16)
    r = _sigmoid(jnp.dot(cb, wa, preferred_element_type=F32) + ba)
    ig = _sigmoid(jnp.dot(cb, wx, preferred_element_type=F32) + bx)
    log_a, a = _lru_decay(r, sp)
    mult = jnp.sqrt(-jnp.tanh(log_a) * (a * a + 1.0))
    return cb, r, ig, a, mult


SUBLANES = 8


def _scan_fwd(a, u, carry, tt):
    w = a.shape[1]
    ng = tt // SUBLANES
    a3 = a.reshape(ng, SUBLANES, w)
    u3 = u.reshape(ng, SUBLANES, w)
    sub = lax.broadcasted_iota(jnp.int32, (ng, SUBLANES, w), 1)
    d = 1
    while d < SUBLANES:
        keep = sub >= d
        u3 = u3 + a3 * jnp.where(keep, pltpu.roll(u3, d, 1), 0.0)
        a3 = a3 * jnp.where(keep, pltpu.roll(a3, d, 1), 1.0)
        d *= 2
    out = []
    for g in range(ng):
        hg = u3[g] + a3[g] * carry
        out.append(hg)
        carry = hg[SUBLANES - 1:SUBLANES, :]
    return jnp.concatenate(out, axis=0)


def _scan_rev(b, g, carry, tt):
    w = b.shape[1]
    ng = tt // SUBLANES
    b3 = b.reshape(ng, SUBLANES, w)
    g3 = g.reshape(ng, SUBLANES, w)
    sub = lax.broadcasted_iota(jnp.int32, (ng, SUBLANES, w), 1)
    d = 1
    while d < SUBLANES:
        keep = sub < SUBLANES - d
        g3 = g3 + b3 * jnp.where(keep, pltpu.roll(g3, SUBLANES - d, 1), 0.0)
        b3 = b3 * jnp.where(keep, pltpu.roll(b3, SUBLANES - d, 1), 1.0)
        d *= 2
    out = [None] * ng
    for k in range(ng - 1, -1, -1):
        hk = g3[k] + b3[k] * carry
        out[k] = hk
        carry = hk[0:1, :]
    return jnp.concatenate(out, axis=0)


def _conv_taps(cw, bias, x, ext_ref, tt):
    x2 = ext_ref[7:7 + tt, :]
    x1 = ext_ref[6:6 + tt, :]
    x0 = ext_ref[5:5 + tt, :]
    return bias + cw[3:4] * x + cw[2:3] * x2 + cw[1:2] * x1 + cw[0:1] * x0


def _rnn_fwd(rx_rg, cw, cb, wa, ba, wx, bx, lam, name, tt=512):
    t = rx_rg.shape[0]
    tt = min(tt, t)
    w = GROUP_W

    nt = t // tt
    n_steps = N_GROUPS * nt

    def body(in_hbm, cw_ref, cb_ref, wa_ref, ba_ref, wx_ref, bx_ref, lam_ref,
             y_ref, z_ref, c_ref, r_ref, ig_ref, mult_ref, ext_ref, hc_ref, x_ring, g_ring, ring_sems):
        @pl.when(pl.program_id(1) == 0)
        def _():
            ext_ref[0:8, :] = jnp.zeros((8, w), F32)
            hc_ref[...] = jnp.zeros((8, w), F32)

        n = pl.program_id(0) * nt + pl.program_id(1)

        def fetch(step):
            slot = lax.rem(step, 3)
            rows = pl.ds(pl.multiple_of(lax.rem(step, nt) * tt, tt), tt)
            group = lax.div(step, nt)
            return [pltpu.make_async_copy(in_hbm.at[rows, pl.ds(pl.multiple_of((half * N_GROUPS + group) * w, w), w)],
                                          ring.at[slot], ring_sems.at[half, slot])
                    for half, ring in enumerate((x_ring, g_ring))]

        @pl.when(n == 0)
        def _():
            for cp in fetch(0) + fetch(1):
                cp.start()

        @pl.when(n + 2 < n_steps)
        def _():
            for cp in fetch(n + 2):
                cp.start()

        for cp in fetch(n):
            cp.wait()
        slot = lax.rem(n, 3)
        x = x_ring[slot]
        ext_ref[8:8 + tt, :] = x
        c = _conv_taps(cw_ref[...], cb_ref[...], x, ext_ref, tt)
        ext_ref[0:8, :] = x[tt - 8:tt, :]
        sp = _softplus(-lam_ref[...])
        _, r, ig, a, mult = _lru_gates(c, wa_ref[...], ba_ref[...], wx_ref[...], bx_ref[...], sp)
        c_ref[...] = c
        r_ref[...] = r
        ig_ref[...] = ig
        mult_ref[...] = mult
        h = _scan_fwd(a, mult * (ig * c), hc_ref[7:8, :], tt)
        hc_ref[...] = h[tt - 8:tt, :]
        y_ref[...] = h
        rg = g_ring[slot]
        z_ref[...] = (h * rg * _sigmoid(rg)).astype(BF16)

    vec = pl.BlockSpec((1, w), lambda g, i: (0, g))
    mat = pl.BlockSpec((None, w, w), lambda g, i: (g, 0, 0))
    tile = pl.BlockSpec((tt, w), lambda g, i: (i, g))
    return pl.pallas_call(
        body, name=name, grid=(N_GROUPS, nt),
        in_specs=[ANY, pl.BlockSpec((4, w), lambda g, i: (0, g)), vec, mat, vec, mat, vec, vec],
        out_specs=[tile, tile] + [pl.BlockSpec((None, tt, w), lambda g, i: (g, i, 0))] * 4,
        out_shape=[jax.ShapeDtypeStruct((t, D_RNN), F32), jax.ShapeDtypeStruct((t, D_RNN), BF16)]
        + [jax.ShapeDtypeStruct((N_GROUPS, t, w), F32)] * 4,
        scratch_shapes=[pltpu.VMEM((tt + 8, w), F32), pltpu.VMEM((8, w), F32),
                        pltpu.VMEM((3, tt, w), F32), pltpu.VMEM((3, tt, w), F32), pltpu.SemaphoreType.DMA((2, 3))],
        compiler_params=_params("arbitrary", "arbitrary"))(rx_rg, cw, cb, wa, ba, wx, bx, lam)


def _rnn_bwd(rx_rg, y, dz, c, r, ig, mult, cw, wa, wx, lam, name, tt=512):
    t = rx_rg.shape[0]
    tt = min(tt, t)
    nt = t // tt
    w = GROUP_W

    def body(rx_ref, rg_ref, y_ref, yt_ref, dz_ref, c_ref, r_ref, ig_ref, mult_ref, cw_ref, wa_ref, wx_ref, lam_ref,
             drx_ref, drg_ref, st_ref, gda_ref, gdx_ref, dcx_ref, wcar_ref, acar_ref, dwa_ref, dwx_ref):
        ii = pl.program_id(1)

        @pl.when(ii == 0)
        def _():
            wcar_ref[...] = jnp.zeros((8, w), F32)
            acar_ref[...] = jnp.zeros((8, w), F32)
            dcx_ref[tt:tt + 8, :] = jnp.zeros((8, w), F32)
            st_ref[...] = jnp.zeros_like(st_ref)
            dwa_ref[...] = jnp.zeros_like(dwa_ref)
            dwx_ref[...] = jnp.zeros_like(dwx_ref)

        has_prev = jnp.where(ii == nt - 1, 0.0, 1.0)
        cwv = cw_ref[...]
        lam = lam_ref[...]
        sp = _softplus(-lam)
        wa = wa_ref[...]
        wx = wx_ref[...]
        c = c_ref[...]
        r = r_ref[...]
        ig = ig_ref[...]
        mult = mult_ref[...]
        _, a = _lru_decay(r, sp)
        cb16 = c.astype(BF16)

        rg = rg_ref[...]
        sg = _sigmoid(rg)
        dz = dz_ref[...].astype(F32)
        yv = y_ref[...]
        drg_ref[...] = (dz * yv * (sg * (1.0 + rg * (1.0 - sg)))).astype(BF16)

        row = lax.broadcasted_iota(jnp.int32, (tt, w), 0)
        b = jnp.where(row < tt - 1, pltpu.roll(a, tt - 1, 0), acar_ref[0:1, :])
        dh = _scan_rev(b, dz * (rg * sg), wcar_ref[0:1, :], tt)
        wcar_ref[...] = dh[0:8, :]
        acar_ref[...] = a[0:8, :]

        hprev = jnp.where(row >= 1, pltpu.roll(yv, 1, 0), yt_ref[7:8, :] * has_prev)
        dmult = dh * (ig * c)
        dig = dh * mult * c
        dlog_a = dh * hprev * a - dmult * (a * a / mult)
        dpa = dlog_a * ((-LRU_C) * sp) * r * (1.0 - r)
        dpx = dig * ig * (1.0 - ig)
        dsp = jnp.sum(dlog_a * r, axis=0, keepdims=True) * (-LRU_C)
        dlam = dsp * (-_sigmoid(-lam))
        dpa16 = dpa.astype(BF16)
        dpx16 = dpx.astype(BF16)
        dwa_ref[...] += lax.dot_general(cb16, dpa16, TN_DIMS, preferred_element_type=F32)
        dwx_ref[...] += lax.dot_general(cb16, dpx16, TN_DIMS, preferred_element_type=F32)
        dc = (dh * mult * ig
              + lax.dot_general(dpa16, wa, NT_DIMS, preferred_element_type=F32)
              + lax.dot_general(dpx16, wx, NT_DIMS, preferred_element_type=F32))

        dcx_ref[0:tt, :] = dc
        dc1 = dcx_ref[1:1 + tt, :]
        dc2 = dcx_ref[2:2 + tt, :]
        dc3 = dcx_ref[3:3 + tt, :]
        drx_ref[...] = (cwv[3:4] * dc + cwv[2:3] * dc1 + cwv[1:2] * dc2 + cwv[0:1] * dc3).astype(BF16)
        dcx_ref[tt:tt + 8, :] = dc[0:8, :]

        def colsum(v):
            return jnp.sum(v, axis=0, keepdims=True)

        x = rx_ref[...]
        st_ref[...] += _rows8([colsum(dc), colsum(dpa), colsum(dpx), dlam,
                               colsum(dc3 * x), colsum(dc2 * x), colsum(dc1 * x), colsum(dc * x)], w)

        @pl.when(ii == nt - 1)
        def _():
            for blk in range(GROUP_W // RNN_BLOCK_W):
                rows = slice(blk * RNN_BLOCK_W, (blk + 1) * RNN_BLOCK_W)
                gda_ref[blk] = dwa_ref[rows, rows]
                gdx_ref[blk] = dwx_ref[rows, rows]

    def rev(ii):
        return nt - 1 - ii

    def tail(g, ii):
        return (jnp.maximum(rev(ii) * (tt // 8) - 1, 0), g)

    vec = pl.BlockSpec((1, w), lambda g, ii: (0, g))
    mat = pl.BlockSpec((None, w, w), lambda g, ii: (g, 0, 0))
    tile = pl.BlockSpec((tt, w), lambda g, ii: (rev(ii), g))
    diag_shape = (N_GROUPS, GROUP_W // RNN_BLOCK_W, RNN_BLOCK_W, RNN_BLOCK_W)
    diag = pl.BlockSpec((None,) + diag_shape[1:], lambda g, ii: (g, 0, 0, 0))
    kept = pl.BlockSpec((None, tt, w), lambda g, ii: (g, rev(ii), 0))
    return pl.pallas_call(
        body, name=name, grid=(N_GROUPS, nt),
        in_specs=[tile, pl.BlockSpec((tt, w), lambda g, ii: (rev(ii), N_GROUPS + g)),
                  tile, pl.BlockSpec((8, w), tail), tile, kept, kept, kept, kept,
                  pl.BlockSpec((4, w), lambda g, ii: (0, g)), mat, mat, vec],
        out_specs=[tile, tile, pl.BlockSpec((8, w), lambda g, ii: (0, g)), diag, diag],
        out_shape=[jax.ShapeDtypeStruct((t, D_RNN), BF16), jax.ShapeDtypeStruct((t, D_RNN), BF16),
                   jax.ShapeDtypeStruct((8, D_RNN), F32),
                   jax.ShapeDtypeStruct(diag_shape, F32), jax.ShapeDtypeStruct(diag_shape, F32)],
        scratch_shapes=[pltpu.VMEM((tt + 8, w), F32), pltpu.VMEM((8, w), F32),
                        pltpu.VMEM((8, w), F32), pltpu.VMEM((w, w), F32), pltpu.VMEM((w, w), F32)],
        compiler_params=_params("parallel", "arbitrary"))(rx_rg, rx_rg, y, y, dz, c, r, ig, mult, cw, wa, wx, lam)


def _half_mask(shape, half):
    lane = lax.broadcasted_iota(jnp.int32, shape, 1)
    return (lane >= HEAD_DIM) if half else (lane < HEAD_DIM)


def _dup_half(t, half):
    sel = jnp.where(_half_mask(t.shape, half), t, 0.0)
    return sel + pltpu.roll(sel, HEAD_DIM, 1)


def _band_geometry(n):
    qi = lax.broadcasted_iota(jnp.int32, (BLOCK, 2 * BLOCK), 0)
    kj = lax.broadcasted_iota(jnp.int32, (BLOCK, 2 * BLOCK), 1)
    dist = BLOCK + qi - kj
    first_key = jnp.where(n > 0, 0, BLOCK)
    valid = (dist >= 0) & (dist < BLOCK) & (kj >= first_key)
    return dist.astype(F32), valid


GROUP = 4


def _kv_dup(prev_ref, cur_ref, hk, scale=1.0):
    tile = hk // 2
    kt = jnp.concatenate([prev_ref[:, tile * LANE:(tile + 1) * LANE], cur_ref[:, tile * LANE:(tile + 1) * LANE]], axis=0)
    return (_dup_half(kt.astype(F32), hk % 2) * scale).astype(BF16)


def _fill_bias(bias_ref, sm_ref, n):
    distf, valid = _band_geometry(n)
    for head in range(N_Q_HEADS):
        bias_ref[head] = jnp.where(valid, -sm_ref[1, head] * distf, MASKED)


def _head_scores(q2s, half, kdup, bias):
    qm = jnp.where(_half_mask(q2s.shape, half), q2s, jnp.zeros_like(q2s))
    return qm, lax.dot_general(qm, kdup, NT_DIMS, preferred_element_type=F32) + bias


def _attn_specs(nb, clamp_last):
    def blk(n):
        return jnp.minimum(n, nb - 1) if clamp_last else n

    q_spec = pl.BlockSpec((BLOCK, D_MODEL), lambda n: (blk(n), 0))
    k_prev = pl.BlockSpec((BLOCK, D_KV), lambda n: (jnp.maximum(blk(n) - 1, 0), D_MODEL // D_KV))
    k_cur = pl.BlockSpec((BLOCK, D_KV), lambda n: (blk(n), D_MODEL // D_KV))
    v_prev = pl.BlockSpec((BLOCK, D_KV), lambda n: (jnp.maximum(blk(n) - 1, 0), D_MODEL // D_KV + 1))
    v_cur = pl.BlockSpec((BLOCK, D_KV), lambda n: (blk(n), D_MODEL // D_KV + 1))
    return q_spec, k_prev, k_cur, v_prev, v_cur


def _attn_fwd(sm, qkv, ag_ml, name, ride=None):
    t = qkv.shape[0]
    nb = t // BLOCK

    host = _Hosted(ride, 7, 3, 1)

    def body(*refs):
        (sm_ref, q_ref, kp_ref, kc_ref, vp_ref, vc_ref, ag_ref, y_ref, z_ref, lse_ref, bias_ref), start, finish = host.split(refs)
        n = pl.program_id(0)
        start(n == 0)
        start.middle(n == (3 * nb) // 4)

        @pl.when(n <= 1)
        def _():
            _fill_bias(bias_ref, sm_ref, n)

        lane = lax.broadcasted_iota(jnp.int32, (BLOCK, LANE), 1)
        low = lane < HEAD_DIM
        lse = jnp.zeros((BLOCK, LANE), F32)
        for hk in range(N_Q_HEADS // GROUP):
            kdup = _kv_dup(kp_ref, kc_ref, hk)
            vdup = _kv_dup(vp_ref, vc_ref, hk)
            for k in (0, 1):
                c = slice((2 * hk + k) * LANE, (2 * hk + k + 1) * LANE)
                q2s = q_ref[:, c] * ATTN_SCALE
                outs = []
                for half in (0, 1):
                    head = GROUP * hk + 2 * k + half
                    _, s = _head_scores(q2s, half, kdup, bias_ref[head])
                    sink = sm_ref[0, head]
                    m = jnp.maximum(jnp.max(s, axis=1, keepdims=True), sink)
                    e = jnp.exp(s - m)
                    l = jnp.sum(e, axis=1, keepdims=True) + jnp.exp(sink - m)
                    outs.append(jnp.dot((e * (1.0 / l)).astype(BF16), vdup, preferred_element_type=F32))
                    lse = jnp.where(lane == head, m + jnp.log(l), lse)
                yt = jnp.where(low, outs[0], outs[1])
                y_ref[:, c] = yt
                ag = ag_ref[:, c].astype(F32)
                z_ref[:, c] = (yt * ag * _sigmoid(ag)).astype(BF16)
        lse_ref[...] = lse
        finish(n == nb - 1)

    q_spec, k_prev, k_cur, v_prev, v_cur = _attn_specs(nb, False)
    wide = pl.BlockSpec((BLOCK, D_MODEL), lambda n: (n, 0))
    outs = pl.pallas_call(
        body, name=name, grid=(nb,),
        in_specs=[pl.BlockSpec(memory_space=pltpu.SMEM), q_spec, k_prev, k_cur, v_prev, v_cur, wide] + host.in_specs,
        out_specs=[wide, wide, pl.BlockSpec((BLOCK, LANE), lambda n: (n, 0))] + host.out_specs,
        out_shape=[jax.ShapeDtypeStruct((t, D_MODEL), F32), jax.ShapeDtypeStruct((t, D_MODEL), BF16),
                   jax.ShapeDtypeStruct((t, LANE), F32)] + host.out_shapes,
        scratch_shapes=[pltpu.VMEM((N_Q_HEADS, BLOCK, 2 * BLOCK), F32)] + host.scratch_shapes,
        compiler_params=_params("arbitrary"))(sm, qkv, qkv, qkv, qkv, qkv, ag_ml, *host.arrays)
    res, landed = host.results(outs, 3)
    return (*res, landed) if ride else tuple(res)


def _attn_bwd(sm, qkv, ag_ml, y, lse, dz, name, ride=None):
    t = qkv.shape[0]
    nb = t // BLOCK
    host = _Hosted(ride, 10, 4, 3)

    def body(*refs):
        host_refs, start, finish = host.split(refs)
        (sm_ref, q_ref, kp_ref, kc_ref, vp_ref, vc_ref, ag_ref, y_ref, lse_ref, dz_ref,
         dq_ref, dkv_ref, dag_ref, ds_ref, ck_ref, cv_ref, bias_ref) = host_refs
        n = pl.program_id(0)
        start(n == 0)
        start.middle(n == (3 * nb) // 4)

        @pl.when(n == 0)
        def _():
            ck_ref[...] = jnp.zeros_like(ck_ref)
            cv_ref[...] = jnp.zeros_like(cv_ref)
            ds_ref[...] = jnp.zeros_like(ds_ref)

        @pl.when(n <= 1)
        def _():
            _fill_bias(bias_ref, sm_ref, n)

        @pl.when(n < nb)
        def _():
            lane8 = lax.broadcasted_iota(jnp.int32, (8, LANE), 1)
            row8 = lax.broadcasted_iota(jnp.int32, (8, LANE), 0)
            dsink = jnp.zeros((8, LANE), F32)
            dk_heads, dv_heads = [], []
            lse_tile = lse_ref[...]
            for hk in range(N_Q_HEADS // GROUP):
                kdup = _kv_dup(kp_ref, kc_ref, hk)
                ks = _kv_dup(kp_ref, kc_ref, hk, ATTN_SCALE)
                vdup = _kv_dup(vp_ref, vc_ref, hk)
                qms, dyhs, y_rows = [], [], []
                for k in (0, 1):
                    c = slice((2 * hk + k) * LANE, (2 * hk + k + 1) * LANE)
                    ag = ag_ref[:, c].astype(F32)
                    sg = _sigmoid(ag)
                    dzt = dz_ref[:, c].astype(F32)
                    yt = y_ref[:, c]
                    dag_ref[:, c] = (dzt * yt * (sg * (1.0 + ag * (1.0 - sg)))).astype(BF16)
                    dyt = dzt * (ag * sg)
                    q2s = q_ref[:, c] * ATTN_SCALE
                    for half in (0, 1):
                        hm = _half_mask(q2s.shape, half)
                        qms.append(jnp.where(hm, q2s, jnp.zeros_like(q2s)))
                        dyhs.append(jnp.where(hm, dyt, 0.0))
                        y_rows.append(yt)
                qm4 = jnp.concatenate(qms, axis=0)
                dy4 = jnp.concatenate(dyhs, axis=0)
                dy4_16 = dy4.astype(BF16)
                s4 = lax.dot_general(qm4, kdup, NT_DIMS, preferred_element_type=F32)
                dp4 = lax.dot_general(dy4_16, vdup, NT_DIMS, preferred_element_type=F32)
                probs16, ds16 = [], []
                for r in range(GROUP):
                    head = GROUP * hk + r
                    rows = slice(r * BLOCK, (r + 1) * BLOCK)
                    lh = lse_tile[:, head:head + 1]
                    probs = jnp.exp(s4[rows] + bias_ref[head] - lh)
                    psink = jnp.exp(sm_ref[0, head] - lh)
                    delta = jnp.sum(dyhs[r] * y_rows[r], axis=1, keepdims=True)
                    ds16.append((probs * (dp4[rows] - delta)).astype(BF16))
                    probs16.append(probs.astype(BF16))
                    dsink = dsink + jnp.where((row8 == 0) & (lane8 == head),
                                              -jnp.sum(psink * delta, axis=0, keepdims=True), 0.0)
                ds4 = jnp.concatenate(ds16, axis=0)
                p4 = jnp.concatenate(probs16, axis=0)
                dq4 = jnp.dot(ds4, ks, preferred_element_type=F32)
                low = _half_mask((BLOCK, LANE), 0)
                for k in (0, 1):
                    c = slice((2 * hk + k) * LANE, (2 * hk + k + 1) * LANE)
                    dq_ref[:, c] = jnp.where(low, dq4[2 * k * BLOCK:(2 * k + 1) * BLOCK],
                                             dq4[(2 * k + 1) * BLOCK:(2 * k + 2) * BLOCK]).astype(BF16)
                dk_acc = lax.dot_general(ds4, qm4, TN_DIMS, preferred_element_type=F32)
                dv_acc = lax.dot_general(p4, dy4_16, TN_DIMS, preferred_element_type=F32)
                dk_heads.append(dk_acc + pltpu.roll(dk_acc, HEAD_DIM, 1))
                dv_heads.append(dv_acc + pltpu.roll(dv_acc, HEAD_DIM, 1))
            ds_ref[...] += dsink
            low = _half_mask((2 * BLOCK, LANE), 0)
            for tile in range(2):
                cols = slice(tile * LANE, (tile + 1) * LANE)
                dkt = jnp.where(low, dk_heads[2 * tile], dk_heads[2 * tile + 1])
                dvt = jnp.where(low, dv_heads[2 * tile], dv_heads[2 * tile + 1])
                dkv_ref[:, cols] = (ck_ref[:, cols] + dkt[0:BLOCK, :]).astype(BF16)
                dkv_ref[:, D_KV + tile * LANE:D_KV + (tile + 1) * LANE] = (cv_ref[:, cols] + dvt[0:BLOCK, :]).astype(BF16)
                ck_ref[:, cols] = dkt[BLOCK:2 * BLOCK, :]
                cv_ref[:, cols] = dvt[BLOCK:2 * BLOCK, :]

        @pl.when(n == nb)
        def _():
            dkv_ref[:, 0:D_KV] = ck_ref[...].astype(BF16)
            dkv_ref[:, D_KV:2 * D_KV] = cv_ref[...].astype(BF16)

        finish(n == nb)

    q_spec, k_prev, k_cur, v_prev, v_cur = _attn_specs(nb, True)
    wide = pl.BlockSpec((BLOCK, D_MODEL), lambda n: (jnp.minimum(n, nb - 1), 0))
    outs = pl.pallas_call(
        body, name=name, grid=(nb + 1,),
        in_specs=[pl.BlockSpec(memory_space=pltpu.SMEM), q_spec, k_prev, k_cur, v_prev, v_cur, wide, wide,
                  pl.BlockSpec((BLOCK, LANE), lambda n: (jnp.minimum(n, nb - 1), 0)), wide] + host.in_specs,
        out_specs=[wide, pl.BlockSpec((BLOCK, 2 * D_KV), lambda n: (jnp.maximum(n - 1, 0), 0)), wide,
                   pl.BlockSpec((8, LANE), lambda n: (0, 0))] + host.out_specs,
        out_shape=[jax.ShapeDtypeStruct((t, D_MODEL), BF16), jax.ShapeDtypeStruct((t, 2 * D_KV), BF16),
                   jax.ShapeDtypeStruct((t, D_MODEL), BF16), jax.ShapeDtypeStruct((8, LANE), F32)] + host.out_shapes,
        scratch_shapes=[pltpu.VMEM((BLOCK, D_KV), F32), pltpu.VMEM((BLOCK, D_KV), F32),
                        pltpu.VMEM((N_Q_HEADS, BLOCK, 2 * BLOCK), F32)] + host.scratch_shapes,
        compiler_params=_params("arbitrary"))(sm, qkv, qkv, qkv, qkv, qkv, ag_ml, y, lse, dz, *host.arrays)
    res, landed = host.results(outs, 4)
    return (*res, landed) if ride else tuple(res)


def _local_grads(x, target, p, project, late_weights, reduce_out, reduce_in):
    h, rx_rg, qkv, ag_ml, wt = project(x, p["pre_g"])
    y_attn, z_attn, lse, landed = _attn_fwd(p["sm"], qkv, ag_ml, "attn_fwd", ride=late_weights[0])
    p = {**p, **late_weights[1](landed)}
    y_rnn, z_rnn, *kept_rnn = _rnn_fwd(
        rx_rg, p["cw"], p["cb"], p["wbd_a"], p["b_a"], p["wbd_x"], p["b_x"], p["lam"], "rnn_fwd")
    br_rnn, br_attn, merged, dout, dy, st_post = _branches_fwd(
        z_rnn, z_attn, ag_ml, p["b_gate"], p["w_rnn"], p["w_attn"], p["w_out"], x, target, p["post_g"], "branches_fwd")

    dbr_rnn, dbr_attn, d_ml, dz_rnn, dz_attn, st_merge = _branches_bwd(
        dout, br_rnn, br_attn, ag_ml, p["b_gate"], p["w_rnn"], p["w_attn"], p["w_out"], "branches_bwd")
    out_pairs = _mm_tn_pairs([(z_rnn, dbr_rnn), (z_attn, dbr_attn), (merged, dout)], "gw_outs")
    d_rx, d_rg, st_rnn, g_rg_a, g_rg_x = _rnn_bwd(
        rx_rg, y_rnn, dz_rnn, *kept_rnn, p["cw"], p["wbd_a"], p["wbd_x"], p["lam"], "rnn_bwd")
    dq, dkv, d_ag, st_sink, red_out = _attn_bwd(p["sm"], qkv, ag_ml, y_attn, lse, dz_attn, "attn_bwd",
                                                ride=reduce_out(out_pairs, g_rg_a, g_rg_x))

    segs = [d_rx, d_rg, dq, dkv, d_ag, d_ml]
    grad_x, st_pre, red_in = _input_grad(segs, wt, x, p["pre_g"], dy, "input_grad",
                                         ride=reduce_in(*_mm_tn_seg_pair(segs, h, "gw_in")))
    return dict(grad_x=grad_x, st_post=st_post, st_merge=st_merge, st_rnn=st_rnn, st_sink=st_sink, st_pre=st_pre,
                red_out=red_out, red_in=red_in)


def _place():
    x, y, c = lax.axis_index("x"), lax.axis_index("y"), lax.axis_index("c")
    return x, y, c


def _gather_ride(shards):
    n = len(shards)

    def copies(ins, outs, sems):
        send_sems, recv_sems, local_sems = sems
        x, y, c = _place()
        me, sibling = (x, y, c), (x, y, 1 - c)
        chips = [(1 - x, y), (x, 1 - y), (1 - x, 1 - y)]

        def slot(a, dev):
            return outs[a].at[4 * dev[0] + 2 * dev[1] + dev[2]]

        def copy(a, k, block, to, src=None):
            return pltpu.make_async_remote_copy(
                src_ref=slot(a, block) if src is None else src, dst_ref=slot(a, block),
                send_sem=send_sems.at[a, k], recv_sem=recv_sems.at[a, k], device_id=to, device_id_type=MESH)

        mine = [pltpu.make_async_copy(ins[a], slot(a, me), local_sems.at[a]) for a in range(n)]
        first = []
        for a in range(n):
            first.append(copy(a, 0, me, sibling, src=ins[a]))
            first += [copy(a, 1 + j, me, (*chip, c), src=ins[a]) for j, chip in enumerate(chips)]
        return me, sibling, chips, c, copy, mine, first

    def start(ins, outs, sems):
        *_, mine, first = copies(ins, outs, sems)
        for cp in mine + first:
            cp.start()

    def middle(ins, outs, sems):
        me, sibling, chips, c, copy, _, _ = copies(ins, outs, sems)
        for j, chip in enumerate(chips):
            for a in range(n):
                copy(a, 1 + j, (*chip, c), me).wait_recv()
                copy(a, 4 + j, (*chip, c), sibling).start()

    def finish(ins, outs, sems):
        me, sibling, chips, c, copy, mine, first = copies(ins, outs, sems)
        passed = [copy(a, 4 + j, (*chip, c), sibling) for j, chip in enumerate(chips) for a in range(n)]
        for a in range(n):
            copy(a, 0, sibling, me).wait_recv()
            for j, chip in enumerate(chips):
                copy(a, 4 + j, (*chip, 1 - c), me).wait_recv()
        for cp in first + passed:
            cp.wait_send()
        for cp in mine:
            cp.wait()

    return _Ride(
        shards, [jax.ShapeDtypeStruct((N_DEV, *s.shape), s.dtype) for s in shards],
        [pltpu.SemaphoreType.DMA((n, 7)), pltpu.SemaphoreType.DMA((n, 7)), pltpu.SemaphoreType.DMA((n,))],
        start, finish, middle)


def _chips_ride(scatter):
    n = len(scatter)

    def copies(ins, outs, sems):
        send_sems, recv_sems, local_sems = sems
        x, y, c = _place()
        own = 2 * x + y
        chips = [(1 - x, y), (x, 1 - y), (1 - x, 1 - y)]
        local = [pltpu.make_async_copy(ins[a].at[own], outs[a].at[own], local_sems.at[a]) for a in range(n)]
        sent = [pltpu.make_async_remote_copy(
            src_ref=ins[a].at[2 * chip[0] + chip[1]], dst_ref=outs[a].at[own],
            send_sem=send_sems.at[a, j], recv_sem=recv_sems.at[a, own], device_id=(*chip, c), device_id_type=MESH)
            for a in range(n) for j, chip in enumerate(chips)]
        return chips, c, local, sent

    def start(ins, outs, sems):
        _, _, local, sent = copies(ins, outs, sems)
        for cp in local + sent:
            cp.start()

    def finish(ins, outs, sems):
        send_sems, recv_sems, _ = sems
        chips, c, local, sent = copies(ins, outs, sems)
        for a in range(n):
            for chip in chips:
                k = 2 * chip[0] + chip[1]
                pltpu.make_async_remote_copy(
                    src_ref=outs[a].at[k], dst_ref=outs[a].at[k], send_sem=send_sems.at[a, 0],
                    recv_sem=recv_sems.at[a, k], device_id=(*chip, c), device_id_type=MESH).wait_recv()
        for cp in sent:
            cp.wait_send()
        for cp in local:
            cp.wait()

    return _Ride(
        list(scatter), [jax.ShapeDtypeStruct(s.shape, s.dtype) for s in scatter],
        [pltpu.SemaphoreType.DMA((n, 3)), pltpu.SemaphoreType.DMA((n, N_CHIP)), pltpu.SemaphoreType.DMA((n,))],
        start, finish)


def _chips_rest_ride(pair, red):
    table = _side_pieces()
    nc = len(CHUNK_ORDER)

    def each(ins, outs, sems, sending, landing):
        send_sems, recv_sems, _ = sems
        pair_ref, red_ref = ins[0], outs[0]
        x, y, c = _place()
        own = 2 * x + y
        for core in (0, 1):
            mine = table[core][0]
            rest = [s for s in range(EARLY_STEPS, nc) if mine[s] is not None]

            @pl.when(c == core)
            def _(mine=mine, rest=rest, core=core):
                for s in rest:
                    _, rows, at = mine[s]
                    k, cp = _to_chip(pair_ref.at[at // SHARD_IN, pl.ds(at % SHARD_IN, rows)], red_ref, mine[s], s,
                                     send_sems, recv_sems, own, core)
                    pl.when(own != k)(lambda cp=cp: sending(cp))
                if landing is not None:
                    for k in range(N_CHIP):
                        @pl.when(own == k)
                        def _(k=k):
                            for s in rest:
                                _, rows, at = mine[s]
                                if at // SHARD_IN == k:
                                    for chip in range(N_CHIP):
                                        if chip != k:
                                            landing(_from_chip(pair_ref.at[k, pl.ds(at % SHARD_IN, rows)], red_ref, mine[s], s,
                                                               send_sems, recv_sems, chip, (x, y, c)))

    def local(ins, outs, sems):
        x, y, _ = _place()
        return pltpu.make_async_copy(ins[0].at[2 * x + y], outs[0].at[2 * x + y], sems[2])

    def start(ins, outs, sems):
        local(ins, outs, sems).start()
        each(ins, outs, sems, lambda cp: cp.start(), None)

    def finish(ins, outs, sems):
        each(ins, outs, sems, lambda cp: cp.wait_send(), lambda cp: cp.wait_recv())
        local(ins, outs, sems).wait()

    return _Ride([pair, red], [jax.ShapeDtypeStruct(red.shape, red.dtype)],
                 [pltpu.SemaphoreType.DMA((nc,)), pltpu.SemaphoreType.DMA((N_CHIP, nc)), pltpu.SemaphoreType.DMA],
                 start, finish, aliases={1: 0})


def _allreduce_small(pack, name):
    shape = pack.shape

    def body(x_ref, o_ref, sib_ref, chip_ref, send_sems, recv_sems):
        x, y, c = _place()
        own = 2 * x + y
        chips = [(1 - x, y), (x, 1 - y), (1 - x, 1 - y)]
        to_sibling = pltpu.make_async_remote_copy(
            src_ref=x_ref, dst_ref=sib_ref, send_sem=send_sems.at[0], recv_sem=recv_sems.at[0],
            device_id=(x, y, 1 - c), device_id_type=MESH)
        to_sibling.start()
        to_sibling.wait()
        chip_ref[own] = x_ref[...] + sib_ref[...]
        sent = [pltpu.make_async_remote_copy(
            src_ref=chip_ref.at[own], dst_ref=chip_ref.at[own], send_sem=send_sems.at[1 + j],
            recv_sem=recv_sems.at[1 + own], device_id=(*chip, c), device_id_type=MESH) for j, chip in enumerate(chips)]
        for cp in sent:
            cp.start()
        for chip in chips:
            k = 2 * chip[0] + chip[1]
            pltpu.make_async_remote_copy(
                src_ref=chip_ref.at[k], dst_ref=chip_ref.at[k], send_sem=send_sems.at[1],
                recv_sem=recv_sems.at[1 + k], device_id=(*chip, c), device_id_type=MESH).wait_recv()
        for cp in sent:
            cp.wait_send()
        o_ref[...] = (chip_ref[0] + chip_ref[1]) + (chip_ref[2] + chip_ref[3])

    return pl.pallas_call(
        body, name=name, out_shape=jax.ShapeDtypeStruct(shape, F32),
        in_specs=[pl.BlockSpec(memory_space=pltpu.VMEM)], out_specs=pl.BlockSpec(memory_space=pltpu.VMEM),
        scratch_shapes=[pltpu.VMEM(shape, F32), pltpu.VMEM((N_CHIP, *shape), F32),
                        pltpu.SemaphoreType.DMA((4,)), pltpu.SemaphoreType.DMA((1 + N_CHIP,))],
    )(pack)


def _adamw(g, w, m, v):
    m = ADAM_B1 * m + (1.0 - ADAM_B1) * g
    v = ADAM_B2 * v + (1.0 - ADAM_B2) * (g * g)
    m_hat = m / (1.0 - ADAM_B1 ** ADAM_STEP)
    v_hat = v / (1.0 - ADAM_B2 ** ADAM_STEP)
    delta = -ADAM_LR * (m_hat / (jnp.sqrt(v_hat) + ADAM_EPS) + ADAM_WD * w)
    return delta, m, v


def _adam_parts(items, name, tr=None):
    ni = len(items)
    npart, r, c = items[0][0].shape
    tr = r if tr is None else min(tr, r)

    def body(*refs):
        for a in range(ni):
            p_ref, w_ref, m_ref, v_ref = refs[4 * a:4 * a + 4]
            g_ref, d_ref, nm_ref, nv_ref = refs[4 * (ni + a):4 * (ni + a) + 4]
            g = p_ref[0].astype(F32)
            for k in range(1, npart):
                g = g + p_ref[k].astype(F32)
            g_ref[...] = g
            d_ref[...], nm_ref[...], nv_ref[...] = _adamw(g, w_ref[...], m_ref[...], v_ref[...])

    tile = pl.BlockSpec((tr, c), lambda i: (i, 0))
    outs = pl.pallas_call(
        body, name=name, grid=(r // tr,),
        in_specs=[pl.BlockSpec((npart, tr, c), lambda i: (0, i, 0)), tile, tile, tile] * ni,
        out_specs=[tile] * (4 * ni), out_shape=[jax.ShapeDtypeStruct((r, c), F32)] * (4 * ni),
        compiler_params=_params("parallel"))(*[arr for item in items for arr in item])
    return [outs[4 * a:4 * a + 4] for a in range(ni)]


def _block_diag(w):
    w4 = w.reshape(N_GROUPS, 4, RNN_BLOCK_W, RNN_BLOCK_W)
    eye = jnp.eye(4, dtype=w.dtype)
    return jnp.einsum("gbij,bc->gbicj", w4, eye).reshape(N_GROUPS, GROUP_W, GROUP_W).astype(BF16)


SMALL_ROWS = 16
ROW_PRE_G, ROW_BGATE, ROW_CONV_B, ROW_B_A, ROW_B_X, ROW_LAM, ROW_POST_G, ROW_LOSS, ROW_SINKS, ROW_CONV_W = 0, 1, 3, 4, 5, 6, 7, 8, 9, 10


def _pack_stats(st_pre, st_merge, st_rnn, st_post, st_sink, name):
    d = D_MODEL

    def body(pre_ref, mg_ref, rnn_ref, post_ref, sink_ref, o_ref):
        rnn = rnn_ref[...]
        sinks = jnp.concatenate([sink_ref[0:1, :], jnp.zeros((1, d - LANE), F32)], axis=1)
        o_ref[0:8, :] = _rows8([pre_ref[0:1, :], mg_ref[0:1, :], mg_ref[1:2, :], rnn[0:1], rnn[1:2], rnn[2:3], rnn[3:4],
                                post_ref[0:1, :]], d)
        o_ref[8:16, :] = _rows8([post_ref[1:2, :], sinks, rnn[4:5], rnn[5:6], rnn[6:7], rnn[7:8]], d)

    return pl.pallas_call(body, name=name, out_shape=jax.ShapeDtypeStruct((SMALL_ROWS, d), F32))(
        st_pre, st_merge, st_rnn, st_post, st_sink)


def _adam_small(total, w, m, v, name):
    d = D_MODEL
    n_in = len(w)
    rows = (ROW_PRE_G, ROW_BGATE, ROW_CONV_B, ROW_B_A, ROW_B_X, ROW_LAM, ROW_POST_G, ROW_SINKS)

    def grad_of(p_ref, row, shape):
        if shape == (1, 2 * d):
            return jnp.concatenate([p_ref[row:row + 1, :], p_ref[row + 1:row + 2, :]], axis=1)
        if shape == (1, RNN_BLOCKS, RNN_BLOCK_W):
            r = p_ref[row:row + 1, :]
            return jnp.concatenate([r[:, b * RNN_BLOCK_W:(b + 1) * RNN_BLOCK_W] for b in range(RNN_BLOCKS)], axis=0)[None]
        return p_ref[row:row + 1, 0:shape[1]]

    def body(*refs):
        p_ref = refs[0]
        w_refs, m_refs, v_refs = (refs[1 + k * n_in:1 + (k + 1) * n_in] for k in range(3))
        outs = refs[1 + 3 * n_in:]
        for k in range(n_in):
            g = grad_of(p_ref, rows[k], w[k].shape)
            res = (g,) + _adamw(g, w_refs[k][...], m_refs[k][...], v_refs[k][...])
            for kind in range(4):
                outs[kind * n_in + k][...] = res[kind]

    outs = pl.pallas_call(
        body, name=name, out_shape=[jax.ShapeDtypeStruct(a.shape, F32) for _ in range(4) for a in w])(total, *w, *m, *v)
    return [outs[kind * n_in:(kind + 1) * n_in] for kind in range(4)]


def kernel(x, pre_norm_g, w_in, b_gate, conv_w, conv_b, w_rg_a, b_rg_a, w_rg_x, b_rg_x, lru_lambda, attn_sinks, w_rnn_out, w_attn_out, w_out, post_norm_g, loss_target, m_pre_norm_g, m_w_in, m_b_gate, m_conv_w, m_conv_b, m_w_rg_a, m_b_rg_a, m_w_rg_x, m_b_rg_x, m_lru_lambda, m_attn_sinks, m_w_rnn_out, m_w_attn_out, m_w_out, m_post_norm_g, v_pre_norm_g, v_w_in, v_b_gate, v_conv_w, v_conv_b, v_w_rg_a, v_b_rg_a, v_w_rg_x, v_b_rg_x, v_lru_lambda, v_attn_sinks, v_w_rnn_out, v_w_attn_out, v_w_out, v_post_norm_g):
    cx, cy, cc = _place()
    dev = 4 * cx + 2 * cy + cc

    w_in_t, m_in_t, v_in_t = (jnp.transpose(a[0]) for a in (w_in, m_w_in, v_w_in))
    wt_shard = w_in_t.astype(BF16)

    def project(xs, pre_g):
        h, rx_rg, qkv, ag_ml, wt_all = _gather_project(xs, pre_g, wt_shard, "gather_project")
        return h, rx_rg, qkv, ag_ml, wt_all.reshape(D_IN, D_MODEL)

    def late_unpack(landed):
        w_rnn_all, w_attn_all, w_out_all, cw_all = landed
        return dict(w_rnn=w_rnn_all.reshape(D_RNN, D_MODEL), w_attn=w_attn_all.reshape(D_MODEL, D_MODEL),
                    w_out=w_out_all.reshape(D_MODEL, D_MODEL), cw=jnp.transpose(cw_all, (1, 0, 2)).reshape(4, D_RNN))

    late_weights = (_gather_ride([w_rnn_out[0].astype(BF16), w_attn_out[0].astype(BF16), w_out[0].astype(BF16), conv_w[0]]),
                    late_unpack)

    heads = jnp.arange(1, N_Q_HEADS + 1, dtype=F32)
    slopes = jnp.exp2(-ALIBI_MAX_BIAS * heads / N_Q_HEADS)
    b_a = b_rg_a.reshape(1, D_RNN)
    b_x = b_rg_x.reshape(1, D_RNN)
    p = dict(
        pre_g=pre_norm_g, post_g=post_norm_g, b_gate=b_gate, cb=conv_b,
        wbd_a=_block_diag(w_rg_a[0]), b_a=b_a, wbd_x=_block_diag(w_rg_x[0]), b_x=b_x, lam=lru_lambda,
        sm=jnp.pad(attn_sinks, ((0, 1), (0, 0))) + jnp.pad(slopes[None, :], ((1, 0), (0, 0))))

    flat = (RNN_BLOCKS * RNN_BLOCK_W, RNN_BLOCK_W)

    def reduce_out(out_pairs, g_rg_a, g_rg_x):
        return _join_rides(_chips_ride(out_pairs), _gather_ride([g_rg_a.reshape(flat), g_rg_x.reshape(flat)]))

    reduce_in = _chips_rest_ride

    g = _local_grads(x[0], loss_target[0], p, project, late_weights, reduce_out, reduce_in)
    small = _allreduce_small(
        _pack_stats(g["st_pre"], g["st_merge"], g["st_rnn"], g["st_post"], g["st_sink"], "pack_stats"), "allreduce_small")

    out = {}
    red = g["red_out"]
    w_in_out, = _adam_parts([(g["red_in"][0], w_in_t, m_in_t, v_in_t)], "adam_w_in", tr=SHARD_IN // 2)
    out["w_in"] = [jnp.transpose(o) for o in w_in_out]
    out["w_rnn_out"], out["w_attn_out"], out["w_out"] = _adam_parts(
        [(red[0], w_rnn_out[0], m_w_rnn_out[0], v_w_rnn_out[0]), (red[1], w_attn_out[0], m_w_attn_out[0], v_w_attn_out[0]),
         (red[2], w_out[0], m_w_out[0], v_w_out[0])], "adam_w_outs")
    out["w_rg_a"], out["w_rg_x"] = _adam_parts(
        [(red[3], w_rg_a.reshape(flat), m_w_rg_a.reshape(flat), v_w_rg_a.reshape(flat)),
         (red[4], w_rg_x.reshape(flat), m_w_rg_x.reshape(flat), v_w_rg_x.reshape(flat))], "adam_w_rg")

    small_names = ("pre_norm_g", "b_gate", "conv_b", "b_rg_a", "b_rg_x", "lru_lambda", "post_norm_g", "attn_sinks")
    small_out = _adam_small(
        small,
        (pre_norm_g, b_gate, conv_b, b_rg_a, b_rg_x, lru_lambda, post_norm_g, attn_sinks),
        (m_pre_norm_g, m_b_gate, m_conv_b, m_b_rg_a, m_b_rg_x, m_lru_lambda, m_post_norm_g, m_attn_sinks),
        (v_pre_norm_g, v_b_gate, v_conv_b, v_b_rg_a, v_b_rg_x, v_lru_lambda, v_post_norm_g, v_attn_sinks),
        "adam_small")
    g_cw = lax.dynamic_slice(small[ROW_CONV_W:ROW_CONV_W + 4], (0, dev * SHARD_OUT), (4, SHARD_OUT))
    out["conv_w"], = _adam_parts([(g_cw[None], conv_w[0], m_conv_w[0], v_conv_w[0])], "adam_conv_w")

    shapes = dict(w_in=(1, D_MODEL, SHARD_IN), w_rnn_out=(1, SHARD_OUT, D_MODEL), w_attn_out=(1, SHARD_OUT, D_MODEL),
                  w_out=(1, SHARD_OUT, D_MODEL), w_rg_a=(1, RNN_BLOCKS, RNN_BLOCK_W, RNN_BLOCK_W),
                  w_rg_x=(1, RNN_BLOCKS, RNN_BLOCK_W, RNN_BLOCK_W), conv_w=(1, 4, SHARD_OUT))
    weights = ["pre_norm_g", "w_in", "b_gate", "conv_w", "conv_b", "w_rg_a", "b_rg_a", "w_rg_x", "b_rg_x",
               "lru_lambda", "attn_sinks", "w_rnn_out", "w_attn_out", "w_out", "post_norm_g"]
    results = []
    for kind in range(4):
        for name in weights:
            if name in out:
                results.append(out[name][kind].reshape(shapes[name]))
            else:
                results.append(small_out[kind][small_names.index(name)])
    loss = 0.5 / D_MODEL * jnp.sum(small[ROW_LOSS])
    return (loss, g["grad_x"][None], *results)
```

```python
import jax
import jax.numpy as jnp
from jax import lax
from jax.experimental import pallas as pl
from jax.experimental.pallas import tpu as pltpu

F32 = jnp.float32
BF16 = jnp.bfloat16

D_MODEL = 1024
D_RNN = 1024
RNN_BLOCKS = 16
RNN_BLOCK_W = 64
LRU_C = 8.0
N_Q_HEADS = 16
HEAD_DIM = 64
D_KV = 256
BLOCK = 128
ALIBI_MAX_BIAS = 8.0
EPS = 1e-6
D_IN = 6656
N_DEV = 8
N_CHIP = 4
SHARD_IN = D_IN // N_DEV
SHARD_OUT = D_MODEL // N_DEV
ATTN_SCALE = HEAD_DIM ** -0.5
MASKED = -1e30

ADAM_LR = 0.001
ADAM_B1 = 0.9
ADAM_B2 = 0.999
ADAM_EPS = 1e-08
ADAM_WD = 0.01
ADAM_STEP = 10

VMEM_LIMIT_BYTES = 52 * 1024 * 1024
LANE = 128
GROUP_W = 256
N_GROUPS = D_RNN // GROUP_W
SEG_CHUNK = 512

NT_DIMS = (((1,), (1,)), ((), ()))
TN_DIMS = (((0,), (0,)), ((), ()))
MESH = pl.DeviceIdType.MESH
ANY = pl.BlockSpec(memory_space=pl.ANY)


def _params(*semantics):
    return pltpu.CompilerParams(dimension_semantics=semantics, vmem_limit_bytes=VMEM_LIMIT_BYTES)


def _sigmoid(x):
    return 0.5 * jnp.tanh(0.5 * x) + 0.5


def _log1p(e):
    u = 1.0 + e
    den = jnp.where(u == 1.0, 1.0, u - 1.0)
    return jnp.where(u == 1.0, e, jnp.log(u) * (e / den))


def _softplus(z):
    return jnp.maximum(z, 0.0) + _log1p(jnp.exp(-jnp.abs(z)))


def _rows8(rows, width):
    idx = lax.broadcasted_iota(jnp.int32, (8, width), 0)
    out = jnp.zeros((8, width), F32)
    for r, v in enumerate(rows):
        out = jnp.where(idx == r, v, out)
    return out


class _Ride:
    def __init__(self, arrays, out_shapes, scratch_shapes, start, finish, middle=None, aliases=None):
        self.arrays, self.out_shapes, self.scratch_shapes = list(arrays), list(out_shapes), list(scratch_shapes)
        self.start, self.finish, self.middle = start, finish, middle
        self.aliases = dict(aliases or {})


class _Hosted:
    def __init__(self, ride, n_in, n_out, n_scratch=0, aliasing=False):
        self.ride = ride
        assert aliasing or not (ride and ride.aliases), "this host does not alias"
        self.aliases = {n_in + i: n_out + o for i, o in ride.aliases.items()} if ride else {}
        self.sizes = (n_in, len(ride.arrays) if ride else 0, n_out, len(ride.out_shapes) if ride else 0, n_scratch)
        self.arrays = ride.arrays if ride else []
        self.in_specs = [ANY] * len(self.arrays)
        self.out_shapes = ride.out_shapes if ride else []
        self.out_specs = [ANY] * len(self.out_shapes)
        self.scratch_shapes = ride.scratch_shapes if ride else []

    def split(self, refs):
        n_in, r_in, n_out, r_out, n_scr = self.sizes
        cuts = [0, n_in, n_in + r_in, n_in + r_in + n_out, n_in + r_in + n_out + r_out, n_in + r_in + n_out + r_out + n_scr]
        host_in, ride_in, host_out, ride_out, host_scr = (refs[cuts[k]:cuts[k + 1]] for k in range(5))
        ride_scr = refs[cuts[5]:]

        def start(when):
            if self.ride is not None:
                pl.when(when)(lambda: self.ride.start(ride_in, ride_out, ride_scr))

        def finish(when):
            if self.ride is not None:
                pl.when(when)(lambda: self.ride.finish(ride_in, ride_out, ride_scr))

        def middle(when):
            if self.ride is not None and self.ride.middle is not None:
                pl.when(when)(lambda: self.ride.middle(ride_in, ride_out, ride_scr))

        start.middle = middle
        return tuple(host_in) + tuple(host_out) + tuple(host_scr), start, finish

    def results(self, outs, n_out):
        outs = list(outs) if isinstance(outs, (list, tuple)) else [outs]
        return outs[:n_out], outs[n_out:]


def _join_rides(a, b):
    na, nb_ = len(a.arrays), len(b.arrays)
    oa = len(a.out_shapes)
    sa = len(a.scratch_shapes)

    def both(fa, fb):
        def run(ins, outs, sems):
            if fa is not None:
                fa(ins[:na], outs[:oa], sems[:sa])
            if fb is not None:
                fb(ins[na:na + nb_], outs[oa:], sems[sa:])
        return run

    middle = both(a.middle, b.middle) if (a.middle or b.middle) else None
    return _Ride(a.arrays + b.arrays, a.out_shapes + b.out_shapes, a.scratch_shapes + b.scratch_shapes,
                 both(a.start, b.start), both(a.finish, b.finish), middle)


def _load_resident(w_hbm, w_vmem, sems, first):
    def piece(c):
        rows = pl.ds(c * SEG_CHUNK, SEG_CHUNK)
        return pltpu.make_async_copy(w_hbm.at[rows], w_vmem.at[rows], sems.at[c])

    @pl.when(first)
    def _():
        for c in range(w_vmem.shape[0] // SEG_CHUNK):
            piece(c).start()

    def ready(c):
        @pl.when(first)
        def _():
            piece(c).wait()

    return ready


CHIP_ROWS = 2 * SHARD_IN
PROJ_WIDTHS = (2 * D_RNN, D_MODEL + 2 * D_KV, 3 * D_MODEL)
PROJ_DTYPES = (F32, BF16, BF16)


def _chip_pieces():
    starts = [0, PROJ_WIDTHS[0], PROJ_WIDTHS[0] + PROJ_WIDTHS[1], D_IN]
    pieces = []
    for k in range(N_CHIP):
        lo, hi = k * CHIP_ROWS, (k + 1) * CHIP_ROWS
        cur = []
        for a in range(len(PROJ_WIDTHS)):
            s0, s1 = max(lo, starts[a]), min(hi, starts[a + 1])
            if s0 < s1:
                cur.append((a, s0 - starts[a], s1 - s0, s0 - lo))
        pieces.append(cur)
    return pieces


def _gather_project(x, g, wt_shard, name, tm=1024):
    t, k = x.shape
    tm = min(tm, t)
    nt = t // tm
    pieces = _chip_pieces()

    def body(x_ref, g_ref, shard_ref, h_out, rx_ref, qkv_ref, ag_ref, wt_all,
             w_c, stage, o32, o16, h_all, send_sems, recv_sems, local_sem, stage_sems, out_sems, h_sems):
        s, ti = pl.program_id(0), pl.program_id(1)
        px, py, pc = _place()
        me, sibling = (px, py, pc), (px, py, 1 - pc)
        chips = [(px, py), (1 - px, py), (px, 1 - py), (1 - px, 1 - py)]
        outs = (rx_ref, qkv_ref, ag_ref)

        def slot(dev):
            return wt_all.at[4 * dev[0] + 2 * dev[1] + dev[2]]

        def copy(kk, block, to, src=None):
            return pltpu.make_async_remote_copy(
                src_ref=slot(block) if src is None else src, dst_ref=slot(block),
                send_sem=send_sems.at[kk], recv_sem=recv_sems.at[kk], device_id=to, device_id_type=MESH)

        mine = pltpu.make_async_copy(shard_ref, slot(me), local_sem)
        first = [copy(0, me, sibling, src=shard_ref)] + [copy(1 + j, me, (*chips[1 + j], pc), src=shard_ref) for j in range(3)]
        passed = [copy(4 + j, (*chips[1 + j], pc), sibling) for j in range(3)]

        @pl.when((s == 0) & (ti == 0))
        def _():
            mine.start()
            for cp in first[:3]:
                cp.start()

        def pass_on(step):
            copy(step, (*chips[step], pc), me).wait_recv()
            passed[step - 1].start()

        @pl.when((s == 2) & (ti == nt - 1))
        def _():
            pass_on(3)

        for step in range(N_CHIP):
            @pl.when((s == step) & (ti == 0))
            def _(step=step):
                chip = chips[step]
                if step == 0:
                    mine.wait()
                    copy(0, sibling, me).wait_recv()
                else:
                    if step == 1:
                        pass_on(1)
                        first[3].start()
                        pass_on(2)
                    copy(3 + step, (*chip, 1 - pc), me).wait_recv()
                loads = [pltpu.make_async_copy(slot((*chip, core)), stage.at[pl.ds(core * SHARD_IN, SHARD_IN)], stage_sems.at[core])
                         for core in (0, 1)]
                for cp in loads:
                    cp.start()
                for cp in loads:
                    cp.wait()
                w_c[...] = stage[...].T

        rows = pl.ds(pl.multiple_of(ti * tm, tm), tm)

        @pl.when(s == 0)
        def _():
            xv = x_ref[...]
            h_all[rows, :] = (xv * lax.rsqrt(jnp.mean(xv * xv, axis=-1, keepdims=True) + EPS) * g_ref[...]).astype(BF16)

        res = jnp.dot(h_all[rows, :], w_c[...], preferred_element_type=F32)

        n = s * nt + ti
        buf = lax.rem(n, 2)
        chip_idx = [2 * cx + cy for cx, cy in chips]

        def chip_at(step):
            return jnp.where(step == 0, chip_idx[0], jnp.where(step == 1, chip_idx[1], jnp.where(step == 2, chip_idx[2], chip_idx[3])))

        def h_write(b, tile):
            return pltpu.make_async_copy(h_all.at[pl.ds(tile * tm, tm)], h_out.at[pl.ds(tile * tm, tm)], h_sems.at[b])

        def writes(kchip, b, tile):
            cps = []
            for idx, (a, col, w, src) in enumerate(pieces[kchip]):
                staged = (o16 if PROJ_DTYPES[a] == BF16 else o32).at[b, :, pl.ds(src, w)]
                cps.append(pltpu.make_async_copy(staged, outs[a].at[pl.ds(tile * tm, tm), pl.ds(col, w)], out_sems.at[b, idx]))
            return cps

        o32[buf] = res
        o16[buf] = res.astype(BF16)
        kcur = chip_at(s)

        @pl.when(n > 0)
        def _():
            kprev = chip_at(lax.div(n - 1, nt))
            for kchip in range(N_CHIP):
                @pl.when(kprev == kchip)
                def _(kchip=kchip):
                    for cp in writes(kchip, 1 - buf, lax.rem(n - 1, nt)):
                        cp.wait()

            @pl.when(n <= nt)
            def _():
                h_write(1 - buf, n - 1).wait()

        @pl.when(s == 0)
        def _():
            h_write(buf, ti).start()

        for kchip in range(N_CHIP):
            @pl.when(kcur == kchip)
            def _(kchip=kchip):
                for cp in writes(kchip, buf, ti):
                    cp.start()

        @pl.when(n == N_CHIP * nt - 1)
        def _():
            for kchip in range(N_CHIP):
                @pl.when(kcur == kchip)
                def _(kchip=kchip):
                    for cp in writes(kchip, buf, ti):
                        cp.wait()
            for cp in first + passed:
                cp.wait_send()

    tile = pl.BlockSpec((tm, k), lambda s, ti: (jnp.where(s == 0, ti, nt - 1), 0))
    return pl.pallas_call(
        body, name=name, grid=(N_CHIP, nt),
        in_specs=[tile, pl.BlockSpec((1, k), lambda s, ti: (0, 0)), ANY],
        out_specs=[ANY, ANY, ANY, ANY, ANY],
        out_shape=[jax.ShapeDtypeStruct((t, k), BF16)]
        + [jax.ShapeDtypeStruct((t, w), dt) for w, dt in zip(PROJ_WIDTHS, PROJ_DTYPES)]
        + [jax.ShapeDtypeStruct((N_DEV, SHARD_IN, k), BF16)],
        scratch_shapes=[pltpu.VMEM((k, CHIP_ROWS), BF16), pltpu.VMEM((CHIP_ROWS, k), BF16),
                        pltpu.VMEM((2, tm, CHIP_ROWS), F32), pltpu.VMEM((2, tm, CHIP_ROWS), BF16), pltpu.VMEM((t, k), BF16),
                        pltpu.SemaphoreType.DMA((7,)), pltpu.SemaphoreType.DMA((7,)), pltpu.SemaphoreType.DMA,
                        pltpu.SemaphoreType.DMA((2,)), pltpu.SemaphoreType.DMA((2, 2)), pltpu.SemaphoreType.DMA((2,))],
        compiler_params=_params("arbitrary", "arbitrary"))(x, g, wt_shard)


def _mm_tn_pairs(prods, name):
    n_prod = len(prods)
    ktok, m = prods[0][0].shape
    n = prods[0][1].shape[1]
    half, blk = m // 2, m // N_DEV
    ns = 2 * n_prod
    kept = N_CHIP * blk

    def side_pieces(s, side):
        i, hf = divmod(s, 2)
        first = hf * N_DEV // 2
        return [((d - first) * blk, i * kept + (d // 2) * blk) for d in range(first, first + N_DEV // 2) if d % 2 == side]

    def body(*refs):
        a_hbm, b_hbm, pair_refs = refs[:n_prod], refs[n_prod:2 * n_prod], refs[2 * n_prod:3 * n_prod]
        a_buf, b_buf, res_buf, recv_all, pair_all, a_sems, b_sems, send_sems, recv_sems, out_sems = refs[3 * n_prod:]
        step = pl.program_id(0)
        px, py, pc = _place()

        def fetch_a(s):
            return pltpu.make_async_copy(a_hbm[s // 2].at[:, pl.ds((s % 2) * half, half)], a_buf.at[s % 2], a_sems.at[s % 2])

        def fetch_b(i):
            return pltpu.make_async_copy(b_hbm[i], b_buf.at[i % 2], b_sems.at[i % 2])

        @pl.when(step == 0)
        def _():
            fetch_a(0).start()
            fetch_b(0).start()

        for s in range(ns):
            @pl.when(step == s)
            def _(s=s):
                if s + 1 < ns:
                    fetch_a(s + 1).start()
                if s % 2 == 0 and s // 2 + 1 < n_prod:
                    fetch_b(s // 2 + 1).start()
                fetch_a(s).wait()
                if s % 2 == 0:
                    fetch_b(s // 2).wait()

        res_buf[step % 2] = lax.dot_general(a_buf[step % 2], b_buf[(step // 2) % 2], TN_DIMS, preferred_element_type=F32)

        def crossing(s, j, piece):
            off, at = piece
            return pltpu.make_async_remote_copy(
                src_ref=res_buf.at[s % 2, pl.ds(off, blk)], dst_ref=recv_all.at[pl.ds(at, blk)],
                send_sem=send_sems.at[s, j], recv_sem=recv_sems.at[s, j], device_id=(px, py, 1 - pc), device_id_type=MESH)

        def write(s, core):
            at = side_pieces(s, core)[0][1]
            return pltpu.make_async_copy(pair_all.at[pl.ds(at, 2 * blk)], pair_refs[s // 2].at[pl.ds(at % kept, 2 * blk)], out_sems.at[s])

        for core in (0, 1):
            def settle(s, core=core):
                for j, piece in enumerate(side_pieces(s, 1 - core)):
                    crossing(s, j, piece).wait_send()
                for j, (off, at) in enumerate(side_pieces(s, core)):
                    crossing(s, j, (off, at)).wait_recv()
                    pair_all[at:at + blk] = (res_buf[s % 2, off:off + blk] + recv_all[at:at + blk]).astype(BF16)
                write(s, core).start()

            for s in range(ns):
                @pl.when((pc == core) & (step == s))
                def _(s=s, settle=settle, core=core):
                    for j, piece in enumerate(side_pieces(s, 1 - core)):
                        crossing(s, j, piece).start()
                    if s > 0:
                        settle(s - 1)
                    if s == ns - 1:
                        settle(s)
                        for t in range(ns):
                            write(t, core).wait()

    pairs = pl.pallas_call(
        body, name=name, grid=(ns,),
        in_specs=[ANY] * (2 * n_prod), out_specs=[ANY] * n_prod,
        out_shape=[jax.ShapeDtypeStruct((kept, n), BF16)] * n_prod,
        scratch_shapes=[pltpu.VMEM((2, ktok, half), prods[0][0].dtype), pltpu.VMEM((2, ktok, n), prods[0][1].dtype),
                        pltpu.VMEM((2, half, n), F32), pltpu.VMEM((n_prod * kept, n), F32), pltpu.VMEM((n_prod * kept, n), BF16),
                        pltpu.SemaphoreType.DMA((2,)), pltpu.SemaphoreType.DMA((2,)),
                        pltpu.SemaphoreType.DMA((ns, 2)), pltpu.SemaphoreType.DMA((ns, 2)), pltpu.SemaphoreType.DMA((ns,))],
        compiler_params=_params("arbitrary"))(*[a for a, _ in prods], *[b for _, b in prods])
    return [pair.reshape(N_CHIP, blk, n) for pair in pairs]


def _segment_chunks(segs):
    bounds = [0]
    for s in segs:
        bounds.append(bounds[-1] + s.shape[1] // SEG_CHUNK)
    return bounds


def _input_grad(segs, wt, x, g, dy, name, tm=512, ride=None):
    m = segs[0].shape[0]
    rows, n = wt.shape
    tm = min(tm, m)
    bounds = _segment_chunks(segs)
    n_seg = len(segs)
    ni = m // tm
    host = _Hosted(ride, n_seg + 4, 2, 2, aliasing=True)

    def body(*refs):
        host_refs, start, finish = host.split(refs)
        a_refs = host_refs[:n_seg]
        wt_hbm, x_ref, g_ref, dy_ref, gx_ref, st_ref, wt_vmem, sems = host_refs[n_seg:]
        i = pl.program_id(0)
        start(i == 0)

        @pl.when(i == 0)
        def _():
            st_ref[...] = jnp.zeros_like(st_ref)

        ready = _load_resident(wt_hbm, wt_vmem, sems, i == 0)
        dh = None
        for s in range(n_seg):
            for c in range(bounds[s], bounds[s + 1]):
                ready(c)
            part = jnp.dot(a_refs[s][...], wt_vmem[bounds[s] * SEG_CHUNK:bounds[s + 1] * SEG_CHUNK, :], preferred_element_type=F32)
            dh = part if dh is None else dh + part
        xv = x_ref[...]
        r = lax.rsqrt(jnp.mean(xv * xv, axis=-1, keepdims=True) + EPS)
        xn = xv * r
        dxn = dh * g_ref[...]
        gx_ref[...] = dy_ref[...] + r * (dxn - xn * jnp.mean(dxn * xn, axis=-1, keepdims=True))
        st_ref[...] += _rows8([jnp.sum(dh * xn, axis=0, keepdims=True)], n)
        finish(i == ni - 1)

    tile = pl.BlockSpec((tm, n), lambda i: (i, 0))
    outs = pl.pallas_call(
        body, name=name, grid=(ni,),
        in_specs=[pl.BlockSpec((tm, sg.shape[1]), lambda i: (i, 0)) for sg in segs]
        + [ANY, tile, pl.BlockSpec((1, n), lambda i: (0, 0)), tile] + host.in_specs,
        out_specs=[tile, pl.BlockSpec((8, n), lambda i: (0, 0))] + host.out_specs,
        out_shape=[jax.ShapeDtypeStruct((m, n), F32), jax.ShapeDtypeStruct((8, n), F32)] + host.out_shapes,
        scratch_shapes=[pltpu.VMEM((rows, n), wt.dtype), pltpu.SemaphoreType.DMA((rows // SEG_CHUNK,))] + host.scratch_shapes,
        input_output_aliases=host.aliases,
        compiler_params=_params("arbitrary"))(*segs, wt, x, g, dy, *host.arrays)
    res, landed = host.results(outs, 2)
    return (*res, landed) if ride else tuple(res)


CHUNK_ORDER = (0, 4, 7, 10, 1, 5, 8, 11, 2, 6, 9, 12, 3)
EARLY_STEPS = 8


def _side_pieces():
    table = []
    for core in (0, 1):
        sides = ([None] * len(CHUNK_ORDER), [None] * len(CHUNK_ORDER))
        for s, cc in enumerate(CHUNK_ORDER):
            g0 = cc * SEG_CHUNK
            for d in range(N_DEV):
                lo, hi = max(g0, d * SHARD_IN), min(g0 + SEG_CHUNK, (d + 1) * SHARD_IN)
                if lo < hi:
                    side = sides[0 if d % 2 == core else 1]
                    assert side[s] is None
                    side[s] = (lo - g0, hi - lo, (d // 2) * SHARD_IN + lo - d * SHARD_IN)
        table.append(sides)
    return table


def _to_chip(src, red_ref, piece, s, send_sems, recv_sems, own, core):
    _, rows, at = piece
    k, r0 = divmod(at, SHARD_IN)
    return k, pltpu.make_async_remote_copy(
        src_ref=src, dst_ref=red_ref.at[own, pl.ds(r0, rows)], send_sem=send_sems.at[s], recv_sem=recv_sems.at[own, s],
        device_id=(k // 2, k % 2, core), device_id_type=MESH)


def _from_chip(src, red_ref, piece, s, send_sems, recv_sems, chip, me):
    _, rows, at = piece
    return pltpu.make_async_remote_copy(
        src_ref=src, dst_ref=red_ref.at[chip, pl.ds(at % SHARD_IN, rows)], send_sem=send_sems.at[s], recv_sem=recv_sems.at[chip, s],
        device_id=me, device_id_type=MESH)


def _mm_tn_seg_pair(segs, b, name):
    ktok = segs[0].shape[0]
    n = b.shape[1]
    bounds = _segment_chunks(segs)
    n_seg = len(segs)
    nc = bounds[-1]
    assert nc == len(CHUNK_ORDER)
    seg_of = [s for s in range(n_seg) for _ in range(bounds[s], bounds[s + 1])]
    table = _side_pieces()

    def body(*refs):
        a_hbm, b_hbm, pair_ref, red_ref = refs[:n_seg], refs[n_seg], refs[n_seg + 1], refs[n_seg + 2]
        (a_buf, b_vmem, res_buf, recv_all, pair_all, a_sems, b_sem, send_sems, recv_sems, out_sems,
         chip_send, chip_recv) = refs[n_seg + 3:]
        step = pl.program_id(0)
        px, py, pc = _place()
        own = 2 * px + py

        def fetch(s):
            sg = seg_of[CHUNK_ORDER[s]]
            cols = pl.ds((CHUNK_ORDER[s] - bounds[sg]) * SEG_CHUNK, SEG_CHUNK)
            return pltpu.make_async_copy(a_hbm[sg].at[:, cols], a_buf.at[s % 2], a_sems.at[s % 2])

        @pl.when(step == 0)
        def _():
            whole = pltpu.make_async_copy(b_hbm, b_vmem, b_sem)
            whole.start()
            fetch(0).start()
            whole.wait()

        for s in range(nc):
            @pl.when(step == s)
            def _(s=s):
                if s + 1 < nc:
                    fetch(s + 1).start()
                fetch(s).wait()

        res_buf[step % 2] = lax.dot_general(a_buf[step % 2], b_vmem[...], TN_DIMS, preferred_element_type=F32)

        def crossing(s, piece):
            off, rows, at = piece
            return pltpu.make_async_remote_copy(
                src_ref=res_buf.at[s % 2, pl.ds(off, rows)], dst_ref=recv_all.at[pl.ds(at, rows)],
                send_sem=send_sems.at[s], recv_sem=recv_sems.at[s], device_id=(px, py, 1 - pc), device_id_type=MESH)

        def write(s, piece):
            _, rows, at = piece
            return pltpu.make_async_copy(pair_all.at[pl.ds(at, rows)], pair_ref.at[pl.ds(at, rows)], out_sems.at[s])

        def to_chip(s, piece, core):
            return _to_chip(pair_all.at[pl.ds(piece[2], piece[1])], red_ref, piece, s, chip_send, chip_recv, own, core)

        for core in (0, 1):
            mine, theirs = table[core]

            def settle(s, mine=mine, theirs=theirs, core=core):
                if theirs[s] is not None:
                    crossing(s, theirs[s]).wait_send()
                if mine[s] is not None:
                    off, rows, at = mine[s]
                    crossing(s, mine[s]).wait_recv()
                    pair_all[at:at + rows] = (res_buf[s % 2, off:off + rows] + recv_all[at:at + rows]).astype(BF16)
                    write(s, mine[s]).start()
                    if s < EARLY_STEPS:
                        k, cp = to_chip(s, mine[s], core)
                        pl.when(own != k)(cp.start)

            for s in range(nc):
                @pl.when((pc == core) & (step == s))
                def _(s=s, settle=settle, mine=mine, theirs=theirs, core=core):
                    if theirs[s] is not None:
                        crossing(s, theirs[s]).start()
                    if s > 0:
                        settle(s - 1)
                    if s == nc - 1:
                        settle(s)
                        kept = [t for t in range(nc) if mine[t] is not None]
                        for t in kept:
                            write(t, mine[t]).wait()
                        early = [t for t in kept if t < EARLY_STEPS]
                        for t in early:
                            k, cp = to_chip(t, mine[t], core)
                            pl.when(own != k)(cp.wait_send)
                        for k in range(N_CHIP):
                            @pl.when(own == k)
                            def _(k=k):
                                for t in early:
                                    if mine[t][2] // SHARD_IN == k:
                                        for chip in range(N_CHIP):
                                            if chip != k:
                                                _from_chip(pair_all.at[pl.ds(mine[t][2], mine[t][1])], red_ref, mine[t], t,
                                                           chip_send, chip_recv, chip, (px, py, pc)).wait_recv()

    flat = jax.ShapeDtypeStruct((N_CHIP * SHARD_IN, n), BF16)
    pair, red = pl.pallas_call(
        body, name=name, grid=(nc,),
        in_specs=[ANY] * (n_seg + 1), out_specs=[ANY, ANY],
        out_shape=[flat, jax.ShapeDtypeStruct((N_CHIP, SHARD_IN, n), BF16)],
        scratch_shapes=[pltpu.VMEM((2, ktok, SEG_CHUNK), segs[0].dtype), pltpu.VMEM((ktok, n), b.dtype),
                        pltpu.VMEM((2, SEG_CHUNK, n), F32), pltpu.VMEM(flat.shape, F32), pltpu.VMEM(flat.shape, BF16),
                        pltpu.SemaphoreType.DMA((2,)), pltpu.SemaphoreType.DMA,
                        pltpu.SemaphoreType.DMA((nc,)), pltpu.SemaphoreType.DMA((nc,)), pltpu.SemaphoreType.DMA((nc,)),
                        pltpu.SemaphoreType.DMA((nc,)), pltpu.SemaphoreType.DMA((N_CHIP, nc))],
        compiler_params=_params("arbitrary"))(*segs, b)
    return pair.reshape(N_CHIP, SHARD_IN, n), red


RING = 3


def _ring_specs(streams, tm):
    return ([pltpu.VMEM((RING, tm, width), a.dtype) for a, _, width in streams]
            + [pltpu.SemaphoreType.DMA((len(streams), RING))])


def _ring_step(streams, hbm_refs, rings, sems, n, n_steps, tm):
    def fetch(step):
        rows = pl.ds(pl.multiple_of(step * tm, tm), tm)
        return [pltpu.make_async_copy(ref.at[rows, pl.ds(col, width)], ring.at[lax.rem(step, RING)], sems.at[k, lax.rem(step, RING)])
                for k, ((_, col, width), ref, ring) in enumerate(zip(streams, hbm_refs, rings))]

    @pl.when(n == 0)
    def _():
        for cp in fetch(0) + fetch(1):
            cp.start()

    @pl.when(n + 2 < n_steps)
    def _():
        for cp in fetch(n + 2):
            cp.start()

    for cp in fetch(n):
        cp.wait()
    return lax.rem(n, RING)


def _branches_fwd(z_rnn, z_attn, ag_ml, b_gate, w_rnn, w_attn, w_out, x, target, g_post, name, tm=512):
    t, d = x.shape
    tm = min(tm, t)
    n_steps = t // tm
    streams = [(z_rnn, 0, d), (z_attn, 0, d), (ag_ml, d, d), (ag_ml, 2 * d, d), (x, 0, d), (target, 0, d)]
    ns = len(streams)

    def body(*refs):
        hbm = refs[:ns]
        br_ref, ba_ref, wr_ref, wa_ref, wo_ref, g_ref, brr_ref, bra_ref, mg_ref, do_ref, dy_ref, st_ref = refs[ns:ns + 12]
        rings, sems = refs[ns + 12:2 * ns + 12], refs[2 * ns + 12]
        slot = _ring_step(streams, hbm, rings, sems, pl.program_id(0), n_steps, tm)
        zr_ref, za_ref, lr_ref, la_ref, x_ref, t_ref = (ring.at[slot] for ring in rings)

        @pl.when(pl.program_id(0) == 0)
        def _():
            st_ref[...] = jnp.zeros_like(st_ref)

        br_rnn = jnp.dot(zr_ref[...], wr_ref[...], preferred_element_type=F32)
        br_attn = jnp.dot(za_ref[...], wa_ref[...], preferred_element_type=F32)
        brr_ref[...] = br_rnn.astype(BF16)
        bra_ref[...] = br_attn.astype(BF16)
        g_rnn = _sigmoid(lr_ref[...].astype(F32) + br_ref[...])
        g_attn = _sigmoid(la_ref[...].astype(F32) + ba_ref[...])
        merged = (g_rnn * br_rnn + g_attn * br_attn).astype(BF16)
        mg_ref[...] = merged
        o = jnp.dot(merged, wo_ref[...], preferred_element_type=F32)
        g = g_ref[...]
        r = lax.rsqrt(jnp.mean(o * o, axis=-1, keepdims=True) + EPS)
        nrm = o * r
        err = x_ref[...] + nrm * g - t_ref[...]
        dy = err * (1.0 / d)
        dy_ref[...] = dy
        dn = dy * g
        do_ref[...] = (r * (dn - nrm * jnp.mean(dn * nrm, axis=-1, keepdims=True))).astype(BF16)
        st_ref[...] += _rows8([jnp.sum(dy * nrm, axis=0, keepdims=True), jnp.sum(err * err, axis=0, keepdims=True)], d)

    tile = pl.BlockSpec((tm, d), lambda i: (i, 0))
    weight = pl.BlockSpec((d, d), lambda i: (0, 0))
    bf = jax.ShapeDtypeStruct((t, d), BF16)
    return pl.pallas_call(
        body, name=name, grid=(n_steps,),
        in_specs=[ANY] * ns + [pl.BlockSpec((1, d), lambda i: (0, 0)), pl.BlockSpec((1, d), lambda i: (0, 1)),
                               weight, weight, weight, pl.BlockSpec((1, d), lambda i: (0, 0))],
        out_specs=[tile, tile, tile, tile, tile, pl.BlockSpec((8, d), lambda i: (0, 0))],
        out_shape=[bf, bf, bf, bf, jax.ShapeDtypeStruct((t, d), F32), jax.ShapeDtypeStruct((8, d), F32)],
        scratch_shapes=_ring_specs(streams, tm),
        compiler_params=_params("arbitrary"))(*[a for a, _, _ in streams], b_gate, b_gate, w_rnn, w_attn, w_out, g_post)


def _branches_bwd(dout, br_rnn, br_attn, ag_ml, b_gate, w_rnn, w_attn, w_out, name, tm=512):
    t, d = br_rnn.shape
    tm = min(tm, t)
    n_steps = t // tm
    streams = [(dout, 0, d), (br_rnn, 0, d), (br_attn, 0, d), (ag_ml, d, d), (ag_ml, 2 * d, d)]
    ns = len(streams)

    def body(*refs):
        hbm = refs[:ns]
        br_ref, ba_ref, wr_ref, wa_ref, wo_ref, dr_ref, da_ref, dl_ref, dzr_ref, dza_ref, st_ref, wt_ref = refs[ns:ns + 12]
        rings, sems = refs[ns + 12:2 * ns + 12], refs[2 * ns + 12]
        slot = _ring_step(streams, hbm, rings, sems, pl.program_id(0), n_steps, tm)
        do_ref, r_ref, a_ref, lr_ref, la_ref = (ring.at[slot] for ring in rings)

        @pl.when(pl.program_id(0) == 0)
        def _():
            st_ref[...] = jnp.zeros_like(st_ref)
            wt_ref[0] = wo_ref[...].T
            wt_ref[1] = wr_ref[...].T
            wt_ref[2] = wa_ref[...].T

        dm = jnp.dot(do_ref[...], wt_ref[0], preferred_element_type=F32)
        g_rnn = _sigmoid(lr_ref[...].astype(F32) + br_ref[...])
        g_attn = _sigmoid(la_ref[...].astype(F32) + ba_ref[...])
        dbr_rnn = (dm * g_rnn).astype(BF16)
        dbr_attn = (dm * g_attn).astype(BF16)
        dr_ref[...] = dbr_rnn
        da_ref[...] = dbr_attn
        dl_rnn = dm * r_ref[...].astype(F32) * g_rnn * (1.0 - g_rnn)
        dl_attn = dm * a_ref[...].astype(F32) * g_attn * (1.0 - g_attn)
        dl_ref[:, 0:d] = dl_rnn.astype(BF16)
        dl_ref[:, d:2 * d] = dl_attn.astype(BF16)
        st_ref[...] += _rows8([jnp.sum(dl_rnn, axis=0, keepdims=True), jnp.sum(dl_attn, axis=0, keepdims=True)], d)
        dzr_ref[...] = jnp.dot(dbr_rnn, wt_ref[1], preferred_element_type=F32).astype(BF16)
        dza_ref[...] = jnp.dot(dbr_attn, wt_ref[2], preferred_element_type=F32).astype(BF16)

    tile = pl.BlockSpec((tm, d), lambda i: (i, 0))
    weight = pl.BlockSpec((d, d), lambda i: (0, 0))
    bf = jax.ShapeDtypeStruct((t, d), BF16)
    return pl.pallas_call(
        body, name=name, grid=(n_steps,),
        in_specs=[ANY] * ns + [pl.BlockSpec((1, d), lambda i: (0, 0)), pl.BlockSpec((1, d), lambda i: (0, 1)), weight, weight, weight],
        out_specs=[tile, tile, pl.BlockSpec((tm, 2 * d), lambda i: (i, 0)), tile, tile, pl.BlockSpec((8, d), lambda i: (0, 0))],
        out_shape=[bf, bf, jax.ShapeDtypeStruct((t, 2 * d), BF16), bf, bf, jax.ShapeDtypeStruct((8, d), F32)],
        scratch_shapes=[pltpu.VMEM((3, d, d), BF16)] + _ring_specs(streams, tm),
        compiler_params=_params("arbitrary"))(*[a for a, _, _ in streams], b_gate, b_gate, w_rnn, w_attn, w_out)


def _lru_decay(r, sp):
    log_a = (-LRU_C) * r * sp
    return log_a, jnp.exp(log_a)


def _lru_gates(c, wa, ba, wx, bx, sp):
    cb = c.astype(BF16)
    r = _sigmoid(jnp.dot(cb, wa, preferred_element_type=F32) + ba)
    ig = _sigmoid(jnp.dot(cb, wx, preferred_element_type=F32) + bx)
    log_a, a = _lru_decay(r, sp)
    mult = jnp.sqrt(-jnp.tanh(log_a) * (a * a + 1.0))
    return cb, r, ig, a, mult


SUBLANES = 8


def _scan_fwd(a, u, carry, tt):
    w = a.shape[1]
    ng = tt // SUBLANES
    a3 = a.reshape(ng, SUBLANES, w)
    u3 = u.reshape(ng, SUBLANES, w)
    sub = lax.broadcasted_iota(jnp.int32, (ng, SUBLANES, w), 1)
    d = 1
    while d < SUBLANES:
        keep = sub >= d
        u3 = u3 + a3 * jnp.where(keep, pltpu.roll(u3, d, 1), 0.0)
        a3 = a3 * jnp.where(keep, pltpu.roll(a3, d, 1), 1.0)
        d *= 2
    out = []
    for g in range(ng):
        hg = u3[g] + a3[g] * carry
        out.append(hg)
        carry = hg[SUBLANES - 1:SUBLANES, :]
    return jnp.concatenate(out, axis=0)


def _scan_rev(b, g, carry, tt):
    w = b.shape[1]
    ng = tt // SUBLANES
    b3 = b.reshape(ng, SUBLANES, w)
    g3 = g.reshape(ng, SUBLANES, w)
    sub = lax.broadcasted_iota(jnp.int32, (ng, SUBLANES, w), 1)
    d = 1
    while d < SUBLANES:
        keep = sub < SUBLANES - d
        g3 = g3 + b3 * jnp.where(keep, pltpu.roll(g3, SUBLANES - d, 1), 0.0)
        b3 = b3 * jnp.where(keep, pltpu.roll(b3, SUBLANES - d, 1), 1.0)
        d *= 2
    out = [None] * ng
    for k in range(ng - 1, -1, -1):
        hk = g3[k] + b3[k] * carry
        out[k] = hk
        carry = hk[0:1, :]
    return jnp.concatenate(out, axis=0)


def _conv_taps(cw, bias, x, ext_ref, tt):
    x2 = ext_ref[7:7 + tt, :]
    x1 = ext_ref[6:6 + tt, :]
    x0 = ext_ref[5:5 + tt, :]
    return bias + cw[3:4] * x + cw[2:3] * x2 + cw[1:2] * x1 + cw[0:1] * x0


def _rnn_fwd(rx_rg, cw, cb, wa, ba, wx, bx, lam, name, tt=512):
    t = rx_rg.shape[0]
    tt = min(tt, t)
    w = GROUP_W

    nt = t // tt
    n_steps = N_GROUPS * nt

    def body(in_hbm, cw_ref, cb_ref, wa_ref, ba_ref, wx_ref, bx_ref, lam_ref,
             y_ref, z_ref, c_ref, r_ref, ig_ref, mult_ref, ext_ref, hc_ref, x_ring, g_ring, ring_sems):
        @pl.when(pl.program_id(1) == 0)
        def _():
            ext_ref[0:8, :] = jnp.zeros((8, w), F32)
            hc_ref[...] = jnp.zeros((8, w), F32)

        n = pl.program_id(0) * nt + pl.program_id(1)

        def fetch(step):
            slot = lax.rem(step, 3)
            rows = pl.ds(pl.multiple_of(lax.rem(step, nt) * tt, tt), tt)
            group = lax.div(step, nt)
            return [pltpu.make_async_copy(in_hbm.at[rows, pl.ds(pl.multiple_of((half * N_GROUPS + group) * w, w), w)],
                                          ring.at[slot], ring_sems.at[half, slot])
                    for half, ring in enumerate((x_ring, g_ring))]

        @pl.when(n == 0)
        def _():
            for cp in fetch(0) + fetch(1):
                cp.start()

        @pl.when(n + 2 < n_steps)
        def _():
            for cp in fetch(n + 2):
                cp.start()

        for cp in fetch(n):
            cp.wait()
        slot = lax.rem(n, 3)
        x = x_ring[slot]
        ext_ref[8:8 + tt, :] = x
        c = _conv_taps(cw_ref[...], cb_ref[...], x, ext_ref, tt)
        ext_ref[0:8, :] = x[tt - 8:tt, :]
        sp = _softplus(-lam_ref[...])
        _, r, ig, a, mult = _lru_gates(c, wa_ref[...], ba_ref[...], wx_ref[...], bx_ref[...], sp)
        c_ref[...] = c
        r_ref[...] = r
        ig_ref[...] = ig
        mult_ref[...] = mult
        h = _scan_fwd(a, mult * (ig * c), hc_ref[7:8, :], tt)
        hc_ref[...] = h[tt - 8:tt, :]
        y_ref[...] = h
        rg = g_ring[slot]
        z_ref[...] = (h * rg * _sigmoid(rg)).astype(BF16)

    vec = pl.BlockSpec((1, w), lambda g, i: (0, g))
    mat = pl.BlockSpec((None, w, w), lambda g, i: (g, 0, 0))
    tile = pl.BlockSpec((tt, w), lambda g, i: (i, g))
    return pl.pallas_call(
        body, name=name, grid=(N_GROUPS, nt),
        in_specs=[ANY, pl.BlockSpec((4, w), lambda g, i: (0, g)), vec, mat, vec, mat, vec, vec],
        out_specs=[tile, tile] + [pl.BlockSpec((None, tt, w), lambda g, i: (g, i, 0))] * 4,
        out_shape=[jax.ShapeDtypeStruct((t, D_RNN), F32), jax.ShapeDtypeStruct((t, D_RNN), BF16)]
        + [jax.ShapeDtypeStruct((N_GROUPS, t, w), F32)] * 4,
        scratch_shapes=[pltpu.VMEM((tt + 8, w), F32), pltpu.VMEM((8, w), F32),
                        pltpu.VMEM((3, tt, w), F32), pltpu.VMEM((3, tt, w), F32), pltpu.SemaphoreType.DMA((2, 3))],
        compiler_params=_params("arbitrary", "arbitrary"))(rx_rg, cw, cb, wa, ba, wx, bx, lam)


def _rnn_bwd(rx_rg, y, dz, c, r, ig, mult, cw, wa, wx, lam, name, tt=512):
    t = rx_rg.shape[0]
    tt = min(tt, t)
    nt = t // tt
    w = GROUP_W

    def body(rx_ref, rg_ref, y_ref, yt_ref, dz_ref, c_ref, r_ref, ig_ref, mult_ref, cw_ref, wa_ref, wx_ref, lam_ref,
             drx_ref, drg_ref, st_ref, gda_ref, gdx_ref, dcx_ref, wcar_ref, acar_ref, dwa_ref, dwx_ref):
        ii = pl.program_id(1)

        @pl.when(ii == 0)
        def _():
            wcar_ref[...] = jnp.zeros((8, w), F32)
            acar_ref[...] = jnp.zeros((8, w), F32)
            dcx_ref[tt:tt + 8, :] = jnp.zeros((8, w), F32)
            st_ref[...] = jnp.zeros_like(st_ref)
            dwa_ref[...] = jnp.zeros_like(dwa_ref)
            dwx_ref[...] = jnp.zeros_like(dwx_ref)

        has_prev = jnp.where(ii == nt - 1, 0.0, 1.0)
        cwv = cw_ref[...]
        lam = lam_ref[...]
        sp = _softplus(-lam)
        wa = wa_ref[...]
        wx = wx_ref[...]
        c = c_ref[...]
        r = r_ref[...]
        ig = ig_ref[...]
        mult = mult_ref[...]
        _, a = _lru_decay(r, sp)
        cb16 = c.astype(BF16)

        rg = rg_ref[...]
        sg = _sigmoid(rg)
        dz = dz_ref[...].astype(F32)
        yv = y_ref[...]
        drg_ref[...] = (dz * yv * (sg * (1.0 + rg * (1.0 - sg)))).astype(BF16)

        row = lax.broadcasted_iota(jnp.int32, (tt, w), 0)
        b = jnp.where(row < tt - 1, pltpu.roll(a, tt - 1, 0), acar_ref[0:1, :])
        dh = _scan_rev(b, dz * (rg * sg), wcar_ref[0:1, :], tt)
        wcar_ref[...] = dh[0:8, :]
        acar_ref[...] = a[0:8, :]

        hprev = jnp.where(row >= 1, pltpu.roll(yv, 1, 0), yt_ref[7:8, :] * has_prev)
        dmult = dh * (ig * c)
        dig = dh * mult * c
        dlog_a = dh * hprev * a - dmult * (a * a / mult)
        dpa = dlog_a * ((-LRU_C) * sp) * r * (1.0 - r)
        dpx = dig * ig * (1.0 - ig)
        dsp = jnp.sum(dlog_a * r, axis=0, keepdims=True) * (-LRU_C)
        dlam = dsp * (-_sigmoid(-lam))
        dpa16 = dpa.astype(BF16)
        dpx16 = dpx.astype(BF16)
        dwa_ref[...] += lax.dot_general(cb16, dpa16, TN_DIMS, preferred_element_type=F32)
        dwx_ref[...] += lax.dot_general(cb16, dpx16, TN_DIMS, preferred_element_type=F32)
        dc = (dh * mult * ig
              + lax.dot_general(dpa16, wa, NT_DIMS, preferred_element_type=F32)
              + lax.dot_general(dpx16, wx, NT_DIMS, preferred_element_type=F32))

        dcx_ref[0:tt, :] = dc
        dc1 = dcx_ref[1:1 + tt, :]
        dc2 = dcx_ref[2:2 + tt, :]
        dc3 = dcx_ref[3:3 + tt, :]
        drx_ref[...] = (cwv[3:4] * dc + cwv[2:3] * dc1 + cwv[1:2] * dc2 + cwv[0:1] * dc3).astype(BF16)
        dcx_ref[tt:tt + 8, :] = dc[0:8, :]

        def colsum(v):
            return jnp.sum(v, axis=0, keepdims=True)

        x = rx_ref[...]
        st_ref[...] += _rows8([colsum(dc), colsum(dpa), colsum(dpx), dlam,
                               colsum(dc3 * x), colsum(dc2 * x), colsum(dc1 * x), colsum(dc * x)], w)

        @pl.when(ii == nt - 1)
        def _():
            for blk in range(GROUP_W // RNN_BLOCK_W):
                rows = slice(blk * RNN_BLOCK_W, (blk + 1) * RNN_BLOCK_W)
                gda_ref[blk] = dwa_ref[rows, rows]
                gdx_ref[blk] = dwx_ref[rows, rows]

    def rev(ii):
        return nt - 1 - ii

    def tail(g, ii):
        return (jnp.maximum(rev(ii) * (tt // 8) - 1, 0), g)

    vec = pl.BlockSpec((1, w), lambda g, ii: (0, g))
    mat = pl.BlockSpec((None, w, w), lambda g, ii: (g, 0, 0))
    tile = pl.BlockSpec((tt, w), lambda g, ii: (rev(ii), g))
    diag_shape = (N_GROUPS, GROUP_W // RNN_BLOCK_W, RNN_BLOCK_W, RNN_BLOCK_W)
    diag = pl.BlockSpec((None,) + diag_shape[1:], lambda g, ii: (g, 0, 0, 0))
    kept = pl.BlockSpec((None, tt, w), lambda g, ii: (g, rev(ii), 0))
    return pl.pallas_call(
        body, name=name, grid=(N_GROUPS, nt),
        in_specs=[tile, pl.BlockSpec((tt, w), lambda g, ii: (rev(ii), N_GROUPS + g)),
                  tile, pl.BlockSpec((8, w), tail), tile, kept, kept, kept, kept,
                  pl.BlockSpec((4, w), lambda g, ii: (0, g)), mat, mat, vec],
        out_specs=[tile, tile, pl.BlockSpec((8, w), lambda g, ii: (0, g)), diag, diag],
        out_shape=[jax.ShapeDtypeStruct((t, D_RNN), BF16), jax.ShapeDtypeStruct((t, D_RNN), BF16),
                   jax.ShapeDtypeStruct((8, D_RNN), F32),
                   jax.ShapeDtypeStruct(diag_shape, F32), jax.ShapeDtypeStruct(diag_shape, F32)],
        scratch_shapes=[pltpu.VMEM((tt + 8, w), F32), pltpu.VMEM((8, w), F32),
                        pltpu.VMEM((8, w), F32), pltpu.VMEM((w, w), F32), pltpu.VMEM((w, w), F32)],
        compiler_params=_params("parallel", "arbitrary"))(rx_rg, rx_rg, y, y, dz, c, r, ig, mult, cw, wa, wx, lam)


def _half_mask(shape, half):
    lane = lax.broadcasted_iota(jnp.int32, shape, 1)
    return (lane >= HEAD_DIM) if half else (lane < HEAD_DIM)


def _dup_half(t, half):
    sel = jnp.where(_half_mask(t.shape, half), t, 0.0)
    return sel + pltpu.roll(sel, HEAD_DIM, 1)


def _band_geometry(n):
    qi = lax.broadcasted_iota(jnp.int32, (BLOCK, 2 * BLOCK), 0)
    kj = lax.broadcasted_iota(jnp.int32, (BLOCK, 2 * BLOCK), 1)
    dist = BLOCK + qi - kj
    first_key = jnp.where(n > 0, 0, BLOCK)
    valid = (dist >= 0) & (dist < BLOCK) & (kj >= first_key)
    return dist.astype(F32), valid


GROUP = 4


def _kv_dup(prev_ref, cur_ref, hk, scale=1.0):
    tile = hk // 2
    kt = jnp.concatenate([prev_ref[:, tile * LANE:(tile + 1) * LANE], cur_ref[:, tile * LANE:(tile + 1) * LANE]], axis=0)
    return (_dup_half(kt.astype(F32), hk % 2) * scale).astype(BF16)


def _fill_bias(bias_ref, sm_ref, n):
    distf, valid = _band_geometry(n)
    for head in range(N_Q_HEADS):
        bias_ref[head] = jnp.where(valid, -sm_ref[1, head] * distf, MASKED)


def _head_scores(q2s, half, kdup, bias):
    qm = jnp.where(_half_mask(q2s.shape, half), q2s, jnp.zeros_like(q2s))
    return qm, lax.dot_general(qm, kdup, NT_DIMS, preferred_element_type=F32) + bias


def _attn_specs(nb, clamp_last):
    def blk(n):
        return jnp.minimum(n, nb - 1) if clamp_last else n

    q_spec = pl.BlockSpec((BLOCK, D_MODEL), lambda n: (blk(n), 0))
    k_prev = pl.BlockSpec((BLOCK, D_KV), lambda n: (jnp.maximum(blk(n) - 1, 0), D_MODEL // D_KV))
    k_cur = pl.BlockSpec((BLOCK, D_KV), lambda n: (blk(n), D_MODEL // D_KV))
    v_prev = pl.BlockSpec((BLOCK, D_KV), lambda n: (jnp.maximum(blk(n) - 1, 0), D_MODEL // D_KV + 1))
    v_cur = pl.BlockSpec((BLOCK, D_KV), lambda n: (blk(n), D_MODEL // D_KV + 1))
    return q_spec, k_prev, k_cur, v_prev, v_cur


def _attn_fwd(sm, qkv, ag_ml, name, ride=None):
    t = qkv.shape[0]
    nb = t // BLOCK

    host = _Hosted(ride, 7, 3, 1)

    def body(*refs):
        (sm_ref, q_ref, kp_ref, kc_ref, vp_ref, vc_ref, ag_ref, y_ref, z_ref, lse_ref, bias_ref), start, finish = host.split(refs)
        n = pl.program_id(0)
        start(n == 0)
        start.middle(n == (3 * nb) // 4)

        @pl.when(n <= 1)
        def _():
            _fill_bias(bias_ref, sm_ref, n)

        lane = lax.broadcasted_iota(jnp.int32, (BLOCK, LANE), 1)
        low = lane < HEAD_DIM
        lse = jnp.zeros((BLOCK, LANE), F32)
        for hk in range(N_Q_HEADS // GROUP):
            kdup = _kv_dup(kp_ref, kc_ref, hk)
            vdup = _kv_dup(vp_ref, vc_ref, hk)
            for k in (0, 1):
                c = slice((2 * hk + k) * LANE, (2 * hk + k + 1) * LANE)
                q2s = q_ref[:, c] * ATTN_SCALE
                outs = []
                for half in (0, 1):
                    head = GROUP * hk + 2 * k + half
                    _, s = _head_scores(q2s, half, kdup, bias_ref[head])
                    sink = sm_ref[0, head]
                    m = jnp.maximum(jnp.max(s, axis=1, keepdims=True), sink)
                    e = jnp.exp(s - m)
                    l = jnp.sum(e, axis=1, keepdims=True) + jnp.exp(sink - m)
                    outs.append(jnp.dot((e * (1.0 / l)).astype(BF16), vdup, preferred_element_type=F32))
                    lse = jnp.where(lane == head, m + jnp.log(l), lse)
                yt = jnp.where(low, outs[0], outs[1])
                y_ref[:, c] = yt
                ag = ag_ref[:, c].astype(F32)
                z_ref[:, c] = (yt * ag * _sigmoid(ag)).astype(BF16)
        lse_ref[...] = lse
        finish(n == nb - 1)

    q_spec, k_prev, k_cur, v_prev, v_cur = _attn_specs(nb, False)
    wide = pl.BlockSpec((BLOCK, D_MODEL), lambda n: (n, 0))
    outs = pl.pallas_call(
        body, name=name, grid=(nb,),
        in_specs=[pl.BlockSpec(memory_space=pltpu.SMEM), q_spec, k_prev, k_cur, v_prev, v_cur, wide] + host.in_specs,
        out_specs=[wide, wide, pl.BlockSpec((BLOCK, LANE), lambda n: (n, 0))] + host.out_specs,
        out_shape=[jax.ShapeDtypeStruct((t, D_MODEL), F32), jax.ShapeDtypeStruct((t, D_MODEL), BF16),
                   jax.ShapeDtypeStruct((t, LANE), F32)] + host.out_shapes,
        scratch_shapes=[pltpu.VMEM((N_Q_HEADS, BLOCK, 2 * BLOCK), F32)] + host.scratch_shapes,
        compiler_params=_params("arbitrary"))(sm, qkv, qkv, qkv, qkv, qkv, ag_ml, *host.arrays)
    res, landed = host.results(outs, 3)
    return (*res, landed) if ride else tuple(res)


def _attn_bwd(sm, qkv, ag_ml, y, lse, dz, name, ride=None):
    t = qkv.shape[0]
    nb = t // BLOCK
    host = _Hosted(ride, 10, 4, 3)

    def body(*refs):
        host_refs, start, finish = host.split(refs)
        (sm_ref, q_ref, kp_ref, kc_ref, vp_ref, vc_ref, ag_ref, y_ref, lse_ref, dz_ref,
         dq_ref, dkv_ref, dag_ref, ds_ref, ck_ref, cv_ref, bias_ref) = host_refs
        n = pl.program_id(0)
        start(n == 0)
        start.middle(n == (3 * nb) // 4)

        @pl.when(n == 0)
        def _():
            ck_ref[...] = jnp.zeros_like(ck_ref)
            cv_ref[...] = jnp.zeros_like(cv_ref)
            ds_ref[...] = jnp.zeros_like(ds_ref)

        @pl.when(n <= 1)
        def _():
            _fill_bias(bias_ref, sm_ref, n)

        @pl.when(n < nb)
        def _():
            lane8 = lax.broadcasted_iota(jnp.int32, (8, LANE), 1)
            row8 = lax.broadcasted_iota(jnp.int32, (8, LANE), 0)
            dsink = jnp.zeros((8, LANE), F32)
            dk_heads, dv_heads = [], []
            lse_tile = lse_ref[...]
            for hk in range(N_Q_HEADS // GROUP):
                kdup = _kv_dup(kp_ref, kc_ref, hk)
                ks = _kv_dup(kp_ref, kc_ref, hk, ATTN_SCALE)
                vdup = _kv_dup(vp_ref, vc_ref, hk)
                qms, dyhs, y_rows = [], [], []
                for k in (0, 1):
                    c = slice((2 * hk + k) * LANE, (2 * hk + k + 1) * LANE)
                    ag = ag_ref[:, c].astype(F32)
                    sg = _sigmoid(ag)
                    dzt = dz_ref[:, c].astype(F32)
                    yt = y_ref[:, c]
                    dag_ref[:, c] = (dzt * yt * (sg * (1.0 + ag * (1.0 - sg)))).astype(BF16)
                    dyt = dzt * (ag * sg)
                    q2s = q_ref[:, c] * ATTN_SCALE
                    for half in (0, 1):
                        hm = _half_mask(q2s.shape, half)
                        qms.append(jnp.where(hm, q2s, jnp.zeros_like(q2s)))
                        dyhs.append(jnp.where(hm, dyt, 0.0))
                        y_rows.append(yt)
                qm4 = jnp.concatenate(qms, axis=0)
                dy4 = jnp.concatenate(dyhs, axis=0)
                dy4_16 = dy4.astype(BF16)
                s4 = lax.dot_general(qm4, kdup, NT_DIMS, preferred_element_type=F32)
                dp4 = lax.dot_general(dy4_16, vdup, NT_DIMS, preferred_element_type=F32)
                probs16, ds16 = [], []
                for r in range(GROUP):
                    head = GROUP * hk + r
                    rows = slice(r * BLOCK, (r + 1) * BLOCK)
                    lh = lse_tile[:, head:head + 1]
                    probs = jnp.exp(s4[rows] + bias_ref[head] - lh)
                    psink = jnp.exp(sm_ref[0, head] - lh)
                    delta = jnp.sum(dyhs[r] * y_rows[r], axis=1, keepdims=True)
                    ds16.append((probs * (dp4[rows] - delta)).astype(BF16))
                    probs16.append(probs.astype(BF16))
                    dsink = dsink + jnp.where((row8 == 0) & (lane8 == head),
                                              -jnp.sum(psink * delta, axis=0, keepdims=True), 0.0)
                ds4 = jnp.concatenate(ds16, axis=0)
                p4 = jnp.concatenate(probs16, axis=0)
                dq4 = jnp.dot(ds4, ks, preferred_element_type=F32)
                low = _half_mask((BLOCK, LANE), 0)
                for k in (0, 1):
                    c = slice((2 * hk + k) * LANE, (2 * hk + k + 1) * LANE)
                    dq_ref[:, c] = jnp.where(low, dq4[2 * k * BLOCK:(2 * k + 1) * BLOCK],
                                             dq4[(2 * k + 1) * BLOCK:(2 * k + 2) * BLOCK]).astype(BF16)
                dk_acc = lax.dot_general(ds4, qm4, TN_DIMS, preferred_element_type=F32)
                dv_acc = lax.dot_general(p4, dy4_16, TN_DIMS, preferred_element_type=F32)
                dk_heads.append(dk_acc + pltpu.roll(dk_acc, HEAD_DIM, 1))
                dv_heads.append(dv_acc + pltpu.roll(dv_acc, HEAD_DIM, 1))
            ds_ref[...] += dsink
            low = _half_mask((2 * BLOCK, LANE), 0)
            for tile in range(2):
                cols = slice(tile * LANE, (tile + 1) * LANE)
                dkt = jnp.where(low, dk_heads[2 * tile], dk_heads[2 * tile + 1])
                dvt = jnp.where(low, dv_heads[2 * tile], dv_heads[2 * tile + 1])
                dkv_ref[:, cols] = (ck_ref[:, cols] + dkt[0:BLOCK, :]).astype(BF16)
                dkv_ref[:, D_KV + tile * LANE:D_KV + (tile + 1) * LANE] = (cv_ref[:, cols] + dvt[0:BLOCK, :]).astype(BF16)
                ck_ref[:, cols] = dkt[BLOCK:2 * BLOCK, :]
                cv_ref[:, cols] = dvt[BLOCK:2 * BLOCK, :]

        @pl.when(n == nb)
        def _():
            dkv_ref[:, 0:D_KV] = ck_ref[...].astype(BF16)
            dkv_ref[:, D_KV:2 * D_KV] = cv_ref[...].astype(BF16)

        finish(n == nb)

    q_spec, k_prev, k_cur, v_prev, v_cur = _attn_specs(nb, True)
    wide = pl.BlockSpec((BLOCK, D_MODEL), lambda n: (jnp.minimum(n, nb - 1), 0))
    outs = pl.pallas_call(
        body, name=name, grid=(nb + 1,),
        in_specs=[pl.BlockSpec(memory_space=pltpu.SMEM), q_spec, k_prev, k_cur, v_prev, v_cur, wide, wide,
                  pl.BlockSpec((BLOCK, LANE), lambda n: (jnp.minimum(n, nb - 1), 0)), wide] + host.in_specs,
        out_specs=[wide, pl.BlockSpec((BLOCK, 2 * D_KV), lambda n: (jnp.maximum(n - 1, 0), 0)), wide,
                   pl.BlockSpec((8, LANE), lambda n: (0, 0))] + host.out_specs,
        out_shape=[jax.ShapeDtypeStruct((t, D_MODEL), BF16), jax.ShapeDtypeStruct((t, 2 * D_KV), BF16),
                   jax.ShapeDtypeStruct((t, D_MODEL), BF16), jax.ShapeDtypeStruct((8, LANE), F32)] + host.out_shapes,
        scratch_shapes=[pltpu.VMEM((BLOCK, D_KV), F32), pltpu.VMEM((BLOCK, D_KV), F32),
                        pltpu.VMEM((N_Q_HEADS, BLOCK, 2 * BLOCK), F32)] + host.scratch_shapes,
        compiler_params=_params("arbitrary"))(sm, qkv, qkv, qkv, qkv, qkv, ag_ml, y, lse, dz, *host.arrays)
    res, landed = host.results(outs, 4)
    return (*res, landed) if ride else tuple(res)


def _local_grads(x, target, p, project, late_weights, reduce_out, reduce_in):
    h, rx_rg, qkv, ag_ml, wt = project(x, p["pre_g"])
    y_attn, z_attn, lse, landed = _attn_fwd(p["sm"], qkv, ag_ml, "attn_fwd", ride=late_weights[0])
    p = {**p, **late_weights[1](landed)}
    y_rnn, z_rnn, *kept_rnn = _rnn_fwd(
        rx_rg, p["cw"], p["cb"], p["wbd_a"], p["b_a"], p["wbd_x"], p["b_x"], p["lam"], "rnn_fwd")
    br_rnn, br_attn, merged, dout, dy, st_post = _branches_fwd(
        z_rnn, z_attn, ag_ml, p["b_gate"], p["w_rnn"], p["w_attn"], p["w_out"], x, target, p["post_g"], "branches_fwd")

    dbr_rnn, dbr_attn, d_ml, dz_rnn, dz_attn, st_merge = _branches_bwd(
        dout, br_rnn, br_attn, ag_ml, p["b_gate"], p["w_rnn"], p["w_attn"], p["w_out"], "branches_bwd")
    out_pairs = _mm_tn_pairs([(z_rnn, dbr_rnn), (z_attn, dbr_attn), (merged, dout)], "gw_outs")
    d_rx, d_rg, st_rnn, g_rg_a, g_rg_x = _rnn_bwd(
        rx_rg, y_rnn, dz_rnn, *kept_rnn, p["cw"], p["wbd_a"], p["wbd_x"], p["lam"], "rnn_bwd")
    dq, dkv, d_ag, st_sink, red_out = _attn_bwd(p["sm"], qkv, ag_ml, y_attn, lse, dz_attn, "attn_bwd",
                                                ride=reduce_out(out_pairs, g_rg_a, g_rg_x))

    segs = [d_rx, d_rg, dq, dkv, d_ag, d_ml]
    grad_x, st_pre, red_in = _input_grad(segs, wt, x, p["pre_g"], dy, "input_grad",
                                         ride=reduce_in(*_mm_tn_seg_pair(segs, h, "gw_in")))
    return dict(grad_x=grad_x, st_post=st_post, st_merge=st_merge, st_rnn=st_rnn, st_sink=st_sink, st_pre=st_pre,
                red_out=red_out, red_in=red_in)


def _place():
    x, y, c = lax.axis_index("x"), lax.axis_index("y"), lax.axis_index("c")
    return x, y, c


def _gather_ride(shards):
    n = len(shards)

    def copies(ins, outs, sems):
        send_sems, recv_sems, local_sems = sems
        x, y, c = _place()
        me, sibling = (x, y, c), (x, y, 1 - c)
        chips = [(1 - x, y), (x, 1 - y), (1 - x, 1 - y)]

        def slot(a, dev):
            return outs[a].at[4 * dev[0] + 2 * dev[1] + dev[2]]

        def copy(a, k, block, to, src=None):
            return pltpu.make_async_remote_copy(
                src_ref=slot(a, block) if src is None else src, dst_ref=slot(a, block),
                send_sem=send_sems.at[a, k], recv_sem=recv_sems.at[a, k], device_id=to, device_id_type=MESH)

        mine = [pltpu.make_async_copy(ins[a], slot(a, me), local_sems.at[a]) for a in range(n)]
        first = []
        for a in range(n):
            first.append(copy(a, 0, me, sibling, src=ins[a]))
            first += [copy(a, 1 + j, me, (*chip, c), src=ins[a]) for j, chip in enumerate(chips)]
        return me, sibling, chips, c, copy, mine, first

    def start(ins, outs, sems):
        *_, mine, first = copies(ins, outs, sems)
        for cp in mine + first:
            cp.start()

    def middle(ins, outs, sems):
        me, sibling, chips, c, copy, _, _ = copies(ins, outs, sems)
        for j, chip in enumerate(chips):
            for a in range(n):
                copy(a, 1 + j, (*chip, c), me).wait_recv()
                copy(a, 4 + j, (*chip, c), sibling).start()

    def finish(ins, outs, sems):
        me, sibling, chips, c, copy, mine, first = copies(ins, outs, sems)
        passed = [copy(a, 4 + j, (*chip, c), sibling) for j, chip in enumerate(chips) for a in range(n)]
        for a in range(n):
            copy(a, 0, sibling, me).wait_recv()
            for j, chip in enumerate(chips):
                copy(a, 4 + j, (*chip, 1 - c), me).wait_recv()
        for cp in first + passed:
            cp.wait_send()
        for cp in mine:
            cp.wait()

    return _Ride(
        shards, [jax.ShapeDtypeStruct((N_DEV, *s.shape), s.dtype) for s in shards],
        [pltpu.SemaphoreType.DMA((n, 7)), pltpu.SemaphoreType.DMA((n, 7)), pltpu.SemaphoreType.DMA((n,))],
        start, finish, middle)


def _chips_ride(scatter):
    n = len(scatter)

    def copies(ins, outs, sems):
        send_sems, recv_sems, local_sems = sems
        x, y, c = _place()
        own = 2 * x + y
        chips = [(1 - x, y), (x, 1 - y), (1 - x, 1 - y)]
        local = [pltpu.make_async_copy(ins[a].at[own], outs[a].at[own], local_sems.at[a]) for a in range(n)]
        sent = [pltpu.make_async_remote_copy(
            src_ref=ins[a].at[2 * chip[0] + chip[1]], dst_ref=outs[a].at[own],
            send_sem=send_sems.at[a, j], recv_sem=recv_sems.at[a, own], device_id=(*chip, c), device_id_type=MESH)
            for a in range(n) for j, chip in enumerate(chips)]
        return chips, c, local, sent

    def start(ins, outs, sems):
        _, _, local, sent = copies(ins, outs, sems)
        for cp in local + sent:
            cp.start()

    def finish(ins, outs, sems):
        send_sems, recv_sems, _ = sems
        chips, c, local, sent = copies(ins, outs, sems)
        for a in range(n):
            for chip in chips:
                k = 2 * chip[0] + chip[1]
                pltpu.make_async_remote_copy(
                    src_ref=outs[a].at[k], dst_ref=outs[a].at[k], send_sem=send_sems.at[a, 0],
                    recv_sem=recv_sems.at[a, k], device_id=(*chip, c), device_id_type=MESH).wait_recv()
        for cp in sent:
            cp.wait_send()
        for cp in local:
            cp.wait()

    return _Ride(
        list(scatter), [jax.ShapeDtypeStruct(s.shape, s.dtype) for s in scatter],
        [pltpu.SemaphoreType.DMA((n, 3)), pltpu.SemaphoreType.DMA((n, N_CHIP)), pltpu.SemaphoreType.DMA((n,))],
        start, finish)


def _chips_rest_ride(pair, red):
    table = _side_pieces()
    nc = len(CHUNK_ORDER)

    def each(ins, outs, sems, sending, landing):
        send_sems, recv_sems, _ = sems
        pair_ref, red_ref = ins[0], outs[0]
        x, y, c = _place()
        own = 2 * x + y
        for core in (0, 1):
            mine = table[core][0]
            rest = [s for s in range(EARLY_STEPS, nc) if mine[s] is not None]

            @pl.when(c == core)
            def _(mine=mine, rest=rest, core=core):
                for s in rest:
                    _, rows, at = mine[s]
                    k, cp = _to_chip(pair_ref.at[at // SHARD_IN, pl.ds(at % SHARD_IN, rows)], red_ref, mine[s], s,
                                     send_sems, recv_sems, own, core)
                    pl.when(own != k)(lambda cp=cp: sending(cp))
                if landing is not None:
                    for k in range(N_CHIP):
                        @pl.when(own == k)
                        def _(k=k):
                            for s in rest:
                                _, rows, at = mine[s]
                                if at // SHARD_IN == k:
                                    for chip in range(N_CHIP):
                                        if chip != k:
                                            landing(_from_chip(pair_ref.at[k, pl.ds(at % SHARD_IN, rows)], red_ref, mine[s], s,
                                                               send_sems, recv_sems, chip, (x, y, c)))

    def local(ins, outs, sems):
        x, y, _ = _place()
        return pltpu.make_async_copy(ins[0].at[2 * x + y], outs[0].at[2 * x + y], sems[2])

    def start(ins, outs, sems):
        local(ins, outs, sems).start()
        each(ins, outs, sems, lambda cp: cp.start(), None)

    def finish(ins, outs, sems):
        each(ins, outs, sems, lambda cp: cp.wait_send(), lambda cp: cp.wait_recv())
        local(ins, outs, sems).wait()

    return _Ride([pair, red], [jax.ShapeDtypeStruct(red.shape, red.dtype)],
                 [pltpu.SemaphoreType.DMA((nc,)), pltpu.SemaphoreType.DMA((N_CHIP, nc)), pltpu.SemaphoreType.DMA],
                 start, finish, aliases={1: 0})


def _allreduce_small(pack, name):
    shape = pack.shape

    def body(x_ref, o_ref, sib_ref, chip_ref, send_sems, recv_sems):
        x, y, c = _place()
        own = 2 * x + y
        chips = [(1 - x, y), (x, 1 - y), (1 - x, 1 - y)]
        to_sibling = pltpu.make_async_remote_copy(
            src_ref=x_ref, dst_ref=sib_ref, send_sem=send_sems.at[0], recv_sem=recv_sems.at[0],
            device_id=(x, y, 1 - c), device_id_type=MESH)
        to_sibling.start()
        to_sibling.wait()
        chip_ref[own] = x_ref[...] + sib_ref[...]
        sent = [pltpu.make_async_remote_copy(
            src_ref=chip_ref.at[own], dst_ref=chip_ref.at[own], send_sem=send_sems.at[1 + j],
            recv_sem=recv_sems.at[1 + own], device_id=(*chip, c), device_id_type=MESH) for j, chip in enumerate(chips)]
        for cp in sent:
            cp.start()
        for chip in chips:
            k = 2 * chip[0] + chip[1]
            pltpu.make_async_remote_copy(
                src_ref=chip_ref.at[k], dst_ref=chip_ref.at[k], send_sem=send_sems.at[1],
                recv_sem=recv_sems.at[1 + k], device_id=(*chip, c), device_id_type=MESH).wait_recv()
        for cp in sent:
            cp.wait_send()
        o_ref[...] = (chip_ref[0] + chip_ref[1]) + (chip_ref[2] + chip_ref[3])

    return pl.pallas_call(
        body, name=name, out_shape=jax.ShapeDtypeStruct(shape, F32),
        in_specs=[pl.BlockSpec(memory_space=pltpu.VMEM)], out_specs=pl.BlockSpec(memory_space=pltpu.VMEM),
        scratch_shapes=[pltpu.VMEM(shape, F32), pltpu.VMEM((N_CHIP, *shape), F32),
                        pltpu.SemaphoreType.DMA((4,)), pltpu.SemaphoreType.DMA((1 + N_CHIP,))],
    )(pack)


def _adamw(g, w, m, v):
    m = ADAM_B1 * m + (1.0 - ADAM_B1) * g
    v = ADAM_B2 * v + (1.0 - ADAM_B2) * (g * g)
    m_hat = m / (1.0 - ADAM_B1 ** ADAM_STEP)
    v_hat = v / (1.0 - ADAM_B2 ** ADAM_STEP)
    delta = -ADAM_LR * (m_hat / (jnp.sqrt(v_hat) + ADAM_EPS) + ADAM_WD * w)
    return delta, m, v


def _adam_parts(items, name, tr=None):
    ni = len(items)
    npart, r, c = items[0][0].shape
    tr = r if tr is None else min(tr, r)

    def body(*refs):
        for a in range(ni):
            p_ref, w_ref, m_ref, v_ref = refs[4 * a:4 * a + 4]
            g_ref, d_ref, nm_ref, nv_ref = refs[4 * (ni + a):4 * (ni + a) + 4]
            g = p_ref[0].astype(F32)
            for k in range(1, npart):
                g = g + p_ref[k].astype(F32)
            g_ref[...] = g
            d_ref[...], nm_ref[...], nv_ref[...] = _adamw(g, w_ref[...], m_ref[...], v_ref[...])

    tile = pl.BlockSpec((tr, c), lambda i: (i, 0))
    outs = pl.pallas_call(
        body, name=name, grid=(r // tr,),
        in_specs=[pl.BlockSpec((npart, tr, c), lambda i: (0, i, 0)), tile, tile, tile] * ni,
        out_specs=[tile] * (4 * ni), out_shape=[jax.ShapeDtypeStruct((r, c), F32)] * (4 * ni),
        compiler_params=_params("parallel"))(*[arr for item in items for arr in item])
    return [outs[4 * a:4 * a + 4] for a in range(ni)]


def _block_diag(w):
    w4 = w.reshape(N_GROUPS, 4, RNN_BLOCK_W, RNN_BLOCK_W)
    eye = jnp.eye(4, dtype=w.dtype)
    return jnp.einsum("gbij,bc->gbicj", w4, eye).reshape(N_GROUPS, GROUP_W, GROUP_W).astype(BF16)


SMALL_ROWS = 16
ROW_PRE_G, ROW_BGATE, ROW_CONV_B, ROW_B_A, ROW_B_X, ROW_LAM, ROW_POST_G, ROW_LOSS, ROW_SINKS, ROW_CONV_W = 0, 1, 3, 4, 5, 6, 7, 8, 9, 10


def _pack_stats(st_pre, st_merge, st_rnn, st_post, st_sink, name):
    d = D_MODEL

    def body(pre_ref, mg_ref, rnn_ref, post_ref, sink_ref, o_ref):
        rnn = rnn_ref[...]
        sinks = jnp.concatenate([sink_ref[0:1, :], jnp.zeros((1, d - LANE), F32)], axis=1)
        o_ref[0:8, :] = _rows8([pre_ref[0:1, :], mg_ref[0:1, :], mg_ref[1:2, :], rnn[0:1], rnn[1:2], rnn[2:3], rnn[3:4],
                                post_ref[0:1, :]], d)
        o_ref[8:16, :] = _rows8([post_ref[1:2, :], sinks, rnn[4:5], rnn[5:6], rnn[6:7], rnn[7:8]], d)

    return pl.pallas_call(body, name=name, out_shape=jax.ShapeDtypeStruct((SMALL_ROWS, d), F32))(
        st_pre, st_merge, st_rnn, st_post, st_sink)


def _adam_small(total, w, m, v, name):
    d = D_MODEL
    n_in = len(w)
    rows = (ROW_PRE_G, ROW_BGATE, ROW_CONV_B, ROW_B_A, ROW_B_X, ROW_LAM, ROW_POST_G, ROW_SINKS)

    def grad_of(p_ref, row, shape):
        if shape == (1, 2 * d):
            return jnp.concatenate([p_ref[row:row + 1, :], p_ref[row + 1:row + 2, :]], axis=1)
        if shape == (1, RNN_BLOCKS, RNN_BLOCK_W):
            r = p_ref[row:row + 1, :]
            return jnp.concatenate([r[:, b * RNN_BLOCK_W:(b + 1) * RNN_BLOCK_W] for b in range(RNN_BLOCKS)], axis=0)[None]
        return p_ref[row:row + 1, 0:shape[1]]

    def body(*refs):
        p_ref = refs[0]
        w_refs, m_refs, v_refs = (refs[1 + k * n_in:1 + (k + 1) * n_in] for k in range(3))
        outs = refs[1 + 3 * n_in:]
        for k in range(n_in):
            g = grad_of(p_ref, rows[k], w[k].shape)
            res = (g,) + _adamw(g, w_refs[k][...], m_refs[k][...], v_refs[k][...])
            for kind in range(4):
                outs[kind * n_in + k][...] = res[kind]

    outs = pl.pallas_call(
        body, name=name, out_shape=[jax.ShapeDtypeStruct(a.shape, F32) for _ in range(4) for a in w])(total, *w, *m, *v)
    return [outs[kind * n_in:(kind + 1) * n_in] for kind in range(4)]


def kernel(x, pre_norm_g, w_in, b_gate, conv_w, conv_b, w_rg_a, b_rg_a, w_rg_x, b_rg_x, lru_lambda, attn_sinks, w_rnn_out, w_attn_out, w_out, post_norm_g, loss_target, m_pre_norm_g, m_w_in, m_b_gate, m_conv_w, m_conv_b, m_w_rg_a, m_b_rg_a, m_w_rg_x, m_b_rg_x, m_lru_lambda, m_attn_sinks, m_w_rnn_out, m_w_attn_out, m_w_out, m_post_norm_g, v_pre_norm_g, v_w_in, v_b_gate, v_conv_w, v_conv_b, v_w_rg_a, v_b_rg_a, v_w_rg_x, v_b_rg_x, v_lru_lambda, v_attn_sinks, v_w_rnn_out, v_w_attn_out, v_w_out, v_post_norm_g):
    cx, cy, cc = _place()
    dev = 4 * cx + 2 * cy + cc

    w_in_t, m_in_t, v_in_t = (jnp.transpose(a[0]) for a in (w_in, m_w_in, v_w_in))
    wt_shard = w_in_t.astype(BF16)

    def project(xs, pre_g):
        h, rx_rg, qkv, ag_ml, wt_all = _gather_project(xs, pre_g, wt_shard, "gather_project")
        return h, rx_rg, qkv, ag_ml, wt_all.reshape(D_IN, D_MODEL)

    def late_unpack(landed):
        w_rnn_all, w_attn_all, w_out_all, cw_all = landed
        return dict(w_rnn=w_rnn_all.reshape(D_RNN, D_MODEL), w_attn=w_attn_all.reshape(D_MODEL, D_MODEL),
                    w_out=w_out_all.reshape(D_MODEL, D_MODEL), cw=jnp.transpose(cw_all, (1, 0, 2)).reshape(4, D_RNN))

    late_weights = (_gather_ride([w_rnn_out[0].astype(BF16), w_attn_out[0].astype(BF16), w_out[0].astype(BF16), conv_w[0]]),
                    late_unpack)

    heads = jnp.arange(1, N_Q_HEADS + 1, dtype=F32)
    slopes = jnp.exp2(-ALIBI_MAX_BIAS * heads / N_Q_HEADS)
    b_a = b_rg_a.reshape(1, D_RNN)
    b_x = b_rg_x.reshape(1, D_RNN)
    p = dict(
        pre_g=pre_norm_g, post_g=post_norm_g, b_gate=b_gate, cb=conv_b,
        wbd_a=_block_diag(w_rg_a[0]), b_a=b_a, wbd_x=_block_diag(w_rg_x[0]), b_x=b_x, lam=lru_lambda,
        sm=jnp.pad(attn_sinks, ((0, 1), (0, 0))) + jnp.pad(slopes[None, :], ((1, 0), (0, 0))))

    flat = (RNN_BLOCKS * RNN_BLOCK_W, RNN_BLOCK_W)

    def reduce_out(out_pairs, g_rg_a, g_rg_x):
        return _join_rides(_chips_ride(out_pairs), _gather_ride([g_rg_a.reshape(flat), g_rg_x.reshape(flat)]))

    reduce_in = _chips_rest_ride

    g = _local_grads(x[0], loss_target[0], p, project, late_weights, reduce_out, reduce_in)
    small = _allreduce_small(
        _pack_stats(g["st_pre"], g["st_merge"], g["st_rnn"], g["st_post"], g["st_sink"], "pack_stats"), "allreduce_small")

    out = {}
    red = g["red_out"]
    w_in_out, = _adam_parts([(g["red_in"][0], w_in_t, m_in_t, v_in_t)], "adam_w_in", tr=SHARD_IN // 2)
    out["w_in"] = [jnp.transpose(o) for o in w_in_out]
    out["w_rnn_out"], out["w_attn_out"], out["w_out"] = _adam_parts(
        [(red[0], w_rnn_out[0], m_w_rnn_out[0], v_w_rnn_out[0]), (red[1], w_attn_out[0], m_w_attn_out[0], v_w_attn_out[0]),
         (red[2], w_out[0], m_w_out[0], v_w_out[0])], "adam_w_outs")
    out["w_rg_a"], out["w_rg_x"] = _adam_parts(
        [(red[3], w_rg_a.reshape(flat), m_w_rg_a.reshape(flat), v_w_rg_a.reshape(flat)),
         (red[4], w_rg_x.reshape(flat), m_w_rg_x.reshape(flat), v_w_rg_x.reshape(flat))], "adam_w_rg")

    small_names = ("pre_norm_g", "b_gate", "conv_b", "b_rg_a", "b_rg_x", "lru_lambda", "post_norm_g", "attn_sinks")
    small_out = _adam_small(
        small,
        (pre_norm_g, b_gate, conv_b, b_rg_a, b_rg_x, lru_lambda, post_norm_g, attn_sinks),
        (m_pre_norm_g, m_b_gate, m_conv_b, m_b_rg_a, m_b_rg_x, m_lru_lambda, m_post_norm_g, m_attn_sinks),
        (v_pre_norm_g, v_b_gate, v_conv_b, v_b_rg_a, v_b_rg_x, v_lru_lambda, v_post_norm_g, v_attn_sinks),
        "adam_small")
    g_cw = lax.dynamic_slice(small[ROW_CONV_W:ROW_CONV_W + 4], (0, dev * SHARD_OUT), (4, SHARD_OUT))
    out["conv_w"], = _adam_parts([(g_cw[None], conv_w[0], m_conv_w[0], v_conv_w[0])], "adam_conv_w")

    shapes = dict(w_in=(1, D_MODEL, SHARD_IN), w_rnn_out=(1, SHARD_OUT, D_MODEL), w_attn_out=(1, SHARD_OUT, D_MODEL),
                  w_out=(1, SHARD_OUT, D_MODEL), w_rg_a=(1, RNN_BLOCKS, RNN_BLOCK_W, RNN_BLOCK_W),
                  w_rg_x=(1, RNN_BLOCKS, RNN_BLOCK_W, RNN_BLOCK_W), conv_w=(1, 4, SHARD_OUT))
    weights = ["pre_norm_g", "w_in", "b_gate", "conv_w", "conv_b", "w_rg_a", "b_rg_a", "w_rg_x", "b_rg_x",
               "lru_lambda", "attn_sinks", "w_rnn_out", "w_attn_out", "w_out", "post_norm_g"]
    results = []
    for kind in range(4):
        for name in weights:
            if name in out:
                results.append(out[name][kind].reshape(shapes[name]))
            else:
                results.append(small_out[kind][small_names.index(name)])
    loss = 0.5 / D_MODEL * jnp.sum(small[ROW_LOSS])
    return (loss, g["grad_x"][None], *results)
```

```python
import jax
import jax.numpy as jnp
from jax import lax
from jax.experimental import pallas as pl
from jax.experimental.pallas import tpu as pltpu

F32 = jnp.float32
BF16 = jnp.bfloat16

D_MODEL = 1024
D_RNN = 1024
RNN_BLOCKS = 16
RNN_BLOCK_W = 64
LRU_C = 8.0
N_Q_HEADS = 16
HEAD_DIM = 64
D_KV = 256
BLOCK = 128
ALIBI_MAX_BIAS = 8.0
EPS = 1e-6
D_IN = 6656
N_DEV = 8
N_CHIP = 4
SHARD_IN = D_IN // N_DEV
SHARD_OUT = D_MODEL // N_DEV
ATTN_SCALE = HEAD_DIM ** -0.5
MASKED = -1e30

ADAM_LR = 0.001
ADAM_B1 = 0.9
ADAM_B2 = 0.999
ADAM_EPS = 1e-08
ADAM_WD = 0.01
ADAM_STEP = 10

VMEM_LIMIT_BYTES = 52 * 1024 * 1024
LANE = 128
GROUP_W = 256
N_GROUPS = D_RNN // GROUP_W
SEG_CHUNK = 512

NT_DIMS = (((1,), (1,)), ((), ()))
TN_DIMS = (((0,), (0,)), ((), ()))
MESH = pl.DeviceIdType.MESH
ANY = pl.BlockSpec(memory_space=pl.ANY)


def _params(*semantics):
    return pltpu.CompilerParams(dimension_semantics=semantics, vmem_limit_bytes=VMEM_LIMIT_BYTES)


def _sigmoid(x):
    return 0.5 * jnp.tanh(0.5 * x) + 0.5


def _log1p(e):
    u = 1.0 + e
    den = jnp.where(u == 1.0, 1.0, u - 1.0)
    return jnp.where(u == 1.0, e, jnp.log(u) * (e / den))


def _softplus(z):
    return jnp.maximum(z, 0.0) + _log1p(jnp.exp(-jnp.abs(z)))


def _rows8(rows, width):
    idx = lax.broadcasted_iota(jnp.int32, (8, width), 0)
    out = jnp.zeros((8, width), F32)
    for r, v in enumerate(rows):
        out = jnp.where(idx == r, v, out)
    return out


class _Ride:
    def __init__(self, arrays, out_shapes, scratch_shapes, start, finish, middle=None, aliases=None):
        self.arrays, self.out_shapes, self.scratch_shapes = list(arrays), list(out_shapes), list(scratch_shapes)
        self.start, self.finish, self.middle = start, finish, middle
        self.aliases = dict(aliases or {})


class _Hosted:
    def __init__(self, ride, n_in, n_out, n_scratch=0, aliasing=False):
        self.ride = ride
        assert aliasing or not (ride and ride.aliases), "this host does not alias"
        self.aliases = {n_in + i: n_out + o for i, o in ride.aliases.items()} if ride else {}
        self.sizes = (n_in, len(ride.arrays) if ride else 0, n_out, len(ride.out_shapes) if ride else 0, n_scratch)
        self.arrays = ride.arrays if ride else []
        self.in_specs = [ANY] * len(self.arrays)
        self.out_shapes = ride.out_shapes if ride else []
        self.out_specs = [ANY] * len(self.out_shapes)
        self.scratch_shapes = ride.scratch_shapes if ride else []

    def split(self, refs):
        n_in, r_in, n_out, r_out, n_scr = self.sizes
        cuts = [0, n_in, n_in + r_in, n_in + r_in + n_out, n_in + r_in + n_out + r_out, n_in + r_in + n_out + r_out + n_scr]
        host_in, ride_in, host_out, ride_out, host_scr = (refs[cuts[k]:cuts[k + 1]] for k in range(5))
        ride_scr = refs[cuts[5]:]

        def start(when):
            if self.ride is not None:
                pl.when(when)(lambda: self.ride.start(ride_in, ride_out, ride_scr))

        def finish(when):
            if self.ride is not None:
                pl.when(when)(lambda: self.ride.finish(ride_in, ride_out, ride_scr))

        def middle(when):
            if self.ride is not None and self.ride.middle is not None:
                pl.when(when)(lambda: self.ride.middle(ride_in, ride_out, ride_scr))

        start.middle = middle
        return tuple(host_in) + tuple(host_out) + tuple(host_scr), start, finish

    def results(self, outs, n_out):
        outs = list(outs) if isinstance(outs, (list, tuple)) else [outs]
        return outs[:n_out], outs[n_out:]


def _join_rides(a, b):
    na, nb_ = len(a.arrays), len(b.arrays)
    oa = len(a.out_shapes)
    sa = len(a.scratch_shapes)

    def both(fa, fb):
        def run(ins, outs, sems):
            if fa is not None:
                fa(ins[:na], outs[:oa], sems[:sa])
            if fb is not None:
                fb(ins[na:na + nb_], outs[oa:], sems[sa:])
        return run

    middle = both(a.middle, b.middle) if (a.middle or b.middle) else None
    return _Ride(a.arrays + b.arrays, a.out_shapes + b.out_shapes, a.scratch_shapes + b.scratch_shapes,
                 both(a.start, b.start), both(a.finish, b.finish), middle)


def _load_resident(w_hbm, w_vmem, sems, first):
    def piece(c):
        rows = pl.ds(c * SEG_CHUNK, SEG_CHUNK)
        return pltpu.make_async_copy(w_hbm.at[rows], w_vmem.at[rows], sems.at[c])

    @pl.when(first)
    def _():
        for c in range(w_vmem.shape[0] // SEG_CHUNK):
            piece(c).start()

    def ready(c):
        @pl.when(first)
        def _():
            piece(c).wait()

    return ready


CHIP_ROWS = 2 * SHARD_IN
PROJ_WIDTHS = (2 * D_RNN, D_MODEL + 2 * D_KV, 3 * D_MODEL)
PROJ_DTYPES = (F32, BF16, BF16)


def _chip_pieces():
    starts = [0, PROJ_WIDTHS[0], PROJ_WIDTHS[0] + PROJ_WIDTHS[1], D_IN]
    pieces = []
    for k in range(N_CHIP):
        lo, hi = k * CHIP_ROWS, (k + 1) * CHIP_ROWS
        cur = []
        for a in range(len(PROJ_WIDTHS)):
            s0, s1 = max(lo, starts[a]), min(hi, starts[a + 1])
            if s0 < s1:
                cur.append((a, s0 - starts[a], s1 - s0, s0 - lo))
        pieces.append(cur)
    return pieces


def _gather_project(x, g, wt_shard, name, tm=1024):
    t, k = x.shape
    tm = min(tm, t)
    nt = t // tm
    pieces = _chip_pieces()

    def body(x_ref, g_ref, shard_ref, h_out, rx_ref, qkv_ref, ag_ref, wt_all,
             w_c, stage, o32, o16, h_all, send_sems, recv_sems, local_sem, stage_sems, out_sems, h_sems):
        s, ti = pl.program_id(0), pl.program_id(1)
        px, py, pc = _place()
        me, sibling = (px, py, pc), (px, py, 1 - pc)
        chips = [(px, py), (1 - px, py), (px, 1 - py), (1 - px, 1 - py)]
        outs = (rx_ref, qkv_ref, ag_ref)

        def slot(dev):
            return wt_all.at[4 * dev[0] + 2 * dev[1] + dev[2]]

        def copy(kk, block, to, src=None):
            return pltpu.make_async_remote_copy(
                src_ref=slot(block) if src is None else src, dst_ref=slot(block),
                send_sem=send_sems.at[kk], recv_sem=recv_sems.at[kk], device_id=to, device_id_type=MESH)

        mine = pltpu.make_async_copy(shard_ref, slot(me), local_sem)
        first = [copy(0, me, sibling, src=shard_ref)] + [copy(1 + j, me, (*chips[1 + j], pc), src=shard_ref) for j in range(3)]
        passed = [copy(4 + j, (*chips[1 + j], pc), sibling) for j in range(3)]

        @pl.when((s == 0) & (ti == 0))
        def _():
            mine.start()
            for cp in first[:3]:
                cp.start()

        def pass_on(step):
            copy(step, (*chips[step], pc), me).wait_recv()
            passed[step - 1].start()

        @pl.when((s == 2) & (ti == nt - 1))
        def _():
            pass_on(3)

        for step in range(N_CHIP):
            @pl.when((s == step) & (ti == 0))
            def _(step=step):
                chip = chips[step]
                if step == 0:
                    mine.wait()
                    copy(0, sibling, me).wait_recv()
                else:
                    if step == 1:
                        pass_on(1)
                        first[3].start()
                        pass_on(2)
                    copy(3 + step, (*chip, 1 - pc), me).wait_recv()
                loads = [pltpu.make_async_copy(slot((*chip, core)), stage.at[pl.ds(core * SHARD_IN, SHARD_IN)], stage_sems.at[core])
                         for core in (0, 1)]
                for cp in loads:
                    cp.start()
                for cp in loads:
                    cp.wait()
                w_c[...] = stage[...].T

        rows = pl.ds(pl.multiple_of(ti * tm, tm), tm)

        @pl.when(s == 0)
        def _():
            xv = x_ref[...]
            h_all[rows, :] = (xv * lax.rsqrt(jnp.mean(xv * xv, axis=-1, keepdims=True) + EPS) * g_ref[...]).astype(BF16)

        res = jnp.dot(h_all[rows, :], w_c[...], preferred_element_type=F32)

        n = s * nt + ti
        buf = lax.rem(n, 2)
        chip_idx = [2 * cx + cy for cx, cy in chips]

        def chip_at(step):
            return jnp.where(step == 0, chip_idx[0], jnp.where(step == 1, chip_idx[1], jnp.where(step == 2, chip_idx[2], chip_idx[3])))

        def h_write(b, tile):
            return pltpu.make_async_copy(h_all.at[pl.ds(tile * tm, tm)], h_out.at[pl.ds(tile * tm, tm)], h_sems.at[b])

        def writes(kchip, b, tile):
            cps = []
            for idx, (a, col, w, src) in enumerate(pieces[kchip]):
                staged = (o16 if PROJ_DTYPES[a] == BF16 else o32).at[b, :, pl.ds(src, w)]
                cps.append(pltpu.make_async_copy(staged, outs[a].at[pl.ds(tile * tm, tm), pl.ds(col, w)], out_sems.at[b, idx]))
            return cps

        o32[buf] = res
        o16[buf] = res.astype(BF16)
        kcur = chip_at(s)

        @pl.when(n > 0)
        def _():
            kprev = chip_at(lax.div(n - 1, nt))
            for kchip in range(N_CHIP):
                @pl.when(kprev == kchip)
                def _(kchip=kchip):
                    for cp in writes(kchip, 1 - buf, lax.rem(n - 1, nt)):
                        cp.wait()

            @pl.when(n <= nt)
            def _():
                h_write(1 - buf, n - 1).wait()

        @pl.when(s == 0)
        def _():
            h_write(buf, ti).start()

        for kchip in range(N_CHIP):
            @pl.when(kcur == kchip)
            def _(kchip=kchip):
                for cp in writes(kchip, buf, ti):
                    cp.start()

        @pl.when(n == N_CHIP * nt - 1)
        def _():
            for kchip in range(N_CHIP):
                @pl.when(kcur == kchip)
                def _(kchip=kchip):
                    for cp in writes(kchip, buf, ti):
                        cp.wait()
            for cp in first + passed:
                cp.wait_send()

    tile = pl.BlockSpec((tm, k), lambda s, ti: (jnp.where(s == 0, ti, nt - 1), 0))
    return pl.pallas_call(
        body, name=name, grid=(N_CHIP, nt),
        in_specs=[tile, pl.BlockSpec((1, k), lambda s, ti: (0, 0)), ANY],
        out_specs=[ANY, ANY, ANY, ANY, ANY],
        out_shape=[jax.ShapeDtypeStruct((t, k), BF16)]
        + [jax.ShapeDtypeStruct((t, w), dt) for w, dt in zip(PROJ_WIDTHS, PROJ_DTYPES)]
        + [jax.ShapeDtypeStruct((N_DEV, SHARD_IN, k), BF16)],
        scratch_shapes=[pltpu.VMEM((k, CHIP_ROWS), BF16), pltpu.VMEM((CHIP_ROWS, k), BF16),
                        pltpu.VMEM((2, tm, CHIP_ROWS), F32), pltpu.VMEM((2, tm, CHIP_ROWS), BF16), pltpu.VMEM((t, k), BF16),
                        pltpu.SemaphoreType.DMA((7,)), pltpu.SemaphoreType.DMA((7,)), pltpu.SemaphoreType.DMA,
                        pltpu.SemaphoreType.DMA((2,)), pltpu.SemaphoreType.DMA((2, 2)), pltpu.SemaphoreType.DMA((2,))],
        compiler_params=_params("arbitrary", "arbitrary"))(x, g, wt_shard)


def _mm_tn_pairs(prods, name):
    n_prod = len(prods)
    ktok, m = prods[0][0].shape
    n = prods[0][1].shape[1]
    half, blk = m // 2, m // N_DEV
    ns = 2 * n_prod
    kept = N_CHIP * blk

    def side_pieces(s, side):
        i, hf = divmod(s, 2)
        first = hf * N_DEV // 2
        return [((d - first) * blk, i * kept + (d // 2) * blk) for d in range(first, first + N_DEV // 2) if d % 2 == side]

    def body(*refs):
        a_hbm, b_hbm, pair_refs = refs[:n_prod], refs[n_prod:2 * n_prod], refs[2 * n_prod:3 * n_prod]
        a_buf, b_buf, res_buf, recv_all, pair_all, a_sems, b_sems, send_sems, recv_sems, out_sems = refs[3 * n_prod:]
        step = pl.program_id(0)
        px, py, pc = _place()

        def fetch_a(s):
            return pltpu.make_async_copy(a_hbm[s // 2].at[:, pl.ds((s % 2) * half, half)], a_buf.at[s % 2], a_sems.at[s % 2])

        def fetch_b(i):
            return pltpu.make_async_copy(b_hbm[i], b_buf.at[i % 2], b_sems.at[i % 2])

        @pl.when(step == 0)
        def _():
            fetch_a(0).start()
            fetch_b(0).start()

        for s in range(ns):
            @pl.when(step == s)
            def _(s=s):
                if s + 1 < ns:
                    fetch_a(s + 1).start()
                if s % 2 == 0 and s // 2 + 1 < n_prod:
                    fetch_b(s // 2 + 1).start()
                fetch_a(s).wait()
                if s % 2 == 0:
                    fetch_b(s // 2).wait()

        res_buf[step % 2] = lax.dot_general(a_buf[step % 2], b_buf[(step // 2) % 2], TN_DIMS, preferred_element_type=F32)

        def crossing(s, j, piece):
            off, at = piece
            return pltpu.make_async_remote_copy(
                src_ref=res_buf.at[s % 2, pl.ds(off, blk)], dst_ref=recv_all.at[pl.ds(at, blk)],
                send_sem=send_sems.at[s, j], recv_sem=recv_sems.at[s, j], device_id=(px, py, 1 - pc), device_id_type=MESH)

        def write(s, core):
            at = side_pieces(s, core)[0][1]
            return pltpu.make_async_copy(pair_all.at[pl.ds(at, 2 * blk)], pair_refs[s // 2].at[pl.ds(at % kept, 2 * blk)], out_sems.at[s])

        for core in (0, 1):
            def settle(s, core=core):
                for j, piece in enumerate(side_pieces(s, 1 - core)):
                    crossing(s, j, piece).wait_send()
                for j, (off, at) in enumerate(side_pieces(s, core)):
                    crossing(s, j, (off, at)).wait_recv()
                    pair_all[at:at + blk] = (res_buf[s % 2, off:off + blk] + recv_all[at:at + blk]).astype(BF16)
                write(s, core).start()

            for s in range(ns):
                @pl.when((pc == core) & (step == s))
                def _(s=s, settle=settle, core=core):
                    for j, piece in enumerate(side_pieces(s, 1 - core)):
                        crossing(s, j, piece).start()
                    if s > 0:
                        settle(s - 1)
                    if s == ns - 1:
                        settle(s)
                        for t in range(ns):
                            write(t, core).wait()

    pairs = pl.pallas_call(
        body, name=name, grid=(ns,),
        in_specs=[ANY] * (2 * n_prod), out_specs=[ANY] * n_prod,
        out_shape=[jax.ShapeDtypeStruct((kept, n), BF16)] * n_prod,
        scratch_shapes=[pltpu.VMEM((2, ktok, half), prods[0][0].dtype), pltpu.VMEM((2, ktok, n), prods[0][1].dtype),
                        pltpu.VMEM((2, half, n), F32), pltpu.VMEM((n_prod * kept, n), F32), pltpu.VMEM((n_prod * kept, n), BF16),
                        pltpu.SemaphoreType.DMA((2,)), pltpu.SemaphoreType.DMA((2,)),
                        pltpu.SemaphoreType.DMA((ns, 2)), pltpu.SemaphoreType.DMA((ns, 2)), pltpu.SemaphoreType.DMA((ns,))],
        compiler_params=_params("arbitrary"))(*[a for a, _ in prods], *[b for _, b in prods])
    return [pair.reshape(N_CHIP, blk, n) for pair in pairs]


def _segment_chunks(segs):
    bounds = [0]
    for s in segs:
        bounds.append(bounds[-1] + s.shape[1] // SEG_CHUNK)
    return bounds


def _input_grad(segs, wt, x, g, dy, name, tm=512, ride=None):
    m = segs[0].shape[0]
    rows, n = wt.shape
    tm = min(tm, m)
    bounds = _segment_chunks(segs)
    n_seg = len(segs)
    ni = m // tm
    host = _Hosted(ride, n_seg + 4, 2, 2, aliasing=True)

    def body(*refs):
        host_refs, start, finish = host.split(refs)
        a_refs = host_refs[:n_seg]
        wt_hbm, x_ref, g_ref, dy_ref, gx_ref, st_ref, wt_vmem, sems = host_refs[n_seg:]
        i = pl.program_id(0)
        start(i == 0)

        @pl.when(i == 0)
        def _():
            st_ref[...] = jnp.zeros_like(st_ref)

        ready = _load_resident(wt_hbm, wt_vmem, sems, i == 0)
        dh = None
        for s in range(n_seg):
            for c in range(bounds[s], bounds[s + 1]):
                ready(c)
            part = jnp.dot(a_refs[s][...], wt_vmem[bounds[s] * SEG_CHUNK:bounds[s + 1] * SEG_CHUNK, :], preferred_element_type=F32)
            dh = part if dh is None else dh + part
        xv = x_ref[...]
        r = lax.rsqrt(jnp.mean(xv * xv, axis=-1, keepdims=True) + EPS)
        xn = xv * r
        dxn = dh * g_ref[...]
        gx_ref[...] = dy_ref[...] + r * (dxn - xn * jnp.mean(dxn * xn, axis=-1, keepdims=True))
        st_ref[...] += _rows8([jnp.sum(dh * xn, axis=0, keepdims=True)], n)
        finish(i == ni - 1)

    tile = pl.BlockSpec((tm, n), lambda i: (i, 0))
    outs = pl.pallas_call(
        body, name=name, grid=(ni,),
        in_specs=[pl.BlockSpec((tm, sg.shape[1]), lambda i: (i, 0)) for sg in segs]
        + [ANY, tile, pl.BlockSpec((1, n), lambda i: (0, 0)), tile] + host.in_specs,
        out_specs=[tile, pl.BlockSpec((8, n), lambda i: (0, 0))] + host.out_specs,
        out_shape=[jax.ShapeDtypeStruct((m, n), F32), jax.ShapeDtypeStruct((8, n), F32)] + host.out_shapes,
        scratch_shapes=[pltpu.VMEM((rows, n), wt.dtype), pltpu.SemaphoreType.DMA((rows // SEG_CHUNK,))] + host.scratch_shapes,
        input_output_aliases=host.aliases,
        compiler_params=_params("arbitrary"))(*segs, wt, x, g, dy, *host.arrays)
    res, landed = host.results(outs, 2)
    return (*res, landed) if ride else tuple(res)


CHUNK_ORDER = (0, 4, 7, 10, 1, 5, 8, 11, 2, 6, 9, 12, 3)
EARLY_STEPS = 8


def _side_pieces():
    table = []
    for core in (0, 1):
        sides = ([None] * len(CHUNK_ORDER), [None] * len(CHUNK_ORDER))
        for s, cc in enumerate(CHUNK_ORDER):
            g0 = cc * SEG_CHUNK
            for d in range(N_DEV):
                lo, hi = max(g0, d * SHARD_IN), min(g0 + SEG_CHUNK, (d + 1) * SHARD_IN)
                if lo < hi:
                    side = sides[0 if d % 2 == core else 1]
                    assert side[s] is None
                    side[s] = (lo - g0, hi - lo, (d // 2) * SHARD_IN + lo - d * SHARD_IN)
        table.append(sides)
    return table


def _to_chip(src, red_ref, piece, s, send_sems, recv_sems, own, core):
    _, rows, at = piece
    k, r0 = divmod(at, SHARD_IN)
    return k, pltpu.make_async_remote_copy(
        src_ref=src, dst_ref=red_ref.at[own, pl.ds(r0, rows)], send_sem=send_sems.at[s], recv_sem=recv_sems.at[own, s],
        device_id=(k // 2, k % 2, core), device_id_type=MESH)


def _from_chip(src, red_ref, piece, s, send_sems, recv_sems, chip, me):
    _, rows, at = piece
    return pltpu.make_async_remote_copy(
        src_ref=src, dst_ref=red_ref.at[chip, pl.ds(at % SHARD_IN, rows)], send_sem=send_sems.at[s], recv_sem=recv_sems.at[chip, s],
        device_id=me, device_id_type=MESH)


def _mm_tn_seg_pair(segs, b, name):
    ktok = segs[0].shape[0]
    n = b.shape[1]
    bounds = _segment_chunks(segs)
    n_seg = len(segs)
    nc = bounds[-1]
    assert nc == len(CHUNK_ORDER)
    seg_of = [s for s in range(n_seg) for _ in range(bounds[s], bounds[s + 1])]
    table = _side_pieces()

    def body(*refs):
        a_hbm, b_hbm, pair_ref, red_ref = refs[:n_seg], refs[n_seg], refs[n_seg + 1], refs[n_seg + 2]
        (a_buf, b_vmem, res_buf, recv_all, pair_all, a_sems, b_sem, send_sems, recv_sems, out_sems,
         chip_send, chip_recv) = refs[n_seg + 3:]
        step = pl.program_id(0)
        px, py, pc = _place()
        own = 2 * px + py

        def fetch(s):
            sg = seg_of[CHUNK_ORDER[s]]
            cols = pl.ds((CHUNK_ORDER[s] - bounds[sg]) * SEG_CHUNK, SEG_CHUNK)
            return pltpu.make_async_copy(a_hbm[sg].at[:, cols], a_buf.at[s % 2], a_sems.at[s % 2])

        @pl.when(step == 0)
        def _():
            whole = pltpu.make_async_copy(b_hbm, b_vmem, b_sem)
            whole.start()
            fetch(0).start()
            whole.wait()

        for s in range(nc):
            @pl.when(step == s)
            def _(s=s):
                if s + 1 < nc:
                    fetch(s + 1).start()
                fetch(s).wait()

        res_buf[step % 2] = lax.dot_general(a_buf[step % 2], b_vmem[...], TN_DIMS, preferred_element_type=F32)

        def crossing(s, piece):
            off, rows, at = piece
            return pltpu.make_async_remote_copy(
                src_ref=res_buf.at[s % 2, pl.ds(off, rows)], dst_ref=recv_all.at[pl.ds(at, rows)],
                send_sem=send_sems.at[s], recv_sem=recv_sems.at[s], device_id=(px, py, 1 - pc), device_id_type=MESH)

        def write(s, piece):
            _, rows, at = piece
            return pltpu.make_async_copy(pair_all.at[pl.ds(at, rows)], pair_ref.at[pl.ds(at, rows)], out_sems.at[s])

        def to_chip(s, piece, core):
            return _to_chip(pair_all.at[pl.ds(piece[2], piece[1])], red_ref, piece, s, chip_send, chip_recv, own, core)

        for core in (0, 1):
            mine, theirs = table[core]

            def settle(s, mine=mine, theirs=theirs, core=core):
                if theirs[s] is not None:
                    crossing(s, theirs[s]).wait_send()
                if mine[s] is not None:
                    off, rows, at = mine[s]
                    crossing(s, mine[s]).wait_recv()
                    pair_all[at:at + rows] = (res_buf[s % 2, off:off + rows] + recv_all[at:at + rows]).astype(BF16)
                    write(s, mine[s]).start()
                    if s < EARLY_STEPS:
                        k, cp = to_chip(s, mine[s], core)
                        pl.when(own != k)(cp.start)

            for s in range(nc):
                @pl.when((pc == core) & (step == s))
                def _(s=s, settle=settle, mine=mine, theirs=theirs, core=core):
                    if theirs[s] is not None:
                        crossing(s, theirs[s]).start()
                    if s > 0:
                        settle(s - 1)
                    if s == nc - 1:
                        settle(s)
                        kept = [t for t in range(nc) if mine[t] is not None]
                        for t in kept:
                            write(t, mine[t]).wait()
                        early = [t for t in kept if t < EARLY_STEPS]
                        for t in early:
                            k, cp = to_chip(t, mine[t], core)
                            pl.when(own != k)(cp.wait_send)
                        for k in range(N_CHIP):
                            @pl.when(own == k)
                            def _(k=k):
                                for t in early:
                                    if mine[t][2] // SHARD_IN == k:
                                        for chip in range(N_CHIP):
                                            if chip != k:
                                                _from_chip(pair_all.at[pl.ds(mine[t][2], mine[t][1])], red_ref, mine[t], t,
                                                           chip_send, chip_recv, chip, (px, py, pc)).wait_recv()

    flat = jax.ShapeDtypeStruct((N_CHIP * SHARD_IN, n), BF16)
    pair, red = pl.pallas_call(
        body, name=name, grid=(nc,),
        in_specs=[ANY] * (n_seg + 1), out_specs=[ANY, ANY],
        out_shape=[flat, jax.ShapeDtypeStruct((N_CHIP, SHARD_IN, n), BF16)],
        scratch_shapes=[pltpu.VMEM((2, ktok, SEG_CHUNK), segs[0].dtype), pltpu.VMEM((ktok, n), b.dtype),
                        pltpu.VMEM((2, SEG_CHUNK, n), F32), pltpu.VMEM(flat.shape, F32), pltpu.VMEM(flat.shape, BF16),
                        pltpu.SemaphoreType.DMA((2,)), pltpu.SemaphoreType.DMA,
                        pltpu.SemaphoreType.DMA((nc,)), pltpu.SemaphoreType.DMA((nc,)), pltpu.SemaphoreType.DMA((nc,)),
                        pltpu.SemaphoreType.DMA((nc,)), pltpu.SemaphoreType.DMA((N_CHIP, nc))],
        compiler_params=_params("arbitrary"))(*segs, b)
    return pair.reshape(N_CHIP, SHARD_IN, n), red


RING = 3


def _ring_specs(streams, tm):
    return ([pltpu.VMEM((RING, tm, width), a.dtype) for a, _, width in streams]
            + [pltpu.SemaphoreType.DMA((len(streams), RING))])


def _ring_step(streams, hbm_refs, rings, sems, n, n_steps, tm):
    def fetch(step):
        rows = pl.ds(pl.multiple_of(step * tm, tm), tm)
        return [pltpu.make_async_copy(ref.at[rows, pl.ds(col, width)], ring.at[lax.rem(step, RING)], sems.at[k, lax.rem(step, RING)])
                for k, ((_, col, width), ref, ring) in enumerate(zip(streams, hbm_refs, rings))]

    @pl.when(n == 0)
    def _():
        for cp in fetch(0) + fetch(1):
            cp.start()

    @pl.when(n + 2 < n_steps)
    def _():
        for cp in fetch(n + 2):
            cp.start()

    for cp in fetch(n):
        cp.wait()
    return lax.rem(n, RING)


def _branches_fwd(z_rnn, z_attn, ag_ml, b_gate, w_rnn, w_attn, w_out, x, target, g_post, name, tm=512):
    t, d = x.shape
    tm = min(tm, t)
    n_steps = t // tm
    streams = [(z_rnn, 0, d), (z_attn, 0, d), (ag_ml, d, d), (ag_ml, 2 * d, d), (x, 0, d), (target, 0, d)]
    ns = len(streams)

    def body(*refs):
        hbm = refs[:ns]
        br_ref, ba_ref, wr_ref, wa_ref, wo_ref, g_ref, brr_ref, bra_ref, mg_ref, do_ref, dy_ref, st_ref = refs[ns:ns + 12]
        rings, sems = refs[ns + 12:2 * ns + 12], refs[2 * ns + 12]
        slot = _ring_step(streams, hbm, rings, sems, pl.program_id(0), n_steps, tm)
        zr_ref, za_ref, lr_ref, la_ref, x_ref, t_ref = (ring.at[slot] for ring in rings)

        @pl.when(pl.program_id(0) == 0)
        def _():
            st_ref[...] = jnp.zeros_like(st_ref)

        br_rnn = jnp.dot(zr_ref[...], wr_ref[...], preferred_element_type=F32)
        br_attn = jnp.dot(za_ref[...], wa_ref[...], preferred_element_type=F32)
        brr_ref[...] = br_rnn.astype(BF16)
        bra_ref[...] = br_attn.astype(BF16)
        g_rnn = _sigmoid(lr_ref[...].astype(F32) + br_ref[...])
        g_attn = _sigmoid(la_ref[...].astype(F32) + ba_ref[...])
        merged = (g_rnn * br_rnn + g_attn * br_attn).astype(BF16)
        mg_ref[...] = merged
        o = jnp.dot(merged, wo_ref[...], preferred_element_type=F32)
        g = g_ref[...]
        r = lax.rsqrt(jnp.mean(o * o, axis=-1, keepdims=True) + EPS)
        nrm = o * r
        err = x_ref[...] + nrm * g - t_ref[...]
        dy = err * (1.0 / d)
        dy_ref[...] = dy
        dn = dy * g
        do_ref[...] = (r * (dn - nrm * jnp.mean(dn * nrm, axis=-1, keepdims=True))).astype(BF16)
        st_ref[...] += _rows8([jnp.sum(dy * nrm, axis=0, keepdims=True), jnp.sum(err * err, axis=0, keepdims=True)], d)

    tile = pl.BlockSpec((tm, d), lambda i: (i, 0))
    weight = pl.BlockSpec((d, d), lambda i: (0, 0))
    bf = jax.ShapeDtypeStruct((t, d), BF16)
    return pl.pallas_call(
        body, name=name, grid=(n_steps,),
        in_specs=[ANY] * ns + [pl.BlockSpec((1, d), lambda i: (0, 0)), pl.BlockSpec((1, d), lambda i: (0, 1)),
                               weight, weight, weight, pl.BlockSpec((1, d), lambda i: (0, 0))],
        out_specs=[tile, tile, tile, tile, tile, pl.BlockSpec((8, d), lambda i: (0, 0))],
        out_shape=[bf, bf, bf, bf, jax.ShapeDtypeStruct((t, d), F32), jax.ShapeDtypeStruct((8, d), F32)],
        scratch_shapes=_ring_specs(streams, tm),
        compiler_params=_params("arbitrary"))(*[a for a, _, _ in streams], b_gate, b_gate, w_rnn, w_attn, w_out, g_post)


def _branches_bwd(dout, br_rnn, br_attn, ag_ml, b_gate, w_rnn, w_attn, w_out, name, tm=512):
    t, d = br_rnn.shape
    tm = min(tm, t)
    n_steps = t // tm
    streams = [(dout, 0, d), (br_rnn, 0, d), (br_attn, 0, d), (ag_ml, d, d), (ag_ml, 2 * d, d)]
    ns = len(streams)

    def body(*refs):
        hbm = refs[:ns]
        br_ref, ba_ref, wr_ref, wa_ref, wo_ref, dr_ref, da_ref, dl_ref, dzr_ref, dza_ref, st_ref, wt_ref = refs[ns:ns + 12]
        rings, sems = refs[ns + 12:2 * ns + 12], refs[2 * ns + 12]
        slot = _ring_step(streams, hbm, rings, sems, pl.program_id(0), n_steps, tm)
        do_ref, r_ref, a_ref, lr_ref, la_ref = (ring.at[slot] for ring in rings)

        @pl.when(pl.program_id(0) == 0)
        def _():
            st_ref[...] = jnp.zeros_like(st_ref)
            wt_ref[0] = wo_ref[...].T
            wt_ref[1] = wr_ref[...].T
            wt_ref[2] = wa_ref[...].T

        dm = jnp.dot(do_ref[...], wt_ref[0], preferred_element_type=F32)
        g_rnn = _sigmoid(lr_ref[...].astype(F32) + br_ref[...])
        g_attn = _sigmoid(la_ref[...].astype(F32) + ba_ref[...])
        dbr_rnn = (dm * g_rnn).astype(BF16)
        dbr_attn = (dm * g_attn).astype(BF16)
        dr_ref[...] = dbr_rnn
        da_ref[...] = dbr_attn
        dl_rnn = dm * r_ref[...].astype(F32) * g_rnn * (1.0 - g_rnn)
        dl_attn = dm * a_ref[...].astype(F32) * g_attn * (1.0 - g_attn)
        dl_ref[:, 0:d] = dl_rnn.astype(BF16)
        dl_ref[:, d:2 * d] = dl_attn.astype(BF16)
        st_ref[...] += _rows8([jnp.sum(dl_rnn, axis=0, keepdims=True), jnp.sum(dl_attn, axis=0, keepdims=True)], d)
        dzr_ref[...] = jnp.dot(dbr_rnn, wt_ref[1], preferred_element_type=F32).astype(BF16)
        dza_ref[...] = jnp.dot(dbr_attn, wt_ref[2], preferred_element_type=F32).astype(BF16)

    tile = pl.BlockSpec((tm, d), lambda i: (i, 0))
    weight = pl.BlockSpec((d, d), lambda i: (0, 0))
    bf = jax.ShapeDtypeStruct((t, d), BF16)
    return pl.pallas_call(
        body, name=name, grid=(n_steps,),
        in_specs=[ANY] * ns + [pl.BlockSpec((1, d), lambda i: (0, 0)), pl.BlockSpec((1, d), lambda i: (0, 1)), weight, weight, weight],
        out_specs=[tile, tile, pl.BlockSpec((tm, 2 * d), lambda i: (i, 0)), tile, tile, pl.BlockSpec((8, d), lambda i: (0, 0))],
        out_shape=[bf, bf, jax.ShapeDtypeStruct((t, 2 * d), BF16), bf, bf, jax.ShapeDtypeStruct((8, d), F32)],
        scratch_shapes=[pltpu.VMEM((3, d, d), BF16)] + _ring_specs(streams, tm),
        compiler_params=_params("arbitrary"))(*[a for a, _, _ in streams], b_gate, b_gate, w_rnn, w_attn, w_out)


def _lru_decay(r, sp):
    log_a = (-LRU_C) * r * sp
    return log_a, jnp.exp(log_a)


def _lru_gates(c, wa, ba, wx, bx, sp):
    cb = c.astype(BF16)
    r = _sigmoid(jnp.dot(cb, wa, preferred_element_type=F32) + ba)
    ig = _sigmoid(jnp.dot(cb, wx, preferred_element_type=F32) + bx)
    log_a, a = _lru_decay(r, sp)
    mult = jnp.sqrt(-jnp.tanh(log_a) * (a * a + 1.0))
    return cb, r, ig, a, mult


SUBLANES = 8


def _scan_fwd(a, u, carry, tt):
    w = a.shape[1]
    ng = tt // SUBLANES
    a3 = a.reshape(ng, SUBLANES, w)
    u3 = u.reshape(ng, SUBLANES, w)
    sub = lax.broadcasted_iota(jnp.int32, (ng, SUBLANES, w), 1)
    d = 1
    while d < SUBLANES:
        keep = sub >= d
        u3 = u3 + a3 * jnp.where(keep, pltpu.roll(u3, d, 1), 0.0)
        a3 = a3 * jnp.where(keep, pltpu.roll(a3, d, 1), 1.0)
        d *= 2
    out = []
    for g in range(ng):
        hg = u3[g] + a3[g] * carry
        out.append(hg)
        carry = hg[SUBLANES - 1:SUBLANES, :]
    return jnp.concatenate(out, axis=0)


def _scan_rev(b, g, carry, tt):
    w = b.shape[1]
    ng = tt // SUBLANES
    b3 = b.reshape(ng, SUBLANES, w)
    g3 = g.reshape(ng, SUBLANES, w)
    sub = lax.broadcasted_iota(jnp.int32, (ng, SUBLANES, w), 1)
    d = 1
    while d < SUBLANES:
        keep = sub < SUBLANES - d
        g3 = g3 + b3 * jnp.where(keep, pltpu.roll(g3, SUBLANES - d, 1), 0.0)
        b3 = b3 * jnp.where(keep, pltpu.roll(b3, SUBLANES - d, 1), 1.0)
        d *= 2
    out = [None] * ng
    for k in range(ng - 1, -1, -1):
        hk = g3[k] + b3[k] * carry
        out[k] = hk
        carry = hk[0:1, :]
    return jnp.concatenate(out, axis=0)


def _conv_taps(cw, bias, x, ext_ref, tt):
    x2 = ext_ref[7:7 + tt, :]
    x1 = ext_ref[6:6 + tt, :]
    x0 = ext_ref[5:5 + tt, :]
    return bias + cw[3:4] * x + cw[2:3] * x2 + cw[1:2] * x1 + cw[0:1] * x0


def _rnn_fwd(rx_rg, cw, cb, wa, ba, wx, bx, lam, name, tt=512):
    t = rx_rg.shape[0]
    tt = min(tt, t)
    w = GROUP_W

    nt = t // tt
    n_steps = N_GROUPS * nt

    def body(in_hbm, cw_ref, cb_ref, wa_ref, ba_ref, wx_ref, bx_ref, lam_ref,
             y_ref, z_ref, c_ref, r_ref, ig_ref, mult_ref, ext_ref, hc_ref, x_ring, g_ring, ring_sems):
        @pl.when(pl.program_id(1) == 0)
        def _():
            ext_ref[0:8, :] = jnp.zeros((8, w), F32)
            hc_ref[...] = jnp.zeros((8, w), F32)

        n = pl.program_id(0) * nt + pl.program_id(1)

        def fetch(step):
            slot = lax.rem(step, 3)
            rows = pl.ds(pl.multiple_of(lax.rem(step, nt) * tt, tt), tt)
            group = lax.div(step, nt)
            return [pltpu.make_async_copy(in_hbm.at[rows, pl.ds(pl.multiple_of((half * N_GROUPS + group) * w, w), w)],
                                          ring.at[slot], ring_sems.at[half, slot])
                    for half, ring in enumerate((x_ring, g_ring))]

        @pl.when(n == 0)
        def _():
            for cp in fetch(0) + fetch(1):
                cp.start()

        @pl.when(n + 2 < n_steps)
        def _():
            for cp in fetch(n + 2):
                cp.start()

        for cp in fetch(n):
            cp.wait()
        slot = lax.rem(n, 3)
        x = x_ring[slot]
        ext_ref[8:8 + tt, :] = x
        c = _conv_taps(cw_ref[...], cb_ref[...], x, ext_ref, tt)
        ext_ref[0:8, :] = x[tt - 8:tt, :]
        sp = _softplus(-lam_ref[...])
        _, r, ig, a, mult = _lru_gates(c, wa_ref[...], ba_ref[...], wx_ref[...], bx_ref[...], sp)
        c_ref[...] = c
        r_ref[...] = r
        ig_ref[...] = ig
        mult_ref[...] = mult
        h = _scan_fwd(a, mult * (ig * c), hc_ref[7:8, :], tt)
        hc_ref[...] = h[tt - 8:tt, :]
        y_ref[...] = h
        rg = g_ring[slot]
        z_ref[...] = (h * rg * _sigmoid(rg)).astype(BF16)

    vec = pl.BlockSpec((1, w), lambda g, i: (0, g))
    mat = pl.BlockSpec((None, w, w), lambda g, i: (g, 0, 0))
    tile = pl.BlockSpec((tt, w), lambda g, i: (i, g))
    return pl.pallas_call(
        body, name=name, grid=(N_GROUPS, nt),
        in_specs=[ANY, pl.BlockSpec((4, w), lambda g, i: (0, g)), vec, mat, vec, mat, vec, vec],
        out_specs=[tile, tile] + [pl.BlockSpec((None, tt, w), lambda g, i: (g, i, 0))] * 4,
        out_shape=[jax.ShapeDtypeStruct((t, D_RNN), F32), jax.ShapeDtypeStruct((t, D_RNN), BF16)]
        + [jax.ShapeDtypeStruct((N_GROUPS, t, w), F32)] * 4,
        scratch_shapes=[pltpu.VMEM((tt + 8, w), F32), pltpu.VMEM((8, w), F32),
                        pltpu.VMEM((3, tt, w), F32), pltpu.VMEM((3, tt, w), F32), pltpu.SemaphoreType.DMA((2, 3))],
        compiler_params=_params("arbitrary", "arbitrary"))(rx_rg, cw, cb, wa, ba, wx, bx, lam)


def _rnn_bwd(rx_rg, y, dz, c, r, ig, mult, cw, wa, wx, lam, name, tt=512):
    t = rx_rg.shape[0]
    tt = min(tt, t)
    nt = t // tt
    w = GROUP_W

    n_steps = N_GROUPS * nt

    def body(in_hbm, y_hbm, dz_hbm, c_hbm, r_hbm, ig_hbm, mult_hbm, yt_ref, cw_ref, wa_ref, wx_ref, lam_ref,
             drx_ref, drg_ref, st_ref, gda_ref, gdx_ref, dcx_ref, wcar_ref, acar_ref, dwa_ref, dwx_ref, *ring_refs):
        ii = pl.program_id(1)
        rings, ring_sems = ring_refs[:-1], ring_refs[-1]
        n = pl.program_id(0) * nt + ii

        def fetch(step):
            slot = lax.rem(step, RING)
            group = lax.div(step, nt)
            rows = pl.ds(pl.multiple_of((nt - 1 - lax.rem(step, nt)) * tt, tt), tt)

            def cols(first_group):
                return pl.ds(pl.multiple_of((first_group + group) * w, w), w)

            srcs = [in_hbm.at[rows, cols(0)], in_hbm.at[rows, cols(N_GROUPS)], y_hbm.at[rows, cols(0)], dz_hbm.at[rows, cols(0)],
                    c_hbm.at[group, rows], r_hbm.at[group, rows], ig_hbm.at[group, rows], mult_hbm.at[group, rows]]
            return [pltpu.make_async_copy(src, ring.at[slot], ring_sems.at[k, slot])
                    for k, (src, ring) in enumerate(zip(srcs, rings))]

        @pl.when(n == 0)
        def _():
            for cp in fetch(0) + fetch(1):
                cp.start()

        @pl.when(n + 2 < n_steps)
        def _():
            for cp in fetch(n + 2):
                cp.start()

        for cp in fetch(n):
            cp.wait()
        rx_ref, rg_ref, y_ref, dz_ref, c_ref, r_ref, ig_ref, mult_ref = (ring.at[lax.rem(n, RING)] for ring in rings)

        @pl.when(ii == 0)
        def _():
            wcar_ref[...] = jnp.zeros((8, w), F32)
            acar_ref[...] = jnp.zeros((8, w), F32)
            dcx_ref[tt:tt + 8, :] = jnp.zeros((8, w), F32)
            st_ref[...] = jnp.zeros_like(st_ref)
            dwa_ref[...] = jnp.zeros_like(dwa_ref)
            dwx_ref[...] = jnp.zeros_like(dwx_ref)

        has_prev = jnp.where(ii == nt - 1, 0.0, 1.0)
        cwv = cw_ref[...]
        lam = lam_ref[...]
        sp = _softplus(-lam)
        wa = wa_ref[...]
        wx = wx_ref[...]
        c = c_ref[...]
        r = r_ref[...]
        ig = ig_ref[...]
        mult = mult_ref[...]
        _, a = _lru_decay(r, sp)
        cb16 = c.astype(BF16)

        rg = rg_ref[...]
        sg = _sigmoid(rg)
        dz = dz_ref[...].astype(F32)
        yv = y_ref[...]
        drg_ref[...] = (dz * yv * (sg * (1.0 + rg * (1.0 - sg)))).astype(BF16)

        row = lax.broadcasted_iota(jnp.int32, (tt, w), 0)
        b = jnp.where(row < tt - 1, pltpu.roll(a, tt - 1, 0), acar_ref[0:1, :])
        dh = _scan_rev(b, dz * (rg * sg), wcar_ref[0:1, :], tt)
        wcar_ref[...] = dh[0:8, :]
        acar_ref[...] = a[0:8, :]

        hprev = jnp.where(row >= 1, pltpu.roll(yv, 1, 0), yt_ref[7:8, :] * has_prev)
        dmult = dh * (ig * c)
        dig = dh * mult * c
        dlog_a = dh * hprev * a - dmult * (a * a / mult)
        dpa = dlog_a * ((-LRU_C) * sp) * r * (1.0 - r)
        dpx = dig * ig * (1.0 - ig)
        dsp = jnp.sum(dlog_a * r, axis=0, keepdims=True) * (-LRU_C)
        dlam = dsp * (-_sigmoid(-lam))
        dpa16 = dpa.astype(BF16)
        dpx16 = dpx.astype(BF16)
        dwa_ref[...] += lax.dot_general(cb16, dpa16, TN_DIMS, preferred_element_type=F32)
        dwx_ref[...] += lax.dot_general(cb16, dpx16, TN_DIMS, preferred_element_type=F32)
        dc = (dh * mult * ig
              + lax.dot_general(dpa16, wa, NT_DIMS, preferred_element_type=F32)
              + lax.dot_general(dpx16, wx, NT_DIMS, preferred_element_type=F32))

        dcx_ref[0:tt, :] = dc
        dc1 = dcx_ref[1:1 + tt, :]
        dc2 = dcx_ref[2:2 + tt, :]
        dc3 = dcx_ref[3:3 + tt, :]
        drx_ref[...] = (cwv[3:4] * dc + cwv[2:3] * dc1 + cwv[1:2] * dc2 + cwv[0:1] * dc3).astype(BF16)
        dcx_ref[tt:tt + 8, :] = dc[0:8, :]

        def colsum(v):
            return jnp.sum(v, axis=0, keepdims=True)

        x = rx_ref[...]
        st_ref[...] += _rows8([colsum(dc), colsum(dpa), colsum(dpx), dlam,
                               colsum(dc3 * x), colsum(dc2 * x), colsum(dc1 * x), colsum(dc * x)], w)

        @pl.when(ii == nt - 1)
        def _():
            for blk in range(GROUP_W // RNN_BLOCK_W):
                rows = slice(blk * RNN_BLOCK_W, (blk + 1) * RNN_BLOCK_W)
                gda_ref[blk] = dwa_ref[rows, rows]
                gdx_ref[blk] = dwx_ref[rows, rows]

    def rev(ii):
        return nt - 1 - ii

    def tail(g, ii):
        return (jnp.maximum(rev(ii) * (tt // 8) - 1, 0), g)

    vec = pl.BlockSpec((1, w), lambda g, ii: (0, g))
    mat = pl.BlockSpec((None, w, w), lambda g, ii: (g, 0, 0))
    tile = pl.BlockSpec((tt, w), lambda g, ii: (rev(ii), g))
    diag_shape = (N_GROUPS, GROUP_W // RNN_BLOCK_W, RNN_BLOCK_W, RNN_BLOCK_W)
    diag = pl.BlockSpec((None,) + diag_shape[1:], lambda g, ii: (g, 0, 0, 0))
    ring_dtypes = [F32, F32, F32, dz.dtype, F32, F32, F32, F32]
    return pl.pallas_call(
        body, name=name, grid=(N_GROUPS, nt),
        in_specs=[ANY] * 7 + [pl.BlockSpec((8, w), tail), pl.BlockSpec((4, w), lambda g, ii: (0, g)), mat, mat, vec],
        out_specs=[tile, tile, pl.BlockSpec((8, w), lambda g, ii: (0, g)), diag, diag],
        out_shape=[jax.ShapeDtypeStruct((t, D_RNN), BF16), jax.ShapeDtypeStruct((t, D_RNN), BF16),
                   jax.ShapeDtypeStruct((8, D_RNN), F32),
                   jax.ShapeDtypeStruct(diag_shape, F32), jax.ShapeDtypeStruct(diag_shape, F32)],
        scratch_shapes=[pltpu.VMEM((tt + 8, w), F32), pltpu.VMEM((8, w), F32),
                        pltpu.VMEM((8, w), F32), pltpu.VMEM((w, w), F32), pltpu.VMEM((w, w), F32)]
        + [pltpu.VMEM((RING, tt, w), dt) for dt in ring_dtypes] + [pltpu.SemaphoreType.DMA((len(ring_dtypes), RING))],
        compiler_params=_params("arbitrary", "arbitrary"))(rx_rg, y, dz, c, r, ig, mult, y, cw, wa, wx, lam)


def _half_mask(shape, half):
    lane = lax.broadcasted_iota(jnp.int32, shape, 1)
    return (lane >= HEAD_DIM) if half else (lane < HEAD_DIM)


def _dup_half(t, half):
    sel = jnp.where(_half_mask(t.shape, half), t, 0.0)
    return sel + pltpu.roll(sel, HEAD_DIM, 1)


def _band_geometry(n):
    qi = lax.broadcasted_iota(jnp.int32, (BLOCK, 2 * BLOCK), 0)
    kj = lax.broadcasted_iota(jnp.int32, (BLOCK, 2 * BLOCK), 1)
    dist = BLOCK + qi - kj
    first_key = jnp.where(n > 0, 0, BLOCK)
    valid = (dist >= 0) & (dist < BLOCK) & (kj >= first_key)
    return dist.astype(F32), valid


GROUP = 4


def _kv_dup(prev_ref, cur_ref, hk, scale=1.0):
    tile = hk // 2
    kt = jnp.concatenate([prev_ref[:, tile * LANE:(tile + 1) * LANE], cur_ref[:, tile * LANE:(tile + 1) * LANE]], axis=0)
    return (_dup_half(kt.astype(F32), hk % 2) * scale).astype(BF16)


def _fill_bias(bias_ref, sm_ref, n):
    distf, valid = _band_geometry(n)
    for head in range(N_Q_HEADS):
        bias_ref[head] = jnp.where(valid, -sm_ref[1, head] * distf, MASKED)


def _head_scores(q2s, half, kdup, bias):
    qm = jnp.where(_half_mask(q2s.shape, half), q2s, jnp.zeros_like(q2s))
    return qm, lax.dot_general(qm, kdup, NT_DIMS, preferred_element_type=F32) + bias


def _attn_specs(nb, clamp_last):
    def blk(n):
        return jnp.minimum(n, nb - 1) if clamp_last else n

    q_spec = pl.BlockSpec((BLOCK, D_MODEL), lambda n: (blk(n), 0))
    k_prev = pl.BlockSpec((BLOCK, D_KV), lambda n: (jnp.maximum(blk(n) - 1, 0), D_MODEL // D_KV))
    k_cur = pl.BlockSpec((BLOCK, D_KV), lambda n: (blk(n), D_MODEL // D_KV))
    v_prev = pl.BlockSpec((BLOCK, D_KV), lambda n: (jnp.maximum(blk(n) - 1, 0), D_MODEL // D_KV + 1))
    v_cur = pl.BlockSpec((BLOCK, D_KV), lambda n: (blk(n), D_MODEL // D_KV + 1))
    return q_spec, k_prev, k_cur, v_prev, v_cur


def _attn_fwd(sm, qkv, ag_ml, name, ride=None):
    t = qkv.shape[0]
    nb = t // BLOCK

    host = _Hosted(ride, 7, 3, 1)

    def body(*refs):
        (sm_ref, q_ref, kp_ref, kc_ref, vp_ref, vc_ref, ag_ref, y_ref, z_ref, lse_ref, bias_ref), start, finish = host.split(refs)
        n = pl.program_id(0)
        start(n == 0)
        start.middle(n == (3 * nb) // 4)

        @pl.when(n <= 1)
        def _():
            _fill_bias(bias_ref, sm_ref, n)

        lane = lax.broadcasted_iota(jnp.int32, (BLOCK, LANE), 1)
        low = lane < HEAD_DIM
        lse = jnp.zeros((BLOCK, LANE), F32)
        for hk in range(N_Q_HEADS // GROUP):
            kdup = _kv_dup(kp_ref, kc_ref, hk)
            vdup = _kv_dup(vp_ref, vc_ref, hk)
            for k in (0, 1):
                c = slice((2 * hk + k) * LANE, (2 * hk + k + 1) * LANE)
                q2s = q_ref[:, c] * ATTN_SCALE
                outs = []
                for half in (0, 1):
                    head = GROUP * hk + 2 * k + half
                    _, s = _head_scores(q2s, half, kdup, bias_ref[head])
                    sink = sm_ref[0, head]
                    m = jnp.maximum(jnp.max(s, axis=1, keepdims=True), sink)
                    e = jnp.exp(s - m)
                    l = jnp.sum(e, axis=1, keepdims=True) + jnp.exp(sink - m)
                    outs.append(jnp.dot((e * (1.0 / l)).astype(BF16), vdup, preferred_element_type=F32))
                    lse = jnp.where(lane == head, m + jnp.log(l), lse)
                yt = jnp.where(low, outs[0], outs[1])
                y_ref[:, c] = yt
                ag = ag_ref[:, c].astype(F32)
                z_ref[:, c] = (yt * ag * _sigmoid(ag)).astype(BF16)
        lse_ref[...] = lse
        finish(n == nb - 1)

    q_spec, k_prev, k_cur, v_prev, v_cur = _attn_specs(nb, False)
    wide = pl.BlockSpec((BLOCK, D_MODEL), lambda n: (n, 0))
    outs = pl.pallas_call(
        body, name=name, grid=(nb,),
        in_specs=[pl.BlockSpec(memory_space=pltpu.SMEM), q_spec, k_prev, k_cur, v_prev, v_cur, wide] + host.in_specs,
        out_specs=[wide, wide, pl.BlockSpec((BLOCK, LANE), lambda n: (n, 0))] + host.out_specs,
        out_shape=[jax.ShapeDtypeStruct((t, D_MODEL), F32), jax.ShapeDtypeStruct((t, D_MODEL), BF16),
                   jax.ShapeDtypeStruct((t, LANE), F32)] + host.out_shapes,
        scratch_shapes=[pltpu.VMEM((N_Q_HEADS, BLOCK, 2 * BLOCK), F32)] + host.scratch_shapes,
        compiler_params=_params("arbitrary"))(sm, qkv, qkv, qkv, qkv, qkv, ag_ml, *host.arrays)
    res, landed = host.results(outs, 3)
    return (*res, landed) if ride else tuple(res)


def _attn_bwd(sm, qkv, ag_ml, y, lse, dz, name, ride=None):
    t = qkv.shape[0]
    nb = t // BLOCK
    host = _Hosted(ride, 10, 4, 3)

    def body(*refs):
        host_refs, start, finish = host.split(refs)
        (sm_ref, q_ref, kp_ref, kc_ref, vp_ref, vc_ref, ag_ref, y_ref, lse_ref, dz_ref,
         dq_ref, dkv_ref, dag_ref, ds_ref, ck_ref, cv_ref, bias_ref) = host_refs
        n = pl.program_id(0)
        start(n == 0)
        start.middle(n == (3 * nb) // 4)

        @pl.when(n == 0)
        def _():
            ck_ref[...] = jnp.zeros_like(ck_ref)
            cv_ref[...] = jnp.zeros_like(cv_ref)
            ds_ref[...] = jnp.zeros_like(ds_ref)

        @pl.when(n <= 1)
        def _():
            _fill_bias(bias_ref, sm_ref, n)

        @pl.when(n < nb)
        def _():
            lane8 = lax.broadcasted_iota(jnp.int32, (8, LANE), 1)
            row8 = lax.broadcasted_iota(jnp.int32, (8, LANE), 0)
            dsink = jnp.zeros((8, LANE), F32)
            dk_heads, dv_heads = [], []
            lse_tile = lse_ref[...]
            for hk in range(N_Q_HEADS // GROUP):
                kdup = _kv_dup(kp_ref, kc_ref, hk)
                ks = _kv_dup(kp_ref, kc_ref, hk, ATTN_SCALE)
                vdup = _kv_dup(vp_ref, vc_ref, hk)
                qms, dyhs, y_rows = [], [], []
                for k in (0, 1):
                    c = slice((2 * hk + k) * LANE, (2 * hk + k + 1) * LANE)
                    ag = ag_ref[:, c].astype(F32)
                    sg = _sigmoid(ag)
                    dzt = dz_ref[:, c].astype(F32)
                    yt = y_ref[:, c]
                    dag_ref[:, c] = (dzt * yt * (sg * (1.0 + ag * (1.0 - sg)))).astype(BF16)
                    dyt = dzt * (ag * sg)
                    q2s = q_ref[:, c] * ATTN_SCALE
                    for half in (0, 1):
                        hm = _half_mask(q2s.shape, half)
                        qms.append(jnp.where(hm, q2s, jnp.zeros_like(q2s)))
                        dyhs.append(jnp.where(hm, dyt, 0.0))
                        y_rows.append(yt)
                qm4 = jnp.concatenate(qms, axis=0)
                dy4 = jnp.concatenate(dyhs, axis=0)
                dy4_16 = dy4.astype(BF16)
                s4 = lax.dot_general(qm4, kdup, NT_DIMS, preferred_element_type=F32)
                dp4 = lax.dot_general(dy4_16, vdup, NT_DIMS, preferred_element_type=F32)
                probs16, ds16 = [], []
                for r in range(GROUP):
                    head = GROUP * hk + r
                    rows = slice(r * BLOCK, (r + 1) * BLOCK)
                    lh = lse_tile[:, head:head + 1]
                    probs = jnp.exp(s4[rows] + bias_ref[head] - lh)
                    psink = jnp.exp(sm_ref[0, head] - lh)
                    delta = jnp.sum(dyhs[r] * y_rows[r], axis=1, keepdims=True)
                    ds16.append((probs * (dp4[rows] - delta)).astype(BF16))
                    probs16.append(probs.astype(BF16))
                    dsink = dsink + jnp.where((row8 == 0) & (lane8 == head),
                                              -jnp.sum(psink * delta, axis=0, keepdims=True), 0.0)
                ds4 = jnp.concatenate(ds16, axis=0)
                p4 = jnp.concatenate(probs16, axis=0)
                dq4 = jnp.dot(ds4, ks, preferred_element_type=F32)
                low = _half_mask((BLOCK, LANE), 0)
                for k in (0, 1):
                    c = slice((2 * hk + k) * LANE, (2 * hk + k + 1) * LANE)
                    dq_ref[:, c] = jnp.where(low, dq4[2 * k * BLOCK:(2 * k + 1) * BLOCK],
                                             dq4[(2 * k + 1) * BLOCK:(2 * k + 2) * BLOCK]).astype(BF16)
                dk_acc = lax.dot_general(ds4, qm4, TN_DIMS, preferred_element_type=F32)
                dv_acc = lax.dot_general(p4, dy4_16, TN_DIMS, preferred_element_type=F32)
                dk_heads.append(dk_acc + pltpu.roll(dk_acc, HEAD_DIM, 1))
                dv_heads.append(dv_acc + pltpu.roll(dv_acc, HEAD_DIM, 1))
            ds_ref[...] += dsink
            low = _half_mask((2 * BLOCK, LANE), 0)
            for tile in range(2):
                cols = slice(tile * LANE, (tile + 1) * LANE)
                dkt = jnp.where(low, dk_heads[2 * tile], dk_heads[2 * tile + 1])
                dvt = jnp.where(low, dv_heads[2 * tile], dv_heads[2 * tile + 1])
                dkv_ref[:, cols] = (ck_ref[:, cols] + dkt[0:BLOCK, :]).astype(BF16)
                dkv_ref[:, D_KV + tile * LANE:D_KV + (tile + 1) * LANE] = (cv_ref[:, cols] + dvt[0:BLOCK, :]).astype(BF16)
                ck_ref[:, cols] = dkt[BLOCK:2 * BLOCK, :]
                cv_ref[:, cols] = dvt[BLOCK:2 * BLOCK, :]

        @pl.when(n == nb)
        def _():
            dkv_ref[:, 0:D_KV] = ck_ref[...].astype(BF16)
            dkv_ref[:, D_KV:2 * D_KV] = cv_ref[...].astype(BF16)

        finish(n == nb)

    q_spec, k_prev, k_cur, v_prev, v_cur = _attn_specs(nb, True)
    wide = pl.BlockSpec((BLOCK, D_MODEL), lambda n: (jnp.minimum(n, nb - 1), 0))
    outs = pl.pallas_call(
        body, name=name, grid=(nb + 1,),
        in_specs=[pl.BlockSpec(memory_space=pltpu.SMEM), q_spec, k_prev, k_cur, v_prev, v_cur, wide, wide,
                  pl.BlockSpec((BLOCK, LANE), lambda n: (jnp.minimum(n, nb - 1), 0)), wide] + host.in_specs,
        out_specs=[wide, pl.BlockSpec((BLOCK, 2 * D_KV), lambda n: (jnp.maximum(n - 1, 0), 0)), wide,
                   pl.BlockSpec((8, LANE), lambda n: (0, 0))] + host.out_specs,
        out_shape=[jax.ShapeDtypeStruct((t, D_MODEL), BF16), jax.ShapeDtypeStruct((t, 2 * D_KV), BF16),
                   jax.ShapeDtypeStruct((t, D_MODEL), BF16), jax.ShapeDtypeStruct((8, LANE), F32)] + host.out_shapes,
        scratch_shapes=[pltpu.VMEM((BLOCK, D_KV), F32), pltpu.VMEM((BLOCK, D_KV), F32),
                        pltpu.VMEM((N_Q_HEADS, BLOCK, 2 * BLOCK), F32)] + host.scratch_shapes,
        compiler_params=_params("arbitrary"))(sm, qkv, qkv, qkv, qkv, qkv, ag_ml, y, lse, dz, *host.arrays)
    res, landed = host.results(outs, 4)
    return (*res, landed) if ride else tuple(res)


def _local_grads(x, target, p, project, late_weights, reduce_out, reduce_in):
    h, rx_rg, qkv, ag_ml, wt = project(x, p["pre_g"])
    y_attn, z_attn, lse, landed = _attn_fwd(p["sm"], qkv, ag_ml, "attn_fwd", ride=late_weights[0])
    p = {**p, **late_weights[1](landed)}
    y_rnn, z_rnn, *kept_rnn = _rnn_fwd(
        rx_rg, p["cw"], p["cb"], p["wbd_a"], p["b_a"], p["wbd_x"], p["b_x"], p["lam"], "rnn_fwd")
    br_rnn, br_attn, merged, dout, dy, st_post = _branches_fwd(
        z_rnn, z_attn, ag_ml, p["b_gate"], p["w_rnn"], p["w_attn"], p["w_out"], x, target, p["post_g"], "branches_fwd")

    dbr_rnn, dbr_attn, d_ml, dz_rnn, dz_attn, st_merge = _branches_bwd(
        dout, br_rnn, br_attn, ag_ml, p["b_gate"], p["w_rnn"], p["w_attn"], p["w_out"], "branches_bwd")
    out_pairs = _mm_tn_pairs([(z_rnn, dbr_rnn), (z_attn, dbr_attn), (merged, dout)], "gw_outs")
    d_rx, d_rg, st_rnn, g_rg_a, g_rg_x = _rnn_bwd(
        rx_rg, y_rnn, dz_rnn, *kept_rnn, p["cw"], p["wbd_a"], p["wbd_x"], p["lam"], "rnn_bwd")
    dq, dkv, d_ag, st_sink, red_out = _attn_bwd(p["sm"], qkv, ag_ml, y_attn, lse, dz_attn, "attn_bwd",
                                                ride=reduce_out(out_pairs, g_rg_a, g_rg_x))

    segs = [d_rx, d_rg, dq, dkv, d_ag, d_ml]
    grad_x, st_pre, red_in = _input_grad(segs, wt, x, p["pre_g"], dy, "input_grad",
                                         ride=reduce_in(*_mm_tn_seg_pair(segs, h, "gw_in")))
    return dict(grad_x=grad_x, st_post=st_post, st_merge=st_merge, st_rnn=st_rnn, st_sink=st_sink, st_pre=st_pre,
                red_out=red_out, red_in=red_in)


def _place():
    x, y, c = lax.axis_index("x"), lax.axis_index("y"), lax.axis_index("c")
    return x, y, c


def _gather_ride(shards):
    n = len(shards)

    def copies(ins, outs, sems):
        send_sems, recv_sems, local_sems = sems
        x, y, c = _place()
        me, sibling = (x, y, c), (x, y, 1 - c)
        chips = [(1 - x, y), (x, 1 - y), (1 - x, 1 - y)]

        def slot(a, dev):
            return outs[a].at[4 * dev[0] + 2 * dev[1] + dev[2]]

        def copy(a, k, block, to, src=None):
            return pltpu.make_async_remote_copy(
                src_ref=slot(a, block) if src is None else src, dst_ref=slot(a, block),
                send_sem=send_sems.at[a, k], recv_sem=recv_sems.at[a, k], device_id=to, device_id_type=MESH)

        mine = [pltpu.make_async_copy(ins[a], slot(a, me), local_sems.at[a]) for a in range(n)]
        first = []
        for a in range(n):
            first.append(copy(a, 0, me, sibling, src=ins[a]))
            first += [copy(a, 1 + j, me, (*chip, c), src=ins[a]) for j, chip in enumerate(chips)]
        return me, sibling, chips, c, copy, mine, first

    def start(ins, outs, sems):
        *_, mine, first = copies(ins, outs, sems)
        for cp in mine + first:
            cp.start()

    def middle(ins, outs, sems):
        me, sibling, chips, c, copy, _, _ = copies(ins, outs, sems)
        for j, chip in enumerate(chips):
            for a in range(n):
                copy(a, 1 + j, (*chip, c), me).wait_recv()
                copy(a, 4 + j, (*chip, c), sibling).start()

    def finish(ins, outs, sems):
        me, sibling, chips, c, copy, mine, first = copies(ins, outs, sems)
        passed = [copy(a, 4 + j, (*chip, c), sibling) for j, chip in enumerate(chips) for a in range(n)]
        for a in range(n):
            copy(a, 0, sibling, me).wait_recv()
            for j, chip in enumerate(chips):
                copy(a, 4 + j, (*chip, 1 - c), me).wait_recv()
        for cp in first + passed:
            cp.wait_send()
        for cp in mine:
            cp.wait()

    return _Ride(
        shards, [jax.ShapeDtypeStruct((N_DEV, *s.shape), s.dtype) for s in shards],
        [pltpu.SemaphoreType.DMA((n, 7)), pltpu.SemaphoreType.DMA((n, 7)), pltpu.SemaphoreType.DMA((n,))],
        start, finish, middle)


def _chips_ride(scatter):
    n = len(scatter)

    def copies(ins, outs, sems):
        send_sems, recv_sems, local_sems = sems
        x, y, c = _place()
        own = 2 * x + y
        chips = [(1 - x, y), (x, 1 - y), (1 - x, 1 - y)]
        local = [pltpu.make_async_copy(ins[a].at[own], outs[a].at[own], local_sems.at[a]) for a in range(n)]
        sent = [pltpu.make_async_remote_copy(
            src_ref=ins[a].at[2 * chip[0] + chip[1]], dst_ref=outs[a].at[own],
            send_sem=send_sems.at[a, j], recv_sem=recv_sems.at[a, own], device_id=(*chip, c), device_id_type=MESH)
            for a in range(n) for j, chip in enumerate(chips)]
        return chips, c, local, sent

    def start(ins, outs, sems):
        _, _, local, sent = copies(ins, outs, sems)
        for cp in local + sent:
            cp.start()

    def finish(ins, outs, sems):
        send_sems, recv_sems, _ = sems
        chips, c, local, sent = copies(ins, outs, sems)
        for a in range(n):
            for chip in chips:
                k = 2 * chip[0] + chip[1]
                pltpu.make_async_remote_copy(
                    src_ref=outs[a].at[k], dst_ref=outs[a].at[k], send_sem=send_sems.at[a, 0],
                    recv_sem=recv_sems.at[a, k], device_id=(*chip, c), device_id_type=MESH).wait_recv()
        for cp in sent:
            cp.wait_send()
        for cp in local:
            cp.wait()

    return _Ride(
        list(scatter), [jax.ShapeDtypeStruct(s.shape, s.dtype) for s in scatter],
        [pltpu.SemaphoreType.DMA((n, 3)), pltpu.SemaphoreType.DMA((n, N_CHIP)), pltpu.SemaphoreType.DMA((n,))],
        start, finish)


def _chips_rest_ride(pair, red):
    table = _side_pieces()
    nc = len(CHUNK_ORDER)

    def each(ins, outs, sems, sending, landing):
        send_sems, recv_sems, _ = sems
        pair_ref, red_ref = ins[0], outs[0]
        x, y, c = _place()
        own = 2 * x + y
        for core in (0, 1):
            mine = table[core][0]
            rest = [s for s in range(EARLY_STEPS, nc) if mine[s] is not None]

            @pl.when(c == core)
            def _(mine=mine, rest=rest, core=core):
                for s in rest:
                    _, rows, at = mine[s]
                    k, cp = _to_chip(pair_ref.at[at // SHARD_IN, pl.ds(at % SHARD_IN, rows)], red_ref, mine[s], s,
                                     send_sems, recv_sems, own, core)
                    pl.when(own != k)(lambda cp=cp: sending(cp))
                if landing is not None:
                    for k in range(N_CHIP):
                        @pl.when(own == k)
                        def _(k=k):
                            for s in rest:
                                _, rows, at = mine[s]
                                if at // SHARD_IN == k:
                                    for chip in range(N_CHIP):
                                        if chip != k:
                                            landing(_from_chip(pair_ref.at[k, pl.ds(at % SHARD_IN, rows)], red_ref, mine[s], s,
                                                               send_sems, recv_sems, chip, (x, y, c)))

    def local(ins, outs, sems):
        x, y, _ = _place()
        return pltpu.make_async_copy(ins[0].at[2 * x + y], outs[0].at[2 * x + y], sems[2])

    def start(ins, outs, sems):
        local(ins, outs, sems).start()
        each(ins, outs, sems, lambda cp: cp.start(), None)

    def finish(ins, outs, sems):
        each(ins, outs, sems, lambda cp: cp.wait_send(), lambda cp: cp.wait_recv())
        local(ins, outs, sems).wait()

    return _Ride([pair, red], [jax.ShapeDtypeStruct(red.shape, red.dtype)],
                 [pltpu.SemaphoreType.DMA((nc,)), pltpu.SemaphoreType.DMA((N_CHIP, nc)), pltpu.SemaphoreType.DMA],
                 start, finish, aliases={1: 0})


def _allreduce_small(pack, name):
    shape = pack.shape

    def body(x_ref, o_ref, sib_ref, chip_ref, send_sems, recv_sems):
        x, y, c = _place()
        own = 2 * x + y
        chips = [(1 - x, y), (x, 1 - y), (1 - x, 1 - y)]
        to_sibling = pltpu.make_async_remote_copy(
            src_ref=x_ref, dst_ref=sib_ref, send_sem=send_sems.at[0], recv_sem=recv_sems.at[0],
            device_id=(x, y, 1 - c), device_id_type=MESH)
        to_sibling.start()
        to_sibling.wait()
        chip_ref[own] = x_ref[...] + sib_ref[...]
        sent = [pltpu.make_async_remote_copy(
            src_ref=chip_ref.at[own], dst_ref=chip_ref.at[own], send_sem=send_sems.at[1 + j],
            recv_sem=recv_sems.at[1 + own], device_id=(*chip, c), device_id_type=MESH) for j, chip in enumerate(chips)]
        for cp in sent:
            cp.start()
        for chip in chips:
            k = 2 * chip[0] + chip[1]
            pltpu.make_async_remote_copy(
                src_ref=chip_ref.at[k], dst_ref=chip_ref.at[k], send_sem=send_sems.at[1],
                recv_sem=recv_sems.at[1 + k], device_id=(*chip, c), device_id_type=MESH).wait_recv()
        for cp in sent:
            cp.wait_send()
        o_ref[...] = (chip_ref[0] + chip_ref[1]) + (chip_ref[2] + chip_ref[3])

    return pl.pallas_call(
        body, name=name, out_shape=jax.ShapeDtypeStruct(shape, F32),
        in_specs=[pl.BlockSpec(memory_space=pltpu.VMEM)], out_specs=pl.BlockSpec(memory_space=pltpu.VMEM),
        scratch_shapes=[pltpu.VMEM(shape, F32), pltpu.VMEM((N_CHIP, *shape), F32),
                        pltpu.SemaphoreType.DMA((4,)), pltpu.SemaphoreType.DMA((1 + N_CHIP,))],
    )(pack)


def _adamw(g, w, m, v):
    m = ADAM_B1 * m + (1.0 - ADAM_B1) * g
    v = ADAM_B2 * v + (1.0 - ADAM_B2) * (g * g)
    m_hat = m / (1.0 - ADAM_B1 ** ADAM_STEP)
    v_hat = v / (1.0 - ADAM_B2 ** ADAM_STEP)
    delta = -ADAM_LR * (m_hat / (jnp.sqrt(v_hat) + ADAM_EPS) + ADAM_WD * w)
    return delta, m, v


def _adam_parts(items, name, tr=None):
    ni = len(items)
    npart, r, c = items[0][0].shape
    tr = r if tr is None else min(tr, r)

    def body(*refs):
        for a in range(ni):
            p_ref, w_ref, m_ref, v_ref = refs[4 * a:4 * a + 4]
            g_ref, d_ref, nm_ref, nv_ref = refs[4 * (ni + a):4 * (ni + a) + 4]
            g = p_ref[0].astype(F32)
            for k in range(1, npart):
                g = g + p_ref[k].astype(F32)
            g_ref[...] = g
            d_ref[...], nm_ref[...], nv_ref[...] = _adamw(g, w_ref[...], m_ref[...], v_ref[...])

    tile = pl.BlockSpec((tr, c), lambda i: (i, 0))
    outs = pl.pallas_call(
        body, name=name, grid=(r // tr,),
        in_specs=[pl.BlockSpec((npart, tr, c), lambda i: (0, i, 0)), tile, tile, tile] * ni,
        out_specs=[tile] * (4 * ni), out_shape=[jax.ShapeDtypeStruct((r, c), F32)] * (4 * ni),
        compiler_params=_params("parallel"))(*[arr for item in items for arr in item])
    return [outs[4 * a:4 * a + 4] for a in range(ni)]


def _block_diag(w):
    w4 = w.reshape(N_GROUPS, 4, RNN_BLOCK_W, RNN_BLOCK_W)
    eye = jnp.eye(4, dtype=w.dtype)
    return jnp.einsum("gbij,bc->gbicj", w4, eye).reshape(N_GROUPS, GROUP_W, GROUP_W).astype(BF16)


SMALL_ROWS = 16
ROW_PRE_G, ROW_BGATE, ROW_CONV_B, ROW_B_A, ROW_B_X, ROW_LAM, ROW_POST_G, ROW_LOSS, ROW_SINKS, ROW_CONV_W = 0, 1, 3, 4, 5, 6, 7, 8, 9, 10


def _pack_stats(st_pre, st_merge, st_rnn, st_post, st_sink, name):
    d = D_MODEL

    def body(pre_ref, mg_ref, rnn_ref, post_ref, sink_ref, o_ref):
        rnn = rnn_ref[...]
        sinks = jnp.concatenate([sink_ref[0:1, :], jnp.zeros((1, d - LANE), F32)], axis=1)
        o_ref[0:8, :] = _rows8([pre_ref[0:1, :], mg_ref[0:1, :], mg_ref[1:2, :], rnn[0:1], rnn[1:2], rnn[2:3], rnn[3:4],
                                post_ref[0:1, :]], d)
        o_ref[8:16, :] = _rows8([post_ref[1:2, :], sinks, rnn[4:5], rnn[5:6], rnn[6:7], rnn[7:8]], d)

    return pl.pallas_call(body, name=name, out_shape=jax.ShapeDtypeStruct((SMALL_ROWS, d), F32))(
        st_pre, st_merge, st_rnn, st_post, st_sink)


def _adam_small(total, w, m, v, name):
    d = D_MODEL
    n_in = len(w)
    rows = (ROW_PRE_G, ROW_BGATE, ROW_CONV_B, ROW_B_A, ROW_B_X, ROW_LAM, ROW_POST_G, ROW_SINKS)

    def grad_of(p_ref, row, shape):
        if shape == (1, 2 * d):
            return jnp.concatenate([p_ref[row:row + 1, :], p_ref[row + 1:row + 2, :]], axis=1)
        if shape == (1, RNN_BLOCKS, RNN_BLOCK_W):
            r = p_ref[row:row + 1, :]
            return jnp.concatenate([r[:, b * RNN_BLOCK_W:(b + 1) * RNN_BLOCK_W] for b in range(RNN_BLOCKS)], axis=0)[None]
        return p_ref[row:row + 1, 0:shape[1]]

    def body(*refs):
        p_ref = refs[0]
        w_refs, m_refs, v_refs = (refs[1 + k * n_in:1 + (k + 1) * n_in] for k in range(3))
        outs = refs[1 + 3 * n_in:]
        for k in range(n_in):
            g = grad_of(p_ref, rows[k], w[k].shape)
            res = (g,) + _adamw(g, w_refs[k][...], m_refs[k][...], v_refs[k][...])
            for kind in range(4):
                outs[kind * n_in + k][...] = res[kind]

    outs = pl.pallas_call(
        body, name=name, out_shape=[jax.ShapeDtypeStruct(a.shape, F32) for _ in range(4) for a in w])(total, *w, *m, *v)
    return [outs[kind * n_in:(kind + 1) * n_in] for kind in range(4)]


def kernel(x, pre_norm_g, w_in, b_gate, conv_w, conv_b, w_rg_a, b_rg_a, w_rg_x, b_rg_x, lru_lambda, attn_sinks, w_rnn_out, w_attn_out, w_out, post_norm_g, loss_target, m_pre_norm_g, m_w_in, m_b_gate, m_conv_w, m_conv_b, m_w_rg_a, m_b_rg_a, m_w_rg_x, m_b_rg_x, m_lru_lambda, m_attn_sinks, m_w_rnn_out, m_w_attn_out, m_w_out, m_post_norm_g, v_pre_norm_g, v_w_in, v_b_gate, v_conv_w, v_conv_b, v_w_rg_a, v_b_rg_a, v_w_rg_x, v_b_rg_x, v_lru_lambda, v_attn_sinks, v_w_rnn_out, v_w_attn_out, v_w_out, v_post_norm_g):
    cx, cy, cc = _place()
    dev = 4 * cx + 2 * cy + cc

    w_in_t, m_in_t, v_in_t = (jnp.transpose(a[0]) for a in (w_in, m_w_in, v_w_in))
    wt_shard = w_in_t.astype(BF16)

    def project(xs, pre_g):
        h, rx_rg, qkv, ag_ml, wt_all = _gather_project(xs, pre_g, wt_shard, "gather_project")
        return h, rx_rg, qkv, ag_ml, wt_all.reshape(D_IN, D_MODEL)

    def late_unpack(landed):
        w_rnn_all, w_attn_all, w_out_all, cw_all = landed
        return dict(w_rnn=w_rnn_all.reshape(D_RNN, D_MODEL), w_attn=w_attn_all.reshape(D_MODEL, D_MODEL),
                    w_out=w_out_all.reshape(D_MODEL, D_MODEL), cw=jnp.transpose(cw_all, (1, 0, 2)).reshape(4, D_RNN))

    late_weights = (_gather_ride([w_rnn_out[0].astype(BF16), w_attn_out[0].astype(BF16), w_out[0].astype(BF16), conv_w[0]]),
                    late_unpack)

    heads = jnp.arange(1, N_Q_HEADS + 1, dtype=F32)
    slopes = jnp.exp2(-ALIBI_MAX_BIAS * heads / N_Q_HEADS)
    b_a = b_rg_a.reshape(1, D_RNN)
    b_x = b_rg_x.reshape(1, D_RNN)
    p = dict(
        pre_g=pre_norm_g, post_g=post_norm_g, b_gate=b_gate, cb=conv_b,
        wbd_a=_block_diag(w_rg_a[0]), b_a=b_a, wbd_x=_block_diag(w_rg_x[0]), b_x=b_x, lam=lru_lambda,
        sm=jnp.pad(attn_sinks, ((0, 1), (0, 0))) + jnp.pad(slopes[None, :], ((1, 0), (0, 0))))

    flat = (RNN_BLOCKS * RNN_BLOCK_W, RNN_BLOCK_W)

    def reduce_out(out_pairs, g_rg_a, g_rg_x):
        return _join_rides(_chips_ride(out_pairs), _gather_ride([g_rg_a.reshape(flat), g_rg_x.reshape(flat)]))

    reduce_in = _chips_rest_ride

    g = _local_grads(x[0], loss_target[0], p, project, late_weights, reduce_out, reduce_in)
    small = _allreduce_small(
        _pack_stats(g["st_pre"], g["st_merge"], g["st_rnn"], g["st_post"], g["st_sink"], "pack_stats"), "allreduce_small")

    out = {}
    red = g["red_out"]
    w_in_out, = _adam_parts([(g["red_in"][0], w_in_t, m_in_t, v_in_t)], "adam_w_in", tr=SHARD_IN // 2)
    out["w_in"] = [jnp.transpose(o) for o in w_in_out]
    out["w_rnn_out"], out["w_attn_out"], out["w_out"] = _adam_parts(
        [(red[0], w_rnn_out[0], m_w_rnn_out[0], v_w_rnn_out[0]), (red[1], w_attn_out[0], m_w_attn_out[0], v_w_attn_out[0]),
         (red[2], w_out[0], m_w_out[0], v_w_out[0])], "adam_w_outs")
    out["w_rg_a"], out["w_rg_x"] = _adam_parts(
        [(red[3], w_rg_a.reshape(flat), m_w_rg_a.reshape(flat), v_w_rg_a.reshape(flat)),
         (red[4], w_rg_x.reshape(flat), m_w_rg_x.reshape(flat), v_w_rg_x.reshape(flat))], "adam_w_rg")

    small_names = ("pre_norm_g", "b_gate", "conv_b", "b_rg_a", "b_rg_x", "lru_lambda", "post_norm_g", "attn_sinks")
    small_out = _adam_small(
        small,
        (pre_norm_g, b_gate, conv_b, b_rg_a, b_rg_x, lru_lambda, post_norm_g, attn_sinks),
        (m_pre_norm_g, m_b_gate, m_conv_b, m_b_rg_a, m_b_rg_x, m_lru_lambda, m_post_norm_g, m_attn_sinks),
        (v_pre_norm_g, v_b_gate, v_conv_b, v_b_rg_a, v_b_rg_x, v_lru_lambda, v_post_norm_g, v_attn_sinks),
        "adam_small")
    g_cw = lax.dynamic_slice(small[ROW_CONV_W:ROW_CONV_W + 4], (0, dev * SHARD_OUT), (4, SHARD_OUT))
    out["conv_w"], = _adam_parts([(g_cw[None], conv_w[0], m_conv_w[0], v_conv_w[0])], "adam_conv_w")

    shapes = dict(w_in=(1, D_MODEL, SHARD_IN), w_rnn_out=(1, SHARD_OUT, D_MODEL), w_attn_out=(1, SHARD_OUT, D_MODEL),
                  w_out=(1, SHARD_OUT, D_MODEL), w_rg_a=(1, RNN_BLOCKS, RNN_BLOCK_W, RNN_BLOCK_W),
                  w_rg_x=(1, RNN_BLOCKS, RNN_BLOCK_W, RNN_BLOCK_W), conv_w=(1, 4, SHARD_OUT))
    weights = ["pre_norm_g", "w_in", "b_gate", "conv_w", "conv_b", "w_rg_a", "b_rg_a", "w_rg_x", "b_rg_x",
               "lru_lambda", "attn_sinks", "w_rnn_out", "w_attn_out", "w_out", "post_norm_g"]
    results = []
    for kind in range(4):
        for name in weights:
            if name in out:
                results.append(out[name][kind].reshape(shapes[name]))
            else:
                results.append(small_out[kind][small_names.index(name)])
    loss = 0.5 / D_MODEL * jnp.sum(small[ROW_LOSS])
    return (loss, g["grad_x"][None], *results)
```

```python
import jax
import jax.numpy as jnp
from jax import lax
from jax.experimental import pallas as pl
from jax.experimental.pallas import tpu as pltpu

F32 = jnp.float32
BF16 = jnp.bfloat16

D_MODEL = 1024
D_RNN = 1024
RNN_BLOCKS = 16
RNN_BLOCK_W = 64
LRU_C = 8.0
N_Q_HEADS = 16
HEAD_DIM = 64
D_KV = 256
BLOCK = 128
ALIBI_MAX_BIAS = 8.0
EPS = 1e-6
D_IN = 6656
N_DEV = 8
N_CHIP = 4
SHARD_IN = D_IN // N_DEV
SHARD_OUT = D_MODEL // N_DEV
ATTN_SCALE = HEAD_DIM ** -0.5
MASKED = -1e30

ADAM_LR = 0.001
ADAM_B1 = 0.9
ADAM_B2 = 0.999
ADAM_EPS = 1e-08
ADAM_WD = 0.01
ADAM_STEP = 10

VMEM_LIMIT_BYTES = 52 * 1024 * 1024
LANE = 128
GROUP_W = 256
N_GROUPS = D_RNN // GROUP_W
SEG_CHUNK = 512

NT_DIMS = (((1,), (1,)), ((), ()))
TN_DIMS = (((0,), (0,)), ((), ()))
MESH = pl.DeviceIdType.MESH
ANY = pl.BlockSpec(memory_space=pl.ANY)


def _params(*semantics):
    return pltpu.CompilerParams(dimension_semantics=semantics, vmem_limit_bytes=VMEM_LIMIT_BYTES)


def _sigmoid(x):
    return 0.5 * jnp.tanh(0.5 * x) + 0.5


def _log1p(e):
    u = 1.0 + e
    den = jnp.where(u == 1.0, 1.0, u - 1.0)
    return jnp.where(u == 1.0, e, jnp.log(u) * (e / den))


def _softplus(z):
    return jnp.maximum(z, 0.0) + _log1p(jnp.exp(-jnp.abs(z)))


def _rows8(rows, width):
    idx = lax.broadcasted_iota(jnp.int32, (8, width), 0)
    out = jnp.zeros((8, width), F32)
    for r, v in enumerate(rows):
        out = jnp.where(idx == r, v, out)
    return out


class _Ride:
    def __init__(self, arrays, out_shapes, scratch_shapes, start, finish, middle=None, aliases=None):
        self.arrays, self.out_shapes, self.scratch_shapes = list(arrays), list(out_shapes), list(scratch_shapes)
        self.start, self.finish, self.middle = start, finish, middle
        self.aliases = dict(aliases or {})


class _Hosted:
    def __init__(self, ride, n_in, n_out, n_scratch=0, aliasing=False):
        self.ride = ride
        assert aliasing or not (ride and ride.aliases), "this host does not alias"
        self.aliases = {n_in + i: n_out + o for i, o in ride.aliases.items()} if ride else {}
        self.sizes = (n_in, len(ride.arrays) if ride else 0, n_out, len(ride.out_shapes) if ride else 0, n_scratch)
        self.arrays = ride.arrays if ride else []
        self.in_specs = [ANY] * len(self.arrays)
        self.out_shapes = ride.out_shapes if ride else []
        self.out_specs = [ANY] * len(self.out_shapes)
        self.scratch_shapes = ride.scratch_shapes if ride else []

    def split(self, refs):
        n_in, r_in, n_out, r_out, n_scr = self.sizes
        cuts = [0, n_in, n_in + r_in, n_in + r_in + n_out, n_in + r_in + n_out + r_out, n_in + r_in + n_out + r_out + n_scr]
        host_in, ride_in, host_out, ride_out, host_scr = (refs[cuts[k]:cuts[k + 1]] for k in range(5))
        ride_scr = refs[cuts[5]:]

        def start(when):
            if self.ride is not None:
                pl.when(when)(lambda: self.ride.start(ride_in, ride_out, ride_scr))

        def finish(when):
            if self.ride is not None:
                pl.when(when)(lambda: self.ride.finish(ride_in, ride_out, ride_scr))

        def middle(when):
            if self.ride is not None and self.ride.middle is not None:
                pl.when(when)(lambda: self.ride.middle(ride_in, ride_out, ride_scr))

        start.middle = middle
        return tuple(host_in) + tuple(host_out) + tuple(host_scr), start, finish

    def results(self, outs, n_out):
        outs = list(outs) if isinstance(outs, (list, tuple)) else [outs]
        return outs[:n_out], outs[n_out:]


def _join_rides(a, b):
    na, nb_ = len(a.arrays), len(b.arrays)
    oa = len(a.out_shapes)
    sa = len(a.scratch_shapes)

    def both(fa, fb):
        def run(ins, outs, sems):
            if fa is not None:
                fa(ins[:na], outs[:oa], sems[:sa])
            if fb is not None:
                fb(ins[na:na + nb_], outs[oa:], sems[sa:])
        return run

    middle = both(a.middle, b.middle) if (a.middle or b.middle) else None
    return _Ride(a.arrays + b.arrays, a.out_shapes + b.out_shapes, a.scratch_shapes + b.scratch_shapes,
                 both(a.start, b.start), both(a.finish, b.finish), middle)


def _load_resident(w_hbm, w_vmem, sems, first):
    def piece(c):
        rows = pl.ds(c * SEG_CHUNK, SEG_CHUNK)
        return pltpu.make_async_copy(w_hbm.at[rows], w_vmem.at[rows], sems.at[c])

    @pl.when(first)
    def _():
        for c in range(w_vmem.shape[0] // SEG_CHUNK):
            piece(c).start()

    def ready(c):
        @pl.when(first)
        def _():
            piece(c).wait()

    return ready


CHIP_ROWS = 2 * SHARD_IN
PROJ_WIDTHS = (2 * D_RNN, D_MODEL + 2 * D_KV, 3 * D_MODEL)
PROJ_DTYPES = (F32, BF16, BF16)


def _chip_pieces():
    starts = [0, PROJ_WIDTHS[0], PROJ_WIDTHS[0] + PROJ_WIDTHS[1], D_IN]
    pieces = []
    for k in range(N_CHIP):
        lo, hi = k * CHIP_ROWS, (k + 1) * CHIP_ROWS
        cur = []
        for a in range(len(PROJ_WIDTHS)):
            s0, s1 = max(lo, starts[a]), min(hi, starts[a + 1])
            if s0 < s1:
                cur.append((a, s0 - starts[a], s1 - s0, s0 - lo))
        pieces.append(cur)
    return pieces


def _gather_project(x, g, wt_shard, name, tm=1024):
    t, k = x.shape
    tm = min(tm, t)
    nt = t // tm
    pieces = _chip_pieces()

    def body(x_ref, g_ref, shard_ref, h_out, rx_ref, qkv_ref, ag_ref, wt_all,
             w_c, stage, o32, o16, h_all, send_sems, recv_sems, local_sem, stage_sems, out_sems, h_sems):
        s, ti = pl.program_id(0), pl.program_id(1)
        px, py, pc = _place()
        me, sibling = (px, py, pc), (px, py, 1 - pc)
        chips = [(px, py), (1 - px, py), (px, 1 - py), (1 - px, 1 - py)]
        outs = (rx_ref, qkv_ref, ag_ref)

        def slot(dev):
            return wt_all.at[4 * dev[0] + 2 * dev[1] + dev[2]]

        def copy(kk, block, to, src=None):
            return pltpu.make_async_remote_copy(
                src_ref=slot(block) if src is None else src, dst_ref=slot(block),
                send_sem=send_sems.at[kk], recv_sem=recv_sems.at[kk], device_id=to, device_id_type=MESH)

        mine = pltpu.make_async_copy(shard_ref, slot(me), local_sem)
        first = [copy(0, me, sibling, src=shard_ref)] + [copy(1 + j, me, (*chips[1 + j], pc), src=shard_ref) for j in range(3)]
        passed = [copy(4 + j, (*chips[1 + j], pc), sibling) for j in range(3)]

        @pl.when((s == 0) & (ti == 0))
        def _():
            mine.start()
            for cp in first[:3]:
                cp.start()

        def pass_on(step):
            copy(step, (*chips[step], pc), me).wait_recv()
            passed[step - 1].start()

        @pl.when((s == 2) & (ti == nt - 1))
        def _():
            pass_on(3)

        for step in range(N_CHIP):
            @pl.when((s == step) & (ti == 0))
            def _(step=step):
                chip = chips[step]
                if step == 0:
                    mine.wait()
                    copy(0, sibling, me).wait_recv()
                else:
                    if step == 1:
                        pass_on(1)
                        first[3].start()
                        pass_on(2)
                    copy(3 + step, (*chip, 1 - pc), me).wait_recv()
                loads = [pltpu.make_async_copy(slot((*chip, core)), stage.at[pl.ds(core * SHARD_IN, SHARD_IN)], stage_sems.at[core])
                         for core in (0, 1)]
                for cp in loads:
                    cp.start()
                for cp in loads:
                    cp.wait()
                w_c[...] = stage[...].T

        rows = pl.ds(pl.multiple_of(ti * tm, tm), tm)

        @pl.when(s == 0)
        def _():
            xv = x_ref[...]
            h_all[rows, :] = (xv * lax.rsqrt(jnp.mean(xv * xv, axis=-1, keepdims=True) + EPS) * g_ref[...]).astype(BF16)

        res = jnp.dot(h_all[rows, :], w_c[...], preferred_element_type=F32)

        n = s * nt + ti
        buf = lax.rem(n, 2)
        chip_idx = [2 * cx + cy for cx, cy in chips]

        def chip_at(step):
            return jnp.where(step == 0, chip_idx[0], jnp.where(step == 1, chip_idx[1], jnp.where(step == 2, chip_idx[2], chip_idx[3])))

        def h_write(b, tile):
            return pltpu.make_async_copy(h_all.at[pl.ds(tile * tm, tm)], h_out.at[pl.ds(tile * tm, tm)], h_sems.at[b])

        def writes(kchip, b, tile):
            cps = []
            for idx, (a, col, w, src) in enumerate(pieces[kchip]):
                staged = (o16 if PROJ_DTYPES[a] == BF16 else o32).at[b, :, pl.ds(src, w)]
                cps.append(pltpu.make_async_copy(staged, outs[a].at[pl.ds(tile * tm, tm), pl.ds(col, w)], out_sems.at[b, idx]))
            return cps

        o32[buf] = res
        o16[buf] = res.astype(BF16)
        kcur = chip_at(s)

        @pl.when(n > 0)
        def _():
            kprev = chip_at(lax.div(n - 1, nt))
            for kchip in range(N_CHIP):
                @pl.when(kprev == kchip)
                def _(kchip=kchip):
                    for cp in writes(kchip, 1 - buf, lax.rem(n - 1, nt)):
                        cp.wait()

            @pl.when(n <= nt)
            def _():
                h_write(1 - buf, n - 1).wait()

        @pl.when(s == 0)
        def _():
            h_write(buf, ti).start()

        for kchip in range(N_CHIP):
            @pl.when(kcur == kchip)
            def _(kchip=kchip):
                for cp in writes(kchip, buf, ti):
                    cp.start()

        @pl.when(n == N_CHIP * nt - 1)
        def _():
            for kchip in range(N_CHIP):
                @pl.when(kcur == kchip)
                def _(kchip=kchip):
                    for cp in writes(kchip, buf, ti):
                        cp.wait()
            for cp in first + passed:
                cp.wait_send()

    tile = pl.BlockSpec((tm, k), lambda s, ti: (jnp.where(s == 0, ti, nt - 1), 0))
    return pl.pallas_call(
        body, name=name, grid=(N_CHIP, nt),
        in_specs=[tile, pl.BlockSpec((1, k), lambda s, ti: (0, 0)), ANY],
        out_specs=[ANY, ANY, ANY, ANY, ANY],
        out_shape=[jax.ShapeDtypeStruct((t, k), BF16)]
        + [jax.ShapeDtypeStruct((t, w), dt) for w, dt in zip(PROJ_WIDTHS, PROJ_DTYPES)]
        + [jax.ShapeDtypeStruct((N_DEV, SHARD_IN, k), BF16)],
        scratch_shapes=[pltpu.VMEM((k, CHIP_ROWS), BF16), pltpu.VMEM((CHIP_ROWS, k), BF16),
                        pltpu.VMEM((2, tm, CHIP_ROWS), F32), pltpu.VMEM((2, tm, CHIP_ROWS), BF16), pltpu.VMEM((t, k), BF16),
                        pltpu.SemaphoreType.DMA((7,)), pltpu.SemaphoreType.DMA((7,)), pltpu.SemaphoreType.DMA,
                        pltpu.SemaphoreType.DMA((2,)), pltpu.SemaphoreType.DMA((2, 2)), pltpu.SemaphoreType.DMA((2,))],
        compiler_params=_params("arbitrary", "arbitrary"))(x, g, wt_shard)


def _mm_tn_pairs(prods, name):
    n_prod = len(prods)
    ktok, m = prods[0][0].shape
    n = prods[0][1].shape[1]
    half, blk = m // 2, m // N_DEV
    ns = 2 * n_prod
    kept = N_CHIP * blk

    def side_pieces(s, side):
        i, hf = divmod(s, 2)
        first = hf * N_DEV // 2
        return [((d - first) * blk, i * kept + (d // 2) * blk) for d in range(first, first + N_DEV // 2) if d % 2 == side]

    def body(*refs):
        a_hbm, b_hbm, pair_refs = refs[:n_prod], refs[n_prod:2 * n_prod], refs[2 * n_prod:3 * n_prod]
        a_buf, b_buf, res_buf, recv_all, pair_all, a_sems, b_sems, send_sems, recv_sems, out_sems = refs[3 * n_prod:]
        step = pl.program_id(0)
        px, py, pc = _place()

        def fetch_a(s):
            return pltpu.make_async_copy(a_hbm[s // 2].at[:, pl.ds((s % 2) * half, half)], a_buf.at[s % 2], a_sems.at[s % 2])

        def fetch_b(i):
            return pltpu.make_async_copy(b_hbm[i], b_buf.at[i % 2], b_sems.at[i % 2])

        @pl.when(step == 0)
        def _():
            fetch_a(0).start()
            fetch_b(0).start()

        for s in range(ns):
            @pl.when(step == s)
            def _(s=s):
                if s + 1 < ns:
                    fetch_a(s + 1).start()
                if s % 2 == 0 and s // 2 + 1 < n_prod:
                    fetch_b(s // 2 + 1).start()
                fetch_a(s).wait()
                if s % 2 == 0:
                    fetch_b(s // 2).wait()

        res_buf[step % 2] = lax.dot_general(a_buf[step % 2], b_buf[(step // 2) % 2], TN_DIMS, preferred_element_type=F32)

        def crossing(s, j, piece):
            off, at = piece
            return pltpu.make_async_remote_copy(
                src_ref=res_buf.at[s % 2, pl.ds(off, blk)], dst_ref=recv_all.at[pl.ds(at, blk)],
                send_sem=send_sems.at[s, j], recv_sem=recv_sems.at[s, j], device_id=(px, py, 1 - pc), device_id_type=MESH)

        def write(s, core):
            at = side_pieces(s, core)[0][1]
            return pltpu.make_async_copy(pair_all.at[pl.ds(at, 2 * blk)], pair_refs[s // 2].at[pl.ds(at % kept, 2 * blk)], out_sems.at[s])

        for core in (0, 1):
            def settle(s, core=core):
                for j, piece in enumerate(side_pieces(s, 1 - core)):
                    crossing(s, j, piece).wait_send()
                for j, (off, at) in enumerate(side_pieces(s, core)):
                    crossing(s, j, (off, at)).wait_recv()
                    pair_all[at:at + blk] = (res_buf[s % 2, off:off + blk] + recv_all[at:at + blk]).astype(BF16)
                write(s, core).start()

            for s in range(ns):
                @pl.when((pc == core) & (step == s))
                def _(s=s, settle=settle, core=core):
                    for j, piece in enumerate(side_pieces(s, 1 - core)):
                        crossing(s, j, piece).start()
                    if s > 0:
                        settle(s - 1)
                    if s == ns - 1:
                        settle(s)
                        for t in range(ns):
                            write(t, core).wait()

    pairs = pl.pallas_call(
        body, name=name, grid=(ns,),
        in_specs=[ANY] * (2 * n_prod), out_specs=[ANY] * n_prod,
        out_shape=[jax.ShapeDtypeStruct((kept, n), BF16)] * n_prod,
        scratch_shapes=[pltpu.VMEM((2, ktok, half), prods[0][0].dtype), pltpu.VMEM((2, ktok, n), prods[0][1].dtype),
                        pltpu.VMEM((2, half, n), F32), pltpu.VMEM((n_prod * kept, n), F32), pltpu.VMEM((n_prod * kept, n), BF16),
                        pltpu.SemaphoreType.DMA((2,)), pltpu.SemaphoreType.DMA((2,)),
                        pltpu.SemaphoreType.DMA((ns, 2)), pltpu.SemaphoreType.DMA((ns, 2)), pltpu.SemaphoreType.DMA((ns,))],
        compiler_params=_params("arbitrary"))(*[a for a, _ in prods], *[b for _, b in prods])
    return [pair.reshape(N_CHIP, blk, n) for pair in pairs]


def _segment_chunks(segs):
    bounds = [0]
    for s in segs:
        bounds.append(bounds[-1] + s.shape[1] // SEG_CHUNK)
    return bounds


def _input_grad(segs, wt, x, g, dy, name, tm=512, ride=None):
    m = segs[0].shape[0]
    rows, n = wt.shape
    tm = min(tm, m)
    bounds = _segment_chunks(segs)
    n_seg = len(segs)
    ni = m // tm
    host = _Hosted(ride, n_seg + 4, 2, 2, aliasing=True)

    def body(*refs):
        host_refs, start, finish = host.split(refs)
        a_refs = host_refs[:n_seg]
        wt_hbm, x_ref, g_ref, dy_ref, gx_ref, st_ref, wt_vmem, sems = host_refs[n_seg:]
        i = pl.program_id(0)
        start(i == 0)

        @pl.when(i == 0)
        def _():
            st_ref[...] = jnp.zeros_like(st_ref)

        ready = _load_resident(wt_hbm, wt_vmem, sems, i == 0)
        dh = None
        for s in range(n_seg):
            for c in range(bounds[s], bounds[s + 1]):
                ready(c)
            part = jnp.dot(a_refs[s][...], wt_vmem[bounds[s] * SEG_CHUNK:bounds[s + 1] * SEG_CHUNK, :], preferred_element_type=F32)
            dh = part if dh is None else dh + part
        xv = x_ref[...]
        r = lax.rsqrt(jnp.mean(xv * xv, axis=-1, keepdims=True) + EPS)
        xn = xv * r
        dxn = dh * g_ref[...]
        gx_ref[...] = dy_ref[...] + r * (dxn - xn * jnp.mean(dxn * xn, axis=-1, keepdims=True))
        st_ref[...] += _rows8([jnp.sum(dh * xn, axis=0, keepdims=True)], n)
        finish(i == ni - 1)

    tile = pl.BlockSpec((tm, n), lambda i: (i, 0))
    outs = pl.pallas_call(
        body, name=name, grid=(ni,),
        in_specs=[pl.BlockSpec((tm, sg.shape[1]), lambda i: (i, 0)) for sg in segs]
        + [ANY, tile, pl.BlockSpec((1, n), lambda i: (0, 0)), tile] + host.in_specs,
        out_specs=[tile, pl.BlockSpec((8, n), lambda i: (0, 0))] + host.out_specs,
        out_shape=[jax.ShapeDtypeStruct((m, n), F32), jax.ShapeDtypeStruct((8, n), F32)] + host.out_shapes,
        scratch_shapes=[pltpu.VMEM((rows, n), wt.dtype), pltpu.SemaphoreType.DMA((rows // SEG_CHUNK,))] + host.scratch_shapes,
        input_output_aliases=host.aliases,
        compiler_params=_params("arbitrary"))(*segs, wt, x, g, dy, *host.arrays)
    res, landed = host.results(outs, 2)
    return (*res, landed) if ride else tuple(res)


CHUNK_ORDER = (0, 4, 7, 10, 1, 5, 8, 11, 2, 6, 9, 12, 3)
EARLY_STEPS = 8


def _side_pieces():
    table = []
    for core in (0, 1):
        sides = ([None] * len(CHUNK_ORDER), [None] * len(CHUNK_ORDER))
        for s, cc in enumerate(CHUNK_ORDER):
            g0 = cc * SEG_CHUNK
            for d in range(N_DEV):
                lo, hi = max(g0, d * SHARD_IN), min(g0 + SEG_CHUNK, (d + 1) * SHARD_IN)
                if lo < hi:
                    side = sides[0 if d % 2 == core else 1]
                    assert side[s] is None
                    side[s] = (lo - g0, hi - lo, (d // 2) * SHARD_IN + lo - d * SHARD_IN)
        table.append(sides)
    return table


def _to_chip(src, red_ref, piece, s, send_sems, recv_sems, own, core):
    _, rows, at = piece
    k, r0 = divmod(at, SHARD_IN)
    return k, pltpu.make_async_remote_copy(
        src_ref=src, dst_ref=red_ref.at[own, pl.ds(r0, rows)], send_sem=send_sems.at[s], recv_sem=recv_sems.at[own, s],
        device_id=(k // 2, k % 2, core), device_id_type=MESH)


def _from_chip(src, red_ref, piece, s, send_sems, recv_sems, chip, me):
    _, rows, at = piece
    return pltpu.make_async_remote_copy(
        src_ref=src, dst_ref=red_ref.at[chip, pl.ds(at % SHARD_IN, rows)], send_sem=send_sems.at[s], recv_sem=recv_sems.at[chip, s],
        device_id=me, device_id_type=MESH)


def _mm_tn_seg_pair(segs, b, name):
    ktok = segs[0].shape[0]
    n = b.shape[1]
    bounds = _segment_chunks(segs)
    n_seg = len(segs)
    nc = bounds[-1]
    assert nc == len(CHUNK_ORDER)
    seg_of = [s for s in range(n_seg) for _ in range(bounds[s], bounds[s + 1])]
    table = _side_pieces()

    def body(*refs):
        a_hbm, b_hbm, pair_ref, red_ref = refs[:n_seg], refs[n_seg], refs[n_seg + 1], refs[n_seg + 2]
        (a_buf, b_vmem, res_buf, recv_all, pair_all, a_sems, b_sem, send_sems, recv_sems, out_sems,
         chip_send, chip_recv) = refs[n_seg + 3:]
        step = pl.program_id(0)
        px, py, pc = _place()
        own = 2 * px + py

        def fetch(s):
            sg = seg_of[CHUNK_ORDER[s]]
            cols = pl.ds((CHUNK_ORDER[s] - bounds[sg]) * SEG_CHUNK, SEG_CHUNK)
            return pltpu.make_async_copy(a_hbm[sg].at[:, cols], a_buf.at[s % 2], a_sems.at[s % 2])

        @pl.when(step == 0)
        def _():
            whole = pltpu.make_async_copy(b_hbm, b_vmem, b_sem)
            whole.start()
            fetch(0).start()
            whole.wait()

        for s in range(nc):
            @pl.when(step == s)
            def _(s=s):
                if s + 1 < nc:
                    fetch(s + 1).start()
                fetch(s).wait()

        res_buf[step % 2] = lax.dot_general(a_buf[step % 2], b_vmem[...], TN_DIMS, preferred_element_type=F32)

        def crossing(s, piece):
            off, rows, at = piece
            return pltpu.make_async_remote_copy(
                src_ref=res_buf.at[s % 2, pl.ds(off, rows)], dst_ref=recv_all.at[pl.ds(at, rows)],
                send_sem=send_sems.at[s], recv_sem=recv_sems.at[s], device_id=(px, py, 1 - pc), device_id_type=MESH)

        def write(s, piece):
            _, rows, at = piece
            return pltpu.make_async_copy(pair_all.at[pl.ds(at, rows)], pair_ref.at[pl.ds(at, rows)], out_sems.at[s])

        def to_chip(s, piece, core):
            return _to_chip(pair_all.at[pl.ds(piece[2], piece[1])], red_ref, piece, s, chip_send, chip_recv, own, core)

        for core in (0, 1):
            mine, theirs = table[core]

            def settle(s, mine=mine, theirs=theirs, core=core):
                if theirs[s] is not None:
                    crossing(s, theirs[s]).wait_send()
                if mine[s] is not None:
                    off, rows, at = mine[s]
                    crossing(s, mine[s]).wait_recv()
                    pair_all[at:at + rows] = (res_buf[s % 2, off:off + rows] + recv_all[at:at + rows]).astype(BF16)
                    write(s, mine[s]).start()
                    if s < EARLY_STEPS:
                        k, cp = to_chip(s, mine[s], core)
                        pl.when(own != k)(cp.start)

            for s in range(nc):
                @pl.when((pc == core) & (step == s))
                def _(s=s, settle=settle, mine=mine, theirs=theirs, core=core):
                    if theirs[s] is not None:
                        crossing(s, theirs[s]).start()
                    if s > 0:
                        settle(s - 1)
                    if s == nc - 1:
                        settle(s)
                        kept = [t for t in range(nc) if mine[t] is not None]
                        for t in kept:
                            write(t, mine[t]).wait()
                        early = [t for t in kept if t < EARLY_STEPS]
                        for t in early:
                            k, cp = to_chip(t, mine[t], core)
                            pl.when(own != k)(cp.wait_send)
                        for k in range(N_CHIP):
                            @pl.when(own == k)
                            def _(k=k):
                                for t in early:
                                    if mine[t][2] // SHARD_IN == k:
                                        for chip in range(N_CHIP):
                                            if chip != k:
                                                _from_chip(pair_all.at[pl.ds(mine[t][2], mine[t][1])], red_ref, mine[t], t,
                                                           chip_send, chip_recv, chip, (px, py, pc)).wait_recv()

    flat = jax.ShapeDtypeStruct((N_CHIP * SHARD_IN, n), BF16)
    pair, red = pl.pallas_call(
        body, name=name, grid=(nc,),
        in_specs=[ANY] * (n_seg + 1), out_specs=[ANY, ANY],
        out_shape=[flat, jax.ShapeDtypeStruct((N_CHIP, SHARD_IN, n), BF16)],
        scratch_shapes=[pltpu.VMEM((2, ktok, SEG_CHUNK), segs[0].dtype), pltpu.VMEM((ktok, n), b.dtype),
                        pltpu.VMEM((2, SEG_CHUNK, n), F32), pltpu.VMEM(flat.shape, F32), pltpu.VMEM(flat.shape, BF16),
                        pltpu.SemaphoreType.DMA((2,)), pltpu.SemaphoreType.DMA,
                        pltpu.SemaphoreType.DMA((nc,)), pltpu.SemaphoreType.DMA((nc,)), pltpu.SemaphoreType.DMA((nc,)),
                        pltpu.SemaphoreType.DMA((nc,)), pltpu.SemaphoreType.DMA((N_CHIP, nc))],
        compiler_params=_params("arbitrary"))(*segs, b)
    return pair.reshape(N_CHIP, SHARD_IN, n), red


RING = 3


def _ring_specs(streams, tm):
    return ([pltpu.VMEM((RING, tm, width), a.dtype) for a, _, width in streams]
            + [pltpu.SemaphoreType.DMA((len(streams), RING))])


def _ring_step(streams, hbm_refs, rings, sems, n, n_steps, tm):
    def fetch(step):
        rows = pl.ds(pl.multiple_of(step * tm, tm), tm)
        return [pltpu.make_async_copy(ref.at[rows, pl.ds(col, width)], ring.at[lax.rem(step, RING)], sems.at[k, lax.rem(step, RING)])
                for k, ((_, col, width), ref, ring) in enumerate(zip(streams, hbm_refs, rings))]

    @pl.when(n == 0)
    def _():
        for cp in fetch(0) + fetch(1):
            cp.start()

    @pl.when(n + 2 < n_steps)
    def _():
        for cp in fetch(n + 2):
            cp.start()

    for cp in fetch(n):
        cp.wait()
    return lax.rem(n, RING)


def _branches_fwd(z_rnn, z_attn, ag_ml, b_gate, w_rnn, w_attn, w_out, x, target, g_post, name, tm=512):
    t, d = x.shape
    tm = min(tm, t)
    n_steps = t // tm
    streams = [(z_rnn, 0, d), (z_attn, 0, d), (ag_ml, d, d), (ag_ml, 2 * d, d), (x, 0, d), (target, 0, d)]
    ns = len(streams)

    def body(*refs):
        hbm = refs[:ns]
        br_ref, ba_ref, wr_ref, wa_ref, wo_ref, g_ref, brr_ref, bra_ref, mg_ref, do_ref, dy_ref, st_ref = refs[ns:ns + 12]
        rings, sems = refs[ns + 12:2 * ns + 12], refs[2 * ns + 12]
        slot = _ring_step(streams, hbm, rings, sems, pl.program_id(0), n_steps, tm)
        zr_ref, za_ref, lr_ref, la_ref, x_ref, t_ref = (ring.at[slot] for ring in rings)

        @pl.when(pl.program_id(0) == 0)
        def _():
            st_ref[...] = jnp.zeros_like(st_ref)

        br_rnn = jnp.dot(zr_ref[...], wr_ref[...], preferred_element_type=F32)
        br_attn = jnp.dot(za_ref[...], wa_ref[...], preferred_element_type=F32)
        brr_ref[...] = br_rnn.astype(BF16)
        bra_ref[...] = br_attn.astype(BF16)
        g_rnn = _sigmoid(lr_ref[...].astype(F32) + br_ref[...])
        g_attn = _sigmoid(la_ref[...].astype(F32) + ba_ref[...])
        merged = (g_rnn * br_rnn + g_attn * br_attn).astype(BF16)
        mg_ref[...] = merged
        o = jnp.dot(merged, wo_ref[...], preferred_element_type=F32)
        g = g_ref[...]
        r = lax.rsqrt(jnp.mean(o * o, axis=-1, keepdims=True) + EPS)
        nrm = o * r
        err = x_ref[...] + nrm * g - t_ref[...]
        dy = err * (1.0 / d)
        dy_ref[...] = dy
        dn = dy * g
        do_ref[...] = (r * (dn - nrm * jnp.mean(dn * nrm, axis=-1, keepdims=True))).astype(BF16)
        st_ref[...] += _rows8([jnp.sum(dy * nrm, axis=0, keepdims=True), jnp.sum(err * err, axis=0, keepdims=True)], d)

    tile = pl.BlockSpec((tm, d), lambda i: (i, 0))
    weight = pl.BlockSpec((d, d), lambda i: (0, 0))
    bf = jax.ShapeDtypeStruct((t, d), BF16)
    return pl.pallas_call(
        body, name=name, grid=(n_steps,),
        in_specs=[ANY] * ns + [pl.BlockSpec((1, d), lambda i: (0, 0)), pl.BlockSpec((1, d), lambda i: (0, 1)),
                               weight, weight, weight, pl.BlockSpec((1, d), lambda i: (0, 0))],
        out_specs=[tile, tile, tile, tile, tile, pl.BlockSpec((8, d), lambda i: (0, 0))],
        out_shape=[bf, bf, bf, bf, jax.ShapeDtypeStruct((t, d), F32), jax.ShapeDtypeStruct((8, d), F32)],
        scratch_shapes=_ring_specs(streams, tm),
        compiler_params=_params("arbitrary"))(*[a for a, _, _ in streams], b_gate, b_gate, w_rnn, w_attn, w_out, g_post)


def _branches_bwd(dout, br_rnn, br_attn, ag_ml, b_gate, w_rnn, w_attn, w_out, name, tm=512):
    t, d = br_rnn.shape
    tm = min(tm, t)
    n_steps = t // tm
    streams = [(dout, 0, d), (br_rnn, 0, d), (br_attn, 0, d), (ag_ml, d, d), (ag_ml, 2 * d, d)]
    ns = len(streams)

    def body(*refs):
        hbm = refs[:ns]
        br_ref, ba_ref, wr_ref, wa_ref, wo_ref, dr_ref, da_ref, dl_ref, dzr_ref, dza_ref, st_ref, wt_ref = refs[ns:ns + 12]
        rings, sems = refs[ns + 12:2 * ns + 12], refs[2 * ns + 12]
        slot = _ring_step(streams, hbm, rings, sems, pl.program_id(0), n_steps, tm)
        do_ref, r_ref, a_ref, lr_ref, la_ref = (ring.at[slot] for ring in rings)

        @pl.when(pl.program_id(0) == 0)
        def _():
            st_ref[...] = jnp.zeros_like(st_ref)
            wt_ref[0] = wo_ref[...].T
            wt_ref[1] = wr_ref[...].T
            wt_ref[2] = wa_ref[...].T

        dm = jnp.dot(do_ref[...], wt_ref[0], preferred_element_type=F32)
        g_rnn = _sigmoid(lr_ref[...].astype(F32) + br_ref[...])
        g_attn = _sigmoid(la_ref[...].astype(F32) + ba_ref[...])
        dbr_rnn = (dm * g_rnn).astype(BF16)
        dbr_attn = (dm * g_attn).astype(BF16)
        dr_ref[...] = dbr_rnn
        da_ref[...] = dbr_attn
        dl_rnn = dm * r_ref[...].astype(F32) * g_rnn * (1.0 - g_rnn)
        dl_attn = dm * a_ref[...].astype(F32) * g_attn * (1.0 - g_attn)
        dl_ref[:, 0:d] = dl_rnn.astype(BF16)
        dl_ref[:, d:2 * d] = dl_attn.astype(BF16)
        st_ref[...] += _rows8([jnp.sum(dl_rnn, axis=0, keepdims=True), jnp.sum(dl_attn, axis=0, keepdims=True)], d)
        dzr_ref[...] = jnp.dot(dbr_rnn, wt_ref[1], preferred_element_type=F32).astype(BF16)
        dza_ref[...] = jnp.dot(dbr_attn, wt_ref[2], preferred_element_type=F32).astype(BF16)

    tile = pl.BlockSpec((tm, d), lambda i: (i, 0))
    weight = pl.BlockSpec((d, d), lambda i: (0, 0))
    bf = jax.ShapeDtypeStruct((t, d), BF16)
    return pl.pallas_call(
        body, name=name, grid=(n_steps,),
        in_specs=[ANY] * ns + [pl.BlockSpec((1, d), lambda i: (0, 0)), pl.BlockSpec((1, d), lambda i: (0, 1)), weight, weight, weight],
        out_specs=[tile, tile, pl.BlockSpec((tm, 2 * d), lambda i: (i, 0)), tile, tile, pl.BlockSpec((8, d), lambda i: (0, 0))],
        out_shape=[bf, bf, jax.ShapeDtypeStruct((t, 2 * d), BF16), bf, bf, jax.ShapeDtypeStruct((8, d), F32)],
        scratch_shapes=[pltpu.VMEM((3, d, d), BF16)] + _ring_specs(streams, tm),
        compiler_params=_params("arbitrary"))(*[a for a, _, _ in streams], b_gate, b_gate, w_rnn, w_attn, w_out)


def _lru_decay(r, sp):
    log_a = (-LRU_C) * r * sp
    return log_a, jnp.exp(log_a)


def _lru_gates(c, wa, ba, wx, bx, sp):
    cb = c.astype(BF16)
    r = _sigmoid(jnp.dot(cb, wa, preferred_element_type=F32) + ba)
    ig = _sigmoid(jnp.dot(cb, wx, preferred_element_type=F32) + bx)
    log_a, a = _lru_decay(r, sp)
    mult = jnp.sqrt(-jnp.tanh(log_a) * (a * a + 1.0))
    return cb, r, ig, a, mult


SUBLANES = 8


def _scan_fwd(a, u, carry, tt):
    w = a.shape[1]
    ng = tt // SUBLANES
    a3 = a.reshape(ng, SUBLANES, w)
    u3 = u.reshape(ng, SUBLANES, w)
    sub = lax.broadcasted_iota(jnp.int32, (ng, SUBLANES, w), 1)
    d = 1
    while d < SUBLANES:
        keep = sub >= d
        u3 = u3 + a3 * jnp.where(keep, pltpu.roll(u3, d, 1), 0.0)
        a3 = a3 * jnp.where(keep, pltpu.roll(a3, d, 1), 1.0)
        d *= 2
    out = []
    for g in range(ng):
        hg = u3[g] + a3[g] * carry
        out.append(hg)
        carry = hg[SUBLANES - 1:SUBLANES, :]
    return jnp.concatenate(out, axis=0)


def _scan_rev(b, g, carry, tt):
    w = b.shape[1]
    ng = tt // SUBLANES
    b3 = b.reshape(ng, SUBLANES, w)
    g3 = g.reshape(ng, SUBLANES, w)
    sub = lax.broadcasted_iota(jnp.int32, (ng, SUBLANES, w), 1)
    d = 1
    while d < SUBLANES:
        keep = sub < SUBLANES - d
        g3 = g3 + b3 * jnp.where(keep, pltpu.roll(g3, SUBLANES - d, 1), 0.0)
        b3 = b3 * jnp.where(keep, pltpu.roll(b3, SUBLANES - d, 1), 1.0)
        d *= 2
    out = [None] * ng
    for k in range(ng - 1, -1, -1):
        hk = g3[k] + b3[k] * carry
        out[k] = hk
        carry = hk[0:1, :]
    return jnp.concatenate(out, axis=0)


def _conv_taps(cw, bias, x, ext_ref, tt):
    x2 = ext_ref[7:7 + tt, :]
    x1 = ext_ref[6:6 + tt, :]
    x0 = ext_ref[5:5 + tt, :]
    return bias + cw[3:4] * x + cw[2:3] * x2 + cw[1:2] * x1 + cw[0:1] * x0


def _rnn_fwd(rx_rg, cw, cb, wa, ba, wx, bx, lam, name, tt=512):
    t = rx_rg.shape[0]
    tt = min(tt, t)
    w = GROUP_W

    nt = t // tt
    n_steps = N_GROUPS * nt

    def body(in_hbm, cw_ref, cb_ref, wa_ref, ba_ref, wx_ref, bx_ref, lam_ref,
             y_ref, z_ref, c_ref, r_ref, ig_ref, mult_ref, ext_ref, hc_ref, x_ring, g_ring, ring_sems):
        @pl.when(pl.program_id(1) == 0)
        def _():
            ext_ref[0:8, :] = jnp.zeros((8, w), F32)
            hc_ref[...] = jnp.zeros((8, w), F32)

        n = pl.program_id(0) * nt + pl.program_id(1)

        def fetch(step):
            slot = lax.rem(step, 3)
            rows = pl.ds(pl.multiple_of(lax.rem(step, nt) * tt, tt), tt)
            group = lax.div(step, nt)
            return [pltpu.make_async_copy(in_hbm.at[rows, pl.ds(pl.multiple_of((half * N_GROUPS + group) * w, w), w)],
                                          ring.at[slot], ring_sems.at[half, slot])
                    for half, ring in enumerate((x_ring, g_ring))]

        @pl.when(n == 0)
        def _():
            for cp in fetch(0) + fetch(1):
                cp.start()

        @pl.when(n + 2 < n_steps)
        def _():
            for cp in fetch(n + 2):
                cp.start()

        for cp in fetch(n):
            cp.wait()
        slot = lax.rem(n, 3)
        x = x_ring[slot]
        ext_ref[8:8 + tt, :] = x
        c = _conv_taps(cw_ref[...], cb_ref[...], x, ext_ref, tt)
        ext_ref[0:8, :] = x[tt - 8:tt, :]
        sp = _softplus(-lam_ref[...])
        _, r, ig, a, mult = _lru_gates(c, wa_ref[...], ba_ref[...], wx_ref[...], bx_ref[...], sp)
        c_ref[...] = c
        r_ref[...] = r
        ig_ref[...] = ig
        mult_ref[...] = mult
        h = _scan_fwd(a, mult * (ig * c), hc_ref[7:8, :], tt)
        hc_ref[...] = h[tt - 8:tt, :]
        y_ref[...] = h
        rg = g_ring[slot]
        z_ref[...] = (h * rg * _sigmoid(rg)).astype(BF16)

    vec = pl.BlockSpec((1, w), lambda g, i: (0, g))
    mat = pl.BlockSpec((None, w, w), lambda g, i: (g, 0, 0))
    tile = pl.BlockSpec((tt, w), lambda g, i: (i, g))
    return pl.pallas_call(
        body, name=name, grid=(N_GROUPS, nt),
        in_specs=[ANY, pl.BlockSpec((4, w), lambda g, i: (0, g)), vec, mat, vec, mat, vec, vec],
        out_specs=[tile, tile] + [pl.BlockSpec((None, tt, w), lambda g, i: (g, i, 0))] * 4,
        out_shape=[jax.ShapeDtypeStruct((t, D_RNN), F32), jax.ShapeDtypeStruct((t, D_RNN), BF16)]
        + [jax.ShapeDtypeStruct((N_GROUPS, t, w), F32)] * 4,
        scratch_shapes=[pltpu.VMEM((tt + 8, w), F32), pltpu.VMEM((8, w), F32),
                        pltpu.VMEM((3, tt, w), F32), pltpu.VMEM((3, tt, w), F32), pltpu.SemaphoreType.DMA((2, 3))],
        compiler_params=_params("arbitrary", "arbitrary"))(rx_rg, cw, cb, wa, ba, wx, bx, lam)


def _rnn_bwd(rx_rg, y, dz, c, r, ig, mult, cw, wa, wx, lam, name, tt=512):
    t = rx_rg.shape[0]
    tt = min(tt, t)
    nt = t // tt
    w = GROUP_W

    def body(rx_ref, rg_ref, y_ref, yt_ref, dz_ref, c_ref, r_ref, ig_ref, mult_ref, cw_ref, wa_ref, wx_ref, lam_ref,
             drx_ref, drg_ref, st_ref, gda_ref, gdx_ref, dcx_ref, wcar_ref, acar_ref, dwa_ref, dwx_ref):
        ii = pl.program_id(1)

        @pl.when(ii == 0)
        def _():
            wcar_ref[...] = jnp.zeros((8, w), F32)
            acar_ref[...] = jnp.zeros((8, w), F32)
            dcx_ref[tt:tt + 8, :] = jnp.zeros((8, w), F32)
            st_ref[...] = jnp.zeros_like(st_ref)
            dwa_ref[...] = jnp.zeros_like(dwa_ref)
            dwx_ref[...] = jnp.zeros_like(dwx_ref)

        has_prev = jnp.where(ii == nt - 1, 0.0, 1.0)
        cwv = cw_ref[...]
        lam = lam_ref[...]
        sp = _softplus(-lam)
        wa = wa_ref[...]
        wx = wx_ref[...]
        c = c_ref[...]
        r = r_ref[...]
        ig = ig_ref[...]
        mult = mult_ref[...]
        _, a = _lru_decay(r, sp)
        cb16 = c.astype(BF16)

        rg = rg_ref[...]
        sg = _sigmoid(rg)
        dz = dz_ref[...].astype(F32)
        yv = y_ref[...]
        drg_ref[...] = (dz * yv * (sg * (1.0 + rg * (1.0 - sg)))).astype(BF16)

        row = lax.broadcasted_iota(jnp.int32, (tt, w), 0)
        b = jnp.where(row < tt - 1, pltpu.roll(a, tt - 1, 0), acar_ref[0:1, :])
        dh = _scan_rev(b, dz * (rg * sg), wcar_ref[0:1, :], tt)
        wcar_ref[...] = dh[0:8, :]
        acar_ref[...] = a[0:8, :]

        hprev = jnp.where(row >= 1, pltpu.roll(yv, 1, 0), yt_ref[7:8, :] * has_prev)
        dmult = dh * (ig * c)
        dig = dh * mult * c
        dlog_a = dh * hprev * a - dmult * (a * a / mult)
        dpa = dlog_a * ((-LRU_C) * sp) * r * (1.0 - r)
        dpx = dig * ig * (1.0 - ig)
        dsp = jnp.sum(dlog_a * r, axis=0, keepdims=True) * (-LRU_C)
        dlam = dsp * (-_sigmoid(-lam))
        dpa16 = dpa.astype(BF16)
        dpx16 = dpx.astype(BF16)
        dwa_ref[...] += lax.dot_general(cb16, dpa16, TN_DIMS, preferred_element_type=F32)
        dwx_ref[...] += lax.dot_general(cb16, dpx16, TN_DIMS, preferred_element_type=F32)
        dc = (dh * mult * ig
              + lax.dot_general(dpa16, wa, NT_DIMS, preferred_element_type=F32)
              + lax.dot_general(dpx16, wx, NT_DIMS, preferred_element_type=F32))

        dcx_ref[0:tt, :] = dc
        dc1 = dcx_ref[1:1 + tt, :]
        dc2 = dcx_ref[2:2 + tt, :]
        dc3 = dcx_ref[3:3 + tt, :]
        drx_ref[...] = (cwv[3:4] * dc + cwv[2:3] * dc1 + cwv[1:2] * dc2 + cwv[0:1] * dc3).astype(BF16)
        dcx_ref[tt:tt + 8, :] = dc[0:8, :]

        def colsum(v):
            return jnp.sum(v, axis=0, keepdims=True)

        x = rx_ref[...]
        st_ref[...] += _rows8([colsum(dc), colsum(dpa), colsum(dpx), dlam,
                               colsum(dc3 * x), colsum(dc2 * x), colsum(dc1 * x), colsum(dc * x)], w)

        @pl.when(ii == nt - 1)
        def _():
            for blk in range(GROUP_W // RNN_BLOCK_W):
                rows = slice(blk * RNN_BLOCK_W, (blk + 1) * RNN_BLOCK_W)
                gda_ref[blk] = dwa_ref[rows, rows]
                gdx_ref[blk] = dwx_ref[rows, rows]

    def rev(ii):
        return nt - 1 - ii

    def tail(g, ii):
        return (jnp.maximum(rev(ii) * (tt // 8) - 1, 0), g)

    vec = pl.BlockSpec((1, w), lambda g, ii: (0, g))
    mat = pl.BlockSpec((None, w, w), lambda g, ii: (g, 0, 0))
    tile = pl.BlockSpec((tt, w), lambda g, ii: (rev(ii), g))
    diag_shape = (N_GROUPS, GROUP_W // RNN_BLOCK_W, RNN_BLOCK_W, RNN_BLOCK_W)
    diag = pl.BlockSpec((None,) + diag_shape[1:], lambda g, ii: (g, 0, 0, 0))
    kept = pl.BlockSpec((None, tt, w), lambda g, ii: (g, rev(ii), 0))
    return pl.pallas_call(
        body, name=name, grid=(N_GROUPS, nt),
        in_specs=[tile, pl.BlockSpec((tt, w), lambda g, ii: (rev(ii), N_GROUPS + g)),
                  tile, pl.BlockSpec((8, w), tail), tile, kept, kept, kept, kept,
                  pl.BlockSpec((4, w), lambda g, ii: (0, g)), mat, mat, vec],
        out_specs=[tile, tile, pl.BlockSpec((8, w), lambda g, ii: (0, g)), diag, diag],
        out_shape=[jax.ShapeDtypeStruct((t, D_RNN), BF16), jax.ShapeDtypeStruct((t, D_RNN), BF16),
                   jax.ShapeDtypeStruct((8, D_RNN), F32),
                   jax.ShapeDtypeStruct(diag_shape, F32), jax.ShapeDtypeStruct(diag_shape, F32)],
        scratch_shapes=[pltpu.VMEM((tt + 8, w), F32), pltpu.VMEM((8, w), F32),
                        pltpu.VMEM((8, w), F32), pltpu.VMEM((w, w), F32), pltpu.VMEM((w, w), F32)],
        compiler_params=_params("parallel", "arbitrary"))(rx_rg, rx_rg, y, y, dz, c, r, ig, mult, cw, wa, wx, lam)


def _half_mask(shape, half):
    lane = lax.broadcasted_iota(jnp.int32, shape, 1)
    return (lane >= HEAD_DIM) if half else (lane < HEAD_DIM)


def _dup_half(t, half):
    sel = jnp.where(_half_mask(t.shape, half), t, 0.0)
    return sel + pltpu.roll(sel, HEAD_DIM, 1)


def _band_geometry(n):
    qi = lax.broadcasted_iota(jnp.int32, (BLOCK, 2 * BLOCK), 0)
    kj = lax.broadcasted_iota(jnp.int32, (BLOCK, 2 * BLOCK), 1)
    dist = BLOCK + qi - kj
    first_key = jnp.where(n > 0, 0, BLOCK)
    valid = (dist >= 0) & (dist < BLOCK) & (kj >= first_key)
    return dist.astype(F32), valid


GROUP = 4
HALF_ROWS = 64


def _kv_dup(prev_ref, cur_ref, hk, scale=1.0):
    tile = hk // 2
    kt = jnp.concatenate([prev_ref[:, tile * LANE:(tile + 1) * LANE], cur_ref[:, tile * LANE:(tile + 1) * LANE]], axis=0)
    return (_dup_half(kt.astype(F32), hk % 2) * scale).astype(BF16)


def _fill_bias(bias_ref, sm_ref, n):
    distf, valid = _band_geometry(n)
    for head in range(N_Q_HEADS):
        bias_ref[head] = jnp.where(valid, -sm_ref[1, head] * distf, MASKED)


def _head_scores(q2s, half, kdup, bias):
    qm = jnp.where(_half_mask(q2s.shape, half), q2s, jnp.zeros_like(q2s))
    return qm, lax.dot_general(qm, kdup, NT_DIMS, preferred_element_type=F32) + bias


def _attn_specs(nb, clamp_last):
    def blk(n):
        return jnp.minimum(n, nb - 1) if clamp_last else n

    q_spec = pl.BlockSpec((BLOCK, D_MODEL), lambda n: (blk(n), 0))
    k_prev = pl.BlockSpec((BLOCK, D_KV), lambda n: (jnp.maximum(blk(n) - 1, 0), D_MODEL // D_KV))
    k_cur = pl.BlockSpec((BLOCK, D_KV), lambda n: (blk(n), D_MODEL // D_KV))
    v_prev = pl.BlockSpec((BLOCK, D_KV), lambda n: (jnp.maximum(blk(n) - 1, 0), D_MODEL // D_KV + 1))
    v_cur = pl.BlockSpec((BLOCK, D_KV), lambda n: (blk(n), D_MODEL // D_KV + 1))
    return q_spec, k_prev, k_cur, v_prev, v_cur


def _attn_fwd(sm, qkv, ag_ml, name, ride=None):
    t = qkv.shape[0]
    nb = t // BLOCK

    host = _Hosted(ride, 7, 3, 1)

    def body(*refs):
        (sm_ref, q_ref, kp_ref, kc_ref, vp_ref, vc_ref, ag_ref, y_ref, z_ref, lse_ref, bias_ref), start, finish = host.split(refs)
        n = pl.program_id(0)
        start(n == 0)
        start.middle(n == (3 * nb) // 4)

        @pl.when(n <= 1)
        def _():
            _fill_bias(bias_ref, sm_ref, n)

        lane = lax.broadcasted_iota(jnp.int32, (BLOCK, LANE), 1)
        low = lane < HEAD_DIM
        lse = jnp.zeros((BLOCK, LANE), F32)
        for hk in range(N_Q_HEADS // GROUP):
            kdup = _kv_dup(kp_ref, kc_ref, hk)
            vdup = _kv_dup(vp_ref, vc_ref, hk)
            for k in (0, 1):
                c = slice((2 * hk + k) * LANE, (2 * hk + k + 1) * LANE)
                q2s = q_ref[:, c] * ATTN_SCALE
                outs = []
                for half in (0, 1):
                    head = GROUP * hk + 2 * k + half
                    _, s = _head_scores(q2s, half, kdup, bias_ref[head])
                    sink = sm_ref[0, head]
                    m = jnp.maximum(jnp.max(s, axis=1, keepdims=True), sink)
                    e = jnp.exp(s - m)
                    l = jnp.sum(e, axis=1, keepdims=True) + jnp.exp(sink - m)
                    outs.append(jnp.dot((e * (1.0 / l)).astype(BF16), vdup, preferred_element_type=F32))
                    lse = jnp.where(lane == head, m + jnp.log(l), lse)
                yt = jnp.where(low, outs[0], outs[1])
                y_ref[:, c] = yt
                ag = ag_ref[:, c].astype(F32)
                z_ref[:, c] = (yt * ag * _sigmoid(ag)).astype(BF16)
        lse_ref[...] = lse
        finish(n == nb - 1)

    q_spec, k_prev, k_cur, v_prev, v_cur = _attn_specs(nb, False)
    wide = pl.BlockSpec((BLOCK, D_MODEL), lambda n: (n, 0))
    outs = pl.pallas_call(
        body, name=name, grid=(nb,),
        in_specs=[pl.BlockSpec(memory_space=pltpu.SMEM), q_spec, k_prev, k_cur, v_prev, v_cur, wide] + host.in_specs,
        out_specs=[wide, wide, pl.BlockSpec((BLOCK, LANE), lambda n: (n, 0))] + host.out_specs,
        out_shape=[jax.ShapeDtypeStruct((t, D_MODEL), F32), jax.ShapeDtypeStruct((t, D_MODEL), BF16),
                   jax.ShapeDtypeStruct((t, LANE), F32)] + host.out_shapes,
        scratch_shapes=[pltpu.VMEM((N_Q_HEADS, BLOCK, 2 * BLOCK), F32)] + host.scratch_shapes,
        compiler_params=_params("arbitrary"))(sm, qkv, qkv, qkv, qkv, qkv, ag_ml, *host.arrays)
    res, landed = host.results(outs, 3)
    return (*res, landed) if ride else tuple(res)


def _attn_bwd(sm, qkv, ag_ml, y, lse, dz, name, ride=None):
    t = qkv.shape[0]
    nb = t // BLOCK
    host = _Hosted(ride, 10, 4, 3)

    def body(*refs):
        host_refs, start, finish = host.split(refs)
        (sm_ref, q_ref, kp_ref, kc_ref, vp_ref, vc_ref, ag_ref, y_ref, lse_ref, dz_ref,
         dq_ref, dkv_ref, dag_ref, ds_ref, ck_ref, cv_ref, bias_ref) = host_refs
        n = pl.program_id(0)
        start(n == 0)
        start.middle(n == (3 * nb) // 4)

        @pl.when(n == 0)
        def _():
            ck_ref[...] = jnp.zeros_like(ck_ref)
            cv_ref[...] = jnp.zeros_like(cv_ref)
            ds_ref[...] = jnp.zeros_like(ds_ref)

        @pl.when(n <= 1)
        def _():
            _fill_bias(bias_ref, sm_ref, n)

        @pl.when(n < nb)
        def _():
            lane8 = lax.broadcasted_iota(jnp.int32, (8, LANE), 1)
            row8 = lax.broadcasted_iota(jnp.int32, (8, LANE), 0)
            dsink = jnp.zeros((8, LANE), F32)
            dk_heads, dv_heads = [], []
            lse_tile = lse_ref[...]
            for hk in range(N_Q_HEADS // GROUP):
                kdup = _kv_dup(kp_ref, kc_ref, hk)
                ks = _kv_dup(kp_ref, kc_ref, hk, ATTN_SCALE)
                vdup = _kv_dup(vp_ref, vc_ref, hk)
                qms, dyhs, y_rows = [], [], []
                for k in (0, 1):
                    c = slice((2 * hk + k) * LANE, (2 * hk + k + 1) * LANE)
                    ag = ag_ref[:, c].astype(F32)
                    sg = _sigmoid(ag)
                    dzt = dz_ref[:, c].astype(F32)
                    yt = y_ref[:, c]
                    dag_ref[:, c] = (dzt * yt * (sg * (1.0 + ag * (1.0 - sg)))).astype(BF16)
                    dyt = dzt * (ag * sg)
                    q2s = q_ref[:, c] * ATTN_SCALE
                    for half in (0, 1):
                        hm = _half_mask(q2s.shape, half)
                        qms.append(jnp.where(hm, q2s, jnp.zeros_like(q2s)))
                        dyhs.append(jnp.where(hm, dyt, 0.0))
                        y_rows.append(yt)
                qm4 = jnp.concatenate(qms, axis=0)
                dy4 = jnp.concatenate(dyhs, axis=0)
                dy4_16 = dy4.astype(BF16)
                s4 = lax.dot_general(qm4, kdup, NT_DIMS, preferred_element_type=F32)
                dp4 = lax.dot_general(dy4_16, vdup, NT_DIMS, preferred_element_type=F32)
                probs16, ds16 = [], []
                for r in range(GROUP):
                    head = GROUP * hk + r
                    lh = lse_tile[:, head:head + 1]
                    psink = jnp.exp(sm_ref[0, head] - lh)
                    delta = jnp.sum(dyhs[r] * y_rows[r], axis=1, keepdims=True)
                    for sub in range(BLOCK // HALF_ROWS):
                        sl = slice(sub * HALF_ROWS, (sub + 1) * HALF_ROWS)
                        rows = slice(r * BLOCK + sub * HALF_ROWS, r * BLOCK + (sub + 1) * HALF_ROWS)
                        probs = jnp.exp(s4[rows] + bias_ref[head, sl, :] - lh[sl])
                        ds16.append((probs * (dp4[rows] - delta[sl])).astype(BF16))
                        probs16.append(probs.astype(BF16))
                    dsink = dsink + jnp.where((row8 == 0) & (lane8 == head),
                                              -jnp.sum(psink * delta, axis=0, keepdims=True), 0.0)
                ds4 = jnp.concatenate(ds16, axis=0)
                p4 = jnp.concatenate(probs16, axis=0)
                dq4 = jnp.dot(ds4, ks, preferred_element_type=F32)
                low = _half_mask((BLOCK, LANE), 0)
                for k in (0, 1):
                    c = slice((2 * hk + k) * LANE, (2 * hk + k + 1) * LANE)
                    dq_ref[:, c] = jnp.where(low, dq4[2 * k * BLOCK:(2 * k + 1) * BLOCK],
                                             dq4[(2 * k + 1) * BLOCK:(2 * k + 2) * BLOCK]).astype(BF16)
                dk_acc = lax.dot_general(ds4, qm4, TN_DIMS, preferred_element_type=F32)
                dv_acc = lax.dot_general(p4, dy4_16, TN_DIMS, preferred_element_type=F32)
                dk_heads.append(dk_acc + pltpu.roll(dk_acc, HEAD_DIM, 1))
                dv_heads.append(dv_acc + pltpu.roll(dv_acc, HEAD_DIM, 1))
            ds_ref[...] += dsink
            low = _half_mask((2 * BLOCK, LANE), 0)
            for tile in range(2):
                cols = slice(tile * LANE, (tile + 1) * LANE)
                dkt = jnp.where(low, dk_heads[2 * tile], dk_heads[2 * tile + 1])
                dvt = jnp.where(low, dv_heads[2 * tile], dv_heads[2 * tile + 1])
                dkv_ref[:, cols] = (ck_ref[:, cols] + dkt[0:BLOCK, :]).astype(BF16)
                dkv_ref[:, D_KV + tile * LANE:D_KV + (tile + 1) * LANE] = (cv_ref[:, cols] + dvt[0:BLOCK, :]).astype(BF16)
                ck_ref[:, cols] = dkt[BLOCK:2 * BLOCK, :]
                cv_ref[:, cols] = dvt[BLOCK:2 * BLOCK, :]

        @pl.when(n == nb)
        def _():
            dkv_ref[:, 0:D_KV] = ck_ref[...].astype(BF16)
            dkv_ref[:, D_KV:2 * D_KV] = cv_ref[...].astype(BF16)

        finish(n == nb)

    q_spec, k_prev, k_cur, v_prev, v_cur = _attn_specs(nb, True)
    wide = pl.BlockSpec((BLOCK, D_MODEL), lambda n: (jnp.minimum(n, nb - 1), 0))
    outs = pl.pallas_call(
        body, name=name, grid=(nb + 1,),
        in_specs=[pl.BlockSpec(memory_space=pltpu.SMEM), q_spec, k_prev, k_cur, v_prev, v_cur, wide, wide,
                  pl.BlockSpec((BLOCK, LANE), lambda n: (jnp.minimum(n, nb - 1), 0)), wide] + host.in_specs,
        out_specs=[wide, pl.BlockSpec((BLOCK, 2 * D_KV), lambda n: (jnp.maximum(n - 1, 0), 0)), wide,
                   pl.BlockSpec((8, LANE), lambda n: (0, 0))] + host.out_specs,
        out_shape=[jax.ShapeDtypeStruct((t, D_MODEL), BF16), jax.ShapeDtypeStruct((t, 2 * D_KV), BF16),
                   jax.ShapeDtypeStruct((t, D_MODEL), BF16), jax.ShapeDtypeStruct((8, LANE), F32)] + host.out_shapes,
        scratch_shapes=[pltpu.VMEM((BLOCK, D_KV), F32), pltpu.VMEM((BLOCK, D_KV), F32),
                        pltpu.VMEM((N_Q_HEADS, BLOCK, 2 * BLOCK), F32)] + host.scratch_shapes,
        compiler_params=_params("arbitrary"))(sm, qkv, qkv, qkv, qkv, qkv, ag_ml, y, lse, dz, *host.arrays)
    res, landed = host.results(outs, 4)
    return (*res, landed) if ride else tuple(res)


def _local_grads(x, target, p, project, late_weights, reduce_out, reduce_in):
    h, rx_rg, qkv, ag_ml, wt = project(x, p["pre_g"])
    y_attn, z_attn, lse, landed = _attn_fwd(p["sm"], qkv, ag_ml, "attn_fwd", ride=late_weights[0])
    p = {**p, **late_weights[1](landed)}
    y_rnn, z_rnn, *kept_rnn = _rnn_fwd(
        rx_rg, p["cw"], p["cb"], p["wbd_a"], p["b_a"], p["wbd_x"], p["b_x"], p["lam"], "rnn_fwd")
    br_rnn, br_attn, merged, dout, dy, st_post = _branches_fwd(
        z_rnn, z_attn, ag_ml, p["b_gate"], p["w_rnn"], p["w_attn"], p["w_out"], x, target, p["post_g"], "branches_fwd")

    dbr_rnn, dbr_attn, d_ml, dz_rnn, dz_attn, st_merge = _branches_bwd(
        dout, br_rnn, br_attn, ag_ml, p["b_gate"], p["w_rnn"], p["w_attn"], p["w_out"], "branches_bwd")
    out_pairs = _mm_tn_pairs([(z_rnn, dbr_rnn), (z_attn, dbr_attn), (merged, dout)], "gw_outs")
    d_rx, d_rg, st_rnn, g_rg_a, g_rg_x = _rnn_bwd(
        rx_rg, y_rnn, dz_rnn, *kept_rnn, p["cw"], p["wbd_a"], p["wbd_x"], p["lam"], "rnn_bwd")
    dq, dkv, d_ag, st_sink, red_out = _attn_bwd(p["sm"], qkv, ag_ml, y_attn, lse, dz_attn, "attn_bwd",
                                                ride=reduce_out(out_pairs, g_rg_a, g_rg_x))

    segs = [d_rx, d_rg, dq, dkv, d_ag, d_ml]
    grad_x, st_pre, red_in = _input_grad(segs, wt, x, p["pre_g"], dy, "input_grad",
                                         ride=reduce_in(*_mm_tn_seg_pair(segs, h, "gw_in")))
    return dict(grad_x=grad_x, st_post=st_post, st_merge=st_merge, st_rnn=st_rnn, st_sink=st_sink, st_pre=st_pre,
                red_out=red_out, red_in=red_in)


def _place():
    x, y, c = lax.axis_index("x"), lax.axis_index("y"), lax.axis_index("c")
    return x, y, c


def _gather_ride(shards):
    n = len(shards)

    def copies(ins, outs, sems):
        send_sems, recv_sems, local_sems = sems
        x, y, c = _place()
        me, sibling = (x, y, c), (x, y, 1 - c)
        chips = [(1 - x, y), (x, 1 - y), (1 - x, 1 - y)]

        def slot(a, dev):
            return outs[a].at[4 * dev[0] + 2 * dev[1] + dev[2]]

        def copy(a, k, block, to, src=None):
            return pltpu.make_async_remote_copy(
                src_ref=slot(a, block) if src is None else src, dst_ref=slot(a, block),
                send_sem=send_sems.at[a, k], recv_sem=recv_sems.at[a, k], device_id=to, device_id_type=MESH)

        mine = [pltpu.make_async_copy(ins[a], slot(a, me), local_sems.at[a]) for a in range(n)]
        first = []
        for a in range(n):
            first.append(copy(a, 0, me, sibling, src=ins[a]))
            first += [copy(a, 1 + j, me, (*chip, c), src=ins[a]) for j, chip in enumerate(chips)]
        return me, sibling, chips, c, copy, mine, first

    def start(ins, outs, sems):
        *_, mine, first = copies(ins, outs, sems)
        for cp in mine + first:
            cp.start()

    def middle(ins, outs, sems):
        me, sibling, chips, c, copy, _, _ = copies(ins, outs, sems)
        for j, chip in enumerate(chips):
            for a in range(n):
                copy(a, 1 + j, (*chip, c), me).wait_recv()
                copy(a, 4 + j, (*chip, c), sibling).start()

    def finish(ins, outs, sems):
        me, sibling, chips, c, copy, mine, first = copies(ins, outs, sems)
        passed = [copy(a, 4 + j, (*chip, c), sibling) for j, chip in enumerate(chips) for a in range(n)]
        for a in range(n):
            copy(a, 0, sibling, me).wait_recv()
            for j, chip in enumerate(chips):
                copy(a, 4 + j, (*chip, 1 - c), me).wait_recv()
        for cp in first + passed:
            cp.wait_send()
        for cp in mine:
            cp.wait()

    return _Ride(
        shards, [jax.ShapeDtypeStruct((N_DEV, *s.shape), s.dtype) for s in shards],
        [pltpu.SemaphoreType.DMA((n, 7)), pltpu.SemaphoreType.DMA((n, 7)), pltpu.SemaphoreType.DMA((n,))],
        start, finish, middle)


def _chips_ride(scatter):
    n = len(scatter)

    def copies(ins, outs, sems):
        send_sems, recv_sems, local_sems = sems
        x, y, c = _place()
        own = 2 * x + y
        chips = [(1 - x, y), (x, 1 - y), (1 - x, 1 - y)]
        local = [pltpu.make_async_copy(ins[a].at[own], outs[a].at[own], local_sems.at[a]) for a in range(n)]
        sent = [pltpu.make_async_remote_copy(
            src_ref=ins[a].at[2 * chip[0] + chip[1]], dst_ref=outs[a].at[own],
            send_sem=send_sems.at[a, j], recv_sem=recv_sems.at[a, own], device_id=(*chip, c), device_id_type=MESH)
            for a in range(n) for j, chip in enumerate(chips)]
        return chips, c, local, sent

    def start(ins, outs, sems):
        _, _, local, sent = copies(ins, outs, sems)
        for cp in local + sent:
            cp.start()

    def finish(ins, outs, sems):
        send_sems, recv_sems, _ = sems
        chips, c, local, sent = copies(ins, outs, sems)
        for a in range(n):
            for chip in chips:
                k = 2 * chip[0] + chip[1]
                pltpu.make_async_remote_copy(
                    src_ref=outs[a].at[k], dst_ref=outs[a].at[k], send_sem=send_sems.at[a, 0],
                    recv_sem=recv_sems.at[a, k], device_id=(*chip, c), device_id_type=MESH).wait_recv()
        for cp in sent:
            cp.wait_send()
        for cp in local:
            cp.wait()

    return _Ride(
        list(scatter), [jax.ShapeDtypeStruct(s.shape, s.dtype) for s in scatter],
        [pltpu.SemaphoreType.DMA((n, 3)), pltpu.SemaphoreType.DMA((n, N_CHIP)), pltpu.SemaphoreType.DMA((n,))],
        start, finish)


def _chips_rest_ride(pair, red):
    table = _side_pieces()
    nc = len(CHUNK_ORDER)

    def each(ins, outs, sems, sending, landing):
        send_sems, recv_sems, _ = sems
        pair_ref, red_ref = ins[0], outs[0]
        x, y, c = _place()
        own = 2 * x + y
        for core in (0, 1):
            mine = table[core][0]
            rest = [s for s in range(EARLY_STEPS, nc) if mine[s] is not None]

            @pl.when(c == core)
            def _(mine=mine, rest=rest, core=core):
                for s in rest:
                    _, rows, at = mine[s]
                    k, cp = _to_chip(pair_ref.at[at // SHARD_IN, pl.ds(at % SHARD_IN, rows)], red_ref, mine[s], s,
                                     send_sems, recv_sems, own, core)
                    pl.when(own != k)(lambda cp=cp: sending(cp))
                if landing is not None:
                    for k in range(N_CHIP):
                        @pl.when(own == k)
                        def _(k=k):
                            for s in rest:
                                _, rows, at = mine[s]
                                if at // SHARD_IN == k:
                                    for chip in range(N_CHIP):
                                        if chip != k:
                                            landing(_from_chip(pair_ref.at[k, pl.ds(at % SHARD_IN, rows)], red_ref, mine[s], s,
                                                               send_sems, recv_sems, chip, (x, y, c)))

    def local(ins, outs, sems):
        x, y, _ = _place()
        return pltpu.make_async_copy(ins[0].at[2 * x + y], outs[0].at[2 * x + y], sems[2])

    def start(ins, outs, sems):
        local(ins, outs, sems).start()
        each(ins, outs, sems, lambda cp: cp.start(), None)

    def finish(ins, outs, sems):
        each(ins, outs, sems, lambda cp: cp.wait_send(), lambda cp: cp.wait_recv())
        local(ins, outs, sems).wait()

    return _Ride([pair, red], [jax.ShapeDtypeStruct(red.shape, red.dtype)],
                 [pltpu.SemaphoreType.DMA((nc,)), pltpu.SemaphoreType.DMA((N_CHIP, nc)), pltpu.SemaphoreType.DMA],
                 start, finish, aliases={1: 0})


def _allreduce_small(pack, name):
    shape = pack.shape

    def body(x_ref, o_ref, sib_ref, chip_ref, send_sems, recv_sems):
        x, y, c = _place()
        own = 2 * x + y
        chips = [(1 - x, y), (x, 1 - y), (1 - x, 1 - y)]
        to_sibling = pltpu.make_async_remote_copy(
            src_ref=x_ref, dst_ref=sib_ref, send_sem=send_sems.at[0], recv_sem=recv_sems.at[0],
            device_id=(x, y, 1 - c), device_id_type=MESH)
        to_sibling.start()
        to_sibling.wait()
        chip_ref[own] = x_ref[...] + sib_ref[...]
        sent = [pltpu.make_async_remote_copy(
            src_ref=chip_ref.at[own], dst_ref=chip_ref.at[own], send_sem=send_sems.at[1 + j],
            recv_sem=recv_sems.at[1 + own], device_id=(*chip, c), device_id_type=MESH) for j, chip in enumerate(chips)]
        for cp in sent:
            cp.start()
        for chip in chips:
            k = 2 * chip[0] + chip[1]
            pltpu.make_async_remote_copy(
                src_ref=chip_ref.at[k], dst_ref=chip_ref.at[k], send_sem=send_sems.at[1],
                recv_sem=recv_sems.at[1 + k], device_id=(*chip, c), device_id_type=MESH).wait_recv()
        for cp in sent:
            cp.wait_send()
        o_ref[...] = (chip_ref[0] + chip_ref[1]) + (chip_ref[2] + chip_ref[3])

    return pl.pallas_call(
        body, name=name, out_shape=jax.ShapeDtypeStruct(shape, F32),
        in_specs=[pl.BlockSpec(memory_space=pltpu.VMEM)], out_specs=pl.BlockSpec(memory_space=pltpu.VMEM),
        scratch_shapes=[pltpu.VMEM(shape, F32), pltpu.VMEM((N_CHIP, *shape), F32),
                        pltpu.SemaphoreType.DMA((4,)), pltpu.SemaphoreType.DMA((1 + N_CHIP,))],
    )(pack)


def _adamw(g, w, m, v):
    m = ADAM_B1 * m + (1.0 - ADAM_B1) * g
    v = ADAM_B2 * v + (1.0 - ADAM_B2) * (g * g)
    m_hat = m / (1.0 - ADAM_B1 ** ADAM_STEP)
    v_hat = v / (1.0 - ADAM_B2 ** ADAM_STEP)
    delta = -ADAM_LR * (m_hat / (jnp.sqrt(v_hat) + ADAM_EPS) + ADAM_WD * w)
    return delta, m, v


def _adam_parts(items, name, tr=None):
    ni = len(items)
    npart, r, c = items[0][0].shape
    tr = r if tr is None else min(tr, r)

    def body(*refs):
        for a in range(ni):
            p_ref, w_ref, m_ref, v_ref = refs[4 * a:4 * a + 4]
            g_ref, d_ref, nm_ref, nv_ref = refs[4 * (ni + a):4 * (ni + a) + 4]
            g = p_ref[0].astype(F32)
            for k in range(1, npart):
                g = g + p_ref[k].astype(F32)
            g_ref[...] = g
            d_ref[...], nm_ref[...], nv_ref[...] = _adamw(g, w_ref[...], m_ref[...], v_ref[...])

    tile = pl.BlockSpec((tr, c), lambda i: (i, 0))
    outs = pl.pallas_call(
        body, name=name, grid=(r // tr,),
        in_specs=[pl.BlockSpec((npart, tr, c), lambda i: (0, i, 0)), tile, tile, tile] * ni,
        out_specs=[tile] * (4 * ni), out_shape=[jax.ShapeDtypeStruct((r, c), F32)] * (4 * ni),
        compiler_params=_params("parallel"))(*[arr for item in items for arr in item])
    return [outs[4 * a:4 * a + 4] for a in range(ni)]


def _block_diag(w):
    w4 = w.reshape(N_GROUPS, 4, RNN_BLOCK_W, RNN_BLOCK_W)
    eye = jnp.eye(4, dtype=w.dtype)
    return jnp.einsum("gbij,bc->gbicj", w4, eye).reshape(N_GROUPS, GROUP_W, GROUP_W).astype(BF16)


SMALL_ROWS = 16
ROW_PRE_G, ROW_BGATE, ROW_CONV_B, ROW_B_A, ROW_B_X, ROW_LAM, ROW_POST_G, ROW_LOSS, ROW_SINKS, ROW_CONV_W = 0, 1, 3, 4, 5, 6, 7, 8, 9, 10


def _pack_stats(st_pre, st_merge, st_rnn, st_post, st_sink, name):
    d = D_MODEL

    def body(pre_ref, mg_ref, rnn_ref, post_ref, sink_ref, o_ref):
        rnn = rnn_ref[...]
        sinks = jnp.concatenate([sink_ref[0:1, :], jnp.zeros((1, d - LANE), F32)], axis=1)
        o_ref[0:8, :] = _rows8([pre_ref[0:1, :], mg_ref[0:1, :], mg_ref[1:2, :], rnn[0:1], rnn[1:2], rnn[2:3], rnn[3:4],
                                post_ref[0:1, :]], d)
        o_ref[8:16, :] = _rows8([post_ref[1:2, :], sinks, rnn[4:5], rnn[5:6], rnn[6:7], rnn[7:8]], d)

    return pl.pallas_call(body, name=name, out_shape=jax.ShapeDtypeStruct((SMALL_ROWS, d), F32))(
        st_pre, st_merge, st_rnn, st_post, st_sink)


def _adam_small(total, w, m, v, name):
    d = D_MODEL
    n_in = len(w)
    rows = (ROW_PRE_G, ROW_BGATE, ROW_CONV_B, ROW_B_A, ROW_B_X, ROW_LAM, ROW_POST_G, ROW_SINKS)

    def grad_of(p_ref, row, shape):
        if shape == (1, 2 * d):
            return jnp.concatenate([p_ref[row:row + 1, :], p_ref[row + 1:row + 2, :]], axis=1)
        if shape == (1, RNN_BLOCKS, RNN_BLOCK_W):
            r = p_ref[row:row + 1, :]
            return jnp.concatenate([r[:, b * RNN_BLOCK_W:(b + 1) * RNN_BLOCK_W] for b in range(RNN_BLOCKS)], axis=0)[None]
        return p_ref[row:row + 1, 0:shape[1]]

    def body(*refs):
        p_ref = refs[0]
        w_refs, m_refs, v_refs = (refs[1 + k * n_in:1 + (k + 1) * n_in] for k in range(3))
        outs = refs[1 + 3 * n_in:]
        for k in range(n_in):
            g = grad_of(p_ref, rows[k], w[k].shape)
            res = (g,) + _adamw(g, w_refs[k][...], m_refs[k][...], v_refs[k][...])
            for kind in range(4):
                outs[kind * n_in + k][...] = res[kind]

    outs = pl.pallas_call(
        body, name=name, out_shape=[jax.ShapeDtypeStruct(a.shape, F32) for _ in range(4) for a in w])(total, *w, *m, *v)
    return [outs[kind * n_in:(kind + 1) * n_in] for kind in range(4)]


def kernel(x, pre_norm_g, w_in, b_gate, conv_w, conv_b, w_rg_a, b_rg_a, w_rg_x, b_rg_x, lru_lambda, attn_sinks, w_rnn_out, w_attn_out, w_out, post_norm_g, loss_target, m_pre_norm_g, m_w_in, m_b_gate, m_conv_w, m_conv_b, m_w_rg_a, m_b_rg_a, m_w_rg_x, m_b_rg_x, m_lru_lambda, m_attn_sinks, m_w_rnn_out, m_w_attn_out, m_w_out, m_post_norm_g, v_pre_norm_g, v_w_in, v_b_gate, v_conv_w, v_conv_b, v_w_rg_a, v_b_rg_a, v_w_rg_x, v_b_rg_x, v_lru_lambda, v_attn_sinks, v_w_rnn_out, v_w_attn_out, v_w_out, v_post_norm_g):
    cx, cy, cc = _place()
    dev = 4 * cx + 2 * cy + cc

    w_in_t, m_in_t, v_in_t = (jnp.transpose(a[0]) for a in (w_in, m_w_in, v_w_in))
    wt_shard = w_in_t.astype(BF16)

    def project(xs, pre_g):
        h, rx_rg, qkv, ag_ml, wt_all = _gather_project(xs, pre_g, wt_shard, "gather_project")
        return h, rx_rg, qkv, ag_ml, wt_all.reshape(D_IN, D_MODEL)

    def late_unpack(landed):
        w_rnn_all, w_attn_all, w_out_all, cw_all = landed
        return dict(w_rnn=w_rnn_all.reshape(D_RNN, D_MODEL), w_attn=w_attn_all.reshape(D_MODEL, D_MODEL),
                    w_out=w_out_all.reshape(D_MODEL, D_MODEL), cw=jnp.transpose(cw_all, (1, 0, 2)).reshape(4, D_RNN))

    late_weights = (_gather_ride([w_rnn_out[0].astype(BF16), w_attn_out[0].astype(BF16), w_out[0].astype(BF16), conv_w[0]]),
                    late_unpack)

    heads = jnp.arange(1, N_Q_HEADS + 1, dtype=F32)
    slopes = jnp.exp2(-ALIBI_MAX_BIAS * heads / N_Q_HEADS)
    b_a = b_rg_a.reshape(1, D_RNN)
    b_x = b_rg_x.reshape(1, D_RNN)
    p = dict(
        pre_g=pre_norm_g, post_g=post_norm_g, b_gate=b_gate, cb=conv_b,
        wbd_a=_block_diag(w_rg_a[0]), b_a=b_a, wbd_x=_block_diag(w_rg_x[0]), b_x=b_x, lam=lru_lambda,
        sm=jnp.pad(attn_sinks, ((0, 1), (0, 0))) + jnp.pad(slopes[None, :], ((1, 0), (0, 0))))

    flat = (RNN_BLOCKS * RNN_BLOCK_W, RNN_BLOCK_W)

    def reduce_out(out_pairs, g_rg_a, g_rg_x):
        return _join_rides(_chips_ride(out_pairs), _gather_ride([g_rg_a.reshape(flat), g_rg_x.reshape(flat)]))

    reduce_in = _chips_rest_ride

    g = _local_grads(x[0], loss_target[0], p, project, late_weights, reduce_out, reduce_in)
    small = _allreduce_small(
        _pack_stats(g["st_pre"], g["st_merge"], g["st_rnn"], g["st_post"], g["st_sink"], "pack_stats"), "allreduce_small")

    out = {}
    red = g["red_out"]
    w_in_out, = _adam_parts([(g["red_in"][0], w_in_t, m_in_t, v_in_t)], "adam_w_in", tr=SHARD_IN // 2)
    out["w_in"] = [jnp.transpose(o) for o in w_in_out]
    out["w_rnn_out"], out["w_attn_out"], out["w_out"] = _adam_parts(
        [(red[0], w_rnn_out[0], m_w_rnn_out[0], v_w_rnn_out[0]), (red[1], w_attn_out[0], m_w_attn_out[0], v_w_attn_out[0]),
         (red[2], w_out[0], m_w_out[0], v_w_out[0])], "adam_w_outs")
    out["w_rg_a"], out["w_rg_x"] = _adam_parts(
        [(red[3], w_rg_a.reshape(flat), m_w_rg_a.reshape(flat), v_w_rg_a.reshape(flat)),
         (red[4], w_rg_x.reshape(flat), m_w_rg_x.reshape(flat), v_w_rg_x.reshape(flat))], "adam_w_rg")

    small_names = ("pre_norm_g", "b_gate", "conv_b", "b_rg_a", "b_rg_x", "lru_lambda", "post_norm_g", "attn_sinks")
    small_out = _adam_small(
        small,
        (pre_norm_g, b_gate, conv_b, b_rg_a, b_rg_x, lru_lambda, post_norm_g, attn_sinks),
        (m_pre_norm_g, m_b_gate, m_conv_b, m_b_rg_a, m_b_rg_x, m_lru_lambda, m_post_norm_g, m_attn_sinks),
        (v_pre_norm_g, v_b_gate, v_conv_b, v_b_rg_a, v_b_rg_x, v_lru_lambda, v_post_norm_g, v_attn_sinks),
        "adam_small")
    g_cw = lax.dynamic_slice(small[ROW_CONV_W:ROW_CONV_W + 4], (0, dev * SHARD_OUT), (4, SHARD_OUT))
    out["conv_w"], = _adam_parts([(g_cw[None], conv_w[0], m_conv_w[0], v_conv_w[0])], "adam_conv_w")

    shapes = dict(w_in=(1, D_MODEL, SHARD_IN), w_rnn_out=(1, SHARD_OUT, D_MODEL), w_attn_out=(1, SHARD_OUT, D_MODEL),
                  w_out=(1, SHARD_OUT, D_MODEL), w_rg_a=(1, RNN_BLOCKS, RNN_BLOCK_W, RNN_BLOCK_W),
                  w_rg_x=(1, RNN_BLOCKS, RNN_BLOCK_W, RNN_BLOCK_W), conv_w=(1, 4, SHARD_OUT))
    weights = ["pre_norm_g", "w_in", "b_gate", "conv_w", "conv_b", "w_rg_a", "b_rg_a", "w_rg_x", "b_rg_x",
               "lru_lambda", "attn_sinks", "w_rnn_out", "w_attn_out", "w_out", "post_norm_g"]
    results = []
    for kind in range(4):
        for name in weights:
            if name in out:
                results.append(out[name][kind].reshape(shapes[name]))
            else:
                results.append(small_out[kind][small_names.index(name)])
    loss = 0.5 / D_MODEL * jnp.sum(small[ROW_LOSS])
    return (loss, g["grad_x"][None], *results)
```

```python
import jax
import jax.numpy as jnp
from jax import lax
from jax.experimental import pallas as pl
from jax.experimental.pallas import tpu as pltpu

F32 = jnp.float32
BF16 = jnp.bfloat16

D_MODEL = 1024
D_RNN = 1024
RNN_BLOCKS = 16
RNN_BLOCK_W = 64
LRU_C = 8.0
N_Q_HEADS = 16
HEAD_DIM = 64
D_KV = 256
BLOCK = 128
ALIBI_MAX_BIAS = 8.0
EPS = 1e-6
D_IN = 6656
N_DEV = 8
N_CHIP = 4
SHARD_IN = D_IN // N_DEV
SHARD_OUT = D_MODEL // N_DEV
ATTN_SCALE = HEAD_DIM ** -0.5
MASKED = -1e30

ADAM_LR = 0.001
ADAM_B1 = 0.9
ADAM_B2 = 0.999
ADAM_EPS = 1e-08
ADAM_WD = 0.01
ADAM_STEP = 10

VMEM_LIMIT_BYTES = 52 * 1024 * 1024
LANE = 128
GROUP_W = 256
N_GROUPS = D_RNN // GROUP_W
SEG_CHUNK = 512

NT_DIMS = (((1,), (1,)), ((), ()))
TN_DIMS = (((0,), (0,)), ((), ()))
MESH = pl.DeviceIdType.MESH
ANY = pl.BlockSpec(memory_space=pl.ANY)


def _params(*semantics):
    return pltpu.CompilerParams(dimension_semantics=semantics, vmem_limit_bytes=VMEM_LIMIT_BYTES)


def _sigmoid(x):
    return 0.5 * jnp.tanh(0.5 * x) + 0.5


def _log1p(e):
    u = 1.0 + e
    den = jnp.where(u == 1.0, 1.0, u - 1.0)
    return jnp.where(u == 1.0, e, jnp.log(u) * (e / den))


def _softplus(z):
    return jnp.maximum(z, 0.0) + _log1p(jnp.exp(-jnp.abs(z)))


def _rows8(rows, width):
    idx = lax.broadcasted_iota(jnp.int32, (8, width), 0)
    out = jnp.zeros((8, width), F32)
    for r, v in enumerate(rows):
        out = jnp.where(idx == r, v, out)
    return out


class _Ride:
    def __init__(self, arrays, out_shapes, scratch_shapes, start, finish, middle=None, aliases=None):
        self.arrays, self.out_shapes, self.scratch_shapes = list(arrays), list(out_shapes), list(scratch_shapes)
        self.start, self.finish, self.middle = start, finish, middle
        self.aliases = dict(aliases or {})


class _Hosted:
    def __init__(self, ride, n_in, n_out, n_scratch=0, aliasing=False):
        self.ride = ride
        assert aliasing or not (ride and ride.aliases), "this host does not alias"
        self.aliases = {n_in + i: n_out + o for i, o in ride.aliases.items()} if ride else {}
        self.sizes = (n_in, len(ride.arrays) if ride else 0, n_out, len(ride.out_shapes) if ride else 0, n_scratch)
        self.arrays = ride.arrays if ride else []
        self.in_specs = [ANY] * len(self.arrays)
        self.out_shapes = ride.out_shapes if ride else []
        self.out_specs = [ANY] * len(self.out_shapes)
        self.scratch_shapes = ride.scratch_shapes if ride else []

    def split(self, refs):
        n_in, r_in, n_out, r_out, n_scr = self.sizes
        cuts = [0, n_in, n_in + r_in, n_in + r_in + n_out, n_in + r_in + n_out + r_out, n_in + r_in + n_out + r_out + n_scr]
        host_in, ride_in, host_out, ride_out, host_scr = (refs[cuts[k]:cuts[k + 1]] for k in range(5))
        ride_scr = refs[cuts[5]:]

        def start(when):
            if self.ride is not None:
                pl.when(when)(lambda: self.ride.start(ride_in, ride_out, ride_scr))

        def finish(when):
            if self.ride is not None:
                pl.when(when)(lambda: self.ride.finish(ride_in, ride_out, ride_scr))

        def middle(when):
            if self.ride is not None and self.ride.middle is not None:
                pl.when(when)(lambda: self.ride.middle(ride_in, ride_out, ride_scr))

        start.middle = middle
        return tuple(host_in) + tuple(host_out) + tuple(host_scr), start, finish

    def results(self, outs, n_out):
        outs = list(outs) if isinstance(outs, (list, tuple)) else [outs]
        return outs[:n_out], outs[n_out:]


def _join_rides(a, b):
    na, nb_ = len(a.arrays), len(b.arrays)
    oa = len(a.out_shapes)
    sa = len(a.scratch_shapes)

    def both(fa, fb):
        def run(ins, outs, sems):
            if fa is not None:
                fa(ins[:na], outs[:oa], sems[:sa])
            if fb is not None:
                fb(ins[na:na + nb_], outs[oa:], sems[sa:])
        return run

    middle = both(a.middle, b.middle) if (a.middle or b.middle) else None
    return _Ride(a.arrays + b.arrays, a.out_shapes + b.out_shapes, a.scratch_shapes + b.scratch_shapes,
                 both(a.start, b.start), both(a.finish, b.finish), middle)


def _load_resident(w_hbm, w_vmem, sems, first):
    def piece(c):
        rows = pl.ds(c * SEG_CHUNK, SEG_CHUNK)
        return pltpu.make_async_copy(w_hbm.at[rows], w_vmem.at[rows], sems.at[c])

    @pl.when(first)
    def _():
        for c in range(w_vmem.shape[0] // SEG_CHUNK):
            piece(c).start()

    def ready(c):
        @pl.when(first)
        def _():
            piece(c).wait()

    return ready


CHIP_ROWS = 2 * SHARD_IN
PROJ_WIDTHS = (2 * D_RNN, D_MODEL + 2 * D_KV, 3 * D_MODEL)
PROJ_DTYPES = (F32, BF16, BF16)


def _chip_pieces():
    starts = [0, PROJ_WIDTHS[0], PROJ_WIDTHS[0] + PROJ_WIDTHS[1], D_IN]
    pieces = []
    for k in range(N_CHIP):
        lo, hi = k * CHIP_ROWS, (k + 1) * CHIP_ROWS
        cur = []
        for a in range(len(PROJ_WIDTHS)):
            s0, s1 = max(lo, starts[a]), min(hi, starts[a + 1])
            if s0 < s1:
                cur.append((a, s0 - starts[a], s1 - s0, s0 - lo))
        pieces.append(cur)
    return pieces


def _gather_project(x, g, wt_shard, name, tm=1024):
    t, k = x.shape
    tm = min(tm, t)
    nt = t // tm
    pieces = _chip_pieces()

    def body(x_ref, g_ref, shard_ref, h_out, rx_ref, qkv_ref, ag_ref, wt_all,
             w_c, stage, o32, o16, h_all, send_sems, recv_sems, local_sem, stage_sems, out_sems, h_sems):
        s, ti = pl.program_id(0), pl.program_id(1)
        px, py, pc = _place()
        me, sibling = (px, py, pc), (px, py, 1 - pc)
        chips = [(px, py), (1 - px, py), (px, 1 - py), (1 - px, 1 - py)]
        outs = (rx_ref, qkv_ref, ag_ref)

        def slot(dev):
            return wt_all.at[4 * dev[0] + 2 * dev[1] + dev[2]]

        def copy(kk, block, to, src=None):
            return pltpu.make_async_remote_copy(
                src_ref=slot(block) if src is None else src, dst_ref=slot(block),
                send_sem=send_sems.at[kk], recv_sem=recv_sems.at[kk], device_id=to, device_id_type=MESH)

        mine = pltpu.make_async_copy(shard_ref, slot(me), local_sem)
        first = [copy(0, me, sibling, src=shard_ref)] + [copy(1 + j, me, (*chips[1 + j], pc), src=shard_ref) for j in range(3)]
        passed = [copy(4 + j, (*chips[1 + j], pc), sibling) for j in range(3)]

        @pl.when((s == 0) & (ti == 0))
        def _():
            mine.start()
            for cp in first[:3]:
                cp.start()

        def pass_on(step):
            copy(step, (*chips[step], pc), me).wait_recv()
            passed[step - 1].start()

        @pl.when((s == 2) & (ti == nt - 1))
        def _():
            pass_on(3)

        for step in range(N_CHIP):
            @pl.when((s == step) & (ti == 0))
            def _(step=step):
                chip = chips[step]
                if step == 0:
                    mine.wait()
                    copy(0, sibling, me).wait_recv()
                else:
                    if step == 1:
                        pass_on(1)
                        first[3].start()
                        pass_on(2)
                    copy(3 + step, (*chip, 1 - pc), me).wait_recv()
                loads = [pltpu.make_async_copy(slot((*chip, core)), stage.at[pl.ds(core * SHARD_IN, SHARD_IN)], stage_sems.at[core])
                         for core in (0, 1)]
                for cp in loads:
                    cp.start()
                for cp in loads:
                    cp.wait()
                w_c[...] = stage[...].T

        rows = pl.ds(pl.multiple_of(ti * tm, tm), tm)

        @pl.when(s == 0)
        def _():
            xv = x_ref[...]
            h_all[rows, :] = (xv * lax.rsqrt(jnp.mean(xv * xv, axis=-1, keepdims=True) + EPS) * g_ref[...]).astype(BF16)

        res = jnp.dot(h_all[rows, :], w_c[...], preferred_element_type=F32)

        n = s * nt + ti
        buf = lax.rem(n, 2)
        chip_idx = [2 * cx + cy for cx, cy in chips]

        def chip_at(step):
            return jnp.where(step == 0, chip_idx[0], jnp.where(step == 1, chip_idx[1], jnp.where(step == 2, chip_idx[2], chip_idx[3])))

        def h_write(b, tile):
            return pltpu.make_async_copy(h_all.at[pl.ds(tile * tm, tm)], h_out.at[pl.ds(tile * tm, tm)], h_sems.at[b])

        def writes(kchip, b, tile):
            cps = []
            for idx, (a, col, w, src) in enumerate(pieces[kchip]):
                staged = (o16 if PROJ_DTYPES[a] == BF16 else o32).at[b, :, pl.ds(src, w)]
                cps.append(pltpu.make_async_copy(staged, outs[a].at[pl.ds(tile * tm, tm), pl.ds(col, w)], out_sems.at[b, idx]))
            return cps

        o32[buf] = res
        o16[buf] = res.astype(BF16)
        kcur = chip_at(s)

        @pl.when(n > 0)
        def _():
            kprev = chip_at(lax.div(n - 1, nt))
            for kchip in range(N_CHIP):
                @pl.when(kprev == kchip)
                def _(kchip=kchip):
                    for cp in writes(kchip, 1 - buf, lax.rem(n - 1, nt)):
                        cp.wait()

            @pl.when(n <= nt)
            def _():
                h_write(1 - buf, n - 1).wait()

        @pl.when(s == 0)
        def _():
            h_write(buf, ti).start()

        for kchip in range(N_CHIP):
            @pl.when(kcur == kchip)
            def _(kchip=kchip):
                for cp in writes(kchip, buf, ti):
                    cp.start()

        @pl.when(n == N_CHIP * nt - 1)
        def _():
            for kchip in range(N_CHIP):
                @pl.when(kcur == kchip)
                def _(kchip=kchip):
                    for cp in writes(kchip, buf, ti):
                        cp.wait()
            for cp in first + passed:
                cp.wait_send()

    tile = pl.BlockSpec((tm, k), lambda s, ti: (jnp.where(s == 0, ti, nt - 1), 0))
    return pl.pallas_call(
        body, name=name, grid=(N_CHIP, nt),
        in_specs=[tile, pl.BlockSpec((1, k), lambda s, ti: (0, 0)), ANY],
        out_specs=[ANY, ANY, ANY, ANY, ANY],
        out_shape=[jax.ShapeDtypeStruct((t, k), BF16)]
        + [jax.ShapeDtypeStruct((t, w), dt) for w, dt in zip(PROJ_WIDTHS, PROJ_DTYPES)]
        + [jax.ShapeDtypeStruct((N_DEV, SHARD_IN, k), BF16)],
        scratch_shapes=[pltpu.VMEM((k, CHIP_ROWS), BF16), pltpu.VMEM((CHIP_ROWS, k), BF16),
                        pltpu.VMEM((2, tm, CHIP_ROWS), F32), pltpu.VMEM((2, tm, CHIP_ROWS), BF16), pltpu.VMEM((t, k), BF16),
                        pltpu.SemaphoreType.DMA((7,)), pltpu.SemaphoreType.DMA((7,)), pltpu.SemaphoreType.DMA,
                        pltpu.SemaphoreType.DMA((2,)), pltpu.SemaphoreType.DMA((2, 2)), pltpu.SemaphoreType.DMA((2,))],
        compiler_params=_params("arbitrary", "arbitrary"))(x, g, wt_shard)


def _mm_tn_pairs(prods, name):
    n_prod = len(prods)
    ktok, m = prods[0][0].shape
    n = prods[0][1].shape[1]
    half, blk = m // 2, m // N_DEV
    ns = 2 * n_prod
    kept = N_CHIP * blk

    def side_pieces(s, side):
        i, hf = divmod(s, 2)
        first = hf * N_DEV // 2
        return [((d - first) * blk, i * kept + (d // 2) * blk) for d in range(first, first + N_DEV // 2) if d % 2 == side]

    def body(*refs):
        a_hbm, b_hbm, pair_refs = refs[:n_prod], refs[n_prod:2 * n_prod], refs[2 * n_prod:3 * n_prod]
        a_buf, b_buf, res_buf, recv_all, pair_all, a_sems, b_sems, send_sems, recv_sems, out_sems = refs[3 * n_prod:]
        step = pl.program_id(0)
        px, py, pc = _place()

        def fetch_a(s):
            return pltpu.make_async_copy(a_hbm[s // 2].at[:, pl.ds((s % 2) * half, half)], a_buf.at[s % 2], a_sems.at[s % 2])

        def fetch_b(i):
            return pltpu.make_async_copy(b_hbm[i], b_buf.at[i % 2], b_sems.at[i % 2])

        @pl.when(step == 0)
        def _():
            fetch_a(0).start()
            fetch_b(0).start()

        for s in range(ns):
            @pl.when(step == s)
            def _(s=s):
                if s + 1 < ns:
                    fetch_a(s + 1).start()
                if s % 2 == 0 and s // 2 + 1 < n_prod:
                    fetch_b(s // 2 + 1).start()
                fetch_a(s).wait()
                if s % 2 == 0:
                    fetch_b(s // 2).wait()

        res_buf[step % 2] = lax.dot_general(a_buf[step % 2], b_buf[(step // 2) % 2], TN_DIMS, preferred_element_type=F32)

        def crossing(s, j, piece):
            off, at = piece
            return pltpu.make_async_remote_copy(
                src_ref=res_buf.at[s % 2, pl.ds(off, blk)], dst_ref=recv_all.at[pl.ds(at, blk)],
                send_sem=send_sems.at[s, j], recv_sem=recv_sems.at[s, j], device_id=(px, py, 1 - pc), device_id_type=MESH)

        def write(s, core):
            at = side_pieces(s, core)[0][1]
            return pltpu.make_async_copy(pair_all.at[pl.ds(at, 2 * blk)], pair_refs[s // 2].at[pl.ds(at % kept, 2 * blk)], out_sems.at[s])

        for core in (0, 1):
            def settle(s, core=core):
                for j, piece in enumerate(side_pieces(s, 1 - core)):
                    crossing(s, j, piece).wait_send()
                for j, (off, at) in enumerate(side_pieces(s, core)):
                    crossing(s, j, (off, at)).wait_recv()
                    pair_all[at:at + blk] = (res_buf[s % 2, off:off + blk] + recv_all[at:at + blk]).astype(BF16)
                write(s, core).start()

            for s in range(ns):
                @pl.when((pc == core) & (step == s))
                def _(s=s, settle=settle, core=core):
                    for j, piece in enumerate(side_pieces(s, 1 - core)):
                        crossing(s, j, piece).start()
                    if s > 0:
                        settle(s - 1)
                    if s == ns - 1:
                        settle(s)
                        for t in range(ns):
                            write(t, core).wait()

    pairs = pl.pallas_call(
        body, name=name, grid=(ns,),
        in_specs=[ANY] * (2 * n_prod), out_specs=[ANY] * n_prod,
        out_shape=[jax.ShapeDtypeStruct((kept, n), BF16)] * n_prod,
        scratch_shapes=[pltpu.VMEM((2, ktok, half), prods[0][0].dtype), pltpu.VMEM((2, ktok, n), prods[0][1].dtype),
                        pltpu.VMEM((2, half, n), F32), pltpu.VMEM((n_prod * kept, n), F32), pltpu.VMEM((n_prod * kept, n), BF16),
                        pltpu.SemaphoreType.DMA((2,)), pltpu.SemaphoreType.DMA((2,)),
                        pltpu.SemaphoreType.DMA((ns, 2)), pltpu.SemaphoreType.DMA((ns, 2)), pltpu.SemaphoreType.DMA((ns,))],
        compiler_params=_params("arbitrary"))(*[a for a, _ in prods], *[b for _, b in prods])
    return [pair.reshape(N_CHIP, blk, n) for pair in pairs]


def _segment_chunks(segs):
    bounds = [0]
    for s in segs:
        bounds.append(bounds[-1] + s.shape[1] // SEG_CHUNK)
    return bounds


def _input_grad(segs, wt, x, g, dy, name, tm=512, ride=None):
    m = segs[0].shape[0]
    rows, n = wt.shape
    tm = min(tm, m)
    bounds = _segment_chunks(segs)
    n_seg = len(segs)
    ni = m // tm
    host = _Hosted(ride, n_seg + 4, 2, 2, aliasing=True)

    def body(*refs):
        host_refs, start, finish = host.split(refs)
        a_refs = host_refs[:n_seg]
        wt_hbm, x_ref, g_ref, dy_ref, gx_ref, st_ref, wt_vmem, sems = host_refs[n_seg:]
        i = pl.program_id(0)
        start(i == 0)

        @pl.when(i == 0)
        def _():
            st_ref[...] = jnp.zeros_like(st_ref)

        ready = _load_resident(wt_hbm, wt_vmem, sems, i == 0)
        dh = None
        for s in range(n_seg):
            for c in range(bounds[s], bounds[s + 1]):
                ready(c)
            part = jnp.dot(a_refs[s][...], wt_vmem[bounds[s] * SEG_CHUNK:bounds[s + 1] * SEG_CHUNK, :], preferred_element_type=F32)
            dh = part if dh is None else dh + part
        xv = x_ref[...]
        r = lax.rsqrt(jnp.mean(xv * xv, axis=-1, keepdims=True) + EPS)
        xn = xv * r
        dxn = dh * g_ref[...]
        gx_ref[...] = dy_ref[...] + r * (dxn - xn * jnp.mean(dxn * xn, axis=-1, keepdims=True))
        st_ref[...] += _rows8([jnp.sum(dh * xn, axis=0, keepdims=True)], n)
        finish(i == ni - 1)

    tile = pl.BlockSpec((tm, n), lambda i: (i, 0))
    outs = pl.pallas_call(
        body, name=name, grid=(ni,),
        in_specs=[pl.BlockSpec((tm, sg.shape[1]), lambda i: (i, 0)) for sg in segs]
        + [ANY, tile, pl.BlockSpec((1, n), lambda i: (0, 0)), tile] + host.in_specs,
        out_specs=[tile, pl.BlockSpec((8, n), lambda i: (0, 0))] + host.out_specs,
        out_shape=[jax.ShapeDtypeStruct((m, n), F32), jax.ShapeDtypeStruct((8, n), F32)] + host.out_shapes,
        scratch_shapes=[pltpu.VMEM((rows, n), wt.dtype), pltpu.SemaphoreType.DMA((rows // SEG_CHUNK,))] + host.scratch_shapes,
        input_output_aliases=host.aliases,
        compiler_params=_params("arbitrary"))(*segs, wt, x, g, dy, *host.arrays)
    res, landed = host.results(outs, 2)
    return (*res, landed) if ride else tuple(res)


CHUNK_ORDER = (0, 4, 7, 10, 1, 5, 8, 11, 2, 6, 9, 12, 3)
EARLY_STEPS = 8


def _side_pieces():
    table = []
    for core in (0, 1):
        sides = ([None] * len(CHUNK_ORDER), [None] * len(CHUNK_ORDER))
        for s, cc in enumerate(CHUNK_ORDER):
            g0 = cc * SEG_CHUNK
            for d in range(N_DEV):
                lo, hi = max(g0, d * SHARD_IN), min(g0 + SEG_CHUNK, (d + 1) * SHARD_IN)
                if lo < hi:
                    side = sides[0 if d % 2 == core else 1]
                    assert side[s] is None
                    side[s] = (lo - g0, hi - lo, (d // 2) * SHARD_IN + lo - d * SHARD_IN)
        table.append(sides)
    return table


def _to_chip(src, red_ref, piece, s, send_sems, recv_sems, own, core):
    _, rows, at = piece
    k, r0 = divmod(at, SHARD_IN)
    return k, pltpu.make_async_remote_copy(
        src_ref=src, dst_ref=red_ref.at[own, pl.ds(r0, rows)], send_sem=send_sems.at[s], recv_sem=recv_sems.at[own, s],
        device_id=(k // 2, k % 2, core), device_id_type=MESH)


def _from_chip(src, red_ref, piece, s, send_sems, recv_sems, chip, me):
    _, rows, at = piece
    return pltpu.make_async_remote_copy(
        src_ref=src, dst_ref=red_ref.at[chip, pl.ds(at % SHARD_IN, rows)], send_sem=send_sems.at[s], recv_sem=recv_sems.at[chip, s],
        device_id=me, device_id_type=MESH)


def _mm_tn_seg_pair(segs, b, name):
    ktok = segs[0].shape[0]
    n = b.shape[1]
    bounds = _segment_chunks(segs)
    n_seg = len(segs)
    nc = bounds[-1]
    assert nc == len(CHUNK_ORDER)
    seg_of = [s for s in range(n_seg) for _ in range(bounds[s], bounds[s + 1])]
    table = _side_pieces()

    def body(*refs):
        a_hbm, b_hbm, pair_ref, red_ref = refs[:n_seg], refs[n_seg], refs[n_seg + 1], refs[n_seg + 2]
        (a_buf, b_vmem, res_buf, recv_all, pair_all, a_sems, b_sem, send_sems, recv_sems, out_sems,
         chip_send, chip_recv) = refs[n_seg + 3:]
        step = pl.program_id(0)
        px, py, pc = _place()
        own = 2 * px + py

        def fetch(s):
            sg = seg_of[CHUNK_ORDER[s]]
            cols = pl.ds((CHUNK_ORDER[s] - bounds[sg]) * SEG_CHUNK, SEG_CHUNK)
            return pltpu.make_async_copy(a_hbm[sg].at[:, cols], a_buf.at[s % 2], a_sems.at[s % 2])

        @pl.when(step == 0)
        def _():
            whole = pltpu.make_async_copy(b_hbm, b_vmem, b_sem)
            whole.start()
            fetch(0).start()
            whole.wait()

        for s in range(nc):
            @pl.when(step == s)
            def _(s=s):
                if s + 1 < nc:
                    fetch(s + 1).start()
                fetch(s).wait()

        res_buf[step % 2] = lax.dot_general(a_buf[step % 2], b_vmem[...], TN_DIMS, preferred_element_type=F32)

        def crossing(s, piece):
            off, rows, at = piece
            return pltpu.make_async_remote_copy(
                src_ref=res_buf.at[s % 2, pl.ds(off, rows)], dst_ref=recv_all.at[pl.ds(at, rows)],
                send_sem=send_sems.at[s], recv_sem=recv_sems.at[s], device_id=(px, py, 1 - pc), device_id_type=MESH)

        def write(s, piece):
            _, rows, at = piece
            return pltpu.make_async_copy(pair_all.at[pl.ds(at, rows)], pair_ref.at[pl.ds(at, rows)], out_sems.at[s])

        def to_chip(s, piece, core):
            return _to_chip(pair_all.at[pl.ds(piece[2], piece[1])], red_ref, piece, s, chip_send, chip_recv, own, core)

        for core in (0, 1):
            mine, theirs = table[core]

            def settle(s, mine=mine, theirs=theirs, core=core):
                if theirs[s] is not None:
                    crossing(s, theirs[s]).wait_send()
                if mine[s] is not None:
                    off, rows, at = mine[s]
                    crossing(s, mine[s]).wait_recv()
                    pair_all[at:at + rows] = (res_buf[s % 2, off:off + rows] + recv_all[at:at + rows]).astype(BF16)
                    write(s, mine[s]).start()
                    if s < EARLY_STEPS:
                        k, cp = to_chip(s, mine[s], core)
                        pl.when(own != k)(cp.start)

            for s in range(nc):
                @pl.when((pc == core) & (step == s))
                def _(s=s, settle=settle, mine=mine, theirs=theirs, core=core):
                    if theirs[s] is not None:
                        crossing(s, theirs[s]).start()
                    if s > 0:
                        settle(s - 1)
                    if s == nc - 1:
                        settle(s)
                        kept = [t for t in range(nc) if mine[t] is not None]
                        for t in kept:
                            write(t, mine[t]).wait()
                        early = [t for t in kept if t < EARLY_STEPS]
                        for t in early:
                            k, cp = to_chip(t, mine[t], core)
                            pl.when(own != k)(cp.wait_send)
                        for k in range(N_CHIP):
                            @pl.when(own == k)
                            def _(k=k):
                                for t in early:
                                    if mine[t][2] // SHARD_IN == k:
                                        for chip in range(N_CHIP):
                                            if chip != k:
                                                _from_chip(pair_all.at[pl.ds(mine[t][2], mine[t][1])], red_ref, mine[t], t,
                                                           chip_send, chip_recv, chip, (px, py, pc)).wait_recv()

    flat = jax.ShapeDtypeStruct((N_CHIP * SHARD_IN, n), BF16)
    pair, red = pl.pallas_call(
        body, name=name, grid=(nc,),
        in_specs=[ANY] * (n_seg + 1), out_specs=[ANY, ANY],
        out_shape=[flat, jax.ShapeDtypeStruct((N_CHIP, SHARD_IN, n), BF16)],
        scratch_shapes=[pltpu.VMEM((2, ktok, SEG_CHUNK), segs[0].dtype), pltpu.VMEM((ktok, n), b.dtype),
                        pltpu.VMEM((2, SEG_CHUNK, n), F32), pltpu.VMEM(flat.shape, F32), pltpu.VMEM(flat.shape, BF16),
                        pltpu.SemaphoreType.DMA((2,)), pltpu.SemaphoreType.DMA,
                        pltpu.SemaphoreType.DMA((nc,)), pltpu.SemaphoreType.DMA((nc,)), pltpu.SemaphoreType.DMA((nc,)),
                        pltpu.SemaphoreType.DMA((nc,)), pltpu.SemaphoreType.DMA((N_CHIP, nc))],
        compiler_params=_params("arbitrary"))(*segs, b)
    return pair.reshape(N_CHIP, SHARD_IN, n), red


RING = 3


def _ring_specs(streams, tm):
    return ([pltpu.VMEM((RING, tm, width), a.dtype) for a, _, width in streams]
            + [pltpu.SemaphoreType.DMA((len(streams), RING))])


def _ring_step(streams, hbm_refs, rings, sems, n, n_steps, tm):
    def fetch(step):
        rows = pl.ds(pl.multiple_of(step * tm, tm), tm)
        return [pltpu.make_async_copy(ref.at[rows, pl.ds(col, width)], ring.at[lax.rem(step, RING)], sems.at[k, lax.rem(step, RING)])
                for k, ((_, col, width), ref, ring) in enumerate(zip(streams, hbm_refs, rings))]

    @pl.when(n == 0)
    def _():
        for cp in fetch(0) + fetch(1):
            cp.start()

    @pl.when(n + 2 < n_steps)
    def _():
        for cp in fetch(n + 2):
            cp.start()

    for cp in fetch(n):
        cp.wait()
    return lax.rem(n, RING)


def _branches_fwd(z_rnn, z_attn, ag_ml, b_gate, w_rnn, w_attn, w_out, x, target, g_post, name, tm=512):
    t, d = x.shape
    tm = min(tm, t)
    n_steps = t // tm
    streams = [(z_rnn, 0, d), (z_attn, 0, d), (ag_ml, d, d), (ag_ml, 2 * d, d), (x, 0, d), (target, 0, d)]
    ns = len(streams)

    def body(*refs):
        hbm = refs[:ns]
        br_ref, ba_ref, wr_ref, wa_ref, wo_ref, g_ref, brr_ref, bra_ref, mg_ref, do_ref, dy_ref, st_ref = refs[ns:ns + 12]
        rings, sems = refs[ns + 12:2 * ns + 12], refs[2 * ns + 12]
        slot = _ring_step(streams, hbm, rings, sems, pl.program_id(0), n_steps, tm)
        zr_ref, za_ref, lr_ref, la_ref, x_ref, t_ref = (ring.at[slot] for ring in rings)

        @pl.when(pl.program_id(0) == 0)
        def _():
            st_ref[...] = jnp.zeros_like(st_ref)

        br_rnn = jnp.dot(zr_ref[...], wr_ref[...], preferred_element_type=F32)
        br_attn = jnp.dot(za_ref[...], wa_ref[...], preferred_element_type=F32)
        brr_ref[...] = br_rnn.astype(BF16)
        bra_ref[...] = br_attn.astype(BF16)
        g_rnn = _sigmoid(lr_ref[...].astype(F32) + br_ref[...])
        g_attn = _sigmoid(la_ref[...].astype(F32) + ba_ref[...])
        merged = (g_rnn * br_rnn + g_attn * br_attn).astype(BF16)
        mg_ref[...] = merged
        o = jnp.dot(merged, wo_ref[...], preferred_element_type=F32)
        g = g_ref[...]
        r = lax.rsqrt(jnp.mean(o * o, axis=-1, keepdims=True) + EPS)
        nrm = o * r
        err = x_ref[...] + nrm * g - t_ref[...]
        dy = err * (1.0 / d)
        dy_ref[...] = dy
        dn = dy * g
        do_ref[...] = (r * (dn - nrm * jnp.mean(dn * nrm, axis=-1, keepdims=True))).astype(BF16)
        st_ref[...] += _rows8([jnp.sum(dy * nrm, axis=0, keepdims=True), jnp.sum(err * err, axis=0, keepdims=True)], d)

    tile = pl.BlockSpec((tm, d), lambda i: (i, 0))
    weight = pl.BlockSpec((d, d), lambda i: (0, 0))
    bf = jax.ShapeDtypeStruct((t, d), BF16)
    return pl.pallas_call(
        body, name=name, grid=(n_steps,),
        in_specs=[ANY] * ns + [pl.BlockSpec((1, d), lambda i: (0, 0)), pl.BlockSpec((1, d), lambda i: (0, 1)),
                               weight, weight, weight, pl.BlockSpec((1, d), lambda i: (0, 0))],
        out_specs=[tile, tile, tile, tile, tile, pl.BlockSpec((8, d), lambda i: (0, 0))],
        out_shape=[bf, bf, bf, bf, jax.ShapeDtypeStruct((t, d), F32), jax.ShapeDtypeStruct((8, d), F32)],
        scratch_shapes=_ring_specs(streams, tm),
        compiler_params=_params("arbitrary"))(*[a for a, _, _ in streams], b_gate, b_gate, w_rnn, w_attn, w_out, g_post)


def _branches_bwd(dout, br_rnn, br_attn, ag_ml, b_gate, w_rnn, w_attn, w_out, name, tm=512):
    t, d = br_rnn.shape
    tm = min(tm, t)
    n_steps = t // tm
    streams = [(dout, 0, d), (br_rnn, 0, d), (br_attn, 0, d), (ag_ml, d, d), (ag_ml, 2 * d, d)]
    ns = len(streams)

    def body(*refs):
        hbm = refs[:ns]
        br_ref, ba_ref, wr_ref, wa_ref, wo_ref, dr_ref, da_ref, dl_ref, dzr_ref, dza_ref, st_ref, wt_ref = refs[ns:ns + 12]
        rings, sems = refs[ns + 12:2 * ns + 12], refs[2 * ns + 12]
        slot = _ring_step(streams, hbm, rings, sems, pl.program_id(0), n_steps, tm)
        do_ref, r_ref, a_ref, lr_ref, la_ref = (ring.at[slot] for ring in rings)

        @pl.when(pl.program_id(0) == 0)
        def _():
            st_ref[...] = jnp.zeros_like(st_ref)
            wt_ref[0] = wo_ref[...].T
            wt_ref[1] = wr_ref[...].T
            wt_ref[2] = wa_ref[...].T

        dm = jnp.dot(do_ref[...], wt_ref[0], preferred_element_type=F32)
        g_rnn = _sigmoid(lr_ref[...].astype(F32) + br_ref[...])
        g_attn = _sigmoid(la_ref[...].astype(F32) + ba_ref[...])
        dbr_rnn = (dm * g_rnn).astype(BF16)
        dbr_attn = (dm * g_attn).astype(BF16)
        dr_ref[...] = dbr_rnn
        da_ref[...] = dbr_attn
        dl_rnn = dm * r_ref[...].astype(F32) * g_rnn * (1.0 - g_rnn)
        dl_attn = dm * a_ref[...].astype(F32) * g_attn * (1.0 - g_attn)
        dl_ref[:, 0:d] = dl_rnn.astype(BF16)
        dl_ref[:, d:2 * d] = dl_attn.astype(BF16)
        st_ref[...] += _rows8([jnp.sum(dl_rnn, axis=0, keepdims=True), jnp.sum(dl_attn, axis=0, keepdims=True)], d)
        dzr_ref[...] = jnp.dot(dbr_rnn, wt_ref[1], preferred_element_type=F32).astype(BF16)
        dza_ref[...] = jnp.dot(dbr_attn, wt_ref[2], preferred_element_type=F32).astype(BF16)

    tile = pl.BlockSpec((tm, d), lambda i: (i, 0))
    weight = pl.BlockSpec((d, d), lambda i: (0, 0))
    bf = jax.ShapeDtypeStruct((t, d), BF16)
    return pl.pallas_call(
        body, name=name, grid=(n_steps,),
        in_specs=[ANY] * ns + [pl.BlockSpec((1, d), lambda i: (0, 0)), pl.BlockSpec((1, d), lambda i: (0, 1)), weight, weight, weight],
        out_specs=[tile, tile, pl.BlockSpec((tm, 2 * d), lambda i: (i, 0)), tile, tile, pl.BlockSpec((8, d), lambda i: (0, 0))],
        out_shape=[bf, bf, jax.ShapeDtypeStruct((t, 2 * d), BF16), bf, bf, jax.ShapeDtypeStruct((8, d), F32)],
        scratch_shapes=[pltpu.VMEM((3, d, d), BF16)] + _ring_specs(streams, tm),
        compiler_params=_params("arbitrary"))(*[a for a, _, _ in streams], b_gate, b_gate, w_rnn, w_attn, w_out)


def _lru_decay(r, sp):
    log_a = (-LRU_C) * r * sp
    return log_a, jnp.exp(log_a)


def _lru_gates(c, wa, ba, wx, bx, sp):
    cb = c.astype(BF16)
    r = _sigmoid(jnp.dot(cb, wa, preferred_element_type=F32) + ba)
    ig = _sigmoid(jnp.dot(cb, wx, preferred_element_type=F32) + bx)
    log_a, a = _lru_decay(r, sp)
    mult = jnp.sqrt(-jnp.tanh(log_a) * (a * a + 1.0))
    return cb, r, ig, a, mult


SUBLANES = 8


def _scan_fwd(a, u, carry, tt):
    w = a.shape[1]
    ng = tt // SUBLANES
    a3 = a.reshape(ng, SUBLANES, w)
    u3 = u.reshape(ng, SUBLANES, w)
    sub = lax.broadcasted_iota(jnp.int32, (ng, SUBLANES, w), 1)
    d = 1
    while d < SUBLANES:
        keep = sub >= d
        u3 = u3 + a3 * jnp.where(keep, pltpu.roll(u3, d, 1), 0.0)
        a3 = a3 * jnp.where(keep, pltpu.roll(a3, d, 1), 1.0)
        d *= 2
    out = []
    for g in range(ng):
        hg = u3[g] + a3[g] * carry
        out.append(hg)
        carry = hg[SUBLANES - 1:SUBLANES, :]
    return jnp.concatenate(out, axis=0)


def _scan_rev(b, g, carry, tt):
    w = b.shape[1]
    ng = tt // SUBLANES
    b3 = b.reshape(ng, SUBLANES, w)
    g3 = g.reshape(ng, SUBLANES, w)
    sub = lax.broadcasted_iota(jnp.int32, (ng, SUBLANES, w), 1)
    d = 1
    while d < SUBLANES:
        keep = sub < SUBLANES - d
        g3 = g3 + b3 * jnp.where(keep, pltpu.roll(g3, SUBLANES - d, 1), 0.0)
        b3 = b3 * jnp.where(keep, pltpu.roll(b3, SUBLANES - d, 1), 1.0)
        d *= 2
    out = [None] * ng
    for k in range(ng - 1, -1, -1):
        hk = g3[k] + b3[k] * carry
        out[k] = hk
        carry = hk[0:1, :]
    return jnp.concatenate(out, axis=0)


def _conv_taps(cw, bias, x, ext_ref, tt):
    x2 = ext_ref[7:7 + tt, :]
    x1 = ext_ref[6:6 + tt, :]
    x0 = ext_ref[5:5 + tt, :]
    return bias + cw[3:4] * x + cw[2:3] * x2 + cw[1:2] * x1 + cw[0:1] * x0


def _rnn_fwd(rx_rg, cw, cb, wa, ba, wx, bx, lam, name, tt=512):
    t = rx_rg.shape[0]
    tt = min(tt, t)
    w = GROUP_W

    nt = t // tt
    n_steps = N_GROUPS * nt

    def body(in_hbm, cw_ref, cb_ref, wa_ref, ba_ref, wx_ref, bx_ref, lam_ref,
             y_ref, z_ref, c_ref, r_ref, ig_ref, mult_ref, ext_ref, hc_ref, x_ring, g_ring, ring_sems):
        @pl.when(pl.program_id(1) == 0)
        def _():
            ext_ref[0:8, :] = jnp.zeros((8, w), F32)
            hc_ref[...] = jnp.zeros((8, w), F32)

        n = pl.program_id(0) * nt + pl.program_id(1)

        def fetch(step):
            slot = lax.rem(step, 3)
            rows = pl.ds(pl.multiple_of(lax.rem(step, nt) * tt, tt), tt)
            group = lax.div(step, nt)
            return [pltpu.make_async_copy(in_hbm.at[rows, pl.ds(pl.multiple_of((half * N_GROUPS + group) * w, w), w)],
                                          ring.at[slot], ring_sems.at[half, slot])
                    for half, ring in enumerate((x_ring, g_ring))]

        @pl.when(n == 0)
        def _():
            for cp in fetch(0) + fetch(1):
                cp.start()

        @pl.when(n + 2 < n_steps)
        def _():
            for cp in fetch(n + 2):
                cp.start()

        for cp in fetch(n):
            cp.wait()
        slot = lax.rem(n, 3)
        x = x_ring[slot]
        ext_ref[8:8 + tt, :] = x
        c = _conv_taps(cw_ref[...], cb_ref[...], x, ext_ref, tt)
        ext_ref[0:8, :] = x[tt - 8:tt, :]
        sp = _softplus(-lam_ref[...])
        _, r, ig, a, mult = _lru_gates(c, wa_ref[...], ba_ref[...], wx_ref[...], bx_ref[...], sp)
        c_ref[...] = c
        r_ref[...] = r
        ig_ref[...] = ig
        mult_ref[...] = mult
        h = _scan_fwd(a, mult * (ig * c), hc_ref[7:8, :], tt)
        hc_ref[...] = h[tt - 8:tt, :]
        y_ref[...] = h
        rg = g_ring[slot]
        z_ref[...] = (h * rg * _sigmoid(rg)).astype(BF16)

    vec = pl.BlockSpec((1, w), lambda g, i: (0, g))
    mat = pl.BlockSpec((None, w, w), lambda g, i: (g, 0, 0))
    tile = pl.BlockSpec((tt, w), lambda g, i: (i, g))
    return pl.pallas_call(
        body, name=name, grid=(N_GROUPS, nt),
        in_specs=[ANY, pl.BlockSpec((4, w), lambda g, i: (0, g)), vec, mat, vec, mat, vec, vec],
        out_specs=[tile, tile] + [pl.BlockSpec((None, tt, w), lambda g, i: (g, i, 0))] * 4,
        out_shape=[jax.ShapeDtypeStruct((t, D_RNN), F32), jax.ShapeDtypeStruct((t, D_RNN), BF16)]
        + [jax.ShapeDtypeStruct((N_GROUPS, t, w), F32)] * 4,
        scratch_shapes=[pltpu.VMEM((tt + 8, w), F32), pltpu.VMEM((8, w), F32),
                        pltpu.VMEM((3, tt, w), F32), pltpu.VMEM((3, tt, w), F32), pltpu.SemaphoreType.DMA((2, 3))],
        compiler_params=_params("arbitrary", "arbitrary"))(rx_rg, cw, cb, wa, ba, wx, bx, lam)


def _rnn_bwd(rx_rg, y, dz, c, r, ig, mult, cw, wa, wx, lam, name, tt=512):
    t = rx_rg.shape[0]
    tt = min(tt, t)
    nt = t // tt
    w = GROUP_W

    def body(rx_ref, rg_ref, y_ref, yt_ref, dz_ref, c_ref, r_ref, ig_ref, mult_ref, cw_ref, wa_ref, wx_ref, lam_ref,
             drx_ref, drg_ref, st_ref, gda_ref, gdx_ref, dcx_ref, wcar_ref, acar_ref, dwa_ref, dwx_ref):
        ii = pl.program_id(1)

        @pl.when(ii == 0)
        def _():
            wcar_ref[...] = jnp.zeros((8, w), F32)
            acar_ref[...] = jnp.zeros((8, w), F32)
            dcx_ref[tt:tt + 8, :] = jnp.zeros((8, w), F32)
            st_ref[...] = jnp.zeros_like(st_ref)
            dwa_ref[...] = jnp.zeros_like(dwa_ref)
            dwx_ref[...] = jnp.zeros_like(dwx_ref)

        has_prev = jnp.where(ii == nt - 1, 0.0, 1.0)
        cwv = cw_ref[...]
        lam = lam_ref[...]
        sp = _softplus(-lam)
        wa = wa_ref[...]
        wx = wx_ref[...]
        c = c_ref[...]
        r = r_ref[...]
        ig = ig_ref[...]
        mult = mult_ref[...]
        _, a = _lru_decay(r, sp)
        cb16 = c.astype(BF16)

        rg = rg_ref[...]
        sg = _sigmoid(rg)
        dz = dz_ref[...].astype(F32)
        yv = y_ref[...]
        drg_ref[...] = (dz * yv * (sg * (1.0 + rg * (1.0 - sg)))).astype(BF16)

        row = lax.broadcasted_iota(jnp.int32, (tt, w), 0)
        b = jnp.where(row < tt - 1, pltpu.roll(a, tt - 1, 0), acar_ref[0:1, :])
        dh = _scan_rev(b, dz * (rg * sg), wcar_ref[0:1, :], tt)
        wcar_ref[...] = dh[0:8, :]
        acar_ref[...] = a[0:8, :]

        hprev = jnp.where(row >= 1, pltpu.roll(yv, 1, 0), yt_ref[7:8, :] * has_prev)
        dmult = dh * (ig * c)
        dig = dh * mult * c
        dlog_a = dh * hprev * a - dmult * (a * a / mult)
        dpa = dlog_a * ((-LRU_C) * sp) * r * (1.0 - r)
        dpx = dig * ig * (1.0 - ig)
        dsp = jnp.sum(dlog_a * r, axis=0, keepdims=True) * (-LRU_C)
        dlam = dsp * (-_sigmoid(-lam))
        dpa16 = dpa.astype(BF16)
        dpx16 = dpx.astype(BF16)
        dwa_ref[...] += lax.dot_general(cb16, dpa16, TN_DIMS, preferred_element_type=F32)
        dwx_ref[...] += lax.dot_general(cb16, dpx16, TN_DIMS, preferred_element_type=F32)
        dc = (dh * mult * ig
              + lax.dot_general(dpa16, wa, NT_DIMS, preferred_element_type=F32)
              + lax.dot_general(dpx16, wx, NT_DIMS, preferred_element_type=F32))

        dcx_ref[0:tt, :] = dc
        dc1 = dcx_ref[1:1 + tt, :]
        dc2 = dcx_ref[2:2 + tt, :]
        dc3 = dcx_ref[3:3 + tt, :]
        drx_ref[...] = (cwv[3:4] * dc + cwv[2:3] * dc1 + cwv[1:2] * dc2 + cwv[0:1] * dc3).astype(BF16)
        dcx_ref[tt:tt + 8, :] = dc[0:8, :]

        def colsum(v):
            return jnp.sum(v, axis=0, keepdims=True)

        x = rx_ref[...]
        st_ref[...] += _rows8([colsum(dc), colsum(dpa), colsum(dpx), dlam,
                               colsum(dc3 * x), colsum(dc2 * x), colsum(dc1 * x), colsum(dc * x)], w)

        @pl.when(ii == nt - 1)
        def _():
            for blk in range(GROUP_W // RNN_BLOCK_W):
                rows = slice(blk * RNN_BLOCK_W, (blk + 1) * RNN_BLOCK_W)
                gda_ref[blk] = dwa_ref[rows, rows]
                gdx_ref[blk] = dwx_ref[rows, rows]

    def rev(ii):
        return nt - 1 - ii

    def tail(g, ii):
        return (jnp.maximum(rev(ii) * (tt // 8) - 1, 0), g)

    vec = pl.BlockSpec((1, w), lambda g, ii: (0, g))
    mat = pl.BlockSpec((None, w, w), lambda g, ii: (g, 0, 0))
    tile = pl.BlockSpec((tt, w), lambda g, ii: (rev(ii), g))
    diag_shape = (N_GROUPS, GROUP_W // RNN_BLOCK_W, RNN_BLOCK_W, RNN_BLOCK_W)
    diag = pl.BlockSpec((None,) + diag_shape[1:], lambda g, ii: (g, 0, 0, 0))
    kept = pl.BlockSpec((None, tt, w), lambda g, ii: (g, rev(ii), 0))
    return pl.pallas_call(
        body, name=name, grid=(N_GROUPS, nt),
        in_specs=[tile, pl.BlockSpec((tt, w), lambda g, ii: (rev(ii), N_GROUPS + g)),
                  tile, pl.BlockSpec((8, w), tail), tile, kept, kept, kept, kept,
                  pl.BlockSpec((4, w), lambda g, ii: (0, g)), mat, mat, vec],
        out_specs=[tile, tile, pl.BlockSpec((8, w), lambda g, ii: (0, g)), diag, diag],
        out_shape=[jax.ShapeDtypeStruct((t, D_RNN), BF16), jax.ShapeDtypeStruct((t, D_RNN), BF16),
                   jax.ShapeDtypeStruct((8, D_RNN), F32),
                   jax.ShapeDtypeStruct(diag_shape, F32), jax.ShapeDtypeStruct(diag_shape, F32)],
        scratch_shapes=[pltpu.VMEM((tt + 8, w), F32), pltpu.VMEM((8, w), F32),
                        pltpu.VMEM((8, w), F32), pltpu.VMEM((w, w), F32), pltpu.VMEM((w, w), F32)],
        compiler_params=_params("parallel", "arbitrary"))(rx_rg, rx_rg, y, y, dz, c, r, ig, mult, cw, wa, wx, lam)


def _half_mask(shape, half):
    lane = lax.broadcasted_iota(jnp.int32, shape, 1)
    return (lane >= HEAD_DIM) if half else (lane < HEAD_DIM)


def _dup_half(t, half):
    sel = jnp.where(_half_mask(t.shape, half), t, 0.0)
    return sel + pltpu.roll(sel, HEAD_DIM, 1)


def _band_geometry(n):
    qi = lax.broadcasted_iota(jnp.int32, (BLOCK, 2 * BLOCK), 0)
    kj = lax.broadcasted_iota(jnp.int32, (BLOCK, 2 * BLOCK), 1)
    dist = BLOCK + qi - kj
    first_key = jnp.where(n > 0, 0, BLOCK)
    valid = (dist >= 0) & (dist < BLOCK) & (kj >= first_key)
    return dist.astype(F32), valid


GROUP = 4
HALF_ROWS = 64


def _kv_dup(prev_ref, cur_ref, hk, scale=1.0):
    tile = hk // 2
    kt = jnp.concatenate([prev_ref[:, tile * LANE:(tile + 1) * LANE], cur_ref[:, tile * LANE:(tile + 1) * LANE]], axis=0)
    return (_dup_half(kt.astype(F32), hk % 2) * scale).astype(BF16)


def _fill_bias(bias_ref, sm_ref, n):
    distf, valid = _band_geometry(n)
    for head in range(N_Q_HEADS):
        bias_ref[head] = jnp.where(valid, -sm_ref[1, head] * distf, MASKED)


def _head_scores(q2s, half, kdup, bias):
    qm = jnp.where(_half_mask(q2s.shape, half), q2s, jnp.zeros_like(q2s))
    return qm, lax.dot_general(qm, kdup, NT_DIMS, preferred_element_type=F32) + bias


def _attn_specs(nb, clamp_last):
    def blk(n):
        return jnp.minimum(n, nb - 1) if clamp_last else n

    q_spec = pl.BlockSpec((BLOCK, D_MODEL), lambda n: (blk(n), 0))
    k_prev = pl.BlockSpec((BLOCK, D_KV), lambda n: (jnp.maximum(blk(n) - 1, 0), D_MODEL // D_KV))
    k_cur = pl.BlockSpec((BLOCK, D_KV), lambda n: (blk(n), D_MODEL // D_KV))
    v_prev = pl.BlockSpec((BLOCK, D_KV), lambda n: (jnp.maximum(blk(n) - 1, 0), D_MODEL // D_KV + 1))
    v_cur = pl.BlockSpec((BLOCK, D_KV), lambda n: (blk(n), D_MODEL // D_KV + 1))
    return q_spec, k_prev, k_cur, v_prev, v_cur


def _attn_fwd(sm, qkv, ag_ml, name, ride=None):
    t = qkv.shape[0]
    nb = t // BLOCK

    host = _Hosted(ride, 7, 3, 1)

    def body(*refs):
        (sm_ref, q_ref, kp_ref, kc_ref, vp_ref, vc_ref, ag_ref, y_ref, z_ref, lse_ref, bias_ref), start, finish = host.split(refs)
        n = pl.program_id(0)
        start(n == 0)
        start.middle(n == (3 * nb) // 4)

        @pl.when(n <= 1)
        def _():
            _fill_bias(bias_ref, sm_ref, n)

        lane = lax.broadcasted_iota(jnp.int32, (BLOCK, LANE), 1)
        low = lane < HEAD_DIM
        lse = jnp.zeros((BLOCK, LANE), F32)
        for hk in range(N_Q_HEADS // GROUP):
            kdup = _kv_dup(kp_ref, kc_ref, hk)
            vdup = _kv_dup(vp_ref, vc_ref, hk)
            for k in (0, 1):
                c = slice((2 * hk + k) * LANE, (2 * hk + k + 1) * LANE)
                q2s = q_ref[:, c] * ATTN_SCALE
                outs = []
                for half in (0, 1):
                    head = GROUP * hk + 2 * k + half
                    _, s = _head_scores(q2s, half, kdup, bias_ref[head])
                    sink = sm_ref[0, head]
                    m = jnp.maximum(jnp.max(s, axis=1, keepdims=True), sink)
                    e = jnp.exp(s - m)
                    l = jnp.sum(e, axis=1, keepdims=True) + jnp.exp(sink - m)
                    outs.append(jnp.dot((e * (1.0 / l)).astype(BF16), vdup, preferred_element_type=F32))
                    lse = jnp.where(lane == head, m + jnp.log(l), lse)
                yt = jnp.where(low, outs[0], outs[1])
                y_ref[:, c] = yt
                ag = ag_ref[:, c].astype(F32)
                z_ref[:, c] = (yt * ag * _sigmoid(ag)).astype(BF16)
        lse_ref[...] = lse
        finish(n == nb - 1)

    q_spec, k_prev, k_cur, v_prev, v_cur = _attn_specs(nb, False)
    wide = pl.BlockSpec((BLOCK, D_MODEL), lambda n: (n, 0))
    outs = pl.pallas_call(
        body, name=name, grid=(nb,),
        in_specs=[pl.BlockSpec(memory_space=pltpu.SMEM), q_spec, k_prev, k_cur, v_prev, v_cur, wide] + host.in_specs,
        out_specs=[wide, wide, pl.BlockSpec((BLOCK, LANE), lambda n: (n, 0))] + host.out_specs,
        out_shape=[jax.ShapeDtypeStruct((t, D_MODEL), F32), jax.ShapeDtypeStruct((t, D_MODEL), BF16),
                   jax.ShapeDtypeStruct((t, LANE), F32)] + host.out_shapes,
        scratch_shapes=[pltpu.VMEM((N_Q_HEADS, BLOCK, 2 * BLOCK), F32)] + host.scratch_shapes,
        compiler_params=_params("arbitrary"))(sm, qkv, qkv, qkv, qkv, qkv, ag_ml, *host.arrays)
    res, landed = host.results(outs, 3)
    return (*res, landed) if ride else tuple(res)


def _attn_bwd(sm, qkv, ag_ml, y, lse, dz, name, ride=None):
    t = qkv.shape[0]
    nb = t // BLOCK
    host = _Hosted(ride, 10, 4, 3)

    def body(*refs):
        host_refs, start, finish = host.split(refs)
        (sm_ref, q_ref, kp_ref, kc_ref, vp_ref, vc_ref, ag_ref, y_ref, lse_ref, dz_ref,
         dq_ref, dkv_ref, dag_ref, ds_ref, ck_ref, cv_ref, bias_ref) = host_refs
        n = pl.program_id(0)
        start(n == 0)
        start.middle(n == (3 * nb) // 4)

        @pl.when(n == 0)
        def _():
            ck_ref[...] = jnp.zeros_like(ck_ref)
            cv_ref[...] = jnp.zeros_like(cv_ref)
            ds_ref[...] = jnp.zeros_like(ds_ref)

        @pl.when(n <= 1)
        def _():
            _fill_bias(bias_ref, sm_ref, n)

        @pl.when(n < nb)
        def _():
            lane8 = lax.broadcasted_iota(jnp.int32, (8, LANE), 1)
            row8 = lax.broadcasted_iota(jnp.int32, (8, LANE), 0)
            dsink = jnp.zeros((8, LANE), F32)
            dk_heads, dv_heads = [], []
            lse_tile = lse_ref[...]
            for hk in range(N_Q_HEADS // GROUP):
                kdup = _kv_dup(kp_ref, kc_ref, hk)
                ks = _kv_dup(kp_ref, kc_ref, hk, ATTN_SCALE)
                vdup = _kv_dup(vp_ref, vc_ref, hk)
                qms, dyhs, y_rows = [], [], []
                for k in (0, 1):
                    c = slice((2 * hk + k) * LANE, (2 * hk + k + 1) * LANE)
                    ag = ag_ref[:, c].astype(F32)
                    sg = _sigmoid(ag)
                    dzt = dz_ref[:, c].astype(F32)
                    yt = y_ref[:, c]
                    dag_ref[:, c] = (dzt * yt * (sg * (1.0 + ag * (1.0 - sg)))).astype(BF16)
                    dyt = dzt * (ag * sg)
                    q2s = q_ref[:, c] * ATTN_SCALE
                    for half in (0, 1):
                        hm = _half_mask(q2s.shape, half)
                        qms.append(jnp.where(hm, q2s, jnp.zeros_like(q2s)))
                        dyhs.append(jnp.where(hm, dyt, 0.0))
                        y_rows.append(yt)
                low = _half_mask((BLOCK, LANE), 0)
                dk_acc = dv_acc = None
                for k in (0, 1):
                    c = slice((2 * hk + k) * LANE, (2 * hk + k + 1) * LANE)
                    qm2 = jnp.concatenate(qms[2 * k:2 * k + 2], axis=0)
                    dy2_16 = jnp.concatenate(dyhs[2 * k:2 * k + 2], axis=0).astype(BF16)
                    s2 = lax.dot_general(qm2, kdup, NT_DIMS, preferred_element_type=F32)
                    dp2 = lax.dot_general(dy2_16, vdup, NT_DIMS, preferred_element_type=F32)
                    probs16, ds16 = [], []
                    for half in (0, 1):
                        r = 2 * k + half
                        head = GROUP * hk + r
                        lh = lse_tile[:, head:head + 1]
                        psink = jnp.exp(sm_ref[0, head] - lh)
                        delta = jnp.sum(dyhs[r] * y_rows[r], axis=1, keepdims=True)
                        for sub in range(BLOCK // HALF_ROWS):
                            sl = slice(sub * HALF_ROWS, (sub + 1) * HALF_ROWS)
                            rows = slice(half * BLOCK + sub * HALF_ROWS, half * BLOCK + (sub + 1) * HALF_ROWS)
                            probs = jnp.exp(s2[rows] + bias_ref[head, sl, :] - lh[sl])
                            ds16.append((probs * (dp2[rows] - delta[sl])).astype(BF16))
                            probs16.append(probs.astype(BF16))
                        dsink = dsink + jnp.where((row8 == 0) & (lane8 == head),
                                                  -jnp.sum(psink * delta, axis=0, keepdims=True), 0.0)
                    ds2 = jnp.concatenate(ds16, axis=0)
                    p2 = jnp.concatenate(probs16, axis=0)
                    dq2 = jnp.dot(ds2, ks, preferred_element_type=F32)
                    dq_ref[:, c] = jnp.where(low, dq2[0:BLOCK], dq2[BLOCK:2 * BLOCK]).astype(BF16)
                    dk_part = lax.dot_general(ds2, qm2, TN_DIMS, preferred_element_type=F32)
                    dv_part = lax.dot_general(p2, dy2_16, TN_DIMS, preferred_element_type=F32)
                    dk_acc = dk_part if dk_acc is None else dk_acc + dk_part
                    dv_acc = dv_part if dv_acc is None else dv_acc + dv_part
                dk_heads.append(dk_acc + pltpu.roll(dk_acc, HEAD_DIM, 1))
                dv_heads.append(dv_acc + pltpu.roll(dv_acc, HEAD_DIM, 1))
            ds_ref[...] += dsink
            low = _half_mask((2 * BLOCK, LANE), 0)
            for tile in range(2):
                cols = slice(tile * LANE, (tile + 1) * LANE)
                dkt = jnp.where(low, dk_heads[2 * tile], dk_heads[2 * tile + 1])
                dvt = jnp.where(low, dv_heads[2 * tile], dv_heads[2 * tile + 1])
                dkv_ref[:, cols] = (ck_ref[:, cols] + dkt[0:BLOCK, :]).astype(BF16)
                dkv_ref[:, D_KV + tile * LANE:D_KV + (tile + 1) * LANE] = (cv_ref[:, cols] + dvt[0:BLOCK, :]).astype(BF16)
                ck_ref[:, cols] = dkt[BLOCK:2 * BLOCK, :]
                cv_ref[:, cols] = dvt[BLOCK:2 * BLOCK, :]

        @pl.when(n == nb)
        def _():
            dkv_ref[:, 0:D_KV] = ck_ref[...].astype(BF16)
            dkv_ref[:, D_KV:2 * D_KV] = cv_ref[...].astype(BF16)

        finish(n == nb)

    q_spec, k_prev, k_cur, v_prev, v_cur = _attn_specs(nb, True)
    wide = pl.BlockSpec((BLOCK, D_MODEL), lambda n: (jnp.minimum(n, nb - 1), 0))
    outs = pl.pallas_call(
        body, name=name, grid=(nb + 1,),
        in_specs=[pl.BlockSpec(memory_space=pltpu.SMEM), q_spec, k_prev, k_cur, v_prev, v_cur, wide, wide,
                  pl.BlockSpec((BLOCK, LANE), lambda n: (jnp.minimum(n, nb - 1), 0)), wide] + host.in_specs,
        out_specs=[wide, pl.BlockSpec((BLOCK, 2 * D_KV), lambda n: (jnp.maximum(n - 1, 0), 0)), wide,
                   pl.BlockSpec((8, LANE), lambda n: (0, 0))] + host.out_specs,
        out_shape=[jax.ShapeDtypeStruct((t, D_MODEL), BF16), jax.ShapeDtypeStruct((t, 2 * D_KV), BF16),
                   jax.ShapeDtypeStruct((t, D_MODEL), BF16), jax.ShapeDtypeStruct((8, LANE), F32)] + host.out_shapes,
        scratch_shapes=[pltpu.VMEM((BLOCK, D_KV), F32), pltpu.VMEM((BLOCK, D_KV), F32),
                        pltpu.VMEM((N_Q_HEADS, BLOCK, 2 * BLOCK), F32)] + host.scratch_shapes,
        compiler_params=_params("arbitrary"))(sm, qkv, qkv, qkv, qkv, qkv, ag_ml, y, lse, dz, *host.arrays)
    res, landed = host.results(outs, 4)
    return (*res, landed) if ride else tuple(res)


def _local_grads(x, target, p, project, late_weights, reduce_out, reduce_in):
    h, rx_rg, qkv, ag_ml, wt = project(x, p["pre_g"])
    y_attn, z_attn, lse, landed = _attn_fwd(p["sm"], qkv, ag_ml, "attn_fwd", ride=late_weights[0])
    p = {**p, **late_weights[1](landed)}
    y_rnn, z_rnn, *kept_rnn = _rnn_fwd(
        rx_rg, p["cw"], p["cb"], p["wbd_a"], p["b_a"], p["wbd_x"], p["b_x"], p["lam"], "rnn_fwd")
    br_rnn, br_attn, merged, dout, dy, st_post = _branches_fwd(
        z_rnn, z_attn, ag_ml, p["b_gate"], p["w_rnn"], p["w_attn"], p["w_out"], x, target, p["post_g"], "branches_fwd")

    dbr_rnn, dbr_attn, d_ml, dz_rnn, dz_attn, st_merge = _branches_bwd(
        dout, br_rnn, br_attn, ag_ml, p["b_gate"], p["w_rnn"], p["w_attn"], p["w_out"], "branches_bwd")
    out_pairs = _mm_tn_pairs([(z_rnn, dbr_rnn), (z_attn, dbr_attn), (merged, dout)], "gw_outs")
    d_rx, d_rg, st_rnn, g_rg_a, g_rg_x = _rnn_bwd(
        rx_rg, y_rnn, dz_rnn, *kept_rnn, p["cw"], p["wbd_a"], p["wbd_x"], p["lam"], "rnn_bwd")
    dq, dkv, d_ag, st_sink, red_out = _attn_bwd(p["sm"], qkv, ag_ml, y_attn, lse, dz_attn, "attn_bwd",
                                                ride=reduce_out(out_pairs, g_rg_a, g_rg_x))

    segs = [d_rx, d_rg, dq, dkv, d_ag, d_ml]
    grad_x, st_pre, red_in = _input_grad(segs, wt, x, p["pre_g"], dy, "input_grad",
                                         ride=reduce_in(*_mm_tn_seg_pair(segs, h, "gw_in")))
    return dict(grad_x=grad_x, st_post=st_post, st_merge=st_merge, st_rnn=st_rnn, st_sink=st_sink, st_pre=st_pre,
                red_out=red_out, red_in=red_in)


def _place():
    x, y, c = lax.axis_index("x"), lax.axis_index("y"), lax.axis_index("c")
    return x, y, c


def _gather_ride(shards):
    n = len(shards)

    def copies(ins, outs, sems):
        send_sems, recv_sems, local_sems = sems
        x, y, c = _place()
        me, sibling = (x, y, c), (x, y, 1 - c)
        chips = [(1 - x, y), (x, 1 - y), (1 - x, 1 - y)]

        def slot(a, dev):
            return outs[a].at[4 * dev[0] + 2 * dev[1] + dev[2]]

        def copy(a, k, block, to, src=None):
            return pltpu.make_async_remote_copy(
                src_ref=slot(a, block) if src is None else src, dst_ref=slot(a, block),
                send_sem=send_sems.at[a, k], recv_sem=recv_sems.at[a, k], device_id=to, device_id_type=MESH)

        mine = [pltpu.make_async_copy(ins[a], slot(a, me), local_sems.at[a]) for a in range(n)]
        first = []
        for a in range(n):
            first.append(copy(a, 0, me, sibling, src=ins[a]))
            first += [copy(a, 1 + j, me, (*chip, c), src=ins[a]) for j, chip in enumerate(chips)]
        return me, sibling, chips, c, copy, mine, first

    def start(ins, outs, sems):
        *_, mine, first = copies(ins, outs, sems)
        for cp in mine + first:
            cp.start()

    def middle(ins, outs, sems):
        me, sibling, chips, c, copy, _, _ = copies(ins, outs, sems)
        for j, chip in enumerate(chips):
            for a in range(n):
                copy(a, 1 + j, (*chip, c), me).wait_recv()
                copy(a, 4 + j, (*chip, c), sibling).start()

    def finish(ins, outs, sems):
        me, sibling, chips, c, copy, mine, first = copies(ins, outs, sems)
        passed = [copy(a, 4 + j, (*chip, c), sibling) for j, chip in enumerate(chips) for a in range(n)]
        for a in range(n):
            copy(a, 0, sibling, me).wait_recv()
            for j, chip in enumerate(chips):
                copy(a, 4 + j, (*chip, 1 - c), me).wait_recv()
        for cp in first + passed:
            cp.wait_send()
        for cp in mine:
            cp.wait()

    return _Ride(
        shards, [jax.ShapeDtypeStruct((N_DEV, *s.shape), s.dtype) for s in shards],
        [pltpu.SemaphoreType.DMA((n, 7)), pltpu.SemaphoreType.DMA((n, 7)), pltpu.SemaphoreType.DMA((n,))],
        start, finish, middle)


def _chips_ride(scatter):
    n = len(scatter)

    def copies(ins, outs, sems):
        send_sems, recv_sems, local_sems = sems
        x, y, c = _place()
        own = 2 * x + y
        chips = [(1 - x, y), (x, 1 - y), (1 - x, 1 - y)]
        local = [pltpu.make_async_copy(ins[a].at[own], outs[a].at[own], local_sems.at[a]) for a in range(n)]
        sent = [pltpu.make_async_remote_copy(
            src_ref=ins[a].at[2 * chip[0] + chip[1]], dst_ref=outs[a].at[own],
            send_sem=send_sems.at[a, j], recv_sem=recv_sems.at[a, own], device_id=(*chip, c), device_id_type=MESH)
            for a in range(n) for j, chip in enumerate(chips)]
        return chips, c, local, sent

    def start(ins, outs, sems):
        _, _, local, sent = copies(ins, outs, sems)
        for cp in local + sent:
            cp.start()

    def finish(ins, outs, sems):
        send_sems, recv_sems, _ = sems
        chips, c, local, sent = copies(ins, outs, sems)
        for a in range(n):
            for chip in chips:
                k = 2 * chip[0] + chip[1]
                pltpu.make_async_remote_copy(
                    src_ref=outs[a].at[k], dst_ref=outs[a].at[k], send_sem=send_sems.at[a, 0],
                    recv_sem=recv_sems.at[a, k], device_id=(*chip, c), device_id_type=MESH).wait_recv()
        for cp in sent:
            cp.wait_send()
        for cp in local:
            cp.wait()

    return _Ride(
        list(scatter), [jax.ShapeDtypeStruct(s.shape, s.dtype) for s in scatter],
        [pltpu.SemaphoreType.DMA((n, 3)), pltpu.SemaphoreType.DMA((n, N_CHIP)), pltpu.SemaphoreType.DMA((n,))],
        start, finish)


def _chips_rest_ride(pair, red):
    table = _side_pieces()
    nc = len(CHUNK_ORDER)

    def each(ins, outs, sems, sending, landing):
        send_sems, recv_sems, _ = sems
        pair_ref, red_ref = ins[0], outs[0]
        x, y, c = _place()
        own = 2 * x + y
        for core in (0, 1):
            mine = table[core][0]
            rest = [s for s in range(EARLY_STEPS, nc) if mine[s] is not None]

            @pl.when(c == core)
            def _(mine=mine, rest=rest, core=core):
                for s in rest:
                    _, rows, at = mine[s]
                    k, cp = _to_chip(pair_ref.at[at // SHARD_IN, pl.ds(at % SHARD_IN, rows)], red_ref, mine[s], s,
                                     send_sems, recv_sems, own, core)
                    pl.when(own != k)(lambda cp=cp: sending(cp))
                if landing is not None:
                    for k in range(N_CHIP):
                        @pl.when(own == k)
                        def _(k=k):
                            for s in rest:
                                _, rows, at = mine[s]
                                if at // SHARD_IN == k:
                                    for chip in range(N_CHIP):
                                        if chip != k:
                                            landing(_from_chip(pair_ref.at[k, pl.ds(at % SHARD_IN, rows)], red_ref, mine[s], s,
                                                               send_sems, recv_sems, chip, (x, y, c)))

    def local(ins, outs, sems):
        x, y, _ = _place()
        return pltpu.make_async_copy(ins[0].at[2 * x + y], outs[0].at[2 * x + y], sems[2])

    def start(ins, outs, sems):
        local(ins, outs, sems).start()
        each(ins, outs, sems, lambda cp: cp.start(), None)

    def finish(ins, outs, sems):
        each(ins, outs, sems, lambda cp: cp.wait_send(), lambda cp: cp.wait_recv())
        local(ins, outs, sems).wait()

    return _Ride([pair, red], [jax.ShapeDtypeStruct(red.shape, red.dtype)],
                 [pltpu.SemaphoreType.DMA((nc,)), pltpu.SemaphoreType.DMA((N_CHIP, nc)), pltpu.SemaphoreType.DMA],
                 start, finish, aliases={1: 0})


def _allreduce_small(pack, name):
    shape = pack.shape

    def body(x_ref, o_ref, sib_ref, chip_ref, send_sems, recv_sems):
        x, y, c = _place()
        own = 2 * x + y
        chips = [(1 - x, y), (x, 1 - y), (1 - x, 1 - y)]
        to_sibling = pltpu.make_async_remote_copy(
            src_ref=x_ref, dst_ref=sib_ref, send_sem=send_sems.at[0], recv_sem=recv_sems.at[0],
            device_id=(x, y, 1 - c), device_id_type=MESH)
        to_sibling.start()
        to_sibling.wait()
        chip_ref[own] = x_ref[...] + sib_ref[...]
        sent = [pltpu.make_async_remote_copy(
            src_ref=chip_ref.at[own], dst_ref=chip_ref.at[own], send_sem=send_sems.at[1 + j],
            recv_sem=recv_sems.at[1 + own], device_id=(*chip, c), device_id_type=MESH) for j, chip in enumerate(chips)]
        for cp in sent:
            cp.start()
        for chip in chips:
            k = 2 * chip[0] + chip[1]
            pltpu.make_async_remote_copy(
                src_ref=chip_ref.at[k], dst_ref=chip_ref.at[k], send_sem=send_sems.at[1],
                recv_sem=recv_sems.at[1 + k], device_id=(*chip, c), device_id_type=MESH).wait_recv()
        for cp in sent:
            cp.wait_send()
        o_ref[...] = (chip_ref[0] + chip_ref[1]) + (chip_ref[2] + chip_ref[3])

    return pl.pallas_call(
        body, name=name, out_shape=jax.ShapeDtypeStruct(shape, F32),
        in_specs=[pl.BlockSpec(memory_space=pltpu.VMEM)], out_specs=pl.BlockSpec(memory_space=pltpu.VMEM),
        scratch_shapes=[pltpu.VMEM(shape, F32), pltpu.VMEM((N_CHIP, *shape), F32),
                        pltpu.SemaphoreType.DMA((4,)), pltpu.SemaphoreType.DMA((1 + N_CHIP,))],
    )(pack)


def _adamw(g, w, m, v):
    m = ADAM_B1 * m + (1.0 - ADAM_B1) * g
    v = ADAM_B2 * v + (1.0 - ADAM_B2) * (g * g)
    m_hat = m / (1.0 - ADAM_B1 ** ADAM_STEP)
    v_hat = v / (1.0 - ADAM_B2 ** ADAM_STEP)
    delta = -ADAM_LR * (m_hat / (jnp.sqrt(v_hat) + ADAM_EPS) + ADAM_WD * w)
    return delta, m, v


def _adam_parts(items, name, tr=None):
    ni = len(items)
    npart, r, c = items[0][0].shape
    tr = r if tr is None else min(tr, r)

    def body(*refs):
        for a in range(ni):
            p_ref, w_ref, m_ref, v_ref = refs[4 * a:4 * a + 4]
            g_ref, d_ref, nm_ref, nv_ref = refs[4 * (ni + a):4 * (ni + a) + 4]
            g = p_ref[0].astype(F32)
            for k in range(1, npart):
                g = g + p_ref[k].astype(F32)
            g_ref[...] = g
            d_ref[...], nm_ref[...], nv_ref[...] = _adamw(g, w_ref[...], m_ref[...], v_ref[...])

    tile = pl.BlockSpec((tr, c), lambda i: (i, 0))
    outs = pl.pallas_call(
        body, name=name, grid=(r // tr,),
        in_specs=[pl.BlockSpec((npart, tr, c), lambda i: (0, i, 0)), tile, tile, tile] * ni,
        out_specs=[tile] * (4 * ni), out_shape=[jax.ShapeDtypeStruct((r, c), F32)] * (4 * ni),
        compiler_params=_params("parallel"))(*[arr for item in items for arr in item])
    return [outs[4 * a:4 * a + 4] for a in range(ni)]


def _block_diag(w):
    w4 = w.reshape(N_GROUPS, 4, RNN_BLOCK_W, RNN_BLOCK_W)
    eye = jnp.eye(4, dtype=w.dtype)
    return jnp.einsum("gbij,bc->gbicj", w4, eye).reshape(N_GROUPS, GROUP_W, GROUP_W).astype(BF16)


SMALL_ROWS = 16
ROW_PRE_G, ROW_BGATE, ROW_CONV_B, ROW_B_A, ROW_B_X, ROW_LAM, ROW_POST_G, ROW_LOSS, ROW_SINKS, ROW_CONV_W = 0, 1, 3, 4, 5, 6, 7, 8, 9, 10


def _pack_stats(st_pre, st_merge, st_rnn, st_post, st_sink, name):
    d = D_MODEL

    def body(pre_ref, mg_ref, rnn_ref, post_ref, sink_ref, o_ref):
        rnn = rnn_ref[...]
        sinks = jnp.concatenate([sink_ref[0:1, :], jnp.zeros((1, d - LANE), F32)], axis=1)
        o_ref[0:8, :] = _rows8([pre_ref[0:1, :], mg_ref[0:1, :], mg_ref[1:2, :], rnn[0:1], rnn[1:2], rnn[2:3], rnn[3:4],
                                post_ref[0:1, :]], d)
        o_ref[8:16, :] = _rows8([post_ref[1:2, :], sinks, rnn[4:5], rnn[5:6], rnn[6:7], rnn[7:8]], d)

    return pl.pallas_call(body, name=name, out_shape=jax.ShapeDtypeStruct((SMALL_ROWS, d), F32))(
        st_pre, st_merge, st_rnn, st_post, st_sink)


def _adam_small(total, w, m, v, name):
    d = D_MODEL
    n_in = len(w)
    rows = (ROW_PRE_G, ROW_BGATE, ROW_CONV_B, ROW_B_A, ROW_B_X, ROW_LAM, ROW_POST_G, ROW_SINKS)

    def grad_of(p_ref, row, shape):
        if shape == (1, 2 * d):
            return jnp.concatenate([p_ref[row:row + 1, :], p_ref[row + 1:row + 2, :]], axis=1)
        if shape == (1, RNN_BLOCKS, RNN_BLOCK_W):
            r = p_ref[row:row + 1, :]
            return jnp.concatenate([r[:, b * RNN_BLOCK_W:(b + 1) * RNN_BLOCK_W] for b in range(RNN_BLOCKS)], axis=0)[None]
        return p_ref[row:row + 1, 0:shape[1]]

    def body(*refs):
        p_ref = refs[0]
        w_refs, m_refs, v_refs = (refs[1 + k * n_in:1 + (k + 1) * n_in] for k in range(3))
        outs = refs[1 + 3 * n_in:]
        for k in range(n_in):
            g = grad_of(p_ref, rows[k], w[k].shape)
            res = (g,) + _adamw(g, w_refs[k][...], m_refs[k][...], v_refs[k][...])
            for kind in range(4):
                outs[kind * n_in + k][...] = res[kind]

    outs = pl.pallas_call(
        body, name=name, out_shape=[jax.ShapeDtypeStruct(a.shape, F32) for _ in range(4) for a in w])(total, *w, *m, *v)
    return [outs[kind * n_in:(kind + 1) * n_in] for kind in range(4)]


def kernel(x, pre_norm_g, w_in, b_gate, conv_w, conv_b, w_rg_a, b_rg_a, w_rg_x, b_rg_x, lru_lambda, attn_sinks, w_rnn_out, w_attn_out, w_out, post_norm_g, loss_target, m_pre_norm_g, m_w_in, m_b_gate, m_conv_w, m_conv_b, m_w_rg_a, m_b_rg_a, m_w_rg_x, m_b_rg_x, m_lru_lambda, m_attn_sinks, m_w_rnn_out, m_w_attn_out, m_w_out, m_post_norm_g, v_pre_norm_g, v_w_in, v_b_gate, v_conv_w, v_conv_b, v_w_rg_a, v_b_rg_a, v_w_rg_x, v_b_rg_x, v_lru_lambda, v_attn_sinks, v_w_rnn_out, v_w_attn_out, v_w_out, v_post_norm_g):
    cx, cy, cc = _place()
    dev = 4 * cx + 2 * cy + cc

    w_in_t, m_in_t, v_in_t = (jnp.transpose(a[0]) for a in (w_in, m_w_in, v_w_in))
    wt_shard = w_in_t.astype(BF16)

    def project(xs, pre_g):
        h, rx_rg, qkv, ag_ml, wt_all = _gather_project(xs, pre_g, wt_shard, "gather_project")
        return h, rx_rg, qkv, ag_ml, wt_all.reshape(D_IN, D_MODEL)

    def late_unpack(landed):
        w_rnn_all, w_attn_all, w_out_all, cw_all = landed
        return dict(w_rnn=w_rnn_all.reshape(D_RNN, D_MODEL), w_attn=w_attn_all.reshape(D_MODEL, D_MODEL),
                    w_out=w_out_all.reshape(D_MODEL, D_MODEL), cw=jnp.transpose(cw_all, (1, 0, 2)).reshape(4, D_RNN))

    late_weights = (_gather_ride([w_rnn_out[0].astype(BF16), w_attn_out[0].astype(BF16), w_out[0].astype(BF16), conv_w[0]]),
                    late_unpack)

    heads = jnp.arange(1, N_Q_HEADS + 1, dtype=F32)
    slopes = jnp.exp2(-ALIBI_MAX_BIAS * heads / N_Q_HEADS)
    b_a = b_rg_a.reshape(1, D_RNN)
    b_x = b_rg_x.reshape(1, D_RNN)
    p = dict(
        pre_g=pre_norm_g, post_g=post_norm_g, b_gate=b_gate, cb=conv_b,
        wbd_a=_block_diag(w_rg_a[0]), b_a=b_a, wbd_x=_block_diag(w_rg_x[0]), b_x=b_x, lam=lru_lambda,
        sm=jnp.pad(attn_sinks, ((0, 1), (0, 0))) + jnp.pad(slopes[None, :], ((1, 0), (0, 0))))

    flat = (RNN_BLOCKS * RNN_BLOCK_W, RNN_BLOCK_W)

    def reduce_out(out_pairs, g_rg_a, g_rg_x):
        return _join_rides(_chips_ride(out_pairs), _gather_ride([g_rg_a.reshape(flat), g_rg_x.reshape(flat)]))

    reduce_in = _chips_rest_ride

    g = _local_grads(x[0], loss_target[0], p, project, late_weights, reduce_out, reduce_in)
    small = _allreduce_small(
        _pack_stats(g["st_pre"], g["st_merge"], g["st_rnn"], g["st_post"], g["st_sink"], "pack_stats"), "allreduce_small")

    out = {}
    red = g["red_out"]
    w_in_out, = _adam_parts([(g["red_in"][0], w_in_t, m_in_t, v_in_t)], "adam_w_in", tr=SHARD_IN // 2)
    out["w_in"] = [jnp.transpose(o) for o in w_in_out]
    out["w_rnn_out"], out["w_attn_out"], out["w_out"] = _adam_parts(
        [(red[0], w_rnn_out[0], m_w_rnn_out[0], v_w_rnn_out[0]), (red[1], w_attn_out[0], m_w_attn_out[0], v_w_attn_out[0]),
         (red[2], w_out[0], m_w_out[0], v_w_out[0])], "adam_w_outs")
    out["w_rg_a"], out["w_rg_x"] = _adam_parts(
        [(red[3], w_rg_a.reshape(flat), m_w_rg_a.reshape(flat), v_w_rg_a.reshape(flat)),
         (red[4], w_rg_x.reshape(flat), m_w_rg_x.reshape(flat), v_w_rg_x.reshape(flat))], "adam_w_rg")

    small_names = ("pre_norm_g", "b_gate", "conv_b", "b_rg_a", "b_rg_x", "lru_lambda", "post_norm_g", "attn_sinks")
    small_out = _adam_small(
        small,
        (pre_norm_g, b_gate, conv_b, b_rg_a, b_rg_x, lru_lambda, post_norm_g, attn_sinks),
        (m_pre_norm_g, m_b_gate, m_conv_b, m_b_rg_a, m_b_rg_x, m_lru_lambda, m_post_norm_g, m_attn_sinks),
        (v_pre_norm_g, v_b_gate, v_conv_b, v_b_rg_a, v_b_rg_x, v_lru_lambda, v_post_norm_g, v_attn_sinks),
        "adam_small")
    g_cw = lax.dynamic_slice(small[ROW_CONV_W:ROW_CONV_W + 4], (0, dev * SHARD_OUT), (4, SHARD_OUT))
    out["conv_w"], = _adam_parts([(g_cw[None], conv_w[0], m_conv_w[0], v_conv_w[0])], "adam_conv_w")

    shapes = dict(w_in=(1, D_MODEL, SHARD_IN), w_rnn_out=(1, SHARD_OUT, D_MODEL), w_attn_out=(1, SHARD_OUT, D_MODEL),
                  w_out=(1, SHARD_OUT, D_MODEL), w_rg_a=(1, RNN_BLOCKS, RNN_BLOCK_W, RNN_BLOCK_W),
                  w_rg_x=(1, RNN_BLOCKS, RNN_BLOCK_W, RNN_BLOCK_W), conv_w=(1, 4, SHARD_OUT))
    weights = ["pre_norm_g", "w_in", "b_gate", "conv_w", "conv_b", "w_rg_a", "b_rg_a", "w_rg_x", "b_rg_x",
               "lru_lambda", "attn_sinks", "w_rnn_out", "w_attn_out", "w_out", "post_norm_g"]
    results = []
    for kind in range(4):
        for name in weights:
            if name in out:
                results.append(out[name][kind].reshape(shapes[name]))
            else:
                results.append(small_out[kind][small_names.index(name)])
    loss = 0.5 / D_MODEL * jnp.sum(small[ROW_LOSS])
    return (loss, g["grad_x"][None], *results)
```

```python
import jax
import jax.numpy as jnp
from jax import lax
from jax.experimental import pallas as pl
from jax.experimental.pallas import tpu as pltpu

F32 = jnp.float32
BF16 = jnp.bfloat16

D_MODEL = 1024
D_RNN = 1024
RNN_BLOCKS = 16
RNN_BLOCK_W = 64
LRU_C = 8.0
N_Q_HEADS = 16
HEAD_DIM = 64
D_KV = 256
BLOCK = 128
ALIBI_MAX_BIAS = 8.0
EPS = 1e-6
D_IN = 6656
N_DEV = 8
N_CHIP = 4
SHARD_IN = D_IN // N_DEV
SHARD_OUT = D_MODEL // N_DEV
ATTN_SCALE = HEAD_DIM ** -0.5
MASKED = -1e30

ADAM_LR = 0.001
ADAM_B1 = 0.9
ADAM_B2 = 0.999
ADAM_EPS = 1e-08
ADAM_WD = 0.01
ADAM_STEP = 10

VMEM_LIMIT_BYTES = 52 * 1024 * 1024
LANE = 128
GROUP_W = 256
N_GROUPS = D_RNN // GROUP_W
SEG_CHUNK = 512

NT_DIMS = (((1,), (1,)), ((), ()))
TN_DIMS = (((0,), (0,)), ((), ()))
MESH = pl.DeviceIdType.MESH
ANY = pl.BlockSpec(memory_space=pl.ANY)


def _params(*semantics):
    return pltpu.CompilerParams(dimension_semantics=semantics, vmem_limit_bytes=VMEM_LIMIT_BYTES)


def _sigmoid(x):
    return 0.5 * jnp.tanh(0.5 * x) + 0.5


def _log1p(e):
    u = 1.0 + e
    den = jnp.where(u == 1.0, 1.0, u - 1.0)
    return jnp.where(u == 1.0, e, jnp.log(u) * (e / den))


def _softplus(z):
    return jnp.maximum(z, 0.0) + _log1p(jnp.exp(-jnp.abs(z)))


def _rows8(rows, width):
    idx = lax.broadcasted_iota(jnp.int32, (8, width), 0)
    out = jnp.zeros((8, width), F32)
    for r, v in enumerate(rows):
        out = jnp.where(idx == r, v, out)
    return out


class _Ride:
    def __init__(self, arrays, out_shapes, scratch_shapes, start, finish, middle=None, aliases=None):
        self.arrays, self.out_shapes, self.scratch_shapes = list(arrays), list(out_shapes), list(scratch_shapes)
        self.start, self.finish, self.middle = start, finish, middle
        self.aliases = dict(aliases or {})


class _Hosted:
    def __init__(self, ride, n_in, n_out, n_scratch=0, aliasing=False):
        self.ride = ride
        assert aliasing or not (ride and ride.aliases), "this host does not alias"
        self.aliases = {n_in + i: n_out + o for i, o in ride.aliases.items()} if ride else {}
        self.sizes = (n_in, len(ride.arrays) if ride else 0, n_out, len(ride.out_shapes) if ride else 0, n_scratch)
        self.arrays = ride.arrays if ride else []
        self.in_specs = [ANY] * len(self.arrays)
        self.out_shapes = ride.out_shapes if ride else []
        self.out_specs = [ANY] * len(self.out_shapes)
        self.scratch_shapes = ride.scratch_shapes if ride else []

    def split(self, refs):
        n_in, r_in, n_out, r_out, n_scr = self.sizes
        cuts = [0, n_in, n_in + r_in, n_in + r_in + n_out, n_in + r_in + n_out + r_out, n_in + r_in + n_out + r_out + n_scr]
        host_in, ride_in, host_out, ride_out, host_scr = (refs[cuts[k]:cuts[k + 1]] for k in range(5))
        ride_scr = refs[cuts[5]:]

        def start(when):
            if self.ride is not None:
                pl.when(when)(lambda: self.ride.start(ride_in, ride_out, ride_scr))

        def finish(when):
            if self.ride is not None:
                pl.when(when)(lambda: self.ride.finish(ride_in, ride_out, ride_scr))

        def middle(when):
            if self.ride is not None and self.ride.middle is not None:
                pl.when(when)(lambda: self.ride.middle(ride_in, ride_out, ride_scr))

        start.middle = middle
        return tuple(host_in) + tuple(host_out) + tuple(host_scr), start, finish

    def results(self, outs, n_out):
        outs = list(outs) if isinstance(outs, (list, tuple)) else [outs]
        return outs[:n_out], outs[n_out:]


def _join_rides(a, b):
    na, nb_ = len(a.arrays), len(b.arrays)
    oa = len(a.out_shapes)
    sa = len(a.scratch_shapes)

    def both(fa, fb):
        def run(ins, outs, sems):
            if fa is not None:
                fa(ins[:na], outs[:oa], sems[:sa])
            if fb is not None:
                fb(ins[na:na + nb_], outs[oa:], sems[sa:])
        return run

    middle = both(a.middle, b.middle) if (a.middle or b.middle) else None
    return _Ride(a.arrays + b.arrays, a.out_shapes + b.out_shapes, a.scratch_shapes + b.scratch_shapes,
                 both(a.start, b.start), both(a.finish, b.finish), middle)


def _load_resident(w_hbm, w_vmem, sems, first):
    def piece(c):
        rows = pl.ds(c * SEG_CHUNK, SEG_CHUNK)
        return pltpu.make_async_copy(w_hbm.at[rows], w_vmem.at[rows], sems.at[c])

    @pl.when(first)
    def _():
        for c in range(w_vmem.shape[0] // SEG_CHUNK):
            piece(c).start()

    def ready(c):
        @pl.when(first)
        def _():
            piece(c).wait()

    return ready


CHIP_ROWS = 2 * SHARD_IN
PROJ_WIDTHS = (2 * D_RNN, D_MODEL + 2 * D_KV, 3 * D_MODEL)
PROJ_DTYPES = (F32, BF16, BF16)


def _chip_pieces():
    starts = [0, PROJ_WIDTHS[0], PROJ_WIDTHS[0] + PROJ_WIDTHS[1], D_IN]
    pieces = []
    for k in range(N_CHIP):
        lo, hi = k * CHIP_ROWS, (k + 1) * CHIP_ROWS
        cur = []
        for a in range(len(PROJ_WIDTHS)):
            s0, s1 = max(lo, starts[a]), min(hi, starts[a + 1])
            if s0 < s1:
                cur.append((a, s0 - starts[a], s1 - s0, s0 - lo))
        pieces.append(cur)
    return pieces


def _gather_project(x, g, wt_shard, name, tm=1024):
    t, k = x.shape
    tm = min(tm, t)
    nt = t // tm
    pieces = _chip_pieces()

    def body(x_ref, g_ref, shard_ref, h_out, rx_ref, qkv_ref, ag_ref, wt_all,
             w_c, stage, o32, o16, h_all, send_sems, recv_sems, local_sem, stage_sems, out_sems, h_sems):
        s, ti = pl.program_id(0), pl.program_id(1)
        px, py, pc = _place()
        me, sibling = (px, py, pc), (px, py, 1 - pc)
        chips = [(px, py), (1 - px, py), (px, 1 - py), (1 - px, 1 - py)]
        outs = (rx_ref, qkv_ref, ag_ref)

        def slot(dev):
            return wt_all.at[4 * dev[0] + 2 * dev[1] + dev[2]]

        def copy(kk, block, to, src=None):
            return pltpu.make_async_remote_copy(
                src_ref=slot(block) if src is None else src, dst_ref=slot(block),
                send_sem=send_sems.at[kk], recv_sem=recv_sems.at[kk], device_id=to, device_id_type=MESH)

        mine = pltpu.make_async_copy(shard_ref, slot(me), local_sem)
        first = [copy(0, me, sibling, src=shard_ref)] + [copy(1 + j, me, (*chips[1 + j], pc), src=shard_ref) for j in range(3)]
        passed = [copy(4 + j, (*chips[1 + j], pc), sibling) for j in range(3)]

        @pl.when((s == 0) & (ti == 0))
        def _():
            mine.start()
            for cp in first[:3]:
                cp.start()

        def pass_on(step):
            copy(step, (*chips[step], pc), me).wait_recv()
            passed[step - 1].start()

        @pl.when((s == 2) & (ti == nt - 1))
        def _():
            pass_on(3)

        for step in range(N_CHIP):
            @pl.when((s == step) & (ti == 0))
            def _(step=step):
                chip = chips[step]
                if step == 0:
                    mine.wait()
                    copy(0, sibling, me).wait_recv()
                else:
                    if step == 1:
                        pass_on(1)
                        first[3].start()
                        pass_on(2)
                    copy(3 + step, (*chip, 1 - pc), me).wait_recv()
                loads = [pltpu.make_async_copy(slot((*chip, core)), stage.at[pl.ds(core * SHARD_IN, SHARD_IN)], stage_sems.at[core])
                         for core in (0, 1)]
                for cp in loads:
                    cp.start()
                for cp in loads:
                    cp.wait()
                w_c[...] = stage[...].T

        rows = pl.ds(pl.multiple_of(ti * tm, tm), tm)

        @pl.when(s == 0)
        def _():
            xv = x_ref[...]
            h_all[rows, :] = (xv * lax.rsqrt(jnp.mean(xv * xv, axis=-1, keepdims=True) + EPS) * g_ref[...]).astype(BF16)

        res = jnp.dot(h_all[rows, :], w_c[...], preferred_element_type=F32)

        n = s * nt + ti
        buf = lax.rem(n, 2)
        chip_idx = [2 * cx + cy for cx, cy in chips]

        def chip_at(step):
            return jnp.where(step == 0, chip_idx[0], jnp.where(step == 1, chip_idx[1], jnp.where(step == 2, chip_idx[2], chip_idx[3])))

        def h_write(b, tile):
            return pltpu.make_async_copy(h_all.at[pl.ds(tile * tm, tm)], h_out.at[pl.ds(tile * tm, tm)], h_sems.at[b])

        def writes(kchip, b, tile):
            cps = []
            for idx, (a, col, w, src) in enumerate(pieces[kchip]):
                staged = (o16 if PROJ_DTYPES[a] == BF16 else o32).at[b, :, pl.ds(src, w)]
                cps.append(pltpu.make_async_copy(staged, outs[a].at[pl.ds(tile * tm, tm), pl.ds(col, w)], out_sems.at[b, idx]))
            return cps

        o32[buf] = res
        o16[buf] = res.astype(BF16)
        kcur = chip_at(s)

        @pl.when(n > 0)
        def _():
            kprev = chip_at(lax.div(n - 1, nt))
            for kchip in range(N_CHIP):
                @pl.when(kprev == kchip)
                def _(kchip=kchip):
                    for cp in writes(kchip, 1 - buf, lax.rem(n - 1, nt)):
                        cp.wait()

            @pl.when(n <= nt)
            def _():
                h_write(1 - buf, n - 1).wait()

        @pl.when(s == 0)
        def _():
            h_write(buf, ti).start()

        for kchip in range(N_CHIP):
            @pl.when(kcur == kchip)
            def _(kchip=kchip):
                for cp in writes(kchip, buf, ti):
                    cp.start()

        @pl.when(n == N_CHIP * nt - 1)
        def _():
            for kchip in range(N_CHIP):
                @pl.when(kcur == kchip)
                def _(kchip=kchip):
                    for cp in writes(kchip, buf, ti):
                        cp.wait()
            for cp in first + passed:
                cp.wait_send()

    tile = pl.BlockSpec((tm, k), lambda s, ti: (jnp.where(s == 0, ti, nt - 1), 0))
    return pl.pallas_call(
        body, name=name, grid=(N_CHIP, nt),
        in_specs=[tile, pl.BlockSpec((1, k), lambda s, ti: (0, 0)), ANY],
        out_specs=[ANY, ANY, ANY, ANY, ANY],
        out_shape=[jax.ShapeDtypeStruct((t, k), BF16)]
        + [jax.ShapeDtypeStruct((t, w), dt) for w, dt in zip(PROJ_WIDTHS, PROJ_DTYPES)]
        + [jax.ShapeDtypeStruct((N_DEV, SHARD_IN, k), BF16)],
        scratch_shapes=[pltpu.VMEM((k, CHIP_ROWS), BF16), pltpu.VMEM((CHIP_ROWS, k), BF16),
                        pltpu.VMEM((2, tm, CHIP_ROWS), F32), pltpu.VMEM((2, tm, CHIP_ROWS), BF16), pltpu.VMEM((t, k), BF16),
                        pltpu.SemaphoreType.DMA((7,)), pltpu.SemaphoreType.DMA((7,)), pltpu.SemaphoreType.DMA,
                        pltpu.SemaphoreType.DMA((2,)), pltpu.SemaphoreType.DMA((2, 2)), pltpu.SemaphoreType.DMA((2,))],
        compiler_params=_params("arbitrary", "arbitrary"))(x, g, wt_shard)


def _mm_tn_pairs(prods, name):
    n_prod = len(prods)
    ktok, m = prods[0][0].shape
    n = prods[0][1].shape[1]
    half, blk = m // 2, m // N_DEV
    ns = 2 * n_prod
    kept = N_CHIP * blk

    def side_pieces(s, side):
        i, hf = divmod(s, 2)
        first = hf * N_DEV // 2
        return [((d - first) * blk, i * kept + (d // 2) * blk) for d in range(first, first + N_DEV // 2) if d % 2 == side]

    def body(*refs):
        a_hbm, b_hbm, pair_refs = refs[:n_prod], refs[n_prod:2 * n_prod], refs[2 * n_prod:3 * n_prod]
        a_buf, b_buf, res_buf, recv_all, pair_all, a_sems, b_sems, send_sems, recv_sems, out_sems = refs[3 * n_prod:]
        step = pl.program_id(0)
        px, py, pc = _place()

        def fetch_a(s):
            return pltpu.make_async_copy(a_hbm[s // 2].at[:, pl.ds((s % 2) * half, half)], a_buf.at[s % 2], a_sems.at[s % 2])

        def fetch_b(i):
            return pltpu.make_async_copy(b_hbm[i], b_buf.at[i % 2], b_sems.at[i % 2])

        @pl.when(step == 0)
        def _():
            fetch_a(0).start()
            fetch_b(0).start()

        for s in range(ns):
            @pl.when(step == s)
            def _(s=s):
                if s + 1 < ns:
                    fetch_a(s + 1).start()
                if s % 2 == 0 and s // 2 + 1 < n_prod:
                    fetch_b(s // 2 + 1).start()
                fetch_a(s).wait()
                if s % 2 == 0:
                    fetch_b(s // 2).wait()

        res_buf[step % 2] = lax.dot_general(a_buf[step % 2], b_buf[(step // 2) % 2], TN_DIMS, preferred_element_type=F32)

        def crossing(s, j, piece):
            off, at = piece
            return pltpu.make_async_remote_copy(
                src_ref=res_buf.at[s % 2, pl.ds(off, blk)], dst_ref=recv_all.at[pl.ds(at, blk)],
                send_sem=send_sems.at[s, j], recv_sem=recv_sems.at[s, j], device_id=(px, py, 1 - pc), device_id_type=MESH)

        def write(s, core):
            at = side_pieces(s, core)[0][1]
            return pltpu.make_async_copy(pair_all.at[pl.ds(at, 2 * blk)], pair_refs[s // 2].at[pl.ds(at % kept, 2 * blk)], out_sems.at[s])

        for core in (0, 1):
            def settle(s, core=core):
                for j, piece in enumerate(side_pieces(s, 1 - core)):
                    crossing(s, j, piece).wait_send()
                for j, (off, at) in enumerate(side_pieces(s, core)):
                    crossing(s, j, (off, at)).wait_recv()
                    pair_all[at:at + blk] = (res_buf[s % 2, off:off + blk] + recv_all[at:at + blk]).astype(BF16)
                write(s, core).start()

            for s in range(ns):
                @pl.when((pc == core) & (step == s))
                def _(s=s, settle=settle, core=core):
                    for j, piece in enumerate(side_pieces(s, 1 - core)):
                        crossing(s, j, piece).start()
                    if s > 0:
                        settle(s - 1)
                    if s == ns - 1:
                        settle(s)
                        for t in range(ns):
                            write(t, core).wait()

    pairs = pl.pallas_call(
        body, name=name, grid=(ns,),
        in_specs=[ANY] * (2 * n_prod), out_specs=[ANY] * n_prod,
        out_shape=[jax.ShapeDtypeStruct((kept, n), BF16)] * n_prod,
        scratch_shapes=[pltpu.VMEM((2, ktok, half), prods[0][0].dtype), pltpu.VMEM((2, ktok, n), prods[0][1].dtype),
                        pltpu.VMEM((2, half, n), F32), pltpu.VMEM((n_prod * kept, n), F32), pltpu.VMEM((n_prod * kept, n), BF16),
                        pltpu.SemaphoreType.DMA((2,)), pltpu.SemaphoreType.DMA((2,)),
                        pltpu.SemaphoreType.DMA((ns, 2)), pltpu.SemaphoreType.DMA((ns, 2)), pltpu.SemaphoreType.DMA((ns,))],
        compiler_params=_params("arbitrary"))(*[a for a, _ in prods], *[b for _, b in prods])
    return [pair.reshape(N_CHIP, blk, n) for pair in pairs]


def _segment_chunks(segs):
    bounds = [0]
    for s in segs:
        bounds.append(bounds[-1] + s.shape[1] // SEG_CHUNK)
    return bounds


def _input_grad(segs, wt, x, g, dy, name, tm=512, ride=None):
    m = segs[0].shape[0]
    rows, n = wt.shape
    tm = min(tm, m)
    bounds = _segment_chunks(segs)
    n_seg = len(segs)
    ni = m // tm
    host = _Hosted(ride, n_seg + 4, 2, 2, aliasing=True)

    def body(*refs):
        host_refs, start, finish = host.split(refs)
        a_refs = host_refs[:n_seg]
        wt_hbm, x_ref, g_ref, dy_ref, gx_ref, st_ref, wt_vmem, sems = host_refs[n_seg:]
        i = pl.program_id(0)
        start(i == 0)

        @pl.when(i == 0)
        def _():
            st_ref[...] = jnp.zeros_like(st_ref)

        ready = _load_resident(wt_hbm, wt_vmem, sems, i == 0)
        dh = None
        for s in range(n_seg):
            for c in range(bounds[s], bounds[s + 1]):
                ready(c)
            part = jnp.dot(a_refs[s][...], wt_vmem[bounds[s] * SEG_CHUNK:bounds[s + 1] * SEG_CHUNK, :], preferred_element_type=F32)
            dh = part if dh is None else dh + part
        xv = x_ref[...]
        r = lax.rsqrt(jnp.mean(xv * xv, axis=-1, keepdims=True) + EPS)
        xn = xv * r
        dxn = dh * g_ref[...]
        gx_ref[...] = dy_ref[...] + r * (dxn - xn * jnp.mean(dxn * xn, axis=-1, keepdims=True))
        st_ref[...] += _rows8([jnp.sum(dh * xn, axis=0, keepdims=True)], n)
        finish(i == ni - 1)

    tile = pl.BlockSpec((tm, n), lambda i: (i, 0))
    outs = pl.pallas_call(
        body, name=name, grid=(ni,),
        in_specs=[pl.BlockSpec((tm, sg.shape[1]), lambda i: (i, 0)) for sg in segs]
        + [ANY, tile, pl.BlockSpec((1, n), lambda i: (0, 0)), tile] + host.in_specs,
        out_specs=[tile, pl.BlockSpec((8, n), lambda i: (0, 0))] + host.out_specs,
        out_shape=[jax.ShapeDtypeStruct((m, n), F32), jax.ShapeDtypeStruct((8, n), F32)] + host.out_shapes,
        scratch_shapes=[pltpu.VMEM((rows, n), wt.dtype), pltpu.SemaphoreType.DMA((rows // SEG_CHUNK,))] + host.scratch_shapes,
        input_output_aliases=host.aliases,
        compiler_params=_params("arbitrary"))(*segs, wt, x, g, dy, *host.arrays)
    res, landed = host.results(outs, 2)
    return (*res, landed) if ride else tuple(res)


CHUNK_ORDER = (0, 4, 7, 10, 1, 5, 8, 11, 2, 6, 9, 12, 3)
EARLY_STEPS = 8


def _side_pieces():
    table = []
    for core in (0, 1):
        sides = ([None] * len(CHUNK_ORDER), [None] * len(CHUNK_ORDER))
        for s, cc in enumerate(CHUNK_ORDER):
            g0 = cc * SEG_CHUNK
            for d in range(N_DEV):
                lo, hi = max(g0, d * SHARD_IN), min(g0 + SEG_CHUNK, (d + 1) * SHARD_IN)
                if lo < hi:
                    side = sides[0 if d % 2 == core else 1]
                    assert side[s] is None
                    side[s] = (lo - g0, hi - lo, (d // 2) * SHARD_IN + lo - d * SHARD_IN)
        table.append(sides)
    return table


def _to_chip(src, red_ref, piece, s, send_sems, recv_sems, own, core):
    _, rows, at = piece
    k, r0 = divmod(at, SHARD_IN)
    return k, pltpu.make_async_remote_copy(
        src_ref=src, dst_ref=red_ref.at[own, pl.ds(r0, rows)], send_sem=send_sems.at[s], recv_sem=recv_sems.at[own, s],
        device_id=(k // 2, k % 2, core), device_id_type=MESH)


def _from_chip(src, red_ref, piece, s, send_sems, recv_sems, chip, me):
    _, rows, at = piece
    return pltpu.make_async_remote_copy(
        src_ref=src, dst_ref=red_ref.at[chip, pl.ds(at % SHARD_IN, rows)], send_sem=send_sems.at[s], recv_sem=recv_sems.at[chip, s],
        device_id=me, device_id_type=MESH)


def _mm_tn_seg_pair(segs, b, name):
    ktok = segs[0].shape[0]
    n = b.shape[1]
    bounds = _segment_chunks(segs)
    n_seg = len(segs)
    nc = bounds[-1]
    assert nc == len(CHUNK_ORDER)
    seg_of = [s for s in range(n_seg) for _ in range(bounds[s], bounds[s + 1])]
    table = _side_pieces()

    def body(*refs):
        a_hbm, b_hbm, pair_ref, red_ref = refs[:n_seg], refs[n_seg], refs[n_seg + 1], refs[n_seg + 2]
        (a_buf, b_vmem, res_buf, recv_all, pair_all, a_sems, b_sem, send_sems, recv_sems, out_sems,
         chip_send, chip_recv) = refs[n_seg + 3:]
        step = pl.program_id(0)
        px, py, pc = _place()
        own = 2 * px + py

        def fetch(s):
            sg = seg_of[CHUNK_ORDER[s]]
            cols = pl.ds((CHUNK_ORDER[s] - bounds[sg]) * SEG_CHUNK, SEG_CHUNK)
            return pltpu.make_async_copy(a_hbm[sg].at[:, cols], a_buf.at[s % 2], a_sems.at[s % 2])

        @pl.when(step == 0)
        def _():
            whole = pltpu.make_async_copy(b_hbm, b_vmem, b_sem)
            whole.start()
            fetch(0).start()
            whole.wait()

        for s in range(nc):
            @pl.when(step == s)
            def _(s=s):
                if s + 1 < nc:
                    fetch(s + 1).start()
                fetch(s).wait()

        res_buf[step % 2] = lax.dot_general(a_buf[step % 2], b_vmem[...], TN_DIMS, preferred_element_type=F32)

        def crossing(s, piece):
            off, rows, at = piece
            return pltpu.make_async_remote_copy(
                src_ref=res_buf.at[s % 2, pl.ds(off, rows)], dst_ref=recv_all.at[pl.ds(at, rows)],
                send_sem=send_sems.at[s], recv_sem=recv_sems.at[s], device_id=(px, py, 1 - pc), device_id_type=MESH)

        def write(s, piece):
            _, rows, at = piece
            return pltpu.make_async_copy(pair_all.at[pl.ds(at, rows)], pair_ref.at[pl.ds(at, rows)], out_sems.at[s])

        def to_chip(s, piece, core):
            return _to_chip(pair_all.at[pl.ds(piece[2], piece[1])], red_ref, piece, s, chip_send, chip_recv, own, core)

        for core in (0, 1):
            mine, theirs = table[core]

            def settle(s, mine=mine, theirs=theirs, core=core):
                if theirs[s] is not None:
                    crossing(s, theirs[s]).wait_send()
                if mine[s] is not None:
                    off, rows, at = mine[s]
                    crossing(s, mine[s]).wait_recv()
                    pair_all[at:at + rows] = (res_buf[s % 2, off:off + rows] + recv_all[at:at + rows]).astype(BF16)
                    write(s, mine[s]).start()
                    if s < EARLY_STEPS:
                        k, cp = to_chip(s, mine[s], core)
                        pl.when(own != k)(cp.start)

            for s in range(nc):
                @pl.when((pc == core) & (step == s))
                def _(s=s, settle=settle, mine=mine, theirs=theirs, core=core):
                    if theirs[s] is not None:
                        crossing(s, theirs[s]).start()
                    if s > 0:
                        settle(s - 1)
                    if s == nc - 1:
                        settle(s)
                        kept = [t for t in range(nc) if mine[t] is not None]
                        for t in kept:
                            write(t, mine[t]).wait()
                        early = [t for t in kept if t < EARLY_STEPS]
                        for t in early:
                            k, cp = to_chip(t, mine[t], core)
                            pl.when(own != k)(cp.wait_send)
                        for k in range(N_CHIP):
                            @pl.when(own == k)
                            def _(k=k):
                                for t in early:
                                    if mine[t][2] // SHARD_IN == k:
                                        for chip in range(N_CHIP):
                                            if chip != k:
                                                _from_chip(pair_all.at[pl.ds(mine[t][2], mine[t][1])], red_ref, mine[t], t,
                                                           chip_send, chip_recv, chip, (px, py, pc)).wait_recv()

    flat = jax.ShapeDtypeStruct((N_CHIP * SHARD_IN, n), BF16)
    pair, red = pl.pallas_call(
        body, name=name, grid=(nc,),
        in_specs=[ANY] * (n_seg + 1), out_specs=[ANY, ANY],
        out_shape=[flat, jax.ShapeDtypeStruct((N_CHIP, SHARD_IN, n), BF16)],
        scratch_shapes=[pltpu.VMEM((2, ktok, SEG_CHUNK), segs[0].dtype), pltpu.VMEM((ktok, n), b.dtype),
                        pltpu.VMEM((2, SEG_CHUNK, n), F32), pltpu.VMEM(flat.shape, F32), pltpu.VMEM(flat.shape, BF16),
                        pltpu.SemaphoreType.DMA((2,)), pltpu.SemaphoreType.DMA,
                        pltpu.SemaphoreType.DMA((nc,)), pltpu.SemaphoreType.DMA((nc,)), pltpu.SemaphoreType.DMA((nc,)),
                        pltpu.SemaphoreType.DMA((nc,)), pltpu.SemaphoreType.DMA((N_CHIP, nc))],
        compiler_params=_params("arbitrary"))(*segs, b)
    return pair.reshape(N_CHIP, SHARD_IN, n), red


RING = 3


def _ring_specs(streams, tm):
    return ([pltpu.VMEM((RING, tm, width), a.dtype) for a, _, width in streams]
            + [pltpu.SemaphoreType.DMA((len(streams), RING))])


def _ring_step(streams, hbm_refs, rings, sems, n, n_steps, tm):
    def fetch(step):
        rows = pl.ds(pl.multiple_of(step * tm, tm), tm)
        return [pltpu.make_async_copy(ref.at[rows, pl.ds(col, width)], ring.at[lax.rem(step, RING)], sems.at[k, lax.rem(step, RING)])
                for k, ((_, col, width), ref, ring) in enumerate(zip(streams, hbm_refs, rings))]

    @pl.when(n == 0)
    def _():
        for cp in fetch(0) + fetch(1):
            cp.start()

    @pl.when(n + 2 < n_steps)
    def _():
        for cp in fetch(n + 2):
            cp.start()

    for cp in fetch(n):
        cp.wait()
    return lax.rem(n, RING)


def _branches_fwd(z_rnn, z_attn, ag_ml, b_gate, w_rnn, w_attn, w_out, x, target, g_post, name, tm=512):
    t, d = x.shape
    tm = min(tm, t)
    n_steps = t // tm
    streams = [(z_rnn, 0, d), (z_attn, 0, d), (ag_ml, d, d), (ag_ml, 2 * d, d), (x, 0, d), (target, 0, d)]
    ns = len(streams)

    def body(*refs):
        hbm = refs[:ns]
        br_ref, ba_ref, wr_ref, wa_ref, wo_ref, g_ref, brr_ref, bra_ref, mg_ref, do_ref, dy_ref, st_ref = refs[ns:ns + 12]
        rings, sems = refs[ns + 12:2 * ns + 12], refs[2 * ns + 12]
        slot = _ring_step(streams, hbm, rings, sems, pl.program_id(0), n_steps, tm)
        zr_ref, za_ref, lr_ref, la_ref, x_ref, t_ref = (ring.at[slot] for ring in rings)

        @pl.when(pl.program_id(0) == 0)
        def _():
            st_ref[...] = jnp.zeros_like(st_ref)

        br_rnn = jnp.dot(zr_ref[...], wr_ref[...], preferred_element_type=F32)
        br_attn = jnp.dot(za_ref[...], wa_ref[...], preferred_element_type=F32)
        brr_ref[...] = br_rnn.astype(BF16)
        bra_ref[...] = br_attn.astype(BF16)
        g_rnn = _sigmoid(lr_ref[...].astype(F32) + br_ref[...])
        g_attn = _sigmoid(la_ref[...].astype(F32) + ba_ref[...])
        merged = (g_rnn * br_rnn + g_attn * br_attn).astype(BF16)
        mg_ref[...] = merged
        o = jnp.dot(merged, wo_ref[...], preferred_element_type=F32)
        g = g_ref[...]
        r = lax.rsqrt(jnp.mean(o * o, axis=-1, keepdims=True) + EPS)
        nrm = o * r
        err = x_ref[...] + nrm * g - t_ref[...]
        dy = err * (1.0 / d)
        dy_ref[...] = dy
        dn = dy * g
        do_ref[...] = (r * (dn - nrm * jnp.mean(dn * nrm, axis=-1, keepdims=True))).astype(BF16)
        st_ref[...] += _rows8([jnp.sum(dy * nrm, axis=0, keepdims=True), jnp.sum(err * err, axis=0, keepdims=True)], d)

    tile = pl.BlockSpec((tm, d), lambda i: (i, 0))
    weight = pl.BlockSpec((d, d), lambda i: (0, 0))
    bf = jax.ShapeDtypeStruct((t, d), BF16)
    return pl.pallas_call(
        body, name=name, grid=(n_steps,),
        in_specs=[ANY] * ns + [pl.BlockSpec((1, d), lambda i: (0, 0)), pl.BlockSpec((1, d), lambda i: (0, 1)),
                               weight, weight, weight, pl.BlockSpec((1, d), lambda i: (0, 0))],
        out_specs=[tile, tile, tile, tile, tile, pl.BlockSpec((8, d), lambda i: (0, 0))],
        out_shape=[bf, bf, bf, bf, jax.ShapeDtypeStruct((t, d), F32), jax.ShapeDtypeStruct((8, d), F32)],
        scratch_shapes=_ring_specs(streams, tm),
        compiler_params=_params("arbitrary"))(*[a for a, _, _ in streams], b_gate, b_gate, w_rnn, w_attn, w_out, g_post)


def _branches_bwd(dout, br_rnn, br_attn, ag_ml, b_gate, w_rnn, w_attn, w_out, name, tm=512):
    t, d = br_rnn.shape
    tm = min(tm, t)
    n_steps = t // tm
    streams = [(dout, 0, d), (br_rnn, 0, d), (br_attn, 0, d), (ag_ml, d, d), (ag_ml, 2 * d, d)]
    ns = len(streams)

    def body(*refs):
        hbm = refs[:ns]
        br_ref, ba_ref, wr_ref, wa_ref, wo_ref, dr_ref, da_ref, dl_ref, dzr_ref, dza_ref, st_ref, wt_ref = refs[ns:ns + 12]
        rings, sems = refs[ns + 12:2 * ns + 12], refs[2 * ns + 12]
        slot = _ring_step(streams, hbm, rings, sems, pl.program_id(0), n_steps, tm)
        do_ref, r_ref, a_ref, lr_ref, la_ref = (ring.at[slot] for ring in rings)

        @pl.when(pl.program_id(0) == 0)
        def _():
            st_ref[...] = jnp.zeros_like(st_ref)
            wt_ref[0] = wo_ref[...].T
            wt_ref[1] = wr_ref[...].T
            wt_ref[2] = wa_ref[...].T

        dm = jnp.dot(do_ref[...], wt_ref[0], preferred_element_type=F32)
        g_rnn = _sigmoid(lr_ref[...].astype(F32) + br_ref[...])
        g_attn = _sigmoid(la_ref[...].astype(F32) + ba_ref[...])
        dbr_rnn = (dm * g_rnn).astype(BF16)
        dbr_attn = (dm * g_attn).astype(BF16)
        dr_ref[...] = dbr_rnn
        da_ref[...] = dbr_attn
        dl_rnn = dm * r_ref[...].astype(F32) * g_rnn * (1.0 - g_rnn)
        dl_attn = dm * a_ref[...].astype(F32) * g_attn * (1.0 - g_attn)
        dl_ref[:, 0:d] = dl_rnn.astype(BF16)
        dl_ref[:, d:2 * d] = dl_attn.astype(BF16)
        st_ref[...] += _rows8([jnp.sum(dl_rnn, axis=0, keepdims=True), jnp.sum(dl_attn, axis=0, keepdims=True)], d)
        dzr_ref[...] = jnp.dot(dbr_rnn, wt_ref[1], preferred_element_type=F32).astype(BF16)
        dza_ref[...] = jnp.dot(dbr_attn, wt_ref[2], preferred_element_type=F32).astype(BF16)

    tile = pl.BlockSpec((tm, d), lambda i: (i, 0))
    weight = pl.BlockSpec((d, d), lambda i: (0, 0))
    bf = jax.ShapeDtypeStruct((t, d), BF16)
    return pl.pallas_call(
        body, name=name, grid=(n_steps,),
        in_specs=[ANY] * ns + [pl.BlockSpec((1, d), lambda i: (0, 0)), pl.BlockSpec((1, d), lambda i: (0, 1)), weight, weight, weight],
        out_specs=[tile, tile, pl.BlockSpec((tm, 2 * d), lambda i: (i, 0)), tile, tile, pl.BlockSpec((8, d), lambda i: (0, 0))],
        out_shape=[bf, bf, jax.ShapeDtypeStruct((t, 2 * d), BF16), bf, bf, jax.ShapeDtypeStruct((8, d), F32)],
        scratch_shapes=[pltpu.VMEM((3, d, d), BF16)] + _ring_specs(streams, tm),
        compiler_params=_params("arbitrary"))(*[a for a, _, _ in streams], b_gate, b_gate, w_rnn, w_attn, w_out)


def _lru_decay(r, sp):
    log_a = (-LRU_C) * r * sp
    return log_a, jnp.exp(log_a)


def _lru_gates(c, wa, ba, wx, bx, sp):
    cb = c.astype(BF16)
    r = _sigmoid(jnp.dot(cb, wa, preferred_element_type=F32) + ba)
    ig = _sigmoid(jnp.dot(cb, wx, preferred_element_type=F32) + bx)
    log_a, a = _lru_decay(r, sp)
    mult = jnp.sqrt(-jnp.tanh(log_a) * (a * a + 1.0))
    return cb, r, ig, a, mult


SUBLANES = 8


def _scan_fwd(a, u, carry, tt):
    w = a.shape[1]
    ng = tt // SUBLANES
    a3 = a.reshape(ng, SUBLANES, w)
    u3 = u.reshape(ng, SUBLANES, w)
    sub = lax.broadcasted_iota(jnp.int32, (ng, SUBLANES, w), 1)
    d = 1
    while d < SUBLANES:
        keep = sub >= d
        u3 = u3 + a3 * jnp.where(keep, pltpu.roll(u3, d, 1), 0.0)
        a3 = a3 * jnp.where(keep, pltpu.roll(a3, d, 1), 1.0)
        d *= 2
    out = []
    for g in range(ng):
        hg = u3[g] + a3[g] * carry
        out.append(hg)
        carry = hg[SUBLANES - 1:SUBLANES, :]
    return jnp.concatenate(out, axis=0)


def _scan_rev(b, g, carry, tt):
    w = b.shape[1]
    ng = tt // SUBLANES
    b3 = b.reshape(ng, SUBLANES, w)
    g3 = g.reshape(ng, SUBLANES, w)
    sub = lax.broadcasted_iota(jnp.int32, (ng, SUBLANES, w), 1)
    d = 1
    while d < SUBLANES:
        keep = sub < SUBLANES - d
        g3 = g3 + b3 * jnp.where(keep, pltpu.roll(g3, SUBLANES - d, 1), 0.0)
        b3 = b3 * jnp.where(keep, pltpu.roll(b3, SUBLANES - d, 1), 1.0)
        d *= 2
    out = [None] * ng
    for k in range(ng - 1, -1, -1):
        hk = g3[k] + b3[k] * carry
        out[k] = hk
        carry = hk[0:1, :]
    return jnp.concatenate(out, axis=0)


def _conv_taps(cw, bias, x, ext_ref, tt):
    x2 = ext_ref[7:7 + tt, :]
    x1 = ext_ref[6:6 + tt, :]
    x0 = ext_ref[5:5 + tt, :]
    return bias + cw[3:4] * x + cw[2:3] * x2 + cw[1:2] * x1 + cw[0:1] * x0


def _rnn_fwd(rx_rg, cw, cb, wa, ba, wx, bx, lam, name, tt=512):
    t = rx_rg.shape[0]
    tt = min(tt, t)
    w = GROUP_W

    nt = t // tt
    n_steps = N_GROUPS * nt

    def body(in_hbm, cw_ref, cb_ref, wa_ref, ba_ref, wx_ref, bx_ref, lam_ref,
             y_ref, z_ref, c_ref, r_ref, ig_ref, mult_ref, ext_ref, hc_ref, x_ring, g_ring, ring_sems):
        @pl.when(pl.program_id(1) == 0)
        def _():
            ext_ref[0:8, :] = jnp.zeros((8, w), F32)
            hc_ref[...] = jnp.zeros((8, w), F32)

        n = pl.program_id(0) * nt + pl.program_id(1)

        def fetch(step):
            slot = lax.rem(step, 3)
            rows = pl.ds(pl.multiple_of(lax.rem(step, nt) * tt, tt), tt)
            group = lax.div(step, nt)
            return [pltpu.make_async_copy(in_hbm.at[rows, pl.ds(pl.multiple_of((half * N_GROUPS + group) * w, w), w)],
                                          ring.at[slot], ring_sems.at[half, slot])
                    for half, ring in enumerate((x_ring, g_ring))]

        @pl.when(n == 0)
        def _():
            for cp in fetch(0) + fetch(1):
                cp.start()

        @pl.when(n + 2 < n_steps)
        def _():
            for cp in fetch(n + 2):
                cp.start()

        for cp in fetch(n):
            cp.wait()
        slot = lax.rem(n, 3)
        x = x_ring[slot]
        ext_ref[8:8 + tt, :] = x
        c = _conv_taps(cw_ref[...], cb_ref[...], x, ext_ref, tt)
        ext_ref[0:8, :] = x[tt - 8:tt, :]
        sp = _softplus(-lam_ref[...])
        _, r, ig, a, mult = _lru_gates(c, wa_ref[...], ba_ref[...], wx_ref[...], bx_ref[...], sp)
        c_ref[...] = c
        r_ref[...] = r
        ig_ref[...] = ig
        mult_ref[...] = mult
        h = _scan_fwd(a, mult * (ig * c), hc_ref[7:8, :], tt)
        hc_ref[...] = h[tt - 8:tt, :]
        y_ref[...] = h
        rg = g_ring[slot]
        z_ref[...] = (h * rg * _sigmoid(rg)).astype(BF16)

    vec = pl.BlockSpec((1, w), lambda g, i: (0, g))
    mat = pl.BlockSpec((None, w, w), lambda g, i: (g, 0, 0))
    tile = pl.BlockSpec((tt, w), lambda g, i: (i, g))
    return pl.pallas_call(
        body, name=name, grid=(N_GROUPS, nt),
        in_specs=[ANY, pl.BlockSpec((4, w), lambda g, i: (0, g)), vec, mat, vec, mat, vec, vec],
        out_specs=[tile, tile] + [pl.BlockSpec((None, tt, w), lambda g, i: (g, i, 0))] * 4,
        out_shape=[jax.ShapeDtypeStruct((t, D_RNN), F32), jax.ShapeDtypeStruct((t, D_RNN), BF16)]
        + [jax.ShapeDtypeStruct((N_GROUPS, t, w), F32)] * 4,
        scratch_shapes=[pltpu.VMEM((tt + 8, w), F32), pltpu.VMEM((8, w), F32),
                        pltpu.VMEM((3, tt, w), F32), pltpu.VMEM((3, tt, w), F32), pltpu.SemaphoreType.DMA((2, 3))],
        compiler_params=_params("arbitrary", "arbitrary"))(rx_rg, cw, cb, wa, ba, wx, bx, lam)


def _rnn_bwd(rx_rg, y, dz, c, r, ig, mult, cw, wa, wx, lam, name, tt=512):
    t = rx_rg.shape[0]
    tt = min(tt, t)
    nt = t // tt
    w = GROUP_W

    def body(rx_ref, rg_ref, y_ref, yt_ref, dz_ref, c_ref, r_ref, ig_ref, mult_ref, cw_ref, wa_ref, wx_ref, lam_ref,
             drx_ref, drg_ref, st_ref, gda_ref, gdx_ref, dcx_ref, wcar_ref, acar_ref, dwa_ref, dwx_ref):
        ii = pl.program_id(1)

        @pl.when(ii == 0)
        def _():
            wcar_ref[...] = jnp.zeros((8, w), F32)
            acar_ref[...] = jnp.zeros((8, w), F32)
            dcx_ref[tt:tt + 8, :] = jnp.zeros((8, w), F32)
            st_ref[...] = jnp.zeros_like(st_ref)
            dwa_ref[...] = jnp.zeros_like(dwa_ref)
            dwx_ref[...] = jnp.zeros_like(dwx_ref)

        has_prev = jnp.where(ii == nt - 1, 0.0, 1.0)
        cwv = cw_ref[...]
        lam = lam_ref[...]
        sp = _softplus(-lam)
        wa = wa_ref[...]
        wx = wx_ref[...]
        c = c_ref[...]
        r = r_ref[...]
        ig = ig_ref[...]
        mult = mult_ref[...]
        _, a = _lru_decay(r, sp)
        cb16 = c.astype(BF16)

        rg = rg_ref[...]
        sg = _sigmoid(rg)
        dz = dz_ref[...].astype(F32)
        yv = y_ref[...]
        drg_ref[...] = (dz * yv * (sg * (1.0 + rg * (1.0 - sg)))).astype(BF16)

        row = lax.broadcasted_iota(jnp.int32, (tt, w), 0)
        b = jnp.where(row < tt - 1, pltpu.roll(a, tt - 1, 0), acar_ref[0:1, :])
        dh = _scan_rev(b, dz * (rg * sg), wcar_ref[0:1, :], tt)
        wcar_ref[...] = dh[0:8, :]
        acar_ref[...] = a[0:8, :]

        hprev = jnp.where(row >= 1, pltpu.roll(yv, 1, 0), yt_ref[7:8, :] * has_prev)
        dmult = dh * (ig * c)
        dig = dh * mult * c
        dlog_a = dh * hprev * a - dmult * (a * a / mult)
        dpa = dlog_a * ((-LRU_C) * sp) * r * (1.0 - r)
        dpx = dig * ig * (1.0 - ig)
        dsp = jnp.sum(dlog_a * r, axis=0, keepdims=True) * (-LRU_C)
        dlam = dsp * (-_sigmoid(-lam))
        dpa16 = dpa.astype(BF16)
        dpx16 = dpx.astype(BF16)
        dwa_ref[...] += lax.dot_general(cb16, dpa16, TN_DIMS, preferred_element_type=F32)
        dwx_ref[...] += lax.dot_general(cb16, dpx16, TN_DIMS, preferred_element_type=F32)
        dc = (dh * mult * ig
              + lax.dot_general(dpa16, wa, NT_DIMS, preferred_element_type=F32)
              + lax.dot_general(dpx16, wx, NT_DIMS, preferred_element_type=F32))

        dcx_ref[0:tt, :] = dc
        dc1 = dcx_ref[1:1 + tt, :]
        dc2 = dcx_ref[2:2 + tt, :]
        dc3 = dcx_ref[3:3 + tt, :]
        drx_ref[...] = (cwv[3:4] * dc + cwv[2:3] * dc1 + cwv[1:2] * dc2 + cwv[0:1] * dc3).astype(BF16)
        dcx_ref[tt:tt + 8, :] = dc[0:8, :]

        def colsum(v):
            return jnp.sum(v, axis=0, keepdims=True)

        x = rx_ref[...]
        st_ref[...] += _rows8([colsum(dc), colsum(dpa), colsum(dpx), dlam,
                               colsum(dc3 * x), colsum(dc2 * x), colsum(dc1 * x), colsum(dc * x)], w)

        @pl.when(ii == nt - 1)
        def _():
            for blk in range(GROUP_W // RNN_BLOCK_W):
                rows = slice(blk * RNN_BLOCK_W, (blk + 1) * RNN_BLOCK_W)
                gda_ref[blk] = dwa_ref[rows, rows]
                gdx_ref[blk] = dwx_ref[rows, rows]

    def rev(ii):
        return nt - 1 - ii

    def tail(g, ii):
        return (jnp.maximum(rev(ii) * (tt // 8) - 1, 0), g)

    vec = pl.BlockSpec((1, w), lambda g, ii: (0, g))
    mat = pl.BlockSpec((None, w, w), lambda g, ii: (g, 0, 0))
    tile = pl.BlockSpec((tt, w), lambda g, ii: (rev(ii), g))
    diag_shape = (N_GROUPS, GROUP_W // RNN_BLOCK_W, RNN_BLOCK_W, RNN_BLOCK_W)
    diag = pl.BlockSpec((None,) + diag_shape[1:], lambda g, ii: (g, 0, 0, 0))
    kept = pl.BlockSpec((None, tt, w), lambda g, ii: (g, rev(ii), 0))
    return pl.pallas_call(
        body, name=name, grid=(N_GROUPS, nt),
        in_specs=[tile, pl.BlockSpec((tt, w), lambda g, ii: (rev(ii), N_GROUPS + g)),
                  tile, pl.BlockSpec((8, w), tail), tile, kept, kept, kept, kept,
                  pl.BlockSpec((4, w), lambda g, ii: (0, g)), mat, mat, vec],
        out_specs=[tile, tile, pl.BlockSpec((8, w), lambda g, ii: (0, g)), diag, diag],
        out_shape=[jax.ShapeDtypeStruct((t, D_RNN), BF16), jax.ShapeDtypeStruct((t, D_RNN), BF16),
                   jax.ShapeDtypeStruct((8, D_RNN), F32),
                   jax.ShapeDtypeStruct(diag_shape, F32), jax.ShapeDtypeStruct(diag_shape, F32)],
        scratch_shapes=[pltpu.VMEM((tt + 8, w), F32), pltpu.VMEM((8, w), F32),
                        pltpu.VMEM((8, w), F32), pltpu.VMEM((w, w), F32), pltpu.VMEM((w, w), F32)],
        compiler_params=_params("parallel", "arbitrary"))(rx_rg, rx_rg, y, y, dz, c, r, ig, mult, cw, wa, wx, lam)


def _half_mask(shape, half):
    lane = lax.broadcasted_iota(jnp.int32, shape, 1)
    return (lane >= HEAD_DIM) if half else (lane < HEAD_DIM)


def _dup_half(t, half):
    sel = jnp.where(_half_mask(t.shape, half), t, 0.0)
    return sel + pltpu.roll(sel, HEAD_DIM, 1)


def _band_geometry(n):
    qi = lax.broadcasted_iota(jnp.int32, (BLOCK, 2 * BLOCK), 0)
    kj = lax.broadcasted_iota(jnp.int32, (BLOCK, 2 * BLOCK), 1)
    dist = BLOCK + qi - kj
    first_key = jnp.where(n > 0, 0, BLOCK)
    valid = (dist >= 0) & (dist < BLOCK) & (kj >= first_key)
    return dist.astype(F32), valid


GROUP = 4


def _kv_dup(prev_ref, cur_ref, hk, scale=1.0):
    tile = hk // 2
    kt = jnp.concatenate([prev_ref[:, tile * LANE:(tile + 1) * LANE], cur_ref[:, tile * LANE:(tile + 1) * LANE]], axis=0)
    return (_dup_half(kt.astype(F32), hk % 2) * scale).astype(BF16)


def _fill_bias(bias_ref, sm_ref, n):
    distf, valid = _band_geometry(n)
    for head in range(N_Q_HEADS):
        bias_ref[head] = jnp.where(valid, -sm_ref[1, head] * distf, MASKED)


def _head_scores(q2s, half, kdup, bias):
    qm = jnp.where(_half_mask(q2s.shape, half), q2s, jnp.zeros_like(q2s))
    return qm, lax.dot_general(qm, kdup, NT_DIMS, preferred_element_type=F32) + bias


def _attn_specs(nb, clamp_last):
    def blk(n):
        return jnp.minimum(n, nb - 1) if clamp_last else n

    q_spec = pl.BlockSpec((BLOCK, D_MODEL), lambda n: (blk(n), 0))
    k_prev = pl.BlockSpec((BLOCK, D_KV), lambda n: (jnp.maximum(blk(n) - 1, 0), D_MODEL // D_KV))
    k_cur = pl.BlockSpec((BLOCK, D_KV), lambda n: (blk(n), D_MODEL // D_KV))
    v_prev = pl.BlockSpec((BLOCK, D_KV), lambda n: (jnp.maximum(blk(n) - 1, 0), D_MODEL // D_KV + 1))
    v_cur = pl.BlockSpec((BLOCK, D_KV), lambda n: (blk(n), D_MODEL // D_KV + 1))
    return q_spec, k_prev, k_cur, v_prev, v_cur


def _attn_fwd(sm, qkv, ag_ml, name, ride=None):
    t = qkv.shape[0]
    nb = t // BLOCK

    host = _Hosted(ride, 7, 3, 1)

    def body(*refs):
        (sm_ref, q_ref, kp_ref, kc_ref, vp_ref, vc_ref, ag_ref, y_ref, z_ref, lse_ref, bias_ref), start, finish = host.split(refs)
        n = pl.program_id(0)
        start(n == 0)
        start.middle(n == (3 * nb) // 4)

        @pl.when(n <= 1)
        def _():
            _fill_bias(bias_ref, sm_ref, n)

        lane = lax.broadcasted_iota(jnp.int32, (BLOCK, LANE), 1)
        low = lane < HEAD_DIM
        lse = jnp.zeros((BLOCK, LANE), F32)
        for hk in range(N_Q_HEADS // GROUP):
            kdup = _kv_dup(kp_ref, kc_ref, hk)
            vdup = _kv_dup(vp_ref, vc_ref, hk)
            for k in (0, 1):
                c = slice((2 * hk + k) * LANE, (2 * hk + k + 1) * LANE)
                q2s = q_ref[:, c] * ATTN_SCALE
                outs = []
                for half in (0, 1):
                    head = GROUP * hk + 2 * k + half
                    _, s = _head_scores(q2s, half, kdup, bias_ref[head])
                    sink = sm_ref[0, head]
                    m = jnp.maximum(jnp.max(s, axis=1, keepdims=True), sink)
                    e = jnp.exp(s - m)
                    l = jnp.sum(e, axis=1, keepdims=True) + jnp.exp(sink - m)
                    outs.append(jnp.dot((e * (1.0 / l)).astype(BF16), vdup, preferred_element_type=F32))
                    lse = jnp.where(lane == head, m + jnp.log(l), lse)
                yt = jnp.where(low, outs[0], outs[1])
                y_ref[:, c] = yt
                ag = ag_ref[:, c].astype(F32)
                z_ref[:, c] = (yt * ag * _sigmoid(ag)).astype(BF16)
        lse_ref[...] = lse
        finish(n == nb - 1)

    q_spec, k_prev, k_cur, v_prev, v_cur = _attn_specs(nb, False)
    wide = pl.BlockSpec((BLOCK, D_MODEL), lambda n: (n, 0))
    outs = pl.pallas_call(
        body, name=name, grid=(nb,),
        in_specs=[pl.BlockSpec(memory_space=pltpu.SMEM), q_spec, k_prev, k_cur, v_prev, v_cur, wide] + host.in_specs,
        out_specs=[wide, wide, pl.BlockSpec((BLOCK, LANE), lambda n: (n, 0))] + host.out_specs,
        out_shape=[jax.ShapeDtypeStruct((t, D_MODEL), F32), jax.ShapeDtypeStruct((t, D_MODEL), BF16),
                   jax.ShapeDtypeStruct((t, LANE), F32)] + host.out_shapes,
        scratch_shapes=[pltpu.VMEM((N_Q_HEADS, BLOCK, 2 * BLOCK), F32)] + host.scratch_shapes,
        compiler_params=_params("arbitrary"))(sm, qkv, qkv, qkv, qkv, qkv, ag_ml, *host.arrays)
    res, landed = host.results(outs, 3)
    return (*res, landed) if ride else tuple(res)


def _attn_bwd(sm, qkv, ag_ml, y, lse, dz, name, ride=None):
    t = qkv.shape[0]
    nb = t // BLOCK
    host = _Hosted(ride, 10, 4, 3)

    def body(*refs):
        host_refs, start, finish = host.split(refs)
        (sm_ref, q_ref, kp_ref, kc_ref, vp_ref, vc_ref, ag_ref, y_ref, lse_ref, dz_ref,
         dq_ref, dkv_ref, dag_ref, ds_ref, ck_ref, cv_ref, bias_ref) = host_refs
        n = pl.program_id(0)
        start(n == 0)
        start.middle(n == (3 * nb) // 4)

        @pl.when(n == 0)
        def _():
            ck_ref[...] = jnp.zeros_like(ck_ref)
            cv_ref[...] = jnp.zeros_like(cv_ref)
            ds_ref[...] = jnp.zeros_like(ds_ref)

        @pl.when(n <= 1)
        def _():
            _fill_bias(bias_ref, sm_ref, n)

        @pl.when(n < nb)
        def _():
            lane8 = lax.broadcasted_iota(jnp.int32, (8, LANE), 1)
            row8 = lax.broadcasted_iota(jnp.int32, (8, LANE), 0)
            dsink = jnp.zeros((8, LANE), F32)
            dk_heads, dv_heads = [], []
            lse_tile = lse_ref[...]
            for hk in range(N_Q_HEADS // GROUP):
                kdup = _kv_dup(kp_ref, kc_ref, hk)
                ks = _kv_dup(kp_ref, kc_ref, hk, ATTN_SCALE)
                vdup = _kv_dup(vp_ref, vc_ref, hk)
                qms, dyhs, y_rows = [], [], []
                for k in (0, 1):
                    c = slice((2 * hk + k) * LANE, (2 * hk + k + 1) * LANE)
                    ag = ag_ref[:, c].astype(F32)
                    sg = _sigmoid(ag)
                    dzt = dz_ref[:, c].astype(F32)
                    yt = y_ref[:, c]
                    dag_ref[:, c] = (dzt * yt * (sg * (1.0 + ag * (1.0 - sg)))).astype(BF16)
                    dyt = dzt * (ag * sg)
                    q2s = q_ref[:, c] * ATTN_SCALE
                    for half in (0, 1):
                        hm = _half_mask(q2s.shape, half)
                        qms.append(jnp.where(hm, q2s, jnp.zeros_like(q2s)))
                        dyhs.append(jnp.where(hm, dyt, 0.0))
                        y_rows.append(yt)
                low = _half_mask((BLOCK, LANE), 0)
                dk_acc = dv_acc = None
                for k in (0, 1):
                    c = slice((2 * hk + k) * LANE, (2 * hk + k + 1) * LANE)
                    qm2 = jnp.concatenate(qms[2 * k:2 * k + 2], axis=0)
                    dy2_16 = jnp.concatenate(dyhs[2 * k:2 * k + 2], axis=0).astype(BF16)
                    s2 = lax.dot_general(qm2, kdup, NT_DIMS, preferred_element_type=F32)
                    dp2 = lax.dot_general(dy2_16, vdup, NT_DIMS, preferred_element_type=F32)
                    probs16, ds16 = [], []
                    for half in (0, 1):
                        r = 2 * k + half
                        head = GROUP * hk + r
                        lh = lse_tile[:, head:head + 1]
                        psink = jnp.exp(sm_ref[0, head] - lh)
                        delta = jnp.sum(dyhs[r] * y_rows[r], axis=1, keepdims=True)
                        rows = slice(half * BLOCK, (half + 1) * BLOCK)
                        probs = jnp.exp(s2[rows] + bias_ref[head] - lh)
                        ds16.append((probs * (dp2[rows] - delta)).astype(BF16))
                        probs16.append(probs.astype(BF16))
                        dsink = dsink + jnp.where((row8 == 0) & (lane8 == head),
                                                  -jnp.sum(psink * delta, axis=0, keepdims=True), 0.0)
                    ds2 = jnp.concatenate(ds16, axis=0)
                    p2 = jnp.concatenate(probs16, axis=0)
                    dq2 = jnp.dot(ds2, ks, preferred_element_type=F32)
                    dq_ref[:, c] = jnp.where(low, dq2[0:BLOCK], dq2[BLOCK:2 * BLOCK]).astype(BF16)
                    dk_part = lax.dot_general(ds2, qm2, TN_DIMS, preferred_element_type=F32)
                    dv_part = lax.dot_general(p2, dy2_16, TN_DIMS, preferred_element_type=F32)
                    dk_acc = dk_part if dk_acc is None else dk_acc + dk_part
                    dv_acc = dv_part if dv_acc is None else dv_acc + dv_part
                dk_heads.append(dk_acc + pltpu.roll(dk_acc, HEAD_DIM, 1))
                dv_heads.append(dv_acc + pltpu.roll(dv_acc, HEAD_DIM, 1))
            ds_ref[...] += dsink
            low = _half_mask((2 * BLOCK, LANE), 0)
            for tile in range(2):
                cols = slice(tile * LANE, (tile + 1) * LANE)
                dkt = jnp.where(low, dk_heads[2 * tile], dk_heads[2 * tile + 1])
                dvt = jnp.where(low, dv_heads[2 * tile], dv_heads[2 * tile + 1])
                dkv_ref[:, cols] = (ck_ref[:, cols] + dkt[0:BLOCK, :]).astype(BF16)
                dkv_ref[:, D_KV + tile * LANE:D_KV + (tile + 1) * LANE] = (cv_ref[:, cols] + dvt[0:BLOCK, :]).astype(BF16)
                ck_ref[:, cols] = dkt[BLOCK:2 * BLOCK, :]
                cv_ref[:, cols] = dvt[BLOCK:2 * BLOCK, :]

        @pl.when(n == nb)
        def _():
            dkv_ref[:, 0:D_KV] = ck_ref[...].astype(BF16)
            dkv_ref[:, D_KV:2 * D_KV] = cv_ref[...].astype(BF16)

        finish(n == nb)

    q_spec, k_prev, k_cur, v_prev, v_cur = _attn_specs(nb, True)
    wide = pl.BlockSpec((BLOCK, D_MODEL), lambda n: (jnp.minimum(n, nb - 1), 0))
    outs = pl.pallas_call(
        body, name=name, grid=(nb + 1,),
        in_specs=[pl.BlockSpec(memory_space=pltpu.SMEM), q_spec, k_prev, k_cur, v_prev, v_cur, wide, wide,
                  pl.BlockSpec((BLOCK, LANE), lambda n: (jnp.minimum(n, nb - 1), 0)), wide] + host.in_specs,
        out_specs=[wide, pl.BlockSpec((BLOCK, 2 * D_KV), lambda n: (jnp.maximum(n - 1, 0), 0)), wide,
                   pl.BlockSpec((8, LANE), lambda n: (0, 0))] + host.out_specs,
        out_shape=[jax.ShapeDtypeStruct((t, D_MODEL), BF16), jax.ShapeDtypeStruct((t, 2 * D_KV), BF16),
                   jax.ShapeDtypeStruct((t, D_MODEL), BF16), jax.ShapeDtypeStruct((8, LANE), F32)] + host.out_shapes,
        scratch_shapes=[pltpu.VMEM((BLOCK, D_KV), F32), pltpu.VMEM((BLOCK, D_KV), F32),
                        pltpu.VMEM((N_Q_HEADS, BLOCK, 2 * BLOCK), F32)] + host.scratch_shapes,
        compiler_params=_params("arbitrary"))(sm, qkv, qkv, qkv, qkv, qkv, ag_ml, y, lse, dz, *host.arrays)
    res, landed = host.results(outs, 4)
    return (*res, landed) if ride else tuple(res)


def _local_grads(x, target, p, project, late_weights, reduce_out, reduce_in):
    h, rx_rg, qkv, ag_ml, wt = project(x, p["pre_g"])
    y_attn, z_attn, lse, landed = _attn_fwd(p["sm"], qkv, ag_ml, "attn_fwd", ride=late_weights[0])
    p = {**p, **late_weights[1](landed)}
    y_rnn, z_rnn, *kept_rnn = _rnn_fwd(
        rx_rg, p["cw"], p["cb"], p["wbd_a"], p["b_a"], p["wbd_x"], p["b_x"], p["lam"], "rnn_fwd")
    br_rnn, br_attn, merged, dout, dy, st_post = _branches_fwd(
        z_rnn, z_attn, ag_ml, p["b_gate"], p["w_rnn"], p["w_attn"], p["w_out"], x, target, p["post_g"], "branches_fwd")

    dbr_rnn, dbr_attn, d_ml, dz_rnn, dz_attn, st_merge = _branches_bwd(
        dout, br_rnn, br_attn, ag_ml, p["b_gate"], p["w_rnn"], p["w_attn"], p["w_out"], "branches_bwd")
    out_pairs = _mm_tn_pairs([(z_rnn, dbr_rnn), (z_attn, dbr_attn), (merged, dout)], "gw_outs")
    d_rx, d_rg, st_rnn, g_rg_a, g_rg_x = _rnn_bwd(
        rx_rg, y_rnn, dz_rnn, *kept_rnn, p["cw"], p["wbd_a"], p["wbd_x"], p["lam"], "rnn_bwd")
    dq, dkv, d_ag, st_sink, red_out = _attn_bwd(p["sm"], qkv, ag_ml, y_attn, lse, dz_attn, "attn_bwd",
                                                ride=reduce_out(out_pairs, g_rg_a, g_rg_x))

    segs = [d_rx, d_rg, dq, dkv, d_ag, d_ml]
    grad_x, st_pre, red_in = _input_grad(segs, wt, x, p["pre_g"], dy, "input_grad",
                                         ride=reduce_in(*_mm_tn_seg_pair(segs, h, "gw_in")))
    return dict(grad_x=grad_x, st_post=st_post, st_merge=st_merge, st_rnn=st_rnn, st_sink=st_sink, st_pre=st_pre,
                red_out=red_out, red_in=red_in)


def _place():
    x, y, c = lax.axis_index("x"), lax.axis_index("y"), lax.axis_index("c")
    return x, y, c


def _gather_ride(shards):
    n = len(shards)

    def copies(ins, outs, sems):
        send_sems, recv_sems, local_sems = sems
        x, y, c = _place()
        me, sibling = (x, y, c), (x, y, 1 - c)
        chips = [(1 - x, y), (x, 1 - y), (1 - x, 1 - y)]

        def slot(a, dev):
            return outs[a].at[4 * dev[0] + 2 * dev[1] + dev[2]]

        def copy(a, k, block, to, src=None):
            return pltpu.make_async_remote_copy(
                src_ref=slot(a, block) if src is None else src, dst_ref=slot(a, block),
                send_sem=send_sems.at[a, k], recv_sem=recv_sems.at[a, k], device_id=to, device_id_type=MESH)

        mine = [pltpu.make_async_copy(ins[a], slot(a, me), local_sems.at[a]) for a in range(n)]
        first = []
        for a in range(n):
            first.append(copy(a, 0, me, sibling, src=ins[a]))
            first += [copy(a, 1 + j, me, (*chip, c), src=ins[a]) for j, chip in enumerate(chips)]
        return me, sibling, chips, c, copy, mine, first

    def start(ins, outs, sems):
        *_, mine, first = copies(ins, outs, sems)
        for cp in mine + first:
            cp.start()

    def middle(ins, outs, sems):
        me, sibling, chips, c, copy, _, _ = copies(ins, outs, sems)
        for j, chip in enumerate(chips):
            for a in range(n):
                copy(a, 1 + j, (*chip, c), me).wait_recv()
                copy(a, 4 + j, (*chip, c), sibling).start()

    def finish(ins, outs, sems):
        me, sibling, chips, c, copy, mine, first = copies(ins, outs, sems)
        passed = [copy(a, 4 + j, (*chip, c), sibling) for j, chip in enumerate(chips) for a in range(n)]
        for a in range(n):
            copy(a, 0, sibling, me).wait_recv()
            for j, chip in enumerate(chips):
                copy(a, 4 + j, (*chip, 1 - c), me).wait_recv()
        for cp in first + passed:
            cp.wait_send()
        for cp in mine:
            cp.wait()

    return _Ride(
        shards, [jax.ShapeDtypeStruct((N_DEV, *s.shape), s.dtype) for s in shards],
        [pltpu.SemaphoreType.DMA((n, 7)), pltpu.SemaphoreType.DMA((n, 7)), pltpu.SemaphoreType.DMA((n,))],
        start, finish, middle)


def _chips_ride(scatter):
    n = len(scatter)

    def copies(ins, outs, sems):
        send_sems, recv_sems, local_sems = sems
        x, y, c = _place()
        own = 2 * x + y
        chips = [(1 - x, y), (x, 1 - y), (1 - x, 1 - y)]
        local = [pltpu.make_async_copy(ins[a].at[own], outs[a].at[own], local_sems.at[a]) for a in range(n)]
        sent = [pltpu.make_async_remote_copy(
            src_ref=ins[a].at[2 * chip[0] + chip[1]], dst_ref=outs[a].at[own],
            send_sem=send_sems.at[a, j], recv_sem=recv_sems.at[a, own], device_id=(*chip, c), device_id_type=MESH)
            for a in range(n) for j, chip in enumerate(chips)]
        return chips, c, local, sent

    def start(ins, outs, sems):
        _, _, local, sent = copies(ins, outs, sems)
        for cp in local + sent:
            cp.start()

    def finish(ins, outs, sems):
        send_sems, recv_sems, _ = sems
        chips, c, local, sent = copies(ins, outs, sems)
        for a in range(n):
            for chip in chips:
                k = 2 * chip[0] + chip[1]
                pltpu.make_async_remote_copy(
                    src_ref=outs[a].at[k], dst_ref=outs[a].at[k], send_sem=send_sems.at[a, 0],
                    recv_sem=recv_sems.at[a, k], device_id=(*chip, c), device_id_type=MESH).wait_recv()
        for cp in sent:
            cp.wait_send()
        for cp in local:
            cp.wait()

    return _Ride(
        list(scatter), [jax.ShapeDtypeStruct(s.shape, s.dtype) for s in scatter],
        [pltpu.SemaphoreType.DMA((n, 3)), pltpu.SemaphoreType.DMA((n, N_CHIP)), pltpu.SemaphoreType.DMA((n,))],
        start, finish)


def _chips_rest_ride(pair, red):
    table = _side_pieces()
    nc = len(CHUNK_ORDER)

    def each(ins, outs, sems, sending, landing):
        send_sems, recv_sems, _ = sems
        pair_ref, red_ref = ins[0], outs[0]
        x, y, c = _place()
        own = 2 * x + y
        for core in (0, 1):
            mine = table[core][0]
            rest = [s for s in range(EARLY_STEPS, nc) if mine[s] is not None]

            @pl.when(c == core)
            def _(mine=mine, rest=rest, core=core):
                for s in rest:
                    _, rows, at = mine[s]
                    k, cp = _to_chip(pair_ref.at[at // SHARD_IN, pl.ds(at % SHARD_IN, rows)], red_ref, mine[s], s,
                                     send_sems, recv_sems, own, core)
                    pl.when(own != k)(lambda cp=cp: sending(cp))
                if landing is not None:
                    for k in range(N_CHIP):
                        @pl.when(own == k)
                        def _(k=k):
                            for s in rest:
                                _, rows, at = mine[s]
                                if at // SHARD_IN == k:
                                    for chip in range(N_CHIP):
                                        if chip != k:
                                            landing(_from_chip(pair_ref.at[k, pl.ds(at % SHARD_IN, rows)], red_ref, mine[s], s,
                                                               send_sems, recv_sems, chip, (x, y, c)))

    def local(ins, outs, sems):
        x, y, _ = _place()
        return pltpu.make_async_copy(ins[0].at[2 * x + y], outs[0].at[2 * x + y], sems[2])

    def start(ins, outs, sems):
        local(ins, outs, sems).start()
        each(ins, outs, sems, lambda cp: cp.start(), None)

    def finish(ins, outs, sems):
        each(ins, outs, sems, lambda cp: cp.wait_send(), lambda cp: cp.wait_recv())
        local(ins, outs, sems).wait()

    return _Ride([pair, red], [jax.ShapeDtypeStruct(red.shape, red.dtype)],
                 [pltpu.SemaphoreType.DMA((nc,)), pltpu.SemaphoreType.DMA((N_CHIP, nc)), pltpu.SemaphoreType.DMA],
                 start, finish, aliases={1: 0})


def _allreduce_small(pack, name):
    shape = pack.shape

    def body(x_ref, o_ref, sib_ref, chip_ref, send_sems, recv_sems):
        x, y, c = _place()
        own = 2 * x + y
        chips = [(1 - x, y), (x, 1 - y), (1 - x, 1 - y)]
        to_sibling = pltpu.make_async_remote_copy(
            src_ref=x_ref, dst_ref=sib_ref, send_sem=send_sems.at[0], recv_sem=recv_sems.at[0],
            device_id=(x, y, 1 - c), device_id_type=MESH)
        to_sibling.start()
        to_sibling.wait()
        chip_ref[own] = x_ref[...] + sib_ref[...]
        sent = [pltpu.make_async_remote_copy(
            src_ref=chip_ref.at[own], dst_ref=chip_ref.at[own], send_sem=send_sems.at[1 + j],
            recv_sem=recv_sems.at[1 + own], device_id=(*chip, c), device_id_type=MESH) for j, chip in enumerate(chips)]
        for cp in sent:
            cp.start()
        for chip in chips:
            k = 2 * chip[0] + chip[1]
            pltpu.make_async_remote_copy(
                src_ref=chip_ref.at[k], dst_ref=chip_ref.at[k], send_sem=send_sems.at[1],
                recv_sem=recv_sems.at[1 + k], device_id=(*chip, c), device_id_type=MESH).wait_recv()
        for cp in sent:
            cp.wait_send()
        o_ref[...] = (chip_ref[0] + chip_ref[1]) + (chip_ref[2] + chip_ref[3])

    return pl.pallas_call(
        body, name=name, out_shape=jax.ShapeDtypeStruct(shape, F32),
        in_specs=[pl.BlockSpec(memory_space=pltpu.VMEM)], out_specs=pl.BlockSpec(memory_space=pltpu.VMEM),
        scratch_shapes=[pltpu.VMEM(shape, F32), pltpu.VMEM((N_CHIP, *shape), F32),
                        pltpu.SemaphoreType.DMA((4,)), pltpu.SemaphoreType.DMA((1 + N_CHIP,))],
    )(pack)


def _adamw(g, w, m, v):
    m = ADAM_B1 * m + (1.0 - ADAM_B1) * g
    v = ADAM_B2 * v + (1.0 - ADAM_B2) * (g * g)
    m_hat = m / (1.0 - ADAM_B1 ** ADAM_STEP)
    v_hat = v / (1.0 - ADAM_B2 ** ADAM_STEP)
    delta = -ADAM_LR * (m_hat / (jnp.sqrt(v_hat) + ADAM_EPS) + ADAM_WD * w)
    return delta, m, v


def _adam_parts(items, name, tr=None):
    ni = len(items)
    npart, r, c = items[0][0].shape
    tr = r if tr is None else min(tr, r)

    def body(*refs):
        for a in range(ni):
            p_ref, w_ref, m_ref, v_ref = refs[4 * a:4 * a + 4]
            g_ref, d_ref, nm_ref, nv_ref = refs[4 * (ni + a):4 * (ni + a) + 4]
            g = p_ref[0].astype(F32)
            for k in range(1, npart):
                g = g + p_ref[k].astype(F32)
            g_ref[...] = g
            d_ref[...], nm_ref[...], nv_ref[...] = _adamw(g, w_ref[...], m_ref[...], v_ref[...])

    tile = pl.BlockSpec((tr, c), lambda i: (i, 0))
    outs = pl.pallas_call(
        body, name=name, grid=(r // tr,),
        in_specs=[pl.BlockSpec((npart, tr, c), lambda i: (0, i, 0)), tile, tile, tile] * ni,
        out_specs=[tile] * (4 * ni), out_shape=[jax.ShapeDtypeStruct((r, c), F32)] * (4 * ni),
        compiler_params=_params("parallel"))(*[arr for item in items for arr in item])
    return [outs[4 * a:4 * a + 4] for a in range(ni)]


def _block_diag(w):
    w4 = w.reshape(N_GROUPS, 4, RNN_BLOCK_W, RNN_BLOCK_W)
    eye = jnp.eye(4, dtype=w.dtype)
    return jnp.einsum("gbij,bc->gbicj", w4, eye).reshape(N_GROUPS, GROUP_W, GROUP_W).astype(BF16)


SMALL_ROWS = 16
ROW_PRE_G, ROW_BGATE, ROW_CONV_B, ROW_B_A, ROW_B_X, ROW_LAM, ROW_POST_G, ROW_LOSS, ROW_SINKS, ROW_CONV_W = 0, 1, 3, 4, 5, 6, 7, 8, 9, 10


def _pack_stats(st_pre, st_merge, st_rnn, st_post, st_sink, name):
    d = D_MODEL

    def body(pre_ref, mg_ref, rnn_ref, post_ref, sink_ref, o_ref):
        rnn = rnn_ref[...]
        sinks = jnp.concatenate([sink_ref[0:1, :], jnp.zeros((1, d - LANE), F32)], axis=1)
        o_ref[0:8, :] = _rows8([pre_ref[0:1, :], mg_ref[0:1, :], mg_ref[1:2, :], rnn[0:1], rnn[1:2], rnn[2:3], rnn[3:4],
                                post_ref[0:1, :]], d)
        o_ref[8:16, :] = _rows8([post_ref[1:2, :], sinks, rnn[4:5], rnn[5:6], rnn[6:7], rnn[7:8]], d)

    return pl.pallas_call(body, name=name, out_shape=jax.ShapeDtypeStruct((SMALL_ROWS, d), F32))(
        st_pre, st_merge, st_rnn, st_post, st_sink)


def _adam_small(total, w, m, v, name):
    d = D_MODEL
    n_in = len(w)
    rows = (ROW_PRE_G, ROW_BGATE, ROW_CONV_B, ROW_B_A, ROW_B_X, ROW_LAM, ROW_POST_G, ROW_SINKS)

    def grad_of(p_ref, row, shape):
        if shape == (1, 2 * d):
            return jnp.concatenate([p_ref[row:row + 1, :], p_ref[row + 1:row + 2, :]], axis=1)
        if shape == (1, RNN_BLOCKS, RNN_BLOCK_W):
            r = p_ref[row:row + 1, :]
            return jnp.concatenate([r[:, b * RNN_BLOCK_W:(b + 1) * RNN_BLOCK_W] for b in range(RNN_BLOCKS)], axis=0)[None]
        return p_ref[row:row + 1, 0:shape[1]]

    def body(*refs):
        p_ref = refs[0]
        w_refs, m_refs, v_refs = (refs[1 + k * n_in:1 + (k + 1) * n_in] for k in range(3))
        outs = refs[1 + 3 * n_in:]
        for k in range(n_in):
            g = grad_of(p_ref, rows[k], w[k].shape)
            res = (g,) + _adamw(g, w_refs[k][...], m_refs[k][...], v_refs[k][...])
            for kind in range(4):
                outs[kind * n_in + k][...] = res[kind]

    outs = pl.pallas_call(
        body, name=name, out_shape=[jax.ShapeDtypeStruct(a.shape, F32) for _ in range(4) for a in w])(total, *w, *m, *v)
    return [outs[kind * n_in:(kind + 1) * n_in] for kind in range(4)]


def kernel(x, pre_norm_g, w_in, b_gate, conv_w, conv_b, w_rg_a, b_rg_a, w_rg_x, b_rg_x, lru_lambda, attn_sinks, w_rnn_out, w_attn_out, w_out, post_norm_g, loss_target, m_pre_norm_g, m_w_in, m_b_gate, m_conv_w, m_conv_b, m_w_rg_a, m_b_rg_a, m_w_rg_x, m_b_rg_x, m_lru_lambda, m_attn_sinks, m_w_rnn_out, m_w_attn_out, m_w_out, m_post_norm_g, v_pre_norm_g, v_w_in, v_b_gate, v_conv_w, v_conv_b, v_w_rg_a, v_b_rg_a, v_w_rg_x, v_b_rg_x, v_lru_lambda, v_attn_sinks, v_w_rnn_out, v_w_attn_out, v_w_out, v_post_norm_g):
    cx, cy, cc = _place()
    dev = 4 * cx + 2 * cy + cc

    w_in_t, m_in_t, v_in_t = (jnp.transpose(a[0]) for a in (w_in, m_w_in, v_w_in))
    wt_shard = w_in_t.astype(BF16)

    def project(xs, pre_g):
        h, rx_rg, qkv, ag_ml, wt_all = _gather_project(xs, pre_g, wt_shard, "gather_project")
        return h, rx_rg, qkv, ag_ml, wt_all.reshape(D_IN, D_MODEL)

    def late_unpack(landed):
        w_rnn_all, w_attn_all, w_out_all, cw_all = landed
        return dict(w_rnn=w_rnn_all.reshape(D_RNN, D_MODEL), w_attn=w_attn_all.reshape(D_MODEL, D_MODEL),
                    w_out=w_out_all.reshape(D_MODEL, D_MODEL), cw=jnp.transpose(cw_all, (1, 0, 2)).reshape(4, D_RNN))

    late_weights = (_gather_ride([w_rnn_out[0].astype(BF16), w_attn_out[0].astype(BF16), w_out[0].astype(BF16), conv_w[0]]),
                    late_unpack)

    heads = jnp.arange(1, N_Q_HEADS + 1, dtype=F32)
    slopes = jnp.exp2(-ALIBI_MAX_BIAS * heads / N_Q_HEADS)
    b_a = b_rg_a.reshape(1, D_RNN)
    b_x = b_rg_x.reshape(1, D_RNN)
    p = dict(
        pre_g=pre_norm_g, post_g=post_norm_g, b_gate=b_gate, cb=conv_b,
        wbd_a=_block_diag(w_rg_a[0]), b_a=b_a, wbd_x=_block_diag(w_rg_x[0]), b_x=b_x, lam=lru_lambda,
        sm=jnp.pad(attn_sinks, ((0, 1), (0, 0))) + jnp.pad(slopes[None, :], ((1, 0), (0, 0))))

    flat = (RNN_BLOCKS * RNN_BLOCK_W, RNN_BLOCK_W)

    def reduce_out(out_pairs, g_rg_a, g_rg_x):
        return _join_rides(_chips_ride(out_pairs), _gather_ride([g_rg_a.reshape(flat), g_rg_x.reshape(flat)]))

    reduce_in = _chips_rest_ride

    g = _local_grads(x[0], loss_target[0], p, project, late_weights, reduce_out, reduce_in)
    small = _allreduce_small(
        _pack_stats(g["st_pre"], g["st_merge"], g["st_rnn"], g["st_post"], g["st_sink"], "pack_stats"), "allreduce_small")

    out = {}
    red = g["red_out"]
    w_in_out, = _adam_parts([(g["red_in"][0], w_in_t, m_in_t, v_in_t)], "adam_w_in", tr=SHARD_IN // 2)
    out["w_in"] = [jnp.transpose(o) for o in w_in_out]
    out["w_rnn_out"], out["w_attn_out"], out["w_out"] = _adam_parts(
        [(red[0], w_rnn_out[0], m_w_rnn_out[0], v_w_rnn_out[0]), (red[1], w_attn_out[0], m_w_attn_out[0], v_w_attn_out[0]),
         (red[2], w_out[0], m_w_out[0], v_w_out[0])], "adam_w_outs")
    out["w_rg_a"], out["w_rg_x"] = _adam_parts(
        [(red[3], w_rg_a.reshape(flat), m_w_rg_a.reshape(flat), v_w_rg_a.reshape(flat)),
         (red[4], w_rg_x.reshape(flat), m_w_rg_x.reshape(flat), v_w_rg_x.reshape(flat))], "adam_w_rg")

    small_names = ("pre_norm_g", "b_gate", "conv_b", "b_rg_a", "b_rg_x", "lru_lambda", "post_norm_g", "attn_sinks")
    small_out = _adam_small(
        small,
        (pre_norm_g, b_gate, conv_b, b_rg_a, b_rg_x, lru_lambda, post_norm_g, attn_sinks),
        (m_pre_norm_g, m_b_gate, m_conv_b, m_b_rg_a, m_b_rg_x, m_lru_lambda, m_post_norm_g, m_attn_sinks),
        (v_pre_norm_g, v_b_gate, v_conv_b, v_b_rg_a, v_b_rg_x, v_lru_lambda, v_post_norm_g, v_attn_sinks),
        "adam_small")
    g_cw = lax.dynamic_slice(small[ROW_CONV_W:ROW_CONV_W + 4], (0, dev * SHARD_OUT), (4, SHARD_OUT))
    out["conv_w"], = _adam_parts([(g_cw[None], conv_w[0], m_conv_w[0], v_conv_w[0])], "adam_conv_w")

    shapes = dict(w_in=(1, D_MODEL, SHARD_IN), w_rnn_out=(1, SHARD_OUT, D_MODEL), w_attn_out=(1, SHARD_OUT, D_MODEL),
                  w_out=(1, SHARD_OUT, D_MODEL), w_rg_a=(1, RNN_BLOCKS, RNN_BLOCK_W, RNN_BLOCK_W),
                  w_rg_x=(1, RNN_BLOCKS, RNN_BLOCK_W, RNN_BLOCK_W), conv_w=(1, 4, SHARD_OUT))
    weights = ["pre_norm_g", "w_in", "b_gate", "conv_w", "conv_b", "w_rg_a", "b_rg_a", "w_rg_x", "b_rg_x",
               "lru_lambda", "attn_sinks", "w_rnn_out", "w_attn_out", "w_out", "post_norm_g"]
    results = []
    for kind in range(4):
        for name in weights:
            if name in out:
                results.append(out[name][kind].reshape(shapes[name]))
            else:
                results.append(small_out[kind][small_names.index(name)])
    loss = 0.5 / D_MODEL * jnp.sum(small[ROW_LOSS])
    return (loss, g["grad_x"][None], *results)
```
